```python
import jax, jax.numpy as jnp
from jax import lax
import numpy as np

D_MODEL = 1024
BATCH = 8
SEQ = 4096
DEPTH = 1

PLE_DIM = 256
ATTN_HEADS = 8
ATTN_KV_HEADS = 2
ATTN_HEAD_DIM = 64
ATTN_GROUPS = ATTN_HEADS // ATTN_KV_HEADS
WINDOW = 128
ATTN_BLOCK = 128
ROPE_THETA = 10000.0
DN_HEADS = 4
DN_HEAD_DIM = 128
DN_CONV = 4
DN_CHUNK = 64
D_FF = 4 * D_MODEL
EPS = 1e-6

ATTN_Q = ATTN_HEADS * ATTN_HEAD_DIM
ATTN_KV = ATTN_KV_HEADS * ATTN_HEAD_DIM
DN_W = DN_HEADS * DN_HEAD_DIM
MIX_WIDTH = ATTN_Q + DN_W
SPLIT_SIZES = (ATTN_Q, ATTN_KV, ATTN_KV, DN_W, DN_W, DN_W, DN_W, DN_HEADS, DN_HEADS)
D_IN = sum(SPLIT_SIZES)
CONV_CH = 3 * DN_W

kernel_name = "hybrid_swa_sink_gated_deltanet_block"


def rmsnorm(x, g):
    xf = x.astype(jnp.float32)
    y = xf * lax.rsqrt(jnp.mean(xf * xf, axis=-1, keepdims=True) + EPS) * g.astype(jnp.float32)
    return y.astype(x.dtype)


def l2norm(x):
    return x * lax.rsqrt(jnp.sum(x * x, axis=-1, keepdims=True) + EPS)


def rope(t, positions):
    dh = t.shape[-1]
    half = dh // 2
    inv = 1.0 / (ROPE_THETA ** (jnp.arange(half, dtype=jnp.float32) * (2.0 / dh)))
    ang = positions.astype(jnp.float32)[:, None] * inv[None, :]
    cos = jnp.cos(ang)[None, :, None, :]
    sin = jnp.sin(ang)[None, :, None, :]
    tf = t.astype(jnp.float32)
    t1, t2 = tf[..., :half], tf[..., half:]
    out = jnp.concatenate([t1 * cos - t2 * sin, t2 * cos + t1 * sin], axis=-1)
    return out.astype(t.dtype)


def sliding_window_sink_attention(q, k, v, sinks):
    B, S = q.shape[0], q.shape[1]
    nb = S // ATTN_BLOCK
    qb = q.reshape(B, nb, ATTN_BLOCK, ATTN_KV_HEADS, ATTN_GROUPS, ATTN_HEAD_DIM)
    kb = k.reshape(B, nb, ATTN_BLOCK, ATTN_KV_HEADS, ATTN_HEAD_DIM)
    vb = v.reshape(B, nb, ATTN_BLOCK, ATTN_KV_HEADS, ATTN_HEAD_DIM)

    def with_prev(t):
        prev = jnp.concatenate([jnp.zeros_like(t[:, :1]), t[:, :-1]], axis=1)
        return jnp.concatenate([prev, t], axis=2)

    kw, vw = with_prev(kb), with_prev(vb)
    scale = ATTN_HEAD_DIM ** -0.5
    s = jnp.einsum('bnqhgd,bnkhd->bnhgqk', qb, kw).astype(jnp.float32) * scale
    blk = jnp.arange(nb)[:, None, None]
    qpos = blk * ATTN_BLOCK + jnp.arange(ATTN_BLOCK)[None, :, None]
    kpos = blk * ATTN_BLOCK - ATTN_BLOCK + jnp.arange(2 * ATTN_BLOCK)[None, None, :]
    valid = (kpos <= qpos) & (qpos - kpos < WINDOW) & (kpos >= 0)
    s = jnp.where(valid[:, None, None, :, :], s, -jnp.inf)
    sink = jnp.broadcast_to(
        sinks.astype(jnp.float32).reshape(1, 1, ATTN_KV_HEADS, ATTN_GROUPS, 1, 1),
        s.shape[:-1] + (1,))
    probs = jax.nn.softmax(jnp.concatenate([s, sink], axis=-1), axis=-1)[..., :-1]
    o = jnp.einsum('bnhgqk,bnkhd->bnqhgd', probs.astype(v.dtype), vw)
    return o.reshape(B, S, ATTN_Q)


def causal_conv(x, w):
    c = x.shape[-1]
    return lax.conv_general_dilated(
        x, w[:, None, :].astype(x.dtype), window_strides=(1,),
        padding=((DN_CONV - 1, 0),), dimension_numbers=('NWC', 'WIO', 'NWC'),
        feature_group_count=c)


def chunk_gated_delta_rule(q, k, v, g, beta):
    B, S, H, DK = q.shape
    DV = v.shape[-1]
    C = DN_CHUNK
    nc = S // C

    def to_chunks(t):
        t = jnp.moveaxis(t, 2, 1)
        return t.reshape(t.shape[:2] + (nc, C) + t.shape[3:])

    q, k, v, g, beta = (to_chunks(t) for t in (q, k, v, g, beta))
    g = jnp.cumsum(g, axis=-1)
    tril = jnp.tril(jnp.ones((C, C), dtype=bool))
    strict = jnp.tril(jnp.ones((C, C), dtype=bool), -1)
    decay = jnp.exp(jnp.where(tril, g[..., :, None] - g[..., None, :], -jnp.inf))
    k_beta = k * beta[..., None]
    v_beta = v * beta[..., None]
    L = jnp.where(strict, jnp.einsum('bhncd,bhnsd->bhncs', k_beta, k) * decay, 0.0)
    eye = jnp.eye(C, dtype=q.dtype)
    T = lax.linalg.triangular_solve(eye + L, jnp.broadcast_to(eye, L.shape),
                                    left_side=True, lower=True)
    u = jnp.einsum('bhncs,bhnsd->bhncd', T, v_beta)
    w = jnp.einsum('bhncs,bhnsd->bhncd', T, k_beta * jnp.exp(g)[..., None])
    a_qk = jnp.einsum('bhncd,bhnsd->bhncs', q, k) * decay
    q_g = q * jnp.exp(g)[..., None]
    g_last = g[..., -1]
    k_d = k * jnp.exp(g_last[..., None] - g)[..., None]
    d_last = jnp.exp(g_last)

    xs = tuple(jnp.moveaxis(t, 2, 0) for t in (q_g, k_d, u, w, a_qk, d_last))

    def step(state, inp):
        q_i, k_i, u_i, w_i, a_i, d_i = inp
        v_new = u_i - jnp.einsum('bhck,bhkv->bhcv', w_i, state)
        o = jnp.einsum('bhck,bhkv->bhcv', q_i, state) + jnp.einsum('bhcs,bhsv->bhcv', a_i, v_new)
        state = state * d_i[..., None, None] + jnp.einsum('bhck,bhcv->bhkv', k_i, v_new)
        return state, o

    s0 = jnp.zeros((B, H, DK, DV), dtype=q.dtype)
    _, o = lax.scan(step, s0, xs)
    o = jnp.moveaxis(o, 0, 2).reshape(B, H, S, DV)
    return jnp.moveaxis(o, 1, 2)


def gated_deltanet(q, k, v, z, b, a, conv_w, a_log, dt_bias, norm_w):
    B, S = q.shape[0], q.shape[1]
    qkv = jax.nn.silu(causal_conv(jnp.concatenate([q, k, v], axis=-1), conv_w))
    qc, kc, vc = jnp.split(qkv.astype(jnp.float32), 3, axis=-1)
    shp = (B, S, DN_HEADS, DN_HEAD_DIM)
    qc = l2norm(qc.reshape(shp)) * (DN_HEAD_DIM ** -0.5)
    kc = l2norm(kc.reshape(shp))
    vc = vc.reshape(shp)
    beta = jax.nn.sigmoid(b.astype(jnp.float32))
    g = -jnp.exp(a_log.astype(jnp.float32)) * jax.nn.softplus(
        a.astype(jnp.float32) + dt_bias.astype(jnp.float32))
    o = chunk_gated_delta_rule(qc, kc, vc, g, beta)
    o = o * lax.rsqrt(jnp.mean(o * o, axis=-1, keepdims=True) + EPS) * norm_w.astype(jnp.float32)
    o = o * jax.nn.silu(z.astype(jnp.float32).reshape(shp))
    return o.reshape(B, S, DN_W).astype(q.dtype)


def _fwd_setup_inputs(seed: int = 0) -> dict:
    key = jax.random.key(seed)
    ks = jax.random.split(key, 20)
    f32 = jnp.float32
    nrm = lambda k, shape, s: jax.random.normal(k, shape, f32) * s
    gain = lambda k, shape: 1.0 + 0.02 * jax.random.normal(k, shape, f32)
    dt = jnp.exp(jax.random.uniform(ks[6], (DEPTH, DN_HEADS), f32, np.log(1e-3), np.log(1e-1)))
    return {
        "x": nrm(ks[0], (BATCH, SEQ, D_MODEL), 1.0),
        "p": nrm(ks[1], (DEPTH, BATCH, SEQ, PLE_DIM), 1.0),
        "norm_mix": gain(ks[2], (DEPTH, D_MODEL)),
        "w_in": nrm(ks[3], (DEPTH, D_MODEL, D_IN), D_MODEL ** -0.5),
        "conv_w": nrm(ks[4], (DEPTH, DN_CONV, CONV_CH), DN_CONV ** -0.5),
        "a_log": jnp.log(jax.random.uniform(ks[5], (DEPTH, DN_HEADS), f32, 1.0, 16.0)),
        "dt_bias": dt + jnp.log(-jnp.expm1(-dt)),
        "dn_norm": gain(ks[7], (DEPTH, DN_HEAD_DIM)),
        "sinks": nrm(ks[8], (DEPTH, ATTN_HEADS), 0.5),
        "w_o": nrm(ks[9], (DEPTH, MIX_WIDTH, D_MODEL), MIX_WIDTH ** -0.5),
        "norm_mlp": gain(ks[10], (DEPTH, D_MODEL)),
        "w_up": nrm(ks[11], (DEPTH, D_MODEL, D_FF), D_MODEL ** -0.5),
        "w_down": nrm(ks[12], (DEPTH, D_FF, D_MODEL), D_FF ** -0.5),
        "norm_ple": gain(ks[13], (DEPTH, D_MODEL)),
        "w_ple_gate": nrm(ks[14], (DEPTH, D_MODEL, D_MODEL), D_MODEL ** -0.5),
        "w_ple_proj": nrm(ks[15], (DEPTH, PLE_DIM, D_MODEL), PLE_DIM ** -0.5),
        "norm_final": gain(ks[16], (D_MODEL,)),
    }


def _fwd_reference(x, p, norm_mix, w_in, conv_w, a_log, dt_bias, dn_norm, sinks, w_o,
              norm_mlp, w_up, w_down, norm_ple, w_ple_gate, w_ple_proj, norm_final):
    B, S, _ = x.shape
    positions = jnp.arange(S)
    split_idx = np.cumsum(SPLIT_SIZES)[:-1].tolist()
    h = x
    for i in range(DEPTH):
        u = rmsnorm(h, norm_mix[i])
        proj = u @ w_in[i]
        aq, ak, av, dq, dk, dv, dz, db, da = jnp.split(proj, split_idx, axis=-1)
        aq = rope(aq.reshape(B, S, ATTN_HEADS, ATTN_HEAD_DIM), positions)
        ak = rope(ak.reshape(B, S, ATTN_KV_HEADS, ATTN_HEAD_DIM), positions)
        av = av.reshape(B, S, ATTN_KV_HEADS, ATTN_HEAD_DIM)
        attn_out = sliding_window_sink_attention(aq, ak, av, sinks[i])
        dn_out = gated_deltanet(dq, dk, dv, dz, db, da, conv_w[i], a_log[i],
                                dt_bias[i], dn_norm[i])
        h = h + jnp.concatenate([attn_out, dn_out], axis=-1) @ w_o[i]
        m = rmsnorm(h, norm_mlp[i])
        h = h + jnp.square(jax.nn.relu(m @ w_up[i])) @ w_down[i]
        gate = jax.nn.sigmoid(rmsnorm(h, norm_ple[i]) @ w_ple_gate[i])
        h = h + gate * (p[i] @ w_ple_proj[i])
    return rmsnorm(h, norm_final)


import jax as _jax
import jax.numpy as _jnp

TWIN_FORMAT = 'train_step'
FWD_PARAMS = ['x', 'p', 'norm_mix', 'w_in', 'conv_w', 'a_log', 'dt_bias', 'dn_norm', 'sinks', 'w_o', 'norm_mlp', 'w_up', 'w_down', 'norm_ple', 'w_ple_gate', 'w_ple_proj', 'norm_final']
TWIN_WEIGHTS = ['norm_mix', 'w_in', 'conv_w', 'a_log', 'dt_bias', 'dn_norm', 'sinks', 'w_o', 'norm_mlp', 'w_up', 'w_down', 'norm_ple', 'w_ple_gate', 'w_ple_proj', 'norm_final']
TWIN_DIFF_INPUT = 'x'
TWIN_INPUTS = ['x', 'p', 'norm_mix', 'w_in', 'conv_w', 'a_log', 'dt_bias', 'dn_norm', 'sinks', 'w_o', 'norm_mlp', 'w_up', 'w_down', 'norm_ple', 'w_ple_gate', 'w_ple_proj', 'norm_final', 'loss_target', 'm_norm_mix', 'm_w_in', 'm_conv_w', 'm_a_log', 'm_dt_bias', 'm_dn_norm', 'm_sinks', 'm_w_o', 'm_norm_mlp', 'm_w_up', 'm_w_down', 'm_norm_ple', 'm_w_ple_gate', 'm_w_ple_proj', 'm_norm_final', 'v_norm_mix', 'v_w_in', 'v_conv_w', 'v_a_log', 'v_dt_bias', 'v_dn_norm', 'v_sinks', 'v_w_o', 'v_norm_mlp', 'v_w_up', 'v_w_down', 'v_norm_ple', 'v_w_ple_gate', 'v_w_ple_proj', 'v_norm_final']
TWIN_OUTPUTS = ['loss', 'grad_x', 'grad_norm_mix', 'grad_w_in', 'grad_conv_w', 'grad_a_log', 'grad_dt_bias', 'grad_dn_norm', 'grad_sinks', 'grad_w_o', 'grad_norm_mlp', 'grad_w_up', 'grad_w_down', 'grad_norm_ple', 'grad_w_ple_gate', 'grad_w_ple_proj', 'grad_norm_final', 'delta_norm_mix', 'delta_w_in', 'delta_conv_w', 'delta_a_log', 'delta_dt_bias', 'delta_dn_norm', 'delta_sinks', 'delta_w_o', 'delta_norm_mlp', 'delta_w_up', 'delta_w_down', 'delta_norm_ple', 'delta_w_ple_gate', 'delta_w_ple_proj', 'delta_norm_final', 'new_m_norm_mix', 'new_m_w_in', 'new_m_conv_w', 'new_m_a_log', 'new_m_dt_bias', 'new_m_dn_norm', 'new_m_sinks', 'new_m_w_o', 'new_m_norm_mlp', 'new_m_w_up', 'new_m_w_down', 'new_m_norm_ple', 'new_m_w_ple_gate', 'new_m_w_ple_proj', 'new_m_norm_final', 'new_v_norm_mix', 'new_v_w_in', 'new_v_conv_w', 'new_v_a_log', 'new_v_dt_bias', 'new_v_dn_norm', 'new_v_sinks', 'new_v_w_o', 'new_v_norm_mlp', 'new_v_w_up', 'new_v_w_down', 'new_v_norm_ple', 'new_v_w_ple_gate', 'new_v_w_ple_proj', 'new_v_norm_final']
TWIN_LEAF_KINDS = {'loss': 'loss', 'grad_x': 'grad_x', 'grad_norm_mix': 'grad_w', 'grad_w_in': 'grad_w', 'grad_conv_w': 'grad_w', 'grad_a_log': 'grad_w', 'grad_dt_bias': 'grad_w', 'grad_dn_norm': 'grad_w', 'grad_sinks': 'grad_w', 'grad_w_o': 'grad_w', 'grad_norm_mlp': 'grad_w', 'grad_w_up': 'grad_w', 'grad_w_down': 'grad_w', 'grad_norm_ple': 'grad_w', 'grad_w_ple_gate': 'grad_w', 'grad_w_ple_proj': 'grad_w', 'grad_norm_final': 'grad_w', 'delta_norm_mix': 'delta_w', 'delta_w_in': 'delta_w', 'delta_conv_w': 'delta_w', 'delta_a_log': 'delta_w', 'delta_dt_bias': 'delta_w', 'delta_dn_norm': 'delta_w', 'delta_sinks': 'delta_w', 'delta_w_o': 'delta_w', 'delta_norm_mlp': 'delta_w', 'delta_w_up': 'delta_w', 'delta_w_down': 'delta_w', 'delta_norm_ple': 'delta_w', 'delta_w_ple_gate': 'delta_w', 'delta_w_ple_proj': 'delta_w', 'delta_norm_final': 'delta_w', 'new_m_norm_mix': 'new_m', 'new_m_w_in': 'new_m', 'new_m_conv_w': 'new_m', 'new_m_a_log': 'new_m', 'new_m_dt_bias': 'new_m', 'new_m_dn_norm': 'new_m', 'new_m_sinks': 'new_m', 'new_m_w_o': 'new_m', 'new_m_norm_mlp': 'new_m', 'new_m_w_up': 'new_m', 'new_m_w_down': 'new_m', 'new_m_norm_ple': 'new_m', 'new_m_w_ple_gate': 'new_m', 'new_m_w_ple_proj': 'new_m', 'new_m_norm_final': 'new_m', 'new_v_norm_mix': 'new_v', 'new_v_w_in': 'new_v', 'new_v_conv_w': 'new_v', 'new_v_a_log': 'new_v', 'new_v_dt_bias': 'new_v', 'new_v_dn_norm': 'new_v', 'new_v_sinks': 'new_v', 'new_v_w_o': 'new_v', 'new_v_norm_mlp': 'new_v', 'new_v_w_up': 'new_v', 'new_v_w_down': 'new_v', 'new_v_norm_ple': 'new_v', 'new_v_w_ple_gate': 'new_v', 'new_v_w_ple_proj': 'new_v', 'new_v_norm_final': 'new_v'}


def _forward(args):
    return _fwd_reference(*[args[k] for k in FWD_PARAMS])


def _output_shape():
    out = _jax.eval_shape(lambda: _forward(_fwd_setup_inputs(0)))
    return out.shape, out.dtype

N_MICROBATCH = 1
ADAM_LR = 0.001
ADAM_B1 = 0.9
ADAM_B2 = 0.999
ADAM_EPS = 1e-08
ADAM_WD = 0.01
ADAM_STEP = 10
PER_EXAMPLE_BATCH_AXIS = {'x': 0, 'p': 1, 'loss_target': 0}
SHARED_INPUTS = []
_WEIGHT_DTYPES = {'norm_mix': _jnp.float32, 'w_in': _jnp.float32, 'conv_w': _jnp.float32, 'a_log': _jnp.float32, 'dt_bias': _jnp.float32, 'dn_norm': _jnp.float32, 'sinks': _jnp.float32, 'w_o': _jnp.float32, 'norm_mlp': _jnp.float32, 'w_up': _jnp.float32, 'w_down': _jnp.float32, 'norm_ple': _jnp.float32, 'w_ple_gate': _jnp.float32, 'w_ple_proj': _jnp.float32, 'norm_final': _jnp.float32}
MOMENT_SCALE = {'norm_mix': 1.274604e-01, 'w_in': 7.467751e-02, 'conv_w': 7.671476e-02, 'a_log': 5.135488e-01, 'dt_bias': 4.963502e-01, 'dn_norm': 2.173677e-01, 'sinks': 3.876157e-02, 'w_o': 7.259662e-02, 'norm_mlp': 1.609107e-01, 'w_up': 7.333354e-02, 'w_down': 1.399821e-01, 'norm_ple': 2.479264e-02, 'w_ple_gate': 2.286742e-02, 'w_ple_proj': 5.590296e-02, 'norm_final': 3.218461e+01}


def _to_microbatches(a, axis):
    t = _jnp.moveaxis(a, axis, 0)
    t = t.reshape((N_MICROBATCH, t.shape[0] // N_MICROBATCH) + t.shape[1:])
    return _jnp.moveaxis(t, 1, axis + 1)


def setup_inputs(seed: int = 0) -> dict:
    inp = _fwd_setup_inputs(seed)
    key = _jax.random.fold_in(_jax.random.key(seed), 7919)
    shape, _ = _output_shape()
    out = dict(inp)
    out["loss_target"] = _jax.random.normal(_jax.random.fold_in(key, 0), shape, _jnp.float32)
    for i, name in enumerate(TWIN_WEIGHTS):
        w = inp[name].astype(_jnp.float32)
        if MOMENT_SCALE is None:
            s = _jnp.sqrt(_jnp.mean(_jnp.square(w)) + 1e-30)
        else:
            s = MOMENT_SCALE[name]
        km, kv = _jax.random.split(_jax.random.fold_in(key, i + 1))
        out[name] = w
        out["m_" + name] = s * _jax.random.normal(km, w.shape, _jnp.float32)
        out["v_" + name] = (s * s) * _jax.random.uniform(kv, w.shape, _jnp.float32, 0.5, 1.5)
    if N_MICROBATCH > 1:
        for name, axis in PER_EXAMPLE_BATCH_AXIS.items():
            out[name] = _to_microbatches(out[name], axis)
    return {'x': out['x'], 'p': out['p'], 'norm_mix': out['norm_mix'], 'w_in': out['w_in'], 'conv_w': out['conv_w'], 'a_log': out['a_log'], 'dt_bias': out['dt_bias'], 'dn_norm': out['dn_norm'], 'sinks': out['sinks'], 'w_o': out['w_o'], 'norm_mlp': out['norm_mlp'], 'w_up': out['w_up'], 'w_down': out['w_down'], 'norm_ple': out['norm_ple'], 'w_ple_gate': out['w_ple_gate'], 'w_ple_proj': out['w_ple_proj'], 'norm_final': out['norm_final'], 'loss_target': out['loss_target'], 'm_norm_mix': out['m_norm_mix'], 'm_w_in': out['m_w_in'], 'm_conv_w': out['m_conv_w'], 'm_a_log': out['m_a_log'], 'm_dt_bias': out['m_dt_bias'], 'm_dn_norm': out['m_dn_norm'], 'm_sinks': out['m_sinks'], 'm_w_o': out['m_w_o'], 'm_norm_mlp': out['m_norm_mlp'], 'm_w_up': out['m_w_up'], 'm_w_down': out['m_w_down'], 'm_norm_ple': out['m_norm_ple'], 'm_w_ple_gate': out['m_w_ple_gate'], 'm_w_ple_proj': out['m_w_ple_proj'], 'm_norm_final': out['m_norm_final'], 'v_norm_mix': out['v_norm_mix'], 'v_w_in': out['v_w_in'], 'v_conv_w': out['v_conv_w'], 'v_a_log': out['v_a_log'], 'v_dt_bias': out['v_dt_bias'], 'v_dn_norm': out['v_dn_norm'], 'v_sinks': out['v_sinks'], 'v_w_o': out['v_w_o'], 'v_norm_mlp': out['v_norm_mlp'], 'v_w_up': out['v_w_up'], 'v_w_down': out['v_w_down'], 'v_norm_ple': out['v_norm_ple'], 'v_w_ple_gate': out['v_w_ple_gate'], 'v_w_ple_proj': out['v_w_ple_proj'], 'v_norm_final': out['v_norm_final']}


def _loss(weights, diff, rest, loss_target):
    with _jax.named_scope("forward"):
        args = {**rest, TWIN_DIFF_INPUT: diff, **{k: w.astype(_WEIGHT_DTYPES[k]) for k, w in weights.items()}}
        y = _forward(args)
    with _jax.named_scope("loss_head"):
        err = _jnp.square(y.astype(_jnp.float32) - loss_target)
        return 0.5 * _jnp.sum(_jnp.mean(err, axis=-1)) if err.ndim else 0.5 * err


def _adamw(w, g, m, v):
    m = ADAM_B1 * m + (1.0 - ADAM_B1) * g
    v = ADAM_B2 * v + (1.0 - ADAM_B2) * _jnp.square(g)
    m_hat = m / (1.0 - ADAM_B1 ** ADAM_STEP)
    v_hat = v / (1.0 - ADAM_B2 ** ADAM_STEP)
    delta = -ADAM_LR * (m_hat / (_jnp.sqrt(v_hat) + ADAM_EPS) + ADAM_WD * w)
    return delta, m, v


def reference(x, p, norm_mix, w_in, conv_w, a_log, dt_bias, dn_norm, sinks, w_o, norm_mlp, w_up, w_down, norm_ple, w_ple_gate, w_ple_proj, norm_final, loss_target, m_norm_mix, m_w_in, m_conv_w, m_a_log, m_dt_bias, m_dn_norm, m_sinks, m_w_o, m_norm_mlp, m_w_up, m_w_down, m_norm_ple, m_w_ple_gate, m_w_ple_proj, m_norm_final, v_norm_mix, v_w_in, v_conv_w, v_a_log, v_dt_bias, v_dn_norm, v_sinks, v_w_o, v_norm_mlp, v_w_up, v_w_down, v_norm_ple, v_w_ple_gate, v_w_ple_proj, v_norm_final):
    given = dict(x=x, p=p, norm_mix=norm_mix, w_in=w_in, conv_w=conv_w, a_log=a_log, dt_bias=dt_bias, dn_norm=dn_norm, sinks=sinks, w_o=w_o, norm_mlp=norm_mlp, w_up=w_up, w_down=w_down, norm_ple=norm_ple, w_ple_gate=w_ple_gate, w_ple_proj=w_ple_proj, norm_final=norm_final, loss_target=loss_target, m_norm_mix=m_norm_mix, m_w_in=m_w_in, m_conv_w=m_conv_w, m_a_log=m_a_log, m_dt_bias=m_dt_bias, m_dn_norm=m_dn_norm, m_sinks=m_sinks, m_w_o=m_w_o, m_norm_mlp=m_norm_mlp, m_w_up=m_w_up, m_w_down=m_w_down, m_norm_ple=m_norm_ple, m_w_ple_gate=m_w_ple_gate, m_w_ple_proj=m_w_ple_proj, m_norm_final=m_norm_final, v_norm_mix=v_norm_mix, v_w_in=v_w_in, v_conv_w=v_conv_w, v_a_log=v_a_log, v_dt_bias=v_dt_bias, v_dn_norm=v_dn_norm, v_sinks=v_sinks, v_w_o=v_w_o, v_norm_mlp=v_norm_mlp, v_w_up=v_w_up, v_w_down=v_w_down, v_norm_ple=v_norm_ple, v_w_ple_gate=v_w_ple_gate, v_w_ple_proj=v_w_ple_proj, v_norm_final=v_norm_final)
    weights = {n: given[n] for n in TWIN_WEIGHTS}
    shared = {n: given[n] for n in SHARED_INPUTS}
    per_example = {n: given[n] for n in ['x', 'p']}
    grad_fn = _jax.value_and_grad(_loss, argnums=(0, 1))

    def one_microbatch(ex, loss_target):
        ex = dict(ex)
        diff = ex.pop(TWIN_DIFF_INPUT)
        return grad_fn(weights, diff, {**shared, **ex}, loss_target)

    if N_MICROBATCH == 1:
        loss, (grad_w, grad_x) = one_microbatch(per_example, given["loss_target"])
    else:
        def body(carry, xs):
            loss_sum, grad_sum = carry
            l_k, (gw_k, gx_k) = one_microbatch(xs[0], xs[1])
            with _jax.named_scope("update"):
                return (loss_sum + l_k, _jax.tree.map(_jnp.add, grad_sum, gw_k)), gx_k

        init = (_jnp.zeros((), _jnp.float32), _jax.tree.map(_jnp.zeros_like, weights))
        (loss, grad_w), grad_x = _jax.lax.scan(body, init, (per_example, given["loss_target"]))
    with _jax.named_scope("update"):
        delta_w, new_m, new_v = {}, {}, {}
        for n in TWIN_WEIGHTS:
            delta_w[n], new_m[n], new_v[n] = _adamw(weights[n], grad_w[n], given["m_" + n], given["v_" + n])
    return (loss, grad_x, *[grad_w[n] for n in TWIN_WEIGHTS], *[delta_w[n] for n in TWIN_WEIGHTS],
            *[new_m[n] for n in TWIN_WEIGHTS], *[new_v[n] for n in TWIN_WEIGHTS])
```

```python
import functools

import jax
import jax.numpy as jnp
from jax import lax
from jax.experimental import pallas as pl
from jax.experimental.pallas import tpu as pltpu

F32 = jnp.float32
BF16 = jnp.bfloat16
MXU_DTYPE = jnp.bfloat16
HI = lax.Precision.HIGHEST

D_MODEL = 1024
PLE_DIM = 256
ATTN_HEADS = 8
ATTN_KV_HEADS = 2
ATTN_GROUPS = ATTN_HEADS // ATTN_KV_HEADS
ATTN_HEAD_DIM = 64
ATTN_BLOCK = 128
ROPE_THETA = 10000.0
DN_HEADS = 4
DN_HEAD_DIM = 128
DN_CONV = 4
DN_CHUNK = 64
D_FF = 4 * D_MODEL
EPS = 1e-6
ATTN_Q = ATTN_HEADS * ATTN_HEAD_DIM
ATTN_KV = ATTN_KV_HEADS * ATTN_HEAD_DIM
DN_W = DN_HEADS * DN_HEAD_DIM
CONV_CH = 3 * DN_W
D_IN = ATTN_Q + 2 * ATTN_KV + 4 * DN_W + 2 * DN_HEADS
DN_COLS = 4 * DN_W + 128
DN_SCALE = DN_HEAD_DIM ** -0.5
ATTN_SCALE = ATTN_HEAD_DIM ** -0.5
FF_BLOCKS = 4
FF_BLOCK = D_FF // FF_BLOCKS

ADAM_LR = 0.001
ADAM_B1 = 0.9
ADAM_B2 = 0.999
ADAM_EPS = 1e-08
ADAM_WD = 0.01
ADAM_STEP = 10

V7X_VMEM_BYTES = 64 * 1024 * 1024
VMEM_LIMIT = 48 * 1024 * 1024

NN = ((1,), (0,))
NT = ((1,), (1,))
TN = ((0,), (0,))


def _dot(a, b, dims=NN, prec=None):
    return lax.dot_general(a, b, (dims, ((), ())), precision=prec, preferred_element_type=F32)


def _sigmoid(x):
    return 1.0 / (1.0 + jnp.exp(-x))


def _softplus(x):
    return jnp.maximum(x, 0.0) + jnp.log(1.0 + jnp.exp(-jnp.abs(x)))


def _params(*sem):
    return pltpu.CompilerParams(dimension_semantics=sem, vmem_limit_bytes=VMEM_LIMIT)


def _rms_fwd(xv, g):
    r = lax.rsqrt(jnp.mean(xv * xv, axis=-1, keepdims=True) + EPS)
    return xv * r * g


def _rms_bwd(xv, g, dn):
    r = lax.rsqrt(jnp.mean(xv * xv, axis=-1, keepdims=True) + EPS)
    xh = xv * r
    dg = jnp.sum(dn * xh, axis=0, keepdims=True)
    dxh = dn * g
    dx = r * (dxh - xh * jnp.mean(dxh * xh, axis=-1, keepdims=True))
    return dx, dg


def _full(shape):
    return pl.BlockSpec(shape, lambda *_: (0,) * len(shape))


def _inproj(x, g_mix, wa, wd, tm):
    t = x.shape[0]

    def body(x_ref, g_ref, wa_ref, wd_ref, u_ref, pa_ref, pd_ref):
        u = _rms_fwd(x_ref[...], g_ref[...]).astype(MXU_DTYPE)
        u_ref[...] = u
        pa_ref[...] = _dot(u, wa_ref[...])
        pd_ref[...] = _dot(u, wd_ref[...])

    na, nd = wa.shape[1], wd.shape[1]
    return pl.pallas_call(
        body, name="inproj", grid=(t // tm,),
        in_specs=[pl.BlockSpec((tm, D_MODEL), lambda i: (i, 0)), _full((1, D_MODEL)),
                  _full((D_MODEL, na)), _full((D_MODEL, nd))],
        out_specs=[pl.BlockSpec((tm, D_MODEL), lambda i: (i, 0)), pl.BlockSpec((tm, na), lambda i: (i, 0)),
                   pl.BlockSpec((tm, nd), lambda i: (i, 0))],
        out_shape=[jax.ShapeDtypeStruct((t, D_MODEL), MXU_DTYPE), jax.ShapeDtypeStruct((t, na), F32),
                   jax.ShapeDtypeStruct((t, nd), F32)],
        compiler_params=_params("parallel"),
    )(x, g_mix, wa, wd)


def _oproj(x, ao, dn, wo_a, wo_d, tm):
    t = x.shape[0]

    def body(x_ref, ao_ref, dn_ref, wa_ref, wd_ref, h_ref):
        h_ref[...] = (x_ref[...] + _dot(ao_ref[...].astype(MXU_DTYPE), wa_ref[...])
                      + _dot(dn_ref[...].astype(MXU_DTYPE), wd_ref[...]))

    half = ao.shape[1]
    return pl.pallas_call(
        body, name="oproj", grid=(t // tm,),
        in_specs=[pl.BlockSpec((tm, D_MODEL), lambda i: (i, 0)), pl.BlockSpec((tm, half), lambda i: (i, 0)),
                  pl.BlockSpec((tm, half), lambda i: (i, 0)), _full((half, D_MODEL)), _full((half, D_MODEL))],
        out_specs=pl.BlockSpec((tm, D_MODEL), lambda i: (i, 0)),
        out_shape=jax.ShapeDtypeStruct((t, D_MODEL), F32),
        compiler_params=_params("parallel"),
    )(x, ao, dn, wo_a, wo_d)


def _mlp_fwd(h1, g_mlp, w_up4, w_down, tm):
    t = h1.shape[0]

    def body(h_ref, g_ref, wu_ref, wd_ref, m_ref, a_ref, h2_ref, acc_ref):
        k = pl.program_id(1)

        @pl.when(k == 0)
        def _():
            m_ref[...] = _rms_fwd(h_ref[...], g_ref[...]).astype(MXU_DTYPE)
            acc_ref[...] = jnp.zeros_like(acc_ref)

        a = _dot(m_ref[...], wu_ref[...])
        a_ref[...] = a
        s = jnp.square(jnp.maximum(a, 0.0)).astype(MXU_DTYPE)
        acc_ref[...] += _dot(s, wd_ref[...])

        @pl.when(k == FF_BLOCKS - 1)
        def _():
            h2_ref[...] = h_ref[...] + acc_ref[...]

    return pl.pallas_call(
        body, name="mlp_fwd", grid=(t // tm, FF_BLOCKS),
        in_specs=[pl.BlockSpec((tm, D_MODEL), lambda i, k: (i, 0)), _full((1, D_MODEL)),
                  pl.BlockSpec((None, D_MODEL, FF_BLOCK), lambda i, k: (k, 0, 0)),
                  pl.BlockSpec((FF_BLOCK, D_MODEL), lambda i, k: (k, 0))],
        out_specs=[pl.BlockSpec((tm, D_MODEL), lambda i, k: (i, 0)), pl.BlockSpec((tm, FF_BLOCK), lambda i, k: (i, k)),
                   pl.BlockSpec((tm, D_MODEL), lambda i, k: (i, 0))],
        out_shape=[jax.ShapeDtypeStruct((t, D_MODEL), MXU_DTYPE), jax.ShapeDtypeStruct((t, D_FF), F32),
                   jax.ShapeDtypeStruct((t, D_MODEL), F32)],
        scratch_shapes=[pltpu.VMEM((tm, D_MODEL), F32)],
        compiler_params=_params("parallel", "arbitrary"),
    )(h1, g_mlp, w_up4, w_down)


def _ple_loss(h2, p, tgt, g_ple, g_fin, w_gate, w_proj, tm):
    t = h2.shape[0]

    def body(h_ref, p_ref, t_ref, gp_ref, gf_ref, wg_ref, wp_ref,
             dh_ref, dhb_ref, dgp_ref, dpp_ref, n3_ref, pb_ref, acc_ref):
        @pl.when(pl.program_id(0) == 0)
        def _():
            acc_ref[...] = jnp.zeros_like(acc_ref)

        h = h_ref[...]
        g_ple_v, g_fin_v = gp_ref[...], gf_ref[...]
        n3 = _rms_fwd(h, g_ple_v).astype(MXU_DTYPE)
        n3_ref[...] = n3
        gate = _sigmoid(_dot(n3, wg_ref[...]))
        pb = p_ref[...].astype(MXU_DTYPE)
        pb_ref[...] = pb
        pp = _dot(pb, wp_ref[...])
        h3 = h + gate * pp
        r4 = lax.rsqrt(jnp.mean(h3 * h3, axis=-1, keepdims=True) + EPS)
        xh4 = h3 * r4
        e = xh4 * g_fin_v - t_ref[...]
        loss = 0.5 * jnp.sum(jnp.mean(e * e, axis=-1, keepdims=True), axis=0, keepdims=True)
        dy = e * (1.0 / D_MODEL)
        dg_fin = jnp.sum(dy * xh4, axis=0, keepdims=True)
        dxh = dy * g_fin_v
        dh3 = r4 * (dxh - xh4 * jnp.mean(dxh * xh4, axis=-1, keepdims=True))
        dpp_ref[...] = (dh3 * gate).astype(MXU_DTYPE)
        dgp = (dh3 * pp * gate * (1.0 - gate)).astype(MXU_DTYPE)
        dgp_ref[...] = dgp
        dn3 = _dot(dgp, wg_ref[...], NT)
        dx, dg_ple = _rms_bwd(h, g_ple_v, dn3)
        dh2 = dh3 + dx
        dh_ref[...] = dh2
        dhb_ref[...] = dh2.astype(MXU_DTYPE)
        acc_ref[0:1, :] += dg_fin
        acc_ref[1:2, :] += dg_ple
        acc_ref[2:3, :] += jnp.broadcast_to(loss, (1, D_MODEL))

    row = lambda w: pl.BlockSpec((tm, w), lambda i: (i, 0))
    return pl.pallas_call(
        body, name="ple_loss", grid=(t // tm,),
        in_specs=[row(D_MODEL), row(PLE_DIM), row(D_MODEL), _full((1, D_MODEL)), _full((1, D_MODEL)),
                  _full((D_MODEL, D_MODEL)), _full((PLE_DIM, D_MODEL))],
        out_specs=[row(D_MODEL), row(D_MODEL), row(D_MODEL), row(D_MODEL), row(D_MODEL), row(PLE_DIM),
                   _full((8, D_MODEL))],
        out_shape=[jax.ShapeDtypeStruct((t, D_MODEL), F32), jax.ShapeDtypeStruct((t, D_MODEL), MXU_DTYPE),
                   jax.ShapeDtypeStruct((t, D_MODEL), MXU_DTYPE), jax.ShapeDtypeStruct((t, D_MODEL), MXU_DTYPE),
                   jax.ShapeDtypeStruct((t, D_MODEL), MXU_DTYPE), jax.ShapeDtypeStruct((t, PLE_DIM), MXU_DTYPE),
                   jax.ShapeDtypeStruct((8, D_MODEL), F32)],
        compiler_params=_params("arbitrary"),
    )(h2, p, tgt, g_ple, g_fin, w_gate, w_proj)


def _mlp_bwd(dh2, dh2b, a, h1, g_mlp, w_up4, w_down, tm):
    t = h1.shape[0]

    def body(dh_ref, dhb_ref, a_ref, h_ref, g_ref, wu_ref, wd_ref,
             s_ref, da_ref, dh1_ref, dh1b_ref, acc_ref, dm_ref):
        i, k = pl.program_id(0), pl.program_id(1)

        @pl.when((i == 0) & (k == 0))
        def _():
            acc_ref[...] = jnp.zeros_like(acc_ref)

        @pl.when(k == 0)
        def _():
            dm_ref[...] = jnp.zeros_like(dm_ref)

        ds = _dot(dhb_ref[...], wd_ref[...], NT)
        r = jnp.maximum(a_ref[...], 0.0)
        s_ref[...] = (r * r).astype(MXU_DTYPE)
        da = (ds * (2.0 * r)).astype(MXU_DTYPE)
        da_ref[...] = da
        dm_ref[...] += _dot(da, wu_ref[...], NT)

        @pl.when(k == FF_BLOCKS - 1)
        def _():
            dx, dg = _rms_bwd(h_ref[...], g_ref[...], dm_ref[...])
            dh1 = dh_ref[...] + dx
            dh1_ref[...] = dh1
            dh1b_ref[...] = dh1.astype(MXU_DTYPE)
            acc_ref[0:1, :] += dg

    tok = lambda w: pl.BlockSpec((tm, w), lambda i, k: (i, 0))
    return pl.pallas_call(
        body, name="mlp_bwd", grid=(t // tm, FF_BLOCKS),
        in_specs=[tok(D_MODEL), tok(D_MODEL), pl.BlockSpec((tm, FF_BLOCK), lambda i, k: (i, k)), tok(D_MODEL),
                  _full((1, D_MODEL)), pl.BlockSpec((None, D_MODEL, FF_BLOCK), lambda i, k: (k, 0, 0)),
                  pl.BlockSpec((FF_BLOCK, D_MODEL), lambda i, k: (k, 0))],
        out_specs=[pl.BlockSpec((tm, FF_BLOCK), lambda i, k: (i, k)), pl.BlockSpec((tm, FF_BLOCK), lambda i, k: (i, k)),
                   tok(D_MODEL), tok(D_MODEL), pl.BlockSpec((8, D_MODEL), lambda i, k: (0, 0))],
        out_shape=[jax.ShapeDtypeStruct((t, D_FF), MXU_DTYPE), jax.ShapeDtypeStruct((t, D_FF), MXU_DTYPE),
                   jax.ShapeDtypeStruct((t, D_MODEL), F32), jax.ShapeDtypeStruct((t, D_MODEL), MXU_DTYPE),
                   jax.ShapeDtypeStruct((8, D_MODEL), F32)],
        scratch_shapes=[pltpu.VMEM((tm, D_MODEL), F32)],
        compiler_params=_params("arbitrary", "arbitrary"),
    )(dh2, dh2b, a, h1, g_mlp, w_up4, w_down)


def _oproj_bwd(dh1b, wo_a, wo_d, tm):
    t = dh1b.shape[0]
    half = wo_a.shape[0]

    def body(d_ref, wa_ref, wd_ref, da_ref, dd_ref):
        d = d_ref[...]
        da_ref[...] = _dot(d, wa_ref[...], NT)
        dd_ref[...] = _dot(d, wd_ref[...], NT)

    return pl.pallas_call(
        body, name="oproj_bwd", grid=(t // tm,),
        in_specs=[pl.BlockSpec((tm, D_MODEL), lambda i: (i, 0)), _full((half, D_MODEL)), _full((half, D_MODEL))],
        out_specs=[pl.BlockSpec((tm, half), lambda i: (i, 0)), pl.BlockSpec((tm, half), lambda i: (i, 0))],
        out_shape=[jax.ShapeDtypeStruct((t, half), F32), jax.ShapeDtypeStruct((t, half), F32)],
        compiler_params=_params("parallel"),
    )(dh1b, wo_a, wo_d)


def _inproj_bwd(x, dh1, g_mix, grads, weights, tm):
    t = x.shape[0]
    n = len(grads)

    def body(*refs):
        x_ref, dh_ref, g_ref = refs[:3]
        g_refs, w_refs = refs[3:3 + n], refs[3 + n:3 + 2 * n]
        dx_ref, acc_ref = refs[3 + 2 * n:]

        @pl.when(pl.program_id(0) == 0)
        def _():
            acc_ref[...] = jnp.zeros_like(acc_ref)

        du = _dot(g_refs[0][...], w_refs[0][...], NT)
        for j in range(1, n):
            du += _dot(g_refs[j][...], w_refs[j][...], NT)
        dx, dg = _rms_bwd(x_ref[...], g_ref[...], du)
        dx_ref[...] = dh_ref[...] + dx
        acc_ref[0:1, :] += dg

    tok = lambda w: pl.BlockSpec((tm, w), lambda i: (i, 0))
    return pl.pallas_call(
        body, name="inproj_bwd", grid=(t // tm,),
        in_specs=[tok(D_MODEL), tok(D_MODEL), _full((1, D_MODEL))] + [tok(g.shape[1]) for g in grads]
                 + [_full(w.shape) for w in weights],
        out_specs=[tok(D_MODEL), _full((8, D_MODEL))],
        out_shape=[jax.ShapeDtypeStruct((t, D_MODEL), F32), jax.ShapeDtypeStruct((8, D_MODEL), F32)],
        compiler_params=_params("arbitrary"),
    )(x, dh1, g_mix, *grads, *weights)


def _wgrad(a, b, name, tk, tn, tt, col0=0, ncols=None):
    t, kdim = a.shape
    ncols = b.shape[1] - col0 if ncols is None else ncols
    cb = col0 // tn

    def body(a_ref, b_ref, o_ref):
        @pl.when(pl.program_id(2) == 0)
        def _():
            o_ref[...] = jnp.zeros_like(o_ref)

        o_ref[...] += _dot(a_ref[...], b_ref[...], TN)

    return pl.pallas_call(
        body, name=name, grid=(kdim // tk, ncols // tn, t // tt),
        in_specs=[pl.BlockSpec((tt, tk), lambda i, j, s: (s, i)), pl.BlockSpec((tt, tn), lambda i, j, s: (s, j + cb))],
        out_specs=pl.BlockSpec((tk, tn), lambda i, j, s: (i, j)),
        out_shape=jax.ShapeDtypeStruct((kdim, ncols), F32),
        compiler_params=_params("parallel", "parallel", "arbitrary"),
    )(a, b)


def _rope_tables(t):
    half = ATTN_HEAD_DIM // 2
    inv = 1.0 / (ROPE_THETA ** (jnp.arange(half, dtype=F32) * (2.0 / ATTN_HEAD_DIM)))
    ang = jnp.arange(t, dtype=F32)[:, None] * inv[None, :]
    cos, sin = jnp.cos(ang), jnp.sin(ang)
    cos2 = jnp.concatenate([cos, cos], axis=-1)
    sin2 = jnp.concatenate([-sin, sin], axis=-1)
    return jnp.tile(cos2, (1, 2)), jnp.tile(sin2, (1, 2))


def _swap_halves(tv):
    w = tv.shape[-1]
    lane = lax.broadcasted_iota(jnp.int32, tv.shape, tv.ndim - 1)
    first = (lane % ATTN_HEAD_DIM) < (ATTN_HEAD_DIM // 2)
    return jnp.where(first, pltpu.roll(tv, w - ATTN_HEAD_DIM // 2, tv.ndim - 1),
                     pltpu.roll(tv, ATTN_HEAD_DIM // 2, tv.ndim - 1))


def _rope(tv, cos, sin):
    return tv * cos + _swap_halves(tv) * sin


def _rope_bwd(dv, cos, sin):
    return dv * cos + _swap_halves(dv * sin)


def _attn_probs(qh, kwin, sink, first_block):
    s = _dot(qh, kwin, NT) * ATTN_SCALE
    r = lax.broadcasted_iota(jnp.int32, s.shape, 0)
    c = lax.broadcasted_iota(jnp.int32, s.shape, 1)
    valid = (c > r) & (c <= r + ATTN_BLOCK) & ((c >= ATTN_BLOCK) | jnp.logical_not(first_block))
    s = jnp.where(valid, s, -jnp.inf)
    m = jnp.maximum(jnp.max(s, axis=-1, keepdims=True), sink)
    e = jnp.where(valid, jnp.exp(s - m), 0.0)
    es = jnp.exp(sink - m)
    inv = 1.0 / (jnp.sum(e, axis=-1, keepdims=True) + es)
    return e * inv, es * inv


def _lane_scalar(vec, idx):
    lane = lax.broadcasted_iota(jnp.int32, vec.shape, 1)
    return jnp.sum(jnp.where(lane == idx, vec, 0.0), axis=-1, keepdims=True)


def _attn_specs(nb):
    cur = lambda w, cb: pl.BlockSpec((ATTN_BLOCK, w), lambda i: (jnp.minimum(i, nb - 1), cb))
    prev = lambda w, cb: pl.BlockSpec((ATTN_BLOCK, w), lambda i: (jnp.maximum(jnp.minimum(i, nb - 1) - 1, 0), cb))
    kcol, vcol = ATTN_Q // ATTN_KV, ATTN_Q // ATTN_KV + 1
    return [cur(ATTN_Q, 0), cur(ATTN_KV, kcol), prev(ATTN_KV, kcol), cur(ATTN_KV, vcol), prev(ATTN_KV, vcol),
            cur(ATTN_KV, 0), cur(ATTN_KV, 0), prev(ATTN_KV, 0), prev(ATTN_KV, 0), _full((1, 128))]


def _attn_fwd(pa, cos, sin, sinks_vec):
    t = pa.shape[0]
    nb = t // ATTN_BLOCK

    def body(q_ref, kc_ref, kp_ref, vc_ref, vp_ref, cc_ref, sc_ref, cp_ref, sp_ref, sk_ref, o_ref):
        first = pl.program_id(0) == 0
        cc, sc = cc_ref[...], sc_ref[...]
        q = _rope(q_ref[...], jnp.tile(cc, (1, ATTN_Q // ATTN_KV)), jnp.tile(sc, (1, ATTN_Q // ATTN_KV)))
        kc = _rope(kc_ref[...], cc, sc)
        kp = _rope(kp_ref[...], cp_ref[...], sp_ref[...])
        vc, vp = vc_ref[...], vp_ref[...]
        sk = sk_ref[...]
        for hk in range(ATTN_KV_HEADS):
            ks = slice(hk * ATTN_HEAD_DIM, (hk + 1) * ATTN_HEAD_DIM)
            kwin = jnp.concatenate([kp[:, ks], kc[:, ks]], axis=0)
            vwin = jnp.concatenate([vp[:, ks], vc[:, ks]], axis=0)
            for g in range(ATTN_GROUPS):
                h = hk * ATTN_GROUPS + g
                hs = slice(h * ATTN_HEAD_DIM, (h + 1) * ATTN_HEAD_DIM)
                probs, _ = _attn_probs(q[:, hs], kwin, _lane_scalar(sk, h), first)
                o_ref[:, hs] = _dot(probs, vwin)

    return pl.pallas_call(
        body, name="attn_fwd", grid=(nb,),
        in_specs=_attn_specs(nb),
        out_specs=pl.BlockSpec((ATTN_BLOCK, ATTN_Q), lambda i: (i, 0)),
        out_shape=jax.ShapeDtypeStruct((t, ATTN_Q), F32),
        compiler_params=_params("parallel"),
    )(pa, pa, pa, pa, pa, cos, sin, cos, sin, sinks_vec)


def _attn_bwd(pa, cos, sin, sinks_vec, dao):
    t = pa.shape[0]
    nb = t // ATTN_BLOCK

    def body(q_ref, kc_ref, kp_ref, vc_ref, vp_ref, cc_ref, sc_ref, cp_ref, sp_ref, sk_ref, do_ref,
             dq_ref, dk_ref, dv_ref, acc_ref, dqr_ref, dkw_ref, dvw_ref, ck_ref, cv_ref):
        i = pl.program_id(0)

        @pl.when(i == 0)
        def _():
            acc_ref[...] = jnp.zeros_like(acc_ref)
            ck_ref[...] = jnp.zeros_like(ck_ref)
            cv_ref[...] = jnp.zeros_like(cv_ref)

        @pl.when(i < nb)
        def _():
            first = i == 0
            cc, sc = cc_ref[...], sc_ref[...]
            cq, sq = jnp.tile(cc, (1, ATTN_Q // ATTN_KV)), jnp.tile(sc, (1, ATTN_Q // ATTN_KV))
            q = _rope(q_ref[...], cq, sq)
            kc = _rope(kc_ref[...], cc, sc)
            kp = _rope(kp_ref[...], cp_ref[...], sp_ref[...])
            vc, vp = vc_ref[...], vp_ref[...]
            sk = sk_ref[...]
            do = do_ref[...]
            lane = lax.broadcasted_iota(jnp.int32, (1, 128), 1)
            dsink = jnp.zeros((1, 128), F32)
            for hk in range(ATTN_KV_HEADS):
                ks = slice(hk * ATTN_HEAD_DIM, (hk + 1) * ATTN_HEAD_DIM)
                kwin = jnp.concatenate([kp[:, ks], kc[:, ks]], axis=0)
                vwin = jnp.concatenate([vp[:, ks], vc[:, ks]], axis=0)
                dkw = jnp.zeros((2 * ATTN_BLOCK, ATTN_HEAD_DIM), F32)
                dvw = jnp.zeros((2 * ATTN_BLOCK, ATTN_HEAD_DIM), F32)
                for g in range(ATTN_GROUPS):
                    h = hk * ATTN_GROUPS + g
                    hs = slice(h * ATTN_HEAD_DIM, (h + 1) * ATTN_HEAD_DIM)
                    qh = q[:, hs]
                    probs, psink = _attn_probs(qh, kwin, _lane_scalar(sk, h), first)
                    doh = do[:, hs]
                    dp = _dot(doh, vwin, NT)
                    delta = jnp.sum(probs * dp, axis=-1, keepdims=True)
                    ds = probs * (dp - delta) * ATTN_SCALE
                    dqr_ref[:, hs] = _dot(ds, kwin)
                    dkw += _dot(ds, qh, TN)
                    dvw += _dot(probs, doh, TN)
                    dsink += jnp.where(lane == h, jnp.sum(-psink * delta, axis=0, keepdims=True), 0.0)
                dkw_ref[:, ks] = dkw
                dvw_ref[:, ks] = dvw
            acc_ref[0:1, :] += dsink
            dq_ref[...] = _rope_bwd(dqr_ref[...], cq, sq).astype(dq_ref.dtype)
            dk_ref[...] = (ck_ref[...] + _rope_bwd(dkw_ref[0:ATTN_BLOCK, :], cp_ref[...], sp_ref[...])).astype(dk_ref.dtype)
            dv_ref[...] = (cv_ref[...] + dvw_ref[0:ATTN_BLOCK, :]).astype(dv_ref.dtype)
            ck_ref[...] = _rope_bwd(dkw_ref[ATTN_BLOCK:2 * ATTN_BLOCK, :], cc, sc)
            cv_ref[...] = dvw_ref[ATTN_BLOCK:2 * ATTN_BLOCK, :]

        @pl.when(i == nb)
        def _():
            dk_ref[...] = ck_ref[...].astype(dk_ref.dtype)
            dv_ref[...] = cv_ref[...].astype(dv_ref.dtype)

    prev_out = lambda w: pl.BlockSpec((ATTN_BLOCK, w), lambda i: (jnp.maximum(i - 1, 0), 0))
    return pl.pallas_call(
        body, name="attn_bwd", grid=(nb + 1,),
        in_specs=_attn_specs(nb) + [pl.BlockSpec((ATTN_BLOCK, ATTN_Q), lambda i: (jnp.minimum(i, nb - 1), 0))],
        out_specs=[pl.BlockSpec((ATTN_BLOCK, ATTN_Q), lambda i: (jnp.minimum(i, nb - 1), 0)), prev_out(ATTN_KV),
                   prev_out(ATTN_KV), _full((8, 128))],
        out_shape=[jax.ShapeDtypeStruct((t, ATTN_Q), MXU_DTYPE), jax.ShapeDtypeStruct((t, ATTN_KV), MXU_DTYPE),
                   jax.ShapeDtypeStruct((t, ATTN_KV), MXU_DTYPE), jax.ShapeDtypeStruct((8, 128), F32)],
        scratch_shapes=[pltpu.VMEM((ATTN_BLOCK, ATTN_Q), F32), pltpu.VMEM((2 * ATTN_BLOCK, ATTN_KV), F32),
                        pltpu.VMEM((2 * ATTN_BLOCK, ATTN_KV), F32), pltpu.VMEM((ATTN_BLOCK, ATTN_KV), F32),
                        pltpu.VMEM((ATTN_BLOCK, ATTN_KV), F32)],
        compiler_params=_params("arbitrary"),
    )(pa, pa, pa, pa, pa, cos, sin, cos, sin, sinks_vec, dao)


PAIR = 2 * DN_CHUNK
HALO = 8


def _conv_window(cur_ref, prev_ref, xs_ref, tm):
    prev = jnp.where(pl.program_id(0) > 0, prev_ref[...], 0.0)
    xs_ref[0:HALO, :] = prev
    xs_ref[HALO:HALO + tm, :] = cur_ref[...]


def _conv_taps(xs_ref, cw_ref, tm):
    y = cw_ref[0:1, :] * xs_ref[pl.ds(HALO - DN_CONV + 1, tm), :]
    for j in range(1, DN_CONV):
        y += cw_ref[j:j + 1, :] * xs_ref[pl.ds(HALO - DN_CONV + 1 + j, tm), :]
    return y


def _gate_values(ba, al, dt):
    beta = _sigmoid(ba)
    pre = ba + dt
    g = -jnp.exp(al) * _softplus(pre)
    return beta, g, pre


def _dn_prep_specs(tm, t):
    return [pl.BlockSpec((tm, CONV_CH), lambda i: (i, 0)),
            pl.BlockSpec((HALO, CONV_CH), lambda i: (jnp.maximum(i * (tm // HALO) - 1, 0), 0)),
            pl.BlockSpec((tm, 128), lambda i: (i, 4 * DN_W // 128)),
            _full((DN_CONV, CONV_CH)), _full((1, 128)), _full((1, 128))]


def _dn_prep(pd, conv_w, al_vec, dt_vec, tm):
    t = pd.shape[0]

    def body(cur_ref, prev_ref, ba_ref, cw_ref, al_ref, dt_ref, qn_ref, kn_ref, vc_ref, gc_ref, gr_ref, xs_ref):
        _conv_window(cur_ref, prev_ref, xs_ref, tm)
        y = _conv_taps(xs_ref, cw_ref, tm)
        c = y * _sigmoid(y)
        for h in range(DN_HEADS):
            qs = slice(h * DN_HEAD_DIM, (h + 1) * DN_HEAD_DIM)
            ksl = slice(DN_W + h * DN_HEAD_DIM, DN_W + (h + 1) * DN_HEAD_DIM)
            qh, kh = c[:, qs], c[:, ksl]
            qn_ref[:, qs] = qh * lax.rsqrt(jnp.sum(qh * qh, axis=-1, keepdims=True) + EPS) * DN_SCALE
            kn_ref[:, qs] = kh * lax.rsqrt(jnp.sum(kh * kh, axis=-1, keepdims=True) + EPS)
        vc_ref[...] = c[:, 2 * DN_W:3 * DN_W]
        beta, g, _ = _gate_values(ba_ref[...], al_ref[...], dt_ref[...])
        lane = lax.broadcasted_iota(jnp.int32, beta.shape, 1)
        gb = jnp.where(lane < DN_HEADS, beta, jnp.where(lane < 2 * DN_HEADS, g, 0.0))
        gc_ref[...] = gb
        gr_ref[...] = gb.T[0:8, :]

    tok = lambda w: pl.BlockSpec((tm, w), lambda i: (i, 0))
    return pl.pallas_call(
        body, name="dn_prep", grid=(t // tm,),
        in_specs=_dn_prep_specs(tm, t),
        out_specs=[tok(DN_W), tok(DN_W), tok(DN_W), tok(128), pl.BlockSpec((8, tm), lambda i: (0, i))],
        out_shape=[jax.ShapeDtypeStruct((t, DN_W), F32)] * 3 + [jax.ShapeDtypeStruct((t, 128), F32),
                                                                 jax.ShapeDtypeStruct((8, t), F32)],
        scratch_shapes=[pltpu.VMEM((HALO + tm, CONV_CH), F32)],
        compiler_params=_params("parallel"),
    )(pd, pd, pd, conv_w, al_vec, dt_vec)


def _tri(n, strict=False):
    r = lax.broadcasted_iota(jnp.int32, (n, n), 0)
    c = lax.broadcasted_iota(jnp.int32, (n, n), 1)
    return (r > c) if strict else (r >= c)


def _chunk_gates(gc, gr, c, h):
    rows = slice(c * DN_CHUNK, (c + 1) * DN_CHUNK)
    gcc = gc[rows, :]
    lane = lax.broadcasted_iota(jnp.int32, gcc.shape, 1)
    beta = jnp.sum(jnp.where(lane == h, gcc, 0.0), axis=-1, keepdims=True)
    g = jnp.sum(jnp.where(lane == DN_HEADS + h, gcc, 0.0), axis=-1, keepdims=True)
    low = _tri(DN_CHUNK)
    gam = _dot(low.astype(F32), jnp.broadcast_to(g, (DN_CHUNK, 128)), NN, HI)
    j = lax.broadcasted_iota(jnp.int32, (PAIR, DN_CHUNK), 0) - c * DN_CHUNK
    i = lax.broadcasted_iota(jnp.int32, (PAIR, DN_CHUNK), 1)
    sel = ((j >= 0) & (j <= i)).astype(F32)
    gam_row = _dot(gr, sel, NN, HI)[DN_HEADS + h:DN_HEADS + h + 1, :]
    diff = gam[:, 0:DN_CHUNK] - gam_row
    dm = jnp.where(low, jnp.exp(jnp.where(low, diff, 0.0)), 0.0)
    return beta, gam, dm


def _unit_lower_inverse(lmat):
    n = lmat.shape[0]
    eye = (lax.broadcasted_iota(jnp.int32, (n, n), 0) == lax.broadcasted_iota(jnp.int32, (n, n), 1)).astype(F32)
    acc = eye - lmat
    pw = lmat
    step = 1
    while 2 * step < n:
        pw = _dot(pw, pw, NN, HI)
        acc = acc + _dot(acc, pw, NN, HI)
        step *= 2
    return acc


def _dn_intra(qn, kn, vc, gc, gr):
    t = qn.shape[0]
    npair = t // PAIR

    def body(q_ref, k_ref, v_ref, gc_ref, gr_ref, u_ref, w_ref, qg_ref, kd_ref, a_ref, ti_ref, dl_ref):
        gc_v, gr_v = gc_ref[...], gr_ref[...]
        for c in range(2):
            rows = slice(c * DN_CHUNK, (c + 1) * DN_CHUNK)
            for h in range(DN_HEADS):
                hs = slice(h * DN_HEAD_DIM, (h + 1) * DN_HEAD_DIM)
                q, k, v = q_ref[rows, hs], k_ref[rows, hs], v_ref[rows, hs]
                beta, gam, dm = _chunk_gates(gc_v, gr_v, c, h)
                kb = k * beta
                lmat = jnp.where(_tri(DN_CHUNK, True), _dot(kb, k, NT) * dm, 0.0)
                tinv = _unit_lower_inverse(lmat)
                eg = jnp.exp(gam)
                u_ref[rows, hs] = _dot(tinv, v * beta)
                w_ref[rows, hs] = _dot(tinv, kb * eg)
                a_ref[h, rows, :] = _dot(q, k, NT) * dm
                ti_ref[h, rows, :] = tinv
                qg_ref[rows, hs] = q * eg
                gl = gam[DN_CHUNK - 1:DN_CHUNK, :]
                kd_ref[rows, hs] = k * jnp.exp(gl - gam)
                dl_ref[c, h] = jnp.broadcast_to(jnp.exp(gl), (8, 128))

    tok = lambda w: pl.BlockSpec((PAIR, w), lambda n: (n, 0))
    hm = pl.BlockSpec((DN_HEADS, PAIR, DN_CHUNK), lambda n: (0, n, 0))
    return pl.pallas_call(
        body, name="dn_intra", grid=(npair,),
        in_specs=[tok(DN_W), tok(DN_W), tok(DN_W), tok(128), pl.BlockSpec((8, PAIR), lambda n: (0, n))],
        out_specs=[tok(DN_W)] * 4 + [hm, hm, pl.BlockSpec((2, DN_HEADS, 8, 128), lambda n: (n, 0, 0, 0))],
        out_shape=[jax.ShapeDtypeStruct((t, DN_W), F32)] * 4 + [jax.ShapeDtypeStruct((DN_HEADS, t, DN_CHUNK), F32)] * 2
                  + [jax.ShapeDtypeStruct((2 * npair, DN_HEADS, 8, 128), F32)],
        compiler_params=_params("parallel"),
    )(qn, kn, vc, gc, gr)


def _dn_scan_fwd(u, w, qg, kd, a_qk, dlast, pd, dn_w):
    t = u.shape[0]
    npair = t // PAIR

    def body(u_ref, w_ref, qg_ref, kd_ref, a_ref, dl_ref, z_ref, nw_ref, out_ref, o_ref, vn_ref, sall_ref, s_ref):
        @pl.when(pl.program_id(0) == 0)
        def _():
            s_ref[...] = jnp.zeros_like(s_ref)

        nw = nw_ref[...]
        for c in range(2):
            rows = slice(c * DN_CHUNK, (c + 1) * DN_CHUNK)
            for h in range(DN_HEADS):
                hs = slice(h * DN_HEAD_DIM, (h + 1) * DN_HEAD_DIM)
                st = s_ref[h]
                sall_ref[c, h] = st
                vn = u_ref[rows, hs] - _dot(w_ref[rows, hs], st)
                o = _dot(qg_ref[rows, hs], st) + _dot(a_ref[h, rows, :], vn)
                s_ref[h] = st * dl_ref[c, h][0:1, :] + _dot(kd_ref[rows, hs], vn, TN)
                vn_ref[rows, hs] = vn
                o_ref[rows, hs] = o
                z = z_ref[rows, hs]
                on = o * lax.rsqrt(jnp.mean(o * o, axis=-1, keepdims=True) + EPS) * nw
                out_ref[rows, hs] = on * (z * _sigmoid(z))

    tok = pl.BlockSpec((PAIR, DN_W), lambda n: (n, 0))
    hm = pl.BlockSpec((DN_HEADS, PAIR, DN_CHUNK), lambda n: (0, n, 0))
    return pl.pallas_call(
        body, name="dn_scan_fwd", grid=(npair,),
        in_specs=[tok, tok, tok, tok, hm, pl.BlockSpec((2, DN_HEADS, 8, 128), lambda n: (n, 0, 0, 0)),
                  pl.BlockSpec((PAIR, DN_W), lambda n: (n, 3)), _full((1, 128))],
        out_specs=[tok, tok, tok, pl.BlockSpec((2, DN_HEADS, DN_HEAD_DIM, DN_HEAD_DIM), lambda n: (n, 0, 0, 0))],
        out_shape=[jax.ShapeDtypeStruct((t, DN_W), F32)] * 3
                  + [jax.ShapeDtypeStruct((2 * npair, DN_HEADS, DN_HEAD_DIM, DN_HEAD_DIM), F32)],
        scratch_shapes=[pltpu.VMEM((DN_HEADS, DN_HEAD_DIM, DN_HEAD_DIM), F32)],
        compiler_params=_params("arbitrary"),
    )(u, w, qg, kd, a_qk, dlast, pd, dn_w)


def _dn_scan_bwd(dout, o, vnew, sall, w, qg, kd, a_qk, dlast, pd, dn_w):
    t = o.shape[0]
    npair = t // PAIR
    rev = lambda n: npair - 1 - n

    def body(do_ref, o_ref, vn_ref, sall_ref, w_ref, qg_ref, kd_ref, a_ref, dl_ref, z_ref, nw_ref,
             dz_ref, du_ref, dw_ref, dqg_ref, dkd_ref, da_ref, ddl_ref, acc_ref, ds_ref):
        @pl.when(pl.program_id(0) == 0)
        def _():
            ds_ref[...] = jnp.zeros_like(ds_ref)
            acc_ref[...] = jnp.zeros_like(acc_ref)

        nw = nw_ref[...]
        dnw = jnp.zeros((1, 128), F32)
        for c in (1, 0):
            rows = slice(c * DN_CHUNK, (c + 1) * DN_CHUNK)
            for h in range(DN_HEADS):
                hs = slice(h * DN_HEAD_DIM, (h + 1) * DN_HEAD_DIM)
                o, z, dout = o_ref[rows, hs], z_ref[rows, hs], do_ref[rows, hs]
                r = lax.rsqrt(jnp.mean(o * o, axis=-1, keepdims=True) + EPS)
                oh = o * r
                sz = _sigmoid(z)
                dz_ref[rows, hs] = dout * (oh * nw) * (sz + z * sz * (1.0 - sz))
                don = dout * (z * sz)
                dnw += jnp.sum(don * oh, axis=0, keepdims=True)
                doh = don * nw
                do = r * (doh - oh * jnp.mean(doh * oh, axis=-1, keepdims=True))

                st, dsp = sall_ref[c, h], ds_ref[h]
                vn, a, kdv, wv, qgv = vn_ref[rows, hs], a_ref[h, rows, :], kd_ref[rows, hs], w_ref[rows, hs], qg_ref[rows, hs]
                da_ref[h, rows, :] = _dot(do, vn, NT)
                dvn = _dot(a, do, TN) + _dot(kdv, dsp)
                du_ref[rows, hs] = dvn
                dqg_ref[rows, hs] = _dot(do, st, NT)
                dkd_ref[rows, hs] = _dot(vn, dsp, NT)
                dw_ref[rows, hs] = -_dot(dvn, st, NT)
                ddl = jnp.sum(jnp.sum(dsp * st, axis=1, keepdims=True), axis=0, keepdims=True)
                ddl_ref[c, h] = jnp.broadcast_to(ddl, (8, 128))
                ds_ref[h] = dsp * dl_ref[c, h][0:1, :] + _dot(qgv, do, TN) - _dot(wv, dvn, TN)
        acc_ref[0:1, :] += dnw

    tok = pl.BlockSpec((PAIR, DN_W), lambda n: (rev(n), 0))
    hm = pl.BlockSpec((DN_HEADS, PAIR, DN_CHUNK), lambda n: (0, rev(n), 0))
    sc = pl.BlockSpec((2, DN_HEADS, 8, 128), lambda n: (rev(n), 0, 0, 0))
    return pl.pallas_call(
        body, name="dn_scan_bwd", grid=(npair,),
        in_specs=[tok, tok, tok, pl.BlockSpec((2, DN_HEADS, DN_HEAD_DIM, DN_HEAD_DIM), lambda n: (rev(n), 0, 0, 0)),
                  tok, tok, tok, hm, sc, pl.BlockSpec((PAIR, DN_W), lambda n: (rev(n), 3)), _full((1, 128))],
        out_specs=[tok] * 5 + [hm, sc, _full((8, 128))],
        out_shape=[jax.ShapeDtypeStruct((t, DN_W), F32)] * 5 + [jax.ShapeDtypeStruct((DN_HEADS, t, DN_CHUNK), F32),
                   jax.ShapeDtypeStruct((2 * npair, DN_HEADS, 8, 128), F32), jax.ShapeDtypeStruct((8, 128), F32)],
        scratch_shapes=[pltpu.VMEM((DN_HEADS, DN_HEAD_DIM, DN_HEAD_DIM), F32)],
        compiler_params=_params("arbitrary"),
    )(dout, o, vnew, sall, w, qg, kd, a_qk, dlast, pd, dn_w)


def _dn_intra_bwd(qn, kn, vc, gc, gr, tinv, a_qk, du, dw, dqg, dkd, da_qk, ddlast, dlast):
    t = qn.shape[0]
    npair = t // PAIR

    def body(q_ref, k_ref, v_ref, gc_ref, gr_ref, ti_ref, a_ref, du_ref, dw_ref, dqg_ref, dkd_ref, da_ref, ddl_ref, dl_ref,
             dq_ref, dk_ref, dv_ref, dg_ref):
        gc_v, gr_v = gc_ref[...], gr_ref[...]
        low, strict = _tri(DN_CHUNK), _tri(DN_CHUNK, True)
        ones = jnp.ones((DN_CHUNK, 128), F32)
        lane = lax.broadcasted_iota(jnp.int32, (DN_CHUNK, 128), 1)
        rowi = lax.broadcasted_iota(jnp.int32, (DN_CHUNK, 128), 0)
        rsum = lambda v: jnp.sum(v, axis=-1, keepdims=True)
        for c in range(2):
            rows = slice(c * DN_CHUNK, (c + 1) * DN_CHUNK)
            dgc = jnp.zeros((DN_CHUNK, 128), F32)
            for h in range(DN_HEADS):
                hs = slice(h * DN_HEAD_DIM, (h + 1) * DN_HEAD_DIM)
                q, k, v = q_ref[rows, hs], k_ref[rows, hs], v_ref[rows, hs]
                beta, gam, dm = _chunk_gates(gc_v, gr_v, c, h)
                tinv, a = ti_ref[h, rows, :], a_ref[h, rows, :]
                du, dw, dqg, dkd = du_ref[rows, hs], dw_ref[rows, hs], dqg_ref[rows, hs], dkd_ref[rows, hs]
                kb = k * beta
                eg = jnp.exp(gam)
                gl = gam[DN_CHUNK - 1:DN_CHUNK, :]
                ekd = jnp.exp(gl - gam)
                kbg, vb = kb * eg, v * beta
                lmat = jnp.where(strict, _dot(kb, k, NT) * dm, 0.0)

                dti = _dot(du, vb, NT) + _dot(dw, kbg, NT)
                dvb = _dot(tinv, du, TN)
                dkbg = _dot(tinv, dw, TN)
                dl = jnp.where(strict, -_dot(_dot(tinv, dti, TN, HI), tinv, NT, HI), 0.0)
                dmm = dl * dm
                dam = jnp.where(low, da_ref[h, rows, :], 0.0)
                dn = dam * dm
                e = dl * lmat + dam * a
                dgam = _dot(e, ones, NN, HI) - _dot(e, ones, TN, HI)
                dkb = _dot(dmm, k) + dkbg * eg
                dk_ref[rows, hs] = _dot(dmm, kb, TN) + _dot(dn, q, TN) + dkd * ekd + dkb * beta
                dq_ref[rows, hs] = _dot(dn, k) + dqg * eg
                dv_ref[rows, hs] = dvb * beta
                t_kd = rsum(dkd * (k * ekd))
                dgam = dgam + rsum(dqg * (q * eg)) + rsum(dkbg * kbg) - t_kd
                dgl = jnp.sum(t_kd, axis=0, keepdims=True) + ddl_ref[c, h][0:1, :] * dl_ref[c, h][0:1, :]
                dgam = dgam + jnp.where(rowi == DN_CHUNK - 1, dgl, 0.0)
                dbeta = rsum(dkb * k) + rsum(dvb * v)
                dg = _dot(low.astype(F32), dgam, TN, HI)
                dgc += jnp.where(lane == h, dbeta, 0.0) + jnp.where(lane == DN_HEADS + h, dg, 0.0)
            dg_ref[rows, :] = dgc

    tok = lambda w: pl.BlockSpec((PAIR, w), lambda n: (n, 0))
    hm = pl.BlockSpec((DN_HEADS, PAIR, DN_CHUNK), lambda n: (0, n, 0))
    sc = pl.BlockSpec((2, DN_HEADS, 8, 128), lambda n: (n, 0, 0, 0))
    return pl.pallas_call(
        body, name="dn_intra_bwd", grid=(npair,),
        in_specs=[tok(DN_W), tok(DN_W), tok(DN_W), tok(128), pl.BlockSpec((8, PAIR), lambda n: (0, n)), hm, hm,
                  tok(DN_W), tok(DN_W), tok(DN_W), tok(DN_W), hm, sc, sc],
        out_specs=[tok(DN_W), tok(DN_W), tok(DN_W), tok(128)],
        out_shape=[jax.ShapeDtypeStruct((t, DN_W), F32)] * 3 + [jax.ShapeDtypeStruct((t, 128), F32)],
        compiler_params=_params("parallel"),
    )(qn, kn, vc, gc, gr, tinv, a_qk, du, dw, dqg, dkd, da_qk, ddlast, dlast)


def _dn_prep_bwd(pd, conv_w, al_vec, dt_vec, dqn, dkn, dvc, dgc, tm):
    t = pd.shape[0]

    def body(cur_ref, prev_ref, ba_ref, cw_ref, al_ref, dt_ref, dq_ref, dk_ref, dv_ref, dg_ref,
             dy_ref, dba_ref, accw_ref, accg_ref, xs_ref, dc_ref):
        @pl.when(pl.program_id(0) == 0)
        def _():
            accw_ref[...] = jnp.zeros_like(accw_ref)
            accg_ref[...] = jnp.zeros_like(accg_ref)

        _conv_window(cur_ref, prev_ref, xs_ref, tm)
        y = _conv_taps(xs_ref, cw_ref, tm)
        sg = _sigmoid(y)
        c = y * sg
        for h in range(DN_HEADS):
            qs = slice(h * DN_HEAD_DIM, (h + 1) * DN_HEAD_DIM)
            ksl = slice(DN_W + h * DN_HEAD_DIM, DN_W + (h + 1) * DN_HEAD_DIM)
            for src, sl, scale in ((dq_ref, qs, DN_SCALE), (dk_ref, ksl, 1.0)):
                xh = c[:, sl]
                r = lax.rsqrt(jnp.sum(xh * xh, axis=-1, keepdims=True) + EPS)
                unit = xh * r
                dn = src[:, qs] * scale
                dc_ref[:, sl] = r * (dn - unit * jnp.sum(dn * unit, axis=-1, keepdims=True))
        dc_ref[:, 2 * DN_W:3 * DN_W] = dv_ref[...]
        dy = dc_ref[...] * (sg + y * sg * (1.0 - sg))
        dy_ref[...] = dy
        for j in range(DN_CONV):
            accw_ref[j:j + 1, :] += jnp.sum(dy * xs_ref[pl.ds(HALO - DN_CONV + 1 + j, tm), :], axis=0, keepdims=True)

        beta, g, pre = _gate_values(ba_ref[...], al_ref[...], dt_ref[...])
        dgb = dg_ref[...]
        lane = lax.broadcasted_iota(jnp.int32, dgb.shape, 1)
        is_b, is_a = lane < DN_HEADS, (lane >= DN_HEADS) & (lane < 2 * DN_HEADS)
        dpre = dgb * (-jnp.exp(al_ref[...])) * _sigmoid(pre)
        dba_ref[...] = jnp.where(is_b, dgb * beta * (1.0 - beta), jnp.where(is_a, dpre, 0.0))
        accg_ref[0:1, :] += jnp.sum(jnp.where(is_a, dgb * g, 0.0), axis=0, keepdims=True)
        accg_ref[1:2, :] += jnp.sum(jnp.where(is_a, dpre, 0.0), axis=0, keepdims=True)

    tok = lambda w: pl.BlockSpec((tm, w), lambda i: (i, 0))
    return pl.pallas_call(
        body, name="dn_prep_bwd", grid=(t // tm,),
        in_specs=_dn_prep_specs(tm, t) + [tok(DN_W), tok(DN_W), tok(DN_W), tok(128)],
        out_specs=[tok(CONV_CH), tok(128), _full((8, CONV_CH)), _full((8, 128))],
        out_shape=[jax.ShapeDtypeStruct((t, CONV_CH), F32), jax.ShapeDtypeStruct((t, 128), F32),
                   jax.ShapeDtypeStruct((8, CONV_CH), F32), jax.ShapeDtypeStruct((8, 128), F32)],
        scratch_shapes=[pltpu.VMEM((HALO + tm, CONV_CH), F32), pltpu.VMEM((tm, CONV_CH), F32)],
        compiler_params=_params("arbitrary"),
    )(pd, pd, pd, conv_w, al_vec, dt_vec, dqn, dkn, dvc, dgc)


def _dn_conv_bwd(dy, dz, dba, conv_w, tm):
    t = dy.shape[0]
    nt = t // tm

    def body(cur_ref, nxt_ref, dz_ref, dba_ref, cw_ref, o_ref, ds_ref):
        nxt = jnp.where(pl.program_id(0) < nt - 1, nxt_ref[...], 0.0)
        ds_ref[0:tm, :] = cur_ref[...]
        ds_ref[tm:tm + HALO, :] = nxt
        dx = cw_ref[0:1, :] * ds_ref[pl.ds(DN_CONV - 1, tm), :]
        for j in range(1, DN_CONV):
            dx += cw_ref[j:j + 1, :] * ds_ref[pl.ds(DN_CONV - 1 - j, tm), :]
        o_ref[:, 0:CONV_CH] = dx.astype(o_ref.dtype)
        o_ref[:, CONV_CH:CONV_CH + DN_W] = dz_ref[...].astype(o_ref.dtype)
        o_ref[:, CONV_CH + DN_W:DN_COLS] = dba_ref[...].astype(o_ref.dtype)

    tok = lambda w: pl.BlockSpec((tm, w), lambda i: (i, 0))
    return pl.pallas_call(
        body, name="dn_conv_bwd", grid=(nt,),
        in_specs=[tok(CONV_CH),
                  pl.BlockSpec((HALO, CONV_CH), lambda i: (jnp.minimum((i + 1) * (tm // HALO), t // HALO - 1), 0)),
                  tok(DN_W), tok(128), _full((DN_CONV, CONV_CH))],
        out_specs=tok(DN_COLS),
        out_shape=jax.ShapeDtypeStruct((t, DN_COLS), MXU_DTYPE),
        scratch_shapes=[pltpu.VMEM((tm + HALO, CONV_CH), F32)],
        compiler_params=_params("parallel"),
    )(dy, dy, dz, dba, conv_w)


def _pad_lanes(v, offset=0):
    return jnp.zeros((1, 128), F32).at[0, offset:offset + v.shape[0]].set(v.astype(F32))


def _local_step(x, p, tgt, sm, w):
    t = x.shape[0]
    tm = min(512, t // 2)
    tm_s = min(256, t // 2)

    w_in = w["w_in"]
    wa = w_in[:, :ATTN_Q + 2 * ATTN_KV]
    wd = jnp.pad(w_in[:, ATTN_Q + 2 * ATTN_KV:], ((0, 0), (0, DN_COLS - (D_IN - ATTN_Q - 2 * ATTN_KV))))
    wo_a, wo_d = w["w_o"][:ATTN_Q], w["w_o"][ATTN_Q:]
    w_proj = jnp.transpose(w["w_proj4"], (1, 0, 2)).reshape(PLE_DIM, D_MODEL)
    conv_w = w["conv_w"]
    al_vec, dt_vec = _pad_lanes(sm["a_log"], DN_HEADS), _pad_lanes(sm["dt_bias"], DN_HEADS)
    sinks_vec = _pad_lanes(sm["sinks"])
    dn_w = sm["dn_norm"].reshape(1, 128)
    row = lambda v: v.reshape(1, D_MODEL)
    cos, sin = _rope_tables(t)

    u, pa, pd = _inproj(x, row(sm["norm_mix"]), wa, wd, tm_s)
    ao = _attn_fwd(pa, cos, sin, sinks_vec)
    qn, kn, vc, gc, gr = _dn_prep(pd, conv_w, al_vec, dt_vec, tm_s)
    uu, ww, qg, kd, a_qk, tinv, dlast = _dn_intra(qn, kn, vc, gc, gr)
    dn_out, o, vnew, sall = _dn_scan_fwd(uu, ww, qg, kd, a_qk, dlast, pd, dn_w)
    h1 = _oproj(x, ao, dn_out, wo_a, wo_d, tm)
    m, a, h2 = _mlp_fwd(h1, row(sm["norm_mlp"]), w["w_up4"], w["w_down"], tm)
    dh2, dh2b, dgp, dpp, n3, pb, acc_ple = _ple_loss(h2, p, tgt, row(sm["norm_ple"]), row(sm["norm_final"]),
                                                     w["w_gate"], w_proj, tm_s)
    s, da, dh1, dh1b, acc_mlp = _mlp_bwd(dh2, dh2b, a, h1, row(sm["norm_mlp"]), w["w_up4"], w["w_down"], tm)
    dao, ddn = _oproj_bwd(dh1b, wo_a, wo_d, tm)
    dz, du, dw, dqg, dkd, da_qk, ddlast, acc_dn = _dn_scan_bwd(ddn, o, vnew, sall, ww, qg, kd, a_qk, dlast, pd, dn_w)
    dqn, dkn, dvc, dgc = _dn_intra_bwd(qn, kn, vc, gc, gr, tinv, a_qk, du, dw, dqg, dkd, da_qk, ddlast, dlast)
    dy, dba, acc_conv, acc_gate = _dn_prep_bwd(pd, conv_w, al_vec, dt_vec, dqn, dkn, dvc, dgc, tm_s)
    d_dn = _dn_conv_bwd(dy, dz, dba, conv_w, tm_s)
    dq, dk, dv, acc_attn = _attn_bwd(pa, cos, sin, sinks_vec, dao)
    wq, wk, wv = wa[:, :ATTN_Q], wa[:, ATTN_Q:ATTN_Q + ATTN_KV], wa[:, ATTN_Q + ATTN_KV:]
    dx, acc_mix = _inproj_bwd(x, dh1, row(sm["norm_mix"]), [dq, dk, dv, d_dn], [wq, wk, wv, wd], tm_s)

    aob, dnb = ao.astype(MXU_DTYPE), dn_out.astype(MXU_DTYPE)
    g_w_in = jnp.concatenate([
        _wgrad(u, dq, "wgrad_q", D_MODEL, ATTN_Q, tm), _wgrad(u, dk, "wgrad_k", D_MODEL, ATTN_KV, tm),
        _wgrad(u, dv, "wgrad_v", D_MODEL, ATTN_KV, tm),
        _wgrad(u, d_dn, "wgrad_dn", D_MODEL, DN_COLS, tm)[:, :D_IN - ATTN_Q - 2 * ATTN_KV]], axis=1)
    g_w_o = jnp.concatenate([_wgrad(aob, dh1b, "wgrad_oa", ATTN_Q, D_MODEL, tm),
                             _wgrad(dnb, dh1b, "wgrad_od", DN_W, D_MODEL, tm)], axis=0)
    g_w_up4 = jnp.stack([_wgrad(m, da, "wgrad_up%d" % k, D_MODEL, FF_BLOCK, tm, col0=k * FF_BLOCK, ncols=FF_BLOCK)
                         for k in range(FF_BLOCKS)])
    g_w_down = _wgrad(s, dh2b, "wgrad_down", FF_BLOCK, D_MODEL, tm)
    g_w_gate = _wgrad(n3, dgp, "wgrad_gate", D_MODEL, D_MODEL, tm)
    g_w_proj = _wgrad(pb, dpp, "wgrad_proj", PLE_DIM, D_MODEL, tm)
    grads = dict(w_in=g_w_in, w_o=g_w_o, w_up4=g_w_up4, w_down=g_w_down, w_gate=g_w_gate, w_proj=g_w_proj)
    sums = dict(loss=acc_ple[2, 0], norm_final=acc_ple[0], norm_ple=acc_ple[1], norm_mlp=acc_mlp[0], norm_mix=acc_mix[0],
                dn_norm=acc_dn[0], sinks=acc_attn[0, :ATTN_HEADS], a_log=acc_gate[0, DN_HEADS:2 * DN_HEADS],
                dt_bias=acc_gate[1, DN_HEADS:2 * DN_HEADS], conv_w=acc_conv[:DN_CONV])
    return sums, dx, grads


MESH = pl.DeviceIdType.MESH
ANY = pl.BlockSpec(memory_space=pl.ANY)
N_CHIPS = 4
N_DEV = 8


def _place():
    x, y, c = lax.axis_index("x"), lax.axis_index("y"), lax.axis_index("c")
    chips = [(1 - x, y), (x, 1 - y), (1 - x, 1 - y)]
    return x, y, c, chips


def _gather_weights(shards, conv_s):
    n = len(shards)
    per = 7

    def body(*refs):
        in_refs, conv_ref = refs[:n], refs[n]
        out_refs, conv_out = refs[n + 1:2 * n + 1], refs[2 * n + 1]
        send_sems, recv_sems, local_sems = refs[2 * n + 2:]
        x, y, c, chips = _place()
        sibling = (x, y, 1 - c)

        def blk(a, px, py, pc):
            hr = in_refs[a].shape[0] // 2
            return out_refs[a].at[2 * px + py, pl.ds(pc * hr, hr), :]

        def mine(a):
            hr = in_refs[a].shape[0] // 2
            return in_refs[a].at[pl.ds(c * hr, hr), :]

        def rcopy(a, k, block, to, src=None):
            return pltpu.make_async_remote_copy(
                src_ref=blk(a, *block) if src is None else src, dst_ref=blk(a, *block),
                send_sem=send_sems.at[per * a + k], recv_sem=recv_sems.at[per * a + k],
                device_id=to, device_id_type=MESH)

        def ccopy(j, chip):
            return pltpu.make_async_remote_copy(
                src_ref=conv_ref, dst_ref=conv_out.at[2 * x + y],
                send_sem=send_sems.at[per * n + j], recv_sem=recv_sems.at[per * n + j],
                device_id=(*chip, c), device_id_type=MESH)

        local = [pltpu.make_async_copy(mine(a), blk(a, x, y, c), local_sems.at[a]) for a in range(n)]
        local.append(pltpu.make_async_copy(conv_ref, conv_out.at[2 * x + y], local_sems.at[n]))
        for cp in local:
            cp.start()
        started = []
        for a in range(n):
            first = [rcopy(a, 0, (x, y, c), sibling, src=mine(a))]
            first += [rcopy(a, 1 + j, (x, y, c), (*chip, c), src=mine(a)) for j, chip in enumerate(chips)]
            for cp in first:
                cp.start()
            started += first
        conv_sends = [ccopy(j, chip) for j, chip in enumerate(chips)]
        for cp in conv_sends:
            cp.start()
        started += conv_sends
        for a in range(n):
            for j, chip in enumerate(chips):
                rcopy(a, 1 + j, (*chip, c), (x, y, c)).wait_recv()
                fwd = rcopy(a, 4 + j, (*chip, c), sibling)
                fwd.start()
                started.append(fwd)
        for a in range(n):
            rcopy(a, 0, (x, y, 1 - c), (x, y, c)).wait_recv()
            for j, chip in enumerate(chips):
                rcopy(a, 4 + j, (*chip, 1 - c), (x, y, c)).wait_recv()
        for j, chip in enumerate(chips):
            pltpu.make_async_remote_copy(
                src_ref=conv_ref, dst_ref=conv_out.at[2 * chip[0] + chip[1]],
                send_sem=send_sems.at[per * n + j], recv_sem=recv_sems.at[per * n + j],
                device_id=(*chip, c), device_id_type=MESH).wait_recv()
        for cp in started:
            cp.wait_send()
        for cp in local:
            cp.wait()

    nsem = per * n + 3
    out_shape = [jax.ShapeDtypeStruct((N_CHIPS,) + s.shape, s.dtype) for s in shards]
    out_shape.append(jax.ShapeDtypeStruct((N_CHIPS,) + conv_s.shape, conv_s.dtype))
    return pl.pallas_call(
        body, name="gather_weights", in_specs=[ANY] * (n + 1), out_specs=[ANY] * (n + 1), out_shape=out_shape,
        scratch_shapes=[pltpu.SemaphoreType.DMA((nsem,)), pltpu.SemaphoreType.DMA((nsem,)),
                        pltpu.SemaphoreType.DMA((n + 1,))],
    )(*shards, conv_s)


def _exchange_halves(grads):
    n = len(grads)

    def body(*refs):
        g_refs, own_refs, got_refs = refs[:n], refs[n:2 * n], refs[2 * n:3 * n]
        send_sems, recv_sems, local_sems = refs[3 * n:]
        x, y, c, _ = _place()
        local, remote = [], []
        for a in range(n):
            hr = g_refs[a].shape[1] // 2
            local.append(pltpu.make_async_copy(g_refs[a].at[:, pl.ds(c * hr, hr), :], own_refs[a], local_sems.at[a]))
            remote.append(pltpu.make_async_remote_copy(
                src_ref=g_refs[a].at[:, pl.ds((1 - c) * hr, hr), :], dst_ref=got_refs[a],
                send_sem=send_sems.at[a], recv_sem=recv_sems.at[a], device_id=(x, y, 1 - c), device_id_type=MESH))
        for cp in local + remote:
            cp.start()
        for cp in remote:
            cp.wait_recv()
        for cp in remote:
            cp.wait_send()
        for cp in local:
            cp.wait()

    half = [jax.ShapeDtypeStruct((g.shape[0], g.shape[1] // 2, g.shape[2]), g.dtype) for g in grads]
    return pl.pallas_call(
        body, name="exchange_halves", in_specs=[ANY] * n, out_specs=[ANY] * (2 * n), out_shape=half + half,
        scratch_shapes=[pltpu.SemaphoreType.DMA((n,)), pltpu.SemaphoreType.DMA((n,)), pltpu.SemaphoreType.DMA((n,))],
    )(*grads)


def _scatter_to_chips(sums32, sums16):
    n = len(sums32)

    def body(*refs):
        s32, s16 = refs[:n], refs[n:2 * n]
        own_refs, got_refs = refs[2 * n:3 * n], refs[3 * n:4 * n]
        send_sems, recv_sems, local_sems = refs[4 * n:]
        x, y, c, chips = _place()
        local, remote = [], []
        for a in range(n):
            local.append(pltpu.make_async_copy(s32[a].at[2 * x + y], own_refs[a], local_sems.at[a]))
            for j, chip in enumerate(chips):
                remote.append(pltpu.make_async_remote_copy(
                    src_ref=s16[a].at[2 * chip[0] + chip[1]], dst_ref=got_refs[a].at[j],
                    send_sem=send_sems.at[3 * a + j], recv_sem=recv_sems.at[3 * a + j],
                    device_id=(*chip, c), device_id_type=MESH))
        for cp in local + remote:
            cp.start()
        for cp in remote:
            cp.wait_recv()
        for cp in remote:
            cp.wait_send()
        for cp in local:
            cp.wait()

    own = [jax.ShapeDtypeStruct(s.shape[1:], F32) for s in sums32]
    got = [jax.ShapeDtypeStruct((3,) + s.shape[1:], BF16) for s in sums32]
    return pl.pallas_call(
        body, name="scatter_to_chips", in_specs=[ANY] * (2 * n), out_specs=[ANY] * (2 * n), out_shape=own + got,
        scratch_shapes=[pltpu.SemaphoreType.DMA((3 * n,)), pltpu.SemaphoreType.DMA((3 * n,)),
                        pltpu.SemaphoreType.DMA((n,))],
    )(*sums32, *sums16)


def _share_halves(halves):
    n = len(halves)

    def body(*refs):
        h_refs, out_refs = refs[:n], refs[n:2 * n]
        send_sems, recv_sems, local_sems = refs[2 * n:]
        x, y, c, _ = _place()
        local, remote = [], []
        for a in range(n):
            local.append(pltpu.make_async_copy(h_refs[a], out_refs[a].at[c], local_sems.at[a]))
            remote.append(pltpu.make_async_remote_copy(
                src_ref=h_refs[a], dst_ref=out_refs[a].at[c], send_sem=send_sems.at[a], recv_sem=recv_sems.at[a],
                device_id=(x, y, 1 - c), device_id_type=MESH))
        for cp in local + remote:
            cp.start()
        for a in range(n):
            pltpu.make_async_remote_copy(
                src_ref=h_refs[a], dst_ref=out_refs[a].at[1 - c], send_sem=send_sems.at[a], recv_sem=recv_sems.at[a],
                device_id=(x, y, 1 - c), device_id_type=MESH).wait_recv()
        for cp in remote:
            cp.wait_send()
        for cp in local:
            cp.wait()

    return pl.pallas_call(
        body, name="share_halves", in_specs=[ANY] * n, out_specs=[ANY] * n,
        out_shape=[jax.ShapeDtypeStruct((2,) + h.shape, h.dtype) for h in halves],
        scratch_shapes=[pltpu.SemaphoreType.DMA((n,)), pltpu.SemaphoreType.DMA((n,)), pltpu.SemaphoreType.DMA((n,))],
    )(*halves)


SMALL_ROWS, SMALL_COLS = 16, CONV_CH


def _allreduce_small(block):
    m_per, ncol = block.shape

    def body(x_ref, sum_ref, all_ref, send_sems, recv_sems, local_sem):
        x, y, c, chips = _place()
        me, sibling = (x, y, c), (x, y, 1 - c)

        def rows(px, py, pc):
            return all_ref.at[pl.ds((4 * px + 2 * py + pc) * m_per, m_per), :]

        def copy(k, block_of, to, src=None):
            return pltpu.make_async_remote_copy(
                src_ref=rows(*block_of) if src is None else src, dst_ref=rows(*block_of),
                send_sem=send_sems.at[k], recv_sem=recv_sems.at[k], device_id=to, device_id_type=MESH)

        mine = pltpu.make_async_copy(x_ref, rows(*me), local_sem)
        mine.start()
        first = [copy(0, me, sibling, src=x_ref)]
        first += [copy(1 + j, me, (*chip, c), src=x_ref) for j, chip in enumerate(chips)]
        for cp in first:
            cp.start()
        passed = [copy(4 + j, (*chip, c), sibling) for j, chip in enumerate(chips)]
        for j, chip in enumerate(chips):
            copy(1 + j, (*chip, c), me).wait_recv()
            passed[j].start()
        copy(0, sibling, me).wait_recv()
        for j, chip in enumerate(chips):
            copy(4 + j, (*chip, 1 - c), me).wait_recv()
        for cp in first + passed:
            cp.wait_send()
        mine.wait()
        total = all_ref[0:m_per, :]
        for d in range(1, N_DEV):
            total = total + all_ref[d * m_per:(d + 1) * m_per, :]
        sum_ref[...] = total

    vm = pl.BlockSpec(memory_space=pltpu.VMEM)
    return pl.pallas_call(
        body, name="allreduce_small", in_specs=[vm], out_specs=vm,
        out_shape=jax.ShapeDtypeStruct((m_per, ncol), F32),
        scratch_shapes=[pltpu.VMEM((N_DEV * m_per, ncol), F32), pltpu.SemaphoreType.DMA((7,)),
                        pltpu.SemaphoreType.DMA((7,)), pltpu.SemaphoreType.DMA],
    )(block)


def _row_tile(rows, cols):
    tile = rows
    while tile * cols * 4 > (1 << 20) and tile % 16 == 0:
        tile //= 2
    return tile


def _elementwise(fn, name, ins, out_dtypes):
    rows, cols = ins[0].shape
    tile = _row_tile(rows, cols)

    def body(*refs):
        outs = fn(*[r[...] for r in refs[:len(ins)]])
        for o_ref, o in zip(refs[len(ins):], outs):
            o_ref[...] = o.astype(o_ref.dtype)

    spec = pl.BlockSpec((tile, cols), lambda i: (i, 0))
    return pl.pallas_call(
        body, name=name, grid=(rows // tile,), in_specs=[spec] * len(ins), out_specs=[spec] * len(out_dtypes),
        out_shape=[jax.ShapeDtypeStruct((rows, cols), d) for d in out_dtypes],
        compiler_params=_params("parallel"),
    )(*ins)


def _adamw_tile(w, g, m, v):
    m = ADAM_B1 * m + (1.0 - ADAM_B1) * g
    v = ADAM_B2 * v + (1.0 - ADAM_B2) * jnp.square(g)
    m_hat = m / (1.0 - ADAM_B1 ** ADAM_STEP)
    v_hat = v / (1.0 - ADAM_B2 ** ADAM_STEP)
    delta = -ADAM_LR * (m_hat / (jnp.sqrt(v_hat) + ADAM_EPS) + ADAM_WD * w)
    return delta, m, v


def _adamw(name, w, g, m, v):
    return _elementwise(_adamw_tile, name, [w, g, m, v], [F32, F32, F32])


def _reduce_scatter(grads):
    names = list(grads)
    own_a, got_a = (lambda r: (r[:len(names)], r[len(names):]))(_exchange_halves([grads[k] for k in names]))
    s32, s16 = [], []
    for k, o, g in zip(names, own_a, got_a):
        flat = lambda v: v.reshape(-1, v.shape[-1])
        a32, a16 = _elementwise(lambda p, q: (p + q, p + q), "chip_sum_" + k, [flat(o), flat(g)], [F32, BF16])
        s32.append(a32.reshape(o.shape))
        s16.append(a16.reshape(o.shape))
    res = _scatter_to_chips(s32, s16)
    own_b, got_b = res[:len(names)], res[len(names):]
    halves = []
    for k, o, g in zip(names, own_b, got_b):
        (h,) = _elementwise(lambda p, q0, q1, q2: (((p + q0.astype(F32)) + q1.astype(F32)) + q2.astype(F32),),
                            "mesh_sum_" + k, [o, g[0], g[1], g[2]], [F32])
        halves.append(h)
    full = _share_halves(halves)
    return {k: f.reshape(-1, f.shape[-1]) for k, f in zip(names, full)}


def kernel(x, p, norm_mix, w_in, conv_w, a_log, dt_bias, dn_norm, sinks, w_o, norm_mlp, w_up, w_down, norm_ple, w_ple_gate, w_ple_proj, norm_final, loss_target, m_norm_mix, m_w_in, m_conv_w, m_a_log, m_dt_bias, m_dn_norm, m_sinks, m_w_o, m_norm_mlp, m_w_up, m_w_down, m_norm_ple, m_w_ple_gate, m_w_ple_proj, m_norm_final, v_norm_mix, v_w_in, v_conv_w, v_a_log, v_dt_bias, v_dn_norm, v_sinks, v_w_o, v_norm_mlp, v_w_up, v_w_down, v_norm_ple, v_w_ple_gate, v_w_ple_proj, v_norm_final):
    chip = 2 * lax.axis_index("x") + lax.axis_index("y")
    big = dict(w_in=w_in[0], w_o=w_o[0], w_up=w_up[0], w_down=w_down[0], w_gate=w_ple_gate[0], w_proj=w_ple_proj[0])
    big_m = dict(w_in=m_w_in[0], w_o=m_w_o[0], w_up=m_w_up[0], w_down=m_w_down[0], w_gate=m_w_ple_gate[0], w_proj=m_w_ple_proj[0])
    big_v = dict(w_in=v_w_in[0], w_o=v_w_o[0], w_up=v_w_up[0], w_down=v_w_down[0], w_gate=v_w_ple_gate[0], w_proj=v_w_ple_proj[0])
    names = list(big)

    gathered = _gather_weights([big[k].astype(BF16) for k in names], conv_w[0])
    gw = dict(zip(names, gathered[:-1]))
    conv_full = jnp.transpose(gathered[-1], (1, 0, 2)).reshape(DN_CONV, CONV_CH)
    w = dict(w_in=jnp.transpose(gw["w_in"], (1, 0, 2)).reshape(D_MODEL, D_IN),
             w_o=gw["w_o"].reshape(D_MODEL, D_MODEL), w_up4=gw["w_up"], w_down=gw["w_down"].reshape(D_FF, D_MODEL),
             w_gate=gw["w_gate"].reshape(D_MODEL, D_MODEL), w_proj4=gw["w_proj"], conv_w=conv_full)
    sm = dict(norm_mix=norm_mix[0], a_log=a_log[0], dt_bias=dt_bias[0], dn_norm=dn_norm[0], sinks=sinks[0],
              norm_mlp=norm_mlp[0], norm_ple=norm_ple[0], norm_final=norm_final)

    sums, grad_x, g = _local_step(x[0], p[0, 0], loss_target[0], sm, w)

    per_chip = dict(
        w_in=jnp.transpose(g["w_in"].reshape(D_MODEL, N_CHIPS, D_IN // N_CHIPS), (1, 0, 2)),
        w_o=g["w_o"].reshape(N_CHIPS, D_MODEL // N_CHIPS, D_MODEL),
        w_up=g["w_up4"],
        w_down=g["w_down"].reshape(N_CHIPS, D_FF // N_CHIPS, D_MODEL),
        w_gate=g["w_gate"].reshape(N_CHIPS, D_MODEL // N_CHIPS, D_MODEL),
        w_proj=jnp.transpose(g["w_proj"].reshape(PLE_DIM, N_CHIPS, D_MODEL // N_CHIPS), (1, 0, 2)))
    red = _reduce_scatter(per_chip)

    row = lambda v: jnp.zeros((SMALL_COLS,), F32).at[:v.shape[0]].set(v)
    misc = jnp.zeros((SMALL_COLS,), F32).at[0:4].set(sums["a_log"]).at[4:8].set(sums["dt_bias"]) \
        .at[8:16].set(sums["sinks"]).at[128:256].set(sums["dn_norm"]).at[256].set(sums["loss"])
    small = jnp.concatenate([sums["conv_w"], jnp.stack([row(sums["norm_mix"]), row(sums["norm_mlp"]), row(sums["norm_ple"]),
                                                        row(sums["norm_final"]), misc]),
                             jnp.zeros((SMALL_ROWS - 9, SMALL_COLS), F32)], axis=0)
    tot = _allreduce_small(small)
    loss = tot[8, 256]
    ncw = CONV_CH // N_CHIPS

    def pack(cw, nmix, nmlp, nple, nfin, al, dtb, sk, dnn):
        misc_p = jnp.zeros((SMALL_COLS,), F32).at[0:4].set(al).at[4:8].set(dtb).at[8:16].set(sk).at[128:256].set(dnn)
        cw_p = jnp.zeros((DN_CONV, SMALL_COLS), F32).at[:, :ncw].set(cw)
        return jnp.concatenate([cw_p, jnp.stack([row(nmix), row(nmlp), row(nple), row(nfin), misc_p]),
                                jnp.zeros((SMALL_ROWS - 9, SMALL_COLS), F32)], axis=0)

    def unpack(buf):
        return dict(conv_w=buf[0:4, :ncw][None], norm_mix=buf[4, :D_MODEL][None], norm_mlp=buf[5, :D_MODEL][None],
                    norm_ple=buf[6, :D_MODEL][None], norm_final=buf[7, :D_MODEL], a_log=buf[8, 0:4][None],
                    dt_bias=buf[8, 4:8][None], sinks=buf[8, 8:16][None], dn_norm=buf[8, 128:256][None])

    g_conv_shard = lax.dynamic_slice(tot[0:4], (0, chip * ncw), (DN_CONV, ncw))
    g_small = pack(g_conv_shard, tot[4, :D_MODEL], tot[5, :D_MODEL], tot[6, :D_MODEL], tot[7, :D_MODEL],
                   tot[8, 0:4], tot[8, 4:8], tot[8, 8:16], tot[8, 128:256])
    w_small = pack(conv_w[0], norm_mix[0], norm_mlp[0], norm_ple[0], norm_final, a_log[0], dt_bias[0], sinks[0], dn_norm[0])
    m_small = pack(m_conv_w[0], m_norm_mix[0], m_norm_mlp[0], m_norm_ple[0], m_norm_final, m_a_log[0], m_dt_bias[0],
                   m_sinks[0], m_dn_norm[0])
    v_small = pack(v_conv_w[0], v_norm_mix[0], v_norm_mlp[0], v_norm_ple[0], v_norm_final, v_a_log[0], v_dt_bias[0],
                   v_sinks[0], v_dn_norm[0])

    d_s, m_s, v_s = (unpack(b) for b in _adamw("adamw_small", w_small, g_small, m_small, v_small))
    g_s = unpack(g_small)
    out_g, out_d, out_m, out_v = dict(g_s), dict(d_s), dict(m_s), dict(v_s)
    ref_name = dict(w_in="w_in", w_o="w_o", w_up="w_up", w_down="w_down", w_gate="w_ple_gate", w_proj="w_ple_proj")
    for k in names:
        d_k, m_k, v_k = _adamw("adamw_" + k, big[k], red[k], big_m[k], big_v[k])
        out_g[ref_name[k]], out_d[ref_name[k]] = red[k][None], d_k[None]
        out_m[ref_name[k]], out_v[ref_name[k]] = m_k[None], v_k[None]
    order = ["norm_mix", "w_in", "conv_w", "a_log", "dt_bias", "dn_norm", "sinks", "w_o", "norm_mlp", "w_up", "w_down",
             "norm_ple", "w_ple_gate", "w_ple_proj", "norm_final"]
    return (loss, grad_x[None], *[out_g[k] for k in order], *[out_d[k] for k in order],
            *[out_m[k] for k in order], *[out_v[k] for k in order])
```

```python
import functools

import jax
import jax.numpy as jnp
from jax import lax
from jax.experimental import pallas as pl
from jax.experimental.pallas import tpu as pltpu

F32 = jnp.float32
BF16 = jnp.bfloat16
MXU_DTYPE = jnp.bfloat16
HI = lax.Precision.HIGHEST

D_MODEL = 1024
PLE_DIM = 256
ATTN_HEADS = 8
ATTN_KV_HEADS = 2
ATTN_GROUPS = ATTN_HEADS // ATTN_KV_HEADS
ATTN_HEAD_DIM = 64
ATTN_BLOCK = 128
ROPE_THETA = 10000.0
DN_HEADS = 4
DN_HEAD_DIM = 128
DN_CONV = 4
DN_CHUNK = 64
D_FF = 4 * D_MODEL
EPS = 1e-6
ATTN_Q = ATTN_HEADS * ATTN_HEAD_DIM
ATTN_KV = ATTN_KV_HEADS * ATTN_HEAD_DIM
DN_W = DN_HEADS * DN_HEAD_DIM
CONV_CH = 3 * DN_W
D_IN = ATTN_Q + 2 * ATTN_KV + 4 * DN_W + 2 * DN_HEADS
DN_COLS = 4 * DN_W + 128
DN_SCALE = DN_HEAD_DIM ** -0.5
ATTN_SCALE = ATTN_HEAD_DIM ** -0.5
FF_BLOCKS = 4
FF_BLOCK = D_FF // FF_BLOCKS

ADAM_LR = 0.001
ADAM_B1 = 0.9
ADAM_B2 = 0.999
ADAM_EPS = 1e-08
ADAM_WD = 0.01
ADAM_STEP = 10

V7X_VMEM_BYTES = 64 * 1024 * 1024
VMEM_LIMIT = 48 * 1024 * 1024

NN = ((1,), (0,))
NT = ((1,), (1,))
TN = ((0,), (0,))


def _dot(a, b, dims=NN, prec=None):
    return lax.dot_general(a, b, (dims, ((), ())), precision=prec, preferred_element_type=F32)


def _sigmoid(x):
    return 1.0 / (1.0 + jnp.exp(-x))


def _softplus(x):
    return jnp.maximum(x, 0.0) + jnp.log(1.0 + jnp.exp(-jnp.abs(x)))


def _params(*sem):
    return pltpu.CompilerParams(dimension_semantics=sem, vmem_limit_bytes=VMEM_LIMIT)


def _rms_fwd(xv, g):
    r = lax.rsqrt(jnp.mean(xv * xv, axis=-1, keepdims=True) + EPS)
    return xv * r * g


def _rms_bwd(xv, g, dn):
    r = lax.rsqrt(jnp.mean(xv * xv, axis=-1, keepdims=True) + EPS)
    xh = xv * r
    dg = jnp.sum(dn * xh, axis=0, keepdims=True)
    dxh = dn * g
    dx = r * (dxh - xh * jnp.mean(dxh * xh, axis=-1, keepdims=True))
    return dx, dg


def _full(shape):
    return pl.BlockSpec(shape, lambda *_: (0,) * len(shape))


def _inproj(x, g_mix, wa, wd, tm):
    t = x.shape[0]

    def body(x_ref, g_ref, wa_ref, wd_ref, u_ref, pa_ref, pd_ref):
        u = _rms_fwd(x_ref[...], g_ref[...]).astype(MXU_DTYPE)
        u_ref[...] = u
        pa_ref[...] = _dot(u, wa_ref[...])
        pd_ref[...] = _dot(u, wd_ref[...])

    na, nd = wa.shape[1], wd.shape[1]
    return pl.pallas_call(
        body, name="inproj", grid=(t // tm,),
        in_specs=[pl.BlockSpec((tm, D_MODEL), lambda i: (i, 0)), _full((1, D_MODEL)),
                  _full((D_MODEL, na)), _full((D_MODEL, nd))],
        out_specs=[pl.BlockSpec((tm, D_MODEL), lambda i: (i, 0)), pl.BlockSpec((tm, na), lambda i: (i, 0)),
                   pl.BlockSpec((tm, nd), lambda i: (i, 0))],
        out_shape=[jax.ShapeDtypeStruct((t, D_MODEL), MXU_DTYPE), jax.ShapeDtypeStruct((t, na), F32),
                   jax.ShapeDtypeStruct((t, nd), F32)],
        compiler_params=_params("parallel"),
    )(x, g_mix, wa, wd)


def _oproj(x, ao, dn, wo_a, wo_d, tm):
    t = x.shape[0]

    def body(x_ref, ao_ref, dn_ref, wa_ref, wd_ref, h_ref):
        h_ref[...] = (x_ref[...] + _dot(ao_ref[...].astype(MXU_DTYPE), wa_ref[...])
                      + _dot(dn_ref[...].astype(MXU_DTYPE), wd_ref[...]))

    half = ao.shape[1]
    return pl.pallas_call(
        body, name="oproj", grid=(t // tm,),
        in_specs=[pl.BlockSpec((tm, D_MODEL), lambda i: (i, 0)), pl.BlockSpec((tm, half), lambda i: (i, 0)),
                  pl.BlockSpec((tm, half), lambda i: (i, 0)), _full((half, D_MODEL)), _full((half, D_MODEL))],
        out_specs=pl.BlockSpec((tm, D_MODEL), lambda i: (i, 0)),
        out_shape=jax.ShapeDtypeStruct((t, D_MODEL), F32),
        compiler_params=_params("parallel"),
    )(x, ao, dn, wo_a, wo_d)


def _mlp_fwd(h1, g_mlp, w_up4, w_down, tm):
    t = h1.shape[0]

    def body(h_ref, g_ref, wu_ref, wd_ref, m_ref, a_ref, h2_ref, acc_ref):
        k = pl.program_id(1)

        @pl.when(k == 0)
        def _():
            m_ref[...] = _rms_fwd(h_ref[...], g_ref[...]).astype(MXU_DTYPE)
            acc_ref[...] = jnp.zeros_like(acc_ref)

        a = _dot(m_ref[...], wu_ref[...])
        a_ref[...] = a
        s = jnp.square(jnp.maximum(a, 0.0)).astype(MXU_DTYPE)
        acc_ref[...] += _dot(s, wd_ref[...])

        @pl.when(k == FF_BLOCKS - 1)
        def _():
            h2_ref[...] = h_ref[...] + acc_ref[...]

    return pl.pallas_call(
        body, name="mlp_fwd", grid=(t // tm, FF_BLOCKS),
        in_specs=[pl.BlockSpec((tm, D_MODEL), lambda i, k: (i, 0)), _full((1, D_MODEL)),
                  pl.BlockSpec((None, D_MODEL, FF_BLOCK), lambda i, k: (k, 0, 0)),
                  pl.BlockSpec((FF_BLOCK, D_MODEL), lambda i, k: (k, 0))],
        out_specs=[pl.BlockSpec((tm, D_MODEL), lambda i, k: (i, 0)), pl.BlockSpec((tm, FF_BLOCK), lambda i, k: (i, k)),
                   pl.BlockSpec((tm, D_MODEL), lambda i, k: (i, 0))],
        out_shape=[jax.ShapeDtypeStruct((t, D_MODEL), MXU_DTYPE), jax.ShapeDtypeStruct((t, D_FF), F32),
                   jax.ShapeDtypeStruct((t, D_MODEL), F32)],
        scratch_shapes=[pltpu.VMEM((tm, D_MODEL), F32)],
        compiler_params=_params("parallel", "arbitrary"),
    )(h1, g_mlp, w_up4, w_down)


def _ple_loss(h2, p, tgt, g_ple, g_fin, w_gate, w_proj, tm):
    t = h2.shape[0]

    def body(h_ref, p_ref, t_ref, gp_ref, gf_ref, wg_ref, wp_ref,
             dh_ref, dhb_ref, dgp_ref, dpp_ref, n3_ref, pb_ref, acc_ref):
        @pl.when(pl.program_id(0) == 0)
        def _():
            acc_ref[...] = jnp.zeros_like(acc_ref)

        h = h_ref[...]
        g_ple_v, g_fin_v = gp_ref[...], gf_ref[...]
        n3 = _rms_fwd(h, g_ple_v).astype(MXU_DTYPE)
        n3_ref[...] = n3
        gate = _sigmoid(_dot(n3, wg_ref[...]))
        pb = p_ref[...].astype(MXU_DTYPE)
        pb_ref[...] = pb
        pp = _dot(pb, wp_ref[...])
        h3 = h + gate * pp
        r4 = lax.rsqrt(jnp.mean(h3 * h3, axis=-1, keepdims=True) + EPS)
        xh4 = h3 * r4
        e = xh4 * g_fin_v - t_ref[...]
        loss = 0.5 * jnp.sum(jnp.mean(e * e, axis=-1, keepdims=True), axis=0, keepdims=True)
        dy = e * (1.0 / D_MODEL)
        dg_fin = jnp.sum(dy * xh4, axis=0, keepdims=True)
        dxh = dy * g_fin_v
        dh3 = r4 * (dxh - xh4 * jnp.mean(dxh * xh4, axis=-1, keepdims=True))
        dpp_ref[...] = (dh3 * gate).astype(MXU_DTYPE)
        dgp = (dh3 * pp * gate * (1.0 - gate)).astype(MXU_DTYPE)
        dgp_ref[...] = dgp
        dn3 = _dot(dgp, wg_ref[...], NT)
        dx, dg_ple = _rms_bwd(h, g_ple_v, dn3)
        dh2 = dh3 + dx
        dh_ref[...] = dh2
        dhb_ref[...] = dh2.astype(MXU_DTYPE)
        acc_ref[0:1, :] += dg_fin
        acc_ref[1:2, :] += dg_ple
        acc_ref[2:3, :] += jnp.broadcast_to(loss, (1, D_MODEL))

    row = lambda w: pl.BlockSpec((tm, w), lambda i: (i, 0))
    return pl.pallas_call(
        body, name="ple_loss", grid=(t // tm,),
        in_specs=[row(D_MODEL), row(PLE_DIM), row(D_MODEL), _full((1, D_MODEL)), _full((1, D_MODEL)),
                  _full((D_MODEL, D_MODEL)), _full((PLE_DIM, D_MODEL))],
        out_specs=[row(D_MODEL), row(D_MODEL), row(D_MODEL), row(D_MODEL), row(D_MODEL), row(PLE_DIM),
                   _full((8, D_MODEL))],
        out_shape=[jax.ShapeDtypeStruct((t, D_MODEL), F32), jax.ShapeDtypeStruct((t, D_MODEL), MXU_DTYPE),
                   jax.ShapeDtypeStruct((t, D_MODEL), MXU_DTYPE), jax.ShapeDtypeStruct((t, D_MODEL), MXU_DTYPE),
                   jax.ShapeDtypeStruct((t, D_MODEL), MXU_DTYPE), jax.ShapeDtypeStruct((t, PLE_DIM), MXU_DTYPE),
                   jax.ShapeDtypeStruct((8, D_MODEL), F32)],
        compiler_params=_params("arbitrary"),
    )(h2, p, tgt, g_ple, g_fin, w_gate, w_proj)


def _mlp_bwd(dh2, dh2b, a, h1, g_mlp, w_up4, w_down, tm):
    t = h1.shape[0]

    def body(dh_ref, dhb_ref, a_ref, h_ref, g_ref, wu_ref, wd_ref,
             s_ref, da_ref, dh1_ref, dh1b_ref, acc_ref, dm_ref):
        i, k = pl.program_id(0), pl.program_id(1)

        @pl.when((i == 0) & (k == 0))
        def _():
            acc_ref[...] = jnp.zeros_like(acc_ref)

        @pl.when(k == 0)
        def _():
            dm_ref[...] = jnp.zeros_like(dm_ref)

        ds = _dot(dhb_ref[...], wd_ref[...], NT)
        r = jnp.maximum(a_ref[...], 0.0)
        s_ref[...] = (r * r).astype(MXU_DTYPE)
        da = (ds * (2.0 * r)).astype(MXU_DTYPE)
        da_ref[...] = da
        dm_ref[...] += _dot(da, wu_ref[...], NT)

        @pl.when(k == FF_BLOCKS - 1)
        def _():
            dx, dg = _rms_bwd(h_ref[...], g_ref[...], dm_ref[...])
            dh1 = dh_ref[...] + dx
            dh1_ref[...] = dh1
            dh1b_ref[...] = dh1.astype(MXU_DTYPE)
            acc_ref[0:1, :] += dg

    tok = lambda w: pl.BlockSpec((tm, w), lambda i, k: (i, 0))
    return pl.pallas_call(
        body, name="mlp_bwd", grid=(t // tm, FF_BLOCKS),
        in_specs=[tok(D_MODEL), tok(D_MODEL), pl.BlockSpec((tm, FF_BLOCK), lambda i, k: (i, k)), tok(D_MODEL),
                  _full((1, D_MODEL)), pl.BlockSpec((None, D_MODEL, FF_BLOCK), lambda i, k: (k, 0, 0)),
                  pl.BlockSpec((FF_BLOCK, D_MODEL), lambda i, k: (k, 0))],
        out_specs=[pl.BlockSpec((tm, FF_BLOCK), lambda i, k: (i, k)), pl.BlockSpec((tm, FF_BLOCK), lambda i, k: (i, k)),
                   tok(D_MODEL), tok(D_MODEL), pl.BlockSpec((8, D_MODEL), lambda i, k: (0, 0))],
        out_shape=[jax.ShapeDtypeStruct((t, D_FF), MXU_DTYPE), jax.ShapeDtypeStruct((t, D_FF), MXU_DTYPE),
                   jax.ShapeDtypeStruct((t, D_MODEL), F32), jax.ShapeDtypeStruct((t, D_MODEL), MXU_DTYPE),
                   jax.ShapeDtypeStruct((8, D_MODEL), F32)],
        scratch_shapes=[pltpu.VMEM((tm, D_MODEL), F32)],
        compiler_params=_params("arbitrary", "arbitrary"),
    )(dh2, dh2b, a, h1, g_mlp, w_up4, w_down)


def _oproj_bwd(dh1b, wo_a, wo_d, tm):
    t = dh1b.shape[0]
    half = wo_a.shape[0]

    def body(d_ref, wa_ref, wd_ref, da_ref, dd_ref):
        d = d_ref[...]
        da_ref[...] = _dot(d, wa_ref[...], NT)
        dd_ref[...] = _dot(d, wd_ref[...], NT)

    return pl.pallas_call(
        body, name="oproj_bwd", grid=(t // tm,),
        in_specs=[pl.BlockSpec((tm, D_MODEL), lambda i: (i, 0)), _full((half, D_MODEL)), _full((half, D_MODEL))],
        out_specs=[pl.BlockSpec((tm, half), lambda i: (i, 0)), pl.BlockSpec((tm, half), lambda i: (i, 0))],
        out_shape=[jax.ShapeDtypeStruct((t, half), F32), jax.ShapeDtypeStruct((t, half), F32)],
        compiler_params=_params("parallel"),
    )(dh1b, wo_a, wo_d)


def _inproj_bwd(x, dh1, g_mix, grads, weights, tm):
    t = x.shape[0]
    n = len(grads)

    def body(*refs):
        x_ref, dh_ref, g_ref = refs[:3]
        g_refs, w_refs = refs[3:3 + n], refs[3 + n:3 + 2 * n]
        dx_ref, acc_ref = refs[3 + 2 * n:]

        @pl.when(pl.program_id(0) == 0)
        def _():
            acc_ref[...] = jnp.zeros_like(acc_ref)

        du = _dot(g_refs[0][...], w_refs[0][...], NT)
        for j in range(1, n):
            du += _dot(g_refs[j][...], w_refs[j][...], NT)
        dx, dg = _rms_bwd(x_ref[...], g_ref[...], du)
        dx_ref[...] = dh_ref[...] + dx
        acc_ref[0:1, :] += dg

    tok = lambda w: pl.BlockSpec((tm, w), lambda i: (i, 0))
    return pl.pallas_call(
        body, name="inproj_bwd", grid=(t // tm,),
        in_specs=[tok(D_MODEL), tok(D_MODEL), _full((1, D_MODEL))] + [tok(g.shape[1]) for g in grads]
                 + [_full(w.shape) for w in weights],
        out_specs=[tok(D_MODEL), _full((8, D_MODEL))],
        out_shape=[jax.ShapeDtypeStruct((t, D_MODEL), F32), jax.ShapeDtypeStruct((8, D_MODEL), F32)],
        compiler_params=_params("arbitrary"),
    )(x, dh1, g_mix, *grads, *weights)


def _wgrad(a, b, name, tk, tn, tt, col0=0, ncols=None):
    t, kdim = a.shape
    ncols = b.shape[1] - col0 if ncols is None else ncols
    cb = col0 // tn

    def body(a_ref, b_ref, o_ref):
        @pl.when(pl.program_id(2) == 0)
        def _():
            o_ref[...] = jnp.zeros_like(o_ref)

        o_ref[...] += _dot(a_ref[...], b_ref[...], TN)

    return pl.pallas_call(
        body, name=name, grid=(kdim // tk, ncols // tn, t // tt),
        in_specs=[pl.BlockSpec((tt, tk), lambda i, j, s: (s, i)), pl.BlockSpec((tt, tn), lambda i, j, s: (s, j + cb))],
        out_specs=pl.BlockSpec((tk, tn), lambda i, j, s: (i, j)),
        out_shape=jax.ShapeDtypeStruct((kdim, ncols), F32),
        compiler_params=_params("parallel", "parallel", "arbitrary"),
    )(a, b)


def _rope_tables(t):
    half = ATTN_HEAD_DIM // 2
    inv = 1.0 / (ROPE_THETA ** (jnp.arange(half, dtype=F32) * (2.0 / ATTN_HEAD_DIM)))
    ang = jnp.arange(t, dtype=F32)[:, None] * inv[None, :]
    cos, sin = jnp.cos(ang), jnp.sin(ang)
    cos2 = jnp.concatenate([cos, cos], axis=-1)
    sin2 = jnp.concatenate([-sin, sin], axis=-1)
    return jnp.tile(cos2, (1, 2)), jnp.tile(sin2, (1, 2))


def _swap_halves(tv):
    w = tv.shape[-1]
    lane = lax.broadcasted_iota(jnp.int32, tv.shape, tv.ndim - 1)
    first = (lane % ATTN_HEAD_DIM) < (ATTN_HEAD_DIM // 2)
    return jnp.where(first, pltpu.roll(tv, w - ATTN_HEAD_DIM // 2, tv.ndim - 1),
                     pltpu.roll(tv, ATTN_HEAD_DIM // 2, tv.ndim - 1))


def _rope(tv, cos, sin):
    return tv * cos + _swap_halves(tv) * sin


def _rope_bwd(dv, cos, sin):
    return dv * cos + _swap_halves(dv * sin)


def _attn_probs(qh, kwin, sink, first_block):
    s = _dot(qh, kwin, NT) * ATTN_SCALE
    r = lax.broadcasted_iota(jnp.int32, s.shape, 0)
    c = lax.broadcasted_iota(jnp.int32, s.shape, 1)
    valid = (c > r) & (c <= r + ATTN_BLOCK) & ((c >= ATTN_BLOCK) | jnp.logical_not(first_block))
    s = jnp.where(valid, s, -jnp.inf)
    m = jnp.maximum(jnp.max(s, axis=-1, keepdims=True), sink)
    e = jnp.where(valid, jnp.exp(s - m), 0.0)
    es = jnp.exp(sink - m)
    inv = 1.0 / (jnp.sum(e, axis=-1, keepdims=True) + es)
    return e * inv, es * inv


def _lane_scalar(vec, idx):
    lane = lax.broadcasted_iota(jnp.int32, vec.shape, 1)
    return jnp.sum(jnp.where(lane == idx, vec, 0.0), axis=-1, keepdims=True)


def _attn_specs(nb):
    cur = lambda w, cb: pl.BlockSpec((ATTN_BLOCK, w), lambda i: (jnp.minimum(i, nb - 1), cb))
    prev = lambda w, cb: pl.BlockSpec((ATTN_BLOCK, w), lambda i: (jnp.maximum(jnp.minimum(i, nb - 1) - 1, 0), cb))
    kcol, vcol = ATTN_Q // ATTN_KV, ATTN_Q // ATTN_KV + 1
    return [cur(ATTN_Q, 0), cur(ATTN_KV, kcol), prev(ATTN_KV, kcol), cur(ATTN_KV, vcol), prev(ATTN_KV, vcol),
            cur(ATTN_KV, 0), cur(ATTN_KV, 0), prev(ATTN_KV, 0), prev(ATTN_KV, 0), _full((1, 128))]


def _attn_fwd(pa, cos, sin, sinks_vec):
    t = pa.shape[0]
    nb = t // ATTN_BLOCK

    def body(q_ref, kc_ref, kp_ref, vc_ref, vp_ref, cc_ref, sc_ref, cp_ref, sp_ref, sk_ref, o_ref):
        first = pl.program_id(0) == 0
        cc, sc = cc_ref[...], sc_ref[...]
        q = _rope(q_ref[...], jnp.tile(cc, (1, ATTN_Q // ATTN_KV)), jnp.tile(sc, (1, ATTN_Q // ATTN_KV)))
        kc = _rope(kc_ref[...], cc, sc)
        kp = _rope(kp_ref[...], cp_ref[...], sp_ref[...])
        vc, vp = vc_ref[...], vp_ref[...]
        sk = sk_ref[...]
        for hk in range(ATTN_KV_HEADS):
            ks = slice(hk * ATTN_HEAD_DIM, (hk + 1) * ATTN_HEAD_DIM)
            kwin = jnp.concatenate([kp[:, ks], kc[:, ks]], axis=0)
            vwin = jnp.concatenate([vp[:, ks], vc[:, ks]], axis=0)
            for g in range(ATTN_GROUPS):
                h = hk * ATTN_GROUPS + g
                hs = slice(h * ATTN_HEAD_DIM, (h + 1) * ATTN_HEAD_DIM)
                probs, _ = _attn_probs(q[:, hs], kwin, _lane_scalar(sk, h), first)
                o_ref[:, hs] = _dot(probs, vwin)

    return pl.pallas_call(
        body, name="attn_fwd", grid=(nb,),
        in_specs=_attn_specs(nb),
        out_specs=pl.BlockSpec((ATTN_BLOCK, ATTN_Q), lambda i: (i, 0)),
        out_shape=jax.ShapeDtypeStruct((t, ATTN_Q), F32),
        compiler_params=_params("parallel"),
    )(pa, pa, pa, pa, pa, cos, sin, cos, sin, sinks_vec)


def _attn_bwd(pa, cos, sin, sinks_vec, dao):
    t = pa.shape[0]
    nb = t // ATTN_BLOCK

    def body(q_ref, kc_ref, kp_ref, vc_ref, vp_ref, cc_ref, sc_ref, cp_ref, sp_ref, sk_ref, do_ref,
             dq_ref, dk_ref, dv_ref, acc_ref, dqr_ref, dkw_ref, dvw_ref, ck_ref, cv_ref):
        i = pl.program_id(0)

        @pl.when(i == 0)
        def _():
            acc_ref[...] = jnp.zeros_like(acc_ref)
            ck_ref[...] = jnp.zeros_like(ck_ref)
            cv_ref[...] = jnp.zeros_like(cv_ref)

        @pl.when(i < nb)
        def _():
            first = i == 0
            cc, sc = cc_ref[...], sc_ref[...]
            cq, sq = jnp.tile(cc, (1, ATTN_Q // ATTN_KV)), jnp.tile(sc, (1, ATTN_Q // ATTN_KV))
            q = _rope(q_ref[...], cq, sq)
            kc = _rope(kc_ref[...], cc, sc)
            kp = _rope(kp_ref[...], cp_ref[...], sp_ref[...])
            vc, vp = vc_ref[...], vp_ref[...]
            sk = sk_ref[...]
            do = do_ref[...]
            lane = lax.broadcasted_iota(jnp.int32, (1, 128), 1)
            dsink = jnp.zeros((1, 128), F32)
            for hk in range(ATTN_KV_HEADS):
                ks = slice(hk * ATTN_HEAD_DIM, (hk + 1) * ATTN_HEAD_DIM)
                kwin = jnp.concatenate([kp[:, ks], kc[:, ks]], axis=0)
                vwin = jnp.concatenate([vp[:, ks], vc[:, ks]], axis=0)
                dkw = jnp.zeros((2 * ATTN_BLOCK, ATTN_HEAD_DIM), F32)
                dvw = jnp.zeros((2 * ATTN_BLOCK, ATTN_HEAD_DIM), F32)
                for g in range(ATTN_GROUPS):
                    h = hk * ATTN_GROUPS + g
                    hs = slice(h * ATTN_HEAD_DIM, (h + 1) * ATTN_HEAD_DIM)
                    qh = q[:, hs]
                    probs, psink = _attn_probs(qh, kwin, _lane_scalar(sk, h), first)
                    doh = do[:, hs]
                    dp = _dot(doh, vwin, NT)
                    delta = jnp.sum(probs * dp, axis=-1, keepdims=True)
                    ds = probs * (dp - delta) * ATTN_SCALE
                    dqr_ref[:, hs] = _dot(ds, kwin)
                    dkw += _dot(ds, qh, TN)
                    dvw += _dot(probs, doh, TN)
                    dsink += jnp.where(lane == h, jnp.sum(-psink * delta, axis=0, keepdims=True), 0.0)
                dkw_ref[:, ks] = dkw
                dvw_ref[:, ks] = dvw
            acc_ref[0:1, :] += dsink
            dq_ref[...] = _rope_bwd(dqr_ref[...], cq, sq).astype(dq_ref.dtype)
            dk_ref[...] = (ck_ref[...] + _rope_bwd(dkw_ref[0:ATTN_BLOCK, :], cp_ref[...], sp_ref[...])).astype(dk_ref.dtype)
            dv_ref[...] = (cv_ref[...] + dvw_ref[0:ATTN_BLOCK, :]).astype(dv_ref.dtype)
            ck_ref[...] = _rope_bwd(dkw_ref[ATTN_BLOCK:2 * ATTN_BLOCK, :], cc, sc)
            cv_ref[...] = dvw_ref[ATTN_BLOCK:2 * ATTN_BLOCK, :]

        @pl.when(i == nb)
        def _():
            dk_ref[...] = ck_ref[...].astype(dk_ref.dtype)
            dv_ref[...] = cv_ref[...].astype(dv_ref.dtype)

    prev_out = lambda w: pl.BlockSpec((ATTN_BLOCK, w), lambda i: (jnp.maximum(i - 1, 0), 0))
    return pl.pallas_call(
        body, name="attn_bwd", grid=(nb + 1,),
        in_specs=_attn_specs(nb) + [pl.BlockSpec((ATTN_BLOCK, ATTN_Q), lambda i: (jnp.minimum(i, nb - 1), 0))],
        out_specs=[pl.BlockSpec((ATTN_BLOCK, ATTN_Q), lambda i: (jnp.minimum(i, nb - 1), 0)), prev_out(ATTN_KV),
                   prev_out(ATTN_KV), _full((8, 128))],
        out_shape=[jax.ShapeDtypeStruct((t, ATTN_Q), MXU_DTYPE), jax.ShapeDtypeStruct((t, ATTN_KV), MXU_DTYPE),
                   jax.ShapeDtypeStruct((t, ATTN_KV), MXU_DTYPE), jax.ShapeDtypeStruct((8, 128), F32)],
        scratch_shapes=[pltpu.VMEM((ATTN_BLOCK, ATTN_Q), F32), pltpu.VMEM((2 * ATTN_BLOCK, ATTN_KV), F32),
                        pltpu.VMEM((2 * ATTN_BLOCK, ATTN_KV), F32), pltpu.VMEM((ATTN_BLOCK, ATTN_KV), F32),
                        pltpu.VMEM((ATTN_BLOCK, ATTN_KV), F32)],
        compiler_params=_params("arbitrary"),
    )(pa, pa, pa, pa, pa, cos, sin, cos, sin, sinks_vec, dao)


PAIR = 2 * DN_CHUNK
HALO = 8


def _conv_window(cur_ref, prev_ref, xs_ref, tm):
    prev = jnp.where(pl.program_id(0) > 0, prev_ref[...], 0.0)
    xs_ref[0:HALO, :] = prev
    xs_ref[HALO:HALO + tm, :] = cur_ref[...]


def _conv_taps(xs_ref, cw_ref, tm):
    y = cw_ref[0:1, :] * xs_ref[pl.ds(HALO - DN_CONV + 1, tm), :]
    for j in range(1, DN_CONV):
        y += cw_ref[j:j + 1, :] * xs_ref[pl.ds(HALO - DN_CONV + 1 + j, tm), :]
    return y


def _gate_values(ba, al, dt):
    beta = _sigmoid(ba)
    pre = ba + dt
    g = -jnp.exp(al) * _softplus(pre)
    return beta, g, pre


def _dn_prep_specs(tm, t):
    return [pl.BlockSpec((tm, CONV_CH), lambda i: (i, 0)),
            pl.BlockSpec((HALO, CONV_CH), lambda i: (jnp.maximum(i * (tm // HALO) - 1, 0), 0)),
            pl.BlockSpec((tm, 128), lambda i: (i, 4 * DN_W // 128)),
            _full((DN_CONV, CONV_CH)), _full((1, 128)), _full((1, 128))]


def _dn_prep(pd, conv_w, al_vec, dt_vec, tm):
    t = pd.shape[0]

    def body(cur_ref, prev_ref, ba_ref, cw_ref, al_ref, dt_ref, qn_ref, kn_ref, vc_ref, gc_ref, gr_ref, xs_ref):
        _conv_window(cur_ref, prev_ref, xs_ref, tm)
        y = _conv_taps(xs_ref, cw_ref, tm)
        c = y * _sigmoid(y)
        for h in range(DN_HEADS):
            qs = slice(h * DN_HEAD_DIM, (h + 1) * DN_HEAD_DIM)
            ksl = slice(DN_W + h * DN_HEAD_DIM, DN_W + (h + 1) * DN_HEAD_DIM)
            qh, kh = c[:, qs], c[:, ksl]
            qn_ref[:, qs] = qh * lax.rsqrt(jnp.sum(qh * qh, axis=-1, keepdims=True) + EPS) * DN_SCALE
            kn_ref[:, qs] = kh * lax.rsqrt(jnp.sum(kh * kh, axis=-1, keepdims=True) + EPS)
        vc_ref[...] = c[:, 2 * DN_W:3 * DN_W]
        beta, g, _ = _gate_values(ba_ref[...], al_ref[...], dt_ref[...])
        lane = lax.broadcasted_iota(jnp.int32, beta.shape, 1)
        gb = jnp.where(lane < DN_HEADS, beta, jnp.where(lane < 2 * DN_HEADS, g, 0.0))
        gc_ref[...] = gb
        gr_ref[...] = gb.T[0:8, :]

    tok = lambda w: pl.BlockSpec((tm, w), lambda i: (i, 0))
    return pl.pallas_call(
        body, name="dn_prep", grid=(t // tm,),
        in_specs=_dn_prep_specs(tm, t),
        out_specs=[tok(DN_W), tok(DN_W), tok(DN_W), tok(128), pl.BlockSpec((8, tm), lambda i: (0, i))],
        out_shape=[jax.ShapeDtypeStruct((t, DN_W), F32)] * 3 + [jax.ShapeDtypeStruct((t, 128), F32),
                                                                 jax.ShapeDtypeStruct((8, t), F32)],
        scratch_shapes=[pltpu.VMEM((HALO + tm, CONV_CH), F32)],
        compiler_params=_params("parallel"),
    )(pd, pd, pd, conv_w, al_vec, dt_vec)


def _tri(n, strict=False):
    r = lax.broadcasted_iota(jnp.int32, (n, n), 0)
    c = lax.broadcasted_iota(jnp.int32, (n, n), 1)
    return (r > c) if strict else (r >= c)


def _chunk_gates(gc, gr, c, h):
    rows = slice(c * DN_CHUNK, (c + 1) * DN_CHUNK)
    gcc = gc[rows, :]
    lane = lax.broadcasted_iota(jnp.int32, gcc.shape, 1)
    beta = jnp.sum(jnp.where(lane == h, gcc, 0.0), axis=-1, keepdims=True)
    g = jnp.sum(jnp.where(lane == DN_HEADS + h, gcc, 0.0), axis=-1, keepdims=True)
    low = _tri(DN_CHUNK)
    gam = _dot(low.astype(F32), jnp.broadcast_to(g, (DN_CHUNK, 128)), NN, HI)
    j = lax.broadcasted_iota(jnp.int32, (PAIR, DN_CHUNK), 0) - c * DN_CHUNK
    i = lax.broadcasted_iota(jnp.int32, (PAIR, DN_CHUNK), 1)
    sel = ((j >= 0) & (j <= i)).astype(F32)
    gam_row = _dot(gr, sel, NN, HI)[DN_HEADS + h:DN_HEADS + h + 1, :]
    diff = gam[:, 0:DN_CHUNK] - gam_row
    dm = jnp.where(low, jnp.exp(jnp.where(low, diff, 0.0)), 0.0)
    return beta, gam, dm


def _unit_lower_inverse(lmat):
    n = lmat.shape[0]
    eye = (lax.broadcasted_iota(jnp.int32, (n, n), 0) == lax.broadcasted_iota(jnp.int32, (n, n), 1)).astype(F32)
    acc = eye - lmat
    pw = lmat
    step = 1
    while 2 * step < n:
        pw = _dot(pw, pw, NN, HI)
        acc = acc + _dot(acc, pw, NN, HI)
        step *= 2
    return acc


def _dn_intra(qn, kn, vc, gc, gr):
    t = qn.shape[0]
    npair = t // PAIR

    def body(q_ref, k_ref, v_ref, gc_ref, gr_ref, u_ref, w_ref, qg_ref, kd_ref, a_ref, ti_ref, dl_ref):
        gc_v, gr_v = gc_ref[...], gr_ref[...]
        for c in range(2):
            rows = slice(c * DN_CHUNK, (c + 1) * DN_CHUNK)
            for h in range(DN_HEADS):
                hs = slice(h * DN_HEAD_DIM, (h + 1) * DN_HEAD_DIM)
                q, k, v = q_ref[rows, hs], k_ref[rows, hs], v_ref[rows, hs]
                beta, gam, dm = _chunk_gates(gc_v, gr_v, c, h)
                kb = k * beta
                lmat = jnp.where(_tri(DN_CHUNK, True), _dot(kb, k, NT) * dm, 0.0)
                tinv = _unit_lower_inverse(lmat)
                eg = jnp.exp(gam)
                u_ref[rows, hs] = _dot(tinv, v * beta)
                w_ref[rows, hs] = _dot(tinv, kb * eg)
                a_ref[h, rows, :] = _dot(q, k, NT) * dm
                ti_ref[h, rows, :] = tinv
                qg_ref[rows, hs] = q * eg
                gl = gam[DN_CHUNK - 1:DN_CHUNK, :]
                kd_ref[rows, hs] = k * jnp.exp(gl - gam)
                dl_ref[c, h] = jnp.broadcast_to(jnp.exp(gl), (8, 128))

    tok = lambda w: pl.BlockSpec((PAIR, w), lambda n: (n, 0))
    hm = pl.BlockSpec((DN_HEADS, PAIR, DN_CHUNK), lambda n: (0, n, 0))
    return pl.pallas_call(
        body, name="dn_intra", grid=(npair,),
        in_specs=[tok(DN_W), tok(DN_W), tok(DN_W), tok(128), pl.BlockSpec((8, PAIR), lambda n: (0, n))],
        out_specs=[tok(DN_W)] * 4 + [hm, hm, pl.BlockSpec((2, DN_HEADS, 8, 128), lambda n: (n, 0, 0, 0))],
        out_shape=[jax.ShapeDtypeStruct((t, DN_W), F32)] * 4 + [jax.ShapeDtypeStruct((DN_HEADS, t, DN_CHUNK), F32)] * 2
                  + [jax.ShapeDtypeStruct((2 * npair, DN_HEADS, 8, 128), F32)],
        compiler_params=_params("parallel"),
    )(qn, kn, vc, gc, gr)


def _dn_scan_fwd(u, w, qg, kd, a_qk, dlast, pd, dn_w):
    t = u.shape[0]
    npair = t // PAIR

    def body(u_ref, w_ref, qg_ref, kd_ref, a_ref, dl_ref, z_ref, nw_ref, out_ref, o_ref, vn_ref, sall_ref, s_ref):
        @pl.when(pl.program_id(0) == 0)
        def _():
            s_ref[...] = jnp.zeros_like(s_ref)

        nw = nw_ref[...]
        for c in range(2):
            rows = slice(c * DN_CHUNK, (c + 1) * DN_CHUNK)
            for h in range(DN_HEADS):
                hs = slice(h * DN_HEAD_DIM, (h + 1) * DN_HEAD_DIM)
                st = s_ref[h]
                sall_ref[c, h] = st
                vn = u_ref[rows, hs] - _dot(w_ref[rows, hs], st)
                o = _dot(qg_ref[rows, hs], st) + _dot(a_ref[h, rows, :], vn)
                s_ref[h] = st * dl_ref[c, h][0:1, :] + _dot(kd_ref[rows, hs], vn, TN)
                vn_ref[rows, hs] = vn
                o_ref[rows, hs] = o
                z = z_ref[rows, hs]
                on = o * lax.rsqrt(jnp.mean(o * o, axis=-1, keepdims=True) + EPS) * nw
                out_ref[rows, hs] = on * (z * _sigmoid(z))

    tok = pl.BlockSpec((PAIR, DN_W), lambda n: (n, 0))
    hm = pl.BlockSpec((DN_HEADS, PAIR, DN_CHUNK), lambda n: (0, n, 0))
    return pl.pallas_call(
        body, name="dn_scan_fwd", grid=(npair,),
        in_specs=[tok, tok, tok, tok, hm, pl.BlockSpec((2, DN_HEADS, 8, 128), lambda n: (n, 0, 0, 0)),
                  pl.BlockSpec((PAIR, DN_W), lambda n: (n, 3)), _full((1, 128))],
        out_specs=[tok, tok, tok, pl.BlockSpec((2, DN_HEADS, DN_HEAD_DIM, DN_HEAD_DIM), lambda n: (n, 0, 0, 0))],
        out_shape=[jax.ShapeDtypeStruct((t, DN_W), F32)] * 3
                  + [jax.ShapeDtypeStruct((2 * npair, DN_HEADS, DN_HEAD_DIM, DN_HEAD_DIM), F32)],
        scratch_shapes=[pltpu.VMEM((DN_HEADS, DN_HEAD_DIM, DN_HEAD_DIM), F32)],
        compiler_params=_params("arbitrary"),
    )(u, w, qg, kd, a_qk, dlast, pd, dn_w)


def _dn_scan_bwd(dout, o, vnew, sall, w, qg, kd, a_qk, dlast, pd, dn_w):
    t = o.shape[0]
    npair = t // PAIR
    rev = lambda n: npair - 1 - n

    def body(do_ref, o_ref, vn_ref, sall_ref, w_ref, qg_ref, kd_ref, a_ref, dl_ref, z_ref, nw_ref,
             dz_ref, du_ref, dw_ref, dqg_ref, dkd_ref, da_ref, ddl_ref, acc_ref, ds_ref):
        @pl.when(pl.program_id(0) == 0)
        def _():
            ds_ref[...] = jnp.zeros_like(ds_ref)
            acc_ref[...] = jnp.zeros_like(acc_ref)

        nw = nw_ref[...]
        dnw = jnp.zeros((1, 128), F32)
        for c in (1, 0):
            rows = slice(c * DN_CHUNK, (c + 1) * DN_CHUNK)
            for h in range(DN_HEADS):
                hs = slice(h * DN_HEAD_DIM, (h + 1) * DN_HEAD_DIM)
                o, z, dout = o_ref[rows, hs], z_ref[rows, hs], do_ref[rows, hs]
                r = lax.rsqrt(jnp.mean(o * o, axis=-1, keepdims=True) + EPS)
                oh = o * r
                sz = _sigmoid(z)
                dz_ref[rows, hs] = dout * (oh * nw) * (sz + z * sz * (1.0 - sz))
                don = dout * (z * sz)
                dnw += jnp.sum(don * oh, axis=0, keepdims=True)
                doh = don * nw
                do = r * (doh - oh * jnp.mean(doh * oh, axis=-1, keepdims=True))

                st, dsp = sall_ref[c, h], ds_ref[h]
                vn, a, kdv, wv, qgv = vn_ref[rows, hs], a_ref[h, rows, :], kd_ref[rows, hs], w_ref[rows, hs], qg_ref[rows, hs]
                da_ref[h, rows, :] = _dot(do, vn, NT)
                dvn = _dot(a, do, TN) + _dot(kdv, dsp)
                du_ref[rows, hs] = dvn
                dqg_ref[rows, hs] = _dot(do, st, NT)
                dkd_ref[rows, hs] = _dot(vn, dsp, NT)
                dw_ref[rows, hs] = -_dot(dvn, st, NT)
                ddl = jnp.sum(jnp.sum(dsp * st, axis=1, keepdims=True), axis=0, keepdims=True)
                ddl_ref[c, h] = jnp.broadcast_to(ddl, (8, 128))
                ds_ref[h] = dsp * dl_ref[c, h][0:1, :] + _dot(qgv, do, TN) - _dot(wv, dvn, TN)
        acc_ref[0:1, :] += dnw

    tok = pl.BlockSpec((PAIR, DN_W), lambda n: (rev(n), 0))
    hm = pl.BlockSpec((DN_HEADS, PAIR, DN_CHUNK), lambda n: (0, rev(n), 0))
    sc = pl.BlockSpec((2, DN_HEADS, 8, 128), lambda n: (rev(n), 0, 0, 0))
    return pl.pallas_call(
        body, name="dn_scan_bwd", grid=(npair,),
        in_specs=[tok, tok, tok, pl.BlockSpec((2, DN_HEADS, DN_HEAD_DIM, DN_HEAD_DIM), lambda n: (rev(n), 0, 0, 0)),
                  tok, tok, tok, hm, sc, pl.BlockSpec((PAIR, DN_W), lambda n: (rev(n), 3)), _full((1, 128))],
        out_specs=[tok] * 5 + [hm, sc, _full((8, 128))],
        out_shape=[jax.ShapeDtypeStruct((t, DN_W), F32)] * 5 + [jax.ShapeDtypeStruct((DN_HEADS, t, DN_CHUNK), F32),
                   jax.ShapeDtypeStruct((2 * npair, DN_HEADS, 8, 128), F32), jax.ShapeDtypeStruct((8, 128), F32)],
        scratch_shapes=[pltpu.VMEM((DN_HEADS, DN_HEAD_DIM, DN_HEAD_DIM), F32)],
        compiler_params=_params("arbitrary"),
    )(dout, o, vnew, sall, w, qg, kd, a_qk, dlast, pd, dn_w)


def _dn_intra_bwd(qn, kn, vc, gc, gr, tinv, a_qk, du, dw, dqg, dkd, da_qk, ddlast, dlast):
    t = qn.shape[0]
    npair = t // PAIR

    def body(q_ref, k_ref, v_ref, gc_ref, gr_ref, ti_ref, a_ref, du_ref, dw_ref, dqg_ref, dkd_ref, da_ref, ddl_ref, dl_ref,
             dq_ref, dk_ref, dv_ref, dg_ref):
        gc_v, gr_v = gc_ref[...], gr_ref[...]
        low, strict = _tri(DN_CHUNK), _tri(DN_CHUNK, True)
        ones = jnp.ones((DN_CHUNK, 128), F32)
        lane = lax.broadcasted_iota(jnp.int32, (DN_CHUNK, 128), 1)
        rowi = lax.broadcasted_iota(jnp.int32, (DN_CHUNK, 128), 0)
        rsum = lambda v: jnp.sum(v, axis=-1, keepdims=True)
        for c in range(2):
            rows = slice(c * DN_CHUNK, (c + 1) * DN_CHUNK)
            dgc = jnp.zeros((DN_CHUNK, 128), F32)
            for h in range(DN_HEADS):
                hs = slice(h * DN_HEAD_DIM, (h + 1) * DN_HEAD_DIM)
                q, k, v = q_ref[rows, hs], k_ref[rows, hs], v_ref[rows, hs]
                beta, gam, dm = _chunk_gates(gc_v, gr_v, c, h)
                tinv, a = ti_ref[h, rows, :], a_ref[h, rows, :]
                du, dw, dqg, dkd = du_ref[rows, hs], dw_ref[rows, hs], dqg_ref[rows, hs], dkd_ref[rows, hs]
                kb = k * beta
                eg = jnp.exp(gam)
                gl = gam[DN_CHUNK - 1:DN_CHUNK, :]
                ekd = jnp.exp(gl - gam)
                kbg, vb = kb * eg, v * beta
                lmat = jnp.where(strict, _dot(kb, k, NT) * dm, 0.0)

                dti = _dot(du, vb, NT) + _dot(dw, kbg, NT)
                dvb = _dot(tinv, du, TN)
                dkbg = _dot(tinv, dw, TN)
                dl = jnp.where(strict, -_dot(_dot(tinv, dti, TN, HI), tinv, NT, HI), 0.0)
                dmm = dl * dm
                dam = jnp.where(low, da_ref[h, rows, :], 0.0)
                dn = dam * dm
                e = dl * lmat + dam * a
                dgam = _dot(e, ones, NN, HI) - _dot(e, ones, TN, HI)
                dkb = _dot(dmm, k) + dkbg * eg
                dk_ref[rows, hs] = _dot(dmm, kb, TN) + _dot(dn, q, TN) + dkd * ekd + dkb * beta
                dq_ref[rows, hs] = _dot(dn, k) + dqg * eg
                dv_ref[rows, hs] = dvb * beta
                t_kd = rsum(dkd * (k * ekd))
                dgam = dgam + rsum(dqg * (q * eg)) + rsum(dkbg * kbg) - t_kd
                dgl = jnp.sum(t_kd, axis=0, keepdims=True) + ddl_ref[c, h][0:1, :] * dl_ref[c, h][0:1, :]
                dgam = dgam + jnp.where(rowi == DN_CHUNK - 1, dgl, 0.0)
                dbeta = rsum(dkb * k) + rsum(dvb * v)
                dg = _dot(low.astype(F32), dgam, TN, HI)
                dgc += jnp.where(lane == h, dbeta, 0.0) + jnp.where(lane == DN_HEADS + h, dg, 0.0)
            dg_ref[rows, :] = dgc

    tok = lambda w: pl.BlockSpec((PAIR, w), lambda n: (n, 0))
    hm = pl.BlockSpec((DN_HEADS, PAIR, DN_CHUNK), lambda n: (0, n, 0))
    sc = pl.BlockSpec((2, DN_HEADS, 8, 128), lambda n: (n, 0, 0, 0))
    return pl.pallas_call(
        body, name="dn_intra_bwd", grid=(npair,),
        in_specs=[tok(DN_W), tok(DN_W), tok(DN_W), tok(128), pl.BlockSpec((8, PAIR), lambda n: (0, n)), hm, hm,
                  tok(DN_W), tok(DN_W), tok(DN_W), tok(DN_W), hm, sc, sc],
        out_specs=[tok(DN_W), tok(DN_W), tok(DN_W), tok(128)],
        out_shape=[jax.ShapeDtypeStruct((t, DN_W), F32)] * 3 + [jax.ShapeDtypeStruct((t, 128), F32)],
        compiler_params=_params("parallel"),
    )(qn, kn, vc, gc, gr, tinv, a_qk, du, dw, dqg, dkd, da_qk, ddlast, dlast)


def _dn_prep_bwd(pd, conv_w, al_vec, dt_vec, dqn, dkn, dvc, dgc, tm):
    t = pd.shape[0]

    def body(cur_ref, prev_ref, ba_ref, cw_ref, al_ref, dt_ref, dq_ref, dk_ref, dv_ref, dg_ref,
             dy_ref, dba_ref, accw_ref, accg_ref, xs_ref, dc_ref):
        @pl.when(pl.program_id(0) == 0)
        def _():
            accw_ref[...] = jnp.zeros_like(accw_ref)
            accg_ref[...] = jnp.zeros_like(accg_ref)

        _conv_window(cur_ref, prev_ref, xs_ref, tm)
        y = _conv_taps(xs_ref, cw_ref, tm)
        sg = _sigmoid(y)
        c = y * sg
        for h in range(DN_HEADS):
            qs = slice(h * DN_HEAD_DIM, (h + 1) * DN_HEAD_DIM)
            ksl = slice(DN_W + h * DN_HEAD_DIM, DN_W + (h + 1) * DN_HEAD_DIM)
            for src, sl, scale in ((dq_ref, qs, DN_SCALE), (dk_ref, ksl, 1.0)):
                xh = c[:, sl]
                r = lax.rsqrt(jnp.sum(xh * xh, axis=-1, keepdims=True) + EPS)
                unit = xh * r
                dn = src[:, qs] * scale
                dc_ref[:, sl] = r * (dn - unit * jnp.sum(dn * unit, axis=-1, keepdims=True))
        dc_ref[:, 2 * DN_W:3 * DN_W] = dv_ref[...]
        dy = dc_ref[...] * (sg + y * sg * (1.0 - sg))
        dy_ref[...] = dy
        for j in range(DN_CONV):
            accw_ref[j:j + 1, :] += jnp.sum(dy * xs_ref[pl.ds(HALO - DN_CONV + 1 + j, tm), :], axis=0, keepdims=True)

        beta, g, pre = _gate_values(ba_ref[...], al_ref[...], dt_ref[...])
        dgb = dg_ref[...]
        lane = lax.broadcasted_iota(jnp.int32, dgb.shape, 1)
        is_b, is_a = lane < DN_HEADS, (lane >= DN_HEADS) & (lane < 2 * DN_HEADS)
        dpre = dgb * (-jnp.exp(al_ref[...])) * _sigmoid(pre)
        dba_ref[...] = jnp.where(is_b, dgb * beta * (1.0 - beta), jnp.where(is_a, dpre, 0.0))
        accg_ref[0:1, :] += jnp.sum(jnp.where(is_a, dgb * g, 0.0), axis=0, keepdims=True)
        accg_ref[1:2, :] += jnp.sum(jnp.where(is_a, dpre, 0.0), axis=0, keepdims=True)

    tok = lambda w: pl.BlockSpec((tm, w), lambda i: (i, 0))
    return pl.pallas_call(
        body, name="dn_prep_bwd", grid=(t // tm,),
        in_specs=_dn_prep_specs(tm, t) + [tok(DN_W), tok(DN_W), tok(DN_W), tok(128)],
        out_specs=[tok(CONV_CH), tok(128), _full((8, CONV_CH)), _full((8, 128))],
        out_shape=[jax.ShapeDtypeStruct((t, CONV_CH), F32), jax.ShapeDtypeStruct((t, 128), F32),
                   jax.ShapeDtypeStruct((8, CONV_CH), F32), jax.ShapeDtypeStruct((8, 128), F32)],
        scratch_shapes=[pltpu.VMEM((HALO + tm, CONV_CH), F32), pltpu.VMEM((tm, CONV_CH), F32)],
        compiler_params=_params("arbitrary"),
    )(pd, pd, pd, conv_w, al_vec, dt_vec, dqn, dkn, dvc, dgc)


def _dn_conv_bwd(dy, dz, dba, conv_w, tm):
    t = dy.shape[0]
    nt = t // tm

    def body(cur_ref, nxt_ref, dz_ref, dba_ref, cw_ref, o_ref, ds_ref):
        nxt = jnp.where(pl.program_id(0) < nt - 1, nxt_ref[...], 0.0)
        ds_ref[0:tm, :] = cur_ref[...]
        ds_ref[tm:tm + HALO, :] = nxt
        dx = cw_ref[0:1, :] * ds_ref[pl.ds(DN_CONV - 1, tm), :]
        for j in range(1, DN_CONV):
            dx += cw_ref[j:j + 1, :] * ds_ref[pl.ds(DN_CONV - 1 - j, tm), :]
        o_ref[:, 0:CONV_CH] = dx.astype(o_ref.dtype)
        o_ref[:, CONV_CH:CONV_CH + DN_W] = dz_ref[...].astype(o_ref.dtype)
        o_ref[:, CONV_CH + DN_W:DN_COLS] = dba_ref[...].astype(o_ref.dtype)

    tok = lambda w: pl.BlockSpec((tm, w), lambda i: (i, 0))
    return pl.pallas_call(
        body, name="dn_conv_bwd", grid=(nt,),
        in_specs=[tok(CONV_CH),
                  pl.BlockSpec((HALO, CONV_CH), lambda i: (jnp.minimum((i + 1) * (tm // HALO), t // HALO - 1), 0)),
                  tok(DN_W), tok(128), _full((DN_CONV, CONV_CH))],
        out_specs=tok(DN_COLS),
        out_shape=jax.ShapeDtypeStruct((t, DN_COLS), MXU_DTYPE),
        scratch_shapes=[pltpu.VMEM((tm + HALO, CONV_CH), F32)],
        compiler_params=_params("parallel"),
    )(dy, dy, dz, dba, conv_w)


def _pad_lanes(v, offset=0):
    return jnp.zeros((1, 128), F32).at[0, offset:offset + v.shape[0]].set(v.astype(F32))


def _local_step(x, p, tgt, sm, w):
    t = x.shape[0]
    tm = min(512, t // 2)
    tm_s = min(256, t // 2)

    w_in = w["w_in"]
    wa = w_in[:, :ATTN_Q + 2 * ATTN_KV]
    wd = jnp.pad(w_in[:, ATTN_Q + 2 * ATTN_KV:], ((0, 0), (0, DN_COLS - (D_IN - ATTN_Q - 2 * ATTN_KV))))
    wo_a, wo_d = w["w_o"][:ATTN_Q], w["w_o"][ATTN_Q:]
    w_proj = jnp.transpose(w["w_proj4"], (1, 0, 2)).reshape(PLE_DIM, D_MODEL)
    conv_w = w["conv_w"]
    al_vec, dt_vec = _pad_lanes(sm["a_log"], DN_HEADS), _pad_lanes(sm["dt_bias"], DN_HEADS)
    sinks_vec = _pad_lanes(sm["sinks"])
    dn_w = sm["dn_norm"].reshape(1, 128)
    row = lambda v: v.reshape(1, D_MODEL)
    cos, sin = _rope_tables(t)

    u, pa, pd = _inproj(x, row(sm["norm_mix"]), wa, wd, tm_s)
    ao = _attn_fwd(pa, cos, sin, sinks_vec)
    qn, kn, vc, gc, gr = _dn_prep(pd, conv_w, al_vec, dt_vec, tm_s)
    uu, ww, qg, kd, a_qk, tinv, dlast = _dn_intra(qn, kn, vc, gc, gr)
    dn_out, o, vnew, sall = _dn_scan_fwd(uu, ww, qg, kd, a_qk, dlast, pd, dn_w)
    h1 = _oproj(x, ao, dn_out, wo_a, wo_d, tm)
    m, a, h2 = _mlp_fwd(h1, row(sm["norm_mlp"]), w["w_up4"], w["w_down"], tm)
    dh2, dh2b, dgp, dpp, n3, pb, acc_ple = _ple_loss(h2, p, tgt, row(sm["norm_ple"]), row(sm["norm_final"]),
                                                     w["w_gate"], w_proj, tm_s)
    s, da, dh1, dh1b, acc_mlp = _mlp_bwd(dh2, dh2b, a, h1, row(sm["norm_mlp"]), w["w_up4"], w["w_down"], tm)
    dao, ddn = _oproj_bwd(dh1b, wo_a, wo_d, tm)
    dz, du, dw, dqg, dkd, da_qk, ddlast, acc_dn = _dn_scan_bwd(ddn, o, vnew, sall, ww, qg, kd, a_qk, dlast, pd, dn_w)
    dqn, dkn, dvc, dgc = _dn_intra_bwd(qn, kn, vc, gc, gr, tinv, a_qk, du, dw, dqg, dkd, da_qk, ddlast, dlast)
    dy, dba, acc_conv, acc_gate = _dn_prep_bwd(pd, conv_w, al_vec, dt_vec, dqn, dkn, dvc, dgc, tm_s)
    d_dn = _dn_conv_bwd(dy, dz, dba, conv_w, tm_s)
    dq, dk, dv, acc_attn = _attn_bwd(pa, cos, sin, sinks_vec, dao)
    wq, wk, wv = wa[:, :ATTN_Q], wa[:, ATTN_Q:ATTN_Q + ATTN_KV], wa[:, ATTN_Q + ATTN_KV:]
    dx, acc_mix = _inproj_bwd(x, dh1, row(sm["norm_mix"]), [dq, dk, dv, d_dn], [wq, wk, wv, wd], tm_s)

    aob, dnb = ao.astype(MXU_DTYPE), dn_out.astype(MXU_DTYPE)
    g_w_in = jnp.concatenate([
        _wgrad(u, dq, "wgrad_q", D_MODEL, ATTN_Q, tm), _wgrad(u, dk, "wgrad_k", D_MODEL, ATTN_KV, tm),
        _wgrad(u, dv, "wgrad_v", D_MODEL, ATTN_KV, tm),
        _wgrad(u, d_dn, "wgrad_dn", D_MODEL, DN_COLS, tm)[:, :D_IN - ATTN_Q - 2 * ATTN_KV]], axis=1)
    g_w_o = jnp.concatenate([_wgrad(aob, dh1b, "wgrad_oa", ATTN_Q, D_MODEL, tm),
                             _wgrad(dnb, dh1b, "wgrad_od", DN_W, D_MODEL, tm)], axis=0)
    g_w_up4 = jnp.stack([_wgrad(m, da, "wgrad_up%d" % k, D_MODEL, FF_BLOCK, tm, col0=k * FF_BLOCK, ncols=FF_BLOCK)
                         for k in range(FF_BLOCKS)])
    g_w_down = _wgrad(s, dh2b, "wgrad_down", FF_BLOCK, D_MODEL, tm)
    g_w_gate = _wgrad(n3, dgp, "wgrad_gate", D_MODEL, D_MODEL, tm)
    g_w_proj = _wgrad(pb, dpp, "wgrad_proj", PLE_DIM, D_MODEL, tm)
    grads = dict(w_in=g_w_in, w_o=g_w_o, w_up4=g_w_up4, w_down=g_w_down, w_gate=g_w_gate, w_proj=g_w_proj)
    sums = dict(loss=acc_ple[2, 0], norm_final=acc_ple[0], norm_ple=acc_ple[1], norm_mlp=acc_mlp[0], norm_mix=acc_mix[0],
                dn_norm=acc_dn[0], sinks=acc_attn[0, :ATTN_HEADS], a_log=acc_gate[0, DN_HEADS:2 * DN_HEADS],
                dt_bias=acc_gate[1, DN_HEADS:2 * DN_HEADS], conv_w=acc_conv[:DN_CONV])
    return sums, dx, grads


MESH = pl.DeviceIdType.MESH
ANY = pl.BlockSpec(memory_space=pl.ANY)
N_CHIPS = 4
N_DEV = 8


def _place():
    x, y, c = lax.axis_index("x"), lax.axis_index("y"), lax.axis_index("c")
    chips = [(1 - x, y), (x, 1 - y), (1 - x, 1 - y)]
    return x, y, c, chips


def _gather_weights(shards, conv_s):
    n = len(shards)
    per = 7

    def body(*refs):
        in_refs, conv_ref = refs[:n], refs[n]
        out_refs, conv_out = refs[n + 1:2 * n + 1], refs[2 * n + 1]
        send_sems, recv_sems = refs[2 * n + 2:]
        x, y, c, chips = _place()
        sibling = (x, y, 1 - c)

        def blk(a, px, py, pc):
            hr = in_refs[a].shape[0] // 2
            return out_refs[a].at[2 * px + py, pl.ds(pc * hr, hr), :]

        def mine(a):
            hr = in_refs[a].shape[0] // 2
            return in_refs[a].at[pl.ds(c * hr, hr), :]

        def rcopy(a, k, block, to, src=None):
            return pltpu.make_async_remote_copy(
                src_ref=blk(a, *block) if src is None else src, dst_ref=blk(a, *block),
                send_sem=send_sems.at[per * a + k], recv_sem=recv_sems.at[per * a + k],
                device_id=to, device_id_type=MESH)

        def whole(a, to):
            return pltpu.make_async_remote_copy(
                src_ref=in_refs[a], dst_ref=out_refs[a].at[2 * x + y],
                send_sem=send_sems.at[per * a], recv_sem=recv_sems.at[per * a], device_id=to, device_id_type=MESH)

        def ccopy(j, to):
            return pltpu.make_async_remote_copy(
                src_ref=conv_ref, dst_ref=conv_out.at[2 * x + y],
                send_sem=send_sems.at[per * n + j], recv_sem=recv_sems.at[per * n + j],
                device_id=to, device_id_type=MESH)

        started = []
        for a in range(n):
            first = [whole(a, sibling)]
            first += [rcopy(a, 1 + j, (x, y, c), (*chip, c), src=mine(a)) for j, chip in enumerate(chips)]
            for cp in first:
                cp.start()
            started += first
        conv_sends = [ccopy(j, (*chip, c)) for j, chip in enumerate(chips)] + [ccopy(3, sibling)]
        for cp in conv_sends:
            cp.start()
        started += conv_sends
        for a in range(n):
            for j, chip in enumerate(chips):
                rcopy(a, 1 + j, (*chip, c), (x, y, c)).wait_recv()
                fwd = rcopy(a, 4 + j, (*chip, c), sibling)
                fwd.start()
                started.append(fwd)
        for a in range(n):
            whole(a, sibling).wait_recv()
            for j, chip in enumerate(chips):
                rcopy(a, 4 + j, (*chip, 1 - c), (x, y, c)).wait_recv()
        for j, chip in enumerate(chips + [(x, y)]):
            pltpu.make_async_remote_copy(
                src_ref=conv_ref, dst_ref=conv_out.at[2 * chip[0] + chip[1]],
                send_sem=send_sems.at[per * n + j], recv_sem=recv_sems.at[per * n + j],
                device_id=sibling, device_id_type=MESH).wait_recv()
        for cp in started:
            cp.wait_send()

    nsem = per * n + 4
    out_shape = [jax.ShapeDtypeStruct((N_CHIPS,) + s.shape, s.dtype) for s in shards]
    out_shape.append(jax.ShapeDtypeStruct((N_CHIPS,) + conv_s.shape, conv_s.dtype))
    return pl.pallas_call(
        body, name="gather_weights", in_specs=[ANY] * (n + 1), out_specs=[ANY] * (n + 1), out_shape=out_shape,
        scratch_shapes=[pltpu.SemaphoreType.DMA((nsem,)), pltpu.SemaphoreType.DMA((nsem,))],
    )(*shards, conv_s)


def _exchange_halves(grads):
    n = len(grads)

    def body(*refs):
        g_refs, got_refs = refs[:n], refs[n:2 * n]
        send_sems, recv_sems = refs[2 * n:]
        x, y, c, _ = _place()
        remote = []
        for a in range(n):
            hr = g_refs[a].shape[1] // 2
            remote.append(pltpu.make_async_remote_copy(
                src_ref=g_refs[a].at[:, pl.ds((1 - c) * hr, hr), :], dst_ref=got_refs[a],
                send_sem=send_sems.at[a], recv_sem=recv_sems.at[a], device_id=(x, y, 1 - c), device_id_type=MESH))
        for cp in remote:
            cp.start()
        for cp in remote:
            cp.wait_recv()
        for cp in remote:
            cp.wait_send()

    half = [jax.ShapeDtypeStruct((g.shape[0], g.shape[1] // 2, g.shape[2]), g.dtype) for g in grads]
    return pl.pallas_call(
        body, name="exchange_halves", in_specs=[ANY] * n, out_specs=[ANY] * n, out_shape=half,
        scratch_shapes=[pltpu.SemaphoreType.DMA((n,)), pltpu.SemaphoreType.DMA((n,))],
    )(*grads)


def _scatter_to_chips(sums16):
    n = len(sums16)

    def body(*refs):
        s16, got_refs = refs[:n], refs[n:2 * n]
        send_sems, recv_sems = refs[2 * n:]
        x, y, c, chips = _place()
        remote = []
        for a in range(n):
            for j, chip in enumerate(chips):
                remote.append(pltpu.make_async_remote_copy(
                    src_ref=s16[a].at[2 * chip[0] + chip[1]], dst_ref=got_refs[a].at[j],
                    send_sem=send_sems.at[3 * a + j], recv_sem=recv_sems.at[3 * a + j],
                    device_id=(*chip, c), device_id_type=MESH))
        for cp in remote:
            cp.start()
        for cp in remote:
            cp.wait_recv()
        for cp in remote:
            cp.wait_send()

    got = [jax.ShapeDtypeStruct((3,) + s.shape[1:], BF16) for s in sums16]
    return pl.pallas_call(
        body, name="scatter_to_chips", in_specs=[ANY] * n, out_specs=[ANY] * n, out_shape=got,
        scratch_shapes=[pltpu.SemaphoreType.DMA((3 * n,)), pltpu.SemaphoreType.DMA((3 * n,))],
    )(*sums16)


def _share_halves(bufs):
    n = len(bufs)

    def body(*refs):
        out_refs = refs[n:2 * n]
        send_sems, recv_sems = refs[2 * n:]
        x, y, c, _ = _place()
        remote = [pltpu.make_async_remote_copy(
            src_ref=out_refs[a].at[c], dst_ref=out_refs[a].at[c], send_sem=send_sems.at[a], recv_sem=recv_sems.at[a],
            device_id=(x, y, 1 - c), device_id_type=MESH) for a in range(n)]
        for cp in remote:
            cp.start()
        for a in range(n):
            pltpu.make_async_remote_copy(
                src_ref=out_refs[a].at[c], dst_ref=out_refs[a].at[1 - c], send_sem=send_sems.at[a],
                recv_sem=recv_sems.at[a], device_id=(x, y, 1 - c), device_id_type=MESH).wait_recv()
        for cp in remote:
            cp.wait_send()

    return pl.pallas_call(
        body, name="share_halves", in_specs=[ANY] * n, out_specs=[ANY] * n,
        out_shape=[jax.ShapeDtypeStruct(b.shape, b.dtype) for b in bufs],
        input_output_aliases={a: a for a in range(n)},
        scratch_shapes=[pltpu.SemaphoreType.DMA((n,)), pltpu.SemaphoreType.DMA((n,))],
    )(*bufs)


SMALL_ROWS, SMALL_COLS = 16, CONV_CH


def _allreduce_small(block):
    m_per, ncol = block.shape

    def body(x_ref, sum_ref, all_ref, send_sems, recv_sems, local_sem):
        x, y, c, chips = _place()
        me, sibling = (x, y, c), (x, y, 1 - c)

        def rows(px, py, pc):
            return all_ref.at[pl.ds((4 * px + 2 * py + pc) * m_per, m_per), :]

        def copy(k, block_of, to, src=None):
            return pltpu.make_async_remote_copy(
                src_ref=rows(*block_of) if src is None else src, dst_ref=rows(*block_of),
                send_sem=send_sems.at[k], recv_sem=recv_sems.at[k], device_id=to, device_id_type=MESH)

        mine = pltpu.make_async_copy(x_ref, rows(*me), local_sem)
        mine.start()
        first = [copy(0, me, sibling, src=x_ref)]
        first += [copy(1 + j, me, (*chip, c), src=x_ref) for j, chip in enumerate(chips)]
        for cp in first:
            cp.start()
        passed = [copy(4 + j, (*chip, c), sibling) for j, chip in enumerate(chips)]
        for j, chip in enumerate(chips):
            copy(1 + j, (*chip, c), me).wait_recv()
            passed[j].start()
        copy(0, sibling, me).wait_recv()
        for j, chip in enumerate(chips):
            copy(4 + j, (*chip, 1 - c), me).wait_recv()
        for cp in first + passed:
            cp.wait_send()
        mine.wait()
        total = all_ref[0:m_per, :]
        for d in range(1, N_DEV):
            total = total + all_ref[d * m_per:(d + 1) * m_per, :]
        sum_ref[...] = total

    vm = pl.BlockSpec(memory_space=pltpu.VMEM)
    return pl.pallas_call(
        body, name="allreduce_small", in_specs=[vm], out_specs=vm,
        out_shape=jax.ShapeDtypeStruct((m_per, ncol), F32),
        scratch_shapes=[pltpu.VMEM((N_DEV * m_per, ncol), F32), pltpu.SemaphoreType.DMA((7,)),
                        pltpu.SemaphoreType.DMA((7,)), pltpu.SemaphoreType.DMA],
    )(block)


def _row_tile(rows, cols):
    tile = rows
    while tile * cols * 4 > (1 << 20) and tile % 16 == 0:
        tile //= 2
    return tile


def _elementwise(fn, name, ins, out_dtypes):
    rows, cols = ins[0].shape
    tile = _row_tile(rows, cols)

    def body(*refs):
        outs = fn(*[r[...] for r in refs[:len(ins)]])
        for o_ref, o in zip(refs[len(ins):], outs):
            o_ref[...] = o.astype(o_ref.dtype)

    spec = pl.BlockSpec((tile, cols), lambda i: (i, 0))
    return pl.pallas_call(
        body, name=name, grid=(rows // tile,), in_specs=[spec] * len(ins), out_specs=[spec] * len(out_dtypes),
        out_shape=[jax.ShapeDtypeStruct((rows, cols), d) for d in out_dtypes],
        compiler_params=_params("parallel"),
    )(*ins)


def _adamw_tile(w, g, m, v):
    m = ADAM_B1 * m + (1.0 - ADAM_B1) * g
    v = ADAM_B2 * v + (1.0 - ADAM_B2) * jnp.square(g)
    m_hat = m / (1.0 - ADAM_B1 ** ADAM_STEP)
    v_hat = v / (1.0 - ADAM_B2 ** ADAM_STEP)
    delta = -ADAM_LR * (m_hat / (jnp.sqrt(v_hat) + ADAM_EPS) + ADAM_WD * w)
    return delta, m, v


def _adamw(name, w, g, m, v):
    return _elementwise(_adamw_tile, name, [w, g, m, v], [F32, F32, F32])


def _chip_sum(name, g4, got, place):
    nchip, hr, cols = got.shape
    tile = _row_tile(hr, cols)
    nblk = hr // tile

    def body(pl_ref, g_ref, o_ref, s32_ref, s16_ref):
        s = g_ref[...] + o_ref[...]
        s32_ref[...] = s
        s16_ref[...] = s.astype(BF16)

    spec = pl.BlockSpec((None, tile, cols), lambda k, i, pr: (k, i, 0))
    return pl.pallas_call(
        body, name=name,
        grid_spec=pltpu.PrefetchScalarGridSpec(
            num_scalar_prefetch=1, grid=(nchip, nblk),
            in_specs=[pl.BlockSpec((None, tile, cols), lambda k, i, pr: (k, pr[1] * nblk + i, 0)), spec],
            out_specs=[spec, spec]),
        out_shape=[jax.ShapeDtypeStruct(got.shape, F32), jax.ShapeDtypeStruct(got.shape, BF16)],
        compiler_params=_params("parallel", "parallel"),
    )(place, g4, got)


def _mesh_sum(name, s32, got, place):
    _, hr, cols = s32.shape
    tile = _row_tile(hr, cols)

    def body(pl_ref, own_ref, g0_ref, g1_ref, g2_ref, o_ref):
        o_ref[...] = ((own_ref[...] + g0_ref[...].astype(F32)) + g1_ref[...].astype(F32)) + g2_ref[...].astype(F32)

    slab = lambda j: pl.BlockSpec((None, tile, cols), lambda i, pr: (j, i, 0))
    return pl.pallas_call(
        body, name=name,
        grid_spec=pltpu.PrefetchScalarGridSpec(
            num_scalar_prefetch=1, grid=(hr // tile,),
            in_specs=[pl.BlockSpec((None, tile, cols), lambda i, pr: (pr[0], i, 0)), slab(0), slab(1), slab(2)],
            out_specs=pl.BlockSpec((None, tile, cols), lambda i, pr: (pr[1], i, 0))),
        out_shape=jax.ShapeDtypeStruct((2, hr, cols), F32),
        compiler_params=_params("parallel"),
    )(place, s32, got, got, got)


def _reduce_scatter(grads):
    names = list(grads)
    place = jnp.stack([2 * lax.axis_index("x") + lax.axis_index("y"), lax.axis_index("c")]).astype(jnp.int32)
    got_a = _exchange_halves([grads[k] for k in names])
    sums = [_chip_sum("chip_sum_" + k, grads[k], g, place) for k, g in zip(names, got_a)]
    got_b = _scatter_to_chips([s[1] for s in sums])
    bufs = [_mesh_sum("mesh_sum_" + k, s[0], g, place) for k, s, g in zip(names, sums, got_b)]
    full = _share_halves(bufs)
    return {k: f.reshape(-1, f.shape[-1]) for k, f in zip(names, full)}


def kernel(x, p, norm_mix, w_in, conv_w, a_log, dt_bias, dn_norm, sinks, w_o, norm_mlp, w_up, w_down, norm_ple, w_ple_gate, w_ple_proj, norm_final, loss_target, m_norm_mix, m_w_in, m_conv_w, m_a_log, m_dt_bias, m_dn_norm, m_sinks, m_w_o, m_norm_mlp, m_w_up, m_w_down, m_norm_ple, m_w_ple_gate, m_w_ple_proj, m_norm_final, v_norm_mix, v_w_in, v_conv_w, v_a_log, v_dt_bias, v_dn_norm, v_sinks, v_w_o, v_norm_mlp, v_w_up, v_w_down, v_norm_ple, v_w_ple_gate, v_w_ple_proj, v_norm_final):
    chip = 2 * lax.axis_index("x") + lax.axis_index("y")
    big = dict(w_in=w_in[0], w_o=w_o[0], w_up=w_up[0], w_down=w_down[0], w_gate=w_ple_gate[0], w_proj=w_ple_proj[0])
    big_m = dict(w_in=m_w_in[0], w_o=m_w_o[0], w_up=m_w_up[0], w_down=m_w_down[0], w_gate=m_w_ple_gate[0], w_proj=m_w_ple_proj[0])
    big_v = dict(w_in=v_w_in[0], w_o=v_w_o[0], w_up=v_w_up[0], w_down=v_w_down[0], w_gate=v_w_ple_gate[0], w_proj=v_w_ple_proj[0])
    names = list(big)

    gathered = _gather_weights([big[k].astype(BF16) for k in names], conv_w[0])
    gw = dict(zip(names, gathered[:-1]))
    conv_full = jnp.transpose(gathered[-1], (1, 0, 2)).reshape(DN_CONV, CONV_CH)
    w = dict(w_in=jnp.transpose(gw["w_in"], (1, 0, 2)).reshape(D_MODEL, D_IN),
             w_o=gw["w_o"].reshape(D_MODEL, D_MODEL), w_up4=gw["w_up"], w_down=gw["w_down"].reshape(D_FF, D_MODEL),
             w_gate=gw["w_gate"].reshape(D_MODEL, D_MODEL), w_proj4=gw["w_proj"], conv_w=conv_full)
    sm = dict(norm_mix=norm_mix[0], a_log=a_log[0], dt_bias=dt_bias[0], dn_norm=dn_norm[0], sinks=sinks[0],
              norm_mlp=norm_mlp[0], norm_ple=norm_ple[0], norm_final=norm_final)

    sums, grad_x, g = _local_step(x[0], p[0, 0], loss_target[0], sm, w)

    per_chip = dict(
        w_in=jnp.transpose(g["w_in"].reshape(D_MODEL, N_CHIPS, D_IN // N_CHIPS), (1, 0, 2)),
        w_o=g["w_o"].reshape(N_CHIPS, D_MODEL // N_CHIPS, D_MODEL),
        w_up=g["w_up4"],
        w_down=g["w_down"].reshape(N_CHIPS, D_FF // N_CHIPS, D_MODEL),
        w_gate=g["w_gate"].reshape(N_CHIPS, D_MODEL // N_CHIPS, D_MODEL),
        w_proj=jnp.transpose(g["w_proj"].reshape(PLE_DIM, N_CHIPS, D_MODEL // N_CHIPS), (1, 0, 2)))
    red = _reduce_scatter(per_chip)

    row = lambda v: jnp.zeros((SMALL_COLS,), F32).at[:v.shape[0]].set(v)
    misc = jnp.zeros((SMALL_COLS,), F32).at[0:4].set(sums["a_log"]).at[4:8].set(sums["dt_bias"]) \
        .at[8:16].set(sums["sinks"]).at[128:256].set(sums["dn_norm"]).at[256].set(sums["loss"])
    small = jnp.concatenate([sums["conv_w"], jnp.stack([row(sums["norm_mix"]), row(sums["norm_mlp"]), row(sums["norm_ple"]),
                                                        row(sums["norm_final"]), misc]),
                             jnp.zeros((SMALL_ROWS - 9, SMALL_COLS), F32)], axis=0)
    tot = _allreduce_small(small)
    loss = tot[8, 256]
    ncw = CONV_CH // N_CHIPS

    def pack(cw, nmix, nmlp, nple, nfin, al, dtb, sk, dnn):
        misc_p = jnp.zeros((SMALL_COLS,), F32).at[0:4].set(al).at[4:8].set(dtb).at[8:16].set(sk).at[128:256].set(dnn)
        cw_p = jnp.zeros((DN_CONV, SMALL_COLS), F32).at[:, :ncw].set(cw)
        return jnp.concatenate([cw_p, jnp.stack([row(nmix), row(nmlp), row(nple), row(nfin), misc_p]),
                                jnp.zeros((SMALL_ROWS - 9, SMALL_COLS), F32)], axis=0)

    def unpack(buf):
        return dict(conv_w=buf[0:4, :ncw][None], norm_mix=buf[4, :D_MODEL][None], norm_mlp=buf[5, :D_MODEL][None],
                    norm_ple=buf[6, :D_MODEL][None], norm_final=buf[7, :D_MODEL], a_log=buf[8, 0:4][None],
                    dt_bias=buf[8, 4:8][None], sinks=buf[8, 8:16][None], dn_norm=buf[8, 128:256][None])

    g_conv_shard = lax.dynamic_slice(tot[0:4], (0, chip * ncw), (DN_CONV, ncw))
    g_small = pack(g_conv_shard, tot[4, :D_MODEL], tot[5, :D_MODEL], tot[6, :D_MODEL], tot[7, :D_MODEL],
                   tot[8, 0:4], tot[8, 4:8], tot[8, 8:16], tot[8, 128:256])
    w_small = pack(conv_w[0], norm_mix[0], norm_mlp[0], norm_ple[0], norm_final, a_log[0], dt_bias[0], sinks[0], dn_norm[0])
    m_small = pack(m_conv_w[0], m_norm_mix[0], m_norm_mlp[0], m_norm_ple[0], m_norm_final, m_a_log[0], m_dt_bias[0],
                   m_sinks[0], m_dn_norm[0])
    v_small = pack(v_conv_w[0], v_norm_mix[0], v_norm_mlp[0], v_norm_ple[0], v_norm_final, v_a_log[0], v_dt_bias[0],
                   v_sinks[0], v_dn_norm[0])

    d_s, m_s, v_s = (unpack(b) for b in _adamw("adamw_small", w_small, g_small, m_small, v_small))
    g_s = unpack(g_small)
    out_g, out_d, out_m, out_v = dict(g_s), dict(d_s), dict(m_s), dict(v_s)
    ref_name = dict(w_in="w_in", w_o="w_o", w_up="w_up", w_down="w_down", w_gate="w_ple_gate", w_proj="w_ple_proj")
    for k in names:
        d_k, m_k, v_k = _adamw("adamw_" + k, big[k], red[k], big_m[k], big_v[k])
        out_g[ref_name[k]], out_d[ref_name[k]] = red[k][None], d_k[None]
        out_m[ref_name[k]], out_v[ref_name[k]] = m_k[None], v_k[None]
    order = ["norm_mix", "w_in", "conv_w", "a_log", "dt_bias", "dn_norm", "sinks", "w_o", "norm_mlp", "w_up", "w_down",
             "norm_ple", "w_ple_gate", "w_ple_proj", "norm_final"]
    return (loss, grad_x[None], *[out_g[k] for k in order], *[out_d[k] for k in order],
            *[out_m[k] for k in order], *[out_v[k] for k in order])
```

```python
import functools

import jax
import jax.numpy as jnp
from jax import lax
from jax.experimental import pallas as pl
from jax.experimental.pallas import tpu as pltpu

F32 = jnp.float32
BF16 = jnp.bfloat16
MXU_DTYPE = jnp.bfloat16
HI = lax.Precision.HIGHEST

D_MODEL = 1024
PLE_DIM = 256
ATTN_HEADS = 8
ATTN_KV_HEADS = 2
ATTN_GROUPS = ATTN_HEADS // ATTN_KV_HEADS
ATTN_HEAD_DIM = 64
ATTN_BLOCK = 128
ROPE_THETA = 10000.0
DN_HEADS = 4
DN_HEAD_DIM = 128
DN_CONV = 4
DN_CHUNK = 64
D_FF = 4 * D_MODEL
EPS = 1e-6
ATTN_Q = ATTN_HEADS * ATTN_HEAD_DIM
ATTN_KV = ATTN_KV_HEADS * ATTN_HEAD_DIM
DN_W = DN_HEADS * DN_HEAD_DIM
CONV_CH = 3 * DN_W
D_IN = ATTN_Q + 2 * ATTN_KV + 4 * DN_W + 2 * DN_HEADS
DN_COLS = 4 * DN_W + 128
DN_SCALE = DN_HEAD_DIM ** -0.5
ATTN_SCALE = ATTN_HEAD_DIM ** -0.5
FF_BLOCKS = 4
FF_BLOCK = D_FF // FF_BLOCKS

ADAM_LR = 0.001
ADAM_B1 = 0.9
ADAM_B2 = 0.999
ADAM_EPS = 1e-08
ADAM_WD = 0.01
ADAM_STEP = 10

V7X_VMEM_BYTES = 64 * 1024 * 1024
VMEM_LIMIT = 48 * 1024 * 1024

NN = ((1,), (0,))
NT = ((1,), (1,))
TN = ((0,), (0,))


def _dot(a, b, dims=NN, prec=None):
    return lax.dot_general(a, b, (dims, ((), ())), precision=prec, preferred_element_type=F32)


def _sigmoid(x):
    return 1.0 / (1.0 + jnp.exp(-x))


def _softplus(x):
    return jnp.maximum(x, 0.0) + jnp.log(1.0 + jnp.exp(-jnp.abs(x)))


def _params(*sem):
    return pltpu.CompilerParams(dimension_semantics=sem, vmem_limit_bytes=VMEM_LIMIT)


def _rms_fwd(xv, g):
    r = lax.rsqrt(jnp.mean(xv * xv, axis=-1, keepdims=True) + EPS)
    return xv * r * g


def _rms_bwd(xv, g, dn):
    r = lax.rsqrt(jnp.mean(xv * xv, axis=-1, keepdims=True) + EPS)
    xh = xv * r
    dg = jnp.sum(dn * xh, axis=0, keepdims=True)
    dxh = dn * g
    dx = r * (dxh - xh * jnp.mean(dxh * xh, axis=-1, keepdims=True))
    return dx, dg


def _full(shape):
    return pl.BlockSpec(shape, lambda *_: (0,) * len(shape))


def _inproj(x, g_mix, wa, wd, tm):
    t = x.shape[0]

    def body(x_ref, g_ref, wa_ref, wd_ref, u_ref, pa_ref, pd_ref):
        u = _rms_fwd(x_ref[...], g_ref[...]).astype(MXU_DTYPE)
        u_ref[...] = u
        pa_ref[...] = _dot(u, wa_ref[...])
        pd_ref[...] = _dot(u, wd_ref[...])

    na, nd = wa.shape[1], wd.shape[1]
    return pl.pallas_call(
        body, name="inproj", grid=(t // tm,),
        in_specs=[pl.BlockSpec((tm, D_MODEL), lambda i: (i, 0)), _full((1, D_MODEL)),
                  _full((D_MODEL, na)), _full((D_MODEL, nd))],
        out_specs=[pl.BlockSpec((tm, D_MODEL), lambda i: (i, 0)), pl.BlockSpec((tm, na), lambda i: (i, 0)),
                   pl.BlockSpec((tm, nd), lambda i: (i, 0))],
        out_shape=[jax.ShapeDtypeStruct((t, D_MODEL), MXU_DTYPE), jax.ShapeDtypeStruct((t, na), F32),
                   jax.ShapeDtypeStruct((t, nd), F32)],
        compiler_params=_params("parallel"),
    )(x, g_mix, wa, wd)


def _oproj(x, ao, dn, wo_a, wo_d, tm):
    t = x.shape[0]

    def body(x_ref, ao_ref, dn_ref, wa_ref, wd_ref, h_ref):
        h_ref[...] = (x_ref[...] + _dot(ao_ref[...].astype(MXU_DTYPE), wa_ref[...])
                      + _dot(dn_ref[...].astype(MXU_DTYPE), wd_ref[...]))

    half = ao.shape[1]
    return pl.pallas_call(
        body, name="oproj", grid=(t // tm,),
        in_specs=[pl.BlockSpec((tm, D_MODEL), lambda i: (i, 0)), pl.BlockSpec((tm, half), lambda i: (i, 0)),
                  pl.BlockSpec((tm, half), lambda i: (i, 0)), _full((half, D_MODEL)), _full((half, D_MODEL))],
        out_specs=pl.BlockSpec((tm, D_MODEL), lambda i: (i, 0)),
        out_shape=jax.ShapeDtypeStruct((t, D_MODEL), F32),
        compiler_params=_params("parallel"),
    )(x, ao, dn, wo_a, wo_d)


def _mlp_fwd(h1, g_mlp, w_up4, w_down, tm):
    t = h1.shape[0]

    def body(h_ref, g_ref, wu_ref, wd_ref, m_ref, a_ref, h2_ref, acc_ref):
        k = pl.program_id(1)

        @pl.when(k == 0)
        def _():
            m_ref[...] = _rms_fwd(h_ref[...], g_ref[...]).astype(MXU_DTYPE)
            acc_ref[...] = jnp.zeros_like(acc_ref)

        a = _dot(m_ref[...], wu_ref[...])
        a_ref[...] = a
        s = jnp.square(jnp.maximum(a, 0.0)).astype(MXU_DTYPE)
        acc_ref[...] += _dot(s, wd_ref[...])

        @pl.when(k == FF_BLOCKS - 1)
        def _():
            h2_ref[...] = h_ref[...] + acc_ref[...]

    return pl.pallas_call(
        body, name="mlp_fwd", grid=(t // tm, FF_BLOCKS),
        in_specs=[pl.BlockSpec((tm, D_MODEL), lambda i, k: (i, 0)), _full((1, D_MODEL)),
                  pl.BlockSpec((None, D_MODEL, FF_BLOCK), lambda i, k: (k, 0, 0)),
                  pl.BlockSpec((FF_BLOCK, D_MODEL), lambda i, k: (k, 0))],
        out_specs=[pl.BlockSpec((tm, D_MODEL), lambda i, k: (i, 0)), pl.BlockSpec((tm, FF_BLOCK), lambda i, k: (i, k)),
                   pl.BlockSpec((tm, D_MODEL), lambda i, k: (i, 0))],
        out_shape=[jax.ShapeDtypeStruct((t, D_MODEL), MXU_DTYPE), jax.ShapeDtypeStruct((t, D_FF), F32),
                   jax.ShapeDtypeStruct((t, D_MODEL), F32)],
        scratch_shapes=[pltpu.VMEM((tm, D_MODEL), F32)],
        compiler_params=_params("parallel", "arbitrary"),
    )(h1, g_mlp, w_up4, w_down)


def _ple_loss(h2, p, tgt, g_ple, g_fin, w_gate, w_proj, tm):
    t = h2.shape[0]

    def body(h_ref, p_ref, t_ref, gp_ref, gf_ref, wg_ref, wp_ref,
             dh_ref, dhb_ref, dgp_ref, dpp_ref, n3_ref, pb_ref, acc_ref):
        @pl.when(pl.program_id(0) == 0)
        def _():
            acc_ref[...] = jnp.zeros_like(acc_ref)

        h = h_ref[...]
        g_ple_v, g_fin_v = gp_ref[...], gf_ref[...]
        n3 = _rms_fwd(h, g_ple_v).astype(MXU_DTYPE)
        n3_ref[...] = n3
        gate = _sigmoid(_dot(n3, wg_ref[...]))
        pb = p_ref[...].astype(MXU_DTYPE)
        pb_ref[...] = pb
        pp = _dot(pb, wp_ref[...])
        h3 = h + gate * pp
        r4 = lax.rsqrt(jnp.mean(h3 * h3, axis=-1, keepdims=True) + EPS)
        xh4 = h3 * r4
        e = xh4 * g_fin_v - t_ref[...]
        loss = 0.5 * jnp.sum(jnp.mean(e * e, axis=-1, keepdims=True), axis=0, keepdims=True)
        dy = e * (1.0 / D_MODEL)
        dg_fin = jnp.sum(dy * xh4, axis=0, keepdims=True)
        dxh = dy * g_fin_v
        dh3 = r4 * (dxh - xh4 * jnp.mean(dxh * xh4, axis=-1, keepdims=True))
        dpp_ref[...] = (dh3 * gate).astype(MXU_DTYPE)
        dgp = (dh3 * pp * gate * (1.0 - gate)).astype(MXU_DTYPE)
        dgp_ref[...] = dgp
        dn3 = _dot(dgp, wg_ref[...], NT)
        dx, dg_ple = _rms_bwd(h, g_ple_v, dn3)
        dh2 = dh3 + dx
        dh_ref[...] = dh2
        dhb_ref[...] = dh2.astype(MXU_DTYPE)
        acc_ref[0:1, :] += dg_fin
        acc_ref[1:2, :] += dg_ple
        acc_ref[2:3, :] += jnp.broadcast_to(loss, (1, D_MODEL))

    row = lambda w: pl.BlockSpec((tm, w), lambda i: (i, 0))
    return pl.pallas_call(
        body, name="ple_loss", grid=(t // tm,),
        in_specs=[row(D_MODEL), row(PLE_DIM), row(D_MODEL), _full((1, D_MODEL)), _full((1, D_MODEL)),
                  _full((D_MODEL, D_MODEL)), _full((PLE_DIM, D_MODEL))],
        out_specs=[row(D_MODEL), row(D_MODEL), row(D_MODEL), row(D_MODEL), row(D_MODEL), row(PLE_DIM),
                   _full((8, D_MODEL))],
        out_shape=[jax.ShapeDtypeStruct((t, D_MODEL), F32), jax.ShapeDtypeStruct((t, D_MODEL), MXU_DTYPE),
                   jax.ShapeDtypeStruct((t, D_MODEL), MXU_DTYPE), jax.ShapeDtypeStruct((t, D_MODEL), MXU_DTYPE),
                   jax.ShapeDtypeStruct((t, D_MODEL), MXU_DTYPE), jax.ShapeDtypeStruct((t, PLE_DIM), MXU_DTYPE),
                   jax.ShapeDtypeStruct((8, D_MODEL), F32)],
        compiler_params=_params("arbitrary"),
    )(h2, p, tgt, g_ple, g_fin, w_gate, w_proj)


def _mlp_bwd(dh2, dh2b, a, h1, g_mlp, w_up4, w_down, tm):
    t = h1.shape[0]

    def body(dh_ref, dhb_ref, a_ref, h_ref, g_ref, wu_ref, wd_ref,
             s_ref, da_ref, dh1_ref, dh1b_ref, acc_ref, dm_ref):
        i, k = pl.program_id(0), pl.program_id(1)

        @pl.when((i == 0) & (k == 0))
        def _():
            acc_ref[...] = jnp.zeros_like(acc_ref)

        @pl.when(k == 0)
        def _():
            dm_ref[...] = jnp.zeros_like(dm_ref)

        ds = _dot(dhb_ref[...], wd_ref[...], NT)
        r = jnp.maximum(a_ref[...], 0.0)
        s_ref[...] = (r * r).astype(MXU_DTYPE)
        da = (ds * (2.0 * r)).astype(MXU_DTYPE)
        da_ref[...] = da
        dm_ref[...] += _dot(da, wu_ref[...], NT)

        @pl.when(k == FF_BLOCKS - 1)
        def _():
            dx, dg = _rms_bwd(h_ref[...], g_ref[...], dm_ref[...])
            dh1 = dh_ref[...] + dx
            dh1_ref[...] = dh1
            dh1b_ref[...] = dh1.astype(MXU_DTYPE)
            acc_ref[0:1, :] += dg

    tok = lambda w: pl.BlockSpec((tm, w), lambda i, k: (i, 0))
    return pl.pallas_call(
        body, name="mlp_bwd", grid=(t // tm, FF_BLOCKS),
        in_specs=[tok(D_MODEL), tok(D_MODEL), pl.BlockSpec((tm, FF_BLOCK), lambda i, k: (i, k)), tok(D_MODEL),
                  _full((1, D_MODEL)), pl.BlockSpec((None, D_MODEL, FF_BLOCK), lambda i, k: (k, 0, 0)),
                  pl.BlockSpec((FF_BLOCK, D_MODEL), lambda i, k: (k, 0))],
        out_specs=[pl.BlockSpec((tm, FF_BLOCK), lambda i, k: (i, k)), pl.BlockSpec((tm, FF_BLOCK), lambda i, k: (i, k)),
                   tok(D_MODEL), tok(D_MODEL), pl.BlockSpec((8, D_MODEL), lambda i, k: (0, 0))],
        out_shape=[jax.ShapeDtypeStruct((t, D_FF), MXU_DTYPE), jax.ShapeDtypeStruct((t, D_FF), MXU_DTYPE),
                   jax.ShapeDtypeStruct((t, D_MODEL), F32), jax.ShapeDtypeStruct((t, D_MODEL), MXU_DTYPE),
                   jax.ShapeDtypeStruct((8, D_MODEL), F32)],
        scratch_shapes=[pltpu.VMEM((tm, D_MODEL), F32)],
        compiler_params=_params("arbitrary", "arbitrary"),
    )(dh2, dh2b, a, h1, g_mlp, w_up4, w_down)


def _oproj_bwd(dh1b, wo_a, wo_d, tm):
    t = dh1b.shape[0]
    half = wo_a.shape[0]

    def body(d_ref, wa_ref, wd_ref, da_ref, dd_ref):
        d = d_ref[...]
        da_ref[...] = _dot(d, wa_ref[...], NT)
        dd_ref[...] = _dot(d, wd_ref[...], NT)

    return pl.pallas_call(
        body, name="oproj_bwd", grid=(t // tm,),
        in_specs=[pl.BlockSpec((tm, D_MODEL), lambda i: (i, 0)), _full((half, D_MODEL)), _full((half, D_MODEL))],
        out_specs=[pl.BlockSpec((tm, half), lambda i: (i, 0)), pl.BlockSpec((tm, half), lambda i: (i, 0))],
        out_shape=[jax.ShapeDtypeStruct((t, half), F32), jax.ShapeDtypeStruct((t, half), F32)],
        compiler_params=_params("parallel"),
    )(dh1b, wo_a, wo_d)


def _inproj_bwd(x, dh1, g_mix, grads, weights, tm):
    t = x.shape[0]
    n = len(grads)

    def body(*refs):
        x_ref, dh_ref, g_ref = refs[:3]
        g_refs, w_refs = refs[3:3 + n], refs[3 + n:3 + 2 * n]
        dx_ref, acc_ref = refs[3 + 2 * n:]

        @pl.when(pl.program_id(0) == 0)
        def _():
            acc_ref[...] = jnp.zeros_like(acc_ref)

        du = _dot(g_refs[0][...], w_refs[0][...], NT)
        for j in range(1, n):
            du += _dot(g_refs[j][...], w_refs[j][...], NT)
        dx, dg = _rms_bwd(x_ref[...], g_ref[...], du)
        dx_ref[...] = dh_ref[...] + dx
        acc_ref[0:1, :] += dg

    tok = lambda w: pl.BlockSpec((tm, w), lambda i: (i, 0))
    return pl.pallas_call(
        body, name="inproj_bwd", grid=(t // tm,),
        in_specs=[tok(D_MODEL), tok(D_MODEL), _full((1, D_MODEL))] + [tok(g.shape[1]) for g in grads]
                 + [_full(w.shape) for w in weights],
        out_specs=[tok(D_MODEL), _full((8, D_MODEL))],
        out_shape=[jax.ShapeDtypeStruct((t, D_MODEL), F32), jax.ShapeDtypeStruct((8, D_MODEL), F32)],
        compiler_params=_params("arbitrary"),
    )(x, dh1, g_mix, *grads, *weights)


def _wgrad(a, b, name, tk, tn, tt, col0=0, ncols=None):
    t, kdim = a.shape
    ncols = b.shape[1] - col0 if ncols is None else ncols
    cb = col0 // tn

    def body(a_ref, b_ref, o_ref):
        @pl.when(pl.program_id(2) == 0)
        def _():
            o_ref[...] = jnp.zeros_like(o_ref)

        o_ref[...] += _dot(a_ref[...], b_ref[...], TN)

    return pl.pallas_call(
        body, name=name, grid=(kdim // tk, ncols // tn, t // tt),
        in_specs=[pl.BlockSpec((tt, tk), lambda i, j, s: (s, i)), pl.BlockSpec((tt, tn), lambda i, j, s: (s, j + cb))],
        out_specs=pl.BlockSpec((tk, tn), lambda i, j, s: (i, j)),
        out_shape=jax.ShapeDtypeStruct((kdim, ncols), F32),
        compiler_params=_params("parallel", "parallel", "arbitrary"),
    )(a, b)


def _rope_tables(t):
    half = ATTN_HEAD_DIM // 2
    inv = 1.0 / (ROPE_THETA ** (jnp.arange(half, dtype=F32) * (2.0 / ATTN_HEAD_DIM)))
    ang = jnp.arange(t, dtype=F32)[:, None] * inv[None, :]
    cos, sin = jnp.cos(ang), jnp.sin(ang)
    cos2 = jnp.concatenate([cos, cos], axis=-1)
    sin2 = jnp.concatenate([-sin, sin], axis=-1)
    return jnp.tile(cos2, (1, 2)), jnp.tile(sin2, (1, 2))


def _swap_halves(tv):
    w = tv.shape[-1]
    lane = lax.broadcasted_iota(jnp.int32, tv.shape, tv.ndim - 1)
    first = (lane % ATTN_HEAD_DIM) < (ATTN_HEAD_DIM // 2)
    return jnp.where(first, pltpu.roll(tv, w - ATTN_HEAD_DIM // 2, tv.ndim - 1),
                     pltpu.roll(tv, ATTN_HEAD_DIM // 2, tv.ndim - 1))


def _rope(tv, cos, sin):
    return tv * cos + _swap_halves(tv) * sin


def _rope_bwd(dv, cos, sin):
    return dv * cos + _swap_halves(dv * sin)


def _attn_probs(qh, kwin, sink, first_block):
    s = _dot(qh, kwin, NT) * ATTN_SCALE
    r = lax.broadcasted_iota(jnp.int32, s.shape, 0)
    c = lax.broadcasted_iota(jnp.int32, s.shape, 1)
    valid = (c > r) & (c <= r + ATTN_BLOCK) & ((c >= ATTN_BLOCK) | jnp.logical_not(first_block))
    s = jnp.where(valid, s, -jnp.inf)
    m = jnp.maximum(jnp.max(s, axis=-1, keepdims=True), sink)
    e = jnp.where(valid, jnp.exp(s - m), 0.0)
    es = jnp.exp(sink - m)
    inv = 1.0 / (jnp.sum(e, axis=-1, keepdims=True) + es)
    return e * inv, es * inv


def _lane_scalar(vec, idx):
    lane = lax.broadcasted_iota(jnp.int32, vec.shape, 1)
    return jnp.sum(jnp.where(lane == idx, vec, 0.0), axis=-1, keepdims=True)


def _attn_specs(nb):
    cur = lambda w, cb: pl.BlockSpec((ATTN_BLOCK, w), lambda i: (jnp.minimum(i, nb - 1), cb))
    prev = lambda w, cb: pl.BlockSpec((ATTN_BLOCK, w), lambda i: (jnp.maximum(jnp.minimum(i, nb - 1) - 1, 0), cb))
    kcol, vcol = ATTN_Q // ATTN_KV, ATTN_Q // ATTN_KV + 1
    return [cur(ATTN_Q, 0), cur(ATTN_KV, kcol), prev(ATTN_KV, kcol), cur(ATTN_KV, vcol), prev(ATTN_KV, vcol),
            cur(ATTN_KV, 0), cur(ATTN_KV, 0), prev(ATTN_KV, 0), prev(ATTN_KV, 0), _full((1, 128))]


def _attn_fwd(pa, cos, sin, sinks_vec):
    t = pa.shape[0]
    nb = t // ATTN_BLOCK

    def body(q_ref, kc_ref, kp_ref, vc_ref, vp_ref, cc_ref, sc_ref, cp_ref, sp_ref, sk_ref, o_ref):
        first = pl.program_id(0) == 0
        cc, sc = cc_ref[...], sc_ref[...]
        q = _rope(q_ref[...], jnp.tile(cc, (1, ATTN_Q // ATTN_KV)), jnp.tile(sc, (1, ATTN_Q // ATTN_KV)))
        kc = _rope(kc_ref[...], cc, sc)
        kp = _rope(kp_ref[...], cp_ref[...], sp_ref[...])
        vc, vp = vc_ref[...], vp_ref[...]
        sk = sk_ref[...]
        for hk in range(ATTN_KV_HEADS):
            ks = slice(hk * ATTN_HEAD_DIM, (hk + 1) * ATTN_HEAD_DIM)
            kwin = jnp.concatenate([kp[:, ks], kc[:, ks]], axis=0)
            vwin = jnp.concatenate([vp[:, ks], vc[:, ks]], axis=0)
            for g in range(ATTN_GROUPS):
                h = hk * ATTN_GROUPS + g
                hs = slice(h * ATTN_HEAD_DIM, (h + 1) * ATTN_HEAD_DIM)
                probs, _ = _attn_probs(q[:, hs], kwin, _lane_scalar(sk, h), first)
                o_ref[:, hs] = _dot(probs, vwin)

    return pl.pallas_call(
        body, name="attn_fwd", grid=(nb,),
        in_specs=_attn_specs(nb),
        out_specs=pl.BlockSpec((ATTN_BLOCK, ATTN_Q), lambda i: (i, 0)),
        out_shape=jax.ShapeDtypeStruct((t, ATTN_Q), F32),
        compiler_params=_params("parallel"),
    )(pa, pa, pa, pa, pa, cos, sin, cos, sin, sinks_vec)


def _attn_bwd(pa, cos, sin, sinks_vec, dao):
    t = pa.shape[0]
    nb = t // ATTN_BLOCK

    def body(q_ref, kc_ref, kp_ref, vc_ref, vp_ref, cc_ref, sc_ref, cp_ref, sp_ref, sk_ref, do_ref,
             dq_ref, dk_ref, dv_ref, acc_ref, dqr_ref, dkw_ref, dvw_ref, ck_ref, cv_ref):
        i = pl.program_id(0)

        @pl.when(i == 0)
        def _():
            acc_ref[...] = jnp.zeros_like(acc_ref)
            ck_ref[...] = jnp.zeros_like(ck_ref)
            cv_ref[...] = jnp.zeros_like(cv_ref)

        @pl.when(i < nb)
        def _():
            first = i == 0
            cc, sc = cc_ref[...], sc_ref[...]
            cq, sq = jnp.tile(cc, (1, ATTN_Q // ATTN_KV)), jnp.tile(sc, (1, ATTN_Q // ATTN_KV))
            q = _rope(q_ref[...], cq, sq)
            kc = _rope(kc_ref[...], cc, sc)
            kp = _rope(kp_ref[...], cp_ref[...], sp_ref[...])
            vc, vp = vc_ref[...], vp_ref[...]
            sk = sk_ref[...]
            do = do_ref[...]
            lane = lax.broadcasted_iota(jnp.int32, (1, 128), 1)
            dsink = jnp.zeros((1, 128), F32)
            for hk in range(ATTN_KV_HEADS):
                ks = slice(hk * ATTN_HEAD_DIM, (hk + 1) * ATTN_HEAD_DIM)
                kwin = jnp.concatenate([kp[:, ks], kc[:, ks]], axis=0)
                vwin = jnp.concatenate([vp[:, ks], vc[:, ks]], axis=0)
                dkw = jnp.zeros((2 * ATTN_BLOCK, ATTN_HEAD_DIM), F32)
                dvw = jnp.zeros((2 * ATTN_BLOCK, ATTN_HEAD_DIM), F32)
                for g in range(ATTN_GROUPS):
                    h = hk * ATTN_GROUPS + g
                    hs = slice(h * ATTN_HEAD_DIM, (h + 1) * ATTN_HEAD_DIM)
                    qh = q[:, hs]
                    probs, psink = _attn_probs(qh, kwin, _lane_scalar(sk, h), first)
                    doh = do[:, hs]
                    dp = _dot(doh, vwin, NT)
                    delta = jnp.sum(probs * dp, axis=-1, keepdims=True)
                    ds = probs * (dp - delta) * ATTN_SCALE
                    dqr_ref[:, hs] = _dot(ds, kwin)
                    dkw += _dot(ds, qh, TN)
                    dvw += _dot(probs, doh, TN)
                    dsink += jnp.where(lane == h, jnp.sum(-psink * delta, axis=0, keepdims=True), 0.0)
                dkw_ref[:, ks] = dkw
                dvw_ref[:, ks] = dvw
            acc_ref[0:1, :] += dsink
            dq_ref[...] = _rope_bwd(dqr_ref[...], cq, sq).astype(dq_ref.dtype)
            dk_ref[...] = (ck_ref[...] + _rope_bwd(dkw_ref[0:ATTN_BLOCK, :], cp_ref[...], sp_ref[...])).astype(dk_ref.dtype)
            dv_ref[...] = (cv_ref[...] + dvw_ref[0:ATTN_BLOCK, :]).astype(dv_ref.dtype)
            ck_ref[...] = _rope_bwd(dkw_ref[ATTN_BLOCK:2 * ATTN_BLOCK, :], cc, sc)
            cv_ref[...] = dvw_ref[ATTN_BLOCK:2 * ATTN_BLOCK, :]

        @pl.when(i == nb)
        def _():
            dk_ref[...] = ck_ref[...].astype(dk_ref.dtype)
            dv_ref[...] = cv_ref[...].astype(dv_ref.dtype)

    prev_out = lambda w: pl.BlockSpec((ATTN_BLOCK, w), lambda i: (jnp.maximum(i - 1, 0), 0))
    return pl.pallas_call(
        body, name="attn_bwd", grid=(nb + 1,),
        in_specs=_attn_specs(nb) + [pl.BlockSpec((ATTN_BLOCK, ATTN_Q), lambda i: (jnp.minimum(i, nb - 1), 0))],
        out_specs=[pl.BlockSpec((ATTN_BLOCK, ATTN_Q), lambda i: (jnp.minimum(i, nb - 1), 0)), prev_out(ATTN_KV),
                   prev_out(ATTN_KV), _full((8, 128))],
        out_shape=[jax.ShapeDtypeStruct((t, ATTN_Q), MXU_DTYPE), jax.ShapeDtypeStruct((t, ATTN_KV), MXU_DTYPE),
                   jax.ShapeDtypeStruct((t, ATTN_KV), MXU_DTYPE), jax.ShapeDtypeStruct((8, 128), F32)],
        scratch_shapes=[pltpu.VMEM((ATTN_BLOCK, ATTN_Q), F32), pltpu.VMEM((2 * ATTN_BLOCK, ATTN_KV), F32),
                        pltpu.VMEM((2 * ATTN_BLOCK, ATTN_KV), F32), pltpu.VMEM((ATTN_BLOCK, ATTN_KV), F32),
                        pltpu.VMEM((ATTN_BLOCK, ATTN_KV), F32)],
        compiler_params=_params("arbitrary"),
    )(pa, pa, pa, pa, pa, cos, sin, cos, sin, sinks_vec, dao)


PAIR = 2 * DN_CHUNK
HALO = 8


def _conv_window(cur_ref, prev_ref, xs_ref, tm):
    prev = jnp.where(pl.program_id(0) > 0, prev_ref[...], 0.0)
    xs_ref[0:HALO, :] = prev
    xs_ref[HALO:HALO + tm, :] = cur_ref[...]


def _conv_taps(xs_ref, cw_ref, tm):
    y = cw_ref[0:1, :] * xs_ref[pl.ds(HALO - DN_CONV + 1, tm), :]
    for j in range(1, DN_CONV):
        y += cw_ref[j:j + 1, :] * xs_ref[pl.ds(HALO - DN_CONV + 1 + j, tm), :]
    return y


def _gate_values(ba, al, dt):
    beta = _sigmoid(ba)
    pre = ba + dt
    g = -jnp.exp(al) * _softplus(pre)
    return beta, g, pre


def _dn_prep_specs(tm, t):
    return [pl.BlockSpec((tm, CONV_CH), lambda i: (i, 0)),
            pl.BlockSpec((HALO, CONV_CH), lambda i: (jnp.maximum(i * (tm // HALO) - 1, 0), 0)),
            pl.BlockSpec((tm, 128), lambda i: (i, 4 * DN_W // 128)),
            _full((DN_CONV, CONV_CH)), _full((1, 128)), _full((1, 128))]


def _dn_prep(pd, conv_w, al_vec, dt_vec, tm):
    t = pd.shape[0]

    def body(cur_ref, prev_ref, ba_ref, cw_ref, al_ref, dt_ref, qn_ref, kn_ref, vc_ref, gc_ref, gr_ref, xs_ref):
        _conv_window(cur_ref, prev_ref, xs_ref, tm)
        y = _conv_taps(xs_ref, cw_ref, tm)
        c = y * _sigmoid(y)
        for h in range(DN_HEADS):
            qs = slice(h * DN_HEAD_DIM, (h + 1) * DN_HEAD_DIM)
            ksl = slice(DN_W + h * DN_HEAD_DIM, DN_W + (h + 1) * DN_HEAD_DIM)
            qh, kh = c[:, qs], c[:, ksl]
            qn_ref[:, qs] = qh * lax.rsqrt(jnp.sum(qh * qh, axis=-1, keepdims=True) + EPS) * DN_SCALE
            kn_ref[:, qs] = kh * lax.rsqrt(jnp.sum(kh * kh, axis=-1, keepdims=True) + EPS)
        vc_ref[...] = c[:, 2 * DN_W:3 * DN_W]
        beta, g, _ = _gate_values(ba_ref[...], al_ref[...], dt_ref[...])
        lane = lax.broadcasted_iota(jnp.int32, beta.shape, 1)
        gb = jnp.where(lane < DN_HEADS, beta, jnp.where(lane < 2 * DN_HEADS, g, 0.0))
        gc_ref[...] = gb
        gr_ref[...] = gb.T[0:8, :]

    tok = lambda w: pl.BlockSpec((tm, w), lambda i: (i, 0))
    return pl.pallas_call(
        body, name="dn_prep", grid=(t // tm,),
        in_specs=_dn_prep_specs(tm, t),
        out_specs=[tok(DN_W), tok(DN_W), tok(DN_W), tok(128), pl.BlockSpec((8, tm), lambda i: (0, i))],
        out_shape=[jax.ShapeDtypeStruct((t, DN_W), F32)] * 3 + [jax.ShapeDtypeStruct((t, 128), F32),
                                                                 jax.ShapeDtypeStruct((8, t), F32)],
        scratch_shapes=[pltpu.VMEM((HALO + tm, CONV_CH), F32)],
        compiler_params=_params("parallel"),
    )(pd, pd, pd, conv_w, al_vec, dt_vec)


def _pair_masks():
    r = lax.broadcasted_iota(jnp.int32, (PAIR, PAIR), 0)
    c = lax.broadcasted_iota(jnp.int32, (PAIR, PAIR), 1)
    same = (r < DN_CHUNK) == (c < DN_CHUNK)
    return same & (r >= c), same & (r > c)


def _lane_col(mat, idx):
    lane = lax.broadcasted_iota(jnp.int32, mat.shape, 1)
    return jnp.sum(jnp.where(lane == idx, mat, 0.0), axis=-1, keepdims=True)


def _pair_cumsums(gc, gr, low):
    lowf = low.astype(F32)
    return _dot(lowf, gc, NN, HI), _dot(gr, lowf, NT, HI)


def _pair_gates(gc, cum_c, cum_r, low, h):
    beta = _lane_col(gc, h)
    gam = _lane_col(cum_c, DN_HEADS + h)
    gam_row = cum_r[DN_HEADS + h:DN_HEADS + h + 1, :]
    dm = jnp.where(low, jnp.exp(jnp.where(low, gam - gam_row, 0.0)), 0.0)
    row = lax.broadcasted_iota(jnp.int32, gam.shape, 0)
    gl = jnp.where(row < DN_CHUNK, gam[DN_CHUNK - 1:DN_CHUNK, :], gam[PAIR - 1:PAIR, :])
    return beta, gam, dm, gl


def _split(a):
    hi = a.astype(BF16)
    return hi, (a - hi.astype(F32)).astype(BF16)


def _dot_split(a, b, dims=NN):
    (ah, al), (bh, bl) = a, b
    la, lb = (1, 1) if dims == TN else ((0, 1) if dims == NN else (0, 0))
    r = _dot(jnp.concatenate([ah, al], axis=la), jnp.concatenate([bh, bl], axis=lb), dims)
    m, n = r.shape[0] // 2, r.shape[1] // 2
    return (r[m:, n:] + (r[:m, n:] + r[m:, :n])) + r[:m, :n]


def _unit_lower_inverses(lmats):
    n = lmats[0].shape[0]
    eye = (lax.broadcasted_iota(jnp.int32, (n, n), 0) == lax.broadcasted_iota(jnp.int32, (n, n), 1)).astype(F32)
    accs = [eye - l for l in lmats]
    splits = [_split(l) for l in lmats]
    step = 1
    while 2 * step < DN_CHUNK:
        splits = [_split(_dot_split(s, s)) for s in splits]
        accs = [acc + _dot_split(_split(acc), s) for acc, s in zip(accs, splits)]
        step *= 2
    return accs


def _dn_intra(qn, kn, vc, gc, gr):
    t = qn.shape[0]
    npair = t // PAIR

    def body(q_ref, k_ref, v_ref, gc_ref, gr_ref, u_ref, w_ref, qg_ref, kd_ref, a_ref, ti_ref, dl_ref):
        gc_v = gc_ref[...]
        low, strict = _pair_masks()
        cum_c, cum_r = _pair_cumsums(gc_v, gr_ref[...], low)
        heads = [slice(h * DN_HEAD_DIM, (h + 1) * DN_HEAD_DIM) for h in range(DN_HEADS)]
        gates = [_pair_gates(gc_v, cum_c, cum_r, low, h) for h in range(DN_HEADS)]
        lmats = []
        for hs, (beta, gam, dm, gl) in zip(heads, gates):
            k = k_ref[:, hs]
            lmats.append(jnp.where(strict, _dot(k * beta, k, NT) * dm, 0.0))
        tinvs = _unit_lower_inverses(lmats)
        for h, (hs, (beta, gam, dm, gl), tinv) in enumerate(zip(heads, gates, tinvs)):
            q, k, v = q_ref[:, hs], k_ref[:, hs], v_ref[:, hs]
            eg = jnp.exp(gam)
            u_ref[:, hs] = _dot(tinv, v * beta)
            w_ref[:, hs] = _dot(tinv, (k * beta) * eg)
            a_ref[h] = _dot(q, k, NT) * dm
            ti_ref[h] = tinv
            qg_ref[:, hs] = q * eg
            kd_ref[:, hs] = k * jnp.exp(gl - gam)
            for c in range(2):
                last = (c + 1) * DN_CHUNK - 1
                dl_ref[c, h] = jnp.broadcast_to(jnp.exp(gam[last:last + 1, :]), (8, 128))

    tok = lambda w: pl.BlockSpec((PAIR, w), lambda n: (n, 0))
    hm = pl.BlockSpec((DN_HEADS, PAIR, PAIR), lambda n: (0, n, 0))
    return pl.pallas_call(
        body, name="dn_intra", grid=(npair,),
        in_specs=[tok(DN_W), tok(DN_W), tok(DN_W), tok(128), pl.BlockSpec((8, PAIR), lambda n: (0, n))],
        out_specs=[tok(DN_W)] * 4 + [hm, hm, pl.BlockSpec((2, DN_HEADS, 8, 128), lambda n: (n, 0, 0, 0))],
        out_shape=[jax.ShapeDtypeStruct((t, DN_W), F32)] * 4 + [jax.ShapeDtypeStruct((DN_HEADS, t, PAIR), F32)] * 2
                  + [jax.ShapeDtypeStruct((2 * npair, DN_HEADS, 8, 128), F32)],
        compiler_params=_params("parallel"),
    )(qn, kn, vc, gc, gr)


def _dn_scan_fwd(u, w, qg, kd, a_qk, dlast, pd, dn_w):
    t = u.shape[0]
    npair = t // PAIR

    def body(u_ref, w_ref, qg_ref, kd_ref, a_ref, dl_ref, z_ref, nw_ref, out_ref, o_ref, vn_ref, sall_ref, s_ref):
        @pl.when(pl.program_id(0) == 0)
        def _():
            s_ref[...] = jnp.zeros_like(s_ref)

        nw = nw_ref[...]
        for c in range(2):
            rows = slice(c * DN_CHUNK, (c + 1) * DN_CHUNK)
            for h in range(DN_HEADS):
                hs = slice(h * DN_HEAD_DIM, (h + 1) * DN_HEAD_DIM)
                st = s_ref[h]
                sall_ref[c, h] = st
                vn_ref[rows, hs] = u_ref[rows, hs] - _dot(w_ref[rows, hs], st)
            for h in range(DN_HEADS):
                hs = slice(h * DN_HEAD_DIM, (h + 1) * DN_HEAD_DIM)
                st, vn = s_ref[h], vn_ref[rows, hs]
                o = _dot(qg_ref[rows, hs], st) + _dot(a_ref[h, rows, rows], vn)
                s_ref[h] = st * dl_ref[c, h][0:1, :] + _dot(kd_ref[rows, hs], vn, TN)
                o_ref[rows, hs] = o
                z = z_ref[rows, hs]
                on = o * lax.rsqrt(jnp.mean(o * o, axis=-1, keepdims=True) + EPS) * nw
                out_ref[rows, hs] = on * (z * _sigmoid(z))

    tok = pl.BlockSpec((PAIR, DN_W), lambda n: (n, 0))
    hm = pl.BlockSpec((DN_HEADS, PAIR, PAIR), lambda n: (0, n, 0))
    return pl.pallas_call(
        body, name="dn_scan_fwd", grid=(npair,),
        in_specs=[tok, tok, tok, tok, hm, pl.BlockSpec((2, DN_HEADS, 8, 128), lambda n: (n, 0, 0, 0)),
                  pl.BlockSpec((PAIR, DN_W), lambda n: (n, 3)), _full((1, 128))],
        out_specs=[tok, tok, tok, pl.BlockSpec((2, DN_HEADS, DN_HEAD_DIM, DN_HEAD_DIM), lambda n: (n, 0, 0, 0))],
        out_shape=[jax.ShapeDtypeStruct((t, DN_W), F32)] * 3
                  + [jax.ShapeDtypeStruct((2 * npair, DN_HEADS, DN_HEAD_DIM, DN_HEAD_DIM), F32)],
        scratch_shapes=[pltpu.VMEM((DN_HEADS, DN_HEAD_DIM, DN_HEAD_DIM), F32)],
        compiler_params=_params("arbitrary"),
    )(u, w, qg, kd, a_qk, dlast, pd, dn_w)


def _dn_scan_bwd(dout, o, vnew, sall, w, qg, kd, a_qk, dlast, pd, dn_w):
    t = o.shape[0]
    npair = t // PAIR
    rev = lambda n: npair - 1 - n

    def body(do_ref, o_ref, vn_ref, sall_ref, w_ref, qg_ref, kd_ref, a_ref, dl_ref, z_ref, nw_ref,
             dz_ref, du_ref, dw_ref, dqg_ref, dkd_ref, da_ref, ddl_ref, acc_ref, ds_ref, dos_ref):
        @pl.when(pl.program_id(0) == 0)
        def _():
            ds_ref[...] = jnp.zeros_like(ds_ref)
            acc_ref[...] = jnp.zeros_like(acc_ref)

        nw = nw_ref[...]
        dnw = jnp.zeros((1, 128), F32)
        for h in range(DN_HEADS):
            hs = slice(h * DN_HEAD_DIM, (h + 1) * DN_HEAD_DIM)
            o, z, dout = o_ref[:, hs], z_ref[:, hs], do_ref[:, hs]
            r = lax.rsqrt(jnp.mean(o * o, axis=-1, keepdims=True) + EPS)
            oh = o * r
            sz = _sigmoid(z)
            dz_ref[:, hs] = dout * (oh * nw) * (sz + z * sz * (1.0 - sz))
            don = dout * (z * sz)
            dnw += jnp.sum(don * oh, axis=0, keepdims=True)
            doh = don * nw
            dos_ref[:, hs] = r * (doh - oh * jnp.mean(doh * oh, axis=-1, keepdims=True))
        acc_ref[0:1, :] += dnw
        for c in (1, 0):
            rows = slice(c * DN_CHUNK, (c + 1) * DN_CHUNK)
            other = slice((1 - c) * DN_CHUNK, (2 - c) * DN_CHUNK)
            for h in range(DN_HEADS):
                hs = slice(h * DN_HEAD_DIM, (h + 1) * DN_HEAD_DIM)
                do, st, dsp, vn = dos_ref[rows, hs], sall_ref[c, h], ds_ref[h], vn_ref[rows, hs]
                da_ref[h, rows, rows] = _dot(do, vn, NT)
                da_ref[h, rows, other] = jnp.zeros((DN_CHUNK, DN_CHUNK), F32)
                du_ref[rows, hs] = _dot(a_ref[h, rows, rows], do, TN) + _dot(kd_ref[rows, hs], dsp)
                dqg_ref[rows, hs] = _dot(do, st, NT)
                dkd_ref[rows, hs] = _dot(vn, dsp, NT)
                ddl = jnp.sum(jnp.sum(dsp * st, axis=1, keepdims=True), axis=0, keepdims=True)
                ddl_ref[c, h] = jnp.broadcast_to(ddl, (8, 128))
            for h in range(DN_HEADS):
                hs = slice(h * DN_HEAD_DIM, (h + 1) * DN_HEAD_DIM)
                do, st, dvn = dos_ref[rows, hs], sall_ref[c, h], du_ref[rows, hs]
                dw_ref[rows, hs] = -_dot(dvn, st, NT)
                ds_ref[h] = (ds_ref[h] * dl_ref[c, h][0:1, :] + _dot(qg_ref[rows, hs], do, TN)
                             - _dot(w_ref[rows, hs], dvn, TN))

    tok = pl.BlockSpec((PAIR, DN_W), lambda n: (rev(n), 0))
    hm = pl.BlockSpec((DN_HEADS, PAIR, PAIR), lambda n: (0, rev(n), 0))
    sc = pl.BlockSpec((2, DN_HEADS, 8, 128), lambda n: (rev(n), 0, 0, 0))
    return pl.pallas_call(
        body, name="dn_scan_bwd", grid=(npair,),
        in_specs=[tok, tok, tok, pl.BlockSpec((2, DN_HEADS, DN_HEAD_DIM, DN_HEAD_DIM), lambda n: (rev(n), 0, 0, 0)),
                  tok, tok, tok, hm, sc, pl.BlockSpec((PAIR, DN_W), lambda n: (rev(n), 3)), _full((1, 128))],
        out_specs=[tok] * 5 + [hm, sc, _full((8, 128))],
        out_shape=[jax.ShapeDtypeStruct((t, DN_W), F32)] * 5 + [jax.ShapeDtypeStruct((DN_HEADS, t, PAIR), F32),
                   jax.ShapeDtypeStruct((2 * npair, DN_HEADS, 8, 128), F32), jax.ShapeDtypeStruct((8, 128), F32)],
        scratch_shapes=[pltpu.VMEM((DN_HEADS, DN_HEAD_DIM, DN_HEAD_DIM), F32), pltpu.VMEM((PAIR, DN_W), F32)],
        compiler_params=_params("arbitrary"),
    )(dout, o, vnew, sall, w, qg, kd, a_qk, dlast, pd, dn_w)


def _dn_intra_bwd(qn, kn, vc, gc, gr, tinv, a_qk, du, dw, dqg, dkd, da_qk, ddlast, dlast):
    t = qn.shape[0]
    npair = t // PAIR

    def body(q_ref, k_ref, v_ref, gc_ref, gr_ref, ti_ref, a_ref, du_ref, dw_ref, dqg_ref, dkd_ref, da_ref, ddl_ref, dl_ref,
             dq_ref, dk_ref, dv_ref, dg_ref):
        gc_v = gc_ref[...]
        low, strict = _pair_masks()
        cum_c, cum_r = _pair_cumsums(gc_v, gr_ref[...], low)
        lane = lax.broadcasted_iota(jnp.int32, (PAIR, 128), 1)
        rowi = lax.broadcasted_iota(jnp.int32, (PAIR, 1), 0)
        rsum = lambda v: jnp.sum(v, axis=-1, keepdims=True)
        dgam_all = jnp.zeros((PAIR, 128), F32)
        dbeta_all = jnp.zeros((PAIR, 128), F32)
        heads = [slice(h * DN_HEAD_DIM, (h + 1) * DN_HEAD_DIM) for h in range(DN_HEADS)]
        gates = [_pair_gates(gc_v, cum_c, cum_r, low, h) for h in range(DN_HEADS)]
        tsplits, dtis = [], []
        for h, (hs, (beta, gam, dm, gl)) in enumerate(zip(heads, gates)):
            k = k_ref[:, hs]
            tsplits.append(_split(ti_ref[h]))
            dtis.append(_dot(du_ref[:, hs], v_ref[:, hs] * beta, NT)
                        + _dot(dw_ref[:, hs], (k * beta) * jnp.exp(gam), NT))
        xs = [_dot_split(ts, _split(dti), TN) for ts, dti in zip(tsplits, dtis)]
        dls = [jnp.where(strict, -_dot_split(_split(x), ts, NT), 0.0) for x, ts in zip(xs, tsplits)]
        for h, (hs, (beta, gam, dm, gl), dl) in enumerate(zip(heads, gates, dls)):
            q, k, v = q_ref[:, hs], k_ref[:, hs], v_ref[:, hs]
            tinv, a = ti_ref[h], a_ref[h]
            du, dw, dqg, dkd = du_ref[:, hs], dw_ref[:, hs], dqg_ref[:, hs], dkd_ref[:, hs]
            kb = k * beta
            eg = jnp.exp(gam)
            ekd = jnp.exp(gl - gam)
            kbg = kb * eg
            lmat = jnp.where(strict, _dot(kb, k, NT) * dm, 0.0)
            dvb = _dot(tinv, du, TN)
            dkbg = _dot(tinv, dw, TN)
            dmm = dl * dm
            dam = jnp.where(low, da_ref[h], 0.0)
            dn = dam * dm
            e = dl * lmat + dam * a
            dkb = _dot(dmm, k) + dkbg * eg
            dk_ref[:, hs] = _dot(dmm, kb, TN) + _dot(dn, q, TN) + dkd * ekd + dkb * beta
            dq_ref[:, hs] = _dot(dn, k) + dqg * eg
            dv_ref[:, hs] = dvb * beta
            t_kd = rsum(dkd * (k * ekd))
            dgam = rsum(e) - rsum(e.T) + rsum(dqg * (q * eg)) + rsum(dkbg * kbg) - t_kd
            for c in range(2):
                rows = slice(c * DN_CHUNK, (c + 1) * DN_CHUNK)
                dgl = (jnp.sum(t_kd[rows, :], axis=0, keepdims=True)
                       + ddl_ref[c, h][0:1, 0:1] * dl_ref[c, h][0:1, 0:1])
                dgam = dgam + jnp.where(rowi == (c + 1) * DN_CHUNK - 1, dgl, 0.0)
            dgam_all += jnp.where(lane == DN_HEADS + h, dgam, 0.0)
            dbeta_all += jnp.where(lane == h, rsum(dkb * k) + rsum(dvb * v), 0.0)
        dg_ref[...] = dbeta_all + _dot(low.astype(F32), dgam_all, TN, HI)

    tok = lambda w: pl.BlockSpec((PAIR, w), lambda n: (n, 0))
    hm = pl.BlockSpec((DN_HEADS, PAIR, PAIR), lambda n: (0, n, 0))
    sc = pl.BlockSpec((2, DN_HEADS, 8, 128), lambda n: (n, 0, 0, 0))
    return pl.pallas_call(
        body, name="dn_intra_bwd", grid=(npair,),
        in_specs=[tok(DN_W), tok(DN_W), tok(DN_W), tok(128), pl.BlockSpec((8, PAIR), lambda n: (0, n)), hm, hm,
                  tok(DN_W), tok(DN_W), tok(DN_W), tok(DN_W), hm, sc, sc],
        out_specs=[tok(DN_W), tok(DN_W), tok(DN_W), tok(128)],
        out_shape=[jax.ShapeDtypeStruct((t, DN_W), F32)] * 3 + [jax.ShapeDtypeStruct((t, 128), F32)],
        compiler_params=_params("parallel"),
    )(qn, kn, vc, gc, gr, tinv, a_qk, du, dw, dqg, dkd, da_qk, ddlast, dlast)


def _dn_prep_bwd(pd, conv_w, al_vec, dt_vec, dqn, dkn, dvc, dgc, tm):
    t = pd.shape[0]

    def body(cur_ref, prev_ref, ba_ref, cw_ref, al_ref, dt_ref, dq_ref, dk_ref, dv_ref, dg_ref,
             dy_ref, dba_ref, accw_ref, accg_ref, xs_ref, dc_ref):
        @pl.when(pl.program_id(0) == 0)
        def _():
            accw_ref[...] = jnp.zeros_like(accw_ref)
            accg_ref[...] = jnp.zeros_like(accg_ref)

        _conv_window(cur_ref, prev_ref, xs_ref, tm)
        y = _conv_taps(xs_ref, cw_ref, tm)
        sg = _sigmoid(y)
        c = y * sg
        for h in range(DN_HEADS):
            qs = slice(h * DN_HEAD_DIM, (h + 1) * DN_HEAD_DIM)
            ksl = slice(DN_W + h * DN_HEAD_DIM, DN_W + (h + 1) * DN_HEAD_DIM)
            for src, sl, scale in ((dq_ref, qs, DN_SCALE), (dk_ref, ksl, 1.0)):
                xh = c[:, sl]
                r = lax.rsqrt(jnp.sum(xh * xh, axis=-1, keepdims=True) + EPS)
                unit = xh * r
                dn = src[:, qs] * scale
                dc_ref[:, sl] = r * (dn - unit * jnp.sum(dn * unit, axis=-1, keepdims=True))
        dc_ref[:, 2 * DN_W:3 * DN_W] = dv_ref[...]
        dy = dc_ref[...] * (sg + y * sg * (1.0 - sg))
        dy_ref[...] = dy
        for j in range(DN_CONV):
            accw_ref[j:j + 1, :] += jnp.sum(dy * xs_ref[pl.ds(HALO - DN_CONV + 1 + j, tm), :], axis=0, keepdims=True)

        beta, g, pre = _gate_values(ba_ref[...], al_ref[...], dt_ref[...])
        dgb = dg_ref[...]
        lane = lax.broadcasted_iota(jnp.int32, dgb.shape, 1)
        is_b, is_a = lane < DN_HEADS, (lane >= DN_HEADS) & (lane < 2 * DN_HEADS)
        dpre = dgb * (-jnp.exp(al_ref[...])) * _sigmoid(pre)
        dba_ref[...] = jnp.where(is_b, dgb * beta * (1.0 - beta), jnp.where(is_a, dpre, 0.0))
        accg_ref[0:1, :] += jnp.sum(jnp.where(is_a, dgb * g, 0.0), axis=0, keepdims=True)
        accg_ref[1:2, :] += jnp.sum(jnp.where(is_a, dpre, 0.0), axis=0, keepdims=True)

    tok = lambda w: pl.BlockSpec((tm, w), lambda i: (i, 0))
    return pl.pallas_call(
        body, name="dn_prep_bwd", grid=(t // tm,),
        in_specs=_dn_prep_specs(tm, t) + [tok(DN_W), tok(DN_W), tok(DN_W), tok(128)],
        out_specs=[tok(CONV_CH), tok(128), _full((8, CONV_CH)), _full((8, 128))],
        out_shape=[jax.ShapeDtypeStruct((t, CONV_CH), F32), jax.ShapeDtypeStruct((t, 128), F32),
                   jax.ShapeDtypeStruct((8, CONV_CH), F32), jax.ShapeDtypeStruct((8, 128), F32)],
        scratch_shapes=[pltpu.VMEM((HALO + tm, CONV_CH), F32), pltpu.VMEM((tm, CONV_CH), F32)],
        compiler_params=_params("arbitrary"),
    )(pd, pd, pd, conv_w, al_vec, dt_vec, dqn, dkn, dvc, dgc)


def _dn_conv_bwd(dy, dz, dba, conv_w, tm):
    t = dy.shape[0]
    nt = t // tm

    def body(cur_ref, nxt_ref, dz_ref, dba_ref, cw_ref, o_ref, ds_ref):
        nxt = jnp.where(pl.program_id(0) < nt - 1, nxt_ref[...], 0.0)
        ds_ref[0:tm, :] = cur_ref[...]
        ds_ref[tm:tm + HALO, :] = nxt
        dx = cw_ref[0:1, :] * ds_ref[pl.ds(DN_CONV - 1, tm), :]
        for j in range(1, DN_CONV):
            dx += cw_ref[j:j + 1, :] * ds_ref[pl.ds(DN_CONV - 1 - j, tm), :]
        o_ref[:, 0:CONV_CH] = dx.astype(o_ref.dtype)
        o_ref[:, CONV_CH:CONV_CH + DN_W] = dz_ref[...].astype(o_ref.dtype)
        o_ref[:, CONV_CH + DN_W:DN_COLS] = dba_ref[...].astype(o_ref.dtype)

    tok = lambda w: pl.BlockSpec((tm, w), lambda i: (i, 0))
    return pl.pallas_call(
        body, name="dn_conv_bwd", grid=(nt,),
        in_specs=[tok(CONV_CH),
                  pl.BlockSpec((HALO, CONV_CH), lambda i: (jnp.minimum((i + 1) * (tm // HALO), t // HALO - 1), 0)),
                  tok(DN_W), tok(128), _full((DN_CONV, CONV_CH))],
        out_specs=tok(DN_COLS),
        out_shape=jax.ShapeDtypeStruct((t, DN_COLS), MXU_DTYPE),
        scratch_shapes=[pltpu.VMEM((tm + HALO, CONV_CH), F32)],
        compiler_params=_params("parallel"),
    )(dy, dy, dz, dba, conv_w)


def _pad_lanes(v, offset=0):
    return jnp.zeros((1, 128), F32).at[0, offset:offset + v.shape[0]].set(v.astype(F32))


def _local_step(x, p, tgt, sm, w):
    t = x.shape[0]
    tm = min(512, t // 2)
    tm_s = min(256, t // 2)

    w_in = w["w_in"]
    wa = w_in[:, :ATTN_Q + 2 * ATTN_KV]
    wd = jnp.pad(w_in[:, ATTN_Q + 2 * ATTN_KV:], ((0, 0), (0, DN_COLS - (D_IN - ATTN_Q - 2 * ATTN_KV))))
    wo_a, wo_d = w["w_o"][:ATTN_Q], w["w_o"][ATTN_Q:]
    w_proj = jnp.transpose(w["w_proj4"], (1, 0, 2)).reshape(PLE_DIM, D_MODEL)
    conv_w = w["conv_w"]
    al_vec, dt_vec = _pad_lanes(sm["a_log"], DN_HEADS), _pad_lanes(sm["dt_bias"], DN_HEADS)
    sinks_vec = _pad_lanes(sm["sinks"])
    dn_w = sm["dn_norm"].reshape(1, 128)
    row = lambda v: v.reshape(1, D_MODEL)
    cos, sin = _rope_tables(t)

    u, pa, pd = _inproj(x, row(sm["norm_mix"]), wa, wd, tm_s)
    ao = _attn_fwd(pa, cos, sin, sinks_vec)
    qn, kn, vc, gc, gr = _dn_prep(pd, conv_w, al_vec, dt_vec, tm_s)
    uu, ww, qg, kd, a_qk, tinv, dlast = _dn_intra(qn, kn, vc, gc, gr)
    dn_out, o, vnew, sall = _dn_scan_fwd(uu, ww, qg, kd, a_qk, dlast, pd, dn_w)
    h1 = _oproj(x, ao, dn_out, wo_a, wo_d, tm)
    m, a, h2 = _mlp_fwd(h1, row(sm["norm_mlp"]), w["w_up4"], w["w_down"], tm)
    dh2, dh2b, dgp, dpp, n3, pb, acc_ple = _ple_loss(h2, p, tgt, row(sm["norm_ple"]), row(sm["norm_final"]),
                                                     w["w_gate"], w_proj, tm_s)
    s, da, dh1, dh1b, acc_mlp = _mlp_bwd(dh2, dh2b, a, h1, row(sm["norm_mlp"]), w["w_up4"], w["w_down"], tm)
    dao, ddn = _oproj_bwd(dh1b, wo_a, wo_d, tm)
    dz, du, dw, dqg, dkd, da_qk, ddlast, acc_dn = _dn_scan_bwd(ddn, o, vnew, sall, ww, qg, kd, a_qk, dlast, pd, dn_w)
    dqn, dkn, dvc, dgc = _dn_intra_bwd(qn, kn, vc, gc, gr, tinv, a_qk, du, dw, dqg, dkd, da_qk, ddlast, dlast)
    dy, dba, acc_conv, acc_gate = _dn_prep_bwd(pd, conv_w, al_vec, dt_vec, dqn, dkn, dvc, dgc, tm_s)
    d_dn = _dn_conv_bwd(dy, dz, dba, conv_w, tm_s)
    dq, dk, dv, acc_attn = _attn_bwd(pa, cos, sin, sinks_vec, dao)
    wq, wk, wv = wa[:, :ATTN_Q], wa[:, ATTN_Q:ATTN_Q + ATTN_KV], wa[:, ATTN_Q + ATTN_KV:]
    dx, acc_mix = _inproj_bwd(x, dh1, row(sm["norm_mix"]), [dq, dk, dv, d_dn], [wq, wk, wv, wd], tm_s)

    aob, dnb = ao.astype(MXU_DTYPE), dn_out.astype(MXU_DTYPE)
    g_w_in = jnp.concatenate([
        _wgrad(u, dq, "wgrad_q", D_MODEL, ATTN_Q, tm), _wgrad(u, dk, "wgrad_k", D_MODEL, ATTN_KV, tm),
        _wgrad(u, dv, "wgrad_v", D_MODEL, ATTN_KV, tm),
        _wgrad(u, d_dn, "wgrad_dn", D_MODEL, DN_COLS, tm)[:, :D_IN - ATTN_Q - 2 * ATTN_KV]], axis=1)
    g_w_o = jnp.concatenate([_wgrad(aob, dh1b, "wgrad_oa", ATTN_Q, D_MODEL, tm),
                             _wgrad(dnb, dh1b, "wgrad_od", DN_W, D_MODEL, tm)], axis=0)
    g_w_up4 = jnp.stack([_wgrad(m, da, "wgrad_up%d" % k, D_MODEL, FF_BLOCK, tm, col0=k * FF_BLOCK, ncols=FF_BLOCK)
                         for k in range(FF_BLOCKS)])
    g_w_down = _wgrad(s, dh2b, "wgrad_down", FF_BLOCK, D_MODEL, tm)
    g_w_gate = _wgrad(n3, dgp, "wgrad_gate", D_MODEL, D_MODEL, tm)
    g_w_proj = _wgrad(pb, dpp, "wgrad_proj", PLE_DIM, D_MODEL, tm)
    grads = dict(w_in=g_w_in, w_o=g_w_o, w_up4=g_w_up4, w_down=g_w_down, w_gate=g_w_gate, w_proj=g_w_proj)
    sums = dict(loss=acc_ple[2, 0], norm_final=acc_ple[0], norm_ple=acc_ple[1], norm_mlp=acc_mlp[0], norm_mix=acc_mix[0],
                dn_norm=acc_dn[0], sinks=acc_attn[0, :ATTN_HEADS], a_log=acc_gate[0, DN_HEADS:2 * DN_HEADS],
                dt_bias=acc_gate[1, DN_HEADS:2 * DN_HEADS], conv_w=acc_conv[:DN_CONV])
    return sums, dx, grads


MESH = pl.DeviceIdType.MESH
ANY = pl.BlockSpec(memory_space=pl.ANY)
N_CHIPS = 4
N_DEV = 8


def _place():
    x, y, c = lax.axis_index("x"), lax.axis_index("y"), lax.axis_index("c")
    chips = [(1 - x, y), (x, 1 - y), (1 - x, 1 - y)]
    return x, y, c, chips


def _gather_weights(shards, conv_s):
    n = len(shards)
    per = 7

    def body(*refs):
        in_refs, conv_ref = refs[:n], refs[n]
        out_refs, conv_out = refs[n + 1:2 * n + 1], refs[2 * n + 1]
        send_sems, recv_sems = refs[2 * n + 2:]
        x, y, c, chips = _place()
        sibling = (x, y, 1 - c)

        def blk(a, px, py, pc):
            hr = in_refs[a].shape[0] // 2
            return out_refs[a].at[2 * px + py, pl.ds(pc * hr, hr), :]

        def mine(a):
            hr = in_refs[a].shape[0] // 2
            return in_refs[a].at[pl.ds(c * hr, hr), :]

        def rcopy(a, k, block, to, src=None):
            return pltpu.make_async_remote_copy(
                src_ref=blk(a, *block) if src is None else src, dst_ref=blk(a, *block),
                send_sem=send_sems.at[per * a + k], recv_sem=recv_sems.at[per * a + k],
                device_id=to, device_id_type=MESH)

        def whole(a, to):
            return pltpu.make_async_remote_copy(
                src_ref=in_refs[a], dst_ref=out_refs[a].at[2 * x + y],
                send_sem=send_sems.at[per * a], recv_sem=recv_sems.at[per * a], device_id=to, device_id_type=MESH)

        def ccopy(j, to):
            return pltpu.make_async_remote_copy(
                src_ref=conv_ref, dst_ref=conv_out.at[2 * x + y],
                send_sem=send_sems.at[per * n + j], recv_sem=recv_sems.at[per * n + j],
                device_id=to, device_id_type=MESH)

        started = []
        for a in range(n):
            first = [whole(a, sibling)]
            first += [rcopy(a, 1 + j, (x, y, c), (*chip, c), src=mine(a)) for j, chip in enumerate(chips)]
            for cp in first:
                cp.start()
            started += first
        conv_sends = [ccopy(j, (*chip, c)) for j, chip in enumerate(chips)] + [ccopy(3, sibling)]
        for cp in conv_sends:
            cp.start()
        started += conv_sends
        for a in range(n):
            for j, chip in enumerate(chips):
                rcopy(a, 1 + j, (*chip, c), (x, y, c)).wait_recv()
                fwd = rcopy(a, 4 + j, (*chip, c), sibling)
                fwd.start()
                started.append(fwd)
        for a in range(n):
            whole(a, sibling).wait_recv()
            for j, chip in enumerate(chips):
                rcopy(a, 4 + j, (*chip, 1 - c), (x, y, c)).wait_recv()
        for j, chip in enumerate(chips + [(x, y)]):
            pltpu.make_async_remote_copy(
                src_ref=conv_ref, dst_ref=conv_out.at[2 * chip[0] + chip[1]],
                send_sem=send_sems.at[per * n + j], recv_sem=recv_sems.at[per * n + j],
                device_id=sibling, device_id_type=MESH).wait_recv()
        for cp in started:
            cp.wait_send()

    nsem = per * n + 4
    out_shape = [jax.ShapeDtypeStruct((N_CHIPS,) + s.shape, s.dtype) for s in shards]
    out_shape.append(jax.ShapeDtypeStruct((N_CHIPS,) + conv_s.shape, conv_s.dtype))
    return pl.pallas_call(
        body, name="gather_weights", in_specs=[ANY] * (n + 1), out_specs=[ANY] * (n + 1), out_shape=out_shape,
        scratch_shapes=[pltpu.SemaphoreType.DMA((nsem,)), pltpu.SemaphoreType.DMA((nsem,))],
    )(*shards, conv_s)


def _exchange_halves(grads):
    n = len(grads)

    def body(*refs):
        g_refs, got_refs = refs[:n], refs[n:2 * n]
        send_sems, recv_sems = refs[2 * n:]
        x, y, c, _ = _place()
        remote = []
        for a in range(n):
            hr = g_refs[a].shape[1] // 2
            remote.append(pltpu.make_async_remote_copy(
                src_ref=g_refs[a].at[:, pl.ds((1 - c) * hr, hr), :], dst_ref=got_refs[a],
                send_sem=send_sems.at[a], recv_sem=recv_sems.at[a], device_id=(x, y, 1 - c), device_id_type=MESH))
        for cp in remote:
            cp.start()
        for cp in remote:
            cp.wait_recv()
        for cp in remote:
            cp.wait_send()

    half = [jax.ShapeDtypeStruct((g.shape[0], g.shape[1] // 2, g.shape[2]), g.dtype) for g in grads]
    return pl.pallas_call(
        body, name="exchange_halves", in_specs=[ANY] * n, out_specs=[ANY] * n, out_shape=half,
        scratch_shapes=[pltpu.SemaphoreType.DMA((n,)), pltpu.SemaphoreType.DMA((n,))],
    )(*grads)


def _scatter_to_chips(sums16):
    n = len(sums16)

    def body(*refs):
        s16, got_refs = refs[:n], refs[n:2 * n]
        send_sems, recv_sems = refs[2 * n:]
        x, y, c, chips = _place()
        remote = []
        for a in range(n):
            for j, chip in enumerate(chips):
                remote.append(pltpu.make_async_remote_copy(
                    src_ref=s16[a].at[2 * chip[0] + chip[1]], dst_ref=got_refs[a].at[j],
                    send_sem=send_sems.at[3 * a + j], recv_sem=recv_sems.at[3 * a + j],
                    device_id=(*chip, c), device_id_type=MESH))
        for cp in remote:
            cp.start()
        for cp in remote:
            cp.wait_recv()
        for cp in remote:
            cp.wait_send()

    got = [jax.ShapeDtypeStruct((3,) + s.shape[1:], BF16) for s in sums16]
    return pl.pallas_call(
        body, name="scatter_to_chips", in_specs=[ANY] * n, out_specs=[ANY] * n, out_shape=got,
        scratch_shapes=[pltpu.SemaphoreType.DMA((3 * n,)), pltpu.SemaphoreType.DMA((3 * n,))],
    )(*sums16)


def _share_halves(bufs):
    n = len(bufs)

    def body(*refs):
        out_refs = refs[n:2 * n]
        send_sems, recv_sems = refs[2 * n:]
        x, y, c, _ = _place()
        remote = [pltpu.make_async_remote_copy(
            src_ref=out_refs[a].at[c], dst_ref=out_refs[a].at[c], send_sem=send_sems.at[a], recv_sem=recv_sems.at[a],
            device_id=(x, y, 1 - c), device_id_type=MESH) for a in range(n)]
        for cp in remote:
            cp.start()
        for a in range(n):
            pltpu.make_async_remote_copy(
                src_ref=out_refs[a].at[c], dst_ref=out_refs[a].at[1 - c], send_sem=send_sems.at[a],
                recv_sem=recv_sems.at[a], device_id=(x, y, 1 - c), device_id_type=MESH).wait_recv()
        for cp in remote:
            cp.wait_send()

    return pl.pallas_call(
        body, name="share_halves", in_specs=[ANY] * n, out_specs=[ANY] * n,
        out_shape=[jax.ShapeDtypeStruct(b.shape, b.dtype) for b in bufs],
        input_output_aliases={a: a for a in range(n)},
        scratch_shapes=[pltpu.SemaphoreType.DMA((n,)), pltpu.SemaphoreType.DMA((n,))],
    )(*bufs)


SMALL_ROWS, SMALL_COLS = 16, CONV_CH


def _allreduce_small(block):
    m_per, ncol = block.shape

    def body(x_ref, sum_ref, all_ref, send_sems, recv_sems, local_sem):
        x, y, c, chips = _place()
        me, sibling = (x, y, c), (x, y, 1 - c)

        def rows(px, py, pc):
            return all_ref.at[pl.ds((4 * px + 2 * py + pc) * m_per, m_per), :]

        def copy(k, block_of, to, src=None):
            return pltpu.make_async_remote_copy(
                src_ref=rows(*block_of) if src is None else src, dst_ref=rows(*block_of),
                send_sem=send_sems.at[k], recv_sem=recv_sems.at[k], device_id=to, device_id_type=MESH)

        mine = pltpu.make_async_copy(x_ref, rows(*me), local_sem)
        mine.start()
        first = [copy(0, me, sibling, src=x_ref)]
        first += [copy(1 + j, me, (*chip, c), src=x_ref) for j, chip in enumerate(chips)]
        for cp in first:
            cp.start()
        passed = [copy(4 + j, (*chip, c), sibling) for j, chip in enumerate(chips)]
        for j, chip in enumerate(chips):
            copy(1 + j, (*chip, c), me).wait_recv()
            passed[j].start()
        copy(0, sibling, me).wait_recv()
        for j, chip in enumerate(chips):
            copy(4 + j, (*chip, 1 - c), me).wait_recv()
        for cp in first + passed:
            cp.wait_send()
        mine.wait()
        total = all_ref[0:m_per, :]
        for d in range(1, N_DEV):
            total = total + all_ref[d * m_per:(d + 1) * m_per, :]
        sum_ref[...] = total

    vm = pl.BlockSpec(memory_space=pltpu.VMEM)
    return pl.pallas_call(
        body, name="allreduce_small", in_specs=[vm], out_specs=vm,
        out_shape=jax.ShapeDtypeStruct((m_per, ncol), F32),
        scratch_shapes=[pltpu.VMEM((N_DEV * m_per, ncol), F32), pltpu.SemaphoreType.DMA((7,)),
                        pltpu.SemaphoreType.DMA((7,)), pltpu.SemaphoreType.DMA],
    )(block)


def _row_tile(rows, cols):
    tile = rows
    while tile * cols * 4 > (1 << 20) and tile % 16 == 0:
        tile //= 2
    return tile


def _elementwise(fn, name, ins, out_dtypes):
    rows, cols = ins[0].shape
    tile = _row_tile(rows, cols)

    def body(*refs):
        outs = fn(*[r[...] for r in refs[:len(ins)]])
        for o_ref, o in zip(refs[len(ins):], outs):
            o_ref[...] = o.astype(o_ref.dtype)

    spec = pl.BlockSpec((tile, cols), lambda i: (i, 0))
    return pl.pallas_call(
        body, name=name, grid=(rows // tile,), in_specs=[spec] * len(ins), out_specs=[spec] * len(out_dtypes),
        out_shape=[jax.ShapeDtypeStruct((rows, cols), d) for d in out_dtypes],
        compiler_params=_params("parallel"),
    )(*ins)


def _adamw_tile(w, g, m, v):
    m = ADAM_B1 * m + (1.0 - ADAM_B1) * g
    v = ADAM_B2 * v + (1.0 - ADAM_B2) * jnp.square(g)
    m_hat = m / (1.0 - ADAM_B1 ** ADAM_STEP)
    v_hat = v / (1.0 - ADAM_B2 ** ADAM_STEP)
    delta = -ADAM_LR * (m_hat / (jnp.sqrt(v_hat) + ADAM_EPS) + ADAM_WD * w)
    return delta, m, v


def _adamw(name, w, g, m, v):
    return _elementwise(_adamw_tile, name, [w, g, m, v], [F32, F32, F32])


def _chip_sum(name, g4, got, place):
    nchip, hr, cols = got.shape
    tile = _row_tile(hr, cols)
    nblk = hr // tile

    def body(pl_ref, g_ref, o_ref, s32_ref, s16_ref):
        s = g_ref[...] + o_ref[...]
        s32_ref[...] = s
        s16_ref[...] = s.astype(BF16)

    spec = pl.BlockSpec((None, tile, cols), lambda k, i, pr: (k, i, 0))
    return pl.pallas_call(
        body, name=name,
        grid_spec=pltpu.PrefetchScalarGridSpec(
            num_scalar_prefetch=1, grid=(nchip, nblk),
            in_specs=[pl.BlockSpec((None, tile, cols), lambda k, i, pr: (k, pr[1] * nblk + i, 0)), spec],
            out_specs=[spec, spec]),
        out_shape=[jax.ShapeDtypeStruct(got.shape, F32), jax.ShapeDtypeStruct(got.shape, BF16)],
        compiler_params=_params("parallel", "parallel"),
    )(place, g4, got)


def _mesh_sum(name, s32, got, place):
    _, hr, cols = s32.shape
    tile = _row_tile(hr, cols)

    def body(pl_ref, own_ref, g0_ref, g1_ref, g2_ref, o_ref):
        o_ref[...] = ((own_ref[...] + g0_ref[...].astype(F32)) + g1_ref[...].astype(F32)) + g2_ref[...].astype(F32)

    slab = lambda j: pl.BlockSpec((None, tile, cols), lambda i, pr: (j, i, 0))
    return pl.pallas_call(
        body, name=name,
        grid_spec=pltpu.PrefetchScalarGridSpec(
            num_scalar_prefetch=1, grid=(hr // tile,),
            in_specs=[pl.BlockSpec((None, tile, cols), lambda i, pr: (pr[0], i, 0)), slab(0), slab(1), slab(2)],
            out_specs=pl.BlockSpec((None, tile, cols), lambda i, pr: (pr[1], i, 0))),
        out_shape=jax.ShapeDtypeStruct((2, hr, cols), F32),
        compiler_params=_params("parallel"),
    )(place, s32, got, got, got)


def _reduce_scatter(grads):
    names = list(grads)
    place = jnp.stack([2 * lax.axis_index("x") + lax.axis_index("y"), lax.axis_index("c")]).astype(jnp.int32)
    got_a = _exchange_halves([grads[k] for k in names])
    sums = [_chip_sum("chip_sum_" + k, grads[k], g, place) for k, g in zip(names, got_a)]
    got_b = _scatter_to_chips([s[1] for s in sums])
    bufs = [_mesh_sum("mesh_sum_" + k, s[0], g, place) for k, s, g in zip(names, sums, got_b)]
    full = _share_halves(bufs)
    return {k: f.reshape(-1, f.shape[-1]) for k, f in zip(names, full)}


def kernel(x, p, norm_mix, w_in, conv_w, a_log, dt_bias, dn_norm, sinks, w_o, norm_mlp, w_up, w_down, norm_ple, w_ple_gate, w_ple_proj, norm_final, loss_target, m_norm_mix, m_w_in, m_conv_w, m_a_log, m_dt_bias, m_dn_norm, m_sinks, m_w_o, m_norm_mlp, m_w_up, m_w_down, m_norm_ple, m_w_ple_gate, m_w_ple_proj, m_norm_final, v_norm_mix, v_w_in, v_conv_w, v_a_log, v_dt_bias, v_dn_norm, v_sinks, v_w_o, v_norm_mlp, v_w_up, v_w_down, v_norm_ple, v_w_ple_gate, v_w_ple_proj, v_norm_final):
    chip = 2 * lax.axis_index("x") + lax.axis_index("y")
    big = dict(w_in=w_in[0], w_o=w_o[0], w_up=w_up[0], w_down=w_down[0], w_gate=w_ple_gate[0], w_proj=w_ple_proj[0])
    big_m = dict(w_in=m_w_in[0], w_o=m_w_o[0], w_up=m_w_up[0], w_down=m_w_down[0], w_gate=m_w_ple_gate[0], w_proj=m_w_ple_proj[0])
    big_v = dict(w_in=v_w_in[0], w_o=v_w_o[0], w_up=v_w_up[0], w_down=v_w_down[0], w_gate=v_w_ple_gate[0], w_proj=v_w_ple_proj[0])
    names = list(big)

    gathered = _gather_weights([big[k].astype(BF16) for k in names], conv_w[0])
    gw = dict(zip(names, gathered[:-1]))
    conv_full = jnp.transpose(gathered[-1], (1, 0, 2)).reshape(DN_CONV, CONV_CH)
    w = dict(w_in=jnp.transpose(gw["w_in"], (1, 0, 2)).reshape(D_MODEL, D_IN),
             w_o=gw["w_o"].reshape(D_MODEL, D_MODEL), w_up4=gw["w_up"], w_down=gw["w_down"].reshape(D_FF, D_MODEL),
             w_gate=gw["w_gate"].reshape(D_MODEL, D_MODEL), w_proj4=gw["w_proj"], conv_w=conv_full)
    sm = dict(norm_mix=norm_mix[0], a_log=a_log[0], dt_bias=dt_bias[0], dn_norm=dn_norm[0], sinks=sinks[0],
              norm_mlp=norm_mlp[0], norm_ple=norm_ple[0], norm_final=norm_final)

    sums, grad_x, g = _local_step(x[0], p[0, 0], loss_target[0], sm, w)

    per_chip = dict(
        w_in=jnp.transpose(g["w_in"].reshape(D_MODEL, N_CHIPS, D_IN // N_CHIPS), (1, 0, 2)),
        w_o=g["w_o"].reshape(N_CHIPS, D_MODEL // N_CHIPS, D_MODEL),
        w_up=g["w_up4"],
        w_down=g["w_down"].reshape(N_CHIPS, D_FF // N_CHIPS, D_MODEL),
        w_gate=g["w_gate"].reshape(N_CHIPS, D_MODEL // N_CHIPS, D_MODEL),
        w_proj=jnp.transpose(g["w_proj"].reshape(PLE_DIM, N_CHIPS, D_MODEL // N_CHIPS), (1, 0, 2)))
    red = _reduce_scatter(per_chip)

    row = lambda v: jnp.zeros((SMALL_COLS,), F32).at[:v.shape[0]].set(v)
    misc = jnp.zeros((SMALL_COLS,), F32).at[0:4].set(sums["a_log"]).at[4:8].set(sums["dt_bias"]) \
        .at[8:16].set(sums["sinks"]).at[128:256].set(sums["dn_norm"]).at[256].set(sums["loss"])
    small = jnp.concatenate([sums["conv_w"], jnp.stack([row(sums["norm_mix"]), row(sums["norm_mlp"]), row(sums["norm_ple"]),
                                                        row(sums["norm_final"]), misc]),
                             jnp.zeros((SMALL_ROWS - 9, SMALL_COLS), F32)], axis=0)
    tot = _allreduce_small(small)
    loss = tot[8, 256]
    ncw = CONV_CH // N_CHIPS

    def pack(cw, nmix, nmlp, nple, nfin, al, dtb, sk, dnn):
        misc_p = jnp.zeros((SMALL_COLS,), F32).at[0:4].set(al).at[4:8].set(dtb).at[8:16].set(sk).at[128:256].set(dnn)
        cw_p = jnp.zeros((DN_CONV, SMALL_COLS), F32).at[:, :ncw].set(cw)
        return jnp.concatenate([cw_p, jnp.stack([row(nmix), row(nmlp), row(nple), row(nfin), misc_p]),
                                jnp.zeros((SMALL_ROWS - 9, SMALL_COLS), F32)], axis=0)

    def unpack(buf):
        return dict(conv_w=buf[0:4, :ncw][None], norm_mix=buf[4, :D_MODEL][None], norm_mlp=buf[5, :D_MODEL][None],
                    norm_ple=buf[6, :D_MODEL][None], norm_final=buf[7, :D_MODEL], a_log=buf[8, 0:4][None],
                    dt_bias=buf[8, 4:8][None], sinks=buf[8, 8:16][None], dn_norm=buf[8, 128:256][None])

    g_conv_shard = lax.dynamic_slice(tot[0:4], (0, chip * ncw), (DN_CONV, ncw))
    g_small = pack(g_conv_shard, tot[4, :D_MODEL], tot[5, :D_MODEL], tot[6, :D_MODEL], tot[7, :D_MODEL],
                   tot[8, 0:4], tot[8, 4:8], tot[8, 8:16], tot[8, 128:256])
    w_small = pack(conv_w[0], norm_mix[0], norm_mlp[0], norm_ple[0], norm_final, a_log[0], dt_bias[0], sinks[0], dn_norm[0])
    m_small = pack(m_conv_w[0], m_norm_mix[0], m_norm_mlp[0], m_norm_ple[0], m_norm_final, m_a_log[0], m_dt_bias[0],
                   m_sinks[0], m_dn_norm[0])
    v_small = pack(v_conv_w[0], v_norm_mix[0], v_norm_mlp[0], v_norm_ple[0], v_norm_final, v_a_log[0], v_dt_bias[0],
                   v_sinks[0], v_dn_norm[0])

    d_s, m_s, v_s = (unpack(b) for b in _adamw("adamw_small", w_small, g_small, m_small, v_small))
    g_s = unpack(g_small)
    out_g, out_d, out_m, out_v = dict(g_s), dict(d_s), dict(m_s), dict(v_s)
    ref_name = dict(w_in="w_in", w_o="w_o", w_up="w_up", w_down="w_down", w_gate="w_ple_gate", w_proj="w_ple_proj")
    for k in names:
        d_k, m_k, v_k = _adamw("adamw_" + k, big[k], red[k], big_m[k], big_v[k])
        out_g[ref_name[k]], out_d[ref_name[k]] = red[k][None], d_k[None]
        out_m[ref_name[k]], out_v[ref_name[k]] = m_k[None], v_k[None]
    order = ["norm_mix", "w_in", "conv_w", "a_log", "dt_bias", "dn_norm", "sinks", "w_o", "norm_mlp", "w_up", "w_down",
             "norm_ple", "w_ple_gate", "w_ple_proj", "norm_final"]
    return (loss, grad_x[None], *[out_g[k] for k in order], *[out_d[k] for k in order],
            *[out_m[k] for k in order], *[out_v[k] for k in order])
```

```python
import functools

import jax
import jax.numpy as jnp
from jax import lax
from jax.experimental import pallas as pl
from jax.experimental.pallas import tpu as pltpu

F32 = jnp.float32
BF16 = jnp.bfloat16
MXU_DTYPE = jnp.bfloat16
HI = lax.Precision.HIGHEST

D_MODEL = 1024
PLE_DIM = 256
ATTN_HEADS = 8
ATTN_KV_HEADS = 2
ATTN_GROUPS = ATTN_HEADS // ATTN_KV_HEADS
ATTN_HEAD_DIM = 64
ATTN_BLOCK = 128
ROPE_THETA = 10000.0
DN_HEADS = 4
DN_HEAD_DIM = 128
DN_CONV = 4
DN_CHUNK = 64
D_FF = 4 * D_MODEL
EPS = 1e-6
ATTN_Q = ATTN_HEADS * ATTN_HEAD_DIM
ATTN_KV = ATTN_KV_HEADS * ATTN_HEAD_DIM
DN_W = DN_HEADS * DN_HEAD_DIM
CONV_CH = 3 * DN_W
D_IN = ATTN_Q + 2 * ATTN_KV + 4 * DN_W + 2 * DN_HEADS
DN_COLS = 4 * DN_W + 128
DN_SCALE = DN_HEAD_DIM ** -0.5
ATTN_SCALE = ATTN_HEAD_DIM ** -0.5
FF_BLOCKS = 4
FF_BLOCK = D_FF // FF_BLOCKS

ADAM_LR = 0.001
ADAM_B1 = 0.9
ADAM_B2 = 0.999
ADAM_EPS = 1e-08
ADAM_WD = 0.01
ADAM_STEP = 10

V7X_VMEM_BYTES = 64 * 1024 * 1024
VMEM_LIMIT = 48 * 1024 * 1024

NN = ((1,), (0,))
NT = ((1,), (1,))
TN = ((0,), (0,))


def _dot(a, b, dims=NN, prec=None):
    return lax.dot_general(a, b, (dims, ((), ())), precision=prec, preferred_element_type=F32)


def _sigmoid(x):
    return 1.0 / (1.0 + jnp.exp(-x))


def _softplus(x):
    return jnp.maximum(x, 0.0) + jnp.log(1.0 + jnp.exp(-jnp.abs(x)))


def _params(*sem):
    return pltpu.CompilerParams(dimension_semantics=sem, vmem_limit_bytes=VMEM_LIMIT)


def _rms_fwd(xv, g):
    r = lax.rsqrt(jnp.mean(xv * xv, axis=-1, keepdims=True) + EPS)
    return xv * r * g


def _rms_bwd(xv, g, dn):
    r = lax.rsqrt(jnp.mean(xv * xv, axis=-1, keepdims=True) + EPS)
    xh = xv * r
    dg = jnp.sum(dn * xh, axis=0, keepdims=True)
    dxh = dn * g
    dx = r * (dxh - xh * jnp.mean(dxh * xh, axis=-1, keepdims=True))
    return dx, dg


def _full(shape):
    return pl.BlockSpec(shape, lambda *_: (0,) * len(shape))


def _inproj(x, g_mix, wa, wd, tm):
    t = x.shape[0]

    def body(x_ref, g_ref, wa_ref, wd_ref, u_ref, pa_ref, pd_ref):
        u = _rms_fwd(x_ref[...], g_ref[...]).astype(MXU_DTYPE)
        u_ref[...] = u
        pa_ref[...] = _dot(u, wa_ref[...])
        pd_ref[...] = _dot(u, wd_ref[...])

    na, nd = wa.shape[1], wd.shape[1]
    return pl.pallas_call(
        body, name="inproj", grid=(t // tm,),
        in_specs=[pl.BlockSpec((tm, D_MODEL), lambda i: (i, 0)), _full((1, D_MODEL)),
                  _full((D_MODEL, na)), _full((D_MODEL, nd))],
        out_specs=[pl.BlockSpec((tm, D_MODEL), lambda i: (i, 0)), pl.BlockSpec((tm, na), lambda i: (i, 0)),
                   pl.BlockSpec((tm, nd), lambda i: (i, 0))],
        out_shape=[jax.ShapeDtypeStruct((t, D_MODEL), MXU_DTYPE), jax.ShapeDtypeStruct((t, na), F32),
                   jax.ShapeDtypeStruct((t, nd), F32)],
        compiler_params=_params("parallel"),
    )(x, g_mix, wa, wd)


def _oproj(x, ao, dn, wo_a, wo_d, tm):
    t = x.shape[0]

    def body(x_ref, ao_ref, dn_ref, wa_ref, wd_ref, h_ref):
        h_ref[...] = (x_ref[...] + _dot(ao_ref[...].astype(MXU_DTYPE), wa_ref[...])
                      + _dot(dn_ref[...].astype(MXU_DTYPE), wd_ref[...]))

    half = ao.shape[1]
    return pl.pallas_call(
        body, name="oproj", grid=(t // tm,),
        in_specs=[pl.BlockSpec((tm, D_MODEL), lambda i: (i, 0)), pl.BlockSpec((tm, half), lambda i: (i, 0)),
                  pl.BlockSpec((tm, half), lambda i: (i, 0)), _full((half, D_MODEL)), _full((half, D_MODEL))],
        out_specs=pl.BlockSpec((tm, D_MODEL), lambda i: (i, 0)),
        out_shape=jax.ShapeDtypeStruct((t, D_MODEL), F32),
        compiler_params=_params("parallel"),
    )(x, ao, dn, wo_a, wo_d)


def _mlp_fwd(h1, g_mlp, w_up4, w_down, tm):
    t = h1.shape[0]

    def body(h_ref, g_ref, wu_ref, wd_ref, m_ref, a_ref, h2_ref, acc_ref):
        k = pl.program_id(1)

        @pl.when(k == 0)
        def _():
            m_ref[...] = _rms_fwd(h_ref[...], g_ref[...]).astype(MXU_DTYPE)
            acc_ref[...] = jnp.zeros_like(acc_ref)

        a = _dot(m_ref[...], wu_ref[...])
        a_ref[...] = a
        s = jnp.square(jnp.maximum(a, 0.0)).astype(MXU_DTYPE)
        acc_ref[...] += _dot(s, wd_ref[...])

        @pl.when(k == FF_BLOCKS - 1)
        def _():
            h2_ref[...] = h_ref[...] + acc_ref[...]

    return pl.pallas_call(
        body, name="mlp_fwd", grid=(t // tm, FF_BLOCKS),
        in_specs=[pl.BlockSpec((tm, D_MODEL), lambda i, k: (i, 0)), _full((1, D_MODEL)),
                  pl.BlockSpec((None, D_MODEL, FF_BLOCK), lambda i, k: (k, 0, 0)),
                  pl.BlockSpec((FF_BLOCK, D_MODEL), lambda i, k: (k, 0))],
        out_specs=[pl.BlockSpec((tm, D_MODEL), lambda i, k: (i, 0)), pl.BlockSpec((tm, FF_BLOCK), lambda i, k: (i, k)),
                   pl.BlockSpec((tm, D_MODEL), lambda i, k: (i, 0))],
        out_shape=[jax.ShapeDtypeStruct((t, D_MODEL), MXU_DTYPE), jax.ShapeDtypeStruct((t, D_FF), F32),
                   jax.ShapeDtypeStruct((t, D_MODEL), F32)],
        scratch_shapes=[pltpu.VMEM((tm, D_MODEL), F32)],
        compiler_params=_params("parallel", "arbitrary"),
    )(h1, g_mlp, w_up4, w_down)


def _ple_loss(h2, p, tgt, g_ple, g_fin, w_gate, w_proj, tm):
    t = h2.shape[0]

    def body(h_ref, p_ref, t_ref, gp_ref, gf_ref, wg_ref, wp_ref,
             dh_ref, dhb_ref, dgp_ref, dpp_ref, n3_ref, pb_ref, acc_ref):
        @pl.when(pl.program_id(0) == 0)
        def _():
            acc_ref[...] = jnp.zeros_like(acc_ref)

        h = h_ref[...]
        g_ple_v, g_fin_v = gp_ref[...], gf_ref[...]
        n3 = _rms_fwd(h, g_ple_v).astype(MXU_DTYPE)
        n3_ref[...] = n3
        gate = _sigmoid(_dot(n3, wg_ref[...]))
        pb = p_ref[...].astype(MXU_DTYPE)
        pb_ref[...] = pb
        pp = _dot(pb, wp_ref[...])
        h3 = h + gate * pp
        r4 = lax.rsqrt(jnp.mean(h3 * h3, axis=-1, keepdims=True) + EPS)
        xh4 = h3 * r4
        e = xh4 * g_fin_v - t_ref[...]
        loss = 0.5 * jnp.sum(jnp.mean(e * e, axis=-1, keepdims=True), axis=0, keepdims=True)
        dy = e * (1.0 / D_MODEL)
        dg_fin = jnp.sum(dy * xh4, axis=0, keepdims=True)
        dxh = dy * g_fin_v
        dh3 = r4 * (dxh - xh4 * jnp.mean(dxh * xh4, axis=-1, keepdims=True))
        dpp_ref[...] = (dh3 * gate).astype(MXU_DTYPE)
        dgp = (dh3 * pp * gate * (1.0 - gate)).astype(MXU_DTYPE)
        dgp_ref[...] = dgp
        dn3 = _dot(dgp, wg_ref[...], NT)
        dx, dg_ple = _rms_bwd(h, g_ple_v, dn3)
        dh2 = dh3 + dx
        dh_ref[...] = dh2
        dhb_ref[...] = dh2.astype(MXU_DTYPE)
        acc_ref[0:1, :] += dg_fin
        acc_ref[1:2, :] += dg_ple
        acc_ref[2:3, :] += jnp.broadcast_to(loss, (1, D_MODEL))

    row = lambda w: pl.BlockSpec((tm, w), lambda i: (i, 0))
    return pl.pallas_call(
        body, name="ple_loss", grid=(t // tm,),
        in_specs=[row(D_MODEL), row(PLE_DIM), row(D_MODEL), _full((1, D_MODEL)), _full((1, D_MODEL)),
                  _full((D_MODEL, D_MODEL)), _full((PLE_DIM, D_MODEL))],
        out_specs=[row(D_MODEL), row(D_MODEL), row(D_MODEL), row(D_MODEL), row(D_MODEL), row(PLE_DIM),
                   _full((8, D_MODEL))],
        out_shape=[jax.ShapeDtypeStruct((t, D_MODEL), F32), jax.ShapeDtypeStruct((t, D_MODEL), MXU_DTYPE),
                   jax.ShapeDtypeStruct((t, D_MODEL), MXU_DTYPE), jax.ShapeDtypeStruct((t, D_MODEL), MXU_DTYPE),
                   jax.ShapeDtypeStruct((t, D_MODEL), MXU_DTYPE), jax.ShapeDtypeStruct((t, PLE_DIM), MXU_DTYPE),
                   jax.ShapeDtypeStruct((8, D_MODEL), F32)],
        compiler_params=_params("arbitrary"),
    )(h2, p, tgt, g_ple, g_fin, w_gate, w_proj)


def _mlp_bwd(dh2, dh2b, a, h1, g_mlp, w_up4, w_down, tm):
    t = h1.shape[0]

    def body(dh_ref, dhb_ref, a_ref, h_ref, g_ref, wu_ref, wd_ref,
             s_ref, da_ref, dh1_ref, dh1b_ref, acc_ref, dm_ref):
        i, k = pl.program_id(0), pl.program_id(1)

        @pl.when((i == 0) & (k == 0))
        def _():
            acc_ref[...] = jnp.zeros_like(acc_ref)

        @pl.when(k == 0)
        def _():
            dm_ref[...] = jnp.zeros_like(dm_ref)

        ds = _dot(dhb_ref[...], wd_ref[...], NT)
        r = jnp.maximum(a_ref[...], 0.0)
        s_ref[...] = (r * r).astype(MXU_DTYPE)
        da = (ds * (2.0 * r)).astype(MXU_DTYPE)
        da_ref[...] = da
        dm_ref[...] += _dot(da, wu_ref[...], NT)

        @pl.when(k == FF_BLOCKS - 1)
        def _():
            dx, dg = _rms_bwd(h_ref[...], g_ref[...], dm_ref[...])
            dh1 = dh_ref[...] + dx
            dh1_ref[...] = dh1
            dh1b_ref[...] = dh1.astype(MXU_DTYPE)
            acc_ref[0:1, :] += dg

    tok = lambda w: pl.BlockSpec((tm, w), lambda i, k: (i, 0))
    return pl.pallas_call(
        body, name="mlp_bwd", grid=(t // tm, FF_BLOCKS),
        in_specs=[tok(D_MODEL), tok(D_MODEL), pl.BlockSpec((tm, FF_BLOCK), lambda i, k: (i, k)), tok(D_MODEL),
                  _full((1, D_MODEL)), pl.BlockSpec((None, D_MODEL, FF_BLOCK), lambda i, k: (k, 0, 0)),
                  pl.BlockSpec((FF_BLOCK, D_MODEL), lambda i, k: (k, 0))],
        out_specs=[pl.BlockSpec((tm, FF_BLOCK), lambda i, k: (i, k)), pl.BlockSpec((tm, FF_BLOCK), lambda i, k: (i, k)),
                   tok(D_MODEL), tok(D_MODEL), pl.BlockSpec((8, D_MODEL), lambda i, k: (0, 0))],
        out_shape=[jax.ShapeDtypeStruct((t, D_FF), MXU_DTYPE), jax.ShapeDtypeStruct((t, D_FF), MXU_DTYPE),
                   jax.ShapeDtypeStruct((t, D_MODEL), F32), jax.ShapeDtypeStruct((t, D_MODEL), MXU_DTYPE),
                   jax.ShapeDtypeStruct((8, D_MODEL), F32)],
        scratch_shapes=[pltpu.VMEM((tm, D_MODEL), F32)],
        compiler_params=_params("arbitrary", "arbitrary"),
    )(dh2, dh2b, a, h1, g_mlp, w_up4, w_down)


def _oproj_bwd(dh1b, wo_a, wo_d, tm):
    t = dh1b.shape[0]
    half = wo_a.shape[0]

    def body(d_ref, wa_ref, wd_ref, da_ref, dd_ref):
        d = d_ref[...]
        da_ref[...] = _dot(d, wa_ref[...], NT)
        dd_ref[...] = _dot(d, wd_ref[...], NT)

    return pl.pallas_call(
        body, name="oproj_bwd", grid=(t // tm,),
        in_specs=[pl.BlockSpec((tm, D_MODEL), lambda i: (i, 0)), _full((half, D_MODEL)), _full((half, D_MODEL))],
        out_specs=[pl.BlockSpec((tm, half), lambda i: (i, 0)), pl.BlockSpec((tm, half), lambda i: (i, 0))],
        out_shape=[jax.ShapeDtypeStruct((t, half), F32), jax.ShapeDtypeStruct((t, half), F32)],
        compiler_params=_params("parallel"),
    )(dh1b, wo_a, wo_d)


def _inproj_bwd(x, dh1, g_mix, grads, weights, tm):
    t = x.shape[0]
    n = len(grads)

    def body(*refs):
        x_ref, dh_ref, g_ref = refs[:3]
        g_refs, w_refs = refs[3:3 + n], refs[3 + n:3 + 2 * n]
        dx_ref, acc_ref = refs[3 + 2 * n:]

        @pl.when(pl.program_id(0) == 0)
        def _():
            acc_ref[...] = jnp.zeros_like(acc_ref)

        du = _dot(g_refs[0][...], w_refs[0][...], NT)
        for j in range(1, n):
            du += _dot(g_refs[j][...], w_refs[j][...], NT)
        dx, dg = _rms_bwd(x_ref[...], g_ref[...], du)
        dx_ref[...] = dh_ref[...] + dx
        acc_ref[0:1, :] += dg

    tok = lambda w: pl.BlockSpec((tm, w), lambda i: (i, 0))
    return pl.pallas_call(
        body, name="inproj_bwd", grid=(t // tm,),
        in_specs=[tok(D_MODEL), tok(D_MODEL), _full((1, D_MODEL))] + [tok(g.shape[1]) for g in grads]
                 + [_full(w.shape) for w in weights],
        out_specs=[tok(D_MODEL), _full((8, D_MODEL))],
        out_shape=[jax.ShapeDtypeStruct((t, D_MODEL), F32), jax.ShapeDtypeStruct((8, D_MODEL), F32)],
        compiler_params=_params("arbitrary"),
    )(x, dh1, g_mix, *grads, *weights)


def _wgrad(a, b, name, tk, tn, tt, stacked=False):
    t, kdim = a.shape
    ncols = b.shape[1]

    def body(a_ref, b_ref, o_ref):
        @pl.when(pl.program_id(2) == 0)
        def _():
            o_ref[...] = jnp.zeros_like(o_ref)

        o_ref[...] += _dot(a_ref[...], b_ref[...], TN)

    if stacked:
        out_spec = pl.BlockSpec((None, tk, tn), lambda i, j, s: (j, i, 0))
        out_shape = jax.ShapeDtypeStruct((ncols // tn, kdim, tn), F32)
    else:
        out_spec = pl.BlockSpec((tk, tn), lambda i, j, s: (i, j))
        out_shape = jax.ShapeDtypeStruct((kdim, ncols), F32)
    return pl.pallas_call(
        body, name=name, grid=(kdim // tk, ncols // tn, t // tt),
        in_specs=[pl.BlockSpec((tt, tk), lambda i, j, s: (s, i)), pl.BlockSpec((tt, tn), lambda i, j, s: (s, j))],
        out_specs=out_spec, out_shape=out_shape,
        compiler_params=_params("parallel", "parallel", "arbitrary"),
    )(a, b)


def _rope_tables(t):
    half = ATTN_HEAD_DIM // 2
    inv = 1.0 / (ROPE_THETA ** (jnp.arange(half, dtype=F32) * (2.0 / ATTN_HEAD_DIM)))
    ang = jnp.arange(t, dtype=F32)[:, None] * inv[None, :]
    cos, sin = jnp.cos(ang), jnp.sin(ang)
    cos2 = jnp.concatenate([cos, cos], axis=-1)
    sin2 = jnp.concatenate([-sin, sin], axis=-1)
    return jnp.tile(cos2, (1, 2)), jnp.tile(sin2, (1, 2))


def _swap_halves(tv):
    w = tv.shape[-1]
    lane = lax.broadcasted_iota(jnp.int32, tv.shape, tv.ndim - 1)
    first = (lane % ATTN_HEAD_DIM) < (ATTN_HEAD_DIM // 2)
    return jnp.where(first, pltpu.roll(tv, w - ATTN_HEAD_DIM // 2, tv.ndim - 1),
                     pltpu.roll(tv, ATTN_HEAD_DIM // 2, tv.ndim - 1))


def _rope(tv, cos, sin):
    return tv * cos + _swap_halves(tv) * sin


def _rope_bwd(dv, cos, sin):
    return dv * cos + _swap_halves(dv * sin)


def _attn_probs(qh, kwin, sink, first_block):
    s = _dot(qh, kwin, NT) * ATTN_SCALE
    r = lax.broadcasted_iota(jnp.int32, s.shape, 0)
    c = lax.broadcasted_iota(jnp.int32, s.shape, 1)
    valid = (c > r) & (c <= r + ATTN_BLOCK) & ((c >= ATTN_BLOCK) | jnp.logical_not(first_block))
    s = jnp.where(valid, s, -jnp.inf)
    m = jnp.maximum(jnp.max(s, axis=-1, keepdims=True), sink)
    e = jnp.where(valid, jnp.exp(s - m), 0.0)
    es = jnp.exp(sink - m)
    inv = 1.0 / (jnp.sum(e, axis=-1, keepdims=True) + es)
    return e * inv, es * inv


def _lane_scalar(vec, idx):
    lane = lax.broadcasted_iota(jnp.int32, vec.shape, 1)
    return jnp.sum(jnp.where(lane == idx, vec, 0.0), axis=-1, keepdims=True)


def _attn_specs(nb):
    cur = lambda w, cb: pl.BlockSpec((ATTN_BLOCK, w), lambda i: (jnp.minimum(i, nb - 1), cb))
    prev = lambda w, cb: pl.BlockSpec((ATTN_BLOCK, w), lambda i: (jnp.maximum(jnp.minimum(i, nb - 1) - 1, 0), cb))
    kcol, vcol = ATTN_Q // ATTN_KV, ATTN_Q // ATTN_KV + 1
    return [cur(ATTN_Q, 0), cur(ATTN_KV, kcol), prev(ATTN_KV, kcol), cur(ATTN_KV, vcol), prev(ATTN_KV, vcol),
            cur(ATTN_KV, 0), cur(ATTN_KV, 0), prev(ATTN_KV, 0), prev(ATTN_KV, 0), _full((1, 128))]


def _attn_fwd(pa, cos, sin, sinks_vec):
    t = pa.shape[0]
    nb = t // ATTN_BLOCK

    def body(q_ref, kc_ref, kp_ref, vc_ref, vp_ref, cc_ref, sc_ref, cp_ref, sp_ref, sk_ref, o_ref):
        first = pl.program_id(0) == 0
        cc, sc = cc_ref[...], sc_ref[...]
        q = _rope(q_ref[...], jnp.tile(cc, (1, ATTN_Q // ATTN_KV)), jnp.tile(sc, (1, ATTN_Q // ATTN_KV)))
        kc = _rope(kc_ref[...], cc, sc)
        kp = _rope(kp_ref[...], cp_ref[...], sp_ref[...])
        vc, vp = vc_ref[...], vp_ref[...]
        sk = sk_ref[...]
        for hk in range(ATTN_KV_HEADS):
            ks = slice(hk * ATTN_HEAD_DIM, (hk + 1) * ATTN_HEAD_DIM)
            kwin = jnp.concatenate([kp[:, ks], kc[:, ks]], axis=0)
            vwin = jnp.concatenate([vp[:, ks], vc[:, ks]], axis=0)
            for g in range(ATTN_GROUPS):
                h = hk * ATTN_GROUPS + g
                hs = slice(h * ATTN_HEAD_DIM, (h + 1) * ATTN_HEAD_DIM)
                probs, _ = _attn_probs(q[:, hs], kwin, _lane_scalar(sk, h), first)
                o_ref[:, hs] = _dot(probs, vwin)

    return pl.pallas_call(
        body, name="attn_fwd", grid=(nb,),
        in_specs=_attn_specs(nb),
        out_specs=pl.BlockSpec((ATTN_BLOCK, ATTN_Q), lambda i: (i, 0)),
        out_shape=jax.ShapeDtypeStruct((t, ATTN_Q), F32),
        compiler_params=_params("parallel"),
    )(pa, pa, pa, pa, pa, cos, sin, cos, sin, sinks_vec)


def _attn_bwd(pa, cos, sin, sinks_vec, dao):
    t = pa.shape[0]
    nb = t // ATTN_BLOCK

    def body(q_ref, kc_ref, kp_ref, vc_ref, vp_ref, cc_ref, sc_ref, cp_ref, sp_ref, sk_ref, do_ref,
             dq_ref, dk_ref, dv_ref, acc_ref, dqr_ref, dkw_ref, dvw_ref, ck_ref, cv_ref):
        i = pl.program_id(0)

        @pl.when(i == 0)
        def _():
            acc_ref[...] = jnp.zeros_like(acc_ref)
            ck_ref[...] = jnp.zeros_like(ck_ref)
            cv_ref[...] = jnp.zeros_like(cv_ref)

        @pl.when(i < nb)
        def _():
            first = i == 0
            cc, sc = cc_ref[...], sc_ref[...]
            cq, sq = jnp.tile(cc, (1, ATTN_Q // ATTN_KV)), jnp.tile(sc, (1, ATTN_Q // ATTN_KV))
            q = _rope(q_ref[...], cq, sq)
            kc = _rope(kc_ref[...], cc, sc)
            kp = _rope(kp_ref[...], cp_ref[...], sp_ref[...])
            vc, vp = vc_ref[...], vp_ref[...]
            sk = sk_ref[...]
            do = do_ref[...]
            lane = lax.broadcasted_iota(jnp.int32, (1, 128), 1)
            dsink = jnp.zeros((1, 128), F32)
            for hk in range(ATTN_KV_HEADS):
                ks = slice(hk * ATTN_HEAD_DIM, (hk + 1) * ATTN_HEAD_DIM)
                kwin = jnp.concatenate([kp[:, ks], kc[:, ks]], axis=0)
                vwin = jnp.concatenate([vp[:, ks], vc[:, ks]], axis=0)
                dkw = jnp.zeros((2 * ATTN_BLOCK, ATTN_HEAD_DIM), F32)
                dvw = jnp.zeros((2 * ATTN_BLOCK, ATTN_HEAD_DIM), F32)
                for g in range(ATTN_GROUPS):
                    h = hk * ATTN_GROUPS + g
                    hs = slice(h * ATTN_HEAD_DIM, (h + 1) * ATTN_HEAD_DIM)
                    qh = q[:, hs]
                    probs, psink = _attn_probs(qh, kwin, _lane_scalar(sk, h), first)
                    doh = do[:, hs]
                    dp = _dot(doh, vwin, NT)
                    delta = jnp.sum(probs * dp, axis=-1, keepdims=True)
                    ds = probs * (dp - delta) * ATTN_SCALE
                    dqr_ref[:, hs] = _dot(ds, kwin)
                    dkw += _dot(ds, qh, TN)
                    dvw += _dot(probs, doh, TN)
                    dsink += jnp.where(lane == h, jnp.sum(-psink * delta, axis=0, keepdims=True), 0.0)
                dkw_ref[:, ks] = dkw
                dvw_ref[:, ks] = dvw
            acc_ref[0:1, :] += dsink
            dq_ref[...] = _rope_bwd(dqr_ref[...], cq, sq).astype(dq_ref.dtype)
            dk_ref[...] = (ck_ref[...] + _rope_bwd(dkw_ref[0:ATTN_BLOCK, :], cp_ref[...], sp_ref[...])).astype(dk_ref.dtype)
            dv_ref[...] = (cv_ref[...] + dvw_ref[0:ATTN_BLOCK, :]).astype(dv_ref.dtype)
            ck_ref[...] = _rope_bwd(dkw_ref[ATTN_BLOCK:2 * ATTN_BLOCK, :], cc, sc)
            cv_ref[...] = dvw_ref[ATTN_BLOCK:2 * ATTN_BLOCK, :]

        @pl.when(i == nb)
        def _():
            dk_ref[...] = ck_ref[...].astype(dk_ref.dtype)
            dv_ref[...] = cv_ref[...].astype(dv_ref.dtype)

    prev_out = lambda w: pl.BlockSpec((ATTN_BLOCK, w), lambda i: (jnp.maximum(i - 1, 0), 0))
    return pl.pallas_call(
        body, name="attn_bwd", grid=(nb + 1,),
        in_specs=_attn_specs(nb) + [pl.BlockSpec((ATTN_BLOCK, ATTN_Q), lambda i: (jnp.minimum(i, nb - 1), 0))],
        out_specs=[pl.BlockSpec((ATTN_BLOCK, ATTN_Q), lambda i: (jnp.minimum(i, nb - 1), 0)), prev_out(ATTN_KV),
                   prev_out(ATTN_KV), _full((8, 128))],
        out_shape=[jax.ShapeDtypeStruct((t, ATTN_Q), MXU_DTYPE), jax.ShapeDtypeStruct((t, ATTN_KV), MXU_DTYPE),
                   jax.ShapeDtypeStruct((t, ATTN_KV), MXU_DTYPE), jax.ShapeDtypeStruct((8, 128), F32)],
        scratch_shapes=[pltpu.VMEM((ATTN_BLOCK, ATTN_Q), F32), pltpu.VMEM((2 * ATTN_BLOCK, ATTN_KV), F32),
                        pltpu.VMEM((2 * ATTN_BLOCK, ATTN_KV), F32), pltpu.VMEM((ATTN_BLOCK, ATTN_KV), F32),
                        pltpu.VMEM((ATTN_BLOCK, ATTN_KV), F32)],
        compiler_params=_params("arbitrary"),
    )(pa, pa, pa, pa, pa, cos, sin, cos, sin, sinks_vec, dao)


PAIR = 2 * DN_CHUNK
HALO = 8


def _conv_window(cur_ref, prev_ref, xs_ref, tm):
    prev = jnp.where(pl.program_id(0) > 0, prev_ref[...], 0.0)
    xs_ref[0:HALO, :] = prev
    xs_ref[HALO:HALO + tm, :] = cur_ref[...]


def _conv_taps(xs_ref, cw_ref, tm):
    y = cw_ref[0:1, :] * xs_ref[pl.ds(HALO - DN_CONV + 1, tm), :]
    for j in range(1, DN_CONV):
        y += cw_ref[j:j + 1, :] * xs_ref[pl.ds(HALO - DN_CONV + 1 + j, tm), :]
    return y


def _gate_values(ba, al, dt):
    beta = _sigmoid(ba)
    pre = ba + dt
    g = -jnp.exp(al) * _softplus(pre)
    return beta, g, pre


def _dn_prep_specs(tm, t):
    return [pl.BlockSpec((tm, CONV_CH), lambda i: (i, 0)),
            pl.BlockSpec((HALO, CONV_CH), lambda i: (jnp.maximum(i * (tm // HALO) - 1, 0), 0)),
            pl.BlockSpec((tm, 128), lambda i: (i, 4 * DN_W // 128)),
            _full((DN_CONV, CONV_CH)), _full((1, 128)), _full((1, 128))]


def _dn_prep(pd, conv_w, al_vec, dt_vec, tm):
    t = pd.shape[0]

    def body(cur_ref, prev_ref, ba_ref, cw_ref, al_ref, dt_ref, qn_ref, kn_ref, vc_ref, gc_ref, gr_ref, xs_ref):
        _conv_window(cur_ref, prev_ref, xs_ref, tm)
        y = _conv_taps(xs_ref, cw_ref, tm)
        c = y * _sigmoid(y)
        for h in range(DN_HEADS):
            qs = slice(h * DN_HEAD_DIM, (h + 1) * DN_HEAD_DIM)
            ksl = slice(DN_W + h * DN_HEAD_DIM, DN_W + (h + 1) * DN_HEAD_DIM)
            qh, kh = c[:, qs], c[:, ksl]
            qn_ref[:, qs] = qh * lax.rsqrt(jnp.sum(qh * qh, axis=-1, keepdims=True) + EPS) * DN_SCALE
            kn_ref[:, qs] = kh * lax.rsqrt(jnp.sum(kh * kh, axis=-1, keepdims=True) + EPS)
        vc_ref[...] = c[:, 2 * DN_W:3 * DN_W]
        beta, g, _ = _gate_values(ba_ref[...], al_ref[...], dt_ref[...])
        lane = lax.broadcasted_iota(jnp.int32, beta.shape, 1)
        gb = jnp.where(lane < DN_HEADS, beta, jnp.where(lane < 2 * DN_HEADS, g, 0.0))
        gc_ref[...] = gb
        gr_ref[...] = gb.T[0:8, :]

    tok = lambda w: pl.BlockSpec((tm, w), lambda i: (i, 0))
    return pl.pallas_call(
        body, name="dn_prep", grid=(t // tm,),
        in_specs=_dn_prep_specs(tm, t),
        out_specs=[tok(DN_W), tok(DN_W), tok(DN_W), tok(128), pl.BlockSpec((8, tm), lambda i: (0, i))],
        out_shape=[jax.ShapeDtypeStruct((t, DN_W), F32)] * 3 + [jax.ShapeDtypeStruct((t, 128), F32),
                                                                 jax.ShapeDtypeStruct((8, t), F32)],
        scratch_shapes=[pltpu.VMEM((HALO + tm, CONV_CH), F32)],
        compiler_params=_params("parallel"),
    )(pd, pd, pd, conv_w, al_vec, dt_vec)


def _pair_masks():
    r = lax.broadcasted_iota(jnp.int32, (PAIR, PAIR), 0)
    c = lax.broadcasted_iota(jnp.int32, (PAIR, PAIR), 1)
    same = (r < DN_CHUNK) == (c < DN_CHUNK)
    return same & (r >= c), same & (r > c)


def _lane_col(mat, idx):
    lane = lax.broadcasted_iota(jnp.int32, mat.shape, 1)
    return jnp.sum(jnp.where(lane == idx, mat, 0.0), axis=-1, keepdims=True)


def _pair_cumsums(gc, gr, low):
    lowf = low.astype(F32)
    return _dot(lowf, gc, NN, HI), _dot(gr, lowf, NT, HI)


def _pair_gates(gc, cum_c, cum_r, low, h):
    beta = _lane_col(gc, h)
    gam = _lane_col(cum_c, DN_HEADS + h)
    gam_row = cum_r[DN_HEADS + h:DN_HEADS + h + 1, :]
    dm = jnp.where(low, jnp.exp(jnp.where(low, gam - gam_row, 0.0)), 0.0)
    row = lax.broadcasted_iota(jnp.int32, gam.shape, 0)
    gl = jnp.where(row < DN_CHUNK, gam[DN_CHUNK - 1:DN_CHUNK, :], gam[PAIR - 1:PAIR, :])
    return beta, gam, dm, gl


def _split(a):
    hi = a.astype(BF16)
    return hi, (a - hi.astype(F32)).astype(BF16)


def _dot_split(a, b, dims=NN):
    (ah, al), (bh, bl) = a, b
    la, lb = (1, 1) if dims == TN else ((0, 1) if dims == NN else (0, 0))
    r = _dot(jnp.concatenate([ah, al], axis=la), jnp.concatenate([bh, bl], axis=lb), dims)
    m, n = r.shape[0] // 2, r.shape[1] // 2
    return (r[m:, n:] + (r[:m, n:] + r[m:, :n])) + r[:m, :n]


def _unit_lower_inverses(lmats):
    n = lmats[0].shape[0]
    eye = (lax.broadcasted_iota(jnp.int32, (n, n), 0) == lax.broadcasted_iota(jnp.int32, (n, n), 1)).astype(F32)
    accs = [eye - l for l in lmats]
    splits = [_split(l) for l in lmats]
    step = 1
    while 2 * step < DN_CHUNK:
        splits = [_split(_dot_split(s, s)) for s in splits]
        accs = [acc + _dot_split(_split(acc), s) for acc, s in zip(accs, splits)]
        step *= 2
    return accs


def _dn_intra(qn, kn, vc, gc, gr):
    t = qn.shape[0]
    npair = t // PAIR

    def body(q_ref, k_ref, v_ref, gc_ref, gr_ref, u_ref, w_ref, qg_ref, kd_ref, a_ref, ti_ref, dl_ref):
        gc_v = gc_ref[...]
        low, strict = _pair_masks()
        cum_c, cum_r = _pair_cumsums(gc_v, gr_ref[...], low)
        heads = [slice(h * DN_HEAD_DIM, (h + 1) * DN_HEAD_DIM) for h in range(DN_HEADS)]
        gates = [_pair_gates(gc_v, cum_c, cum_r, low, h) for h in range(DN_HEADS)]
        lmats = []
        for hs, (beta, gam, dm, gl) in zip(heads, gates):
            k = k_ref[:, hs]
            lmats.append(jnp.where(strict, _dot(k * beta, k, NT) * dm, 0.0))
        tinvs = _unit_lower_inverses(lmats)
        for h, (hs, (beta, gam, dm, gl), tinv) in enumerate(zip(heads, gates, tinvs)):
            q, k, v = q_ref[:, hs], k_ref[:, hs], v_ref[:, hs]
            eg = jnp.exp(gam)
            u_ref[:, hs] = _dot(tinv, v * beta)
            w_ref[:, hs] = _dot(tinv, (k * beta) * eg)
            a_ref[h] = _dot(q, k, NT) * dm
            ti_ref[h] = tinv
            qg_ref[:, hs] = q * eg
            kd_ref[:, hs] = k * jnp.exp(gl - gam)
            for c in range(2):
                last = (c + 1) * DN_CHUNK - 1
                dl_ref[c, h] = jnp.broadcast_to(jnp.exp(gam[last:last + 1, :]), (8, 128))

    tok = lambda w: pl.BlockSpec((PAIR, w), lambda n: (n, 0))
    hm = pl.BlockSpec((DN_HEADS, PAIR, PAIR), lambda n: (0, n, 0))
    return pl.pallas_call(
        body, name="dn_intra", grid=(npair,),
        in_specs=[tok(DN_W), tok(DN_W), tok(DN_W), tok(128), pl.BlockSpec((8, PAIR), lambda n: (0, n))],
        out_specs=[tok(DN_W)] * 4 + [hm, hm, pl.BlockSpec((2, DN_HEADS, 8, 128), lambda n: (n, 0, 0, 0))],
        out_shape=[jax.ShapeDtypeStruct((t, DN_W), F32)] * 4 + [jax.ShapeDtypeStruct((DN_HEADS, t, PAIR), F32)] * 2
                  + [jax.ShapeDtypeStruct((2 * npair, DN_HEADS, 8, 128), F32)],
        compiler_params=_params("parallel"),
    )(qn, kn, vc, gc, gr)


def _dn_scan_fwd(u, w, qg, kd, a_qk, dlast, pd, dn_w):
    t = u.shape[0]
    npair = t // PAIR

    def body(u_ref, w_ref, qg_ref, kd_ref, a_ref, dl_ref, z_ref, nw_ref, out_ref, o_ref, vn_ref, sall_ref, s_ref):
        @pl.when(pl.program_id(0) == 0)
        def _():
            s_ref[...] = jnp.zeros_like(s_ref)

        nw = nw_ref[...]
        for c in range(2):
            rows = slice(c * DN_CHUNK, (c + 1) * DN_CHUNK)
            for h in range(DN_HEADS):
                hs = slice(h * DN_HEAD_DIM, (h + 1) * DN_HEAD_DIM)
                st = s_ref[h]
                sall_ref[c, h] = st
                vn_ref[rows, hs] = u_ref[rows, hs] - _dot(w_ref[rows, hs], st)
            for h in range(DN_HEADS):
                hs = slice(h * DN_HEAD_DIM, (h + 1) * DN_HEAD_DIM)
                st, vn = s_ref[h], vn_ref[rows, hs]
                o = _dot(qg_ref[rows, hs], st) + _dot(a_ref[h, rows, rows], vn)
                s_ref[h] = st * dl_ref[c, h][0:1, :] + _dot(kd_ref[rows, hs], vn, TN)
                o_ref[rows, hs] = o
                z = z_ref[rows, hs]
                on = o * lax.rsqrt(jnp.mean(o * o, axis=-1, keepdims=True) + EPS) * nw
                out_ref[rows, hs] = on * (z * _sigmoid(z))

    tok = pl.BlockSpec((PAIR, DN_W), lambda n: (n, 0))
    hm = pl.BlockSpec((DN_HEADS, PAIR, PAIR), lambda n: (0, n, 0))
    return pl.pallas_call(
        body, name="dn_scan_fwd", grid=(npair,),
        in_specs=[tok, tok, tok, tok, hm, pl.BlockSpec((2, DN_HEADS, 8, 128), lambda n: (n, 0, 0, 0)),
                  pl.BlockSpec((PAIR, DN_W), lambda n: (n, 3)), _full((1, 128))],
        out_specs=[tok, tok, tok, pl.BlockSpec((2, DN_HEADS, DN_HEAD_DIM, DN_HEAD_DIM), lambda n: (n, 0, 0, 0))],
        out_shape=[jax.ShapeDtypeStruct((t, DN_W), F32)] * 3
                  + [jax.ShapeDtypeStruct((2 * npair, DN_HEADS, DN_HEAD_DIM, DN_HEAD_DIM), F32)],
        scratch_shapes=[pltpu.VMEM((DN_HEADS, DN_HEAD_DIM, DN_HEAD_DIM), F32)],
        compiler_params=_params("arbitrary"),
    )(u, w, qg, kd, a_qk, dlast, pd, dn_w)


def _dn_scan_bwd(dout, o, vnew, sall, w, qg, kd, a_qk, dlast, pd, dn_w):
    t = o.shape[0]
    npair = t // PAIR
    rev = lambda n: npair - 1 - n

    def body(do_ref, o_ref, vn_ref, sall_ref, w_ref, qg_ref, kd_ref, a_ref, dl_ref, z_ref, nw_ref,
             dz_ref, du_ref, dw_ref, dqg_ref, dkd_ref, da_ref, ddl_ref, acc_ref, ds_ref, dos_ref):
        @pl.when(pl.program_id(0) == 0)
        def _():
            ds_ref[...] = jnp.zeros_like(ds_ref)
            acc_ref[...] = jnp.zeros_like(acc_ref)

        nw = nw_ref[...]
        dnw = jnp.zeros((1, 128), F32)
        for h in range(DN_HEADS):
            hs = slice(h * DN_HEAD_DIM, (h + 1) * DN_HEAD_DIM)
            o, z, dout = o_ref[:, hs], z_ref[:, hs], do_ref[:, hs]
            r = lax.rsqrt(jnp.mean(o * o, axis=-1, keepdims=True) + EPS)
            oh = o * r
            sz = _sigmoid(z)
            dz_ref[:, hs] = dout * (oh * nw) * (sz + z * sz * (1.0 - sz))
            don = dout * (z * sz)
            dnw += jnp.sum(don * oh, axis=0, keepdims=True)
            doh = don * nw
            dos_ref[:, hs] = r * (doh - oh * jnp.mean(doh * oh, axis=-1, keepdims=True))
        acc_ref[0:1, :] += dnw
        for c in (1, 0):
            rows = slice(c * DN_CHUNK, (c + 1) * DN_CHUNK)
            other = slice((1 - c) * DN_CHUNK, (2 - c) * DN_CHUNK)
            for h in range(DN_HEADS):
                hs = slice(h * DN_HEAD_DIM, (h + 1) * DN_HEAD_DIM)
                do, st, dsp, vn = dos_ref[rows, hs], sall_ref[c, h], ds_ref[h], vn_ref[rows, hs]
                da_ref[h, rows, rows] = _dot(do, vn, NT)
                da_ref[h, rows, other] = jnp.zeros((DN_CHUNK, DN_CHUNK), F32)
                du_ref[rows, hs] = _dot(a_ref[h, rows, rows], do, TN) + _dot(kd_ref[rows, hs], dsp)
                dqg_ref[rows, hs] = _dot(do, st, NT)
                dkd_ref[rows, hs] = _dot(vn, dsp, NT)
                ddl = jnp.sum(jnp.sum(dsp * st, axis=1, keepdims=True), axis=0, keepdims=True)
                ddl_ref[c, h] = jnp.broadcast_to(ddl, (8, 128))
            for h in range(DN_HEADS):
                hs = slice(h * DN_HEAD_DIM, (h + 1) * DN_HEAD_DIM)
                do, st, dvn = dos_ref[rows, hs], sall_ref[c, h], du_ref[rows, hs]
                dw_ref[rows, hs] = -_dot(dvn, st, NT)
                ds_ref[h] = (ds_ref[h] * dl_ref[c, h][0:1, :] + _dot(qg_ref[rows, hs], do, TN)
                             - _dot(w_ref[rows, hs], dvn, TN))

    tok = pl.BlockSpec((PAIR, DN_W), lambda n: (rev(n), 0))
    hm = pl.BlockSpec((DN_HEADS, PAIR, PAIR), lambda n: (0, rev(n), 0))
    sc = pl.BlockSpec((2, DN_HEADS, 8, 128), lambda n: (rev(n), 0, 0, 0))
    return pl.pallas_call(
        body, name="dn_scan_bwd", grid=(npair,),
        in_specs=[tok, tok, tok, pl.BlockSpec((2, DN_HEADS, DN_HEAD_DIM, DN_HEAD_DIM), lambda n: (rev(n), 0, 0, 0)),
                  tok, tok, tok, hm, sc, pl.BlockSpec((PAIR, DN_W), lambda n: (rev(n), 3)), _full((1, 128))],
        out_specs=[tok] * 5 + [hm, sc, _full((8, 128))],
        out_shape=[jax.ShapeDtypeStruct((t, DN_W), F32)] * 5 + [jax.ShapeDtypeStruct((DN_HEADS, t, PAIR), F32),
                   jax.ShapeDtypeStruct((2 * npair, DN_HEADS, 8, 128), F32), jax.ShapeDtypeStruct((8, 128), F32)],
        scratch_shapes=[pltpu.VMEM((DN_HEADS, DN_HEAD_DIM, DN_HEAD_DIM), F32), pltpu.VMEM((PAIR, DN_W), F32)],
        compiler_params=_params("arbitrary"),
    )(dout, o, vnew, sall, w, qg, kd, a_qk, dlast, pd, dn_w)


def _dn_intra_bwd(qn, kn, vc, gc, gr, tinv, a_qk, du, dw, dqg, dkd, da_qk, ddlast, dlast):
    t = qn.shape[0]
    npair = t // PAIR

    def body(q_ref, k_ref, v_ref, gc_ref, gr_ref, ti_ref, a_ref, du_ref, dw_ref, dqg_ref, dkd_ref, da_ref, ddl_ref, dl_ref,
             dq_ref, dk_ref, dv_ref, dg_ref):
        gc_v = gc_ref[...]
        low, strict = _pair_masks()
        cum_c, cum_r = _pair_cumsums(gc_v, gr_ref[...], low)
        lane = lax.broadcasted_iota(jnp.int32, (PAIR, 128), 1)
        rowi = lax.broadcasted_iota(jnp.int32, (PAIR, 1), 0)
        rsum = lambda v: jnp.sum(v, axis=-1, keepdims=True)
        dgam_all = jnp.zeros((PAIR, 128), F32)
        dbeta_all = jnp.zeros((PAIR, 128), F32)
        heads = [slice(h * DN_HEAD_DIM, (h + 1) * DN_HEAD_DIM) for h in range(DN_HEADS)]
        gates = [_pair_gates(gc_v, cum_c, cum_r, low, h) for h in range(DN_HEADS)]
        tsplits, dtis = [], []
        for h, (hs, (beta, gam, dm, gl)) in enumerate(zip(heads, gates)):
            k = k_ref[:, hs]
            tsplits.append(_split(ti_ref[h]))
            dtis.append(_dot(du_ref[:, hs], v_ref[:, hs] * beta, NT)
                        + _dot(dw_ref[:, hs], (k * beta) * jnp.exp(gam), NT))
        xs = [_dot_split(ts, _split(dti), TN) for ts, dti in zip(tsplits, dtis)]
        dls = [jnp.where(strict, -_dot_split(_split(x), ts, NT), 0.0) for x, ts in zip(xs, tsplits)]
        for h, (hs, (beta, gam, dm, gl), dl) in enumerate(zip(heads, gates, dls)):
            q, k, v = q_ref[:, hs], k_ref[:, hs], v_ref[:, hs]
            tinv, a = ti_ref[h], a_ref[h]
            du, dw, dqg, dkd = du_ref[:, hs], dw_ref[:, hs], dqg_ref[:, hs], dkd_ref[:, hs]
            kb = k * beta
            eg = jnp.exp(gam)
            ekd = jnp.exp(gl - gam)
            kbg = kb * eg
            lmat = jnp.where(strict, _dot(kb, k, NT) * dm, 0.0)
            dvb = _dot(tinv, du, TN)
            dkbg = _dot(tinv, dw, TN)
            dmm = dl * dm
            dam = jnp.where(low, da_ref[h], 0.0)
            dn = dam * dm
            e = dl * lmat + dam * a
            dkb = _dot(dmm, k) + dkbg * eg
            dk_ref[:, hs] = _dot(dmm, kb, TN) + _dot(dn, q, TN) + dkd * ekd + dkb * beta
            dq_ref[:, hs] = _dot(dn, k) + dqg * eg
            dv_ref[:, hs] = dvb * beta
            t_kd = rsum(dkd * (k * ekd))
            dgam = rsum(e) - rsum(e.T) + rsum(dqg * (q * eg)) + rsum(dkbg * kbg) - t_kd
            for c in range(2):
                rows = slice(c * DN_CHUNK, (c + 1) * DN_CHUNK)
                dgl = (jnp.sum(t_kd[rows, :], axis=0, keepdims=True)
                       + ddl_ref[c, h][0:1, 0:1] * dl_ref[c, h][0:1, 0:1])
                dgam = dgam + jnp.where(rowi == (c + 1) * DN_CHUNK - 1, dgl, 0.0)
            dgam_all += jnp.where(lane == DN_HEADS + h, dgam, 0.0)
            dbeta_all += jnp.where(lane == h, rsum(dkb * k) + rsum(dvb * v), 0.0)
        dg_ref[...] = dbeta_all + _dot(low.astype(F32), dgam_all, TN, HI)

    tok = lambda w: pl.BlockSpec((PAIR, w), lambda n: (n, 0))
    hm = pl.BlockSpec((DN_HEADS, PAIR, PAIR), lambda n: (0, n, 0))
    sc = pl.BlockSpec((2, DN_HEADS, 8, 128), lambda n: (n, 0, 0, 0))
    return pl.pallas_call(
        body, name="dn_intra_bwd", grid=(npair,),
        in_specs=[tok(DN_W), tok(DN_W), tok(DN_W), tok(128), pl.BlockSpec((8, PAIR), lambda n: (0, n)), hm, hm,
                  tok(DN_W), tok(DN_W), tok(DN_W), tok(DN_W), hm, sc, sc],
        out_specs=[tok(DN_W), tok(DN_W), tok(DN_W), tok(128)],
        out_shape=[jax.ShapeDtypeStruct((t, DN_W), F32)] * 3 + [jax.ShapeDtypeStruct((t, 128), F32)],
        compiler_params=_params("parallel"),
    )(qn, kn, vc, gc, gr, tinv, a_qk, du, dw, dqg, dkd, da_qk, ddlast, dlast)


def _dn_prep_bwd(pd, conv_w, al_vec, dt_vec, dqn, dkn, dvc, dgc, tm):
    t = pd.shape[0]

    def body(cur_ref, prev_ref, ba_ref, cw_ref, al_ref, dt_ref, dq_ref, dk_ref, dv_ref, dg_ref,
             dy_ref, dba_ref, accw_ref, accg_ref, xs_ref, dc_ref):
        @pl.when(pl.program_id(0) == 0)
        def _():
            accw_ref[...] = jnp.zeros_like(accw_ref)
            accg_ref[...] = jnp.zeros_like(accg_ref)

        _conv_window(cur_ref, prev_ref, xs_ref, tm)
        y = _conv_taps(xs_ref, cw_ref, tm)
        sg = _sigmoid(y)
        c = y * sg
        for h in range(DN_HEADS):
            qs = slice(h * DN_HEAD_DIM, (h + 1) * DN_HEAD_DIM)
            ksl = slice(DN_W + h * DN_HEAD_DIM, DN_W + (h + 1) * DN_HEAD_DIM)
            for src, sl, scale in ((dq_ref, qs, DN_SCALE), (dk_ref, ksl, 1.0)):
                xh = c[:, sl]
                r = lax.rsqrt(jnp.sum(xh * xh, axis=-1, keepdims=True) + EPS)
                unit = xh * r
                dn = src[:, qs] * scale
                dc_ref[:, sl] = r * (dn - unit * jnp.sum(dn * unit, axis=-1, keepdims=True))
        dc_ref[:, 2 * DN_W:3 * DN_W] = dv_ref[...]
        dy = dc_ref[...] * (sg + y * sg * (1.0 - sg))
        dy_ref[...] = dy
        for j in range(DN_CONV):
            accw_ref[j:j + 1, :] += jnp.sum(dy * xs_ref[pl.ds(HALO - DN_CONV + 1 + j, tm), :], axis=0, keepdims=True)

        beta, g, pre = _gate_values(ba_ref[...], al_ref[...], dt_ref[...])
        dgb = dg_ref[...]
        lane = lax.broadcasted_iota(jnp.int32, dgb.shape, 1)
        is_b, is_a = lane < DN_HEADS, (lane >= DN_HEADS) & (lane < 2 * DN_HEADS)
        dpre = dgb * (-jnp.exp(al_ref[...])) * _sigmoid(pre)
        dba_ref[...] = jnp.where(is_b, dgb * beta * (1.0 - beta), jnp.where(is_a, dpre, 0.0))
        accg_ref[0:1, :] += jnp.sum(jnp.where(is_a, dgb * g, 0.0), axis=0, keepdims=True)
        accg_ref[1:2, :] += jnp.sum(jnp.where(is_a, dpre, 0.0), axis=0, keepdims=True)

    tok = lambda w: pl.BlockSpec((tm, w), lambda i: (i, 0))
    return pl.pallas_call(
        body, name="dn_prep_bwd", grid=(t // tm,),
        in_specs=_dn_prep_specs(tm, t) + [tok(DN_W), tok(DN_W), tok(DN_W), tok(128)],
        out_specs=[tok(CONV_CH), tok(128), _full((8, CONV_CH)), _full((8, 128))],
        out_shape=[jax.ShapeDtypeStruct((t, CONV_CH), F32), jax.ShapeDtypeStruct((t, 128), F32),
                   jax.ShapeDtypeStruct((8, CONV_CH), F32), jax.ShapeDtypeStruct((8, 128), F32)],
        scratch_shapes=[pltpu.VMEM((HALO + tm, CONV_CH), F32), pltpu.VMEM((tm, CONV_CH), F32)],
        compiler_params=_params("arbitrary"),
    )(pd, pd, pd, conv_w, al_vec, dt_vec, dqn, dkn, dvc, dgc)


def _dn_conv_bwd(dy, dz, dba, conv_w, tm):
    t = dy.shape[0]
    nt = t // tm

    def body(cur_ref, nxt_ref, dz_ref, dba_ref, cw_ref, o_ref, ds_ref):
        nxt = jnp.where(pl.program_id(0) < nt - 1, nxt_ref[...], 0.0)
        ds_ref[0:tm, :] = cur_ref[...]
        ds_ref[tm:tm + HALO, :] = nxt
        dx = cw_ref[0:1, :] * ds_ref[pl.ds(DN_CONV - 1, tm), :]
        for j in range(1, DN_CONV):
            dx += cw_ref[j:j + 1, :] * ds_ref[pl.ds(DN_CONV - 1 - j, tm), :]
        o_ref[:, 0:CONV_CH] = dx.astype(o_ref.dtype)
        o_ref[:, CONV_CH:CONV_CH + DN_W] = dz_ref[...].astype(o_ref.dtype)
        o_ref[:, CONV_CH + DN_W:DN_COLS] = dba_ref[...].astype(o_ref.dtype)

    tok = lambda w: pl.BlockSpec((tm, w), lambda i: (i, 0))
    return pl.pallas_call(
        body, name="dn_conv_bwd", grid=(nt,),
        in_specs=[tok(CONV_CH),
                  pl.BlockSpec((HALO, CONV_CH), lambda i: (jnp.minimum((i + 1) * (tm // HALO), t // HALO - 1), 0)),
                  tok(DN_W), tok(128), _full((DN_CONV, CONV_CH))],
        out_specs=tok(DN_COLS),
        out_shape=jax.ShapeDtypeStruct((t, DN_COLS), MXU_DTYPE),
        scratch_shapes=[pltpu.VMEM((tm + HALO, CONV_CH), F32)],
        compiler_params=_params("parallel"),
    )(dy, dy, dz, dba, conv_w)


def _pad_lanes(v, offset=0):
    return jnp.zeros((1, 128), F32).at[0, offset:offset + v.shape[0]].set(v.astype(F32))


def _local_step(x, p, tgt, sm, w, late):
    t = x.shape[0]
    tm = min(512, t // 2)
    tm_s = min(256, t // 2)

    w_in = w["w_in"]
    wa = w_in[:, :ATTN_Q + 2 * ATTN_KV]
    wd = jnp.pad(w_in[:, ATTN_Q + 2 * ATTN_KV:], ((0, 0), (0, DN_COLS - (D_IN - ATTN_Q - 2 * ATTN_KV))))
    conv_w = w["conv_w"]
    al_vec, dt_vec = _pad_lanes(sm["a_log"], DN_HEADS), _pad_lanes(sm["dt_bias"], DN_HEADS)
    sinks_vec = _pad_lanes(sm["sinks"])
    dn_w = sm["dn_norm"].reshape(1, 128)
    row = lambda v: v.reshape(1, D_MODEL)
    cos, sin = _rope_tables(t)

    u, pa, pd = _inproj(x, row(sm["norm_mix"]), wa, wd, tm_s)
    ao = _attn_fwd(pa, cos, sin, sinks_vec)
    qn, kn, vc, gc, gr = _dn_prep(pd, conv_w, al_vec, dt_vec, tm_s)
    uu, ww, qg, kd, a_qk, tinv, dlast = _dn_intra(qn, kn, vc, gc, gr)
    dn_out, o, vnew, sall = _dn_scan_fwd(uu, ww, qg, kd, a_qk, dlast, pd, dn_w)
    w = dict(w, **late(dn_out))
    wo_a, wo_d = w["w_o"][:ATTN_Q], w["w_o"][ATTN_Q:]
    w_proj = jnp.transpose(w["w_proj4"], (1, 0, 2)).reshape(PLE_DIM, D_MODEL)
    h1 = _oproj(x, ao, dn_out, wo_a, wo_d, tm)
    m, a, h2 = _mlp_fwd(h1, row(sm["norm_mlp"]), w["w_up4"], w["w_down"], tm)
    dh2, dh2b, dgp, dpp, n3, pb, acc_ple = _ple_loss(h2, p, tgt, row(sm["norm_ple"]), row(sm["norm_final"]),
                                                     w["w_gate"], w_proj, tm_s)
    s, da, dh1, dh1b, acc_mlp = _mlp_bwd(dh2, dh2b, a, h1, row(sm["norm_mlp"]), w["w_up4"], w["w_down"], tm)
    dao, ddn = _oproj_bwd(dh1b, wo_a, wo_d, tm)
    dz, du, dw, dqg, dkd, da_qk, ddlast, acc_dn = _dn_scan_bwd(ddn, o, vnew, sall, ww, qg, kd, a_qk, dlast, pd, dn_w)
    dqn, dkn, dvc, dgc = _dn_intra_bwd(qn, kn, vc, gc, gr, tinv, a_qk, du, dw, dqg, dkd, da_qk, ddlast, dlast)
    dy, dba, acc_conv, acc_gate = _dn_prep_bwd(pd, conv_w, al_vec, dt_vec, dqn, dkn, dvc, dgc, tm_s)
    d_dn = _dn_conv_bwd(dy, dz, dba, conv_w, tm_s)
    dq, dk, dv, acc_attn = _attn_bwd(pa, cos, sin, sinks_vec, dao)
    wq, wk, wv = wa[:, :ATTN_Q], wa[:, ATTN_Q:ATTN_Q + ATTN_KV], wa[:, ATTN_Q + ATTN_KV:]
    dx, acc_mix = _inproj_bwd(x, dh1, row(sm["norm_mix"]), [dq, dk, dv, d_dn], [wq, wk, wv, wd], tm_s)

    aob, dnb = ao.astype(MXU_DTYPE), dn_out.astype(MXU_DTYPE)
    g_w_in = jnp.concatenate([
        _wgrad(u, dq, "wgrad_q", D_MODEL, ATTN_Q, tm), _wgrad(u, dk, "wgrad_k", D_MODEL, ATTN_KV, tm),
        _wgrad(u, dv, "wgrad_v", D_MODEL, ATTN_KV, tm),
        _wgrad(u, d_dn, "wgrad_dn", D_MODEL, DN_COLS, tm)[:, :D_IN - ATTN_Q - 2 * ATTN_KV]], axis=1)
    g_w_o = jnp.concatenate([_wgrad(aob, dh1b, "wgrad_oa", ATTN_Q, D_MODEL, tm),
                             _wgrad(dnb, dh1b, "wgrad_od", DN_W, D_MODEL, tm)], axis=0)
    g_w_up4 = _wgrad(m, da, "wgrad_up", D_MODEL, FF_BLOCK, tm, stacked=True)
    g_w_down = _wgrad(s, dh2b, "wgrad_down", FF_BLOCK, D_MODEL, tm)
    g_w_gate = _wgrad(n3, dgp, "wgrad_gate", D_MODEL, D_MODEL, tm)
    g_w_proj4 = _wgrad(pb, dpp, "wgrad_proj", PLE_DIM, D_MODEL // N_CHIPS, tm, stacked=True)
    grads = dict(w_in=g_w_in, w_o=g_w_o, w_up4=g_w_up4, w_down=g_w_down, w_gate=g_w_gate, w_proj4=g_w_proj4)
    sums = dict(loss=acc_ple[2, 0], norm_final=acc_ple[0], norm_ple=acc_ple[1], norm_mlp=acc_mlp[0], norm_mix=acc_mix[0],
                dn_norm=acc_dn[0], sinks=acc_attn[0, :ATTN_HEADS], a_log=acc_gate[0, DN_HEADS:2 * DN_HEADS],
                dt_bias=acc_gate[1, DN_HEADS:2 * DN_HEADS], conv_w=acc_conv[:DN_CONV])
    return sums, dx, grads


MESH = pl.DeviceIdType.MESH
ANY = pl.BlockSpec(memory_space=pl.ANY)
N_CHIPS = 4
N_DEV = 8


def _place():
    x, y, c = lax.axis_index("x"), lax.axis_index("y"), lax.axis_index("c")
    chips = [(1 - x, y), (x, 1 - y), (1 - x, 1 - y)]
    return x, y, c, chips


def _gather_weights(shards, conv_s):
    n = len(shards)
    per = 7

    def body(*refs):
        in_refs, conv_ref = refs[:n], refs[n]
        out_refs, conv_out = refs[n + 1:2 * n + 1], refs[2 * n + 1]
        send_sems, recv_sems = refs[2 * n + 2:]
        x, y, c, chips = _place()
        sibling = (x, y, 1 - c)

        def blk(a, px, py, pc):
            hr = in_refs[a].shape[0] // 2
            return out_refs[a].at[2 * px + py, pl.ds(pc * hr, hr), :]

        def mine(a):
            hr = in_refs[a].shape[0] // 2
            return in_refs[a].at[pl.ds(c * hr, hr), :]

        def rcopy(a, k, block, to, src=None):
            return pltpu.make_async_remote_copy(
                src_ref=blk(a, *block) if src is None else src, dst_ref=blk(a, *block),
                send_sem=send_sems.at[per * a + k], recv_sem=recv_sems.at[per * a + k],
                device_id=to, device_id_type=MESH)

        def whole(a, to):
            return pltpu.make_async_remote_copy(
                src_ref=in_refs[a], dst_ref=out_refs[a].at[2 * x + y],
                send_sem=send_sems.at[per * a], recv_sem=recv_sems.at[per * a], device_id=to, device_id_type=MESH)

        def ccopy(j, to):
            return pltpu.make_async_remote_copy(
                src_ref=conv_ref, dst_ref=conv_out.at[2 * x + y],
                send_sem=send_sems.at[per * n + j], recv_sem=recv_sems.at[per * n + j],
                device_id=to, device_id_type=MESH)

        started = []
        for a in range(n):
            first = [whole(a, sibling)]
            first += [rcopy(a, 1 + j, (x, y, c), (*chip, c), src=mine(a)) for j, chip in enumerate(chips)]
            for cp in first:
                cp.start()
            started += first
        conv_sends = [ccopy(j, (*chip, c)) for j, chip in enumerate(chips)] + [ccopy(3, sibling)]
        for cp in conv_sends:
            cp.start()
        started += conv_sends
        for a in range(n):
            for j, chip in enumerate(chips):
                rcopy(a, 1 + j, (*chip, c), (x, y, c)).wait_recv()
                fwd = rcopy(a, 4 + j, (*chip, c), sibling)
                fwd.start()
                started.append(fwd)
        for a in range(n):
            whole(a, sibling).wait_recv()
            for j, chip in enumerate(chips):
                rcopy(a, 4 + j, (*chip, 1 - c), (x, y, c)).wait_recv()
        for j, chip in enumerate(chips + [(x, y)]):
            pltpu.make_async_remote_copy(
                src_ref=conv_ref, dst_ref=conv_out.at[2 * chip[0] + chip[1]],
                send_sem=send_sems.at[per * n + j], recv_sem=recv_sems.at[per * n + j],
                device_id=sibling, device_id_type=MESH).wait_recv()
        for cp in started:
            cp.wait_send()

    nsem = per * n + 4
    out_shape = [jax.ShapeDtypeStruct((N_CHIPS,) + s.shape, s.dtype) for s in shards]
    out_shape.append(jax.ShapeDtypeStruct((N_CHIPS,) + conv_s.shape, conv_s.dtype))
    return pl.pallas_call(
        body, name="gather_weights", in_specs=[ANY] * (n + 1), out_specs=[ANY] * (n + 1), out_shape=out_shape,
        scratch_shapes=[pltpu.SemaphoreType.DMA((nsem,)), pltpu.SemaphoreType.DMA((nsem,))],
    )(*shards, conv_s)


HBM = pl.BlockSpec(memory_space=pltpu.HBM)
SEM = pl.BlockSpec(memory_space=pltpu.SEMAPHORE)
EFFECT = pltpu.SideEffectType.DATAFLOW_SIDE_EFFECTING
LATE_COPIES = 7


def _late_copies(in_refs, land_refs, send_sems, recv_sems):
    x, y, c, chips = _place()
    sends, arrivals = [], []
    for a, (src, land) in enumerate(zip(in_refs, land_refs)):
        hr = src.shape[0] // 2
        base = LATE_COPIES * a

        def cp(src_ref, dst_ref, s_idx, r_idx, to):
            return pltpu.make_async_remote_copy(src_ref=src_ref, dst_ref=dst_ref, send_sem=send_sems.at[base + s_idx],
                                                recv_sem=recv_sems.at[base + r_idx], device_id=to, device_id_type=MESH)

        sends.append(cp(src, land.at[2 * x + y], 0, 0, (x, y, 1 - c)))
        arrivals.append(cp(src, land.at[2 * x + y], 0, 0, (x, y, 1 - c)))
        for j, chip in enumerate(chips):
            for pc in range(2):
                half = src.at[pl.ds(c * hr, hr), :]
                sends.append(cp(half, land.at[2 * x + y, pl.ds(c * hr, hr), :], 1 + 2 * j + pc, 1 + 2 * j + c, (*chip, pc)))
                arrivals.append(cp(half, land.at[2 * chip[0] + chip[1], pl.ds(pc * hr, hr), :], 1 + 2 * j + pc,
                                   1 + 2 * j + pc, (*chip, pc)))
    return sends, arrivals


def _gather_start(shards, after):
    n = len(shards)

    def body(*refs):
        in_refs, land_refs = refs[:n], refs[n:2 * n]
        send_sems, recv_sems = refs[2 * n + 1], refs[2 * n + 2]
        token = refs[-1]
        sends, _ = _late_copies(in_refs, land_refs, send_sems, recv_sems)
        for cp in sends:
            cp.start()
        token[...] = jnp.zeros_like(token)

    lands = [pltpu.with_memory_space_constraint(lax.empty((N_CHIPS,) + s.shape, s.dtype), pltpu.HBM) for s in shards]
    ins = [pltpu.with_memory_space_constraint(s, pltpu.HBM) for s in shards]
    nsem = LATE_COPIES * n
    out = pl.pallas_call(
        body, name="gather_start",
        out_shape=(pltpu.SemaphoreType.DMA((nsem,)), pltpu.SemaphoreType.DMA((nsem,)),
                   *[pltpu.HBM(s.shape, s.dtype) for s in shards],
                   *[pltpu.HBM((N_CHIPS,) + s.shape, s.dtype) for s in shards],
                   jax.ShapeDtypeStruct((8, 128), F32)),
        in_specs=[HBM] * (2 * n) + [ANY],
        out_specs=(SEM, SEM, *[HBM] * (2 * n), pl.BlockSpec(memory_space=pltpu.VMEM)),
        input_output_aliases={i: 2 + i for i in range(2 * n)},
        compiler_params=pltpu.CompilerParams(has_side_effects=EFFECT),
    )(*ins, *lands, after)
    return out[0], out[1], out[2:2 + n], out[2 + n:2 + 2 * n], out[-1]


def _gather_wait(send_sems, recv_sems, shards, lands, after):
    n = len(shards)

    def body(*refs):
        in_refs, land_refs = refs[:n], refs[n:2 * n]
        s_sems, r_sems = refs[2 * n], refs[2 * n + 1]
        sends, arrivals = _late_copies(in_refs, land_refs, s_sems, r_sems)
        for cp in sends:
            cp.wait_send()
        for cp in arrivals:
            cp.wait_recv()

    out = pl.pallas_call(
        body, name="gather_wait",
        out_shape=(*[pltpu.HBM(s.shape, s.dtype) for s in shards], *[pltpu.HBM(l.shape, l.dtype) for l in lands]),
        in_specs=[HBM] * (2 * n) + [SEM, SEM, ANY],
        out_specs=tuple([HBM] * (2 * n)),
        input_output_aliases={i: i for i in range(2 * n)},
        compiler_params=pltpu.CompilerParams(has_side_effects=EFFECT),
    )(*shards, *lands, send_sems, recv_sems, after)
    return out[n:]


def _exchange_halves(grads):
    n = len(grads)

    def body(*refs):
        g_refs, got_refs = refs[:n], refs[n:2 * n]
        send_sems, recv_sems = refs[2 * n:]
        x, y, c, _ = _place()
        remote = []
        for a in range(n):
            hr = g_refs[a].shape[1] // 2
            remote.append(pltpu.make_async_remote_copy(
                src_ref=g_refs[a].at[:, pl.ds((1 - c) * hr, hr), :], dst_ref=got_refs[a],
                send_sem=send_sems.at[a], recv_sem=recv_sems.at[a], device_id=(x, y, 1 - c), device_id_type=MESH))
        for cp in remote:
            cp.start()
        for cp in remote:
            cp.wait_recv()
        for cp in remote:
            cp.wait_send()

    half = [jax.ShapeDtypeStruct((g.shape[0], g.shape[1] // 2, g.shape[2]), g.dtype) for g in grads]
    return pl.pallas_call(
        body, name="exchange_halves", in_specs=[ANY] * n, out_specs=[ANY] * n, out_shape=half,
        scratch_shapes=[pltpu.SemaphoreType.DMA((n,)), pltpu.SemaphoreType.DMA((n,))],
    )(*grads)


def _scatter_to_chips(sums16):
    n = len(sums16)

    def body(*refs):
        s16, got_refs = refs[:n], refs[n:2 * n]
        send_sems, recv_sems = refs[2 * n:]
        x, y, c, chips = _place()
        remote = []
        for a in range(n):
            for j, chip in enumerate(chips):
                remote.append(pltpu.make_async_remote_copy(
                    src_ref=s16[a].at[2 * chip[0] + chip[1]], dst_ref=got_refs[a].at[j],
                    send_sem=send_sems.at[3 * a + j], recv_sem=recv_sems.at[3 * a + j],
                    device_id=(*chip, c), device_id_type=MESH))
        for cp in remote:
            cp.start()
        for cp in remote:
            cp.wait_recv()
        for cp in remote:
            cp.wait_send()

    got = [jax.ShapeDtypeStruct((3,) + s.shape[1:], BF16) for s in sums16]
    return pl.pallas_call(
        body, name="scatter_to_chips", in_specs=[ANY] * n, out_specs=[ANY] * n, out_shape=got,
        scratch_shapes=[pltpu.SemaphoreType.DMA((3 * n,)), pltpu.SemaphoreType.DMA((3 * n,))],
    )(*sums16)


def _share_halves(bufs):
    n = len(bufs)

    def body(*refs):
        out_refs = refs[n:2 * n]
        send_sems, recv_sems = refs[2 * n:]
        x, y, c, _ = _place()
        remote = [pltpu.make_async_remote_copy(
            src_ref=out_refs[a].at[c], dst_ref=out_refs[a].at[c], send_sem=send_sems.at[a], recv_sem=recv_sems.at[a],
            device_id=(x, y, 1 - c), device_id_type=MESH) for a in range(n)]
        for cp in remote:
            cp.start()
        for a in range(n):
            pltpu.make_async_remote_copy(
                src_ref=out_refs[a].at[c], dst_ref=out_refs[a].at[1 - c], send_sem=send_sems.at[a],
                recv_sem=recv_sems.at[a], device_id=(x, y, 1 - c), device_id_type=MESH).wait_recv()
        for cp in remote:
            cp.wait_send()

    return pl.pallas_call(
        body, name="share_halves", in_specs=[ANY] * n, out_specs=[ANY] * n,
        out_shape=[jax.ShapeDtypeStruct(b.shape, b.dtype) for b in bufs],
        input_output_aliases={a: a for a in range(n)},
        scratch_shapes=[pltpu.SemaphoreType.DMA((n,)), pltpu.SemaphoreType.DMA((n,))],
    )(*bufs)


SMALL_ROWS, SMALL_COLS = 16, CONV_CH


def _allreduce_small(block):
    m_per, ncol = block.shape

    def body(x_ref, sum_ref, all_ref, send_sems, recv_sems, local_sem):
        x, y, c, chips = _place()
        me, sibling = (x, y, c), (x, y, 1 - c)

        def rows(px, py, pc):
            return all_ref.at[pl.ds((4 * px + 2 * py + pc) * m_per, m_per), :]

        def copy(k, block_of, to, src=None):
            return pltpu.make_async_remote_copy(
                src_ref=rows(*block_of) if src is None else src, dst_ref=rows(*block_of),
                send_sem=send_sems.at[k], recv_sem=recv_sems.at[k], device_id=to, device_id_type=MESH)

        mine = pltpu.make_async_copy(x_ref, rows(*me), local_sem)
        mine.start()
        first = [copy(0, me, sibling, src=x_ref)]
        first += [copy(1 + j, me, (*chip, c), src=x_ref) for j, chip in enumerate(chips)]
        for cp in first:
            cp.start()
        passed = [copy(4 + j, (*chip, c), sibling) for j, chip in enumerate(chips)]
        for j, chip in enumerate(chips):
            copy(1 + j, (*chip, c), me).wait_recv()
            passed[j].start()
        copy(0, sibling, me).wait_recv()
        for j, chip in enumerate(chips):
            copy(4 + j, (*chip, 1 - c), me).wait_recv()
        for cp in first + passed:
            cp.wait_send()
        mine.wait()
        total = all_ref[0:m_per, :]
        for d in range(1, N_DEV):
            total = total + all_ref[d * m_per:(d + 1) * m_per, :]
        sum_ref[...] = total

    vm = pl.BlockSpec(memory_space=pltpu.VMEM)
    return pl.pallas_call(
        body, name="allreduce_small", in_specs=[vm], out_specs=vm,
        out_shape=jax.ShapeDtypeStruct((m_per, ncol), F32),
        scratch_shapes=[pltpu.VMEM((N_DEV * m_per, ncol), F32), pltpu.SemaphoreType.DMA((7,)),
                        pltpu.SemaphoreType.DMA((7,)), pltpu.SemaphoreType.DMA],
    )(block)


def _row_tile(rows, cols):
    tile = rows
    while tile * cols * 4 > (1 << 20) and tile % 16 == 0:
        tile //= 2
    return tile


def _elementwise(fn, name, ins, out_dtypes):
    rows, cols = ins[0].shape
    tile = _row_tile(rows, cols)

    def body(*refs):
        outs = fn(*[r[...] for r in refs[:len(ins)]])
        for o_ref, o in zip(refs[len(ins):], outs):
            o_ref[...] = o.astype(o_ref.dtype)

    spec = pl.BlockSpec((tile, cols), lambda i: (i, 0))
    return pl.pallas_call(
        body, name=name, grid=(rows // tile,), in_specs=[spec] * len(ins), out_specs=[spec] * len(out_dtypes),
        out_shape=[jax.ShapeDtypeStruct((rows, cols), d) for d in out_dtypes],
        compiler_params=_params("parallel"),
    )(*ins)


def _adamw_tile(w, g, m, v):
    m = ADAM_B1 * m + (1.0 - ADAM_B1) * g
    v = ADAM_B2 * v + (1.0 - ADAM_B2) * jnp.square(g)
    m_hat = m / (1.0 - ADAM_B1 ** ADAM_STEP)
    v_hat = v / (1.0 - ADAM_B2 ** ADAM_STEP)
    delta = -ADAM_LR * (m_hat / (jnp.sqrt(v_hat) + ADAM_EPS) + ADAM_WD * w)
    return delta, m, v


def _adamw(name, w, g, m, v):
    return _elementwise(_adamw_tile, name, [w, g, m, v], [F32, F32, F32])


def _chip_sum(name, g4, got, place):
    nchip, hr, cols = got.shape
    tile = _row_tile(hr, cols)
    nblk = hr // tile

    def body(pl_ref, g_ref, o_ref, s32_ref, s16_ref):
        s = g_ref[...] + o_ref[...]
        s32_ref[...] = s
        s16_ref[...] = s.astype(BF16)

    spec = pl.BlockSpec((None, tile, cols), lambda k, i, pr: (k, i, 0))
    return pl.pallas_call(
        body, name=name,
        grid_spec=pltpu.PrefetchScalarGridSpec(
            num_scalar_prefetch=1, grid=(nchip, nblk),
            in_specs=[pl.BlockSpec((None, tile, cols), lambda k, i, pr: (k, pr[1] * nblk + i, 0)), spec],
            out_specs=[spec, spec]),
        out_shape=[jax.ShapeDtypeStruct(got.shape, F32), jax.ShapeDtypeStruct(got.shape, BF16)],
        compiler_params=_params("parallel", "parallel"),
    )(place, g4, got)


def _mesh_sum(name, s32, got, place):
    _, hr, cols = s32.shape
    tile = _row_tile(hr, cols)

    def body(pl_ref, own_ref, g0_ref, g1_ref, g2_ref, o_ref):
        o_ref[...] = ((own_ref[...] + g0_ref[...].astype(F32)) + g1_ref[...].astype(F32)) + g2_ref[...].astype(F32)

    slab = lambda j: pl.BlockSpec((None, tile, cols), lambda i, pr: (j, i, 0))
    return pl.pallas_call(
        body, name=name,
        grid_spec=pltpu.PrefetchScalarGridSpec(
            num_scalar_prefetch=1, grid=(hr // tile,),
            in_specs=[pl.BlockSpec((None, tile, cols), lambda i, pr: (pr[0], i, 0)), slab(0), slab(1), slab(2)],
            out_specs=pl.BlockSpec((None, tile, cols), lambda i, pr: (pr[1], i, 0))),
        out_shape=jax.ShapeDtypeStruct((2, hr, cols), F32),
        compiler_params=_params("parallel"),
    )(place, s32, got, got, got)


def _reduce_scatter(grads):
    names = list(grads)
    place = jnp.stack([2 * lax.axis_index("x") + lax.axis_index("y"), lax.axis_index("c")]).astype(jnp.int32)
    got_a = _exchange_halves([grads[k] for k in names])
    sums = [_chip_sum("chip_sum_" + k, grads[k], g, place) for k, g in zip(names, got_a)]
    got_b = _scatter_to_chips([s[1] for s in sums])
    bufs = [_mesh_sum("mesh_sum_" + k, s[0], g, place) for k, s, g in zip(names, sums, got_b)]
    full = _share_halves(bufs)
    return {k: f.reshape(-1, f.shape[-1]) for k, f in zip(names, full)}


def kernel(x, p, norm_mix, w_in, conv_w, a_log, dt_bias, dn_norm, sinks, w_o, norm_mlp, w_up, w_down, norm_ple, w_ple_gate, w_ple_proj, norm_final, loss_target, m_norm_mix, m_w_in, m_conv_w, m_a_log, m_dt_bias, m_dn_norm, m_sinks, m_w_o, m_norm_mlp, m_w_up, m_w_down, m_norm_ple, m_w_ple_gate, m_w_ple_proj, m_norm_final, v_norm_mix, v_w_in, v_conv_w, v_a_log, v_dt_bias, v_dn_norm, v_sinks, v_w_o, v_norm_mlp, v_w_up, v_w_down, v_norm_ple, v_w_ple_gate, v_w_ple_proj, v_norm_final):
    chip = 2 * lax.axis_index("x") + lax.axis_index("y")
    big = dict(w_in=w_in[0], w_o=w_o[0], w_up=w_up[0], w_down=w_down[0], w_gate=w_ple_gate[0], w_proj=w_ple_proj[0])
    big_m = dict(w_in=m_w_in[0], w_o=m_w_o[0], w_up=m_w_up[0], w_down=m_w_down[0], w_gate=m_w_ple_gate[0], w_proj=m_w_ple_proj[0])
    big_v = dict(w_in=v_w_in[0], w_o=v_w_o[0], w_up=v_w_up[0], w_down=v_w_down[0], w_gate=v_w_ple_gate[0], w_proj=v_w_ple_proj[0])
    names = list(big)

    w_in_all, conv_all = _gather_weights([big["w_in"].astype(BF16)], conv_w[0])
    late_names = names[1:]
    s_sems, r_sems, thru, lands, token = _gather_start([big[k].astype(BF16) for k in late_names], w_in_all)
    w = dict(w_in=jnp.transpose(w_in_all, (1, 0, 2)).reshape(D_MODEL, D_IN),
             conv_w=jnp.transpose(conv_all, (1, 0, 2)).reshape(DN_CONV, CONV_CH))
    sm = dict(norm_mix=norm_mix[0] + token[0, 0], a_log=a_log[0], dt_bias=dt_bias[0], dn_norm=dn_norm[0],
              sinks=sinks[0], norm_mlp=norm_mlp[0], norm_ple=norm_ple[0], norm_final=norm_final)

    def late(after):
        gw = dict(zip(late_names, _gather_wait(s_sems, r_sems, thru, lands, after)))
        return dict(w_o=gw["w_o"].reshape(D_MODEL, D_MODEL), w_up4=gw["w_up"], w_down=gw["w_down"].reshape(D_FF, D_MODEL),
                    w_gate=gw["w_gate"].reshape(D_MODEL, D_MODEL), w_proj4=gw["w_proj"])

    sums, grad_x, g = _local_step(x[0], p[0, 0], loss_target[0], sm, w, late)

    per_chip = dict(
        w_in=jnp.transpose(g["w_in"].reshape(D_MODEL, N_CHIPS, D_IN // N_CHIPS), (1, 0, 2)),
        w_o=g["w_o"].reshape(N_CHIPS, D_MODEL // N_CHIPS, D_MODEL),
        w_up=g["w_up4"],
        w_down=g["w_down"].reshape(N_CHIPS, D_FF // N_CHIPS, D_MODEL),
        w_gate=g["w_gate"].reshape(N_CHIPS, D_MODEL // N_CHIPS, D_MODEL),
        w_proj=g["w_proj4"])
    red = _reduce_scatter(per_chip)

    row = lambda v: jnp.zeros((SMALL_COLS,), F32).at[:v.shape[0]].set(v)
    misc = jnp.zeros((SMALL_COLS,), F32).at[0:4].set(sums["a_log"]).at[4:8].set(sums["dt_bias"]) \
        .at[8:16].set(sums["sinks"]).at[128:256].set(sums["dn_norm"]).at[256].set(sums["loss"])
    small = jnp.concatenate([sums["conv_w"], jnp.stack([row(sums["norm_mix"]), row(sums["norm_mlp"]), row(sums["norm_ple"]),
                                                        row(sums["norm_final"]), misc]),
                             jnp.zeros((SMALL_ROWS - 9, SMALL_COLS), F32)], axis=0)
    tot = _allreduce_small(small)
    loss = tot[8, 256]
    ncw = CONV_CH // N_CHIPS

    def pack(cw, nmix, nmlp, nple, nfin, al, dtb, sk, dnn):
        misc_p = jnp.zeros((SMALL_COLS,), F32).at[0:4].set(al).at[4:8].set(dtb).at[8:16].set(sk).at[128:256].set(dnn)
        cw_p = jnp.zeros((DN_CONV, SMALL_COLS), F32).at[:, :ncw].set(cw)
        return jnp.concatenate([cw_p, jnp.stack([row(nmix), row(nmlp), row(nple), row(nfin), misc_p]),
                                jnp.zeros((SMALL_ROWS - 9, SMALL_COLS), F32)], axis=0)

    def unpack(buf):
        return dict(conv_w=buf[0:4, :ncw][None], norm_mix=buf[4, :D_MODEL][None], norm_mlp=buf[5, :D_MODEL][None],
                    norm_ple=buf[6, :D_MODEL][None], norm_final=buf[7, :D_MODEL], a_log=buf[8, 0:4][None],
                    dt_bias=buf[8, 4:8][None], sinks=buf[8, 8:16][None], dn_norm=buf[8, 128:256][None])

    g_conv_shard = lax.dynamic_slice(tot[0:4], (0, chip * ncw), (DN_CONV, ncw))
    g_small = pack(g_conv_shard, tot[4, :D_MODEL], tot[5, :D_MODEL], tot[6, :D_MODEL], tot[7, :D_MODEL],
                   tot[8, 0:4], tot[8, 4:8], tot[8, 8:16], tot[8, 128:256])
    w_small = pack(conv_w[0], norm_mix[0], norm_mlp[0], norm_ple[0], norm_final, a_log[0], dt_bias[0], sinks[0], dn_norm[0])
    m_small = pack(m_conv_w[0], m_norm_mix[0], m_norm_mlp[0], m_norm_ple[0], m_norm_final, m_a_log[0], m_dt_bias[0],
                   m_sinks[0], m_dn_norm[0])
    v_small = pack(v_conv_w[0], v_norm_mix[0], v_norm_mlp[0], v_norm_ple[0], v_norm_final, v_a_log[0], v_dt_bias[0],
                   v_sinks[0], v_dn_norm[0])

    d_s, m_s, v_s = (unpack(b) for b in _adamw("adamw_small", w_small, g_small, m_small, v_small))
    g_s = unpack(g_small)
    out_g, out_d, out_m, out_v = dict(g_s), dict(d_s), dict(m_s), dict(v_s)
    ref_name = dict(w_in="w_in", w_o="w_o", w_up="w_up", w_down="w_down", w_gate="w_ple_gate", w_proj="w_ple_proj")
    for k in names:
        d_k, m_k, v_k = _adamw("adamw_" + k, big[k], red[k], big_m[k], big_v[k])
        out_g[ref_name[k]], out_d[ref_name[k]] = red[k][None], d_k[None]
        out_m[ref_name[k]], out_v[ref_name[k]] = m_k[None], v_k[None]
    order = ["norm_mix", "w_in", "conv_w", "a_log", "dt_bias", "dn_norm", "sinks", "w_o", "norm_mlp", "w_up", "w_down",
             "norm_ple", "w_ple_gate", "w_ple_proj", "norm_final"]
    return (loss, grad_x[None], *[out_g[k] for k in order], *[out_d[k] for k in order],
            *[out_m[k] for k in order], *[out_v[k] for k in order])
```

```python
import functools

import jax
import jax.numpy as jnp
from jax import lax
from jax.experimental import pallas as pl
from jax.experimental.pallas import tpu as pltpu

F32 = jnp.float32
BF16 = jnp.bfloat16
MXU_DTYPE = jnp.bfloat16
HI = lax.Precision.HIGHEST

D_MODEL = 1024
PLE_DIM = 256
ATTN_HEADS = 8
ATTN_KV_HEADS = 2
ATTN_GROUPS = ATTN_HEADS // ATTN_KV_HEADS
ATTN_HEAD_DIM = 64
ATTN_BLOCK = 128
ROPE_THETA = 10000.0
DN_HEADS = 4
DN_HEAD_DIM = 128
DN_CONV = 4
DN_CHUNK = 64
D_FF = 4 * D_MODEL
EPS = 1e-6
ATTN_Q = ATTN_HEADS * ATTN_HEAD_DIM
ATTN_KV = ATTN_KV_HEADS * ATTN_HEAD_DIM
DN_W = DN_HEADS * DN_HEAD_DIM
CONV_CH = 3 * DN_W
D_IN = ATTN_Q + 2 * ATTN_KV + 4 * DN_W + 2 * DN_HEADS
DN_COLS = 4 * DN_W + 128
DN_SCALE = DN_HEAD_DIM ** -0.5
ATTN_SCALE = ATTN_HEAD_DIM ** -0.5
FF_BLOCKS = 4
FF_BLOCK = D_FF // FF_BLOCKS

ADAM_LR = 0.001
ADAM_B1 = 0.9
ADAM_B2 = 0.999
ADAM_EPS = 1e-08
ADAM_WD = 0.01
ADAM_STEP = 10

V7X_VMEM_BYTES = 64 * 1024 * 1024
VMEM_LIMIT = 48 * 1024 * 1024

NN = ((1,), (0,))
NT = ((1,), (1,))
TN = ((0,), (0,))


def _dot(a, b, dims=NN, prec=None):
    return lax.dot_general(a, b, (dims, ((), ())), precision=prec, preferred_element_type=F32)


def _sigmoid(x):
    return 1.0 / (1.0 + jnp.exp(-x))


def _softplus(x):
    return jnp.maximum(x, 0.0) + jnp.log(1.0 + jnp.exp(-jnp.abs(x)))


def _params(*sem):
    return pltpu.CompilerParams(dimension_semantics=sem, vmem_limit_bytes=VMEM_LIMIT)


def _rms_fwd(xv, g):
    r = lax.rsqrt(jnp.mean(xv * xv, axis=-1, keepdims=True) + EPS)
    return xv * r * g


def _rms_bwd(xv, g, dn):
    r = lax.rsqrt(jnp.mean(xv * xv, axis=-1, keepdims=True) + EPS)
    xh = xv * r
    dg = jnp.sum(dn * xh, axis=0, keepdims=True)
    dxh = dn * g
    dx = r * (dxh - xh * jnp.mean(dxh * xh, axis=-1, keepdims=True))
    return dx, dg


def _full(shape):
    return pl.BlockSpec(shape, lambda *_: (0,) * len(shape))


def _inproj(x, g_mix, wa, wd, tm):
    t = x.shape[0]

    def body(x_ref, g_ref, wa_ref, wd_ref, u_ref, pa_ref, pd_ref):
        u = _rms_fwd(x_ref[...], g_ref[...]).astype(MXU_DTYPE)
        u_ref[...] = u
        pa_ref[...] = _dot(u, wa_ref[...])
        pd_ref[...] = _dot(u, wd_ref[...])

    na, nd = wa.shape[1], wd.shape[1]
    return pl.pallas_call(
        body, name="inproj", grid=(t // tm,),
        in_specs=[pl.BlockSpec((tm, D_MODEL), lambda i: (i, 0)), _full((1, D_MODEL)),
                  _full((D_MODEL, na)), _full((D_MODEL, nd))],
        out_specs=[pl.BlockSpec((tm, D_MODEL), lambda i: (i, 0)), pl.BlockSpec((tm, na), lambda i: (i, 0)),
                   pl.BlockSpec((tm, nd), lambda i: (i, 0))],
        out_shape=[jax.ShapeDtypeStruct((t, D_MODEL), MXU_DTYPE), jax.ShapeDtypeStruct((t, na), F32),
                   jax.ShapeDtypeStruct((t, nd), F32)],
        compiler_params=_params("parallel"),
    )(x, g_mix, wa, wd)


def _oproj(x, ao, dn, wo_a, wo_d, tm):
    t = x.shape[0]

    def body(x_ref, ao_ref, dn_ref, wa_ref, wd_ref, h_ref):
        h_ref[...] = (x_ref[...] + _dot(ao_ref[...].astype(MXU_DTYPE), wa_ref[...])
                      + _dot(dn_ref[...].astype(MXU_DTYPE), wd_ref[...]))

    half = ao.shape[1]
    return pl.pallas_call(
        body, name="oproj", grid=(t // tm,),
        in_specs=[pl.BlockSpec((tm, D_MODEL), lambda i: (i, 0)), pl.BlockSpec((tm, half), lambda i: (i, 0)),
                  pl.BlockSpec((tm, half), lambda i: (i, 0)), _full((half, D_MODEL)), _full((half, D_MODEL))],
        out_specs=pl.BlockSpec((tm, D_MODEL), lambda i: (i, 0)),
        out_shape=jax.ShapeDtypeStruct((t, D_MODEL), F32),
        compiler_params=_params("parallel"),
    )(x, ao, dn, wo_a, wo_d)


def _mlp_fwd(h1, g_mlp, w_up4, w_down, tm):
    t = h1.shape[0]

    def body(h_ref, g_ref, wu_ref, wd_ref, m_ref, a_ref, h2_ref, acc_ref):
        k = pl.program_id(1)

        @pl.when(k == 0)
        def _():
            m_ref[...] = _rms_fwd(h_ref[...], g_ref[...]).astype(MXU_DTYPE)
            acc_ref[...] = jnp.zeros_like(acc_ref)

        a = _dot(m_ref[...], wu_ref[...])
        a_ref[...] = a
        s = jnp.square(jnp.maximum(a, 0.0)).astype(MXU_DTYPE)
        acc_ref[...] += _dot(s, wd_ref[...])

        @pl.when(k == FF_BLOCKS - 1)
        def _():
            h2_ref[...] = h_ref[...] + acc_ref[...]

    return pl.pallas_call(
        body, name="mlp_fwd", grid=(t // tm, FF_BLOCKS),
        in_specs=[pl.BlockSpec((tm, D_MODEL), lambda i, k: (i, 0)), _full((1, D_MODEL)),
                  pl.BlockSpec((None, D_MODEL, FF_BLOCK), lambda i, k: (k, 0, 0)),
                  pl.BlockSpec((FF_BLOCK, D_MODEL), lambda i, k: (k, 0))],
        out_specs=[pl.BlockSpec((tm, D_MODEL), lambda i, k: (i, 0)), pl.BlockSpec((tm, FF_BLOCK), lambda i, k: (i, k)),
                   pl.BlockSpec((tm, D_MODEL), lambda i, k: (i, 0))],
        out_shape=[jax.ShapeDtypeStruct((t, D_MODEL), MXU_DTYPE), jax.ShapeDtypeStruct((t, D_FF), F32),
                   jax.ShapeDtypeStruct((t, D_MODEL), F32)],
        scratch_shapes=[pltpu.VMEM((tm, D_MODEL), F32)],
        compiler_params=_params("parallel", "arbitrary"),
    )(h1, g_mlp, w_up4, w_down)


def _ple_loss(h2, p, tgt, g_ple, g_fin, w_gate, w_proj, tm):
    t = h2.shape[0]

    def body(h_ref, p_ref, t_ref, gp_ref, gf_ref, wg_ref, wp_ref,
             dh_ref, dhb_ref, dgp_ref, dpp_ref, n3_ref, pb_ref, acc_ref):
        @pl.when(pl.program_id(0) == 0)
        def _():
            acc_ref[...] = jnp.zeros_like(acc_ref)

        h = h_ref[...]
        g_ple_v, g_fin_v = gp_ref[...], gf_ref[...]
        n3 = _rms_fwd(h, g_ple_v).astype(MXU_DTYPE)
        n3_ref[...] = n3
        gate = _sigmoid(_dot(n3, wg_ref[...]))
        pb = p_ref[...].astype(MXU_DTYPE)
        pb_ref[...] = pb
        pp = _dot(pb, wp_ref[...])
        h3 = h + gate * pp
        r4 = lax.rsqrt(jnp.mean(h3 * h3, axis=-1, keepdims=True) + EPS)
        xh4 = h3 * r4
        e = xh4 * g_fin_v - t_ref[...]
        loss = 0.5 * jnp.sum(jnp.mean(e * e, axis=-1, keepdims=True), axis=0, keepdims=True)
        dy = e * (1.0 / D_MODEL)
        dg_fin = jnp.sum(dy * xh4, axis=0, keepdims=True)
        dxh = dy * g_fin_v
        dh3 = r4 * (dxh - xh4 * jnp.mean(dxh * xh4, axis=-1, keepdims=True))
        dpp_ref[...] = (dh3 * gate).astype(MXU_DTYPE)
        dgp = (dh3 * pp * gate * (1.0 - gate)).astype(MXU_DTYPE)
        dgp_ref[...] = dgp
        dn3 = _dot(dgp, wg_ref[...], NT)
        dx, dg_ple = _rms_bwd(h, g_ple_v, dn3)
        dh2 = dh3 + dx
        dh_ref[...] = dh2
        dhb_ref[...] = dh2.astype(MXU_DTYPE)
        acc_ref[0:1, :] += dg_fin
        acc_ref[1:2, :] += dg_ple
        acc_ref[2:3, :] += jnp.broadcast_to(loss, (1, D_MODEL))

    row = lambda w: pl.BlockSpec((tm, w), lambda i: (i, 0))
    return pl.pallas_call(
        body, name="ple_loss", grid=(t // tm,),
        in_specs=[row(D_MODEL), row(PLE_DIM), row(D_MODEL), _full((1, D_MODEL)), _full((1, D_MODEL)),
                  _full((D_MODEL, D_MODEL)), _full((PLE_DIM, D_MODEL))],
        out_specs=[row(D_MODEL), row(D_MODEL), row(D_MODEL), row(D_MODEL), row(D_MODEL), row(PLE_DIM),
                   _full((8, D_MODEL))],
        out_shape=[jax.ShapeDtypeStruct((t, D_MODEL), F32), jax.ShapeDtypeStruct((t, D_MODEL), MXU_DTYPE),
                   jax.ShapeDtypeStruct((t, D_MODEL), MXU_DTYPE), jax.ShapeDtypeStruct((t, D_MODEL), MXU_DTYPE),
                   jax.ShapeDtypeStruct((t, D_MODEL), MXU_DTYPE), jax.ShapeDtypeStruct((t, PLE_DIM), MXU_DTYPE),
                   jax.ShapeDtypeStruct((8, D_MODEL), F32)],
        compiler_params=_params("arbitrary"),
    )(h2, p, tgt, g_ple, g_fin, w_gate, w_proj)


def _mlp_bwd(dh2, dh2b, a, h1, g_mlp, w_up4, w_down, tm):
    t = h1.shape[0]

    def body(dh_ref, dhb_ref, a_ref, h_ref, g_ref, wu_ref, wd_ref,
             s_ref, da_ref, dh1_ref, dh1b_ref, acc_ref, dm_ref):
        i, k = pl.program_id(0), pl.program_id(1)

        @pl.when((i == 0) & (k == 0))
        def _():
            acc_ref[...] = jnp.zeros_like(acc_ref)

        @pl.when(k == 0)
        def _():
            dm_ref[...] = jnp.zeros_like(dm_ref)

        ds = _dot(dhb_ref[...], wd_ref[...], NT)
        r = jnp.maximum(a_ref[...], 0.0)
        s_ref[...] = (r * r).astype(MXU_DTYPE)
        da = (ds * (2.0 * r)).astype(MXU_DTYPE)
        da_ref[...] = da
        dm_ref[...] += _dot(da, wu_ref[...], NT)

        @pl.when(k == FF_BLOCKS - 1)
        def _():
            dx, dg = _rms_bwd(h_ref[...], g_ref[...], dm_ref[...])
            dh1 = dh_ref[...] + dx
            dh1_ref[...] = dh1
            dh1b_ref[...] = dh1.astype(MXU_DTYPE)
            acc_ref[0:1, :] += dg

    tok = lambda w: pl.BlockSpec((tm, w), lambda i, k: (i, 0))
    return pl.pallas_call(
        body, name="mlp_bwd", grid=(t // tm, FF_BLOCKS),
        in_specs=[tok(D_MODEL), tok(D_MODEL), pl.BlockSpec((tm, FF_BLOCK), lambda i, k: (i, k)), tok(D_MODEL),
                  _full((1, D_MODEL)), pl.BlockSpec((None, D_MODEL, FF_BLOCK), lambda i, k: (k, 0, 0)),
                  pl.BlockSpec((FF_BLOCK, D_MODEL), lambda i, k: (k, 0))],
        out_specs=[pl.BlockSpec((tm, FF_BLOCK), lambda i, k: (i, k)), pl.BlockSpec((tm, FF_BLOCK), lambda i, k: (i, k)),
                   tok(D_MODEL), tok(D_MODEL), pl.BlockSpec((8, D_MODEL), lambda i, k: (0, 0))],
        out_shape=[jax.ShapeDtypeStruct((t, D_FF), MXU_DTYPE), jax.ShapeDtypeStruct((t, D_FF), MXU_DTYPE),
                   jax.ShapeDtypeStruct((t, D_MODEL), F32), jax.ShapeDtypeStruct((t, D_MODEL), MXU_DTYPE),
                   jax.ShapeDtypeStruct((8, D_MODEL), F32)],
        scratch_shapes=[pltpu.VMEM((tm, D_MODEL), F32)],
        compiler_params=_params("arbitrary", "arbitrary"),
    )(dh2, dh2b, a, h1, g_mlp, w_up4, w_down)


def _oproj_bwd(dh1b, wo_a, wo_d, tm, dep):
    t = dh1b.shape[0]
    half = wo_a.shape[0]

    def body(d_ref, wa_ref, wd_ref, dep_ref, da_ref, dd_ref):
        d = d_ref[...]
        da_ref[...] = _dot(d, wa_ref[...], NT)
        dd_ref[...] = _dot(d, wd_ref[...], NT)

    return pl.pallas_call(
        body, name="oproj_bwd", grid=(t // tm,),
        in_specs=[pl.BlockSpec((tm, D_MODEL), lambda i: (i, 0)), _full((half, D_MODEL)), _full((half, D_MODEL)),
                  pl.BlockSpec(memory_space=pl.ANY)],
        out_specs=[pl.BlockSpec((tm, half), lambda i: (i, 0)), pl.BlockSpec((tm, half), lambda i: (i, 0))],
        out_shape=[jax.ShapeDtypeStruct((t, half), F32), jax.ShapeDtypeStruct((t, half), F32)],
        compiler_params=_params("parallel"),
    )(dh1b, wo_a, wo_d, dep)


def _inproj_bwd(x, dh1, g_mix, grads, weights, tm):
    t = x.shape[0]
    n = len(grads)

    def body(*refs):
        x_ref, dh_ref, g_ref = refs[:3]
        g_refs, w_refs = refs[3:3 + n], refs[3 + n:3 + 2 * n]
        dx_ref, acc_ref = refs[3 + 2 * n:]

        @pl.when(pl.program_id(0) == 0)
        def _():
            acc_ref[...] = jnp.zeros_like(acc_ref)

        du = _dot(g_refs[0][...], w_refs[0][...], NT)
        for j in range(1, n):
            du += _dot(g_refs[j][...], w_refs[j][...], NT)
        dx, dg = _rms_bwd(x_ref[...], g_ref[...], du)
        dx_ref[...] = dh_ref[...] + dx
        acc_ref[0:1, :] += dg

    tok = lambda w: pl.BlockSpec((tm, w), lambda i: (i, 0))
    return pl.pallas_call(
        body, name="inproj_bwd", grid=(t // tm,),
        in_specs=[tok(D_MODEL), tok(D_MODEL), _full((1, D_MODEL))] + [tok(g.shape[1]) for g in grads]
                 + [_full(w.shape) for w in weights],
        out_specs=[tok(D_MODEL), _full((8, D_MODEL))],
        out_shape=[jax.ShapeDtypeStruct((t, D_MODEL), F32), jax.ShapeDtypeStruct((8, D_MODEL), F32)],
        compiler_params=_params("arbitrary"),
    )(x, dh1, g_mix, *grads, *weights)


def _wgrad(a, b, name, tk, tn, tt, stacked=False):
    t, kdim = a.shape
    ncols = b.shape[1]

    def body(a_ref, b_ref, o_ref):
        @pl.when(pl.program_id(2) == 0)
        def _():
            o_ref[...] = jnp.zeros_like(o_ref)

        o_ref[...] += _dot(a_ref[...], b_ref[...], TN)

    if stacked:
        out_spec = pl.BlockSpec((None, tk, tn), lambda i, j, s: (j, i, 0))
        out_shape = jax.ShapeDtypeStruct((ncols // tn, kdim, tn), F32)
    else:
        out_spec = pl.BlockSpec((tk, tn), lambda i, j, s: (i, j))
        out_shape = jax.ShapeDtypeStruct((kdim, ncols), F32)
    return pl.pallas_call(
        body, name=name, grid=(kdim // tk, ncols // tn, t // tt),
        in_specs=[pl.BlockSpec((tt, tk), lambda i, j, s: (s, i)), pl.BlockSpec((tt, tn), lambda i, j, s: (s, j))],
        out_specs=out_spec, out_shape=out_shape,
        compiler_params=_params("parallel", "parallel", "arbitrary"),
    )(a, b)


def _rope_tables(t):
    half = ATTN_HEAD_DIM // 2
    inv = 1.0 / (ROPE_THETA ** (jnp.arange(half, dtype=F32) * (2.0 / ATTN_HEAD_DIM)))
    ang = jnp.arange(t, dtype=F32)[:, None] * inv[None, :]
    cos, sin = jnp.cos(ang), jnp.sin(ang)
    cos2 = jnp.concatenate([cos, cos], axis=-1)
    sin2 = jnp.concatenate([-sin, sin], axis=-1)
    return jnp.tile(cos2, (1, 2)), jnp.tile(sin2, (1, 2))


def _swap_halves(tv):
    w = tv.shape[-1]
    lane = lax.broadcasted_iota(jnp.int32, tv.shape, tv.ndim - 1)
    first = (lane % ATTN_HEAD_DIM) < (ATTN_HEAD_DIM // 2)
    return jnp.where(first, pltpu.roll(tv, w - ATTN_HEAD_DIM // 2, tv.ndim - 1),
                     pltpu.roll(tv, ATTN_HEAD_DIM // 2, tv.ndim - 1))


def _rope(tv, cos, sin):
    return tv * cos + _swap_halves(tv) * sin


def _rope_bwd(dv, cos, sin):
    return dv * cos + _swap_halves(dv * sin)


def _attn_probs(qh, kwin, sink, first_block):
    s = _dot(qh, kwin, NT) * ATTN_SCALE
    r = lax.broadcasted_iota(jnp.int32, s.shape, 0)
    c = lax.broadcasted_iota(jnp.int32, s.shape, 1)
    valid = (c > r) & (c <= r + ATTN_BLOCK) & ((c >= ATTN_BLOCK) | jnp.logical_not(first_block))
    s = jnp.where(valid, s, -jnp.inf)
    m = jnp.maximum(jnp.max(s, axis=-1, keepdims=True), sink)
    e = jnp.where(valid, jnp.exp(s - m), 0.0)
    es = jnp.exp(sink - m)
    inv = 1.0 / (jnp.sum(e, axis=-1, keepdims=True) + es)
    return e * inv, es * inv


def _lane_scalar(vec, idx):
    lane = lax.broadcasted_iota(jnp.int32, vec.shape, 1)
    return jnp.sum(jnp.where(lane == idx, vec, 0.0), axis=-1, keepdims=True)


def _attn_specs(nb):
    cur = lambda w, cb: pl.BlockSpec((ATTN_BLOCK, w), lambda i: (jnp.minimum(i, nb - 1), cb))
    prev = lambda w, cb: pl.BlockSpec((ATTN_BLOCK, w), lambda i: (jnp.maximum(jnp.minimum(i, nb - 1) - 1, 0), cb))
    kcol, vcol = ATTN_Q // ATTN_KV, ATTN_Q // ATTN_KV + 1
    return [cur(ATTN_Q, 0), cur(ATTN_KV, kcol), prev(ATTN_KV, kcol), cur(ATTN_KV, vcol), prev(ATTN_KV, vcol),
            cur(ATTN_KV, 0), cur(ATTN_KV, 0), prev(ATTN_KV, 0), prev(ATTN_KV, 0), _full((1, 128))]


def _attn_fwd(pa, cos, sin, sinks_vec):
    t = pa.shape[0]
    nb = t // ATTN_BLOCK

    def body(q_ref, kc_ref, kp_ref, vc_ref, vp_ref, cc_ref, sc_ref, cp_ref, sp_ref, sk_ref, o_ref):
        first = pl.program_id(0) == 0
        cc, sc = cc_ref[...], sc_ref[...]
        q = _rope(q_ref[...], jnp.tile(cc, (1, ATTN_Q // ATTN_KV)), jnp.tile(sc, (1, ATTN_Q // ATTN_KV)))
        kc = _rope(kc_ref[...], cc, sc)
        kp = _rope(kp_ref[...], cp_ref[...], sp_ref[...])
        vc, vp = vc_ref[...], vp_ref[...]
        sk = sk_ref[...]
        for hk in range(ATTN_KV_HEADS):
            ks = slice(hk * ATTN_HEAD_DIM, (hk + 1) * ATTN_HEAD_DIM)
            kwin = jnp.concatenate([kp[:, ks], kc[:, ks]], axis=0)
            vwin = jnp.concatenate([vp[:, ks], vc[:, ks]], axis=0)
            for g in range(ATTN_GROUPS):
                h = hk * ATTN_GROUPS + g
                hs = slice(h * ATTN_HEAD_DIM, (h + 1) * ATTN_HEAD_DIM)
                probs, _ = _attn_probs(q[:, hs], kwin, _lane_scalar(sk, h), first)
                o_ref[:, hs] = _dot(probs, vwin)

    return pl.pallas_call(
        body, name="attn_fwd", grid=(nb,),
        in_specs=_attn_specs(nb),
        out_specs=pl.BlockSpec((ATTN_BLOCK, ATTN_Q), lambda i: (i, 0)),
        out_shape=jax.ShapeDtypeStruct((t, ATTN_Q), F32),
        compiler_params=_params("parallel"),
    )(pa, pa, pa, pa, pa, cos, sin, cos, sin, sinks_vec)


def _attn_bwd(pa, cos, sin, sinks_vec, dao):
    t = pa.shape[0]
    nb = t // ATTN_BLOCK

    def body(q_ref, kc_ref, kp_ref, vc_ref, vp_ref, cc_ref, sc_ref, cp_ref, sp_ref, sk_ref, do_ref,
             dq_ref, dk_ref, dv_ref, acc_ref, dqr_ref, dkw_ref, dvw_ref, ck_ref, cv_ref):
        i = pl.program_id(0)

        @pl.when(i == 0)
        def _():
            acc_ref[...] = jnp.zeros_like(acc_ref)
            ck_ref[...] = jnp.zeros_like(ck_ref)
            cv_ref[...] = jnp.zeros_like(cv_ref)

        @pl.when(i < nb)
        def _():
            first = i == 0
            cc, sc = cc_ref[...], sc_ref[...]
            cq, sq = jnp.tile(cc, (1, ATTN_Q // ATTN_KV)), jnp.tile(sc, (1, ATTN_Q // ATTN_KV))
            q = _rope(q_ref[...], cq, sq)
            kc = _rope(kc_ref[...], cc, sc)
            kp = _rope(kp_ref[...], cp_ref[...], sp_ref[...])
            vc, vp = vc_ref[...], vp_ref[...]
            sk = sk_ref[...]
            do = do_ref[...]
            lane = lax.broadcasted_iota(jnp.int32, (1, 128), 1)
            dsink = jnp.zeros((1, 128), F32)
            for hk in range(ATTN_KV_HEADS):
                ks = slice(hk * ATTN_HEAD_DIM, (hk + 1) * ATTN_HEAD_DIM)
                kwin = jnp.concatenate([kp[:, ks], kc[:, ks]], axis=0)
                vwin = jnp.concatenate([vp[:, ks], vc[:, ks]], axis=0)
                dkw = jnp.zeros((2 * ATTN_BLOCK, ATTN_HEAD_DIM), F32)
                dvw = jnp.zeros((2 * ATTN_BLOCK, ATTN_HEAD_DIM), F32)
                for g in range(ATTN_GROUPS):
                    h = hk * ATTN_GROUPS + g
                    hs = slice(h * ATTN_HEAD_DIM, (h + 1) * ATTN_HEAD_DIM)
                    qh = q[:, hs]
                    probs, psink = _attn_probs(qh, kwin, _lane_scalar(sk, h), first)
                    doh = do[:, hs]
                    dp = _dot(doh, vwin, NT)
                    delta = jnp.sum(probs * dp, axis=-1, keepdims=True)
                    ds = probs * (dp - delta) * ATTN_SCALE
                    dqr_ref[:, hs] = _dot(ds, kwin)
                    dkw += _dot(ds, qh, TN)
                    dvw += _dot(probs, doh, TN)
                    dsink += jnp.where(lane == h, jnp.sum(-psink * delta, axis=0, keepdims=True), 0.0)
                dkw_ref[:, ks] = dkw
                dvw_ref[:, ks] = dvw
            acc_ref[0:1, :] += dsink
            dq_ref[...] = _rope_bwd(dqr_ref[...], cq, sq).astype(dq_ref.dtype)
            dk_ref[...] = (ck_ref[...] + _rope_bwd(dkw_ref[0:ATTN_BLOCK, :], cp_ref[...], sp_ref[...])).astype(dk_ref.dtype)
            dv_ref[...] = (cv_ref[...] + dvw_ref[0:ATTN_BLOCK, :]).astype(dv_ref.dtype)
            ck_ref[...] = _rope_bwd(dkw_ref[ATTN_BLOCK:2 * ATTN_BLOCK, :], cc, sc)
            cv_ref[...] = dvw_ref[ATTN_BLOCK:2 * ATTN_BLOCK, :]

        @pl.when(i == nb)
        def _():
            dk_ref[...] = ck_ref[...].astype(dk_ref.dtype)
            dv_ref[...] = cv_ref[...].astype(dv_ref.dtype)

    prev_out = lambda w: pl.BlockSpec((ATTN_BLOCK, w), lambda i: (jnp.maximum(i - 1, 0), 0))
    return pl.pallas_call(
        body, name="attn_bwd", grid=(nb + 1,),
        in_specs=_attn_specs(nb) + [pl.BlockSpec((ATTN_BLOCK, ATTN_Q), lambda i: (jnp.minimum(i, nb - 1), 0))],
        out_specs=[pl.BlockSpec((ATTN_BLOCK, ATTN_Q), lambda i: (jnp.minimum(i, nb - 1), 0)), prev_out(ATTN_KV),
                   prev_out(ATTN_KV), _full((8, 128))],
        out_shape=[jax.ShapeDtypeStruct((t, ATTN_Q), MXU_DTYPE), jax.ShapeDtypeStruct((t, ATTN_KV), MXU_DTYPE),
                   jax.ShapeDtypeStruct((t, ATTN_KV), MXU_DTYPE), jax.ShapeDtypeStruct((8, 128), F32)],
        scratch_shapes=[pltpu.VMEM((ATTN_BLOCK, ATTN_Q), F32), pltpu.VMEM((2 * ATTN_BLOCK, ATTN_KV), F32),
                        pltpu.VMEM((2 * ATTN_BLOCK, ATTN_KV), F32), pltpu.VMEM((ATTN_BLOCK, ATTN_KV), F32),
                        pltpu.VMEM((ATTN_BLOCK, ATTN_KV), F32)],
        compiler_params=_params("arbitrary"),
    )(pa, pa, pa, pa, pa, cos, sin, cos, sin, sinks_vec, dao)


PAIR = 2 * DN_CHUNK
HALO = 8


def _conv_window(cur_ref, prev_ref, xs_ref, tm):
    prev = jnp.where(pl.program_id(0) > 0, prev_ref[...], 0.0)
    xs_ref[0:HALO, :] = prev
    xs_ref[HALO:HALO + tm, :] = cur_ref[...]


def _conv_taps(xs_ref, cw_ref, tm):
    y = cw_ref[0:1, :] * xs_ref[pl.ds(HALO - DN_CONV + 1, tm), :]
    for j in range(1, DN_CONV):
        y += cw_ref[j:j + 1, :] * xs_ref[pl.ds(HALO - DN_CONV + 1 + j, tm), :]
    return y


def _gate_values(ba, al, dt):
    beta = _sigmoid(ba)
    pre = ba + dt
    g = -jnp.exp(al) * _softplus(pre)
    return beta, g, pre


def _dn_prep_specs(tm, t):
    return [pl.BlockSpec((tm, CONV_CH), lambda i: (i, 0)),
            pl.BlockSpec((HALO, CONV_CH), lambda i: (jnp.maximum(i * (tm // HALO) - 1, 0), 0)),
            pl.BlockSpec((tm, 128), lambda i: (i, 4 * DN_W // 128)),
            _full((DN_CONV, CONV_CH)), _full((1, 128)), _full((1, 128))]


def _dn_prep(pd, conv_w, al_vec, dt_vec, tm):
    t = pd.shape[0]

    def body(cur_ref, prev_ref, ba_ref, cw_ref, al_ref, dt_ref, qn_ref, kn_ref, vc_ref, gc_ref, gr_ref, xs_ref):
        _conv_window(cur_ref, prev_ref, xs_ref, tm)
        y = _conv_taps(xs_ref, cw_ref, tm)
        c = y * _sigmoid(y)
        for h in range(DN_HEADS):
            qs = slice(h * DN_HEAD_DIM, (h + 1) * DN_HEAD_DIM)
            ksl = slice(DN_W + h * DN_HEAD_DIM, DN_W + (h + 1) * DN_HEAD_DIM)
            qh, kh = c[:, qs], c[:, ksl]
            qn_ref[:, qs] = qh * lax.rsqrt(jnp.sum(qh * qh, axis=-1, keepdims=True) + EPS) * DN_SCALE
            kn_ref[:, qs] = kh * lax.rsqrt(jnp.sum(kh * kh, axis=-1, keepdims=True) + EPS)
        vc_ref[...] = c[:, 2 * DN_W:3 * DN_W]
        beta, g, _ = _gate_values(ba_ref[...], al_ref[...], dt_ref[...])
        lane = lax.broadcasted_iota(jnp.int32, beta.shape, 1)
        gb = jnp.where(lane < DN_HEADS, beta, jnp.where(lane < 2 * DN_HEADS, g, 0.0))
        gc_ref[...] = gb
        gr_ref[...] = gb.T[0:8, :]

    tok = lambda w: pl.BlockSpec((tm, w), lambda i: (i, 0))
    return pl.pallas_call(
        body, name="dn_prep", grid=(t // tm,),
        in_specs=_dn_prep_specs(tm, t),
        out_specs=[tok(DN_W), tok(DN_W), tok(DN_W), tok(128), pl.BlockSpec((8, tm), lambda i: (0, i))],
        out_shape=[jax.ShapeDtypeStruct((t, DN_W), F32)] * 3 + [jax.ShapeDtypeStruct((t, 128), F32),
                                                                 jax.ShapeDtypeStruct((8, t), F32)],
        scratch_shapes=[pltpu.VMEM((HALO + tm, CONV_CH), F32)],
        compiler_params=_params("parallel"),
    )(pd, pd, pd, conv_w, al_vec, dt_vec)


def _pair_masks():
    r = lax.broadcasted_iota(jnp.int32, (PAIR, PAIR), 0)
    c = lax.broadcasted_iota(jnp.int32, (PAIR, PAIR), 1)
    same = (r < DN_CHUNK) == (c < DN_CHUNK)
    return same & (r >= c), same & (r > c)


def _lane_col(mat, idx):
    lane = lax.broadcasted_iota(jnp.int32, mat.shape, 1)
    return jnp.sum(jnp.where(lane == idx, mat, 0.0), axis=-1, keepdims=True)


def _pair_cumsums(gc, gr, low):
    lowf = low.astype(F32)
    return _dot(lowf, gc, NN, HI), _dot(gr, lowf, NT, HI)


def _pair_gates(gc, cum_c, cum_r, low, h):
    beta = _lane_col(gc, h)
    gam = _lane_col(cum_c, DN_HEADS + h)
    gam_row = cum_r[DN_HEADS + h:DN_HEADS + h + 1, :]
    dm = jnp.where(low, jnp.exp(jnp.where(low, gam - gam_row, 0.0)), 0.0)
    row = lax.broadcasted_iota(jnp.int32, gam.shape, 0)
    gl = jnp.where(row < DN_CHUNK, gam[DN_CHUNK - 1:DN_CHUNK, :], gam[PAIR - 1:PAIR, :])
    return beta, gam, dm, gl


def _split(a):
    hi = a.astype(BF16)
    return hi, (a - hi.astype(F32)).astype(BF16)


def _dot_split(a, b, dims=NN):
    (ah, al), (bh, bl) = a, b
    la, lb = (1, 1) if dims == TN else ((0, 1) if dims == NN else (0, 0))
    r = _dot(jnp.concatenate([ah, al], axis=la), jnp.concatenate([bh, bl], axis=lb), dims)
    m, n = r.shape[0] // 2, r.shape[1] // 2
    return (r[m:, n:] + (r[:m, n:] + r[m:, :n])) + r[:m, :n]


def _unit_lower_inverses(lmats):
    n = lmats[0].shape[0]
    eye = (lax.broadcasted_iota(jnp.int32, (n, n), 0) == lax.broadcasted_iota(jnp.int32, (n, n), 1)).astype(F32)
    accs = [eye - l for l in lmats]
    splits = [_split(l) for l in lmats]
    step = 1
    while 2 * step < DN_CHUNK:
        splits = [_split(_dot_split(s, s)) for s in splits]
        accs = [acc + _dot_split(_split(acc), s) for acc, s in zip(accs, splits)]
        step *= 2
    return accs


def _dn_intra(qn, kn, vc, gc, gr):
    t = qn.shape[0]
    npair = t // PAIR

    def body(q_ref, k_ref, v_ref, gc_ref, gr_ref, u_ref, w_ref, qg_ref, kd_ref, a_ref, ti_ref, dl_ref):
        gc_v = gc_ref[...]
        low, strict = _pair_masks()
        cum_c, cum_r = _pair_cumsums(gc_v, gr_ref[...], low)
        heads = [slice(h * DN_HEAD_DIM, (h + 1) * DN_HEAD_DIM) for h in range(DN_HEADS)]
        gates = [_pair_gates(gc_v, cum_c, cum_r, low, h) for h in range(DN_HEADS)]
        lmats = []
        for hs, (beta, gam, dm, gl) in zip(heads, gates):
            k = k_ref[:, hs]
            lmats.append(jnp.where(strict, _dot(k * beta, k, NT) * dm, 0.0))
        tinvs = _unit_lower_inverses(lmats)
        for h, (hs, (beta, gam, dm, gl), tinv) in enumerate(zip(heads, gates, tinvs)):
            q, k, v = q_ref[:, hs], k_ref[:, hs], v_ref[:, hs]
            eg = jnp.exp(gam)
            u_ref[:, hs] = _dot(tinv, v * beta)
            w_ref[:, hs] = _dot(tinv, (k * beta) * eg)
            a_ref[h] = _dot(q, k, NT) * dm
            ti_ref[h] = tinv
            qg_ref[:, hs] = q * eg
            kd_ref[:, hs] = k * jnp.exp(gl - gam)
            for c in range(2):
                last = (c + 1) * DN_CHUNK - 1
                dl_ref[c, h] = jnp.broadcast_to(jnp.exp(gam[last:last + 1, :]), (8, 128))

    tok = lambda w: pl.BlockSpec((PAIR, w), lambda n: (n, 0))
    hm = pl.BlockSpec((DN_HEADS, PAIR, PAIR), lambda n: (0, n, 0))
    return pl.pallas_call(
        body, name="dn_intra", grid=(npair,),
        in_specs=[tok(DN_W), tok(DN_W), tok(DN_W), tok(128), pl.BlockSpec((8, PAIR), lambda n: (0, n))],
        out_specs=[tok(DN_W)] * 4 + [hm, hm, pl.BlockSpec((2, DN_HEADS, 8, 128), lambda n: (n, 0, 0, 0))],
        out_shape=[jax.ShapeDtypeStruct((t, DN_W), F32)] * 4 + [jax.ShapeDtypeStruct((DN_HEADS, t, PAIR), F32)] * 2
                  + [jax.ShapeDtypeStruct((2 * npair, DN_HEADS, 8, 128), F32)],
        compiler_params=_params("parallel"),
    )(qn, kn, vc, gc, gr)


def _dn_scan_fwd(u, w, qg, kd, a_qk, dlast, pd, dn_w):
    t = u.shape[0]
    npair = t // PAIR

    def body(u_ref, w_ref, qg_ref, kd_ref, a_ref, dl_ref, z_ref, nw_ref, out_ref, o_ref, vn_ref, sall_ref, s_ref):
        @pl.when(pl.program_id(0) == 0)
        def _():
            s_ref[...] = jnp.zeros_like(s_ref)

        nw = nw_ref[...]
        for c in range(2):
            rows = slice(c * DN_CHUNK, (c + 1) * DN_CHUNK)
            for h in range(DN_HEADS):
                hs = slice(h * DN_HEAD_DIM, (h + 1) * DN_HEAD_DIM)
                st = s_ref[h]
                sall_ref[c, h] = st
                vn_ref[rows, hs] = u_ref[rows, hs] - _dot(w_ref[rows, hs], st)
            for h in range(DN_HEADS):
                hs = slice(h * DN_HEAD_DIM, (h + 1) * DN_HEAD_DIM)
                st, vn = s_ref[h], vn_ref[rows, hs]
                o = _dot(qg_ref[rows, hs], st) + _dot(a_ref[h, rows, rows], vn)
                s_ref[h] = st * dl_ref[c, h][0:1, :] + _dot(kd_ref[rows, hs], vn, TN)
                o_ref[rows, hs] = o
                z = z_ref[rows, hs]
                on = o * lax.rsqrt(jnp.mean(o * o, axis=-1, keepdims=True) + EPS) * nw
                out_ref[rows, hs] = on * (z * _sigmoid(z))

    tok = pl.BlockSpec((PAIR, DN_W), lambda n: (n, 0))
    hm = pl.BlockSpec((DN_HEADS, PAIR, PAIR), lambda n: (0, n, 0))
    return pl.pallas_call(
        body, name="dn_scan_fwd", grid=(npair,),
        in_specs=[tok, tok, tok, tok, hm, pl.BlockSpec((2, DN_HEADS, 8, 128), lambda n: (n, 0, 0, 0)),
                  pl.BlockSpec((PAIR, DN_W), lambda n: (n, 3)), _full((1, 128))],
        out_specs=[tok, tok, tok, pl.BlockSpec((2, DN_HEADS, DN_HEAD_DIM, DN_HEAD_DIM), lambda n: (n, 0, 0, 0))],
        out_shape=[jax.ShapeDtypeStruct((t, DN_W), F32)] * 3
                  + [jax.ShapeDtypeStruct((2 * npair, DN_HEADS, DN_HEAD_DIM, DN_HEAD_DIM), F32)],
        scratch_shapes=[pltpu.VMEM((DN_HEADS, DN_HEAD_DIM, DN_HEAD_DIM), F32)],
        compiler_params=_params("arbitrary"),
    )(u, w, qg, kd, a_qk, dlast, pd, dn_w)


def _dn_scan_bwd(dout, o, vnew, sall, w, qg, kd, a_qk, dlast, pd, dn_w):
    t = o.shape[0]
    npair = t // PAIR
    rev = lambda n: npair - 1 - n

    def body(do_ref, o_ref, vn_ref, sall_ref, w_ref, qg_ref, kd_ref, a_ref, dl_ref, z_ref, nw_ref,
             dz_ref, du_ref, dw_ref, dqg_ref, dkd_ref, da_ref, ddl_ref, acc_ref, ds_ref, dos_ref):
        @pl.when(pl.program_id(0) == 0)
        def _():
            ds_ref[...] = jnp.zeros_like(ds_ref)
            acc_ref[...] = jnp.zeros_like(acc_ref)

        nw = nw_ref[...]
        dnw = jnp.zeros((1, 128), F32)
        for h in range(DN_HEADS):
            hs = slice(h * DN_HEAD_DIM, (h + 1) * DN_HEAD_DIM)
            o, z, dout = o_ref[:, hs], z_ref[:, hs], do_ref[:, hs]
            r = lax.rsqrt(jnp.mean(o * o, axis=-1, keepdims=True) + EPS)
            oh = o * r
            sz = _sigmoid(z)
            dz_ref[:, hs] = dout * (oh * nw) * (sz + z * sz * (1.0 - sz))
            don = dout * (z * sz)
            dnw += jnp.sum(don * oh, axis=0, keepdims=True)
            doh = don * nw
            dos_ref[:, hs] = r * (doh - oh * jnp.mean(doh * oh, axis=-1, keepdims=True))
        acc_ref[0:1, :] += dnw
        for c in (1, 0):
            rows = slice(c * DN_CHUNK, (c + 1) * DN_CHUNK)
            other = slice((1 - c) * DN_CHUNK, (2 - c) * DN_CHUNK)
            for h in range(DN_HEADS):
                hs = slice(h * DN_HEAD_DIM, (h + 1) * DN_HEAD_DIM)
                do, st, dsp, vn = dos_ref[rows, hs], sall_ref[c, h], ds_ref[h], vn_ref[rows, hs]
                da_ref[h, rows, rows] = _dot(do, vn, NT)
                da_ref[h, rows, other] = jnp.zeros((DN_CHUNK, DN_CHUNK), F32)
                du_ref[rows, hs] = _dot(a_ref[h, rows, rows], do, TN) + _dot(kd_ref[rows, hs], dsp)
                dqg_ref[rows, hs] = _dot(do, st, NT)
                dkd_ref[rows, hs] = _dot(vn, dsp, NT)
                ddl = jnp.sum(jnp.sum(dsp * st, axis=1, keepdims=True), axis=0, keepdims=True)
                ddl_ref[c, h] = jnp.broadcast_to(ddl, (8, 128))
            for h in range(DN_HEADS):
                hs = slice(h * DN_HEAD_DIM, (h + 1) * DN_HEAD_DIM)
                do, st, dvn = dos_ref[rows, hs], sall_ref[c, h], du_ref[rows, hs]
                dw_ref[rows, hs] = -_dot(dvn, st, NT)
                ds_ref[h] = (ds_ref[h] * dl_ref[c, h][0:1, :] + _dot(qg_ref[rows, hs], do, TN)
                             - _dot(w_ref[rows, hs], dvn, TN))

    tok = pl.BlockSpec((PAIR, DN_W), lambda n: (rev(n), 0))
    hm = pl.BlockSpec((DN_HEADS, PAIR, PAIR), lambda n: (0, rev(n), 0))
    sc = pl.BlockSpec((2, DN_HEADS, 8, 128), lambda n: (rev(n), 0, 0, 0))
    return pl.pallas_call(
        body, name="dn_scan_bwd", grid=(npair,),
        in_specs=[tok, tok, tok, pl.BlockSpec((2, DN_HEADS, DN_HEAD_DIM, DN_HEAD_DIM), lambda n: (rev(n), 0, 0, 0)),
                  tok, tok, tok, hm, sc, pl.BlockSpec((PAIR, DN_W), lambda n: (rev(n), 3)), _full((1, 128))],
        out_specs=[tok] * 5 + [hm, sc, _full((8, 128))],
        out_shape=[jax.ShapeDtypeStruct((t, DN_W), F32)] * 5 + [jax.ShapeDtypeStruct((DN_HEADS, t, PAIR), F32),
                   jax.ShapeDtypeStruct((2 * npair, DN_HEADS, 8, 128), F32), jax.ShapeDtypeStruct((8, 128), F32)],
        scratch_shapes=[pltpu.VMEM((DN_HEADS, DN_HEAD_DIM, DN_HEAD_DIM), F32), pltpu.VMEM((PAIR, DN_W), F32)],
        compiler_params=_params("arbitrary"),
    )(dout, o, vnew, sall, w, qg, kd, a_qk, dlast, pd, dn_w)


def _dn_intra_bwd(qn, kn, vc, gc, gr, tinv, a_qk, du, dw, dqg, dkd, da_qk, ddlast, dlast, dep):
    t = qn.shape[0]
    npair = t // PAIR

    def body(q_ref, k_ref, v_ref, gc_ref, gr_ref, ti_ref, a_ref, du_ref, dw_ref, dqg_ref, dkd_ref, da_ref, ddl_ref, dl_ref,
             dep_ref, dq_ref, dk_ref, dv_ref, dg_ref):
        gc_v = gc_ref[...]
        low, strict = _pair_masks()
        cum_c, cum_r = _pair_cumsums(gc_v, gr_ref[...], low)
        lane = lax.broadcasted_iota(jnp.int32, (PAIR, 128), 1)
        rowi = lax.broadcasted_iota(jnp.int32, (PAIR, 1), 0)
        rsum = lambda v: jnp.sum(v, axis=-1, keepdims=True)
        dgam_all = jnp.zeros((PAIR, 128), F32)
        dbeta_all = jnp.zeros((PAIR, 128), F32)
        heads = [slice(h * DN_HEAD_DIM, (h + 1) * DN_HEAD_DIM) for h in range(DN_HEADS)]
        gates = [_pair_gates(gc_v, cum_c, cum_r, low, h) for h in range(DN_HEADS)]
        tsplits, dtis = [], []
        for h, (hs, (beta, gam, dm, gl)) in enumerate(zip(heads, gates)):
            k = k_ref[:, hs]
            tsplits.append(_split(ti_ref[h]))
            dtis.append(_dot(du_ref[:, hs], v_ref[:, hs] * beta, NT)
                        + _dot(dw_ref[:, hs], (k * beta) * jnp.exp(gam), NT))
        xs = [_dot_split(ts, _split(dti), TN) for ts, dti in zip(tsplits, dtis)]
        dls = [jnp.where(strict, -_dot_split(_split(x), ts, NT), 0.0) for x, ts in zip(xs, tsplits)]
        for h, (hs, (beta, gam, dm, gl), dl) in enumerate(zip(heads, gates, dls)):
            q, k, v = q_ref[:, hs], k_ref[:, hs], v_ref[:, hs]
            tinv, a = ti_ref[h], a_ref[h]
            du, dw, dqg, dkd = du_ref[:, hs], dw_ref[:, hs], dqg_ref[:, hs], dkd_ref[:, hs]
            kb = k * beta
            eg = jnp.exp(gam)
            ekd = jnp.exp(gl - gam)
            kbg = kb * eg
            lmat = jnp.where(strict, _dot(kb, k, NT) * dm, 0.0)
            dvb = _dot(tinv, du, TN)
            dkbg = _dot(tinv, dw, TN)
            dmm = dl * dm
            dam = jnp.where(low, da_ref[h], 0.0)
            dn = dam * dm
            e = dl * lmat + dam * a
            dkb = _dot(dmm, k) + dkbg * eg
            dk_ref[:, hs] = _dot(dmm, kb, TN) + _dot(dn, q, TN) + dkd * ekd + dkb * beta
            dq_ref[:, hs] = _dot(dn, k) + dqg * eg
            dv_ref[:, hs] = dvb * beta
            t_kd = rsum(dkd * (k * ekd))
            dgam = rsum(e) - rsum(e.T) + rsum(dqg * (q * eg)) + rsum(dkbg * kbg) - t_kd
            for c in range(2):
                rows = slice(c * DN_CHUNK, (c + 1) * DN_CHUNK)
                dgl = (jnp.sum(t_kd[rows, :], axis=0, keepdims=True)
                       + ddl_ref[c, h][0:1, 0:1] * dl_ref[c, h][0:1, 0:1])
                dgam = dgam + jnp.where(rowi == (c + 1) * DN_CHUNK - 1, dgl, 0.0)
            dgam_all += jnp.where(lane == DN_HEADS + h, dgam, 0.0)
            dbeta_all += jnp.where(lane == h, rsum(dkb * k) + rsum(dvb * v), 0.0)
        dg_ref[...] = dbeta_all + _dot(low.astype(F32), dgam_all, TN, HI)

    tok = lambda w: pl.BlockSpec((PAIR, w), lambda n: (n, 0))
    hm = pl.BlockSpec((DN_HEADS, PAIR, PAIR), lambda n: (0, n, 0))
    sc = pl.BlockSpec((2, DN_HEADS, 8, 128), lambda n: (n, 0, 0, 0))
    return pl.pallas_call(
        body, name="dn_intra_bwd", grid=(npair,),
        in_specs=[tok(DN_W), tok(DN_W), tok(DN_W), tok(128), pl.BlockSpec((8, PAIR), lambda n: (0, n)), hm, hm,
                  tok(DN_W), tok(DN_W), tok(DN_W), tok(DN_W), hm, sc, sc, pl.BlockSpec(memory_space=pl.ANY)],
        out_specs=[tok(DN_W), tok(DN_W), tok(DN_W), tok(128)],
        out_shape=[jax.ShapeDtypeStruct((t, DN_W), F32)] * 3 + [jax.ShapeDtypeStruct((t, 128), F32)],
        compiler_params=_params("parallel"),
    )(qn, kn, vc, gc, gr, tinv, a_qk, du, dw, dqg, dkd, da_qk, ddlast, dlast, dep)


def _dn_prep_bwd(pd, conv_w, al_vec, dt_vec, dqn, dkn, dvc, dgc, tm):
    t = pd.shape[0]

    def body(cur_ref, prev_ref, ba_ref, cw_ref, al_ref, dt_ref, dq_ref, dk_ref, dv_ref, dg_ref,
             dy_ref, dba_ref, accw_ref, accg_ref, xs_ref, dc_ref):
        @pl.when(pl.program_id(0) == 0)
        def _():
            accw_ref[...] = jnp.zeros_like(accw_ref)
            accg_ref[...] = jnp.zeros_like(accg_ref)

        _conv_window(cur_ref, prev_ref, xs_ref, tm)
        y = _conv_taps(xs_ref, cw_ref, tm)
        sg = _sigmoid(y)
        c = y * sg
        for h in range(DN_HEADS):
            qs = slice(h * DN_HEAD_DIM, (h + 1) * DN_HEAD_DIM)
            ksl = slice(DN_W + h * DN_HEAD_DIM, DN_W + (h + 1) * DN_HEAD_DIM)
            for src, sl, scale in ((dq_ref, qs, DN_SCALE), (dk_ref, ksl, 1.0)):
                xh = c[:, sl]
                r = lax.rsqrt(jnp.sum(xh * xh, axis=-1, keepdims=True) + EPS)
                unit = xh * r
                dn = src[:, qs] * scale
                dc_ref[:, sl] = r * (dn - unit * jnp.sum(dn * unit, axis=-1, keepdims=True))
        dc_ref[:, 2 * DN_W:3 * DN_W] = dv_ref[...]
        dy = dc_ref[...] * (sg + y * sg * (1.0 - sg))
        dy_ref[...] = dy
        for j in range(DN_CONV):
            accw_ref[j:j + 1, :] += jnp.sum(dy * xs_ref[pl.ds(HALO - DN_CONV + 1 + j, tm), :], axis=0, keepdims=True)

        beta, g, pre = _gate_values(ba_ref[...], al_ref[...], dt_ref[...])
        dgb = dg_ref[...]
        lane = lax.broadcasted_iota(jnp.int32, dgb.shape, 1)
        is_b, is_a = lane < DN_HEADS, (lane >= DN_HEADS) & (lane < 2 * DN_HEADS)
        dpre = dgb * (-jnp.exp(al_ref[...])) * _sigmoid(pre)
        dba_ref[...] = jnp.where(is_b, dgb * beta * (1.0 - beta), jnp.where(is_a, dpre, 0.0))
        accg_ref[0:1, :] += jnp.sum(jnp.where(is_a, dgb * g, 0.0), axis=0, keepdims=True)
        accg_ref[1:2, :] += jnp.sum(jnp.where(is_a, dpre, 0.0), axis=0, keepdims=True)

    tok = lambda w: pl.BlockSpec((tm, w), lambda i: (i, 0))
    return pl.pallas_call(
        body, name="dn_prep_bwd", grid=(t // tm,),
        in_specs=_dn_prep_specs(tm, t) + [tok(DN_W), tok(DN_W), tok(DN_W), tok(128)],
        out_specs=[tok(CONV_CH), tok(128), _full((8, CONV_CH)), _full((8, 128))],
        out_shape=[jax.ShapeDtypeStruct((t, CONV_CH), F32), jax.ShapeDtypeStruct((t, 128), F32),
                   jax.ShapeDtypeStruct((8, CONV_CH), F32), jax.ShapeDtypeStruct((8, 128), F32)],
        scratch_shapes=[pltpu.VMEM((HALO + tm, CONV_CH), F32), pltpu.VMEM((tm, CONV_CH), F32)],
        compiler_params=_params("arbitrary"),
    )(pd, pd, pd, conv_w, al_vec, dt_vec, dqn, dkn, dvc, dgc)


def _dn_conv_bwd(dy, dz, dba, conv_w, tm):
    t = dy.shape[0]
    nt = t // tm

    def body(cur_ref, nxt_ref, dz_ref, dba_ref, cw_ref, o_ref, ds_ref):
        nxt = jnp.where(pl.program_id(0) < nt - 1, nxt_ref[...], 0.0)
        ds_ref[0:tm, :] = cur_ref[...]
        ds_ref[tm:tm + HALO, :] = nxt
        dx = cw_ref[0:1, :] * ds_ref[pl.ds(DN_CONV - 1, tm), :]
        for j in range(1, DN_CONV):
            dx += cw_ref[j:j + 1, :] * ds_ref[pl.ds(DN_CONV - 1 - j, tm), :]
        o_ref[:, 0:CONV_CH] = dx.astype(o_ref.dtype)
        o_ref[:, CONV_CH:CONV_CH + DN_W] = dz_ref[...].astype(o_ref.dtype)
        o_ref[:, CONV_CH + DN_W:DN_COLS] = dba_ref[...].astype(o_ref.dtype)

    tok = lambda w: pl.BlockSpec((tm, w), lambda i: (i, 0))
    return pl.pallas_call(
        body, name="dn_conv_bwd", grid=(nt,),
        in_specs=[tok(CONV_CH),
                  pl.BlockSpec((HALO, CONV_CH), lambda i: (jnp.minimum((i + 1) * (tm // HALO), t // HALO - 1), 0)),
                  tok(DN_W), tok(128), _full((DN_CONV, CONV_CH))],
        out_specs=tok(DN_COLS),
        out_shape=jax.ShapeDtypeStruct((t, DN_COLS), MXU_DTYPE),
        scratch_shapes=[pltpu.VMEM((tm + HALO, CONV_CH), F32)],
        compiler_params=_params("parallel"),
    )(dy, dy, dz, dba, conv_w)


def _pad_lanes(v, offset=0):
    return jnp.zeros((1, 128), F32).at[0, offset:offset + v.shape[0]].set(v.astype(F32))


class _LocalReducer:
    def start(self, grads):
        return jnp.zeros((8, 128), F32)

    def middle(self, after):
        return jnp.zeros((8, 128), F32)

    def finish(self, after):
        return None


def _local_step(x, p, tgt, sm, w, late, reducer):
    t = x.shape[0]
    tm = min(512, t // 2)
    tm_s = min(256, t // 2)

    w_in = w["w_in"]
    wa = w_in[:, :ATTN_Q + 2 * ATTN_KV]
    wd = jnp.pad(w_in[:, ATTN_Q + 2 * ATTN_KV:], ((0, 0), (0, DN_COLS - (D_IN - ATTN_Q - 2 * ATTN_KV))))
    conv_w = w["conv_w"]
    al_vec, dt_vec = _pad_lanes(sm["a_log"], DN_HEADS), _pad_lanes(sm["dt_bias"], DN_HEADS)
    sinks_vec = _pad_lanes(sm["sinks"])
    dn_w = sm["dn_norm"].reshape(1, 128)
    row = lambda v: v.reshape(1, D_MODEL)
    cos, sin = _rope_tables(t)

    u, pa, pd = _inproj(x, row(sm["norm_mix"]), wa, wd, tm_s)
    ao = _attn_fwd(pa, cos, sin, sinks_vec)
    qn, kn, vc, gc, gr = _dn_prep(pd, conv_w, al_vec, dt_vec, tm_s)
    uu, ww, qg, kd, a_qk, tinv, dlast = _dn_intra(qn, kn, vc, gc, gr)
    dn_out, o, vnew, sall = _dn_scan_fwd(uu, ww, qg, kd, a_qk, dlast, pd, dn_w)
    w = dict(w, **late(dn_out))
    wo_a, wo_d = w["w_o"][:ATTN_Q], w["w_o"][ATTN_Q:]
    w_proj = jnp.transpose(w["w_proj4"], (1, 0, 2)).reshape(PLE_DIM, D_MODEL)
    h1 = _oproj(x, ao, dn_out, wo_a, wo_d, tm)
    m, a, h2 = _mlp_fwd(h1, row(sm["norm_mlp"]), w["w_up4"], w["w_down"], tm)
    dh2, dh2b, dgp, dpp, n3, pb, acc_ple = _ple_loss(h2, p, tgt, row(sm["norm_ple"]), row(sm["norm_final"]),
                                                     w["w_gate"], w_proj, tm_s)
    g_w_gate = _wgrad(n3, dgp, "wgrad_gate", D_MODEL, D_MODEL, tm)
    g_w_proj = _wgrad(pb, dpp, "wgrad_proj", PLE_DIM, D_MODEL, tm)
    s, da, dh1, dh1b, acc_mlp = _mlp_bwd(dh2, dh2b, a, h1, row(sm["norm_mlp"]), w["w_up4"], w["w_down"], tm)
    g_w_up4 = _wgrad(m, da, "wgrad_up", D_MODEL, FF_BLOCK, tm, stacked=True)
    g_w_down = _wgrad(s, dh2b, "wgrad_down", FF_BLOCK, D_MODEL, tm)
    early = dict(w_up4=g_w_up4, w_down=g_w_down, w_gate=g_w_gate, w_proj=g_w_proj)
    dep = reducer.start(early)
    dao, ddn = _oproj_bwd(dh1b, wo_a, wo_d, tm, dep)
    dz, du, dw, dqg, dkd, da_qk, ddlast, acc_dn = _dn_scan_bwd(ddn, o, vnew, sall, ww, qg, kd, a_qk, dlast, pd, dn_w)
    dep = reducer.middle(du)
    dqn, dkn, dvc, dgc = _dn_intra_bwd(qn, kn, vc, gc, gr, tinv, a_qk, du, dw, dqg, dkd, da_qk, ddlast, dlast, dep)
    dy, dba, acc_conv, acc_gate = _dn_prep_bwd(pd, conv_w, al_vec, dt_vec, dqn, dkn, dvc, dgc, tm_s)
    d_dn = _dn_conv_bwd(dy, dz, dba, conv_w, tm_s)
    dq, dk, dv, acc_attn = _attn_bwd(pa, cos, sin, sinks_vec, dao)
    reducer.finish(dq)
    wq, wk, wv = wa[:, :ATTN_Q], wa[:, ATTN_Q:ATTN_Q + ATTN_KV], wa[:, ATTN_Q + ATTN_KV:]
    dx, acc_mix = _inproj_bwd(x, dh1, row(sm["norm_mix"]), [dq, dk, dv, d_dn], [wq, wk, wv, wd], tm_s)

    aob, dnb = ao.astype(MXU_DTYPE), dn_out.astype(MXU_DTYPE)
    g_w_in = jnp.concatenate([
        _wgrad(u, dq, "wgrad_q", D_MODEL, ATTN_Q, tm), _wgrad(u, dk, "wgrad_k", D_MODEL, ATTN_KV, tm),
        _wgrad(u, dv, "wgrad_v", D_MODEL, ATTN_KV, tm),
        _wgrad(u, d_dn, "wgrad_dn", D_MODEL, DN_COLS, tm)[:, :D_IN - ATTN_Q - 2 * ATTN_KV]], axis=1)
    g_w_o = jnp.concatenate([_wgrad(aob, dh1b, "wgrad_oa", ATTN_Q, D_MODEL, tm),
                             _wgrad(dnb, dh1b, "wgrad_od", DN_W, D_MODEL, tm)], axis=0)
    grads = dict(early, w_in=g_w_in, w_o=g_w_o)
    sums = dict(loss=acc_ple[2, 0], norm_final=acc_ple[0], norm_ple=acc_ple[1], norm_mlp=acc_mlp[0], norm_mix=acc_mix[0],
                dn_norm=acc_dn[0], sinks=acc_attn[0, :ATTN_HEADS], a_log=acc_gate[0, DN_HEADS:2 * DN_HEADS],
                dt_bias=acc_gate[1, DN_HEADS:2 * DN_HEADS], conv_w=acc_conv[:DN_CONV])
    return sums, dx, grads


MESH = pl.DeviceIdType.MESH
ANY = pl.BlockSpec(memory_space=pl.ANY)
N_CHIPS = 4
N_DEV = 8


def _place():
    x, y, c = lax.axis_index("x"), lax.axis_index("y"), lax.axis_index("c")
    chips = [(1 - x, y), (x, 1 - y), (1 - x, 1 - y)]
    return x, y, c, chips


def _gather_weights(shards, conv_s):
    n = len(shards)
    per = 7

    def body(*refs):
        in_refs, conv_ref = refs[:n], refs[n]
        out_refs, conv_out = refs[n + 1:2 * n + 1], refs[2 * n + 1]
        send_sems, recv_sems = refs[2 * n + 2:]
        x, y, c, chips = _place()
        sibling = (x, y, 1 - c)

        def blk(a, px, py, pc):
            hr = in_refs[a].shape[0] // 2
            return out_refs[a].at[2 * px + py, pl.ds(pc * hr, hr), :]

        def mine(a):
            hr = in_refs[a].shape[0] // 2
            return in_refs[a].at[pl.ds(c * hr, hr), :]

        def rcopy(a, k, block, to, src=None):
            return pltpu.make_async_remote_copy(
                src_ref=blk(a, *block) if src is None else src, dst_ref=blk(a, *block),
                send_sem=send_sems.at[per * a + k], recv_sem=recv_sems.at[per * a + k],
                device_id=to, device_id_type=MESH)

        def whole(a, to):
            return pltpu.make_async_remote_copy(
                src_ref=in_refs[a], dst_ref=out_refs[a].at[2 * x + y],
                send_sem=send_sems.at[per * a], recv_sem=recv_sems.at[per * a], device_id=to, device_id_type=MESH)

        def ccopy(j, to):
            return pltpu.make_async_remote_copy(
                src_ref=conv_ref, dst_ref=conv_out.at[2 * x + y],
                send_sem=send_sems.at[per * n + j], recv_sem=recv_sems.at[per * n + j],
                device_id=to, device_id_type=MESH)

        started = []
        for a in range(n):
            first = [whole(a, sibling)]
            first += [rcopy(a, 1 + j, (x, y, c), (*chip, c), src=mine(a)) for j, chip in enumerate(chips)]
            for cp in first:
                cp.start()
            started += first
        conv_sends = [ccopy(j, (*chip, c)) for j, chip in enumerate(chips)] + [ccopy(3, sibling)]
        for cp in conv_sends:
            cp.start()
        started += conv_sends
        for a in range(n):
            for j, chip in enumerate(chips):
                rcopy(a, 1 + j, (*chip, c), (x, y, c)).wait_recv()
                fwd = rcopy(a, 4 + j, (*chip, c), sibling)
                fwd.start()
                started.append(fwd)
        for a in range(n):
            whole(a, sibling).wait_recv()
            for j, chip in enumerate(chips):
                rcopy(a, 4 + j, (*chip, 1 - c), (x, y, c)).wait_recv()
        for j, chip in enumerate(chips + [(x, y)]):
            pltpu.make_async_remote_copy(
                src_ref=conv_ref, dst_ref=conv_out.at[2 * chip[0] + chip[1]],
                send_sem=send_sems.at[per * n + j], recv_sem=recv_sems.at[per * n + j],
                device_id=sibling, device_id_type=MESH).wait_recv()
        for cp in started:
            cp.wait_send()

    nsem = per * n + 4
    out_shape = [jax.ShapeDtypeStruct((N_CHIPS,) + s.shape, s.dtype) for s in shards]
    out_shape.append(jax.ShapeDtypeStruct((N_CHIPS,) + conv_s.shape, conv_s.dtype))
    return pl.pallas_call(
        body, name="gather_weights", in_specs=[ANY] * (n + 1), out_specs=[ANY] * (n + 1), out_shape=out_shape,
        scratch_shapes=[pltpu.SemaphoreType.DMA((nsem,)), pltpu.SemaphoreType.DMA((nsem,))],
    )(*shards, conv_s)


HBM = pl.BlockSpec(memory_space=pltpu.HBM)
SEM = pl.BlockSpec(memory_space=pltpu.SEMAPHORE)
EFFECT = pltpu.SideEffectType.DATAFLOW_SIDE_EFFECTING
LATE_COPIES = 7


def _late_copies(in_refs, land_refs, send_sems, recv_sems):
    x, y, c, chips = _place()
    sends, arrivals = [], []
    for a, (src, land) in enumerate(zip(in_refs, land_refs)):
        hr = src.shape[0] // 2
        base = LATE_COPIES * a

        def cp(src_ref, dst_ref, s_idx, r_idx, to):
            return pltpu.make_async_remote_copy(src_ref=src_ref, dst_ref=dst_ref, send_sem=send_sems.at[base + s_idx],
                                                recv_sem=recv_sems.at[base + r_idx], device_id=to, device_id_type=MESH)

        sends.append(cp(src, land.at[2 * x + y], 0, 0, (x, y, 1 - c)))
        arrivals.append(cp(src, land.at[2 * x + y], 0, 0, (x, y, 1 - c)))
        for j, chip in enumerate(chips):
            for pc in range(2):
                half = src.at[pl.ds(c * hr, hr), :]
                sends.append(cp(half, land.at[2 * x + y, pl.ds(c * hr, hr), :], 1 + 2 * j + pc, 1 + 2 * j + c, (*chip, pc)))
                arrivals.append(cp(half, land.at[2 * chip[0] + chip[1], pl.ds(pc * hr, hr), :], 1 + 2 * j + pc,
                                   1 + 2 * j + pc, (*chip, pc)))
    return sends, arrivals


def _copies_start(name, build, nsem, srcs, land_shapes, after):
    n = len(srcs)

    def body(*refs):
        sends, _ = build(refs[:n], refs[n:2 * n], refs[2 * n + 1], refs[2 * n + 2])
        for cp in sends:
            cp.start()
        refs[-1][...] = jnp.zeros_like(refs[-1])

    lands = [pltpu.with_memory_space_constraint(lax.empty(s.shape, s.dtype), pltpu.HBM) for s in land_shapes]
    ins = [pltpu.with_memory_space_constraint(s, pltpu.HBM) for s in srcs]
    out = pl.pallas_call(
        body, name=name,
        out_shape=(pltpu.SemaphoreType.DMA((nsem,)), pltpu.SemaphoreType.DMA((nsem,)),
                   *[pltpu.HBM(s.shape, s.dtype) for s in srcs], *[pltpu.HBM(s.shape, s.dtype) for s in land_shapes],
                   jax.ShapeDtypeStruct((8, 128), F32)),
        in_specs=[HBM] * (2 * n) + [ANY],
        out_specs=(SEM, SEM, *[HBM] * (2 * n), pl.BlockSpec(memory_space=pltpu.VMEM)),
        input_output_aliases={i: 2 + i for i in range(2 * n)},
        compiler_params=pltpu.CompilerParams(has_side_effects=EFFECT),
    )(*ins, *lands, after)
    return out[0], out[1], out[2:2 + n], out[2 + n:2 + 2 * n], out[-1]


def _copies_wait(name, build, started, after):
    send_sems, recv_sems, srcs, lands, _ = started
    n = len(srcs)

    def body(*refs):
        sends, arrivals = build(refs[:n], refs[n:2 * n], refs[2 * n], refs[2 * n + 1])
        for cp in sends:
            cp.wait_send()
        for cp in arrivals:
            cp.wait_recv()

    out = pl.pallas_call(
        body, name=name,
        out_shape=(*[pltpu.HBM(s.shape, s.dtype) for s in srcs], *[pltpu.HBM(l.shape, l.dtype) for l in lands]),
        in_specs=[HBM] * (2 * n) + [SEM, SEM, ANY],
        out_specs=tuple([HBM] * (2 * n)),
        input_output_aliases={i: i for i in range(2 * n)},
        compiler_params=pltpu.CompilerParams(has_side_effects=EFFECT),
    )(*srcs, *lands, send_sems, recv_sems, after)
    return out[:n], out[n:]


def _exchange_copies(g_refs, got_refs, send_sems, recv_sems):
    x, y, c, _ = _place()
    sends, arrivals = [], []
    for a, (g, got) in enumerate(zip(g_refs, got_refs)):
        hr = g.shape[1] // 2
        cp = pltpu.make_async_remote_copy(
            src_ref=g.at[:, pl.ds((1 - c) * hr, hr), :], dst_ref=got, send_sem=send_sems.at[a],
            recv_sem=recv_sems.at[a], device_id=(x, y, 1 - c), device_id_type=MESH)
        sends.append(cp)
        arrivals.append(cp)
    return sends, arrivals


def _scatter_copies(s_refs, got_refs, send_sems, recv_sems):
    x, y, c, chips = _place()
    sends, arrivals = [], []
    for a, (s16, got) in enumerate(zip(s_refs, got_refs)):
        for j, chip in enumerate(chips):
            cp = pltpu.make_async_remote_copy(
                src_ref=s16.at[2 * chip[0] + chip[1]], dst_ref=got.at[j], send_sem=send_sems.at[3 * a + j],
                recv_sem=recv_sems.at[3 * a + j], device_id=(*chip, c), device_id_type=MESH)
            sends.append(cp)
            arrivals.append(cp)
    return sends, arrivals


def _exchange_halves(grads):
    n = len(grads)

    def body(*refs):
        g_refs, got_refs = refs[:n], refs[n:2 * n]
        send_sems, recv_sems = refs[2 * n:]
        x, y, c, _ = _place()
        remote = []
        for a in range(n):
            hr = g_refs[a].shape[1] // 2
            remote.append(pltpu.make_async_remote_copy(
                src_ref=g_refs[a].at[:, pl.ds((1 - c) * hr, hr), :], dst_ref=got_refs[a],
                send_sem=send_sems.at[a], recv_sem=recv_sems.at[a], device_id=(x, y, 1 - c), device_id_type=MESH))
        for cp in remote:
            cp.start()
        for cp in remote:
            cp.wait_recv()
        for cp in remote:
            cp.wait_send()

    half = [jax.ShapeDtypeStruct((g.shape[0], g.shape[1] // 2, g.shape[2]), g.dtype) for g in grads]
    return pl.pallas_call(
        body, name="exchange_halves", in_specs=[ANY] * n, out_specs=[ANY] * n, out_shape=half,
        scratch_shapes=[pltpu.SemaphoreType.DMA((n,)), pltpu.SemaphoreType.DMA((n,))],
    )(*grads)


def _scatter_to_chips(sums16):
    n = len(sums16)

    def body(*refs):
        s16, got_refs = refs[:n], refs[n:2 * n]
        send_sems, recv_sems = refs[2 * n:]
        x, y, c, chips = _place()
        remote = []
        for a in range(n):
            for j, chip in enumerate(chips):
                remote.append(pltpu.make_async_remote_copy(
                    src_ref=s16[a].at[2 * chip[0] + chip[1]], dst_ref=got_refs[a].at[j],
                    send_sem=send_sems.at[3 * a + j], recv_sem=recv_sems.at[3 * a + j],
                    device_id=(*chip, c), device_id_type=MESH))
        for cp in remote:
            cp.start()
        for cp in remote:
            cp.wait_recv()
        for cp in remote:
            cp.wait_send()

    got = [jax.ShapeDtypeStruct((3,) + s.shape[1:], BF16) for s in sums16]
    return pl.pallas_call(
        body, name="scatter_to_chips", in_specs=[ANY] * n, out_specs=[ANY] * n, out_shape=got,
        scratch_shapes=[pltpu.SemaphoreType.DMA((3 * n,)), pltpu.SemaphoreType.DMA((3 * n,))],
    )(*sums16)


def _share_halves(bufs):
    n = len(bufs)

    def body(*refs):
        out_refs = refs[n:2 * n]
        send_sems, recv_sems = refs[2 * n:]
        x, y, c, _ = _place()
        remote = [pltpu.make_async_remote_copy(
            src_ref=out_refs[a].at[c], dst_ref=out_refs[a].at[c], send_sem=send_sems.at[a], recv_sem=recv_sems.at[a],
            device_id=(x, y, 1 - c), device_id_type=MESH) for a in range(n)]
        for cp in remote:
            cp.start()
        for a in range(n):
            pltpu.make_async_remote_copy(
                src_ref=out_refs[a].at[c], dst_ref=out_refs[a].at[1 - c], send_sem=send_sems.at[a],
                recv_sem=recv_sems.at[a], device_id=(x, y, 1 - c), device_id_type=MESH).wait_recv()
        for cp in remote:
            cp.wait_send()

    return pl.pallas_call(
        body, name="share_halves", in_specs=[ANY] * n, out_specs=[ANY] * n,
        out_shape=[jax.ShapeDtypeStruct(b.shape, b.dtype) for b in bufs],
        input_output_aliases={a: a for a in range(n)},
        scratch_shapes=[pltpu.SemaphoreType.DMA((n,)), pltpu.SemaphoreType.DMA((n,))],
    )(*bufs)


SMALL_ROWS, SMALL_COLS = 16, CONV_CH


def _allreduce_small(block):
    m_per, ncol = block.shape

    def body(x_ref, sum_ref, all_ref, send_sems, recv_sems, local_sem):
        x, y, c, chips = _place()
        me, sibling = (x, y, c), (x, y, 1 - c)

        def rows(px, py, pc):
            return all_ref.at[pl.ds((4 * px + 2 * py + pc) * m_per, m_per), :]

        def copy(k, block_of, to, src=None):
            return pltpu.make_async_remote_copy(
                src_ref=rows(*block_of) if src is None else src, dst_ref=rows(*block_of),
                send_sem=send_sems.at[k], recv_sem=recv_sems.at[k], device_id=to, device_id_type=MESH)

        mine = pltpu.make_async_copy(x_ref, rows(*me), local_sem)
        mine.start()
        first = [copy(0, me, sibling, src=x_ref)]
        first += [copy(1 + j, me, (*chip, c), src=x_ref) for j, chip in enumerate(chips)]
        for cp in first:
            cp.start()
        passed = [copy(4 + j, (*chip, c), sibling) for j, chip in enumerate(chips)]
        for j, chip in enumerate(chips):
            copy(1 + j, (*chip, c), me).wait_recv()
            passed[j].start()
        copy(0, sibling, me).wait_recv()
        for j, chip in enumerate(chips):
            copy(4 + j, (*chip, 1 - c), me).wait_recv()
        for cp in first + passed:
            cp.wait_send()
        mine.wait()
        total = all_ref[0:m_per, :]
        for d in range(1, N_DEV):
            total = total + all_ref[d * m_per:(d + 1) * m_per, :]
        sum_ref[...] = total

    vm = pl.BlockSpec(memory_space=pltpu.VMEM)
    return pl.pallas_call(
        body, name="allreduce_small", in_specs=[vm], out_specs=vm,
        out_shape=jax.ShapeDtypeStruct((m_per, ncol), F32),
        scratch_shapes=[pltpu.VMEM((N_DEV * m_per, ncol), F32), pltpu.SemaphoreType.DMA((7,)),
                        pltpu.SemaphoreType.DMA((7,)), pltpu.SemaphoreType.DMA],
    )(block)


def _row_tile(rows, cols):
    tile = rows
    while tile * cols * 4 > (1 << 20) and tile % 16 == 0:
        tile //= 2
    return tile


def _elementwise(fn, name, ins, out_dtypes):
    rows, cols = ins[0].shape
    tile = _row_tile(rows, cols)

    def body(*refs):
        outs = fn(*[r[...] for r in refs[:len(ins)]])
        for o_ref, o in zip(refs[len(ins):], outs):
            o_ref[...] = o.astype(o_ref.dtype)

    spec = pl.BlockSpec((tile, cols), lambda i: (i, 0))
    return pl.pallas_call(
        body, name=name, grid=(rows // tile,), in_specs=[spec] * len(ins), out_specs=[spec] * len(out_dtypes),
        out_shape=[jax.ShapeDtypeStruct((rows, cols), d) for d in out_dtypes],
        compiler_params=_params("parallel"),
    )(*ins)


def _adamw_tile(w, g, m, v):
    m = ADAM_B1 * m + (1.0 - ADAM_B1) * g
    v = ADAM_B2 * v + (1.0 - ADAM_B2) * jnp.square(g)
    m_hat = m / (1.0 - ADAM_B1 ** ADAM_STEP)
    v_hat = v / (1.0 - ADAM_B2 ** ADAM_STEP)
    delta = -ADAM_LR * (m_hat / (jnp.sqrt(v_hat) + ADAM_EPS) + ADAM_WD * w)
    return delta, m, v


def _adamw(name, w, g, m, v):
    return _elementwise(_adamw_tile, name, [w, g, m, v], [F32, F32, F32])


def _chip_sum(name, g4, got, place):
    nchip, hr, cols = got.shape
    tile = _row_tile(hr, cols)
    nblk = hr // tile

    def body(pl_ref, g_ref, o_ref, s32_ref, s16_ref):
        s = g_ref[...] + o_ref[...]
        s32_ref[...] = s
        s16_ref[...] = s.astype(BF16)

    spec = pl.BlockSpec((None, tile, cols), lambda k, i, pr: (k, i, 0))
    return pl.pallas_call(
        body, name=name,
        grid_spec=pltpu.PrefetchScalarGridSpec(
            num_scalar_prefetch=1, grid=(nchip, nblk),
            in_specs=[pl.BlockSpec((None, tile, cols), lambda k, i, pr: (k, pr[1] * nblk + i, 0)), spec],
            out_specs=[spec, spec]),
        out_shape=[jax.ShapeDtypeStruct(got.shape, F32), jax.ShapeDtypeStruct(got.shape, BF16)],
        compiler_params=_params("parallel", "parallel"),
    )(place, g4, got)


def _mesh_sum(name, s32, got, place):
    _, hr, cols = s32.shape
    tile = _row_tile(hr, cols)

    def body(pl_ref, own_ref, g0_ref, g1_ref, g2_ref, o_ref):
        o_ref[...] = ((own_ref[...] + g0_ref[...].astype(F32)) + g1_ref[...].astype(F32)) + g2_ref[...].astype(F32)

    slab = lambda j: pl.BlockSpec((None, tile, cols), lambda i, pr: (j, i, 0))
    return pl.pallas_call(
        body, name=name,
        grid_spec=pltpu.PrefetchScalarGridSpec(
            num_scalar_prefetch=1, grid=(hr // tile,),
            in_specs=[pl.BlockSpec((None, tile, cols), lambda i, pr: (pr[0], i, 0)), slab(0), slab(1), slab(2)],
            out_specs=pl.BlockSpec((None, tile, cols), lambda i, pr: (pr[1], i, 0))),
        out_shape=jax.ShapeDtypeStruct((2, hr, cols), F32),
        compiler_params=_params("parallel"),
    )(place, s32, got, got, got)


def _reduce_scatter(grads):
    names = list(grads)
    place = _place_operand()
    got_a = _exchange_halves([grads[k] for k in names])
    sums = [_chip_sum("chip_sum_" + k, grads[k], g, place) for k, g in zip(names, got_a)]
    got_b = _scatter_to_chips([s[1] for s in sums])
    return {k: _mesh_sum("mesh_sum_" + k, s[0], g, place) for k, s, g in zip(names, sums, got_b)}


def _place_operand():
    return jnp.stack([2 * lax.axis_index("x") + lax.axis_index("y"), lax.axis_index("c")]).astype(jnp.int32)


def _per_chip(name, g):
    if name == "w_in":
        return jnp.transpose(g.reshape(D_MODEL, N_CHIPS, D_IN // N_CHIPS), (1, 0, 2))
    if name == "w_proj":
        return jnp.transpose(g.reshape(PLE_DIM, N_CHIPS, D_MODEL // N_CHIPS), (1, 0, 2))
    if name == "w_up4":
        return g
    return g.reshape(N_CHIPS, g.shape[0] // N_CHIPS, g.shape[1])


class _EarlyReducer:
    def start(self, grads):
        self.names = list(grads)
        self.place = _place_operand()
        slabs = [_per_chip(k, grads[k]) for k in self.names]
        halves = [jax.ShapeDtypeStruct((s.shape[0], s.shape[1] // 2, s.shape[2]), F32) for s in slabs]
        self.a = _copies_start("exchange_start", _exchange_copies, len(slabs), slabs, halves, slabs[0])
        return self.a[-1]

    def middle(self, after):
        slabs, got = _copies_wait("exchange_wait", _exchange_copies, self.a, after)
        self.sums = [_chip_sum("early_chip_sum_" + k, s, g, self.place) for k, s, g in zip(self.names, slabs, got)]
        s16 = [s[1] for s in self.sums]
        lands = [jax.ShapeDtypeStruct((3,) + s.shape[1:], BF16) for s in s16]
        self.b = _copies_start("scatter_start", _scatter_copies, 3 * len(s16), s16, lands, s16[0])
        return self.b[-1]

    def finish(self, after):
        _, got = _copies_wait("scatter_wait", _scatter_copies, self.b, after)
        self.bufs = {k: _mesh_sum("early_mesh_sum_" + k, s[0], g, self.place)
                     for k, s, g in zip(self.names, self.sums, got)}


def kernel(x, p, norm_mix, w_in, conv_w, a_log, dt_bias, dn_norm, sinks, w_o, norm_mlp, w_up, w_down, norm_ple, w_ple_gate, w_ple_proj, norm_final, loss_target, m_norm_mix, m_w_in, m_conv_w, m_a_log, m_dt_bias, m_dn_norm, m_sinks, m_w_o, m_norm_mlp, m_w_up, m_w_down, m_norm_ple, m_w_ple_gate, m_w_ple_proj, m_norm_final, v_norm_mix, v_w_in, v_conv_w, v_a_log, v_dt_bias, v_dn_norm, v_sinks, v_w_o, v_norm_mlp, v_w_up, v_w_down, v_norm_ple, v_w_ple_gate, v_w_ple_proj, v_norm_final):
    chip = 2 * lax.axis_index("x") + lax.axis_index("y")
    big = dict(w_in=w_in[0], w_o=w_o[0], w_up=w_up[0], w_down=w_down[0], w_gate=w_ple_gate[0], w_proj=w_ple_proj[0])
    big_m = dict(w_in=m_w_in[0], w_o=m_w_o[0], w_up=m_w_up[0], w_down=m_w_down[0], w_gate=m_w_ple_gate[0], w_proj=m_w_ple_proj[0])
    big_v = dict(w_in=v_w_in[0], w_o=v_w_o[0], w_up=v_w_up[0], w_down=v_w_down[0], w_gate=v_w_ple_gate[0], w_proj=v_w_ple_proj[0])
    names = list(big)

    w_in_all, conv_all = _gather_weights([big["w_in"].astype(BF16)], conv_w[0])
    late_names = names[1:]
    late_shards = [big[k].astype(BF16) for k in late_names]
    gather = _copies_start("gather_start", _late_copies, LATE_COPIES * len(late_shards), late_shards,
                           [jax.ShapeDtypeStruct((N_CHIPS,) + s.shape, BF16) for s in late_shards], w_in_all)
    token = gather[-1]
    w = dict(w_in=jnp.transpose(w_in_all, (1, 0, 2)).reshape(D_MODEL, D_IN),
             conv_w=jnp.transpose(conv_all, (1, 0, 2)).reshape(DN_CONV, CONV_CH))
    sm = dict(norm_mix=norm_mix[0] + token[0, 0], a_log=a_log[0], dt_bias=dt_bias[0], dn_norm=dn_norm[0],
              sinks=sinks[0], norm_mlp=norm_mlp[0], norm_ple=norm_ple[0], norm_final=norm_final)

    def late(after):
        gw = dict(zip(late_names, _copies_wait("gather_wait", _late_copies, gather, after)[1]))
        return dict(w_o=gw["w_o"].reshape(D_MODEL, D_MODEL), w_up4=gw["w_up"], w_down=gw["w_down"].reshape(D_FF, D_MODEL),
                    w_gate=gw["w_gate"].reshape(D_MODEL, D_MODEL), w_proj4=gw["w_proj"])

    reducer = _EarlyReducer()
    sums, grad_x, g = _local_step(x[0], p[0, 0], loss_target[0], sm, w, late, reducer)

    bufs = dict(reducer.bufs, **_reduce_scatter({k: _per_chip(k, g[k]) for k in ("w_in", "w_o")}))
    grad_key = dict(w_in="w_in", w_o="w_o", w_up="w_up4", w_down="w_down", w_gate="w_gate", w_proj="w_proj")
    full = _share_halves([bufs[grad_key[k]] for k in names])
    red = {k: f.reshape(-1, f.shape[-1]) for k, f in zip(names, full)}

    row = lambda v: jnp.zeros((SMALL_COLS,), F32).at[:v.shape[0]].set(v)
    misc = jnp.zeros((SMALL_COLS,), F32).at[0:4].set(sums["a_log"]).at[4:8].set(sums["dt_bias"]) \
        .at[8:16].set(sums["sinks"]).at[128:256].set(sums["dn_norm"]).at[256].set(sums["loss"])
    small = jnp.concatenate([sums["conv_w"], jnp.stack([row(sums["norm_mix"]), row(sums["norm_mlp"]), row(sums["norm_ple"]),
                                                        row(sums["norm_final"]), misc]),
                             jnp.zeros((SMALL_ROWS - 9, SMALL_COLS), F32)], axis=0)
    tot = _allreduce_small(small)
    loss = tot[8, 256]
    ncw = CONV_CH // N_CHIPS

    def pack(cw, nmix, nmlp, nple, nfin, al, dtb, sk, dnn):
        misc_p = jnp.zeros((SMALL_COLS,), F32).at[0:4].set(al).at[4:8].set(dtb).at[8:16].set(sk).at[128:256].set(dnn)
        cw_p = jnp.zeros((DN_CONV, SMALL_COLS), F32).at[:, :ncw].set(cw)
        return jnp.concatenate([cw_p, jnp.stack([row(nmix), row(nmlp), row(nple), row(nfin), misc_p]),
                                jnp.zeros((SMALL_ROWS - 9, SMALL_COLS), F32)], axis=0)

    def unpack(buf):
        return dict(conv_w=buf[0:4, :ncw][None], norm_mix=buf[4, :D_MODEL][None], norm_mlp=buf[5, :D_MODEL][None],
                    norm_ple=buf[6, :D_MODEL][None], norm_final=buf[7, :D_MODEL], a_log=buf[8, 0:4][None],
                    dt_bias=buf[8, 4:8][None], sinks=buf[8, 8:16][None], dn_norm=buf[8, 128:256][None])

    g_conv_shard = lax.dynamic_slice(tot[0:4], (0, chip * ncw), (DN_CONV, ncw))
    g_small = pack(g_conv_shard, tot[4, :D_MODEL], tot[5, :D_MODEL], tot[6, :D_MODEL], tot[7, :D_MODEL],
                   tot[8, 0:4], tot[8, 4:8], tot[8, 8:16], tot[8, 128:256])
    w_small = pack(conv_w[0], norm_mix[0], norm_mlp[0], norm_ple[0], norm_final, a_log[0], dt_bias[0], sinks[0], dn_norm[0])
    m_small = pack(m_conv_w[0], m_norm_mix[0], m_norm_mlp[0], m_norm_ple[0], m_norm_final, m_a_log[0], m_dt_bias[0],
                   m_sinks[0], m_dn_norm[0])
    v_small = pack(v_conv_w[0], v_norm_mix[0], v_norm_mlp[0], v_norm_ple[0], v_norm_final, v_a_log[0], v_dt_bias[0],
                   v_sinks[0], v_dn_norm[0])

    d_s, m_s, v_s = (unpack(b) for b in _adamw("adamw_small", w_small, g_small, m_small, v_small))
    g_s = unpack(g_small)
    out_g, out_d, out_m, out_v = dict(g_s), dict(d_s), dict(m_s), dict(v_s)
    ref_name = dict(w_in="w_in", w_o="w_o", w_up="w_up", w_down="w_down", w_gate="w_ple_gate", w_proj="w_ple_proj")
    for k in names:
        d_k, m_k, v_k = _adamw("adamw_" + k, big[k], red[k], big_m[k], big_v[k])
        out_g[ref_name[k]], out_d[ref_name[k]] = red[k][None], d_k[None]
        out_m[ref_name[k]], out_v[ref_name[k]] = m_k[None], v_k[None]
    order = ["norm_mix", "w_in", "conv_w", "a_log", "dt_bias", "dn_norm", "sinks", "w_o", "norm_mlp", "w_up", "w_down",
             "norm_ple", "w_ple_gate", "w_ple_proj", "norm_final"]
    return (loss, grad_x[None], *[out_g[k] for k in order], *[out_d[k] for k in order],
            *[out_m[k] for k in order], *[out_v[k] for k in order])
```

```python
import functools

import jax
import jax.numpy as jnp
from jax import lax
from jax.experimental import pallas as pl
from jax.experimental.pallas import tpu as pltpu

F32 = jnp.float32
BF16 = jnp.bfloat16
MXU_DTYPE = jnp.bfloat16
HI = lax.Precision.HIGHEST

D_MODEL = 1024
PLE_DIM = 256
ATTN_HEADS = 8
ATTN_KV_HEADS = 2
ATTN_GROUPS = ATTN_HEADS // ATTN_KV_HEADS
ATTN_HEAD_DIM = 64
ATTN_BLOCK = 128
ROPE_THETA = 10000.0
DN_HEADS = 4
DN_HEAD_DIM = 128
DN_CONV = 4
DN_CHUNK = 64
D_FF = 4 * D_MODEL
EPS = 1e-6
ATTN_Q = ATTN_HEADS * ATTN_HEAD_DIM
ATTN_KV = ATTN_KV_HEADS * ATTN_HEAD_DIM
DN_W = DN_HEADS * DN_HEAD_DIM
CONV_CH = 3 * DN_W
D_IN = ATTN_Q + 2 * ATTN_KV + 4 * DN_W + 2 * DN_HEADS
DN_COLS = 4 * DN_W + 128
DN_SCALE = DN_HEAD_DIM ** -0.5
ATTN_SCALE = ATTN_HEAD_DIM ** -0.5
FF_BLOCKS = 4
FF_BLOCK = D_FF // FF_BLOCKS

ADAM_LR = 0.001
ADAM_B1 = 0.9
ADAM_B2 = 0.999
ADAM_EPS = 1e-08
ADAM_WD = 0.01
ADAM_STEP = 10

V7X_VMEM_BYTES = 64 * 1024 * 1024
VMEM_LIMIT = 48 * 1024 * 1024

NN = ((1,), (0,))
NT = ((1,), (1,))
TN = ((0,), (0,))


def _dot(a, b, dims=NN, prec=None):
    return lax.dot_general(a, b, (dims, ((), ())), precision=prec, preferred_element_type=F32)


def _sigmoid(x):
    return 1.0 / (1.0 + jnp.exp(-x))


def _softplus(x):
    return jnp.maximum(x, 0.0) + jnp.log(1.0 + jnp.exp(-jnp.abs(x)))


def _params(*sem):
    return pltpu.CompilerParams(dimension_semantics=sem, vmem_limit_bytes=VMEM_LIMIT)


def _rms_fwd(xv, g):
    r = lax.rsqrt(jnp.mean(xv * xv, axis=-1, keepdims=True) + EPS)
    return xv * r * g


def _rms_bwd(xv, g, dn):
    r = lax.rsqrt(jnp.mean(xv * xv, axis=-1, keepdims=True) + EPS)
    xh = xv * r
    dg = jnp.sum(dn * xh, axis=0, keepdims=True)
    dxh = dn * g
    dx = r * (dxh - xh * jnp.mean(dxh * xh, axis=-1, keepdims=True))
    return dx, dg


def _full(shape):
    return pl.BlockSpec(shape, lambda *_: (0,) * len(shape))


def _inproj(x, g_mix, wa, wd, tm):
    t = x.shape[0]

    def body(x_ref, g_ref, wa_ref, wd_ref, u_ref, pa_ref, pd_ref):
        u = _rms_fwd(x_ref[...], g_ref[...]).astype(MXU_DTYPE)
        u_ref[...] = u
        pa_ref[...] = _dot(u, wa_ref[...])
        pd_ref[...] = _dot(u, wd_ref[...])

    na, nd = wa.shape[1], wd.shape[1]
    return pl.pallas_call(
        body, name="inproj", grid=(t // tm,),
        in_specs=[pl.BlockSpec((tm, D_MODEL), lambda i: (i, 0)), _full((1, D_MODEL)),
                  _full((D_MODEL, na)), _full((D_MODEL, nd))],
        out_specs=[pl.BlockSpec((tm, D_MODEL), lambda i: (i, 0)), pl.BlockSpec((tm, na), lambda i: (i, 0)),
                   pl.BlockSpec((tm, nd), lambda i: (i, 0))],
        out_shape=[jax.ShapeDtypeStruct((t, D_MODEL), MXU_DTYPE), jax.ShapeDtypeStruct((t, na), F32),
                   jax.ShapeDtypeStruct((t, nd), F32)],
        compiler_params=_params("parallel"),
    )(x, g_mix, wa, wd)


def _oproj(x, ao, dn, wo_a, wo_d, tm):
    t = x.shape[0]

    def body(x_ref, ao_ref, dn_ref, wa_ref, wd_ref, h_ref):
        h_ref[...] = (x_ref[...] + _dot(ao_ref[...].astype(MXU_DTYPE), wa_ref[...])
                      + _dot(dn_ref[...].astype(MXU_DTYPE), wd_ref[...]))

    half = ao.shape[1]
    return pl.pallas_call(
        body, name="oproj", grid=(t // tm,),
        in_specs=[pl.BlockSpec((tm, D_MODEL), lambda i: (i, 0)), pl.BlockSpec((tm, half), lambda i: (i, 0)),
                  pl.BlockSpec((tm, half), lambda i: (i, 0)), _full((half, D_MODEL)), _full((half, D_MODEL))],
        out_specs=pl.BlockSpec((tm, D_MODEL), lambda i: (i, 0)),
        out_shape=jax.ShapeDtypeStruct((t, D_MODEL), F32),
        compiler_params=_params("parallel"),
    )(x, ao, dn, wo_a, wo_d)


def _mlp_fwd(h1, g_mlp, w_up4, w_down, tm):
    t = h1.shape[0]

    def body(h_ref, g_ref, wu_ref, wd_ref, m_ref, a_ref, h2_ref, acc_ref):
        k = pl.program_id(1)

        @pl.when(k == 0)
        def _():
            m_ref[...] = _rms_fwd(h_ref[...], g_ref[...]).astype(MXU_DTYPE)
            acc_ref[...] = jnp.zeros_like(acc_ref)

        a = _dot(m_ref[...], wu_ref[...])
        a_ref[...] = a
        s = jnp.square(jnp.maximum(a, 0.0)).astype(MXU_DTYPE)
        acc_ref[...] += _dot(s, wd_ref[...])

        @pl.when(k == FF_BLOCKS - 1)
        def _():
            h2_ref[...] = h_ref[...] + acc_ref[...]

    return pl.pallas_call(
        body, name="mlp_fwd", grid=(t // tm, FF_BLOCKS),
        in_specs=[pl.BlockSpec((tm, D_MODEL), lambda i, k: (i, 0)), _full((1, D_MODEL)),
                  pl.BlockSpec((None, D_MODEL, FF_BLOCK), lambda i, k: (k, 0, 0)),
                  pl.BlockSpec((FF_BLOCK, D_MODEL), lambda i, k: (k, 0))],
        out_specs=[pl.BlockSpec((tm, D_MODEL), lambda i, k: (i, 0)), pl.BlockSpec((tm, FF_BLOCK), lambda i, k: (i, k)),
                   pl.BlockSpec((tm, D_MODEL), lambda i, k: (i, 0))],
        out_shape=[jax.ShapeDtypeStruct((t, D_MODEL), MXU_DTYPE), jax.ShapeDtypeStruct((t, D_FF), F32),
                   jax.ShapeDtypeStruct((t, D_MODEL), F32)],
        scratch_shapes=[pltpu.VMEM((tm, D_MODEL), F32)],
        compiler_params=_params("parallel", "arbitrary"),
    )(h1, g_mlp, w_up4, w_down)


def _ple_loss(h2, p, tgt, g_ple, g_fin, w_gate, w_proj, tm):
    t = h2.shape[0]

    def body(h_ref, p_ref, t_ref, gp_ref, gf_ref, wg_ref, wp_ref,
             dh_ref, dhb_ref, dgp_ref, dpp_ref, n3_ref, pb_ref, acc_ref):
        @pl.when(pl.program_id(0) == 0)
        def _():
            acc_ref[...] = jnp.zeros_like(acc_ref)

        h = h_ref[...]
        g_ple_v, g_fin_v = gp_ref[...], gf_ref[...]
        n3 = _rms_fwd(h, g_ple_v).astype(MXU_DTYPE)
        n3_ref[...] = n3
        gate = _sigmoid(_dot(n3, wg_ref[...]))
        pb = p_ref[...].astype(MXU_DTYPE)
        pb_ref[...] = pb
        pp = _dot(pb, wp_ref[...])
        h3 = h + gate * pp
        r4 = lax.rsqrt(jnp.mean(h3 * h3, axis=-1, keepdims=True) + EPS)
        xh4 = h3 * r4
        e = xh4 * g_fin_v - t_ref[...]
        loss = 0.5 * jnp.sum(jnp.mean(e * e, axis=-1, keepdims=True), axis=0, keepdims=True)
        dy = e * (1.0 / D_MODEL)
        dg_fin = jnp.sum(dy * xh4, axis=0, keepdims=True)
        dxh = dy * g_fin_v
        dh3 = r4 * (dxh - xh4 * jnp.mean(dxh * xh4, axis=-1, keepdims=True))
        dpp_ref[...] = (dh3 * gate).astype(MXU_DTYPE)
        dgp = (dh3 * pp * gate * (1.0 - gate)).astype(MXU_DTYPE)
        dgp_ref[...] = dgp
        dn3 = _dot(dgp, wg_ref[...], NT)
        dx, dg_ple = _rms_bwd(h, g_ple_v, dn3)
        dh2 = dh3 + dx
        dh_ref[...] = dh2
        dhb_ref[...] = dh2.astype(MXU_DTYPE)
        acc_ref[0:1, :] += dg_fin
        acc_ref[1:2, :] += dg_ple
        acc_ref[2:3, :] += jnp.broadcast_to(loss, (1, D_MODEL))

    row = lambda w: pl.BlockSpec((tm, w), lambda i: (i, 0))
    return pl.pallas_call(
        body, name="ple_loss", grid=(t // tm,),
        in_specs=[row(D_MODEL), row(PLE_DIM), row(D_MODEL), _full((1, D_MODEL)), _full((1, D_MODEL)),
                  _full((D_MODEL, D_MODEL)), _full((PLE_DIM, D_MODEL))],
        out_specs=[row(D_MODEL), row(D_MODEL), row(D_MODEL), row(D_MODEL), row(D_MODEL), row(PLE_DIM),
                   _full((8, D_MODEL))],
        out_shape=[jax.ShapeDtypeStruct((t, D_MODEL), F32), jax.ShapeDtypeStruct((t, D_MODEL), MXU_DTYPE),
                   jax.ShapeDtypeStruct((t, D_MODEL), MXU_DTYPE), jax.ShapeDtypeStruct((t, D_MODEL), MXU_DTYPE),
                   jax.ShapeDtypeStruct((t, D_MODEL), MXU_DTYPE), jax.ShapeDtypeStruct((t, PLE_DIM), MXU_DTYPE),
                   jax.ShapeDtypeStruct((8, D_MODEL), F32)],
        compiler_params=_params("arbitrary"),
    )(h2, p, tgt, g_ple, g_fin, w_gate, w_proj)


def _mlp_bwd(dh2, dh2b, a, h1, g_mlp, w_up4, w_down, tm):
    t = h1.shape[0]

    def body(dh_ref, dhb_ref, a_ref, h_ref, g_ref, wu_ref, wd_ref,
             s_ref, da_ref, dh1_ref, dh1b_ref, acc_ref, dm_ref):
        i, k = pl.program_id(0), pl.program_id(1)

        @pl.when((i == 0) & (k == 0))
        def _():
            acc_ref[...] = jnp.zeros_like(acc_ref)

        @pl.when(k == 0)
        def _():
            dm_ref[...] = jnp.zeros_like(dm_ref)

        ds = _dot(dhb_ref[...], wd_ref[...], NT)
        r = jnp.maximum(a_ref[...], 0.0)
        s_ref[...] = (r * r).astype(MXU_DTYPE)
        da = (ds * (2.0 * r)).astype(MXU_DTYPE)
        da_ref[...] = da
        dm_ref[...] += _dot(da, wu_ref[...], NT)

        @pl.when(k == FF_BLOCKS - 1)
        def _():
            dx, dg = _rms_bwd(h_ref[...], g_ref[...], dm_ref[...])
            dh1 = dh_ref[...] + dx
            dh1_ref[...] = dh1
            dh1b_ref[...] = dh1.astype(MXU_DTYPE)
            acc_ref[0:1, :] += dg

    tok = lambda w: pl.BlockSpec((tm, w), lambda i, k: (i, 0))
    return pl.pallas_call(
        body, name="mlp_bwd", grid=(t // tm, FF_BLOCKS),
        in_specs=[tok(D_MODEL), tok(D_MODEL), pl.BlockSpec((tm, FF_BLOCK), lambda i, k: (i, k)), tok(D_MODEL),
                  _full((1, D_MODEL)), pl.BlockSpec((None, D_MODEL, FF_BLOCK), lambda i, k: (k, 0, 0)),
                  pl.BlockSpec((FF_BLOCK, D_MODEL), lambda i, k: (k, 0))],
        out_specs=[pl.BlockSpec((tm, FF_BLOCK), lambda i, k: (i, k)), pl.BlockSpec((tm, FF_BLOCK), lambda i, k: (i, k)),
                   tok(D_MODEL), tok(D_MODEL), pl.BlockSpec((8, D_MODEL), lambda i, k: (0, 0))],
        out_shape=[jax.ShapeDtypeStruct((t, D_FF), MXU_DTYPE), jax.ShapeDtypeStruct((t, D_FF), MXU_DTYPE),
                   jax.ShapeDtypeStruct((t, D_MODEL), F32), jax.ShapeDtypeStruct((t, D_MODEL), MXU_DTYPE),
                   jax.ShapeDtypeStruct((8, D_MODEL), F32)],
        scratch_shapes=[pltpu.VMEM((tm, D_MODEL), F32)],
        compiler_params=_params("arbitrary", "arbitrary"),
    )(dh2, dh2b, a, h1, g_mlp, w_up4, w_down)


def _oproj_bwd(dh1b, wo_a, wo_d, tm, dep):
    t = dh1b.shape[0]
    half = wo_a.shape[0]

    def body(d_ref, wa_ref, wd_ref, dep_ref, da_ref, dd_ref):
        d = d_ref[...]
        da_ref[...] = _dot(d, wa_ref[...], NT)
        dd_ref[...] = _dot(d, wd_ref[...], NT)

    return pl.pallas_call(
        body, name="oproj_bwd", grid=(t // tm,),
        in_specs=[pl.BlockSpec((tm, D_MODEL), lambda i: (i, 0)), _full((half, D_MODEL)), _full((half, D_MODEL)),
                  pl.BlockSpec(memory_space=pl.ANY)],
        out_specs=[pl.BlockSpec((tm, half), lambda i: (i, 0)), pl.BlockSpec((tm, half), lambda i: (i, 0))],
        out_shape=[jax.ShapeDtypeStruct((t, half), F32), jax.ShapeDtypeStruct((t, half), F32)],
        compiler_params=_params("parallel"),
    )(dh1b, wo_a, wo_d, dep)


def _inproj_bwd(x, dh1, g_mix, grads, weights, tm):
    t = x.shape[0]
    n = len(grads)

    def body(*refs):
        x_ref, dh_ref, g_ref = refs[:3]
        g_refs, w_refs = refs[3:3 + n], refs[3 + n:3 + 2 * n]
        dx_ref, acc_ref = refs[3 + 2 * n:]

        @pl.when(pl.program_id(0) == 0)
        def _():
            acc_ref[...] = jnp.zeros_like(acc_ref)

        du = _dot(g_refs[0][...], w_refs[0][...], NT)
        for j in range(1, n):
            du += _dot(g_refs[j][...], w_refs[j][...], NT)
        dx, dg = _rms_bwd(x_ref[...], g_ref[...], du)
        dx_ref[...] = dh_ref[...] + dx
        acc_ref[0:1, :] += dg

    tok = lambda w: pl.BlockSpec((tm, w), lambda i: (i, 0))
    return pl.pallas_call(
        body, name="inproj_bwd", grid=(t // tm,),
        in_specs=[tok(D_MODEL), tok(D_MODEL), _full((1, D_MODEL))] + [tok(g.shape[1]) for g in grads]
                 + [_full(w.shape) for w in weights],
        out_specs=[tok(D_MODEL), _full((8, D_MODEL))],
        out_shape=[jax.ShapeDtypeStruct((t, D_MODEL), F32), jax.ShapeDtypeStruct((8, D_MODEL), F32)],
        compiler_params=_params("arbitrary"),
    )(x, dh1, g_mix, *grads, *weights)


def _wgrad(a, b, name, tk, tn, tt, stacked=False):
    t, kdim = a.shape
    ncols = b.shape[1]

    def body(a_ref, b_ref, o_ref):
        @pl.when(pl.program_id(2) == 0)
        def _():
            o_ref[...] = jnp.zeros_like(o_ref)

        o_ref[...] += _dot(a_ref[...], b_ref[...], TN)

    if stacked:
        out_spec = pl.BlockSpec((None, tk, tn), lambda i, j, s: (j, i, 0))
        out_shape = jax.ShapeDtypeStruct((ncols // tn, kdim, tn), F32)
    else:
        out_spec = pl.BlockSpec((tk, tn), lambda i, j, s: (i, j))
        out_shape = jax.ShapeDtypeStruct((kdim, ncols), F32)
    return pl.pallas_call(
        body, name=name, grid=(kdim // tk, ncols // tn, t // tt),
        in_specs=[pl.BlockSpec((tt, tk), lambda i, j, s: (s, i)), pl.BlockSpec((tt, tn), lambda i, j, s: (s, j))],
        out_specs=out_spec, out_shape=out_shape,
        compiler_params=_params("parallel", "parallel", "arbitrary"),
    )(a, b)


def _rope_tables(t):
    half = ATTN_HEAD_DIM // 2
    inv = 1.0 / (ROPE_THETA ** (jnp.arange(half, dtype=F32) * (2.0 / ATTN_HEAD_DIM)))
    ang = jnp.arange(t, dtype=F32)[:, None] * inv[None, :]
    cos, sin = jnp.cos(ang), jnp.sin(ang)
    cos2 = jnp.concatenate([cos, cos], axis=-1)
    sin2 = jnp.concatenate([-sin, sin], axis=-1)
    return jnp.tile(cos2, (1, 2)), jnp.tile(sin2, (1, 2))


def _swap_halves(tv):
    w = tv.shape[-1]
    lane = lax.broadcasted_iota(jnp.int32, tv.shape, tv.ndim - 1)
    first = (lane % ATTN_HEAD_DIM) < (ATTN_HEAD_DIM // 2)
    return jnp.where(first, pltpu.roll(tv, w - ATTN_HEAD_DIM // 2, tv.ndim - 1),
                     pltpu.roll(tv, ATTN_HEAD_DIM // 2, tv.ndim - 1))


def _rope(tv, cos, sin):
    return tv * cos + _swap_halves(tv) * sin


def _rope_bwd(dv, cos, sin):
    return dv * cos + _swap_halves(dv * sin)


def _attn_valid(first_block):
    c = lax.broadcasted_iota(jnp.int32, (2 * ATTN_BLOCK, ATTN_BLOCK), 0)
    r = lax.broadcasted_iota(jnp.int32, (2 * ATTN_BLOCK, ATTN_BLOCK), 1)
    return (c > r) & (c <= r + ATTN_BLOCK) & ((c >= ATTN_BLOCK) | jnp.logical_not(first_block))


def _attn_probs(st, sink, valid):
    s = jnp.where(valid, st * ATTN_SCALE, -jnp.inf)
    m = jnp.maximum(jnp.max(s, axis=0, keepdims=True), sink)
    e = jnp.where(valid, jnp.exp(s - m), 0.0)
    es = jnp.exp(sink - m)
    inv = 1.0 / (jnp.sum(e, axis=0, keepdims=True) + es)
    return e * inv, es * inv


def _lane_scalar(vec, idx):
    lane = lax.broadcasted_iota(jnp.int32, vec.shape, 1)
    return jnp.sum(jnp.where(lane == idx, vec, 0.0), axis=-1, keepdims=True)


def _attn_specs(nb):
    cur = lambda w, cb: pl.BlockSpec((ATTN_BLOCK, w), lambda i: (jnp.minimum(i, nb - 1), cb))
    prev = lambda w, cb: pl.BlockSpec((ATTN_BLOCK, w), lambda i: (jnp.maximum(jnp.minimum(i, nb - 1) - 1, 0), cb))
    kcol, vcol = ATTN_Q // ATTN_KV, ATTN_Q // ATTN_KV + 1
    return [cur(ATTN_Q, 0), cur(ATTN_KV, kcol), prev(ATTN_KV, kcol), cur(ATTN_KV, vcol), prev(ATTN_KV, vcol),
            cur(ATTN_KV, 0), cur(ATTN_KV, 0), prev(ATTN_KV, 0), prev(ATTN_KV, 0), _full((1, 128))]


def _attn_fwd(pa, cos, sin, sinks_vec):
    t = pa.shape[0]
    nb = t // ATTN_BLOCK

    def body(q_ref, kc_ref, kp_ref, vc_ref, vp_ref, cc_ref, sc_ref, cp_ref, sp_ref, sk_ref, o_ref):
        first = pl.program_id(0) == 0
        cc, sc = cc_ref[...], sc_ref[...]
        q = _rope(q_ref[...], jnp.tile(cc, (1, ATTN_Q // ATTN_KV)), jnp.tile(sc, (1, ATTN_Q // ATTN_KV)))
        kc = _rope(kc_ref[...], cc, sc)
        kp = _rope(kp_ref[...], cp_ref[...], sp_ref[...])
        vc, vp = vc_ref[...], vp_ref[...]
        sk = sk_ref[...]
        valid = _attn_valid(first)
        kv = lambda tp, tc, hk: jnp.concatenate([tp[:, hk * ATTN_HEAD_DIM:(hk + 1) * ATTN_HEAD_DIM],
                                                 tc[:, hk * ATTN_HEAD_DIM:(hk + 1) * ATTN_HEAD_DIM]], axis=0)
        kwins = [kv(kp, kc, hk) for hk in range(ATTN_KV_HEADS)]
        vwins_t = [kv(vp, vc, hk).T for hk in range(ATTN_KV_HEADS)]
        heads = [slice(h * ATTN_HEAD_DIM, (h + 1) * ATTN_HEAD_DIM) for h in range(ATTN_HEADS)]
        scores = [_dot(kwins[h // ATTN_GROUPS], q[:, hs], NT) for h, hs in enumerate(heads)]
        probs = [_attn_probs(st, _lane_scalar(sk, h), valid)[0] for h, st in enumerate(scores)]
        for h, (hs, pt) in enumerate(zip(heads, probs)):
            o_ref[:, hs] = _dot(vwins_t[h // ATTN_GROUPS], pt).T

    return pl.pallas_call(
        body, name="attn_fwd", grid=(nb,),
        in_specs=_attn_specs(nb),
        out_specs=pl.BlockSpec((ATTN_BLOCK, ATTN_Q), lambda i: (i, 0)),
        out_shape=jax.ShapeDtypeStruct((t, ATTN_Q), F32),
        compiler_params=_params("parallel"),
    )(pa, pa, pa, pa, pa, cos, sin, cos, sin, sinks_vec)


def _attn_bwd(pa, cos, sin, sinks_vec, dao):
    t = pa.shape[0]
    nb = t // ATTN_BLOCK

    def body(q_ref, kc_ref, kp_ref, vc_ref, vp_ref, cc_ref, sc_ref, cp_ref, sp_ref, sk_ref, do_ref,
             dq_ref, dk_ref, dv_ref, acc_ref, dqr_ref, dkw_ref, dvw_ref, ck_ref, cv_ref):
        i = pl.program_id(0)

        @pl.when(i == 0)
        def _():
            acc_ref[...] = jnp.zeros_like(acc_ref)
            ck_ref[...] = jnp.zeros_like(ck_ref)
            cv_ref[...] = jnp.zeros_like(cv_ref)

        @pl.when(i < nb)
        def _():
            first = i == 0
            cc, sc = cc_ref[...], sc_ref[...]
            cq, sq = jnp.tile(cc, (1, ATTN_Q // ATTN_KV)), jnp.tile(sc, (1, ATTN_Q // ATTN_KV))
            q = _rope(q_ref[...], cq, sq)
            kc = _rope(kc_ref[...], cc, sc)
            kp = _rope(kp_ref[...], cp_ref[...], sp_ref[...])
            vc, vp = vc_ref[...], vp_ref[...]
            sk = sk_ref[...]
            do = do_ref[...]
            lane = lax.broadcasted_iota(jnp.int32, (1, 128), 1)
            dsink = jnp.zeros((1, 128), F32)
            valid = _attn_valid(first)
            kv = lambda tp, tc, hk: jnp.concatenate([tp[:, hk * ATTN_HEAD_DIM:(hk + 1) * ATTN_HEAD_DIM],
                                                     tc[:, hk * ATTN_HEAD_DIM:(hk + 1) * ATTN_HEAD_DIM]], axis=0)
            kwins = [kv(kp, kc, hk) for hk in range(ATTN_KV_HEADS)]
            vwins = [kv(vp, vc, hk) for hk in range(ATTN_KV_HEADS)]
            kwins_t = [kw.T for kw in kwins]
            heads = [slice(h * ATTN_HEAD_DIM, (h + 1) * ATTN_HEAD_DIM) for h in range(ATTN_HEADS)]
            scores = [_dot(kwins[h // ATTN_GROUPS], q[:, hs], NT) for h, hs in enumerate(heads)]
            dps = [_dot(vwins[h // ATTN_GROUPS], do[:, hs], NT) for h, hs in enumerate(heads)]
            pts, dsts = [], []
            for h, (st, dp_t) in enumerate(zip(scores, dps)):
                probs_t, psink = _attn_probs(st, _lane_scalar(sk, h), valid)
                delta = jnp.sum(probs_t * dp_t, axis=0, keepdims=True)
                pts.append(probs_t)
                dsts.append(probs_t * (dp_t - delta) * ATTN_SCALE)
                dsink += jnp.where(lane == h, jnp.sum(-psink * delta, axis=1, keepdims=True), 0.0)
            for h, (hs, ds_t) in enumerate(zip(heads, dsts)):
                dqr_ref[:, hs] = _dot(kwins_t[h // ATTN_GROUPS], ds_t).T
            for hk in range(ATTN_KV_HEADS):
                ks = slice(hk * ATTN_HEAD_DIM, (hk + 1) * ATTN_HEAD_DIM)
                group = range(hk * ATTN_GROUPS, (hk + 1) * ATTN_GROUPS)
                ds_g = jnp.concatenate([dsts[h] for h in group], axis=1)
                p_g = jnp.concatenate([pts[h] for h in group], axis=1)
                q_g = jnp.concatenate([q[:, heads[h]] for h in group], axis=0)
                do_g = jnp.concatenate([do[:, heads[h]] for h in group], axis=0)
                dkw_ref[:, ks] = _dot(ds_g, q_g)
                dvw_ref[:, ks] = _dot(p_g, do_g)
            acc_ref[0:1, :] += dsink
            dq_ref[...] = _rope_bwd(dqr_ref[...], cq, sq).astype(dq_ref.dtype)
            dk_ref[...] = (ck_ref[...] + _rope_bwd(dkw_ref[0:ATTN_BLOCK, :], cp_ref[...], sp_ref[...])).astype(dk_ref.dtype)
            dv_ref[...] = (cv_ref[...] + dvw_ref[0:ATTN_BLOCK, :]).astype(dv_ref.dtype)
            ck_ref[...] = _rope_bwd(dkw_ref[ATTN_BLOCK:2 * ATTN_BLOCK, :], cc, sc)
            cv_ref[...] = dvw_ref[ATTN_BLOCK:2 * ATTN_BLOCK, :]

        @pl.when(i == nb)
        def _():
            dk_ref[...] = ck_ref[...].astype(dk_ref.dtype)
            dv_ref[...] = cv_ref[...].astype(dv_ref.dtype)

    prev_out = lambda w: pl.BlockSpec((ATTN_BLOCK, w), lambda i: (jnp.maximum(i - 1, 0), 0))
    return pl.pallas_call(
        body, name="attn_bwd", grid=(nb + 1,),
        in_specs=_attn_specs(nb) + [pl.BlockSpec((ATTN_BLOCK, ATTN_Q), lambda i: (jnp.minimum(i, nb - 1), 0))],
        out_specs=[pl.BlockSpec((ATTN_BLOCK, ATTN_Q), lambda i: (jnp.minimum(i, nb - 1), 0)), prev_out(ATTN_KV),
                   prev_out(ATTN_KV), _full((8, 128))],
        out_shape=[jax.ShapeDtypeStruct((t, ATTN_Q), MXU_DTYPE), jax.ShapeDtypeStruct((t, ATTN_KV), MXU_DTYPE),
                   jax.ShapeDtypeStruct((t, ATTN_KV), MXU_DTYPE), jax.ShapeDtypeStruct((8, 128), F32)],
        scratch_shapes=[pltpu.VMEM((ATTN_BLOCK, ATTN_Q), F32), pltpu.VMEM((2 * ATTN_BLOCK, ATTN_KV), F32),
                        pltpu.VMEM((2 * ATTN_BLOCK, ATTN_KV), F32), pltpu.VMEM((ATTN_BLOCK, ATTN_KV), F32),
                        pltpu.VMEM((ATTN_BLOCK, ATTN_KV), F32)],
        compiler_params=_params("arbitrary"),
    )(pa, pa, pa, pa, pa, cos, sin, cos, sin, sinks_vec, dao)


PAIR = 2 * DN_CHUNK
HALO = 8


def _conv_window(cur_ref, prev_ref, xs_ref, tm):
    prev = jnp.where(pl.program_id(0) > 0, prev_ref[...], 0.0)
    xs_ref[0:HALO, :] = prev
    xs_ref[HALO:HALO + tm, :] = cur_ref[...]


def _conv_taps(xs_ref, cw_ref, tm):
    y = cw_ref[0:1, :] * xs_ref[pl.ds(HALO - DN_CONV + 1, tm), :]
    for j in range(1, DN_CONV):
        y += cw_ref[j:j + 1, :] * xs_ref[pl.ds(HALO - DN_CONV + 1 + j, tm), :]
    return y


def _gate_values(ba, al, dt):
    beta = _sigmoid(ba)
    pre = ba + dt
    g = -jnp.exp(al) * _softplus(pre)
    return beta, g, pre


def _dn_prep_specs(tm, t):
    return [pl.BlockSpec((tm, CONV_CH), lambda i: (i, 0)),
            pl.BlockSpec((HALO, CONV_CH), lambda i: (jnp.maximum(i * (tm // HALO) - 1, 0), 0)),
            pl.BlockSpec((tm, 128), lambda i: (i, 4 * DN_W // 128)),
            _full((DN_CONV, CONV_CH)), _full((1, 128)), _full((1, 128))]


def _dn_prep(pd, conv_w, al_vec, dt_vec, tm):
    t = pd.shape[0]

    def body(cur_ref, prev_ref, ba_ref, cw_ref, al_ref, dt_ref, qn_ref, kn_ref, vc_ref, gc_ref, gr_ref, xs_ref):
        _conv_window(cur_ref, prev_ref, xs_ref, tm)
        y = _conv_taps(xs_ref, cw_ref, tm)
        c = y * _sigmoid(y)
        for h in range(DN_HEADS):
            qs = slice(h * DN_HEAD_DIM, (h + 1) * DN_HEAD_DIM)
            ksl = slice(DN_W + h * DN_HEAD_DIM, DN_W + (h + 1) * DN_HEAD_DIM)
            qh, kh = c[:, qs], c[:, ksl]
            qn_ref[:, qs] = qh * lax.rsqrt(jnp.sum(qh * qh, axis=-1, keepdims=True) + EPS) * DN_SCALE
            kn_ref[:, qs] = kh * lax.rsqrt(jnp.sum(kh * kh, axis=-1, keepdims=True) + EPS)
        vc_ref[...] = c[:, 2 * DN_W:3 * DN_W]
        beta, g, _ = _gate_values(ba_ref[...], al_ref[...], dt_ref[...])
        lane = lax.broadcasted_iota(jnp.int32, beta.shape, 1)
        gb = jnp.where(lane < DN_HEADS, beta, jnp.where(lane < 2 * DN_HEADS, g, 0.0))
        gc_ref[...] = gb
        gr_ref[...] = gb.T[0:8, :]

    tok = lambda w: pl.BlockSpec((tm, w), lambda i: (i, 0))
    return pl.pallas_call(
        body, name="dn_prep", grid=(t // tm,),
        in_specs=_dn_prep_specs(tm, t),
        out_specs=[tok(DN_W), tok(DN_W), tok(DN_W), tok(128), pl.BlockSpec((8, tm), lambda i: (0, i))],
        out_shape=[jax.ShapeDtypeStruct((t, DN_W), F32)] * 3 + [jax.ShapeDtypeStruct((t, 128), F32),
                                                                 jax.ShapeDtypeStruct((8, t), F32)],
        scratch_shapes=[pltpu.VMEM((HALO + tm, CONV_CH), F32)],
        compiler_params=_params("parallel"),
    )(pd, pd, pd, conv_w, al_vec, dt_vec)


def _pair_masks():
    r = lax.broadcasted_iota(jnp.int32, (PAIR, PAIR), 0)
    c = lax.broadcasted_iota(jnp.int32, (PAIR, PAIR), 1)
    same = (r < DN_CHUNK) == (c < DN_CHUNK)
    return same & (r >= c), same & (r > c)


def _lane_col(mat, idx):
    lane = lax.broadcasted_iota(jnp.int32, mat.shape, 1)
    return jnp.sum(jnp.where(lane == idx, mat, 0.0), axis=-1, keepdims=True)


def _pair_cumsums(gc, gr, low):
    lowf = low.astype(F32)
    return _dot(lowf, gc, NN, HI), _dot(gr, lowf, NT, HI)


def _pair_gates(gc, cum_c, cum_r, low, h):
    beta = _lane_col(gc, h)
    gam = _lane_col(cum_c, DN_HEADS + h)
    gam_row = cum_r[DN_HEADS + h:DN_HEADS + h + 1, :]
    dm = jnp.where(low, jnp.exp(jnp.where(low, gam - gam_row, 0.0)), 0.0)
    row = lax.broadcasted_iota(jnp.int32, gam.shape, 0)
    gl = jnp.where(row < DN_CHUNK, gam[DN_CHUNK - 1:DN_CHUNK, :], gam[PAIR - 1:PAIR, :])
    return beta, gam, dm, gl


def _split(a):
    hi = a.astype(BF16)
    return hi, (a - hi.astype(F32)).astype(BF16)


def _dot_split(a, b, dims=NN):
    (ah, al), (bh, bl) = a, b
    la, lb = (1, 1) if dims == TN else ((0, 1) if dims == NN else (0, 0))
    r = _dot(jnp.concatenate([ah, al], axis=la), jnp.concatenate([bh, bl], axis=lb), dims)
    m, n = r.shape[0] // 2, r.shape[1] // 2
    return (r[m:, n:] + (r[:m, n:] + r[m:, :n])) + r[:m, :n]


def _unit_lower_inverses(lmats):
    n = lmats[0].shape[0]
    eye = (lax.broadcasted_iota(jnp.int32, (n, n), 0) == lax.broadcasted_iota(jnp.int32, (n, n), 1)).astype(F32)
    accs = [eye - l for l in lmats]
    splits = [_split(l) for l in lmats]
    step = 1
    while 2 * step < DN_CHUNK:
        splits = [_split(_dot_split(s, s)) for s in splits]
        accs = [acc + _dot_split(_split(acc), s) for acc, s in zip(accs, splits)]
        step *= 2
    return accs


def _dn_intra(qn, kn, vc, gc, gr):
    t = qn.shape[0]
    npair = t // PAIR

    def body(q_ref, k_ref, v_ref, gc_ref, gr_ref, u_ref, w_ref, qg_ref, kd_ref, a_ref, ti_ref, dl_ref):
        gc_v = gc_ref[...]
        low, strict = _pair_masks()
        cum_c, cum_r = _pair_cumsums(gc_v, gr_ref[...], low)
        heads = [slice(h * DN_HEAD_DIM, (h + 1) * DN_HEAD_DIM) for h in range(DN_HEADS)]
        gates = [_pair_gates(gc_v, cum_c, cum_r, low, h) for h in range(DN_HEADS)]
        lmats = []
        for hs, (beta, gam, dm, gl) in zip(heads, gates):
            k = k_ref[:, hs]
            lmats.append(jnp.where(strict, _dot(k * beta, k, NT) * dm, 0.0))
        tinvs = _unit_lower_inverses(lmats)
        for h, (hs, (beta, gam, dm, gl), tinv) in enumerate(zip(heads, gates, tinvs)):
            q, k, v = q_ref[:, hs], k_ref[:, hs], v_ref[:, hs]
            eg = jnp.exp(gam)
            u_ref[:, hs] = _dot(tinv, v * beta)
            w_ref[:, hs] = _dot(tinv, (k * beta) * eg)
            a_ref[h] = _dot(q, k, NT) * dm
            ti_ref[h] = tinv
            qg_ref[:, hs] = q * eg
            kd_ref[:, hs] = k * jnp.exp(gl - gam)
            for c in range(2):
                last = (c + 1) * DN_CHUNK - 1
                dl_ref[c, h] = jnp.broadcast_to(jnp.exp(gam[last:last + 1, :]), (8, 128))

    tok = lambda w: pl.BlockSpec((PAIR, w), lambda n: (n, 0))
    hm = pl.BlockSpec((DN_HEADS, PAIR, PAIR), lambda n: (0, n, 0))
    return pl.pallas_call(
        body, name="dn_intra", grid=(npair,),
        in_specs=[tok(DN_W), tok(DN_W), tok(DN_W), tok(128), pl.BlockSpec((8, PAIR), lambda n: (0, n))],
        out_specs=[tok(DN_W)] * 4 + [hm, hm, pl.BlockSpec((2, DN_HEADS, 8, 128), lambda n: (n, 0, 0, 0))],
        out_shape=[jax.ShapeDtypeStruct((t, DN_W), F32)] * 4 + [jax.ShapeDtypeStruct((DN_HEADS, t, PAIR), F32)] * 2
                  + [jax.ShapeDtypeStruct((2 * npair, DN_HEADS, 8, 128), F32)],
        compiler_params=_params("parallel"),
    )(qn, kn, vc, gc, gr)


def _dn_scan_fwd(u, w, qg, kd, a_qk, dlast, pd, dn_w):
    t = u.shape[0]
    npair = t // PAIR

    def body(u_ref, w_ref, qg_ref, kd_ref, a_ref, dl_ref, z_ref, nw_ref, out_ref, o_ref, vn_ref, sall_ref, s_ref):
        @pl.when(pl.program_id(0) == 0)
        def _():
            s_ref[...] = jnp.zeros_like(s_ref)

        nw = nw_ref[...]
        for c in range(2):
            rows = slice(c * DN_CHUNK, (c + 1) * DN_CHUNK)
            for h in range(DN_HEADS):
                hs = slice(h * DN_HEAD_DIM, (h + 1) * DN_HEAD_DIM)
                st = s_ref[h]
                sall_ref[c, h] = st
                vn_ref[rows, hs] = u_ref[rows, hs] - _dot(w_ref[rows, hs], st)
            for h in range(DN_HEADS):
                hs = slice(h * DN_HEAD_DIM, (h + 1) * DN_HEAD_DIM)
                st, vn = s_ref[h], vn_ref[rows, hs]
                o = _dot(qg_ref[rows, hs], st) + _dot(a_ref[h, rows, rows], vn)
                s_ref[h] = st * dl_ref[c, h][0:1, :] + _dot(kd_ref[rows, hs], vn, TN)
                o_ref[rows, hs] = o
                z = z_ref[rows, hs]
                on = o * lax.rsqrt(jnp.mean(o * o, axis=-1, keepdims=True) + EPS) * nw
                out_ref[rows, hs] = on * (z * _sigmoid(z))

    tok = pl.BlockSpec((PAIR, DN_W), lambda n: (n, 0))
    hm = pl.BlockSpec((DN_HEADS, PAIR, PAIR), lambda n: (0, n, 0))
    return pl.pallas_call(
        body, name="dn_scan_fwd", grid=(npair,),
        in_specs=[tok, tok, tok, tok, hm, pl.BlockSpec((2, DN_HEADS, 8, 128), lambda n: (n, 0, 0, 0)),
                  pl.BlockSpec((PAIR, DN_W), lambda n: (n, 3)), _full((1, 128))],
        out_specs=[tok, tok, tok, pl.BlockSpec((2, DN_HEADS, DN_HEAD_DIM, DN_HEAD_DIM), lambda n: (n, 0, 0, 0))],
        out_shape=[jax.ShapeDtypeStruct((t, DN_W), F32)] * 3
                  + [jax.ShapeDtypeStruct((2 * npair, DN_HEADS, DN_HEAD_DIM, DN_HEAD_DIM), F32)],
        scratch_shapes=[pltpu.VMEM((DN_HEADS, DN_HEAD_DIM, DN_HEAD_DIM), F32)],
        compiler_params=_params("arbitrary"),
    )(u, w, qg, kd, a_qk, dlast, pd, dn_w)


def _dn_scan_bwd(dout, o, vnew, sall, w, qg, kd, a_qk, dlast, pd, dn_w):
    t = o.shape[0]
    npair = t // PAIR
    rev = lambda n: npair - 1 - n

    def body(do_ref, o_ref, vn_ref, sall_ref, w_ref, qg_ref, kd_ref, a_ref, dl_ref, z_ref, nw_ref,
             dz_ref, du_ref, dw_ref, dqg_ref, dkd_ref, da_ref, ddl_ref, acc_ref, ds_ref, dos_ref):
        @pl.when(pl.program_id(0) == 0)
        def _():
            ds_ref[...] = jnp.zeros_like(ds_ref)
            acc_ref[...] = jnp.zeros_like(acc_ref)

        nw = nw_ref[...]
        dnw = jnp.zeros((1, 128), F32)
        for h in range(DN_HEADS):
            hs = slice(h * DN_HEAD_DIM, (h + 1) * DN_HEAD_DIM)
            o, z, dout = o_ref[:, hs], z_ref[:, hs], do_ref[:, hs]
            r = lax.rsqrt(jnp.mean(o * o, axis=-1, keepdims=True) + EPS)
            oh = o * r
            sz = _sigmoid(z)
            dz_ref[:, hs] = dout * (oh * nw) * (sz + z * sz * (1.0 - sz))
            don = dout * (z * sz)
            dnw += jnp.sum(don * oh, axis=0, keepdims=True)
            doh = don * nw
            dos_ref[:, hs] = r * (doh - oh * jnp.mean(doh * oh, axis=-1, keepdims=True))
        acc_ref[0:1, :] += dnw
        for c in (1, 0):
            rows = slice(c * DN_CHUNK, (c + 1) * DN_CHUNK)
            other = slice((1 - c) * DN_CHUNK, (2 - c) * DN_CHUNK)
            for h in range(DN_HEADS):
                hs = slice(h * DN_HEAD_DIM, (h + 1) * DN_HEAD_DIM)
                do, st, dsp, vn = dos_ref[rows, hs], sall_ref[c, h], ds_ref[h], vn_ref[rows, hs]
                da_ref[h, rows, rows] = _dot(do, vn, NT)
                da_ref[h, rows, other] = jnp.zeros((DN_CHUNK, DN_CHUNK), F32)
                du_ref[rows, hs] = _dot(a_ref[h, rows, rows], do, TN) + _dot(kd_ref[rows, hs], dsp)
                dqg_ref[rows, hs] = _dot(do, st, NT)
                dkd_ref[rows, hs] = _dot(vn, dsp, NT)
                ddl = jnp.sum(jnp.sum(dsp * st, axis=1, keepdims=True), axis=0, keepdims=True)
                ddl_ref[c, h] = jnp.broadcast_to(ddl, (8, 128))
            for h in range(DN_HEADS):
                hs = slice(h * DN_HEAD_DIM, (h + 1) * DN_HEAD_DIM)
                do, st, dvn = dos_ref[rows, hs], sall_ref[c, h], du_ref[rows, hs]
                dw_ref[rows, hs] = -_dot(dvn, st, NT)
                ds_ref[h] = (ds_ref[h] * dl_ref[c, h][0:1, :] + _dot(qg_ref[rows, hs], do, TN)
                             - _dot(w_ref[rows, hs], dvn, TN))

    tok = pl.BlockSpec((PAIR, DN_W), lambda n: (rev(n), 0))
    hm = pl.BlockSpec((DN_HEADS, PAIR, PAIR), lambda n: (0, rev(n), 0))
    sc = pl.BlockSpec((2, DN_HEADS, 8, 128), lambda n: (rev(n), 0, 0, 0))
    return pl.pallas_call(
        body, name="dn_scan_bwd", grid=(npair,),
        in_specs=[tok, tok, tok, pl.BlockSpec((2, DN_HEADS, DN_HEAD_DIM, DN_HEAD_DIM), lambda n: (rev(n), 0, 0, 0)),
                  tok, tok, tok, hm, sc, pl.BlockSpec((PAIR, DN_W), lambda n: (rev(n), 3)), _full((1, 128))],
        out_specs=[tok] * 5 + [hm, sc, _full((8, 128))],
        out_shape=[jax.ShapeDtypeStruct((t, DN_W), F32)] * 5 + [jax.ShapeDtypeStruct((DN_HEADS, t, PAIR), F32),
                   jax.ShapeDtypeStruct((2 * npair, DN_HEADS, 8, 128), F32), jax.ShapeDtypeStruct((8, 128), F32)],
        scratch_shapes=[pltpu.VMEM((DN_HEADS, DN_HEAD_DIM, DN_HEAD_DIM), F32), pltpu.VMEM((PAIR, DN_W), F32)],
        compiler_params=_params("arbitrary"),
    )(dout, o, vnew, sall, w, qg, kd, a_qk, dlast, pd, dn_w)


def _dn_intra_bwd(qn, kn, vc, gc, gr, tinv, a_qk, du, dw, dqg, dkd, da_qk, ddlast, dlast, dep):
    t = qn.shape[0]
    npair = t // PAIR

    def body(q_ref, k_ref, v_ref, gc_ref, gr_ref, ti_ref, a_ref, du_ref, dw_ref, dqg_ref, dkd_ref, da_ref, ddl_ref, dl_ref,
             dep_ref, dq_ref, dk_ref, dv_ref, dg_ref):
        gc_v = gc_ref[...]
        low, strict = _pair_masks()
        cum_c, cum_r = _pair_cumsums(gc_v, gr_ref[...], low)
        lane = lax.broadcasted_iota(jnp.int32, (PAIR, 128), 1)
        rowi = lax.broadcasted_iota(jnp.int32, (PAIR, 1), 0)
        rsum = lambda v: jnp.sum(v, axis=-1, keepdims=True)
        dgam_all = jnp.zeros((PAIR, 128), F32)
        dbeta_all = jnp.zeros((PAIR, 128), F32)
        heads = [slice(h * DN_HEAD_DIM, (h + 1) * DN_HEAD_DIM) for h in range(DN_HEADS)]
        gates = [_pair_gates(gc_v, cum_c, cum_r, low, h) for h in range(DN_HEADS)]
        tsplits, dtis = [], []
        for h, (hs, (beta, gam, dm, gl)) in enumerate(zip(heads, gates)):
            k = k_ref[:, hs]
            tsplits.append(_split(ti_ref[h]))
            dtis.append(_dot(du_ref[:, hs], v_ref[:, hs] * beta, NT)
                        + _dot(dw_ref[:, hs], (k * beta) * jnp.exp(gam), NT))
        xs = [_dot_split(ts, _split(dti), TN) for ts, dti in zip(tsplits, dtis)]
        dls = [jnp.where(strict, -_dot_split(_split(x), ts, NT), 0.0) for x, ts in zip(xs, tsplits)]
        for h, (hs, (beta, gam, dm, gl), dl) in enumerate(zip(heads, gates, dls)):
            q, k, v = q_ref[:, hs], k_ref[:, hs], v_ref[:, hs]
            tinv, a = ti_ref[h], a_ref[h]
            du, dw, dqg, dkd = du_ref[:, hs], dw_ref[:, hs], dqg_ref[:, hs], dkd_ref[:, hs]
            kb = k * beta
            eg = jnp.exp(gam)
            ekd = jnp.exp(gl - gam)
            kbg = kb * eg
            lmat = jnp.where(strict, _dot(kb, k, NT) * dm, 0.0)
            dvb = _dot(tinv, du, TN)
            dkbg = _dot(tinv, dw, TN)
            dmm = dl * dm
            dam = jnp.where(low, da_ref[h], 0.0)
            dn = dam * dm
            e = dl * lmat + dam * a
            dkb = _dot(dmm, k) + dkbg * eg
            dk_ref[:, hs] = _dot(dmm, kb, TN) + _dot(dn, q, TN) + dkd * ekd + dkb * beta
            dq_ref[:, hs] = _dot(dn, k) + dqg * eg
            dv_ref[:, hs] = dvb * beta
            t_kd = rsum(dkd * (k * ekd))
            dgam = rsum(e) - rsum(e.T) + rsum(dqg * (q * eg)) + rsum(dkbg * kbg) - t_kd
            for c in range(2):
                rows = slice(c * DN_CHUNK, (c + 1) * DN_CHUNK)
                dgl = (jnp.sum(t_kd[rows, :], axis=0, keepdims=True)
                       + ddl_ref[c, h][0:1, 0:1] * dl_ref[c, h][0:1, 0:1])
                dgam = dgam + jnp.where(rowi == (c + 1) * DN_CHUNK - 1, dgl, 0.0)
            dgam_all += jnp.where(lane == DN_HEADS + h, dgam, 0.0)
            dbeta_all += jnp.where(lane == h, rsum(dkb * k) + rsum(dvb * v), 0.0)
        dg_ref[...] = dbeta_all + _dot(low.astype(F32), dgam_all, TN, HI)

    tok = lambda w: pl.BlockSpec((PAIR, w), lambda n: (n, 0))
    hm = pl.BlockSpec((DN_HEADS, PAIR, PAIR), lambda n: (0, n, 0))
    sc = pl.BlockSpec((2, DN_HEADS, 8, 128), lambda n: (n, 0, 0, 0))
    return pl.pallas_call(
        body, name="dn_intra_bwd", grid=(npair,),
        in_specs=[tok(DN_W), tok(DN_W), tok(DN_W), tok(128), pl.BlockSpec((8, PAIR), lambda n: (0, n)), hm, hm,
                  tok(DN_W), tok(DN_W), tok(DN_W), tok(DN_W), hm, sc, sc, pl.BlockSpec(memory_space=pl.ANY)],
        out_specs=[tok(DN_W), tok(DN_W), tok(DN_W), tok(128)],
        out_shape=[jax.ShapeDtypeStruct((t, DN_W), F32)] * 3 + [jax.ShapeDtypeStruct((t, 128), F32)],
        compiler_params=_params("parallel"),
    )(qn, kn, vc, gc, gr, tinv, a_qk, du, dw, dqg, dkd, da_qk, ddlast, dlast, dep)


def _dn_prep_bwd(pd, conv_w, al_vec, dt_vec, dqn, dkn, dvc, dgc, tm):
    t = pd.shape[0]

    def body(cur_ref, prev_ref, ba_ref, cw_ref, al_ref, dt_ref, dq_ref, dk_ref, dv_ref, dg_ref,
             dy_ref, dba_ref, accw_ref, accg_ref, xs_ref, dc_ref):
        @pl.when(pl.program_id(0) == 0)
        def _():
            accw_ref[...] = jnp.zeros_like(accw_ref)
            accg_ref[...] = jnp.zeros_like(accg_ref)

        _conv_window(cur_ref, prev_ref, xs_ref, tm)
        y = _conv_taps(xs_ref, cw_ref, tm)
        sg = _sigmoid(y)
        c = y * sg
        for h in range(DN_HEADS):
            qs = slice(h * DN_HEAD_DIM, (h + 1) * DN_HEAD_DIM)
            ksl = slice(DN_W + h * DN_HEAD_DIM, DN_W + (h + 1) * DN_HEAD_DIM)
            for src, sl, scale in ((dq_ref, qs, DN_SCALE), (dk_ref, ksl, 1.0)):
                xh = c[:, sl]
                r = lax.rsqrt(jnp.sum(xh * xh, axis=-1, keepdims=True) + EPS)
                unit = xh * r
                dn = src[:, qs] * scale
                dc_ref[:, sl] = r * (dn - unit * jnp.sum(dn * unit, axis=-1, keepdims=True))
        dc_ref[:, 2 * DN_W:3 * DN_W] = dv_ref[...]
        dy = dc_ref[...] * (sg + y * sg * (1.0 - sg))
        dy_ref[...] = dy
        for j in range(DN_CONV):
            accw_ref[j:j + 1, :] += jnp.sum(dy * xs_ref[pl.ds(HALO - DN_CONV + 1 + j, tm), :], axis=0, keepdims=True)

        beta, g, pre = _gate_values(ba_ref[...], al_ref[...], dt_ref[...])
        dgb = dg_ref[...]
        lane = lax.broadcasted_iota(jnp.int32, dgb.shape, 1)
        is_b, is_a = lane < DN_HEADS, (lane >= DN_HEADS) & (lane < 2 * DN_HEADS)
        dpre = dgb * (-jnp.exp(al_ref[...])) * _sigmoid(pre)
        dba_ref[...] = jnp.where(is_b, dgb * beta * (1.0 - beta), jnp.where(is_a, dpre, 0.0))
        accg_ref[0:1, :] += jnp.sum(jnp.where(is_a, dgb * g, 0.0), axis=0, keepdims=True)
        accg_ref[1:2, :] += jnp.sum(jnp.where(is_a, dpre, 0.0), axis=0, keepdims=True)

    tok = lambda w: pl.BlockSpec((tm, w), lambda i: (i, 0))
    return pl.pallas_call(
        body, name="dn_prep_bwd", grid=(t // tm,),
        in_specs=_dn_prep_specs(tm, t) + [tok(DN_W), tok(DN_W), tok(DN_W), tok(128)],
        out_specs=[tok(CONV_CH), tok(128), _full((8, CONV_CH)), _full((8, 128))],
        out_shape=[jax.ShapeDtypeStruct((t, CONV_CH), F32), jax.ShapeDtypeStruct((t, 128), F32),
                   jax.ShapeDtypeStruct((8, CONV_CH), F32), jax.ShapeDtypeStruct((8, 128), F32)],
        scratch_shapes=[pltpu.VMEM((HALO + tm, CONV_CH), F32), pltpu.VMEM((tm, CONV_CH), F32)],
        compiler_params=_params("arbitrary"),
    )(pd, pd, pd, conv_w, al_vec, dt_vec, dqn, dkn, dvc, dgc)


def _dn_conv_bwd(dy, dz, dba, conv_w, tm):
    t = dy.shape[0]
    nt = t // tm

    def body(cur_ref, nxt_ref, dz_ref, dba_ref, cw_ref, o_ref, ds_ref):
        nxt = jnp.where(pl.program_id(0) < nt - 1, nxt_ref[...], 0.0)
        ds_ref[0:tm, :] = cur_ref[...]
        ds_ref[tm:tm + HALO, :] = nxt
        dx = cw_ref[0:1, :] * ds_ref[pl.ds(DN_CONV - 1, tm), :]
        for j in range(1, DN_CONV):
            dx += cw_ref[j:j + 1, :] * ds_ref[pl.ds(DN_CONV - 1 - j, tm), :]
        o_ref[:, 0:CONV_CH] = dx.astype(o_ref.dtype)
        o_ref[:, CONV_CH:CONV_CH + DN_W] = dz_ref[...].astype(o_ref.dtype)
        o_ref[:, CONV_CH + DN_W:DN_COLS] = dba_ref[...].astype(o_ref.dtype)

    tok = lambda w: pl.BlockSpec((tm, w), lambda i: (i, 0))
    return pl.pallas_call(
        body, name="dn_conv_bwd", grid=(nt,),
        in_specs=[tok(CONV_CH),
                  pl.BlockSpec((HALO, CONV_CH), lambda i: (jnp.minimum((i + 1) * (tm // HALO), t // HALO - 1), 0)),
                  tok(DN_W), tok(128), _full((DN_CONV, CONV_CH))],
        out_specs=tok(DN_COLS),
        out_shape=jax.ShapeDtypeStruct((t, DN_COLS), MXU_DTYPE),
        scratch_shapes=[pltpu.VMEM((tm + HALO, CONV_CH), F32)],
        compiler_params=_params("parallel"),
    )(dy, dy, dz, dba, conv_w)


def _pad_lanes(v, offset=0):
    return jnp.zeros((1, 128), F32).at[0, offset:offset + v.shape[0]].set(v.astype(F32))


class _LocalReducer:
    def start(self, grads):
        return jnp.zeros((8, 128), F32)

    def middle(self, after):
        return jnp.zeros((8, 128), F32)

    def finish(self, after):
        return None


def _local_step(x, p, tgt, sm, w, late, reducer):
    t = x.shape[0]
    tm = min(512, t // 2)
    tm_s = min(256, t // 2)
    tw = min(1024, t // 2)

    w_in = w["w_in"]
    wa = w_in[:, :ATTN_Q + 2 * ATTN_KV]
    wd = jnp.pad(w_in[:, ATTN_Q + 2 * ATTN_KV:], ((0, 0), (0, DN_COLS - (D_IN - ATTN_Q - 2 * ATTN_KV))))
    conv_w = w["conv_w"]
    al_vec, dt_vec = _pad_lanes(sm["a_log"], DN_HEADS), _pad_lanes(sm["dt_bias"], DN_HEADS)
    sinks_vec = _pad_lanes(sm["sinks"])
    dn_w = sm["dn_norm"].reshape(1, 128)
    row = lambda v: v.reshape(1, D_MODEL)
    cos, sin = _rope_tables(t)

    u, pa, pd = _inproj(x, row(sm["norm_mix"]), wa, wd, tm_s)
    ao = _attn_fwd(pa, cos, sin, sinks_vec)
    qn, kn, vc, gc, gr = _dn_prep(pd, conv_w, al_vec, dt_vec, tm_s)
    uu, ww, qg, kd, a_qk, tinv, dlast = _dn_intra(qn, kn, vc, gc, gr)
    dn_out, o, vnew, sall = _dn_scan_fwd(uu, ww, qg, kd, a_qk, dlast, pd, dn_w)
    w = dict(w, **late(dn_out))
    wo_a, wo_d = w["w_o"][:ATTN_Q], w["w_o"][ATTN_Q:]
    w_proj = jnp.transpose(w["w_proj4"], (1, 0, 2)).reshape(PLE_DIM, D_MODEL)
    h1 = _oproj(x, ao, dn_out, wo_a, wo_d, tm)
    m, a, h2 = _mlp_fwd(h1, row(sm["norm_mlp"]), w["w_up4"], w["w_down"], tm)
    dh2, dh2b, dgp, dpp, n3, pb, acc_ple = _ple_loss(h2, p, tgt, row(sm["norm_ple"]), row(sm["norm_final"]),
                                                     w["w_gate"], w_proj, tm_s)
    g_w_gate = _wgrad(n3, dgp, "wgrad_gate", D_MODEL, D_MODEL, tw)
    g_w_proj = _wgrad(pb, dpp, "wgrad_proj", PLE_DIM, D_MODEL, tw)
    s, da, dh1, dh1b, acc_mlp = _mlp_bwd(dh2, dh2b, a, h1, row(sm["norm_mlp"]), w["w_up4"], w["w_down"], tm)
    g_w_up4 = _wgrad(m, da, "wgrad_up", D_MODEL, FF_BLOCK, tw, stacked=True)
    g_w_down = _wgrad(s, dh2b, "wgrad_down", FF_BLOCK, D_MODEL, tw)
    early = dict(w_up4=g_w_up4, w_down=g_w_down, w_gate=g_w_gate, w_proj=g_w_proj)
    dep = reducer.start(early)
    dao, ddn = _oproj_bwd(dh1b, wo_a, wo_d, tm, dep)
    dz, du, dw, dqg, dkd, da_qk, ddlast, acc_dn = _dn_scan_bwd(ddn, o, vnew, sall, ww, qg, kd, a_qk, dlast, pd, dn_w)
    dep = reducer.middle(du)
    dqn, dkn, dvc, dgc = _dn_intra_bwd(qn, kn, vc, gc, gr, tinv, a_qk, du, dw, dqg, dkd, da_qk, ddlast, dlast, dep)
    dy, dba, acc_conv, acc_gate = _dn_prep_bwd(pd, conv_w, al_vec, dt_vec, dqn, dkn, dvc, dgc, tm_s)
    d_dn = _dn_conv_bwd(dy, dz, dba, conv_w, tm_s)
    dq, dk, dv, acc_attn = _attn_bwd(pa, cos, sin, sinks_vec, dao)
    reducer.finish(dq)
    wq, wk, wv = wa[:, :ATTN_Q], wa[:, ATTN_Q:ATTN_Q + ATTN_KV], wa[:, ATTN_Q + ATTN_KV:]
    dx, acc_mix = _inproj_bwd(x, dh1, row(sm["norm_mix"]), [dq, dk, dv, d_dn], [wq, wk, wv, wd], tm_s)

    aob, dnb = ao.astype(MXU_DTYPE), dn_out.astype(MXU_DTYPE)
    g_w_in = jnp.concatenate([
        _wgrad(u, dq, "wgrad_q", D_MODEL, ATTN_Q, tw), _wgrad(u, dk, "wgrad_k", D_MODEL, ATTN_KV, tw),
        _wgrad(u, dv, "wgrad_v", D_MODEL, ATTN_KV, tw),
        _wgrad(u, d_dn, "wgrad_dn", D_MODEL, DN_COLS, tw)[:, :D_IN - ATTN_Q - 2 * ATTN_KV]], axis=1)
    g_w_o = jnp.concatenate([_wgrad(aob, dh1b, "wgrad_oa", ATTN_Q, D_MODEL, tw),
                             _wgrad(dnb, dh1b, "wgrad_od", DN_W, D_MODEL, tw)], axis=0)
    grads = dict(early, w_in=g_w_in, w_o=g_w_o)
    sums = dict(loss=acc_ple[2, 0], norm_final=acc_ple[0], norm_ple=acc_ple[1], norm_mlp=acc_mlp[0], norm_mix=acc_mix[0],
                dn_norm=acc_dn[0], sinks=acc_attn[0, :ATTN_HEADS], a_log=acc_gate[0, DN_HEADS:2 * DN_HEADS],
                dt_bias=acc_gate[1, DN_HEADS:2 * DN_HEADS], conv_w=acc_conv[:DN_CONV])
    return sums, dx, grads


MESH = pl.DeviceIdType.MESH
ANY = pl.BlockSpec(memory_space=pl.ANY)
N_CHIPS = 4
N_DEV = 8


def _place():
    x, y, c = lax.axis_index("x"), lax.axis_index("y"), lax.axis_index("c")
    chips = [(1 - x, y), (x, 1 - y), (1 - x, 1 - y)]
    return x, y, c, chips


def _gather_weights(shards, conv_s):
    n = len(shards)
    per = 7

    def body(*refs):
        in_refs, conv_ref = refs[:n], refs[n]
        out_refs, conv_out = refs[n + 1:2 * n + 1], refs[2 * n + 1]
        send_sems, recv_sems = refs[2 * n + 2:]
        x, y, c, chips = _place()
        sibling = (x, y, 1 - c)

        def blk(a, px, py, pc):
            hr = in_refs[a].shape[0] // 2
            return out_refs[a].at[2 * px + py, pl.ds(pc * hr, hr), :]

        def mine(a):
            hr = in_refs[a].shape[0] // 2
            return in_refs[a].at[pl.ds(c * hr, hr), :]

        def rcopy(a, k, block, to, src=None):
            return pltpu.make_async_remote_copy(
                src_ref=blk(a, *block) if src is None else src, dst_ref=blk(a, *block),
                send_sem=send_sems.at[per * a + k], recv_sem=recv_sems.at[per * a + k],
                device_id=to, device_id_type=MESH)

        def whole(a, to):
            return pltpu.make_async_remote_copy(
                src_ref=in_refs[a], dst_ref=out_refs[a].at[2 * x + y],
                send_sem=send_sems.at[per * a], recv_sem=recv_sems.at[per * a], device_id=to, device_id_type=MESH)

        def ccopy(j, to):
            return pltpu.make_async_remote_copy(
                src_ref=conv_ref, dst_ref=conv_out.at[2 * x + y],
                send_sem=send_sems.at[per * n + j], recv_sem=recv_sems.at[per * n + j],
                device_id=to, device_id_type=MESH)

        started = []
        for a in range(n):
            first = [whole(a, sibling)]
            first += [rcopy(a, 1 + j, (x, y, c), (*chip, c), src=mine(a)) for j, chip in enumerate(chips)]
            for cp in first:
                cp.start()
            started += first
        conv_sends = [ccopy(j, (*chip, c)) for j, chip in enumerate(chips)] + [ccopy(3, sibling)]
        for cp in conv_sends:
            cp.start()
        started += conv_sends
        for a in range(n):
            for j, chip in enumerate(chips):
                rcopy(a, 1 + j, (*chip, c), (x, y, c)).wait_recv()
                fwd = rcopy(a, 4 + j, (*chip, c), sibling)
                fwd.start()
                started.append(fwd)
        for a in range(n):
            whole(a, sibling).wait_recv()
            for j, chip in enumerate(chips):
                rcopy(a, 4 + j, (*chip, 1 - c), (x, y, c)).wait_recv()
        for j, chip in enumerate(chips + [(x, y)]):
            pltpu.make_async_remote_copy(
                src_ref=conv_ref, dst_ref=conv_out.at[2 * chip[0] + chip[1]],
                send_sem=send_sems.at[per * n + j], recv_sem=recv_sems.at[per * n + j],
                device_id=sibling, device_id_type=MESH).wait_recv()
        for cp in started:
            cp.wait_send()

    nsem = per * n + 4
    out_shape = [jax.ShapeDtypeStruct((N_CHIPS,) + s.shape, s.dtype) for s in shards]
    out_shape.append(jax.ShapeDtypeStruct((N_CHIPS,) + conv_s.shape, conv_s.dtype))
    return pl.pallas_call(
        body, name="gather_weights", in_specs=[ANY] * (n + 1), out_specs=[ANY] * (n + 1), out_shape=out_shape,
        scratch_shapes=[pltpu.SemaphoreType.DMA((nsem,)), pltpu.SemaphoreType.DMA((nsem,))],
    )(*shards, conv_s)


HBM = pl.BlockSpec(memory_space=pltpu.HBM)
SEM = pl.BlockSpec(memory_space=pltpu.SEMAPHORE)
EFFECT = pltpu.SideEffectType.DATAFLOW_SIDE_EFFECTING
LATE_COPIES = 7


def _late_copies(in_refs, land_refs, send_sems, recv_sems):
    x, y, c, chips = _place()
    sends, arrivals = [], []
    for a, (src, land) in enumerate(zip(in_refs, land_refs)):
        hr = src.shape[0] // 2
        base = LATE_COPIES * a

        def cp(src_ref, dst_ref, s_idx, r_idx, to):
            return pltpu.make_async_remote_copy(src_ref=src_ref, dst_ref=dst_ref, send_sem=send_sems.at[base + s_idx],
                                                recv_sem=recv_sems.at[base + r_idx], device_id=to, device_id_type=MESH)

        sends.append(cp(src, land.at[2 * x + y], 0, 0, (x, y, 1 - c)))
        arrivals.append(cp(src, land.at[2 * x + y], 0, 0, (x, y, 1 - c)))
        for j, chip in enumerate(chips):
            for pc in range(2):
                half = src.at[pl.ds(c * hr, hr), :]
                sends.append(cp(half, land.at[2 * x + y, pl.ds(c * hr, hr), :], 1 + 2 * j + pc, 1 + 2 * j + c, (*chip, pc)))
                arrivals.append(cp(half, land.at[2 * chip[0] + chip[1], pl.ds(pc * hr, hr), :], 1 + 2 * j + pc,
                                   1 + 2 * j + pc, (*chip, pc)))
    return sends, arrivals


def _copies_start(name, build, nsem, srcs, land_shapes, after):
    n = len(srcs)

    def body(*refs):
        sends, _ = build(refs[:n], refs[n:2 * n], refs[2 * n + 1], refs[2 * n + 2])
        for cp in sends:
            cp.start()
        refs[-1][...] = jnp.zeros_like(refs[-1])

    lands = [pltpu.with_memory_space_constraint(lax.empty(s.shape, s.dtype), pltpu.HBM) for s in land_shapes]
    ins = [pltpu.with_memory_space_constraint(s, pltpu.HBM) for s in srcs]
    out = pl.pallas_call(
        body, name=name,
        out_shape=(pltpu.SemaphoreType.DMA((nsem,)), pltpu.SemaphoreType.DMA((nsem,)),
                   *[pltpu.HBM(s.shape, s.dtype) for s in srcs], *[pltpu.HBM(s.shape, s.dtype) for s in land_shapes],
                   jax.ShapeDtypeStruct((8, 128), F32)),
        in_specs=[HBM] * (2 * n) + [ANY],
        out_specs=(SEM, SEM, *[HBM] * (2 * n), pl.BlockSpec(memory_space=pltpu.VMEM)),
        input_output_aliases={i: 2 + i for i in range(2 * n)},
        compiler_params=pltpu.CompilerParams(has_side_effects=EFFECT),
    )(*ins, *lands, after)
    return out[0], out[1], out[2:2 + n], out[2 + n:2 + 2 * n], out[-1]


def _copies_wait(name, build, started, after):
    send_sems, recv_sems, srcs, lands, _ = started
    n = len(srcs)

    def body(*refs):
        sends, arrivals = build(refs[:n], refs[n:2 * n], refs[2 * n], refs[2 * n + 1])
        for cp in sends:
            cp.wait_send()
        for cp in arrivals:
            cp.wait_recv()

    out = pl.pallas_call(
        body, name=name,
        out_shape=(*[pltpu.HBM(s.shape, s.dtype) for s in srcs], *[pltpu.HBM(l.shape, l.dtype) for l in lands]),
        in_specs=[HBM] * (2 * n) + [SEM, SEM, ANY],
        out_specs=tuple([HBM] * (2 * n)),
        input_output_aliases={i: i for i in range(2 * n)},
        compiler_params=pltpu.CompilerParams(has_side_effects=EFFECT),
    )(*srcs, *lands, send_sems, recv_sems, after)
    return out[:n], out[n:]


def _exchange_copies(g_refs, got_refs, send_sems, recv_sems):
    x, y, c, _ = _place()
    sends, arrivals = [], []
    for a, (g, got) in enumerate(zip(g_refs, got_refs)):
        hr = g.shape[1] // 2
        cp = pltpu.make_async_remote_copy(
            src_ref=g.at[:, pl.ds((1 - c) * hr, hr), :], dst_ref=got, send_sem=send_sems.at[a],
            recv_sem=recv_sems.at[a], device_id=(x, y, 1 - c), device_id_type=MESH)
        sends.append(cp)
        arrivals.append(cp)
    return sends, arrivals


def _scatter_copies(s_refs, got_refs, send_sems, recv_sems):
    x, y, c, chips = _place()
    sends, arrivals = [], []
    for a, (s16, got) in enumerate(zip(s_refs, got_refs)):
        for j, chip in enumerate(chips):
            cp = pltpu.make_async_remote_copy(
                src_ref=s16.at[2 * chip[0] + chip[1]], dst_ref=got.at[j], send_sem=send_sems.at[3 * a + j],
                recv_sem=recv_sems.at[3 * a + j], device_id=(*chip, c), device_id_type=MESH)
            sends.append(cp)
            arrivals.append(cp)
    return sends, arrivals


def _exchange_halves(grads):
    n = len(grads)

    def body(*refs):
        g_refs, got_refs = refs[:n], refs[n:2 * n]
        send_sems, recv_sems = refs[2 * n:]
        x, y, c, _ = _place()
        remote = []
        for a in range(n):
            hr = g_refs[a].shape[1] // 2
            remote.append(pltpu.make_async_remote_copy(
                src_ref=g_refs[a].at[:, pl.ds((1 - c) * hr, hr), :], dst_ref=got_refs[a],
                send_sem=send_sems.at[a], recv_sem=recv_sems.at[a], device_id=(x, y, 1 - c), device_id_type=MESH))
        for cp in remote:
            cp.start()
        for cp in remote:
            cp.wait_recv()
        for cp in remote:
            cp.wait_send()

    half = [jax.ShapeDtypeStruct((g.shape[0], g.shape[1] // 2, g.shape[2]), g.dtype) for g in grads]
    return pl.pallas_call(
        body, name="exchange_halves", in_specs=[ANY] * n, out_specs=[ANY] * n, out_shape=half,
        scratch_shapes=[pltpu.SemaphoreType.DMA((n,)), pltpu.SemaphoreType.DMA((n,))],
    )(*grads)


def _scatter_to_chips(sums16):
    n = len(sums16)

    def body(*refs):
        s16, got_refs = refs[:n], refs[n:2 * n]
        send_sems, recv_sems = refs[2 * n:]
        x, y, c, chips = _place()
        remote = []
        for a in range(n):
            for j, chip in enumerate(chips):
                remote.append(pltpu.make_async_remote_copy(
                    src_ref=s16[a].at[2 * chip[0] + chip[1]], dst_ref=got_refs[a].at[j],
                    send_sem=send_sems.at[3 * a + j], recv_sem=recv_sems.at[3 * a + j],
                    device_id=(*chip, c), device_id_type=MESH))
        for cp in remote:
            cp.start()
        for cp in remote:
            cp.wait_recv()
        for cp in remote:
            cp.wait_send()

    got = [jax.ShapeDtypeStruct((3,) + s.shape[1:], BF16) for s in sums16]
    return pl.pallas_call(
        body, name="scatter_to_chips", in_specs=[ANY] * n, out_specs=[ANY] * n, out_shape=got,
        scratch_shapes=[pltpu.SemaphoreType.DMA((3 * n,)), pltpu.SemaphoreType.DMA((3 * n,))],
    )(*sums16)


def _share_halves(bufs):
    n = len(bufs)

    def body(*refs):
        out_refs = refs[n:2 * n]
        send_sems, recv_sems = refs[2 * n:]
        x, y, c, _ = _place()
        remote = [pltpu.make_async_remote_copy(
            src_ref=out_refs[a].at[c], dst_ref=out_refs[a].at[c], send_sem=send_sems.at[a], recv_sem=recv_sems.at[a],
            device_id=(x, y, 1 - c), device_id_type=MESH) for a in range(n)]
        for cp in remote:
            cp.start()
        for a in range(n):
            pltpu.make_async_remote_copy(
                src_ref=out_refs[a].at[c], dst_ref=out_refs[a].at[1 - c], send_sem=send_sems.at[a],
                recv_sem=recv_sems.at[a], device_id=(x, y, 1 - c), device_id_type=MESH).wait_recv()
        for cp in remote:
            cp.wait_send()

    return pl.pallas_call(
        body, name="share_halves", in_specs=[ANY] * n, out_specs=[ANY] * n,
        out_shape=[jax.ShapeDtypeStruct(b.shape, b.dtype) for b in bufs],
        input_output_aliases={a: a for a in range(n)},
        scratch_shapes=[pltpu.SemaphoreType.DMA((n,)), pltpu.SemaphoreType.DMA((n,))],
    )(*bufs)


SMALL_ROWS, SMALL_COLS = 16, CONV_CH


def _allreduce_small(block):
    m_per, ncol = block.shape

    def body(x_ref, sum_ref, all_ref, send_sems, recv_sems, local_sem):
        x, y, c, chips = _place()
        me, sibling = (x, y, c), (x, y, 1 - c)

        def rows(px, py, pc):
            return all_ref.at[pl.ds((4 * px + 2 * py + pc) * m_per, m_per), :]

        def copy(k, block_of, to, src=None):
            return pltpu.make_async_remote_copy(
                src_ref=rows(*block_of) if src is None else src, dst_ref=rows(*block_of),
                send_sem=send_sems.at[k], recv_sem=recv_sems.at[k], device_id=to, device_id_type=MESH)

        mine = pltpu.make_async_copy(x_ref, rows(*me), local_sem)
        mine.start()
        first = [copy(0, me, sibling, src=x_ref)]
        first += [copy(1 + j, me, (*chip, c), src=x_ref) for j, chip in enumerate(chips)]
        for cp in first:
            cp.start()
        passed = [copy(4 + j, (*chip, c), sibling) for j, chip in enumerate(chips)]
        for j, chip in enumerate(chips):
            copy(1 + j, (*chip, c), me).wait_recv()
            passed[j].start()
        copy(0, sibling, me).wait_recv()
        for j, chip in enumerate(chips):
            copy(4 + j, (*chip, 1 - c), me).wait_recv()
        for cp in first + passed:
            cp.wait_send()
        mine.wait()
        total = all_ref[0:m_per, :]
        for d in range(1, N_DEV):
            total = total + all_ref[d * m_per:(d + 1) * m_per, :]
        sum_ref[...] = total

    vm = pl.BlockSpec(memory_space=pltpu.VMEM)
    return pl.pallas_call(
        body, name="allreduce_small", in_specs=[vm], out_specs=vm,
        out_shape=jax.ShapeDtypeStruct((m_per, ncol), F32),
        scratch_shapes=[pltpu.VMEM((N_DEV * m_per, ncol), F32), pltpu.SemaphoreType.DMA((7,)),
                        pltpu.SemaphoreType.DMA((7,)), pltpu.SemaphoreType.DMA],
    )(block)


def _row_tile(rows, cols):
    tile = rows
    while tile * cols * 4 > (1 << 20) and tile % 16 == 0:
        tile //= 2
    return tile


def _elementwise(fn, name, ins, out_dtypes):
    rows, cols = ins[0].shape
    tile = _row_tile(rows, cols)

    def body(*refs):
        outs = fn(*[r[...] for r in refs[:len(ins)]])
        for o_ref, o in zip(refs[len(ins):], outs):
            o_ref[...] = o.astype(o_ref.dtype)

    spec = pl.BlockSpec((tile, cols), lambda i: (i, 0))
    return pl.pallas_call(
        body, name=name, grid=(rows // tile,), in_specs=[spec] * len(ins), out_specs=[spec] * len(out_dtypes),
        out_shape=[jax.ShapeDtypeStruct((rows, cols), d) for d in out_dtypes],
        compiler_params=_params("parallel"),
    )(*ins)


def _adamw_tile(w, g, m, v):
    m = ADAM_B1 * m + (1.0 - ADAM_B1) * g
    v = ADAM_B2 * v + (1.0 - ADAM_B2) * jnp.square(g)
    m_hat = m / (1.0 - ADAM_B1 ** ADAM_STEP)
    v_hat = v / (1.0 - ADAM_B2 ** ADAM_STEP)
    delta = -ADAM_LR * (m_hat / (jnp.sqrt(v_hat) + ADAM_EPS) + ADAM_WD * w)
    return delta, m, v


def _adamw(name, w, g, m, v):
    return _elementwise(_adamw_tile, name, [w, g, m, v], [F32, F32, F32])


def _chip_sum(name, g4, got, place):
    nchip, hr, cols = got.shape
    tile = _row_tile(hr, cols)
    nblk = hr // tile

    def body(pl_ref, g_ref, o_ref, s32_ref, s16_ref):
        s = g_ref[...] + o_ref[...]
        s32_ref[...] = s
        s16_ref[...] = s.astype(BF16)

    spec = pl.BlockSpec((None, tile, cols), lambda k, i, pr: (k, i, 0))
    return pl.pallas_call(
        body, name=name,
        grid_spec=pltpu.PrefetchScalarGridSpec(
            num_scalar_prefetch=1, grid=(nchip, nblk),
            in_specs=[pl.BlockSpec((None, tile, cols), lambda k, i, pr: (k, pr[1] * nblk + i, 0)), spec],
            out_specs=[spec, spec]),
        out_shape=[jax.ShapeDtypeStruct(got.shape, F32), jax.ShapeDtypeStruct(got.shape, BF16)],
        compiler_params=_params("parallel", "parallel"),
    )(place, g4, got)


def _mesh_sum(name, s32, got, place):
    _, hr, cols = s32.shape
    tile = _row_tile(hr, cols)

    def body(pl_ref, own_ref, g0_ref, g1_ref, g2_ref, o_ref):
        o_ref[...] = ((own_ref[...] + g0_ref[...].astype(F32)) + g1_ref[...].astype(F32)) + g2_ref[...].astype(F32)

    slab = lambda j: pl.BlockSpec((None, tile, cols), lambda i, pr: (j, i, 0))
    return pl.pallas_call(
        body, name=name,
        grid_spec=pltpu.PrefetchScalarGridSpec(
            num_scalar_prefetch=1, grid=(hr // tile,),
            in_specs=[pl.BlockSpec((None, tile, cols), lambda i, pr: (pr[0], i, 0)), slab(0), slab(1), slab(2)],
            out_specs=pl.BlockSpec((None, tile, cols), lambda i, pr: (pr[1], i, 0))),
        out_shape=jax.ShapeDtypeStruct((2, hr, cols), F32),
        compiler_params=_params("parallel"),
    )(place, s32, got, got, got)


def _reduce_scatter(grads):
    names = list(grads)
    place = _place_operand()
    got_a = _exchange_halves([grads[k] for k in names])
    sums = [_chip_sum("chip_sum_" + k, grads[k], g, place) for k, g in zip(names, got_a)]
    got_b = _scatter_to_chips([s[1] for s in sums])
    return {k: _mesh_sum("mesh_sum_" + k, s[0], g, place) for k, s, g in zip(names, sums, got_b)}


def _place_operand():
    return jnp.stack([2 * lax.axis_index("x") + lax.axis_index("y"), lax.axis_index("c")]).astype(jnp.int32)


def _per_chip(name, g):
    if name == "w_in":
        return jnp.transpose(g.reshape(D_MODEL, N_CHIPS, D_IN // N_CHIPS), (1, 0, 2))
    if name == "w_proj":
        return jnp.transpose(g.reshape(PLE_DIM, N_CHIPS, D_MODEL // N_CHIPS), (1, 0, 2))
    if name == "w_up4":
        return g
    return g.reshape(N_CHIPS, g.shape[0] // N_CHIPS, g.shape[1])


class _EarlyReducer:
    def start(self, grads):
        self.names = list(grads)
        self.place = _place_operand()
        slabs = [_per_chip(k, grads[k]) for k in self.names]
        halves = [jax.ShapeDtypeStruct((s.shape[0], s.shape[1] // 2, s.shape[2]), F32) for s in slabs]
        self.a = _copies_start("exchange_start", _exchange_copies, len(slabs), slabs, halves, slabs[0])
        return self.a[-1]

    def middle(self, after):
        slabs, got = _copies_wait("exchange_wait", _exchange_copies, self.a, after)
        self.sums = [_chip_sum("early_chip_sum_" + k, s, g, self.place) for k, s, g in zip(self.names, slabs, got)]
        s16 = [s[1] for s in self.sums]
        lands = [jax.ShapeDtypeStruct((3,) + s.shape[1:], BF16) for s in s16]
        self.b = _copies_start("scatter_start", _scatter_copies, 3 * len(s16), s16, lands, s16[0])
        return self.b[-1]

    def finish(self, after):
        _, got = _copies_wait("scatter_wait", _scatter_copies, self.b, after)
        self.bufs = {k: _mesh_sum("early_mesh_sum_" + k, s[0], g, self.place)
                     for k, s, g in zip(self.names, self.sums, got)}


def kernel(x, p, norm_mix, w_in, conv_w, a_log, dt_bias, dn_norm, sinks, w_o, norm_mlp, w_up, w_down, norm_ple, w_ple_gate, w_ple_proj, norm_final, loss_target, m_norm_mix, m_w_in, m_conv_w, m_a_log, m_dt_bias, m_dn_norm, m_sinks, m_w_o, m_norm_mlp, m_w_up, m_w_down, m_norm_ple, m_w_ple_gate, m_w_ple_proj, m_norm_final, v_norm_mix, v_w_in, v_conv_w, v_a_log, v_dt_bias, v_dn_norm, v_sinks, v_w_o, v_norm_mlp, v_w_up, v_w_down, v_norm_ple, v_w_ple_gate, v_w_ple_proj, v_norm_final):
    chip = 2 * lax.axis_index("x") + lax.axis_index("y")
    big = dict(w_in=w_in[0], w_o=w_o[0], w_up=w_up[0], w_down=w_down[0], w_gate=w_ple_gate[0], w_proj=w_ple_proj[0])
    big_m = dict(w_in=m_w_in[0], w_o=m_w_o[0], w_up=m_w_up[0], w_down=m_w_down[0], w_gate=m_w_ple_gate[0], w_proj=m_w_ple_proj[0])
    big_v = dict(w_in=v_w_in[0], w_o=v_w_o[0], w_up=v_w_up[0], w_down=v_w_down[0], w_gate=v_w_ple_gate[0], w_proj=v_w_ple_proj[0])
    names = list(big)

    w_in_all, conv_all = _gather_weights([big["w_in"].astype(BF16)], conv_w[0])
    late_names = names[1:]
    late_shards = [big[k].astype(BF16) for k in late_names]
    gather = _copies_start("gather_start", _late_copies, LATE_COPIES * len(late_shards), late_shards,
                           [jax.ShapeDtypeStruct((N_CHIPS,) + s.shape, BF16) for s in late_shards], w_in_all)
    token = gather[-1]
    w = dict(w_in=jnp.transpose(w_in_all, (1, 0, 2)).reshape(D_MODEL, D_IN),
             conv_w=jnp.transpose(conv_all, (1, 0, 2)).reshape(DN_CONV, CONV_CH))
    sm = dict(norm_mix=norm_mix[0] + token[0, 0], a_log=a_log[0], dt_bias=dt_bias[0], dn_norm=dn_norm[0],
              sinks=sinks[0], norm_mlp=norm_mlp[0], norm_ple=norm_ple[0], norm_final=norm_final)

    def late(after):
        gw = dict(zip(late_names, _copies_wait("gather_wait", _late_copies, gather, after)[1]))
        return dict(w_o=gw["w_o"].reshape(D_MODEL, D_MODEL), w_up4=gw["w_up"], w_down=gw["w_down"].reshape(D_FF, D_MODEL),
                    w_gate=gw["w_gate"].reshape(D_MODEL, D_MODEL), w_proj4=gw["w_proj"])

    reducer = _EarlyReducer()
    sums, grad_x, g = _local_step(x[0], p[0, 0], loss_target[0], sm, w, late, reducer)

    bufs = dict(reducer.bufs, **_reduce_scatter({k: _per_chip(k, g[k]) for k in ("w_in", "w_o")}))
    grad_key = dict(w_in="w_in", w_o="w_o", w_up="w_up4", w_down="w_down", w_gate="w_gate", w_proj="w_proj")
    full = _share_halves([bufs[grad_key[k]] for k in names])
    red = {k: f.reshape(-1, f.shape[-1]) for k, f in zip(names, full)}

    row = lambda v: jnp.zeros((SMALL_COLS,), F32).at[:v.shape[0]].set(v)
    misc = jnp.zeros((SMALL_COLS,), F32).at[0:4].set(sums["a_log"]).at[4:8].set(sums["dt_bias"]) \
        .at[8:16].set(sums["sinks"]).at[128:256].set(sums["dn_norm"]).at[256].set(sums["loss"])
    small = jnp.concatenate([sums["conv_w"], jnp.stack([row(sums["norm_mix"]), row(sums["norm_mlp"]), row(sums["norm_ple"]),
                                                        row(sums["norm_final"]), misc]),
                             jnp.zeros((SMALL_ROWS - 9, SMALL_COLS), F32)], axis=0)
    tot = _allreduce_small(small)
    loss = tot[8, 256]
    ncw = CONV_CH // N_CHIPS

    def pack(cw, nmix, nmlp, nple, nfin, al, dtb, sk, dnn):
        misc_p = jnp.zeros((SMALL_COLS,), F32).at[0:4].set(al).at[4:8].set(dtb).at[8:16].set(sk).at[128:256].set(dnn)
        cw_p = jnp.zeros((DN_CONV, SMALL_COLS), F32).at[:, :ncw].set(cw)
        return jnp.concatenate([cw_p, jnp.stack([row(nmix), row(nmlp), row(nple), row(nfin), misc_p]),
                                jnp.zeros((SMALL_ROWS - 9, SMALL_COLS), F32)], axis=0)

    def unpack(buf):
        return dict(conv_w=buf[0:4, :ncw][None], norm_mix=buf[4, :D_MODEL][None], norm_mlp=buf[5, :D_MODEL][None],
                    norm_ple=buf[6, :D_MODEL][None], norm_final=buf[7, :D_MODEL], a_log=buf[8, 0:4][None],
                    dt_bias=buf[8, 4:8][None], sinks=buf[8, 8:16][None], dn_norm=buf[8, 128:256][None])

    g_conv_shard = lax.dynamic_slice(tot[0:4], (0, chip * ncw), (DN_CONV, ncw))
    g_small = pack(g_conv_shard, tot[4, :D_MODEL], tot[5, :D_MODEL], tot[6, :D_MODEL], tot[7, :D_MODEL],
                   tot[8, 0:4], tot[8, 4:8], tot[8, 8:16], tot[8, 128:256])
    w_small = pack(conv_w[0], norm_mix[0], norm_mlp[0], norm_ple[0], norm_final, a_log[0], dt_bias[0], sinks[0], dn_norm[0])
    m_small = pack(m_conv_w[0], m_norm_mix[0], m_norm_mlp[0], m_norm_ple[0], m_norm_final, m_a_log[0], m_dt_bias[0],
                   m_sinks[0], m_dn_norm[0])
    v_small = pack(v_conv_w[0], v_norm_mix[0], v_norm_mlp[0], v_norm_ple[0], v_norm_final, v_a_log[0], v_dt_bias[0],
                   v_sinks[0], v_dn_norm[0])

    d_s, m_s, v_s = (unpack(b) for b in _adamw("adamw_small", w_small, g_small, m_small, v_small))
    g_s = unpack(g_small)
    out_g, out_d, out_m, out_v = dict(g_s), dict(d_s), dict(m_s), dict(v_s)
    ref_name = dict(w_in="w_in", w_o="w_o", w_up="w_up", w_down="w_down", w_gate="w_ple_gate", w_proj="w_ple_proj")
    for k in names:
        d_k, m_k, v_k = _adamw("adamw_" + k, big[k], red[k], big_m[k], big_v[k])
        out_g[ref_name[k]], out_d[ref_name[k]] = red[k][None], d_k[None]
        out_m[ref_name[k]], out_v[ref_name[k]] = m_k[None], v_k[None]
    order = ["norm_mix", "w_in", "conv_w", "a_log", "dt_bias", "dn_norm", "sinks", "w_o", "norm_mlp", "w_up", "w_down",
             "norm_ple", "w_ple_gate", "w_ple_proj", "norm_final"]
    return (loss, grad_x[None], *[out_g[k] for k in order], *[out_d[k] for k in order],
            *[out_m[k] for k in order], *[out_v[k] for k in order])
```

```python
import functools

import jax
import jax.numpy as jnp
from jax import lax
from jax.experimental import pallas as pl
from jax.experimental.pallas import tpu as pltpu

F32 = jnp.float32
BF16 = jnp.bfloat16
MXU_DTYPE = jnp.bfloat16
HI = lax.Precision.HIGHEST

D_MODEL = 1024
PLE_DIM = 256
ATTN_HEADS = 8
ATTN_KV_HEADS = 2
ATTN_GROUPS = ATTN_HEADS // ATTN_KV_HEADS
ATTN_HEAD_DIM = 64
ATTN_BLOCK = 128
ROPE_THETA = 10000.0
DN_HEADS = 4
DN_HEAD_DIM = 128
DN_CONV = 4
DN_CHUNK = 64
D_FF = 4 * D_MODEL
EPS = 1e-6
ATTN_Q = ATTN_HEADS * ATTN_HEAD_DIM
ATTN_KV = ATTN_KV_HEADS * ATTN_HEAD_DIM
DN_W = DN_HEADS * DN_HEAD_DIM
CONV_CH = 3 * DN_W
D_IN = ATTN_Q + 2 * ATTN_KV + 4 * DN_W + 2 * DN_HEADS
DN_COLS = 4 * DN_W + 128
DN_SCALE = DN_HEAD_DIM ** -0.5
ATTN_SCALE = ATTN_HEAD_DIM ** -0.5
FF_BLOCKS = 4
FF_BLOCK = D_FF // FF_BLOCKS

ADAM_LR = 0.001
ADAM_B1 = 0.9
ADAM_B2 = 0.999
ADAM_EPS = 1e-08
ADAM_WD = 0.01
ADAM_STEP = 10

V7X_VMEM_BYTES = 64 * 1024 * 1024
VMEM_LIMIT = 48 * 1024 * 1024

NN = ((1,), (0,))
NT = ((1,), (1,))
TN = ((0,), (0,))


def _dot(a, b, dims=NN, prec=None):
    return lax.dot_general(a, b, (dims, ((), ())), precision=prec, preferred_element_type=F32)


def _sigmoid(x):
    return 1.0 / (1.0 + jnp.exp(-x))


def _softplus(x):
    return jnp.maximum(x, 0.0) + jnp.log(1.0 + jnp.exp(-jnp.abs(x)))


def _params(*sem):
    return pltpu.CompilerParams(dimension_semantics=sem, vmem_limit_bytes=VMEM_LIMIT)


def _rms_fwd(xv, g):
    r = lax.rsqrt(jnp.mean(xv * xv, axis=-1, keepdims=True) + EPS)
    return xv * r * g


def _rms_bwd(xv, g, dn):
    r = lax.rsqrt(jnp.mean(xv * xv, axis=-1, keepdims=True) + EPS)
    xh = xv * r
    dg = jnp.sum(dn * xh, axis=0, keepdims=True)
    dxh = dn * g
    dx = r * (dxh - xh * jnp.mean(dxh * xh, axis=-1, keepdims=True))
    return dx, dg


def _full(shape):
    return pl.BlockSpec(shape, lambda *_: (0,) * len(shape))


def _inproj(x, g_mix, wa, wd, tm):
    t = x.shape[0]

    def body(x_ref, g_ref, wa_ref, wd_ref, u_ref, pa_ref, pd_ref):
        u = _rms_fwd(x_ref[...], g_ref[...]).astype(MXU_DTYPE)
        u_ref[...] = u
        pa_ref[...] = _dot(u, wa_ref[...])
        pd_ref[...] = _dot(u, wd_ref[...])

    na, nd = wa.shape[1], wd.shape[1]
    return pl.pallas_call(
        body, name="inproj", grid=(t // tm,),
        in_specs=[pl.BlockSpec((tm, D_MODEL), lambda i: (i, 0)), _full((1, D_MODEL)),
                  _full((D_MODEL, na)), _full((D_MODEL, nd))],
        out_specs=[pl.BlockSpec((tm, D_MODEL), lambda i: (i, 0)), pl.BlockSpec((tm, na), lambda i: (i, 0)),
                   pl.BlockSpec((tm, nd), lambda i: (i, 0))],
        out_shape=[jax.ShapeDtypeStruct((t, D_MODEL), MXU_DTYPE), jax.ShapeDtypeStruct((t, na), F32),
                   jax.ShapeDtypeStruct((t, nd), F32)],
        compiler_params=_params("parallel"),
    )(x, g_mix, wa, wd)


def _oproj(x, ao, dn, wo_a, wo_d, tm):
    t = x.shape[0]

    def body(x_ref, ao_ref, dn_ref, wa_ref, wd_ref, h_ref):
        h_ref[...] = (x_ref[...] + _dot(ao_ref[...].astype(MXU_DTYPE), wa_ref[...])
                      + _dot(dn_ref[...].astype(MXU_DTYPE), wd_ref[...]))

    half = ao.shape[1]
    return pl.pallas_call(
        body, name="oproj", grid=(t // tm,),
        in_specs=[pl.BlockSpec((tm, D_MODEL), lambda i: (i, 0)), pl.BlockSpec((tm, half), lambda i: (i, 0)),
                  pl.BlockSpec((tm, half), lambda i: (i, 0)), _full((half, D_MODEL)), _full((half, D_MODEL))],
        out_specs=pl.BlockSpec((tm, D_MODEL), lambda i: (i, 0)),
        out_shape=jax.ShapeDtypeStruct((t, D_MODEL), F32),
        compiler_params=_params("parallel"),
    )(x, ao, dn, wo_a, wo_d)


def _mlp_fwd(h1, g_mlp, w_up4, w_down, tm):
    t = h1.shape[0]

    def body(h_ref, g_ref, wu_ref, wd_ref, m_ref, r_ref, h2_ref, acc_ref):
        k = pl.program_id(1)

        @pl.when(k == 0)
        def _():
            m_ref[...] = _rms_fwd(h_ref[...], g_ref[...]).astype(MXU_DTYPE)
            acc_ref[...] = jnp.zeros_like(acc_ref)

        r = jnp.maximum(_dot(m_ref[...], wu_ref[...]), 0.0)
        r_ref[...] = r.astype(MXU_DTYPE)
        s = jnp.square(r).astype(MXU_DTYPE)
        acc_ref[...] += _dot(s, wd_ref[...])

        @pl.when(k == FF_BLOCKS - 1)
        def _():
            h2_ref[...] = h_ref[...] + acc_ref[...]

    return pl.pallas_call(
        body, name="mlp_fwd", grid=(t // tm, FF_BLOCKS),
        in_specs=[pl.BlockSpec((tm, D_MODEL), lambda i, k: (i, 0)), _full((1, D_MODEL)),
                  pl.BlockSpec((None, D_MODEL, FF_BLOCK), lambda i, k: (k, 0, 0)),
                  pl.BlockSpec((FF_BLOCK, D_MODEL), lambda i, k: (k, 0))],
        out_specs=[pl.BlockSpec((tm, D_MODEL), lambda i, k: (i, 0)), pl.BlockSpec((tm, FF_BLOCK), lambda i, k: (i, k)),
                   pl.BlockSpec((tm, D_MODEL), lambda i, k: (i, 0))],
        out_shape=[jax.ShapeDtypeStruct((t, D_MODEL), MXU_DTYPE), jax.ShapeDtypeStruct((t, D_FF), MXU_DTYPE),
                   jax.ShapeDtypeStruct((t, D_MODEL), F32)],
        scratch_shapes=[pltpu.VMEM((tm, D_MODEL), F32)],
        compiler_params=_params("parallel", "arbitrary"),
    )(h1, g_mlp, w_up4, w_down)


def _ple_loss(h2, p, tgt, g_ple, g_fin, w_gate, w_proj, tm):
    t = h2.shape[0]

    def body(h_ref, p_ref, t_ref, gp_ref, gf_ref, wg_ref, wp_ref,
             dh_ref, dhb_ref, dgp_ref, dpp_ref, n3_ref, pb_ref, acc_ref):
        @pl.when(pl.program_id(0) == 0)
        def _():
            acc_ref[...] = jnp.zeros_like(acc_ref)

        h = h_ref[...]
        g_ple_v, g_fin_v = gp_ref[...], gf_ref[...]
        n3 = _rms_fwd(h, g_ple_v).astype(MXU_DTYPE)
        n3_ref[...] = n3
        gate = _sigmoid(_dot(n3, wg_ref[...]))
        pb = p_ref[...].astype(MXU_DTYPE)
        pb_ref[...] = pb
        pp = _dot(pb, wp_ref[...])
        h3 = h + gate * pp
        r4 = lax.rsqrt(jnp.mean(h3 * h3, axis=-1, keepdims=True) + EPS)
        xh4 = h3 * r4
        e = xh4 * g_fin_v - t_ref[...]
        loss = 0.5 * jnp.sum(jnp.mean(e * e, axis=-1, keepdims=True), axis=0, keepdims=True)
        dy = e * (1.0 / D_MODEL)
        dg_fin = jnp.sum(dy * xh4, axis=0, keepdims=True)
        dxh = dy * g_fin_v
        dh3 = r4 * (dxh - xh4 * jnp.mean(dxh * xh4, axis=-1, keepdims=True))
        dpp_ref[...] = (dh3 * gate).astype(MXU_DTYPE)
        dgp = (dh3 * pp * gate * (1.0 - gate)).astype(MXU_DTYPE)
        dgp_ref[...] = dgp
        dn3 = _dot(dgp, wg_ref[...], NT)
        dx, dg_ple = _rms_bwd(h, g_ple_v, dn3)
        dh2 = dh3 + dx
        dh_ref[...] = dh2
        dhb_ref[...] = dh2.astype(MXU_DTYPE)
        acc_ref[0:1, :] += dg_fin
        acc_ref[1:2, :] += dg_ple
        acc_ref[2:3, :] += jnp.broadcast_to(loss, (1, D_MODEL))

    row = lambda w: pl.BlockSpec((tm, w), lambda i: (i, 0))
    return pl.pallas_call(
        body, name="ple_loss", grid=(t // tm,),
        in_specs=[row(D_MODEL), row(PLE_DIM), row(D_MODEL), _full((1, D_MODEL)), _full((1, D_MODEL)),
                  _full((D_MODEL, D_MODEL)), _full((PLE_DIM, D_MODEL))],
        out_specs=[row(D_MODEL), row(D_MODEL), row(D_MODEL), row(D_MODEL), row(D_MODEL), row(PLE_DIM),
                   _full((8, D_MODEL))],
        out_shape=[jax.ShapeDtypeStruct((t, D_MODEL), F32), jax.ShapeDtypeStruct((t, D_MODEL), MXU_DTYPE),
                   jax.ShapeDtypeStruct((t, D_MODEL), MXU_DTYPE), jax.ShapeDtypeStruct((t, D_MODEL), MXU_DTYPE),
                   jax.ShapeDtypeStruct((t, D_MODEL), MXU_DTYPE), jax.ShapeDtypeStruct((t, PLE_DIM), MXU_DTYPE),
                   jax.ShapeDtypeStruct((8, D_MODEL), F32)],
        compiler_params=_params("arbitrary"),
    )(h2, p, tgt, g_ple, g_fin, w_gate, w_proj)


def _mlp_bwd(dh2, dh2b, r, h1, g_mlp, w_up4, w_down, tm):
    t = h1.shape[0]

    def body(dh_ref, dhb_ref, r_ref, h_ref, g_ref, wu_ref, wd_ref,
             da_ref, dh1_ref, dh1b_ref, acc_ref, dm_ref):
        i, k = pl.program_id(0), pl.program_id(1)

        @pl.when((i == 0) & (k == 0))
        def _():
            acc_ref[...] = jnp.zeros_like(acc_ref)

        @pl.when(k == 0)
        def _():
            dm_ref[...] = jnp.zeros_like(dm_ref)

        ds = _dot(dhb_ref[...], wd_ref[...], NT)
        da = (ds * (2.0 * r_ref[...].astype(F32))).astype(MXU_DTYPE)
        da_ref[...] = da
        dm_ref[...] += _dot(da, wu_ref[...], NT)

        @pl.when(k == FF_BLOCKS - 1)
        def _():
            dx, dg = _rms_bwd(h_ref[...], g_ref[...], dm_ref[...])
            dh1 = dh_ref[...] + dx
            dh1_ref[...] = dh1
            dh1b_ref[...] = dh1.astype(MXU_DTYPE)
            acc_ref[0:1, :] += dg

    tok = lambda w: pl.BlockSpec((tm, w), lambda i, k: (i, 0))
    return pl.pallas_call(
        body, name="mlp_bwd", grid=(t // tm, FF_BLOCKS),
        in_specs=[tok(D_MODEL), tok(D_MODEL), pl.BlockSpec((tm, FF_BLOCK), lambda i, k: (i, k)), tok(D_MODEL),
                  _full((1, D_MODEL)), pl.BlockSpec((None, D_MODEL, FF_BLOCK), lambda i, k: (k, 0, 0)),
                  pl.BlockSpec((FF_BLOCK, D_MODEL), lambda i, k: (k, 0))],
        out_specs=[pl.BlockSpec((tm, FF_BLOCK), lambda i, k: (i, k)),
                   tok(D_MODEL), tok(D_MODEL), pl.BlockSpec((8, D_MODEL), lambda i, k: (0, 0))],
        out_shape=[jax.ShapeDtypeStruct((t, D_FF), MXU_DTYPE),
                   jax.ShapeDtypeStruct((t, D_MODEL), F32), jax.ShapeDtypeStruct((t, D_MODEL), MXU_DTYPE),
                   jax.ShapeDtypeStruct((8, D_MODEL), F32)],
        scratch_shapes=[pltpu.VMEM((tm, D_MODEL), F32)],
        compiler_params=_params("arbitrary", "arbitrary"),
    )(dh2, dh2b, r, h1, g_mlp, w_up4, w_down)


def _oproj_bwd(dh1b, wo_a, wo_d, tm, dep):
    t = dh1b.shape[0]
    half = wo_a.shape[0]

    def body(d_ref, wa_ref, wd_ref, dep_ref, da_ref, dd_ref):
        d = d_ref[...]
        da_ref[...] = _dot(d, wa_ref[...], NT)
        dd_ref[...] = _dot(d, wd_ref[...], NT)

    return pl.pallas_call(
        body, name="oproj_bwd", grid=(t // tm,),
        in_specs=[pl.BlockSpec((tm, D_MODEL), lambda i: (i, 0)), _full((half, D_MODEL)), _full((half, D_MODEL)),
                  pl.BlockSpec(memory_space=pl.ANY)],
        out_specs=[pl.BlockSpec((tm, half), lambda i: (i, 0)), pl.BlockSpec((tm, half), lambda i: (i, 0))],
        out_shape=[jax.ShapeDtypeStruct((t, half), F32), jax.ShapeDtypeStruct((t, half), F32)],
        compiler_params=_params("parallel"),
    )(dh1b, wo_a, wo_d, dep)


def _inproj_bwd(x, dh1, g_mix, grads, weights, tm):
    t = x.shape[0]
    n = len(grads)

    def body(*refs):
        x_ref, dh_ref, g_ref = refs[:3]
        g_refs, w_refs = refs[3:3 + n], refs[3 + n:3 + 2 * n]
        dx_ref, acc_ref = refs[3 + 2 * n:]

        @pl.when(pl.program_id(0) == 0)
        def _():
            acc_ref[...] = jnp.zeros_like(acc_ref)

        du = _dot(g_refs[0][...], w_refs[0][...], NT)
        for j in range(1, n):
            du += _dot(g_refs[j][...], w_refs[j][...], NT)
        dx, dg = _rms_bwd(x_ref[...], g_ref[...], du)
        dx_ref[...] = dh_ref[...] + dx
        acc_ref[0:1, :] += dg

    tok = lambda w: pl.BlockSpec((tm, w), lambda i: (i, 0))
    return pl.pallas_call(
        body, name="inproj_bwd", grid=(t // tm,),
        in_specs=[tok(D_MODEL), tok(D_MODEL), _full((1, D_MODEL))] + [tok(g.shape[1]) for g in grads]
                 + [_full(w.shape) for w in weights],
        out_specs=[tok(D_MODEL), _full((8, D_MODEL))],
        out_shape=[jax.ShapeDtypeStruct((t, D_MODEL), F32), jax.ShapeDtypeStruct((8, D_MODEL), F32)],
        compiler_params=_params("arbitrary"),
    )(x, dh1, g_mix, *grads, *weights)


def _wgrad(a, b, name, tk, tn, tt, stacked=False, prep=None):
    t, kdim = a.shape
    ncols = b.shape[1]

    def body(a_ref, b_ref, o_ref):
        @pl.when(pl.program_id(2) == 0)
        def _():
            o_ref[...] = jnp.zeros_like(o_ref)

        av = a_ref[...] if prep is None else prep(a_ref[...])
        o_ref[...] += _dot(av, b_ref[...], TN)

    if stacked:
        out_spec = pl.BlockSpec((None, tk, tn), lambda i, j, s: (j, i, 0))
        out_shape = jax.ShapeDtypeStruct((ncols // tn, kdim, tn), F32)
    else:
        out_spec = pl.BlockSpec((tk, tn), lambda i, j, s: (i, j))
        out_shape = jax.ShapeDtypeStruct((kdim, ncols), F32)
    return pl.pallas_call(
        body, name=name, grid=(kdim // tk, ncols // tn, t // tt),
        in_specs=[pl.BlockSpec((tt, tk), lambda i, j, s: (s, i)), pl.BlockSpec((tt, tn), lambda i, j, s: (s, j))],
        out_specs=out_spec, out_shape=out_shape,
        compiler_params=_params("parallel", "parallel", "arbitrary"),
    )(a, b)


def _rope_tables(t):
    half = ATTN_HEAD_DIM // 2
    inv = 1.0 / (ROPE_THETA ** (jnp.arange(half, dtype=F32) * (2.0 / ATTN_HEAD_DIM)))
    ang = jnp.arange(t, dtype=F32)[:, None] * inv[None, :]
    cos, sin = jnp.cos(ang), jnp.sin(ang)
    cos2 = jnp.concatenate([cos, cos], axis=-1)
    sin2 = jnp.concatenate([-sin, sin], axis=-1)
    return jnp.tile(cos2, (1, 2)), jnp.tile(sin2, (1, 2))


def _swap_halves(tv):
    w = tv.shape[-1]
    lane = lax.broadcasted_iota(jnp.int32, tv.shape, tv.ndim - 1)
    first = (lane % ATTN_HEAD_DIM) < (ATTN_HEAD_DIM // 2)
    return jnp.where(first, pltpu.roll(tv, w - ATTN_HEAD_DIM // 2, tv.ndim - 1),
                     pltpu.roll(tv, ATTN_HEAD_DIM // 2, tv.ndim - 1))


def _rope(tv, cos, sin):
    return tv * cos + _swap_halves(tv) * sin


def _rope_bwd(dv, cos, sin):
    return dv * cos + _swap_halves(dv * sin)


def _attn_valid(first_block):
    c = lax.broadcasted_iota(jnp.int32, (2 * ATTN_BLOCK, ATTN_BLOCK), 0)
    r = lax.broadcasted_iota(jnp.int32, (2 * ATTN_BLOCK, ATTN_BLOCK), 1)
    return (c > r) & (c <= r + ATTN_BLOCK) & ((c >= ATTN_BLOCK) | jnp.logical_not(first_block))


def _attn_probs(st, sink, valid):
    s = jnp.where(valid, st * ATTN_SCALE, -jnp.inf)
    m = jnp.maximum(jnp.max(s, axis=0, keepdims=True), sink)
    e = jnp.where(valid, jnp.exp(s - m), 0.0)
    es = jnp.exp(sink - m)
    inv = 1.0 / (jnp.sum(e, axis=0, keepdims=True) + es)
    return e * inv, es * inv


def _lane_scalar(vec, idx):
    lane = lax.broadcasted_iota(jnp.int32, vec.shape, 1)
    return jnp.sum(jnp.where(lane == idx, vec, 0.0), axis=-1, keepdims=True)


def _attn_specs(nb):
    cur = lambda w, cb: pl.BlockSpec((ATTN_BLOCK, w), lambda i: (jnp.minimum(i, nb - 1), cb))
    prev = lambda w, cb: pl.BlockSpec((ATTN_BLOCK, w), lambda i: (jnp.maximum(jnp.minimum(i, nb - 1) - 1, 0), cb))
    kcol, vcol = ATTN_Q // ATTN_KV, ATTN_Q // ATTN_KV + 1
    return [cur(ATTN_Q, 0), cur(ATTN_KV, kcol), prev(ATTN_KV, kcol), cur(ATTN_KV, vcol), prev(ATTN_KV, vcol),
            cur(ATTN_KV, 0), cur(ATTN_KV, 0), prev(ATTN_KV, 0), prev(ATTN_KV, 0), _full((1, 128))]


def _attn_fwd(pa, cos, sin, sinks_vec):
    t = pa.shape[0]
    nb = t // ATTN_BLOCK

    def body(q_ref, kc_ref, kp_ref, vc_ref, vp_ref, cc_ref, sc_ref, cp_ref, sp_ref, sk_ref, o_ref):
        first = pl.program_id(0) == 0
        cc, sc = cc_ref[...], sc_ref[...]
        q = _rope(q_ref[...], jnp.tile(cc, (1, ATTN_Q // ATTN_KV)), jnp.tile(sc, (1, ATTN_Q // ATTN_KV)))
        kc = _rope(kc_ref[...], cc, sc)
        kp = _rope(kp_ref[...], cp_ref[...], sp_ref[...])
        vc, vp = vc_ref[...], vp_ref[...]
        sk = sk_ref[...]
        valid = _attn_valid(first)
        kv = lambda tp, tc, hk: jnp.concatenate([tp[:, hk * ATTN_HEAD_DIM:(hk + 1) * ATTN_HEAD_DIM],
                                                 tc[:, hk * ATTN_HEAD_DIM:(hk + 1) * ATTN_HEAD_DIM]], axis=0)
        kwins = [kv(kp, kc, hk) for hk in range(ATTN_KV_HEADS)]
        vwins_t = [kv(vp, vc, hk).T for hk in range(ATTN_KV_HEADS)]
        heads = [slice(h * ATTN_HEAD_DIM, (h + 1) * ATTN_HEAD_DIM) for h in range(ATTN_HEADS)]
        scores = [_dot(kwins[h // ATTN_GROUPS], q[:, hs], NT) for h, hs in enumerate(heads)]
        probs = [_attn_probs(st, _lane_scalar(sk, h), valid)[0] for h, st in enumerate(scores)]
        for h, (hs, pt) in enumerate(zip(heads, probs)):
            o_ref[:, hs] = _dot(vwins_t[h // ATTN_GROUPS], pt).T.astype(o_ref.dtype)

    return pl.pallas_call(
        body, name="attn_fwd", grid=(nb,),
        in_specs=_attn_specs(nb),
        out_specs=pl.BlockSpec((ATTN_BLOCK, ATTN_Q), lambda i: (i, 0)),
        out_shape=jax.ShapeDtypeStruct((t, ATTN_Q), MXU_DTYPE),
        compiler_params=_params("parallel"),
    )(pa, pa, pa, pa, pa, cos, sin, cos, sin, sinks_vec)


def _attn_bwd(pa, cos, sin, sinks_vec, dao):
    t = pa.shape[0]
    nb = t // ATTN_BLOCK

    def body(q_ref, kc_ref, kp_ref, vc_ref, vp_ref, cc_ref, sc_ref, cp_ref, sp_ref, sk_ref, do_ref,
             dq_ref, dk_ref, dv_ref, acc_ref, dqr_ref, dkw_ref, dvw_ref, ck_ref, cv_ref):
        i = pl.program_id(0)

        @pl.when(i == 0)
        def _():
            acc_ref[...] = jnp.zeros_like(acc_ref)
            ck_ref[...] = jnp.zeros_like(ck_ref)
            cv_ref[...] = jnp.zeros_like(cv_ref)

        @pl.when(i < nb)
        def _():
            first = i == 0
            cc, sc = cc_ref[...], sc_ref[...]
            cq, sq = jnp.tile(cc, (1, ATTN_Q // ATTN_KV)), jnp.tile(sc, (1, ATTN_Q // ATTN_KV))
            q = _rope(q_ref[...], cq, sq)
            kc = _rope(kc_ref[...], cc, sc)
            kp = _rope(kp_ref[...], cp_ref[...], sp_ref[...])
            vc, vp = vc_ref[...], vp_ref[...]
            sk = sk_ref[...]
            do = do_ref[...]
            lane = lax.broadcasted_iota(jnp.int32, (1, 128), 1)
            dsink = jnp.zeros((1, 128), F32)
            valid = _attn_valid(first)
            kv = lambda tp, tc, hk: jnp.concatenate([tp[:, hk * ATTN_HEAD_DIM:(hk + 1) * ATTN_HEAD_DIM],
                                                     tc[:, hk * ATTN_HEAD_DIM:(hk + 1) * ATTN_HEAD_DIM]], axis=0)
            kwins = [kv(kp, kc, hk) for hk in range(ATTN_KV_HEADS)]
            vwins = [kv(vp, vc, hk) for hk in range(ATTN_KV_HEADS)]
            kwins_t = [kw.T for kw in kwins]
            heads = [slice(h * ATTN_HEAD_DIM, (h + 1) * ATTN_HEAD_DIM) for h in range(ATTN_HEADS)]
            scores = [_dot(kwins[h // ATTN_GROUPS], q[:, hs], NT) for h, hs in enumerate(heads)]
            dps = [_dot(vwins[h // ATTN_GROUPS], do[:, hs], NT) for h, hs in enumerate(heads)]
            pts, dsts = [], []
            for h, (st, dp_t) in enumerate(zip(scores, dps)):
                probs_t, psink = _attn_probs(st, _lane_scalar(sk, h), valid)
                delta = jnp.sum(probs_t * dp_t, axis=0, keepdims=True)
                pts.append(probs_t)
                dsts.append(probs_t * (dp_t - delta) * ATTN_SCALE)
                dsink += jnp.where(lane == h, jnp.sum(-psink * delta, axis=1, keepdims=True), 0.0)
            for h, (hs, ds_t) in enumerate(zip(heads, dsts)):
                dqr_ref[:, hs] = _dot(kwins_t[h // ATTN_GROUPS], ds_t).T
            for hk in range(ATTN_KV_HEADS):
                ks = slice(hk * ATTN_HEAD_DIM, (hk + 1) * ATTN_HEAD_DIM)
                group = range(hk * ATTN_GROUPS, (hk + 1) * ATTN_GROUPS)
                ds_g = jnp.concatenate([dsts[h] for h in group], axis=1)
                p_g = jnp.concatenate([pts[h] for h in group], axis=1)
                q_g = jnp.concatenate([q[:, heads[h]] for h in group], axis=0)
                do_g = jnp.concatenate([do[:, heads[h]] for h in group], axis=0)
                dkw_ref[:, ks] = _dot(ds_g, q_g)
                dvw_ref[:, ks] = _dot(p_g, do_g)
            acc_ref[0:1, :] += dsink
            dq_ref[...] = _rope_bwd(dqr_ref[...], cq, sq).astype(dq_ref.dtype)
            dk_ref[...] = (ck_ref[...] + _rope_bwd(dkw_ref[0:ATTN_BLOCK, :], cp_ref[...], sp_ref[...])).astype(dk_ref.dtype)
            dv_ref[...] = (cv_ref[...] + dvw_ref[0:ATTN_BLOCK, :]).astype(dv_ref.dtype)
            ck_ref[...] = _rope_bwd(dkw_ref[ATTN_BLOCK:2 * ATTN_BLOCK, :], cc, sc)
            cv_ref[...] = dvw_ref[ATTN_BLOCK:2 * ATTN_BLOCK, :]

        @pl.when(i == nb)
        def _():
            dk_ref[...] = ck_ref[...].astype(dk_ref.dtype)
            dv_ref[...] = cv_ref[...].astype(dv_ref.dtype)

    prev_out = lambda w: pl.BlockSpec((ATTN_BLOCK, w), lambda i: (jnp.maximum(i - 1, 0), 0))
    return pl.pallas_call(
        body, name="attn_bwd", grid=(nb + 1,),
        in_specs=_attn_specs(nb) + [pl.BlockSpec((ATTN_BLOCK, ATTN_Q), lambda i: (jnp.minimum(i, nb - 1), 0))],
        out_specs=[pl.BlockSpec((ATTN_BLOCK, ATTN_Q), lambda i: (jnp.minimum(i, nb - 1), 0)), prev_out(ATTN_KV),
                   prev_out(ATTN_KV), _full((8, 128))],
        out_shape=[jax.ShapeDtypeStruct((t, ATTN_Q), MXU_DTYPE), jax.ShapeDtypeStruct((t, ATTN_KV), MXU_DTYPE),
                   jax.ShapeDtypeStruct((t, ATTN_KV), MXU_DTYPE), jax.ShapeDtypeStruct((8, 128), F32)],
        scratch_shapes=[pltpu.VMEM((ATTN_BLOCK, ATTN_Q), F32), pltpu.VMEM((2 * ATTN_BLOCK, ATTN_KV), F32),
                        pltpu.VMEM((2 * ATTN_BLOCK, ATTN_KV), F32), pltpu.VMEM((ATTN_BLOCK, ATTN_KV), F32),
                        pltpu.VMEM((ATTN_BLOCK, ATTN_KV), F32)],
        compiler_params=_params("arbitrary"),
    )(pa, pa, pa, pa, pa, cos, sin, cos, sin, sinks_vec, dao)


PAIR = 2 * DN_CHUNK
HALO = 8


def _conv_window(cur_ref, prev_ref, xs_ref, tm):
    prev = jnp.where(pl.program_id(0) > 0, prev_ref[...], 0.0)
    xs_ref[0:HALO, :] = prev
    xs_ref[HALO:HALO + tm, :] = cur_ref[...]


def _conv_taps(xs_ref, cw_ref, tm):
    y = cw_ref[0:1, :] * xs_ref[pl.ds(HALO - DN_CONV + 1, tm), :]
    for j in range(1, DN_CONV):
        y += cw_ref[j:j + 1, :] * xs_ref[pl.ds(HALO - DN_CONV + 1 + j, tm), :]
    return y


def _gate_values(ba, al, dt):
    beta = _sigmoid(ba)
    pre = ba + dt
    g = -jnp.exp(al) * _softplus(pre)
    return beta, g, pre


def _dn_prep_specs(tm, t):
    return [pl.BlockSpec((tm, CONV_CH), lambda i: (i, 0)),
            pl.BlockSpec((HALO, CONV_CH), lambda i: (jnp.maximum(i * (tm // HALO) - 1, 0), 0)),
            pl.BlockSpec((tm, 128), lambda i: (i, 4 * DN_W // 128)),
            _full((DN_CONV, CONV_CH)), _full((1, 128)), _full((1, 128))]


def _dn_prep(pd, conv_w, al_vec, dt_vec, tm):
    t = pd.shape[0]

    def body(cur_ref, prev_ref, ba_ref, cw_ref, al_ref, dt_ref, qn_ref, kn_ref, vc_ref, gc_ref, gr_ref, xs_ref):
        _conv_window(cur_ref, prev_ref, xs_ref, tm)
        y = _conv_taps(xs_ref, cw_ref, tm)
        c = y * _sigmoid(y)
        for h in range(DN_HEADS):
            qs = slice(h * DN_HEAD_DIM, (h + 1) * DN_HEAD_DIM)
            ksl = slice(DN_W + h * DN_HEAD_DIM, DN_W + (h + 1) * DN_HEAD_DIM)
            qh, kh = c[:, qs], c[:, ksl]
            qn_ref[:, qs] = qh * lax.rsqrt(jnp.sum(qh * qh, axis=-1, keepdims=True) + EPS) * DN_SCALE
            kn_ref[:, qs] = kh * lax.rsqrt(jnp.sum(kh * kh, axis=-1, keepdims=True) + EPS)
        vc_ref[...] = c[:, 2 * DN_W:3 * DN_W]
        beta, g, _ = _gate_values(ba_ref[...], al_ref[...], dt_ref[...])
        lane = lax.broadcasted_iota(jnp.int32, beta.shape, 1)
        gb = jnp.where(lane < DN_HEADS, beta, jnp.where(lane < 2 * DN_HEADS, g, 0.0))
        gc_ref[...] = gb
        gr_ref[...] = gb.T[0:8, :]

    tok = lambda w: pl.BlockSpec((tm, w), lambda i: (i, 0))
    return pl.pallas_call(
        body, name="dn_prep", grid=(t // tm,),
        in_specs=_dn_prep_specs(tm, t),
        out_specs=[tok(DN_W), tok(DN_W), tok(DN_W), tok(128), pl.BlockSpec((8, tm), lambda i: (0, i))],
        out_shape=[jax.ShapeDtypeStruct((t, DN_W), F32)] * 3 + [jax.ShapeDtypeStruct((t, 128), F32),
                                                                 jax.ShapeDtypeStruct((8, t), F32)],
        scratch_shapes=[pltpu.VMEM((HALO + tm, CONV_CH), F32)],
        compiler_params=_params("parallel"),
    )(pd, pd, pd, conv_w, al_vec, dt_vec)


def _pair_masks():
    r = lax.broadcasted_iota(jnp.int32, (PAIR, PAIR), 0)
    c = lax.broadcasted_iota(jnp.int32, (PAIR, PAIR), 1)
    same = (r < DN_CHUNK) == (c < DN_CHUNK)
    return same & (r >= c), same & (r > c)


def _lane_col(mat, idx):
    lane = lax.broadcasted_iota(jnp.int32, mat.shape, 1)
    return jnp.sum(jnp.where(lane == idx, mat, 0.0), axis=-1, keepdims=True)


def _pair_cumsums(gc, gr, low):
    lowf = low.astype(F32)
    return _dot(lowf, gc, NN, HI), _dot(gr, lowf, NT, HI)


def _pair_gates(gc, cum_c, cum_r, low, h):
    beta = _lane_col(gc, h)
    gam = _lane_col(cum_c, DN_HEADS + h)
    gam_row = cum_r[DN_HEADS + h:DN_HEADS + h + 1, :]
    dm = jnp.where(low, jnp.exp(jnp.where(low, gam - gam_row, 0.0)), 0.0)
    row = lax.broadcasted_iota(jnp.int32, gam.shape, 0)
    gl = jnp.where(row < DN_CHUNK, gam[DN_CHUNK - 1:DN_CHUNK, :], gam[PAIR - 1:PAIR, :])
    return beta, gam, dm, gl


def _split(a):
    hi = a.astype(BF16)
    return hi, (a - hi.astype(F32)).astype(BF16)


def _dot_split(a, b, dims=NN):
    (ah, al), (bh, bl) = a, b
    la, lb = (1, 1) if dims == TN else ((0, 1) if dims == NN else (0, 0))
    r = _dot(jnp.concatenate([ah, al], axis=la), jnp.concatenate([bh, bl], axis=lb), dims)
    m, n = r.shape[0] // 2, r.shape[1] // 2
    return (r[m:, n:] + (r[:m, n:] + r[m:, :n])) + r[:m, :n]


def _unit_lower_inverses(lmats):
    n = lmats[0].shape[0]
    eye = (lax.broadcasted_iota(jnp.int32, (n, n), 0) == lax.broadcasted_iota(jnp.int32, (n, n), 1)).astype(F32)
    accs = [eye - l for l in lmats]
    splits = [_split(l) for l in lmats]
    step = 1
    while 2 * step < DN_CHUNK:
        splits = [_split(_dot_split(s, s)) for s in splits]
        accs = [acc + _dot_split(_split(acc), s) for acc, s in zip(accs, splits)]
        step *= 2
    return accs


def _dn_intra(qn, kn, vc, gc, gr):
    t = qn.shape[0]
    npair = t // PAIR

    def body(q_ref, k_ref, v_ref, gc_ref, gr_ref, u_ref, w_ref, qg_ref, kd_ref, a_ref, ti_ref, dl_ref):
        gc_v = gc_ref[...]
        low, strict = _pair_masks()
        cum_c, cum_r = _pair_cumsums(gc_v, gr_ref[...], low)
        heads = [slice(h * DN_HEAD_DIM, (h + 1) * DN_HEAD_DIM) for h in range(DN_HEADS)]
        gates = [_pair_gates(gc_v, cum_c, cum_r, low, h) for h in range(DN_HEADS)]
        lmats = []
        for hs, (beta, gam, dm, gl) in zip(heads, gates):
            k = k_ref[:, hs]
            lmats.append(jnp.where(strict, _dot(k * beta, k, NT) * dm, 0.0))
        tinvs = _unit_lower_inverses(lmats)
        for h, (hs, (beta, gam, dm, gl), tinv) in enumerate(zip(heads, gates, tinvs)):
            q, k, v = q_ref[:, hs], k_ref[:, hs], v_ref[:, hs]
            eg = jnp.exp(gam)
            u_ref[:, hs] = _dot(tinv, v * beta)
            w_ref[:, hs] = _dot(tinv, (k * beta) * eg)
            a_ref[h] = _dot(q, k, NT) * dm
            ti_ref[h] = tinv
            qg_ref[:, hs] = q * eg
            kd_ref[:, hs] = k * jnp.exp(gl - gam)
            for c in range(2):
                last = (c + 1) * DN_CHUNK - 1
                dl_ref[c, h] = jnp.broadcast_to(jnp.exp(gam[last:last + 1, :]), (8, 128))

    tok = lambda w: pl.BlockSpec((PAIR, w), lambda n: (n, 0))
    hm = pl.BlockSpec((DN_HEADS, PAIR, PAIR), lambda n: (0, n, 0))
    return pl.pallas_call(
        body, name="dn_intra", grid=(npair,),
        in_specs=[tok(DN_W), tok(DN_W), tok(DN_W), tok(128), pl.BlockSpec((8, PAIR), lambda n: (0, n))],
        out_specs=[tok(DN_W)] * 4 + [hm, hm, pl.BlockSpec((2, DN_HEADS, 8, 128), lambda n: (n, 0, 0, 0))],
        out_shape=[jax.ShapeDtypeStruct((t, DN_W), F32)] * 4 + [jax.ShapeDtypeStruct((DN_HEADS, t, PAIR), F32)] * 2
                  + [jax.ShapeDtypeStruct((2 * npair, DN_HEADS, 8, 128), F32)],
        compiler_params=_params("parallel"),
    )(qn, kn, vc, gc, gr)


def _dn_scan_fwd(u, w, qg, kd, a_qk, dlast, pd, dn_w):
    t = u.shape[0]
    npair = t // PAIR

    def body(u_ref, w_ref, qg_ref, kd_ref, a_ref, dl_ref, z_ref, nw_ref, out_ref, o_ref, vn_ref, sall_ref, s_ref):
        @pl.when(pl.program_id(0) == 0)
        def _():
            s_ref[...] = jnp.zeros_like(s_ref)

        nw = nw_ref[...]
        for c in range(2):
            rows = slice(c * DN_CHUNK, (c + 1) * DN_CHUNK)
            for h in range(DN_HEADS):
                hs = slice(h * DN_HEAD_DIM, (h + 1) * DN_HEAD_DIM)
                st = s_ref[h]
                sall_ref[c, h] = st
                vn_ref[rows, hs] = u_ref[rows, hs] - _dot(w_ref[rows, hs], st)
            for h in range(DN_HEADS):
                hs = slice(h * DN_HEAD_DIM, (h + 1) * DN_HEAD_DIM)
                st, vn = s_ref[h], vn_ref[rows, hs]
                o = _dot(qg_ref[rows, hs], st) + _dot(a_ref[h, rows, rows], vn)
                s_ref[h] = st * dl_ref[c, h][0:1, :] + _dot(kd_ref[rows, hs], vn, TN)
                o_ref[rows, hs] = o
                z = z_ref[rows, hs]
                on = o * lax.rsqrt(jnp.mean(o * o, axis=-1, keepdims=True) + EPS) * nw
                out_ref[rows, hs] = (on * (z * _sigmoid(z))).astype(out_ref.dtype)

    tok = pl.BlockSpec((PAIR, DN_W), lambda n: (n, 0))
    hm = pl.BlockSpec((DN_HEADS, PAIR, PAIR), lambda n: (0, n, 0))
    return pl.pallas_call(
        body, name="dn_scan_fwd", grid=(npair,),
        in_specs=[tok, tok, tok, tok, hm, pl.BlockSpec((2, DN_HEADS, 8, 128), lambda n: (n, 0, 0, 0)),
                  pl.BlockSpec((PAIR, DN_W), lambda n: (n, 3)), _full((1, 128))],
        out_specs=[tok, tok, tok, pl.BlockSpec((2, DN_HEADS, DN_HEAD_DIM, DN_HEAD_DIM), lambda n: (n, 0, 0, 0))],
        out_shape=[jax.ShapeDtypeStruct((t, DN_W), MXU_DTYPE)] + [jax.ShapeDtypeStruct((t, DN_W), F32)] * 2
                  + [jax.ShapeDtypeStruct((2 * npair, DN_HEADS, DN_HEAD_DIM, DN_HEAD_DIM), F32)],
        scratch_shapes=[pltpu.VMEM((DN_HEADS, DN_HEAD_DIM, DN_HEAD_DIM), F32)],
        compiler_params=_params("arbitrary"),
    )(u, w, qg, kd, a_qk, dlast, pd, dn_w)


def _dn_scan_bwd(dout, o, vnew, sall, w, qg, kd, a_qk, dlast, pd, dn_w):
    t = o.shape[0]
    npair = t // PAIR
    rev = lambda n: npair - 1 - n

    def body(do_ref, o_ref, vn_ref, sall_ref, w_ref, qg_ref, kd_ref, a_ref, dl_ref, z_ref, nw_ref,
             dz_ref, du_ref, dw_ref, dqg_ref, dkd_ref, da_ref, ddl_ref, acc_ref, ds_ref, dos_ref):
        @pl.when(pl.program_id(0) == 0)
        def _():
            ds_ref[...] = jnp.zeros_like(ds_ref)
            acc_ref[...] = jnp.zeros_like(acc_ref)

        nw = nw_ref[...]
        dnw = jnp.zeros((1, 128), F32)
        for h in range(DN_HEADS):
            hs = slice(h * DN_HEAD_DIM, (h + 1) * DN_HEAD_DIM)
            o, z, dout = o_ref[:, hs], z_ref[:, hs], do_ref[:, hs]
            r = lax.rsqrt(jnp.mean(o * o, axis=-1, keepdims=True) + EPS)
            oh = o * r
            sz = _sigmoid(z)
            dz_ref[:, hs] = dout * (oh * nw) * (sz + z * sz * (1.0 - sz))
            don = dout * (z * sz)
            dnw += jnp.sum(don * oh, axis=0, keepdims=True)
            doh = don * nw
            dos_ref[:, hs] = r * (doh - oh * jnp.mean(doh * oh, axis=-1, keepdims=True))
        acc_ref[0:1, :] += dnw
        for c in (1, 0):
            rows = slice(c * DN_CHUNK, (c + 1) * DN_CHUNK)
            other = slice((1 - c) * DN_CHUNK, (2 - c) * DN_CHUNK)
            for h in range(DN_HEADS):
                hs = slice(h * DN_HEAD_DIM, (h + 1) * DN_HEAD_DIM)
                do, st, dsp, vn = dos_ref[rows, hs], sall_ref[c, h], ds_ref[h], vn_ref[rows, hs]
                da_ref[h, rows, rows] = _dot(do, vn, NT)
                da_ref[h, rows, other] = jnp.zeros((DN_CHUNK, DN_CHUNK), F32)
                du_ref[rows, hs] = _dot(a_ref[h, rows, rows], do, TN) + _dot(kd_ref[rows, hs], dsp)
                dqg_ref[rows, hs] = _dot(do, st, NT)
                dkd_ref[rows, hs] = _dot(vn, dsp, NT)
                ddl = jnp.sum(jnp.sum(dsp * st, axis=1, keepdims=True), axis=0, keepdims=True)
                ddl_ref[c, h] = jnp.broadcast_to(ddl, (8, 128))
            for h in range(DN_HEADS):
                hs = slice(h * DN_HEAD_DIM, (h + 1) * DN_HEAD_DIM)
                do, st, dvn = dos_ref[rows, hs], sall_ref[c, h], du_ref[rows, hs]
                dw_ref[rows, hs] = -_dot(dvn, st, NT)
                ds_ref[h] = (ds_ref[h] * dl_ref[c, h][0:1, :] + _dot(qg_ref[rows, hs], do, TN)
                             - _dot(w_ref[rows, hs], dvn, TN))

    tok = pl.BlockSpec((PAIR, DN_W), lambda n: (rev(n), 0))
    hm = pl.BlockSpec((DN_HEADS, PAIR, PAIR), lambda n: (0, rev(n), 0))
    sc = pl.BlockSpec((2, DN_HEADS, 8, 128), lambda n: (rev(n), 0, 0, 0))
    return pl.pallas_call(
        body, name="dn_scan_bwd", grid=(npair,),
        in_specs=[tok, tok, tok, pl.BlockSpec((2, DN_HEADS, DN_HEAD_DIM, DN_HEAD_DIM), lambda n: (rev(n), 0, 0, 0)),
                  tok, tok, tok, hm, sc, pl.BlockSpec((PAIR, DN_W), lambda n: (rev(n), 3)), _full((1, 128))],
        out_specs=[tok] * 5 + [hm, sc, _full((8, 128))],
        out_shape=[jax.ShapeDtypeStruct((t, DN_W), F32)] * 5 + [jax.ShapeDtypeStruct((DN_HEADS, t, PAIR), F32),
                   jax.ShapeDtypeStruct((2 * npair, DN_HEADS, 8, 128), F32), jax.ShapeDtypeStruct((8, 128), F32)],
        scratch_shapes=[pltpu.VMEM((DN_HEADS, DN_HEAD_DIM, DN_HEAD_DIM), F32), pltpu.VMEM((PAIR, DN_W), F32)],
        compiler_params=_params("arbitrary"),
    )(dout, o, vnew, sall, w, qg, kd, a_qk, dlast, pd, dn_w)


def _dn_intra_bwd(qn, kn, vc, gc, gr, tinv, a_qk, du, dw, dqg, dkd, da_qk, ddlast, dlast, dep):
    t = qn.shape[0]
    npair = t // PAIR

    def body(q_ref, k_ref, v_ref, gc_ref, gr_ref, ti_ref, a_ref, du_ref, dw_ref, dqg_ref, dkd_ref, da_ref, ddl_ref, dl_ref,
             dep_ref, dq_ref, dk_ref, dv_ref, dg_ref):
        gc_v = gc_ref[...]
        low, strict = _pair_masks()
        cum_c, cum_r = _pair_cumsums(gc_v, gr_ref[...], low)
        lane = lax.broadcasted_iota(jnp.int32, (PAIR, 128), 1)
        rowi = lax.broadcasted_iota(jnp.int32, (PAIR, 1), 0)
        rsum = lambda v: jnp.sum(v, axis=-1, keepdims=True)
        dgam_all = jnp.zeros((PAIR, 128), F32)
        dbeta_all = jnp.zeros((PAIR, 128), F32)
        heads = [slice(h * DN_HEAD_DIM, (h + 1) * DN_HEAD_DIM) for h in range(DN_HEADS)]
        gates = [_pair_gates(gc_v, cum_c, cum_r, low, h) for h in range(DN_HEADS)]
        tsplits, dtis = [], []
        for h, (hs, (beta, gam, dm, gl)) in enumerate(zip(heads, gates)):
            k = k_ref[:, hs]
            tsplits.append(_split(ti_ref[h]))
            dtis.append(_dot(du_ref[:, hs], v_ref[:, hs] * beta, NT)
                        + _dot(dw_ref[:, hs], (k * beta) * jnp.exp(gam), NT))
        xs = [_dot_split(ts, _split(dti), TN) for ts, dti in zip(tsplits, dtis)]
        dls = [jnp.where(strict, -_dot_split(_split(x), ts, NT), 0.0) for x, ts in zip(xs, tsplits)]
        for h, (hs, (beta, gam, dm, gl), dl) in enumerate(zip(heads, gates, dls)):
            q, k, v = q_ref[:, hs], k_ref[:, hs], v_ref[:, hs]
            tinv, a = ti_ref[h], a_ref[h]
            du, dw, dqg, dkd = du_ref[:, hs], dw_ref[:, hs], dqg_ref[:, hs], dkd_ref[:, hs]
            kb = k * beta
            eg = jnp.exp(gam)
            ekd = jnp.exp(gl - gam)
            kbg = kb * eg
            lmat = jnp.where(strict, _dot(kb, k, NT) * dm, 0.0)
            dvb = _dot(tinv, du, TN)
            dkbg = _dot(tinv, dw, TN)
            dmm = dl * dm
            dam = jnp.where(low, da_ref[h], 0.0)
            dn = dam * dm
            e = dl * lmat + dam * a
            dkb = _dot(dmm, k) + dkbg * eg
            dk_ref[:, hs] = _dot(dmm, kb, TN) + _dot(dn, q, TN) + dkd * ekd + dkb * beta
            dq_ref[:, hs] = _dot(dn, k) + dqg * eg
            dv_ref[:, hs] = dvb * beta
            t_kd = rsum(dkd * (k * ekd))
            dgam = rsum(e) - rsum(e.T) + rsum(dqg * (q * eg)) + rsum(dkbg * kbg) - t_kd
            for c in range(2):
                rows = slice(c * DN_CHUNK, (c + 1) * DN_CHUNK)
                dgl = (jnp.sum(t_kd[rows, :], axis=0, keepdims=True)
                       + ddl_ref[c, h][0:1, 0:1] * dl_ref[c, h][0:1, 0:1])
                dgam = dgam + jnp.where(rowi == (c + 1) * DN_CHUNK - 1, dgl, 0.0)
            dgam_all += jnp.where(lane == DN_HEADS + h, dgam, 0.0)
            dbeta_all += jnp.where(lane == h, rsum(dkb * k) + rsum(dvb * v), 0.0)
        dg_ref[...] = dbeta_all + _dot(low.astype(F32), dgam_all, TN, HI)

    tok = lambda w: pl.BlockSpec((PAIR, w), lambda n: (n, 0))
    hm = pl.BlockSpec((DN_HEADS, PAIR, PAIR), lambda n: (0, n, 0))
    sc = pl.BlockSpec((2, DN_HEADS, 8, 128), lambda n: (n, 0, 0, 0))
    return pl.pallas_call(
        body, name="dn_intra_bwd", grid=(npair,),
        in_specs=[tok(DN_W), tok(DN_W), tok(DN_W), tok(128), pl.BlockSpec((8, PAIR), lambda n: (0, n)), hm, hm,
                  tok(DN_W), tok(DN_W), tok(DN_W), tok(DN_W), hm, sc, sc, pl.BlockSpec(memory_space=pl.ANY)],
        out_specs=[tok(DN_W), tok(DN_W), tok(DN_W), tok(128)],
        out_shape=[jax.ShapeDtypeStruct((t, DN_W), F32)] * 3 + [jax.ShapeDtypeStruct((t, 128), F32)],
        compiler_params=_params("parallel"),
    )(qn, kn, vc, gc, gr, tinv, a_qk, du, dw, dqg, dkd, da_qk, ddlast, dlast, dep)


def _dn_prep_bwd(pd, conv_w, al_vec, dt_vec, dqn, dkn, dvc, dgc, tm):
    t = pd.shape[0]

    def body(cur_ref, prev_ref, ba_ref, cw_ref, al_ref, dt_ref, dq_ref, dk_ref, dv_ref, dg_ref,
             dy_ref, dba_ref, accw_ref, accg_ref, xs_ref, dc_ref):
        @pl.when(pl.program_id(0) == 0)
        def _():
            accw_ref[...] = jnp.zeros_like(accw_ref)
            accg_ref[...] = jnp.zeros_like(accg_ref)

        _conv_window(cur_ref, prev_ref, xs_ref, tm)
        y = _conv_taps(xs_ref, cw_ref, tm)
        sg = _sigmoid(y)
        c = y * sg
        for h in range(DN_HEADS):
            qs = slice(h * DN_HEAD_DIM, (h + 1) * DN_HEAD_DIM)
            ksl = slice(DN_W + h * DN_HEAD_DIM, DN_W + (h + 1) * DN_HEAD_DIM)
            for src, sl, scale in ((dq_ref, qs, DN_SCALE), (dk_ref, ksl, 1.0)):
                xh = c[:, sl]
                r = lax.rsqrt(jnp.sum(xh * xh, axis=-1, keepdims=True) + EPS)
                unit = xh * r
                dn = src[:, qs] * scale
                dc_ref[:, sl] = r * (dn - unit * jnp.sum(dn * unit, axis=-1, keepdims=True))
        dc_ref[:, 2 * DN_W:3 * DN_W] = dv_ref[...]
        dy = dc_ref[...] * (sg + y * sg * (1.0 - sg))
        dy_ref[...] = dy
        for j in range(DN_CONV):
            accw_ref[j:j + 1, :] += jnp.sum(dy * xs_ref[pl.ds(HALO - DN_CONV + 1 + j, tm), :], axis=0, keepdims=True)

        beta, g, pre = _gate_values(ba_ref[...], al_ref[...], dt_ref[...])
        dgb = dg_ref[...]
        lane = lax.broadcasted_iota(jnp.int32, dgb.shape, 1)
        is_b, is_a = lane < DN_HEADS, (lane >= DN_HEADS) & (lane < 2 * DN_HEADS)
        dpre = dgb * (-jnp.exp(al_ref[...])) * _sigmoid(pre)
        dba_ref[...] = jnp.where(is_b, dgb * beta * (1.0 - beta), jnp.where(is_a, dpre, 0.0))
        accg_ref[0:1, :] += jnp.sum(jnp.where(is_a, dgb * g, 0.0), axis=0, keepdims=True)
        accg_ref[1:2, :] += jnp.sum(jnp.where(is_a, dpre, 0.0), axis=0, keepdims=True)

    tok = lambda w: pl.BlockSpec((tm, w), lambda i: (i, 0))
    return pl.pallas_call(
        body, name="dn_prep_bwd", grid=(t // tm,),
        in_specs=_dn_prep_specs(tm, t) + [tok(DN_W), tok(DN_W), tok(DN_W), tok(128)],
        out_specs=[tok(CONV_CH), tok(128), _full((8, CONV_CH)), _full((8, 128))],
        out_shape=[jax.ShapeDtypeStruct((t, CONV_CH), F32), jax.ShapeDtypeStruct((t, 128), F32),
                   jax.ShapeDtypeStruct((8, CONV_CH), F32), jax.ShapeDtypeStruct((8, 128), F32)],
        scratch_shapes=[pltpu.VMEM((HALO + tm, CONV_CH), F32), pltpu.VMEM((tm, CONV_CH), F32)],
        compiler_params=_params("arbitrary"),
    )(pd, pd, pd, conv_w, al_vec, dt_vec, dqn, dkn, dvc, dgc)


def _dn_conv_bwd(dy, dz, dba, conv_w, tm):
    t = dy.shape[0]
    nt = t // tm

    def body(cur_ref, nxt_ref, dz_ref, dba_ref, cw_ref, o_ref, ds_ref):
        nxt = jnp.where(pl.program_id(0) < nt - 1, nxt_ref[...], 0.0)
        ds_ref[0:tm, :] = cur_ref[...]
        ds_ref[tm:tm + HALO, :] = nxt
        dx = cw_ref[0:1, :] * ds_ref[pl.ds(DN_CONV - 1, tm), :]
        for j in range(1, DN_CONV):
            dx += cw_ref[j:j + 1, :] * ds_ref[pl.ds(DN_CONV - 1 - j, tm), :]
        o_ref[:, 0:CONV_CH] = dx.astype(o_ref.dtype)
        o_ref[:, CONV_CH:CONV_CH + DN_W] = dz_ref[...].astype(o_ref.dtype)
        o_ref[:, CONV_CH + DN_W:DN_COLS] = dba_ref[...].astype(o_ref.dtype)

    tok = lambda w: pl.BlockSpec((tm, w), lambda i: (i, 0))
    return pl.pallas_call(
        body, name="dn_conv_bwd", grid=(nt,),
        in_specs=[tok(CONV_CH),
                  pl.BlockSpec((HALO, CONV_CH), lambda i: (jnp.minimum((i + 1) * (tm // HALO), t // HALO - 1), 0)),
                  tok(DN_W), tok(128), _full((DN_CONV, CONV_CH))],
        out_specs=tok(DN_COLS),
        out_shape=jax.ShapeDtypeStruct((t, DN_COLS), MXU_DTYPE),
        scratch_shapes=[pltpu.VMEM((tm + HALO, CONV_CH), F32)],
        compiler_params=_params("parallel"),
    )(dy, dy, dz, dba, conv_w)


def _pad_lanes(v, offset=0):
    return jnp.zeros((1, 128), F32).at[0, offset:offset + v.shape[0]].set(v.astype(F32))


class _LocalReducer:
    def start(self, grads):
        return jnp.zeros((8, 128), F32)

    def middle(self, after):
        return jnp.zeros((8, 128), F32)

    def finish(self, after):
        return None


def _local_step(x, p, tgt, sm, w, late, reducer):
    t = x.shape[0]
    tm = min(512, t // 2)
    tm_s = min(256, t // 2)
    tw = min(1024, t // 2)

    w_in = w["w_in"]
    wa = w_in[:, :ATTN_Q + 2 * ATTN_KV]
    wd = jnp.pad(w_in[:, ATTN_Q + 2 * ATTN_KV:], ((0, 0), (0, DN_COLS - (D_IN - ATTN_Q - 2 * ATTN_KV))))
    conv_w = w["conv_w"]
    al_vec, dt_vec = _pad_lanes(sm["a_log"], DN_HEADS), _pad_lanes(sm["dt_bias"], DN_HEADS)
    sinks_vec = _pad_lanes(sm["sinks"])
    dn_w = sm["dn_norm"].reshape(1, 128)
    row = lambda v: v.reshape(1, D_MODEL)
    cos, sin = _rope_tables(t)

    u, pa, pd = _inproj(x, row(sm["norm_mix"]), wa, wd, tm_s)
    ao = _attn_fwd(pa, cos, sin, sinks_vec)
    qn, kn, vc, gc, gr = _dn_prep(pd, conv_w, al_vec, dt_vec, tm_s)
    uu, ww, qg, kd, a_qk, tinv, dlast = _dn_intra(qn, kn, vc, gc, gr)
    dn_out, o, vnew, sall = _dn_scan_fwd(uu, ww, qg, kd, a_qk, dlast, pd, dn_w)
    w = dict(w, **late(dn_out))
    wo_a, wo_d = w["w_o"][:ATTN_Q], w["w_o"][ATTN_Q:]
    w_proj = jnp.transpose(w["w_proj4"], (1, 0, 2)).reshape(PLE_DIM, D_MODEL)
    h1 = _oproj(x, ao, dn_out, wo_a, wo_d, tm)
    m, r, h2 = _mlp_fwd(h1, row(sm["norm_mlp"]), w["w_up4"], w["w_down"], tw)
    dh2, dh2b, dgp, dpp, n3, pb, acc_ple = _ple_loss(h2, p, tgt, row(sm["norm_ple"]), row(sm["norm_final"]),
                                                     w["w_gate"], w_proj, tm_s)
    g_w_gate = _wgrad(n3, dgp, "wgrad_gate", D_MODEL, D_MODEL, tw)
    g_w_proj = _wgrad(pb, dpp, "wgrad_proj", PLE_DIM, D_MODEL, tw)
    da, dh1, dh1b, acc_mlp = _mlp_bwd(dh2, dh2b, r, h1, row(sm["norm_mlp"]), w["w_up4"], w["w_down"], tm)
    g_w_up4 = _wgrad(m, da, "wgrad_up", D_MODEL, FF_BLOCK, tw, stacked=True)
    g_w_down = _wgrad(r, dh2b, "wgrad_down", FF_BLOCK, D_MODEL, tw,
                      prep=lambda rv: jnp.square(rv.astype(F32)).astype(MXU_DTYPE))
    g_w_o = jnp.concatenate([_wgrad(ao, dh1b, "wgrad_oa", ATTN_Q, D_MODEL, tw),
                             _wgrad(dn_out, dh1b, "wgrad_od", DN_W, D_MODEL, tw)], axis=0)
    early = dict(w_up4=g_w_up4, w_down=g_w_down, w_gate=g_w_gate, w_proj=g_w_proj, w_o=g_w_o)
    dep = reducer.start(early)
    dao, ddn = _oproj_bwd(dh1b, wo_a, wo_d, tm, dep)
    dz, du, dw, dqg, dkd, da_qk, ddlast, acc_dn = _dn_scan_bwd(ddn, o, vnew, sall, ww, qg, kd, a_qk, dlast, pd, dn_w)
    dep = reducer.middle(du)
    dqn, dkn, dvc, dgc = _dn_intra_bwd(qn, kn, vc, gc, gr, tinv, a_qk, du, dw, dqg, dkd, da_qk, ddlast, dlast, dep)
    dy, dba, acc_conv, acc_gate = _dn_prep_bwd(pd, conv_w, al_vec, dt_vec, dqn, dkn, dvc, dgc, tm_s)
    d_dn = _dn_conv_bwd(dy, dz, dba, conv_w, tm_s)
    dq, dk, dv, acc_attn = _attn_bwd(pa, cos, sin, sinks_vec, dao)
    reducer.finish(dq)
    wq, wk, wv = wa[:, :ATTN_Q], wa[:, ATTN_Q:ATTN_Q + ATTN_KV], wa[:, ATTN_Q + ATTN_KV:]
    dx, acc_mix = _inproj_bwd(x, dh1, row(sm["norm_mix"]), [dq, dk, dv, d_dn], [wq, wk, wv, wd], tm_s)

    g_w_in = jnp.concatenate([
        _wgrad(u, dq, "wgrad_q", D_MODEL, ATTN_Q, tw), _wgrad(u, dk, "wgrad_k", D_MODEL, ATTN_KV, tw),
        _wgrad(u, dv, "wgrad_v", D_MODEL, ATTN_KV, tw),
        _wgrad(u, d_dn, "wgrad_dn", D_MODEL, DN_COLS, tw)[:, :D_IN - ATTN_Q - 2 * ATTN_KV]], axis=1)
    grads = dict(early, w_in=g_w_in)
    sums = dict(loss=acc_ple[2, 0], norm_final=acc_ple[0], norm_ple=acc_ple[1], norm_mlp=acc_mlp[0], norm_mix=acc_mix[0],
                dn_norm=acc_dn[0], sinks=acc_attn[0, :ATTN_HEADS], a_log=acc_gate[0, DN_HEADS:2 * DN_HEADS],
                dt_bias=acc_gate[1, DN_HEADS:2 * DN_HEADS], conv_w=acc_conv[:DN_CONV])
    return sums, dx, grads


MESH = pl.DeviceIdType.MESH
ANY = pl.BlockSpec(memory_space=pl.ANY)
N_CHIPS = 4
N_DEV = 8


def _place():
    x, y, c = lax.axis_index("x"), lax.axis_index("y"), lax.axis_index("c")
    chips = [(1 - x, y), (x, 1 - y), (1 - x, 1 - y)]
    return x, y, c, chips


def _gather_weights(shards, conv_s):
    n = len(shards)
    per = 7

    def body(*refs):
        in_refs, conv_ref = refs[:n], refs[n]
        out_refs, conv_out = refs[n + 1:2 * n + 1], refs[2 * n + 1]
        send_sems, recv_sems = refs[2 * n + 2:]
        x, y, c, chips = _place()
        sibling = (x, y, 1 - c)

        def blk(a, px, py, pc):
            hr = in_refs[a].shape[0] // 2
            return out_refs[a].at[2 * px + py, pl.ds(pc * hr, hr), :]

        def mine(a):
            hr = in_refs[a].shape[0] // 2
            return in_refs[a].at[pl.ds(c * hr, hr), :]

        def rcopy(a, k, block, to, src=None):
            return pltpu.make_async_remote_copy(
                src_ref=blk(a, *block) if src is None else src, dst_ref=blk(a, *block),
                send_sem=send_sems.at[per * a + k], recv_sem=recv_sems.at[per * a + k],
                device_id=to, device_id_type=MESH)

        def whole(a, to):
            return pltpu.make_async_remote_copy(
                src_ref=in_refs[a], dst_ref=out_refs[a].at[2 * x + y],
                send_sem=send_sems.at[per * a], recv_sem=recv_sems.at[per * a], device_id=to, device_id_type=MESH)

        def ccopy(j, to):
            return pltpu.make_async_remote_copy(
                src_ref=conv_ref, dst_ref=conv_out.at[2 * x + y],
                send_sem=send_sems.at[per * n + j], recv_sem=recv_sems.at[per * n + j],
                device_id=to, device_id_type=MESH)

        started = []
        for a in range(n):
            first = [whole(a, sibling)]
            first += [rcopy(a, 1 + j, (x, y, c), (*chip, c), src=mine(a)) for j, chip in enumerate(chips)]
            for cp in first:
                cp.start()
            started += first
        conv_sends = [ccopy(j, (*chip, c)) for j, chip in enumerate(chips)] + [ccopy(3, sibling)]
        for cp in conv_sends:
            cp.start()
        started += conv_sends
        for a in range(n):
            for j, chip in enumerate(chips):
                rcopy(a, 1 + j, (*chip, c), (x, y, c)).wait_recv()
                fwd = rcopy(a, 4 + j, (*chip, c), sibling)
                fwd.start()
                started.append(fwd)
        for a in range(n):
            whole(a, sibling).wait_recv()
            for j, chip in enumerate(chips):
                rcopy(a, 4 + j, (*chip, 1 - c), (x, y, c)).wait_recv()
        for j, chip in enumerate(chips + [(x, y)]):
            pltpu.make_async_remote_copy(
                src_ref=conv_ref, dst_ref=conv_out.at[2 * chip[0] + chip[1]],
                send_sem=send_sems.at[per * n + j], recv_sem=recv_sems.at[per * n + j],
                device_id=sibling, device_id_type=MESH).wait_recv()
        for cp in started:
            cp.wait_send()

    nsem = per * n + 4
    out_shape = [jax.ShapeDtypeStruct((N_CHIPS,) + s.shape, s.dtype) for s in shards]
    out_shape.append(jax.ShapeDtypeStruct((N_CHIPS,) + conv_s.shape, conv_s.dtype))
    return pl.pallas_call(
        body, name="gather_weights", in_specs=[ANY] * (n + 1), out_specs=[ANY] * (n + 1), out_shape=out_shape,
        scratch_shapes=[pltpu.SemaphoreType.DMA((nsem,)), pltpu.SemaphoreType.DMA((nsem,))],
    )(*shards, conv_s)


HBM = pl.BlockSpec(memory_space=pltpu.HBM)
SEM = pl.BlockSpec(memory_space=pltpu.SEMAPHORE)
EFFECT = pltpu.SideEffectType.DATAFLOW_SIDE_EFFECTING
LATE_COPIES = 7


def _late_copies(in_refs, land_refs, send_sems, recv_sems):
    x, y, c, chips = _place()
    sends, arrivals = [], []
    for a, (src, land) in enumerate(zip(in_refs, land_refs)):
        hr = src.shape[0] // 2
        base = LATE_COPIES * a

        def cp(src_ref, dst_ref, s_idx, r_idx, to):
            return pltpu.make_async_remote_copy(src_ref=src_ref, dst_ref=dst_ref, send_sem=send_sems.at[base + s_idx],
                                                recv_sem=recv_sems.at[base + r_idx], device_id=to, device_id_type=MESH)

        sends.append(cp(src, land.at[2 * x + y], 0, 0, (x, y, 1 - c)))
        arrivals.append(cp(src, land.at[2 * x + y], 0, 0, (x, y, 1 - c)))
        for j, chip in enumerate(chips):
            for pc in range(2):
                half = src.at[pl.ds(c * hr, hr), :]
                sends.append(cp(half, land.at[2 * x + y, pl.ds(c * hr, hr), :], 1 + 2 * j + pc, 1 + 2 * j + c, (*chip, pc)))
                arrivals.append(cp(half, land.at[2 * chip[0] + chip[1], pl.ds(pc * hr, hr), :], 1 + 2 * j + pc,
                                   1 + 2 * j + pc, (*chip, pc)))
    return sends, arrivals


def _copies_start(name, build, nsem, srcs, land_shapes, after):
    n = len(srcs)

    def body(*refs):
        sends, _ = build(refs[:n], refs[n:2 * n], refs[2 * n + 1], refs[2 * n + 2])
        for cp in sends:
            cp.start()
        refs[-1][...] = jnp.zeros_like(refs[-1])

    lands = [pltpu.with_memory_space_constraint(lax.empty(s.shape, s.dtype), pltpu.HBM) for s in land_shapes]
    ins = [pltpu.with_memory_space_constraint(s, pltpu.HBM) for s in srcs]
    out = pl.pallas_call(
        body, name=name,
        out_shape=(pltpu.SemaphoreType.DMA((nsem,)), pltpu.SemaphoreType.DMA((nsem,)),
                   *[pltpu.HBM(s.shape, s.dtype) for s in srcs], *[pltpu.HBM(s.shape, s.dtype) for s in land_shapes],
                   jax.ShapeDtypeStruct((8, 128), F32)),
        in_specs=[HBM] * (2 * n) + [ANY],
        out_specs=(SEM, SEM, *[HBM] * (2 * n), pl.BlockSpec(memory_space=pltpu.VMEM)),
        input_output_aliases={i: 2 + i for i in range(2 * n)},
        compiler_params=pltpu.CompilerParams(has_side_effects=EFFECT),
    )(*ins, *lands, after)
    return out[0], out[1], out[2:2 + n], out[2 + n:2 + 2 * n], out[-1]


def _copies_wait(name, build, started, after):
    send_sems, recv_sems, srcs, lands, _ = started
    n = len(srcs)

    def body(*refs):
        sends, arrivals = build(refs[:n], refs[n:2 * n], refs[2 * n], refs[2 * n + 1])
        for cp in sends:
            cp.wait_send()
        for cp in arrivals:
            cp.wait_recv()

    out = pl.pallas_call(
        body, name=name,
        out_shape=(*[pltpu.HBM(s.shape, s.dtype) for s in srcs], *[pltpu.HBM(l.shape, l.dtype) for l in lands]),
        in_specs=[HBM] * (2 * n) + [SEM, SEM, ANY],
        out_specs=tuple([HBM] * (2 * n)),
        input_output_aliases={i: i for i in range(2 * n)},
        compiler_params=pltpu.CompilerParams(has_side_effects=EFFECT),
    )(*srcs, *lands, send_sems, recv_sems, after)
    return out[:n], out[n:]


def _exchange_copies(g_refs, got_refs, send_sems, recv_sems):
    x, y, c, _ = _place()
    sends, arrivals = [], []
    for a, (g, got) in enumerate(zip(g_refs, got_refs)):
        hr = g.shape[1] // 2
        cp = pltpu.make_async_remote_copy(
            src_ref=g.at[:, pl.ds((1 - c) * hr, hr), :], dst_ref=got, send_sem=send_sems.at[a],
            recv_sem=recv_sems.at[a], device_id=(x, y, 1 - c), device_id_type=MESH)
        sends.append(cp)
        arrivals.append(cp)
    return sends, arrivals


def _scatter_copies(s_refs, got_refs, send_sems, recv_sems):
    x, y, c, chips = _place()
    sends, arrivals = [], []
    for a, (s16, got) in enumerate(zip(s_refs, got_refs)):
        for j, chip in enumerate(chips):
            cp = pltpu.make_async_remote_copy(
                src_ref=s16.at[2 * chip[0] + chip[1]], dst_ref=got.at[j], send_sem=send_sems.at[3 * a + j],
                recv_sem=recv_sems.at[3 * a + j], device_id=(*chip, c), device_id_type=MESH)
            sends.append(cp)
            arrivals.append(cp)
    return sends, arrivals


def _exchange_halves(grads):
    n = len(grads)

    def body(*refs):
        g_refs, got_refs = refs[:n], refs[n:2 * n]
        send_sems, recv_sems = refs[2 * n:]
        x, y, c, _ = _place()
        remote = []
        for a in range(n):
            hr = g_refs[a].shape[1] // 2
            remote.append(pltpu.make_async_remote_copy(
                src_ref=g_refs[a].at[:, pl.ds((1 - c) * hr, hr), :], dst_ref=got_refs[a],
                send_sem=send_sems.at[a], recv_sem=recv_sems.at[a], device_id=(x, y, 1 - c), device_id_type=MESH))
        for cp in remote:
            cp.start()
        for cp in remote:
            cp.wait_recv()
        for cp in remote:
            cp.wait_send()

    half = [jax.ShapeDtypeStruct((g.shape[0], g.shape[1] // 2, g.shape[2]), g.dtype) for g in grads]
    return pl.pallas_call(
        body, name="exchange_halves", in_specs=[ANY] * n, out_specs=[ANY] * n, out_shape=half,
        scratch_shapes=[pltpu.SemaphoreType.DMA((n,)), pltpu.SemaphoreType.DMA((n,))],
    )(*grads)


def _scatter_to_chips(sums16):
    n = len(sums16)

    def body(*refs):
        s16, got_refs = refs[:n], refs[n:2 * n]
        send_sems, recv_sems = refs[2 * n:]
        x, y, c, chips = _place()
        remote = []
        for a in range(n):
            for j, chip in enumerate(chips):
                remote.append(pltpu.make_async_remote_copy(
                    src_ref=s16[a].at[2 * chip[0] + chip[1]], dst_ref=got_refs[a].at[j],
                    send_sem=send_sems.at[3 * a + j], recv_sem=recv_sems.at[3 * a + j],
                    device_id=(*chip, c), device_id_type=MESH))
        for cp in remote:
            cp.start()
        for cp in remote:
            cp.wait_recv()
        for cp in remote:
            cp.wait_send()

    got = [jax.ShapeDtypeStruct((3,) + s.shape[1:], BF16) for s in sums16]
    return pl.pallas_call(
        body, name="scatter_to_chips", in_specs=[ANY] * n, out_specs=[ANY] * n, out_shape=got,
        scratch_shapes=[pltpu.SemaphoreType.DMA((3 * n,)), pltpu.SemaphoreType.DMA((3 * n,))],
    )(*sums16)


def _share_halves(bufs):
    n = len(bufs)

    def body(*refs):
        out_refs = refs[n:2 * n]
        send_sems, recv_sems = refs[2 * n:]
        x, y, c, _ = _place()
        remote = [pltpu.make_async_remote_copy(
            src_ref=out_refs[a].at[c], dst_ref=out_refs[a].at[c], send_sem=send_sems.at[a], recv_sem=recv_sems.at[a],
            device_id=(x, y, 1 - c), device_id_type=MESH) for a in range(n)]
        for cp in remote:
            cp.start()
        for a in range(n):
            pltpu.make_async_remote_copy(
                src_ref=out_refs[a].at[c], dst_ref=out_refs[a].at[1 - c], send_sem=send_sems.at[a],
                recv_sem=recv_sems.at[a], device_id=(x, y, 1 - c), device_id_type=MESH).wait_recv()
        for cp in remote:
            cp.wait_send()

    return pl.pallas_call(
        body, name="share_halves", in_specs=[ANY] * n, out_specs=[ANY] * n,
        out_shape=[jax.ShapeDtypeStruct(b.shape, b.dtype) for b in bufs],
        input_output_aliases={a: a for a in range(n)},
        scratch_shapes=[pltpu.SemaphoreType.DMA((n,)), pltpu.SemaphoreType.DMA((n,))],
    )(*bufs)


SMALL_ROWS, SMALL_COLS = 16, CONV_CH


def _allreduce_small(block):
    m_per, ncol = block.shape

    def body(x_ref, sum_ref, all_ref, send_sems, recv_sems, local_sem):
        x, y, c, chips = _place()
        me, sibling = (x, y, c), (x, y, 1 - c)

        def rows(px, py, pc):
            return all_ref.at[pl.ds((4 * px + 2 * py + pc) * m_per, m_per), :]

        def copy(k, block_of, to, src=None):
            return pltpu.make_async_remote_copy(
                src_ref=rows(*block_of) if src is None else src, dst_ref=rows(*block_of),
                send_sem=send_sems.at[k], recv_sem=recv_sems.at[k], device_id=to, device_id_type=MESH)

        mine = pltpu.make_async_copy(x_ref, rows(*me), local_sem)
        mine.start()
        first = [copy(0, me, sibling, src=x_ref)]
        first += [copy(1 + j, me, (*chip, c), src=x_ref) for j, chip in enumerate(chips)]
        for cp in first:
            cp.start()
        passed = [copy(4 + j, (*chip, c), sibling) for j, chip in enumerate(chips)]
        for j, chip in enumerate(chips):
            copy(1 + j, (*chip, c), me).wait_recv()
            passed[j].start()
        copy(0, sibling, me).wait_recv()
        for j, chip in enumerate(chips):
            copy(4 + j, (*chip, 1 - c), me).wait_recv()
        for cp in first + passed:
            cp.wait_send()
        mine.wait()
        total = all_ref[0:m_per, :]
        for d in range(1, N_DEV):
            total = total + all_ref[d * m_per:(d + 1) * m_per, :]
        sum_ref[...] = total

    vm = pl.BlockSpec(memory_space=pltpu.VMEM)
    return pl.pallas_call(
        body, name="allreduce_small", in_specs=[vm], out_specs=vm,
        out_shape=jax.ShapeDtypeStruct((m_per, ncol), F32),
        scratch_shapes=[pltpu.VMEM((N_DEV * m_per, ncol), F32), pltpu.SemaphoreType.DMA((7,)),
                        pltpu.SemaphoreType.DMA((7,)), pltpu.SemaphoreType.DMA],
    )(block)


def _row_tile(rows, cols):
    tile = rows
    while tile * cols * 4 > (1 << 20) and tile % 16 == 0:
        tile //= 2
    return tile


def _elementwise(fn, name, ins, out_dtypes):
    rows, cols = ins[0].shape
    tile = _row_tile(rows, cols)

    def body(*refs):
        outs = fn(*[r[...] for r in refs[:len(ins)]])
        for o_ref, o in zip(refs[len(ins):], outs):
            o_ref[...] = o.astype(o_ref.dtype)

    spec = pl.BlockSpec((tile, cols), lambda i: (i, 0))
    return pl.pallas_call(
        body, name=name, grid=(rows // tile,), in_specs=[spec] * len(ins), out_specs=[spec] * len(out_dtypes),
        out_shape=[jax.ShapeDtypeStruct((rows, cols), d) for d in out_dtypes],
        compiler_params=_params("parallel"),
    )(*ins)


def _adamw_tile(w, g, m, v):
    m = ADAM_B1 * m + (1.0 - ADAM_B1) * g
    v = ADAM_B2 * v + (1.0 - ADAM_B2) * jnp.square(g)
    m_hat = m / (1.0 - ADAM_B1 ** ADAM_STEP)
    v_hat = v / (1.0 - ADAM_B2 ** ADAM_STEP)
    delta = -ADAM_LR * (m_hat / (jnp.sqrt(v_hat) + ADAM_EPS) + ADAM_WD * w)
    return delta, m, v


def _adamw(name, w, g, m, v):
    return _elementwise(_adamw_tile, name, [w, g, m, v], [F32, F32, F32])


def _chip_sum(name, g4, got, place):
    nchip, hr, cols = got.shape
    tile = _row_tile(hr, cols)
    nblk = hr // tile

    def body(pl_ref, g_ref, o_ref, s32_ref, s16_ref):
        s = g_ref[...] + o_ref[...]
        s32_ref[...] = s
        s16_ref[...] = s.astype(BF16)

    spec = pl.BlockSpec((None, tile, cols), lambda k, i, pr: (k, i, 0))
    return pl.pallas_call(
        body, name=name,
        grid_spec=pltpu.PrefetchScalarGridSpec(
            num_scalar_prefetch=1, grid=(nchip, nblk),
            in_specs=[pl.BlockSpec((None, tile, cols), lambda k, i, pr: (k, pr[1] * nblk + i, 0)), spec],
            out_specs=[spec, spec]),
        out_shape=[jax.ShapeDtypeStruct(got.shape, F32), jax.ShapeDtypeStruct(got.shape, BF16)],
        compiler_params=_params("parallel", "parallel"),
    )(place, g4, got)


def _mesh_sum(name, s32, got, place):
    _, hr, cols = s32.shape
    tile = _row_tile(hr, cols)

    def body(pl_ref, own_ref, g0_ref, g1_ref, g2_ref, o_ref):
        o_ref[...] = ((own_ref[...] + g0_ref[...].astype(F32)) + g1_ref[...].astype(F32)) + g2_ref[...].astype(F32)

    slab = lambda j: pl.BlockSpec((None, tile, cols), lambda i, pr: (j, i, 0))
    return pl.pallas_call(
        body, name=name,
        grid_spec=pltpu.PrefetchScalarGridSpec(
            num_scalar_prefetch=1, grid=(hr // tile,),
            in_specs=[pl.BlockSpec((None, tile, cols), lambda i, pr: (pr[0], i, 0)), slab(0), slab(1), slab(2)],
            out_specs=pl.BlockSpec((None, tile, cols), lambda i, pr: (pr[1], i, 0))),
        out_shape=jax.ShapeDtypeStruct((2, hr, cols), F32),
        compiler_params=_params("parallel"),
    )(place, s32, got, got, got)


def _reduce_scatter(grads):
    names = list(grads)
    place = _place_operand()
    got_a = _exchange_halves([grads[k] for k in names])
    sums = [_chip_sum("chip_sum_" + k, grads[k], g, place) for k, g in zip(names, got_a)]
    got_b = _scatter_to_chips([s[1] for s in sums])
    return {k: _mesh_sum("mesh_sum_" + k, s[0], g, place) for k, s, g in zip(names, sums, got_b)}


def _place_operand():
    return jnp.stack([2 * lax.axis_index("x") + lax.axis_index("y"), lax.axis_index("c")]).astype(jnp.int32)


def _per_chip(name, g):
    if name == "w_in":
        return jnp.transpose(g.reshape(D_MODEL, N_CHIPS, D_IN // N_CHIPS), (1, 0, 2))
    if name == "w_proj":
        return jnp.transpose(g.reshape(PLE_DIM, N_CHIPS, D_MODEL // N_CHIPS), (1, 0, 2))
    if name == "w_up4":
        return g
    return g.reshape(N_CHIPS, g.shape[0] // N_CHIPS, g.shape[1])


class _EarlyReducer:
    def start(self, grads):
        self.names = list(grads)
        self.place = _place_operand()
        slabs = [_per_chip(k, grads[k]) for k in self.names]
        halves = [jax.ShapeDtypeStruct((s.shape[0], s.shape[1] // 2, s.shape[2]), F32) for s in slabs]
        self.a = _copies_start("exchange_start", _exchange_copies, len(slabs), slabs, halves, slabs[0])
        return self.a[-1]

    def middle(self, after):
        slabs, got = _copies_wait("exchange_wait", _exchange_copies, self.a, after)
        self.sums = [_chip_sum("early_chip_sum_" + k, s, g, self.place) for k, s, g in zip(self.names, slabs, got)]
        s16 = [s[1] for s in self.sums]
        lands = [jax.ShapeDtypeStruct((3,) + s.shape[1:], BF16) for s in s16]
        self.b = _copies_start("scatter_start", _scatter_copies, 3 * len(s16), s16, lands, s16[0])
        return self.b[-1]

    def finish(self, after):
        _, got = _copies_wait("scatter_wait", _scatter_copies, self.b, after)
        self.bufs = {k: _mesh_sum("early_mesh_sum_" + k, s[0], g, self.place)
                     for k, s, g in zip(self.names, self.sums, got)}


def kernel(x, p, norm_mix, w_in, conv_w, a_log, dt_bias, dn_norm, sinks, w_o, norm_mlp, w_up, w_down, norm_ple, w_ple_gate, w_ple_proj, norm_final, loss_target, m_norm_mix, m_w_in, m_conv_w, m_a_log, m_dt_bias, m_dn_norm, m_sinks, m_w_o, m_norm_mlp, m_w_up, m_w_down, m_norm_ple, m_w_ple_gate, m_w_ple_proj, m_norm_final, v_norm_mix, v_w_in, v_conv_w, v_a_log, v_dt_bias, v_dn_norm, v_sinks, v_w_o, v_norm_mlp, v_w_up, v_w_down, v_norm_ple, v_w_ple_gate, v_w_ple_proj, v_norm_final):
    chip = 2 * lax.axis_index("x") + lax.axis_index("y")
    big = dict(w_in=w_in[0], w_o=w_o[0], w_up=w_up[0], w_down=w_down[0], w_gate=w_ple_gate[0], w_proj=w_ple_proj[0])
    big_m = dict(w_in=m_w_in[0], w_o=m_w_o[0], w_up=m_w_up[0], w_down=m_w_down[0], w_gate=m_w_ple_gate[0], w_proj=m_w_ple_proj[0])
    big_v = dict(w_in=v_w_in[0], w_o=v_w_o[0], w_up=v_w_up[0], w_down=v_w_down[0], w_gate=v_w_ple_gate[0], w_proj=v_w_ple_proj[0])
    names = list(big)

    w_in_all, conv_all = _gather_weights([big["w_in"].astype(BF16)], conv_w[0])
    late_names = names[1:]
    late_shards = [big[k].astype(BF16) for k in late_names]
    gather = _copies_start("gather_start", _late_copies, LATE_COPIES * len(late_shards), late_shards,
                           [jax.ShapeDtypeStruct((N_CHIPS,) + s.shape, BF16) for s in late_shards], w_in_all)
    token = gather[-1]
    w = dict(w_in=jnp.transpose(w_in_all, (1, 0, 2)).reshape(D_MODEL, D_IN),
             conv_w=jnp.transpose(conv_all, (1, 0, 2)).reshape(DN_CONV, CONV_CH))
    sm = dict(norm_mix=norm_mix[0] + token[0, 0], a_log=a_log[0], dt_bias=dt_bias[0], dn_norm=dn_norm[0],
              sinks=sinks[0], norm_mlp=norm_mlp[0], norm_ple=norm_ple[0], norm_final=norm_final)

    def late(after):
        gw = dict(zip(late_names, _copies_wait("gather_wait", _late_copies, gather, after)[1]))
        return dict(w_o=gw["w_o"].reshape(D_MODEL, D_MODEL), w_up4=gw["w_up"], w_down=gw["w_down"].reshape(D_FF, D_MODEL),
                    w_gate=gw["w_gate"].reshape(D_MODEL, D_MODEL), w_proj4=gw["w_proj"])

    reducer = _EarlyReducer()
    sums, grad_x, g = _local_step(x[0], p[0, 0], loss_target[0], sm, w, late, reducer)

    bufs = dict(reducer.bufs, **_reduce_scatter({"w_in": _per_chip("w_in", g["w_in"])}))
    grad_key = dict(w_in="w_in", w_o="w_o", w_up="w_up4", w_down="w_down", w_gate="w_gate", w_proj="w_proj")
    full = _share_halves([bufs[grad_key[k]] for k in names])
    red = {k: f.reshape(-1, f.shape[-1]) for k, f in zip(names, full)}

    row = lambda v: jnp.zeros((SMALL_COLS,), F32).at[:v.shape[0]].set(v)
    misc = jnp.zeros((SMALL_COLS,), F32).at[0:4].set(sums["a_log"]).at[4:8].set(sums["dt_bias"]) \
        .at[8:16].set(sums["sinks"]).at[128:256].set(sums["dn_norm"]).at[256].set(sums["loss"])
    small = jnp.concatenate([sums["conv_w"], jnp.stack([row(sums["norm_mix"]), row(sums["norm_mlp"]), row(sums["norm_ple"]),
                                                        row(sums["norm_final"]), misc]),
                             jnp.zeros((SMALL_ROWS - 9, SMALL_COLS), F32)], axis=0)
    tot = _allreduce_small(small)
    loss = tot[8, 256]
    ncw = CONV_CH // N_CHIPS

    def pack(cw, nmix, nmlp, nple, nfin, al, dtb, sk, dnn):
        misc_p = jnp.zeros((SMALL_COLS,), F32).at[0:4].set(al).at[4:8].set(dtb).at[8:16].set(sk).at[128:256].set(dnn)
        cw_p = jnp.zeros((DN_CONV, SMALL_COLS), F32).at[:, :ncw].set(cw)
        return jnp.concatenate([cw_p, jnp.stack([row(nmix), row(nmlp), row(nple), row(nfin), misc_p]),
                                jnp.zeros((SMALL_ROWS - 9, SMALL_COLS), F32)], axis=0)

    def unpack(buf):
        return dict(conv_w=buf[0:4, :ncw][None], norm_mix=buf[4, :D_MODEL][None], norm_mlp=buf[5, :D_MODEL][None],
                    norm_ple=buf[6, :D_MODEL][None], norm_final=buf[7, :D_MODEL], a_log=buf[8, 0:4][None],
                    dt_bias=buf[8, 4:8][None], sinks=buf[8, 8:16][None], dn_norm=buf[8, 128:256][None])

    g_conv_shard = lax.dynamic_slice(tot[0:4], (0, chip * ncw), (DN_CONV, ncw))
    g_small = pack(g_conv_shard, tot[4, :D_MODEL], tot[5, :D_MODEL], tot[6, :D_MODEL], tot[7, :D_MODEL],
                   tot[8, 0:4], tot[8, 4:8], tot[8, 8:16], tot[8, 128:256])
    w_small = pack(conv_w[0], norm_mix[0], norm_mlp[0], norm_ple[0], norm_final, a_log[0], dt_bias[0], sinks[0], dn_norm[0])
    m_small = pack(m_conv_w[0], m_norm_mix[0], m_norm_mlp[0], m_norm_ple[0], m_norm_final, m_a_log[0], m_dt_bias[0],
                   m_sinks[0], m_dn_norm[0])
    v_small = pack(v_conv_w[0], v_norm_mix[0], v_norm_mlp[0], v_norm_ple[0], v_norm_final, v_a_log[0], v_dt_bias[0],
                   v_sinks[0], v_dn_norm[0])

    d_s, m_s, v_s = (unpack(b) for b in _adamw("adamw_small", w_small, g_small, m_small, v_small))
    g_s = unpack(g_small)
    out_g, out_d, out_m, out_v = dict(g_s), dict(d_s), dict(m_s), dict(v_s)
    ref_name = dict(w_in="w_in", w_o="w_o", w_up="w_up", w_down="w_down", w_gate="w_ple_gate", w_proj="w_ple_proj")
    for k in names:
        d_k, m_k, v_k = _adamw("adamw_" + k, big[k], red[k], big_m[k], big_v[k])
        out_g[ref_name[k]], out_d[ref_name[k]] = red[k][None], d_k[None]
        out_m[ref_name[k]], out_v[ref_name[k]] = m_k[None], v_k[None]
    order = ["norm_mix", "w_in", "conv_w", "a_log", "dt_bias", "dn_norm", "sinks", "w_o", "norm_mlp", "w_up", "w_down",
             "norm_ple", "w_ple_gate", "w_ple_proj", "norm_final"]
    return (loss, grad_x[None], *[out_g[k] for k in order], *[out_d[k] for k in order],
            *[out_m[k] for k in order], *[out_v[k] for k in order])
```

```python
import functools

import jax
import jax.numpy as jnp
from jax import lax
from jax.experimental import pallas as pl
from jax.experimental.pallas import tpu as pltpu

F32 = jnp.float32
BF16 = jnp.bfloat16
MXU_DTYPE = jnp.bfloat16
HI = lax.Precision.HIGHEST

D_MODEL = 1024
PLE_DIM = 256
ATTN_HEADS = 8
ATTN_KV_HEADS = 2
ATTN_GROUPS = ATTN_HEADS // ATTN_KV_HEADS
ATTN_HEAD_DIM = 64
ATTN_BLOCK = 128
ROPE_THETA = 10000.0
DN_HEADS = 4
DN_HEAD_DIM = 128
DN_CONV = 4
DN_CHUNK = 64
D_FF = 4 * D_MODEL
EPS = 1e-6
ATTN_Q = ATTN_HEADS * ATTN_HEAD_DIM
ATTN_KV = ATTN_KV_HEADS * ATTN_HEAD_DIM
DN_W = DN_HEADS * DN_HEAD_DIM
CONV_CH = 3 * DN_W
D_IN = ATTN_Q + 2 * ATTN_KV + 4 * DN_W + 2 * DN_HEADS
DN_COLS = 4 * DN_W + 128
DN_SCALE = DN_HEAD_DIM ** -0.5
ATTN_SCALE = ATTN_HEAD_DIM ** -0.5
FF_BLOCKS = 4
FF_BLOCK = D_FF // FF_BLOCKS

ADAM_LR = 0.001
ADAM_B1 = 0.9
ADAM_B2 = 0.999
ADAM_EPS = 1e-08
ADAM_WD = 0.01
ADAM_STEP = 10

V7X_VMEM_BYTES = 64 * 1024 * 1024
VMEM_LIMIT = 48 * 1024 * 1024

NN = ((1,), (0,))
NT = ((1,), (1,))
TN = ((0,), (0,))


def _dot(a, b, dims=NN, prec=None):
    return lax.dot_general(a, b, (dims, ((), ())), precision=prec, preferred_element_type=F32)


def _sigmoid(x):
    return 1.0 / (1.0 + jnp.exp(-x))


def _softplus(x):
    return jnp.maximum(x, 0.0) + jnp.log(1.0 + jnp.exp(-jnp.abs(x)))


def _params(*sem):
    return pltpu.CompilerParams(dimension_semantics=sem, vmem_limit_bytes=VMEM_LIMIT)


def _rms_fwd(xv, g):
    r = lax.rsqrt(jnp.mean(xv * xv, axis=-1, keepdims=True) + EPS)
    return xv * r * g


def _rms_bwd(xv, g, dn):
    r = lax.rsqrt(jnp.mean(xv * xv, axis=-1, keepdims=True) + EPS)
    xh = xv * r
    dg = jnp.sum(dn * xh, axis=0, keepdims=True)
    dxh = dn * g
    dx = r * (dxh - xh * jnp.mean(dxh * xh, axis=-1, keepdims=True))
    return dx, dg


def _full(shape):
    return pl.BlockSpec(shape, lambda *_: (0,) * len(shape))


def _inproj(x, g_mix, wa, wd, tm):
    t = x.shape[0]

    def body(x_ref, g_ref, wa_ref, wd_ref, u_ref, pa_ref, pd_ref):
        u = _rms_fwd(x_ref[...], g_ref[...]).astype(MXU_DTYPE)
        u_ref[...] = u
        pa_ref[...] = _dot(u, wa_ref[...])
        pd_ref[...] = _dot(u, wd_ref[...])

    na, nd = wa.shape[1], wd.shape[1]
    return pl.pallas_call(
        body, name="inproj", grid=(t // tm,),
        in_specs=[pl.BlockSpec((tm, D_MODEL), lambda i: (i, 0)), _full((1, D_MODEL)),
                  _full((D_MODEL, na)), _full((D_MODEL, nd))],
        out_specs=[pl.BlockSpec((tm, D_MODEL), lambda i: (i, 0)), pl.BlockSpec((tm, na), lambda i: (i, 0)),
                   pl.BlockSpec((tm, nd), lambda i: (i, 0))],
        out_shape=[jax.ShapeDtypeStruct((t, D_MODEL), MXU_DTYPE), jax.ShapeDtypeStruct((t, na), F32),
                   jax.ShapeDtypeStruct((t, nd), F32)],
        compiler_params=_params("parallel"),
    )(x, g_mix, wa, wd)


def _oproj(x, ao, dn, wo_a, wo_d, tm):
    t = x.shape[0]

    def body(x_ref, ao_ref, dn_ref, wa_ref, wd_ref, h_ref):
        h_ref[...] = (x_ref[...] + _dot(ao_ref[...].astype(MXU_DTYPE), wa_ref[...])
                      + _dot(dn_ref[...].astype(MXU_DTYPE), wd_ref[...]))

    half = ao.shape[1]
    return pl.pallas_call(
        body, name="oproj", grid=(t // tm,),
        in_specs=[pl.BlockSpec((tm, D_MODEL), lambda i: (i, 0)), pl.BlockSpec((tm, half), lambda i: (i, 0)),
                  pl.BlockSpec((tm, half), lambda i: (i, 0)), _full((half, D_MODEL)), _full((half, D_MODEL))],
        out_specs=pl.BlockSpec((tm, D_MODEL), lambda i: (i, 0)),
        out_shape=jax.ShapeDtypeStruct((t, D_MODEL), F32),
        compiler_params=_params("parallel"),
    )(x, ao, dn, wo_a, wo_d)


def _mlp_fwd(h1, g_mlp, w_up4, w_down, tm):
    t = h1.shape[0]

    def body(h_ref, g_ref, wu_ref, wd_ref, m_ref, r_ref, h2_ref, acc_ref):
        k = pl.program_id(1)

        @pl.when(k == 0)
        def _():
            m_ref[...] = _rms_fwd(h_ref[...], g_ref[...]).astype(MXU_DTYPE)
            acc_ref[...] = jnp.zeros_like(acc_ref)

        r = jnp.maximum(_dot(m_ref[...], wu_ref[...]), 0.0)
        r_ref[...] = r.astype(MXU_DTYPE)
        s = jnp.square(r).astype(MXU_DTYPE)
        acc_ref[...] += _dot(s, wd_ref[...])

        @pl.when(k == FF_BLOCKS - 1)
        def _():
            h2_ref[...] = h_ref[...] + acc_ref[...]

    return pl.pallas_call(
        body, name="mlp_fwd", grid=(t // tm, FF_BLOCKS),
        in_specs=[pl.BlockSpec((tm, D_MODEL), lambda i, k: (i, 0)), _full((1, D_MODEL)),
                  pl.BlockSpec((None, D_MODEL, FF_BLOCK), lambda i, k: (k, 0, 0)),
                  pl.BlockSpec((FF_BLOCK, D_MODEL), lambda i, k: (k, 0))],
        out_specs=[pl.BlockSpec((tm, D_MODEL), lambda i, k: (i, 0)), pl.BlockSpec((tm, FF_BLOCK), lambda i, k: (i, k)),
                   pl.BlockSpec((tm, D_MODEL), lambda i, k: (i, 0))],
        out_shape=[jax.ShapeDtypeStruct((t, D_MODEL), MXU_DTYPE), jax.ShapeDtypeStruct((t, D_FF), MXU_DTYPE),
                   jax.ShapeDtypeStruct((t, D_MODEL), F32)],
        scratch_shapes=[pltpu.VMEM((tm, D_MODEL), F32)],
        compiler_params=_params("parallel", "arbitrary"),
    )(h1, g_mlp, w_up4, w_down)


def _ple_loss(h2, p, tgt, g_ple, g_fin, w_gate, w_proj, tm):
    t = h2.shape[0]

    def body(h_ref, p_ref, t_ref, gp_ref, gf_ref, wg_ref, wp_ref,
             dh_ref, dhb_ref, dgp_ref, dpp_ref, n3_ref, pb_ref, acc_ref):
        @pl.when(pl.program_id(0) == 0)
        def _():
            acc_ref[...] = jnp.zeros_like(acc_ref)

        h = h_ref[...]
        g_ple_v, g_fin_v = gp_ref[...], gf_ref[...]
        n3 = _rms_fwd(h, g_ple_v).astype(MXU_DTYPE)
        n3_ref[...] = n3
        gate = _sigmoid(_dot(n3, wg_ref[...]))
        pb = p_ref[...].astype(MXU_DTYPE)
        pb_ref[...] = pb
        pp = _dot(pb, wp_ref[...])
        h3 = h + gate * pp
        r4 = lax.rsqrt(jnp.mean(h3 * h3, axis=-1, keepdims=True) + EPS)
        xh4 = h3 * r4
        e = xh4 * g_fin_v - t_ref[...]
        loss = 0.5 * jnp.sum(jnp.mean(e * e, axis=-1, keepdims=True), axis=0, keepdims=True)
        dy = e * (1.0 / D_MODEL)
        dg_fin = jnp.sum(dy * xh4, axis=0, keepdims=True)
        dxh = dy * g_fin_v
        dh3 = r4 * (dxh - xh4 * jnp.mean(dxh * xh4, axis=-1, keepdims=True))
        dpp_ref[...] = (dh3 * gate).astype(MXU_DTYPE)
        dgp = (dh3 * pp * gate * (1.0 - gate)).astype(MXU_DTYPE)
        dgp_ref[...] = dgp
        dn3 = _dot(dgp, wg_ref[...], NT)
        dx, dg_ple = _rms_bwd(h, g_ple_v, dn3)
        dh2 = dh3 + dx
        dh_ref[...] = dh2
        dhb_ref[...] = dh2.astype(MXU_DTYPE)
        acc_ref[0:1, :] += dg_fin
        acc_ref[1:2, :] += dg_ple
        acc_ref[2:3, :] += jnp.broadcast_to(loss, (1, D_MODEL))

    row = lambda w: pl.BlockSpec((tm, w), lambda i: (i, 0))
    return pl.pallas_call(
        body, name="ple_loss", grid=(t // tm,),
        in_specs=[row(D_MODEL), row(PLE_DIM), row(D_MODEL), _full((1, D_MODEL)), _full((1, D_MODEL)),
                  _full((D_MODEL, D_MODEL)), _full((PLE_DIM, D_MODEL))],
        out_specs=[row(D_MODEL), row(D_MODEL), row(D_MODEL), row(D_MODEL), row(D_MODEL), row(PLE_DIM),
                   _full((8, D_MODEL))],
        out_shape=[jax.ShapeDtypeStruct((t, D_MODEL), F32), jax.ShapeDtypeStruct((t, D_MODEL), MXU_DTYPE),
                   jax.ShapeDtypeStruct((t, D_MODEL), MXU_DTYPE), jax.ShapeDtypeStruct((t, D_MODEL), MXU_DTYPE),
                   jax.ShapeDtypeStruct((t, D_MODEL), MXU_DTYPE), jax.ShapeDtypeStruct((t, PLE_DIM), MXU_DTYPE),
                   jax.ShapeDtypeStruct((8, D_MODEL), F32)],
        compiler_params=_params("arbitrary"),
    )(h2, p, tgt, g_ple, g_fin, w_gate, w_proj)


def _mlp_bwd(dh2, dh2b, r, h1, g_mlp, w_up4, w_down, tm):
    t = h1.shape[0]

    def body(dh_ref, dhb_ref, r_ref, h_ref, g_ref, wu_ref, wd_ref,
             da_ref, dh1_ref, dh1b_ref, acc_ref, dm_ref):
        i, k = pl.program_id(0), pl.program_id(1)

        @pl.when((i == 0) & (k == 0))
        def _():
            acc_ref[...] = jnp.zeros_like(acc_ref)

        @pl.when(k == 0)
        def _():
            dm_ref[...] = jnp.zeros_like(dm_ref)

        ds = _dot(dhb_ref[...], wd_ref[...], NT)
        da = (ds * (2.0 * r_ref[...].astype(F32))).astype(MXU_DTYPE)
        da_ref[...] = da
        dm_ref[...] += _dot(da, wu_ref[...], NT)

        @pl.when(k == FF_BLOCKS - 1)
        def _():
            dx, dg = _rms_bwd(h_ref[...], g_ref[...], dm_ref[...])
            dh1 = dh_ref[...] + dx
            dh1_ref[...] = dh1
            dh1b_ref[...] = dh1.astype(MXU_DTYPE)
            acc_ref[0:1, :] += dg

    tok = lambda w: pl.BlockSpec((tm, w), lambda i, k: (i, 0))
    return pl.pallas_call(
        body, name="mlp_bwd", grid=(t // tm, FF_BLOCKS),
        in_specs=[tok(D_MODEL), tok(D_MODEL), pl.BlockSpec((tm, FF_BLOCK), lambda i, k: (i, k)), tok(D_MODEL),
                  _full((1, D_MODEL)), pl.BlockSpec((None, D_MODEL, FF_BLOCK), lambda i, k: (k, 0, 0)),
                  pl.BlockSpec((FF_BLOCK, D_MODEL), lambda i, k: (k, 0))],
        out_specs=[pl.BlockSpec((tm, FF_BLOCK), lambda i, k: (i, k)),
                   tok(D_MODEL), tok(D_MODEL), pl.BlockSpec((8, D_MODEL), lambda i, k: (0, 0))],
        out_shape=[jax.ShapeDtypeStruct((t, D_FF), MXU_DTYPE),
                   jax.ShapeDtypeStruct((t, D_MODEL), F32), jax.ShapeDtypeStruct((t, D_MODEL), MXU_DTYPE),
                   jax.ShapeDtypeStruct((8, D_MODEL), F32)],
        scratch_shapes=[pltpu.VMEM((tm, D_MODEL), F32)],
        compiler_params=_params("arbitrary", "arbitrary"),
    )(dh2, dh2b, r, h1, g_mlp, w_up4, w_down)


def _oproj_bwd(dh1b, wo_a, wo_d, tm, dep):
    t = dh1b.shape[0]
    half = wo_a.shape[0]

    def body(d_ref, wa_ref, wd_ref, dep_ref, da_ref, dd_ref):
        d = d_ref[...]
        da_ref[...] = _dot(d, wa_ref[...], NT)
        dd_ref[...] = _dot(d, wd_ref[...], NT)

    return pl.pallas_call(
        body, name="oproj_bwd", grid=(t // tm,),
        in_specs=[pl.BlockSpec((tm, D_MODEL), lambda i: (i, 0)), _full((half, D_MODEL)), _full((half, D_MODEL)),
                  pl.BlockSpec(memory_space=pl.ANY)],
        out_specs=[pl.BlockSpec((tm, half), lambda i: (i, 0)), pl.BlockSpec((tm, half), lambda i: (i, 0))],
        out_shape=[jax.ShapeDtypeStruct((t, half), F32), jax.ShapeDtypeStruct((t, half), F32)],
        compiler_params=_params("parallel"),
    )(dh1b, wo_a, wo_d, dep)


def _inproj_bwd(x, dh1, g_mix, grads, weights, tm):
    t = x.shape[0]
    n = len(grads)

    def body(*refs):
        x_ref, dh_ref, g_ref = refs[:3]
        g_refs, w_refs = refs[3:3 + n], refs[3 + n:3 + 2 * n]
        dx_ref, acc_ref = refs[3 + 2 * n:]

        @pl.when(pl.program_id(0) == 0)
        def _():
            acc_ref[...] = jnp.zeros_like(acc_ref)

        du = _dot(g_refs[0][...], w_refs[0][...], NT)
        for j in range(1, n):
            du += _dot(g_refs[j][...], w_refs[j][...], NT)
        dx, dg = _rms_bwd(x_ref[...], g_ref[...], du)
        dx_ref[...] = dh_ref[...] + dx
        acc_ref[0:1, :] += dg

    tok = lambda w: pl.BlockSpec((tm, w), lambda i: (i, 0))
    return pl.pallas_call(
        body, name="inproj_bwd", grid=(t // tm,),
        in_specs=[tok(D_MODEL), tok(D_MODEL), _full((1, D_MODEL))] + [tok(g.shape[1]) for g in grads]
                 + [_full(w.shape) for w in weights],
        out_specs=[tok(D_MODEL), _full((8, D_MODEL))],
        out_shape=[jax.ShapeDtypeStruct((t, D_MODEL), F32), jax.ShapeDtypeStruct((8, D_MODEL), F32)],
        compiler_params=_params("arbitrary"),
    )(x, dh1, g_mix, *grads, *weights)


def _wgrad(a, b, name, tk, tn, tt, stacked=False, prep=None):
    t, kdim = a.shape
    ncols = b.shape[1]

    def body(a_ref, b_ref, o_ref):
        @pl.when(pl.program_id(2) == 0)
        def _():
            o_ref[...] = jnp.zeros_like(o_ref)

        av = a_ref[...] if prep is None else prep(a_ref[...])
        o_ref[...] += _dot(av, b_ref[...], TN)

    if stacked:
        out_spec = pl.BlockSpec((None, tk, tn), lambda i, j, s: (j, i, 0))
        out_shape = jax.ShapeDtypeStruct((ncols // tn, kdim, tn), F32)
    else:
        out_spec = pl.BlockSpec((tk, tn), lambda i, j, s: (i, j))
        out_shape = jax.ShapeDtypeStruct((kdim, ncols), F32)
    return pl.pallas_call(
        body, name=name, grid=(kdim // tk, ncols // tn, t // tt),
        in_specs=[pl.BlockSpec((tt, tk), lambda i, j, s: (s, i)), pl.BlockSpec((tt, tn), lambda i, j, s: (s, j))],
        out_specs=out_spec, out_shape=out_shape,
        compiler_params=_params("parallel", "parallel", "arbitrary"),
    )(a, b)


def _rope_tables(t):
    half = ATTN_HEAD_DIM // 2
    inv = 1.0 / (ROPE_THETA ** (jnp.arange(half, dtype=F32) * (2.0 / ATTN_HEAD_DIM)))
    ang = jnp.arange(t, dtype=F32)[:, None] * inv[None, :]
    cos, sin = jnp.cos(ang), jnp.sin(ang)
    cos2 = jnp.concatenate([cos, cos], axis=-1)
    sin2 = jnp.concatenate([-sin, sin], axis=-1)
    return jnp.tile(cos2, (1, 2)), jnp.tile(sin2, (1, 2))


def _swap_halves(tv):
    w = tv.shape[-1]
    lane = lax.broadcasted_iota(jnp.int32, tv.shape, tv.ndim - 1)
    first = (lane % ATTN_HEAD_DIM) < (ATTN_HEAD_DIM // 2)
    return jnp.where(first, pltpu.roll(tv, w - ATTN_HEAD_DIM // 2, tv.ndim - 1),
                     pltpu.roll(tv, ATTN_HEAD_DIM // 2, tv.ndim - 1))


def _rope(tv, cos, sin):
    return tv * cos + _swap_halves(tv) * sin


def _rope_bwd(dv, cos, sin):
    return dv * cos + _swap_halves(dv * sin)


def _attn_valid(first_block):
    c = lax.broadcasted_iota(jnp.int32, (2 * ATTN_BLOCK, ATTN_BLOCK), 0)
    r = lax.broadcasted_iota(jnp.int32, (2 * ATTN_BLOCK, ATTN_BLOCK), 1)
    return (c > r) & (c <= r + ATTN_BLOCK) & ((c >= ATTN_BLOCK) | jnp.logical_not(first_block))


def _attn_probs(st, sink, valid):
    s = jnp.where(valid, st * ATTN_SCALE, -jnp.inf)
    m = jnp.maximum(jnp.max(s, axis=0, keepdims=True), sink)
    e = jnp.where(valid, jnp.exp(s - m), 0.0)
    es = jnp.exp(sink - m)
    inv = 1.0 / (jnp.sum(e, axis=0, keepdims=True) + es)
    return e * inv, es * inv


def _lane_scalar(vec, idx):
    lane = lax.broadcasted_iota(jnp.int32, vec.shape, 1)
    return jnp.sum(jnp.where(lane == idx, vec, 0.0), axis=-1, keepdims=True)


def _attn_specs(nb):
    cur = lambda w, cb: pl.BlockSpec((ATTN_BLOCK, w), lambda i: (jnp.minimum(i, nb - 1), cb))
    prev = lambda w, cb: pl.BlockSpec((ATTN_BLOCK, w), lambda i: (jnp.maximum(jnp.minimum(i, nb - 1) - 1, 0), cb))
    kcol, vcol = ATTN_Q // ATTN_KV, ATTN_Q // ATTN_KV + 1
    return [cur(ATTN_Q, 0), cur(ATTN_KV, kcol), prev(ATTN_KV, kcol), cur(ATTN_KV, vcol), prev(ATTN_KV, vcol),
            cur(ATTN_KV, 0), cur(ATTN_KV, 0), prev(ATTN_KV, 0), prev(ATTN_KV, 0), _full((1, 128))]


def _attn_fwd(pa, cos, sin, sinks_vec):
    t = pa.shape[0]
    nb = t // ATTN_BLOCK

    def body(q_ref, kc_ref, kp_ref, vc_ref, vp_ref, cc_ref, sc_ref, cp_ref, sp_ref, sk_ref, o_ref):
        first = pl.program_id(0) == 0
        cc, sc = cc_ref[...], sc_ref[...]
        q = _rope(q_ref[...], jnp.tile(cc, (1, ATTN_Q // ATTN_KV)), jnp.tile(sc, (1, ATTN_Q // ATTN_KV)))
        kc = _rope(kc_ref[...], cc, sc)
        kp = _rope(kp_ref[...], cp_ref[...], sp_ref[...])
        vc, vp = vc_ref[...], vp_ref[...]
        sk = sk_ref[...]
        valid = _attn_valid(first)
        kv = lambda tp, tc, hk: jnp.concatenate([tp[:, hk * ATTN_HEAD_DIM:(hk + 1) * ATTN_HEAD_DIM],
                                                 tc[:, hk * ATTN_HEAD_DIM:(hk + 1) * ATTN_HEAD_DIM]], axis=0)
        kwins = [kv(kp, kc, hk) for hk in range(ATTN_KV_HEADS)]
        vwins_t = [kv(vp, vc, hk).T for hk in range(ATTN_KV_HEADS)]
        heads = [slice(h * ATTN_HEAD_DIM, (h + 1) * ATTN_HEAD_DIM) for h in range(ATTN_HEADS)]
        scores = [_dot(kwins[h // ATTN_GROUPS], q[:, hs], NT) for h, hs in enumerate(heads)]
        probs = [_attn_probs(st, _lane_scalar(sk, h), valid)[0] for h, st in enumerate(scores)]
        for h, (hs, pt) in enumerate(zip(heads, probs)):
            o_ref[:, hs] = _dot(vwins_t[h // ATTN_GROUPS], pt).T.astype(o_ref.dtype)

    return pl.pallas_call(
        body, name="attn_fwd", grid=(nb,),
        in_specs=_attn_specs(nb),
        out_specs=pl.BlockSpec((ATTN_BLOCK, ATTN_Q), lambda i: (i, 0)),
        out_shape=jax.ShapeDtypeStruct((t, ATTN_Q), MXU_DTYPE),
        compiler_params=_params("parallel"),
    )(pa, pa, pa, pa, pa, cos, sin, cos, sin, sinks_vec)


def _attn_bwd(pa, cos, sin, sinks_vec, dao):
    t = pa.shape[0]
    nb = t // ATTN_BLOCK

    def body(q_ref, kc_ref, kp_ref, vc_ref, vp_ref, cc_ref, sc_ref, cp_ref, sp_ref, sk_ref, do_ref,
             dq_ref, dk_ref, dv_ref, acc_ref, dqr_ref, dkw_ref, dvw_ref, ck_ref, cv_ref):
        i = pl.program_id(0)

        @pl.when(i == 0)
        def _():
            acc_ref[...] = jnp.zeros_like(acc_ref)
            ck_ref[...] = jnp.zeros_like(ck_ref)
            cv_ref[...] = jnp.zeros_like(cv_ref)

        @pl.when(i < nb)
        def _():
            first = i == 0
            cc, sc = cc_ref[...], sc_ref[...]
            cq, sq = jnp.tile(cc, (1, ATTN_Q // ATTN_KV)), jnp.tile(sc, (1, ATTN_Q // ATTN_KV))
            q = _rope(q_ref[...], cq, sq)
            kc = _rope(kc_ref[...], cc, sc)
            kp = _rope(kp_ref[...], cp_ref[...], sp_ref[...])
            vc, vp = vc_ref[...], vp_ref[...]
            sk = sk_ref[...]
            do = do_ref[...]
            lane = lax.broadcasted_iota(jnp.int32, (1, 128), 1)
            dsink = jnp.zeros((1, 128), F32)
            valid = _attn_valid(first)
            kv = lambda tp, tc, hk: jnp.concatenate([tp[:, hk * ATTN_HEAD_DIM:(hk + 1) * ATTN_HEAD_DIM],
                                                     tc[:, hk * ATTN_HEAD_DIM:(hk + 1) * ATTN_HEAD_DIM]], axis=0)
            kwins = [kv(kp, kc, hk) for hk in range(ATTN_KV_HEADS)]
            vwins = [kv(vp, vc, hk) for hk in range(ATTN_KV_HEADS)]
            kwins_t = [kw.T for kw in kwins]
            heads = [slice(h * ATTN_HEAD_DIM, (h + 1) * ATTN_HEAD_DIM) for h in range(ATTN_HEADS)]
            scores = [_dot(kwins[h // ATTN_GROUPS], q[:, hs], NT) for h, hs in enumerate(heads)]
            dps = [_dot(vwins[h // ATTN_GROUPS], do[:, hs], NT) for h, hs in enumerate(heads)]
            pts, dsts = [], []
            for h, (st, dp_t) in enumerate(zip(scores, dps)):
                probs_t, psink = _attn_probs(st, _lane_scalar(sk, h), valid)
                delta = jnp.sum(probs_t * dp_t, axis=0, keepdims=True)
                pts.append(probs_t)
                dsts.append(probs_t * (dp_t - delta) * ATTN_SCALE)
                dsink += jnp.where(lane == h, jnp.sum(-psink * delta, axis=1, keepdims=True), 0.0)
            for h, (hs, ds_t) in enumerate(zip(heads, dsts)):
                dqr_ref[:, hs] = _dot(kwins_t[h // ATTN_GROUPS], ds_t).T
            for hk in range(ATTN_KV_HEADS):
                ks = slice(hk * ATTN_HEAD_DIM, (hk + 1) * ATTN_HEAD_DIM)
                group = range(hk * ATTN_GROUPS, (hk + 1) * ATTN_GROUPS)
                ds_g = jnp.concatenate([dsts[h] for h in group], axis=1)
                p_g = jnp.concatenate([pts[h] for h in group], axis=1)
                q_g = jnp.concatenate([q[:, heads[h]] for h in group], axis=0)
                do_g = jnp.concatenate([do[:, heads[h]] for h in group], axis=0)
                dkw_ref[:, ks] = _dot(ds_g, q_g)
                dvw_ref[:, ks] = _dot(p_g, do_g)
            acc_ref[0:1, :] += dsink
            dq_ref[...] = _rope_bwd(dqr_ref[...], cq, sq).astype(dq_ref.dtype)
            dk_ref[...] = (ck_ref[...] + _rope_bwd(dkw_ref[0:ATTN_BLOCK, :], cp_ref[...], sp_ref[...])).astype(dk_ref.dtype)
            dv_ref[...] = (cv_ref[...] + dvw_ref[0:ATTN_BLOCK, :]).astype(dv_ref.dtype)
            ck_ref[...] = _rope_bwd(dkw_ref[ATTN_BLOCK:2 * ATTN_BLOCK, :], cc, sc)
            cv_ref[...] = dvw_ref[ATTN_BLOCK:2 * ATTN_BLOCK, :]

        @pl.when(i == nb)
        def _():
            dk_ref[...] = ck_ref[...].astype(dk_ref.dtype)
            dv_ref[...] = cv_ref[...].astype(dv_ref.dtype)

    prev_out = lambda w: pl.BlockSpec((ATTN_BLOCK, w), lambda i: (jnp.maximum(i - 1, 0), 0))
    return pl.pallas_call(
        body, name="attn_bwd", grid=(nb + 1,),
        in_specs=_attn_specs(nb) + [pl.BlockSpec((ATTN_BLOCK, ATTN_Q), lambda i: (jnp.minimum(i, nb - 1), 0))],
        out_specs=[pl.BlockSpec((ATTN_BLOCK, ATTN_Q), lambda i: (jnp.minimum(i, nb - 1), 0)), prev_out(ATTN_KV),
                   prev_out(ATTN_KV), _full((8, 128))],
        out_shape=[jax.ShapeDtypeStruct((t, ATTN_Q), MXU_DTYPE), jax.ShapeDtypeStruct((t, ATTN_KV), MXU_DTYPE),
                   jax.ShapeDtypeStruct((t, ATTN_KV), MXU_DTYPE), jax.ShapeDtypeStruct((8, 128), F32)],
        scratch_shapes=[pltpu.VMEM((ATTN_BLOCK, ATTN_Q), F32), pltpu.VMEM((2 * ATTN_BLOCK, ATTN_KV), F32),
                        pltpu.VMEM((2 * ATTN_BLOCK, ATTN_KV), F32), pltpu.VMEM((ATTN_BLOCK, ATTN_KV), F32),
                        pltpu.VMEM((ATTN_BLOCK, ATTN_KV), F32)],
        compiler_params=_params("arbitrary"),
    )(pa, pa, pa, pa, pa, cos, sin, cos, sin, sinks_vec, dao)


PAIR = 2 * DN_CHUNK
HALO = 8


def _conv_window(cur_ref, prev_ref, xs_ref, tm):
    prev = jnp.where(pl.program_id(0) > 0, prev_ref[...], 0.0)
    xs_ref[0:HALO, :] = prev
    xs_ref[HALO:HALO + tm, :] = cur_ref[...]


def _conv_taps(xs_ref, cw_ref, tm):
    y = cw_ref[0:1, :] * xs_ref[pl.ds(HALO - DN_CONV + 1, tm), :]
    for j in range(1, DN_CONV):
        y += cw_ref[j:j + 1, :] * xs_ref[pl.ds(HALO - DN_CONV + 1 + j, tm), :]
    return y


def _gate_values(ba, al, dt):
    beta = _sigmoid(ba)
    pre = ba + dt
    g = -jnp.exp(al) * _softplus(pre)
    return beta, g, pre


def _dn_prep_specs(tm, t):
    return [pl.BlockSpec((tm, CONV_CH), lambda i: (i, 0)),
            pl.BlockSpec((HALO, CONV_CH), lambda i: (jnp.maximum(i * (tm // HALO) - 1, 0), 0)),
            pl.BlockSpec((tm, 128), lambda i: (i, 4 * DN_W // 128)),
            _full((DN_CONV, CONV_CH)), _full((1, 128)), _full((1, 128))]


def _dn_prep(pd, conv_w, al_vec, dt_vec, tm):
    t = pd.shape[0]

    def body(cur_ref, prev_ref, ba_ref, cw_ref, al_ref, dt_ref, qn_ref, kn_ref, vc_ref, gc_ref, gr_ref, xs_ref):
        _conv_window(cur_ref, prev_ref, xs_ref, tm)
        y = _conv_taps(xs_ref, cw_ref, tm)
        c = y * _sigmoid(y)
        for h in range(DN_HEADS):
            qs = slice(h * DN_HEAD_DIM, (h + 1) * DN_HEAD_DIM)
            ksl = slice(DN_W + h * DN_HEAD_DIM, DN_W + (h + 1) * DN_HEAD_DIM)
            qh, kh = c[:, qs], c[:, ksl]
            qn_ref[:, qs] = qh * lax.rsqrt(jnp.sum(qh * qh, axis=-1, keepdims=True) + EPS) * DN_SCALE
            kn_ref[:, qs] = kh * lax.rsqrt(jnp.sum(kh * kh, axis=-1, keepdims=True) + EPS)
        vc_ref[...] = c[:, 2 * DN_W:3 * DN_W]
        beta, g, _ = _gate_values(ba_ref[...], al_ref[...], dt_ref[...])
        lane = lax.broadcasted_iota(jnp.int32, beta.shape, 1)
        gb = jnp.where(lane < DN_HEADS, beta, jnp.where(lane < 2 * DN_HEADS, g, 0.0))
        gc_ref[...] = gb
        gr_ref[...] = gb.T[0:8, :]

    tok = lambda w: pl.BlockSpec((tm, w), lambda i: (i, 0))
    return pl.pallas_call(
        body, name="dn_prep", grid=(t // tm,),
        in_specs=_dn_prep_specs(tm, t),
        out_specs=[tok(DN_W), tok(DN_W), tok(DN_W), tok(128), pl.BlockSpec((8, tm), lambda i: (0, i))],
        out_shape=[jax.ShapeDtypeStruct((t, DN_W), F32)] * 3 + [jax.ShapeDtypeStruct((t, 128), F32),
                                                                 jax.ShapeDtypeStruct((8, t), F32)],
        scratch_shapes=[pltpu.VMEM((HALO + tm, CONV_CH), F32)],
        compiler_params=_params("parallel"),
    )(pd, pd, pd, conv_w, al_vec, dt_vec)


def _pair_masks():
    r = lax.broadcasted_iota(jnp.int32, (PAIR, PAIR), 0)
    c = lax.broadcasted_iota(jnp.int32, (PAIR, PAIR), 1)
    same = (r < DN_CHUNK) == (c < DN_CHUNK)
    return same & (r >= c), same & (r > c)


def _lane_col(mat, idx):
    lane = lax.broadcasted_iota(jnp.int32, mat.shape, 1)
    return jnp.sum(jnp.where(lane == idx, mat, 0.0), axis=-1, keepdims=True)


def _pair_cumsums(gc, gr, low):
    lowf = low.astype(F32)
    return _dot(lowf, gc, NN, HI), _dot(gr, lowf, NT, HI)


def _pair_gates(gc, cum_c, cum_r, low, h):
    beta = _lane_col(gc, h)
    gam = _lane_col(cum_c, DN_HEADS + h)
    gam_row = cum_r[DN_HEADS + h:DN_HEADS + h + 1, :]
    dm = jnp.where(low, jnp.exp(jnp.where(low, gam - gam_row, 0.0)), 0.0)
    row = lax.broadcasted_iota(jnp.int32, gam.shape, 0)
    gl = jnp.where(row < DN_CHUNK, gam[DN_CHUNK - 1:DN_CHUNK, :], gam[PAIR - 1:PAIR, :])
    return beta, gam, dm, gl


def _split(a):
    hi = a.astype(BF16)
    return hi, (a - hi.astype(F32)).astype(BF16)


def _dot_split(a, b, dims=NN):
    (ah, al), (bh, bl) = a, b
    la, lb = (1, 1) if dims == TN else ((0, 1) if dims == NN else (0, 0))
    r = _dot(jnp.concatenate([ah, al], axis=la), jnp.concatenate([bh, bl], axis=lb), dims)
    m, n = r.shape[0] // 2, r.shape[1] // 2
    return (r[m:, n:] + (r[:m, n:] + r[m:, :n])) + r[:m, :n]


def _unit_lower_inverses(lmats):
    n = lmats[0].shape[0]
    r = lax.broadcasted_iota(jnp.int32, (n, n), 0)
    c = lax.broadcasted_iota(jnp.int32, (n, n), 1)
    same = lambda size: (r & ~(size - 1)) == (c & ~(size - 1))
    base = DN_CHUNK // 4
    diag = [jnp.where(same(base), l, 0.0) for l in lmats]
    accs = [(r == c).astype(F32) - d for d in diag]
    splits = [_split(d) for d in diag]
    step = 1
    while 2 * step < base:
        splits = [_split(_dot_split(s, s)) for s in splits]
        accs = [acc + _dot_split(_split(acc), s) for acc, s in zip(accs, splits)]
        step *= 2
    size = base
    while size < DN_CHUNK:
        below = same(2 * size) & jnp.logical_not(same(size))
        tb = [_dot(acc, jnp.where(below, l, 0.0)) for acc, l in zip(accs, lmats)]
        accs = [acc - _dot(t, acc) for acc, t in zip(accs, tb)]
        size *= 2
    return accs


def _dn_intra(qn, kn, vc, gc, gr):
    t = qn.shape[0]
    npair = t // PAIR

    def body(q_ref, k_ref, v_ref, gc_ref, gr_ref, u_ref, w_ref, qg_ref, kd_ref, a_ref, ti_ref, dl_ref):
        gc_v = gc_ref[...]
        low, strict = _pair_masks()
        cum_c, cum_r = _pair_cumsums(gc_v, gr_ref[...], low)
        heads = [slice(h * DN_HEAD_DIM, (h + 1) * DN_HEAD_DIM) for h in range(DN_HEADS)]
        gates = [_pair_gates(gc_v, cum_c, cum_r, low, h) for h in range(DN_HEADS)]
        lmats = []
        for hs, (beta, gam, dm, gl) in zip(heads, gates):
            k = k_ref[:, hs]
            lmats.append(jnp.where(strict, _dot(k * beta, k, NT) * dm, 0.0))
        tinvs = _unit_lower_inverses(lmats)
        for h, (hs, (beta, gam, dm, gl), tinv) in enumerate(zip(heads, gates, tinvs)):
            q, k, v = q_ref[:, hs], k_ref[:, hs], v_ref[:, hs]
            eg = jnp.exp(gam)
            u_ref[:, hs] = _dot(tinv, v * beta)
            w_ref[:, hs] = _dot(tinv, (k * beta) * eg)
            a_ref[h] = _dot(q, k, NT) * dm
            ti_ref[h] = tinv
            qg_ref[:, hs] = q * eg
            kd_ref[:, hs] = k * jnp.exp(gl - gam)
            for c in range(2):
                last = (c + 1) * DN_CHUNK - 1
                dl_ref[c, h] = jnp.broadcast_to(jnp.exp(gam[last:last + 1, :]), (8, 128))

    tok = lambda w: pl.BlockSpec((PAIR, w), lambda n: (n, 0))
    hm = pl.BlockSpec((DN_HEADS, PAIR, PAIR), lambda n: (0, n, 0))
    return pl.pallas_call(
        body, name="dn_intra", grid=(npair,),
        in_specs=[tok(DN_W), tok(DN_W), tok(DN_W), tok(128), pl.BlockSpec((8, PAIR), lambda n: (0, n))],
        out_specs=[tok(DN_W)] * 4 + [hm, hm, pl.BlockSpec((2, DN_HEADS, 8, 128), lambda n: (n, 0, 0, 0))],
        out_shape=[jax.ShapeDtypeStruct((t, DN_W), F32)] * 4 + [jax.ShapeDtypeStruct((DN_HEADS, t, PAIR), F32)] * 2
                  + [jax.ShapeDtypeStruct((2 * npair, DN_HEADS, 8, 128), F32)],
        compiler_params=_params("parallel"),
    )(qn, kn, vc, gc, gr)


def _dn_scan_fwd(u, w, qg, kd, a_qk, dlast, pd, dn_w):
    t = u.shape[0]
    npair = t // PAIR

    def body(u_ref, w_ref, qg_ref, kd_ref, a_ref, dl_ref, z_ref, nw_ref, out_ref, o_ref, vn_ref, sall_ref, s_ref):
        @pl.when(pl.program_id(0) == 0)
        def _():
            s_ref[...] = jnp.zeros_like(s_ref)

        nw = nw_ref[...]
        for c in range(2):
            rows = slice(c * DN_CHUNK, (c + 1) * DN_CHUNK)
            for h in range(DN_HEADS):
                hs = slice(h * DN_HEAD_DIM, (h + 1) * DN_HEAD_DIM)
                st = s_ref[h]
                sall_ref[c, h] = st
                vn_ref[rows, hs] = u_ref[rows, hs] - _dot(w_ref[rows, hs], st)
            for h in range(DN_HEADS):
                hs = slice(h * DN_HEAD_DIM, (h + 1) * DN_HEAD_DIM)
                st, vn = s_ref[h], vn_ref[rows, hs]
                o = _dot(qg_ref[rows, hs], st) + _dot(a_ref[h, rows, rows], vn)
                s_ref[h] = st * dl_ref[c, h][0:1, :] + _dot(kd_ref[rows, hs], vn, TN)
                o_ref[rows, hs] = o
                z = z_ref[rows, hs]
                on = o * lax.rsqrt(jnp.mean(o * o, axis=-1, keepdims=True) + EPS) * nw
                out_ref[rows, hs] = (on * (z * _sigmoid(z))).astype(out_ref.dtype)

    tok = pl.BlockSpec((PAIR, DN_W), lambda n: (n, 0))
    hm = pl.BlockSpec((DN_HEADS, PAIR, PAIR), lambda n: (0, n, 0))
    return pl.pallas_call(
        body, name="dn_scan_fwd", grid=(npair,),
        in_specs=[tok, tok, tok, tok, hm, pl.BlockSpec((2, DN_HEADS, 8, 128), lambda n: (n, 0, 0, 0)),
                  pl.BlockSpec((PAIR, DN_W), lambda n: (n, 3)), _full((1, 128))],
        out_specs=[tok, tok, tok, pl.BlockSpec((2, DN_HEADS, DN_HEAD_DIM, DN_HEAD_DIM), lambda n: (n, 0, 0, 0))],
        out_shape=[jax.ShapeDtypeStruct((t, DN_W), MXU_DTYPE)] + [jax.ShapeDtypeStruct((t, DN_W), F32)] * 2
                  + [jax.ShapeDtypeStruct((2 * npair, DN_HEADS, DN_HEAD_DIM, DN_HEAD_DIM), F32)],
        scratch_shapes=[pltpu.VMEM((DN_HEADS, DN_HEAD_DIM, DN_HEAD_DIM), F32)],
        compiler_params=_params("arbitrary"),
    )(u, w, qg, kd, a_qk, dlast, pd, dn_w)


def _dn_scan_bwd(dout, o, vnew, sall, w, qg, kd, a_qk, dlast, pd, dn_w):
    t = o.shape[0]
    npair = t // PAIR
    rev = lambda n: npair - 1 - n

    def body(do_ref, o_ref, vn_ref, sall_ref, w_ref, qg_ref, kd_ref, a_ref, dl_ref, z_ref, nw_ref,
             dz_ref, du_ref, dw_ref, dqg_ref, dkd_ref, da_ref, ddl_ref, acc_ref, ds_ref, dos_ref):
        @pl.when(pl.program_id(0) == 0)
        def _():
            ds_ref[...] = jnp.zeros_like(ds_ref)
            acc_ref[...] = jnp.zeros_like(acc_ref)

        nw = nw_ref[...]
        dnw = jnp.zeros((1, 128), F32)
        for h in range(DN_HEADS):
            hs = slice(h * DN_HEAD_DIM, (h + 1) * DN_HEAD_DIM)
            o, z, dout = o_ref[:, hs], z_ref[:, hs], do_ref[:, hs]
            r = lax.rsqrt(jnp.mean(o * o, axis=-1, keepdims=True) + EPS)
            oh = o * r
            sz = _sigmoid(z)
            dz_ref[:, hs] = dout * (oh * nw) * (sz + z * sz * (1.0 - sz))
            don = dout * (z * sz)
            dnw += jnp.sum(don * oh, axis=0, keepdims=True)
            doh = don * nw
            dos_ref[:, hs] = r * (doh - oh * jnp.mean(doh * oh, axis=-1, keepdims=True))
        acc_ref[0:1, :] += dnw
        for c in (1, 0):
            rows = slice(c * DN_CHUNK, (c + 1) * DN_CHUNK)
            other = slice((1 - c) * DN_CHUNK, (2 - c) * DN_CHUNK)
            for h in range(DN_HEADS):
                hs = slice(h * DN_HEAD_DIM, (h + 1) * DN_HEAD_DIM)
                do, st, dsp, vn = dos_ref[rows, hs], sall_ref[c, h], ds_ref[h], vn_ref[rows, hs]
                da_ref[h, rows, rows] = _dot(do, vn, NT)
                da_ref[h, rows, other] = jnp.zeros((DN_CHUNK, DN_CHUNK), F32)
                du_ref[rows, hs] = _dot(a_ref[h, rows, rows], do, TN) + _dot(kd_ref[rows, hs], dsp)
                dqg_ref[rows, hs] = _dot(do, st, NT)
                dkd_ref[rows, hs] = _dot(vn, dsp, NT)
                ddl = jnp.sum(jnp.sum(dsp * st, axis=1, keepdims=True), axis=0, keepdims=True)
                ddl_ref[c, h] = jnp.broadcast_to(ddl, (8, 128))
            for h in range(DN_HEADS):
                hs = slice(h * DN_HEAD_DIM, (h + 1) * DN_HEAD_DIM)
                do, st, dvn = dos_ref[rows, hs], sall_ref[c, h], du_ref[rows, hs]
                dw_ref[rows, hs] = -_dot(dvn, st, NT)
                ds_ref[h] = (ds_ref[h] * dl_ref[c, h][0:1, :] + _dot(qg_ref[rows, hs], do, TN)
                             - _dot(w_ref[rows, hs], dvn, TN))

    tok = pl.BlockSpec((PAIR, DN_W), lambda n: (rev(n), 0))
    hm = pl.BlockSpec((DN_HEADS, PAIR, PAIR), lambda n: (0, rev(n), 0))
    sc = pl.BlockSpec((2, DN_HEADS, 8, 128), lambda n: (rev(n), 0, 0, 0))
    return pl.pallas_call(
        body, name="dn_scan_bwd", grid=(npair,),
        in_specs=[tok, tok, tok, pl.BlockSpec((2, DN_HEADS, DN_HEAD_DIM, DN_HEAD_DIM), lambda n: (rev(n), 0, 0, 0)),
                  tok, tok, tok, hm, sc, pl.BlockSpec((PAIR, DN_W), lambda n: (rev(n), 3)), _full((1, 128))],
        out_specs=[tok] * 5 + [hm, sc, _full((8, 128))],
        out_shape=[jax.ShapeDtypeStruct((t, DN_W), F32)] * 5 + [jax.ShapeDtypeStruct((DN_HEADS, t, PAIR), F32),
                   jax.ShapeDtypeStruct((2 * npair, DN_HEADS, 8, 128), F32), jax.ShapeDtypeStruct((8, 128), F32)],
        scratch_shapes=[pltpu.VMEM((DN_HEADS, DN_HEAD_DIM, DN_HEAD_DIM), F32), pltpu.VMEM((PAIR, DN_W), F32)],
        compiler_params=_params("arbitrary"),
    )(dout, o, vnew, sall, w, qg, kd, a_qk, dlast, pd, dn_w)


def _dn_intra_bwd(qn, kn, vc, gc, gr, tinv, a_qk, du, dw, dqg, dkd, da_qk, ddlast, dlast, dep):
    t = qn.shape[0]
    npair = t // PAIR

    def body(q_ref, k_ref, v_ref, gc_ref, gr_ref, ti_ref, a_ref, du_ref, dw_ref, dqg_ref, dkd_ref, da_ref, ddl_ref, dl_ref,
             dep_ref, dq_ref, dk_ref, dv_ref, dg_ref):
        gc_v = gc_ref[...]
        low, strict = _pair_masks()
        cum_c, cum_r = _pair_cumsums(gc_v, gr_ref[...], low)
        lane = lax.broadcasted_iota(jnp.int32, (PAIR, 128), 1)
        rowi = lax.broadcasted_iota(jnp.int32, (PAIR, 1), 0)
        rsum = lambda v: jnp.sum(v, axis=-1, keepdims=True)
        dgam_all = jnp.zeros((PAIR, 128), F32)
        dbeta_all = jnp.zeros((PAIR, 128), F32)
        heads = [slice(h * DN_HEAD_DIM, (h + 1) * DN_HEAD_DIM) for h in range(DN_HEADS)]
        gates = [_pair_gates(gc_v, cum_c, cum_r, low, h) for h in range(DN_HEADS)]
        dtis = []
        for h, (hs, (beta, gam, dm, gl)) in enumerate(zip(heads, gates)):
            k = k_ref[:, hs]
            dtis.append(_dot(du_ref[:, hs], v_ref[:, hs] * beta, NT)
                        + _dot(dw_ref[:, hs], (k * beta) * jnp.exp(gam), NT))
        xs = [_dot(ti_ref[h], dti, TN) for h, dti in enumerate(dtis)]
        dls = [jnp.where(strict, -_dot(x, ti_ref[h], NT), 0.0) for h, x in enumerate(xs)]
        for h, (hs, (beta, gam, dm, gl), dl) in enumerate(zip(heads, gates, dls)):
            q, k, v = q_ref[:, hs], k_ref[:, hs], v_ref[:, hs]
            tinv, a = ti_ref[h], a_ref[h]
            du, dw, dqg, dkd = du_ref[:, hs], dw_ref[:, hs], dqg_ref[:, hs], dkd_ref[:, hs]
            kb = k * beta
            eg = jnp.exp(gam)
            ekd = jnp.exp(gl - gam)
            kbg = kb * eg
            lmat = jnp.where(strict, _dot(kb, k, NT) * dm, 0.0)
            dvb = _dot(tinv, du, TN)
            dkbg = _dot(tinv, dw, TN)
            dmm = dl * dm
            dam = jnp.where(low, da_ref[h], 0.0)
            dn = dam * dm
            e = dl * lmat + dam * a
            dkb = _dot(dmm, k) + dkbg * eg
            dk_ref[:, hs] = _dot(dmm, kb, TN) + _dot(dn, q, TN) + dkd * ekd + dkb * beta
            dq_ref[:, hs] = _dot(dn, k) + dqg * eg
            dv_ref[:, hs] = dvb * beta
            t_kd = rsum(dkd * (k * ekd))
            dgam = rsum(e) - rsum(e.T) + rsum(dqg * (q * eg)) + rsum(dkbg * kbg) - t_kd
            for c in range(2):
                rows = slice(c * DN_CHUNK, (c + 1) * DN_CHUNK)
                dgl = (jnp.sum(t_kd[rows, :], axis=0, keepdims=True)
                       + ddl_ref[c, h][0:1, 0:1] * dl_ref[c, h][0:1, 0:1])
                dgam = dgam + jnp.where(rowi == (c + 1) * DN_CHUNK - 1, dgl, 0.0)
            dgam_all += jnp.where(lane == DN_HEADS + h, dgam, 0.0)
            dbeta_all += jnp.where(lane == h, rsum(dkb * k) + rsum(dvb * v), 0.0)
        dg_ref[...] = dbeta_all + _dot(low.astype(F32), dgam_all, TN, HI)

    tok = lambda w: pl.BlockSpec((PAIR, w), lambda n: (n, 0))
    hm = pl.BlockSpec((DN_HEADS, PAIR, PAIR), lambda n: (0, n, 0))
    sc = pl.BlockSpec((2, DN_HEADS, 8, 128), lambda n: (n, 0, 0, 0))
    return pl.pallas_call(
        body, name="dn_intra_bwd", grid=(npair,),
        in_specs=[tok(DN_W), tok(DN_W), tok(DN_W), tok(128), pl.BlockSpec((8, PAIR), lambda n: (0, n)), hm, hm,
                  tok(DN_W), tok(DN_W), tok(DN_W), tok(DN_W), hm, sc, sc, pl.BlockSpec(memory_space=pl.ANY)],
        out_specs=[tok(DN_W), tok(DN_W), tok(DN_W), tok(128)],
        out_shape=[jax.ShapeDtypeStruct((t, DN_W), F32)] * 3 + [jax.ShapeDtypeStruct((t, 128), F32)],
        compiler_params=_params("parallel"),
    )(qn, kn, vc, gc, gr, tinv, a_qk, du, dw, dqg, dkd, da_qk, ddlast, dlast, dep)


def _dn_prep_bwd(pd, conv_w, al_vec, dt_vec, dqn, dkn, dvc, dgc, tm):
    t = pd.shape[0]

    def body(cur_ref, prev_ref, ba_ref, cw_ref, al_ref, dt_ref, dq_ref, dk_ref, dv_ref, dg_ref,
             dy_ref, dba_ref, accw_ref, accg_ref, xs_ref, dc_ref):
        @pl.when(pl.program_id(0) == 0)
        def _():
            accw_ref[...] = jnp.zeros_like(accw_ref)
            accg_ref[...] = jnp.zeros_like(accg_ref)

        _conv_window(cur_ref, prev_ref, xs_ref, tm)
        y = _conv_taps(xs_ref, cw_ref, tm)
        sg = _sigmoid(y)
        c = y * sg
        for h in range(DN_HEADS):
            qs = slice(h * DN_HEAD_DIM, (h + 1) * DN_HEAD_DIM)
            ksl = slice(DN_W + h * DN_HEAD_DIM, DN_W + (h + 1) * DN_HEAD_DIM)
            for src, sl, scale in ((dq_ref, qs, DN_SCALE), (dk_ref, ksl, 1.0)):
                xh = c[:, sl]
                r = lax.rsqrt(jnp.sum(xh * xh, axis=-1, keepdims=True) + EPS)
                unit = xh * r
                dn = src[:, qs] * scale
                dc_ref[:, sl] = r * (dn - unit * jnp.sum(dn * unit, axis=-1, keepdims=True))
        dc_ref[:, 2 * DN_W:3 * DN_W] = dv_ref[...]
        dy = dc_ref[...] * (sg + y * sg * (1.0 - sg))
        dy_ref[...] = dy
        for j in range(DN_CONV):
            accw_ref[j:j + 1, :] += jnp.sum(dy * xs_ref[pl.ds(HALO - DN_CONV + 1 + j, tm), :], axis=0, keepdims=True)

        beta, g, pre = _gate_values(ba_ref[...], al_ref[...], dt_ref[...])
        dgb = dg_ref[...]
        lane = lax.broadcasted_iota(jnp.int32, dgb.shape, 1)
        is_b, is_a = lane < DN_HEADS, (lane >= DN_HEADS) & (lane < 2 * DN_HEADS)
        dpre = dgb * (-jnp.exp(al_ref[...])) * _sigmoid(pre)
        dba_ref[...] = jnp.where(is_b, dgb * beta * (1.0 - beta), jnp.where(is_a, dpre, 0.0))
        accg_ref[0:1, :] += jnp.sum(jnp.where(is_a, dgb * g, 0.0), axis=0, keepdims=True)
        accg_ref[1:2, :] += jnp.sum(jnp.where(is_a, dpre, 0.0), axis=0, keepdims=True)

    tok = lambda w: pl.BlockSpec((tm, w), lambda i: (i, 0))
    return pl.pallas_call(
        body, name="dn_prep_bwd", grid=(t // tm,),
        in_specs=_dn_prep_specs(tm, t) + [tok(DN_W), tok(DN_W), tok(DN_W), tok(128)],
        out_specs=[tok(CONV_CH), tok(128), _full((8, CONV_CH)), _full((8, 128))],
        out_shape=[jax.ShapeDtypeStruct((t, CONV_CH), F32), jax.ShapeDtypeStruct((t, 128), F32),
                   jax.ShapeDtypeStruct((8, CONV_CH), F32), jax.ShapeDtypeStruct((8, 128), F32)],
        scratch_shapes=[pltpu.VMEM((HALO + tm, CONV_CH), F32), pltpu.VMEM((tm, CONV_CH), F32)],
        compiler_params=_params("arbitrary"),
    )(pd, pd, pd, conv_w, al_vec, dt_vec, dqn, dkn, dvc, dgc)


def _dn_conv_bwd(dy, dz, dba, conv_w, tm):
    t = dy.shape[0]
    nt = t // tm

    def body(cur_ref, nxt_ref, dz_ref, dba_ref, cw_ref, o_ref, ds_ref):
        nxt = jnp.where(pl.program_id(0) < nt - 1, nxt_ref[...], 0.0)
        ds_ref[0:tm, :] = cur_ref[...]
        ds_ref[tm:tm + HALO, :] = nxt
        dx = cw_ref[0:1, :] * ds_ref[pl.ds(DN_CONV - 1, tm), :]
        for j in range(1, DN_CONV):
            dx += cw_ref[j:j + 1, :] * ds_ref[pl.ds(DN_CONV - 1 - j, tm), :]
        o_ref[:, 0:CONV_CH] = dx.astype(o_ref.dtype)
        o_ref[:, CONV_CH:CONV_CH + DN_W] = dz_ref[...].astype(o_ref.dtype)
        o_ref[:, CONV_CH + DN_W:DN_COLS] = dba_ref[...].astype(o_ref.dtype)

    tok = lambda w: pl.BlockSpec((tm, w), lambda i: (i, 0))
    return pl.pallas_call(
        body, name="dn_conv_bwd", grid=(nt,),
        in_specs=[tok(CONV_CH),
                  pl.BlockSpec((HALO, CONV_CH), lambda i: (jnp.minimum((i + 1) * (tm // HALO), t // HALO - 1), 0)),
                  tok(DN_W), tok(128), _full((DN_CONV, CONV_CH))],
        out_specs=tok(DN_COLS),
        out_shape=jax.ShapeDtypeStruct((t, DN_COLS), MXU_DTYPE),
        scratch_shapes=[pltpu.VMEM((tm + HALO, CONV_CH), F32)],
        compiler_params=_params("parallel"),
    )(dy, dy, dz, dba, conv_w)


def _pad_lanes(v, offset=0):
    return jnp.zeros((1, 128), F32).at[0, offset:offset + v.shape[0]].set(v.astype(F32))


class _LocalReducer:
    def start(self, grads):
        return jnp.zeros((8, 128), F32)

    def middle(self, after):
        return jnp.zeros((8, 128), F32)

    def finish(self, after):
        return None


def _local_step(x, p, tgt, sm, w, late, reducer):
    t = x.shape[0]
    tm = min(512, t // 2)
    tm_s = min(256, t // 2)
    tw = min(1024, t // 2)

    w_in = w["w_in"]
    wa = w_in[:, :ATTN_Q + 2 * ATTN_KV]
    wd = jnp.pad(w_in[:, ATTN_Q + 2 * ATTN_KV:], ((0, 0), (0, DN_COLS - (D_IN - ATTN_Q - 2 * ATTN_KV))))
    conv_w = w["conv_w"]
    al_vec, dt_vec = _pad_lanes(sm["a_log"], DN_HEADS), _pad_lanes(sm["dt_bias"], DN_HEADS)
    sinks_vec = _pad_lanes(sm["sinks"])
    dn_w = sm["dn_norm"].reshape(1, 128)
    row = lambda v: v.reshape(1, D_MODEL)
    cos, sin = _rope_tables(t)

    u, pa, pd = _inproj(x, row(sm["norm_mix"]), wa, wd, tm_s)
    ao = _attn_fwd(pa, cos, sin, sinks_vec)
    qn, kn, vc, gc, gr = _dn_prep(pd, conv_w, al_vec, dt_vec, tm_s)
    uu, ww, qg, kd, a_qk, tinv, dlast = _dn_intra(qn, kn, vc, gc, gr)
    dn_out, o, vnew, sall = _dn_scan_fwd(uu, ww, qg, kd, a_qk, dlast, pd, dn_w)
    w = dict(w, **late(dn_out))
    wo_a, wo_d = w["w_o"][:ATTN_Q], w["w_o"][ATTN_Q:]
    w_proj = jnp.transpose(w["w_proj4"], (1, 0, 2)).reshape(PLE_DIM, D_MODEL)
    h1 = _oproj(x, ao, dn_out, wo_a, wo_d, tm)
    m, r, h2 = _mlp_fwd(h1, row(sm["norm_mlp"]), w["w_up4"], w["w_down"], tw)
    dh2, dh2b, dgp, dpp, n3, pb, acc_ple = _ple_loss(h2, p, tgt, row(sm["norm_ple"]), row(sm["norm_final"]),
                                                     w["w_gate"], w_proj, tm_s)
    g_w_gate = _wgrad(n3, dgp, "wgrad_gate", D_MODEL, D_MODEL, tw)
    g_w_proj = _wgrad(pb, dpp, "wgrad_proj", PLE_DIM, D_MODEL, tw)
    da, dh1, dh1b, acc_mlp = _mlp_bwd(dh2, dh2b, r, h1, row(sm["norm_mlp"]), w["w_up4"], w["w_down"], tm)
    g_w_up4 = _wgrad(m, da, "wgrad_up", D_MODEL, FF_BLOCK, tw, stacked=True)
    g_w_down = _wgrad(r, dh2b, "wgrad_down", FF_BLOCK, D_MODEL, tw,
                      prep=lambda rv: jnp.square(rv.astype(F32)).astype(MXU_DTYPE))
    g_w_o = jnp.concatenate([_wgrad(ao, dh1b, "wgrad_oa", ATTN_Q, D_MODEL, tw),
                             _wgrad(dn_out, dh1b, "wgrad_od", DN_W, D_MODEL, tw)], axis=0)
    early = dict(w_up4=g_w_up4, w_down=g_w_down, w_gate=g_w_gate, w_proj=g_w_proj, w_o=g_w_o)
    dep = reducer.start(early)
    dao, ddn = _oproj_bwd(dh1b, wo_a, wo_d, tm, dep)
    dz, du, dw, dqg, dkd, da_qk, ddlast, acc_dn = _dn_scan_bwd(ddn, o, vnew, sall, ww, qg, kd, a_qk, dlast, pd, dn_w)
    dep = reducer.middle(du)
    dqn, dkn, dvc, dgc = _dn_intra_bwd(qn, kn, vc, gc, gr, tinv, a_qk, du, dw, dqg, dkd, da_qk, ddlast, dlast, dep)
    dy, dba, acc_conv, acc_gate = _dn_prep_bwd(pd, conv_w, al_vec, dt_vec, dqn, dkn, dvc, dgc, tm_s)
    d_dn = _dn_conv_bwd(dy, dz, dba, conv_w, tm_s)
    dq, dk, dv, acc_attn = _attn_bwd(pa, cos, sin, sinks_vec, dao)
    reducer.finish(dq)
    wq, wk, wv = wa[:, :ATTN_Q], wa[:, ATTN_Q:ATTN_Q + ATTN_KV], wa[:, ATTN_Q + ATTN_KV:]
    dx, acc_mix = _inproj_bwd(x, dh1, row(sm["norm_mix"]), [dq, dk, dv, d_dn], [wq, wk, wv, wd], tm_s)

    g_w_in = jnp.concatenate([
        _wgrad(u, dq, "wgrad_q", D_MODEL, ATTN_Q, tw), _wgrad(u, dk, "wgrad_k", D_MODEL, ATTN_KV, tw),
        _wgrad(u, dv, "wgrad_v", D_MODEL, ATTN_KV, tw),
        _wgrad(u, d_dn, "wgrad_dn", D_MODEL, DN_COLS, tw)[:, :D_IN - ATTN_Q - 2 * ATTN_KV]], axis=1)
    grads = dict(early, w_in=g_w_in)
    sums = dict(loss=acc_ple[2, 0], norm_final=acc_ple[0], norm_ple=acc_ple[1], norm_mlp=acc_mlp[0], norm_mix=acc_mix[0],
                dn_norm=acc_dn[0], sinks=acc_attn[0, :ATTN_HEADS], a_log=acc_gate[0, DN_HEADS:2 * DN_HEADS],
                dt_bias=acc_gate[1, DN_HEADS:2 * DN_HEADS], conv_w=acc_conv[:DN_CONV])
    return sums, dx, grads


MESH = pl.DeviceIdType.MESH
ANY = pl.BlockSpec(memory_space=pl.ANY)
N_CHIPS = 4
N_DEV = 8


def _place():
    x, y, c = lax.axis_index("x"), lax.axis_index("y"), lax.axis_index("c")
    chips = [(1 - x, y), (x, 1 - y), (1 - x, 1 - y)]
    return x, y, c, chips


def _gather_weights(shards, conv_s):
    n = len(shards)
    per = 7

    def body(*refs):
        in_refs, conv_ref = refs[:n], refs[n]
        out_refs, conv_out = refs[n + 1:2 * n + 1], refs[2 * n + 1]
        send_sems, recv_sems = refs[2 * n + 2:]
        x, y, c, chips = _place()
        sibling = (x, y, 1 - c)

        def blk(a, px, py, pc):
            hr = in_refs[a].shape[0] // 2
            return out_refs[a].at[2 * px + py, pl.ds(pc * hr, hr), :]

        def mine(a):
            hr = in_refs[a].shape[0] // 2
            return in_refs[a].at[pl.ds(c * hr, hr), :]

        def rcopy(a, k, block, to, src=None):
            return pltpu.make_async_remote_copy(
                src_ref=blk(a, *block) if src is None else src, dst_ref=blk(a, *block),
                send_sem=send_sems.at[per * a + k], recv_sem=recv_sems.at[per * a + k],
                device_id=to, device_id_type=MESH)

        def whole(a, to):
            return pltpu.make_async_remote_copy(
                src_ref=in_refs[a], dst_ref=out_refs[a].at[2 * x + y],
                send_sem=send_sems.at[per * a], recv_sem=recv_sems.at[per * a], device_id=to, device_id_type=MESH)

        def ccopy(j, to):
            return pltpu.make_async_remote_copy(
                src_ref=conv_ref, dst_ref=conv_out.at[2 * x + y],
                send_sem=send_sems.at[per * n + j], recv_sem=recv_sems.at[per * n + j],
                device_id=to, device_id_type=MESH)

        started = []
        for a in range(n):
            first = [whole(a, sibling)]
            first += [rcopy(a, 1 + j, (x, y, c), (*chip, c), src=mine(a)) for j, chip in enumerate(chips)]
            for cp in first:
                cp.start()
            started += first
        conv_sends = [ccopy(j, (*chip, c)) for j, chip in enumerate(chips)] + [ccopy(3, sibling)]
        for cp in conv_sends:
            cp.start()
        started += conv_sends
        for a in range(n):
            for j, chip in enumerate(chips):
                rcopy(a, 1 + j, (*chip, c), (x, y, c)).wait_recv()
                fwd = rcopy(a, 4 + j, (*chip, c), sibling)
                fwd.start()
                started.append(fwd)
        for a in range(n):
            whole(a, sibling).wait_recv()
            for j, chip in enumerate(chips):
                rcopy(a, 4 + j, (*chip, 1 - c), (x, y, c)).wait_recv()
        for j, chip in enumerate(chips + [(x, y)]):
            pltpu.make_async_remote_copy(
                src_ref=conv_ref, dst_ref=conv_out.at[2 * chip[0] + chip[1]],
                send_sem=send_sems.at[per * n + j], recv_sem=recv_sems.at[per * n + j],
                device_id=sibling, device_id_type=MESH).wait_recv()
        for cp in started:
            cp.wait_send()

    nsem = per * n + 4
    out_shape = [jax.ShapeDtypeStruct((N_CHIPS,) + s.shape, s.dtype) for s in shards]
    out_shape.append(jax.ShapeDtypeStruct((N_CHIPS,) + conv_s.shape, conv_s.dtype))
    return pl.pallas_call(
        body, name="gather_weights", in_specs=[ANY] * (n + 1), out_specs=[ANY] * (n + 1), out_shape=out_shape,
        scratch_shapes=[pltpu.SemaphoreType.DMA((nsem,)), pltpu.SemaphoreType.DMA((nsem,))],
    )(*shards, conv_s)


HBM = pl.BlockSpec(memory_space=pltpu.HBM)
SEM = pl.BlockSpec(memory_space=pltpu.SEMAPHORE)
EFFECT = pltpu.SideEffectType.DATAFLOW_SIDE_EFFECTING
LATE_COPIES = 7


def _late_copies(in_refs, land_refs, send_sems, recv_sems):
    x, y, c, chips = _place()
    sends, arrivals = [], []
    for a, (src, land) in enumerate(zip(in_refs, land_refs)):
        hr = src.shape[0] // 2
        base = LATE_COPIES * a

        def cp(src_ref, dst_ref, s_idx, r_idx, to):
            return pltpu.make_async_remote_copy(src_ref=src_ref, dst_ref=dst_ref, send_sem=send_sems.at[base + s_idx],
                                                recv_sem=recv_sems.at[base + r_idx], device_id=to, device_id_type=MESH)

        sends.append(cp(src, land.at[2 * x + y], 0, 0, (x, y, 1 - c)))
        arrivals.append(cp(src, land.at[2 * x + y], 0, 0, (x, y, 1 - c)))
        for j, chip in enumerate(chips):
            for pc in range(2):
                half = src.at[pl.ds(c * hr, hr), :]
                sends.append(cp(half, land.at[2 * x + y, pl.ds(c * hr, hr), :], 1 + 2 * j + pc, 1 + 2 * j + c, (*chip, pc)))
                arrivals.append(cp(half, land.at[2 * chip[0] + chip[1], pl.ds(pc * hr, hr), :], 1 + 2 * j + pc,
                                   1 + 2 * j + pc, (*chip, pc)))
    return sends, arrivals


def _copies_start(name, build, nsem, srcs, land_shapes, after):
    n = len(srcs)

    def body(*refs):
        sends, _ = build(refs[:n], refs[n:2 * n], refs[2 * n + 1], refs[2 * n + 2])
        for cp in sends:
            cp.start()
        refs[-1][...] = jnp.zeros_like(refs[-1])

    lands = [pltpu.with_memory_space_constraint(lax.empty(s.shape, s.dtype), pltpu.HBM) for s in land_shapes]
    ins = [pltpu.with_memory_space_constraint(s, pltpu.HBM) for s in srcs]
    out = pl.pallas_call(
        body, name=name,
        out_shape=(pltpu.SemaphoreType.DMA((nsem,)), pltpu.SemaphoreType.DMA((nsem,)),
                   *[pltpu.HBM(s.shape, s.dtype) for s in srcs], *[pltpu.HBM(s.shape, s.dtype) for s in land_shapes],
                   jax.ShapeDtypeStruct((8, 128), F32)),
        in_specs=[HBM] * (2 * n) + [ANY],
        out_specs=(SEM, SEM, *[HBM] * (2 * n), pl.BlockSpec(memory_space=pltpu.VMEM)),
        input_output_aliases={i: 2 + i for i in range(2 * n)},
        compiler_params=pltpu.CompilerParams(has_side_effects=EFFECT),
    )(*ins, *lands, after)
    return out[0], out[1], out[2:2 + n], out[2 + n:2 + 2 * n], out[-1]


def _copies_wait(name, build, started, after):
    send_sems, recv_sems, srcs, lands, _ = started
    n = len(srcs)

    def body(*refs):
        sends, arrivals = build(refs[:n], refs[n:2 * n], refs[2 * n], refs[2 * n + 1])
        for cp in sends:
            cp.wait_send()
        for cp in arrivals:
            cp.wait_recv()

    out = pl.pallas_call(
        body, name=name,
        out_shape=(*[pltpu.HBM(s.shape, s.dtype) for s in srcs], *[pltpu.HBM(l.shape, l.dtype) for l in lands]),
        in_specs=[HBM] * (2 * n) + [SEM, SEM, ANY],
        out_specs=tuple([HBM] * (2 * n)),
        input_output_aliases={i: i for i in range(2 * n)},
        compiler_params=pltpu.CompilerParams(has_side_effects=EFFECT),
    )(*srcs, *lands, send_sems, recv_sems, after)
    return out[:n], out[n:]


def _exchange_copies(g_refs, got_refs, send_sems, recv_sems):
    x, y, c, _ = _place()
    sends, arrivals = [], []
    for a, (g, got) in enumerate(zip(g_refs, got_refs)):
        hr = g.shape[1] // 2
        cp = pltpu.make_async_remote_copy(
            src_ref=g.at[:, pl.ds((1 - c) * hr, hr), :], dst_ref=got, send_sem=send_sems.at[a],
            recv_sem=recv_sems.at[a], device_id=(x, y, 1 - c), device_id_type=MESH)
        sends.append(cp)
        arrivals.append(cp)
    return sends, arrivals


def _scatter_copies(s_refs, got_refs, send_sems, recv_sems):
    x, y, c, chips = _place()
    sends, arrivals = [], []
    for a, (s16, got) in enumerate(zip(s_refs, got_refs)):
        for j, chip in enumerate(chips):
            cp = pltpu.make_async_remote_copy(
                src_ref=s16.at[2 * chip[0] + chip[1]], dst_ref=got.at[j], send_sem=send_sems.at[3 * a + j],
                recv_sem=recv_sems.at[3 * a + j], device_id=(*chip, c), device_id_type=MESH)
            sends.append(cp)
            arrivals.append(cp)
    return sends, arrivals


def _exchange_halves(grads):
    n = len(grads)

    def body(*refs):
        g_refs, got_refs = refs[:n], refs[n:2 * n]
        send_sems, recv_sems = refs[2 * n:]
        x, y, c, _ = _place()
        remote = []
        for a in range(n):
            hr = g_refs[a].shape[1] // 2
            remote.append(pltpu.make_async_remote_copy(
                src_ref=g_refs[a].at[:, pl.ds((1 - c) * hr, hr), :], dst_ref=got_refs[a],
                send_sem=send_sems.at[a], recv_sem=recv_sems.at[a], device_id=(x, y, 1 - c), device_id_type=MESH))
        for cp in remote:
            cp.start()
        for cp in remote:
            cp.wait_recv()
        for cp in remote:
            cp.wait_send()

    half = [jax.ShapeDtypeStruct((g.shape[0], g.shape[1] // 2, g.shape[2]), g.dtype) for g in grads]
    return pl.pallas_call(
        body, name="exchange_halves", in_specs=[ANY] * n, out_specs=[ANY] * n, out_shape=half,
        scratch_shapes=[pltpu.SemaphoreType.DMA((n,)), pltpu.SemaphoreType.DMA((n,))],
    )(*grads)


def _scatter_to_chips(sums16):
    n = len(sums16)

    def body(*refs):
        s16, got_refs = refs[:n], refs[n:2 * n]
        send_sems, recv_sems = refs[2 * n:]
        x, y, c, chips = _place()
        remote = []
        for a in range(n):
            for j, chip in enumerate(chips):
                remote.append(pltpu.make_async_remote_copy(
                    src_ref=s16[a].at[2 * chip[0] + chip[1]], dst_ref=got_refs[a].at[j],
                    send_sem=send_sems.at[3 * a + j], recv_sem=recv_sems.at[3 * a + j],
                    device_id=(*chip, c), device_id_type=MESH))
        for cp in remote:
            cp.start()
        for cp in remote:
            cp.wait_recv()
        for cp in remote:
            cp.wait_send()

    got = [jax.ShapeDtypeStruct((3,) + s.shape[1:], BF16) for s in sums16]
    return pl.pallas_call(
        body, name="scatter_to_chips", in_specs=[ANY] * n, out_specs=[ANY] * n, out_shape=got,
        scratch_shapes=[pltpu.SemaphoreType.DMA((3 * n,)), pltpu.SemaphoreType.DMA((3 * n,))],
    )(*sums16)


def _share_halves(bufs):
    n = len(bufs)

    def body(*refs):
        out_refs = refs[n:2 * n]
        send_sems, recv_sems = refs[2 * n:]
        x, y, c, _ = _place()
        remote = [pltpu.make_async_remote_copy(
            src_ref=out_refs[a].at[c], dst_ref=out_refs[a].at[c], send_sem=send_sems.at[a], recv_sem=recv_sems.at[a],
            device_id=(x, y, 1 - c), device_id_type=MESH) for a in range(n)]
        for cp in remote:
            cp.start()
        for a in range(n):
            pltpu.make_async_remote_copy(
                src_ref=out_refs[a].at[c], dst_ref=out_refs[a].at[1 - c], send_sem=send_sems.at[a],
                recv_sem=recv_sems.at[a], device_id=(x, y, 1 - c), device_id_type=MESH).wait_recv()
        for cp in remote:
            cp.wait_send()

    return pl.pallas_call(
        body, name="share_halves", in_specs=[ANY] * n, out_specs=[ANY] * n,
        out_shape=[jax.ShapeDtypeStruct(b.shape, b.dtype) for b in bufs],
        input_output_aliases={a: a for a in range(n)},
        scratch_shapes=[pltpu.SemaphoreType.DMA((n,)), pltpu.SemaphoreType.DMA((n,))],
    )(*bufs)


SMALL_ROWS, SMALL_COLS = 16, CONV_CH


def _allreduce_small(block):
    m_per, ncol = block.shape

    def body(x_ref, sum_ref, all_ref, send_sems, recv_sems, local_sem):
        x, y, c, chips = _place()
        me, sibling = (x, y, c), (x, y, 1 - c)

        def rows(px, py, pc):
            return all_ref.at[pl.ds((4 * px + 2 * py + pc) * m_per, m_per), :]

        def copy(k, block_of, to, src=None):
            return pltpu.make_async_remote_copy(
                src_ref=rows(*block_of) if src is None else src, dst_ref=rows(*block_of),
                send_sem=send_sems.at[k], recv_sem=recv_sems.at[k], device_id=to, device_id_type=MESH)

        mine = pltpu.make_async_copy(x_ref, rows(*me), local_sem)
        mine.start()
        first = [copy(0, me, sibling, src=x_ref)]
        first += [copy(1 + j, me, (*chip, c), src=x_ref) for j, chip in enumerate(chips)]
        for cp in first:
            cp.start()
        passed = [copy(4 + j, (*chip, c), sibling) for j, chip in enumerate(chips)]
        for j, chip in enumerate(chips):
            copy(1 + j, (*chip, c), me).wait_recv()
            passed[j].start()
        copy(0, sibling, me).wait_recv()
        for j, chip in enumerate(chips):
            copy(4 + j, (*chip, 1 - c), me).wait_recv()
        for cp in first + passed:
            cp.wait_send()
        mine.wait()
        total = all_ref[0:m_per, :]
        for d in range(1, N_DEV):
            total = total + all_ref[d * m_per:(d + 1) * m_per, :]
        sum_ref[...] = total

    vm = pl.BlockSpec(memory_space=pltpu.VMEM)
    return pl.pallas_call(
        body, name="allreduce_small", in_specs=[vm], out_specs=vm,
        out_shape=jax.ShapeDtypeStruct((m_per, ncol), F32),
        scratch_shapes=[pltpu.VMEM((N_DEV * m_per, ncol), F32), pltpu.SemaphoreType.DMA((7,)),
                        pltpu.SemaphoreType.DMA((7,)), pltpu.SemaphoreType.DMA],
    )(block)


def _row_tile(rows, cols):
    tile = rows
    while tile * cols * 4 > (1 << 20) and tile % 16 == 0:
        tile //= 2
    return tile


def _elementwise(fn, name, ins, out_dtypes):
    rows, cols = ins[0].shape
    tile = _row_tile(rows, cols)

    def body(*refs):
        outs = fn(*[r[...] for r in refs[:len(ins)]])
        for o_ref, o in zip(refs[len(ins):], outs):
            o_ref[...] = o.astype(o_ref.dtype)

    spec = pl.BlockSpec((tile, cols), lambda i: (i, 0))
    return pl.pallas_call(
        body, name=name, grid=(rows // tile,), in_specs=[spec] * len(ins), out_specs=[spec] * len(out_dtypes),
        out_shape=[jax.ShapeDtypeStruct((rows, cols), d) for d in out_dtypes],
        compiler_params=_params("parallel"),
    )(*ins)


def _adamw_tile(w, g, m, v):
    m = ADAM_B1 * m + (1.0 - ADAM_B1) * g
    v = ADAM_B2 * v + (1.0 - ADAM_B2) * jnp.square(g)
    m_hat = m / (1.0 - ADAM_B1 ** ADAM_STEP)
    v_hat = v / (1.0 - ADAM_B2 ** ADAM_STEP)
    delta = -ADAM_LR * (m_hat / (jnp.sqrt(v_hat) + ADAM_EPS) + ADAM_WD * w)
    return delta, m, v


def _adamw(name, w, g, m, v):
    return _elementwise(_adamw_tile, name, [w, g, m, v], [F32, F32, F32])


def _chip_sum(name, g4, got, place):
    nchip, hr, cols = got.shape
    tile = _row_tile(hr, cols)
    nblk = hr // tile

    def body(pl_ref, g_ref, o_ref, s32_ref, s16_ref):
        s = g_ref[...] + o_ref[...]
        s32_ref[...] = s
        s16_ref[...] = s.astype(BF16)

    spec = pl.BlockSpec((None, tile, cols), lambda k, i, pr: (k, i, 0))
    return pl.pallas_call(
        body, name=name,
        grid_spec=pltpu.PrefetchScalarGridSpec(
            num_scalar_prefetch=1, grid=(nchip, nblk),
            in_specs=[pl.BlockSpec((None, tile, cols), lambda k, i, pr: (k, pr[1] * nblk + i, 0)), spec],
            out_specs=[spec, spec]),
        out_shape=[jax.ShapeDtypeStruct(got.shape, F32), jax.ShapeDtypeStruct(got.shape, BF16)],
        compiler_params=_params("parallel", "parallel"),
    )(place, g4, got)


def _mesh_sum(name, s32, got, place):
    _, hr, cols = s32.shape
    tile = _row_tile(hr, cols)

    def body(pl_ref, own_ref, g0_ref, g1_ref, g2_ref, o_ref):
        o_ref[...] = ((own_ref[...] + g0_ref[...].astype(F32)) + g1_ref[...].astype(F32)) + g2_ref[...].astype(F32)

    slab = lambda j: pl.BlockSpec((None, tile, cols), lambda i, pr: (j, i, 0))
    return pl.pallas_call(
        body, name=name,
        grid_spec=pltpu.PrefetchScalarGridSpec(
            num_scalar_prefetch=1, grid=(hr // tile,),
            in_specs=[pl.BlockSpec((None, tile, cols), lambda i, pr: (pr[0], i, 0)), slab(0), slab(1), slab(2)],
            out_specs=pl.BlockSpec((None, tile, cols), lambda i, pr: (pr[1], i, 0))),
        out_shape=jax.ShapeDtypeStruct((2, hr, cols), F32),
        compiler_params=_params("parallel"),
    )(place, s32, got, got, got)


def _reduce_scatter(grads):
    names = list(grads)
    place = _place_operand()
    got_a = _exchange_halves([grads[k] for k in names])
    sums = [_chip_sum("chip_sum_" + k, grads[k], g, place) for k, g in zip(names, got_a)]
    got_b = _scatter_to_chips([s[1] for s in sums])
    return {k: _mesh_sum("mesh_sum_" + k, s[0], g, place) for k, s, g in zip(names, sums, got_b)}


def _place_operand():
    return jnp.stack([2 * lax.axis_index("x") + lax.axis_index("y"), lax.axis_index("c")]).astype(jnp.int32)


def _per_chip(name, g):
    if name == "w_in":
        return jnp.transpose(g.reshape(D_MODEL, N_CHIPS, D_IN // N_CHIPS), (1, 0, 2))
    if name == "w_proj":
        return jnp.transpose(g.reshape(PLE_DIM, N_CHIPS, D_MODEL // N_CHIPS), (1, 0, 2))
    if name == "w_up4":
        return g
    return g.reshape(N_CHIPS, g.shape[0] // N_CHIPS, g.shape[1])


class _EarlyReducer:
    def start(self, grads):
        self.names = list(grads)
        self.place = _place_operand()
        slabs = [_per_chip(k, grads[k]) for k in self.names]
        halves = [jax.ShapeDtypeStruct((s.shape[0], s.shape[1] // 2, s.shape[2]), F32) for s in slabs]
        self.a = _copies_start("exchange_start", _exchange_copies, len(slabs), slabs, halves, slabs[0])
        return self.a[-1]

    def middle(self, after):
        slabs, got = _copies_wait("exchange_wait", _exchange_copies, self.a, after)
        self.sums = [_chip_sum("early_chip_sum_" + k, s, g, self.place) for k, s, g in zip(self.names, slabs, got)]
        s16 = [s[1] for s in self.sums]
        lands = [jax.ShapeDtypeStruct((3,) + s.shape[1:], BF16) for s in s16]
        self.b = _copies_start("scatter_start", _scatter_copies, 3 * len(s16), s16, lands, s16[0])
        return self.b[-1]

    def finish(self, after):
        _, got = _copies_wait("scatter_wait", _scatter_copies, self.b, after)
        self.bufs = {k: _mesh_sum("early_mesh_sum_" + k, s[0], g, self.place)
                     for k, s, g in zip(self.names, self.sums, got)}


def kernel(x, p, norm_mix, w_in, conv_w, a_log, dt_bias, dn_norm, sinks, w_o, norm_mlp, w_up, w_down, norm_ple, w_ple_gate, w_ple_proj, norm_final, loss_target, m_norm_mix, m_w_in, m_conv_w, m_a_log, m_dt_bias, m_dn_norm, m_sinks, m_w_o, m_norm_mlp, m_w_up, m_w_down, m_norm_ple, m_w_ple_gate, m_w_ple_proj, m_norm_final, v_norm_mix, v_w_in, v_conv_w, v_a_log, v_dt_bias, v_dn_norm, v_sinks, v_w_o, v_norm_mlp, v_w_up, v_w_down, v_norm_ple, v_w_ple_gate, v_w_ple_proj, v_norm_final):
    chip = 2 * lax.axis_index("x") + lax.axis_index("y")
    big = dict(w_in=w_in[0], w_o=w_o[0], w_up=w_up[0], w_down=w_down[0], w_gate=w_ple_gate[0], w_proj=w_ple_proj[0])
    big_m = dict(w_in=m_w_in[0], w_o=m_w_o[0], w_up=m_w_up[0], w_down=m_w_down[0], w_gate=m_w_ple_gate[0], w_proj=m_w_ple_proj[0])
    big_v = dict(w_in=v_w_in[0], w_o=v_w_o[0], w_up=v_w_up[0], w_down=v_w_down[0], w_gate=v_w_ple_gate[0], w_proj=v_w_ple_proj[0])
    names = list(big)

    w_in_all, conv_all = _gather_weights([big["w_in"].astype(BF16)], conv_w[0])
    late_names = names[1:]
    late_shards = [big[k].astype(BF16) for k in late_names]
    gather = _copies_start("gather_start", _late_copies, LATE_COPIES * len(late_shards), late_shards,
                           [jax.ShapeDtypeStruct((N_CHIPS,) + s.shape, BF16) for s in late_shards], w_in_all)
    token = gather[-1]
    w = dict(w_in=jnp.transpose(w_in_all, (1, 0, 2)).reshape(D_MODEL, D_IN),
             conv_w=jnp.transpose(conv_all, (1, 0, 2)).reshape(DN_CONV, CONV_CH))
    sm = dict(norm_mix=norm_mix[0] + token[0, 0], a_log=a_log[0], dt_bias=dt_bias[0], dn_norm=dn_norm[0],
              sinks=sinks[0], norm_mlp=norm_mlp[0], norm_ple=norm_ple[0], norm_final=norm_final)

    def late(after):
        gw = dict(zip(late_names, _copies_wait("gather_wait", _late_copies, gather, after)[1]))
        return dict(w_o=gw["w_o"].reshape(D_MODEL, D_MODEL), w_up4=gw["w_up"], w_down=gw["w_down"].reshape(D_FF, D_MODEL),
                    w_gate=gw["w_gate"].reshape(D_MODEL, D_MODEL), w_proj4=gw["w_proj"])

    reducer = _EarlyReducer()
    sums, grad_x, g = _local_step(x[0], p[0, 0], loss_target[0], sm, w, late, reducer)

    bufs = dict(reducer.bufs, **_reduce_scatter({"w_in": _per_chip("w_in", g["w_in"])}))
    grad_key = dict(w_in="w_in", w_o="w_o", w_up="w_up4", w_down="w_down", w_gate="w_gate", w_proj="w_proj")
    full = _share_halves([bufs[grad_key[k]] for k in names])
    red = {k: f.reshape(-1, f.shape[-1]) for k, f in zip(names, full)}

    row = lambda v: jnp.zeros((SMALL_COLS,), F32).at[:v.shape[0]].set(v)
    misc = jnp.zeros((SMALL_COLS,), F32).at[0:4].set(sums["a_log"]).at[4:8].set(sums["dt_bias"]) \
        .at[8:16].set(sums["sinks"]).at[128:256].set(sums["dn_norm"]).at[256].set(sums["loss"])
    small = jnp.concatenate([sums["conv_w"], jnp.stack([row(sums["norm_mix"]), row(sums["norm_mlp"]), row(sums["norm_ple"]),
                                                        row(sums["norm_final"]), misc]),
                             jnp.zeros((SMALL_ROWS - 9, SMALL_COLS), F32)], axis=0)
    tot = _allreduce_small(small)
    loss = tot[8, 256]
    ncw = CONV_CH // N_CHIPS

    def pack(cw, nmix, nmlp, nple, nfin, al, dtb, sk, dnn):
        misc_p = jnp.zeros((SMALL_COLS,), F32).at[0:4].set(al).at[4:8].set(dtb).at[8:16].set(sk).at[128:256].set(dnn)
        cw_p = jnp.zeros((DN_CONV, SMALL_COLS), F32).at[:, :ncw].set(cw)
        return jnp.concatenate([cw_p, jnp.stack([row(nmix), row(nmlp), row(nple), row(nfin), misc_p]),
                                jnp.zeros((SMALL_ROWS - 9, SMALL_COLS), F32)], axis=0)

    def unpack(buf):
        return dict(conv_w=buf[0:4, :ncw][None], norm_mix=buf[4, :D_MODEL][None], norm_mlp=buf[5, :D_MODEL][None],
                    norm_ple=buf[6, :D_MODEL][None], norm_final=buf[7, :D_MODEL], a_log=buf[8, 0:4][None],
                    dt_bias=buf[8, 4:8][None], sinks=buf[8, 8:16][None], dn_norm=buf[8, 128:256][None])

    g_conv_shard = lax.dynamic_slice(tot[0:4], (0, chip * ncw), (DN_CONV, ncw))
    g_small = pack(g_conv_shard, tot[4, :D_MODEL], tot[5, :D_MODEL], tot[6, :D_MODEL], tot[7, :D_MODEL],
                   tot[8, 0:4], tot[8, 4:8], tot[8, 8:16], tot[8, 128:256])
    w_small = pack(conv_w[0], norm_mix[0], norm_mlp[0], norm_ple[0], norm_final, a_log[0], dt_bias[0], sinks[0], dn_norm[0])
    m_small = pack(m_conv_w[0], m_norm_mix[0], m_norm_mlp[0], m_norm_ple[0], m_norm_final, m_a_log[0], m_dt_bias[0],
                   m_sinks[0], m_dn_norm[0])
    v_small = pack(v_conv_w[0], v_norm_mix[0], v_norm_mlp[0], v_norm_ple[0], v_norm_final, v_a_log[0], v_dt_bias[0],
                   v_sinks[0], v_dn_norm[0])

    d_s, m_s, v_s = (unpack(b) for b in _adamw("adamw_small", w_small, g_small, m_small, v_small))
    g_s = unpack(g_small)
    out_g, out_d, out_m, out_v = dict(g_s), dict(d_s), dict(m_s), dict(v_s)
    ref_name = dict(w_in="w_in", w_o="w_o", w_up="w_up", w_down="w_down", w_gate="w_ple_gate", w_proj="w_ple_proj")
    for k in names:
        d_k, m_k, v_k = _adamw("adamw_" + k, big[k], red[k], big_m[k], big_v[k])
        out_g[ref_name[k]], out_d[ref_name[k]] = red[k][None], d_k[None]
        out_m[ref_name[k]], out_v[ref_name[k]] = m_k[None], v_k[None]
    order = ["norm_mix", "w_in", "conv_w", "a_log", "dt_bias", "dn_norm", "sinks", "w_o", "norm_mlp", "w_up", "w_down",
             "norm_ple", "w_ple_gate", "w_ple_proj", "norm_final"]
    return (loss, grad_x[None], *[out_g[k] for k in order], *[out_d[k] for k in order],
            *[out_m[k] for k in order], *[out_v[k] for k in order])
```

```python
import functools

import jax
import jax.numpy as jnp
from jax import lax
from jax.experimental import pallas as pl
from jax.experimental.pallas import tpu as pltpu

F32 = jnp.float32
BF16 = jnp.bfloat16
MXU_DTYPE = jnp.bfloat16
HI = lax.Precision.HIGHEST

D_MODEL = 1024
PLE_DIM = 256
ATTN_HEADS = 8
ATTN_KV_HEADS = 2
ATTN_GROUPS = ATTN_HEADS // ATTN_KV_HEADS
ATTN_HEAD_DIM = 64
ATTN_BLOCK = 128
ROPE_THETA = 10000.0
DN_HEADS = 4
DN_HEAD_DIM = 128
DN_CONV = 4
DN_CHUNK = 64
D_FF = 4 * D_MODEL
EPS = 1e-6
ATTN_Q = ATTN_HEADS * ATTN_HEAD_DIM
ATTN_KV = ATTN_KV_HEADS * ATTN_HEAD_DIM
DN_W = DN_HEADS * DN_HEAD_DIM
CONV_CH = 3 * DN_W
D_IN = ATTN_Q + 2 * ATTN_KV + 4 * DN_W + 2 * DN_HEADS
DN_COLS = 4 * DN_W + 128
DN_SCALE = DN_HEAD_DIM ** -0.5
ATTN_SCALE = ATTN_HEAD_DIM ** -0.5
FF_BLOCKS = 4
FF_BLOCK = D_FF // FF_BLOCKS

ADAM_LR = 0.001
ADAM_B1 = 0.9
ADAM_B2 = 0.999
ADAM_EPS = 1e-08
ADAM_WD = 0.01
ADAM_STEP = 10

V7X_VMEM_BYTES = 64 * 1024 * 1024
VMEM_LIMIT = 48 * 1024 * 1024

NN = ((1,), (0,))
NT = ((1,), (1,))
TN = ((0,), (0,))


def _dot(a, b, dims=NN, prec=None):
    return lax.dot_general(a, b, (dims, ((), ())), precision=prec, preferred_element_type=F32)


def _sigmoid(x):
    return 1.0 / (1.0 + jnp.exp(-x))


def _softplus(x):
    return jnp.maximum(x, 0.0) + jnp.log(1.0 + jnp.exp(-jnp.abs(x)))


def _params(*sem):
    return pltpu.CompilerParams(dimension_semantics=sem, vmem_limit_bytes=VMEM_LIMIT)


def _rms_fwd(xv, g):
    r = lax.rsqrt(jnp.mean(xv * xv, axis=-1, keepdims=True) + EPS)
    return xv * r * g


def _rms_bwd(xv, g, dn):
    r = lax.rsqrt(jnp.mean(xv * xv, axis=-1, keepdims=True) + EPS)
    xh = xv * r
    dg = jnp.sum(dn * xh, axis=0, keepdims=True)
    dxh = dn * g
    dx = r * (dxh - xh * jnp.mean(dxh * xh, axis=-1, keepdims=True))
    return dx, dg


def _full(shape):
    return pl.BlockSpec(shape, lambda *_: (0,) * len(shape))


def _inproj(x, g_mix, wa, wd, tm):
    t = x.shape[0]

    def body(x_ref, g_ref, wa_ref, wd_ref, u_ref, pa_ref, pd_ref):
        u = _rms_fwd(x_ref[...], g_ref[...]).astype(MXU_DTYPE)
        u_ref[...] = u
        pa_ref[...] = _dot(u, wa_ref[...])
        pd_ref[...] = _dot(u, wd_ref[...])

    na, nd = wa.shape[1], wd.shape[1]
    return pl.pallas_call(
        body, name="inproj", grid=(t // tm,),
        in_specs=[pl.BlockSpec((tm, D_MODEL), lambda i: (i, 0)), _full((1, D_MODEL)),
                  _full((D_MODEL, na)), _full((D_MODEL, nd))],
        out_specs=[pl.BlockSpec((tm, D_MODEL), lambda i: (i, 0)), pl.BlockSpec((tm, na), lambda i: (i, 0)),
                   pl.BlockSpec((tm, nd), lambda i: (i, 0))],
        out_shape=[jax.ShapeDtypeStruct((t, D_MODEL), MXU_DTYPE), jax.ShapeDtypeStruct((t, na), F32),
                   jax.ShapeDtypeStruct((t, nd), F32)],
        compiler_params=_params("parallel"),
    )(x, g_mix, wa, wd)


def _oproj(x, ao, dn, wo_a, wo_d, tm):
    t = x.shape[0]

    def body(x_ref, ao_ref, dn_ref, wa_ref, wd_ref, h_ref):
        h_ref[...] = (x_ref[...] + _dot(ao_ref[...].astype(MXU_DTYPE), wa_ref[...])
                      + _dot(dn_ref[...].astype(MXU_DTYPE), wd_ref[...]))

    half = ao.shape[1]
    return pl.pallas_call(
        body, name="oproj", grid=(t // tm,),
        in_specs=[pl.BlockSpec((tm, D_MODEL), lambda i: (i, 0)), pl.BlockSpec((tm, half), lambda i: (i, 0)),
                  pl.BlockSpec((tm, half), lambda i: (i, 0)), _full((half, D_MODEL)), _full((half, D_MODEL))],
        out_specs=pl.BlockSpec((tm, D_MODEL), lambda i: (i, 0)),
        out_shape=jax.ShapeDtypeStruct((t, D_MODEL), F32),
        compiler_params=_params("parallel"),
    )(x, ao, dn, wo_a, wo_d)


def _mlp_fwd(h1, g_mlp, w_up4, w_down, tm):
    t = h1.shape[0]

    def body(h_ref, g_ref, wu_ref, wd_ref, m_ref, r_ref, h2_ref, acc_ref):
        k = pl.program_id(1)

        @pl.when(k == 0)
        def _():
            m_ref[...] = _rms_fwd(h_ref[...], g_ref[...]).astype(MXU_DTYPE)
            acc_ref[...] = jnp.zeros_like(acc_ref)

        r = jnp.maximum(_dot(m_ref[...], wu_ref[...]), 0.0)
        r_ref[...] = r.astype(MXU_DTYPE)
        s = jnp.square(r).astype(MXU_DTYPE)
        acc_ref[...] += _dot(s, wd_ref[...])

        @pl.when(k == FF_BLOCKS - 1)
        def _():
            h2_ref[...] = h_ref[...] + acc_ref[...]

    return pl.pallas_call(
        body, name="mlp_fwd", grid=(t // tm, FF_BLOCKS),
        in_specs=[pl.BlockSpec((tm, D_MODEL), lambda i, k: (i, 0)), _full((1, D_MODEL)),
                  pl.BlockSpec((None, D_MODEL, FF_BLOCK), lambda i, k: (k, 0, 0)),
                  pl.BlockSpec((FF_BLOCK, D_MODEL), lambda i, k: (k, 0))],
        out_specs=[pl.BlockSpec((tm, D_MODEL), lambda i, k: (i, 0)), pl.BlockSpec((tm, FF_BLOCK), lambda i, k: (i, k)),
                   pl.BlockSpec((tm, D_MODEL), lambda i, k: (i, 0))],
        out_shape=[jax.ShapeDtypeStruct((t, D_MODEL), MXU_DTYPE), jax.ShapeDtypeStruct((t, D_FF), MXU_DTYPE),
                   jax.ShapeDtypeStruct((t, D_MODEL), F32)],
        scratch_shapes=[pltpu.VMEM((tm, D_MODEL), F32)],
        compiler_params=_params("parallel", "arbitrary"),
    )(h1, g_mlp, w_up4, w_down)


def _ple_loss(h2, p, tgt, g_ple, g_fin, w_gate, w_proj, tm):
    t = h2.shape[0]

    def body(h_ref, p_ref, t_ref, gp_ref, gf_ref, wg_ref, wp_ref,
             dh_ref, dhb_ref, dgp_ref, dpp_ref, n3_ref, pb_ref, acc_ref):
        @pl.when(pl.program_id(0) == 0)
        def _():
            acc_ref[...] = jnp.zeros_like(acc_ref)

        h = h_ref[...]
        g_ple_v, g_fin_v = gp_ref[...], gf_ref[...]
        n3 = _rms_fwd(h, g_ple_v).astype(MXU_DTYPE)
        n3_ref[...] = n3
        gate = _sigmoid(_dot(n3, wg_ref[...]))
        pb = p_ref[...].astype(MXU_DTYPE)
        pb_ref[...] = pb
        pp = _dot(pb, wp_ref[...])
        h3 = h + gate * pp
        r4 = lax.rsqrt(jnp.mean(h3 * h3, axis=-1, keepdims=True) + EPS)
        xh4 = h3 * r4
        e = xh4 * g_fin_v - t_ref[...]
        loss = 0.5 * jnp.sum(jnp.mean(e * e, axis=-1, keepdims=True), axis=0, keepdims=True)
        dy = e * (1.0 / D_MODEL)
        dg_fin = jnp.sum(dy * xh4, axis=0, keepdims=True)
        dxh = dy * g_fin_v
        dh3 = r4 * (dxh - xh4 * jnp.mean(dxh * xh4, axis=-1, keepdims=True))
        dpp_ref[...] = (dh3 * gate).astype(MXU_DTYPE)
        dgp = (dh3 * pp * gate * (1.0 - gate)).astype(MXU_DTYPE)
        dgp_ref[...] = dgp
        dn3 = _dot(dgp, wg_ref[...], NT)
        dx, dg_ple = _rms_bwd(h, g_ple_v, dn3)
        dh2 = dh3 + dx
        dh_ref[...] = dh2
        dhb_ref[...] = dh2.astype(MXU_DTYPE)
        acc_ref[0:1, :] += dg_fin
        acc_ref[1:2, :] += dg_ple
        acc_ref[2:3, :] += jnp.broadcast_to(loss, (1, D_MODEL))

    row = lambda w: pl.BlockSpec((tm, w), lambda i: (i, 0))
    return pl.pallas_call(
        body, name="ple_loss", grid=(t // tm,),
        in_specs=[row(D_MODEL), row(PLE_DIM), row(D_MODEL), _full((1, D_MODEL)), _full((1, D_MODEL)),
                  _full((D_MODEL, D_MODEL)), _full((PLE_DIM, D_MODEL))],
        out_specs=[row(D_MODEL), row(D_MODEL), row(D_MODEL), row(D_MODEL), row(D_MODEL), row(PLE_DIM),
                   _full((8, D_MODEL))],
        out_shape=[jax.ShapeDtypeStruct((t, D_MODEL), F32), jax.ShapeDtypeStruct((t, D_MODEL), MXU_DTYPE),
                   jax.ShapeDtypeStruct((t, D_MODEL), MXU_DTYPE), jax.ShapeDtypeStruct((t, D_MODEL), MXU_DTYPE),
                   jax.ShapeDtypeStruct((t, D_MODEL), MXU_DTYPE), jax.ShapeDtypeStruct((t, PLE_DIM), MXU_DTYPE),
                   jax.ShapeDtypeStruct((8, D_MODEL), F32)],
        compiler_params=_params("arbitrary"),
    )(h2, p, tgt, g_ple, g_fin, w_gate, w_proj)


def _mlp_bwd(dh2, dh2b, r, h1, g_mlp, w_up4, w_down, tm):
    t = h1.shape[0]

    def body(dh_ref, dhb_ref, r_ref, h_ref, g_ref, wu_ref, wd_ref,
             da_ref, dh1_ref, dh1b_ref, acc_ref, dm_ref):
        i, k = pl.program_id(0), pl.program_id(1)

        @pl.when((i == 0) & (k == 0))
        def _():
            acc_ref[...] = jnp.zeros_like(acc_ref)

        @pl.when(k == 0)
        def _():
            dm_ref[...] = jnp.zeros_like(dm_ref)

        ds = _dot(dhb_ref[...], wd_ref[...], NT)
        da = (ds * (2.0 * r_ref[...].astype(F32))).astype(MXU_DTYPE)
        da_ref[...] = da
        dm_ref[...] += _dot(da, wu_ref[...], NT)

        @pl.when(k == FF_BLOCKS - 1)
        def _():
            dx, dg = _rms_bwd(h_ref[...], g_ref[...], dm_ref[...])
            dh1 = dh_ref[...] + dx
            dh1_ref[...] = dh1
            dh1b_ref[...] = dh1.astype(MXU_DTYPE)
            acc_ref[0:1, :] += dg

    tok = lambda w: pl.BlockSpec((tm, w), lambda i, k: (i, 0))
    return pl.pallas_call(
        body, name="mlp_bwd", grid=(t // tm, FF_BLOCKS),
        in_specs=[tok(D_MODEL), tok(D_MODEL), pl.BlockSpec((tm, FF_BLOCK), lambda i, k: (i, k)), tok(D_MODEL),
                  _full((1, D_MODEL)), pl.BlockSpec((None, D_MODEL, FF_BLOCK), lambda i, k: (k, 0, 0)),
                  pl.BlockSpec((FF_BLOCK, D_MODEL), lambda i, k: (k, 0))],
        out_specs=[pl.BlockSpec((tm, FF_BLOCK), lambda i, k: (i, k)),
                   tok(D_MODEL), tok(D_MODEL), pl.BlockSpec((8, D_MODEL), lambda i, k: (0, 0))],
        out_shape=[jax.ShapeDtypeStruct((t, D_FF), MXU_DTYPE),
                   jax.ShapeDtypeStruct((t, D_MODEL), F32), jax.ShapeDtypeStruct((t, D_MODEL), MXU_DTYPE),
                   jax.ShapeDtypeStruct((8, D_MODEL), F32)],
        scratch_shapes=[pltpu.VMEM((tm, D_MODEL), F32)],
        compiler_params=_params("arbitrary", "arbitrary"),
    )(dh2, dh2b, r, h1, g_mlp, w_up4, w_down)


def _oproj_bwd(dh1b, wo_a, wo_d, tm, dep):
    t = dh1b.shape[0]
    half = wo_a.shape[0]

    def body(d_ref, wa_ref, wd_ref, dep_ref, da_ref, dd_ref):
        d = d_ref[...]
        da_ref[...] = _dot(d, wa_ref[...], NT)
        dd_ref[...] = _dot(d, wd_ref[...], NT)

    return pl.pallas_call(
        body, name="oproj_bwd", grid=(t // tm,),
        in_specs=[pl.BlockSpec((tm, D_MODEL), lambda i: (i, 0)), _full((half, D_MODEL)), _full((half, D_MODEL)),
                  pl.BlockSpec(memory_space=pl.ANY)],
        out_specs=[pl.BlockSpec((tm, half), lambda i: (i, 0)), pl.BlockSpec((tm, half), lambda i: (i, 0))],
        out_shape=[jax.ShapeDtypeStruct((t, half), F32), jax.ShapeDtypeStruct((t, half), F32)],
        compiler_params=_params("parallel"),
    )(dh1b, wo_a, wo_d, dep)


def _inproj_bwd(x, dh1, g_mix, grads, weights, tm):
    t = x.shape[0]
    n = len(grads)

    def body(*refs):
        x_ref, dh_ref, g_ref = refs[:3]
        g_refs, w_refs = refs[3:3 + n], refs[3 + n:3 + 2 * n]
        dx_ref, acc_ref = refs[3 + 2 * n:]

        @pl.when(pl.program_id(0) == 0)
        def _():
            acc_ref[...] = jnp.zeros_like(acc_ref)

        du = _dot(g_refs[0][...], w_refs[0][...], NT)
        for j in range(1, n):
            du += _dot(g_refs[j][...], w_refs[j][...], NT)
        dx, dg = _rms_bwd(x_ref[...], g_ref[...], du)
        dx_ref[...] = dh_ref[...] + dx
        acc_ref[0:1, :] += dg

    tok = lambda w: pl.BlockSpec((tm, w), lambda i: (i, 0))
    return pl.pallas_call(
        body, name="inproj_bwd", grid=(t // tm,),
        in_specs=[tok(D_MODEL), tok(D_MODEL), _full((1, D_MODEL))] + [tok(g.shape[1]) for g in grads]
                 + [_full(w.shape) for w in weights],
        out_specs=[tok(D_MODEL), _full((8, D_MODEL))],
        out_shape=[jax.ShapeDtypeStruct((t, D_MODEL), F32), jax.ShapeDtypeStruct((8, D_MODEL), F32)],
        compiler_params=_params("arbitrary"),
    )(x, dh1, g_mix, *grads, *weights)


def _wgrad(a, b, name, tk, tn, tt, stacked=False, prep=None):
    t, kdim = a.shape
    ncols = b.shape[1]

    def body(a_ref, b_ref, o_ref):
        @pl.when(pl.program_id(2) == 0)
        def _():
            o_ref[...] = jnp.zeros_like(o_ref)

        av = a_ref[...] if prep is None else prep(a_ref[...])
        o_ref[...] += _dot(av, b_ref[...], TN)

    if stacked:
        out_spec = pl.BlockSpec((None, tk, tn), lambda i, j, s: (j, i, 0))
        out_shape = jax.ShapeDtypeStruct((ncols // tn, kdim, tn), F32)
    else:
        out_spec = pl.BlockSpec((tk, tn), lambda i, j, s: (i, j))
        out_shape = jax.ShapeDtypeStruct((kdim, ncols), F32)
    return pl.pallas_call(
        body, name=name, grid=(kdim // tk, ncols // tn, t // tt),
        in_specs=[pl.BlockSpec((tt, tk), lambda i, j, s: (s, i)), pl.BlockSpec((tt, tn), lambda i, j, s: (s, j))],
        out_specs=out_spec, out_shape=out_shape,
        compiler_params=_params("parallel", "parallel", "arbitrary"),
    )(a, b)


def _rope_tables(t):
    half = ATTN_HEAD_DIM // 2
    inv = 1.0 / (ROPE_THETA ** (jnp.arange(half, dtype=F32) * (2.0 / ATTN_HEAD_DIM)))
    ang = jnp.arange(t, dtype=F32)[:, None] * inv[None, :]
    cos, sin = jnp.cos(ang), jnp.sin(ang)
    cos2 = jnp.concatenate([cos, cos], axis=-1)
    sin2 = jnp.concatenate([-sin, sin], axis=-1)
    return jnp.tile(cos2, (1, 2)), jnp.tile(sin2, (1, 2))


def _swap_halves(tv):
    w = tv.shape[-1]
    lane = lax.broadcasted_iota(jnp.int32, tv.shape, tv.ndim - 1)
    first = (lane % ATTN_HEAD_DIM) < (ATTN_HEAD_DIM // 2)
    return jnp.where(first, pltpu.roll(tv, w - ATTN_HEAD_DIM // 2, tv.ndim - 1),
                     pltpu.roll(tv, ATTN_HEAD_DIM // 2, tv.ndim - 1))


def _rope(tv, cos, sin):
    return tv * cos + _swap_halves(tv) * sin


def _rope_bwd(dv, cos, sin):
    return dv * cos + _swap_halves(dv * sin)


def _attn_valid(first_block):
    c = lax.broadcasted_iota(jnp.int32, (2 * ATTN_BLOCK, ATTN_BLOCK), 0)
    r = lax.broadcasted_iota(jnp.int32, (2 * ATTN_BLOCK, ATTN_BLOCK), 1)
    return (c > r) & (c <= r + ATTN_BLOCK) & ((c >= ATTN_BLOCK) | jnp.logical_not(first_block))


def _attn_probs(st, sink, valid):
    s = jnp.where(valid, st * ATTN_SCALE, -jnp.inf)
    m = jnp.maximum(jnp.max(s, axis=0, keepdims=True), sink)
    e = jnp.where(valid, jnp.exp(s - m), 0.0)
    es = jnp.exp(sink - m)
    inv = 1.0 / (jnp.sum(e, axis=0, keepdims=True) + es)
    return e * inv, es * inv


def _lane_scalar(vec, idx):
    lane = lax.broadcasted_iota(jnp.int32, vec.shape, 1)
    return jnp.sum(jnp.where(lane == idx, vec, 0.0), axis=-1, keepdims=True)


def _attn_specs(nb):
    cur = lambda w, cb: pl.BlockSpec((ATTN_BLOCK, w), lambda i: (jnp.minimum(i, nb - 1), cb))
    prev = lambda w, cb: pl.BlockSpec((ATTN_BLOCK, w), lambda i: (jnp.maximum(jnp.minimum(i, nb - 1) - 1, 0), cb))
    kcol, vcol = ATTN_Q // ATTN_KV, ATTN_Q // ATTN_KV + 1
    return [cur(ATTN_Q, 0), cur(ATTN_KV, kcol), prev(ATTN_KV, kcol), cur(ATTN_KV, vcol), prev(ATTN_KV, vcol),
            cur(ATTN_KV, 0), cur(ATTN_KV, 0), prev(ATTN_KV, 0), prev(ATTN_KV, 0), _full((1, 128))]


def _attn_fwd(pa, cos, sin, sinks_vec):
    t = pa.shape[0]
    nb = t // ATTN_BLOCK

    def body(q_ref, kc_ref, kp_ref, vc_ref, vp_ref, cc_ref, sc_ref, cp_ref, sp_ref, sk_ref, o_ref):
        first = pl.program_id(0) == 0
        cc, sc = cc_ref[...], sc_ref[...]
        q = _rope(q_ref[...], jnp.tile(cc, (1, ATTN_Q // ATTN_KV)), jnp.tile(sc, (1, ATTN_Q // ATTN_KV)))
        kc = _rope(kc_ref[...], cc, sc)
        kp = _rope(kp_ref[...], cp_ref[...], sp_ref[...])
        vc, vp = vc_ref[...], vp_ref[...]
        sk = sk_ref[...]
        valid = _attn_valid(first)
        kv = lambda tp, tc, hk: jnp.concatenate([tp[:, hk * ATTN_HEAD_DIM:(hk + 1) * ATTN_HEAD_DIM],
                                                 tc[:, hk * ATTN_HEAD_DIM:(hk + 1) * ATTN_HEAD_DIM]], axis=0)
        kwins = [kv(kp, kc, hk) for hk in range(ATTN_KV_HEADS)]
        vwins_t = [kv(vp, vc, hk).T for hk in range(ATTN_KV_HEADS)]
        heads = [slice(h * ATTN_HEAD_DIM, (h + 1) * ATTN_HEAD_DIM) for h in range(ATTN_HEADS)]
        scores = [_dot(kwins[h // ATTN_GROUPS], q[:, hs], NT) for h, hs in enumerate(heads)]
        probs = [_attn_probs(st, _lane_scalar(sk, h), valid)[0] for h, st in enumerate(scores)]
        for h, (hs, pt) in enumerate(zip(heads, probs)):
            o_ref[:, hs] = _dot(vwins_t[h // ATTN_GROUPS], pt).T.astype(o_ref.dtype)

    return pl.pallas_call(
        body, name="attn_fwd", grid=(nb,),
        in_specs=_attn_specs(nb),
        out_specs=pl.BlockSpec((ATTN_BLOCK, ATTN_Q), lambda i: (i, 0)),
        out_shape=jax.ShapeDtypeStruct((t, ATTN_Q), MXU_DTYPE),
        compiler_params=_params("parallel"),
    )(pa, pa, pa, pa, pa, cos, sin, cos, sin, sinks_vec)


def _attn_bwd(pa, cos, sin, sinks_vec, dao):
    t = pa.shape[0]
    nb = t // ATTN_BLOCK

    def body(q_ref, kc_ref, kp_ref, vc_ref, vp_ref, cc_ref, sc_ref, cp_ref, sp_ref, sk_ref, do_ref,
             dq_ref, dk_ref, dv_ref, acc_ref, dqr_ref, dkw_ref, dvw_ref, ck_ref, cv_ref):
        i = pl.program_id(0)

        @pl.when(i == 0)
        def _():
            acc_ref[...] = jnp.zeros_like(acc_ref)
            ck_ref[...] = jnp.zeros_like(ck_ref)
            cv_ref[...] = jnp.zeros_like(cv_ref)

        @pl.when(i < nb)
        def _():
            first = i == 0
            cc, sc = cc_ref[...], sc_ref[...]
            cq, sq = jnp.tile(cc, (1, ATTN_Q // ATTN_KV)), jnp.tile(sc, (1, ATTN_Q // ATTN_KV))
            q = _rope(q_ref[...], cq, sq)
            kc = _rope(kc_ref[...], cc, sc)
            kp = _rope(kp_ref[...], cp_ref[...], sp_ref[...])
            vc, vp = vc_ref[...], vp_ref[...]
            sk = sk_ref[...]
            do = do_ref[...]
            lane = lax.broadcasted_iota(jnp.int32, (1, 128), 1)
            dsink = jnp.zeros((1, 128), F32)
            valid = _attn_valid(first)
            kv = lambda tp, tc, hk: jnp.concatenate([tp[:, hk * ATTN_HEAD_DIM:(hk + 1) * ATTN_HEAD_DIM],
                                                     tc[:, hk * ATTN_HEAD_DIM:(hk + 1) * ATTN_HEAD_DIM]], axis=0)
            kwins = [kv(kp, kc, hk) for hk in range(ATTN_KV_HEADS)]
            vwins = [kv(vp, vc, hk) for hk in range(ATTN_KV_HEADS)]
            kwins_t = [kw.T for kw in kwins]
            heads = [slice(h * ATTN_HEAD_DIM, (h + 1) * ATTN_HEAD_DIM) for h in range(ATTN_HEADS)]
            scores = [_dot(kwins[h // ATTN_GROUPS], q[:, hs], NT) for h, hs in enumerate(heads)]
            dps = [_dot(vwins[h // ATTN_GROUPS], do[:, hs], NT) for h, hs in enumerate(heads)]
            pts, dsts = [], []
            for h, (st, dp_t) in enumerate(zip(scores, dps)):
                probs_t, psink = _attn_probs(st, _lane_scalar(sk, h), valid)
                delta = jnp.sum(probs_t * dp_t, axis=0, keepdims=True)
                pts.append(probs_t)
                dsts.append(probs_t * (dp_t - delta) * ATTN_SCALE)
                dsink += jnp.where(lane == h, jnp.sum(-psink * delta, axis=1, keepdims=True), 0.0)
            for h, (hs, ds_t) in enumerate(zip(heads, dsts)):
                dqr_ref[:, hs] = _dot(kwins_t[h // ATTN_GROUPS], ds_t).T
            for hk in range(ATTN_KV_HEADS):
                ks = slice(hk * ATTN_HEAD_DIM, (hk + 1) * ATTN_HEAD_DIM)
                group = range(hk * ATTN_GROUPS, (hk + 1) * ATTN_GROUPS)
                ds_g = jnp.concatenate([dsts[h] for h in group], axis=1)
                p_g = jnp.concatenate([pts[h] for h in group], axis=1)
                q_g = jnp.concatenate([q[:, heads[h]] for h in group], axis=0)
                do_g = jnp.concatenate([do[:, heads[h]] for h in group], axis=0)
                dkw_ref[:, ks] = _dot(ds_g, q_g)
                dvw_ref[:, ks] = _dot(p_g, do_g)
            acc_ref[0:1, :] += dsink
            dq_ref[...] = _rope_bwd(dqr_ref[...], cq, sq).astype(dq_ref.dtype)
            dk_ref[...] = (ck_ref[...] + _rope_bwd(dkw_ref[0:ATTN_BLOCK, :], cp_ref[...], sp_ref[...])).astype(dk_ref.dtype)
            dv_ref[...] = (cv_ref[...] + dvw_ref[0:ATTN_BLOCK, :]).astype(dv_ref.dtype)
            ck_ref[...] = _rope_bwd(dkw_ref[ATTN_BLOCK:2 * ATTN_BLOCK, :], cc, sc)
            cv_ref[...] = dvw_ref[ATTN_BLOCK:2 * ATTN_BLOCK, :]

        @pl.when(i == nb)
        def _():
            dk_ref[...] = ck_ref[...].astype(dk_ref.dtype)
            dv_ref[...] = cv_ref[...].astype(dv_ref.dtype)

    prev_out = lambda w: pl.BlockSpec((ATTN_BLOCK, w), lambda i: (jnp.maximum(i - 1, 0), 0))
    return pl.pallas_call(
        body, name="attn_bwd", grid=(nb + 1,),
        in_specs=_attn_specs(nb) + [pl.BlockSpec((ATTN_BLOCK, ATTN_Q), lambda i: (jnp.minimum(i, nb - 1), 0))],
        out_specs=[pl.BlockSpec((ATTN_BLOCK, ATTN_Q), lambda i: (jnp.minimum(i, nb - 1), 0)), prev_out(ATTN_KV),
                   prev_out(ATTN_KV), _full((8, 128))],
        out_shape=[jax.ShapeDtypeStruct((t, ATTN_Q), MXU_DTYPE), jax.ShapeDtypeStruct((t, ATTN_KV), MXU_DTYPE),
                   jax.ShapeDtypeStruct((t, ATTN_KV), MXU_DTYPE), jax.ShapeDtypeStruct((8, 128), F32)],
        scratch_shapes=[pltpu.VMEM((ATTN_BLOCK, ATTN_Q), F32), pltpu.VMEM((2 * ATTN_BLOCK, ATTN_KV), F32),
                        pltpu.VMEM((2 * ATTN_BLOCK, ATTN_KV), F32), pltpu.VMEM((ATTN_BLOCK, ATTN_KV), F32),
                        pltpu.VMEM((ATTN_BLOCK, ATTN_KV), F32)],
        compiler_params=_params("arbitrary"),
    )(pa, pa, pa, pa, pa, cos, sin, cos, sin, sinks_vec, dao)


PAIR = 2 * DN_CHUNK
INTRA_PAIRS = 4
HALO = 8


def _conv_window(cur_ref, prev_ref, xs_ref, tm):
    prev = jnp.where(pl.program_id(0) > 0, prev_ref[...], 0.0)
    xs_ref[0:HALO, :] = prev
    xs_ref[HALO:HALO + tm, :] = cur_ref[...]


def _conv_taps(xs_ref, cw_ref, tm):
    y = cw_ref[0:1, :] * xs_ref[pl.ds(HALO - DN_CONV + 1, tm), :]
    for j in range(1, DN_CONV):
        y += cw_ref[j:j + 1, :] * xs_ref[pl.ds(HALO - DN_CONV + 1 + j, tm), :]
    return y


def _gate_values(ba, al, dt):
    beta = _sigmoid(ba)
    pre = ba + dt
    g = -jnp.exp(al) * _softplus(pre)
    return beta, g, pre


def _dn_prep_specs(tm, t):
    return [pl.BlockSpec((tm, CONV_CH), lambda i: (i, 0)),
            pl.BlockSpec((HALO, CONV_CH), lambda i: (jnp.maximum(i * (tm // HALO) - 1, 0), 0)),
            pl.BlockSpec((tm, 128), lambda i: (i, 4 * DN_W // 128)),
            _full((DN_CONV, CONV_CH)), _full((1, 128)), _full((1, 128))]


def _dn_prep(pd, conv_w, al_vec, dt_vec, tm):
    t = pd.shape[0]

    def body(cur_ref, prev_ref, ba_ref, cw_ref, al_ref, dt_ref, qn_ref, kn_ref, vc_ref, gc_ref, gr_ref, xs_ref):
        _conv_window(cur_ref, prev_ref, xs_ref, tm)
        y = _conv_taps(xs_ref, cw_ref, tm)
        c = y * _sigmoid(y)
        for h in range(DN_HEADS):
            qs = slice(h * DN_HEAD_DIM, (h + 1) * DN_HEAD_DIM)
            ksl = slice(DN_W + h * DN_HEAD_DIM, DN_W + (h + 1) * DN_HEAD_DIM)
            qh, kh = c[:, qs], c[:, ksl]
            qn_ref[:, qs] = qh * lax.rsqrt(jnp.sum(qh * qh, axis=-1, keepdims=True) + EPS) * DN_SCALE
            kn_ref[:, qs] = kh * lax.rsqrt(jnp.sum(kh * kh, axis=-1, keepdims=True) + EPS)
        vc_ref[...] = c[:, 2 * DN_W:3 * DN_W]
        beta, g, _ = _gate_values(ba_ref[...], al_ref[...], dt_ref[...])
        lane = lax.broadcasted_iota(jnp.int32, beta.shape, 1)
        gb = jnp.where(lane < DN_HEADS, beta, jnp.where(lane < 2 * DN_HEADS, g, 0.0))
        gc_ref[...] = gb
        gr_ref[...] = gb.T[0:8, :]

    tok = lambda w: pl.BlockSpec((tm, w), lambda i: (i, 0))
    return pl.pallas_call(
        body, name="dn_prep", grid=(t // tm,),
        in_specs=_dn_prep_specs(tm, t),
        out_specs=[tok(DN_W), tok(DN_W), tok(DN_W), tok(128), pl.BlockSpec((8, tm), lambda i: (0, i))],
        out_shape=[jax.ShapeDtypeStruct((t, DN_W), F32)] * 3 + [jax.ShapeDtypeStruct((t, 128), F32),
                                                                 jax.ShapeDtypeStruct((8, t), F32)],
        scratch_shapes=[pltpu.VMEM((HALO + tm, CONV_CH), F32)],
        compiler_params=_params("parallel"),
    )(pd, pd, pd, conv_w, al_vec, dt_vec)


def _pair_masks():
    r = lax.broadcasted_iota(jnp.int32, (PAIR, PAIR), 0)
    c = lax.broadcasted_iota(jnp.int32, (PAIR, PAIR), 1)
    same = (r < DN_CHUNK) == (c < DN_CHUNK)
    return same & (r >= c), same & (r > c)


def _lane_col(mat, idx):
    lane = lax.broadcasted_iota(jnp.int32, mat.shape, 1)
    return jnp.sum(jnp.where(lane == idx, mat, 0.0), axis=-1, keepdims=True)


def _pair_cumsums(gc, gr, low):
    lowf = low.astype(F32)
    return _dot(lowf, gc, NN, HI), _dot(gr, lowf, NT, HI)


def _pair_gates(gc, cum_c, cum_r, low, h):
    beta = _lane_col(gc, h)
    gam = _lane_col(cum_c, DN_HEADS + h)
    gam_row = cum_r[DN_HEADS + h:DN_HEADS + h + 1, :]
    dm = jnp.where(low, jnp.exp(jnp.where(low, gam - gam_row, 0.0)), 0.0)
    row = lax.broadcasted_iota(jnp.int32, gam.shape, 0)
    gl = jnp.where(row < DN_CHUNK, gam[DN_CHUNK - 1:DN_CHUNK, :], gam[PAIR - 1:PAIR, :])
    return beta, gam, dm, gl


def _split(a):
    hi = a.astype(BF16)
    return hi, (a - hi.astype(F32)).astype(BF16)


def _dot_split(a, b, dims=NN):
    (ah, al), (bh, bl) = a, b
    la, lb = (1, 1) if dims == TN else ((0, 1) if dims == NN else (0, 0))
    r = _dot(jnp.concatenate([ah, al], axis=la), jnp.concatenate([bh, bl], axis=lb), dims)
    m, n = r.shape[0] // 2, r.shape[1] // 2
    return (r[m:, n:] + (r[:m, n:] + r[m:, :n])) + r[:m, :n]


def _unit_lower_inverses(lmats):
    n = lmats[0].shape[0]
    r = lax.broadcasted_iota(jnp.int32, (n, n), 0)
    c = lax.broadcasted_iota(jnp.int32, (n, n), 1)
    same = lambda size: (r & ~(size - 1)) == (c & ~(size - 1))
    base = DN_CHUNK // 4
    diag = [jnp.where(same(base), l, 0.0) for l in lmats]
    accs = [(r == c).astype(F32) - d for d in diag]
    splits = [_split(d) for d in diag]
    step = 1
    while 2 * step < base:
        splits = [_split(_dot_split(s, s)) for s in splits]
        accs = [acc + _dot_split(_split(acc), s) for acc, s in zip(accs, splits)]
        step *= 2
    size = base
    while size < DN_CHUNK:
        below = same(2 * size) & jnp.logical_not(same(size))
        tb = [_dot(acc, jnp.where(below, l, 0.0)) for acc, l in zip(accs, lmats)]
        accs = [acc - _dot(t, acc) for acc, t in zip(accs, tb)]
        size *= 2
    return accs


def _dn_intra(qn, kn, vc, gc, gr):
    t = qn.shape[0]
    npair = t // PAIR
    rows_step = INTRA_PAIRS * PAIR

    def body(q_ref, k_ref, v_ref, gc_ref, gr_ref, u_ref, w_ref, qg_ref, kd_ref, a_ref, ti_ref, dl_ref):
        low, strict = _pair_masks()
        items = []
        for p in range(INTRA_PAIRS):
            rows = slice(p * PAIR, (p + 1) * PAIR)
            gc_v = gc_ref[rows, :]
            cum_c, cum_r = _pair_cumsums(gc_v, gr_ref[:, rows], low)
            for h in range(DN_HEADS):
                hs = slice(h * DN_HEAD_DIM, (h + 1) * DN_HEAD_DIM)
                items.append((p, h, rows, hs, _pair_gates(gc_v, cum_c, cum_r, low, h)))
        lmats = []
        for p, h, rows, hs, (beta, gam, dm, gl) in items:
            k = k_ref[rows, hs]
            lmats.append(jnp.where(strict, _dot(k * beta, k, NT) * dm, 0.0))
        tinvs = _unit_lower_inverses(lmats)
        for (p, h, rows, hs, (beta, gam, dm, gl)), tinv in zip(items, tinvs):
            q, k, v = q_ref[rows, hs], k_ref[rows, hs], v_ref[rows, hs]
            eg = jnp.exp(gam)
            u_ref[rows, hs] = _dot(tinv, v * beta)
            w_ref[rows, hs] = _dot(tinv, (k * beta) * eg)
            a_ref[h, rows, :] = _dot(q, k, NT) * dm
            ti_ref[h, rows, :] = tinv
            qg_ref[rows, hs] = q * eg
            kd_ref[rows, hs] = k * jnp.exp(gl - gam)
            for c in range(2):
                last = (c + 1) * DN_CHUNK - 1
                dl_ref[2 * p + c, h] = jnp.broadcast_to(jnp.exp(gam[last:last + 1, :]), (8, 128))

    tok = lambda w: pl.BlockSpec((rows_step, w), lambda n: (n, 0))
    hm = pl.BlockSpec((DN_HEADS, rows_step, PAIR), lambda n: (0, n, 0))
    return pl.pallas_call(
        body, name="dn_intra", grid=(npair // INTRA_PAIRS,),
        in_specs=[tok(DN_W), tok(DN_W), tok(DN_W), tok(128), pl.BlockSpec((8, rows_step), lambda n: (0, n))],
        out_specs=[tok(DN_W)] * 4 + [hm, hm, pl.BlockSpec((2 * INTRA_PAIRS, DN_HEADS, 8, 128), lambda n: (n, 0, 0, 0))],
        out_shape=[jax.ShapeDtypeStruct((t, DN_W), F32)] * 4 + [jax.ShapeDtypeStruct((DN_HEADS, t, PAIR), F32)] * 2
                  + [jax.ShapeDtypeStruct((2 * npair, DN_HEADS, 8, 128), F32)],
        compiler_params=_params("parallel"),
    )(qn, kn, vc, gc, gr)


def _dn_scan_fwd(u, w, qg, kd, a_qk, dlast, pd, dn_w):
    t = u.shape[0]
    npair = t // PAIR

    def body(u_ref, w_ref, qg_ref, kd_ref, a_ref, dl_ref, z_ref, nw_ref, out_ref, o_ref, vn_ref, sall_ref, s_ref):
        @pl.when(pl.program_id(0) == 0)
        def _():
            s_ref[...] = jnp.zeros_like(s_ref)

        nw = nw_ref[...]
        for c in range(2):
            rows = slice(c * DN_CHUNK, (c + 1) * DN_CHUNK)
            for h in range(DN_HEADS):
                hs = slice(h * DN_HEAD_DIM, (h + 1) * DN_HEAD_DIM)
                st = s_ref[h]
                sall_ref[c, h] = st
                vn_ref[rows, hs] = u_ref[rows, hs] - _dot(w_ref[rows, hs], st)
            for h in range(DN_HEADS):
                hs = slice(h * DN_HEAD_DIM, (h + 1) * DN_HEAD_DIM)
                st, vn = s_ref[h], vn_ref[rows, hs]
                o = _dot(qg_ref[rows, hs], st) + _dot(a_ref[h, rows, rows], vn)
                s_ref[h] = st * dl_ref[c, h][0:1, :] + _dot(kd_ref[rows, hs], vn, TN)
                o_ref[rows, hs] = o
                z = z_ref[rows, hs]
                on = o * lax.rsqrt(jnp.mean(o * o, axis=-1, keepdims=True) + EPS) * nw
                out_ref[rows, hs] = (on * (z * _sigmoid(z))).astype(out_ref.dtype)

    tok = pl.BlockSpec((PAIR, DN_W), lambda n: (n, 0))
    hm = pl.BlockSpec((DN_HEADS, PAIR, PAIR), lambda n: (0, n, 0))
    return pl.pallas_call(
        body, name="dn_scan_fwd", grid=(npair,),
        in_specs=[tok, tok, tok, tok, hm, pl.BlockSpec((2, DN_HEADS, 8, 128), lambda n: (n, 0, 0, 0)),
                  pl.BlockSpec((PAIR, DN_W), lambda n: (n, 3)), _full((1, 128))],
        out_specs=[tok, tok, tok, pl.BlockSpec((2, DN_HEADS, DN_HEAD_DIM, DN_HEAD_DIM), lambda n: (n, 0, 0, 0))],
        out_shape=[jax.ShapeDtypeStruct((t, DN_W), MXU_DTYPE)] + [jax.ShapeDtypeStruct((t, DN_W), F32)] * 2
                  + [jax.ShapeDtypeStruct((2 * npair, DN_HEADS, DN_HEAD_DIM, DN_HEAD_DIM), F32)],
        scratch_shapes=[pltpu.VMEM((DN_HEADS, DN_HEAD_DIM, DN_HEAD_DIM), F32)],
        compiler_params=_params("arbitrary"),
    )(u, w, qg, kd, a_qk, dlast, pd, dn_w)


def _dn_scan_bwd(dout, o, vnew, sall, w, qg, kd, a_qk, dlast, pd, dn_w):
    t = o.shape[0]
    npair = t // PAIR
    rev = lambda n: npair - 1 - n

    def body(do_ref, o_ref, vn_ref, sall_ref, w_ref, qg_ref, kd_ref, a_ref, dl_ref, z_ref, nw_ref,
             dz_ref, du_ref, dw_ref, dqg_ref, dkd_ref, da_ref, ddl_ref, acc_ref, ds_ref, dos_ref):
        @pl.when(pl.program_id(0) == 0)
        def _():
            ds_ref[...] = jnp.zeros_like(ds_ref)
            acc_ref[...] = jnp.zeros_like(acc_ref)

        nw = nw_ref[...]
        dnw = jnp.zeros((1, 128), F32)
        for h in range(DN_HEADS):
            hs = slice(h * DN_HEAD_DIM, (h + 1) * DN_HEAD_DIM)
            o, z, dout = o_ref[:, hs], z_ref[:, hs], do_ref[:, hs]
            r = lax.rsqrt(jnp.mean(o * o, axis=-1, keepdims=True) + EPS)
            oh = o * r
            sz = _sigmoid(z)
            dz_ref[:, hs] = dout * (oh * nw) * (sz + z * sz * (1.0 - sz))
            don = dout * (z * sz)
            dnw += jnp.sum(don * oh, axis=0, keepdims=True)
            doh = don * nw
            dos_ref[:, hs] = r * (doh - oh * jnp.mean(doh * oh, axis=-1, keepdims=True))
        acc_ref[0:1, :] += dnw
        for c in (1, 0):
            rows = slice(c * DN_CHUNK, (c + 1) * DN_CHUNK)
            other = slice((1 - c) * DN_CHUNK, (2 - c) * DN_CHUNK)
            for h in range(DN_HEADS):
                hs = slice(h * DN_HEAD_DIM, (h + 1) * DN_HEAD_DIM)
                do, st, dsp, vn = dos_ref[rows, hs], sall_ref[c, h], ds_ref[h], vn_ref[rows, hs]
                da_ref[h, rows, rows] = _dot(do, vn, NT)
                da_ref[h, rows, other] = jnp.zeros((DN_CHUNK, DN_CHUNK), F32)
                du_ref[rows, hs] = _dot(a_ref[h, rows, rows], do, TN) + _dot(kd_ref[rows, hs], dsp)
                dqg_ref[rows, hs] = _dot(do, st, NT)
                dkd_ref[rows, hs] = _dot(vn, dsp, NT)
                ddl = jnp.sum(jnp.sum(dsp * st, axis=1, keepdims=True), axis=0, keepdims=True)
                ddl_ref[c, h] = jnp.broadcast_to(ddl, (8, 128))
            for h in range(DN_HEADS):
                hs = slice(h * DN_HEAD_DIM, (h + 1) * DN_HEAD_DIM)
                do, st, dvn = dos_ref[rows, hs], sall_ref[c, h], du_ref[rows, hs]
                dw_ref[rows, hs] = -_dot(dvn, st, NT)
                ds_ref[h] = (ds_ref[h] * dl_ref[c, h][0:1, :] + _dot(qg_ref[rows, hs], do, TN)
                             - _dot(w_ref[rows, hs], dvn, TN))

    tok = pl.BlockSpec((PAIR, DN_W), lambda n: (rev(n), 0))
    hm = pl.BlockSpec((DN_HEADS, PAIR, PAIR), lambda n: (0, rev(n), 0))
    sc = pl.BlockSpec((2, DN_HEADS, 8, 128), lambda n: (rev(n), 0, 0, 0))
    return pl.pallas_call(
        body, name="dn_scan_bwd", grid=(npair,),
        in_specs=[tok, tok, tok, pl.BlockSpec((2, DN_HEADS, DN_HEAD_DIM, DN_HEAD_DIM), lambda n: (rev(n), 0, 0, 0)),
                  tok, tok, tok, hm, sc, pl.BlockSpec((PAIR, DN_W), lambda n: (rev(n), 3)), _full((1, 128))],
        out_specs=[tok] * 5 + [hm, sc, _full((8, 128))],
        out_shape=[jax.ShapeDtypeStruct((t, DN_W), F32)] * 5 + [jax.ShapeDtypeStruct((DN_HEADS, t, PAIR), F32),
                   jax.ShapeDtypeStruct((2 * npair, DN_HEADS, 8, 128), F32), jax.ShapeDtypeStruct((8, 128), F32)],
        scratch_shapes=[pltpu.VMEM((DN_HEADS, DN_HEAD_DIM, DN_HEAD_DIM), F32), pltpu.VMEM((PAIR, DN_W), F32)],
        compiler_params=_params("arbitrary"),
    )(dout, o, vnew, sall, w, qg, kd, a_qk, dlast, pd, dn_w)


def _dn_intra_bwd(qn, kn, vc, gc, gr, tinv, a_qk, du, dw, dqg, dkd, da_qk, ddlast, dlast, dep):
    t = qn.shape[0]
    npair = t // PAIR

    def body(q_ref, k_ref, v_ref, gc_ref, gr_ref, ti_ref, a_ref, du_ref, dw_ref, dqg_ref, dkd_ref, da_ref, ddl_ref, dl_ref,
             dep_ref, dq_ref, dk_ref, dv_ref, dg_ref):
        low, strict = _pair_masks()
        lane = lax.broadcasted_iota(jnp.int32, (PAIR, 128), 1)
        rowi = lax.broadcasted_iota(jnp.int32, (PAIR, 1), 0)
        rsum = lambda v: jnp.sum(v, axis=-1, keepdims=True)
        items = []
        for p in range(INTRA_PAIRS):
            rows = slice(p * PAIR, (p + 1) * PAIR)
            gc_v = gc_ref[rows, :]
            cum_c, cum_r = _pair_cumsums(gc_v, gr_ref[:, rows], low)
            for h in range(DN_HEADS):
                hs = slice(h * DN_HEAD_DIM, (h + 1) * DN_HEAD_DIM)
                items.append((p, h, rows, hs, _pair_gates(gc_v, cum_c, cum_r, low, h)))
        dtis, lmats, dvbs, dkbgs = [], [], [], []
        for p, h, rows, hs, (beta, gam, dm, gl) in items:
            k, tinv = k_ref[rows, hs], ti_ref[h, rows, :]
            kb = k * beta
            dtis.append(_dot(du_ref[rows, hs], v_ref[rows, hs] * beta, NT)
                        + _dot(dw_ref[rows, hs], kb * jnp.exp(gam), NT))
            lmats.append(jnp.where(strict, _dot(kb, k, NT) * dm, 0.0))
            dvbs.append(_dot(tinv, du_ref[rows, hs], TN))
            dkbgs.append(_dot(tinv, dw_ref[rows, hs], TN))
        xs = [_dot(ti_ref[h, rows, :], dti, TN) for (p, h, rows, hs, g), dti in zip(items, dtis)]
        dls = [jnp.where(strict, -_dot(x, ti_ref[h, rows, :], NT), 0.0) for (p, h, rows, hs, g), x in zip(items, xs)]
        dgam_all = [jnp.zeros((PAIR, 128), F32) for _ in range(INTRA_PAIRS)]
        dbeta_all = [jnp.zeros((PAIR, 128), F32) for _ in range(INTRA_PAIRS)]
        for (p, h, rows, hs, (beta, gam, dm, gl)), dl, lmat, dvb, dkbg in zip(items, dls, lmats, dvbs, dkbgs):
            q, k, v = q_ref[rows, hs], k_ref[rows, hs], v_ref[rows, hs]
            a = a_ref[h, rows, :]
            dqg, dkd = dqg_ref[rows, hs], dkd_ref[rows, hs]
            kb = k * beta
            eg = jnp.exp(gam)
            ekd = jnp.exp(gl - gam)
            dmm = dl * dm
            dam = jnp.where(low, da_ref[h, rows, :], 0.0)
            dn = dam * dm
            e = dl * lmat + dam * a
            dkb = _dot(dmm, k) + dkbg * eg
            dk_ref[rows, hs] = _dot(dmm, kb, TN) + _dot(dn, q, TN) + dkd * ekd + dkb * beta
            dq_ref[rows, hs] = _dot(dn, k) + dqg * eg
            dv_ref[rows, hs] = dvb * beta
            t_kd = rsum(dkd * (k * ekd))
            dgam = rsum(e) - rsum(e.T) + rsum(dqg * (q * eg)) + rsum(dkbg * (kb * eg)) - t_kd
            for c in range(2):
                crows = slice(c * DN_CHUNK, (c + 1) * DN_CHUNK)
                dgl = (jnp.sum(t_kd[crows, :], axis=0, keepdims=True)
                       + ddl_ref[2 * p + c, h][0:1, 0:1] * dl_ref[2 * p + c, h][0:1, 0:1])
                dgam = dgam + jnp.where(rowi == (c + 1) * DN_CHUNK - 1, dgl, 0.0)
            dgam_all[p] += jnp.where(lane == DN_HEADS + h, dgam, 0.0)
            dbeta_all[p] += jnp.where(lane == h, rsum(dkb * k) + rsum(dvb * v), 0.0)
        for p in range(INTRA_PAIRS):
            dg_ref[p * PAIR:(p + 1) * PAIR, :] = dbeta_all[p] + _dot(low.astype(F32), dgam_all[p], TN, HI)

    rows_step = INTRA_PAIRS * PAIR
    tok = lambda w: pl.BlockSpec((rows_step, w), lambda n: (n, 0))
    hm = pl.BlockSpec((DN_HEADS, rows_step, PAIR), lambda n: (0, n, 0))
    sc = pl.BlockSpec((2 * INTRA_PAIRS, DN_HEADS, 8, 128), lambda n: (n, 0, 0, 0))
    return pl.pallas_call(
        body, name="dn_intra_bwd", grid=(npair // INTRA_PAIRS,),
        in_specs=[tok(DN_W), tok(DN_W), tok(DN_W), tok(128), pl.BlockSpec((8, rows_step), lambda n: (0, n)), hm, hm,
                  tok(DN_W), tok(DN_W), tok(DN_W), tok(DN_W), hm, sc, sc, pl.BlockSpec(memory_space=pl.ANY)],
        out_specs=[tok(DN_W), tok(DN_W), tok(DN_W), tok(128)],
        out_shape=[jax.ShapeDtypeStruct((t, DN_W), F32)] * 3 + [jax.ShapeDtypeStruct((t, 128), F32)],
        compiler_params=_params("parallel"),
    )(qn, kn, vc, gc, gr, tinv, a_qk, du, dw, dqg, dkd, da_qk, ddlast, dlast, dep)


def _dn_prep_bwd(pd, conv_w, al_vec, dt_vec, dqn, dkn, dvc, dgc, tm):
    t = pd.shape[0]

    def body(cur_ref, prev_ref, ba_ref, cw_ref, al_ref, dt_ref, dq_ref, dk_ref, dv_ref, dg_ref,
             dy_ref, dba_ref, accw_ref, accg_ref, xs_ref, dc_ref):
        @pl.when(pl.program_id(0) == 0)
        def _():
            accw_ref[...] = jnp.zeros_like(accw_ref)
            accg_ref[...] = jnp.zeros_like(accg_ref)

        _conv_window(cur_ref, prev_ref, xs_ref, tm)
        y = _conv_taps(xs_ref, cw_ref, tm)
        sg = _sigmoid(y)
        c = y * sg
        for h in range(DN_HEADS):
            qs = slice(h * DN_HEAD_DIM, (h + 1) * DN_HEAD_DIM)
            ksl = slice(DN_W + h * DN_HEAD_DIM, DN_W + (h + 1) * DN_HEAD_DIM)
            for src, sl, scale in ((dq_ref, qs, DN_SCALE), (dk_ref, ksl, 1.0)):
                xh = c[:, sl]
                r = lax.rsqrt(jnp.sum(xh * xh, axis=-1, keepdims=True) + EPS)
                unit = xh * r
                dn = src[:, qs] * scale
                dc_ref[:, sl] = r * (dn - unit * jnp.sum(dn * unit, axis=-1, keepdims=True))
        dc_ref[:, 2 * DN_W:3 * DN_W] = dv_ref[...]
        dy = dc_ref[...] * (sg + y * sg * (1.0 - sg))
        dy_ref[...] = dy
        for j in range(DN_CONV):
            accw_ref[j:j + 1, :] += jnp.sum(dy * xs_ref[pl.ds(HALO - DN_CONV + 1 + j, tm), :], axis=0, keepdims=True)

        beta, g, pre = _gate_values(ba_ref[...], al_ref[...], dt_ref[...])
        dgb = dg_ref[...]
        lane = lax.broadcasted_iota(jnp.int32, dgb.shape, 1)
        is_b, is_a = lane < DN_HEADS, (lane >= DN_HEADS) & (lane < 2 * DN_HEADS)
        dpre = dgb * (-jnp.exp(al_ref[...])) * _sigmoid(pre)
        dba_ref[...] = jnp.where(is_b, dgb * beta * (1.0 - beta), jnp.where(is_a, dpre, 0.0))
        accg_ref[0:1, :] += jnp.sum(jnp.where(is_a, dgb * g, 0.0), axis=0, keepdims=True)
        accg_ref[1:2, :] += jnp.sum(jnp.where(is_a, dpre, 0.0), axis=0, keepdims=True)

    tok = lambda w: pl.BlockSpec((tm, w), lambda i: (i, 0))
    return pl.pallas_call(
        body, name="dn_prep_bwd", grid=(t // tm,),
        in_specs=_dn_prep_specs(tm, t) + [tok(DN_W), tok(DN_W), tok(DN_W), tok(128)],
        out_specs=[tok(CONV_CH), tok(128), _full((8, CONV_CH)), _full((8, 128))],
        out_shape=[jax.ShapeDtypeStruct((t, CONV_CH), F32), jax.ShapeDtypeStruct((t, 128), F32),
                   jax.ShapeDtypeStruct((8, CONV_CH), F32), jax.ShapeDtypeStruct((8, 128), F32)],
        scratch_shapes=[pltpu.VMEM((HALO + tm, CONV_CH), F32), pltpu.VMEM((tm, CONV_CH), F32)],
        compiler_params=_params("arbitrary"),
    )(pd, pd, pd, conv_w, al_vec, dt_vec, dqn, dkn, dvc, dgc)


def _dn_conv_bwd(dy, dz, dba, conv_w, tm):
    t = dy.shape[0]
    nt = t // tm

    def body(cur_ref, nxt_ref, dz_ref, dba_ref, cw_ref, o_ref, ds_ref):
        nxt = jnp.where(pl.program_id(0) < nt - 1, nxt_ref[...], 0.0)
        ds_ref[0:tm, :] = cur_ref[...]
        ds_ref[tm:tm + HALO, :] = nxt
        dx = cw_ref[0:1, :] * ds_ref[pl.ds(DN_CONV - 1, tm), :]
        for j in range(1, DN_CONV):
            dx += cw_ref[j:j + 1, :] * ds_ref[pl.ds(DN_CONV - 1 - j, tm), :]
        o_ref[:, 0:CONV_CH] = dx.astype(o_ref.dtype)
        o_ref[:, CONV_CH:CONV_CH + DN_W] = dz_ref[...].astype(o_ref.dtype)
        o_ref[:, CONV_CH + DN_W:DN_COLS] = dba_ref[...].astype(o_ref.dtype)

    tok = lambda w: pl.BlockSpec((tm, w), lambda i: (i, 0))
    return pl.pallas_call(
        body, name="dn_conv_bwd", grid=(nt,),
        in_specs=[tok(CONV_CH),
                  pl.BlockSpec((HALO, CONV_CH), lambda i: (jnp.minimum((i + 1) * (tm // HALO), t // HALO - 1), 0)),
                  tok(DN_W), tok(128), _full((DN_CONV, CONV_CH))],
        out_specs=tok(DN_COLS),
        out_shape=jax.ShapeDtypeStruct((t, DN_COLS), MXU_DTYPE),
        scratch_shapes=[pltpu.VMEM((tm + HALO, CONV_CH), F32)],
        compiler_params=_params("parallel"),
    )(dy, dy, dz, dba, conv_w)


def _pad_lanes(v, offset=0):
    return jnp.zeros((1, 128), F32).at[0, offset:offset + v.shape[0]].set(v.astype(F32))


class _LocalReducer:
    def start(self, grads):
        return jnp.zeros((8, 128), F32)

    def middle(self, after):
        return jnp.zeros((8, 128), F32)

    def finish(self, after):
        return None


def _local_step(x, p, tgt, sm, w, late, reducer):
    t = x.shape[0]
    tm = min(512, t // 2)
    tm_s = min(256, t // 2)
    tw = min(1024, t // 2)

    w_in = w["w_in"]
    wa = w_in[:, :ATTN_Q + 2 * ATTN_KV]
    wd = jnp.pad(w_in[:, ATTN_Q + 2 * ATTN_KV:], ((0, 0), (0, DN_COLS - (D_IN - ATTN_Q - 2 * ATTN_KV))))
    conv_w = w["conv_w"]
    al_vec, dt_vec = _pad_lanes(sm["a_log"], DN_HEADS), _pad_lanes(sm["dt_bias"], DN_HEADS)
    sinks_vec = _pad_lanes(sm["sinks"])
    dn_w = sm["dn_norm"].reshape(1, 128)
    row = lambda v: v.reshape(1, D_MODEL)
    cos, sin = _rope_tables(t)

    u, pa, pd = _inproj(x, row(sm["norm_mix"]), wa, wd, tm_s)
    ao = _attn_fwd(pa, cos, sin, sinks_vec)
    qn, kn, vc, gc, gr = _dn_prep(pd, conv_w, al_vec, dt_vec, tm_s)
    uu, ww, qg, kd, a_qk, tinv, dlast = _dn_intra(qn, kn, vc, gc, gr)
    dn_out, o, vnew, sall = _dn_scan_fwd(uu, ww, qg, kd, a_qk, dlast, pd, dn_w)
    w = dict(w, **late(dn_out))
    wo_a, wo_d = w["w_o"][:ATTN_Q], w["w_o"][ATTN_Q:]
    w_proj = jnp.transpose(w["w_proj4"], (1, 0, 2)).reshape(PLE_DIM, D_MODEL)
    h1 = _oproj(x, ao, dn_out, wo_a, wo_d, tm)
    m, r, h2 = _mlp_fwd(h1, row(sm["norm_mlp"]), w["w_up4"], w["w_down"], tw)
    dh2, dh2b, dgp, dpp, n3, pb, acc_ple = _ple_loss(h2, p, tgt, row(sm["norm_ple"]), row(sm["norm_final"]),
                                                     w["w_gate"], w_proj, tm_s)
    g_w_gate = _wgrad(n3, dgp, "wgrad_gate", D_MODEL, D_MODEL, tw)
    g_w_proj = _wgrad(pb, dpp, "wgrad_proj", PLE_DIM, D_MODEL, tw)
    da, dh1, dh1b, acc_mlp = _mlp_bwd(dh2, dh2b, r, h1, row(sm["norm_mlp"]), w["w_up4"], w["w_down"], tm)
    g_w_up4 = _wgrad(m, da, "wgrad_up", D_MODEL, FF_BLOCK, tw, stacked=True)
    g_w_down = _wgrad(r, dh2b, "wgrad_down", FF_BLOCK, D_MODEL, tw,
                      prep=lambda rv: jnp.square(rv.astype(F32)).astype(MXU_DTYPE))
    g_w_o = jnp.concatenate([_wgrad(ao, dh1b, "wgrad_oa", ATTN_Q, D_MODEL, tw),
                             _wgrad(dn_out, dh1b, "wgrad_od", DN_W, D_MODEL, tw)], axis=0)
    early = dict(w_up4=g_w_up4, w_down=g_w_down, w_gate=g_w_gate, w_proj=g_w_proj, w_o=g_w_o)
    dep = reducer.start(early)
    dao, ddn = _oproj_bwd(dh1b, wo_a, wo_d, tm, dep)
    dz, du, dw, dqg, dkd, da_qk, ddlast, acc_dn = _dn_scan_bwd(ddn, o, vnew, sall, ww, qg, kd, a_qk, dlast, pd, dn_w)
    dep = reducer.middle(du)
    dqn, dkn, dvc, dgc = _dn_intra_bwd(qn, kn, vc, gc, gr, tinv, a_qk, du, dw, dqg, dkd, da_qk, ddlast, dlast, dep)
    dy, dba, acc_conv, acc_gate = _dn_prep_bwd(pd, conv_w, al_vec, dt_vec, dqn, dkn, dvc, dgc, tm_s)
    d_dn = _dn_conv_bwd(dy, dz, dba, conv_w, tm_s)
    dq, dk, dv, acc_attn = _attn_bwd(pa, cos, sin, sinks_vec, dao)
    reducer.finish(dq)
    wq, wk, wv = wa[:, :ATTN_Q], wa[:, ATTN_Q:ATTN_Q + ATTN_KV], wa[:, ATTN_Q + ATTN_KV:]
    dx, acc_mix = _inproj_bwd(x, dh1, row(sm["norm_mix"]), [dq, dk, dv, d_dn], [wq, wk, wv, wd], tm_s)

    g_w_in = jnp.concatenate([
        _wgrad(u, dq, "wgrad_q", D_MODEL, ATTN_Q, tw), _wgrad(u, dk, "wgrad_k", D_MODEL, ATTN_KV, tw),
        _wgrad(u, dv, "wgrad_v", D_MODEL, ATTN_KV, tw),
        _wgrad(u, d_dn, "wgrad_dn", D_MODEL, DN_COLS, tw)[:, :D_IN - ATTN_Q - 2 * ATTN_KV]], axis=1)
    grads = dict(early, w_in=g_w_in)
    sums = dict(loss=acc_ple[2, 0], norm_final=acc_ple[0], norm_ple=acc_ple[1], norm_mlp=acc_mlp[0], norm_mix=acc_mix[0],
                dn_norm=acc_dn[0], sinks=acc_attn[0, :ATTN_HEADS], a_log=acc_gate[0, DN_HEADS:2 * DN_HEADS],
                dt_bias=acc_gate[1, DN_HEADS:2 * DN_HEADS], conv_w=acc_conv[:DN_CONV])
    return sums, dx, grads


MESH = pl.DeviceIdType.MESH
ANY = pl.BlockSpec(memory_space=pl.ANY)
N_CHIPS = 4
N_DEV = 8


def _place():
    x, y, c = lax.axis_index("x"), lax.axis_index("y"), lax.axis_index("c")
    chips = [(1 - x, y), (x, 1 - y), (1 - x, 1 - y)]
    return x, y, c, chips


def _gather_weights(shards, conv_s):
    n = len(shards)
    per = 7

    def body(*refs):
        in_refs, conv_ref = refs[:n], refs[n]
        out_refs, conv_out = refs[n + 1:2 * n + 1], refs[2 * n + 1]
        send_sems, recv_sems = refs[2 * n + 2:]
        x, y, c, chips = _place()
        sibling = (x, y, 1 - c)

        def blk(a, px, py, pc):
            hr = in_refs[a].shape[0] // 2
            return out_refs[a].at[2 * px + py, pl.ds(pc * hr, hr), :]

        def mine(a):
            hr = in_refs[a].shape[0] // 2
            return in_refs[a].at[pl.ds(c * hr, hr), :]

        def rcopy(a, k, block, to, src=None):
            return pltpu.make_async_remote_copy(
                src_ref=blk(a, *block) if src is None else src, dst_ref=blk(a, *block),
                send_sem=send_sems.at[per * a + k], recv_sem=recv_sems.at[per * a + k],
                device_id=to, device_id_type=MESH)

        def whole(a, to):
            return pltpu.make_async_remote_copy(
                src_ref=in_refs[a], dst_ref=out_refs[a].at[2 * x + y],
                send_sem=send_sems.at[per * a], recv_sem=recv_sems.at[per * a], device_id=to, device_id_type=MESH)

        def ccopy(j, to):
            return pltpu.make_async_remote_copy(
                src_ref=conv_ref, dst_ref=conv_out.at[2 * x + y],
                send_sem=send_sems.at[per * n + j], recv_sem=recv_sems.at[per * n + j],
                device_id=to, device_id_type=MESH)

        started = []
        for a in range(n):
            first = [whole(a, sibling)]
            first += [rcopy(a, 1 + j, (x, y, c), (*chip, c), src=mine(a)) for j, chip in enumerate(chips)]
            for cp in first:
                cp.start()
            started += first
        conv_sends = [ccopy(j, (*chip, c)) for j, chip in enumerate(chips)] + [ccopy(3, sibling)]
        for cp in conv_sends:
            cp.start()
        started += conv_sends
        for a in range(n):
            for j, chip in enumerate(chips):
                rcopy(a, 1 + j, (*chip, c), (x, y, c)).wait_recv()
                fwd = rcopy(a, 4 + j, (*chip, c), sibling)
                fwd.start()
                started.append(fwd)
        for a in range(n):
            whole(a, sibling).wait_recv()
            for j, chip in enumerate(chips):
                rcopy(a, 4 + j, (*chip, 1 - c), (x, y, c)).wait_recv()
        for j, chip in enumerate(chips + [(x, y)]):
            pltpu.make_async_remote_copy(
                src_ref=conv_ref, dst_ref=conv_out.at[2 * chip[0] + chip[1]],
                send_sem=send_sems.at[per * n + j], recv_sem=recv_sems.at[per * n + j],
                device_id=sibling, device_id_type=MESH).wait_recv()
        for cp in started:
            cp.wait_send()

    nsem = per * n + 4
    out_shape = [jax.ShapeDtypeStruct((N_CHIPS,) + s.shape, s.dtype) for s in shards]
    out_shape.append(jax.ShapeDtypeStruct((N_CHIPS,) + conv_s.shape, conv_s.dtype))
    return pl.pallas_call(
        body, name="gather_weights", in_specs=[ANY] * (n + 1), out_specs=[ANY] * (n + 1), out_shape=out_shape,
        scratch_shapes=[pltpu.SemaphoreType.DMA((nsem,)), pltpu.SemaphoreType.DMA((nsem,))],
    )(*shards, conv_s)


HBM = pl.BlockSpec(memory_space=pltpu.HBM)
SEM = pl.BlockSpec(memory_space=pltpu.SEMAPHORE)
EFFECT = pltpu.SideEffectType.DATAFLOW_SIDE_EFFECTING
LATE_COPIES = 7


def _late_copies(in_refs, land_refs, send_sems, recv_sems):
    x, y, c, chips = _place()
    sends, arrivals = [], []
    for a, (src, land) in enumerate(zip(in_refs, land_refs)):
        hr = src.shape[0] // 2
        base = LATE_COPIES * a

        def cp(src_ref, dst_ref, s_idx, r_idx, to):
            return pltpu.make_async_remote_copy(src_ref=src_ref, dst_ref=dst_ref, send_sem=send_sems.at[base + s_idx],
                                                recv_sem=recv_sems.at[base + r_idx], device_id=to, device_id_type=MESH)

        sends.append(cp(src, land.at[2 * x + y], 0, 0, (x, y, 1 - c)))
        arrivals.append(cp(src, land.at[2 * x + y], 0, 0, (x, y, 1 - c)))
        for j, chip in enumerate(chips):
            for pc in range(2):
                half = src.at[pl.ds(c * hr, hr), :]
                sends.append(cp(half, land.at[2 * x + y, pl.ds(c * hr, hr), :], 1 + 2 * j + pc, 1 + 2 * j + c, (*chip, pc)))
                arrivals.append(cp(half, land.at[2 * chip[0] + chip[1], pl.ds(pc * hr, hr), :], 1 + 2 * j + pc,
                                   1 + 2 * j + pc, (*chip, pc)))
    return sends, arrivals


def _copies_start(name, build, nsem, srcs, land_shapes, after):
    n = len(srcs)

    def body(*refs):
        sends, _ = build(refs[:n], refs[n:2 * n], refs[2 * n + 1], refs[2 * n + 2])
        for cp in sends:
            cp.start()
        refs[-1][...] = jnp.zeros_like(refs[-1])

    lands = [pltpu.with_memory_space_constraint(lax.empty(s.shape, s.dtype), pltpu.HBM) for s in land_shapes]
    ins = [pltpu.with_memory_space_constraint(s, pltpu.HBM) for s in srcs]
    out = pl.pallas_call(
        body, name=name,
        out_shape=(pltpu.SemaphoreType.DMA((nsem,)), pltpu.SemaphoreType.DMA((nsem,)),
                   *[pltpu.HBM(s.shape, s.dtype) for s in srcs], *[pltpu.HBM(s.shape, s.dtype) for s in land_shapes],
                   jax.ShapeDtypeStruct((8, 128), F32)),
        in_specs=[HBM] * (2 * n) + [ANY],
        out_specs=(SEM, SEM, *[HBM] * (2 * n), pl.BlockSpec(memory_space=pltpu.VMEM)),
        input_output_aliases={i: 2 + i for i in range(2 * n)},
        compiler_params=pltpu.CompilerParams(has_side_effects=EFFECT),
    )(*ins, *lands, after)
    return out[0], out[1], out[2:2 + n], out[2 + n:2 + 2 * n], out[-1]


def _copies_wait(name, build, started, after):
    send_sems, recv_sems, srcs, lands, _ = started
    n = len(srcs)

    def body(*refs):
        sends, arrivals = build(refs[:n], refs[n:2 * n], refs[2 * n], refs[2 * n + 1])
        for cp in sends:
            cp.wait_send()
        for cp in arrivals:
            cp.wait_recv()

    out = pl.pallas_call(
        body, name=name,
        out_shape=(*[pltpu.HBM(s.shape, s.dtype) for s in srcs], *[pltpu.HBM(l.shape, l.dtype) for l in lands]),
        in_specs=[HBM] * (2 * n) + [SEM, SEM, ANY],
        out_specs=tuple([HBM] * (2 * n)),
        input_output_aliases={i: i for i in range(2 * n)},
        compiler_params=pltpu.CompilerParams(has_side_effects=EFFECT),
    )(*srcs, *lands, send_sems, recv_sems, after)
    return out[:n], out[n:]


def _exchange_copies(g_refs, got_refs, send_sems, recv_sems):
    x, y, c, _ = _place()
    sends, arrivals = [], []
    for a, (g, got) in enumerate(zip(g_refs, got_refs)):
        hr = g.shape[1] // 2
        cp = pltpu.make_async_remote_copy(
            src_ref=g.at[:, pl.ds((1 - c) * hr, hr), :], dst_ref=got, send_sem=send_sems.at[a],
            recv_sem=recv_sems.at[a], device_id=(x, y, 1 - c), device_id_type=MESH)
        sends.append(cp)
        arrivals.append(cp)
    return sends, arrivals


def _scatter_copies(s_refs, got_refs, send_sems, recv_sems):
    x, y, c, chips = _place()
    sends, arrivals = [], []
    for a, (s16, got) in enumerate(zip(s_refs, got_refs)):
        for j, chip in enumerate(chips):
            cp = pltpu.make_async_remote_copy(
                src_ref=s16.at[2 * chip[0] + chip[1]], dst_ref=got.at[j], send_sem=send_sems.at[3 * a + j],
                recv_sem=recv_sems.at[3 * a + j], device_id=(*chip, c), device_id_type=MESH)
            sends.append(cp)
            arrivals.append(cp)
    return sends, arrivals


def _exchange_halves(grads):
    n = len(grads)

    def body(*refs):
        g_refs, got_refs = refs[:n], refs[n:2 * n]
        send_sems, recv_sems = refs[2 * n:]
        x, y, c, _ = _place()
        remote = []
        for a in range(n):
            hr = g_refs[a].shape[1] // 2
            remote.append(pltpu.make_async_remote_copy(
                src_ref=g_refs[a].at[:, pl.ds((1 - c) * hr, hr), :], dst_ref=got_refs[a],
                send_sem=send_sems.at[a], recv_sem=recv_sems.at[a], device_id=(x, y, 1 - c), device_id_type=MESH))
        for cp in remote:
            cp.start()
        for cp in remote:
            cp.wait_recv()
        for cp in remote:
            cp.wait_send()

    half = [jax.ShapeDtypeStruct((g.shape[0], g.shape[1] // 2, g.shape[2]), g.dtype) for g in grads]
    return pl.pallas_call(
        body, name="exchange_halves", in_specs=[ANY] * n, out_specs=[ANY] * n, out_shape=half,
        scratch_shapes=[pltpu.SemaphoreType.DMA((n,)), pltpu.SemaphoreType.DMA((n,))],
    )(*grads)


def _scatter_to_chips(sums16):
    n = len(sums16)

    def body(*refs):
        s16, got_refs = refs[:n], refs[n:2 * n]
        send_sems, recv_sems = refs[2 * n:]
        x, y, c, chips = _place()
        remote = []
        for a in range(n):
            for j, chip in enumerate(chips):
                remote.append(pltpu.make_async_remote_copy(
                    src_ref=s16[a].at[2 * chip[0] + chip[1]], dst_ref=got_refs[a].at[j],
                    send_sem=send_sems.at[3 * a + j], recv_sem=recv_sems.at[3 * a + j],
                    device_id=(*chip, c), device_id_type=MESH))
        for cp in remote:
            cp.start()
        for cp in remote:
            cp.wait_recv()
        for cp in remote:
            cp.wait_send()

    got = [jax.ShapeDtypeStruct((3,) + s.shape[1:], BF16) for s in sums16]
    return pl.pallas_call(
        body, name="scatter_to_chips", in_specs=[ANY] * n, out_specs=[ANY] * n, out_shape=got,
        scratch_shapes=[pltpu.SemaphoreType.DMA((3 * n,)), pltpu.SemaphoreType.DMA((3 * n,))],
    )(*sums16)


def _share_halves(bufs):
    n = len(bufs)

    def body(*refs):
        out_refs = refs[n:2 * n]
        send_sems, recv_sems = refs[2 * n:]
        x, y, c, _ = _place()
        remote = [pltpu.make_async_remote_copy(
            src_ref=out_refs[a].at[c], dst_ref=out_refs[a].at[c], send_sem=send_sems.at[a], recv_sem=recv_sems.at[a],
            device_id=(x, y, 1 - c), device_id_type=MESH) for a in range(n)]
        for cp in remote:
            cp.start()
        for a in range(n):
            pltpu.make_async_remote_copy(
                src_ref=out_refs[a].at[c], dst_ref=out_refs[a].at[1 - c], send_sem=send_sems.at[a],
                recv_sem=recv_sems.at[a], device_id=(x, y, 1 - c), device_id_type=MESH).wait_recv()
        for cp in remote:
            cp.wait_send()

    return pl.pallas_call(
        body, name="share_halves", in_specs=[ANY] * n, out_specs=[ANY] * n,
        out_shape=[jax.ShapeDtypeStruct(b.shape, b.dtype) for b in bufs],
        input_output_aliases={a: a for a in range(n)},
        scratch_shapes=[pltpu.SemaphoreType.DMA((n,)), pltpu.SemaphoreType.DMA((n,))],
    )(*bufs)


SMALL_ROWS, SMALL_COLS = 16, CONV_CH


def _allreduce_small(block):
    m_per, ncol = block.shape

    def body(x_ref, sum_ref, all_ref, send_sems, recv_sems, local_sem):
        x, y, c, chips = _place()
        me, sibling = (x, y, c), (x, y, 1 - c)

        def rows(px, py, pc):
            return all_ref.at[pl.ds((4 * px + 2 * py + pc) * m_per, m_per), :]

        def copy(k, block_of, to, src=None):
            return pltpu.make_async_remote_copy(
                src_ref=rows(*block_of) if src is None else src, dst_ref=rows(*block_of),
                send_sem=send_sems.at[k], recv_sem=recv_sems.at[k], device_id=to, device_id_type=MESH)

        mine = pltpu.make_async_copy(x_ref, rows(*me), local_sem)
        mine.start()
        first = [copy(0, me, sibling, src=x_ref)]
        first += [copy(1 + j, me, (*chip, c), src=x_ref) for j, chip in enumerate(chips)]
        for cp in first:
            cp.start()
        passed = [copy(4 + j, (*chip, c), sibling) for j, chip in enumerate(chips)]
        for j, chip in enumerate(chips):
            copy(1 + j, (*chip, c), me).wait_recv()
            passed[j].start()
        copy(0, sibling, me).wait_recv()
        for j, chip in enumerate(chips):
            copy(4 + j, (*chip, 1 - c), me).wait_recv()
        for cp in first + passed:
            cp.wait_send()
        mine.wait()
        total = all_ref[0:m_per, :]
        for d in range(1, N_DEV):
            total = total + all_ref[d * m_per:(d + 1) * m_per, :]
        sum_ref[...] = total

    vm = pl.BlockSpec(memory_space=pltpu.VMEM)
    return pl.pallas_call(
        body, name="allreduce_small", in_specs=[vm], out_specs=vm,
        out_shape=jax.ShapeDtypeStruct((m_per, ncol), F32),
        scratch_shapes=[pltpu.VMEM((N_DEV * m_per, ncol), F32), pltpu.SemaphoreType.DMA((7,)),
                        pltpu.SemaphoreType.DMA((7,)), pltpu.SemaphoreType.DMA],
    )(block)


def _row_tile(rows, cols):
    tile = rows
    while tile * cols * 4 > (1 << 20) and tile % 16 == 0:
        tile //= 2
    return tile


def _elementwise(fn, name, ins, out_dtypes):
    rows, cols = ins[0].shape
    tile = _row_tile(rows, cols)

    def body(*refs):
        outs = fn(*[r[...] for r in refs[:len(ins)]])
        for o_ref, o in zip(refs[len(ins):], outs):
            o_ref[...] = o.astype(o_ref.dtype)

    spec = pl.BlockSpec((tile, cols), lambda i: (i, 0))
    return pl.pallas_call(
        body, name=name, grid=(rows // tile,), in_specs=[spec] * len(ins), out_specs=[spec] * len(out_dtypes),
        out_shape=[jax.ShapeDtypeStruct((rows, cols), d) for d in out_dtypes],
        compiler_params=_params("parallel"),
    )(*ins)


def _adamw_tile(w, g, m, v):
    m = ADAM_B1 * m + (1.0 - ADAM_B1) * g
    v = ADAM_B2 * v + (1.0 - ADAM_B2) * jnp.square(g)
    m_hat = m / (1.0 - ADAM_B1 ** ADAM_STEP)
    v_hat = v / (1.0 - ADAM_B2 ** ADAM_STEP)
    delta = -ADAM_LR * (m_hat / (jnp.sqrt(v_hat) + ADAM_EPS) + ADAM_WD * w)
    return delta, m, v


def _adamw(name, w, g, m, v):
    return _elementwise(_adamw_tile, name, [w, g, m, v], [F32, F32, F32])


def _chip_sum(name, g4, got, place):
    nchip, hr, cols = got.shape
    tile = _row_tile(hr, cols)
    nblk = hr // tile

    def body(pl_ref, g_ref, o_ref, s32_ref, s16_ref):
        s = g_ref[...] + o_ref[...]
        s32_ref[...] = s
        s16_ref[...] = s.astype(BF16)

    spec = pl.BlockSpec((None, tile, cols), lambda k, i, pr: (k, i, 0))
    return pl.pallas_call(
        body, name=name,
        grid_spec=pltpu.PrefetchScalarGridSpec(
            num_scalar_prefetch=1, grid=(nchip, nblk),
            in_specs=[pl.BlockSpec((None, tile, cols), lambda k, i, pr: (k, pr[1] * nblk + i, 0)), spec],
            out_specs=[spec, spec]),
        out_shape=[jax.ShapeDtypeStruct(got.shape, F32), jax.ShapeDtypeStruct(got.shape, BF16)],
        compiler_params=_params("parallel", "parallel"),
    )(place, g4, got)


def _mesh_sum(name, s32, got, place):
    _, hr, cols = s32.shape
    tile = _row_tile(hr, cols)

    def body(pl_ref, own_ref, g0_ref, g1_ref, g2_ref, o_ref):
        o_ref[...] = ((own_ref[...] + g0_ref[...].astype(F32)) + g1_ref[...].astype(F32)) + g2_ref[...].astype(F32)

    slab = lambda j: pl.BlockSpec((None, tile, cols), lambda i, pr: (j, i, 0))
    return pl.pallas_call(
        body, name=name,
        grid_spec=pltpu.PrefetchScalarGridSpec(
            num_scalar_prefetch=1, grid=(hr // tile,),
            in_specs=[pl.BlockSpec((None, tile, cols), lambda i, pr: (pr[0], i, 0)), slab(0), slab(1), slab(2)],
            out_specs=pl.BlockSpec((None, tile, cols), lambda i, pr: (pr[1], i, 0))),
        out_shape=jax.ShapeDtypeStruct((2, hr, cols), F32),
        compiler_params=_params("parallel"),
    )(place, s32, got, got, got)


def _reduce_scatter(grads):
    names = list(grads)
    place = _place_operand()
    got_a = _exchange_halves([grads[k] for k in names])
    sums = [_chip_sum("chip_sum_" + k, grads[k], g, place) for k, g in zip(names, got_a)]
    got_b = _scatter_to_chips([s[1] for s in sums])
    return {k: _mesh_sum("mesh_sum_" + k, s[0], g, place) for k, s, g in zip(names, sums, got_b)}


def _place_operand():
    return jnp.stack([2 * lax.axis_index("x") + lax.axis_index("y"), lax.axis_index("c")]).astype(jnp.int32)


def _per_chip(name, g):
    if name == "w_in":
        return jnp.transpose(g.reshape(D_MODEL, N_CHIPS, D_IN // N_CHIPS), (1, 0, 2))
    if name == "w_proj":
        return jnp.transpose(g.reshape(PLE_DIM, N_CHIPS, D_MODEL // N_CHIPS), (1, 0, 2))
    if name == "w_up4":
        return g
    return g.reshape(N_CHIPS, g.shape[0] // N_CHIPS, g.shape[1])


class _EarlyReducer:
    def start(self, grads):
        self.names = list(grads)
        self.place = _place_operand()
        slabs = [_per_chip(k, grads[k]) for k in self.names]
        halves = [jax.ShapeDtypeStruct((s.shape[0], s.shape[1] // 2, s.shape[2]), F32) for s in slabs]
        self.a = _copies_start("exchange_start", _exchange_copies, len(slabs), slabs, halves, slabs[0])
        return self.a[-1]

    def middle(self, after):
        slabs, got = _copies_wait("exchange_wait", _exchange_copies, self.a, after)
        self.sums = [_chip_sum("early_chip_sum_" + k, s, g, self.place) for k, s, g in zip(self.names, slabs, got)]
        s16 = [s[1] for s in self.sums]
        lands = [jax.ShapeDtypeStruct((3,) + s.shape[1:], BF16) for s in s16]
        self.b = _copies_start("scatter_start", _scatter_copies, 3 * len(s16), s16, lands, s16[0])
        return self.b[-1]

    def finish(self, after):
        _, got = _copies_wait("scatter_wait", _scatter_copies, self.b, after)
        self.bufs = {k: _mesh_sum("early_mesh_sum_" + k, s[0], g, self.place)
                     for k, s, g in zip(self.names, self.sums, got)}


def kernel(x, p, norm_mix, w_in, conv_w, a_log, dt_bias, dn_norm, sinks, w_o, norm_mlp, w_up, w_down, norm_ple, w_ple_gate, w_ple_proj, norm_final, loss_target, m_norm_mix, m_w_in, m_conv_w, m_a_log, m_dt_bias, m_dn_norm, m_sinks, m_w_o, m_norm_mlp, m_w_up, m_w_down, m_norm_ple, m_w_ple_gate, m_w_ple_proj, m_norm_final, v_norm_mix, v_w_in, v_conv_w, v_a_log, v_dt_bias, v_dn_norm, v_sinks, v_w_o, v_norm_mlp, v_w_up, v_w_down, v_norm_ple, v_w_ple_gate, v_w_ple_proj, v_norm_final):
    chip = 2 * lax.axis_index("x") + lax.axis_index("y")
    big = dict(w_in=w_in[0], w_o=w_o[0], w_up=w_up[0], w_down=w_down[0], w_gate=w_ple_gate[0], w_proj=w_ple_proj[0])
    big_m = dict(w_in=m_w_in[0], w_o=m_w_o[0], w_up=m_w_up[0], w_down=m_w_down[0], w_gate=m_w_ple_gate[0], w_proj=m_w_ple_proj[0])
    big_v = dict(w_in=v_w_in[0], w_o=v_w_o[0], w_up=v_w_up[0], w_down=v_w_down[0], w_gate=v_w_ple_gate[0], w_proj=v_w_ple_proj[0])
    names = list(big)

    w_in_all, conv_all = _gather_weights([big["w_in"].astype(BF16)], conv_w[0])
    late_names = names[1:]
    late_shards = [big[k].astype(BF16) for k in late_names]
    gather = _copies_start("gather_start", _late_copies, LATE_COPIES * len(late_shards), late_shards,
                           [jax.ShapeDtypeStruct((N_CHIPS,) + s.shape, BF16) for s in late_shards], w_in_all)
    token = gather[-1]
    w = dict(w_in=jnp.transpose(w_in_all, (1, 0, 2)).reshape(D_MODEL, D_IN),
             conv_w=jnp.transpose(conv_all, (1, 0, 2)).reshape(DN_CONV, CONV_CH))
    sm = dict(norm_mix=norm_mix[0] + token[0, 0], a_log=a_log[0], dt_bias=dt_bias[0], dn_norm=dn_norm[0],
              sinks=sinks[0], norm_mlp=norm_mlp[0], norm_ple=norm_ple[0], norm_final=norm_final)

    def late(after):
        gw = dict(zip(late_names, _copies_wait("gather_wait", _late_copies, gather, after)[1]))
        return dict(w_o=gw["w_o"].reshape(D_MODEL, D_MODEL), w_up4=gw["w_up"], w_down=gw["w_down"].reshape(D_FF, D_MODEL),
                    w_gate=gw["w_gate"].reshape(D_MODEL, D_MODEL), w_proj4=gw["w_proj"])

    reducer = _EarlyReducer()
    sums, grad_x, g = _local_step(x[0], p[0, 0], loss_target[0], sm, w, late, reducer)

    bufs = dict(reducer.bufs, **_reduce_scatter({"w_in": _per_chip("w_in", g["w_in"])}))
    grad_key = dict(w_in="w_in", w_o="w_o", w_up="w_up4", w_down="w_down", w_gate="w_gate", w_proj="w_proj")
    full = _share_halves([bufs[grad_key[k]] for k in names])
    red = {k: f.reshape(-1, f.shape[-1]) for k, f in zip(names, full)}

    row = lambda v: jnp.zeros((SMALL_COLS,), F32).at[:v.shape[0]].set(v)
    misc = jnp.zeros((SMALL_COLS,), F32).at[0:4].set(sums["a_log"]).at[4:8].set(sums["dt_bias"]) \
        .at[8:16].set(sums["sinks"]).at[128:256].set(sums["dn_norm"]).at[256].set(sums["loss"])
    small = jnp.concatenate([sums["conv_w"], jnp.stack([row(sums["norm_mix"]), row(sums["norm_mlp"]), row(sums["norm_ple"]),
                                                        row(sums["norm_final"]), misc]),
                             jnp.zeros((SMALL_ROWS - 9, SMALL_COLS), F32)], axis=0)
    tot = _allreduce_small(small)
    loss = tot[8, 256]
    ncw = CONV_CH // N_CHIPS

    def pack(cw, nmix, nmlp, nple, nfin, al, dtb, sk, dnn):
        misc_p = jnp.zeros((SMALL_COLS,), F32).at[0:4].set(al).at[4:8].set(dtb).at[8:16].set(sk).at[128:256].set(dnn)
        cw_p = jnp.zeros((DN_CONV, SMALL_COLS), F32).at[:, :ncw].set(cw)
        return jnp.concatenate([cw_p, jnp.stack([row(nmix), row(nmlp), row(nple), row(nfin), misc_p]),
                                jnp.zeros((SMALL_ROWS - 9, SMALL_COLS), F32)], axis=0)

    def unpack(buf):
        return dict(conv_w=buf[0:4, :ncw][None], norm_mix=buf[4, :D_MODEL][None], norm_mlp=buf[5, :D_MODEL][None],
                    norm_ple=buf[6, :D_MODEL][None], norm_final=buf[7, :D_MODEL], a_log=buf[8, 0:4][None],
                    dt_bias=buf[8, 4:8][None], sinks=buf[8, 8:16][None], dn_norm=buf[8, 128:256][None])

    g_conv_shard = lax.dynamic_slice(tot[0:4], (0, chip * ncw), (DN_CONV, ncw))
    g_small = pack(g_conv_shard, tot[4, :D_MODEL], tot[5, :D_MODEL], tot[6, :D_MODEL], tot[7, :D_MODEL],
                   tot[8, 0:4], tot[8, 4:8], tot[8, 8:16], tot[8, 128:256])
    w_small = pack(conv_w[0], norm_mix[0], norm_mlp[0], norm_ple[0], norm_final, a_log[0], dt_bias[0], sinks[0], dn_norm[0])
    m_small = pack(m_conv_w[0], m_norm_mix[0], m_norm_mlp[0], m_norm_ple[0], m_norm_final, m_a_log[0], m_dt_bias[0],
                   m_sinks[0], m_dn_norm[0])
    v_small = pack(v_conv_w[0], v_norm_mix[0], v_norm_mlp[0], v_norm_ple[0], v_norm_final, v_a_log[0], v_dt_bias[0],
                   v_sinks[0], v_dn_norm[0])

    d_s, m_s, v_s = (unpack(b) for b in _adamw("adamw_small", w_small, g_small, m_small, v_small))
    g_s = unpack(g_small)
    out_g, out_d, out_m, out_v = dict(g_s), dict(d_s), dict(m_s), dict(v_s)
    ref_name = dict(w_in="w_in", w_o="w_o", w_up="w_up", w_down="w_down", w_gate="w_ple_gate", w_proj="w_ple_proj")
    for k in names:
        d_k, m_k, v_k = _adamw("adamw_" + k, big[k], red[k], big_m[k], big_v[k])
        out_g[ref_name[k]], out_d[ref_name[k]] = red[k][None], d_k[None]
        out_m[ref_name[k]], out_v[ref_name[k]] = m_k[None], v_k[None]
    order = ["norm_mix", "w_in", "conv_w", "a_log", "dt_bias", "dn_norm", "sinks", "w_o", "norm_mlp", "w_up", "w_down",
             "norm_ple", "w_ple_gate", "w_ple_proj", "norm_final"]
    return (loss, grad_x[None], *[out_g[k] for k in order], *[out_d[k] for k in order],
            *[out_m[k] for k in order], *[out_v[k] for k in order])
```

```python
import functools

import jax
import jax.numpy as jnp
from jax import lax
from jax.experimental import pallas as pl
from jax.experimental.pallas import tpu as pltpu

F32 = jnp.float32
BF16 = jnp.bfloat16
MXU_DTYPE = jnp.bfloat16
HI = lax.Precision.HIGHEST

D_MODEL = 1024
PLE_DIM = 256
ATTN_HEADS = 8
ATTN_KV_HEADS = 2
ATTN_GROUPS = ATTN_HEADS // ATTN_KV_HEADS
ATTN_HEAD_DIM = 64
ATTN_BLOCK = 128
ROPE_THETA = 10000.0
DN_HEADS = 4
DN_HEAD_DIM = 128
DN_CONV = 4
DN_CHUNK = 64
D_FF = 4 * D_MODEL
EPS = 1e-6
ATTN_Q = ATTN_HEADS * ATTN_HEAD_DIM
ATTN_KV = ATTN_KV_HEADS * ATTN_HEAD_DIM
DN_W = DN_HEADS * DN_HEAD_DIM
CONV_CH = 3 * DN_W
D_IN = ATTN_Q + 2 * ATTN_KV + 4 * DN_W + 2 * DN_HEADS
DN_COLS = 4 * DN_W + 128
DN_SCALE = DN_HEAD_DIM ** -0.5
ATTN_SCALE = ATTN_HEAD_DIM ** -0.5
FF_BLOCKS = 4
FF_BLOCK = D_FF // FF_BLOCKS

ADAM_LR = 0.001
ADAM_B1 = 0.9
ADAM_B2 = 0.999
ADAM_EPS = 1e-08
ADAM_WD = 0.01
ADAM_STEP = 10

V7X_VMEM_BYTES = 64 * 1024 * 1024
VMEM_LIMIT = 48 * 1024 * 1024

NN = ((1,), (0,))
NT = ((1,), (1,))
TN = ((0,), (0,))


def _dot(a, b, dims=NN, prec=None):
    return lax.dot_general(a, b, (dims, ((), ())), precision=prec, preferred_element_type=F32)


def _sigmoid(x):
    return 1.0 / (1.0 + jnp.exp(-x))


def _softplus(x):
    return jnp.maximum(x, 0.0) + jnp.log(1.0 + jnp.exp(-jnp.abs(x)))


def _params(*sem):
    return pltpu.CompilerParams(dimension_semantics=sem, vmem_limit_bytes=VMEM_LIMIT)


def _rms_fwd(xv, g):
    r = lax.rsqrt(jnp.mean(xv * xv, axis=-1, keepdims=True) + EPS)
    return xv * r * g


def _rms_bwd(xv, g, dn):
    r = lax.rsqrt(jnp.mean(xv * xv, axis=-1, keepdims=True) + EPS)
    xh = xv * r
    dg = jnp.sum(dn * xh, axis=0, keepdims=True)
    dxh = dn * g
    dx = r * (dxh - xh * jnp.mean(dxh * xh, axis=-1, keepdims=True))
    return dx, dg


def _full(shape):
    return pl.BlockSpec(shape, lambda *_: (0,) * len(shape))


def _inproj(x, g_mix, wa, wd, tm):
    t = x.shape[0]

    def body(x_ref, g_ref, wa_ref, wd_ref, u_ref, pa_ref, pd_ref):
        u = _rms_fwd(x_ref[...], g_ref[...]).astype(MXU_DTYPE)
        u_ref[...] = u
        pa_ref[...] = _dot(u, wa_ref[...])
        pd_ref[...] = _dot(u, wd_ref[...])

    na, nd = wa.shape[1], wd.shape[1]
    return pl.pallas_call(
        body, name="inproj", grid=(t // tm,),
        in_specs=[pl.BlockSpec((tm, D_MODEL), lambda i: (i, 0)), _full((1, D_MODEL)),
                  _full((D_MODEL, na)), _full((D_MODEL, nd))],
        out_specs=[pl.BlockSpec((tm, D_MODEL), lambda i: (i, 0)), pl.BlockSpec((tm, na), lambda i: (i, 0)),
                   pl.BlockSpec((tm, nd), lambda i: (i, 0))],
        out_shape=[jax.ShapeDtypeStruct((t, D_MODEL), MXU_DTYPE), jax.ShapeDtypeStruct((t, na), F32),
                   jax.ShapeDtypeStruct((t, nd), F32)],
        compiler_params=_params("parallel"),
    )(x, g_mix, wa, wd)


def _oproj(x, ao, dn, wo_a, wo_d, tm):
    t = x.shape[0]

    def body(x_ref, ao_ref, dn_ref, wa_ref, wd_ref, h_ref):
        h_ref[...] = (x_ref[...] + _dot(ao_ref[...].astype(MXU_DTYPE), wa_ref[...])
                      + _dot(dn_ref[...].astype(MXU_DTYPE), wd_ref[...]))

    half = ao.shape[1]
    return pl.pallas_call(
        body, name="oproj", grid=(t // tm,),
        in_specs=[pl.BlockSpec((tm, D_MODEL), lambda i: (i, 0)), pl.BlockSpec((tm, half), lambda i: (i, 0)),
                  pl.BlockSpec((tm, half), lambda i: (i, 0)), _full((half, D_MODEL)), _full((half, D_MODEL))],
        out_specs=pl.BlockSpec((tm, D_MODEL), lambda i: (i, 0)),
        out_shape=jax.ShapeDtypeStruct((t, D_MODEL), F32),
        compiler_params=_params("parallel"),
    )(x, ao, dn, wo_a, wo_d)


def _mlp_fwd(h1, g_mlp, w_up4, w_down, tm):
    t = h1.shape[0]

    def body(h_ref, g_ref, wu_ref, wd_ref, m_ref, r_ref, h2_ref, acc_ref):
        k = pl.program_id(1)

        @pl.when(k == 0)
        def _():
            m_ref[...] = _rms_fwd(h_ref[...], g_ref[...]).astype(MXU_DTYPE)
            acc_ref[...] = jnp.zeros_like(acc_ref)

        r = jnp.maximum(_dot(m_ref[...], wu_ref[...]), 0.0)
        r_ref[...] = r.astype(MXU_DTYPE)
        s = jnp.square(r).astype(MXU_DTYPE)
        acc_ref[...] += _dot(s, wd_ref[...])

        @pl.when(k == FF_BLOCKS - 1)
        def _():
            h2_ref[...] = h_ref[...] + acc_ref[...]

    return pl.pallas_call(
        body, name="mlp_fwd", grid=(t // tm, FF_BLOCKS),
        in_specs=[pl.BlockSpec((tm, D_MODEL), lambda i, k: (i, 0)), _full((1, D_MODEL)),
                  pl.BlockSpec((None, D_MODEL, FF_BLOCK), lambda i, k: (k, 0, 0)),
                  pl.BlockSpec((FF_BLOCK, D_MODEL), lambda i, k: (k, 0))],
        out_specs=[pl.BlockSpec((tm, D_MODEL), lambda i, k: (i, 0)), pl.BlockSpec((tm, FF_BLOCK), lambda i, k: (i, k)),
                   pl.BlockSpec((tm, D_MODEL), lambda i, k: (i, 0))],
        out_shape=[jax.ShapeDtypeStruct((t, D_MODEL), MXU_DTYPE), jax.ShapeDtypeStruct((t, D_FF), MXU_DTYPE),
                   jax.ShapeDtypeStruct((t, D_MODEL), F32)],
        scratch_shapes=[pltpu.VMEM((tm, D_MODEL), F32)],
        compiler_params=_params("parallel", "arbitrary"),
    )(h1, g_mlp, w_up4, w_down)


def _ple_loss(h2, p, tgt, g_ple, g_fin, w_gate, w_proj, tm):
    t = h2.shape[0]

    def body(h_ref, p_ref, t_ref, gp_ref, gf_ref, wg_ref, wp_ref,
             dh_ref, dhb_ref, dgp_ref, dpp_ref, n3_ref, pb_ref, acc_ref):
        @pl.when(pl.program_id(0) == 0)
        def _():
            acc_ref[...] = jnp.zeros_like(acc_ref)

        h = h_ref[...]
        g_ple_v, g_fin_v = gp_ref[...], gf_ref[...]
        n3 = _rms_fwd(h, g_ple_v).astype(MXU_DTYPE)
        n3_ref[...] = n3
        gate = _sigmoid(_dot(n3, wg_ref[...]))
        pb = p_ref[...].astype(MXU_DTYPE)
        pb_ref[...] = pb
        pp = _dot(pb, wp_ref[...])
        h3 = h + gate * pp
        r4 = lax.rsqrt(jnp.mean(h3 * h3, axis=-1, keepdims=True) + EPS)
        xh4 = h3 * r4
        e = xh4 * g_fin_v - t_ref[...]
        loss = 0.5 * jnp.sum(jnp.mean(e * e, axis=-1, keepdims=True), axis=0, keepdims=True)
        dy = e * (1.0 / D_MODEL)
        dg_fin = jnp.sum(dy * xh4, axis=0, keepdims=True)
        dxh = dy * g_fin_v
        dh3 = r4 * (dxh - xh4 * jnp.mean(dxh * xh4, axis=-1, keepdims=True))
        dpp_ref[...] = (dh3 * gate).astype(MXU_DTYPE)
        dgp = (dh3 * pp * gate * (1.0 - gate)).astype(MXU_DTYPE)
        dgp_ref[...] = dgp
        dn3 = _dot(dgp, wg_ref[...], NT)
        dx, dg_ple = _rms_bwd(h, g_ple_v, dn3)
        dh2 = dh3 + dx
        dh_ref[...] = dh2
        dhb_ref[...] = dh2.astype(MXU_DTYPE)
        acc_ref[0:1, :] += dg_fin
        acc_ref[1:2, :] += dg_ple
        acc_ref[2:3, :] += jnp.broadcast_to(loss, (1, D_MODEL))

    row = lambda w: pl.BlockSpec((tm, w), lambda i: (i, 0))
    return pl.pallas_call(
        body, name="ple_loss", grid=(t // tm,),
        in_specs=[row(D_MODEL), row(PLE_DIM), row(D_MODEL), _full((1, D_MODEL)), _full((1, D_MODEL)),
                  _full((D_MODEL, D_MODEL)), _full((PLE_DIM, D_MODEL))],
        out_specs=[row(D_MODEL), row(D_MODEL), row(D_MODEL), row(D_MODEL), row(D_MODEL), row(PLE_DIM),
                   _full((8, D_MODEL))],
        out_shape=[jax.ShapeDtypeStruct((t, D_MODEL), F32), jax.ShapeDtypeStruct((t, D_MODEL), MXU_DTYPE),
                   jax.ShapeDtypeStruct((t, D_MODEL), MXU_DTYPE), jax.ShapeDtypeStruct((t, D_MODEL), MXU_DTYPE),
                   jax.ShapeDtypeStruct((t, D_MODEL), MXU_DTYPE), jax.ShapeDtypeStruct((t, PLE_DIM), MXU_DTYPE),
                   jax.ShapeDtypeStruct((8, D_MODEL), F32)],
        compiler_params=_params("arbitrary"),
    )(h2, p, tgt, g_ple, g_fin, w_gate, w_proj)


def _mlp_bwd(dh2, dh2b, r, h1, g_mlp, w_up4, w_down, tm):
    t = h1.shape[0]

    def body(dh_ref, dhb_ref, r_ref, h_ref, g_ref, wu_ref, wd_ref,
             da_ref, dh1_ref, dh1b_ref, acc_ref, dm_ref):
        i, k = pl.program_id(0), pl.program_id(1)

        @pl.when((i == 0) & (k == 0))
        def _():
            acc_ref[...] = jnp.zeros_like(acc_ref)

        @pl.when(k == 0)
        def _():
            dm_ref[...] = jnp.zeros_like(dm_ref)

        ds = _dot(dhb_ref[...], wd_ref[...], NT)
        da = (ds * (2.0 * r_ref[...].astype(F32))).astype(MXU_DTYPE)
        da_ref[...] = da
        dm_ref[...] += _dot(da, wu_ref[...], NT)

        @pl.when(k == FF_BLOCKS - 1)
        def _():
            dx, dg = _rms_bwd(h_ref[...], g_ref[...], dm_ref[...])
            dh1 = dh_ref[...] + dx
            dh1_ref[...] = dh1
            dh1b_ref[...] = dh1.astype(MXU_DTYPE)
            acc_ref[0:1, :] += dg

    tok = lambda w: pl.BlockSpec((tm, w), lambda i, k: (i, 0))
    return pl.pallas_call(
        body, name="mlp_bwd", grid=(t // tm, FF_BLOCKS),
        in_specs=[tok(D_MODEL), tok(D_MODEL), pl.BlockSpec((tm, FF_BLOCK), lambda i, k: (i, k)), tok(D_MODEL),
                  _full((1, D_MODEL)), pl.BlockSpec((None, D_MODEL, FF_BLOCK), lambda i, k: (k, 0, 0)),
                  pl.BlockSpec((FF_BLOCK, D_MODEL), lambda i, k: (k, 0))],
        out_specs=[pl.BlockSpec((tm, FF_BLOCK), lambda i, k: (i, k)),
                   tok(D_MODEL), tok(D_MODEL), pl.BlockSpec((8, D_MODEL), lambda i, k: (0, 0))],
        out_shape=[jax.ShapeDtypeStruct((t, D_FF), MXU_DTYPE),
                   jax.ShapeDtypeStruct((t, D_MODEL), F32), jax.ShapeDtypeStruct((t, D_MODEL), MXU_DTYPE),
                   jax.ShapeDtypeStruct((8, D_MODEL), F32)],
        scratch_shapes=[pltpu.VMEM((tm, D_MODEL), F32)],
        compiler_params=_params("arbitrary", "arbitrary"),
    )(dh2, dh2b, r, h1, g_mlp, w_up4, w_down)


def _oproj_bwd(dh1b, wo_a, wo_d, tm, dep):
    t = dh1b.shape[0]
    half = wo_a.shape[0]

    def body(d_ref, wa_ref, wd_ref, dep_ref, da_ref, dd_ref):
        d = d_ref[...]
        da_ref[...] = _dot(d, wa_ref[...], NT)
        dd_ref[...] = _dot(d, wd_ref[...], NT)

    return pl.pallas_call(
        body, name="oproj_bwd", grid=(t // tm,),
        in_specs=[pl.BlockSpec((tm, D_MODEL), lambda i: (i, 0)), _full((half, D_MODEL)), _full((half, D_MODEL)),
                  pl.BlockSpec(memory_space=pl.ANY)],
        out_specs=[pl.BlockSpec((tm, half), lambda i: (i, 0)), pl.BlockSpec((tm, half), lambda i: (i, 0))],
        out_shape=[jax.ShapeDtypeStruct((t, half), F32), jax.ShapeDtypeStruct((t, half), F32)],
        compiler_params=_params("parallel"),
    )(dh1b, wo_a, wo_d, dep)


def _inproj_bwd(x, dh1, g_mix, grads, weights, tm):
    t = x.shape[0]
    n = len(grads)

    def body(*refs):
        x_ref, dh_ref, g_ref = refs[:3]
        g_refs, w_refs = refs[3:3 + n], refs[3 + n:3 + 2 * n]
        dx_ref, acc_ref = refs[3 + 2 * n:]

        @pl.when(pl.program_id(0) == 0)
        def _():
            acc_ref[...] = jnp.zeros_like(acc_ref)

        du = _dot(g_refs[0][...], w_refs[0][...], NT)
        for j in range(1, n):
            du += _dot(g_refs[j][...], w_refs[j][...], NT)
        dx, dg = _rms_bwd(x_ref[...], g_ref[...], du)
        dx_ref[...] = dh_ref[...] + dx
        acc_ref[0:1, :] += dg

    tok = lambda w: pl.BlockSpec((tm, w), lambda i: (i, 0))
    return pl.pallas_call(
        body, name="inproj_bwd", grid=(t // tm,),
        in_specs=[tok(D_MODEL), tok(D_MODEL), _full((1, D_MODEL))] + [tok(g.shape[1]) for g in grads]
                 + [_full(w.shape) for w in weights],
        out_specs=[tok(D_MODEL), _full((8, D_MODEL))],
        out_shape=[jax.ShapeDtypeStruct((t, D_MODEL), F32), jax.ShapeDtypeStruct((8, D_MODEL), F32)],
        compiler_params=_params("arbitrary"),
    )(x, dh1, g_mix, *grads, *weights)


def _wgrad(a, b, name, tk, tn, tt, stacked=False, prep=None):
    t, kdim = a.shape
    ncols = b.shape[1]

    def body(a_ref, b_ref, o_ref):
        @pl.when(pl.program_id(2) == 0)
        def _():
            o_ref[...] = jnp.zeros_like(o_ref)

        av = a_ref[...] if prep is None else prep(a_ref[...])
        o_ref[...] += _dot(av, b_ref[...], TN)

    if stacked:
        out_spec = pl.BlockSpec((None, tk, tn), lambda i, j, s: (j, i, 0))
        out_shape = jax.ShapeDtypeStruct((ncols // tn, kdim, tn), F32)
    else:
        out_spec = pl.BlockSpec((tk, tn), lambda i, j, s: (i, j))
        out_shape = jax.ShapeDtypeStruct((kdim, ncols), F32)
    return pl.pallas_call(
        body, name=name, grid=(kdim // tk, ncols // tn, t // tt),
        in_specs=[pl.BlockSpec((tt, tk), lambda i, j, s: (s, i)), pl.BlockSpec((tt, tn), lambda i, j, s: (s, j))],
        out_specs=out_spec, out_shape=out_shape,
        compiler_params=_params("parallel", "parallel", "arbitrary"),
    )(a, b)


def _rope_tables(t):
    half = ATTN_HEAD_DIM // 2
    inv = 1.0 / (ROPE_THETA ** (jnp.arange(half, dtype=F32) * (2.0 / ATTN_HEAD_DIM)))
    ang = jnp.arange(t, dtype=F32)[:, None] * inv[None, :]
    cos, sin = jnp.cos(ang), jnp.sin(ang)
    cos2 = jnp.concatenate([cos, cos], axis=-1)
    sin2 = jnp.concatenate([-sin, sin], axis=-1)
    return jnp.tile(cos2, (1, 2)), jnp.tile(sin2, (1, 2))


def _swap_halves(tv):
    w = tv.shape[-1]
    lane = lax.broadcasted_iota(jnp.int32, tv.shape, tv.ndim - 1)
    first = (lane % ATTN_HEAD_DIM) < (ATTN_HEAD_DIM // 2)
    return jnp.where(first, pltpu.roll(tv, w - ATTN_HEAD_DIM // 2, tv.ndim - 1),
                     pltpu.roll(tv, ATTN_HEAD_DIM // 2, tv.ndim - 1))


def _rope(tv, cos, sin):
    return tv * cos + _swap_halves(tv) * sin


def _rope_bwd(dv, cos, sin):
    return dv * cos + _swap_halves(dv * sin)


def _attn_valid(first_block):
    c = lax.broadcasted_iota(jnp.int32, (2 * ATTN_BLOCK, ATTN_BLOCK), 0)
    r = lax.broadcasted_iota(jnp.int32, (2 * ATTN_BLOCK, ATTN_BLOCK), 1)
    return (c > r) & (c <= r + ATTN_BLOCK) & ((c >= ATTN_BLOCK) | jnp.logical_not(first_block))


def _attn_probs(st, sink, valid):
    s = jnp.where(valid, st * ATTN_SCALE, -jnp.inf)
    m = jnp.maximum(jnp.max(s, axis=0, keepdims=True), sink)
    e = jnp.where(valid, jnp.exp(s - m), 0.0)
    es = jnp.exp(sink - m)
    inv = 1.0 / (jnp.sum(e, axis=0, keepdims=True) + es)
    return e * inv, es * inv


def _lane_scalar(vec, idx):
    lane = lax.broadcasted_iota(jnp.int32, vec.shape, 1)
    return jnp.sum(jnp.where(lane == idx, vec, 0.0), axis=-1, keepdims=True)


def _attn_specs(nb):
    cur = lambda w, cb: pl.BlockSpec((ATTN_BLOCK, w), lambda i: (jnp.minimum(i, nb - 1), cb))
    prev = lambda w, cb: pl.BlockSpec((ATTN_BLOCK, w), lambda i: (jnp.maximum(jnp.minimum(i, nb - 1) - 1, 0), cb))
    kcol, vcol = ATTN_Q // ATTN_KV, ATTN_Q // ATTN_KV + 1
    return [cur(ATTN_Q, 0), cur(ATTN_KV, kcol), prev(ATTN_KV, kcol), cur(ATTN_KV, vcol), prev(ATTN_KV, vcol),
            cur(ATTN_KV, 0), cur(ATTN_KV, 0), prev(ATTN_KV, 0), prev(ATTN_KV, 0), _full((1, 128))]


def _attn_fwd(pa, cos, sin, sinks_vec):
    t = pa.shape[0]
    nb = t // ATTN_BLOCK

    def body(q_ref, kc_ref, kp_ref, vc_ref, vp_ref, cc_ref, sc_ref, cp_ref, sp_ref, sk_ref, o_ref):
        first = pl.program_id(0) == 0
        cc, sc = cc_ref[...], sc_ref[...]
        q = _rope(q_ref[...], jnp.tile(cc, (1, ATTN_Q // ATTN_KV)), jnp.tile(sc, (1, ATTN_Q // ATTN_KV)))
        kc = _rope(kc_ref[...], cc, sc)
        kp = _rope(kp_ref[...], cp_ref[...], sp_ref[...])
        vc, vp = vc_ref[...], vp_ref[...]
        sk = sk_ref[...]
        valid = _attn_valid(first)
        kv = lambda tp, tc, hk: jnp.concatenate([tp[:, hk * ATTN_HEAD_DIM:(hk + 1) * ATTN_HEAD_DIM],
                                                 tc[:, hk * ATTN_HEAD_DIM:(hk + 1) * ATTN_HEAD_DIM]], axis=0)
        kwins = [kv(kp, kc, hk) for hk in range(ATTN_KV_HEADS)]
        vwins_t = [kv(vp, vc, hk).T for hk in range(ATTN_KV_HEADS)]
        heads = [slice(h * ATTN_HEAD_DIM, (h + 1) * ATTN_HEAD_DIM) for h in range(ATTN_HEADS)]
        scores = [_dot(kwins[h // ATTN_GROUPS], q[:, hs], NT) for h, hs in enumerate(heads)]
        probs = [_attn_probs(st, _lane_scalar(sk, h), valid)[0] for h, st in enumerate(scores)]
        for h, (hs, pt) in enumerate(zip(heads, probs)):
            o_ref[:, hs] = _dot(vwins_t[h // ATTN_GROUPS], pt).T.astype(o_ref.dtype)

    return pl.pallas_call(
        body, name="attn_fwd", grid=(nb,),
        in_specs=_attn_specs(nb),
        out_specs=pl.BlockSpec((ATTN_BLOCK, ATTN_Q), lambda i: (i, 0)),
        out_shape=jax.ShapeDtypeStruct((t, ATTN_Q), MXU_DTYPE),
        compiler_params=_params("parallel"),
    )(pa, pa, pa, pa, pa, cos, sin, cos, sin, sinks_vec)


def _attn_bwd(pa, cos, sin, sinks_vec, dao):
    t = pa.shape[0]
    nb = t // ATTN_BLOCK

    def body(q_ref, kc_ref, kp_ref, vc_ref, vp_ref, cc_ref, sc_ref, cp_ref, sp_ref, sk_ref, do_ref,
             dq_ref, dk_ref, dv_ref, acc_ref, dqr_ref, dkw_ref, dvw_ref, ck_ref, cv_ref):
        i = pl.program_id(0)

        @pl.when(i == 0)
        def _():
            acc_ref[...] = jnp.zeros_like(acc_ref)
            ck_ref[...] = jnp.zeros_like(ck_ref)
            cv_ref[...] = jnp.zeros_like(cv_ref)

        @pl.when(i < nb)
        def _():
            first = i == 0
            cc, sc = cc_ref[...], sc_ref[...]
            cq, sq = jnp.tile(cc, (1, ATTN_Q // ATTN_KV)), jnp.tile(sc, (1, ATTN_Q // ATTN_KV))
            q = _rope(q_ref[...], cq, sq)
            kc = _rope(kc_ref[...], cc, sc)
            kp = _rope(kp_ref[...], cp_ref[...], sp_ref[...])
            vc, vp = vc_ref[...], vp_ref[...]
            sk = sk_ref[...]
            do = do_ref[...]
            lane = lax.broadcasted_iota(jnp.int32, (1, 128), 1)
            dsink = jnp.zeros((1, 128), F32)
            valid = _attn_valid(first)
            kv = lambda tp, tc, hk: jnp.concatenate([tp[:, hk * ATTN_HEAD_DIM:(hk + 1) * ATTN_HEAD_DIM],
                                                     tc[:, hk * ATTN_HEAD_DIM:(hk + 1) * ATTN_HEAD_DIM]], axis=0)
            kwins = [kv(kp, kc, hk) for hk in range(ATTN_KV_HEADS)]
            vwins = [kv(vp, vc, hk) for hk in range(ATTN_KV_HEADS)]
            kwins_t = [kw.T for kw in kwins]
            heads = [slice(h * ATTN_HEAD_DIM, (h + 1) * ATTN_HEAD_DIM) for h in range(ATTN_HEADS)]
            scores = [_dot(kwins[h // ATTN_GROUPS], q[:, hs], NT) for h, hs in enumerate(heads)]
            dps = [_dot(vwins[h // ATTN_GROUPS], do[:, hs], NT) for h, hs in enumerate(heads)]
            pts, dsts = [], []
            for h, (st, dp_t) in enumerate(zip(scores, dps)):
                probs_t, psink = _attn_probs(st, _lane_scalar(sk, h), valid)
                delta = jnp.sum(probs_t * dp_t, axis=0, keepdims=True)
                pts.append(probs_t)
                dsts.append(probs_t * (dp_t - delta) * ATTN_SCALE)
                dsink += jnp.where(lane == h, jnp.sum(-psink * delta, axis=1, keepdims=True), 0.0)
            for h, (hs, ds_t) in enumerate(zip(heads, dsts)):
                dqr_ref[:, hs] = _dot(kwins_t[h // ATTN_GROUPS], ds_t).T
            for hk in range(ATTN_KV_HEADS):
                ks = slice(hk * ATTN_HEAD_DIM, (hk + 1) * ATTN_HEAD_DIM)
                group = range(hk * ATTN_GROUPS, (hk + 1) * ATTN_GROUPS)
                ds_g = jnp.concatenate([dsts[h] for h in group], axis=1)
                p_g = jnp.concatenate([pts[h] for h in group], axis=1)
                q_g = jnp.concatenate([q[:, heads[h]] for h in group], axis=0)
                do_g = jnp.concatenate([do[:, heads[h]] for h in group], axis=0)
                dkw_ref[:, ks] = _dot(ds_g, q_g)
                dvw_ref[:, ks] = _dot(p_g, do_g)
            acc_ref[0:1, :] += dsink
            dq_ref[...] = _rope_bwd(dqr_ref[...], cq, sq).astype(dq_ref.dtype)
            dk_ref[...] = (ck_ref[...] + _rope_bwd(dkw_ref[0:ATTN_BLOCK, :], cp_ref[...], sp_ref[...])).astype(dk_ref.dtype)
            dv_ref[...] = (cv_ref[...] + dvw_ref[0:ATTN_BLOCK, :]).astype(dv_ref.dtype)
            ck_ref[...] = _rope_bwd(dkw_ref[ATTN_BLOCK:2 * ATTN_BLOCK, :], cc, sc)
            cv_ref[...] = dvw_ref[ATTN_BLOCK:2 * ATTN_BLOCK, :]

        @pl.when(i == nb)
        def _():
            dk_ref[...] = ck_ref[...].astype(dk_ref.dtype)
            dv_ref[...] = cv_ref[...].astype(dv_ref.dtype)

    prev_out = lambda w: pl.BlockSpec((ATTN_BLOCK, w), lambda i: (jnp.maximum(i - 1, 0), 0))
    return pl.pallas_call(
        body, name="attn_bwd", grid=(nb + 1,),
        in_specs=_attn_specs(nb) + [pl.BlockSpec((ATTN_BLOCK, ATTN_Q), lambda i: (jnp.minimum(i, nb - 1), 0))],
        out_specs=[pl.BlockSpec((ATTN_BLOCK, ATTN_Q), lambda i: (jnp.minimum(i, nb - 1), 0)), prev_out(ATTN_KV),
                   prev_out(ATTN_KV), _full((8, 128))],
        out_shape=[jax.ShapeDtypeStruct((t, ATTN_Q), MXU_DTYPE), jax.ShapeDtypeStruct((t, ATTN_KV), MXU_DTYPE),
                   jax.ShapeDtypeStruct((t, ATTN_KV), MXU_DTYPE), jax.ShapeDtypeStruct((8, 128), F32)],
        scratch_shapes=[pltpu.VMEM((ATTN_BLOCK, ATTN_Q), F32), pltpu.VMEM((2 * ATTN_BLOCK, ATTN_KV), F32),
                        pltpu.VMEM((2 * ATTN_BLOCK, ATTN_KV), F32), pltpu.VMEM((ATTN_BLOCK, ATTN_KV), F32),
                        pltpu.VMEM((ATTN_BLOCK, ATTN_KV), F32)],
        compiler_params=_params("arbitrary"),
    )(pa, pa, pa, pa, pa, cos, sin, cos, sin, sinks_vec, dao)


PAIR = 2 * DN_CHUNK
INTRA_PAIRS = 4
HALO = 8


def _conv_window(cur_ref, prev_ref, xs_ref, tm):
    prev = jnp.where(pl.program_id(0) > 0, prev_ref[...], 0.0)
    xs_ref[0:HALO, :] = prev
    xs_ref[HALO:HALO + tm, :] = cur_ref[...]


def _conv_taps(xs_ref, cw_ref, tm):
    y = cw_ref[0:1, :] * xs_ref[pl.ds(HALO - DN_CONV + 1, tm), :]
    for j in range(1, DN_CONV):
        y += cw_ref[j:j + 1, :] * xs_ref[pl.ds(HALO - DN_CONV + 1 + j, tm), :]
    return y


def _gate_values(ba, al, dt):
    beta = _sigmoid(ba)
    pre = ba + dt
    g = -jnp.exp(al) * _softplus(pre)
    return beta, g, pre


def _dn_prep_specs(tm, t):
    return [pl.BlockSpec((tm, CONV_CH), lambda i: (i, 0)),
            pl.BlockSpec((HALO, CONV_CH), lambda i: (jnp.maximum(i * (tm // HALO) - 1, 0), 0)),
            pl.BlockSpec((tm, 128), lambda i: (i, 4 * DN_W // 128)),
            _full((DN_CONV, CONV_CH)), _full((1, 128)), _full((1, 128))]


def _dn_prep(pd, conv_w, al_vec, dt_vec, tm):
    t = pd.shape[0]

    def body(cur_ref, prev_ref, ba_ref, cw_ref, al_ref, dt_ref, qn_ref, kn_ref, vc_ref, gc_ref, gr_ref, xs_ref):
        _conv_window(cur_ref, prev_ref, xs_ref, tm)
        y = _conv_taps(xs_ref, cw_ref, tm)
        c = y * _sigmoid(y)
        for h in range(DN_HEADS):
            qs = slice(h * DN_HEAD_DIM, (h + 1) * DN_HEAD_DIM)
            ksl = slice(DN_W + h * DN_HEAD_DIM, DN_W + (h + 1) * DN_HEAD_DIM)
            qh, kh = c[:, qs], c[:, ksl]
            qn_ref[:, qs] = qh * lax.rsqrt(jnp.sum(qh * qh, axis=-1, keepdims=True) + EPS) * DN_SCALE
            kn_ref[:, qs] = kh * lax.rsqrt(jnp.sum(kh * kh, axis=-1, keepdims=True) + EPS)
        vc_ref[...] = c[:, 2 * DN_W:3 * DN_W]
        beta, g, _ = _gate_values(ba_ref[...], al_ref[...], dt_ref[...])
        lane = lax.broadcasted_iota(jnp.int32, beta.shape, 1)
        gb = jnp.where(lane < DN_HEADS, beta, jnp.where(lane < 2 * DN_HEADS, g, 0.0))
        gc_ref[...] = gb
        gr_ref[...] = gb.T[0:8, :]

    tok = lambda w: pl.BlockSpec((tm, w), lambda i: (i, 0))
    return pl.pallas_call(
        body, name="dn_prep", grid=(t // tm,),
        in_specs=_dn_prep_specs(tm, t),
        out_specs=[tok(DN_W), tok(DN_W), tok(DN_W), tok(128), pl.BlockSpec((8, tm), lambda i: (0, i))],
        out_shape=[jax.ShapeDtypeStruct((t, DN_W), F32)] * 3 + [jax.ShapeDtypeStruct((t, 128), F32),
                                                                 jax.ShapeDtypeStruct((8, t), F32)],
        scratch_shapes=[pltpu.VMEM((HALO + tm, CONV_CH), F32)],
        compiler_params=_params("parallel"),
    )(pd, pd, pd, conv_w, al_vec, dt_vec)


def _pair_masks():
    r = lax.broadcasted_iota(jnp.int32, (PAIR, PAIR), 0)
    c = lax.broadcasted_iota(jnp.int32, (PAIR, PAIR), 1)
    same = (r < DN_CHUNK) == (c < DN_CHUNK)
    return same & (r >= c), same & (r > c)


def _lane_col(mat, idx):
    lane = lax.broadcasted_iota(jnp.int32, mat.shape, 1)
    return jnp.sum(jnp.where(lane == idx, mat, 0.0), axis=-1, keepdims=True)


def _pair_cumsums(gc, gr, low):
    lowf = low.astype(F32)
    return _dot(lowf, gc, NN, HI), _dot(gr, lowf, NT, HI)


def _pair_gates(gc, cum_c, cum_r, low, h):
    beta = _lane_col(gc, h)
    gam = _lane_col(cum_c, DN_HEADS + h)
    gam_row = cum_r[DN_HEADS + h:DN_HEADS + h + 1, :]
    dm = jnp.where(low, jnp.exp(jnp.where(low, gam - gam_row, 0.0)), 0.0)
    row = lax.broadcasted_iota(jnp.int32, gam.shape, 0)
    gl = jnp.where(row < DN_CHUNK, gam[DN_CHUNK - 1:DN_CHUNK, :], gam[PAIR - 1:PAIR, :])
    return beta, gam, dm, gl


def _split(a):
    hi = a.astype(BF16)
    return hi, (a - hi.astype(F32)).astype(BF16)


def _dot_split(a, b, dims=NN):
    (ah, al), (bh, bl) = a, b
    la, lb = (1, 1) if dims == TN else ((0, 1) if dims == NN else (0, 0))
    r = _dot(jnp.concatenate([ah, al], axis=la), jnp.concatenate([bh, bl], axis=lb), dims)
    m, n = r.shape[0] // 2, r.shape[1] // 2
    return (r[m:, n:] + (r[:m, n:] + r[m:, :n])) + r[:m, :n]


def _unit_lower_inverses(lmats):
    n = lmats[0].shape[0]
    r = lax.broadcasted_iota(jnp.int32, (n, n), 0)
    c = lax.broadcasted_iota(jnp.int32, (n, n), 1)
    same = lambda size: (r & ~(size - 1)) == (c & ~(size - 1))
    base = DN_CHUNK // 4
    diag = [jnp.where(same(base), l, 0.0) for l in lmats]
    accs = [(r == c).astype(F32) - d for d in diag]
    splits = [_split(d) for d in diag]
    step = 1
    while 2 * step < base:
        splits = [_split(_dot_split(s, s)) for s in splits]
        accs = [acc + _dot_split(_split(acc), s) for acc, s in zip(accs, splits)]
        step *= 2
    size = base
    while size < DN_CHUNK:
        below = same(2 * size) & jnp.logical_not(same(size))
        tb = [_dot(acc, jnp.where(below, l, 0.0)) for acc, l in zip(accs, lmats)]
        accs = [acc - _dot(t, acc) for acc, t in zip(accs, tb)]
        size *= 2
    return accs


def _dn_intra(qn, kn, vc, gc, gr):
    t = qn.shape[0]
    npair = t // PAIR
    rows_step = INTRA_PAIRS * PAIR

    def body(q_ref, k_ref, v_ref, gc_ref, gr_ref, u_ref, w_ref, qg_ref, kd_ref, a_ref, ti_ref, dl_ref):
        low, strict = _pair_masks()
        items = []
        for p in range(INTRA_PAIRS):
            rows = slice(p * PAIR, (p + 1) * PAIR)
            gc_v = gc_ref[rows, :]
            cum_c, cum_r = _pair_cumsums(gc_v, gr_ref[:, rows], low)
            for h in range(DN_HEADS):
                hs = slice(h * DN_HEAD_DIM, (h + 1) * DN_HEAD_DIM)
                items.append((p, h, rows, hs, _pair_gates(gc_v, cum_c, cum_r, low, h)))
        lmats = []
        for p, h, rows, hs, (beta, gam, dm, gl) in items:
            k = k_ref[rows, hs]
            lmats.append(jnp.where(strict, _dot(k * beta, k, NT) * dm, 0.0))
        tinvs = _unit_lower_inverses(lmats)
        for (p, h, rows, hs, (beta, gam, dm, gl)), tinv in zip(items, tinvs):
            q, k, v = q_ref[rows, hs], k_ref[rows, hs], v_ref[rows, hs]
            eg = jnp.exp(gam)
            u_ref[rows, hs] = _dot(tinv, v * beta)
            w_ref[rows, hs] = _dot(tinv, (k * beta) * eg)
            a_ref[h, rows, :] = _dot(q, k, NT) * dm
            ti_ref[h, rows, :] = tinv
            qg_ref[rows, hs] = q * eg
            kd_ref[rows, hs] = k * jnp.exp(gl - gam)
            for c in range(2):
                last = (c + 1) * DN_CHUNK - 1
                dl_ref[2 * p + c, h] = jnp.broadcast_to(jnp.exp(gam[last:last + 1, :]), (8, 128))

    tok = lambda w: pl.BlockSpec((rows_step, w), lambda n: (n, 0))
    hm = pl.BlockSpec((DN_HEADS, rows_step, PAIR), lambda n: (0, n, 0))
    return pl.pallas_call(
        body, name="dn_intra", grid=(npair // INTRA_PAIRS,),
        in_specs=[tok(DN_W), tok(DN_W), tok(DN_W), tok(128), pl.BlockSpec((8, rows_step), lambda n: (0, n))],
        out_specs=[tok(DN_W)] * 4 + [hm, hm, pl.BlockSpec((2 * INTRA_PAIRS, DN_HEADS, 8, 128), lambda n: (n, 0, 0, 0))],
        out_shape=[jax.ShapeDtypeStruct((t, DN_W), F32)] * 4 + [jax.ShapeDtypeStruct((DN_HEADS, t, PAIR), F32)] * 2
                  + [jax.ShapeDtypeStruct((2 * npair, DN_HEADS, 8, 128), F32)],
        compiler_params=_params("parallel"),
    )(qn, kn, vc, gc, gr)


def _dn_scan_fwd(u, w, qg, kd, a_qk, dlast, pd, dn_w):
    t = u.shape[0]
    npair = t // PAIR

    def body(u_ref, w_ref, qg_ref, kd_ref, a_ref, dl_ref, z_ref, nw_ref, out_ref, o_ref, vn_ref, sall_ref, s_ref):
        @pl.when(pl.program_id(0) == 0)
        def _():
            s_ref[...] = jnp.zeros_like(s_ref)

        nw = nw_ref[...]
        for c in range(2):
            rows = slice(c * DN_CHUNK, (c + 1) * DN_CHUNK)
            for h in range(DN_HEADS):
                hs = slice(h * DN_HEAD_DIM, (h + 1) * DN_HEAD_DIM)
                st = s_ref[h]
                sall_ref[c, h] = st
                vn_ref[rows, hs] = u_ref[rows, hs] - _dot(w_ref[rows, hs], st)
            for h in range(DN_HEADS):
                hs = slice(h * DN_HEAD_DIM, (h + 1) * DN_HEAD_DIM)
                st, vn = s_ref[h], vn_ref[rows, hs]
                o = _dot(qg_ref[rows, hs], st) + _dot(a_ref[h, rows, rows], vn)
                s_ref[h] = st * dl_ref[c, h][0:1, :] + _dot(kd_ref[rows, hs], vn, TN)
                o_ref[rows, hs] = o
                z = z_ref[rows, hs]
                on = o * lax.rsqrt(jnp.mean(o * o, axis=-1, keepdims=True) + EPS) * nw
                out_ref[rows, hs] = (on * (z * _sigmoid(z))).astype(out_ref.dtype)

    tok = pl.BlockSpec((PAIR, DN_W), lambda n: (n, 0))
    hm = pl.BlockSpec((DN_HEADS, PAIR, PAIR), lambda n: (0, n, 0))
    return pl.pallas_call(
        body, name="dn_scan_fwd", grid=(npair,),
        in_specs=[tok, tok, tok, tok, hm, pl.BlockSpec((2, DN_HEADS, 8, 128), lambda n: (n, 0, 0, 0)),
                  pl.BlockSpec((PAIR, DN_W), lambda n: (n, 3)), _full((1, 128))],
        out_specs=[tok, tok, tok, pl.BlockSpec((2, DN_HEADS, DN_HEAD_DIM, DN_HEAD_DIM), lambda n: (n, 0, 0, 0))],
        out_shape=[jax.ShapeDtypeStruct((t, DN_W), MXU_DTYPE)] + [jax.ShapeDtypeStruct((t, DN_W), F32)] * 2
                  + [jax.ShapeDtypeStruct((2 * npair, DN_HEADS, DN_HEAD_DIM, DN_HEAD_DIM), F32)],
        scratch_shapes=[pltpu.VMEM((DN_HEADS, DN_HEAD_DIM, DN_HEAD_DIM), F32)],
        compiler_params=_params("arbitrary"),
    )(u, w, qg, kd, a_qk, dlast, pd, dn_w)


def _dn_scan_bwd(dout, o, vnew, sall, w, qg, kd, a_qk, dlast, pd, dn_w):
    t = o.shape[0]
    npair = t // PAIR
    rev = lambda n: npair - 1 - n

    def body(do_ref, o_ref, vn_ref, sall_ref, w_ref, qg_ref, kd_ref, a_ref, dl_ref, z_ref, nw_ref,
             dz_ref, du_ref, dw_ref, dqg_ref, dkd_ref, da_ref, ddl_ref, acc_ref, ds_ref, dos_ref):
        @pl.when(pl.program_id(0) == 0)
        def _():
            ds_ref[...] = jnp.zeros_like(ds_ref)
            acc_ref[...] = jnp.zeros_like(acc_ref)

        nw = nw_ref[...]
        dnw = jnp.zeros((1, 128), F32)
        for h in range(DN_HEADS):
            hs = slice(h * DN_HEAD_DIM, (h + 1) * DN_HEAD_DIM)
            o, z, dout = o_ref[:, hs], z_ref[:, hs], do_ref[:, hs]
            r = lax.rsqrt(jnp.mean(o * o, axis=-1, keepdims=True) + EPS)
            oh = o * r
            sz = _sigmoid(z)
            dz_ref[:, hs] = dout * (oh * nw) * (sz + z * sz * (1.0 - sz))
            don = dout * (z * sz)
            dnw += jnp.sum(don * oh, axis=0, keepdims=True)
            doh = don * nw
            dos_ref[:, hs] = r * (doh - oh * jnp.mean(doh * oh, axis=-1, keepdims=True))
        acc_ref[0:1, :] += dnw
        for c in (1, 0):
            rows = slice(c * DN_CHUNK, (c + 1) * DN_CHUNK)
            other = slice((1 - c) * DN_CHUNK, (2 - c) * DN_CHUNK)
            for h in range(DN_HEADS):
                hs = slice(h * DN_HEAD_DIM, (h + 1) * DN_HEAD_DIM)
                do, st, dsp, vn = dos_ref[rows, hs], sall_ref[c, h], ds_ref[h], vn_ref[rows, hs]
                da_ref[h, rows, rows] = _dot(do, vn, NT)
                da_ref[h, rows, other] = jnp.zeros((DN_CHUNK, DN_CHUNK), F32)
                du_ref[rows, hs] = _dot(a_ref[h, rows, rows], do, TN) + _dot(kd_ref[rows, hs], dsp)
                dqg_ref[rows, hs] = _dot(do, st, NT)
                dkd_ref[rows, hs] = _dot(vn, dsp, NT)
                ddl = jnp.sum(jnp.sum(dsp * st, axis=1, keepdims=True), axis=0, keepdims=True)
                ddl_ref[c, h] = jnp.broadcast_to(ddl, (8, 128))
            for h in range(DN_HEADS):
                hs = slice(h * DN_HEAD_DIM, (h + 1) * DN_HEAD_DIM)
                do, st, dvn = dos_ref[rows, hs], sall_ref[c, h], du_ref[rows, hs]
                dw_ref[rows, hs] = -_dot(dvn, st, NT)
                ds_ref[h] = (ds_ref[h] * dl_ref[c, h][0:1, :] + _dot(qg_ref[rows, hs], do, TN)
                             - _dot(w_ref[rows, hs], dvn, TN))

    tok = pl.BlockSpec((PAIR, DN_W), lambda n: (rev(n), 0))
    hm = pl.BlockSpec((DN_HEADS, PAIR, PAIR), lambda n: (0, rev(n), 0))
    sc = pl.BlockSpec((2, DN_HEADS, 8, 128), lambda n: (rev(n), 0, 0, 0))
    return pl.pallas_call(
        body, name="dn_scan_bwd", grid=(npair,),
        in_specs=[tok, tok, tok, pl.BlockSpec((2, DN_HEADS, DN_HEAD_DIM, DN_HEAD_DIM), lambda n: (rev(n), 0, 0, 0)),
                  tok, tok, tok, hm, sc, pl.BlockSpec((PAIR, DN_W), lambda n: (rev(n), 3)), _full((1, 128))],
        out_specs=[tok] * 5 + [hm, sc, _full((8, 128))],
        out_shape=[jax.ShapeDtypeStruct((t, DN_W), F32)] * 5 + [jax.ShapeDtypeStruct((DN_HEADS, t, PAIR), F32),
                   jax.ShapeDtypeStruct((2 * npair, DN_HEADS, 8, 128), F32), jax.ShapeDtypeStruct((8, 128), F32)],
        scratch_shapes=[pltpu.VMEM((DN_HEADS, DN_HEAD_DIM, DN_HEAD_DIM), F32), pltpu.VMEM((PAIR, DN_W), F32)],
        compiler_params=_params("arbitrary"),
    )(dout, o, vnew, sall, w, qg, kd, a_qk, dlast, pd, dn_w)


def _dn_intra_bwd(qn, kn, vc, gc, gr, tinv, a_qk, du, dw, dqg, dkd, da_qk, ddlast, dlast, dep):
    t = qn.shape[0]
    npair = t // PAIR

    def body(q_ref, k_ref, v_ref, gc_ref, gr_ref, ti_ref, a_ref, du_ref, dw_ref, dqg_ref, dkd_ref, da_ref, ddl_ref, dl_ref,
             dep_ref, dq_ref, dk_ref, dv_ref, dg_ref):
        low, strict = _pair_masks()
        lane = lax.broadcasted_iota(jnp.int32, (PAIR, 128), 1)
        rowi = lax.broadcasted_iota(jnp.int32, (PAIR, 1), 0)
        rsum = lambda v: jnp.sum(v, axis=-1, keepdims=True)
        items = []
        for p in range(INTRA_PAIRS):
            rows = slice(p * PAIR, (p + 1) * PAIR)
            gc_v = gc_ref[rows, :]
            cum_c, cum_r = _pair_cumsums(gc_v, gr_ref[:, rows], low)
            for h in range(DN_HEADS):
                hs = slice(h * DN_HEAD_DIM, (h + 1) * DN_HEAD_DIM)
                items.append((p, h, rows, hs, _pair_gates(gc_v, cum_c, cum_r, low, h)))
        dtis, lmats, dvbs, dkbgs = [], [], [], []
        for p, h, rows, hs, (beta, gam, dm, gl) in items:
            k, tinv = k_ref[rows, hs], ti_ref[h, rows, :]
            kb = k * beta
            dtis.append(_dot(du_ref[rows, hs], v_ref[rows, hs] * beta, NT)
                        + _dot(dw_ref[rows, hs], kb * jnp.exp(gam), NT))
            lmats.append(jnp.where(strict, _dot(kb, k, NT) * dm, 0.0))
            dvbs.append(_dot(tinv, du_ref[rows, hs], TN))
            dkbgs.append(_dot(tinv, dw_ref[rows, hs], TN))
        xs = [_dot(ti_ref[h, rows, :], dti, TN) for (p, h, rows, hs, g), dti in zip(items, dtis)]
        dls = [jnp.where(strict, -_dot(x, ti_ref[h, rows, :], NT), 0.0) for (p, h, rows, hs, g), x in zip(items, xs)]
        dgam_all = [jnp.zeros((PAIR, 128), F32) for _ in range(INTRA_PAIRS)]
        dbeta_all = [jnp.zeros((PAIR, 128), F32) for _ in range(INTRA_PAIRS)]
        for (p, h, rows, hs, (beta, gam, dm, gl)), dl, lmat, dvb, dkbg in zip(items, dls, lmats, dvbs, dkbgs):
            q, k, v = q_ref[rows, hs], k_ref[rows, hs], v_ref[rows, hs]
            a = a_ref[h, rows, :]
            dqg, dkd = dqg_ref[rows, hs], dkd_ref[rows, hs]
            kb = k * beta
            eg = jnp.exp(gam)
            ekd = jnp.exp(gl - gam)
            dmm = dl * dm
            dam = jnp.where(low, da_ref[h, rows, :], 0.0)
            dn = dam * dm
            e = dl * lmat + dam * a
            dkb = _dot(dmm, k) + dkbg * eg
            dk_ref[rows, hs] = _dot(dmm, kb, TN) + _dot(dn, q, TN) + dkd * ekd + dkb * beta
            dq_ref[rows, hs] = _dot(dn, k) + dqg * eg
            dv_ref[rows, hs] = dvb * beta
            t_kd = rsum(dkd * (k * ekd))
            dgam = rsum(e) - rsum(e.T) + rsum(dqg * (q * eg)) + rsum(dkbg * (kb * eg)) - t_kd
            for c in range(2):
                crows = slice(c * DN_CHUNK, (c + 1) * DN_CHUNK)
                dgl = (jnp.sum(t_kd[crows, :], axis=0, keepdims=True)
                       + ddl_ref[2 * p + c, h][0:1, 0:1] * dl_ref[2 * p + c, h][0:1, 0:1])
                dgam = dgam + jnp.where(rowi == (c + 1) * DN_CHUNK - 1, dgl, 0.0)
            dgam_all[p] += jnp.where(lane == DN_HEADS + h, dgam, 0.0)
            dbeta_all[p] += jnp.where(lane == h, rsum(dkb * k) + rsum(dvb * v), 0.0)
        for p in range(INTRA_PAIRS):
            dg_ref[p * PAIR:(p + 1) * PAIR, :] = dbeta_all[p] + _dot(low.astype(F32), dgam_all[p], TN, HI)

    rows_step = INTRA_PAIRS * PAIR
    tok = lambda w: pl.BlockSpec((rows_step, w), lambda n: (n, 0))
    hm = pl.BlockSpec((DN_HEADS, rows_step, PAIR), lambda n: (0, n, 0))
    sc = pl.BlockSpec((2 * INTRA_PAIRS, DN_HEADS, 8, 128), lambda n: (n, 0, 0, 0))
    return pl.pallas_call(
        body, name="dn_intra_bwd", grid=(npair // INTRA_PAIRS,),
        in_specs=[tok(DN_W), tok(DN_W), tok(DN_W), tok(128), pl.BlockSpec((8, rows_step), lambda n: (0, n)), hm, hm,
                  tok(DN_W), tok(DN_W), tok(DN_W), tok(DN_W), hm, sc, sc, pl.BlockSpec(memory_space=pl.ANY)],
        out_specs=[tok(DN_W), tok(DN_W), tok(DN_W), tok(128)],
        out_shape=[jax.ShapeDtypeStruct((t, DN_W), F32)] * 3 + [jax.ShapeDtypeStruct((t, 128), F32)],
        compiler_params=_params("parallel"),
    )(qn, kn, vc, gc, gr, tinv, a_qk, du, dw, dqg, dkd, da_qk, ddlast, dlast, dep)


def _dn_prep_bwd(pd, conv_w, al_vec, dt_vec, dqn, dkn, dvc, dgc, tm):
    t = pd.shape[0]

    def body(cur_ref, prev_ref, ba_ref, cw_ref, al_ref, dt_ref, dq_ref, dk_ref, dv_ref, dg_ref,
             dy_ref, dba_ref, accw_ref, accg_ref, xs_ref, dc_ref):
        @pl.when(pl.program_id(0) == 0)
        def _():
            accw_ref[...] = jnp.zeros_like(accw_ref)
            accg_ref[...] = jnp.zeros_like(accg_ref)

        _conv_window(cur_ref, prev_ref, xs_ref, tm)
        y = _conv_taps(xs_ref, cw_ref, tm)
        sg = _sigmoid(y)
        c = y * sg
        for h in range(DN_HEADS):
            qs = slice(h * DN_HEAD_DIM, (h + 1) * DN_HEAD_DIM)
            ksl = slice(DN_W + h * DN_HEAD_DIM, DN_W + (h + 1) * DN_HEAD_DIM)
            for src, sl, scale in ((dq_ref, qs, DN_SCALE), (dk_ref, ksl, 1.0)):
                xh = c[:, sl]
                r = lax.rsqrt(jnp.sum(xh * xh, axis=-1, keepdims=True) + EPS)
                unit = xh * r
                dn = src[:, qs] * scale
                dc_ref[:, sl] = r * (dn - unit * jnp.sum(dn * unit, axis=-1, keepdims=True))
        dc_ref[:, 2 * DN_W:3 * DN_W] = dv_ref[...]
        dy = dc_ref[...] * (sg + y * sg * (1.0 - sg))
        dy_ref[...] = dy
        for j in range(DN_CONV):
            accw_ref[j:j + 1, :] += jnp.sum(dy * xs_ref[pl.ds(HALO - DN_CONV + 1 + j, tm), :], axis=0, keepdims=True)

        beta, g, pre = _gate_values(ba_ref[...], al_ref[...], dt_ref[...])
        dgb = dg_ref[...]
        lane = lax.broadcasted_iota(jnp.int32, dgb.shape, 1)
        is_b, is_a = lane < DN_HEADS, (lane >= DN_HEADS) & (lane < 2 * DN_HEADS)
        dpre = dgb * (-jnp.exp(al_ref[...])) * _sigmoid(pre)
        dba_ref[...] = jnp.where(is_b, dgb * beta * (1.0 - beta), jnp.where(is_a, dpre, 0.0))
        accg_ref[0:1, :] += jnp.sum(jnp.where(is_a, dgb * g, 0.0), axis=0, keepdims=True)
        accg_ref[1:2, :] += jnp.sum(jnp.where(is_a, dpre, 0.0), axis=0, keepdims=True)

    tok = lambda w: pl.BlockSpec((tm, w), lambda i: (i, 0))
    return pl.pallas_call(
        body, name="dn_prep_bwd", grid=(t // tm,),
        in_specs=_dn_prep_specs(tm, t) + [tok(DN_W), tok(DN_W), tok(DN_W), tok(128)],
        out_specs=[tok(CONV_CH), tok(128), _full((8, CONV_CH)), _full((8, 128))],
        out_shape=[jax.ShapeDtypeStruct((t, CONV_CH), F32), jax.ShapeDtypeStruct((t, 128), F32),
                   jax.ShapeDtypeStruct((8, CONV_CH), F32), jax.ShapeDtypeStruct((8, 128), F32)],
        scratch_shapes=[pltpu.VMEM((HALO + tm, CONV_CH), F32), pltpu.VMEM((tm, CONV_CH), F32)],
        compiler_params=_params("arbitrary"),
    )(pd, pd, pd, conv_w, al_vec, dt_vec, dqn, dkn, dvc, dgc)


def _dn_conv_bwd(dy, dz, dba, conv_w, tm):
    t = dy.shape[0]
    nt = t // tm

    def body(cur_ref, nxt_ref, dz_ref, dba_ref, cw_ref, o_ref, ds_ref):
        nxt = jnp.where(pl.program_id(0) < nt - 1, nxt_ref[...], 0.0)
        ds_ref[0:tm, :] = cur_ref[...]
        ds_ref[tm:tm + HALO, :] = nxt
        dx = cw_ref[0:1, :] * ds_ref[pl.ds(DN_CONV - 1, tm), :]
        for j in range(1, DN_CONV):
            dx += cw_ref[j:j + 1, :] * ds_ref[pl.ds(DN_CONV - 1 - j, tm), :]
        o_ref[:, 0:CONV_CH] = dx.astype(o_ref.dtype)
        o_ref[:, CONV_CH:CONV_CH + DN_W] = dz_ref[...].astype(o_ref.dtype)
        o_ref[:, CONV_CH + DN_W:DN_COLS] = dba_ref[...].astype(o_ref.dtype)

    tok = lambda w: pl.BlockSpec((tm, w), lambda i: (i, 0))
    return pl.pallas_call(
        body, name="dn_conv_bwd", grid=(nt,),
        in_specs=[tok(CONV_CH),
                  pl.BlockSpec((HALO, CONV_CH), lambda i: (jnp.minimum((i + 1) * (tm // HALO), t // HALO - 1), 0)),
                  tok(DN_W), tok(128), _full((DN_CONV, CONV_CH))],
        out_specs=tok(DN_COLS),
        out_shape=jax.ShapeDtypeStruct((t, DN_COLS), MXU_DTYPE),
        scratch_shapes=[pltpu.VMEM((tm + HALO, CONV_CH), F32)],
        compiler_params=_params("parallel"),
    )(dy, dy, dz, dba, conv_w)


def _pad_lanes(v, offset=0):
    return jnp.zeros((1, 128), F32).at[0, offset:offset + v.shape[0]].set(v.astype(F32))


class _LocalReducer:
    def start(self, grads):
        return jnp.zeros((8, 128), F32)

    def middle(self, after):
        return jnp.zeros((8, 128), F32)

    def finish(self, after):
        return None


def _local_step(x, p, tgt, sm, w, late, reducer):
    t = x.shape[0]
    tm = min(512, t // 2)
    tm_s = min(256, t // 2)
    tw = min(1024, t // 2)

    w_in = w["w_in"]
    wa = w_in[:, :ATTN_Q + 2 * ATTN_KV]
    wd = jnp.pad(w_in[:, ATTN_Q + 2 * ATTN_KV:], ((0, 0), (0, DN_COLS - (D_IN - ATTN_Q - 2 * ATTN_KV))))
    conv_w = w["conv_w"]
    al_vec, dt_vec = _pad_lanes(sm["a_log"], DN_HEADS), _pad_lanes(sm["dt_bias"], DN_HEADS)
    sinks_vec = _pad_lanes(sm["sinks"])
    dn_w = sm["dn_norm"].reshape(1, 128)
    row = lambda v: v.reshape(1, D_MODEL)
    cos, sin = _rope_tables(t)

    u, pa, pd = _inproj(x, row(sm["norm_mix"]), wa, wd, tm_s)
    ao = _attn_fwd(pa, cos, sin, sinks_vec)
    qn, kn, vc, gc, gr = _dn_prep(pd, conv_w, al_vec, dt_vec, tm_s)
    uu, ww, qg, kd, a_qk, tinv, dlast = _dn_intra(qn, kn, vc, gc, gr)
    dn_out, o, vnew, sall = _dn_scan_fwd(uu, ww, qg, kd, a_qk, dlast, pd, dn_w)
    w = dict(w, **late(dn_out))
    wo_a, wo_d = w["w_o"][:ATTN_Q], w["w_o"][ATTN_Q:]
    w_proj = jnp.transpose(w["w_proj4"], (1, 0, 2)).reshape(PLE_DIM, D_MODEL)
    h1 = _oproj(x, ao, dn_out, wo_a, wo_d, tm)
    m, r, h2 = _mlp_fwd(h1, row(sm["norm_mlp"]), w["w_up4"], w["w_down"], tw)
    dh2, dh2b, dgp, dpp, n3, pb, acc_ple = _ple_loss(h2, p, tgt, row(sm["norm_ple"]), row(sm["norm_final"]),
                                                     w["w_gate"], w_proj, tm_s)
    g_w_gate = _wgrad(n3, dgp, "wgrad_gate", D_MODEL, D_MODEL, tw)
    g_w_proj = _wgrad(pb, dpp, "wgrad_proj", PLE_DIM, D_MODEL, tw)
    da, dh1, dh1b, acc_mlp = _mlp_bwd(dh2, dh2b, r, h1, row(sm["norm_mlp"]), w["w_up4"], w["w_down"], tm)
    g_w_up4 = _wgrad(m, da, "wgrad_up", D_MODEL, FF_BLOCK, tw, stacked=True)
    g_w_down = _wgrad(r, dh2b, "wgrad_down", FF_BLOCK, D_MODEL, tw,
                      prep=lambda rv: jnp.square(rv.astype(F32)).astype(MXU_DTYPE))
    g_w_o = jnp.concatenate([_wgrad(ao, dh1b, "wgrad_oa", ATTN_Q, D_MODEL, tw),
                             _wgrad(dn_out, dh1b, "wgrad_od", DN_W, D_MODEL, tw)], axis=0)
    early = dict(w_up4=g_w_up4, w_down=g_w_down, w_gate=g_w_gate, w_proj=g_w_proj, w_o=g_w_o)
    dep = reducer.start(early)
    dao, ddn = _oproj_bwd(dh1b, wo_a, wo_d, tm, dep)
    dz, du, dw, dqg, dkd, da_qk, ddlast, acc_dn = _dn_scan_bwd(ddn, o, vnew, sall, ww, qg, kd, a_qk, dlast, pd, dn_w)
    dep = reducer.middle(du)
    dqn, dkn, dvc, dgc = _dn_intra_bwd(qn, kn, vc, gc, gr, tinv, a_qk, du, dw, dqg, dkd, da_qk, ddlast, dlast, dep)
    dy, dba, acc_conv, acc_gate = _dn_prep_bwd(pd, conv_w, al_vec, dt_vec, dqn, dkn, dvc, dgc, tm_s)
    d_dn = _dn_conv_bwd(dy, dz, dba, conv_w, tm_s)
    dq, dk, dv, acc_attn = _attn_bwd(pa, cos, sin, sinks_vec, dao)
    reducer.finish(dq)
    wq, wk, wv = wa[:, :ATTN_Q], wa[:, ATTN_Q:ATTN_Q + ATTN_KV], wa[:, ATTN_Q + ATTN_KV:]
    dx, acc_mix = _inproj_bwd(x, dh1, row(sm["norm_mix"]), [dq, dk, dv, d_dn], [wq, wk, wv, wd], tm_s)

    g_w_in = jnp.concatenate([
        _wgrad(u, dq, "wgrad_q", D_MODEL, ATTN_Q, tw), _wgrad(u, dk, "wgrad_k", D_MODEL, ATTN_KV, tw),
        _wgrad(u, dv, "wgrad_v", D_MODEL, ATTN_KV, tw),
        _wgrad(u, d_dn, "wgrad_dn", D_MODEL, DN_COLS, tw)[:, :D_IN - ATTN_Q - 2 * ATTN_KV]], axis=1)
    grads = dict(early, w_in=g_w_in)
    sums = dict(loss=acc_ple[2, 0], norm_final=acc_ple[0], norm_ple=acc_ple[1], norm_mlp=acc_mlp[0], norm_mix=acc_mix[0],
                dn_norm=acc_dn[0], sinks=acc_attn[0, :ATTN_HEADS], a_log=acc_gate[0, DN_HEADS:2 * DN_HEADS],
                dt_bias=acc_gate[1, DN_HEADS:2 * DN_HEADS], conv_w=acc_conv[:DN_CONV])
    return sums, dx, grads


MESH = pl.DeviceIdType.MESH
ANY = pl.BlockSpec(memory_space=pl.ANY)
N_CHIPS = 4
N_DEV = 8


def _place():
    x, y, c = lax.axis_index("x"), lax.axis_index("y"), lax.axis_index("c")
    chips = [(1 - x, y), (x, 1 - y), (1 - x, 1 - y)]
    return x, y, c, chips


def _gather_weights(shards, conv_s):
    n = len(shards)
    per = 7

    def body(*refs):
        in_refs, conv_ref = refs[:n], refs[n]
        out_refs, conv_out = refs[n + 1:2 * n + 1], refs[2 * n + 1]
        send_sems, recv_sems = refs[2 * n + 2:]
        x, y, c, chips = _place()
        sibling = (x, y, 1 - c)

        def blk(a, px, py, pc):
            hr = in_refs[a].shape[0] // 2
            return out_refs[a].at[2 * px + py, pl.ds(pc * hr, hr), :]

        def mine(a):
            hr = in_refs[a].shape[0] // 2
            return in_refs[a].at[pl.ds(c * hr, hr), :]

        def rcopy(a, k, block, to, src=None):
            return pltpu.make_async_remote_copy(
                src_ref=blk(a, *block) if src is None else src, dst_ref=blk(a, *block),
                send_sem=send_sems.at[per * a + k], recv_sem=recv_sems.at[per * a + k],
                device_id=to, device_id_type=MESH)

        def whole(a, to):
            return pltpu.make_async_remote_copy(
                src_ref=in_refs[a], dst_ref=out_refs[a].at[2 * x + y],
                send_sem=send_sems.at[per * a], recv_sem=recv_sems.at[per * a], device_id=to, device_id_type=MESH)

        def ccopy(j, to):
            return pltpu.make_async_remote_copy(
                src_ref=conv_ref, dst_ref=conv_out.at[2 * x + y],
                send_sem=send_sems.at[per * n + j], recv_sem=recv_sems.at[per * n + j],
                device_id=to, device_id_type=MESH)

        started = []
        for a in range(n):
            first = [whole(a, sibling)]
            first += [rcopy(a, 1 + j, (x, y, c), (*chip, c), src=mine(a)) for j, chip in enumerate(chips)]
            for cp in first:
                cp.start()
            started += first
        conv_sends = [ccopy(j, (*chip, c)) for j, chip in enumerate(chips)] + [ccopy(3, sibling)]
        for cp in conv_sends:
            cp.start()
        started += conv_sends
        for a in range(n):
            for j, chip in enumerate(chips):
                rcopy(a, 1 + j, (*chip, c), (x, y, c)).wait_recv()
                fwd = rcopy(a, 4 + j, (*chip, c), sibling)
                fwd.start()
                started.append(fwd)
        for a in range(n):
            whole(a, sibling).wait_recv()
            for j, chip in enumerate(chips):
                rcopy(a, 4 + j, (*chip, 1 - c), (x, y, c)).wait_recv()
        for j, chip in enumerate(chips + [(x, y)]):
            pltpu.make_async_remote_copy(
                src_ref=conv_ref, dst_ref=conv_out.at[2 * chip[0] + chip[1]],
                send_sem=send_sems.at[per * n + j], recv_sem=recv_sems.at[per * n + j],
                device_id=sibling, device_id_type=MESH).wait_recv()
        for cp in started:
            cp.wait_send()

    nsem = per * n + 4
    out_shape = [jax.ShapeDtypeStruct((N_CHIPS,) + s.shape, s.dtype) for s in shards]
    out_shape.append(jax.ShapeDtypeStruct((N_CHIPS,) + conv_s.shape, conv_s.dtype))
    return pl.pallas_call(
        body, name="gather_weights", in_specs=[ANY] * (n + 1), out_specs=[ANY] * (n + 1), out_shape=out_shape,
        scratch_shapes=[pltpu.SemaphoreType.DMA((nsem,)), pltpu.SemaphoreType.DMA((nsem,))],
    )(*shards, conv_s)


HBM = pl.BlockSpec(memory_space=pltpu.HBM)
SEM = pl.BlockSpec(memory_space=pltpu.SEMAPHORE)
EFFECT = pltpu.SideEffectType.DATAFLOW_SIDE_EFFECTING
LATE_COPIES = 7


def _late_copies(in_refs, land_refs, send_sems, recv_sems):
    x, y, c, chips = _place()
    sends, arrivals = [], []
    for a, (src, land) in enumerate(zip(in_refs, land_refs)):
        hr = src.shape[0] // 2
        base = LATE_COPIES * a

        def cp(src_ref, dst_ref, s_idx, r_idx, to):
            return pltpu.make_async_remote_copy(src_ref=src_ref, dst_ref=dst_ref, send_sem=send_sems.at[base + s_idx],
                                                recv_sem=recv_sems.at[base + r_idx], device_id=to, device_id_type=MESH)

        sends.append(cp(src, land.at[2 * x + y], 0, 0, (x, y, 1 - c)))
        arrivals.append(cp(src, land.at[2 * x + y], 0, 0, (x, y, 1 - c)))
        for j, chip in enumerate(chips):
            for pc in range(2):
                half = src.at[pl.ds(c * hr, hr), :]
                sends.append(cp(half, land.at[2 * x + y, pl.ds(c * hr, hr), :], 1 + 2 * j + pc, 1 + 2 * j + c, (*chip, pc)))
                arrivals.append(cp(half, land.at[2 * chip[0] + chip[1], pl.ds(pc * hr, hr), :], 1 + 2 * j + pc,
                                   1 + 2 * j + pc, (*chip, pc)))
    return sends, arrivals


def _copies_start(name, build, nsem, srcs, land_shapes, after):
    n = len(srcs)

    def body(*refs):
        sends, _ = build(refs[:n], refs[n:2 * n], refs[2 * n + 1], refs[2 * n + 2])
        for cp in sends:
            cp.start()
        refs[-1][...] = jnp.zeros_like(refs[-1])

    lands = [pltpu.with_memory_space_constraint(lax.empty(s.shape, s.dtype), pltpu.HBM) for s in land_shapes]
    ins = [pltpu.with_memory_space_constraint(s, pltpu.HBM) for s in srcs]
    out = pl.pallas_call(
        body, name=name,
        out_shape=(pltpu.SemaphoreType.DMA((nsem,)), pltpu.SemaphoreType.DMA((nsem,)),
                   *[pltpu.HBM(s.shape, s.dtype) for s in srcs], *[pltpu.HBM(s.shape, s.dtype) for s in land_shapes],
                   jax.ShapeDtypeStruct((8, 128), F32)),
        in_specs=[HBM] * (2 * n) + [ANY],
        out_specs=(SEM, SEM, *[HBM] * (2 * n), pl.BlockSpec(memory_space=pltpu.VMEM)),
        input_output_aliases={i: 2 + i for i in range(2 * n)},
        compiler_params=pltpu.CompilerParams(has_side_effects=EFFECT),
    )(*ins, *lands, after)
    return out[0], out[1], out[2:2 + n], out[2 + n:2 + 2 * n], out[-1]


def _copies_wait(name, build, started, after):
    send_sems, recv_sems, srcs, lands, _ = started
    n = len(srcs)

    def body(*refs):
        sends, arrivals = build(refs[:n], refs[n:2 * n], refs[2 * n], refs[2 * n + 1])
        for cp in sends:
            cp.wait_send()
        for cp in arrivals:
            cp.wait_recv()

    out = pl.pallas_call(
        body, name=name,
        out_shape=(*[pltpu.HBM(s.shape, s.dtype) for s in srcs], *[pltpu.HBM(l.shape, l.dtype) for l in lands]),
        in_specs=[HBM] * (2 * n) + [SEM, SEM, ANY],
        out_specs=tuple([HBM] * (2 * n)),
        input_output_aliases={i: i for i in range(2 * n)},
        compiler_params=pltpu.CompilerParams(has_side_effects=EFFECT),
    )(*srcs, *lands, send_sems, recv_sems, after)
    return out[:n], out[n:]


def _exchange_copies(g_refs, got_refs, send_sems, recv_sems):
    x, y, c, _ = _place()
    sends, arrivals = [], []
    for a, (g, got) in enumerate(zip(g_refs, got_refs)):
        hr = g.shape[1] // 2
        cp = pltpu.make_async_remote_copy(
            src_ref=g.at[:, pl.ds((1 - c) * hr, hr), :], dst_ref=got, send_sem=send_sems.at[a],
            recv_sem=recv_sems.at[a], device_id=(x, y, 1 - c), device_id_type=MESH)
        sends.append(cp)
        arrivals.append(cp)
    return sends, arrivals


def _scatter_copies(s_refs, got_refs, send_sems, recv_sems):
    x, y, c, chips = _place()
    sends, arrivals = [], []
    for a, (s16, got) in enumerate(zip(s_refs, got_refs)):
        for j, chip in enumerate(chips):
            cp = pltpu.make_async_remote_copy(
                src_ref=s16.at[2 * chip[0] + chip[1]], dst_ref=got.at[j], send_sem=send_sems.at[3 * a + j],
                recv_sem=recv_sems.at[3 * a + j], device_id=(*chip, c), device_id_type=MESH)
            sends.append(cp)
            arrivals.append(cp)
    return sends, arrivals


def _share_halves(name, bufs):
    n = len(bufs)

    def body(*refs):
        out_refs = refs[n:2 * n]
        send_sems, recv_sems = refs[2 * n:]
        x, y, c, _ = _place()
        remote = [pltpu.make_async_remote_copy(
            src_ref=out_refs[a].at[c], dst_ref=out_refs[a].at[c], send_sem=send_sems.at[a], recv_sem=recv_sems.at[a],
            device_id=(x, y, 1 - c), device_id_type=MESH) for a in range(n)]
        for cp in remote:
            cp.start()
        for a in range(n):
            pltpu.make_async_remote_copy(
                src_ref=out_refs[a].at[c], dst_ref=out_refs[a].at[1 - c], send_sem=send_sems.at[a],
                recv_sem=recv_sems.at[a], device_id=(x, y, 1 - c), device_id_type=MESH).wait_recv()
        for cp in remote:
            cp.wait_send()

    return pl.pallas_call(
        body, name=name, in_specs=[ANY] * n, out_specs=[ANY] * n,
        out_shape=[jax.ShapeDtypeStruct(b.shape, b.dtype) for b in bufs],
        input_output_aliases={a: a for a in range(n)},
        scratch_shapes=[pltpu.SemaphoreType.DMA((n,)), pltpu.SemaphoreType.DMA((n,))],
    )(*bufs)


SMALL_ROWS, SMALL_COLS = 16, CONV_CH


def _allreduce_small(block):
    m_per, ncol = block.shape

    def body(x_ref, sum_ref, all_ref, send_sems, recv_sems, local_sem):
        x, y, c, chips = _place()
        me, sibling = (x, y, c), (x, y, 1 - c)

        def rows(px, py, pc):
            return all_ref.at[pl.ds((4 * px + 2 * py + pc) * m_per, m_per), :]

        def copy(k, block_of, to, src=None):
            return pltpu.make_async_remote_copy(
                src_ref=rows(*block_of) if src is None else src, dst_ref=rows(*block_of),
                send_sem=send_sems.at[k], recv_sem=recv_sems.at[k], device_id=to, device_id_type=MESH)

        mine = pltpu.make_async_copy(x_ref, rows(*me), local_sem)
        mine.start()
        first = [copy(0, me, sibling, src=x_ref)]
        first += [copy(1 + j, me, (*chip, c), src=x_ref) for j, chip in enumerate(chips)]
        for cp in first:
            cp.start()
        passed = [copy(4 + j, (*chip, c), sibling) for j, chip in enumerate(chips)]
        for j, chip in enumerate(chips):
            copy(1 + j, (*chip, c), me).wait_recv()
            passed[j].start()
        copy(0, sibling, me).wait_recv()
        for j, chip in enumerate(chips):
            copy(4 + j, (*chip, 1 - c), me).wait_recv()
        for cp in first + passed:
            cp.wait_send()
        mine.wait()
        total = all_ref[0:m_per, :]
        for d in range(1, N_DEV):
            total = total + all_ref[d * m_per:(d + 1) * m_per, :]
        sum_ref[...] = total

    vm = pl.BlockSpec(memory_space=pltpu.VMEM)
    return pl.pallas_call(
        body, name="allreduce_small", in_specs=[vm], out_specs=vm,
        out_shape=jax.ShapeDtypeStruct((m_per, ncol), F32),
        scratch_shapes=[pltpu.VMEM((N_DEV * m_per, ncol), F32), pltpu.SemaphoreType.DMA((7,)),
                        pltpu.SemaphoreType.DMA((7,)), pltpu.SemaphoreType.DMA],
    )(block)


def _row_tile(rows, cols):
    tile = rows
    while tile * cols * 4 > (1 << 20) and tile % 16 == 0:
        tile //= 2
    return tile


def _elementwise(fn, name, ins, out_dtypes, dep):
    rows, cols = ins[0].shape
    tile = _row_tile(rows, cols)

    def body(*refs):
        outs = fn(*[r[...] for r in refs[:len(ins)]])
        for o_ref, o in zip(refs[len(ins) + 1:], outs):
            o_ref[...] = o.astype(o_ref.dtype)

    spec = pl.BlockSpec((tile, cols), lambda i: (i, 0))
    return pl.pallas_call(
        body, name=name, grid=(rows // tile,), in_specs=[spec] * len(ins) + [pl.BlockSpec(memory_space=pl.ANY)],
        out_specs=[spec] * len(out_dtypes),
        out_shape=[jax.ShapeDtypeStruct((rows, cols), d) for d in out_dtypes],
        compiler_params=_params("parallel"),
    )(*ins, dep)


def _adamw_tile(w, g, m, v):
    m = ADAM_B1 * m + (1.0 - ADAM_B1) * g
    v = ADAM_B2 * v + (1.0 - ADAM_B2) * jnp.square(g)
    m_hat = m / (1.0 - ADAM_B1 ** ADAM_STEP)
    v_hat = v / (1.0 - ADAM_B2 ** ADAM_STEP)
    delta = -ADAM_LR * (m_hat / (jnp.sqrt(v_hat) + ADAM_EPS) + ADAM_WD * w)
    return delta, m, v


def _adamw(name, w, g, m, v, dep):
    return _elementwise(_adamw_tile, name, [w, g, m, v], [F32, F32, F32], dep)


def _chip_sum(name, g4, got, place):
    nchip, hr, cols = got.shape
    tile = _row_tile(hr, cols)
    nblk = hr // tile

    def body(pl_ref, g_ref, o_ref, s32_ref, s16_ref):
        s = g_ref[...] + o_ref[...]
        s32_ref[...] = s
        s16_ref[...] = s.astype(BF16)

    spec = pl.BlockSpec((None, tile, cols), lambda k, i, pr: (k, i, 0))
    return pl.pallas_call(
        body, name=name,
        grid_spec=pltpu.PrefetchScalarGridSpec(
            num_scalar_prefetch=1, grid=(nchip, nblk),
            in_specs=[pl.BlockSpec((None, tile, cols), lambda k, i, pr: (k, pr[1] * nblk + i, 0)), spec],
            out_specs=[spec, spec]),
        out_shape=[jax.ShapeDtypeStruct(got.shape, F32), jax.ShapeDtypeStruct(got.shape, BF16)],
        compiler_params=_params("parallel", "parallel"),
    )(place, g4, got)


def _mesh_sum(name, s32, got, place):
    _, hr, cols = s32.shape
    tile = _row_tile(hr, cols)

    def body(pl_ref, own_ref, g0_ref, g1_ref, g2_ref, o_ref):
        o_ref[...] = ((own_ref[...] + g0_ref[...].astype(F32)) + g1_ref[...].astype(F32)) + g2_ref[...].astype(F32)

    slab = lambda j: pl.BlockSpec((None, tile, cols), lambda i, pr: (j, i, 0))
    return pl.pallas_call(
        body, name=name,
        grid_spec=pltpu.PrefetchScalarGridSpec(
            num_scalar_prefetch=1, grid=(hr // tile,),
            in_specs=[pl.BlockSpec((None, tile, cols), lambda i, pr: (pr[0], i, 0)), slab(0), slab(1), slab(2)],
            out_specs=pl.BlockSpec((None, tile, cols), lambda i, pr: (pr[1], i, 0))),
        out_shape=jax.ShapeDtypeStruct((2, hr, cols), F32),
        compiler_params=_params("parallel"),
    )(place, s32, got, got, got)


def _place_operand():
    return jnp.stack([2 * lax.axis_index("x") + lax.axis_index("y"), lax.axis_index("c")]).astype(jnp.int32)


def _per_chip(name, g):
    if name == "w_in":
        return jnp.transpose(g.reshape(D_MODEL, N_CHIPS, D_IN // N_CHIPS), (1, 0, 2))
    if name == "w_proj":
        return jnp.transpose(g.reshape(PLE_DIM, N_CHIPS, D_MODEL // N_CHIPS), (1, 0, 2))
    if name == "w_up4":
        return g
    return g.reshape(N_CHIPS, g.shape[0] // N_CHIPS, g.shape[1])


class _EarlyReducer:
    def __init__(self, tag):
        self.tag = tag

    def start(self, grads):
        self.names = list(grads)
        self.place = _place_operand()
        slabs = [_per_chip(k, grads[k]) for k in self.names]
        halves = [jax.ShapeDtypeStruct((s.shape[0], s.shape[1] // 2, s.shape[2]), F32) for s in slabs]
        self.a = _copies_start(self.tag + "exchange_start", _exchange_copies, len(slabs), slabs, halves, slabs[0])
        return self.a[-1]

    def middle(self, after):
        slabs, got = _copies_wait(self.tag + "exchange_wait", _exchange_copies, self.a, after)
        self.sums = [_chip_sum(self.tag + "chip_sum_" + k, s, g, self.place) for k, s, g in zip(self.names, slabs, got)]
        s16 = [s[1] for s in self.sums]
        lands = [jax.ShapeDtypeStruct((3,) + s.shape[1:], BF16) for s in s16]
        self.b = _copies_start(self.tag + "scatter_start", _scatter_copies, 3 * len(s16), s16, lands, s16[0])
        return self.b[-1]

    def finish(self, after):
        _, got = _copies_wait(self.tag + "scatter_wait", _scatter_copies, self.b, after)
        self.bufs = {k: _mesh_sum(self.tag + "mesh_sum_" + k, s[0], g, self.place)
                     for k, s, g in zip(self.names, self.sums, got)}


def kernel(x, p, norm_mix, w_in, conv_w, a_log, dt_bias, dn_norm, sinks, w_o, norm_mlp, w_up, w_down, norm_ple, w_ple_gate, w_ple_proj, norm_final, loss_target, m_norm_mix, m_w_in, m_conv_w, m_a_log, m_dt_bias, m_dn_norm, m_sinks, m_w_o, m_norm_mlp, m_w_up, m_w_down, m_norm_ple, m_w_ple_gate, m_w_ple_proj, m_norm_final, v_norm_mix, v_w_in, v_conv_w, v_a_log, v_dt_bias, v_dn_norm, v_sinks, v_w_o, v_norm_mlp, v_w_up, v_w_down, v_norm_ple, v_w_ple_gate, v_w_ple_proj, v_norm_final):
    chip = 2 * lax.axis_index("x") + lax.axis_index("y")
    big = dict(w_in=w_in[0], w_o=w_o[0], w_up=w_up[0], w_down=w_down[0], w_gate=w_ple_gate[0], w_proj=w_ple_proj[0])
    big_m = dict(w_in=m_w_in[0], w_o=m_w_o[0], w_up=m_w_up[0], w_down=m_w_down[0], w_gate=m_w_ple_gate[0], w_proj=m_w_ple_proj[0])
    big_v = dict(w_in=v_w_in[0], w_o=v_w_o[0], w_up=v_w_up[0], w_down=v_w_down[0], w_gate=v_w_ple_gate[0], w_proj=v_w_ple_proj[0])
    names = list(big)

    w_in_all, conv_all = _gather_weights([big["w_in"].astype(BF16)], conv_w[0])
    late_names = names[1:]
    late_shards = [big[k].astype(BF16) for k in late_names]
    gather = _copies_start("gather_start", _late_copies, LATE_COPIES * len(late_shards), late_shards,
                           [jax.ShapeDtypeStruct((N_CHIPS,) + s.shape, BF16) for s in late_shards], w_in_all)
    token = gather[-1]
    w = dict(w_in=jnp.transpose(w_in_all, (1, 0, 2)).reshape(D_MODEL, D_IN),
             conv_w=jnp.transpose(conv_all, (1, 0, 2)).reshape(DN_CONV, CONV_CH))
    sm = dict(norm_mix=norm_mix[0] + token[0, 0], a_log=a_log[0], dt_bias=dt_bias[0], dn_norm=dn_norm[0],
              sinks=sinks[0], norm_mlp=norm_mlp[0], norm_ple=norm_ple[0], norm_final=norm_final)

    def late(after):
        gw = dict(zip(late_names, _copies_wait("gather_wait", _late_copies, gather, after)[1]))
        return dict(w_o=gw["w_o"].reshape(D_MODEL, D_MODEL), w_up4=gw["w_up"], w_down=gw["w_down"].reshape(D_FF, D_MODEL),
                    w_gate=gw["w_gate"].reshape(D_MODEL, D_MODEL), w_proj4=gw["w_proj"])

    reducer = _EarlyReducer("early_")
    sums, grad_x, g = _local_step(x[0], p[0, 0], loss_target[0], sm, w, late, reducer)

    last = _EarlyReducer("last_")
    dep_a = last.start({"w_in": g["w_in"]})
    grad_key = dict(w_o="w_o", w_up="w_up4", w_down="w_down", w_gate="w_gate", w_proj="w_proj")
    full = _share_halves("share_halves", [reducer.bufs[grad_key[k]] for k in late_names])
    red = {k: f.reshape(-1, f.shape[-1]) for k, f in zip(late_names, full)}

    row = lambda v: jnp.zeros((SMALL_COLS,), F32).at[:v.shape[0]].set(v)
    misc = jnp.zeros((SMALL_COLS,), F32).at[0:4].set(sums["a_log"]).at[4:8].set(sums["dt_bias"]) \
        .at[8:16].set(sums["sinks"]).at[128:256].set(sums["dn_norm"]).at[256].set(sums["loss"])
    small = jnp.concatenate([sums["conv_w"], jnp.stack([row(sums["norm_mix"]), row(sums["norm_mlp"]), row(sums["norm_ple"]),
                                                        row(sums["norm_final"]), misc]),
                             jnp.zeros((SMALL_ROWS - 9, SMALL_COLS), F32)], axis=0)
    tot = _allreduce_small(small)
    loss = tot[8, 256]
    ncw = CONV_CH // N_CHIPS

    def pack(cw, nmix, nmlp, nple, nfin, al, dtb, sk, dnn):
        misc_p = jnp.zeros((SMALL_COLS,), F32).at[0:4].set(al).at[4:8].set(dtb).at[8:16].set(sk).at[128:256].set(dnn)
        cw_p = jnp.zeros((DN_CONV, SMALL_COLS), F32).at[:, :ncw].set(cw)
        return jnp.concatenate([cw_p, jnp.stack([row(nmix), row(nmlp), row(nple), row(nfin), misc_p]),
                                jnp.zeros((SMALL_ROWS - 9, SMALL_COLS), F32)], axis=0)

    def unpack(buf):
        return dict(conv_w=buf[0:4, :ncw][None], norm_mix=buf[4, :D_MODEL][None], norm_mlp=buf[5, :D_MODEL][None],
                    norm_ple=buf[6, :D_MODEL][None], norm_final=buf[7, :D_MODEL], a_log=buf[8, 0:4][None],
                    dt_bias=buf[8, 4:8][None], sinks=buf[8, 8:16][None], dn_norm=buf[8, 128:256][None])

    g_conv_shard = lax.dynamic_slice(tot[0:4], (0, chip * ncw), (DN_CONV, ncw))
    g_small = pack(g_conv_shard, tot[4, :D_MODEL], tot[5, :D_MODEL], tot[6, :D_MODEL], tot[7, :D_MODEL],
                   tot[8, 0:4], tot[8, 4:8], tot[8, 8:16], tot[8, 128:256])
    w_small = pack(conv_w[0], norm_mix[0], norm_mlp[0], norm_ple[0], norm_final, a_log[0], dt_bias[0], sinks[0], dn_norm[0])
    m_small = pack(m_conv_w[0], m_norm_mix[0], m_norm_mlp[0], m_norm_ple[0], m_norm_final, m_a_log[0], m_dt_bias[0],
                   m_sinks[0], m_dn_norm[0])
    v_small = pack(v_conv_w[0], v_norm_mix[0], v_norm_mlp[0], v_norm_ple[0], v_norm_final, v_a_log[0], v_dt_bias[0],
                   v_sinks[0], v_dn_norm[0])

    ref_name = dict(w_in="w_in", w_o="w_o", w_up="w_up", w_down="w_down", w_gate="w_ple_gate", w_proj="w_ple_proj")
    out_g, out_d, out_m, out_v = {}, {}, {}, {}

    def update(k, dep):
        d_k, m_k, v_k = _adamw("adamw_" + k, big[k], red[k], big_m[k], big_v[k], dep)
        out_g[ref_name[k]], out_d[ref_name[k]] = red[k][None], d_k[None]
        out_m[ref_name[k]], out_v[ref_name[k]] = m_k[None], v_k[None]
        return d_k

    done = update("w_up", dep_a)
    dep_b = last.middle(done)
    for k in ("w_down", "w_o", "w_gate", "w_proj"):
        done = update(k, dep_b)
    small_out = _adamw("adamw_small", w_small, g_small, m_small, v_small, dep_b)
    d_s, m_s, v_s = (unpack(b) for b in small_out)
    g_s = unpack(g_small)
    for src, dst in ((g_s, out_g), (d_s, out_d), (m_s, out_m), (v_s, out_v)):
        dst.update(src)
    last.finish(done + small_out[0][0:1, 0:1])
    (w_in_full,) = _share_halves("share_halves_w_in", [last.bufs["w_in"]])
    red["w_in"] = w_in_full.reshape(-1, w_in_full.shape[-1])
    update("w_in", dep_b)
    order = ["norm_mix", "w_in", "conv_w", "a_log", "dt_bias", "dn_norm", "sinks", "w_o", "norm_mlp", "w_up", "w_down",
             "norm_ple", "w_ple_gate", "w_ple_proj", "norm_final"]
    return (loss, grad_x[None], *[out_g[k] for k in order], *[out_d[k] for k in order],
            *[out_m[k] for k in order], *[out_v[k] for k in order])
```

```python
import functools

import jax
import jax.numpy as jnp
from jax import lax
from jax.experimental import pallas as pl
from jax.experimental.pallas import tpu as pltpu

F32 = jnp.float32
BF16 = jnp.bfloat16
MXU_DTYPE = jnp.bfloat16
HI = lax.Precision.HIGHEST

D_MODEL = 1024
PLE_DIM = 256
ATTN_HEADS = 8
ATTN_KV_HEADS = 2
ATTN_GROUPS = ATTN_HEADS // ATTN_KV_HEADS
ATTN_HEAD_DIM = 64
ATTN_BLOCK = 128
ROPE_THETA = 10000.0
DN_HEADS = 4
DN_HEAD_DIM = 128
DN_CONV = 4
DN_CHUNK = 64
D_FF = 4 * D_MODEL
EPS = 1e-6
ATTN_Q = ATTN_HEADS * ATTN_HEAD_DIM
ATTN_KV = ATTN_KV_HEADS * ATTN_HEAD_DIM
DN_W = DN_HEADS * DN_HEAD_DIM
CONV_CH = 3 * DN_W
D_IN = ATTN_Q + 2 * ATTN_KV + 4 * DN_W + 2 * DN_HEADS
DN_COLS = 4 * DN_W + 128
DN_SCALE = DN_HEAD_DIM ** -0.5
ATTN_SCALE = ATTN_HEAD_DIM ** -0.5
FF_BLOCKS = 4
FF_BLOCK = D_FF // FF_BLOCKS

ADAM_LR = 0.001
ADAM_B1 = 0.9
ADAM_B2 = 0.999
ADAM_EPS = 1e-08
ADAM_WD = 0.01
ADAM_STEP = 10

V7X_VMEM_BYTES = 64 * 1024 * 1024
VMEM_LIMIT = 48 * 1024 * 1024

NN = ((1,), (0,))
NT = ((1,), (1,))
TN = ((0,), (0,))


def _dot(a, b, dims=NN, prec=None):
    return lax.dot_general(a, b, (dims, ((), ())), precision=prec, preferred_element_type=F32)


def _sigmoid(x):
    return 1.0 / (1.0 + jnp.exp(-x))


def _softplus(x):
    return jnp.maximum(x, 0.0) + jnp.log(1.0 + jnp.exp(-jnp.abs(x)))


def _params(*sem):
    return pltpu.CompilerParams(dimension_semantics=sem, vmem_limit_bytes=VMEM_LIMIT)


def _rms_fwd(xv, g):
    r = lax.rsqrt(jnp.mean(xv * xv, axis=-1, keepdims=True) + EPS)
    return xv * r * g


def _rms_bwd(xv, g, dn):
    r = lax.rsqrt(jnp.mean(xv * xv, axis=-1, keepdims=True) + EPS)
    xh = xv * r
    dg = jnp.sum(dn * xh, axis=0, keepdims=True)
    dxh = dn * g
    dx = r * (dxh - xh * jnp.mean(dxh * xh, axis=-1, keepdims=True))
    return dx, dg


def _full(shape):
    return pl.BlockSpec(shape, lambda *_: (0,) * len(shape))


def _inproj(x, g_mix, wa, wd, tm):
    t = x.shape[0]

    def body(x_ref, g_ref, wa_ref, wd_ref, u_ref, pa_ref, pd_ref):
        u = _rms_fwd(x_ref[...], g_ref[...]).astype(MXU_DTYPE)
        u_ref[...] = u
        pa_ref[...] = _dot(u, wa_ref[...])
        pd_ref[...] = _dot(u, wd_ref[...])

    na, nd = wa.shape[1], wd.shape[1]
    return pl.pallas_call(
        body, name="inproj", grid=(t // tm,),
        in_specs=[pl.BlockSpec((tm, D_MODEL), lambda i: (i, 0)), _full((1, D_MODEL)),
                  _full((D_MODEL, na)), _full((D_MODEL, nd))],
        out_specs=[pl.BlockSpec((tm, D_MODEL), lambda i: (i, 0)), pl.BlockSpec((tm, na), lambda i: (i, 0)),
                   pl.BlockSpec((tm, nd), lambda i: (i, 0))],
        out_shape=[jax.ShapeDtypeStruct((t, D_MODEL), MXU_DTYPE), jax.ShapeDtypeStruct((t, na), F32),
                   jax.ShapeDtypeStruct((t, nd), F32)],
        compiler_params=_params("parallel"),
    )(x, g_mix, wa, wd)


def _oproj(x, ao, dn, wo_a, wo_d, tm):
    t = x.shape[0]

    def body(x_ref, ao_ref, dn_ref, wa_ref, wd_ref, h_ref):
        h_ref[...] = (x_ref[...] + _dot(ao_ref[...].astype(MXU_DTYPE), wa_ref[...])
                      + _dot(dn_ref[...].astype(MXU_DTYPE), wd_ref[...]))

    half = ao.shape[1]
    return pl.pallas_call(
        body, name="oproj", grid=(t // tm,),
        in_specs=[pl.BlockSpec((tm, D_MODEL), lambda i: (i, 0)), pl.BlockSpec((tm, half), lambda i: (i, 0)),
                  pl.BlockSpec((tm, half), lambda i: (i, 0)), _full((half, D_MODEL)), _full((half, D_MODEL))],
        out_specs=pl.BlockSpec((tm, D_MODEL), lambda i: (i, 0)),
        out_shape=jax.ShapeDtypeStruct((t, D_MODEL), F32),
        compiler_params=_params("parallel"),
    )(x, ao, dn, wo_a, wo_d)


def _mlp_fwd(h1, g_mlp, w_up4, w_down, tm):
    t = h1.shape[0]

    def body(h_ref, g_ref, wu_ref, wd_ref, m_ref, r_ref, h2_ref, acc_ref):
        k = pl.program_id(1)

        @pl.when(k == 0)
        def _():
            m_ref[...] = _rms_fwd(h_ref[...], g_ref[...]).astype(MXU_DTYPE)
            acc_ref[...] = jnp.zeros_like(acc_ref)

        r = jnp.maximum(_dot(m_ref[...], wu_ref[...]), 0.0)
        r_ref[...] = r.astype(MXU_DTYPE)
        s = jnp.square(r).astype(MXU_DTYPE)
        acc_ref[...] += _dot(s, wd_ref[...])

        @pl.when(k == FF_BLOCKS - 1)
        def _():
            h2_ref[...] = h_ref[...] + acc_ref[...]

    return pl.pallas_call(
        body, name="mlp_fwd", grid=(t // tm, FF_BLOCKS),
        in_specs=[pl.BlockSpec((tm, D_MODEL), lambda i, k: (i, 0)), _full((1, D_MODEL)),
                  pl.BlockSpec((None, D_MODEL, FF_BLOCK), lambda i, k: (k, 0, 0)),
                  pl.BlockSpec((FF_BLOCK, D_MODEL), lambda i, k: (k, 0))],
        out_specs=[pl.BlockSpec((tm, D_MODEL), lambda i, k: (i, 0)), pl.BlockSpec((tm, FF_BLOCK), lambda i, k: (i, k)),
                   pl.BlockSpec((tm, D_MODEL), lambda i, k: (i, 0))],
        out_shape=[jax.ShapeDtypeStruct((t, D_MODEL), MXU_DTYPE), jax.ShapeDtypeStruct((t, D_FF), MXU_DTYPE),
                   jax.ShapeDtypeStruct((t, D_MODEL), F32)],
        scratch_shapes=[pltpu.VMEM((tm, D_MODEL), F32)],
        compiler_params=_params("parallel", "arbitrary"),
    )(h1, g_mlp, w_up4, w_down)


def _ple_loss(h2, p, tgt, g_ple, g_fin, w_gate, w_proj, tm):
    t = h2.shape[0]

    def body(h_ref, p_ref, t_ref, gp_ref, gf_ref, wg_ref, wp_ref,
             dh_ref, dhb_ref, dgp_ref, dpp_ref, n3_ref, pb_ref, acc_ref):
        @pl.when(pl.program_id(0) == 0)
        def _():
            acc_ref[...] = jnp.zeros_like(acc_ref)

        h = h_ref[...]
        g_ple_v, g_fin_v = gp_ref[...], gf_ref[...]
        n3 = _rms_fwd(h, g_ple_v).astype(MXU_DTYPE)
        n3_ref[...] = n3
        gate = _sigmoid(_dot(n3, wg_ref[...]))
        pb = p_ref[...].astype(MXU_DTYPE)
        pb_ref[...] = pb
        pp = _dot(pb, wp_ref[...])
        h3 = h + gate * pp
        r4 = lax.rsqrt(jnp.mean(h3 * h3, axis=-1, keepdims=True) + EPS)
        xh4 = h3 * r4
        e = xh4 * g_fin_v - t_ref[...]
        loss = 0.5 * jnp.sum(jnp.mean(e * e, axis=-1, keepdims=True), axis=0, keepdims=True)
        dy = e * (1.0 / D_MODEL)
        dg_fin = jnp.sum(dy * xh4, axis=0, keepdims=True)
        dxh = dy * g_fin_v
        dh3 = r4 * (dxh - xh4 * jnp.mean(dxh * xh4, axis=-1, keepdims=True))
        dpp_ref[...] = (dh3 * gate).astype(MXU_DTYPE)
        dgp = (dh3 * pp * gate * (1.0 - gate)).astype(MXU_DTYPE)
        dgp_ref[...] = dgp
        dn3 = _dot(dgp, wg_ref[...], NT)
        dx, dg_ple = _rms_bwd(h, g_ple_v, dn3)
        dh2 = dh3 + dx
        dh_ref[...] = dh2
        dhb_ref[...] = dh2.astype(MXU_DTYPE)
        acc_ref[0:1, :] += dg_fin
        acc_ref[1:2, :] += dg_ple
        acc_ref[2:3, :] += jnp.broadcast_to(loss, (1, D_MODEL))

    row = lambda w: pl.BlockSpec((tm, w), lambda i: (i, 0))
    return pl.pallas_call(
        body, name="ple_loss", grid=(t // tm,),
        in_specs=[row(D_MODEL), row(PLE_DIM), row(D_MODEL), _full((1, D_MODEL)), _full((1, D_MODEL)),
                  _full((D_MODEL, D_MODEL)), _full((PLE_DIM, D_MODEL))],
        out_specs=[row(D_MODEL), row(D_MODEL), row(D_MODEL), row(D_MODEL), row(D_MODEL), row(PLE_DIM),
                   _full((8, D_MODEL))],
        out_shape=[jax.ShapeDtypeStruct((t, D_MODEL), F32), jax.ShapeDtypeStruct((t, D_MODEL), MXU_DTYPE),
                   jax.ShapeDtypeStruct((t, D_MODEL), MXU_DTYPE), jax.ShapeDtypeStruct((t, D_MODEL), MXU_DTYPE),
                   jax.ShapeDtypeStruct((t, D_MODEL), MXU_DTYPE), jax.ShapeDtypeStruct((t, PLE_DIM), MXU_DTYPE),
                   jax.ShapeDtypeStruct((8, D_MODEL), F32)],
        compiler_params=_params("arbitrary"),
    )(h2, p, tgt, g_ple, g_fin, w_gate, w_proj)


def _mlp_bwd(dh2, dh2b, r, h1, g_mlp, w_up4, w_down, tm):
    t = h1.shape[0]

    def body(dh_ref, dhb_ref, r_ref, h_ref, g_ref, wu_ref, wd_ref,
             da_ref, dh1_ref, dh1b_ref, acc_ref, dm_ref):
        i, k = pl.program_id(0), pl.program_id(1)

        @pl.when((i == 0) & (k == 0))
        def _():
            acc_ref[...] = jnp.zeros_like(acc_ref)

        @pl.when(k == 0)
        def _():
            dm_ref[...] = jnp.zeros_like(dm_ref)

        ds = _dot(dhb_ref[...], wd_ref[...], NT)
        da = (ds * (2.0 * r_ref[...].astype(F32))).astype(MXU_DTYPE)
        da_ref[...] = da
        dm_ref[...] += _dot(da, wu_ref[...], NT)

        @pl.when(k == FF_BLOCKS - 1)
        def _():
            dx, dg = _rms_bwd(h_ref[...], g_ref[...], dm_ref[...])
            dh1 = dh_ref[...] + dx
            dh1_ref[...] = dh1
            dh1b_ref[...] = dh1.astype(MXU_DTYPE)
            acc_ref[0:1, :] += dg

    tok = lambda w: pl.BlockSpec((tm, w), lambda i, k: (i, 0))
    return pl.pallas_call(
        body, name="mlp_bwd", grid=(t // tm, FF_BLOCKS),
        in_specs=[tok(D_MODEL), tok(D_MODEL), pl.BlockSpec((tm, FF_BLOCK), lambda i, k: (i, k)), tok(D_MODEL),
                  _full((1, D_MODEL)), pl.BlockSpec((None, D_MODEL, FF_BLOCK), lambda i, k: (k, 0, 0)),
                  pl.BlockSpec((FF_BLOCK, D_MODEL), lambda i, k: (k, 0))],
        out_specs=[pl.BlockSpec((tm, FF_BLOCK), lambda i, k: (i, k)),
                   tok(D_MODEL), tok(D_MODEL), pl.BlockSpec((8, D_MODEL), lambda i, k: (0, 0))],
        out_shape=[jax.ShapeDtypeStruct((t, D_FF), MXU_DTYPE),
                   jax.ShapeDtypeStruct((t, D_MODEL), F32), jax.ShapeDtypeStruct((t, D_MODEL), MXU_DTYPE),
                   jax.ShapeDtypeStruct((8, D_MODEL), F32)],
        scratch_shapes=[pltpu.VMEM((tm, D_MODEL), F32)],
        compiler_params=_params("arbitrary", "arbitrary"),
    )(dh2, dh2b, r, h1, g_mlp, w_up4, w_down)


def _oproj_bwd(dh1b, wo_a, wo_d, tm, dep):
    t = dh1b.shape[0]
    half = wo_a.shape[0]

    def body(d_ref, wa_ref, wd_ref, dep_ref, da_ref, dd_ref):
        d = d_ref[...]
        da_ref[...] = _dot(d, wa_ref[...], NT)
        dd_ref[...] = _dot(d, wd_ref[...], NT)

    return pl.pallas_call(
        body, name="oproj_bwd", grid=(t // tm,),
        in_specs=[pl.BlockSpec((tm, D_MODEL), lambda i: (i, 0)), _full((half, D_MODEL)), _full((half, D_MODEL)),
                  pl.BlockSpec(memory_space=pl.ANY)],
        out_specs=[pl.BlockSpec((tm, half), lambda i: (i, 0)), pl.BlockSpec((tm, half), lambda i: (i, 0))],
        out_shape=[jax.ShapeDtypeStruct((t, half), F32), jax.ShapeDtypeStruct((t, half), F32)],
        compiler_params=_params("parallel"),
    )(dh1b, wo_a, wo_d, dep)


def _inproj_bwd(x, dh1, g_mix, grads, weights, tm):
    t = x.shape[0]
    n = len(grads)

    def body(*refs):
        x_ref, dh_ref, g_ref = refs[:3]
        g_refs, w_refs = refs[3:3 + n], refs[3 + n:3 + 2 * n]
        dx_ref, acc_ref = refs[3 + 2 * n:]

        @pl.when(pl.program_id(0) == 0)
        def _():
            acc_ref[...] = jnp.zeros_like(acc_ref)

        du = _dot(g_refs[0][...], w_refs[0][...], NT)
        for j in range(1, n):
            du += _dot(g_refs[j][...], w_refs[j][...], NT)
        dx, dg = _rms_bwd(x_ref[...], g_ref[...], du)
        dx_ref[...] = dh_ref[...] + dx
        acc_ref[0:1, :] += dg

    tok = lambda w: pl.BlockSpec((tm, w), lambda i: (i, 0))
    return pl.pallas_call(
        body, name="inproj_bwd", grid=(t // tm,),
        in_specs=[tok(D_MODEL), tok(D_MODEL), _full((1, D_MODEL))] + [tok(g.shape[1]) for g in grads]
                 + [_full(w.shape) for w in weights],
        out_specs=[tok(D_MODEL), _full((8, D_MODEL))],
        out_shape=[jax.ShapeDtypeStruct((t, D_MODEL), F32), jax.ShapeDtypeStruct((8, D_MODEL), F32)],
        compiler_params=_params("arbitrary"),
    )(x, dh1, g_mix, *grads, *weights)


def _wgrad(a, b, name, tk, tn, tt, stacked=False, prep=None):
    t, kdim = a.shape
    ncols = b.shape[1]

    def body(a_ref, b_ref, o_ref):
        @pl.when(pl.program_id(2) == 0)
        def _():
            o_ref[...] = jnp.zeros_like(o_ref)

        av = a_ref[...] if prep is None else prep(a_ref[...])
        o_ref[...] += _dot(av, b_ref[...], TN)

    if stacked:
        out_spec = pl.BlockSpec((None, tk, tn), lambda i, j, s: (j, i, 0))
        out_shape = jax.ShapeDtypeStruct((ncols // tn, kdim, tn), F32)
    else:
        out_spec = pl.BlockSpec((tk, tn), lambda i, j, s: (i, j))
        out_shape = jax.ShapeDtypeStruct((kdim, ncols), F32)
    return pl.pallas_call(
        body, name=name, grid=(kdim // tk, ncols // tn, t // tt),
        in_specs=[pl.BlockSpec((tt, tk), lambda i, j, s: (s, i)), pl.BlockSpec((tt, tn), lambda i, j, s: (s, j))],
        out_specs=out_spec, out_shape=out_shape,
        compiler_params=_params("parallel", "parallel", "arbitrary"),
    )(a, b)


def _rope_tables(t):
    half = ATTN_HEAD_DIM // 2
    inv = 1.0 / (ROPE_THETA ** (jnp.arange(half, dtype=F32) * (2.0 / ATTN_HEAD_DIM)))
    ang = jnp.arange(t, dtype=F32)[:, None] * inv[None, :]
    cos, sin = jnp.cos(ang), jnp.sin(ang)
    cos2 = jnp.concatenate([cos, cos], axis=-1)
    sin2 = jnp.concatenate([-sin, sin], axis=-1)
    return jnp.tile(cos2, (1, 2)), jnp.tile(sin2, (1, 2))


def _swap_halves(tv):
    w = tv.shape[-1]
    lane = lax.broadcasted_iota(jnp.int32, tv.shape, tv.ndim - 1)
    first = (lane % ATTN_HEAD_DIM) < (ATTN_HEAD_DIM // 2)
    return jnp.where(first, pltpu.roll(tv, w - ATTN_HEAD_DIM // 2, tv.ndim - 1),
                     pltpu.roll(tv, ATTN_HEAD_DIM // 2, tv.ndim - 1))


def _rope(tv, cos, sin):
    return tv * cos + _swap_halves(tv) * sin


def _rope_bwd(dv, cos, sin):
    return dv * cos + _swap_halves(dv * sin)


def _attn_valid(first_block):
    c = lax.broadcasted_iota(jnp.int32, (2 * ATTN_BLOCK, ATTN_BLOCK), 0)
    r = lax.broadcasted_iota(jnp.int32, (2 * ATTN_BLOCK, ATTN_BLOCK), 1)
    return (c > r) & (c <= r + ATTN_BLOCK) & ((c >= ATTN_BLOCK) | jnp.logical_not(first_block))


def _attn_probs(st, sink, valid):
    s = jnp.where(valid, st * ATTN_SCALE, -jnp.inf)
    m = jnp.maximum(jnp.max(s, axis=0, keepdims=True), sink)
    e = jnp.where(valid, jnp.exp(s - m), 0.0)
    es = jnp.exp(sink - m)
    inv = 1.0 / (jnp.sum(e, axis=0, keepdims=True) + es)
    return e * inv, es * inv


def _lane_scalar(vec, idx):
    lane = lax.broadcasted_iota(jnp.int32, vec.shape, 1)
    return jnp.sum(jnp.where(lane == idx, vec, 0.0), axis=-1, keepdims=True)


def _attn_specs(nb):
    cur = lambda w, cb: pl.BlockSpec((ATTN_BLOCK, w), lambda i: (jnp.minimum(i, nb - 1), cb))
    prev = lambda w, cb: pl.BlockSpec((ATTN_BLOCK, w), lambda i: (jnp.maximum(jnp.minimum(i, nb - 1) - 1, 0), cb))
    kcol, vcol = ATTN_Q // ATTN_KV, ATTN_Q // ATTN_KV + 1
    return [cur(ATTN_Q, 0), cur(ATTN_KV, kcol), prev(ATTN_KV, kcol), cur(ATTN_KV, vcol), prev(ATTN_KV, vcol),
            cur(ATTN_KV, 0), cur(ATTN_KV, 0), prev(ATTN_KV, 0), prev(ATTN_KV, 0), _full((1, 128))]


def _attn_fwd(pa, cos, sin, sinks_vec):
    t = pa.shape[0]
    nb = t // ATTN_BLOCK

    def body(q_ref, kc_ref, kp_ref, vc_ref, vp_ref, cc_ref, sc_ref, cp_ref, sp_ref, sk_ref, o_ref):
        first = pl.program_id(0) == 0
        cc, sc = cc_ref[...], sc_ref[...]
        q = _rope(q_ref[...], jnp.tile(cc, (1, ATTN_Q // ATTN_KV)), jnp.tile(sc, (1, ATTN_Q // ATTN_KV)))
        kc = _rope(kc_ref[...], cc, sc)
        kp = _rope(kp_ref[...], cp_ref[...], sp_ref[...])
        vc, vp = vc_ref[...], vp_ref[...]
        sk = sk_ref[...]
        valid = _attn_valid(first)
        kv = lambda tp, tc, hk: jnp.concatenate([tp[:, hk * ATTN_HEAD_DIM:(hk + 1) * ATTN_HEAD_DIM],
                                                 tc[:, hk * ATTN_HEAD_DIM:(hk + 1) * ATTN_HEAD_DIM]], axis=0)
        kwins = [kv(kp, kc, hk) for hk in range(ATTN_KV_HEADS)]
        vwins_t = [kv(vp, vc, hk).T for hk in range(ATTN_KV_HEADS)]
        heads = [slice(h * ATTN_HEAD_DIM, (h + 1) * ATTN_HEAD_DIM) for h in range(ATTN_HEADS)]
        scores = [_dot(kwins[h // ATTN_GROUPS], q[:, hs], NT) for h, hs in enumerate(heads)]
        probs = [_attn_probs(st, _lane_scalar(sk, h), valid)[0] for h, st in enumerate(scores)]
        for h, (hs, pt) in enumerate(zip(heads, probs)):
            o_ref[:, hs] = _dot(vwins_t[h // ATTN_GROUPS], pt).T.astype(o_ref.dtype)

    return pl.pallas_call(
        body, name="attn_fwd", grid=(nb,),
        in_specs=_attn_specs(nb),
        out_specs=pl.BlockSpec((ATTN_BLOCK, ATTN_Q), lambda i: (i, 0)),
        out_shape=jax.ShapeDtypeStruct((t, ATTN_Q), MXU_DTYPE),
        compiler_params=_params("parallel"),
    )(pa, pa, pa, pa, pa, cos, sin, cos, sin, sinks_vec)


def _attn_bwd(pa, cos, sin, sinks_vec, dao):
    t = pa.shape[0]
    nb = t // ATTN_BLOCK

    def body(q_ref, kc_ref, kp_ref, vc_ref, vp_ref, cc_ref, sc_ref, cp_ref, sp_ref, sk_ref, do_ref,
             dq_ref, dk_ref, dv_ref, acc_ref, dqr_ref, dkw_ref, dvw_ref, ck_ref, cv_ref):
        i = pl.program_id(0)

        @pl.when(i == 0)
        def _():
            acc_ref[...] = jnp.zeros_like(acc_ref)
            ck_ref[...] = jnp.zeros_like(ck_ref)
            cv_ref[...] = jnp.zeros_like(cv_ref)

        @pl.when(i < nb)
        def _():
            first = i == 0
            cc, sc = cc_ref[...], sc_ref[...]
            cq, sq = jnp.tile(cc, (1, ATTN_Q // ATTN_KV)), jnp.tile(sc, (1, ATTN_Q // ATTN_KV))
            q = _rope(q_ref[...], cq, sq)
            kc = _rope(kc_ref[...], cc, sc)
            kp = _rope(kp_ref[...], cp_ref[...], sp_ref[...])
            vc, vp = vc_ref[...], vp_ref[...]
            sk = sk_ref[...]
            do = do_ref[...]
            lane = lax.broadcasted_iota(jnp.int32, (1, 128), 1)
            dsink = jnp.zeros((1, 128), F32)
            valid = _attn_valid(first)
            kv = lambda tp, tc, hk: jnp.concatenate([tp[:, hk * ATTN_HEAD_DIM:(hk + 1) * ATTN_HEAD_DIM],
                                                     tc[:, hk * ATTN_HEAD_DIM:(hk + 1) * ATTN_HEAD_DIM]], axis=0)
            kwins = [kv(kp, kc, hk) for hk in range(ATTN_KV_HEADS)]
            vwins = [kv(vp, vc, hk) for hk in range(ATTN_KV_HEADS)]
            kwins_t = [kw.T for kw in kwins]
            heads = [slice(h * ATTN_HEAD_DIM, (h + 1) * ATTN_HEAD_DIM) for h in range(ATTN_HEADS)]
            scores = [_dot(kwins[h // ATTN_GROUPS], q[:, hs], NT) for h, hs in enumerate(heads)]
            dps = [_dot(vwins[h // ATTN_GROUPS], do[:, hs], NT) for h, hs in enumerate(heads)]
            pts, dsts = [], []
            for h, (st, dp_t) in enumerate(zip(scores, dps)):
                probs_t, psink = _attn_probs(st, _lane_scalar(sk, h), valid)
                delta = jnp.sum(probs_t * dp_t, axis=0, keepdims=True)
                pts.append(probs_t)
                dsts.append(probs_t * (dp_t - delta) * ATTN_SCALE)
                dsink += jnp.where(lane == h, jnp.sum(-psink * delta, axis=1, keepdims=True), 0.0)
            for h, (hs, ds_t) in enumerate(zip(heads, dsts)):
                dqr_ref[:, hs] = _dot(kwins_t[h // ATTN_GROUPS], ds_t).T
            for hk in range(ATTN_KV_HEADS):
                ks = slice(hk * ATTN_HEAD_DIM, (hk + 1) * ATTN_HEAD_DIM)
                group = range(hk * ATTN_GROUPS, (hk + 1) * ATTN_GROUPS)
                ds_g = jnp.concatenate([dsts[h] for h in group], axis=1)
                p_g = jnp.concatenate([pts[h] for h in group], axis=1)
                q_g = jnp.concatenate([q[:, heads[h]] for h in group], axis=0)
                do_g = jnp.concatenate([do[:, heads[h]] for h in group], axis=0)
                dkw_ref[:, ks] = _dot(ds_g, q_g)
                dvw_ref[:, ks] = _dot(p_g, do_g)
            acc_ref[0:1, :] += dsink
            dq_ref[...] = _rope_bwd(dqr_ref[...], cq, sq).astype(dq_ref.dtype)
            dk_ref[...] = (ck_ref[...] + _rope_bwd(dkw_ref[0:ATTN_BLOCK, :], cp_ref[...], sp_ref[...])).astype(dk_ref.dtype)
            dv_ref[...] = (cv_ref[...] + dvw_ref[0:ATTN_BLOCK, :]).astype(dv_ref.dtype)
            ck_ref[...] = _rope_bwd(dkw_ref[ATTN_BLOCK:2 * ATTN_BLOCK, :], cc, sc)
            cv_ref[...] = dvw_ref[ATTN_BLOCK:2 * ATTN_BLOCK, :]

        @pl.when(i == nb)
        def _():
            dk_ref[...] = ck_ref[...].astype(dk_ref.dtype)
            dv_ref[...] = cv_ref[...].astype(dv_ref.dtype)

    prev_out = lambda w: pl.BlockSpec((ATTN_BLOCK, w), lambda i: (jnp.maximum(i - 1, 0), 0))
    return pl.pallas_call(
        body, name="attn_bwd", grid=(nb + 1,),
        in_specs=_attn_specs(nb) + [pl.BlockSpec((ATTN_BLOCK, ATTN_Q), lambda i: (jnp.minimum(i, nb - 1), 0))],
        out_specs=[pl.BlockSpec((ATTN_BLOCK, ATTN_Q), lambda i: (jnp.minimum(i, nb - 1), 0)), prev_out(ATTN_KV),
                   prev_out(ATTN_KV), _full((8, 128))],
        out_shape=[jax.ShapeDtypeStruct((t, ATTN_Q), MXU_DTYPE), jax.ShapeDtypeStruct((t, ATTN_KV), MXU_DTYPE),
                   jax.ShapeDtypeStruct((t, ATTN_KV), MXU_DTYPE), jax.ShapeDtypeStruct((8, 128), F32)],
        scratch_shapes=[pltpu.VMEM((ATTN_BLOCK, ATTN_Q), F32), pltpu.VMEM((2 * ATTN_BLOCK, ATTN_KV), F32),
                        pltpu.VMEM((2 * ATTN_BLOCK, ATTN_KV), F32), pltpu.VMEM((ATTN_BLOCK, ATTN_KV), F32),
                        pltpu.VMEM((ATTN_BLOCK, ATTN_KV), F32)],
        compiler_params=_params("arbitrary"),
    )(pa, pa, pa, pa, pa, cos, sin, cos, sin, sinks_vec, dao)


PAIR = 2 * DN_CHUNK
INTRA_PAIRS = 4
HALO = 8


def _conv_window(cur_ref, prev_ref, xs_ref, tm):
    prev = jnp.where(pl.program_id(0) > 0, prev_ref[...], 0.0)
    xs_ref[0:HALO, :] = prev
    xs_ref[HALO:HALO + tm, :] = cur_ref[...]


def _conv_taps(xs_ref, cw_ref, tm):
    y = cw_ref[0:1, :] * xs_ref[pl.ds(HALO - DN_CONV + 1, tm), :]
    for j in range(1, DN_CONV):
        y += cw_ref[j:j + 1, :] * xs_ref[pl.ds(HALO - DN_CONV + 1 + j, tm), :]
    return y


def _gate_values(ba, al, dt):
    beta = _sigmoid(ba)
    pre = ba + dt
    g = -jnp.exp(al) * _softplus(pre)
    return beta, g, pre


def _dn_prep_specs(tm, t):
    return [pl.BlockSpec((tm, CONV_CH), lambda i: (i, 0)),
            pl.BlockSpec((HALO, CONV_CH), lambda i: (jnp.maximum(i * (tm // HALO) - 1, 0), 0)),
            pl.BlockSpec((tm, 128), lambda i: (i, 4 * DN_W // 128)),
            _full((DN_CONV, CONV_CH)), _full((1, 128)), _full((1, 128))]


def _dn_prep(pd, conv_w, al_vec, dt_vec, tm):
    t = pd.shape[0]

    def body(cur_ref, prev_ref, ba_ref, cw_ref, al_ref, dt_ref, qn_ref, kn_ref, vc_ref, gc_ref, gr_ref, xs_ref):
        _conv_window(cur_ref, prev_ref, xs_ref, tm)
        y = _conv_taps(xs_ref, cw_ref, tm)
        c = y * _sigmoid(y)
        for h in range(DN_HEADS):
            qs = slice(h * DN_HEAD_DIM, (h + 1) * DN_HEAD_DIM)
            ksl = slice(DN_W + h * DN_HEAD_DIM, DN_W + (h + 1) * DN_HEAD_DIM)
            qh, kh = c[:, qs], c[:, ksl]
            qn_ref[:, qs] = qh * lax.rsqrt(jnp.sum(qh * qh, axis=-1, keepdims=True) + EPS) * DN_SCALE
            kn_ref[:, qs] = kh * lax.rsqrt(jnp.sum(kh * kh, axis=-1, keepdims=True) + EPS)
        vc_ref[...] = c[:, 2 * DN_W:3 * DN_W]
        beta, g, _ = _gate_values(ba_ref[...], al_ref[...], dt_ref[...])
        lane = lax.broadcasted_iota(jnp.int32, beta.shape, 1)
        gb = jnp.where(lane < DN_HEADS, beta, jnp.where(lane < 2 * DN_HEADS, g, 0.0))
        gc_ref[...] = gb
        gr_ref[...] = gb.T[0:8, :]

    tok = lambda w: pl.BlockSpec((tm, w), lambda i: (i, 0))
    return pl.pallas_call(
        body, name="dn_prep", grid=(t // tm,),
        in_specs=_dn_prep_specs(tm, t),
        out_specs=[tok(DN_W), tok(DN_W), tok(DN_W), tok(128), pl.BlockSpec((8, tm), lambda i: (0, i))],
        out_shape=[jax.ShapeDtypeStruct((t, DN_W), F32)] * 3 + [jax.ShapeDtypeStruct((t, 128), F32),
                                                                 jax.ShapeDtypeStruct((8, t), F32)],
        scratch_shapes=[pltpu.VMEM((HALO + tm, CONV_CH), F32)],
        compiler_params=_params("parallel"),
    )(pd, pd, pd, conv_w, al_vec, dt_vec)


def _pair_masks():
    r = lax.broadcasted_iota(jnp.int32, (PAIR, PAIR), 0)
    c = lax.broadcasted_iota(jnp.int32, (PAIR, PAIR), 1)
    same = (r < DN_CHUNK) == (c < DN_CHUNK)
    return same & (r >= c), same & (r > c)


def _lane_col(mat, idx):
    lane = lax.broadcasted_iota(jnp.int32, mat.shape, 1)
    return jnp.sum(jnp.where(lane == idx, mat, 0.0), axis=-1, keepdims=True)


def _pair_cumsums(gc, gr, low):
    lowf = low.astype(F32)
    return _dot(lowf, gc, NN, HI), _dot(gr, lowf, NT, HI)


def _pair_gates(gc, cum_c, cum_r, low, h):
    beta = _lane_col(gc, h)
    gam = _lane_col(cum_c, DN_HEADS + h)
    gam_row = cum_r[DN_HEADS + h:DN_HEADS + h + 1, :]
    dm = jnp.where(low, jnp.exp(jnp.where(low, gam - gam_row, 0.0)), 0.0)
    row = lax.broadcasted_iota(jnp.int32, gam.shape, 0)
    gl = jnp.where(row < DN_CHUNK, gam[DN_CHUNK - 1:DN_CHUNK, :], gam[PAIR - 1:PAIR, :])
    return beta, gam, dm, gl


def _split(a):
    hi = a.astype(BF16)
    return hi, (a - hi.astype(F32)).astype(BF16)


def _dot_split(a, b, dims=NN):
    (ah, al), (bh, bl) = a, b
    la, lb = (1, 1) if dims == TN else ((0, 1) if dims == NN else (0, 0))
    r = _dot(jnp.concatenate([ah, al], axis=la), jnp.concatenate([bh, bl], axis=lb), dims)
    m, n = r.shape[0] // 2, r.shape[1] // 2
    return (r[m:, n:] + (r[:m, n:] + r[m:, :n])) + r[:m, :n]


def _unit_lower_inverses(lmats):
    n = lmats[0].shape[0]
    r = lax.broadcasted_iota(jnp.int32, (n, n), 0)
    c = lax.broadcasted_iota(jnp.int32, (n, n), 1)
    same = lambda size: (r & ~(size - 1)) == (c & ~(size - 1))
    base = DN_CHUNK // 4
    diag = [jnp.where(same(base), l, 0.0) for l in lmats]
    accs = [(r == c).astype(F32) - d for d in diag]
    splits = [_split(d) for d in diag]
    step = 1
    while 2 * step < base:
        splits = [_split(_dot_split(s, s)) for s in splits]
        accs = [acc + _dot_split(_split(acc), s) for acc, s in zip(accs, splits)]
        step *= 2
    size = base
    while size < DN_CHUNK:
        below = same(2 * size) & jnp.logical_not(same(size))
        tb = [_dot(acc, jnp.where(below, l, 0.0)) for acc, l in zip(accs, lmats)]
        accs = [acc - _dot(t, acc) for acc, t in zip(accs, tb)]
        size *= 2
    return accs


def _dn_intra(qn, kn, vc, gc, gr):
    t = qn.shape[0]
    npair = t // PAIR
    rows_step = INTRA_PAIRS * PAIR

    def body(q_ref, k_ref, v_ref, gc_ref, gr_ref, u_ref, w_ref, qg_ref, kd_ref, a_ref, ti_ref, dl_ref):
        low, strict = _pair_masks()
        items = []
        for p in range(INTRA_PAIRS):
            rows = slice(p * PAIR, (p + 1) * PAIR)
            gc_v = gc_ref[rows, :]
            cum_c, cum_r = _pair_cumsums(gc_v, gr_ref[:, rows], low)
            for h in range(DN_HEADS):
                hs = slice(h * DN_HEAD_DIM, (h + 1) * DN_HEAD_DIM)
                items.append((p, h, rows, hs, _pair_gates(gc_v, cum_c, cum_r, low, h)))
        lmats = []
        for p, h, rows, hs, (beta, gam, dm, gl) in items:
            k = k_ref[rows, hs]
            lmats.append(jnp.where(strict, _dot(k * beta, k, NT) * dm, 0.0))
        tinvs = _unit_lower_inverses(lmats)
        for (p, h, rows, hs, (beta, gam, dm, gl)), tinv in zip(items, tinvs):
            q, k, v = q_ref[rows, hs], k_ref[rows, hs], v_ref[rows, hs]
            eg = jnp.exp(gam)
            u_ref[rows, hs] = _dot(tinv, v * beta)
            w_ref[rows, hs] = _dot(tinv, (k * beta) * eg)
            a_ref[h, rows, :] = _dot(q, k, NT) * dm
            ti_ref[h, rows, :] = tinv
            qg_ref[rows, hs] = q * eg
            kd_ref[rows, hs] = k * jnp.exp(gl - gam)
            for c in range(2):
                last = (c + 1) * DN_CHUNK - 1
                dl_ref[2 * p + c, h] = jnp.broadcast_to(jnp.exp(gam[last:last + 1, :]), (8, 128))

    tok = lambda w: pl.BlockSpec((rows_step, w), lambda n: (n, 0))
    hm = pl.BlockSpec((DN_HEADS, rows_step, PAIR), lambda n: (0, n, 0))
    return pl.pallas_call(
        body, name="dn_intra", grid=(npair // INTRA_PAIRS,),
        in_specs=[tok(DN_W), tok(DN_W), tok(DN_W), tok(128), pl.BlockSpec((8, rows_step), lambda n: (0, n))],
        out_specs=[tok(DN_W)] * 4 + [hm, hm, pl.BlockSpec((2 * INTRA_PAIRS, DN_HEADS, 8, 128), lambda n: (n, 0, 0, 0))],
        out_shape=[jax.ShapeDtypeStruct((t, DN_W), F32)] * 4 + [jax.ShapeDtypeStruct((DN_HEADS, t, PAIR), F32)] * 2
                  + [jax.ShapeDtypeStruct((2 * npair, DN_HEADS, 8, 128), F32)],
        compiler_params=_params("parallel"),
    )(qn, kn, vc, gc, gr)


def _dn_scan_fwd(u, w, qg, kd, a_qk, dlast, pd, dn_w):
    t = u.shape[0]
    npair = t // PAIR

    def body(u_ref, w_ref, qg_ref, kd_ref, a_ref, dl_ref, z_ref, nw_ref, out_ref, o_ref, vn_ref, sall_ref, s_ref):
        @pl.when(pl.program_id(0) == 0)
        def _():
            s_ref[...] = jnp.zeros_like(s_ref)

        nw = nw_ref[...]
        for c in range(2):
            rows = slice(c * DN_CHUNK, (c + 1) * DN_CHUNK)
            for h in range(DN_HEADS):
                hs = slice(h * DN_HEAD_DIM, (h + 1) * DN_HEAD_DIM)
                st = s_ref[h]
                sall_ref[c, h] = st
                vn_ref[rows, hs] = u_ref[rows, hs] - _dot(w_ref[rows, hs], st)
            for h in range(DN_HEADS):
                hs = slice(h * DN_HEAD_DIM, (h + 1) * DN_HEAD_DIM)
                st, vn = s_ref[h], vn_ref[rows, hs]
                o = _dot(qg_ref[rows, hs], st) + _dot(a_ref[h, rows, rows], vn)
                s_ref[h] = st * dl_ref[c, h][0:1, :] + _dot(kd_ref[rows, hs], vn, TN)
                o_ref[rows, hs] = o
                z = z_ref[rows, hs]
                on = o * lax.rsqrt(jnp.mean(o * o, axis=-1, keepdims=True) + EPS) * nw
                out_ref[rows, hs] = (on * (z * _sigmoid(z))).astype(out_ref.dtype)

    tok = pl.BlockSpec((PAIR, DN_W), lambda n: (n, 0))
    hm = pl.BlockSpec((DN_HEADS, PAIR, PAIR), lambda n: (0, n, 0))
    return pl.pallas_call(
        body, name="dn_scan_fwd", grid=(npair,),
        in_specs=[tok, tok, tok, tok, hm, pl.BlockSpec((2, DN_HEADS, 8, 128), lambda n: (n, 0, 0, 0)),
                  pl.BlockSpec((PAIR, DN_W), lambda n: (n, 3)), _full((1, 128))],
        out_specs=[tok, tok, tok, pl.BlockSpec((2, DN_HEADS, DN_HEAD_DIM, DN_HEAD_DIM), lambda n: (n, 0, 0, 0))],
        out_shape=[jax.ShapeDtypeStruct((t, DN_W), MXU_DTYPE)] + [jax.ShapeDtypeStruct((t, DN_W), F32)] * 2
                  + [jax.ShapeDtypeStruct((2 * npair, DN_HEADS, DN_HEAD_DIM, DN_HEAD_DIM), F32)],
        scratch_shapes=[pltpu.VMEM((DN_HEADS, DN_HEAD_DIM, DN_HEAD_DIM), F32)],
        compiler_params=_params("arbitrary"),
    )(u, w, qg, kd, a_qk, dlast, pd, dn_w)


def _dn_scan_bwd(dout, o, vnew, sall, w, qg, kd, a_qk, dlast, pd, dn_w):
    t = o.shape[0]
    npair = t // PAIR
    rev = lambda n: npair - 1 - n

    def body(do_ref, o_ref, vn_ref, sall_ref, w_ref, qg_ref, kd_ref, a_ref, dl_ref, z_ref, nw_ref,
             dz_ref, du_ref, dw_ref, dqg_ref, dkd_ref, da_ref, ddl_ref, acc_ref, ds_ref, dos_ref):
        @pl.when(pl.program_id(0) == 0)
        def _():
            ds_ref[...] = jnp.zeros_like(ds_ref)
            acc_ref[...] = jnp.zeros_like(acc_ref)

        nw = nw_ref[...]
        dnw = jnp.zeros((1, 128), F32)
        for h in range(DN_HEADS):
            hs = slice(h * DN_HEAD_DIM, (h + 1) * DN_HEAD_DIM)
            o, z, dout = o_ref[:, hs], z_ref[:, hs], do_ref[:, hs]
            r = lax.rsqrt(jnp.mean(o * o, axis=-1, keepdims=True) + EPS)
            oh = o * r
            sz = _sigmoid(z)
            dz_ref[:, hs] = dout * (oh * nw) * (sz + z * sz * (1.0 - sz))
            don = dout * (z * sz)
            dnw += jnp.sum(don * oh, axis=0, keepdims=True)
            doh = don * nw
            dos_ref[:, hs] = r * (doh - oh * jnp.mean(doh * oh, axis=-1, keepdims=True))
        acc_ref[0:1, :] += dnw
        for c in (1, 0):
            rows = slice(c * DN_CHUNK, (c + 1) * DN_CHUNK)
            other = slice((1 - c) * DN_CHUNK, (2 - c) * DN_CHUNK)
            for h in range(DN_HEADS):
                hs = slice(h * DN_HEAD_DIM, (h + 1) * DN_HEAD_DIM)
                do, st, dsp, vn = dos_ref[rows, hs], sall_ref[c, h], ds_ref[h], vn_ref[rows, hs]
                da_ref[h, rows, rows] = _dot(do, vn, NT)
                da_ref[h, rows, other] = jnp.zeros((DN_CHUNK, DN_CHUNK), F32)
                du_ref[rows, hs] = _dot(a_ref[h, rows, rows], do, TN) + _dot(kd_ref[rows, hs], dsp)
                dqg_ref[rows, hs] = _dot(do, st, NT)
                dkd_ref[rows, hs] = _dot(vn, dsp, NT)
                ddl = jnp.sum(jnp.sum(dsp * st, axis=1, keepdims=True), axis=0, keepdims=True)
                ddl_ref[c, h] = jnp.broadcast_to(ddl, (8, 128))
            for h in range(DN_HEADS):
                hs = slice(h * DN_HEAD_DIM, (h + 1) * DN_HEAD_DIM)
                do, st, dvn = dos_ref[rows, hs], sall_ref[c, h], du_ref[rows, hs]
                dw_ref[rows, hs] = -_dot(dvn, st, NT)
                ds_ref[h] = (ds_ref[h] * dl_ref[c, h][0:1, :] + _dot(qg_ref[rows, hs], do, TN)
                             - _dot(w_ref[rows, hs], dvn, TN))

    tok = pl.BlockSpec((PAIR, DN_W), lambda n: (rev(n), 0))
    hm = pl.BlockSpec((DN_HEADS, PAIR, PAIR), lambda n: (0, rev(n), 0))
    sc = pl.BlockSpec((2, DN_HEADS, 8, 128), lambda n: (rev(n), 0, 0, 0))
    return pl.pallas_call(
        body, name="dn_scan_bwd", grid=(npair,),
        in_specs=[tok, tok, tok, pl.BlockSpec((2, DN_HEADS, DN_HEAD_DIM, DN_HEAD_DIM), lambda n: (rev(n), 0, 0, 0)),
                  tok, tok, tok, hm, sc, pl.BlockSpec((PAIR, DN_W), lambda n: (rev(n), 3)), _full((1, 128))],
        out_specs=[tok] * 5 + [hm, sc, _full((8, 128))],
        out_shape=[jax.ShapeDtypeStruct((t, DN_W), F32)] * 5 + [jax.ShapeDtypeStruct((DN_HEADS, t, PAIR), F32),
                   jax.ShapeDtypeStruct((2 * npair, DN_HEADS, 8, 128), F32), jax.ShapeDtypeStruct((8, 128), F32)],
        scratch_shapes=[pltpu.VMEM((DN_HEADS, DN_HEAD_DIM, DN_HEAD_DIM), F32), pltpu.VMEM((PAIR, DN_W), F32)],
        compiler_params=_params("arbitrary"),
    )(dout, o, vnew, sall, w, qg, kd, a_qk, dlast, pd, dn_w)


def _dn_intra_bwd(qn, kn, vc, gc, gr, tinv, a_qk, du, dw, dqg, dkd, da_qk, ddlast, dlast, dep):
    t = qn.shape[0]
    npair = t // PAIR

    def body(q_ref, k_ref, v_ref, gc_ref, gr_ref, ti_ref, a_ref, du_ref, dw_ref, dqg_ref, dkd_ref, da_ref, ddl_ref, dl_ref,
             dep_ref, dq_ref, dk_ref, dv_ref, dg_ref):
        low, strict = _pair_masks()
        lane = lax.broadcasted_iota(jnp.int32, (PAIR, 128), 1)
        rowi = lax.broadcasted_iota(jnp.int32, (PAIR, 1), 0)
        rsum = lambda v: jnp.sum(v, axis=-1, keepdims=True)
        items = []
        for p in range(INTRA_PAIRS):
            rows = slice(p * PAIR, (p + 1) * PAIR)
            gc_v = gc_ref[rows, :]
            cum_c, cum_r = _pair_cumsums(gc_v, gr_ref[:, rows], low)
            for h in range(DN_HEADS):
                hs = slice(h * DN_HEAD_DIM, (h + 1) * DN_HEAD_DIM)
                items.append((p, h, rows, hs, _pair_gates(gc_v, cum_c, cum_r, low, h)))
        dtis, lmats, dvbs, dkbgs = [], [], [], []
        for p, h, rows, hs, (beta, gam, dm, gl) in items:
            k, tinv = k_ref[rows, hs], ti_ref[h, rows, :]
            kb = k * beta
            dtis.append(_dot(du_ref[rows, hs], v_ref[rows, hs] * beta, NT)
                        + _dot(dw_ref[rows, hs], kb * jnp.exp(gam), NT))
            lmats.append(jnp.where(strict, _dot(kb, k, NT) * dm, 0.0))
            dvbs.append(_dot(tinv, du_ref[rows, hs], TN))
            dkbgs.append(_dot(tinv, dw_ref[rows, hs], TN))
        xs = [_dot(ti_ref[h, rows, :], dti, TN) for (p, h, rows, hs, g), dti in zip(items, dtis)]
        dls = [jnp.where(strict, -_dot(x, ti_ref[h, rows, :], NT), 0.0) for (p, h, rows, hs, g), x in zip(items, xs)]
        dgam_all = [jnp.zeros((PAIR, 128), F32) for _ in range(INTRA_PAIRS)]
        dbeta_all = [jnp.zeros((PAIR, 128), F32) for _ in range(INTRA_PAIRS)]
        for (p, h, rows, hs, (beta, gam, dm, gl)), dl, lmat, dvb, dkbg in zip(items, dls, lmats, dvbs, dkbgs):
            q, k, v = q_ref[rows, hs], k_ref[rows, hs], v_ref[rows, hs]
            a = a_ref[h, rows, :]
            dqg, dkd = dqg_ref[rows, hs], dkd_ref[rows, hs]
            kb = k * beta
            eg = jnp.exp(gam)
            ekd = jnp.exp(gl - gam)
            dmm = dl * dm
            dam = jnp.where(low, da_ref[h, rows, :], 0.0)
            dn = dam * dm
            e = dl * lmat + dam * a
            dkb = _dot(dmm, k) + dkbg * eg
            dk_ref[rows, hs] = _dot(dmm, kb, TN) + _dot(dn, q, TN) + dkd * ekd + dkb * beta
            dq_ref[rows, hs] = _dot(dn, k) + dqg * eg
            dv_ref[rows, hs] = dvb * beta
            t_kd = rsum(dkd * (k * ekd))
            dgam = rsum(e) - rsum(e.T) + rsum(dqg * (q * eg)) + rsum(dkbg * (kb * eg)) - t_kd
            for c in range(2):
                crows = slice(c * DN_CHUNK, (c + 1) * DN_CHUNK)
                dgl = (jnp.sum(t_kd[crows, :], axis=0, keepdims=True)
                       + ddl_ref[2 * p + c, h][0:1, 0:1] * dl_ref[2 * p + c, h][0:1, 0:1])
                dgam = dgam + jnp.where(rowi == (c + 1) * DN_CHUNK - 1, dgl, 0.0)
            dgam_all[p] += jnp.where(lane == DN_HEADS + h, dgam, 0.0)
            dbeta_all[p] += jnp.where(lane == h, rsum(dkb * k) + rsum(dvb * v), 0.0)
        for p in range(INTRA_PAIRS):
            dg_ref[p * PAIR:(p + 1) * PAIR, :] = dbeta_all[p] + _dot(low.astype(F32), dgam_all[p], TN, HI)

    rows_step = INTRA_PAIRS * PAIR
    tok = lambda w: pl.BlockSpec((rows_step, w), lambda n: (n, 0))
    hm = pl.BlockSpec((DN_HEADS, rows_step, PAIR), lambda n: (0, n, 0))
    sc = pl.BlockSpec((2 * INTRA_PAIRS, DN_HEADS, 8, 128), lambda n: (n, 0, 0, 0))
    return pl.pallas_call(
        body, name="dn_intra_bwd", grid=(npair // INTRA_PAIRS,),
        in_specs=[tok(DN_W), tok(DN_W), tok(DN_W), tok(128), pl.BlockSpec((8, rows_step), lambda n: (0, n)), hm, hm,
                  tok(DN_W), tok(DN_W), tok(DN_W), tok(DN_W), hm, sc, sc, pl.BlockSpec(memory_space=pl.ANY)],
        out_specs=[tok(DN_W), tok(DN_W), tok(DN_W), tok(128)],
        out_shape=[jax.ShapeDtypeStruct((t, DN_W), F32)] * 3 + [jax.ShapeDtypeStruct((t, 128), F32)],
        compiler_params=_params("parallel"),
    )(qn, kn, vc, gc, gr, tinv, a_qk, du, dw, dqg, dkd, da_qk, ddlast, dlast, dep)


def _dn_prep_bwd(pd, conv_w, al_vec, dt_vec, dqn, dkn, dvc, dgc, tm):
    t = pd.shape[0]

    def body(cur_ref, prev_ref, ba_ref, cw_ref, al_ref, dt_ref, dq_ref, dk_ref, dv_ref, dg_ref,
             dy_ref, dba_ref, accw_ref, accg_ref, xs_ref, dc_ref):
        @pl.when(pl.program_id(0) == 0)
        def _():
            accw_ref[...] = jnp.zeros_like(accw_ref)
            accg_ref[...] = jnp.zeros_like(accg_ref)

        _conv_window(cur_ref, prev_ref, xs_ref, tm)
        y = _conv_taps(xs_ref, cw_ref, tm)
        sg = _sigmoid(y)
        c = y * sg
        for h in range(DN_HEADS):
            qs = slice(h * DN_HEAD_DIM, (h + 1) * DN_HEAD_DIM)
            ksl = slice(DN_W + h * DN_HEAD_DIM, DN_W + (h + 1) * DN_HEAD_DIM)
            for src, sl, scale in ((dq_ref, qs, DN_SCALE), (dk_ref, ksl, 1.0)):
                xh = c[:, sl]
                r = lax.rsqrt(jnp.sum(xh * xh, axis=-1, keepdims=True) + EPS)
                unit = xh * r
                dn = src[:, qs] * scale
                dc_ref[:, sl] = r * (dn - unit * jnp.sum(dn * unit, axis=-1, keepdims=True))
        dc_ref[:, 2 * DN_W:3 * DN_W] = dv_ref[...]
        dy = dc_ref[...] * (sg + y * sg * (1.0 - sg))
        dy_ref[...] = dy
        for j in range(DN_CONV):
            accw_ref[j:j + 1, :] += jnp.sum(dy * xs_ref[pl.ds(HALO - DN_CONV + 1 + j, tm), :], axis=0, keepdims=True)

        beta, g, pre = _gate_values(ba_ref[...], al_ref[...], dt_ref[...])
        dgb = dg_ref[...]
        lane = lax.broadcasted_iota(jnp.int32, dgb.shape, 1)
        is_b, is_a = lane < DN_HEADS, (lane >= DN_HEADS) & (lane < 2 * DN_HEADS)
        dpre = dgb * (-jnp.exp(al_ref[...])) * _sigmoid(pre)
        dba_ref[...] = jnp.where(is_b, dgb * beta * (1.0 - beta), jnp.where(is_a, dpre, 0.0))
        accg_ref[0:1, :] += jnp.sum(jnp.where(is_a, dgb * g, 0.0), axis=0, keepdims=True)
        accg_ref[1:2, :] += jnp.sum(jnp.where(is_a, dpre, 0.0), axis=0, keepdims=True)

    tok = lambda w: pl.BlockSpec((tm, w), lambda i: (i, 0))
    return pl.pallas_call(
        body, name="dn_prep_bwd", grid=(t // tm,),
        in_specs=_dn_prep_specs(tm, t) + [tok(DN_W), tok(DN_W), tok(DN_W), tok(128)],
        out_specs=[tok(CONV_CH), tok(128), _full((8, CONV_CH)), _full((8, 128))],
        out_shape=[jax.ShapeDtypeStruct((t, CONV_CH), F32), jax.ShapeDtypeStruct((t, 128), F32),
                   jax.ShapeDtypeStruct((8, CONV_CH), F32), jax.ShapeDtypeStruct((8, 128), F32)],
        scratch_shapes=[pltpu.VMEM((HALO + tm, CONV_CH), F32), pltpu.VMEM((tm, CONV_CH), F32)],
        compiler_params=_params("arbitrary"),
    )(pd, pd, pd, conv_w, al_vec, dt_vec, dqn, dkn, dvc, dgc)


def _dn_conv_bwd(dy, dz, dba, conv_w, tm):
    t = dy.shape[0]
    nt = t // tm

    def body(cur_ref, nxt_ref, dz_ref, dba_ref, cw_ref, o_ref, ds_ref):
        nxt = jnp.where(pl.program_id(0) < nt - 1, nxt_ref[...], 0.0)
        ds_ref[0:tm, :] = cur_ref[...]
        ds_ref[tm:tm + HALO, :] = nxt
        dx = cw_ref[0:1, :] * ds_ref[pl.ds(DN_CONV - 1, tm), :]
        for j in range(1, DN_CONV):
            dx += cw_ref[j:j + 1, :] * ds_ref[pl.ds(DN_CONV - 1 - j, tm), :]
        o_ref[:, 0:CONV_CH] = dx.astype(o_ref.dtype)
        o_ref[:, CONV_CH:CONV_CH + DN_W] = dz_ref[...].astype(o_ref.dtype)
        o_ref[:, CONV_CH + DN_W:DN_COLS] = dba_ref[...].astype(o_ref.dtype)

    tok = lambda w: pl.BlockSpec((tm, w), lambda i: (i, 0))
    return pl.pallas_call(
        body, name="dn_conv_bwd", grid=(nt,),
        in_specs=[tok(CONV_CH),
                  pl.BlockSpec((HALO, CONV_CH), lambda i: (jnp.minimum((i + 1) * (tm // HALO), t // HALO - 1), 0)),
                  tok(DN_W), tok(128), _full((DN_CONV, CONV_CH))],
        out_specs=tok(DN_COLS),
        out_shape=jax.ShapeDtypeStruct((t, DN_COLS), MXU_DTYPE),
        scratch_shapes=[pltpu.VMEM((tm + HALO, CONV_CH), F32)],
        compiler_params=_params("parallel"),
    )(dy, dy, dz, dba, conv_w)


def _pad_lanes(v, offset=0):
    return jnp.zeros((1, 128), F32).at[0, offset:offset + v.shape[0]].set(v.astype(F32))


class _LocalReducer:
    def start(self, grads):
        return jnp.zeros((8, 128), F32)

    def middle(self, after):
        return jnp.zeros((8, 128), F32)

    def finish(self, after):
        return None


def _local_step(x, p, tgt, sm, w, late, reducer):
    t = x.shape[0]
    tm = min(512, t // 2)
    tm_s = min(256, t // 2)
    tw = min(1024, t // 2)

    w_in = w["w_in"]
    wa = w_in[:, :ATTN_Q + 2 * ATTN_KV]
    wd = jnp.pad(w_in[:, ATTN_Q + 2 * ATTN_KV:], ((0, 0), (0, DN_COLS - (D_IN - ATTN_Q - 2 * ATTN_KV))))
    conv_w = w["conv_w"]
    al_vec, dt_vec = _pad_lanes(sm["a_log"], DN_HEADS), _pad_lanes(sm["dt_bias"], DN_HEADS)
    sinks_vec = _pad_lanes(sm["sinks"])
    dn_w = sm["dn_norm"].reshape(1, 128)
    row = lambda v: v.reshape(1, D_MODEL)
    cos, sin = _rope_tables(t)

    u, pa, pd = _inproj(x, row(sm["norm_mix"]), wa, wd, tm_s)
    ao = _attn_fwd(pa, cos, sin, sinks_vec)
    qn, kn, vc, gc, gr = _dn_prep(pd, conv_w, al_vec, dt_vec, tm_s)
    uu, ww, qg, kd, a_qk, tinv, dlast = _dn_intra(qn, kn, vc, gc, gr)
    dn_out, o, vnew, sall = _dn_scan_fwd(uu, ww, qg, kd, a_qk, dlast, pd, dn_w)
    w_o, late_rest = late(dn_out)
    wo_a, wo_d = w_o[:ATTN_Q], w_o[ATTN_Q:]
    h1 = _oproj(x, ao, dn_out, wo_a, wo_d, tm)
    w = dict(w, **late_rest(h1))
    w_proj = jnp.transpose(w["w_proj4"], (1, 0, 2)).reshape(PLE_DIM, D_MODEL)
    m, r, h2 = _mlp_fwd(h1, row(sm["norm_mlp"]), w["w_up4"], w["w_down"], tw)
    dh2, dh2b, dgp, dpp, n3, pb, acc_ple = _ple_loss(h2, p, tgt, row(sm["norm_ple"]), row(sm["norm_final"]),
                                                     w["w_gate"], w_proj, tm_s)
    g_w_gate = _wgrad(n3, dgp, "wgrad_gate", D_MODEL, D_MODEL, tw)
    g_w_proj = _wgrad(pb, dpp, "wgrad_proj", PLE_DIM, D_MODEL, tw)
    da, dh1, dh1b, acc_mlp = _mlp_bwd(dh2, dh2b, r, h1, row(sm["norm_mlp"]), w["w_up4"], w["w_down"], tm)
    g_w_up4 = _wgrad(m, da, "wgrad_up", D_MODEL, FF_BLOCK, tw, stacked=True)
    g_w_down = _wgrad(r, dh2b, "wgrad_down", FF_BLOCK, D_MODEL, tw,
                      prep=lambda rv: jnp.square(rv.astype(F32)).astype(MXU_DTYPE))
    g_w_o = jnp.concatenate([_wgrad(ao, dh1b, "wgrad_oa", ATTN_Q, D_MODEL, tw),
                             _wgrad(dn_out, dh1b, "wgrad_od", DN_W, D_MODEL, tw)], axis=0)
    early = dict(w_up4=g_w_up4, w_down=g_w_down, w_gate=g_w_gate, w_proj=g_w_proj, w_o=g_w_o)
    dep = reducer.start(early)
    dao, ddn = _oproj_bwd(dh1b, wo_a, wo_d, tm, dep)
    dz, du, dw, dqg, dkd, da_qk, ddlast, acc_dn = _dn_scan_bwd(ddn, o, vnew, sall, ww, qg, kd, a_qk, dlast, pd, dn_w)
    dep = reducer.middle(du)
    dqn, dkn, dvc, dgc = _dn_intra_bwd(qn, kn, vc, gc, gr, tinv, a_qk, du, dw, dqg, dkd, da_qk, ddlast, dlast, dep)
    dy, dba, acc_conv, acc_gate = _dn_prep_bwd(pd, conv_w, al_vec, dt_vec, dqn, dkn, dvc, dgc, tm_s)
    d_dn = _dn_conv_bwd(dy, dz, dba, conv_w, tm_s)
    dq, dk, dv, acc_attn = _attn_bwd(pa, cos, sin, sinks_vec, dao)
    reducer.finish(dq)
    wq, wk, wv = wa[:, :ATTN_Q], wa[:, ATTN_Q:ATTN_Q + ATTN_KV], wa[:, ATTN_Q + ATTN_KV:]
    dx, acc_mix = _inproj_bwd(x, dh1, row(sm["norm_mix"]), [dq, dk, dv, d_dn], [wq, wk, wv, wd], tm_s)

    g_w_in = jnp.concatenate([
        _wgrad(u, dq, "wgrad_q", D_MODEL, ATTN_Q, tw), _wgrad(u, dk, "wgrad_k", D_MODEL, ATTN_KV, tw),
        _wgrad(u, dv, "wgrad_v", D_MODEL, ATTN_KV, tw),
        _wgrad(u, d_dn, "wgrad_dn", D_MODEL, DN_COLS, tw)[:, :D_IN - ATTN_Q - 2 * ATTN_KV]], axis=1)
    grads = dict(early, w_in=g_w_in)
    sums = dict(loss=acc_ple[2, 0], norm_final=acc_ple[0], norm_ple=acc_ple[1], norm_mlp=acc_mlp[0], norm_mix=acc_mix[0],
                dn_norm=acc_dn[0], sinks=acc_attn[0, :ATTN_HEADS], a_log=acc_gate[0, DN_HEADS:2 * DN_HEADS],
                dt_bias=acc_gate[1, DN_HEADS:2 * DN_HEADS], conv_w=acc_conv[:DN_CONV])
    return sums, dx, grads


MESH = pl.DeviceIdType.MESH
ANY = pl.BlockSpec(memory_space=pl.ANY)
N_CHIPS = 4
N_DEV = 8


def _place():
    x, y, c = lax.axis_index("x"), lax.axis_index("y"), lax.axis_index("c")
    chips = [(1 - x, y), (x, 1 - y), (1 - x, 1 - y)]
    return x, y, c, chips


def _gather_weights(shards, conv_s):
    n = len(shards)
    per = 7

    def body(*refs):
        in_refs, conv_ref = refs[:n], refs[n]
        out_refs, conv_out = refs[n + 1:2 * n + 1], refs[2 * n + 1]
        send_sems, recv_sems = refs[2 * n + 2:]
        x, y, c, chips = _place()
        sibling = (x, y, 1 - c)

        def blk(a, px, py, pc):
            hr = in_refs[a].shape[0] // 2
            return out_refs[a].at[2 * px + py, pl.ds(pc * hr, hr), :]

        def mine(a):
            hr = in_refs[a].shape[0] // 2
            return in_refs[a].at[pl.ds(c * hr, hr), :]

        def rcopy(a, k, block, to, src=None):
            return pltpu.make_async_remote_copy(
                src_ref=blk(a, *block) if src is None else src, dst_ref=blk(a, *block),
                send_sem=send_sems.at[per * a + k], recv_sem=recv_sems.at[per * a + k],
                device_id=to, device_id_type=MESH)

        def whole(a, to):
            return pltpu.make_async_remote_copy(
                src_ref=in_refs[a], dst_ref=out_refs[a].at[2 * x + y],
                send_sem=send_sems.at[per * a], recv_sem=recv_sems.at[per * a], device_id=to, device_id_type=MESH)

        def ccopy(j, to):
            return pltpu.make_async_remote_copy(
                src_ref=conv_ref, dst_ref=conv_out.at[2 * x + y],
                send_sem=send_sems.at[per * n + j], recv_sem=recv_sems.at[per * n + j],
                device_id=to, device_id_type=MESH)

        started = []
        for a in range(n):
            first = [whole(a, sibling)]
            first += [rcopy(a, 1 + j, (x, y, c), (*chip, c), src=mine(a)) for j, chip in enumerate(chips)]
            for cp in first:
                cp.start()
            started += first
        conv_sends = [ccopy(j, (*chip, c)) for j, chip in enumerate(chips)] + [ccopy(3, sibling)]
        for cp in conv_sends:
            cp.start()
        started += conv_sends
        for a in range(n):
            for j, chip in enumerate(chips):
                rcopy(a, 1 + j, (*chip, c), (x, y, c)).wait_recv()
                fwd = rcopy(a, 4 + j, (*chip, c), sibling)
                fwd.start()
                started.append(fwd)
        for a in range(n):
            whole(a, sibling).wait_recv()
            for j, chip in enumerate(chips):
                rcopy(a, 4 + j, (*chip, 1 - c), (x, y, c)).wait_recv()
        for j, chip in enumerate(chips + [(x, y)]):
            pltpu.make_async_remote_copy(
                src_ref=conv_ref, dst_ref=conv_out.at[2 * chip[0] + chip[1]],
                send_sem=send_sems.at[per * n + j], recv_sem=recv_sems.at[per * n + j],
                device_id=sibling, device_id_type=MESH).wait_recv()
        for cp in started:
            cp.wait_send()

    nsem = per * n + 4
    out_shape = [jax.ShapeDtypeStruct((N_CHIPS,) + s.shape, s.dtype) for s in shards]
    out_shape.append(jax.ShapeDtypeStruct((N_CHIPS,) + conv_s.shape, conv_s.dtype))
    return pl.pallas_call(
        body, name="gather_weights", in_specs=[ANY] * (n + 1), out_specs=[ANY] * (n + 1), out_shape=out_shape,
        scratch_shapes=[pltpu.SemaphoreType.DMA((nsem,)), pltpu.SemaphoreType.DMA((nsem,))],
    )(*shards, conv_s)


HBM = pl.BlockSpec(memory_space=pltpu.HBM)
SEM = pl.BlockSpec(memory_space=pltpu.SEMAPHORE)
EFFECT = pltpu.SideEffectType.DATAFLOW_SIDE_EFFECTING
LATE_COPIES = 7


def _late_copies(in_refs, land_refs, send_sems, recv_sems, only=None):
    x, y, c, chips = _place()
    sends, arrivals = [], []
    for a, (src, land) in enumerate(zip(in_refs, land_refs)):
        if only is not None and a not in only:
            continue
        hr = src.shape[0] // 2
        base = LATE_COPIES * a

        def cp(src_ref, dst_ref, s_idx, r_idx, to):
            return pltpu.make_async_remote_copy(src_ref=src_ref, dst_ref=dst_ref, send_sem=send_sems.at[base + s_idx],
                                                recv_sem=recv_sems.at[base + r_idx], device_id=to, device_id_type=MESH)

        sends.append(cp(src, land.at[2 * x + y], 0, 0, (x, y, 1 - c)))
        arrivals.append(cp(src, land.at[2 * x + y], 0, 0, (x, y, 1 - c)))
        for j, chip in enumerate(chips):
            for pc in range(2):
                half = src.at[pl.ds(c * hr, hr), :]
                sends.append(cp(half, land.at[2 * x + y, pl.ds(c * hr, hr), :], 1 + 2 * j + pc, 1 + 2 * j + c, (*chip, pc)))
                arrivals.append(cp(half, land.at[2 * chip[0] + chip[1], pl.ds(pc * hr, hr), :], 1 + 2 * j + pc,
                                   1 + 2 * j + pc, (*chip, pc)))
    return sends, arrivals


def _copies_start(name, build, nsem, srcs, land_shapes, after):
    n = len(srcs)

    def body(*refs):
        sends, _ = build(refs[:n], refs[n:2 * n], refs[2 * n + 1], refs[2 * n + 2])
        for cp in sends:
            cp.start()
        refs[-1][...] = jnp.zeros_like(refs[-1])

    lands = [pltpu.with_memory_space_constraint(lax.empty(s.shape, s.dtype), pltpu.HBM) for s in land_shapes]
    ins = [pltpu.with_memory_space_constraint(s, pltpu.HBM) for s in srcs]
    out = pl.pallas_call(
        body, name=name,
        out_shape=(pltpu.SemaphoreType.DMA((nsem,)), pltpu.SemaphoreType.DMA((nsem,)),
                   *[pltpu.HBM(s.shape, s.dtype) for s in srcs], *[pltpu.HBM(s.shape, s.dtype) for s in land_shapes],
                   jax.ShapeDtypeStruct((8, 128), F32)),
        in_specs=[HBM] * (2 * n) + [ANY],
        out_specs=(SEM, SEM, *[HBM] * (2 * n), pl.BlockSpec(memory_space=pltpu.VMEM)),
        input_output_aliases={i: 2 + i for i in range(2 * n)},
        compiler_params=pltpu.CompilerParams(has_side_effects=EFFECT),
    )(*ins, *lands, after)
    return out[0], out[1], out[2:2 + n], out[2 + n:2 + 2 * n], out[-1]


def _copies_wait(name, build, started, after):
    send_sems, recv_sems, srcs, lands, _ = started
    n = len(srcs)

    def body(*refs):
        sends, arrivals = build(refs[:n], refs[n:2 * n], refs[2 * n], refs[2 * n + 1])
        for cp in sends:
            cp.wait_send()
        for cp in arrivals:
            cp.wait_recv()

    out = pl.pallas_call(
        body, name=name,
        out_shape=(*[pltpu.HBM(s.shape, s.dtype) for s in srcs], *[pltpu.HBM(l.shape, l.dtype) for l in lands]),
        in_specs=[HBM] * (2 * n) + [SEM, SEM, ANY],
        out_specs=tuple([HBM] * (2 * n)),
        input_output_aliases={i: i for i in range(2 * n)},
        compiler_params=pltpu.CompilerParams(has_side_effects=EFFECT),
    )(*srcs, *lands, send_sems, recv_sems, after)
    return out[:n], out[n:]


def _exchange_copies(g_refs, got_refs, send_sems, recv_sems):
    x, y, c, _ = _place()
    sends, arrivals = [], []
    for a, (g, got) in enumerate(zip(g_refs, got_refs)):
        hr = g.shape[1] // 2
        cp = pltpu.make_async_remote_copy(
            src_ref=g.at[:, pl.ds((1 - c) * hr, hr), :], dst_ref=got, send_sem=send_sems.at[a],
            recv_sem=recv_sems.at[a], device_id=(x, y, 1 - c), device_id_type=MESH)
        sends.append(cp)
        arrivals.append(cp)
    return sends, arrivals


def _scatter_copies(s_refs, got_refs, send_sems, recv_sems):
    x, y, c, chips = _place()
    sends, arrivals = [], []
    for a, (s16, got) in enumerate(zip(s_refs, got_refs)):
        for j, chip in enumerate(chips):
            cp = pltpu.make_async_remote_copy(
                src_ref=s16.at[2 * chip[0] + chip[1]], dst_ref=got.at[j], send_sem=send_sems.at[3 * a + j],
                recv_sem=recv_sems.at[3 * a + j], device_id=(*chip, c), device_id_type=MESH)
            sends.append(cp)
            arrivals.append(cp)
    return sends, arrivals


def _share_halves(name, bufs, dep):
    n = len(bufs)

    def body(*refs):
        out_refs = refs[n + 1:2 * n + 1]
        send_sems, recv_sems = refs[2 * n + 1:]
        x, y, c, _ = _place()
        remote = [pltpu.make_async_remote_copy(
            src_ref=out_refs[a].at[c], dst_ref=out_refs[a].at[c], send_sem=send_sems.at[a], recv_sem=recv_sems.at[a],
            device_id=(x, y, 1 - c), device_id_type=MESH) for a in range(n)]
        for cp in remote:
            cp.start()
        for a in range(n):
            pltpu.make_async_remote_copy(
                src_ref=out_refs[a].at[c], dst_ref=out_refs[a].at[1 - c], send_sem=send_sems.at[a],
                recv_sem=recv_sems.at[a], device_id=(x, y, 1 - c), device_id_type=MESH).wait_recv()
        for cp in remote:
            cp.wait_send()

    return pl.pallas_call(
        body, name=name, in_specs=[ANY] * (n + 1), out_specs=[ANY] * n,
        out_shape=[jax.ShapeDtypeStruct(b.shape, b.dtype) for b in bufs],
        input_output_aliases={a: a for a in range(n)},
        scratch_shapes=[pltpu.SemaphoreType.DMA((n,)), pltpu.SemaphoreType.DMA((n,))],
    )(*bufs, dep)


SMALL_ROWS, SMALL_COLS = 16, CONV_CH


def _allreduce_small(block):
    m_per, ncol = block.shape

    def body(x_ref, sum_ref, all_ref, send_sems, recv_sems, local_sem):
        x, y, c, chips = _place()
        me, sibling = (x, y, c), (x, y, 1 - c)

        def rows(px, py, pc):
            return all_ref.at[pl.ds((4 * px + 2 * py + pc) * m_per, m_per), :]

        def copy(k, block_of, to, src=None):
            return pltpu.make_async_remote_copy(
                src_ref=rows(*block_of) if src is None else src, dst_ref=rows(*block_of),
                send_sem=send_sems.at[k], recv_sem=recv_sems.at[k], device_id=to, device_id_type=MESH)

        mine = pltpu.make_async_copy(x_ref, rows(*me), local_sem)
        mine.start()
        first = [copy(0, me, sibling, src=x_ref)]
        first += [copy(1 + j, me, (*chip, c), src=x_ref) for j, chip in enumerate(chips)]
        for cp in first:
            cp.start()
        passed = [copy(4 + j, (*chip, c), sibling) for j, chip in enumerate(chips)]
        for j, chip in enumerate(chips):
            copy(1 + j, (*chip, c), me).wait_recv()
            passed[j].start()
        copy(0, sibling, me).wait_recv()
        for j, chip in enumerate(chips):
            copy(4 + j, (*chip, 1 - c), me).wait_recv()
        for cp in first + passed:
            cp.wait_send()
        mine.wait()
        total = all_ref[0:m_per, :]
        for d in range(1, N_DEV):
            total = total + all_ref[d * m_per:(d + 1) * m_per, :]
        sum_ref[...] = total

    vm = pl.BlockSpec(memory_space=pltpu.VMEM)
    return pl.pallas_call(
        body, name="allreduce_small", in_specs=[vm], out_specs=vm,
        out_shape=jax.ShapeDtypeStruct((m_per, ncol), F32),
        scratch_shapes=[pltpu.VMEM((N_DEV * m_per, ncol), F32), pltpu.SemaphoreType.DMA((7,)),
                        pltpu.SemaphoreType.DMA((7,)), pltpu.SemaphoreType.DMA],
    )(block)


def _row_tile(rows, cols):
    tile = rows
    while tile * cols * 4 > (1 << 20) and tile % 16 == 0:
        tile //= 2
    return tile


def _elementwise(fn, name, ins, out_dtypes, dep):
    rows, cols = ins[0].shape
    tile = _row_tile(rows, cols)

    def body(*refs):
        outs = fn(*[r[...] for r in refs[:len(ins)]])
        for o_ref, o in zip(refs[len(ins) + 1:], outs):
            o_ref[...] = o.astype(o_ref.dtype)

    spec = pl.BlockSpec((tile, cols), lambda i: (i, 0))
    return pl.pallas_call(
        body, name=name, grid=(rows // tile,), in_specs=[spec] * len(ins) + [pl.BlockSpec(memory_space=pl.ANY)],
        out_specs=[spec] * len(out_dtypes),
        out_shape=[jax.ShapeDtypeStruct((rows, cols), d) for d in out_dtypes],
        compiler_params=_params("parallel"),
    )(*ins, dep)


def _adamw_tile(w, g, m, v):
    m = ADAM_B1 * m + (1.0 - ADAM_B1) * g
    v = ADAM_B2 * v + (1.0 - ADAM_B2) * jnp.square(g)
    m_hat = m / (1.0 - ADAM_B1 ** ADAM_STEP)
    v_hat = v / (1.0 - ADAM_B2 ** ADAM_STEP)
    delta = -ADAM_LR * (m_hat / (jnp.sqrt(v_hat) + ADAM_EPS) + ADAM_WD * w)
    return delta, m, v


def _adamw(name, w, g, m, v, dep):
    return _elementwise(_adamw_tile, name, [w, g, m, v], [F32, F32, F32], dep)


def _chip_sum(name, g4, got, place):
    nchip, hr, cols = got.shape
    tile = _row_tile(hr, cols)
    nblk = hr // tile

    def body(pl_ref, g_ref, o_ref, s32_ref, s16_ref):
        s = g_ref[...] + o_ref[...]
        s32_ref[...] = s
        s16_ref[...] = s.astype(BF16)

    spec = pl.BlockSpec((None, tile, cols), lambda k, i, pr: (k, i, 0))
    return pl.pallas_call(
        body, name=name,
        grid_spec=pltpu.PrefetchScalarGridSpec(
            num_scalar_prefetch=1, grid=(nchip, nblk),
            in_specs=[pl.BlockSpec((None, tile, cols), lambda k, i, pr: (k, pr[1] * nblk + i, 0)), spec],
            out_specs=[spec, spec]),
        out_shape=[jax.ShapeDtypeStruct(got.shape, F32), jax.ShapeDtypeStruct(got.shape, BF16)],
        compiler_params=_params("parallel", "parallel"),
    )(place, g4, got)


def _mesh_sum(name, s32, got, place):
    _, hr, cols = s32.shape
    tile = _row_tile(hr, cols)

    def body(pl_ref, own_ref, g0_ref, g1_ref, g2_ref, o_ref):
        o_ref[...] = ((own_ref[...] + g0_ref[...].astype(F32)) + g1_ref[...].astype(F32)) + g2_ref[...].astype(F32)

    slab = lambda j: pl.BlockSpec((None, tile, cols), lambda i, pr: (j, i, 0))
    return pl.pallas_call(
        body, name=name,
        grid_spec=pltpu.PrefetchScalarGridSpec(
            num_scalar_prefetch=1, grid=(hr // tile,),
            in_specs=[pl.BlockSpec((None, tile, cols), lambda i, pr: (pr[0], i, 0)), slab(0), slab(1), slab(2)],
            out_specs=pl.BlockSpec((None, tile, cols), lambda i, pr: (pr[1], i, 0))),
        out_shape=jax.ShapeDtypeStruct((2, hr, cols), F32),
        compiler_params=_params("parallel"),
    )(place, s32, got, got, got)


def _place_operand():
    return jnp.stack([2 * lax.axis_index("x") + lax.axis_index("y"), lax.axis_index("c")]).astype(jnp.int32)


def _per_chip(name, g):
    if name == "w_in":
        return jnp.transpose(g.reshape(D_MODEL, N_CHIPS, D_IN // N_CHIPS), (1, 0, 2))
    if name == "w_proj":
        return jnp.transpose(g.reshape(PLE_DIM, N_CHIPS, D_MODEL // N_CHIPS), (1, 0, 2))
    if name == "w_up4":
        return g
    return g.reshape(N_CHIPS, g.shape[0] // N_CHIPS, g.shape[1])


class _EarlyReducer:
    def __init__(self, tag):
        self.tag = tag

    def start(self, grads):
        self.names = list(grads)
        self.place = _place_operand()
        slabs = [_per_chip(k, grads[k]) for k in self.names]
        halves = [jax.ShapeDtypeStruct((s.shape[0], s.shape[1] // 2, s.shape[2]), F32) for s in slabs]
        self.a = _copies_start(self.tag + "exchange_start", _exchange_copies, len(slabs), slabs, halves, slabs[0])
        return self.a[-1]

    def middle(self, after):
        slabs, got = _copies_wait(self.tag + "exchange_wait", _exchange_copies, self.a, after)
        self.sums = [_chip_sum(self.tag + "chip_sum_" + k, s, g, self.place) for k, s, g in zip(self.names, slabs, got)]
        s16 = [s[1] for s in self.sums]
        lands = [jax.ShapeDtypeStruct((3,) + s.shape[1:], BF16) for s in s16]
        self.b = _copies_start(self.tag + "scatter_start", _scatter_copies, 3 * len(s16), s16, lands, s16[0])
        return self.b[-1]

    def finish(self, after):
        _, got = _copies_wait(self.tag + "scatter_wait", _scatter_copies, self.b, after)
        self.bufs = {k: _mesh_sum(self.tag + "mesh_sum_" + k, s[0], g, self.place)
                     for k, s, g in zip(self.names, self.sums, got)}


def kernel(x, p, norm_mix, w_in, conv_w, a_log, dt_bias, dn_norm, sinks, w_o, norm_mlp, w_up, w_down, norm_ple, w_ple_gate, w_ple_proj, norm_final, loss_target, m_norm_mix, m_w_in, m_conv_w, m_a_log, m_dt_bias, m_dn_norm, m_sinks, m_w_o, m_norm_mlp, m_w_up, m_w_down, m_norm_ple, m_w_ple_gate, m_w_ple_proj, m_norm_final, v_norm_mix, v_w_in, v_conv_w, v_a_log, v_dt_bias, v_dn_norm, v_sinks, v_w_o, v_norm_mlp, v_w_up, v_w_down, v_norm_ple, v_w_ple_gate, v_w_ple_proj, v_norm_final):
    chip = 2 * lax.axis_index("x") + lax.axis_index("y")
    big = dict(w_in=w_in[0], w_o=w_o[0], w_up=w_up[0], w_down=w_down[0], w_gate=w_ple_gate[0], w_proj=w_ple_proj[0])
    big_m = dict(w_in=m_w_in[0], w_o=m_w_o[0], w_up=m_w_up[0], w_down=m_w_down[0], w_gate=m_w_ple_gate[0], w_proj=m_w_ple_proj[0])
    big_v = dict(w_in=v_w_in[0], w_o=v_w_o[0], w_up=v_w_up[0], w_down=v_w_down[0], w_gate=v_w_ple_gate[0], w_proj=v_w_ple_proj[0])
    names = list(big)

    w_in_all, conv_all = _gather_weights([big["w_in"].astype(BF16)], conv_w[0])
    late_names = names[1:]
    late_shards = [big[k].astype(BF16) for k in late_names]
    gather = _copies_start("gather_start", _late_copies, LATE_COPIES * len(late_shards), late_shards,
                           [jax.ShapeDtypeStruct((N_CHIPS,) + s.shape, BF16) for s in late_shards], w_in_all)
    token = gather[-1]
    w = dict(w_in=jnp.transpose(w_in_all, (1, 0, 2)).reshape(D_MODEL, D_IN),
             conv_w=jnp.transpose(conv_all, (1, 0, 2)).reshape(DN_CONV, CONV_CH))
    sm = dict(norm_mix=norm_mix[0] + token[0, 0], a_log=a_log[0], dt_bias=dt_bias[0], dn_norm=dn_norm[0],
              sinks=sinks[0], norm_mlp=norm_mlp[0], norm_ple=norm_ple[0], norm_final=norm_final)

    def late(after):
        first = functools.partial(_late_copies, only=(0,))
        srcs, lands = _copies_wait("gather_wait_o", first, gather, after)

        def rest(after2):
            others = functools.partial(_late_copies, only=tuple(range(1, len(late_names))))
            gw = dict(zip(late_names, _copies_wait("gather_wait_rest", others, gather[:2] + (srcs, lands, None), after2)[1]))
            return dict(w_up4=gw["w_up"], w_down=gw["w_down"].reshape(D_FF, D_MODEL),
                        w_gate=gw["w_gate"].reshape(D_MODEL, D_MODEL), w_proj4=gw["w_proj"])

        return lands[0].reshape(D_MODEL, D_MODEL), rest

    reducer = _EarlyReducer("early_")
    sums, grad_x, g = _local_step(x[0], p[0, 0], loss_target[0], sm, w, late, reducer)

    last = _EarlyReducer("last_")
    dep_a = last.start({"w_in": g["w_in"]})

    row = lambda v: jnp.zeros((SMALL_COLS,), F32).at[:v.shape[0]].set(v)
    misc = jnp.zeros((SMALL_COLS,), F32).at[0:4].set(sums["a_log"]).at[4:8].set(sums["dt_bias"]) \
        .at[8:16].set(sums["sinks"]).at[128:256].set(sums["dn_norm"]).at[256].set(sums["loss"])
    small = jnp.concatenate([sums["conv_w"], jnp.stack([row(sums["norm_mix"]), row(sums["norm_mlp"]), row(sums["norm_ple"]),
                                                        row(sums["norm_final"]), misc]),
                             jnp.zeros((SMALL_ROWS - 9, SMALL_COLS), F32)], axis=0)
    tot = _allreduce_small(small + dep_a[0, 0])
    dep_b = last.middle(tot)
    grad_key = dict(w_o="w_o", w_up="w_up4", w_down="w_down", w_gate="w_gate", w_proj="w_proj")
    full = _share_halves("share_halves", [reducer.bufs[grad_key[k]] for k in late_names], dep_b)
    red = {k: f.reshape(-1, f.shape[-1]) for k, f in zip(late_names, full)}
    loss = tot[8, 256]
    ncw = CONV_CH // N_CHIPS

    def pack(cw, nmix, nmlp, nple, nfin, al, dtb, sk, dnn):
        misc_p = jnp.zeros((SMALL_COLS,), F32).at[0:4].set(al).at[4:8].set(dtb).at[8:16].set(sk).at[128:256].set(dnn)
        cw_p = jnp.zeros((DN_CONV, SMALL_COLS), F32).at[:, :ncw].set(cw)
        return jnp.concatenate([cw_p, jnp.stack([row(nmix), row(nmlp), row(nple), row(nfin), misc_p]),
                                jnp.zeros((SMALL_ROWS - 9, SMALL_COLS), F32)], axis=0)

    def unpack(buf):
        return dict(conv_w=buf[0:4, :ncw][None], norm_mix=buf[4, :D_MODEL][None], norm_mlp=buf[5, :D_MODEL][None],
                    norm_ple=buf[6, :D_MODEL][None], norm_final=buf[7, :D_MODEL], a_log=buf[8, 0:4][None],
                    dt_bias=buf[8, 4:8][None], sinks=buf[8, 8:16][None], dn_norm=buf[8, 128:256][None])

    g_conv_shard = lax.dynamic_slice(tot[0:4], (0, chip * ncw), (DN_CONV, ncw))
    g_small = pack(g_conv_shard, tot[4, :D_MODEL], tot[5, :D_MODEL], tot[6, :D_MODEL], tot[7, :D_MODEL],
                   tot[8, 0:4], tot[8, 4:8], tot[8, 8:16], tot[8, 128:256])
    w_small = pack(conv_w[0], norm_mix[0], norm_mlp[0], norm_ple[0], norm_final, a_log[0], dt_bias[0], sinks[0], dn_norm[0])
    m_small = pack(m_conv_w[0], m_norm_mix[0], m_norm_mlp[0], m_norm_ple[0], m_norm_final, m_a_log[0], m_dt_bias[0],
                   m_sinks[0], m_dn_norm[0])
    v_small = pack(v_conv_w[0], v_norm_mix[0], v_norm_mlp[0], v_norm_ple[0], v_norm_final, v_a_log[0], v_dt_bias[0],
                   v_sinks[0], v_dn_norm[0])

    ref_name = dict(w_in="w_in", w_o="w_o", w_up="w_up", w_down="w_down", w_gate="w_ple_gate", w_proj="w_ple_proj")
    out_g, out_d, out_m, out_v = {}, {}, {}, {}

    def update(k, dep):
        d_k, m_k, v_k = _adamw("adamw_" + k, big[k], red[k], big_m[k], big_v[k], dep)
        out_g[ref_name[k]], out_d[ref_name[k]] = red[k][None], d_k[None]
        out_m[ref_name[k]], out_v[ref_name[k]] = m_k[None], v_k[None]
        return d_k

    for k in late_names:
        done = update(k, dep_b)
    small_out = _adamw("adamw_small", w_small, g_small, m_small, v_small, dep_b)
    d_s, m_s, v_s = (unpack(b) for b in small_out)
    g_s = unpack(g_small)
    for src, dst in ((g_s, out_g), (d_s, out_d), (m_s, out_m), (v_s, out_v)):
        dst.update(src)
    last.finish(done + small_out[0][0:1, 0:1])
    (w_in_full,) = _share_halves("share_halves_w_in", [last.bufs["w_in"]], dep_b)
    red["w_in"] = w_in_full.reshape(-1, w_in_full.shape[-1])
    update("w_in", dep_b)
    order = ["norm_mix", "w_in", "conv_w", "a_log", "dt_bias", "dn_norm", "sinks", "w_o", "norm_mlp", "w_up", "w_down",
             "norm_ple", "w_ple_gate", "w_ple_proj", "norm_final"]
    return (loss, grad_x[None], *[out_g[k] for k in order], *[out_d[k] for k in order],
            *[out_m[k] for k in order], *[out_v[k] for k in order])
```

```python
import functools

import jax
import jax.numpy as jnp
from jax import lax
from jax.experimental import pallas as pl
from jax.experimental.pallas import tpu as pltpu

F32 = jnp.float32
BF16 = jnp.bfloat16
MXU_DTYPE = jnp.bfloat16
HI = lax.Precision.HIGHEST

D_MODEL = 1024
PLE_DIM = 256
ATTN_HEADS = 8
ATTN_KV_HEADS = 2
ATTN_GROUPS = ATTN_HEADS // ATTN_KV_HEADS
ATTN_HEAD_DIM = 64
ATTN_BLOCK = 128
ROPE_THETA = 10000.0
DN_HEADS = 4
DN_HEAD_DIM = 128
DN_CONV = 4
DN_CHUNK = 64
D_FF = 4 * D_MODEL
EPS = 1e-6
ATTN_Q = ATTN_HEADS * ATTN_HEAD_DIM
ATTN_KV = ATTN_KV_HEADS * ATTN_HEAD_DIM
DN_W = DN_HEADS * DN_HEAD_DIM
CONV_CH = 3 * DN_W
D_IN = ATTN_Q + 2 * ATTN_KV + 4 * DN_W + 2 * DN_HEADS
DN_COLS = 4 * DN_W + 128
DN_SCALE = DN_HEAD_DIM ** -0.5
ATTN_SCALE = ATTN_HEAD_DIM ** -0.5
FF_BLOCKS = 4
FF_BLOCK = D_FF // FF_BLOCKS

ADAM_LR = 0.001
ADAM_B1 = 0.9
ADAM_B2 = 0.999
ADAM_EPS = 1e-08
ADAM_WD = 0.01
ADAM_STEP = 10

V7X_VMEM_BYTES = 64 * 1024 * 1024
VMEM_LIMIT = 48 * 1024 * 1024

NN = ((1,), (0,))
NT = ((1,), (1,))
TN = ((0,), (0,))


def _dot(a, b, dims=NN, prec=None):
    return lax.dot_general(a, b, (dims, ((), ())), precision=prec, preferred_element_type=F32)


def _sigmoid(x):
    return 1.0 / (1.0 + jnp.exp(-x))


def _softplus(x):
    return jnp.maximum(x, 0.0) + jnp.log(1.0 + jnp.exp(-jnp.abs(x)))


def _params(*sem):
    return pltpu.CompilerParams(dimension_semantics=sem, vmem_limit_bytes=VMEM_LIMIT)


def _rms_fwd(xv, g):
    r = lax.rsqrt(jnp.mean(xv * xv, axis=-1, keepdims=True) + EPS)
    return xv * r * g


def _rms_bwd(xv, g, dn):
    r = lax.rsqrt(jnp.mean(xv * xv, axis=-1, keepdims=True) + EPS)
    xh = xv * r
    dg = jnp.sum(dn * xh, axis=0, keepdims=True)
    dxh = dn * g
    dx = r * (dxh - xh * jnp.mean(dxh * xh, axis=-1, keepdims=True))
    return dx, dg


def _full(shape):
    return pl.BlockSpec(shape, lambda *_: (0,) * len(shape))


def _inproj(x, g_mix, wa, wd, tm):
    t = x.shape[0]

    def body(x_ref, g_ref, wa_ref, wd_ref, u_ref, pa_ref, pd_ref):
        u = _rms_fwd(x_ref[...], g_ref[...]).astype(MXU_DTYPE)
        u_ref[...] = u
        pa_ref[...] = _dot(u, wa_ref[...])
        pd_ref[...] = _dot(u, wd_ref[...])

    na, nd = wa.shape[1], wd.shape[1]
    return pl.pallas_call(
        body, name="inproj", grid=(t // tm,),
        in_specs=[pl.BlockSpec((tm, D_MODEL), lambda i: (i, 0)), _full((1, D_MODEL)),
                  _full((D_MODEL, na)), _full((D_MODEL, nd))],
        out_specs=[pl.BlockSpec((tm, D_MODEL), lambda i: (i, 0)), pl.BlockSpec((tm, na), lambda i: (i, 0)),
                   pl.BlockSpec((tm, nd), lambda i: (i, 0))],
        out_shape=[jax.ShapeDtypeStruct((t, D_MODEL), MXU_DTYPE), jax.ShapeDtypeStruct((t, na), F32),
                   jax.ShapeDtypeStruct((t, nd), F32)],
        compiler_params=_params("parallel"),
    )(x, g_mix, wa, wd)


def _oproj(x, ao, dn, wo_a, wo_d, tm):
    t = x.shape[0]

    def body(x_ref, ao_ref, dn_ref, wa_ref, wd_ref, h_ref):
        h_ref[...] = (x_ref[...] + _dot(ao_ref[...].astype(MXU_DTYPE), wa_ref[...])
                      + _dot(dn_ref[...].astype(MXU_DTYPE), wd_ref[...]))

    half = ao.shape[1]
    return pl.pallas_call(
        body, name="oproj", grid=(t // tm,),
        in_specs=[pl.BlockSpec((tm, D_MODEL), lambda i: (i, 0)), pl.BlockSpec((tm, half), lambda i: (i, 0)),
                  pl.BlockSpec((tm, half), lambda i: (i, 0)), _full((half, D_MODEL)), _full((half, D_MODEL))],
        out_specs=pl.BlockSpec((tm, D_MODEL), lambda i: (i, 0)),
        out_shape=jax.ShapeDtypeStruct((t, D_MODEL), F32),
        compiler_params=_params("parallel"),
    )(x, ao, dn, wo_a, wo_d)


def _mlp_fwd(h1, g_mlp, w_up4, w_down, tm):
    t = h1.shape[0]

    def body(h_ref, g_ref, wu_ref, wd_ref, m_ref, r_ref, h2_ref, acc_ref):
        k = pl.program_id(1)

        @pl.when(k == 0)
        def _():
            m_ref[...] = _rms_fwd(h_ref[...], g_ref[...]).astype(MXU_DTYPE)
            acc_ref[...] = jnp.zeros_like(acc_ref)

        r = jnp.maximum(_dot(m_ref[...], wu_ref[...]), 0.0)
        r_ref[...] = r.astype(MXU_DTYPE)
        s = jnp.square(r).astype(MXU_DTYPE)
        acc_ref[...] += _dot(s, wd_ref[...])

        @pl.when(k == FF_BLOCKS - 1)
        def _():
            h2_ref[...] = h_ref[...] + acc_ref[...]

    return pl.pallas_call(
        body, name="mlp_fwd", grid=(t // tm, FF_BLOCKS),
        in_specs=[pl.BlockSpec((tm, D_MODEL), lambda i, k: (i, 0)), _full((1, D_MODEL)),
                  pl.BlockSpec((None, D_MODEL, FF_BLOCK), lambda i, k: (k, 0, 0)),
                  pl.BlockSpec((FF_BLOCK, D_MODEL), lambda i, k: (k, 0))],
        out_specs=[pl.BlockSpec((tm, D_MODEL), lambda i, k: (i, 0)), pl.BlockSpec((tm, FF_BLOCK), lambda i, k: (i, k)),
                   pl.BlockSpec((tm, D_MODEL), lambda i, k: (i, 0))],
        out_shape=[jax.ShapeDtypeStruct((t, D_MODEL), MXU_DTYPE), jax.ShapeDtypeStruct((t, D_FF), MXU_DTYPE),
                   jax.ShapeDtypeStruct((t, D_MODEL), F32)],
        scratch_shapes=[pltpu.VMEM((tm, D_MODEL), F32)],
        compiler_params=_params("parallel", "arbitrary"),
    )(h1, g_mlp, w_up4, w_down)


def _ple_loss(h2, p, tgt, g_ple, g_fin, w_gate, w_proj, tm):
    t = h2.shape[0]

    def body(h_ref, p_ref, t_ref, gp_ref, gf_ref, wg_ref, wp_ref,
             dh_ref, dhb_ref, dgp_ref, dpp_ref, n3_ref, pb_ref, acc_ref):
        @pl.when(pl.program_id(0) == 0)
        def _():
            acc_ref[...] = jnp.zeros_like(acc_ref)

        h = h_ref[...]
        g_ple_v, g_fin_v = gp_ref[...], gf_ref[...]
        n3 = _rms_fwd(h, g_ple_v).astype(MXU_DTYPE)
        n3_ref[...] = n3
        gate = _sigmoid(_dot(n3, wg_ref[...]))
        pb = p_ref[...].astype(MXU_DTYPE)
        pb_ref[...] = pb
        pp = _dot(pb, wp_ref[...])
        h3 = h + gate * pp
        r4 = lax.rsqrt(jnp.mean(h3 * h3, axis=-1, keepdims=True) + EPS)
        xh4 = h3 * r4
        e = xh4 * g_fin_v - t_ref[...]
        loss = 0.5 * jnp.sum(jnp.mean(e * e, axis=-1, keepdims=True), axis=0, keepdims=True)
        dy = e * (1.0 / D_MODEL)
        dg_fin = jnp.sum(dy * xh4, axis=0, keepdims=True)
        dxh = dy * g_fin_v
        dh3 = r4 * (dxh - xh4 * jnp.mean(dxh * xh4, axis=-1, keepdims=True))
        dpp_ref[...] = (dh3 * gate).astype(MXU_DTYPE)
        dgp = (dh3 * pp * gate * (1.0 - gate)).astype(MXU_DTYPE)
        dgp_ref[...] = dgp
        dn3 = _dot(dgp, wg_ref[...], NT)
        dx, dg_ple = _rms_bwd(h, g_ple_v, dn3)
        dh2 = dh3 + dx
        dh_ref[...] = dh2
        dhb_ref[...] = dh2.astype(MXU_DTYPE)
        acc_ref[0:1, :] += dg_fin
        acc_ref[1:2, :] += dg_ple
        acc_ref[2:3, :] += jnp.broadcast_to(loss, (1, D_MODEL))

    row = lambda w: pl.BlockSpec((tm, w), lambda i: (i, 0))
    return pl.pallas_call(
        body, name="ple_loss", grid=(t // tm,),
        in_specs=[row(D_MODEL), row(PLE_DIM), row(D_MODEL), _full((1, D_MODEL)), _full((1, D_MODEL)),
                  _full((D_MODEL, D_MODEL)), _full((PLE_DIM, D_MODEL))],
        out_specs=[row(D_MODEL), row(D_MODEL), row(D_MODEL), row(D_MODEL), row(D_MODEL), row(PLE_DIM),
                   _full((8, D_MODEL))],
        out_shape=[jax.ShapeDtypeStruct((t, D_MODEL), F32), jax.ShapeDtypeStruct((t, D_MODEL), MXU_DTYPE),
                   jax.ShapeDtypeStruct((t, D_MODEL), MXU_DTYPE), jax.ShapeDtypeStruct((t, D_MODEL), MXU_DTYPE),
                   jax.ShapeDtypeStruct((t, D_MODEL), MXU_DTYPE), jax.ShapeDtypeStruct((t, PLE_DIM), MXU_DTYPE),
                   jax.ShapeDtypeStruct((8, D_MODEL), F32)],
        compiler_params=_params("arbitrary"),
    )(h2, p, tgt, g_ple, g_fin, w_gate, w_proj)


def _mlp_bwd(dh2, dh2b, r, h1, g_mlp, w_up4, w_down, tm):
    t = h1.shape[0]

    def body(dh_ref, dhb_ref, r_ref, h_ref, g_ref, wu_ref, wd_ref,
             da_ref, dh1_ref, dh1b_ref, acc_ref, dm_ref):
        i, k = pl.program_id(0), pl.program_id(1)

        @pl.when((i == 0) & (k == 0))
        def _():
            acc_ref[...] = jnp.zeros_like(acc_ref)

        @pl.when(k == 0)
        def _():
            dm_ref[...] = jnp.zeros_like(dm_ref)

        ds = _dot(dhb_ref[...], wd_ref[...], NT)
        da = (ds * (2.0 * r_ref[...].astype(F32))).astype(MXU_DTYPE)
        da_ref[...] = da
        dm_ref[...] += _dot(da, wu_ref[...], NT)

        @pl.when(k == FF_BLOCKS - 1)
        def _():
            dx, dg = _rms_bwd(h_ref[...], g_ref[...], dm_ref[...])
            dh1 = dh_ref[...] + dx
            dh1_ref[...] = dh1
            dh1b_ref[...] = dh1.astype(MXU_DTYPE)
            acc_ref[0:1, :] += dg

    tok = lambda w: pl.BlockSpec((tm, w), lambda i, k: (i, 0))
    return pl.pallas_call(
        body, name="mlp_bwd", grid=(t // tm, FF_BLOCKS),
        in_specs=[tok(D_MODEL), tok(D_MODEL), pl.BlockSpec((tm, FF_BLOCK), lambda i, k: (i, k)), tok(D_MODEL),
                  _full((1, D_MODEL)), pl.BlockSpec((None, D_MODEL, FF_BLOCK), lambda i, k: (k, 0, 0)),
                  pl.BlockSpec((FF_BLOCK, D_MODEL), lambda i, k: (k, 0))],
        out_specs=[pl.BlockSpec((tm, FF_BLOCK), lambda i, k: (i, k)),
                   tok(D_MODEL), tok(D_MODEL), pl.BlockSpec((8, D_MODEL), lambda i, k: (0, 0))],
        out_shape=[jax.ShapeDtypeStruct((t, D_FF), MXU_DTYPE),
                   jax.ShapeDtypeStruct((t, D_MODEL), F32), jax.ShapeDtypeStruct((t, D_MODEL), MXU_DTYPE),
                   jax.ShapeDtypeStruct((8, D_MODEL), F32)],
        scratch_shapes=[pltpu.VMEM((tm, D_MODEL), F32)],
        compiler_params=_params("arbitrary", "arbitrary"),
    )(dh2, dh2b, r, h1, g_mlp, w_up4, w_down)


def _oproj_bwd(dh1b, wo_a, wo_d, tm, dep):
    t = dh1b.shape[0]
    half = wo_a.shape[0]

    def body(d_ref, wa_ref, wd_ref, dep_ref, da_ref, dd_ref):
        d = d_ref[...]
        da_ref[...] = _dot(d, wa_ref[...], NT)
        dd_ref[...] = _dot(d, wd_ref[...], NT)

    return pl.pallas_call(
        body, name="oproj_bwd", grid=(t // tm,),
        in_specs=[pl.BlockSpec((tm, D_MODEL), lambda i: (i, 0)), _full((half, D_MODEL)), _full((half, D_MODEL)),
                  pl.BlockSpec(memory_space=pl.ANY)],
        out_specs=[pl.BlockSpec((tm, half), lambda i: (i, 0)), pl.BlockSpec((tm, half), lambda i: (i, 0))],
        out_shape=[jax.ShapeDtypeStruct((t, half), F32), jax.ShapeDtypeStruct((t, half), F32)],
        compiler_params=_params("parallel"),
    )(dh1b, wo_a, wo_d, dep)


def _inproj_bwd(x, dh1, g_mix, grads, weights, tm):
    t = x.shape[0]
    n = len(grads)

    def body(*refs):
        x_ref, dh_ref, g_ref = refs[:3]
        g_refs, w_refs = refs[3:3 + n], refs[3 + n:3 + 2 * n]
        dx_ref, acc_ref = refs[3 + 2 * n:]

        @pl.when(pl.program_id(0) == 0)
        def _():
            acc_ref[...] = jnp.zeros_like(acc_ref)

        du = _dot(g_refs[0][...], w_refs[0][...], NT)
        for j in range(1, n):
            du += _dot(g_refs[j][...], w_refs[j][...], NT)
        dx, dg = _rms_bwd(x_ref[...], g_ref[...], du)
        dx_ref[...] = dh_ref[...] + dx
        acc_ref[0:1, :] += dg

    tok = lambda w: pl.BlockSpec((tm, w), lambda i: (i, 0))
    return pl.pallas_call(
        body, name="inproj_bwd", grid=(t // tm,),
        in_specs=[tok(D_MODEL), tok(D_MODEL), _full((1, D_MODEL))] + [tok(g.shape[1]) for g in grads]
                 + [_full(w.shape) for w in weights],
        out_specs=[tok(D_MODEL), _full((8, D_MODEL))],
        out_shape=[jax.ShapeDtypeStruct((t, D_MODEL), F32), jax.ShapeDtypeStruct((8, D_MODEL), F32)],
        compiler_params=_params("arbitrary"),
    )(x, dh1, g_mix, *grads, *weights)


def _wgrad(a, b, name, tk, tn, tt, stacked=False, prep=None):
    t, kdim = a.shape
    ncols = b.shape[1]

    def body(a_ref, b_ref, o_ref):
        @pl.when(pl.program_id(2) == 0)
        def _():
            o_ref[...] = jnp.zeros_like(o_ref)

        av = a_ref[...] if prep is None else prep(a_ref[...])
        o_ref[...] += _dot(av, b_ref[...], TN)

    if stacked:
        out_spec = pl.BlockSpec((None, tk, tn), lambda i, j, s: (j, i, 0))
        out_shape = jax.ShapeDtypeStruct((ncols // tn, kdim, tn), F32)
    else:
        out_spec = pl.BlockSpec((tk, tn), lambda i, j, s: (i, j))
        out_shape = jax.ShapeDtypeStruct((kdim, ncols), F32)
    return pl.pallas_call(
        body, name=name, grid=(kdim // tk, ncols // tn, t // tt),
        in_specs=[pl.BlockSpec((tt, tk), lambda i, j, s: (s, i)), pl.BlockSpec((tt, tn), lambda i, j, s: (s, j))],
        out_specs=out_spec, out_shape=out_shape,
        compiler_params=_params("parallel", "parallel", "arbitrary"),
    )(a, b)


def _wgrad_cat(as_, bs, name, tt):
    t = as_[0].shape[0]
    heights = [a.shape[1] for a in as_]
    widths = [b.shape[1] for b in bs]

    def body(*refs):
        a_refs, b_refs, o_ref = refs[:len(as_)], refs[len(as_):-1], refs[-1]

        @pl.when(pl.program_id(0) == 0)
        def _():
            o_ref[...] = jnp.zeros_like(o_ref)

        row = 0
        for a_ref, k in zip(a_refs, heights):
            av = a_ref[...]
            col = 0
            for b_ref, n in zip(b_refs, widths):
                o_ref[row:row + k, col:col + n] += _dot(av, b_ref[...], TN)
                col += n
            row += k

    tok = lambda w: pl.BlockSpec((tt, w), lambda s: (s, 0))
    shape = (sum(heights), sum(widths))
    return pl.pallas_call(
        body, name=name, grid=(t // tt,),
        in_specs=[tok(k) for k in heights] + [tok(n) for n in widths],
        out_specs=_full(shape), out_shape=jax.ShapeDtypeStruct(shape, F32),
        compiler_params=_params("arbitrary"),
    )(*as_, *bs)


def _rope_tables(t):
    half = ATTN_HEAD_DIM // 2
    inv = 1.0 / (ROPE_THETA ** (jnp.arange(half, dtype=F32) * (2.0 / ATTN_HEAD_DIM)))
    ang = jnp.arange(t, dtype=F32)[:, None] * inv[None, :]
    cos, sin = jnp.cos(ang), jnp.sin(ang)
    cos2 = jnp.concatenate([cos, cos], axis=-1)
    sin2 = jnp.concatenate([-sin, sin], axis=-1)
    return jnp.tile(cos2, (1, 2)), jnp.tile(sin2, (1, 2))


def _swap_halves(tv):
    w = tv.shape[-1]
    lane = lax.broadcasted_iota(jnp.int32, tv.shape, tv.ndim - 1)
    first = (lane % ATTN_HEAD_DIM) < (ATTN_HEAD_DIM // 2)
    return jnp.where(first, pltpu.roll(tv, w - ATTN_HEAD_DIM // 2, tv.ndim - 1),
                     pltpu.roll(tv, ATTN_HEAD_DIM // 2, tv.ndim - 1))


def _rope(tv, cos, sin):
    return tv * cos + _swap_halves(tv) * sin


def _rope_bwd(dv, cos, sin):
    return dv * cos + _swap_halves(dv * sin)


def _attn_valid(first_block):
    c = lax.broadcasted_iota(jnp.int32, (2 * ATTN_BLOCK, ATTN_BLOCK), 0)
    r = lax.broadcasted_iota(jnp.int32, (2 * ATTN_BLOCK, ATTN_BLOCK), 1)
    return (c > r) & (c <= r + ATTN_BLOCK) & ((c >= ATTN_BLOCK) | jnp.logical_not(first_block))


def _attn_probs(st, sink, valid):
    s = jnp.where(valid, st * ATTN_SCALE, -jnp.inf)
    m = jnp.maximum(jnp.max(s, axis=0, keepdims=True), sink)
    e = jnp.where(valid, jnp.exp(s - m), 0.0)
    es = jnp.exp(sink - m)
    inv = 1.0 / (jnp.sum(e, axis=0, keepdims=True) + es)
    return e * inv, es * inv


def _lane_scalar(vec, idx):
    lane = lax.broadcasted_iota(jnp.int32, vec.shape, 1)
    return jnp.sum(jnp.where(lane == idx, vec, 0.0), axis=-1, keepdims=True)


def _attn_specs(nb):
    cur = lambda w, cb: pl.BlockSpec((ATTN_BLOCK, w), lambda i: (jnp.minimum(i, nb - 1), cb))
    prev = lambda w, cb: pl.BlockSpec((ATTN_BLOCK, w), lambda i: (jnp.maximum(jnp.minimum(i, nb - 1) - 1, 0), cb))
    kcol, vcol = ATTN_Q // ATTN_KV, ATTN_Q // ATTN_KV + 1
    return [cur(ATTN_Q, 0), cur(ATTN_KV, kcol), prev(ATTN_KV, kcol), cur(ATTN_KV, vcol), prev(ATTN_KV, vcol),
            cur(ATTN_KV, 0), cur(ATTN_KV, 0), prev(ATTN_KV, 0), prev(ATTN_KV, 0), _full((1, 128))]


def _attn_fwd(pa, cos, sin, sinks_vec):
    t = pa.shape[0]
    nb = t // ATTN_BLOCK

    def body(q_ref, kc_ref, kp_ref, vc_ref, vp_ref, cc_ref, sc_ref, cp_ref, sp_ref, sk_ref, o_ref):
        first = pl.program_id(0) == 0
        cc, sc = cc_ref[...], sc_ref[...]
        q = _rope(q_ref[...], jnp.tile(cc, (1, ATTN_Q // ATTN_KV)), jnp.tile(sc, (1, ATTN_Q // ATTN_KV)))
        kc = _rope(kc_ref[...], cc, sc)
        kp = _rope(kp_ref[...], cp_ref[...], sp_ref[...])
        vc, vp = vc_ref[...], vp_ref[...]
        sk = sk_ref[...]
        valid = _attn_valid(first)
        kv = lambda tp, tc, hk: jnp.concatenate([tp[:, hk * ATTN_HEAD_DIM:(hk + 1) * ATTN_HEAD_DIM],
                                                 tc[:, hk * ATTN_HEAD_DIM:(hk + 1) * ATTN_HEAD_DIM]], axis=0)
        kwins = [kv(kp, kc, hk) for hk in range(ATTN_KV_HEADS)]
        vwins_t = [kv(vp, vc, hk).T for hk in range(ATTN_KV_HEADS)]
        heads = [slice(h * ATTN_HEAD_DIM, (h + 1) * ATTN_HEAD_DIM) for h in range(ATTN_HEADS)]
        scores = [_dot(kwins[h // ATTN_GROUPS], q[:, hs], NT) for h, hs in enumerate(heads)]
        probs = [_attn_probs(st, _lane_scalar(sk, h), valid)[0] for h, st in enumerate(scores)]
        for h, (hs, pt) in enumerate(zip(heads, probs)):
            o_ref[:, hs] = _dot(vwins_t[h // ATTN_GROUPS], pt).T.astype(o_ref.dtype)

    return pl.pallas_call(
        body, name="attn_fwd", grid=(nb,),
        in_specs=_attn_specs(nb),
        out_specs=pl.BlockSpec((ATTN_BLOCK, ATTN_Q), lambda i: (i, 0)),
        out_shape=jax.ShapeDtypeStruct((t, ATTN_Q), MXU_DTYPE),
        compiler_params=_params("parallel"),
    )(pa, pa, pa, pa, pa, cos, sin, cos, sin, sinks_vec)


def _attn_bwd(pa, cos, sin, sinks_vec, dao):
    t = pa.shape[0]
    nb = t // ATTN_BLOCK

    def body(q_ref, kc_ref, kp_ref, vc_ref, vp_ref, cc_ref, sc_ref, cp_ref, sp_ref, sk_ref, do_ref,
             dq_ref, dk_ref, dv_ref, acc_ref, dqr_ref, dkw_ref, dvw_ref, ck_ref, cv_ref):
        i = pl.program_id(0)

        @pl.when(i == 0)
        def _():
            acc_ref[...] = jnp.zeros_like(acc_ref)
            ck_ref[...] = jnp.zeros_like(ck_ref)
            cv_ref[...] = jnp.zeros_like(cv_ref)

        @pl.when(i < nb)
        def _():
            first = i == 0
            cc, sc = cc_ref[...], sc_ref[...]
            cq, sq = jnp.tile(cc, (1, ATTN_Q // ATTN_KV)), jnp.tile(sc, (1, ATTN_Q // ATTN_KV))
            q = _rope(q_ref[...], cq, sq)
            kc = _rope(kc_ref[...], cc, sc)
            kp = _rope(kp_ref[...], cp_ref[...], sp_ref[...])
            vc, vp = vc_ref[...], vp_ref[...]
            sk = sk_ref[...]
            do = do_ref[...]
            lane = lax.broadcasted_iota(jnp.int32, (1, 128), 1)
            dsink = jnp.zeros((1, 128), F32)
            valid = _attn_valid(first)
            kv = lambda tp, tc, hk: jnp.concatenate([tp[:, hk * ATTN_HEAD_DIM:(hk + 1) * ATTN_HEAD_DIM],
                                                     tc[:, hk * ATTN_HEAD_DIM:(hk + 1) * ATTN_HEAD_DIM]], axis=0)
            kwins = [kv(kp, kc, hk) for hk in range(ATTN_KV_HEADS)]
            vwins = [kv(vp, vc, hk) for hk in range(ATTN_KV_HEADS)]
            kwins_t = [kw.T for kw in kwins]
            heads = [slice(h * ATTN_HEAD_DIM, (h + 1) * ATTN_HEAD_DIM) for h in range(ATTN_HEADS)]
            scores = [_dot(kwins[h // ATTN_GROUPS], q[:, hs], NT) for h, hs in enumerate(heads)]
            dps = [_dot(vwins[h // ATTN_GROUPS], do[:, hs], NT) for h, hs in enumerate(heads)]
            pts, dsts = [], []
            for h, (st, dp_t) in enumerate(zip(scores, dps)):
                probs_t, psink = _attn_probs(st, _lane_scalar(sk, h), valid)
                delta = jnp.sum(probs_t * dp_t, axis=0, keepdims=True)
                pts.append(probs_t)
                dsts.append(probs_t * (dp_t - delta) * ATTN_SCALE)
                dsink += jnp.where(lane == h, jnp.sum(-psink * delta, axis=1, keepdims=True), 0.0)
            for h, (hs, ds_t) in enumerate(zip(heads, dsts)):
                dqr_ref[:, hs] = _dot(kwins_t[h // ATTN_GROUPS], ds_t).T
            for hk in range(ATTN_KV_HEADS):
                ks = slice(hk * ATTN_HEAD_DIM, (hk + 1) * ATTN_HEAD_DIM)
                group = range(hk * ATTN_GROUPS, (hk + 1) * ATTN_GROUPS)
                ds_g = jnp.concatenate([dsts[h] for h in group], axis=1)
                p_g = jnp.concatenate([pts[h] for h in group], axis=1)
                q_g = jnp.concatenate([q[:, heads[h]] for h in group], axis=0)
                do_g = jnp.concatenate([do[:, heads[h]] for h in group], axis=0)
                dkw_ref[:, ks] = _dot(ds_g, q_g)
                dvw_ref[:, ks] = _dot(p_g, do_g)
            acc_ref[0:1, :] += dsink
            dq_ref[...] = _rope_bwd(dqr_ref[...], cq, sq).astype(dq_ref.dtype)
            dk_ref[...] = (ck_ref[...] + _rope_bwd(dkw_ref[0:ATTN_BLOCK, :], cp_ref[...], sp_ref[...])).astype(dk_ref.dtype)
            dv_ref[...] = (cv_ref[...] + dvw_ref[0:ATTN_BLOCK, :]).astype(dv_ref.dtype)
            ck_ref[...] = _rope_bwd(dkw_ref[ATTN_BLOCK:2 * ATTN_BLOCK, :], cc, sc)
            cv_ref[...] = dvw_ref[ATTN_BLOCK:2 * ATTN_BLOCK, :]

        @pl.when(i == nb)
        def _():
            dk_ref[...] = ck_ref[...].astype(dk_ref.dtype)
            dv_ref[...] = cv_ref[...].astype(dv_ref.dtype)

    prev_out = lambda w: pl.BlockSpec((ATTN_BLOCK, w), lambda i: (jnp.maximum(i - 1, 0), 0))
    return pl.pallas_call(
        body, name="attn_bwd", grid=(nb + 1,),
        in_specs=_attn_specs(nb) + [pl.BlockSpec((ATTN_BLOCK, ATTN_Q), lambda i: (jnp.minimum(i, nb - 1), 0))],
        out_specs=[pl.BlockSpec((ATTN_BLOCK, ATTN_Q), lambda i: (jnp.minimum(i, nb - 1), 0)), prev_out(ATTN_KV),
                   prev_out(ATTN_KV), _full((8, 128))],
        out_shape=[jax.ShapeDtypeStruct((t, ATTN_Q), MXU_DTYPE), jax.ShapeDtypeStruct((t, ATTN_KV), MXU_DTYPE),
                   jax.ShapeDtypeStruct((t, ATTN_KV), MXU_DTYPE), jax.ShapeDtypeStruct((8, 128), F32)],
        scratch_shapes=[pltpu.VMEM((ATTN_BLOCK, ATTN_Q), F32), pltpu.VMEM((2 * ATTN_BLOCK, ATTN_KV), F32),
                        pltpu.VMEM((2 * ATTN_BLOCK, ATTN_KV), F32), pltpu.VMEM((ATTN_BLOCK, ATTN_KV), F32),
                        pltpu.VMEM((ATTN_BLOCK, ATTN_KV), F32)],
        compiler_params=_params("arbitrary"),
    )(pa, pa, pa, pa, pa, cos, sin, cos, sin, sinks_vec, dao)


PAIR = 2 * DN_CHUNK
INTRA_PAIRS = 4
HALO = 8


def _conv_window(cur_ref, prev_ref, xs_ref, tm, has_prev):
    prev = jnp.where(has_prev, prev_ref[...], 0.0)
    xs_ref[0:HALO, :] = prev
    xs_ref[HALO:HALO + tm, :] = cur_ref[...]


def _conv_taps(xs_ref, cw_ref, tm):
    y = cw_ref[0:1, :] * xs_ref[pl.ds(HALO - DN_CONV + 1, tm), :]
    for j in range(1, DN_CONV):
        y += cw_ref[j:j + 1, :] * xs_ref[pl.ds(HALO - DN_CONV + 1 + j, tm), :]
    return y


def _gate_values(ba, al, dt):
    beta = _sigmoid(ba)
    pre = ba + dt
    g = -jnp.exp(al) * _softplus(pre)
    return beta, g, pre


def _dn_prep_specs(tm, tile):
    return [pl.BlockSpec((tm, CONV_CH), lambda i: (tile(i), 0)),
            pl.BlockSpec((HALO, CONV_CH), lambda i: (jnp.maximum(tile(i) * (tm // HALO) - 1, 0), 0)),
            pl.BlockSpec((tm, 128), lambda i: (tile(i), 4 * DN_W // 128)),
            _full((DN_CONV, CONV_CH)), _full((1, 128)), _full((1, 128))]


def _dn_prep(pd, conv_w, al_vec, dt_vec, tm):
    t = pd.shape[0]

    def body(cur_ref, prev_ref, ba_ref, cw_ref, al_ref, dt_ref, qn_ref, kn_ref, vc_ref, gc_ref, gr_ref, xs_ref):
        _conv_window(cur_ref, prev_ref, xs_ref, tm, pl.program_id(0) > 0)
        y = _conv_taps(xs_ref, cw_ref, tm)
        c = y * _sigmoid(y)
        for h in range(DN_HEADS):
            qs = slice(h * DN_HEAD_DIM, (h + 1) * DN_HEAD_DIM)
            ksl = slice(DN_W + h * DN_HEAD_DIM, DN_W + (h + 1) * DN_HEAD_DIM)
            qh, kh = c[:, qs], c[:, ksl]
            qn_ref[:, qs] = qh * lax.rsqrt(jnp.sum(qh * qh, axis=-1, keepdims=True) + EPS) * DN_SCALE
            kn_ref[:, qs] = kh * lax.rsqrt(jnp.sum(kh * kh, axis=-1, keepdims=True) + EPS)
        vc_ref[...] = c[:, 2 * DN_W:3 * DN_W]
        beta, g, _ = _gate_values(ba_ref[...], al_ref[...], dt_ref[...])
        lane = lax.broadcasted_iota(jnp.int32, beta.shape, 1)
        gb = jnp.where(lane < DN_HEADS, beta, jnp.where(lane < 2 * DN_HEADS, g, 0.0))
        gc_ref[...] = gb
        gr_ref[...] = gb.T[0:8, :]

    tok = lambda w: pl.BlockSpec((tm, w), lambda i: (i, 0))
    return pl.pallas_call(
        body, name="dn_prep", grid=(t // tm,),
        in_specs=_dn_prep_specs(tm, lambda i: i),
        out_specs=[tok(DN_W), tok(DN_W), tok(DN_W), tok(128), pl.BlockSpec((8, tm), lambda i: (0, i))],
        out_shape=[jax.ShapeDtypeStruct((t, DN_W), F32)] * 3 + [jax.ShapeDtypeStruct((t, 128), F32),
                                                                 jax.ShapeDtypeStruct((8, t), F32)],
        scratch_shapes=[pltpu.VMEM((HALO + tm, CONV_CH), F32)],
        compiler_params=_params("parallel"),
    )(pd, pd, pd, conv_w, al_vec, dt_vec)


def _pair_masks():
    r = lax.broadcasted_iota(jnp.int32, (PAIR, PAIR), 0)
    c = lax.broadcasted_iota(jnp.int32, (PAIR, PAIR), 1)
    same = (r < DN_CHUNK) == (c < DN_CHUNK)
    return same & (r >= c), same & (r > c)


def _lane_col(mat, idx):
    lane = lax.broadcasted_iota(jnp.int32, mat.shape, 1)
    return jnp.sum(jnp.where(lane == idx, mat, 0.0), axis=-1, keepdims=True)


def _pair_cumsums(gc, gr, low):
    lowf = low.astype(F32)
    return _dot(lowf, gc, NN, HI), _dot(gr, lowf, NT, HI)


def _pair_gates(gc, cum_c, cum_r, low, h):
    beta = _lane_col(gc, h)
    gam = _lane_col(cum_c, DN_HEADS + h)
    gam_row = cum_r[DN_HEADS + h:DN_HEADS + h + 1, :]
    dm = jnp.where(low, jnp.exp(jnp.where(low, gam - gam_row, 0.0)), 0.0)
    row = lax.broadcasted_iota(jnp.int32, gam.shape, 0)
    gl = jnp.where(row < DN_CHUNK, gam[DN_CHUNK - 1:DN_CHUNK, :], gam[PAIR - 1:PAIR, :])
    return beta, gam, dm, gl


def _split(a):
    hi = a.astype(BF16)
    return hi, (a - hi.astype(F32)).astype(BF16)


def _dot_split(a, b, dims=NN):
    (ah, al), (bh, bl) = a, b
    la, lb = (1, 1) if dims == TN else ((0, 1) if dims == NN else (0, 0))
    r = _dot(jnp.concatenate([ah, al], axis=la), jnp.concatenate([bh, bl], axis=lb), dims)
    m, n = r.shape[0] // 2, r.shape[1] // 2
    return (r[m:, n:] + (r[:m, n:] + r[m:, :n])) + r[:m, :n]


def _unit_lower_inverses(lmats):
    n = lmats[0].shape[0]
    r = lax.broadcasted_iota(jnp.int32, (n, n), 0)
    c = lax.broadcasted_iota(jnp.int32, (n, n), 1)
    same = lambda size: (r & ~(size - 1)) == (c & ~(size - 1))
    base = DN_CHUNK // 4
    diag = [jnp.where(same(base), l, 0.0) for l in lmats]
    accs = [(r == c).astype(F32) - d for d in diag]
    splits = [_split(d) for d in diag]
    step = 1
    while 2 * step < base:
        splits = [_split(_dot_split(s, s)) for s in splits]
        accs = [acc + _dot_split(_split(acc), s) for acc, s in zip(accs, splits)]
        step *= 2
    size = base
    while size < DN_CHUNK:
        below = same(2 * size) & jnp.logical_not(same(size))
        tb = [_dot(acc, jnp.where(below, l, 0.0)) for acc, l in zip(accs, lmats)]
        accs = [acc - _dot(t, acc) for acc, t in zip(accs, tb)]
        size *= 2
    return accs


def _dn_intra(qn, kn, vc, gc, gr):
    t = qn.shape[0]
    npair = t // PAIR
    rows_step = INTRA_PAIRS * PAIR

    def body(q_ref, k_ref, v_ref, gc_ref, gr_ref, u_ref, w_ref, qg_ref, kd_ref, a_ref, ti_ref, dl_ref):
        low, strict = _pair_masks()
        items = []
        for p in range(INTRA_PAIRS):
            rows = slice(p * PAIR, (p + 1) * PAIR)
            gc_v = gc_ref[rows, :]
            cum_c, cum_r = _pair_cumsums(gc_v, gr_ref[:, rows], low)
            for h in range(DN_HEADS):
                hs = slice(h * DN_HEAD_DIM, (h + 1) * DN_HEAD_DIM)
                items.append((p, h, rows, hs, _pair_gates(gc_v, cum_c, cum_r, low, h)))
        lmats = []
        for p, h, rows, hs, (beta, gam, dm, gl) in items:
            k = k_ref[rows, hs]
            lmats.append(jnp.where(strict, _dot(k * beta, k, NT) * dm, 0.0))
        tinvs = _unit_lower_inverses(lmats)
        for (p, h, rows, hs, (beta, gam, dm, gl)), tinv in zip(items, tinvs):
            q, k, v = q_ref[rows, hs], k_ref[rows, hs], v_ref[rows, hs]
            eg = jnp.exp(gam)
            u_ref[rows, hs] = _dot(tinv, v * beta)
            w_ref[rows, hs] = _dot(tinv, (k * beta) * eg)
            a_ref[h, rows, :] = _dot(q, k, NT) * dm
            ti_ref[h, rows, :] = tinv
            qg_ref[rows, hs] = q * eg
            kd_ref[rows, hs] = k * jnp.exp(gl - gam)
            for c in range(2):
                last = (c + 1) * DN_CHUNK - 1
                dl_ref[2 * p + c, h] = jnp.broadcast_to(jnp.exp(gam[last:last + 1, :]), (8, 128))

    tok = lambda w: pl.BlockSpec((rows_step, w), lambda n: (n, 0))
    hm = pl.BlockSpec((DN_HEADS, rows_step, PAIR), lambda n: (0, n, 0))
    return pl.pallas_call(
        body, name="dn_intra", grid=(npair // INTRA_PAIRS,),
        in_specs=[tok(DN_W), tok(DN_W), tok(DN_W), tok(128), pl.BlockSpec((8, rows_step), lambda n: (0, n))],
        out_specs=[tok(DN_W)] * 4 + [hm, hm, pl.BlockSpec((2 * INTRA_PAIRS, DN_HEADS, 8, 128), lambda n: (n, 0, 0, 0))],
        out_shape=[jax.ShapeDtypeStruct((t, DN_W), F32)] * 4 + [jax.ShapeDtypeStruct((DN_HEADS, t, PAIR), F32)] * 2
                  + [jax.ShapeDtypeStruct((2 * npair, DN_HEADS, 8, 128), F32)],
        compiler_params=_params("parallel"),
    )(qn, kn, vc, gc, gr)


def _dn_scan_fwd(u, w, qg, kd, a_qk, dlast, pd, dn_w):
    t = u.shape[0]
    npair = t // PAIR

    def body(u_ref, w_ref, qg_ref, kd_ref, a_ref, dl_ref, z_ref, nw_ref, out_ref, o_ref, vn_ref, sall_ref, s_ref):
        @pl.when(pl.program_id(0) == 0)
        def _():
            s_ref[...] = jnp.zeros_like(s_ref)

        nw = nw_ref[...]
        for c in range(2):
            rows = slice(c * DN_CHUNK, (c + 1) * DN_CHUNK)
            for h in range(DN_HEADS):
                hs = slice(h * DN_HEAD_DIM, (h + 1) * DN_HEAD_DIM)
                st = s_ref[h]
                sall_ref[c, h] = st
                vn_ref[rows, hs] = u_ref[rows, hs] - _dot(w_ref[rows, hs], st)
            for h in range(DN_HEADS):
                hs = slice(h * DN_HEAD_DIM, (h + 1) * DN_HEAD_DIM)
                st, vn = s_ref[h], vn_ref[rows, hs]
                o = _dot(qg_ref[rows, hs], st) + _dot(a_ref[h, rows, rows], vn)
                s_ref[h] = st * dl_ref[c, h][0:1, :] + _dot(kd_ref[rows, hs], vn, TN)
                o_ref[rows, hs] = o
                z = z_ref[rows, hs]
                on = o * lax.rsqrt(jnp.mean(o * o, axis=-1, keepdims=True) + EPS) * nw
                out_ref[rows, hs] = (on * (z * _sigmoid(z))).astype(out_ref.dtype)

    tok = pl.BlockSpec((PAIR, DN_W), lambda n: (n, 0))
    hm = pl.BlockSpec((DN_HEADS, PAIR, PAIR), lambda n: (0, n, 0))
    return pl.pallas_call(
        body, name="dn_scan_fwd", grid=(npair,),
        in_specs=[tok, tok, tok, tok, hm, pl.BlockSpec((2, DN_HEADS, 8, 128), lambda n: (n, 0, 0, 0)),
                  pl.BlockSpec((PAIR, DN_W), lambda n: (n, 3)), _full((1, 128))],
        out_specs=[tok, tok, tok, pl.BlockSpec((2, DN_HEADS, DN_HEAD_DIM, DN_HEAD_DIM), lambda n: (n, 0, 0, 0))],
        out_shape=[jax.ShapeDtypeStruct((t, DN_W), MXU_DTYPE)] + [jax.ShapeDtypeStruct((t, DN_W), F32)] * 2
                  + [jax.ShapeDtypeStruct((2 * npair, DN_HEADS, DN_HEAD_DIM, DN_HEAD_DIM), F32)],
        scratch_shapes=[pltpu.VMEM((DN_HEADS, DN_HEAD_DIM, DN_HEAD_DIM), F32)],
        compiler_params=_params("arbitrary"),
    )(u, w, qg, kd, a_qk, dlast, pd, dn_w)


def _dn_scan_bwd(dout, o, vnew, sall, w, qg, kd, a_qk, dlast, pd, dn_w):
    t = o.shape[0]
    npair = t // PAIR
    rev = lambda n: npair - 1 - n

    def body(do_ref, o_ref, vn_ref, sall_ref, w_ref, qg_ref, kd_ref, a_ref, dl_ref, z_ref, nw_ref,
             dz_ref, du_ref, dw_ref, dqg_ref, dkd_ref, da_ref, ddl_ref, acc_ref, ds_ref, dos_ref):
        @pl.when(pl.program_id(0) == 0)
        def _():
            ds_ref[...] = jnp.zeros_like(ds_ref)
            acc_ref[...] = jnp.zeros_like(acc_ref)

        nw = nw_ref[...]
        dnw = jnp.zeros((1, 128), F32)
        for h in range(DN_HEADS):
            hs = slice(h * DN_HEAD_DIM, (h + 1) * DN_HEAD_DIM)
            o, z, dout = o_ref[:, hs], z_ref[:, hs], do_ref[:, hs]
            r = lax.rsqrt(jnp.mean(o * o, axis=-1, keepdims=True) + EPS)
            oh = o * r
            sz = _sigmoid(z)
            dz_ref[:, hs] = dout * (oh * nw) * (sz + z * sz * (1.0 - sz))
            don = dout * (z * sz)
            dnw += jnp.sum(don * oh, axis=0, keepdims=True)
            doh = don * nw
            dos_ref[:, hs] = r * (doh - oh * jnp.mean(doh * oh, axis=-1, keepdims=True))
        acc_ref[0:1, :] += dnw
        for c in (1, 0):
            rows = slice(c * DN_CHUNK, (c + 1) * DN_CHUNK)
            other = slice((1 - c) * DN_CHUNK, (2 - c) * DN_CHUNK)
            for h in range(DN_HEADS):
                hs = slice(h * DN_HEAD_DIM, (h + 1) * DN_HEAD_DIM)
                do, st, dsp, vn = dos_ref[rows, hs], sall_ref[c, h], ds_ref[h], vn_ref[rows, hs]
                da_ref[h, rows, rows] = _dot(do, vn, NT)
                da_ref[h, rows, other] = jnp.zeros((DN_CHUNK, DN_CHUNK), F32)
                du_ref[rows, hs] = _dot(a_ref[h, rows, rows], do, TN) + _dot(kd_ref[rows, hs], dsp)
                dqg_ref[rows, hs] = _dot(do, st, NT)
                dkd_ref[rows, hs] = _dot(vn, dsp, NT)
                ddl = jnp.sum(jnp.sum(dsp * st, axis=1, keepdims=True), axis=0, keepdims=True)
                ddl_ref[c, h] = jnp.broadcast_to(ddl, (8, 128))
            for h in range(DN_HEADS):
                hs = slice(h * DN_HEAD_DIM, (h + 1) * DN_HEAD_DIM)
                do, st, dvn = dos_ref[rows, hs], sall_ref[c, h], du_ref[rows, hs]
                dw_ref[rows, hs] = -_dot(dvn, st, NT)
                ds_ref[h] = (ds_ref[h] * dl_ref[c, h][0:1, :] + _dot(qg_ref[rows, hs], do, TN)
                             - _dot(w_ref[rows, hs], dvn, TN))

    tok = pl.BlockSpec((PAIR, DN_W), lambda n: (rev(n), 0))
    hm = pl.BlockSpec((DN_HEADS, PAIR, PAIR), lambda n: (0, rev(n), 0))
    sc = pl.BlockSpec((2, DN_HEADS, 8, 128), lambda n: (rev(n), 0, 0, 0))
    return pl.pallas_call(
        body, name="dn_scan_bwd", grid=(npair,),
        in_specs=[tok, tok, tok, pl.BlockSpec((2, DN_HEADS, DN_HEAD_DIM, DN_HEAD_DIM), lambda n: (rev(n), 0, 0, 0)),
                  tok, tok, tok, hm, sc, pl.BlockSpec((PAIR, DN_W), lambda n: (rev(n), 3)), _full((1, 128))],
        out_specs=[tok] * 5 + [hm, sc, _full((8, 128))],
        out_shape=[jax.ShapeDtypeStruct((t, DN_W), F32)] * 5 + [jax.ShapeDtypeStruct((DN_HEADS, t, PAIR), F32),
                   jax.ShapeDtypeStruct((2 * npair, DN_HEADS, 8, 128), F32), jax.ShapeDtypeStruct((8, 128), F32)],
        scratch_shapes=[pltpu.VMEM((DN_HEADS, DN_HEAD_DIM, DN_HEAD_DIM), F32), pltpu.VMEM((PAIR, DN_W), F32)],
        compiler_params=_params("arbitrary"),
    )(dout, o, vnew, sall, w, qg, kd, a_qk, dlast, pd, dn_w)


def _dn_intra_bwd(qn, kn, vc, gc, gr, tinv, a_qk, du, dw, dqg, dkd, da_qk, ddlast, dlast, dep):
    t = qn.shape[0]
    npair = t // PAIR

    def body(q_ref, k_ref, v_ref, gc_ref, gr_ref, ti_ref, a_ref, du_ref, dw_ref, dqg_ref, dkd_ref, da_ref, ddl_ref, dl_ref,
             dep_ref, dq_ref, dk_ref, dv_ref, dg_ref):
        low, strict = _pair_masks()
        lane = lax.broadcasted_iota(jnp.int32, (PAIR, 128), 1)
        rowi = lax.broadcasted_iota(jnp.int32, (PAIR, 1), 0)
        rsum = lambda v: jnp.sum(v, axis=-1, keepdims=True)
        items = []
        for p in range(INTRA_PAIRS):
            rows = slice(p * PAIR, (p + 1) * PAIR)
            gc_v = gc_ref[rows, :]
            cum_c, cum_r = _pair_cumsums(gc_v, gr_ref[:, rows], low)
            for h in range(DN_HEADS):
                hs = slice(h * DN_HEAD_DIM, (h + 1) * DN_HEAD_DIM)
                items.append((p, h, rows, hs, _pair_gates(gc_v, cum_c, cum_r, low, h)))
        dtis, lmats, dvbs, dkbgs = [], [], [], []
        for p, h, rows, hs, (beta, gam, dm, gl) in items:
            k, tinv = k_ref[rows, hs], ti_ref[h, rows, :]
            kb = k * beta
            dtis.append(_dot(du_ref[rows, hs], v_ref[rows, hs] * beta, NT)
                        + _dot(dw_ref[rows, hs], kb * jnp.exp(gam), NT))
            lmats.append(jnp.where(strict, _dot(kb, k, NT) * dm, 0.0))
            dvbs.append(_dot(tinv, du_ref[rows, hs], TN))
            dkbgs.append(_dot(tinv, dw_ref[rows, hs], TN))
        xs = [_dot(ti_ref[h, rows, :], dti, TN) for (p, h, rows, hs, g), dti in zip(items, dtis)]
        dls = [jnp.where(strict, -_dot(x, ti_ref[h, rows, :], NT), 0.0) for (p, h, rows, hs, g), x in zip(items, xs)]
        dgam_all = [jnp.zeros((PAIR, 128), F32) for _ in range(INTRA_PAIRS)]
        dbeta_all = [jnp.zeros((PAIR, 128), F32) for _ in range(INTRA_PAIRS)]
        for (p, h, rows, hs, (beta, gam, dm, gl)), dl, lmat, dvb, dkbg in zip(items, dls, lmats, dvbs, dkbgs):
            q, k, v = q_ref[rows, hs], k_ref[rows, hs], v_ref[rows, hs]
            a = a_ref[h, rows, :]
            dqg, dkd = dqg_ref[rows, hs], dkd_ref[rows, hs]
            kb = k * beta
            eg = jnp.exp(gam)
            ekd = jnp.exp(gl - gam)
            dmm = dl * dm
            dam = jnp.where(low, da_ref[h, rows, :], 0.0)
            dn = dam * dm
            e = dl * lmat + dam * a
            dkb = _dot(dmm, k) + dkbg * eg
            dk_ref[rows, hs] = _dot(dmm, kb, TN) + _dot(dn, q, TN) + dkd * ekd + dkb * beta
            dq_ref[rows, hs] = _dot(dn, k) + dqg * eg
            dv_ref[rows, hs] = dvb * beta
            t_kd = rsum(dkd * (k * ekd))
            dgam = rsum(e) - rsum(e.T) + rsum(dqg * (q * eg)) + rsum(dkbg * (kb * eg)) - t_kd
            for c in range(2):
                crows = slice(c * DN_CHUNK, (c + 1) * DN_CHUNK)
                dgl = (jnp.sum(t_kd[crows, :], axis=0, keepdims=True)
                       + ddl_ref[2 * p + c, h][0:1, 0:1] * dl_ref[2 * p + c, h][0:1, 0:1])
                dgam = dgam + jnp.where(rowi == (c + 1) * DN_CHUNK - 1, dgl, 0.0)
            dgam_all[p] += jnp.where(lane == DN_HEADS + h, dgam, 0.0)
            dbeta_all[p] += jnp.where(lane == h, rsum(dkb * k) + rsum(dvb * v), 0.0)
        for p in range(INTRA_PAIRS):
            dg_ref[p * PAIR:(p + 1) * PAIR, :] = dbeta_all[p] + _dot(low.astype(F32), dgam_all[p], TN, HI)

    rows_step = INTRA_PAIRS * PAIR
    tok = lambda w: pl.BlockSpec((rows_step, w), lambda n: (n, 0))
    hm = pl.BlockSpec((DN_HEADS, rows_step, PAIR), lambda n: (0, n, 0))
    sc = pl.BlockSpec((2 * INTRA_PAIRS, DN_HEADS, 8, 128), lambda n: (n, 0, 0, 0))
    return pl.pallas_call(
        body, name="dn_intra_bwd", grid=(npair // INTRA_PAIRS,),
        in_specs=[tok(DN_W), tok(DN_W), tok(DN_W), tok(128), pl.BlockSpec((8, rows_step), lambda n: (0, n)), hm, hm,
                  tok(DN_W), tok(DN_W), tok(DN_W), tok(DN_W), hm, sc, sc, pl.BlockSpec(memory_space=pl.ANY)],
        out_specs=[tok(DN_W), tok(DN_W), tok(DN_W), tok(128)],
        out_shape=[jax.ShapeDtypeStruct((t, DN_W), F32)] * 3 + [jax.ShapeDtypeStruct((t, 128), F32)],
        compiler_params=_params("parallel"),
    )(qn, kn, vc, gc, gr, tinv, a_qk, du, dw, dqg, dkd, da_qk, ddlast, dlast, dep)


def _dn_prep_bwd(pd, conv_w, al_vec, dt_vec, dqn, dkn, dvc, dgc, dz, tm):
    t = pd.shape[0]
    nt = t // tm
    tile = lambda i: nt - 1 - i

    def body(cur_ref, prev_ref, ba_ref, cw_ref, al_ref, dt_ref, dq_ref, dk_ref, dv_ref, dg_ref, dz_ref,
             o_ref, accw_ref, accg_ref, xs_ref, dc_ref, ds_ref, carry_ref):
        @pl.when(pl.program_id(0) == 0)
        def _():
            accw_ref[...] = jnp.zeros_like(accw_ref)
            accg_ref[...] = jnp.zeros_like(accg_ref)
            carry_ref[...] = jnp.zeros_like(carry_ref)

        _conv_window(cur_ref, prev_ref, xs_ref, tm, tile(pl.program_id(0)) > 0)
        y = _conv_taps(xs_ref, cw_ref, tm)
        sg = _sigmoid(y)
        c = y * sg
        for h in range(DN_HEADS):
            qs = slice(h * DN_HEAD_DIM, (h + 1) * DN_HEAD_DIM)
            ksl = slice(DN_W + h * DN_HEAD_DIM, DN_W + (h + 1) * DN_HEAD_DIM)
            for src, sl, scale in ((dq_ref, qs, DN_SCALE), (dk_ref, ksl, 1.0)):
                xh = c[:, sl]
                r = lax.rsqrt(jnp.sum(xh * xh, axis=-1, keepdims=True) + EPS)
                unit = xh * r
                dn = src[:, qs] * scale
                dc_ref[:, sl] = r * (dn - unit * jnp.sum(dn * unit, axis=-1, keepdims=True))
        dc_ref[:, 2 * DN_W:3 * DN_W] = dv_ref[...]
        dy = dc_ref[...] * (sg + y * sg * (1.0 - sg))
        for j in range(DN_CONV):
            accw_ref[j:j + 1, :] += jnp.sum(dy * xs_ref[pl.ds(HALO - DN_CONV + 1 + j, tm), :], axis=0, keepdims=True)
        ds_ref[0:tm, :] = dy
        ds_ref[tm:tm + HALO, :] = carry_ref[...]
        carry_ref[...] = ds_ref[0:HALO, :]
        dx = cw_ref[0:1, :] * ds_ref[pl.ds(DN_CONV - 1, tm), :]
        for j in range(1, DN_CONV):
            dx += cw_ref[j:j + 1, :] * ds_ref[pl.ds(DN_CONV - 1 - j, tm), :]

        beta, g, pre = _gate_values(ba_ref[...], al_ref[...], dt_ref[...])
        dgb = dg_ref[...]
        lane = lax.broadcasted_iota(jnp.int32, dgb.shape, 1)
        is_b, is_a = lane < DN_HEADS, (lane >= DN_HEADS) & (lane < 2 * DN_HEADS)
        dpre = dgb * (-jnp.exp(al_ref[...])) * _sigmoid(pre)
        dba = jnp.where(is_b, dgb * beta * (1.0 - beta), jnp.where(is_a, dpre, 0.0))
        accg_ref[0:1, :] += jnp.sum(jnp.where(is_a, dgb * g, 0.0), axis=0, keepdims=True)
        accg_ref[1:2, :] += jnp.sum(jnp.where(is_a, dpre, 0.0), axis=0, keepdims=True)
        o_ref[:, 0:CONV_CH] = dx.astype(o_ref.dtype)
        o_ref[:, CONV_CH:CONV_CH + DN_W] = dz_ref[...].astype(o_ref.dtype)
        o_ref[:, CONV_CH + DN_W:DN_COLS] = dba.astype(o_ref.dtype)

    tok = lambda w: pl.BlockSpec((tm, w), lambda i: (tile(i), 0))
    return pl.pallas_call(
        body, name="dn_prep_bwd", grid=(nt,),
        in_specs=_dn_prep_specs(tm, tile) + [tok(DN_W), tok(DN_W), tok(DN_W), tok(128), tok(DN_W)],
        out_specs=[tok(DN_COLS), _full((8, CONV_CH)), _full((8, 128))],
        out_shape=[jax.ShapeDtypeStruct((t, DN_COLS), MXU_DTYPE),
                   jax.ShapeDtypeStruct((8, CONV_CH), F32), jax.ShapeDtypeStruct((8, 128), F32)],
        scratch_shapes=[pltpu.VMEM((HALO + tm, CONV_CH), F32), pltpu.VMEM((tm, CONV_CH), F32),
                        pltpu.VMEM((tm + HALO, CONV_CH), F32), pltpu.VMEM((HALO, CONV_CH), F32)],
        compiler_params=_params("arbitrary"),
    )(pd, pd, pd, conv_w, al_vec, dt_vec, dqn, dkn, dvc, dgc, dz)


def _pad_lanes(v, offset=0):
    return jnp.zeros((1, 128), F32).at[0, offset:offset + v.shape[0]].set(v.astype(F32))


class _LocalReducer:
    def start(self, grads):
        return jnp.zeros((8, 128), F32)

    def middle(self, after):
        return jnp.zeros((8, 128), F32)

    def finish(self, after):
        return None


def _local_step(x, p, tgt, sm, w, late, reducer):
    t = x.shape[0]
    tm = min(512, t // 2)
    tm_s = min(256, t // 2)
    tw = min(1024, t // 2)

    w_in = w["w_in"]
    wa = w_in[:, :ATTN_Q + 2 * ATTN_KV]
    wd = jnp.pad(w_in[:, ATTN_Q + 2 * ATTN_KV:], ((0, 0), (0, DN_COLS - (D_IN - ATTN_Q - 2 * ATTN_KV))))
    conv_w = w["conv_w"]
    al_vec, dt_vec = _pad_lanes(sm["a_log"], DN_HEADS), _pad_lanes(sm["dt_bias"], DN_HEADS)
    sinks_vec = _pad_lanes(sm["sinks"])
    dn_w = sm["dn_norm"].reshape(1, 128)
    row = lambda v: v.reshape(1, D_MODEL)
    cos, sin = _rope_tables(t)

    u, pa, pd = _inproj(x, row(sm["norm_mix"]), wa, wd, tm_s)
    ao = _attn_fwd(pa, cos, sin, sinks_vec)
    qn, kn, vc, gc, gr = _dn_prep(pd, conv_w, al_vec, dt_vec, tm_s)
    uu, ww, qg, kd, a_qk, tinv, dlast = _dn_intra(qn, kn, vc, gc, gr)
    dn_out, o, vnew, sall = _dn_scan_fwd(uu, ww, qg, kd, a_qk, dlast, pd, dn_w)
    w_o, late_rest = late(dn_out)
    wo_a, wo_d = w_o[:ATTN_Q], w_o[ATTN_Q:]
    h1 = _oproj(x, ao, dn_out, wo_a, wo_d, tm)
    w = dict(w, **late_rest(h1))
    w_proj = jnp.transpose(w["w_proj4"], (1, 0, 2)).reshape(PLE_DIM, D_MODEL)
    m, r, h2 = _mlp_fwd(h1, row(sm["norm_mlp"]), w["w_up4"], w["w_down"], tw)
    dh2, dh2b, dgp, dpp, n3, pb, acc_ple = _ple_loss(h2, p, tgt, row(sm["norm_ple"]), row(sm["norm_final"]),
                                                     w["w_gate"], w_proj, tm_s)
    g_w_gate = _wgrad(n3, dgp, "wgrad_gate", D_MODEL, D_MODEL, tw)
    g_w_proj = _wgrad(pb, dpp, "wgrad_proj", PLE_DIM, D_MODEL, tw)
    da, dh1, dh1b, acc_mlp = _mlp_bwd(dh2, dh2b, r, h1, row(sm["norm_mlp"]), w["w_up4"], w["w_down"], tm)
    g_w_up4 = _wgrad(m, da, "wgrad_up", D_MODEL, FF_BLOCK, tw, stacked=True)
    g_w_down = _wgrad(r, dh2b, "wgrad_down", FF_BLOCK, D_MODEL, tw,
                      prep=lambda rv: jnp.square(rv.astype(F32)).astype(MXU_DTYPE))
    g_w_o = _wgrad_cat([ao, dn_out], [dh1b], "wgrad_o", tw)
    early = dict(w_up4=g_w_up4, w_down=g_w_down, w_gate=g_w_gate, w_proj=g_w_proj, w_o=g_w_o)
    dep = reducer.start(early)
    dao, ddn = _oproj_bwd(dh1b, wo_a, wo_d, tm, dep)
    dz, du, dw, dqg, dkd, da_qk, ddlast, acc_dn = _dn_scan_bwd(ddn, o, vnew, sall, ww, qg, kd, a_qk, dlast, pd, dn_w)
    dep = reducer.middle(du)
    dqn, dkn, dvc, dgc = _dn_intra_bwd(qn, kn, vc, gc, gr, tinv, a_qk, du, dw, dqg, dkd, da_qk, ddlast, dlast, dep)
    d_dn, acc_conv, acc_gate = _dn_prep_bwd(pd, conv_w, al_vec, dt_vec, dqn, dkn, dvc, dgc, dz, tm_s)
    dq, dk, dv, acc_attn = _attn_bwd(pa, cos, sin, sinks_vec, dao)
    reducer.finish(dq)
    wq, wk, wv = wa[:, :ATTN_Q], wa[:, ATTN_Q:ATTN_Q + ATTN_KV], wa[:, ATTN_Q + ATTN_KV:]
    dx, acc_mix = _inproj_bwd(x, dh1, row(sm["norm_mix"]), [dq, dk, dv, d_dn], [wq, wk, wv, wd], tm_s)

    g_w_in = _wgrad_cat([u], [dq, dk, dv, d_dn], "wgrad_in", tw)[:, :D_IN]
    grads = dict(early, w_in=g_w_in)
    sums = dict(loss=acc_ple[2, 0], norm_final=acc_ple[0], norm_ple=acc_ple[1], norm_mlp=acc_mlp[0], norm_mix=acc_mix[0],
                dn_norm=acc_dn[0], sinks=acc_attn[0, :ATTN_HEADS], a_log=acc_gate[0, DN_HEADS:2 * DN_HEADS],
                dt_bias=acc_gate[1, DN_HEADS:2 * DN_HEADS], conv_w=acc_conv[:DN_CONV])
    return sums, dx, grads


MESH = pl.DeviceIdType.MESH
ANY = pl.BlockSpec(memory_space=pl.ANY)
N_CHIPS = 4
N_DEV = 8


def _place():
    x, y, c = lax.axis_index("x"), lax.axis_index("y"), lax.axis_index("c")
    chips = [(1 - x, y), (x, 1 - y), (1 - x, 1 - y)]
    return x, y, c, chips


def _gather_weights(shards, conv_s):
    n = len(shards)
    per = 7

    def body(*refs):
        in_refs, conv_ref = refs[:n], refs[n]
        out_refs, conv_out = refs[n + 1:2 * n + 1], refs[2 * n + 1]
        send_sems, recv_sems = refs[2 * n + 2:]
        x, y, c, chips = _place()
        sibling = (x, y, 1 - c)

        def blk(a, px, py, pc):
            hr = in_refs[a].shape[0] // 2
            return out_refs[a].at[2 * px + py, pl.ds(pc * hr, hr), :]

        def mine(a):
            hr = in_refs[a].shape[0] // 2
            return in_refs[a].at[pl.ds(c * hr, hr), :]

        def rcopy(a, k, block, to, src=None):
            return pltpu.make_async_remote_copy(
                src_ref=blk(a, *block) if src is None else src, dst_ref=blk(a, *block),
                send_sem=send_sems.at[per * a + k], recv_sem=recv_sems.at[per * a + k],
                device_id=to, device_id_type=MESH)

        def whole(a, to):
            return pltpu.make_async_remote_copy(
                src_ref=in_refs[a], dst_ref=out_refs[a].at[2 * x + y],
                send_sem=send_sems.at[per * a], recv_sem=recv_sems.at[per * a], device_id=to, device_id_type=MESH)

        def ccopy(j, to):
            return pltpu.make_async_remote_copy(
                src_ref=conv_ref, dst_ref=conv_out.at[2 * x + y],
                send_sem=send_sems.at[per * n + j], recv_sem=recv_sems.at[per * n + j],
                device_id=to, device_id_type=MESH)

        started = []
        for a in range(n):
            first = [whole(a, sibling)]
            first += [rcopy(a, 1 + j, (x, y, c), (*chip, c), src=mine(a)) for j, chip in enumerate(chips)]
            for cp in first:
                cp.start()
            started += first
        conv_sends = [ccopy(j, (*chip, c)) for j, chip in enumerate(chips)] + [ccopy(3, sibling)]
        for cp in conv_sends:
            cp.start()
        started += conv_sends
        for a in range(n):
            for j, chip in enumerate(chips):
                rcopy(a, 1 + j, (*chip, c), (x, y, c)).wait_recv()
                fwd = rcopy(a, 4 + j, (*chip, c), sibling)
                fwd.start()
                started.append(fwd)
        for a in range(n):
            whole(a, sibling).wait_recv()
            for j, chip in enumerate(chips):
                rcopy(a, 4 + j, (*chip, 1 - c), (x, y, c)).wait_recv()
        for j, chip in enumerate(chips + [(x, y)]):
            pltpu.make_async_remote_copy(
                src_ref=conv_ref, dst_ref=conv_out.at[2 * chip[0] + chip[1]],
                send_sem=send_sems.at[per * n + j], recv_sem=recv_sems.at[per * n + j],
                device_id=sibling, device_id_type=MESH).wait_recv()
        for cp in started:
            cp.wait_send()

    nsem = per * n + 4
    out_shape = [jax.ShapeDtypeStruct((N_CHIPS,) + s.shape, s.dtype) for s in shards]
    out_shape.append(jax.ShapeDtypeStruct((N_CHIPS,) + conv_s.shape, conv_s.dtype))
    return pl.pallas_call(
        body, name="gather_weights", in_specs=[ANY] * (n + 1), out_specs=[ANY] * (n + 1), out_shape=out_shape,
        scratch_shapes=[pltpu.SemaphoreType.DMA((nsem,)), pltpu.SemaphoreType.DMA((nsem,))],
    )(*shards, conv_s)


HBM = pl.BlockSpec(memory_space=pltpu.HBM)
SEM = pl.BlockSpec(memory_space=pltpu.SEMAPHORE)
EFFECT = pltpu.SideEffectType.DATAFLOW_SIDE_EFFECTING
LATE_COPIES = 7


def _late_copies(in_refs, land_refs, send_sems, recv_sems, only=None):
    x, y, c, chips = _place()
    sends, arrivals = [], []
    for a, (src, land) in enumerate(zip(in_refs, land_refs)):
        if only is not None and a not in only:
            continue
        hr = src.shape[0] // 2
        base = LATE_COPIES * a

        def cp(src_ref, dst_ref, s_idx, r_idx, to):
            return pltpu.make_async_remote_copy(src_ref=src_ref, dst_ref=dst_ref, send_sem=send_sems.at[base + s_idx],
                                                recv_sem=recv_sems.at[base + r_idx], device_id=to, device_id_type=MESH)

        sends.append(cp(src, land.at[2 * x + y], 0, 0, (x, y, 1 - c)))
        arrivals.append(cp(src, land.at[2 * x + y], 0, 0, (x, y, 1 - c)))
        for j, chip in enumerate(chips):
            for pc in range(2):
                half = src.at[pl.ds(c * hr, hr), :]
                sends.append(cp(half, land.at[2 * x + y, pl.ds(c * hr, hr), :], 1 + 2 * j + pc, 1 + 2 * j + c, (*chip, pc)))
                arrivals.append(cp(half, land.at[2 * chip[0] + chip[1], pl.ds(pc * hr, hr), :], 1 + 2 * j + pc,
                                   1 + 2 * j + pc, (*chip, pc)))
    return sends, arrivals


def _copies_start(name, build, nsem, srcs, land_shapes, after):
    n = len(srcs)

    def body(*refs):
        sends, _ = build(refs[:n], refs[n:2 * n], refs[2 * n + 1], refs[2 * n + 2])
        for cp in sends:
            cp.start()
        refs[-1][...] = jnp.zeros_like(refs[-1])

    lands = [pltpu.with_memory_space_constraint(lax.empty(s.shape, s.dtype), pltpu.HBM) for s in land_shapes]
    ins = [pltpu.with_memory_space_constraint(s, pltpu.HBM) for s in srcs]
    out = pl.pallas_call(
        body, name=name,
        out_shape=(pltpu.SemaphoreType.DMA((nsem,)), pltpu.SemaphoreType.DMA((nsem,)),
                   *[pltpu.HBM(s.shape, s.dtype) for s in srcs], *[pltpu.HBM(s.shape, s.dtype) for s in land_shapes],
                   jax.ShapeDtypeStruct((8, 128), F32)),
        in_specs=[HBM] * (2 * n) + [ANY],
        out_specs=(SEM, SEM, *[HBM] * (2 * n), pl.BlockSpec(memory_space=pltpu.VMEM)),
        input_output_aliases={i: 2 + i for i in range(2 * n)},
        compiler_params=pltpu.CompilerParams(has_side_effects=EFFECT),
    )(*ins, *lands, after)
    return out[0], out[1], out[2:2 + n], out[2 + n:2 + 2 * n], out[-1]


def _copies_wait(name, build, started, after):
    send_sems, recv_sems, srcs, lands, _ = started
    n = len(srcs)

    def body(*refs):
        sends, arrivals = build(refs[:n], refs[n:2 * n], refs[2 * n], refs[2 * n + 1])
        for cp in sends:
            cp.wait_send()
        for cp in arrivals:
            cp.wait_recv()

    out = pl.pallas_call(
        body, name=name,
        out_shape=(*[pltpu.HBM(s.shape, s.dtype) for s in srcs], *[pltpu.HBM(l.shape, l.dtype) for l in lands]),
        in_specs=[HBM] * (2 * n) + [SEM, SEM, ANY],
        out_specs=tuple([HBM] * (2 * n)),
        input_output_aliases={i: i for i in range(2 * n)},
        compiler_params=pltpu.CompilerParams(has_side_effects=EFFECT),
    )(*srcs, *lands, send_sems, recv_sems, after)
    return out[:n], out[n:]


def _exchange_copies(g_refs, got_refs, send_sems, recv_sems):
    x, y, c, _ = _place()
    sends, arrivals = [], []
    for a, (g, got) in enumerate(zip(g_refs, got_refs)):
        hr = g.shape[1] // 2
        cp = pltpu.make_async_remote_copy(
            src_ref=g.at[:, pl.ds((1 - c) * hr, hr), :], dst_ref=got, send_sem=send_sems.at[a],
            recv_sem=recv_sems.at[a], device_id=(x, y, 1 - c), device_id_type=MESH)
        sends.append(cp)
        arrivals.append(cp)
    return sends, arrivals


def _scatter_copies(s_refs, got_refs, send_sems, recv_sems):
    x, y, c, chips = _place()
    sends, arrivals = [], []
    for a, (s16, got) in enumerate(zip(s_refs, got_refs)):
        for j, chip in enumerate(chips):
            cp = pltpu.make_async_remote_copy(
                src_ref=s16.at[2 * chip[0] + chip[1]], dst_ref=got.at[j], send_sem=send_sems.at[3 * a + j],
                recv_sem=recv_sems.at[3 * a + j], device_id=(*chip, c), device_id_type=MESH)
            sends.append(cp)
            arrivals.append(cp)
    return sends, arrivals


def _share_halves(name, bufs, dep):
    n = len(bufs)

    def body(*refs):
        out_refs = refs[n + 1:2 * n + 1]
        send_sems, recv_sems = refs[2 * n + 1:]
        x, y, c, _ = _place()
        remote = [pltpu.make_async_remote_copy(
            src_ref=out_refs[a].at[c], dst_ref=out_refs[a].at[c], send_sem=send_sems.at[a], recv_sem=recv_sems.at[a],
            device_id=(x, y, 1 - c), device_id_type=MESH) for a in range(n)]
        for cp in remote:
            cp.start()
        for a in range(n):
            pltpu.make_async_remote_copy(
                src_ref=out_refs[a].at[c], dst_ref=out_refs[a].at[1 - c], send_sem=send_sems.at[a],
                recv_sem=recv_sems.at[a], device_id=(x, y, 1 - c), device_id_type=MESH).wait_recv()
        for cp in remote:
            cp.wait_send()

    return pl.pallas_call(
        body, name=name, in_specs=[ANY] * (n + 1), out_specs=[ANY] * n,
        out_shape=[jax.ShapeDtypeStruct(b.shape, b.dtype) for b in bufs],
        input_output_aliases={a: a for a in range(n)},
        scratch_shapes=[pltpu.SemaphoreType.DMA((n,)), pltpu.SemaphoreType.DMA((n,))],
    )(*bufs, dep)


SMALL_ROWS, SMALL_COLS = 16, CONV_CH


def _allreduce_small(block):
    m_per, ncol = block.shape

    def body(x_ref, sum_ref, all_ref, send_sems, recv_sems, local_sem):
        x, y, c, chips = _place()
        me, sibling = (x, y, c), (x, y, 1 - c)

        def rows(px, py, pc):
            return all_ref.at[pl.ds((4 * px + 2 * py + pc) * m_per, m_per), :]

        def copy(k, block_of, to, src=None):
            return pltpu.make_async_remote_copy(
                src_ref=rows(*block_of) if src is None else src, dst_ref=rows(*block_of),
                send_sem=send_sems.at[k], recv_sem=recv_sems.at[k], device_id=to, device_id_type=MESH)

        mine = pltpu.make_async_copy(x_ref, rows(*me), local_sem)
        mine.start()
        first = [copy(0, me, sibling, src=x_ref)]
        first += [copy(1 + j, me, (*chip, c), src=x_ref) for j, chip in enumerate(chips)]
        for cp in first:
            cp.start()
        passed = [copy(4 + j, (*chip, c), sibling) for j, chip in enumerate(chips)]
        for j, chip in enumerate(chips):
            copy(1 + j, (*chip, c), me).wait_recv()
            passed[j].start()
        copy(0, sibling, me).wait_recv()
        for j, chip in enumerate(chips):
            copy(4 + j, (*chip, 1 - c), me).wait_recv()
        for cp in first + passed:
            cp.wait_send()
        mine.wait()
        total = all_ref[0:m_per, :]
        for d in range(1, N_DEV):
            total = total + all_ref[d * m_per:(d + 1) * m_per, :]
        sum_ref[...] = total

    vm = pl.BlockSpec(memory_space=pltpu.VMEM)
    return pl.pallas_call(
        body, name="allreduce_small", in_specs=[vm], out_specs=vm,
        out_shape=jax.ShapeDtypeStruct((m_per, ncol), F32),
        scratch_shapes=[pltpu.VMEM((N_DEV * m_per, ncol), F32), pltpu.SemaphoreType.DMA((7,)),
                        pltpu.SemaphoreType.DMA((7,)), pltpu.SemaphoreType.DMA],
    )(block)


def _row_tile(rows, cols):
    tile = rows
    while tile * cols * 4 > (1 << 20) and tile % 16 == 0:
        tile //= 2
    return tile


def _elementwise(fn, name, ins, out_dtypes, dep):
    rows, cols = ins[0].shape
    tile = _row_tile(rows, cols)

    def body(*refs):
        outs = fn(*[r[...] for r in refs[:len(ins)]])
        for o_ref, o in zip(refs[len(ins) + 1:], outs):
            o_ref[...] = o.astype(o_ref.dtype)

    spec = pl.BlockSpec((tile, cols), lambda i: (i, 0))
    return pl.pallas_call(
        body, name=name, grid=(rows // tile,), in_specs=[spec] * len(ins) + [pl.BlockSpec(memory_space=pl.ANY)],
        out_specs=[spec] * len(out_dtypes),
        out_shape=[jax.ShapeDtypeStruct((rows, cols), d) for d in out_dtypes],
        compiler_params=_params("parallel"),
    )(*ins, dep)


def _adamw_tile(w, g, m, v):
    m = ADAM_B1 * m + (1.0 - ADAM_B1) * g
    v = ADAM_B2 * v + (1.0 - ADAM_B2) * jnp.square(g)
    m_hat = m / (1.0 - ADAM_B1 ** ADAM_STEP)
    v_hat = v / (1.0 - ADAM_B2 ** ADAM_STEP)
    delta = -ADAM_LR * (m_hat / (jnp.sqrt(v_hat) + ADAM_EPS) + ADAM_WD * w)
    return delta, m, v


def _adamw(name, w, g, m, v, dep):
    return _elementwise(_adamw_tile, name, [w, g, m, v], [F32, F32, F32], dep)


def _chip_sum(name, g4, got, place):
    nchip, hr, cols = got.shape
    tile = _row_tile(hr, cols)
    nblk = hr // tile

    def body(pl_ref, g_ref, o_ref, s32_ref, s16_ref):
        s = g_ref[...] + o_ref[...]
        s32_ref[...] = s
        s16_ref[...] = s.astype(BF16)

    spec = pl.BlockSpec((None, tile, cols), lambda k, i, pr: (k, i, 0))
    return pl.pallas_call(
        body, name=name,
        grid_spec=pltpu.PrefetchScalarGridSpec(
            num_scalar_prefetch=1, grid=(nchip, nblk),
            in_specs=[pl.BlockSpec((None, tile, cols), lambda k, i, pr: (k, pr[1] * nblk + i, 0)), spec],
            out_specs=[spec, spec]),
        out_shape=[jax.ShapeDtypeStruct(got.shape, F32), jax.ShapeDtypeStruct(got.shape, BF16)],
        compiler_params=_params("parallel", "parallel"),
    )(place, g4, got)


def _mesh_sum(name, s32, got, place):
    _, hr, cols = s32.shape
    tile = _row_tile(hr, cols)

    def body(pl_ref, own_ref, g0_ref, g1_ref, g2_ref, o_ref):
        o_ref[...] = ((own_ref[...] + g0_ref[...].astype(F32)) + g1_ref[...].astype(F32)) + g2_ref[...].astype(F32)

    slab = lambda j: pl.BlockSpec((None, tile, cols), lambda i, pr: (j, i, 0))
    return pl.pallas_call(
        body, name=name,
        grid_spec=pltpu.PrefetchScalarGridSpec(
            num_scalar_prefetch=1, grid=(hr // tile,),
            in_specs=[pl.BlockSpec((None, tile, cols), lambda i, pr: (pr[0], i, 0)), slab(0), slab(1), slab(2)],
            out_specs=pl.BlockSpec((None, tile, cols), lambda i, pr: (pr[1], i, 0))),
        out_shape=jax.ShapeDtypeStruct((2, hr, cols), F32),
        compiler_params=_params("parallel"),
    )(place, s32, got, got, got)


def _place_operand():
    return jnp.stack([2 * lax.axis_index("x") + lax.axis_index("y"), lax.axis_index("c")]).astype(jnp.int32)


def _per_chip(name, g):
    if name == "w_in":
        return jnp.transpose(g.reshape(D_MODEL, N_CHIPS, D_IN // N_CHIPS), (1, 0, 2))
    if name == "w_proj":
        return jnp.transpose(g.reshape(PLE_DIM, N_CHIPS, D_MODEL // N_CHIPS), (1, 0, 2))
    if name == "w_up4":
        return g
    return g.reshape(N_CHIPS, g.shape[0] // N_CHIPS, g.shape[1])


class _EarlyReducer:
    def __init__(self, tag):
        self.tag = tag

    def start(self, grads):
        self.names = list(grads)
        self.place = _place_operand()
        slabs = [_per_chip(k, grads[k]) for k in self.names]
        halves = [jax.ShapeDtypeStruct((s.shape[0], s.shape[1] // 2, s.shape[2]), F32) for s in slabs]
        self.a = _copies_start(self.tag + "exchange_start", _exchange_copies, len(slabs), slabs, halves,
                               slabs[0][0, :8, :128])
        return self.a[-1]

    def middle(self, after):
        slabs, got = _copies_wait(self.tag + "exchange_wait", _exchange_copies, self.a, after)
        self.sums = [_chip_sum(self.tag + "chip_sum_" + k, s, g, self.place) for k, s, g in zip(self.names, slabs, got)]
        s16 = [s[1] for s in self.sums]
        lands = [jax.ShapeDtypeStruct((3,) + s.shape[1:], BF16) for s in s16]
        self.b = _copies_start(self.tag + "scatter_start", _scatter_copies, 3 * len(s16), s16, lands,
                               self.sums[0][0][0, :8, :128])
        return self.b[-1]

    def finish(self, after):
        _, got = _copies_wait(self.tag + "scatter_wait", _scatter_copies, self.b, after)
        self.bufs = {k: _mesh_sum(self.tag + "mesh_sum_" + k, s[0], g, self.place)
                     for k, s, g in zip(self.names, self.sums, got)}


def kernel(x, p, norm_mix, w_in, conv_w, a_log, dt_bias, dn_norm, sinks, w_o, norm_mlp, w_up, w_down, norm_ple, w_ple_gate, w_ple_proj, norm_final, loss_target, m_norm_mix, m_w_in, m_conv_w, m_a_log, m_dt_bias, m_dn_norm, m_sinks, m_w_o, m_norm_mlp, m_w_up, m_w_down, m_norm_ple, m_w_ple_gate, m_w_ple_proj, m_norm_final, v_norm_mix, v_w_in, v_conv_w, v_a_log, v_dt_bias, v_dn_norm, v_sinks, v_w_o, v_norm_mlp, v_w_up, v_w_down, v_norm_ple, v_w_ple_gate, v_w_ple_proj, v_norm_final):
    chip = 2 * lax.axis_index("x") + lax.axis_index("y")
    big = dict(w_in=w_in[0], w_o=w_o[0], w_up=w_up[0], w_down=w_down[0], w_gate=w_ple_gate[0], w_proj=w_ple_proj[0])
    big_m = dict(w_in=m_w_in[0], w_o=m_w_o[0], w_up=m_w_up[0], w_down=m_w_down[0], w_gate=m_w_ple_gate[0], w_proj=m_w_ple_proj[0])
    big_v = dict(w_in=v_w_in[0], w_o=v_w_o[0], w_up=v_w_up[0], w_down=v_w_down[0], w_gate=v_w_ple_gate[0], w_proj=v_w_ple_proj[0])
    names = list(big)

    w_in_all, conv_all = _gather_weights([big["w_in"].astype(BF16)], conv_w[0])
    late_names = names[1:]
    late_shards = [big[k].astype(BF16) for k in late_names]
    gather = _copies_start("gather_start", _late_copies, LATE_COPIES * len(late_shards), late_shards,
                           [jax.ShapeDtypeStruct((N_CHIPS,) + s.shape, BF16) for s in late_shards], w_in_all)
    token = gather[-1]
    w = dict(w_in=jnp.transpose(w_in_all, (1, 0, 2)).reshape(D_MODEL, D_IN),
             conv_w=jnp.transpose(conv_all, (1, 0, 2)).reshape(DN_CONV, CONV_CH))
    sm = dict(norm_mix=norm_mix[0] + token[0, 0], a_log=a_log[0], dt_bias=dt_bias[0], dn_norm=dn_norm[0],
              sinks=sinks[0], norm_mlp=norm_mlp[0], norm_ple=norm_ple[0], norm_final=norm_final)

    def late(after):
        first = functools.partial(_late_copies, only=(0,))
        srcs, lands = _copies_wait("gather_wait_o", first, gather, after)

        def rest(after2):
            others = functools.partial(_late_copies, only=tuple(range(1, len(late_names))))
            gw = dict(zip(late_names, _copies_wait("gather_wait_rest", others, gather[:2] + (srcs, lands, None), after2)[1]))
            return dict(w_up4=gw["w_up"], w_down=gw["w_down"].reshape(D_FF, D_MODEL),
                        w_gate=gw["w_gate"].reshape(D_MODEL, D_MODEL), w_proj4=gw["w_proj"])

        return lands[0].reshape(D_MODEL, D_MODEL), rest

    reducer = _EarlyReducer("early_")
    sums, grad_x, g = _local_step(x[0], p[0, 0], loss_target[0], sm, w, late, reducer)

    last = _EarlyReducer("last_")
    dep_a = last.start({"w_in": g["w_in"]})

    row = lambda v: jnp.zeros((SMALL_COLS,), F32).at[:v.shape[0]].set(v)
    misc = jnp.zeros((SMALL_COLS,), F32).at[0:4].set(sums["a_log"]).at[4:8].set(sums["dt_bias"]) \
        .at[8:16].set(sums["sinks"]).at[128:256].set(sums["dn_norm"]).at[256].set(sums["loss"])
    small = jnp.concatenate([sums["conv_w"], jnp.stack([row(sums["norm_mix"]), row(sums["norm_mlp"]), row(sums["norm_ple"]),
                                                        row(sums["norm_final"]), misc]),
                             jnp.zeros((SMALL_ROWS - 9, SMALL_COLS), F32)], axis=0)
    tot = _allreduce_small(small + dep_a[0, 0])
    dep_b = last.middle(tot)
    grad_key = dict(w_o="w_o", w_up="w_up4", w_down="w_down", w_gate="w_gate", w_proj="w_proj")
    full = _share_halves("share_halves", [reducer.bufs[grad_key[k]] for k in late_names], dep_b)
    red = {k: f.reshape(-1, f.shape[-1]) for k, f in zip(late_names, full)}
    loss = tot[8, 256]
    ncw = CONV_CH // N_CHIPS

    def pack(cw, nmix, nmlp, nple, nfin, al, dtb, sk, dnn):
        misc_p = jnp.zeros((SMALL_COLS,), F32).at[0:4].set(al).at[4:8].set(dtb).at[8:16].set(sk).at[128:256].set(dnn)
        cw_p = jnp.zeros((DN_CONV, SMALL_COLS), F32).at[:, :ncw].set(cw)
        return jnp.concatenate([cw_p, jnp.stack([row(nmix), row(nmlp), row(nple), row(nfin), misc_p]),
                                jnp.zeros((SMALL_ROWS - 9, SMALL_COLS), F32)], axis=0)

    def unpack(buf):
        return dict(conv_w=buf[0:4, :ncw][None], norm_mix=buf[4, :D_MODEL][None], norm_mlp=buf[5, :D_MODEL][None],
                    norm_ple=buf[6, :D_MODEL][None], norm_final=buf[7, :D_MODEL], a_log=buf[8, 0:4][None],
                    dt_bias=buf[8, 4:8][None], sinks=buf[8, 8:16][None], dn_norm=buf[8, 128:256][None])

    g_conv_shard = lax.dynamic_slice(tot[0:4], (0, chip * ncw), (DN_CONV, ncw))
    g_small = pack(g_conv_shard, tot[4, :D_MODEL], tot[5, :D_MODEL], tot[6, :D_MODEL], tot[7, :D_MODEL],
                   tot[8, 0:4], tot[8, 4:8], tot[8, 8:16], tot[8, 128:256])
    w_small = pack(conv_w[0], norm_mix[0], norm_mlp[0], norm_ple[0], norm_final, a_log[0], dt_bias[0], sinks[0], dn_norm[0])
    m_small = pack(m_conv_w[0], m_norm_mix[0], m_norm_mlp[0], m_norm_ple[0], m_norm_final, m_a_log[0], m_dt_bias[0],
                   m_sinks[0], m_dn_norm[0])
    v_small = pack(v_conv_w[0], v_norm_mix[0], v_norm_mlp[0], v_norm_ple[0], v_norm_final, v_a_log[0], v_dt_bias[0],
                   v_sinks[0], v_dn_norm[0])

    ref_name = dict(w_in="w_in", w_o="w_o", w_up="w_up", w_down="w_down", w_gate="w_ple_gate", w_proj="w_ple_proj")
    out_g, out_d, out_m, out_v = {}, {}, {}, {}

    def update(k, dep):
        d_k, m_k, v_k = _adamw("adamw_" + k, big[k], red[k], big_m[k], big_v[k], dep)
        out_g[ref_name[k]], out_d[ref_name[k]] = red[k][None], d_k[None]
        out_m[ref_name[k]], out_v[ref_name[k]] = m_k[None], v_k[None]
        return d_k

    for k in late_names:
        done = update(k, dep_b)
    small_out = _adamw("adamw_small", w_small, g_small, m_small, v_small, dep_b)
    d_s, m_s, v_s = (unpack(b) for b in small_out)
    g_s = unpack(g_small)
    for src, dst in ((g_s, out_g), (d_s, out_d), (m_s, out_m), (v_s, out_v)):
        dst.update(src)
    last.finish(done + small_out[0][0:1, 0:1])
    (w_in_full,) = _share_halves("share_halves_w_in", [last.bufs["w_in"]], dep_b)
    red["w_in"] = w_in_full.reshape(-1, w_in_full.shape[-1])
    update("w_in", dep_b)
    order = ["norm_mix", "w_in", "conv_w", "a_log", "dt_bias", "dn_norm", "sinks", "w_o", "norm_mlp", "w_up", "w_down",
             "norm_ple", "w_ple_gate", "w_ple_proj", "norm_final"]
    return (loss, grad_x[None], *[out_g[k] for k in order], *[out_d[k] for k in order],
            *[out_m[k] for k in order], *[out_v[k] for k in order])
```

```python
import functools

import jax
import jax.numpy as jnp
from jax import lax
from jax.experimental import pallas as pl
from jax.experimental.pallas import tpu as pltpu

F32 = jnp.float32
BF16 = jnp.bfloat16
MXU_DTYPE = jnp.bfloat16
HI = lax.Precision.HIGHEST

D_MODEL = 1024
PLE_DIM = 256
ATTN_HEADS = 8
ATTN_KV_HEADS = 2
ATTN_GROUPS = ATTN_HEADS // ATTN_KV_HEADS
ATTN_HEAD_DIM = 64
ATTN_BLOCK = 128
ROPE_THETA = 10000.0
DN_HEADS = 4
DN_HEAD_DIM = 128
DN_CONV = 4
DN_CHUNK = 64
D_FF = 4 * D_MODEL
EPS = 1e-6
ATTN_Q = ATTN_HEADS * ATTN_HEAD_DIM
ATTN_KV = ATTN_KV_HEADS * ATTN_HEAD_DIM
DN_W = DN_HEADS * DN_HEAD_DIM
CONV_CH = 3 * DN_W
D_IN = ATTN_Q + 2 * ATTN_KV + 4 * DN_W + 2 * DN_HEADS
DN_COLS = 4 * DN_W + 128
DN_SCALE = DN_HEAD_DIM ** -0.5
ATTN_SCALE = ATTN_HEAD_DIM ** -0.5
FF_BLOCKS = 4
FF_BLOCK = D_FF // FF_BLOCKS

ADAM_LR = 0.001
ADAM_B1 = 0.9
ADAM_B2 = 0.999
ADAM_EPS = 1e-08
ADAM_WD = 0.01
ADAM_STEP = 10

V7X_VMEM_BYTES = 64 * 1024 * 1024
VMEM_LIMIT = 48 * 1024 * 1024

NN = ((1,), (0,))
NT = ((1,), (1,))
TN = ((0,), (0,))


def _dot(a, b, dims=NN, prec=None):
    return lax.dot_general(a, b, (dims, ((), ())), precision=prec, preferred_element_type=F32)


def _sigmoid(x):
    return 1.0 / (1.0 + jnp.exp(-x))


def _softplus(x):
    return jnp.maximum(x, 0.0) + jnp.log(1.0 + jnp.exp(-jnp.abs(x)))


def _params(*sem):
    return pltpu.CompilerParams(dimension_semantics=sem, vmem_limit_bytes=VMEM_LIMIT)


def _rms_fwd(xv, g):
    r = lax.rsqrt(jnp.mean(xv * xv, axis=-1, keepdims=True) + EPS)
    return xv * r * g


def _rms_bwd(xv, g, dn):
    r = lax.rsqrt(jnp.mean(xv * xv, axis=-1, keepdims=True) + EPS)
    xh = xv * r
    dg = jnp.sum(dn * xh, axis=0, keepdims=True)
    dxh = dn * g
    dx = r * (dxh - xh * jnp.mean(dxh * xh, axis=-1, keepdims=True))
    return dx, dg


def _full(shape):
    return pl.BlockSpec(shape, lambda *_: (0,) * len(shape))


def _inproj(x, g_mix, wa, wd, tm):
    t = x.shape[0]

    def body(x_ref, g_ref, wa_ref, wd_ref, u_ref, pa_ref, pd_ref):
        u = _rms_fwd(x_ref[...], g_ref[...]).astype(MXU_DTYPE)
        u_ref[...] = u
        pa_ref[...] = _dot(u, wa_ref[...])
        pd_ref[...] = _dot(u, wd_ref[...])

    na, nd = wa.shape[1], wd.shape[1]
    return pl.pallas_call(
        body, name="inproj", grid=(t // tm,),
        in_specs=[pl.BlockSpec((tm, D_MODEL), lambda i: (i, 0)), _full((1, D_MODEL)),
                  _full((D_MODEL, na)), _full((D_MODEL, nd))],
        out_specs=[pl.BlockSpec((tm, D_MODEL), lambda i: (i, 0)), pl.BlockSpec((tm, na), lambda i: (i, 0)),
                   pl.BlockSpec((tm, nd), lambda i: (i, 0))],
        out_shape=[jax.ShapeDtypeStruct((t, D_MODEL), MXU_DTYPE), jax.ShapeDtypeStruct((t, na), F32),
                   jax.ShapeDtypeStruct((t, nd), F32)],
        compiler_params=_params("parallel"),
    )(x, g_mix, wa, wd)


def _oproj(x, ao, dn, wo_a, wo_d, tm):
    t = x.shape[0]

    def body(x_ref, ao_ref, dn_ref, wa_ref, wd_ref, h_ref):
        h_ref[...] = (x_ref[...] + _dot(ao_ref[...].astype(MXU_DTYPE), wa_ref[...])
                      + _dot(dn_ref[...].astype(MXU_DTYPE), wd_ref[...]))

    half = ao.shape[1]
    return pl.pallas_call(
        body, name="oproj", grid=(t // tm,),
        in_specs=[pl.BlockSpec((tm, D_MODEL), lambda i: (i, 0)), pl.BlockSpec((tm, half), lambda i: (i, 0)),
                  pl.BlockSpec((tm, half), lambda i: (i, 0)), _full((half, D_MODEL)), _full((half, D_MODEL))],
        out_specs=pl.BlockSpec((tm, D_MODEL), lambda i: (i, 0)),
        out_shape=jax.ShapeDtypeStruct((t, D_MODEL), F32),
        compiler_params=_params("parallel"),
    )(x, ao, dn, wo_a, wo_d)


def _ff_weight_scratch():
    return [pltpu.VMEM((FF_BLOCKS, D_MODEL, FF_BLOCK), MXU_DTYPE), pltpu.VMEM((FF_BLOCKS, FF_BLOCK, D_MODEL), MXU_DTYPE),
            pltpu.SemaphoreType.DMA((2,))]


def _load_ff_block(wu_hbm, wd_hbm, wu_ref, wd_ref, sems, k):
    @pl.when(pl.program_id(0) == 0)
    def _():
        up = pltpu.make_async_copy(wu_hbm.at[k], wu_ref.at[k], sems.at[0])
        down = pltpu.make_async_copy(wd_hbm.at[pl.ds(k * FF_BLOCK, FF_BLOCK), :], wd_ref.at[k], sems.at[1])
        up.start()
        down.start()
        up.wait()
        down.wait()


def _mlp_fwd(h1, g_mlp, w_up4, w_down, tm):
    t = h1.shape[0]

    def body(h_ref, g_ref, wu_hbm, wd_hbm, m_ref, r_ref, h2_ref, acc_ref, wu_ref, wd_ref, sems):
        k = pl.program_id(1)
        _load_ff_block(wu_hbm, wd_hbm, wu_ref, wd_ref, sems, k)

        @pl.when(k == 0)
        def _():
            m_ref[...] = _rms_fwd(h_ref[...], g_ref[...]).astype(MXU_DTYPE)
            acc_ref[...] = jnp.zeros_like(acc_ref)

        r = jnp.maximum(_dot(m_ref[...], wu_ref[k]), 0.0)
        r_ref[...] = r.astype(MXU_DTYPE)
        s = jnp.square(r).astype(MXU_DTYPE)
        acc_ref[...] += _dot(s, wd_ref[k])

        @pl.when(k == FF_BLOCKS - 1)
        def _():
            h2_ref[...] = h_ref[...] + acc_ref[...]

    return pl.pallas_call(
        body, name="mlp_fwd", grid=(t // tm, FF_BLOCKS),
        in_specs=[pl.BlockSpec((tm, D_MODEL), lambda i, k: (i, 0)), _full((1, D_MODEL)),
                  pl.BlockSpec(memory_space=pl.ANY), pl.BlockSpec(memory_space=pl.ANY)],
        out_specs=[pl.BlockSpec((tm, D_MODEL), lambda i, k: (i, 0)), pl.BlockSpec((tm, FF_BLOCK), lambda i, k: (i, k)),
                   pl.BlockSpec((tm, D_MODEL), lambda i, k: (i, 0))],
        out_shape=[jax.ShapeDtypeStruct((t, D_MODEL), MXU_DTYPE), jax.ShapeDtypeStruct((t, D_FF), MXU_DTYPE),
                   jax.ShapeDtypeStruct((t, D_MODEL), F32)],
        scratch_shapes=[pltpu.VMEM((tm, D_MODEL), F32)] + _ff_weight_scratch(),
        compiler_params=_params("arbitrary", "arbitrary"),
    )(h1, g_mlp, w_up4, w_down)


def _ple_loss(h2, p, tgt, g_ple, g_fin, w_gate, w_proj, tm):
    t = h2.shape[0]

    def body(h_ref, p_ref, t_ref, gp_ref, gf_ref, wg_ref, wp_ref,
             dh_ref, dhb_ref, dgp_ref, dpp_ref, n3_ref, pb_ref, acc_ref):
        @pl.when(pl.program_id(0) == 0)
        def _():
            acc_ref[...] = jnp.zeros_like(acc_ref)

        h = h_ref[...]
        g_ple_v, g_fin_v = gp_ref[...], gf_ref[...]
        n3 = _rms_fwd(h, g_ple_v).astype(MXU_DTYPE)
        n3_ref[...] = n3
        gate = _sigmoid(_dot(n3, wg_ref[...]))
        pb = p_ref[...].astype(MXU_DTYPE)
        pb_ref[...] = pb
        pp = _dot(pb, wp_ref[...])
        h3 = h + gate * pp
        r4 = lax.rsqrt(jnp.mean(h3 * h3, axis=-1, keepdims=True) + EPS)
        xh4 = h3 * r4
        e = xh4 * g_fin_v - t_ref[...]
        loss = 0.5 * jnp.sum(jnp.mean(e * e, axis=-1, keepdims=True), axis=0, keepdims=True)
        dy = e * (1.0 / D_MODEL)
        dg_fin = jnp.sum(dy * xh4, axis=0, keepdims=True)
        dxh = dy * g_fin_v
        dh3 = r4 * (dxh - xh4 * jnp.mean(dxh * xh4, axis=-1, keepdims=True))
        dpp_ref[...] = (dh3 * gate).astype(MXU_DTYPE)
        dgp = (dh3 * pp * gate * (1.0 - gate)).astype(MXU_DTYPE)
        dgp_ref[...] = dgp
        dn3 = _dot(dgp, wg_ref[...], NT)
        dx, dg_ple = _rms_bwd(h, g_ple_v, dn3)
        dh2 = dh3 + dx
        dh_ref[...] = dh2
        dhb_ref[...] = dh2.astype(MXU_DTYPE)
        acc_ref[0:1, :] += dg_fin
        acc_ref[1:2, :] += dg_ple
        acc_ref[2:3, :] += jnp.broadcast_to(loss, (1, D_MODEL))

    row = lambda w: pl.BlockSpec((tm, w), lambda i: (i, 0))
    return pl.pallas_call(
        body, name="ple_loss", grid=(t // tm,),
        in_specs=[row(D_MODEL), row(PLE_DIM), row(D_MODEL), _full((1, D_MODEL)), _full((1, D_MODEL)),
                  _full((D_MODEL, D_MODEL)), _full((PLE_DIM, D_MODEL))],
        out_specs=[row(D_MODEL), row(D_MODEL), row(D_MODEL), row(D_MODEL), row(D_MODEL), row(PLE_DIM),
                   _full((8, D_MODEL))],
        out_shape=[jax.ShapeDtypeStruct((t, D_MODEL), F32), jax.ShapeDtypeStruct((t, D_MODEL), MXU_DTYPE),
                   jax.ShapeDtypeStruct((t, D_MODEL), MXU_DTYPE), jax.ShapeDtypeStruct((t, D_MODEL), MXU_DTYPE),
                   jax.ShapeDtypeStruct((t, D_MODEL), MXU_DTYPE), jax.ShapeDtypeStruct((t, PLE_DIM), MXU_DTYPE),
                   jax.ShapeDtypeStruct((8, D_MODEL), F32)],
        compiler_params=_params("arbitrary"),
    )(h2, p, tgt, g_ple, g_fin, w_gate, w_proj)


def _mlp_bwd(dh2, dh2b, r, h1, g_mlp, w_up4, w_down, tm):
    t = h1.shape[0]

    def body(dh_ref, dhb_ref, r_ref, h_ref, g_ref, wu_hbm, wd_hbm,
             da_ref, dh1_ref, dh1b_ref, acc_ref, dm_ref, wu_ref, wd_ref, sems):
        i, k = pl.program_id(0), pl.program_id(1)
        _load_ff_block(wu_hbm, wd_hbm, wu_ref, wd_ref, sems, k)

        @pl.when((i == 0) & (k == 0))
        def _():
            acc_ref[...] = jnp.zeros_like(acc_ref)

        @pl.when(k == 0)
        def _():
            dm_ref[...] = jnp.zeros_like(dm_ref)

        ds = _dot(dhb_ref[...], wd_ref[k], NT)
        da = (ds * (2.0 * r_ref[...].astype(F32))).astype(MXU_DTYPE)
        da_ref[...] = da
        dm_ref[...] += _dot(da, wu_ref[k], NT)

        @pl.when(k == FF_BLOCKS - 1)
        def _():
            dx, dg = _rms_bwd(h_ref[...], g_ref[...], dm_ref[...])
            dh1 = dh_ref[...] + dx
            dh1_ref[...] = dh1
            dh1b_ref[...] = dh1.astype(MXU_DTYPE)
            acc_ref[0:1, :] += dg

    tok = lambda w: pl.BlockSpec((tm, w), lambda i, k: (i, 0))
    return pl.pallas_call(
        body, name="mlp_bwd", grid=(t // tm, FF_BLOCKS),
        in_specs=[tok(D_MODEL), tok(D_MODEL), pl.BlockSpec((tm, FF_BLOCK), lambda i, k: (i, k)), tok(D_MODEL),
                  _full((1, D_MODEL)), pl.BlockSpec(memory_space=pl.ANY), pl.BlockSpec(memory_space=pl.ANY)],
        out_specs=[pl.BlockSpec((tm, FF_BLOCK), lambda i, k: (i, k)),
                   tok(D_MODEL), tok(D_MODEL), pl.BlockSpec((8, D_MODEL), lambda i, k: (0, 0))],
        out_shape=[jax.ShapeDtypeStruct((t, D_FF), MXU_DTYPE),
                   jax.ShapeDtypeStruct((t, D_MODEL), F32), jax.ShapeDtypeStruct((t, D_MODEL), MXU_DTYPE),
                   jax.ShapeDtypeStruct((8, D_MODEL), F32)],
        scratch_shapes=[pltpu.VMEM((tm, D_MODEL), F32)] + _ff_weight_scratch(),
        compiler_params=_params("arbitrary", "arbitrary"),
    )(dh2, dh2b, r, h1, g_mlp, w_up4, w_down)


def _oproj_bwd(dh1b, wo_a, wo_d, tm, dep):
    t = dh1b.shape[0]
    half = wo_a.shape[0]

    def body(d_ref, wa_ref, wd_ref, dep_ref, da_ref, dd_ref):
        d = d_ref[...]
        da_ref[...] = _dot(d, wa_ref[...], NT)
        dd_ref[...] = _dot(d, wd_ref[...], NT)

    return pl.pallas_call(
        body, name="oproj_bwd", grid=(t // tm,),
        in_specs=[pl.BlockSpec((tm, D_MODEL), lambda i: (i, 0)), _full((half, D_MODEL)), _full((half, D_MODEL)),
                  pl.BlockSpec(memory_space=pl.ANY)],
        out_specs=[pl.BlockSpec((tm, half), lambda i: (i, 0)), pl.BlockSpec((tm, half), lambda i: (i, 0))],
        out_shape=[jax.ShapeDtypeStruct((t, half), F32), jax.ShapeDtypeStruct((t, half), F32)],
        compiler_params=_params("parallel"),
    )(dh1b, wo_a, wo_d, dep)


def _inproj_bwd(x, dh1, g_mix, grads, weights, tm):
    t = x.shape[0]
    n = len(grads)

    def body(*refs):
        x_ref, dh_ref, g_ref = refs[:3]
        g_refs, w_refs = refs[3:3 + n], refs[3 + n:3 + 2 * n]
        dx_ref, acc_ref = refs[3 + 2 * n:]

        @pl.when(pl.program_id(0) == 0)
        def _():
            acc_ref[...] = jnp.zeros_like(acc_ref)

        du = _dot(g_refs[0][...], w_refs[0][...], NT)
        for j in range(1, n):
            du += _dot(g_refs[j][...], w_refs[j][...], NT)
        dx, dg = _rms_bwd(x_ref[...], g_ref[...], du)
        dx_ref[...] = dh_ref[...] + dx
        acc_ref[0:1, :] += dg

    tok = lambda w: pl.BlockSpec((tm, w), lambda i: (i, 0))
    return pl.pallas_call(
        body, name="inproj_bwd", grid=(t // tm,),
        in_specs=[tok(D_MODEL), tok(D_MODEL), _full((1, D_MODEL))] + [tok(g.shape[1]) for g in grads]
                 + [_full(w.shape) for w in weights],
        out_specs=[tok(D_MODEL), _full((8, D_MODEL))],
        out_shape=[jax.ShapeDtypeStruct((t, D_MODEL), F32), jax.ShapeDtypeStruct((8, D_MODEL), F32)],
        compiler_params=_params("arbitrary"),
    )(x, dh1, g_mix, *grads, *weights)


def _wgrad(a, b, name, tk, tn, tt, stacked=False, prep=None):
    t, kdim = a.shape
    ncols = b.shape[1]

    def body(a_ref, b_ref, o_ref):
        @pl.when(pl.program_id(2) == 0)
        def _():
            o_ref[...] = jnp.zeros_like(o_ref)

        av = a_ref[...] if prep is None else prep(a_ref[...])
        o_ref[...] += _dot(av, b_ref[...], TN)

    if stacked:
        out_spec = pl.BlockSpec((None, tk, tn), lambda i, j, s: (j, i, 0))
        out_shape = jax.ShapeDtypeStruct((ncols // tn, kdim, tn), F32)
    else:
        out_spec = pl.BlockSpec((tk, tn), lambda i, j, s: (i, j))
        out_shape = jax.ShapeDtypeStruct((kdim, ncols), F32)
    return pl.pallas_call(
        body, name=name, grid=(kdim // tk, ncols // tn, t // tt),
        in_specs=[pl.BlockSpec((tt, tk), lambda i, j, s: (s, i)), pl.BlockSpec((tt, tn), lambda i, j, s: (s, j))],
        out_specs=out_spec, out_shape=out_shape,
        compiler_params=_params("parallel", "parallel", "arbitrary"),
    )(a, b)


def _wgrad_cat(as_, bs, name, tt):
    t = as_[0].shape[0]
    heights = [a.shape[1] for a in as_]
    widths = [b.shape[1] for b in bs]

    def body(*refs):
        a_refs, b_refs, o_ref = refs[:len(as_)], refs[len(as_):-1], refs[-1]

        @pl.when(pl.program_id(0) == 0)
        def _():
            o_ref[...] = jnp.zeros_like(o_ref)

        row = 0
        for a_ref, k in zip(a_refs, heights):
            av = a_ref[...]
            col = 0
            for b_ref, n in zip(b_refs, widths):
                o_ref[row:row + k, col:col + n] += _dot(av, b_ref[...], TN)
                col += n
            row += k

    tok = lambda w: pl.BlockSpec((tt, w), lambda s: (s, 0))
    shape = (sum(heights), sum(widths))
    return pl.pallas_call(
        body, name=name, grid=(t // tt,),
        in_specs=[tok(k) for k in heights] + [tok(n) for n in widths],
        out_specs=_full(shape), out_shape=jax.ShapeDtypeStruct(shape, F32),
        compiler_params=_params("arbitrary"),
    )(*as_, *bs)


def _rope_tables(t):
    half = ATTN_HEAD_DIM // 2
    inv = 1.0 / (ROPE_THETA ** (jnp.arange(half, dtype=F32) * (2.0 / ATTN_HEAD_DIM)))
    ang = jnp.arange(t, dtype=F32)[:, None] * inv[None, :]
    cos, sin = jnp.cos(ang), jnp.sin(ang)
    cos2 = jnp.concatenate([cos, cos], axis=-1)
    sin2 = jnp.concatenate([-sin, sin], axis=-1)
    return jnp.tile(cos2, (1, 2)), jnp.tile(sin2, (1, 2))


def _swap_halves(tv):
    w = tv.shape[-1]
    lane = lax.broadcasted_iota(jnp.int32, tv.shape, tv.ndim - 1)
    first = (lane % ATTN_HEAD_DIM) < (ATTN_HEAD_DIM // 2)
    return jnp.where(first, pltpu.roll(tv, w - ATTN_HEAD_DIM // 2, tv.ndim - 1),
                     pltpu.roll(tv, ATTN_HEAD_DIM // 2, tv.ndim - 1))


def _rope(tv, cos, sin):
    return tv * cos + _swap_halves(tv) * sin


def _rope_bwd(dv, cos, sin):
    return dv * cos + _swap_halves(dv * sin)


def _attn_valid(first_block):
    c = lax.broadcasted_iota(jnp.int32, (2 * ATTN_BLOCK, ATTN_BLOCK), 0)
    r = lax.broadcasted_iota(jnp.int32, (2 * ATTN_BLOCK, ATTN_BLOCK), 1)
    return (c > r) & (c <= r + ATTN_BLOCK) & ((c >= ATTN_BLOCK) | jnp.logical_not(first_block))


def _attn_probs(st, sink, valid):
    s = jnp.where(valid, st * ATTN_SCALE, -jnp.inf)
    m = jnp.maximum(jnp.max(s, axis=0, keepdims=True), sink)
    e = jnp.where(valid, jnp.exp(s - m), 0.0)
    es = jnp.exp(sink - m)
    inv = 1.0 / (jnp.sum(e, axis=0, keepdims=True) + es)
    return e * inv, es * inv


def _lane_scalar(vec, idx):
    lane = lax.broadcasted_iota(jnp.int32, vec.shape, 1)
    return jnp.sum(jnp.where(lane == idx, vec, 0.0), axis=-1, keepdims=True)


def _attn_specs(nb):
    cur = lambda w, cb: pl.BlockSpec((ATTN_BLOCK, w), lambda i: (jnp.minimum(i, nb - 1), cb))
    prev = lambda w, cb: pl.BlockSpec((ATTN_BLOCK, w), lambda i: (jnp.maximum(jnp.minimum(i, nb - 1) - 1, 0), cb))
    kcol, vcol = ATTN_Q // ATTN_KV, ATTN_Q // ATTN_KV + 1
    return [cur(ATTN_Q, 0), cur(ATTN_KV, kcol), prev(ATTN_KV, kcol), cur(ATTN_KV, vcol), prev(ATTN_KV, vcol),
            cur(ATTN_KV, 0), cur(ATTN_KV, 0), prev(ATTN_KV, 0), prev(ATTN_KV, 0), _full((1, 128))]


def _attn_fwd(pa, cos, sin, sinks_vec):
    t = pa.shape[0]
    nb = t // ATTN_BLOCK

    def body(q_ref, kc_ref, kp_ref, vc_ref, vp_ref, cc_ref, sc_ref, cp_ref, sp_ref, sk_ref, o_ref):
        first = pl.program_id(0) == 0
        cc, sc = cc_ref[...], sc_ref[...]
        q = _rope(q_ref[...], jnp.tile(cc, (1, ATTN_Q // ATTN_KV)), jnp.tile(sc, (1, ATTN_Q // ATTN_KV)))
        kc = _rope(kc_ref[...], cc, sc)
        kp = _rope(kp_ref[...], cp_ref[...], sp_ref[...])
        vc, vp = vc_ref[...], vp_ref[...]
        sk = sk_ref[...]
        valid = _attn_valid(first)
        kv = lambda tp, tc, hk: jnp.concatenate([tp[:, hk * ATTN_HEAD_DIM:(hk + 1) * ATTN_HEAD_DIM],
                                                 tc[:, hk * ATTN_HEAD_DIM:(hk + 1) * ATTN_HEAD_DIM]], axis=0)
        kwins = [kv(kp, kc, hk) for hk in range(ATTN_KV_HEADS)]
        vwins_t = [kv(vp, vc, hk).T for hk in range(ATTN_KV_HEADS)]
        heads = [slice(h * ATTN_HEAD_DIM, (h + 1) * ATTN_HEAD_DIM) for h in range(ATTN_HEADS)]
        scores = [_dot(kwins[h // ATTN_GROUPS], q[:, hs], NT) for h, hs in enumerate(heads)]
        probs = [_attn_probs(st, _lane_scalar(sk, h), valid)[0] for h, st in enumerate(scores)]
        for h, (hs, pt) in enumerate(zip(heads, probs)):
            o_ref[:, hs] = _dot(vwins_t[h // ATTN_GROUPS], pt).T.astype(o_ref.dtype)

    return pl.pallas_call(
        body, name="attn_fwd", grid=(nb,),
        in_specs=_attn_specs(nb),
        out_specs=pl.BlockSpec((ATTN_BLOCK, ATTN_Q), lambda i: (i, 0)),
        out_shape=jax.ShapeDtypeStruct((t, ATTN_Q), MXU_DTYPE),
        compiler_params=_params("parallel"),
    )(pa, pa, pa, pa, pa, cos, sin, cos, sin, sinks_vec)


def _attn_bwd(pa, cos, sin, sinks_vec, dao):
    t = pa.shape[0]
    nb = t // ATTN_BLOCK

    def body(q_ref, kc_ref, kp_ref, vc_ref, vp_ref, cc_ref, sc_ref, cp_ref, sp_ref, sk_ref, do_ref,
             dq_ref, dk_ref, dv_ref, acc_ref, dqr_ref, dkw_ref, dvw_ref, ck_ref, cv_ref):
        i = pl.program_id(0)

        @pl.when(i == 0)
        def _():
            acc_ref[...] = jnp.zeros_like(acc_ref)
            ck_ref[...] = jnp.zeros_like(ck_ref)
            cv_ref[...] = jnp.zeros_like(cv_ref)

        @pl.when(i < nb)
        def _():
            first = i == 0
            cc, sc = cc_ref[...], sc_ref[...]
            cq, sq = jnp.tile(cc, (1, ATTN_Q // ATTN_KV)), jnp.tile(sc, (1, ATTN_Q // ATTN_KV))
            q = _rope(q_ref[...], cq, sq)
            kc = _rope(kc_ref[...], cc, sc)
            kp = _rope(kp_ref[...], cp_ref[...], sp_ref[...])
            vc, vp = vc_ref[...], vp_ref[...]
            sk = sk_ref[...]
            do = do_ref[...]
            lane = lax.broadcasted_iota(jnp.int32, (1, 128), 1)
            dsink = jnp.zeros((1, 128), F32)
            valid = _attn_valid(first)
            kv = lambda tp, tc, hk: jnp.concatenate([tp[:, hk * ATTN_HEAD_DIM:(hk + 1) * ATTN_HEAD_DIM],
                                                     tc[:, hk * ATTN_HEAD_DIM:(hk + 1) * ATTN_HEAD_DIM]], axis=0)
            kwins = [kv(kp, kc, hk) for hk in range(ATTN_KV_HEADS)]
            vwins = [kv(vp, vc, hk) for hk in range(ATTN_KV_HEADS)]
            kwins_t = [kw.T for kw in kwins]
            heads = [slice(h * ATTN_HEAD_DIM, (h + 1) * ATTN_HEAD_DIM) for h in range(ATTN_HEADS)]
            scores = [_dot(kwins[h // ATTN_GROUPS], q[:, hs], NT) for h, hs in enumerate(heads)]
            dps = [_dot(vwins[h // ATTN_GROUPS], do[:, hs], NT) for h, hs in enumerate(heads)]
            pts, dsts = [], []
            for h, (st, dp_t) in enumerate(zip(scores, dps)):
                probs_t, psink = _attn_probs(st, _lane_scalar(sk, h), valid)
                delta = jnp.sum(probs_t * dp_t, axis=0, keepdims=True)
                pts.append(probs_t)
                dsts.append(probs_t * (dp_t - delta) * ATTN_SCALE)
                dsink += jnp.where(lane == h, jnp.sum(-psink * delta, axis=1, keepdims=True), 0.0)
            for h, (hs, ds_t) in enumerate(zip(heads, dsts)):
                dqr_ref[:, hs] = _dot(kwins_t[h // ATTN_GROUPS], ds_t).T
            for hk in range(ATTN_KV_HEADS):
                ks = slice(hk * ATTN_HEAD_DIM, (hk + 1) * ATTN_HEAD_DIM)
                group = range(hk * ATTN_GROUPS, (hk + 1) * ATTN_GROUPS)
                ds_g = jnp.concatenate([dsts[h] for h in group], axis=1)
                p_g = jnp.concatenate([pts[h] for h in group], axis=1)
                q_g = jnp.concatenate([q[:, heads[h]] for h in group], axis=0)
                do_g = jnp.concatenate([do[:, heads[h]] for h in group], axis=0)
                dkw_ref[:, ks] = _dot(ds_g, q_g)
                dvw_ref[:, ks] = _dot(p_g, do_g)
            acc_ref[0:1, :] += dsink
            dq_ref[...] = _rope_bwd(dqr_ref[...], cq, sq).astype(dq_ref.dtype)
            dk_ref[...] = (ck_ref[...] + _rope_bwd(dkw_ref[0:ATTN_BLOCK, :], cp_ref[...], sp_ref[...])).astype(dk_ref.dtype)
            dv_ref[...] = (cv_ref[...] + dvw_ref[0:ATTN_BLOCK, :]).astype(dv_ref.dtype)
            ck_ref[...] = _rope_bwd(dkw_ref[ATTN_BLOCK:2 * ATTN_BLOCK, :], cc, sc)
            cv_ref[...] = dvw_ref[ATTN_BLOCK:2 * ATTN_BLOCK, :]

        @pl.when(i == nb)
        def _():
            dk_ref[...] = ck_ref[...].astype(dk_ref.dtype)
            dv_ref[...] = cv_ref[...].astype(dv_ref.dtype)

    prev_out = lambda w: pl.BlockSpec((ATTN_BLOCK, w), lambda i: (jnp.maximum(i - 1, 0), 0))
    return pl.pallas_call(
        body, name="attn_bwd", grid=(nb + 1,),
        in_specs=_attn_specs(nb) + [pl.BlockSpec((ATTN_BLOCK, ATTN_Q), lambda i: (jnp.minimum(i, nb - 1), 0))],
        out_specs=[pl.BlockSpec((ATTN_BLOCK, ATTN_Q), lambda i: (jnp.minimum(i, nb - 1), 0)), prev_out(ATTN_KV),
                   prev_out(ATTN_KV), _full((8, 128))],
        out_shape=[jax.ShapeDtypeStruct((t, ATTN_Q), MXU_DTYPE), jax.ShapeDtypeStruct((t, ATTN_KV), MXU_DTYPE),
                   jax.ShapeDtypeStruct((t, ATTN_KV), MXU_DTYPE), jax.ShapeDtypeStruct((8, 128), F32)],
        scratch_shapes=[pltpu.VMEM((ATTN_BLOCK, ATTN_Q), F32), pltpu.VMEM((2 * ATTN_BLOCK, ATTN_KV), F32),
                        pltpu.VMEM((2 * ATTN_BLOCK, ATTN_KV), F32), pltpu.VMEM((ATTN_BLOCK, ATTN_KV), F32),
                        pltpu.VMEM((ATTN_BLOCK, ATTN_KV), F32)],
        compiler_params=_params("arbitrary"),
    )(pa, pa, pa, pa, pa, cos, sin, cos, sin, sinks_vec, dao)


PAIR = 2 * DN_CHUNK
INTRA_PAIRS = 4
HALO = 8


def _conv_window(cur_ref, prev_ref, xs_ref, tm, has_prev):
    prev = jnp.where(has_prev, prev_ref[...], 0.0)
    xs_ref[0:HALO, :] = prev
    xs_ref[HALO:HALO + tm, :] = cur_ref[...]


def _conv_taps(xs_ref, cw_ref, tm):
    y = cw_ref[0:1, :] * xs_ref[pl.ds(HALO - DN_CONV + 1, tm), :]
    for j in range(1, DN_CONV):
        y += cw_ref[j:j + 1, :] * xs_ref[pl.ds(HALO - DN_CONV + 1 + j, tm), :]
    return y


def _gate_values(ba, al, dt):
    beta = _sigmoid(ba)
    pre = ba + dt
    g = -jnp.exp(al) * _softplus(pre)
    return beta, g, pre


def _dn_prep_specs(tm, tile):
    return [pl.BlockSpec((tm, CONV_CH), lambda i: (tile(i), 0)),
            pl.BlockSpec((HALO, CONV_CH), lambda i: (jnp.maximum(tile(i) * (tm // HALO) - 1, 0), 0)),
            pl.BlockSpec((tm, 128), lambda i: (tile(i), 4 * DN_W // 128)),
            _full((DN_CONV, CONV_CH)), _full((1, 128)), _full((1, 128))]


def _dn_prep(pd, conv_w, al_vec, dt_vec, tm):
    t = pd.shape[0]

    def body(cur_ref, prev_ref, ba_ref, cw_ref, al_ref, dt_ref, qn_ref, kn_ref, vc_ref, gc_ref, gr_ref, xs_ref):
        _conv_window(cur_ref, prev_ref, xs_ref, tm, pl.program_id(0) > 0)
        y = _conv_taps(xs_ref, cw_ref, tm)
        c = y * _sigmoid(y)
        for h in range(DN_HEADS):
            qs = slice(h * DN_HEAD_DIM, (h + 1) * DN_HEAD_DIM)
            ksl = slice(DN_W + h * DN_HEAD_DIM, DN_W + (h + 1) * DN_HEAD_DIM)
            qh, kh = c[:, qs], c[:, ksl]
            qn_ref[:, qs] = qh * lax.rsqrt(jnp.sum(qh * qh, axis=-1, keepdims=True) + EPS) * DN_SCALE
            kn_ref[:, qs] = kh * lax.rsqrt(jnp.sum(kh * kh, axis=-1, keepdims=True) + EPS)
        vc_ref[...] = c[:, 2 * DN_W:3 * DN_W]
        beta, g, _ = _gate_values(ba_ref[...], al_ref[...], dt_ref[...])
        lane = lax.broadcasted_iota(jnp.int32, beta.shape, 1)
        gb = jnp.where(lane < DN_HEADS, beta, jnp.where(lane < 2 * DN_HEADS, g, 0.0))
        gc_ref[...] = gb
        gr_ref[...] = gb.T[0:8, :]

    tok = lambda w: pl.BlockSpec((tm, w), lambda i: (i, 0))
    return pl.pallas_call(
        body, name="dn_prep", grid=(t // tm,),
        in_specs=_dn_prep_specs(tm, lambda i: i),
        out_specs=[tok(DN_W), tok(DN_W), tok(DN_W), tok(128), pl.BlockSpec((8, tm), lambda i: (0, i))],
        out_shape=[jax.ShapeDtypeStruct((t, DN_W), F32)] * 3 + [jax.ShapeDtypeStruct((t, 128), F32),
                                                                 jax.ShapeDtypeStruct((8, t), F32)],
        scratch_shapes=[pltpu.VMEM((HALO + tm, CONV_CH), F32)],
        compiler_params=_params("parallel"),
    )(pd, pd, pd, conv_w, al_vec, dt_vec)


def _pair_masks():
    r = lax.broadcasted_iota(jnp.int32, (PAIR, PAIR), 0)
    c = lax.broadcasted_iota(jnp.int32, (PAIR, PAIR), 1)
    same = (r < DN_CHUNK) == (c < DN_CHUNK)
    return same & (r >= c), same & (r > c)


def _lane_col(mat, idx):
    lane = lax.broadcasted_iota(jnp.int32, mat.shape, 1)
    return jnp.sum(jnp.where(lane == idx, mat, 0.0), axis=-1, keepdims=True)


def _pair_cumsums(gc, gr, low):
    lowf = low.astype(F32)
    return _dot(lowf, gc, NN, HI), _dot(gr, lowf, NT, HI)


def _pair_gates(gc, cum_c, cum_r, low, h):
    beta = _lane_col(gc, h)
    gam = _lane_col(cum_c, DN_HEADS + h)
    gam_row = cum_r[DN_HEADS + h:DN_HEADS + h + 1, :]
    dm = jnp.where(low, jnp.exp(jnp.where(low, gam - gam_row, 0.0)), 0.0)
    row = lax.broadcasted_iota(jnp.int32, gam.shape, 0)
    gl = jnp.where(row < DN_CHUNK, gam[DN_CHUNK - 1:DN_CHUNK, :], gam[PAIR - 1:PAIR, :])
    return beta, gam, dm, gl


def _split(a):
    hi = a.astype(BF16)
    return hi, (a - hi.astype(F32)).astype(BF16)


def _dot_split(a, b, dims=NN):
    (ah, al), (bh, bl) = a, b
    la, lb = (1, 1) if dims == TN else ((0, 1) if dims == NN else (0, 0))
    r = _dot(jnp.concatenate([ah, al], axis=la), jnp.concatenate([bh, bl], axis=lb), dims)
    m, n = r.shape[0] // 2, r.shape[1] // 2
    return (r[m:, n:] + (r[:m, n:] + r[m:, :n])) + r[:m, :n]


def _unit_lower_inverses(lmats):
    n = lmats[0].shape[0]
    r = lax.broadcasted_iota(jnp.int32, (n, n), 0)
    c = lax.broadcasted_iota(jnp.int32, (n, n), 1)
    same = lambda size: (r & ~(size - 1)) == (c & ~(size - 1))
    base = DN_CHUNK // 4
    diag = [jnp.where(same(base), l, 0.0) for l in lmats]
    accs = [(r == c).astype(F32) - d for d in diag]
    splits = [_split(d) for d in diag]
    step = 1
    while 2 * step < base:
        splits = [_split(_dot_split(s, s)) for s in splits]
        accs = [acc + _dot_split(_split(acc), s) for acc, s in zip(accs, splits)]
        step *= 2
    size = base
    while size < DN_CHUNK:
        below = same(2 * size) & jnp.logical_not(same(size))
        tb = [_dot(acc, jnp.where(below, l, 0.0)) for acc, l in zip(accs, lmats)]
        accs = [acc - _dot(t, acc) for acc, t in zip(accs, tb)]
        size *= 2
    return accs


def _dn_intra(qn, kn, vc, gc, gr):
    t = qn.shape[0]
    npair = t // PAIR
    rows_step = INTRA_PAIRS * PAIR

    def body(q_ref, k_ref, v_ref, gc_ref, gr_ref, u_ref, w_ref, qg_ref, kd_ref, a_ref, ti_ref, dl_ref):
        low, strict = _pair_masks()
        items = []
        for p in range(INTRA_PAIRS):
            rows = slice(p * PAIR, (p + 1) * PAIR)
            gc_v = gc_ref[rows, :]
            cum_c, cum_r = _pair_cumsums(gc_v, gr_ref[:, rows], low)
            for h in range(DN_HEADS):
                hs = slice(h * DN_HEAD_DIM, (h + 1) * DN_HEAD_DIM)
                items.append((p, h, rows, hs, _pair_gates(gc_v, cum_c, cum_r, low, h)))
        lmats = []
        for p, h, rows, hs, (beta, gam, dm, gl) in items:
            k = k_ref[rows, hs]
            lmats.append(jnp.where(strict, _dot(k * beta, k, NT) * dm, 0.0))
        tinvs = _unit_lower_inverses(lmats)
        for (p, h, rows, hs, (beta, gam, dm, gl)), tinv in zip(items, tinvs):
            q, k, v = q_ref[rows, hs], k_ref[rows, hs], v_ref[rows, hs]
            eg = jnp.exp(gam)
            u_ref[rows, hs] = _dot(tinv, v * beta)
            w_ref[rows, hs] = _dot(tinv, (k * beta) * eg)
            a_ref[h, rows, :] = _dot(q, k, NT) * dm
            ti_ref[h, rows, :] = tinv
            qg_ref[rows, hs] = q * eg
            kd_ref[rows, hs] = k * jnp.exp(gl - gam)
            for c in range(2):
                last = (c + 1) * DN_CHUNK - 1
                dl_ref[2 * p + c, h] = jnp.broadcast_to(jnp.exp(gam[last:last + 1, :]), (8, 128))

    tok = lambda w: pl.BlockSpec((rows_step, w), lambda n: (n, 0))
    hm = pl.BlockSpec((DN_HEADS, rows_step, PAIR), lambda n: (0, n, 0))
    return pl.pallas_call(
        body, name="dn_intra", grid=(npair // INTRA_PAIRS,),
        in_specs=[tok(DN_W), tok(DN_W), tok(DN_W), tok(128), pl.BlockSpec((8, rows_step), lambda n: (0, n))],
        out_specs=[tok(DN_W)] * 4 + [hm, hm, pl.BlockSpec((2 * INTRA_PAIRS, DN_HEADS, 8, 128), lambda n: (n, 0, 0, 0))],
        out_shape=[jax.ShapeDtypeStruct((t, DN_W), F32)] * 4 + [jax.ShapeDtypeStruct((DN_HEADS, t, PAIR), F32)] * 2
                  + [jax.ShapeDtypeStruct((2 * npair, DN_HEADS, 8, 128), F32)],
        compiler_params=_params("parallel"),
    )(qn, kn, vc, gc, gr)


def _dn_scan_fwd(u, w, qg, kd, a_qk, dlast, pd, dn_w):
    t = u.shape[0]
    npair = t // PAIR

    def body(u_ref, w_ref, qg_ref, kd_ref, a_ref, dl_ref, z_ref, nw_ref, out_ref, o_ref, vn_ref, sall_ref, s_ref):
        @pl.when(pl.program_id(0) == 0)
        def _():
            s_ref[...] = jnp.zeros_like(s_ref)

        nw = nw_ref[...]
        for c in range(2):
            rows = slice(c * DN_CHUNK, (c + 1) * DN_CHUNK)
            for h in range(DN_HEADS):
                hs = slice(h * DN_HEAD_DIM, (h + 1) * DN_HEAD_DIM)
                st = s_ref[h]
                sall_ref[c, h] = st
                vn_ref[rows, hs] = u_ref[rows, hs] - _dot(w_ref[rows, hs], st)
            for h in range(DN_HEADS):
                hs = slice(h * DN_HEAD_DIM, (h + 1) * DN_HEAD_DIM)
                st, vn = s_ref[h], vn_ref[rows, hs]
                o = _dot(qg_ref[rows, hs], st) + _dot(a_ref[h, rows, rows], vn)
                s_ref[h] = st * dl_ref[c, h][0:1, :] + _dot(kd_ref[rows, hs], vn, TN)
                o_ref[rows, hs] = o
                z = z_ref[rows, hs]
                on = o * lax.rsqrt(jnp.mean(o * o, axis=-1, keepdims=True) + EPS) * nw
                out_ref[rows, hs] = (on * (z * _sigmoid(z))).astype(out_ref.dtype)

    tok = pl.BlockSpec((PAIR, DN_W), lambda n: (n, 0))
    hm = pl.BlockSpec((DN_HEADS, PAIR, PAIR), lambda n: (0, n, 0))
    return pl.pallas_call(
        body, name="dn_scan_fwd", grid=(npair,),
        in_specs=[tok, tok, tok, tok, hm, pl.BlockSpec((2, DN_HEADS, 8, 128), lambda n: (n, 0, 0, 0)),
                  pl.BlockSpec((PAIR, DN_W), lambda n: (n, 3)), _full((1, 128))],
        out_specs=[tok, tok, tok, pl.BlockSpec((2, DN_HEADS, DN_HEAD_DIM, DN_HEAD_DIM), lambda n: (n, 0, 0, 0))],
        out_shape=[jax.ShapeDtypeStruct((t, DN_W), MXU_DTYPE)] + [jax.ShapeDtypeStruct((t, DN_W), F32)] * 2
                  + [jax.ShapeDtypeStruct((2 * npair, DN_HEADS, DN_HEAD_DIM, DN_HEAD_DIM), F32)],
        scratch_shapes=[pltpu.VMEM((DN_HEADS, DN_HEAD_DIM, DN_HEAD_DIM), F32)],
        compiler_params=_params("arbitrary"),
    )(u, w, qg, kd, a_qk, dlast, pd, dn_w)


def _dn_scan_bwd(dout, o, vnew, sall, w, qg, kd, a_qk, dlast, pd, dn_w):
    t = o.shape[0]
    npair = t // PAIR
    rev = lambda n: npair - 1 - n

    def body(do_ref, o_ref, vn_ref, sall_ref, w_ref, qg_ref, kd_ref, a_ref, dl_ref, z_ref, nw_ref,
             dz_ref, du_ref, dw_ref, dqg_ref, dkd_ref, da_ref, ddl_ref, acc_ref, ds_ref, dos_ref):
        @pl.when(pl.program_id(0) == 0)
        def _():
            ds_ref[...] = jnp.zeros_like(ds_ref)
            acc_ref[...] = jnp.zeros_like(acc_ref)

        nw = nw_ref[...]
        dnw = jnp.zeros((1, 128), F32)
        for h in range(DN_HEADS):
            hs = slice(h * DN_HEAD_DIM, (h + 1) * DN_HEAD_DIM)
            o, z, dout = o_ref[:, hs], z_ref[:, hs], do_ref[:, hs]
            r = lax.rsqrt(jnp.mean(o * o, axis=-1, keepdims=True) + EPS)
            oh = o * r
            sz = _sigmoid(z)
            dz_ref[:, hs] = dout * (oh * nw) * (sz + z * sz * (1.0 - sz))
            don = dout * (z * sz)
            dnw += jnp.sum(don * oh, axis=0, keepdims=True)
            doh = don * nw
            dos_ref[:, hs] = r * (doh - oh * jnp.mean(doh * oh, axis=-1, keepdims=True))
        acc_ref[0:1, :] += dnw
        for c in (1, 0):
            rows = slice(c * DN_CHUNK, (c + 1) * DN_CHUNK)
            other = slice((1 - c) * DN_CHUNK, (2 - c) * DN_CHUNK)
            for h in range(DN_HEADS):
                hs = slice(h * DN_HEAD_DIM, (h + 1) * DN_HEAD_DIM)
                do, st, dsp, vn = dos_ref[rows, hs], sall_ref[c, h], ds_ref[h], vn_ref[rows, hs]
                da_ref[h, rows, rows] = _dot(do, vn, NT)
                da_ref[h, rows, other] = jnp.zeros((DN_CHUNK, DN_CHUNK), F32)
                du_ref[rows, hs] = _dot(a_ref[h, rows, rows], do, TN) + _dot(kd_ref[rows, hs], dsp)
                dqg_ref[rows, hs] = _dot(do, st, NT)
                dkd_ref[rows, hs] = _dot(vn, dsp, NT)
                ddl = jnp.sum(jnp.sum(dsp * st, axis=1, keepdims=True), axis=0, keepdims=True)
                ddl_ref[c, h] = jnp.broadcast_to(ddl, (8, 128))
            for h in range(DN_HEADS):
                hs = slice(h * DN_HEAD_DIM, (h + 1) * DN_HEAD_DIM)
                do, st, dvn = dos_ref[rows, hs], sall_ref[c, h], du_ref[rows, hs]
                dw_ref[rows, hs] = -_dot(dvn, st, NT)
                ds_ref[h] = (ds_ref[h] * dl_ref[c, h][0:1, :] + _dot(qg_ref[rows, hs], do, TN)
                             - _dot(w_ref[rows, hs], dvn, TN))

    tok = pl.BlockSpec((PAIR, DN_W), lambda n: (rev(n), 0))
    hm = pl.BlockSpec((DN_HEADS, PAIR, PAIR), lambda n: (0, rev(n), 0))
    sc = pl.BlockSpec((2, DN_HEADS, 8, 128), lambda n: (rev(n), 0, 0, 0))
    return pl.pallas_call(
        body, name="dn_scan_bwd", grid=(npair,),
        in_specs=[tok, tok, tok, pl.BlockSpec((2, DN_HEADS, DN_HEAD_DIM, DN_HEAD_DIM), lambda n: (rev(n), 0, 0, 0)),
                  tok, tok, tok, hm, sc, pl.BlockSpec((PAIR, DN_W), lambda n: (rev(n), 3)), _full((1, 128))],
        out_specs=[tok] * 5 + [hm, sc, _full((8, 128))],
        out_shape=[jax.ShapeDtypeStruct((t, DN_W), F32)] * 5 + [jax.ShapeDtypeStruct((DN_HEADS, t, PAIR), F32),
                   jax.ShapeDtypeStruct((2 * npair, DN_HEADS, 8, 128), F32), jax.ShapeDtypeStruct((8, 128), F32)],
        scratch_shapes=[pltpu.VMEM((DN_HEADS, DN_HEAD_DIM, DN_HEAD_DIM), F32), pltpu.VMEM((PAIR, DN_W), F32)],
        compiler_params=_params("arbitrary"),
    )(dout, o, vnew, sall, w, qg, kd, a_qk, dlast, pd, dn_w)


def _dn_intra_bwd(qn, kn, vc, gc, gr, tinv, a_qk, du, dw, dqg, dkd, da_qk, ddlast, dlast, dep):
    t = qn.shape[0]
    npair = t // PAIR

    def body(q_ref, k_ref, v_ref, gc_ref, gr_ref, ti_ref, a_ref, du_ref, dw_ref, dqg_ref, dkd_ref, da_ref, ddl_ref, dl_ref,
             dep_ref, dq_ref, dk_ref, dv_ref, dg_ref):
        low, strict = _pair_masks()
        lane = lax.broadcasted_iota(jnp.int32, (PAIR, 128), 1)
        rowi = lax.broadcasted_iota(jnp.int32, (PAIR, 1), 0)
        rsum = lambda v: jnp.sum(v, axis=-1, keepdims=True)
        items = []
        for p in range(INTRA_PAIRS):
            rows = slice(p * PAIR, (p + 1) * PAIR)
            gc_v = gc_ref[rows, :]
            cum_c, cum_r = _pair_cumsums(gc_v, gr_ref[:, rows], low)
            for h in range(DN_HEADS):
                hs = slice(h * DN_HEAD_DIM, (h + 1) * DN_HEAD_DIM)
                items.append((p, h, rows, hs, _pair_gates(gc_v, cum_c, cum_r, low, h)))
        dtis, lmats, dvbs, dkbgs = [], [], [], []
        for p, h, rows, hs, (beta, gam, dm, gl) in items:
            k, tinv = k_ref[rows, hs], ti_ref[h, rows, :]
            kb = k * beta
            dtis.append(_dot(du_ref[rows, hs], v_ref[rows, hs] * beta, NT)
                        + _dot(dw_ref[rows, hs], kb * jnp.exp(gam), NT))
            lmats.append(jnp.where(strict, _dot(kb, k, NT) * dm, 0.0))
            dvbs.append(_dot(tinv, du_ref[rows, hs], TN))
            dkbgs.append(_dot(tinv, dw_ref[rows, hs], TN))
        xs = [_dot(ti_ref[h, rows, :], dti, TN) for (p, h, rows, hs, g), dti in zip(items, dtis)]
        dls = [jnp.where(strict, -_dot(x, ti_ref[h, rows, :], NT), 0.0) for (p, h, rows, hs, g), x in zip(items, xs)]
        dgam_all = [jnp.zeros((PAIR, 128), F32) for _ in range(INTRA_PAIRS)]
        dbeta_all = [jnp.zeros((PAIR, 128), F32) for _ in range(INTRA_PAIRS)]
        for (p, h, rows, hs, (beta, gam, dm, gl)), dl, lmat, dvb, dkbg in zip(items, dls, lmats, dvbs, dkbgs):
            q, k, v = q_ref[rows, hs], k_ref[rows, hs], v_ref[rows, hs]
            a = a_ref[h, rows, :]
            dqg, dkd = dqg_ref[rows, hs], dkd_ref[rows, hs]
            kb = k * beta
            eg = jnp.exp(gam)
            ekd = jnp.exp(gl - gam)
            dmm = dl * dm
            dam = jnp.where(low, da_ref[h, rows, :], 0.0)
            dn = dam * dm
            e = dl * lmat + dam * a
            dkb = _dot(dmm, k) + dkbg * eg
            dk_ref[rows, hs] = _dot(dmm, kb, TN) + _dot(dn, q, TN) + dkd * ekd + dkb * beta
            dq_ref[rows, hs] = _dot(dn, k) + dqg * eg
            dv_ref[rows, hs] = dvb * beta
            t_kd = rsum(dkd * (k * ekd))
            dgam = rsum(e) - rsum(e.T) + rsum(dqg * (q * eg)) + rsum(dkbg * (kb * eg)) - t_kd
            for c in range(2):
                crows = slice(c * DN_CHUNK, (c + 1) * DN_CHUNK)
                dgl = (jnp.sum(t_kd[crows, :], axis=0, keepdims=True)
                       + ddl_ref[2 * p + c, h][0:1, 0:1] * dl_ref[2 * p + c, h][0:1, 0:1])
                dgam = dgam + jnp.where(rowi == (c + 1) * DN_CHUNK - 1, dgl, 0.0)
            dgam_all[p] += jnp.where(lane == DN_HEADS + h, dgam, 0.0)
            dbeta_all[p] += jnp.where(lane == h, rsum(dkb * k) + rsum(dvb * v), 0.0)
        for p in range(INTRA_PAIRS):
            dg_ref[p * PAIR:(p + 1) * PAIR, :] = dbeta_all[p] + _dot(low.astype(F32), dgam_all[p], TN, HI)

    rows_step = INTRA_PAIRS * PAIR
    tok = lambda w: pl.BlockSpec((rows_step, w), lambda n: (n, 0))
    hm = pl.BlockSpec((DN_HEADS, rows_step, PAIR), lambda n: (0, n, 0))
    sc = pl.BlockSpec((2 * INTRA_PAIRS, DN_HEADS, 8, 128), lambda n: (n, 0, 0, 0))
    return pl.pallas_call(
        body, name="dn_intra_bwd", grid=(npair // INTRA_PAIRS,),
        in_specs=[tok(DN_W), tok(DN_W), tok(DN_W), tok(128), pl.BlockSpec((8, rows_step), lambda n: (0, n)), hm, hm,
                  tok(DN_W), tok(DN_W), tok(DN_W), tok(DN_W), hm, sc, sc, pl.BlockSpec(memory_space=pl.ANY)],
        out_specs=[tok(DN_W), tok(DN_W), tok(DN_W), tok(128)],
        out_shape=[jax.ShapeDtypeStruct((t, DN_W), F32)] * 3 + [jax.ShapeDtypeStruct((t, 128), F32)],
        compiler_params=_params("parallel"),
    )(qn, kn, vc, gc, gr, tinv, a_qk, du, dw, dqg, dkd, da_qk, ddlast, dlast, dep)


def _dn_prep_bwd(pd, conv_w, al_vec, dt_vec, dqn, dkn, dvc, dgc, dz, tm):
    t = pd.shape[0]
    nt = t // tm
    tile = lambda i: nt - 1 - i

    def body(cur_ref, prev_ref, ba_ref, cw_ref, al_ref, dt_ref, dq_ref, dk_ref, dv_ref, dg_ref, dz_ref,
             o_ref, accw_ref, accg_ref, xs_ref, dc_ref, ds_ref, carry_ref):
        @pl.when(pl.program_id(0) == 0)
        def _():
            accw_ref[...] = jnp.zeros_like(accw_ref)
            accg_ref[...] = jnp.zeros_like(accg_ref)
            carry_ref[...] = jnp.zeros_like(carry_ref)

        _conv_window(cur_ref, prev_ref, xs_ref, tm, tile(pl.program_id(0)) > 0)
        y = _conv_taps(xs_ref, cw_ref, tm)
        sg = _sigmoid(y)
        c = y * sg
        for h in range(DN_HEADS):
            qs = slice(h * DN_HEAD_DIM, (h + 1) * DN_HEAD_DIM)
            ksl = slice(DN_W + h * DN_HEAD_DIM, DN_W + (h + 1) * DN_HEAD_DIM)
            for src, sl, scale in ((dq_ref, qs, DN_SCALE), (dk_ref, ksl, 1.0)):
                xh = c[:, sl]
                r = lax.rsqrt(jnp.sum(xh * xh, axis=-1, keepdims=True) + EPS)
                unit = xh * r
                dn = src[:, qs] * scale
                dc_ref[:, sl] = r * (dn - unit * jnp.sum(dn * unit, axis=-1, keepdims=True))
        dc_ref[:, 2 * DN_W:3 * DN_W] = dv_ref[...]
        dy = dc_ref[...] * (sg + y * sg * (1.0 - sg))
        for j in range(DN_CONV):
            accw_ref[j:j + 1, :] += jnp.sum(dy * xs_ref[pl.ds(HALO - DN_CONV + 1 + j, tm), :], axis=0, keepdims=True)
        ds_ref[0:tm, :] = dy
        ds_ref[tm:tm + HALO, :] = carry_ref[...]
        carry_ref[...] = ds_ref[0:HALO, :]
        dx = cw_ref[0:1, :] * ds_ref[pl.ds(DN_CONV - 1, tm), :]
        for j in range(1, DN_CONV):
            dx += cw_ref[j:j + 1, :] * ds_ref[pl.ds(DN_CONV - 1 - j, tm), :]

        beta, g, pre = _gate_values(ba_ref[...], al_ref[...], dt_ref[...])
        dgb = dg_ref[...]
        lane = lax.broadcasted_iota(jnp.int32, dgb.shape, 1)
        is_b, is_a = lane < DN_HEADS, (lane >= DN_HEADS) & (lane < 2 * DN_HEADS)
        dpre = dgb * (-jnp.exp(al_ref[...])) * _sigmoid(pre)
        dba = jnp.where(is_b, dgb * beta * (1.0 - beta), jnp.where(is_a, dpre, 0.0))
        accg_ref[0:1, :] += jnp.sum(jnp.where(is_a, dgb * g, 0.0), axis=0, keepdims=True)
        accg_ref[1:2, :] += jnp.sum(jnp.where(is_a, dpre, 0.0), axis=0, keepdims=True)
        o_ref[:, 0:CONV_CH] = dx.astype(o_ref.dtype)
        o_ref[:, CONV_CH:CONV_CH + DN_W] = dz_ref[...].astype(o_ref.dtype)
        o_ref[:, CONV_CH + DN_W:DN_COLS] = dba.astype(o_ref.dtype)

    tok = lambda w: pl.BlockSpec((tm, w), lambda i: (tile(i), 0))
    return pl.pallas_call(
        body, name="dn_prep_bwd", grid=(nt,),
        in_specs=_dn_prep_specs(tm, tile) + [tok(DN_W), tok(DN_W), tok(DN_W), tok(128), tok(DN_W)],
        out_specs=[tok(DN_COLS), _full((8, CONV_CH)), _full((8, 128))],
        out_shape=[jax.ShapeDtypeStruct((t, DN_COLS), MXU_DTYPE),
                   jax.ShapeDtypeStruct((8, CONV_CH), F32), jax.ShapeDtypeStruct((8, 128), F32)],
        scratch_shapes=[pltpu.VMEM((HALO + tm, CONV_CH), F32), pltpu.VMEM((tm, CONV_CH), F32),
                        pltpu.VMEM((tm + HALO, CONV_CH), F32), pltpu.VMEM((HALO, CONV_CH), F32)],
        compiler_params=_params("arbitrary"),
    )(pd, pd, pd, conv_w, al_vec, dt_vec, dqn, dkn, dvc, dgc, dz)


def _pad_lanes(v, offset=0):
    return jnp.zeros((1, 128), F32).at[0, offset:offset + v.shape[0]].set(v.astype(F32))


class _LocalReducer:
    def start(self, grads):
        return jnp.zeros((8, 128), F32)

    def middle(self, after):
        return jnp.zeros((8, 128), F32)

    def finish(self, after):
        return None


def _local_step(x, p, tgt, sm, w, late, reducer):
    t = x.shape[0]
    tm = min(512, t // 2)
    tm_s = min(256, t // 2)
    tw = min(1024, t // 2)

    w_in = w["w_in"]
    wa = w_in[:, :ATTN_Q + 2 * ATTN_KV]
    wd = jnp.pad(w_in[:, ATTN_Q + 2 * ATTN_KV:], ((0, 0), (0, DN_COLS - (D_IN - ATTN_Q - 2 * ATTN_KV))))
    conv_w = w["conv_w"]
    al_vec, dt_vec = _pad_lanes(sm["a_log"], DN_HEADS), _pad_lanes(sm["dt_bias"], DN_HEADS)
    sinks_vec = _pad_lanes(sm["sinks"])
    dn_w = sm["dn_norm"].reshape(1, 128)
    row = lambda v: v.reshape(1, D_MODEL)
    cos, sin = _rope_tables(t)

    u, pa, pd = _inproj(x, row(sm["norm_mix"]), wa, wd, tm_s)
    ao = _attn_fwd(pa, cos, sin, sinks_vec)
    qn, kn, vc, gc, gr = _dn_prep(pd, conv_w, al_vec, dt_vec, tm_s)
    uu, ww, qg, kd, a_qk, tinv, dlast = _dn_intra(qn, kn, vc, gc, gr)
    dn_out, o, vnew, sall = _dn_scan_fwd(uu, ww, qg, kd, a_qk, dlast, pd, dn_w)
    w_o, late_rest = late(dn_out)
    wo_a, wo_d = w_o[:ATTN_Q], w_o[ATTN_Q:]
    h1 = _oproj(x, ao, dn_out, wo_a, wo_d, tm)
    w = dict(w, **late_rest(h1))
    w_proj = jnp.transpose(w["w_proj4"], (1, 0, 2)).reshape(PLE_DIM, D_MODEL)
    m, r, h2 = _mlp_fwd(h1, row(sm["norm_mlp"]), w["w_up4"], w["w_down"], tm)
    dh2, dh2b, dgp, dpp, n3, pb, acc_ple = _ple_loss(h2, p, tgt, row(sm["norm_ple"]), row(sm["norm_final"]),
                                                     w["w_gate"], w_proj, tm_s)
    g_w_gate = _wgrad(n3, dgp, "wgrad_gate", D_MODEL, D_MODEL, tw)
    g_w_proj = _wgrad(pb, dpp, "wgrad_proj", PLE_DIM, D_MODEL, tw)
    da, dh1, dh1b, acc_mlp = _mlp_bwd(dh2, dh2b, r, h1, row(sm["norm_mlp"]), w["w_up4"], w["w_down"], tm)
    g_w_up4 = _wgrad(m, da, "wgrad_up", D_MODEL, FF_BLOCK, tw, stacked=True)
    g_w_down = _wgrad(r, dh2b, "wgrad_down", FF_BLOCK, D_MODEL, tw,
                      prep=lambda rv: jnp.square(rv.astype(F32)).astype(MXU_DTYPE))
    g_w_o = _wgrad_cat([ao, dn_out], [dh1b], "wgrad_o", tw)
    early = dict(w_up4=g_w_up4, w_down=g_w_down, w_gate=g_w_gate, w_proj=g_w_proj, w_o=g_w_o)
    dep = reducer.start(early)
    dao, ddn = _oproj_bwd(dh1b, wo_a, wo_d, tm, dep)
    dz, du, dw, dqg, dkd, da_qk, ddlast, acc_dn = _dn_scan_bwd(ddn, o, vnew, sall, ww, qg, kd, a_qk, dlast, pd, dn_w)
    dep = reducer.middle(du)
    dqn, dkn, dvc, dgc = _dn_intra_bwd(qn, kn, vc, gc, gr, tinv, a_qk, du, dw, dqg, dkd, da_qk, ddlast, dlast, dep)
    d_dn, acc_conv, acc_gate = _dn_prep_bwd(pd, conv_w, al_vec, dt_vec, dqn, dkn, dvc, dgc, dz, tm_s)
    dq, dk, dv, acc_attn = _attn_bwd(pa, cos, sin, sinks_vec, dao)
    reducer.finish(dq)
    wq, wk, wv = wa[:, :ATTN_Q], wa[:, ATTN_Q:ATTN_Q + ATTN_KV], wa[:, ATTN_Q + ATTN_KV:]
    dx, acc_mix = _inproj_bwd(x, dh1, row(sm["norm_mix"]), [dq, dk, dv, d_dn], [wq, wk, wv, wd], tm_s)

    g_w_in = _wgrad_cat([u], [dq, dk, dv, d_dn], "wgrad_in", tw)[:, :D_IN]
    grads = dict(early, w_in=g_w_in)
    sums = dict(loss=acc_ple[2, 0], norm_final=acc_ple[0], norm_ple=acc_ple[1], norm_mlp=acc_mlp[0], norm_mix=acc_mix[0],
                dn_norm=acc_dn[0], sinks=acc_attn[0, :ATTN_HEADS], a_log=acc_gate[0, DN_HEADS:2 * DN_HEADS],
                dt_bias=acc_gate[1, DN_HEADS:2 * DN_HEADS], conv_w=acc_conv[:DN_CONV])
    return sums, dx, grads


MESH = pl.DeviceIdType.MESH
ANY = pl.BlockSpec(memory_space=pl.ANY)
N_CHIPS = 4
N_DEV = 8


def _place():
    x, y, c = lax.axis_index("x"), lax.axis_index("y"), lax.axis_index("c")
    chips = [(1 - x, y), (x, 1 - y), (1 - x, 1 - y)]
    return x, y, c, chips


def _gather_weights(shards, conv_s):
    n = len(shards)
    per = 7

    def body(*refs):
        in_refs, conv_ref = refs[:n], refs[n]
        out_refs, conv_out = refs[n + 1:2 * n + 1], refs[2 * n + 1]
        send_sems, recv_sems = refs[2 * n + 2:]
        x, y, c, chips = _place()
        sibling = (x, y, 1 - c)

        def blk(a, px, py, pc):
            hr = in_refs[a].shape[0] // 2
            return out_refs[a].at[2 * px + py, pl.ds(pc * hr, hr), :]

        def mine(a):
            hr = in_refs[a].shape[0] // 2
            return in_refs[a].at[pl.ds(c * hr, hr), :]

        def rcopy(a, k, block, to, src=None):
            return pltpu.make_async_remote_copy(
                src_ref=blk(a, *block) if src is None else src, dst_ref=blk(a, *block),
                send_sem=send_sems.at[per * a + k], recv_sem=recv_sems.at[per * a + k],
                device_id=to, device_id_type=MESH)

        def whole(a, to):
            return pltpu.make_async_remote_copy(
                src_ref=in_refs[a], dst_ref=out_refs[a].at[2 * x + y],
                send_sem=send_sems.at[per * a], recv_sem=recv_sems.at[per * a], device_id=to, device_id_type=MESH)

        def ccopy(j, to):
            return pltpu.make_async_remote_copy(
                src_ref=conv_ref, dst_ref=conv_out.at[2 * x + y],
                send_sem=send_sems.at[per * n + j], recv_sem=recv_sems.at[per * n + j],
                device_id=to, device_id_type=MESH)

        started = []
        for a in range(n):
            first = [whole(a, sibling)]
            first += [rcopy(a, 1 + j, (x, y, c), (*chip, c), src=mine(a)) for j, chip in enumerate(chips)]
            for cp in first:
                cp.start()
            started += first
        conv_sends = [ccopy(j, (*chip, c)) for j, chip in enumerate(chips)] + [ccopy(3, sibling)]
        for cp in conv_sends:
            cp.start()
        started += conv_sends
        for a in range(n):
            for j, chip in enumerate(chips):
                rcopy(a, 1 + j, (*chip, c), (x, y, c)).wait_recv()
                fwd = rcopy(a, 4 + j, (*chip, c), sibling)
                fwd.start()
                started.append(fwd)
        for a in range(n):
            whole(a, sibling).wait_recv()
            for j, chip in enumerate(chips):
                rcopy(a, 4 + j, (*chip, 1 - c), (x, y, c)).wait_recv()
        for j, chip in enumerate(chips + [(x, y)]):
            pltpu.make_async_remote_copy(
                src_ref=conv_ref, dst_ref=conv_out.at[2 * chip[0] + chip[1]],
                send_sem=send_sems.at[per * n + j], recv_sem=recv_sems.at[per * n + j],
                device_id=sibling, device_id_type=MESH).wait_recv()
        for cp in started:
            cp.wait_send()

    nsem = per * n + 4
    out_shape = [jax.ShapeDtypeStruct((N_CHIPS,) + s.shape, s.dtype) for s in shards]
    out_shape.append(jax.ShapeDtypeStruct((N_CHIPS,) + conv_s.shape, conv_s.dtype))
    return pl.pallas_call(
        body, name="gather_weights", in_specs=[ANY] * (n + 1), out_specs=[ANY] * (n + 1), out_shape=out_shape,
        scratch_shapes=[pltpu.SemaphoreType.DMA((nsem,)), pltpu.SemaphoreType.DMA((nsem,))],
    )(*shards, conv_s)


HBM = pl.BlockSpec(memory_space=pltpu.HBM)
SEM = pl.BlockSpec(memory_space=pltpu.SEMAPHORE)
EFFECT = pltpu.SideEffectType.DATAFLOW_SIDE_EFFECTING
LATE_COPIES = 7


def _late_copies(in_refs, land_refs, send_sems, recv_sems, only=None):
    x, y, c, chips = _place()
    sends, arrivals = [], []
    for a, (src, land) in enumerate(zip(in_refs, land_refs)):
        if only is not None and a not in only:
            continue
        hr = src.shape[0] // 2
        base = LATE_COPIES * a

        def cp(src_ref, dst_ref, s_idx, r_idx, to):
            return pltpu.make_async_remote_copy(src_ref=src_ref, dst_ref=dst_ref, send_sem=send_sems.at[base + s_idx],
                                                recv_sem=recv_sems.at[base + r_idx], device_id=to, device_id_type=MESH)

        sends.append(cp(src, land.at[2 * x + y], 0, 0, (x, y, 1 - c)))
        arrivals.append(cp(src, land.at[2 * x + y], 0, 0, (x, y, 1 - c)))
        for j, chip in enumerate(chips):
            for pc in range(2):
                half = src.at[pl.ds(c * hr, hr), :]
                sends.append(cp(half, land.at[2 * x + y, pl.ds(c * hr, hr), :], 1 + 2 * j + pc, 1 + 2 * j + c, (*chip, pc)))
                arrivals.append(cp(half, land.at[2 * chip[0] + chip[1], pl.ds(pc * hr, hr), :], 1 + 2 * j + pc,
                                   1 + 2 * j + pc, (*chip, pc)))
    return sends, arrivals


def _copies_start(name, build, nsem, srcs, land_shapes, after):
    n = len(srcs)

    def body(*refs):
        sends, _ = build(refs[:n], refs[n:2 * n], refs[2 * n + 1], refs[2 * n + 2])
        for cp in sends:
            cp.start()
        refs[-1][...] = jnp.zeros_like(refs[-1])

    lands = [pltpu.with_memory_space_constraint(lax.empty(s.shape, s.dtype), pltpu.HBM) for s in land_shapes]
    ins = [pltpu.with_memory_space_constraint(s, pltpu.HBM) for s in srcs]
    out = pl.pallas_call(
        body, name=name,
        out_shape=(pltpu.SemaphoreType.DMA((nsem,)), pltpu.SemaphoreType.DMA((nsem,)),
                   *[pltpu.HBM(s.shape, s.dtype) for s in srcs], *[pltpu.HBM(s.shape, s.dtype) for s in land_shapes],
                   jax.ShapeDtypeStruct((8, 128), F32)),
        in_specs=[HBM] * (2 * n) + [ANY],
        out_specs=(SEM, SEM, *[HBM] * (2 * n), pl.BlockSpec(memory_space=pltpu.VMEM)),
        input_output_aliases={i: 2 + i for i in range(2 * n)},
        compiler_params=pltpu.CompilerParams(has_side_effects=EFFECT),
    )(*ins, *lands, after)
    return out[0], out[1], out[2:2 + n], out[2 + n:2 + 2 * n], out[-1]


def _copies_wait(name, build, started, after):
    send_sems, recv_sems, srcs, lands, _ = started
    n = len(srcs)

    def body(*refs):
        sends, arrivals = build(refs[:n], refs[n:2 * n], refs[2 * n], refs[2 * n + 1])
        for cp in sends:
            cp.wait_send()
        for cp in arrivals:
            cp.wait_recv()

    out = pl.pallas_call(
        body, name=name,
        out_shape=(*[pltpu.HBM(s.shape, s.dtype) for s in srcs], *[pltpu.HBM(l.shape, l.dtype) for l in lands]),
        in_specs=[HBM] * (2 * n) + [SEM, SEM, ANY],
        out_specs=tuple([HBM] * (2 * n)),
        input_output_aliases={i: i for i in range(2 * n)},
        compiler_params=pltpu.CompilerParams(has_side_effects=EFFECT),
    )(*srcs, *lands, send_sems, recv_sems, after)
    return out[:n], out[n:]


def _exchange_copies(g_refs, got_refs, send_sems, recv_sems):
    x, y, c, _ = _place()
    sends, arrivals = [], []
    for a, (g, got) in enumerate(zip(g_refs, got_refs)):
        hr = g.shape[1] // 2
        cp = pltpu.make_async_remote_copy(
            src_ref=g.at[:, pl.ds((1 - c) * hr, hr), :], dst_ref=got, send_sem=send_sems.at[a],
            recv_sem=recv_sems.at[a], device_id=(x, y, 1 - c), device_id_type=MESH)
        sends.append(cp)
        arrivals.append(cp)
    return sends, arrivals


def _scatter_copies(s_refs, got_refs, send_sems, recv_sems):
    x, y, c, chips = _place()
    sends, arrivals = [], []
    for a, (s16, got) in enumerate(zip(s_refs, got_refs)):
        for j, chip in enumerate(chips):
            cp = pltpu.make_async_remote_copy(
                src_ref=s16.at[2 * chip[0] + chip[1]], dst_ref=got.at[j], send_sem=send_sems.at[3 * a + j],
                recv_sem=recv_sems.at[3 * a + j], device_id=(*chip, c), device_id_type=MESH)
            sends.append(cp)
            arrivals.append(cp)
    return sends, arrivals


def _share_halves(name, bufs, dep):
    n = len(bufs)

    def body(*refs):
        out_refs = refs[n + 1:2 * n + 1]
        send_sems, recv_sems = refs[2 * n + 1:]
        x, y, c, _ = _place()
        remote = [pltpu.make_async_remote_copy(
            src_ref=out_refs[a].at[c], dst_ref=out_refs[a].at[c], send_sem=send_sems.at[a], recv_sem=recv_sems.at[a],
            device_id=(x, y, 1 - c), device_id_type=MESH) for a in range(n)]
        for cp in remote:
            cp.start()
        for a in range(n):
            pltpu.make_async_remote_copy(
                src_ref=out_refs[a].at[c], dst_ref=out_refs[a].at[1 - c], send_sem=send_sems.at[a],
                recv_sem=recv_sems.at[a], device_id=(x, y, 1 - c), device_id_type=MESH).wait_recv()
        for cp in remote:
            cp.wait_send()

    return pl.pallas_call(
        body, name=name, in_specs=[ANY] * (n + 1), out_specs=[ANY] * n,
        out_shape=[jax.ShapeDtypeStruct(b.shape, b.dtype) for b in bufs],
        input_output_aliases={a: a for a in range(n)},
        scratch_shapes=[pltpu.SemaphoreType.DMA((n,)), pltpu.SemaphoreType.DMA((n,))],
    )(*bufs, dep)


SMALL_ROWS, SMALL_COLS = 16, CONV_CH


def _allreduce_small(block):
    m_per, ncol = block.shape

    def body(x_ref, sum_ref, all_ref, send_sems, recv_sems, local_sem):
        x, y, c, chips = _place()
        me, sibling = (x, y, c), (x, y, 1 - c)

        def rows(px, py, pc):
            return all_ref.at[pl.ds((4 * px + 2 * py + pc) * m_per, m_per), :]

        def copy(k, block_of, to, src=None):
            return pltpu.make_async_remote_copy(
                src_ref=rows(*block_of) if src is None else src, dst_ref=rows(*block_of),
                send_sem=send_sems.at[k], recv_sem=recv_sems.at[k], device_id=to, device_id_type=MESH)

        mine = pltpu.make_async_copy(x_ref, rows(*me), local_sem)
        mine.start()
        first = [copy(0, me, sibling, src=x_ref)]
        first += [copy(1 + j, me, (*chip, c), src=x_ref) for j, chip in enumerate(chips)]
        for cp in first:
            cp.start()
        passed = [copy(4 + j, (*chip, c), sibling) for j, chip in enumerate(chips)]
        for j, chip in enumerate(chips):
            copy(1 + j, (*chip, c), me).wait_recv()
            passed[j].start()
        copy(0, sibling, me).wait_recv()
        for j, chip in enumerate(chips):
            copy(4 + j, (*chip, 1 - c), me).wait_recv()
        for cp in first + passed:
            cp.wait_send()
        mine.wait()
        total = all_ref[0:m_per, :]
        for d in range(1, N_DEV):
            total = total + all_ref[d * m_per:(d + 1) * m_per, :]
        sum_ref[...] = total

    vm = pl.BlockSpec(memory_space=pltpu.VMEM)
    return pl.pallas_call(
        body, name="allreduce_small", in_specs=[vm], out_specs=vm,
        out_shape=jax.ShapeDtypeStruct((m_per, ncol), F32),
        scratch_shapes=[pltpu.VMEM((N_DEV * m_per, ncol), F32), pltpu.SemaphoreType.DMA((7,)),
                        pltpu.SemaphoreType.DMA((7,)), pltpu.SemaphoreType.DMA],
    )(block)


def _row_tile(rows, cols):
    tile = rows
    while tile * cols * 4 > (1 << 20) and tile % 16 == 0:
        tile //= 2
    return tile


def _elementwise(fn, name, ins, out_dtypes, dep):
    rows, cols = ins[0].shape
    tile = _row_tile(rows, cols)

    def body(*refs):
        outs = fn(*[r[...] for r in refs[:len(ins)]])
        for o_ref, o in zip(refs[len(ins) + 1:], outs):
            o_ref[...] = o.astype(o_ref.dtype)

    spec = pl.BlockSpec((tile, cols), lambda i: (i, 0))
    return pl.pallas_call(
        body, name=name, grid=(rows // tile,), in_specs=[spec] * len(ins) + [pl.BlockSpec(memory_space=pl.ANY)],
        out_specs=[spec] * len(out_dtypes),
        out_shape=[jax.ShapeDtypeStruct((rows, cols), d) for d in out_dtypes],
        compiler_params=_params("parallel"),
    )(*ins, dep)


def _adamw_tile(w, g, m, v):
    m = ADAM_B1 * m + (1.0 - ADAM_B1) * g
    v = ADAM_B2 * v + (1.0 - ADAM_B2) * jnp.square(g)
    m_hat = m / (1.0 - ADAM_B1 ** ADAM_STEP)
    v_hat = v / (1.0 - ADAM_B2 ** ADAM_STEP)
    delta = -ADAM_LR * (m_hat / (jnp.sqrt(v_hat) + ADAM_EPS) + ADAM_WD * w)
    return delta, m, v


def _adamw(name, w, g, m, v, dep):
    return _elementwise(_adamw_tile, name, [w, g, m, v], [F32, F32, F32], dep)


def _chip_sum(name, g4, got, place):
    nchip, hr, cols = got.shape
    tile = _row_tile(hr, cols)
    nblk = hr // tile

    def body(pl_ref, g_ref, o_ref, s32_ref, s16_ref):
        s = g_ref[...] + o_ref[...]
        s32_ref[...] = s
        s16_ref[...] = s.astype(BF16)

    spec = pl.BlockSpec((None, tile, cols), lambda k, i, pr: (k, i, 0))
    return pl.pallas_call(
        body, name=name,
        grid_spec=pltpu.PrefetchScalarGridSpec(
            num_scalar_prefetch=1, grid=(nchip, nblk),
            in_specs=[pl.BlockSpec((None, tile, cols), lambda k, i, pr: (k, pr[1] * nblk + i, 0)), spec],
            out_specs=[spec, spec]),
        out_shape=[jax.ShapeDtypeStruct(got.shape, F32), jax.ShapeDtypeStruct(got.shape, BF16)],
        compiler_params=_params("parallel", "parallel"),
    )(place, g4, got)


def _mesh_sum(name, s32, got, place):
    _, hr, cols = s32.shape
    tile = _row_tile(hr, cols)

    def body(pl_ref, own_ref, g0_ref, g1_ref, g2_ref, o_ref):
        o_ref[...] = ((own_ref[...] + g0_ref[...].astype(F32)) + g1_ref[...].astype(F32)) + g2_ref[...].astype(F32)

    slab = lambda j: pl.BlockSpec((None, tile, cols), lambda i, pr: (j, i, 0))
    return pl.pallas_call(
        body, name=name,
        grid_spec=pltpu.PrefetchScalarGridSpec(
            num_scalar_prefetch=1, grid=(hr // tile,),
            in_specs=[pl.BlockSpec((None, tile, cols), lambda i, pr: (pr[0], i, 0)), slab(0), slab(1), slab(2)],
            out_specs=pl.BlockSpec((None, tile, cols), lambda i, pr: (pr[1], i, 0))),
        out_shape=jax.ShapeDtypeStruct((2, hr, cols), F32),
        compiler_params=_params("parallel"),
    )(place, s32, got, got, got)


def _place_operand():
    return jnp.stack([2 * lax.axis_index("x") + lax.axis_index("y"), lax.axis_index("c")]).astype(jnp.int32)


def _per_chip(name, g):
    if name == "w_in":
        return jnp.transpose(g.reshape(D_MODEL, N_CHIPS, D_IN // N_CHIPS), (1, 0, 2))
    if name == "w_proj":
        return jnp.transpose(g.reshape(PLE_DIM, N_CHIPS, D_MODEL // N_CHIPS), (1, 0, 2))
    if name == "w_up4":
        return g
    return g.reshape(N_CHIPS, g.shape[0] // N_CHIPS, g.shape[1])


class _EarlyReducer:
    def __init__(self, tag):
        self.tag = tag

    def start(self, grads):
        self.names = list(grads)
        self.place = _place_operand()
        slabs = [_per_chip(k, grads[k]) for k in self.names]
        halves = [jax.ShapeDtypeStruct((s.shape[0], s.shape[1] // 2, s.shape[2]), F32) for s in slabs]
        self.a = _copies_start(self.tag + "exchange_start", _exchange_copies, len(slabs), slabs, halves,
                               slabs[0][0, :8, :128])
        return self.a[-1]

    def middle(self, after):
        slabs, got = _copies_wait(self.tag + "exchange_wait", _exchange_copies, self.a, after)
        self.sums = [_chip_sum(self.tag + "chip_sum_" + k, s, g, self.place) for k, s, g in zip(self.names, slabs, got)]
        s16 = [s[1] for s in self.sums]
        lands = [jax.ShapeDtypeStruct((3,) + s.shape[1:], BF16) for s in s16]
        self.b = _copies_start(self.tag + "scatter_start", _scatter_copies, 3 * len(s16), s16, lands,
                               self.sums[0][0][0, :8, :128])
        return self.b[-1]

    def finish(self, after):
        _, got = _copies_wait(self.tag + "scatter_wait", _scatter_copies, self.b, after)
        self.bufs = {k: _mesh_sum(self.tag + "mesh_sum_" + k, s[0], g, self.place)
                     for k, s, g in zip(self.names, self.sums, got)}


def kernel(x, p, norm_mix, w_in, conv_w, a_log, dt_bias, dn_norm, sinks, w_o, norm_mlp, w_up, w_down, norm_ple, w_ple_gate, w_ple_proj, norm_final, loss_target, m_norm_mix, m_w_in, m_conv_w, m_a_log, m_dt_bias, m_dn_norm, m_sinks, m_w_o, m_norm_mlp, m_w_up, m_w_down, m_norm_ple, m_w_ple_gate, m_w_ple_proj, m_norm_final, v_norm_mix, v_w_in, v_conv_w, v_a_log, v_dt_bias, v_dn_norm, v_sinks, v_w_o, v_norm_mlp, v_w_up, v_w_down, v_norm_ple, v_w_ple_gate, v_w_ple_proj, v_norm_final):
    chip = 2 * lax.axis_index("x") + lax.axis_index("y")
    big = dict(w_in=w_in[0], w_o=w_o[0], w_up=w_up[0], w_down=w_down[0], w_gate=w_ple_gate[0], w_proj=w_ple_proj[0])
    big_m = dict(w_in=m_w_in[0], w_o=m_w_o[0], w_up=m_w_up[0], w_down=m_w_down[0], w_gate=m_w_ple_gate[0], w_proj=m_w_ple_proj[0])
    big_v = dict(w_in=v_w_in[0], w_o=v_w_o[0], w_up=v_w_up[0], w_down=v_w_down[0], w_gate=v_w_ple_gate[0], w_proj=v_w_ple_proj[0])
    names = list(big)

    w_in_all, conv_all = _gather_weights([big["w_in"].astype(BF16)], conv_w[0])
    late_names = names[1:]
    late_shards = [big[k].astype(BF16) for k in late_names]
    gather = _copies_start("gather_start", _late_copies, LATE_COPIES * len(late_shards), late_shards,
                           [jax.ShapeDtypeStruct((N_CHIPS,) + s.shape, BF16) for s in late_shards], w_in_all)
    token = gather[-1]
    w = dict(w_in=jnp.transpose(w_in_all, (1, 0, 2)).reshape(D_MODEL, D_IN),
             conv_w=jnp.transpose(conv_all, (1, 0, 2)).reshape(DN_CONV, CONV_CH))
    sm = dict(norm_mix=norm_mix[0] + token[0, 0], a_log=a_log[0], dt_bias=dt_bias[0], dn_norm=dn_norm[0],
              sinks=sinks[0], norm_mlp=norm_mlp[0], norm_ple=norm_ple[0], norm_final=norm_final)

    def late(after):
        first = functools.partial(_late_copies, only=(0,))
        srcs, lands = _copies_wait("gather_wait_o", first, gather, after)

        def rest(after2):
            others = functools.partial(_late_copies, only=tuple(range(1, len(late_names))))
            gw = dict(zip(late_names, _copies_wait("gather_wait_rest", others, gather[:2] + (srcs, lands, None), after2)[1]))
            return dict(w_up4=gw["w_up"], w_down=gw["w_down"].reshape(D_FF, D_MODEL),
                        w_gate=gw["w_gate"].reshape(D_MODEL, D_MODEL), w_proj4=gw["w_proj"])

        return lands[0].reshape(D_MODEL, D_MODEL), rest

    reducer = _EarlyReducer("early_")
    sums, grad_x, g = _local_step(x[0], p[0, 0], loss_target[0], sm, w, late, reducer)

    last = _EarlyReducer("last_")
    dep_a = last.start({"w_in": g["w_in"]})

    row = lambda v: jnp.zeros((SMALL_COLS,), F32).at[:v.shape[0]].set(v)
    misc = jnp.zeros((SMALL_COLS,), F32).at[0:4].set(sums["a_log"]).at[4:8].set(sums["dt_bias"]) \
        .at[8:16].set(sums["sinks"]).at[128:256].set(sums["dn_norm"]).at[256].set(sums["loss"])
    small = jnp.concatenate([sums["conv_w"], jnp.stack([row(sums["norm_mix"]), row(sums["norm_mlp"]), row(sums["norm_ple"]),
                                                        row(sums["norm_final"]), misc]),
                             jnp.zeros((SMALL_ROWS - 9, SMALL_COLS), F32)], axis=0)
    tot = _allreduce_small(small + dep_a[0, 0])
    dep_b = last.middle(tot)
    grad_key = dict(w_o="w_o", w_up="w_up4", w_down="w_down", w_gate="w_gate", w_proj="w_proj")
    full = _share_halves("share_halves", [reducer.bufs[grad_key[k]] for k in late_names], dep_b)
    red = {k: f.reshape(-1, f.shape[-1]) for k, f in zip(late_names, full)}
    loss = tot[8, 256]
    ncw = CONV_CH // N_CHIPS

    def pack(cw, nmix, nmlp, nple, nfin, al, dtb, sk, dnn):
        misc_p = jnp.zeros((SMALL_COLS,), F32).at[0:4].set(al).at[4:8].set(dtb).at[8:16].set(sk).at[128:256].set(dnn)
        cw_p = jnp.zeros((DN_CONV, SMALL_COLS), F32).at[:, :ncw].set(cw)
        return jnp.concatenate([cw_p, jnp.stack([row(nmix), row(nmlp), row(nple), row(nfin), misc_p]),
                                jnp.zeros((SMALL_ROWS - 9, SMALL_COLS), F32)], axis=0)

    def unpack(buf):
        return dict(conv_w=buf[0:4, :ncw][None], norm_mix=buf[4, :D_MODEL][None], norm_mlp=buf[5, :D_MODEL][None],
                    norm_ple=buf[6, :D_MODEL][None], norm_final=buf[7, :D_MODEL], a_log=buf[8, 0:4][None],
                    dt_bias=buf[8, 4:8][None], sinks=buf[8, 8:16][None], dn_norm=buf[8, 128:256][None])

    g_conv_shard = lax.dynamic_slice(tot[0:4], (0, chip * ncw), (DN_CONV, ncw))
    g_small = pack(g_conv_shard, tot[4, :D_MODEL], tot[5, :D_MODEL], tot[6, :D_MODEL], tot[7, :D_MODEL],
                   tot[8, 0:4], tot[8, 4:8], tot[8, 8:16], tot[8, 128:256])
    w_small = pack(conv_w[0], norm_mix[0], norm_mlp[0], norm_ple[0], norm_final, a_log[0], dt_bias[0], sinks[0], dn_norm[0])
    m_small = pack(m_conv_w[0], m_norm_mix[0], m_norm_mlp[0], m_norm_ple[0], m_norm_final, m_a_log[0], m_dt_bias[0],
                   m_sinks[0], m_dn_norm[0])
    v_small = pack(v_conv_w[0], v_norm_mix[0], v_norm_mlp[0], v_norm_ple[0], v_norm_final, v_a_log[0], v_dt_bias[0],
                   v_sinks[0], v_dn_norm[0])

    ref_name = dict(w_in="w_in", w_o="w_o", w_up="w_up", w_down="w_down", w_gate="w_ple_gate", w_proj="w_ple_proj")
    out_g, out_d, out_m, out_v = {}, {}, {}, {}

    def update(k, dep):
        d_k, m_k, v_k = _adamw("adamw_" + k, big[k], red[k], big_m[k], big_v[k], dep)
        out_g[ref_name[k]], out_d[ref_name[k]] = red[k][None], d_k[None]
        out_m[ref_name[k]], out_v[ref_name[k]] = m_k[None], v_k[None]
        return d_k

    for k in late_names:
        done = update(k, dep_b)
    small_out = _adamw("adamw_small", w_small, g_small, m_small, v_small, dep_b)
    d_s, m_s, v_s = (unpack(b) for b in small_out)
    g_s = unpack(g_small)
    for src, dst in ((g_s, out_g), (d_s, out_d), (m_s, out_m), (v_s, out_v)):
        dst.update(src)
    last.finish(done + small_out[0][0:1, 0:1])
    (w_in_full,) = _share_halves("share_halves_w_in", [last.bufs["w_in"]], dep_b)
    red["w_in"] = w_in_full.reshape(-1, w_in_full.shape[-1])
    update("w_in", dep_b)
    order = ["norm_mix", "w_in", "conv_w", "a_log", "dt_bias", "dn_norm", "sinks", "w_o", "norm_mlp", "w_up", "w_down",
             "norm_ple", "w_ple_gate", "w_ple_proj", "norm_final"]
    return (loss, grad_x[None], *[out_g[k] for k in order], *[out_d[k] for k in order],
            *[out_m[k] for k in order], *[out_v[k] for k in order])
```

```python
import functools

import jax
import jax.numpy as jnp
from jax import lax
from jax.experimental import pallas as pl
from jax.experimental.pallas import tpu as pltpu

F32 = jnp.float32
BF16 = jnp.bfloat16
MXU_DTYPE = jnp.bfloat16
HI = lax.Precision.HIGHEST

D_MODEL = 1024
PLE_DIM = 256
ATTN_HEADS = 8
ATTN_KV_HEADS = 2
ATTN_GROUPS = ATTN_HEADS // ATTN_KV_HEADS
ATTN_HEAD_DIM = 64
ATTN_BLOCK = 128
ROPE_THETA = 10000.0
DN_HEADS = 4
DN_HEAD_DIM = 128
DN_CONV = 4
DN_CHUNK = 64
D_FF = 4 * D_MODEL
EPS = 1e-6
ATTN_Q = ATTN_HEADS * ATTN_HEAD_DIM
ATTN_KV = ATTN_KV_HEADS * ATTN_HEAD_DIM
DN_W = DN_HEADS * DN_HEAD_DIM
CONV_CH = 3 * DN_W
D_IN = ATTN_Q + 2 * ATTN_KV + 4 * DN_W + 2 * DN_HEADS
DN_COLS = 4 * DN_W + 128
DN_SCALE = DN_HEAD_DIM ** -0.5
ATTN_SCALE = ATTN_HEAD_DIM ** -0.5
FF_BLOCKS = 4
FF_BLOCK = D_FF // FF_BLOCKS

ADAM_LR = 0.001
ADAM_B1 = 0.9
ADAM_B2 = 0.999
ADAM_EPS = 1e-08
ADAM_WD = 0.01
ADAM_STEP = 10

V7X_VMEM_BYTES = 64 * 1024 * 1024
VMEM_LIMIT = 48 * 1024 * 1024

NN = ((1,), (0,))
NT = ((1,), (1,))
TN = ((0,), (0,))


def _dot(a, b, dims=NN, prec=None):
    return lax.dot_general(a, b, (dims, ((), ())), precision=prec, preferred_element_type=F32)


def _sigmoid(x):
    return 1.0 / (1.0 + jnp.exp(-x))


def _softplus(x):
    return jnp.maximum(x, 0.0) + jnp.log(1.0 + jnp.exp(-jnp.abs(x)))


def _params(*sem):
    return pltpu.CompilerParams(dimension_semantics=sem, vmem_limit_bytes=VMEM_LIMIT)


def _rms_fwd(xv, g):
    r = lax.rsqrt(jnp.mean(xv * xv, axis=-1, keepdims=True) + EPS)
    return xv * r * g


def _rms_bwd(xv, g, dn):
    r = lax.rsqrt(jnp.mean(xv * xv, axis=-1, keepdims=True) + EPS)
    xh = xv * r
    dg = jnp.sum(dn * xh, axis=0, keepdims=True)
    dxh = dn * g
    dx = r * (dxh - xh * jnp.mean(dxh * xh, axis=-1, keepdims=True))
    return dx, dg


def _full(shape):
    return pl.BlockSpec(shape, lambda *_: (0,) * len(shape))


def _inproj(x, g_mix, wa, wd, tm):
    t = x.shape[0]

    def body(x_ref, g_ref, wa_ref, wd_ref, u_ref, pa_ref, pd_ref):
        u = _rms_fwd(x_ref[...], g_ref[...]).astype(MXU_DTYPE)
        u_ref[...] = u
        pa_ref[...] = _dot(u, wa_ref[...])
        pd_ref[...] = _dot(u, wd_ref[...])

    na, nd = wa.shape[1], wd.shape[1]
    return pl.pallas_call(
        body, name="inproj", grid=(t // tm,),
        in_specs=[pl.BlockSpec((tm, D_MODEL), lambda i: (i, 0)), _full((1, D_MODEL)),
                  _full((D_MODEL, na)), _full((D_MODEL, nd))],
        out_specs=[pl.BlockSpec((tm, D_MODEL), lambda i: (i, 0)), pl.BlockSpec((tm, na), lambda i: (i, 0)),
                   pl.BlockSpec((tm, nd), lambda i: (i, 0))],
        out_shape=[jax.ShapeDtypeStruct((t, D_MODEL), MXU_DTYPE), jax.ShapeDtypeStruct((t, na), F32),
                   jax.ShapeDtypeStruct((t, nd), F32)],
        compiler_params=_params("parallel"),
    )(x, g_mix, wa, wd)


def _oproj(x, ao, dn, wo_a, wo_d, tm):
    t = x.shape[0]

    def body(x_ref, ao_ref, dn_ref, wa_ref, wd_ref, h_ref):
        h_ref[...] = (x_ref[...] + _dot(ao_ref[...].astype(MXU_DTYPE), wa_ref[...])
                      + _dot(dn_ref[...].astype(MXU_DTYPE), wd_ref[...]))

    half = ao.shape[1]
    return pl.pallas_call(
        body, name="oproj", grid=(t // tm,),
        in_specs=[pl.BlockSpec((tm, D_MODEL), lambda i: (i, 0)), pl.BlockSpec((tm, half), lambda i: (i, 0)),
                  pl.BlockSpec((tm, half), lambda i: (i, 0)), _full((half, D_MODEL)), _full((half, D_MODEL))],
        out_specs=pl.BlockSpec((tm, D_MODEL), lambda i: (i, 0)),
        out_shape=jax.ShapeDtypeStruct((t, D_MODEL), F32),
        compiler_params=_params("parallel"),
    )(x, ao, dn, wo_a, wo_d)


def _mlp_fwd(h1, g_mlp, w_up4, w_down, tm):
    t = h1.shape[0]

    def body(h_ref, g_ref, wu_ref, wd_ref, m_ref, r_ref, h2_ref, acc_ref):
        k = pl.program_id(1)

        @pl.when(k == 0)
        def _():
            m_ref[...] = _rms_fwd(h_ref[...], g_ref[...]).astype(MXU_DTYPE)
            acc_ref[...] = jnp.zeros_like(acc_ref)

        r = jnp.maximum(_dot(m_ref[...], wu_ref[...]), 0.0)
        r_ref[...] = r.astype(MXU_DTYPE)
        s = jnp.square(r).astype(MXU_DTYPE)
        acc_ref[...] += _dot(s, wd_ref[...])

        @pl.when(k == FF_BLOCKS - 1)
        def _():
            h2_ref[...] = h_ref[...] + acc_ref[...]

    return pl.pallas_call(
        body, name="mlp_fwd", grid=(t // tm, FF_BLOCKS),
        in_specs=[pl.BlockSpec((tm, D_MODEL), lambda i, k: (i, 0)), _full((1, D_MODEL)),
                  pl.BlockSpec((None, D_MODEL, FF_BLOCK), lambda i, k: (k, 0, 0)),
                  pl.BlockSpec((FF_BLOCK, D_MODEL), lambda i, k: (k, 0))],
        out_specs=[pl.BlockSpec((tm, D_MODEL), lambda i, k: (i, 0)), pl.BlockSpec((tm, FF_BLOCK), lambda i, k: (i, k)),
                   pl.BlockSpec((tm, D_MODEL), lambda i, k: (i, 0))],
        out_shape=[jax.ShapeDtypeStruct((t, D_MODEL), MXU_DTYPE), jax.ShapeDtypeStruct((t, D_FF), MXU_DTYPE),
                   jax.ShapeDtypeStruct((t, D_MODEL), F32)],
        scratch_shapes=[pltpu.VMEM((tm, D_MODEL), F32)],
        compiler_params=_params("parallel", "arbitrary"),
    )(h1, g_mlp, w_up4, w_down)


def _ple_loss(h2, p, tgt, g_ple, g_fin, w_gate, w_proj, tm):
    t = h2.shape[0]

    def body(h_ref, p_ref, t_ref, gp_ref, gf_ref, wg_ref, wp_ref,
             dh_ref, dhb_ref, dgp_ref, dpp_ref, n3_ref, pb_ref, acc_ref):
        @pl.when(pl.program_id(0) == 0)
        def _():
            acc_ref[...] = jnp.zeros_like(acc_ref)

        h = h_ref[...]
        g_ple_v, g_fin_v = gp_ref[...], gf_ref[...]
        n3 = _rms_fwd(h, g_ple_v).astype(MXU_DTYPE)
        n3_ref[...] = n3
        gate = _sigmoid(_dot(n3, wg_ref[...]))
        pb = p_ref[...].astype(MXU_DTYPE)
        pb_ref[...] = pb
        pp = _dot(pb, wp_ref[...])
        h3 = h + gate * pp
        r4 = lax.rsqrt(jnp.mean(h3 * h3, axis=-1, keepdims=True) + EPS)
        xh4 = h3 * r4
        e = xh4 * g_fin_v - t_ref[...]
        loss = 0.5 * jnp.sum(jnp.mean(e * e, axis=-1, keepdims=True), axis=0, keepdims=True)
        dy = e * (1.0 / D_MODEL)
        dg_fin = jnp.sum(dy * xh4, axis=0, keepdims=True)
        dxh = dy * g_fin_v
        dh3 = r4 * (dxh - xh4 * jnp.mean(dxh * xh4, axis=-1, keepdims=True))
        dpp_ref[...] = (dh3 * gate).astype(MXU_DTYPE)
        dgp = (dh3 * pp * gate * (1.0 - gate)).astype(MXU_DTYPE)
        dgp_ref[...] = dgp
        dn3 = _dot(dgp, wg_ref[...], NT)
        dx, dg_ple = _rms_bwd(h, g_ple_v, dn3)
        dh2 = dh3 + dx
        dh_ref[...] = dh2
        dhb_ref[...] = dh2.astype(MXU_DTYPE)
        acc_ref[0:1, :] += dg_fin
        acc_ref[1:2, :] += dg_ple
        acc_ref[2:3, :] += jnp.broadcast_to(loss, (1, D_MODEL))

    row = lambda w: pl.BlockSpec((tm, w), lambda i: (i, 0))
    return pl.pallas_call(
        body, name="ple_loss", grid=(t // tm,),
        in_specs=[row(D_MODEL), row(PLE_DIM), row(D_MODEL), _full((1, D_MODEL)), _full((1, D_MODEL)),
                  _full((D_MODEL, D_MODEL)), _full((PLE_DIM, D_MODEL))],
        out_specs=[row(D_MODEL), row(D_MODEL), row(D_MODEL), row(D_MODEL), row(D_MODEL), row(PLE_DIM),
                   _full((8, D_MODEL))],
        out_shape=[jax.ShapeDtypeStruct((t, D_MODEL), F32), jax.ShapeDtypeStruct((t, D_MODEL), MXU_DTYPE),
                   jax.ShapeDtypeStruct((t, D_MODEL), MXU_DTYPE), jax.ShapeDtypeStruct((t, D_MODEL), MXU_DTYPE),
                   jax.ShapeDtypeStruct((t, D_MODEL), MXU_DTYPE), jax.ShapeDtypeStruct((t, PLE_DIM), MXU_DTYPE),
                   jax.ShapeDtypeStruct((8, D_MODEL), F32)],
        compiler_params=_params("arbitrary"),
    )(h2, p, tgt, g_ple, g_fin, w_gate, w_proj)


def _mlp_bwd(dh2, dh2b, r, h1, g_mlp, w_up4, w_down, tm):
    t = h1.shape[0]

    def body(dh_ref, dhb_ref, r_ref, h_ref, g_ref, wu_ref, wd_ref,
             da_ref, dh1_ref, dh1b_ref, acc_ref, dm_ref):
        i, k = pl.program_id(0), pl.program_id(1)

        @pl.when((i == 0) & (k == 0))
        def _():
            acc_ref[...] = jnp.zeros_like(acc_ref)

        @pl.when(k == 0)
        def _():
            dm_ref[...] = jnp.zeros_like(dm_ref)

        ds = _dot(dhb_ref[...], wd_ref[...], NT)
        da = (ds * (2.0 * r_ref[...].astype(F32))).astype(MXU_DTYPE)
        da_ref[...] = da
        dm_ref[...] += _dot(da, wu_ref[...], NT)

        @pl.when(k == FF_BLOCKS - 1)
        def _():
            dx, dg = _rms_bwd(h_ref[...], g_ref[...], dm_ref[...])
            dh1 = dh_ref[...] + dx
            dh1_ref[...] = dh1
            dh1b_ref[...] = dh1.astype(MXU_DTYPE)
            acc_ref[0:1, :] += dg

    tok = lambda w: pl.BlockSpec((tm, w), lambda i, k: (i, 0))
    return pl.pallas_call(
        body, name="mlp_bwd", grid=(t // tm, FF_BLOCKS),
        in_specs=[tok(D_MODEL), tok(D_MODEL), pl.BlockSpec((tm, FF_BLOCK), lambda i, k: (i, k)), tok(D_MODEL),
                  _full((1, D_MODEL)), pl.BlockSpec((None, D_MODEL, FF_BLOCK), lambda i, k: (k, 0, 0)),
                  pl.BlockSpec((FF_BLOCK, D_MODEL), lambda i, k: (k, 0))],
        out_specs=[pl.BlockSpec((tm, FF_BLOCK), lambda i, k: (i, k)),
                   tok(D_MODEL), tok(D_MODEL), pl.BlockSpec((8, D_MODEL), lambda i, k: (0, 0))],
        out_shape=[jax.ShapeDtypeStruct((t, D_FF), MXU_DTYPE),
                   jax.ShapeDtypeStruct((t, D_MODEL), F32), jax.ShapeDtypeStruct((t, D_MODEL), MXU_DTYPE),
                   jax.ShapeDtypeStruct((8, D_MODEL), F32)],
        scratch_shapes=[pltpu.VMEM((tm, D_MODEL), F32)],
        compiler_params=_params("arbitrary", "arbitrary"),
    )(dh2, dh2b, r, h1, g_mlp, w_up4, w_down)


def _oproj_bwd(dh1b, wo_a, wo_d, tm, dep):
    t = dh1b.shape[0]
    half = wo_a.shape[0]

    def body(d_ref, wa_ref, wd_ref, dep_ref, da_ref, dd_ref):
        d = d_ref[...]
        da_ref[...] = _dot(d, wa_ref[...], NT)
        dd_ref[...] = _dot(d, wd_ref[...], NT)

    return pl.pallas_call(
        body, name="oproj_bwd", grid=(t // tm,),
        in_specs=[pl.BlockSpec((tm, D_MODEL), lambda i: (i, 0)), _full((half, D_MODEL)), _full((half, D_MODEL)),
                  pl.BlockSpec(memory_space=pl.ANY)],
        out_specs=[pl.BlockSpec((tm, half), lambda i: (i, 0)), pl.BlockSpec((tm, half), lambda i: (i, 0))],
        out_shape=[jax.ShapeDtypeStruct((t, half), F32), jax.ShapeDtypeStruct((t, half), F32)],
        compiler_params=_params("parallel"),
    )(dh1b, wo_a, wo_d, dep)


def _inproj_bwd(x, dh1, g_mix, grads, weights, tm):
    t = x.shape[0]
    n = len(grads)

    def body(*refs):
        x_ref, dh_ref, g_ref = refs[:3]
        g_refs, w_refs = refs[3:3 + n], refs[3 + n:3 + 2 * n]
        dx_ref, acc_ref = refs[3 + 2 * n:]

        @pl.when(pl.program_id(0) == 0)
        def _():
            acc_ref[...] = jnp.zeros_like(acc_ref)

        du = _dot(g_refs[0][...], w_refs[0][...], NT)
        for j in range(1, n):
            du += _dot(g_refs[j][...], w_refs[j][...], NT)
        dx, dg = _rms_bwd(x_ref[...], g_ref[...], du)
        dx_ref[...] = dh_ref[...] + dx
        acc_ref[0:1, :] += dg

    tok = lambda w: pl.BlockSpec((tm, w), lambda i: (i, 0))
    return pl.pallas_call(
        body, name="inproj_bwd", grid=(t // tm,),
        in_specs=[tok(D_MODEL), tok(D_MODEL), _full((1, D_MODEL))] + [tok(g.shape[1]) for g in grads]
                 + [_full(w.shape) for w in weights],
        out_specs=[tok(D_MODEL), _full((8, D_MODEL))],
        out_shape=[jax.ShapeDtypeStruct((t, D_MODEL), F32), jax.ShapeDtypeStruct((8, D_MODEL), F32)],
        compiler_params=_params("arbitrary"),
    )(x, dh1, g_mix, *grads, *weights)


def _wgrad(a, b, name, tk, tn, tt, stacked=False, prep=None):
    t, kdim = a.shape
    ncols = b.shape[1]

    def body(a_ref, b_ref, o_ref):
        @pl.when(pl.program_id(2) == 0)
        def _():
            o_ref[...] = jnp.zeros_like(o_ref)

        av = a_ref[...] if prep is None else prep(a_ref[...])
        o_ref[...] += _dot(av, b_ref[...], TN)

    if stacked:
        out_spec = pl.BlockSpec((None, tk, tn), lambda i, j, s: (j, i, 0))
        out_shape = jax.ShapeDtypeStruct((ncols // tn, kdim, tn), F32)
    else:
        out_spec = pl.BlockSpec((tk, tn), lambda i, j, s: (i, j))
        out_shape = jax.ShapeDtypeStruct((kdim, ncols), F32)
    return pl.pallas_call(
        body, name=name, grid=(kdim // tk, ncols // tn, t // tt),
        in_specs=[pl.BlockSpec((tt, tk), lambda i, j, s: (s, i)), pl.BlockSpec((tt, tn), lambda i, j, s: (s, j))],
        out_specs=out_spec, out_shape=out_shape,
        compiler_params=_params("parallel", "parallel", "arbitrary"),
    )(a, b)


def _wgrad_cat(as_, bs, name, tt):
    t = as_[0].shape[0]
    heights = [a.shape[1] for a in as_]
    widths = [b.shape[1] for b in bs]

    def body(*refs):
        a_refs, b_refs, o_ref = refs[:len(as_)], refs[len(as_):-1], refs[-1]

        @pl.when(pl.program_id(0) == 0)
        def _():
            o_ref[...] = jnp.zeros_like(o_ref)

        row = 0
        for a_ref, k in zip(a_refs, heights):
            av = a_ref[...]
            col = 0
            for b_ref, n in zip(b_refs, widths):
                o_ref[row:row + k, col:col + n] += _dot(av, b_ref[...], TN)
                col += n
            row += k

    tok = lambda w: pl.BlockSpec((tt, w), lambda s: (s, 0))
    shape = (sum(heights), sum(widths))
    return pl.pallas_call(
        body, name=name, grid=(t // tt,),
        in_specs=[tok(k) for k in heights] + [tok(n) for n in widths],
        out_specs=_full(shape), out_shape=jax.ShapeDtypeStruct(shape, F32),
        compiler_params=_params("arbitrary"),
    )(*as_, *bs)


def _rope_tables(t):
    half = ATTN_HEAD_DIM // 2
    inv = 1.0 / (ROPE_THETA ** (jnp.arange(half, dtype=F32) * (2.0 / ATTN_HEAD_DIM)))
    ang = jnp.arange(t, dtype=F32)[:, None] * inv[None, :]
    cos, sin = jnp.cos(ang), jnp.sin(ang)
    cos2 = jnp.concatenate([cos, cos], axis=-1)
    sin2 = jnp.concatenate([-sin, sin], axis=-1)
    return jnp.tile(cos2, (1, 2)), jnp.tile(sin2, (1, 2))


def _swap_halves(tv):
    w = tv.shape[-1]
    lane = lax.broadcasted_iota(jnp.int32, tv.shape, tv.ndim - 1)
    first = (lane % ATTN_HEAD_DIM) < (ATTN_HEAD_DIM // 2)
    return jnp.where(first, pltpu.roll(tv, w - ATTN_HEAD_DIM // 2, tv.ndim - 1),
                     pltpu.roll(tv, ATTN_HEAD_DIM // 2, tv.ndim - 1))


def _rope(tv, cos, sin):
    return tv * cos + _swap_halves(tv) * sin


def _rope_bwd(dv, cos, sin):
    return dv * cos + _swap_halves(dv * sin)


def _attn_valid(first_block):
    c = lax.broadcasted_iota(jnp.int32, (2 * ATTN_BLOCK, ATTN_BLOCK), 0)
    r = lax.broadcasted_iota(jnp.int32, (2 * ATTN_BLOCK, ATTN_BLOCK), 1)
    return (c > r) & (c <= r + ATTN_BLOCK) & ((c >= ATTN_BLOCK) | jnp.logical_not(first_block))


def _attn_probs(st, sink, valid):
    s = jnp.where(valid, st * ATTN_SCALE, -jnp.inf)
    m = jnp.maximum(jnp.max(s, axis=0, keepdims=True), sink)
    e = jnp.where(valid, jnp.exp(s - m), 0.0)
    es = jnp.exp(sink - m)
    inv = 1.0 / (jnp.sum(e, axis=0, keepdims=True) + es)
    return e * inv, es * inv


def _lane_scalar(vec, idx):
    lane = lax.broadcasted_iota(jnp.int32, vec.shape, 1)
    return jnp.sum(jnp.where(lane == idx, vec, 0.0), axis=-1, keepdims=True)


def _attn_specs(nb):
    cur = lambda w, cb: pl.BlockSpec((ATTN_BLOCK, w), lambda i: (jnp.minimum(i, nb - 1), cb))
    prev = lambda w, cb: pl.BlockSpec((ATTN_BLOCK, w), lambda i: (jnp.maximum(jnp.minimum(i, nb - 1) - 1, 0), cb))
    kcol, vcol = ATTN_Q // ATTN_KV, ATTN_Q // ATTN_KV + 1
    return [cur(ATTN_Q, 0), cur(ATTN_KV, kcol), prev(ATTN_KV, kcol), cur(ATTN_KV, vcol), prev(ATTN_KV, vcol),
            cur(ATTN_KV, 0), cur(ATTN_KV, 0), prev(ATTN_KV, 0), prev(ATTN_KV, 0), _full((1, 128))]


def _attn_fwd(pa, cos, sin, sinks_vec):
    t = pa.shape[0]
    nb = t // ATTN_BLOCK

    def body(q_ref, kc_ref, kp_ref, vc_ref, vp_ref, cc_ref, sc_ref, cp_ref, sp_ref, sk_ref, o_ref):
        first = pl.program_id(0) == 0
        cc, sc = cc_ref[...], sc_ref[...]
        q = _rope(q_ref[...], jnp.tile(cc, (1, ATTN_Q // ATTN_KV)), jnp.tile(sc, (1, ATTN_Q // ATTN_KV)))
        kc = _rope(kc_ref[...], cc, sc)
        kp = _rope(kp_ref[...], cp_ref[...], sp_ref[...])
        vc, vp = vc_ref[...], vp_ref[...]
        sk = sk_ref[...]
        valid = _attn_valid(first)
        kv = lambda tp, tc, hk: jnp.concatenate([tp[:, hk * ATTN_HEAD_DIM:(hk + 1) * ATTN_HEAD_DIM],
                                                 tc[:, hk * ATTN_HEAD_DIM:(hk + 1) * ATTN_HEAD_DIM]], axis=0)
        kwins = [kv(kp, kc, hk) for hk in range(ATTN_KV_HEADS)]
        vwins_t = [kv(vp, vc, hk).T for hk in range(ATTN_KV_HEADS)]
        heads = [slice(h * ATTN_HEAD_DIM, (h + 1) * ATTN_HEAD_DIM) for h in range(ATTN_HEADS)]
        scores = [_dot(kwins[h // ATTN_GROUPS], q[:, hs], NT) for h, hs in enumerate(heads)]
        probs = [_attn_probs(st, _lane_scalar(sk, h), valid)[0] for h, st in enumerate(scores)]
        for h, (hs, pt) in enumerate(zip(heads, probs)):
            o_ref[:, hs] = _dot(vwins_t[h // ATTN_GROUPS], pt).T.astype(o_ref.dtype)

    return pl.pallas_call(
        body, name="attn_fwd", grid=(nb,),
        in_specs=_attn_specs(nb),
        out_specs=pl.BlockSpec((ATTN_BLOCK, ATTN_Q), lambda i: (i, 0)),
        out_shape=jax.ShapeDtypeStruct((t, ATTN_Q), MXU_DTYPE),
        compiler_params=_params("parallel"),
    )(pa, pa, pa, pa, pa, cos, sin, cos, sin, sinks_vec)


def _attn_bwd(pa, cos, sin, sinks_vec, dao):
    t = pa.shape[0]
    nb = t // ATTN_BLOCK

    def body(q_ref, kc_ref, kp_ref, vc_ref, vp_ref, cc_ref, sc_ref, cp_ref, sp_ref, sk_ref, do_ref,
             dq_ref, dk_ref, dv_ref, acc_ref, dqr_ref, dkw_ref, dvw_ref, ck_ref, cv_ref):
        i = pl.program_id(0)

        @pl.when(i == 0)
        def _():
            acc_ref[...] = jnp.zeros_like(acc_ref)
            ck_ref[...] = jnp.zeros_like(ck_ref)
            cv_ref[...] = jnp.zeros_like(cv_ref)

        @pl.when(i < nb)
        def _():
            first = i == 0
            cc, sc = cc_ref[...], sc_ref[...]
            cq, sq = jnp.tile(cc, (1, ATTN_Q // ATTN_KV)), jnp.tile(sc, (1, ATTN_Q // ATTN_KV))
            q = _rope(q_ref[...], cq, sq)
            kc = _rope(kc_ref[...], cc, sc)
            kp = _rope(kp_ref[...], cp_ref[...], sp_ref[...])
            vc, vp = vc_ref[...], vp_ref[...]
            sk = sk_ref[...]
            do = do_ref[...]
            lane = lax.broadcasted_iota(jnp.int32, (1, 128), 1)
            dsink = jnp.zeros((1, 128), F32)
            valid = _attn_valid(first)
            kv = lambda tp, tc, hk: jnp.concatenate([tp[:, hk * ATTN_HEAD_DIM:(hk + 1) * ATTN_HEAD_DIM],
                                                     tc[:, hk * ATTN_HEAD_DIM:(hk + 1) * ATTN_HEAD_DIM]], axis=0)
            kwins = [kv(kp, kc, hk) for hk in range(ATTN_KV_HEADS)]
            vwins = [kv(vp, vc, hk) for hk in range(ATTN_KV_HEADS)]
            kwins_t = [kw.T for kw in kwins]
            heads = [slice(h * ATTN_HEAD_DIM, (h + 1) * ATTN_HEAD_DIM) for h in range(ATTN_HEADS)]
            scores = [_dot(kwins[h // ATTN_GROUPS], q[:, hs], NT) for h, hs in enumerate(heads)]
            dps = [_dot(vwins[h // ATTN_GROUPS], do[:, hs], NT) for h, hs in enumerate(heads)]
            pts, dsts = [], []
            for h, (st, dp_t) in enumerate(zip(scores, dps)):
                probs_t, psink = _attn_probs(st, _lane_scalar(sk, h), valid)
                delta = jnp.sum(probs_t * dp_t, axis=0, keepdims=True)
                pts.append(probs_t)
                dsts.append(probs_t * (dp_t - delta) * ATTN_SCALE)
                dsink += jnp.where(lane == h, jnp.sum(-psink * delta, axis=1, keepdims=True), 0.0)
            for h, (hs, ds_t) in enumerate(zip(heads, dsts)):
                dqr_ref[:, hs] = _dot(kwins_t[h // ATTN_GROUPS], ds_t).T
            for hk in range(ATTN_KV_HEADS):
                ks = slice(hk * ATTN_HEAD_DIM, (hk + 1) * ATTN_HEAD_DIM)
                group = range(hk * ATTN_GROUPS, (hk + 1) * ATTN_GROUPS)
                ds_g = jnp.concatenate([dsts[h] for h in group], axis=1)
                p_g = jnp.concatenate([pts[h] for h in group], axis=1)
                q_g = jnp.concatenate([q[:, heads[h]] for h in group], axis=0)
                do_g = jnp.concatenate([do[:, heads[h]] for h in group], axis=0)
                dkw_ref[:, ks] = _dot(ds_g, q_g)
                dvw_ref[:, ks] = _dot(p_g, do_g)
            acc_ref[0:1, :] += dsink
            dq_ref[...] = _rope_bwd(dqr_ref[...], cq, sq).astype(dq_ref.dtype)
            dk_ref[...] = (ck_ref[...] + _rope_bwd(dkw_ref[0:ATTN_BLOCK, :], cp_ref[...], sp_ref[...])).astype(dk_ref.dtype)
            dv_ref[...] = (cv_ref[...] + dvw_ref[0:ATTN_BLOCK, :]).astype(dv_ref.dtype)
            ck_ref[...] = _rope_bwd(dkw_ref[ATTN_BLOCK:2 * ATTN_BLOCK, :], cc, sc)
            cv_ref[...] = dvw_ref[ATTN_BLOCK:2 * ATTN_BLOCK, :]

        @pl.when(i == nb)
        def _():
            dk_ref[...] = ck_ref[...].astype(dk_ref.dtype)
            dv_ref[...] = cv_ref[...].astype(dv_ref.dtype)

    prev_out = lambda w: pl.BlockSpec((ATTN_BLOCK, w), lambda i: (jnp.maximum(i - 1, 0), 0))
    return pl.pallas_call(
        body, name="attn_bwd", grid=(nb + 1,),
        in_specs=_attn_specs(nb) + [pl.BlockSpec((ATTN_BLOCK, ATTN_Q), lambda i: (jnp.minimum(i, nb - 1), 0))],
        out_specs=[pl.BlockSpec((ATTN_BLOCK, ATTN_Q), lambda i: (jnp.minimum(i, nb - 1), 0)), prev_out(ATTN_KV),
                   prev_out(ATTN_KV), _full((8, 128))],
        out_shape=[jax.ShapeDtypeStruct((t, ATTN_Q), MXU_DTYPE), jax.ShapeDtypeStruct((t, ATTN_KV), MXU_DTYPE),
                   jax.ShapeDtypeStruct((t, ATTN_KV), MXU_DTYPE), jax.ShapeDtypeStruct((8, 128), F32)],
        scratch_shapes=[pltpu.VMEM((ATTN_BLOCK, ATTN_Q), F32), pltpu.VMEM((2 * ATTN_BLOCK, ATTN_KV), F32),
                        pltpu.VMEM((2 * ATTN_BLOCK, ATTN_KV), F32), pltpu.VMEM((ATTN_BLOCK, ATTN_KV), F32),
                        pltpu.VMEM((ATTN_BLOCK, ATTN_KV), F32)],
        compiler_params=_params("arbitrary"),
    )(pa, pa, pa, pa, pa, cos, sin, cos, sin, sinks_vec, dao)


PAIR = 2 * DN_CHUNK
INTRA_PAIRS = 4
HALO = 8


def _conv_window(cur_ref, prev_ref, xs_ref, tm, has_prev):
    prev = jnp.where(has_prev, prev_ref[...], 0.0)
    xs_ref[0:HALO, :] = prev
    xs_ref[HALO:HALO + tm, :] = cur_ref[...]


def _conv_taps(xs_ref, cw_ref, tm):
    y = cw_ref[0:1, :] * xs_ref[pl.ds(HALO - DN_CONV + 1, tm), :]
    for j in range(1, DN_CONV):
        y += cw_ref[j:j + 1, :] * xs_ref[pl.ds(HALO - DN_CONV + 1 + j, tm), :]
    return y


def _gate_values(ba, al, dt):
    beta = _sigmoid(ba)
    pre = ba + dt
    g = -jnp.exp(al) * _softplus(pre)
    return beta, g, pre


def _dn_prep_specs(tm, tile):
    return [pl.BlockSpec((tm, CONV_CH), lambda i: (tile(i), 0)),
            pl.BlockSpec((HALO, CONV_CH), lambda i: (jnp.maximum(tile(i) * (tm // HALO) - 1, 0), 0)),
            pl.BlockSpec((tm, 128), lambda i: (tile(i), 4 * DN_W // 128)),
            _full((DN_CONV, CONV_CH)), _full((1, 128)), _full((1, 128))]


def _dn_prep(pd, conv_w, al_vec, dt_vec, tm):
    t = pd.shape[0]

    def body(cur_ref, prev_ref, ba_ref, cw_ref, al_ref, dt_ref, qn_ref, kn_ref, vc_ref, gc_ref, gr_ref, xs_ref):
        _conv_window(cur_ref, prev_ref, xs_ref, tm, pl.program_id(0) > 0)
        y = _conv_taps(xs_ref, cw_ref, tm)
        c = y * _sigmoid(y)
        for h in range(DN_HEADS):
            qs = slice(h * DN_HEAD_DIM, (h + 1) * DN_HEAD_DIM)
            ksl = slice(DN_W + h * DN_HEAD_DIM, DN_W + (h + 1) * DN_HEAD_DIM)
            qh, kh = c[:, qs], c[:, ksl]
            qn_ref[:, qs] = qh * lax.rsqrt(jnp.sum(qh * qh, axis=-1, keepdims=True) + EPS) * DN_SCALE
            kn_ref[:, qs] = kh * lax.rsqrt(jnp.sum(kh * kh, axis=-1, keepdims=True) + EPS)
        vc_ref[...] = c[:, 2 * DN_W:3 * DN_W]
        beta, g, _ = _gate_values(ba_ref[...], al_ref[...], dt_ref[...])
        lane = lax.broadcasted_iota(jnp.int32, beta.shape, 1)
        gb = jnp.where(lane < DN_HEADS, beta, jnp.where(lane < 2 * DN_HEADS, g, 0.0))
        gc_ref[...] = gb
        gr_ref[...] = gb.T[0:8, :]

    tok = lambda w: pl.BlockSpec((tm, w), lambda i: (i, 0))
    return pl.pallas_call(
        body, name="dn_prep", grid=(t // tm,),
        in_specs=_dn_prep_specs(tm, lambda i: i),
        out_specs=[tok(DN_W), tok(DN_W), tok(DN_W), tok(128), pl.BlockSpec((8, tm), lambda i: (0, i))],
        out_shape=[jax.ShapeDtypeStruct((t, DN_W), F32)] * 3 + [jax.ShapeDtypeStruct((t, 128), F32),
                                                                 jax.ShapeDtypeStruct((8, t), F32)],
        scratch_shapes=[pltpu.VMEM((HALO + tm, CONV_CH), F32)],
        compiler_params=_params("parallel"),
    )(pd, pd, pd, conv_w, al_vec, dt_vec)


def _pair_masks():
    r = lax.broadcasted_iota(jnp.int32, (PAIR, PAIR), 0)
    c = lax.broadcasted_iota(jnp.int32, (PAIR, PAIR), 1)
    same = (r < DN_CHUNK) == (c < DN_CHUNK)
    return same & (r >= c), same & (r > c)


def _lane_col(mat, idx):
    lane = lax.broadcasted_iota(jnp.int32, mat.shape, 1)
    return jnp.sum(jnp.where(lane == idx, mat, 0.0), axis=-1, keepdims=True)


def _pair_cumsums(gc, gr, low):
    lowf = low.astype(F32)
    return _dot(lowf, gc, NN, HI), _dot(gr, lowf, NT, HI)


def _pair_gates(gc, cum_c, cum_r, low, h):
    beta = _lane_col(gc, h)
    gam = _lane_col(cum_c, DN_HEADS + h)
    gam_row = cum_r[DN_HEADS + h:DN_HEADS + h + 1, :]
    dm = jnp.where(low, jnp.exp(jnp.where(low, gam - gam_row, 0.0)), 0.0)
    row = lax.broadcasted_iota(jnp.int32, gam.shape, 0)
    gl = jnp.where(row < DN_CHUNK, gam[DN_CHUNK - 1:DN_CHUNK, :], gam[PAIR - 1:PAIR, :])
    return beta, gam, dm, gl


def _split(a):
    hi = a.astype(BF16)
    return hi, (a - hi.astype(F32)).astype(BF16)


def _dot_split(a, b, dims=NN):
    (ah, al), (bh, bl) = a, b
    la, lb = (1, 1) if dims == TN else ((0, 1) if dims == NN else (0, 0))
    r = _dot(jnp.concatenate([ah, al], axis=la), jnp.concatenate([bh, bl], axis=lb), dims)
    m, n = r.shape[0] // 2, r.shape[1] // 2
    return (r[m:, n:] + (r[:m, n:] + r[m:, :n])) + r[:m, :n]


def _unit_lower_inverses(lmats):
    n = lmats[0].shape[0]
    r = lax.broadcasted_iota(jnp.int32, (n, n), 0)
    c = lax.broadcasted_iota(jnp.int32, (n, n), 1)
    same = lambda size: (r & ~(size - 1)) == (c & ~(size - 1))
    base = DN_CHUNK // 4
    diag = [jnp.where(same(base), l, 0.0) for l in lmats]
    accs = [(r == c).astype(F32) - d for d in diag]
    splits = [_split(d) for d in diag]
    step = 1
    while 2 * step < base:
        splits = [_split(_dot_split(s, s)) for s in splits]
        accs = [acc + _dot_split(_split(acc), s) for acc, s in zip(accs, splits)]
        step *= 2
    size = base
    while size < DN_CHUNK:
        below = same(2 * size) & jnp.logical_not(same(size))
        tb = [_dot(acc, jnp.where(below, l, 0.0)) for acc, l in zip(accs, lmats)]
        accs = [acc - _dot(t, acc) for acc, t in zip(accs, tb)]
        size *= 2
    return accs


def _dn_intra(qn, kn, vc, gc, gr):
    t = qn.shape[0]
    npair = t // PAIR
    rows_step = INTRA_PAIRS * PAIR

    def body(q_ref, k_ref, v_ref, gc_ref, gr_ref, u_ref, w_ref, qg_ref, kd_ref, a_ref, ti_ref, dl_ref):
        low, strict = _pair_masks()
        items = []
        for p in range(INTRA_PAIRS):
            rows = slice(p * PAIR, (p + 1) * PAIR)
            gc_v = gc_ref[rows, :]
            cum_c, cum_r = _pair_cumsums(gc_v, gr_ref[:, rows], low)
            for h in range(DN_HEADS):
                hs = slice(h * DN_HEAD_DIM, (h + 1) * DN_HEAD_DIM)
                items.append((p, h, rows, hs, _pair_gates(gc_v, cum_c, cum_r, low, h)))
        lmats = []
        for p, h, rows, hs, (beta, gam, dm, gl) in items:
            k = k_ref[rows, hs]
            lmats.append(jnp.where(strict, _dot(k * beta, k, NT) * dm, 0.0))
        tinvs = _unit_lower_inverses(lmats)
        for (p, h, rows, hs, (beta, gam, dm, gl)), tinv in zip(items, tinvs):
            q, k, v = q_ref[rows, hs], k_ref[rows, hs], v_ref[rows, hs]
            eg = jnp.exp(gam)
            u_ref[rows, hs] = _dot(tinv, v * beta)
            w_ref[rows, hs] = _dot(tinv, (k * beta) * eg)
            a_ref[h, rows, :] = _dot(q, k, NT) * dm
            ti_ref[h, rows, :] = tinv
            qg_ref[rows, hs] = q * eg
            kd_ref[rows, hs] = k * jnp.exp(gl - gam)
            for c in range(2):
                last = (c + 1) * DN_CHUNK - 1
                dl_ref[2 * p + c, h] = jnp.broadcast_to(jnp.exp(gam[last:last + 1, :]), (8, 128))

    tok = lambda w: pl.BlockSpec((rows_step, w), lambda n: (n, 0))
    hm = pl.BlockSpec((DN_HEADS, rows_step, PAIR), lambda n: (0, n, 0))
    return pl.pallas_call(
        body, name="dn_intra", grid=(npair // INTRA_PAIRS,),
        in_specs=[tok(DN_W), tok(DN_W), tok(DN_W), tok(128), pl.BlockSpec((8, rows_step), lambda n: (0, n))],
        out_specs=[tok(DN_W)] * 4 + [hm, hm, pl.BlockSpec((2 * INTRA_PAIRS, DN_HEADS, 8, 128), lambda n: (n, 0, 0, 0))],
        out_shape=[jax.ShapeDtypeStruct((t, DN_W), F32)] * 4 + [jax.ShapeDtypeStruct((DN_HEADS, t, PAIR), F32)] * 2
                  + [jax.ShapeDtypeStruct((2 * npair, DN_HEADS, 8, 128), F32)],
        compiler_params=_params("parallel"),
    )(qn, kn, vc, gc, gr)


def _dn_scan_fwd(u, w, qg, kd, a_qk, dlast, pd, dn_w):
    t = u.shape[0]
    npair = t // PAIR

    def body(u_ref, w_ref, qg_ref, kd_ref, a_ref, dl_ref, z_ref, nw_ref, out_ref, o_ref, vn_ref, sall_ref, s_ref):
        @pl.when(pl.program_id(0) == 0)
        def _():
            s_ref[...] = jnp.zeros_like(s_ref)

        nw = nw_ref[...]
        for c in range(2):
            rows = slice(c * DN_CHUNK, (c + 1) * DN_CHUNK)
            for h in range(DN_HEADS):
                hs = slice(h * DN_HEAD_DIM, (h + 1) * DN_HEAD_DIM)
                st = s_ref[h]
                sall_ref[c, h] = st
                vn_ref[rows, hs] = u_ref[rows, hs] - _dot(w_ref[rows, hs], st)
            for h in range(DN_HEADS):
                hs = slice(h * DN_HEAD_DIM, (h + 1) * DN_HEAD_DIM)
                st, vn = s_ref[h], vn_ref[rows, hs]
                o = _dot(qg_ref[rows, hs], st) + _dot(a_ref[h, rows, rows], vn)
                s_ref[h] = st * dl_ref[c, h][0:1, :] + _dot(kd_ref[rows, hs], vn, TN)
                o_ref[rows, hs] = o
                z = z_ref[rows, hs]
                on = o * lax.rsqrt(jnp.mean(o * o, axis=-1, keepdims=True) + EPS) * nw
                out_ref[rows, hs] = (on * (z * _sigmoid(z))).astype(out_ref.dtype)

    tok = pl.BlockSpec((PAIR, DN_W), lambda n: (n, 0))
    hm = pl.BlockSpec((DN_HEADS, PAIR, PAIR), lambda n: (0, n, 0))
    return pl.pallas_call(
        body, name="dn_scan_fwd", grid=(npair,),
        in_specs=[tok, tok, tok, tok, hm, pl.BlockSpec((2, DN_HEADS, 8, 128), lambda n: (n, 0, 0, 0)),
                  pl.BlockSpec((PAIR, DN_W), lambda n: (n, 3)), _full((1, 128))],
        out_specs=[tok, tok, tok, pl.BlockSpec((2, DN_HEADS, DN_HEAD_DIM, DN_HEAD_DIM), lambda n: (n, 0, 0, 0))],
        out_shape=[jax.ShapeDtypeStruct((t, DN_W), MXU_DTYPE)] + [jax.ShapeDtypeStruct((t, DN_W), F32)] * 2
                  + [jax.ShapeDtypeStruct((2 * npair, DN_HEADS, DN_HEAD_DIM, DN_HEAD_DIM), F32)],
        scratch_shapes=[pltpu.VMEM((DN_HEADS, DN_HEAD_DIM, DN_HEAD_DIM), F32)],
        compiler_params=_params("arbitrary"),
    )(u, w, qg, kd, a_qk, dlast, pd, dn_w)


def _dn_scan_bwd(dout, o, vnew, sall, w, qg, kd, a_qk, dlast, pd, dn_w):
    t = o.shape[0]
    npair = t // PAIR
    rev = lambda n: npair - 1 - n

    def body(do_ref, o_ref, vn_ref, sall_ref, w_ref, qg_ref, kd_ref, a_ref, dl_ref, z_ref, nw_ref,
             dz_ref, du_ref, dw_ref, dqg_ref, dkd_ref, da_ref, ddl_ref, acc_ref, ds_ref, dos_ref):
        @pl.when(pl.program_id(0) == 0)
        def _():
            ds_ref[...] = jnp.zeros_like(ds_ref)
            acc_ref[...] = jnp.zeros_like(acc_ref)

        nw = nw_ref[...]
        dnw = jnp.zeros((1, 128), F32)
        for h in range(DN_HEADS):
            hs = slice(h * DN_HEAD_DIM, (h + 1) * DN_HEAD_DIM)
            o, z, dout = o_ref[:, hs], z_ref[:, hs], do_ref[:, hs]
            r = lax.rsqrt(jnp.mean(o * o, axis=-1, keepdims=True) + EPS)
            oh = o * r
            sz = _sigmoid(z)
            dz_ref[:, hs] = dout * (oh * nw) * (sz + z * sz * (1.0 - sz))
            don = dout * (z * sz)
            dnw += jnp.sum(don * oh, axis=0, keepdims=True)
            doh = don * nw
            dos_ref[:, hs] = r * (doh - oh * jnp.mean(doh * oh, axis=-1, keepdims=True))
        acc_ref[0:1, :] += dnw
        for c in (1, 0):
            rows = slice(c * DN_CHUNK, (c + 1) * DN_CHUNK)
            other = slice((1 - c) * DN_CHUNK, (2 - c) * DN_CHUNK)
            for h in range(DN_HEADS):
                hs = slice(h * DN_HEAD_DIM, (h + 1) * DN_HEAD_DIM)
                do, st, dsp, vn = dos_ref[rows, hs], sall_ref[c, h], ds_ref[h], vn_ref[rows, hs]
                da_ref[h, rows, rows] = _dot(do, vn, NT)
                da_ref[h, rows, other] = jnp.zeros((DN_CHUNK, DN_CHUNK), F32)
                du_ref[rows, hs] = _dot(a_ref[h, rows, rows], do, TN) + _dot(kd_ref[rows, hs], dsp)
                dqg_ref[rows, hs] = _dot(do, st, NT)
                dkd_ref[rows, hs] = _dot(vn, dsp, NT)
                ddl = jnp.sum(jnp.sum(dsp * st, axis=1, keepdims=True), axis=0, keepdims=True)
                ddl_ref[c, h] = jnp.broadcast_to(ddl, (8, 128))
            for h in range(DN_HEADS):
                hs = slice(h * DN_HEAD_DIM, (h + 1) * DN_HEAD_DIM)
                do, st, dvn = dos_ref[rows, hs], sall_ref[c, h], du_ref[rows, hs]
                dw_ref[rows, hs] = -_dot(dvn, st, NT)
                ds_ref[h] = (ds_ref[h] * dl_ref[c, h][0:1, :] + _dot(qg_ref[rows, hs], do, TN)
                             - _dot(w_ref[rows, hs], dvn, TN))

    tok = pl.BlockSpec((PAIR, DN_W), lambda n: (rev(n), 0))
    hm = pl.BlockSpec((DN_HEADS, PAIR, PAIR), lambda n: (0, rev(n), 0))
    sc = pl.BlockSpec((2, DN_HEADS, 8, 128), lambda n: (rev(n), 0, 0, 0))
    return pl.pallas_call(
        body, name="dn_scan_bwd", grid=(npair,),
        in_specs=[tok, tok, tok, pl.BlockSpec((2, DN_HEADS, DN_HEAD_DIM, DN_HEAD_DIM), lambda n: (rev(n), 0, 0, 0)),
                  tok, tok, tok, hm, sc, pl.BlockSpec((PAIR, DN_W), lambda n: (rev(n), 3)), _full((1, 128))],
        out_specs=[tok] * 5 + [hm, sc, _full((8, 128))],
        out_shape=[jax.ShapeDtypeStruct((t, DN_W), F32)] * 5 + [jax.ShapeDtypeStruct((DN_HEADS, t, PAIR), F32),
                   jax.ShapeDtypeStruct((2 * npair, DN_HEADS, 8, 128), F32), jax.ShapeDtypeStruct((8, 128), F32)],
        scratch_shapes=[pltpu.VMEM((DN_HEADS, DN_HEAD_DIM, DN_HEAD_DIM), F32), pltpu.VMEM((PAIR, DN_W), F32)],
        compiler_params=_params("arbitrary"),
    )(dout, o, vnew, sall, w, qg, kd, a_qk, dlast, pd, dn_w)


def _dn_intra_bwd(qn, kn, vc, gc, gr, tinv, a_qk, du, dw, dqg, dkd, da_qk, ddlast, dlast, dep):
    t = qn.shape[0]
    npair = t // PAIR

    def body(q_ref, k_ref, v_ref, gc_ref, gr_ref, ti_ref, a_ref, du_ref, dw_ref, dqg_ref, dkd_ref, da_ref, ddl_ref, dl_ref,
             dep_ref, dq_ref, dk_ref, dv_ref, dg_ref):
        low, strict = _pair_masks()
        lane = lax.broadcasted_iota(jnp.int32, (PAIR, 128), 1)
        rowi = lax.broadcasted_iota(jnp.int32, (PAIR, 1), 0)
        rsum = lambda v: jnp.sum(v, axis=-1, keepdims=True)
        items = []
        for p in range(INTRA_PAIRS):
            rows = slice(p * PAIR, (p + 1) * PAIR)
            gc_v = gc_ref[rows, :]
            cum_c, cum_r = _pair_cumsums(gc_v, gr_ref[:, rows], low)
            for h in range(DN_HEADS):
                hs = slice(h * DN_HEAD_DIM, (h + 1) * DN_HEAD_DIM)
                items.append((p, h, rows, hs, _pair_gates(gc_v, cum_c, cum_r, low, h)))
        dtis, lmats, dvbs, dkbgs = [], [], [], []
        for p, h, rows, hs, (beta, gam, dm, gl) in items:
            k, tinv = k_ref[rows, hs], ti_ref[h, rows, :]
            kb = k * beta
            dtis.append(_dot(du_ref[rows, hs], v_ref[rows, hs] * beta, NT)
                        + _dot(dw_ref[rows, hs], kb * jnp.exp(gam), NT))
            lmats.append(jnp.where(strict, _dot(kb, k, NT) * dm, 0.0))
            dvbs.append(_dot(tinv, du_ref[rows, hs], TN))
            dkbgs.append(_dot(tinv, dw_ref[rows, hs], TN))
        xs = [_dot(ti_ref[h, rows, :], dti, TN) for (p, h, rows, hs, g), dti in zip(items, dtis)]
        dls = [jnp.where(strict, -_dot(x, ti_ref[h, rows, :], NT), 0.0) for (p, h, rows, hs, g), x in zip(items, xs)]
        dgam_all = [jnp.zeros((PAIR, 128), F32) for _ in range(INTRA_PAIRS)]
        dbeta_all = [jnp.zeros((PAIR, 128), F32) for _ in range(INTRA_PAIRS)]
        for (p, h, rows, hs, (beta, gam, dm, gl)), dl, lmat, dvb, dkbg in zip(items, dls, lmats, dvbs, dkbgs):
            q, k, v = q_ref[rows, hs], k_ref[rows, hs], v_ref[rows, hs]
            a = a_ref[h, rows, :]
            dqg, dkd = dqg_ref[rows, hs], dkd_ref[rows, hs]
            kb = k * beta
            eg = jnp.exp(gam)
            ekd = jnp.exp(gl - gam)
            dmm = dl * dm
            dam = jnp.where(low, da_ref[h, rows, :], 0.0)
            dn = dam * dm
            e = dl * lmat + dam * a
            dkb = _dot(dmm, k) + dkbg * eg
            dk_ref[rows, hs] = _dot(dmm, kb, TN) + _dot(dn, q, TN) + dkd * ekd + dkb * beta
            dq_ref[rows, hs] = _dot(dn, k) + dqg * eg
            dv_ref[rows, hs] = dvb * beta
            t_kd = rsum(dkd * (k * ekd))
            dgam = rsum(e) - rsum(e.T) + rsum(dqg * (q * eg)) + rsum(dkbg * (kb * eg)) - t_kd
            for c in range(2):
                crows = slice(c * DN_CHUNK, (c + 1) * DN_CHUNK)
                dgl = (jnp.sum(t_kd[crows, :], axis=0, keepdims=True)
                       + ddl_ref[2 * p + c, h][0:1, 0:1] * dl_ref[2 * p + c, h][0:1, 0:1])
                dgam = dgam + jnp.where(rowi == (c + 1) * DN_CHUNK - 1, dgl, 0.0)
            dgam_all[p] += jnp.where(lane == DN_HEADS + h, dgam, 0.0)
            dbeta_all[p] += jnp.where(lane == h, rsum(dkb * k) + rsum(dvb * v), 0.0)
        for p in range(INTRA_PAIRS):
            dg_ref[p * PAIR:(p + 1) * PAIR, :] = dbeta_all[p] + _dot(low.astype(F32), dgam_all[p], TN, HI)

    rows_step = INTRA_PAIRS * PAIR
    tok = lambda w: pl.BlockSpec((rows_step, w), lambda n: (n, 0))
    hm = pl.BlockSpec((DN_HEADS, rows_step, PAIR), lambda n: (0, n, 0))
    sc = pl.BlockSpec((2 * INTRA_PAIRS, DN_HEADS, 8, 128), lambda n: (n, 0, 0, 0))
    return pl.pallas_call(
        body, name="dn_intra_bwd", grid=(npair // INTRA_PAIRS,),
        in_specs=[tok(DN_W), tok(DN_W), tok(DN_W), tok(128), pl.BlockSpec((8, rows_step), lambda n: (0, n)), hm, hm,
                  tok(DN_W), tok(DN_W), tok(DN_W), tok(DN_W), hm, sc, sc, pl.BlockSpec(memory_space=pl.ANY)],
        out_specs=[tok(DN_W), tok(DN_W), tok(DN_W), tok(128)],
        out_shape=[jax.ShapeDtypeStruct((t, DN_W), F32)] * 3 + [jax.ShapeDtypeStruct((t, 128), F32)],
        compiler_params=_params("parallel"),
    )(qn, kn, vc, gc, gr, tinv, a_qk, du, dw, dqg, dkd, da_qk, ddlast, dlast, dep)


def _dn_prep_bwd(pd, conv_w, al_vec, dt_vec, dqn, dkn, dvc, dgc, dz, tm):
    t = pd.shape[0]
    nt = t // tm
    tile = lambda i: nt - 1 - i

    def body(cur_ref, prev_ref, ba_ref, cw_ref, al_ref, dt_ref, dq_ref, dk_ref, dv_ref, dg_ref, dz_ref,
             o_ref, accw_ref, accg_ref, xs_ref, dc_ref, ds_ref, carry_ref):
        @pl.when(pl.program_id(0) == 0)
        def _():
            accw_ref[...] = jnp.zeros_like(accw_ref)
            accg_ref[...] = jnp.zeros_like(accg_ref)
            carry_ref[...] = jnp.zeros_like(carry_ref)

        _conv_window(cur_ref, prev_ref, xs_ref, tm, tile(pl.program_id(0)) > 0)
        y = _conv_taps(xs_ref, cw_ref, tm)
        sg = _sigmoid(y)
        c = y * sg
        for h in range(DN_HEADS):
            qs = slice(h * DN_HEAD_DIM, (h + 1) * DN_HEAD_DIM)
            ksl = slice(DN_W + h * DN_HEAD_DIM, DN_W + (h + 1) * DN_HEAD_DIM)
            for src, sl, scale in ((dq_ref, qs, DN_SCALE), (dk_ref, ksl, 1.0)):
                xh = c[:, sl]
                r = lax.rsqrt(jnp.sum(xh * xh, axis=-1, keepdims=True) + EPS)
                unit = xh * r
                dn = src[:, qs] * scale
                dc_ref[:, sl] = r * (dn - unit * jnp.sum(dn * unit, axis=-1, keepdims=True))
        dc_ref[:, 2 * DN_W:3 * DN_W] = dv_ref[...]
        dy = dc_ref[...] * (sg + y * sg * (1.0 - sg))
        for j in range(DN_CONV):
            accw_ref[j:j + 1, :] += jnp.sum(dy * xs_ref[pl.ds(HALO - DN_CONV + 1 + j, tm), :], axis=0, keepdims=True)
        ds_ref[0:tm, :] = dy
        ds_ref[tm:tm + HALO, :] = carry_ref[...]
        carry_ref[...] = ds_ref[0:HALO, :]
        dx = cw_ref[0:1, :] * ds_ref[pl.ds(DN_CONV - 1, tm), :]
        for j in range(1, DN_CONV):
            dx += cw_ref[j:j + 1, :] * ds_ref[pl.ds(DN_CONV - 1 - j, tm), :]

        beta, g, pre = _gate_values(ba_ref[...], al_ref[...], dt_ref[...])
        dgb = dg_ref[...]
        lane = lax.broadcasted_iota(jnp.int32, dgb.shape, 1)
        is_b, is_a = lane < DN_HEADS, (lane >= DN_HEADS) & (lane < 2 * DN_HEADS)
        dpre = dgb * (-jnp.exp(al_ref[...])) * _sigmoid(pre)
        dba = jnp.where(is_b, dgb * beta * (1.0 - beta), jnp.where(is_a, dpre, 0.0))
        accg_ref[0:1, :] += jnp.sum(jnp.where(is_a, dgb * g, 0.0), axis=0, keepdims=True)
        accg_ref[1:2, :] += jnp.sum(jnp.where(is_a, dpre, 0.0), axis=0, keepdims=True)
        o_ref[:, 0:CONV_CH] = dx.astype(o_ref.dtype)
        o_ref[:, CONV_CH:CONV_CH + DN_W] = dz_ref[...].astype(o_ref.dtype)
        o_ref[:, CONV_CH + DN_W:DN_COLS] = dba.astype(o_ref.dtype)

    tok = lambda w: pl.BlockSpec((tm, w), lambda i: (tile(i), 0))
    return pl.pallas_call(
        body, name="dn_prep_bwd", grid=(nt,),
        in_specs=_dn_prep_specs(tm, tile) + [tok(DN_W), tok(DN_W), tok(DN_W), tok(128), tok(DN_W)],
        out_specs=[tok(DN_COLS), _full((8, CONV_CH)), _full((8, 128))],
        out_shape=[jax.ShapeDtypeStruct((t, DN_COLS), MXU_DTYPE),
                   jax.ShapeDtypeStruct((8, CONV_CH), F32), jax.ShapeDtypeStruct((8, 128), F32)],
        scratch_shapes=[pltpu.VMEM((HALO + tm, CONV_CH), F32), pltpu.VMEM((tm, CONV_CH), F32),
                        pltpu.VMEM((tm + HALO, CONV_CH), F32), pltpu.VMEM((HALO, CONV_CH), F32)],
        compiler_params=_params("arbitrary"),
    )(pd, pd, pd, conv_w, al_vec, dt_vec, dqn, dkn, dvc, dgc, dz)


def _pad_lanes(v, offset=0):
    return jnp.zeros((1, 128), F32).at[0, offset:offset + v.shape[0]].set(v.astype(F32))


class _LocalReducer:
    def start(self, grads):
        return jnp.zeros((8, 128), F32)

    def middle(self, after):
        return jnp.zeros((8, 128), F32)

    def finish(self, after):
        return None


def _local_step(x, p, tgt, sm, w, late, reducer):
    t = x.shape[0]
    tm = min(512, t // 2)
    tm_s = min(512, t // 2)
    tw = min(1024, t // 2)

    w_in = w["w_in"]
    wa = w_in[:, :ATTN_Q + 2 * ATTN_KV]
    wd = jnp.pad(w_in[:, ATTN_Q + 2 * ATTN_KV:], ((0, 0), (0, DN_COLS - (D_IN - ATTN_Q - 2 * ATTN_KV))))
    conv_w = w["conv_w"]
    al_vec, dt_vec = _pad_lanes(sm["a_log"], DN_HEADS), _pad_lanes(sm["dt_bias"], DN_HEADS)
    sinks_vec = _pad_lanes(sm["sinks"])
    dn_w = sm["dn_norm"].reshape(1, 128)
    row = lambda v: v.reshape(1, D_MODEL)
    cos, sin = _rope_tables(t)

    u, pa, pd = _inproj(x, row(sm["norm_mix"]), wa, wd, tm_s)
    ao = _attn_fwd(pa, cos, sin, sinks_vec)
    qn, kn, vc, gc, gr = _dn_prep(pd, conv_w, al_vec, dt_vec, tm_s)
    uu, ww, qg, kd, a_qk, tinv, dlast = _dn_intra(qn, kn, vc, gc, gr)
    dn_out, o, vnew, sall = _dn_scan_fwd(uu, ww, qg, kd, a_qk, dlast, pd, dn_w)
    w_o, late_rest = late(dn_out)
    wo_a, wo_d = w_o[:ATTN_Q], w_o[ATTN_Q:]
    h1 = _oproj(x, ao, dn_out, wo_a, wo_d, tm)
    w = dict(w, **late_rest(h1))
    w_proj = jnp.transpose(w["w_proj4"], (1, 0, 2)).reshape(PLE_DIM, D_MODEL)
    m, r, h2 = _mlp_fwd(h1, row(sm["norm_mlp"]), w["w_up4"], w["w_down"], tw)
    dh2, dh2b, dgp, dpp, n3, pb, acc_ple = _ple_loss(h2, p, tgt, row(sm["norm_ple"]), row(sm["norm_final"]),
                                                     w["w_gate"], w_proj, tm_s)
    g_w_gate = _wgrad(n3, dgp, "wgrad_gate", D_MODEL, D_MODEL, tw)
    g_w_proj = _wgrad(pb, dpp, "wgrad_proj", PLE_DIM, D_MODEL, tw)
    da, dh1, dh1b, acc_mlp = _mlp_bwd(dh2, dh2b, r, h1, row(sm["norm_mlp"]), w["w_up4"], w["w_down"], tm)
    g_w_up4 = _wgrad(m, da, "wgrad_up", D_MODEL, FF_BLOCK, tw, stacked=True)
    g_w_down = _wgrad(r, dh2b, "wgrad_down", FF_BLOCK, D_MODEL, tw,
                      prep=lambda rv: jnp.square(rv.astype(F32)).astype(MXU_DTYPE))
    g_w_o = _wgrad_cat([ao, dn_out], [dh1b], "wgrad_o", tw)
    early = dict(w_up4=g_w_up4, w_down=g_w_down, w_gate=g_w_gate, w_proj=g_w_proj, w_o=g_w_o)
    dep = reducer.start(early)
    dao, ddn = _oproj_bwd(dh1b, wo_a, wo_d, tm, dep)
    dz, du, dw, dqg, dkd, da_qk, ddlast, acc_dn = _dn_scan_bwd(ddn, o, vnew, sall, ww, qg, kd, a_qk, dlast, pd, dn_w)
    dep = reducer.middle(du)
    dqn, dkn, dvc, dgc = _dn_intra_bwd(qn, kn, vc, gc, gr, tinv, a_qk, du, dw, dqg, dkd, da_qk, ddlast, dlast, dep)
    d_dn, acc_conv, acc_gate = _dn_prep_bwd(pd, conv_w, al_vec, dt_vec, dqn, dkn, dvc, dgc, dz, tm_s)
    dq, dk, dv, acc_attn = _attn_bwd(pa, cos, sin, sinks_vec, dao)
    reducer.finish(dq)
    wq, wk, wv = wa[:, :ATTN_Q], wa[:, ATTN_Q:ATTN_Q + ATTN_KV], wa[:, ATTN_Q + ATTN_KV:]
    dx, acc_mix = _inproj_bwd(x, dh1, row(sm["norm_mix"]), [dq, dk, dv, d_dn], [wq, wk, wv, wd], tm_s)

    g_w_in = _wgrad_cat([u], [dq, dk, dv, d_dn], "wgrad_in", tw)[:, :D_IN]
    grads = dict(early, w_in=g_w_in)
    sums = dict(loss=acc_ple[2, 0], norm_final=acc_ple[0], norm_ple=acc_ple[1], norm_mlp=acc_mlp[0], norm_mix=acc_mix[0],
                dn_norm=acc_dn[0], sinks=acc_attn[0, :ATTN_HEADS], a_log=acc_gate[0, DN_HEADS:2 * DN_HEADS],
                dt_bias=acc_gate[1, DN_HEADS:2 * DN_HEADS], conv_w=acc_conv[:DN_CONV])
    return sums, dx, grads


MESH = pl.DeviceIdType.MESH
ANY = pl.BlockSpec(memory_space=pl.ANY)
N_CHIPS = 4
N_DEV = 8


def _place():
    x, y, c = lax.axis_index("x"), lax.axis_index("y"), lax.axis_index("c")
    chips = [(1 - x, y), (x, 1 - y), (1 - x, 1 - y)]
    return x, y, c, chips


def _gather_weights(shards, conv_s):
    n = len(shards)
    per = 7

    def body(*refs):
        in_refs, conv_ref = refs[:n], refs[n]
        out_refs, conv_out = refs[n + 1:2 * n + 1], refs[2 * n + 1]
        send_sems, recv_sems = refs[2 * n + 2:]
        x, y, c, chips = _place()
        sibling = (x, y, 1 - c)

        def blk(a, px, py, pc):
            hr = in_refs[a].shape[0] // 2
            return out_refs[a].at[2 * px + py, pl.ds(pc * hr, hr), :]

        def mine(a):
            hr = in_refs[a].shape[0] // 2
            return in_refs[a].at[pl.ds(c * hr, hr), :]

        def rcopy(a, k, block, to, src=None):
            return pltpu.make_async_remote_copy(
                src_ref=blk(a, *block) if src is None else src, dst_ref=blk(a, *block),
                send_sem=send_sems.at[per * a + k], recv_sem=recv_sems.at[per * a + k],
                device_id=to, device_id_type=MESH)

        def whole(a, to):
            return pltpu.make_async_remote_copy(
                src_ref=in_refs[a], dst_ref=out_refs[a].at[2 * x + y],
                send_sem=send_sems.at[per * a], recv_sem=recv_sems.at[per * a], device_id=to, device_id_type=MESH)

        def ccopy(j, to):
            return pltpu.make_async_remote_copy(
                src_ref=conv_ref, dst_ref=conv_out.at[2 * x + y],
                send_sem=send_sems.at[per * n + j], recv_sem=recv_sems.at[per * n + j],
                device_id=to, device_id_type=MESH)

        started = []
        for a in range(n):
            first = [whole(a, sibling)]
            first += [rcopy(a, 1 + j, (x, y, c), (*chip, c), src=mine(a)) for j, chip in enumerate(chips)]
            for cp in first:
                cp.start()
            started += first
        conv_sends = [ccopy(j, (*chip, c)) for j, chip in enumerate(chips)] + [ccopy(3, sibling)]
        for cp in conv_sends:
            cp.start()
        started += conv_sends
        for a in range(n):
            for j, chip in enumerate(chips):
                rcopy(a, 1 + j, (*chip, c), (x, y, c)).wait_recv()
                fwd = rcopy(a, 4 + j, (*chip, c), sibling)
                fwd.start()
                started.append(fwd)
        for a in range(n):
            whole(a, sibling).wait_recv()
            for j, chip in enumerate(chips):
                rcopy(a, 4 + j, (*chip, 1 - c), (x, y, c)).wait_recv()
        for j, chip in enumerate(chips + [(x, y)]):
            pltpu.make_async_remote_copy(
                src_ref=conv_ref, dst_ref=conv_out.at[2 * chip[0] + chip[1]],
                send_sem=send_sems.at[per * n + j], recv_sem=recv_sems.at[per * n + j],
                device_id=sibling, device_id_type=MESH).wait_recv()
        for cp in started:
            cp.wait_send()

    nsem = per * n + 4
    out_shape = [jax.ShapeDtypeStruct((N_CHIPS,) + s.shape, s.dtype) for s in shards]
    out_shape.append(jax.ShapeDtypeStruct((N_CHIPS,) + conv_s.shape, conv_s.dtype))
    return pl.pallas_call(
        body, name="gather_weights", in_specs=[ANY] * (n + 1), out_specs=[ANY] * (n + 1), out_shape=out_shape,
        scratch_shapes=[pltpu.SemaphoreType.DMA((nsem,)), pltpu.SemaphoreType.DMA((nsem,))],
    )(*shards, conv_s)


HBM = pl.BlockSpec(memory_space=pltpu.HBM)
SEM = pl.BlockSpec(memory_space=pltpu.SEMAPHORE)
EFFECT = pltpu.SideEffectType.DATAFLOW_SIDE_EFFECTING
LATE_COPIES = 7


def _late_copies(in_refs, land_refs, send_sems, recv_sems, only=None):
    x, y, c, chips = _place()
    sends, arrivals = [], []
    for a, (src, land) in enumerate(zip(in_refs, land_refs)):
        if only is not None and a not in only:
            continue
        hr = src.shape[0] // 2
        base = LATE_COPIES * a

        def cp(src_ref, dst_ref, s_idx, r_idx, to):
            return pltpu.make_async_remote_copy(src_ref=src_ref, dst_ref=dst_ref, send_sem=send_sems.at[base + s_idx],
                                                recv_sem=recv_sems.at[base + r_idx], device_id=to, device_id_type=MESH)

        sends.append(cp(src, land.at[2 * x + y], 0, 0, (x, y, 1 - c)))
        arrivals.append(cp(src, land.at[2 * x + y], 0, 0, (x, y, 1 - c)))
        for j, chip in enumerate(chips):
            for pc in range(2):
                half = src.at[pl.ds(c * hr, hr), :]
                sends.append(cp(half, land.at[2 * x + y, pl.ds(c * hr, hr), :], 1 + 2 * j + pc, 1 + 2 * j + c, (*chip, pc)))
                arrivals.append(cp(half, land.at[2 * chip[0] + chip[1], pl.ds(pc * hr, hr), :], 1 + 2 * j + pc,
                                   1 + 2 * j + pc, (*chip, pc)))
    return sends, arrivals


def _copies_start(name, build, nsem, srcs, land_shapes, after):
    n = len(srcs)

    def body(*refs):
        sends, _ = build(refs[:n], refs[n:2 * n], refs[2 * n + 1], refs[2 * n + 2])
        for cp in sends:
            cp.start()
        refs[-1][...] = jnp.zeros_like(refs[-1])

    lands = [pltpu.with_memory_space_constraint(lax.empty(s.shape, s.dtype), pltpu.HBM) for s in land_shapes]
    ins = [pltpu.with_memory_space_constraint(s, pltpu.HBM) for s in srcs]
    out = pl.pallas_call(
        body, name=name,
        out_shape=(pltpu.SemaphoreType.DMA((nsem,)), pltpu.SemaphoreType.DMA((nsem,)),
                   *[pltpu.HBM(s.shape, s.dtype) for s in srcs], *[pltpu.HBM(s.shape, s.dtype) for s in land_shapes],
                   jax.ShapeDtypeStruct((8, 128), F32)),
        in_specs=[HBM] * (2 * n) + [ANY],
        out_specs=(SEM, SEM, *[HBM] * (2 * n), pl.BlockSpec(memory_space=pltpu.VMEM)),
        input_output_aliases={i: 2 + i for i in range(2 * n)},
        compiler_params=pltpu.CompilerParams(has_side_effects=EFFECT),
    )(*ins, *lands, after)
    return out[0], out[1], out[2:2 + n], out[2 + n:2 + 2 * n], out[-1]


def _copies_wait(name, build, started, after):
    send_sems, recv_sems, srcs, lands, _ = started
    n = len(srcs)

    def body(*refs):
        sends, arrivals = build(refs[:n], refs[n:2 * n], refs[2 * n], refs[2 * n + 1])
        for cp in sends:
            cp.wait_send()
        for cp in arrivals:
            cp.wait_recv()

    out = pl.pallas_call(
        body, name=name,
        out_shape=(*[pltpu.HBM(s.shape, s.dtype) for s in srcs], *[pltpu.HBM(l.shape, l.dtype) for l in lands]),
        in_specs=[HBM] * (2 * n) + [SEM, SEM, ANY],
        out_specs=tuple([HBM] * (2 * n)),
        input_output_aliases={i: i for i in range(2 * n)},
        compiler_params=pltpu.CompilerParams(has_side_effects=EFFECT),
    )(*srcs, *lands, send_sems, recv_sems, after)
    return out[:n], out[n:]


def _exchange_copies(g_refs, got_refs, send_sems, recv_sems):
    x, y, c, _ = _place()
    sends, arrivals = [], []
    for a, (g, got) in enumerate(zip(g_refs, got_refs)):
        hr = g.shape[1] // 2
        cp = pltpu.make_async_remote_copy(
            src_ref=g.at[:, pl.ds((1 - c) * hr, hr), :], dst_ref=got, send_sem=send_sems.at[a],
            recv_sem=recv_sems.at[a], device_id=(x, y, 1 - c), device_id_type=MESH)
        sends.append(cp)
        arrivals.append(cp)
    return sends, arrivals


def _scatter_copies(s_refs, got_refs, send_sems, recv_sems):
    x, y, c, chips = _place()
    sends, arrivals = [], []
    for a, (s16, got) in enumerate(zip(s_refs, got_refs)):
        for j, chip in enumerate(chips):
            cp = pltpu.make_async_remote_copy(
                src_ref=s16.at[2 * chip[0] + chip[1]], dst_ref=got.at[j], send_sem=send_sems.at[3 * a + j],
                recv_sem=recv_sems.at[3 * a + j], device_id=(*chip, c), device_id_type=MESH)
            sends.append(cp)
            arrivals.append(cp)
    return sends, arrivals


def _share_halves(name, bufs, dep):
    n = len(bufs)

    def body(*refs):
        out_refs = refs[n + 1:2 * n + 1]
        send_sems, recv_sems = refs[2 * n + 1:]
        x, y, c, _ = _place()
        remote = [pltpu.make_async_remote_copy(
            src_ref=out_refs[a].at[c], dst_ref=out_refs[a].at[c], send_sem=send_sems.at[a], recv_sem=recv_sems.at[a],
            device_id=(x, y, 1 - c), device_id_type=MESH) for a in range(n)]
        for cp in remote:
            cp.start()
        for a in range(n):
            pltpu.make_async_remote_copy(
                src_ref=out_refs[a].at[c], dst_ref=out_refs[a].at[1 - c], send_sem=send_sems.at[a],
                recv_sem=recv_sems.at[a], device_id=(x, y, 1 - c), device_id_type=MESH).wait_recv()
        for cp in remote:
            cp.wait_send()

    return pl.pallas_call(
        body, name=name, in_specs=[ANY] * (n + 1), out_specs=[ANY] * n,
        out_shape=[jax.ShapeDtypeStruct(b.shape, b.dtype) for b in bufs],
        input_output_aliases={a: a for a in range(n)},
        scratch_shapes=[pltpu.SemaphoreType.DMA((n,)), pltpu.SemaphoreType.DMA((n,))],
    )(*bufs, dep)


SMALL_ROWS, SMALL_COLS = 16, CONV_CH


def _allreduce_small(block):
    m_per, ncol = block.shape

    def body(x_ref, sum_ref, all_ref, send_sems, recv_sems, local_sem):
        x, y, c, chips = _place()
        me, sibling = (x, y, c), (x, y, 1 - c)

        def rows(px, py, pc):
            return all_ref.at[pl.ds((4 * px + 2 * py + pc) * m_per, m_per), :]

        def copy(k, block_of, to, src=None):
            return pltpu.make_async_remote_copy(
                src_ref=rows(*block_of) if src is None else src, dst_ref=rows(*block_of),
                send_sem=send_sems.at[k], recv_sem=recv_sems.at[k], device_id=to, device_id_type=MESH)

        mine = pltpu.make_async_copy(x_ref, rows(*me), local_sem)
        mine.start()
        first = [copy(0, me, sibling, src=x_ref)]
        first += [copy(1 + j, me, (*chip, c), src=x_ref) for j, chip in enumerate(chips)]
        for cp in first:
            cp.start()
        passed = [copy(4 + j, (*chip, c), sibling) for j, chip in enumerate(chips)]
        for j, chip in enumerate(chips):
            copy(1 + j, (*chip, c), me).wait_recv()
            passed[j].start()
        copy(0, sibling, me).wait_recv()
        for j, chip in enumerate(chips):
            copy(4 + j, (*chip, 1 - c), me).wait_recv()
        for cp in first + passed:
            cp.wait_send()
        mine.wait()
        total = all_ref[0:m_per, :]
        for d in range(1, N_DEV):
            total = total + all_ref[d * m_per:(d + 1) * m_per, :]
        sum_ref[...] = total

    vm = pl.BlockSpec(memory_space=pltpu.VMEM)
    return pl.pallas_call(
        body, name="allreduce_small", in_specs=[vm], out_specs=vm,
        out_shape=jax.ShapeDtypeStruct((m_per, ncol), F32),
        scratch_shapes=[pltpu.VMEM((N_DEV * m_per, ncol), F32), pltpu.SemaphoreType.DMA((7,)),
                        pltpu.SemaphoreType.DMA((7,)), pltpu.SemaphoreType.DMA],
    )(block)


def _row_tile(rows, cols):
    tile = rows
    while tile * cols * 4 > (1 << 20) and tile % 16 == 0:
        tile //= 2
    return tile


def _elementwise(fn, name, ins, out_dtypes, dep):
    rows, cols = ins[0].shape
    tile = _row_tile(rows, cols)

    def body(*refs):
        outs = fn(*[r[...] for r in refs[:len(ins)]])
        for o_ref, o in zip(refs[len(ins) + 1:], outs):
            o_ref[...] = o.astype(o_ref.dtype)

    if tile * cols * 4 > (1 << 21) and cols % 512 == 0:
        spec = pl.BlockSpec((rows, 256), lambda i: (0, i))
        steps = cols // 256
    else:
        spec = pl.BlockSpec((tile, cols), lambda i: (i, 0))
        steps = rows // tile
    return pl.pallas_call(
        body, name=name, grid=(steps,), in_specs=[spec] * len(ins) + [pl.BlockSpec(memory_space=pl.ANY)],
        out_specs=[spec] * len(out_dtypes),
        out_shape=[jax.ShapeDtypeStruct((rows, cols), d) for d in out_dtypes],
        compiler_params=_params("parallel"),
    )(*ins, dep)


def _adamw_tile(w, g, m, v):
    m = ADAM_B1 * m + (1.0 - ADAM_B1) * g
    v = ADAM_B2 * v + (1.0 - ADAM_B2) * jnp.square(g)
    m_hat = m / (1.0 - ADAM_B1 ** ADAM_STEP)
    v_hat = v / (1.0 - ADAM_B2 ** ADAM_STEP)
    delta = -ADAM_LR * (m_hat / (jnp.sqrt(v_hat) + ADAM_EPS) + ADAM_WD * w)
    return delta, m, v


def _adamw(name, w, g, m, v, dep):
    return _elementwise(_adamw_tile, name, [w, g, m, v], [F32, F32, F32], dep)


def _chip_sum(name, g4, got, place):
    nchip, hr, cols = got.shape
    tile = _row_tile(hr, cols)
    nblk = hr // tile

    def body(pl_ref, g_ref, o_ref, s32_ref, s16_ref):
        s = g_ref[...] + o_ref[...]
        s32_ref[...] = s
        s16_ref[...] = s.astype(BF16)

    spec = pl.BlockSpec((None, tile, cols), lambda k, i, pr: (k, i, 0))
    return pl.pallas_call(
        body, name=name,
        grid_spec=pltpu.PrefetchScalarGridSpec(
            num_scalar_prefetch=1, grid=(nchip, nblk),
            in_specs=[pl.BlockSpec((None, tile, cols), lambda k, i, pr: (k, pr[1] * nblk + i, 0)), spec],
            out_specs=[spec, spec]),
        out_shape=[jax.ShapeDtypeStruct(got.shape, F32), jax.ShapeDtypeStruct(got.shape, BF16)],
        compiler_params=_params("parallel", "parallel"),
    )(place, g4, got)


def _mesh_sum(name, s32, got, place):
    _, hr, cols = s32.shape
    tile = _row_tile(hr, cols)

    def body(pl_ref, own_ref, g0_ref, g1_ref, g2_ref, o_ref):
        o_ref[...] = ((own_ref[...] + g0_ref[...].astype(F32)) + g1_ref[...].astype(F32)) + g2_ref[...].astype(F32)

    slab = lambda j: pl.BlockSpec((None, tile, cols), lambda i, pr: (j, i, 0))
    return pl.pallas_call(
        body, name=name,
        grid_spec=pltpu.PrefetchScalarGridSpec(
            num_scalar_prefetch=1, grid=(hr // tile,),
            in_specs=[pl.BlockSpec((None, tile, cols), lambda i, pr: (pr[0], i, 0)), slab(0), slab(1), slab(2)],
            out_specs=pl.BlockSpec((None, tile, cols), lambda i, pr: (pr[1], i, 0))),
        out_shape=jax.ShapeDtypeStruct((2, hr, cols), F32),
        compiler_params=_params("parallel"),
    )(place, s32, got, got, got)


def _place_operand():
    return jnp.stack([2 * lax.axis_index("x") + lax.axis_index("y"), lax.axis_index("c")]).astype(jnp.int32)


def _per_chip(name, g):
    if name == "w_in":
        return jnp.transpose(g.reshape(D_MODEL, N_CHIPS, D_IN // N_CHIPS), (1, 0, 2))
    if name == "w_proj":
        return jnp.transpose(g.reshape(PLE_DIM, N_CHIPS, D_MODEL // N_CHIPS), (1, 0, 2))
    if name == "w_up4":
        return g
    return g.reshape(N_CHIPS, g.shape[0] // N_CHIPS, g.shape[1])


class _EarlyReducer:
    def __init__(self, tag):
        self.tag = tag

    def start(self, grads):
        self.names = list(grads)
        self.place = _place_operand()
        slabs = [_per_chip(k, grads[k]) for k in self.names]
        halves = [jax.ShapeDtypeStruct((s.shape[0], s.shape[1] // 2, s.shape[2]), F32) for s in slabs]
        self.a = _copies_start(self.tag + "exchange_start", _exchange_copies, len(slabs), slabs, halves,
                               slabs[0][0, :8, :128])
        return self.a[-1]

    def middle(self, after):
        slabs, got = _copies_wait(self.tag + "exchange_wait", _exchange_copies, self.a, after)
        self.sums = [_chip_sum(self.tag + "chip_sum_" + k, s, g, self.place) for k, s, g in zip(self.names, slabs, got)]
        s16 = [s[1] for s in self.sums]
        lands = [jax.ShapeDtypeStruct((3,) + s.shape[1:], BF16) for s in s16]
        self.b = _copies_start(self.tag + "scatter_start", _scatter_copies, 3 * len(s16), s16, lands,
                               self.sums[0][0][0, :8, :128])
        return self.b[-1]

    def finish(self, after):
        _, got = _copies_wait(self.tag + "scatter_wait", _scatter_copies, self.b, after)
        self.bufs = {k: _mesh_sum(self.tag + "mesh_sum_" + k, s[0], g, self.place)
                     for k, s, g in zip(self.names, self.sums, got)}


def kernel(x, p, norm_mix, w_in, conv_w, a_log, dt_bias, dn_norm, sinks, w_o, norm_mlp, w_up, w_down, norm_ple, w_ple_gate, w_ple_proj, norm_final, loss_target, m_norm_mix, m_w_in, m_conv_w, m_a_log, m_dt_bias, m_dn_norm, m_sinks, m_w_o, m_norm_mlp, m_w_up, m_w_down, m_norm_ple, m_w_ple_gate, m_w_ple_proj, m_norm_final, v_norm_mix, v_w_in, v_conv_w, v_a_log, v_dt_bias, v_dn_norm, v_sinks, v_w_o, v_norm_mlp, v_w_up, v_w_down, v_norm_ple, v_w_ple_gate, v_w_ple_proj, v_norm_final):
    chip = 2 * lax.axis_index("x") + lax.axis_index("y")
    big = dict(w_in=w_in[0], w_o=w_o[0], w_up=w_up[0], w_down=w_down[0], w_gate=w_ple_gate[0], w_proj=w_ple_proj[0])
    big_m = dict(w_in=m_w_in[0], w_o=m_w_o[0], w_up=m_w_up[0], w_down=m_w_down[0], w_gate=m_w_ple_gate[0], w_proj=m_w_ple_proj[0])
    big_v = dict(w_in=v_w_in[0], w_o=v_w_o[0], w_up=v_w_up[0], w_down=v_w_down[0], w_gate=v_w_ple_gate[0], w_proj=v_w_ple_proj[0])
    names = list(big)

    w_in_all, conv_all = _gather_weights([big["w_in"].astype(BF16)], conv_w[0])
    late_names = names[1:]
    late_shards = [big[k].astype(BF16) for k in late_names]
    gather = _copies_start("gather_start", _late_copies, LATE_COPIES * len(late_shards), late_shards,
                           [jax.ShapeDtypeStruct((N_CHIPS,) + s.shape, BF16) for s in late_shards], w_in_all)
    token = gather[-1]
    w = dict(w_in=jnp.transpose(w_in_all, (1, 0, 2)).reshape(D_MODEL, D_IN),
             conv_w=jnp.transpose(conv_all, (1, 0, 2)).reshape(DN_CONV, CONV_CH))
    sm = dict(norm_mix=norm_mix[0] + token[0, 0], a_log=a_log[0], dt_bias=dt_bias[0], dn_norm=dn_norm[0],
              sinks=sinks[0], norm_mlp=norm_mlp[0], norm_ple=norm_ple[0], norm_final=norm_final)

    def late(after):
        first = functools.partial(_late_copies, only=(0,))
        srcs, lands = _copies_wait("gather_wait_o", first, gather, after)

        def rest(after2):
            others = functools.partial(_late_copies, only=tuple(range(1, len(late_names))))
            gw = dict(zip(late_names, _copies_wait("gather_wait_rest", others, gather[:2] + (srcs, lands, None), after2)[1]))
            return dict(w_up4=gw["w_up"], w_down=gw["w_down"].reshape(D_FF, D_MODEL),
                        w_gate=gw["w_gate"].reshape(D_MODEL, D_MODEL), w_proj4=gw["w_proj"])

        return lands[0].reshape(D_MODEL, D_MODEL), rest

    reducer = _EarlyReducer("early_")
    sums, grad_x, g = _local_step(x[0], p[0, 0], loss_target[0], sm, w, late, reducer)

    last = _EarlyReducer("last_")
    dep_a = last.start({"w_in": g["w_in"]})

    row = lambda v: jnp.zeros((SMALL_COLS,), F32).at[:v.shape[0]].set(v)
    misc = jnp.zeros((SMALL_COLS,), F32).at[0:4].set(sums["a_log"]).at[4:8].set(sums["dt_bias"]) \
        .at[8:16].set(sums["sinks"]).at[128:256].set(sums["dn_norm"]).at[256].set(sums["loss"])
    small = jnp.concatenate([sums["conv_w"], jnp.stack([row(sums["norm_mix"]), row(sums["norm_mlp"]), row(sums["norm_ple"]),
                                                        row(sums["norm_final"]), misc]),
                             jnp.zeros((SMALL_ROWS - 9, SMALL_COLS), F32)], axis=0)
    tot = _allreduce_small(small + dep_a[0, 0])
    dep_b = last.middle(tot)
    grad_key = dict(w_o="w_o", w_up="w_up4", w_down="w_down", w_gate="w_gate", w_proj="w_proj")
    full = _share_halves("share_halves", [reducer.bufs[grad_key[k]] for k in late_names], dep_b)
    red = {k: f.reshape(-1, f.shape[-1]) for k, f in zip(late_names, full)}
    loss = tot[8, 256]
    ncw = CONV_CH // N_CHIPS

    def pack(cw, nmix, nmlp, nple, nfin, al, dtb, sk, dnn):
        misc_p = jnp.zeros((SMALL_COLS,), F32).at[0:4].set(al).at[4:8].set(dtb).at[8:16].set(sk).at[128:256].set(dnn)
        cw_p = jnp.zeros((DN_CONV, SMALL_COLS), F32).at[:, :ncw].set(cw)
        return jnp.concatenate([cw_p, jnp.stack([row(nmix), row(nmlp), row(nple), row(nfin), misc_p]),
                                jnp.zeros((SMALL_ROWS - 9, SMALL_COLS), F32)], axis=0)

    def unpack(buf):
        return dict(conv_w=buf[0:4, :ncw][None], norm_mix=buf[4, :D_MODEL][None], norm_mlp=buf[5, :D_MODEL][None],
                    norm_ple=buf[6, :D_MODEL][None], norm_final=buf[7, :D_MODEL], a_log=buf[8, 0:4][None],
                    dt_bias=buf[8, 4:8][None], sinks=buf[8, 8:16][None], dn_norm=buf[8, 128:256][None])

    g_conv_shard = lax.dynamic_slice(tot[0:4], (0, chip * ncw), (DN_CONV, ncw))
    g_small = pack(g_conv_shard, tot[4, :D_MODEL], tot[5, :D_MODEL], tot[6, :D_MODEL], tot[7, :D_MODEL],
                   tot[8, 0:4], tot[8, 4:8], tot[8, 8:16], tot[8, 128:256])
    w_small = pack(conv_w[0], norm_mix[0], norm_mlp[0], norm_ple[0], norm_final, a_log[0], dt_bias[0], sinks[0], dn_norm[0])
    m_small = pack(m_conv_w[0], m_norm_mix[0], m_norm_mlp[0], m_norm_ple[0], m_norm_final, m_a_log[0], m_dt_bias[0],
                   m_sinks[0], m_dn_norm[0])
    v_small = pack(v_conv_w[0], v_norm_mix[0], v_norm_mlp[0], v_norm_ple[0], v_norm_final, v_a_log[0], v_dt_bias[0],
                   v_sinks[0], v_dn_norm[0])

    ref_name = dict(w_in="w_in", w_o="w_o", w_up="w_up", w_down="w_down", w_gate="w_ple_gate", w_proj="w_ple_proj")
    out_g, out_d, out_m, out_v = {}, {}, {}, {}

    def update(k, dep):
        d_k, m_k, v_k = _adamw("adamw_" + k, big[k], red[k], big_m[k], big_v[k], dep)
        out_g[ref_name[k]], out_d[ref_name[k]] = red[k][None], d_k[None]
        out_m[ref_name[k]], out_v[ref_name[k]] = m_k[None], v_k[None]
        return d_k

    for k in late_names:
        done = update(k, dep_b)
    small_out = _adamw("adamw_small", w_small, g_small, m_small, v_small, dep_b)
    d_s, m_s, v_s = (unpack(b) for b in small_out)
    g_s = unpack(g_small)
    for src, dst in ((g_s, out_g), (d_s, out_d), (m_s, out_m), (v_s, out_v)):
        dst.update(src)
    last.finish(done + small_out[0][0:1, 0:1])
    (w_in_full,) = _share_halves("share_halves_w_in", [last.bufs["w_in"]], dep_b)
    red["w_in"] = w_in_full.reshape(-1, w_in_full.shape[-1])
    d_t, m_t, v_t = _adamw("adamw_w_in", big["w_in"].T, red["w_in"].T, big_m["w_in"].T, big_v["w_in"].T, dep_b)
    out_g["w_in"], out_d["w_in"], out_m["w_in"], out_v["w_in"] = red["w_in"][None], d_t.T[None], m_t.T[None], v_t.T[None]
    order = ["norm_mix", "w_in", "conv_w", "a_log", "dt_bias", "dn_norm", "sinks", "w_o", "norm_mlp", "w_up", "w_down",
             "norm_ple", "w_ple_gate", "w_ple_proj", "norm_final"]
    return (loss, grad_x[None], *[out_g[k] for k in order], *[out_d[k] for k in order],
            *[out_m[k] for k in order], *[out_v[k] for k in order])
```

```python
import functools

import jax
import jax.numpy as jnp
from jax import lax
from jax.experimental import pallas as pl
from jax.experimental.pallas import tpu as pltpu

F32 = jnp.float32
BF16 = jnp.bfloat16
MXU_DTYPE = jnp.bfloat16
HI = lax.Precision.HIGHEST

D_MODEL = 1024
PLE_DIM = 256
ATTN_HEADS = 8
ATTN_KV_HEADS = 2
ATTN_GROUPS = ATTN_HEADS // ATTN_KV_HEADS
ATTN_HEAD_DIM = 64
ATTN_BLOCK = 128
ROPE_THETA = 10000.0
DN_HEADS = 4
DN_HEAD_DIM = 128
DN_CONV = 4
DN_CHUNK = 64
D_FF = 4 * D_MODEL
EPS = 1e-6
ATTN_Q = ATTN_HEADS * ATTN_HEAD_DIM
ATTN_KV = ATTN_KV_HEADS * ATTN_HEAD_DIM
DN_W = DN_HEADS * DN_HEAD_DIM
CONV_CH = 3 * DN_W
D_IN = ATTN_Q + 2 * ATTN_KV + 4 * DN_W + 2 * DN_HEADS
DN_COLS = 4 * DN_W + 128
DN_SCALE = DN_HEAD_DIM ** -0.5
ATTN_SCALE = ATTN_HEAD_DIM ** -0.5
FF_BLOCKS = 4
FF_BLOCK = D_FF // FF_BLOCKS

ADAM_LR = 0.001
ADAM_B1 = 0.9
ADAM_B2 = 0.999
ADAM_EPS = 1e-08
ADAM_WD = 0.01
ADAM_STEP = 10

V7X_VMEM_BYTES = 64 * 1024 * 1024
VMEM_LIMIT = 48 * 1024 * 1024

NN = ((1,), (0,))
NT = ((1,), (1,))
TN = ((0,), (0,))


def _dot(a, b, dims=NN, prec=None):
    return lax.dot_general(a, b, (dims, ((), ())), precision=prec, preferred_element_type=F32)


def _sigmoid(x):
    return 1.0 / (1.0 + jnp.exp(-x))


def _softplus(x):
    return jnp.maximum(x, 0.0) + jnp.log(1.0 + jnp.exp(-jnp.abs(x)))


def _params(*sem):
    return pltpu.CompilerParams(dimension_semantics=sem, vmem_limit_bytes=VMEM_LIMIT)


def _rms_fwd(xv, g):
    r = lax.rsqrt(jnp.mean(xv * xv, axis=-1, keepdims=True) + EPS)
    return xv * r * g


def _rms_bwd(xv, g, dn):
    r = lax.rsqrt(jnp.mean(xv * xv, axis=-1, keepdims=True) + EPS)
    xh = xv * r
    dg = jnp.sum(dn * xh, axis=0, keepdims=True)
    dxh = dn * g
    dx = r * (dxh - xh * jnp.mean(dxh * xh, axis=-1, keepdims=True))
    return dx, dg


def _full(shape):
    return pl.BlockSpec(shape, lambda *_: (0,) * len(shape))


def _inproj(x, g_mix, wa, wd, tm):
    t = x.shape[0]

    def body(x_ref, g_ref, wa_ref, wd_ref, u_ref, pa_ref, pd_ref):
        u = _rms_fwd(x_ref[...], g_ref[...]).astype(MXU_DTYPE)
        u_ref[...] = u
        pa_ref[...] = _dot(u, wa_ref[...])
        pd_ref[...] = _dot(u, wd_ref[...])

    na, nd = wa.shape[1], wd.shape[1]
    return pl.pallas_call(
        body, name="inproj", grid=(t // tm,),
        in_specs=[pl.BlockSpec((tm, D_MODEL), lambda i: (i, 0)), _full((1, D_MODEL)),
                  _full((D_MODEL, na)), _full((D_MODEL, nd))],
        out_specs=[pl.BlockSpec((tm, D_MODEL), lambda i: (i, 0)), pl.BlockSpec((tm, na), lambda i: (i, 0)),
                   pl.BlockSpec((tm, nd), lambda i: (i, 0))],
        out_shape=[jax.ShapeDtypeStruct((t, D_MODEL), MXU_DTYPE), jax.ShapeDtypeStruct((t, na), F32),
                   jax.ShapeDtypeStruct((t, nd), F32)],
        compiler_params=_params("parallel"),
    )(x, g_mix, wa, wd)


def _oproj(x, ao, dn, wo_a, wo_d, tm):
    t = x.shape[0]

    def body(x_ref, ao_ref, dn_ref, wa_ref, wd_ref, h_ref):
        h_ref[...] = (x_ref[...] + _dot(ao_ref[...].astype(MXU_DTYPE), wa_ref[...])
                      + _dot(dn_ref[...].astype(MXU_DTYPE), wd_ref[...]))

    half = ao.shape[1]
    return pl.pallas_call(
        body, name="oproj", grid=(t // tm,),
        in_specs=[pl.BlockSpec((tm, D_MODEL), lambda i: (i, 0)), pl.BlockSpec((tm, half), lambda i: (i, 0)),
                  pl.BlockSpec((tm, half), lambda i: (i, 0)), _full((half, D_MODEL)), _full((half, D_MODEL))],
        out_specs=pl.BlockSpec((tm, D_MODEL), lambda i: (i, 0)),
        out_shape=jax.ShapeDtypeStruct((t, D_MODEL), F32),
        compiler_params=_params("parallel"),
    )(x, ao, dn, wo_a, wo_d)


def _mlp_fwd(h1, g_mlp, w_up4, w_down, tm):
    t = h1.shape[0]

    def body(h_ref, g_ref, wu_ref, wd_ref, m_ref, r_ref, h2_ref, acc_ref):
        k = pl.program_id(1)

        @pl.when(k == 0)
        def _():
            m_ref[...] = _rms_fwd(h_ref[...], g_ref[...]).astype(MXU_DTYPE)
            acc_ref[...] = jnp.zeros_like(acc_ref)

        r = jnp.maximum(_dot(m_ref[...], wu_ref[...]), 0.0)
        r_ref[...] = r.astype(MXU_DTYPE)
        s = jnp.square(r).astype(MXU_DTYPE)
        acc_ref[...] += _dot(s, wd_ref[...])

        @pl.when(k == FF_BLOCKS - 1)
        def _():
            h2_ref[...] = h_ref[...] + acc_ref[...]

    return pl.pallas_call(
        body, name="mlp_fwd", grid=(t // tm, FF_BLOCKS),
        in_specs=[pl.BlockSpec((tm, D_MODEL), lambda i, k: (i, 0)), _full((1, D_MODEL)),
                  pl.BlockSpec((None, D_MODEL, FF_BLOCK), lambda i, k: (k, 0, 0)),
                  pl.BlockSpec((FF_BLOCK, D_MODEL), lambda i, k: (k, 0))],
        out_specs=[pl.BlockSpec((tm, D_MODEL), lambda i, k: (i, 0)), pl.BlockSpec((tm, FF_BLOCK), lambda i, k: (i, k)),
                   pl.BlockSpec((tm, D_MODEL), lambda i, k: (i, 0))],
        out_shape=[jax.ShapeDtypeStruct((t, D_MODEL), MXU_DTYPE), jax.ShapeDtypeStruct((t, D_FF), MXU_DTYPE),
                   jax.ShapeDtypeStruct((t, D_MODEL), F32)],
        scratch_shapes=[pltpu.VMEM((tm, D_MODEL), F32)],
        compiler_params=_params("parallel", "arbitrary"),
    )(h1, g_mlp, w_up4, w_down)


def _ple_loss(h2, p, tgt, g_ple, g_fin, w_gate, w_proj, tm):
    t = h2.shape[0]

    def body(h_ref, p_ref, t_ref, gp_ref, gf_ref, wg_ref, wp_ref,
             dh_ref, dhb_ref, dgp_ref, dpp_ref, n3_ref, pb_ref, acc_ref):
        @pl.when(pl.program_id(0) == 0)
        def _():
            acc_ref[...] = jnp.zeros_like(acc_ref)

        h = h_ref[...]
        g_ple_v, g_fin_v = gp_ref[...], gf_ref[...]
        n3 = _rms_fwd(h, g_ple_v).astype(MXU_DTYPE)
        n3_ref[...] = n3
        gate = _sigmoid(_dot(n3, wg_ref[...]))
        pb = p_ref[...].astype(MXU_DTYPE)
        pb_ref[...] = pb
        pp = _dot(pb, wp_ref[...])
        h3 = h + gate * pp
        r4 = lax.rsqrt(jnp.mean(h3 * h3, axis=-1, keepdims=True) + EPS)
        xh4 = h3 * r4
        e = xh4 * g_fin_v - t_ref[...]
        loss = 0.5 * jnp.sum(jnp.mean(e * e, axis=-1, keepdims=True), axis=0, keepdims=True)
        dy = e * (1.0 / D_MODEL)
        dg_fin = jnp.sum(dy * xh4, axis=0, keepdims=True)
        dxh = dy * g_fin_v
        dh3 = r4 * (dxh - xh4 * jnp.mean(dxh * xh4, axis=-1, keepdims=True))
        dpp_ref[...] = (dh3 * gate).astype(MXU_DTYPE)
        dgp = (dh3 * pp * gate * (1.0 - gate)).astype(MXU_DTYPE)
        dgp_ref[...] = dgp
        dn3 = _dot(dgp, wg_ref[...], NT)
        dx, dg_ple = _rms_bwd(h, g_ple_v, dn3)
        dh2 = dh3 + dx
        dh_ref[...] = dh2
        dhb_ref[...] = dh2.astype(MXU_DTYPE)
        acc_ref[0:1, :] += dg_fin
        acc_ref[1:2, :] += dg_ple
        acc_ref[2:3, :] += jnp.broadcast_to(loss, (1, D_MODEL))

    row = lambda w: pl.BlockSpec((tm, w), lambda i: (i, 0))
    return pl.pallas_call(
        body, name="ple_loss", grid=(t // tm,),
        in_specs=[row(D_MODEL), row(PLE_DIM), row(D_MODEL), _full((1, D_MODEL)), _full((1, D_MODEL)),
                  _full((D_MODEL, D_MODEL)), _full((PLE_DIM, D_MODEL))],
        out_specs=[row(D_MODEL), row(D_MODEL), row(D_MODEL), row(D_MODEL), row(D_MODEL), row(PLE_DIM),
                   _full((8, D_MODEL))],
        out_shape=[jax.ShapeDtypeStruct((t, D_MODEL), F32), jax.ShapeDtypeStruct((t, D_MODEL), MXU_DTYPE),
                   jax.ShapeDtypeStruct((t, D_MODEL), MXU_DTYPE), jax.ShapeDtypeStruct((t, D_MODEL), MXU_DTYPE),
                   jax.ShapeDtypeStruct((t, D_MODEL), MXU_DTYPE), jax.ShapeDtypeStruct((t, PLE_DIM), MXU_DTYPE),
                   jax.ShapeDtypeStruct((8, D_MODEL), F32)],
        compiler_params=_params("arbitrary"),
    )(h2, p, tgt, g_ple, g_fin, w_gate, w_proj)


def _mlp_bwd(dh2, dh2b, r, h1, g_mlp, w_up4, w_down, tm):
    t = h1.shape[0]

    def body(dh_ref, dhb_ref, r_ref, h_ref, g_ref, wu_ref, wd_ref,
             da_ref, dh1_ref, dh1b_ref, acc_ref, dm_ref):
        i, k = pl.program_id(0), pl.program_id(1)

        @pl.when((i == 0) & (k == 0))
        def _():
            acc_ref[...] = jnp.zeros_like(acc_ref)

        @pl.when(k == 0)
        def _():
            dm_ref[...] = jnp.zeros_like(dm_ref)

        ds = _dot(dhb_ref[...], wd_ref[...], NT)
        da = (ds * (2.0 * r_ref[...].astype(F32))).astype(MXU_DTYPE)
        da_ref[...] = da
        dm_ref[...] += _dot(da, wu_ref[...], NT)

        @pl.when(k == FF_BLOCKS - 1)
        def _():
            dx, dg = _rms_bwd(h_ref[...], g_ref[...], dm_ref[...])
            dh1 = dh_ref[...] + dx
            dh1_ref[...] = dh1
            dh1b_ref[...] = dh1.astype(MXU_DTYPE)
            acc_ref[0:1, :] += dg

    tok = lambda w: pl.BlockSpec((tm, w), lambda i, k: (i, 0))
    return pl.pallas_call(
        body, name="mlp_bwd", grid=(t // tm, FF_BLOCKS),
        in_specs=[tok(D_MODEL), tok(D_MODEL), pl.BlockSpec((tm, FF_BLOCK), lambda i, k: (i, k)), tok(D_MODEL),
                  _full((1, D_MODEL)), pl.BlockSpec((None, D_MODEL, FF_BLOCK), lambda i, k: (k, 0, 0)),
                  pl.BlockSpec((FF_BLOCK, D_MODEL), lambda i, k: (k, 0))],
        out_specs=[pl.BlockSpec((tm, FF_BLOCK), lambda i, k: (i, k)),
                   tok(D_MODEL), tok(D_MODEL), pl.BlockSpec((8, D_MODEL), lambda i, k: (0, 0))],
        out_shape=[jax.ShapeDtypeStruct((t, D_FF), MXU_DTYPE),
                   jax.ShapeDtypeStruct((t, D_MODEL), F32), jax.ShapeDtypeStruct((t, D_MODEL), MXU_DTYPE),
                   jax.ShapeDtypeStruct((8, D_MODEL), F32)],
        scratch_shapes=[pltpu.VMEM((tm, D_MODEL), F32)],
        compiler_params=_params("arbitrary", "arbitrary"),
    )(dh2, dh2b, r, h1, g_mlp, w_up4, w_down)


def _oproj_bwd(dh1b, wo_a, wo_d, tm, dep):
    t = dh1b.shape[0]
    half = wo_a.shape[0]

    def body(d_ref, wa_ref, wd_ref, dep_ref, da_ref, dd_ref):
        d = d_ref[...]
        da_ref[...] = _dot(d, wa_ref[...], NT)
        dd_ref[...] = _dot(d, wd_ref[...], NT)

    return pl.pallas_call(
        body, name="oproj_bwd", grid=(t // tm,),
        in_specs=[pl.BlockSpec((tm, D_MODEL), lambda i: (i, 0)), _full((half, D_MODEL)), _full((half, D_MODEL)),
                  pl.BlockSpec(memory_space=pl.ANY)],
        out_specs=[pl.BlockSpec((tm, half), lambda i: (i, 0)), pl.BlockSpec((tm, half), lambda i: (i, 0))],
        out_shape=[jax.ShapeDtypeStruct((t, half), F32), jax.ShapeDtypeStruct((t, half), F32)],
        compiler_params=_params("parallel"),
    )(dh1b, wo_a, wo_d, dep)


def _inproj_bwd(x, dh1, g_mix, grads, weights, tm):
    t = x.shape[0]
    n = len(grads)

    def body(*refs):
        x_ref, dh_ref, g_ref = refs[:3]
        g_refs, w_refs = refs[3:3 + n], refs[3 + n:3 + 2 * n]
        dx_ref, acc_ref = refs[3 + 2 * n:]

        @pl.when(pl.program_id(0) == 0)
        def _():
            acc_ref[...] = jnp.zeros_like(acc_ref)

        du = _dot(g_refs[0][...], w_refs[0][...], NT)
        for j in range(1, n):
            du += _dot(g_refs[j][...], w_refs[j][...], NT)
        dx, dg = _rms_bwd(x_ref[...], g_ref[...], du)
        dx_ref[...] = dh_ref[...] + dx
        acc_ref[0:1, :] += dg

    tok = lambda w: pl.BlockSpec((tm, w), lambda i: (i, 0))
    return pl.pallas_call(
        body, name="inproj_bwd", grid=(t // tm,),
        in_specs=[tok(D_MODEL), tok(D_MODEL), _full((1, D_MODEL))] + [tok(g.shape[1]) for g in grads]
                 + [_full(w.shape) for w in weights],
        out_specs=[tok(D_MODEL), _full((8, D_MODEL))],
        out_shape=[jax.ShapeDtypeStruct((t, D_MODEL), F32), jax.ShapeDtypeStruct((8, D_MODEL), F32)],
        compiler_params=_params("arbitrary"),
    )(x, dh1, g_mix, *grads, *weights)


def _wgrad(a, b, name, tk, tn, tt, stacked=False, prep=None):
    t, kdim = a.shape
    ncols = b.shape[1]

    def body(a_ref, b_ref, o_ref):
        @pl.when(pl.program_id(2) == 0)
        def _():
            o_ref[...] = jnp.zeros_like(o_ref)

        av = a_ref[...] if prep is None else prep(a_ref[...])
        o_ref[...] += _dot(av, b_ref[...], TN)

    if stacked:
        out_spec = pl.BlockSpec((None, tk, tn), lambda i, j, s: (j, i, 0))
        out_shape = jax.ShapeDtypeStruct((ncols // tn, kdim, tn), F32)
    else:
        out_spec = pl.BlockSpec((tk, tn), lambda i, j, s: (i, j))
        out_shape = jax.ShapeDtypeStruct((kdim, ncols), F32)
    return pl.pallas_call(
        body, name=name, grid=(kdim // tk, ncols // tn, t // tt),
        in_specs=[pl.BlockSpec((tt, tk), lambda i, j, s: (s, i)), pl.BlockSpec((tt, tn), lambda i, j, s: (s, j))],
        out_specs=out_spec, out_shape=out_shape,
        compiler_params=_params("parallel", "parallel", "arbitrary"),
    )(a, b)


def _wgrad_cat(as_, bs, name, tt):
    t = as_[0].shape[0]
    heights = [a.shape[1] for a in as_]
    widths = [b.shape[1] for b in bs]

    def body(*refs):
        a_refs, b_refs, o_ref = refs[:len(as_)], refs[len(as_):-1], refs[-1]

        @pl.when(pl.program_id(0) == 0)
        def _():
            o_ref[...] = jnp.zeros_like(o_ref)

        row = 0
        for a_ref, k in zip(a_refs, heights):
            av = a_ref[...]
            col = 0
            for b_ref, n in zip(b_refs, widths):
                o_ref[row:row + k, col:col + n] += _dot(av, b_ref[...], TN)
                col += n
            row += k

    tok = lambda w: pl.BlockSpec((tt, w), lambda s: (s, 0))
    shape = (sum(heights), sum(widths))
    return pl.pallas_call(
        body, name=name, grid=(t // tt,),
        in_specs=[tok(k) for k in heights] + [tok(n) for n in widths],
        out_specs=_full(shape), out_shape=jax.ShapeDtypeStruct(shape, F32),
        compiler_params=_params("arbitrary"),
    )(*as_, *bs)


def _rope_tables(t):
    half = ATTN_HEAD_DIM // 2
    inv = 1.0 / (ROPE_THETA ** (jnp.arange(half, dtype=F32) * (2.0 / ATTN_HEAD_DIM)))
    ang = jnp.arange(t, dtype=F32)[:, None] * inv[None, :]
    cos, sin = jnp.cos(ang), jnp.sin(ang)
    cos2 = jnp.concatenate([cos, cos], axis=-1)
    sin2 = jnp.concatenate([-sin, sin], axis=-1)
    return jnp.tile(cos2, (1, 2)), jnp.tile(sin2, (1, 2))


def _swap_halves(tv):
    w = tv.shape[-1]
    lane = lax.broadcasted_iota(jnp.int32, tv.shape, tv.ndim - 1)
    first = (lane % ATTN_HEAD_DIM) < (ATTN_HEAD_DIM // 2)
    return jnp.where(first, pltpu.roll(tv, w - ATTN_HEAD_DIM // 2, tv.ndim - 1),
                     pltpu.roll(tv, ATTN_HEAD_DIM // 2, tv.ndim - 1))


def _rope(tv, cos, sin):
    return tv * cos + _swap_halves(tv) * sin


def _rope_bwd(dv, cos, sin):
    return dv * cos + _swap_halves(dv * sin)


def _attn_valid(first_block):
    c = lax.broadcasted_iota(jnp.int32, (2 * ATTN_BLOCK, ATTN_BLOCK), 0)
    r = lax.broadcasted_iota(jnp.int32, (2 * ATTN_BLOCK, ATTN_BLOCK), 1)
    return (c > r) & (c <= r + ATTN_BLOCK) & ((c >= ATTN_BLOCK) | jnp.logical_not(first_block))


def _attn_probs(st, sink, valid):
    s = jnp.where(valid, st * ATTN_SCALE, -jnp.inf)
    m = jnp.maximum(jnp.max(s, axis=0, keepdims=True), sink)
    e = jnp.where(valid, jnp.exp(s - m), 0.0)
    es = jnp.exp(sink - m)
    inv = 1.0 / (jnp.sum(e, axis=0, keepdims=True) + es)
    return e * inv, es * inv


def _lane_scalar(vec, idx):
    lane = lax.broadcasted_iota(jnp.int32, vec.shape, 1)
    return jnp.sum(jnp.where(lane == idx, vec, 0.0), axis=-1, keepdims=True)


def _attn_specs(nb):
    cur = lambda w, cb: pl.BlockSpec((ATTN_BLOCK, w), lambda i: (jnp.minimum(i, nb - 1), cb))
    prev = lambda w, cb: pl.BlockSpec((ATTN_BLOCK, w), lambda i: (jnp.maximum(jnp.minimum(i, nb - 1) - 1, 0), cb))
    kcol, vcol = ATTN_Q // ATTN_KV, ATTN_Q // ATTN_KV + 1
    return [cur(ATTN_Q, 0), cur(ATTN_KV, kcol), prev(ATTN_KV, kcol), cur(ATTN_KV, vcol), prev(ATTN_KV, vcol),
            cur(ATTN_KV, 0), cur(ATTN_KV, 0), prev(ATTN_KV, 0), prev(ATTN_KV, 0), _full((1, 128))]


def _attn_fwd(pa, cos, sin, sinks_vec):
    t = pa.shape[0]
    nb = t // ATTN_BLOCK

    def body(q_ref, kc_ref, kp_ref, vc_ref, vp_ref, cc_ref, sc_ref, cp_ref, sp_ref, sk_ref, o_ref):
        first = pl.program_id(0) == 0
        cc, sc = cc_ref[...], sc_ref[...]
        q = _rope(q_ref[...], jnp.tile(cc, (1, ATTN_Q // ATTN_KV)), jnp.tile(sc, (1, ATTN_Q // ATTN_KV)))
        kc = _rope(kc_ref[...], cc, sc)
        kp = _rope(kp_ref[...], cp_ref[...], sp_ref[...])
        vc, vp = vc_ref[...], vp_ref[...]
        sk = sk_ref[...]
        valid = _attn_valid(first)
        kv = lambda tp, tc, hk: jnp.concatenate([tp[:, hk * ATTN_HEAD_DIM:(hk + 1) * ATTN_HEAD_DIM],
                                                 tc[:, hk * ATTN_HEAD_DIM:(hk + 1) * ATTN_HEAD_DIM]], axis=0)
        kwins = [kv(kp, kc, hk) for hk in range(ATTN_KV_HEADS)]
        vwins_t = [kv(vp, vc, hk).T for hk in range(ATTN_KV_HEADS)]
        heads = [slice(h * ATTN_HEAD_DIM, (h + 1) * ATTN_HEAD_DIM) for h in range(ATTN_HEADS)]
        scores = [_dot(kwins[h // ATTN_GROUPS], q[:, hs], NT) for h, hs in enumerate(heads)]
        probs = [_attn_probs(st, _lane_scalar(sk, h), valid)[0] for h, st in enumerate(scores)]
        for h, (hs, pt) in enumerate(zip(heads, probs)):
            o_ref[:, hs] = _dot(vwins_t[h // ATTN_GROUPS], pt).T.astype(o_ref.dtype)

    return pl.pallas_call(
        body, name="attn_fwd", grid=(nb,),
        in_specs=_attn_specs(nb),
        out_specs=pl.BlockSpec((ATTN_BLOCK, ATTN_Q), lambda i: (i, 0)),
        out_shape=jax.ShapeDtypeStruct((t, ATTN_Q), MXU_DTYPE),
        compiler_params=_params("parallel"),
    )(pa, pa, pa, pa, pa, cos, sin, cos, sin, sinks_vec)


def _attn_bwd(pa, cos, sin, sinks_vec, dao):
    t = pa.shape[0]
    nb = t // ATTN_BLOCK

    def body(q_ref, kc_ref, kp_ref, vc_ref, vp_ref, cc_ref, sc_ref, cp_ref, sp_ref, sk_ref, do_ref,
             dq_ref, dk_ref, dv_ref, acc_ref, dqr_ref, dkw_ref, dvw_ref, ck_ref, cv_ref):
        i = pl.program_id(0)

        @pl.when(i == 0)
        def _():
            acc_ref[...] = jnp.zeros_like(acc_ref)
            ck_ref[...] = jnp.zeros_like(ck_ref)
            cv_ref[...] = jnp.zeros_like(cv_ref)

        @pl.when(i < nb)
        def _():
            first = i == 0
            cc, sc = cc_ref[...], sc_ref[...]
            cq, sq = jnp.tile(cc, (1, ATTN_Q // ATTN_KV)), jnp.tile(sc, (1, ATTN_Q // ATTN_KV))
            q = _rope(q_ref[...], cq, sq)
            kc = _rope(kc_ref[...], cc, sc)
            kp = _rope(kp_ref[...], cp_ref[...], sp_ref[...])
            vc, vp = vc_ref[...], vp_ref[...]
            sk = sk_ref[...]
            do = do_ref[...]
            lane = lax.broadcasted_iota(jnp.int32, (1, 128), 1)
            dsink = jnp.zeros((1, 128), F32)
            valid = _attn_valid(first)
            kv = lambda tp, tc, hk: jnp.concatenate([tp[:, hk * ATTN_HEAD_DIM:(hk + 1) * ATTN_HEAD_DIM],
                                                     tc[:, hk * ATTN_HEAD_DIM:(hk + 1) * ATTN_HEAD_DIM]], axis=0)
            kwins = [kv(kp, kc, hk) for hk in range(ATTN_KV_HEADS)]
            vwins = [kv(vp, vc, hk) for hk in range(ATTN_KV_HEADS)]
            kwins_t = [kw.T for kw in kwins]
            heads = [slice(h * ATTN_HEAD_DIM, (h + 1) * ATTN_HEAD_DIM) for h in range(ATTN_HEADS)]
            scores = [_dot(kwins[h // ATTN_GROUPS], q[:, hs], NT) for h, hs in enumerate(heads)]
            dps = [_dot(vwins[h // ATTN_GROUPS], do[:, hs], NT) for h, hs in enumerate(heads)]
            pts, dsts = [], []
            for h, (st, dp_t) in enumerate(zip(scores, dps)):
                probs_t, psink = _attn_probs(st, _lane_scalar(sk, h), valid)
                delta = jnp.sum(probs_t * dp_t, axis=0, keepdims=True)
                pts.append(probs_t)
                dsts.append(probs_t * (dp_t - delta) * ATTN_SCALE)
                dsink += jnp.where(lane == h, jnp.sum(-psink * delta, axis=1, keepdims=True), 0.0)
            for h, (hs, ds_t) in enumerate(zip(heads, dsts)):
                dqr_ref[:, hs] = _dot(kwins_t[h // ATTN_GROUPS], ds_t).T
            for hk in range(ATTN_KV_HEADS):
                ks = slice(hk * ATTN_HEAD_DIM, (hk + 1) * ATTN_HEAD_DIM)
                group = range(hk * ATTN_GROUPS, (hk + 1) * ATTN_GROUPS)
                ds_g = jnp.concatenate([dsts[h] for h in group], axis=1)
                p_g = jnp.concatenate([pts[h] for h in group], axis=1)
                q_g = jnp.concatenate([q[:, heads[h]] for h in group], axis=0)
                do_g = jnp.concatenate([do[:, heads[h]] for h in group], axis=0)
                dkw_ref[:, ks] = _dot(ds_g, q_g)
                dvw_ref[:, ks] = _dot(p_g, do_g)
            acc_ref[0:1, :] += dsink
            dq_ref[...] = _rope_bwd(dqr_ref[...], cq, sq).astype(dq_ref.dtype)
            dk_ref[...] = (ck_ref[...] + _rope_bwd(dkw_ref[0:ATTN_BLOCK, :], cp_ref[...], sp_ref[...])).astype(dk_ref.dtype)
            dv_ref[...] = (cv_ref[...] + dvw_ref[0:ATTN_BLOCK, :]).astype(dv_ref.dtype)
            ck_ref[...] = _rope_bwd(dkw_ref[ATTN_BLOCK:2 * ATTN_BLOCK, :], cc, sc)
            cv_ref[...] = dvw_ref[ATTN_BLOCK:2 * ATTN_BLOCK, :]

        @pl.when(i == nb)
        def _():
            dk_ref[...] = ck_ref[...].astype(dk_ref.dtype)
            dv_ref[...] = cv_ref[...].astype(dv_ref.dtype)

    prev_out = lambda w: pl.BlockSpec((ATTN_BLOCK, w), lambda i: (jnp.maximum(i - 1, 0), 0))
    return pl.pallas_call(
        body, name="attn_bwd", grid=(nb + 1,),
        in_specs=_attn_specs(nb) + [pl.BlockSpec((ATTN_BLOCK, ATTN_Q), lambda i: (jnp.minimum(i, nb - 1), 0))],
        out_specs=[pl.BlockSpec((ATTN_BLOCK, ATTN_Q), lambda i: (jnp.minimum(i, nb - 1), 0)), prev_out(ATTN_KV),
                   prev_out(ATTN_KV), _full((8, 128))],
        out_shape=[jax.ShapeDtypeStruct((t, ATTN_Q), MXU_DTYPE), jax.ShapeDtypeStruct((t, ATTN_KV), MXU_DTYPE),
                   jax.ShapeDtypeStruct((t, ATTN_KV), MXU_DTYPE), jax.ShapeDtypeStruct((8, 128), F32)],
        scratch_shapes=[pltpu.VMEM((ATTN_BLOCK, ATTN_Q), F32), pltpu.VMEM((2 * ATTN_BLOCK, ATTN_KV), F32),
                        pltpu.VMEM((2 * ATTN_BLOCK, ATTN_KV), F32), pltpu.VMEM((ATTN_BLOCK, ATTN_KV), F32),
                        pltpu.VMEM((ATTN_BLOCK, ATTN_KV), F32)],
        compiler_params=_params("arbitrary"),
    )(pa, pa, pa, pa, pa, cos, sin, cos, sin, sinks_vec, dao)


PAIR = 2 * DN_CHUNK
INTRA_PAIRS = 4
HALO = 8


def _conv_window(cur_ref, prev_ref, xs_ref, tm, has_prev):
    prev = jnp.where(has_prev, prev_ref[...], 0.0)
    xs_ref[0:HALO, :] = prev
    xs_ref[HALO:HALO + tm, :] = cur_ref[...]


def _conv_taps(xs_ref, cw_ref, tm):
    y = cw_ref[0:1, :] * xs_ref[pl.ds(HALO - DN_CONV + 1, tm), :]
    for j in range(1, DN_CONV):
        y += cw_ref[j:j + 1, :] * xs_ref[pl.ds(HALO - DN_CONV + 1 + j, tm), :]
    return y


def _gate_values(ba, al, dt):
    beta = _sigmoid(ba)
    pre = ba + dt
    g = -jnp.exp(al) * _softplus(pre)
    return beta, g, pre


def _dn_prep_specs(tm, tile):
    return [pl.BlockSpec((tm, CONV_CH), lambda i: (tile(i), 0)),
            pl.BlockSpec((HALO, CONV_CH), lambda i: (jnp.maximum(tile(i) * (tm // HALO) - 1, 0), 0)),
            pl.BlockSpec((tm, 128), lambda i: (tile(i), 4 * DN_W // 128)),
            _full((DN_CONV, CONV_CH)), _full((1, 128)), _full((1, 128))]


def _dn_prep(pd, conv_w, al_vec, dt_vec, tm):
    t = pd.shape[0]

    def body(cur_ref, prev_ref, ba_ref, cw_ref, al_ref, dt_ref, qn_ref, kn_ref, vc_ref, gc_ref, gr_ref, xs_ref):
        _conv_window(cur_ref, prev_ref, xs_ref, tm, pl.program_id(0) > 0)
        y = _conv_taps(xs_ref, cw_ref, tm)
        c = y * _sigmoid(y)
        for h in range(DN_HEADS):
            qs = slice(h * DN_HEAD_DIM, (h + 1) * DN_HEAD_DIM)
            ksl = slice(DN_W + h * DN_HEAD_DIM, DN_W + (h + 1) * DN_HEAD_DIM)
            qh, kh = c[:, qs], c[:, ksl]
            qn_ref[:, qs] = qh * lax.rsqrt(jnp.sum(qh * qh, axis=-1, keepdims=True) + EPS) * DN_SCALE
            kn_ref[:, qs] = kh * lax.rsqrt(jnp.sum(kh * kh, axis=-1, keepdims=True) + EPS)
        vc_ref[...] = c[:, 2 * DN_W:3 * DN_W]
        beta, g, _ = _gate_values(ba_ref[...], al_ref[...], dt_ref[...])
        lane = lax.broadcasted_iota(jnp.int32, beta.shape, 1)
        gb = jnp.where(lane < DN_HEADS, beta, jnp.where(lane < 2 * DN_HEADS, g, 0.0))
        gc_ref[...] = gb
        gr_ref[...] = gb.T[0:8, :]

    tok = lambda w: pl.BlockSpec((tm, w), lambda i: (i, 0))
    return pl.pallas_call(
        body, name="dn_prep", grid=(t // tm,),
        in_specs=_dn_prep_specs(tm, lambda i: i),
        out_specs=[tok(DN_W), tok(DN_W), tok(DN_W), tok(128), pl.BlockSpec((8, tm), lambda i: (0, i))],
        out_shape=[jax.ShapeDtypeStruct((t, DN_W), F32)] * 3 + [jax.ShapeDtypeStruct((t, 128), F32),
                                                                 jax.ShapeDtypeStruct((8, t), F32)],
        scratch_shapes=[pltpu.VMEM((HALO + tm, CONV_CH), F32)],
        compiler_params=_params("parallel"),
    )(pd, pd, pd, conv_w, al_vec, dt_vec)


def _pair_masks():
    r = lax.broadcasted_iota(jnp.int32, (PAIR, PAIR), 0)
    c = lax.broadcasted_iota(jnp.int32, (PAIR, PAIR), 1)
    same = (r < DN_CHUNK) == (c < DN_CHUNK)
    return same & (r >= c), same & (r > c)


def _lane_col(mat, idx):
    lane = lax.broadcasted_iota(jnp.int32, mat.shape, 1)
    return jnp.sum(jnp.where(lane == idx, mat, 0.0), axis=-1, keepdims=True)


def _pair_cumsums(gc, gr, low):
    lowf = low.astype(F32)
    return _dot(lowf, gc, NN, HI), _dot(gr, lowf, NT, HI)


def _pair_gates(gc, cum_c, cum_r, low, h):
    beta = _lane_col(gc, h)
    gam = _lane_col(cum_c, DN_HEADS + h)
    gam_row = cum_r[DN_HEADS + h:DN_HEADS + h + 1, :]
    dm = jnp.where(low, jnp.exp(jnp.where(low, gam - gam_row, 0.0)), 0.0)
    row = lax.broadcasted_iota(jnp.int32, gam.shape, 0)
    gl = jnp.where(row < DN_CHUNK, gam[DN_CHUNK - 1:DN_CHUNK, :], gam[PAIR - 1:PAIR, :])
    return beta, gam, dm, gl


def _split(a):
    hi = a.astype(BF16)
    return hi, (a - hi.astype(F32)).astype(BF16)


def _dot_split(a, b, dims=NN):
    (ah, al), (bh, bl) = a, b
    la, lb = (1, 1) if dims == TN else ((0, 1) if dims == NN else (0, 0))
    r = _dot(jnp.concatenate([ah, al], axis=la), jnp.concatenate([bh, bl], axis=lb), dims)
    m, n = r.shape[0] // 2, r.shape[1] // 2
    return (r[m:, n:] + (r[:m, n:] + r[m:, :n])) + r[:m, :n]


def _unit_lower_inverses(lmats):
    n = lmats[0].shape[0]
    r = lax.broadcasted_iota(jnp.int32, (n, n), 0)
    c = lax.broadcasted_iota(jnp.int32, (n, n), 1)
    same = lambda size: (r & ~(size - 1)) == (c & ~(size - 1))
    base = DN_CHUNK // 4
    diag = [jnp.where(same(base), l, 0.0) for l in lmats]
    accs = [(r == c).astype(F32) - d for d in diag]
    splits = [_split(d) for d in diag]
    step = 1
    while 2 * step < base:
        splits = [_split(_dot_split(s, s)) for s in splits]
        accs = [acc + _dot_split(_split(acc), s) for acc, s in zip(accs, splits)]
        step *= 2
    size = base
    while size < DN_CHUNK:
        below = same(2 * size) & jnp.logical_not(same(size))
        tb = [_dot(acc, jnp.where(below, l, 0.0)) for acc, l in zip(accs, lmats)]
        accs = [acc - _dot(t, acc) for acc, t in zip(accs, tb)]
        size *= 2
    return accs


def _dn_intra(qn, kn, vc, gc, gr):
    t = qn.shape[0]
    npair = t // PAIR
    rows_step = INTRA_PAIRS * PAIR

    def body(q_ref, k_ref, v_ref, gc_ref, gr_ref, u_ref, w_ref, qg_ref, kd_ref, a_ref, ti_ref, dl_ref):
        low, strict = _pair_masks()
        items = []
        for p in range(INTRA_PAIRS):
            rows = slice(p * PAIR, (p + 1) * PAIR)
            gc_v = gc_ref[rows, :]
            cum_c, cum_r = _pair_cumsums(gc_v, gr_ref[:, rows], low)
            for h in range(DN_HEADS):
                hs = slice(h * DN_HEAD_DIM, (h + 1) * DN_HEAD_DIM)
                items.append((p, h, rows, hs, _pair_gates(gc_v, cum_c, cum_r, low, h)))
        lmats = []
        for p, h, rows, hs, (beta, gam, dm, gl) in items:
            k = k_ref[rows, hs]
            lmats.append(jnp.where(strict, _dot(k * beta, k, NT) * dm, 0.0))
        tinvs = _unit_lower_inverses(lmats)
        for (p, h, rows, hs, (beta, gam, dm, gl)), tinv in zip(items, tinvs):
            q, k, v = q_ref[rows, hs], k_ref[rows, hs], v_ref[rows, hs]
            eg = jnp.exp(gam)
            u_ref[rows, hs] = _dot(tinv, v * beta)
            w_ref[rows, hs] = _dot(tinv, (k * beta) * eg)
            a_ref[h, rows, :] = _dot(q, k, NT) * dm
            ti_ref[h, rows, :] = tinv
            qg_ref[rows, hs] = q * eg
            kd_ref[rows, hs] = k * jnp.exp(gl - gam)
            for c in range(2):
                last = (c + 1) * DN_CHUNK - 1
                dl_ref[2 * p + c, h] = jnp.broadcast_to(jnp.exp(gam[last:last + 1, :]), (8, 128))

    tok = lambda w: pl.BlockSpec((rows_step, w), lambda n: (n, 0))
    hm = pl.BlockSpec((DN_HEADS, rows_step, PAIR), lambda n: (0, n, 0))
    return pl.pallas_call(
        body, name="dn_intra", grid=(npair // INTRA_PAIRS,),
        in_specs=[tok(DN_W), tok(DN_W), tok(DN_W), tok(128), pl.BlockSpec((8, rows_step), lambda n: (0, n))],
        out_specs=[tok(DN_W)] * 4 + [hm, hm, pl.BlockSpec((2 * INTRA_PAIRS, DN_HEADS, 8, 128), lambda n: (n, 0, 0, 0))],
        out_shape=[jax.ShapeDtypeStruct((t, DN_W), F32)] * 4 + [jax.ShapeDtypeStruct((DN_HEADS, t, PAIR), F32)] * 2
                  + [jax.ShapeDtypeStruct((2 * npair, DN_HEADS, 8, 128), F32)],
        compiler_params=_params("parallel"),
    )(qn, kn, vc, gc, gr)


def _dn_scan_fwd(u, w, qg, kd, a_qk, dlast, pd, dn_w):
    t = u.shape[0]
    npair = t // PAIR

    def body(u_ref, w_ref, qg_ref, kd_ref, a_ref, dl_ref, z_ref, nw_ref, out_ref, o_ref, vn_ref, sall_ref, s_ref):
        @pl.when(pl.program_id(0) == 0)
        def _():
            s_ref[...] = jnp.zeros_like(s_ref)

        nw = nw_ref[...]
        for c in range(2):
            rows = slice(c * DN_CHUNK, (c + 1) * DN_CHUNK)
            for h in range(DN_HEADS):
                hs = slice(h * DN_HEAD_DIM, (h + 1) * DN_HEAD_DIM)
                st = s_ref[h]
                sall_ref[c, h] = st
                vn_ref[rows, hs] = u_ref[rows, hs] - _dot(w_ref[rows, hs], st)
            for h in range(DN_HEADS):
                hs = slice(h * DN_HEAD_DIM, (h + 1) * DN_HEAD_DIM)
                st, vn = s_ref[h], vn_ref[rows, hs]
                o = _dot(qg_ref[rows, hs], st) + _dot(a_ref[h, rows, rows], vn)
                s_ref[h] = st * dl_ref[c, h][0:1, :] + _dot(kd_ref[rows, hs], vn, TN)
                o_ref[rows, hs] = o
                z = z_ref[rows, hs]
                on = o * lax.rsqrt(jnp.mean(o * o, axis=-1, keepdims=True) + EPS) * nw
                out_ref[rows, hs] = (on * (z * _sigmoid(z))).astype(out_ref.dtype)

    tok = pl.BlockSpec((PAIR, DN_W), lambda n: (n, 0))
    hm = pl.BlockSpec((DN_HEADS, PAIR, PAIR), lambda n: (0, n, 0))
    return pl.pallas_call(
        body, name="dn_scan_fwd", grid=(npair,),
        in_specs=[tok, tok, tok, tok, hm, pl.BlockSpec((2, DN_HEADS, 8, 128), lambda n: (n, 0, 0, 0)),
                  pl.BlockSpec((PAIR, DN_W), lambda n: (n, 3)), _full((1, 128))],
        out_specs=[tok, tok, tok, pl.BlockSpec((2, DN_HEADS, DN_HEAD_DIM, DN_HEAD_DIM), lambda n: (n, 0, 0, 0))],
        out_shape=[jax.ShapeDtypeStruct((t, DN_W), MXU_DTYPE)] + [jax.ShapeDtypeStruct((t, DN_W), F32)] * 2
                  + [jax.ShapeDtypeStruct((2 * npair, DN_HEADS, DN_HEAD_DIM, DN_HEAD_DIM), F32)],
        scratch_shapes=[pltpu.VMEM((DN_HEADS, DN_HEAD_DIM, DN_HEAD_DIM), F32)],
        compiler_params=_params("arbitrary"),
    )(u, w, qg, kd, a_qk, dlast, pd, dn_w)


def _dn_scan_bwd(dout, o, vnew, sall, w, qg, kd, a_qk, dlast, pd, dn_w):
    t = o.shape[0]
    npair = t // PAIR
    rev = lambda n: npair - 1 - n

    def body(do_ref, o_ref, vn_ref, sall_ref, w_ref, qg_ref, kd_ref, a_ref, dl_ref, z_ref, nw_ref,
             dz_ref, du_ref, dw_ref, dqg_ref, dkd_ref, da_ref, ddl_ref, acc_ref, ds_ref, dos_ref):
        @pl.when(pl.program_id(0) == 0)
        def _():
            ds_ref[...] = jnp.zeros_like(ds_ref)
            acc_ref[...] = jnp.zeros_like(acc_ref)

        nw = nw_ref[...]
        dnw = jnp.zeros((1, 128), F32)
        for h in range(DN_HEADS):
            hs = slice(h * DN_HEAD_DIM, (h + 1) * DN_HEAD_DIM)
            o, z, dout = o_ref[:, hs], z_ref[:, hs], do_ref[:, hs]
            r = lax.rsqrt(jnp.mean(o * o, axis=-1, keepdims=True) + EPS)
            oh = o * r
            sz = _sigmoid(z)
            dz_ref[:, hs] = dout * (oh * nw) * (sz + z * sz * (1.0 - sz))
            don = dout * (z * sz)
            dnw += jnp.sum(don * oh, axis=0, keepdims=True)
            doh = don * nw
            dos_ref[:, hs] = r * (doh - oh * jnp.mean(doh * oh, axis=-1, keepdims=True))
        acc_ref[0:1, :] += dnw
        for c in (1, 0):
            rows = slice(c * DN_CHUNK, (c + 1) * DN_CHUNK)
            other = slice((1 - c) * DN_CHUNK, (2 - c) * DN_CHUNK)
            for h in range(DN_HEADS):
                hs = slice(h * DN_HEAD_DIM, (h + 1) * DN_HEAD_DIM)
                do, st, dsp, vn = dos_ref[rows, hs], sall_ref[c, h], ds_ref[h], vn_ref[rows, hs]
                da_ref[h, rows, rows] = _dot(do, vn, NT)
                da_ref[h, rows, other] = jnp.zeros((DN_CHUNK, DN_CHUNK), F32)
                du_ref[rows, hs] = _dot(a_ref[h, rows, rows], do, TN) + _dot(kd_ref[rows, hs], dsp)
                dqg_ref[rows, hs] = _dot(do, st, NT)
                dkd_ref[rows, hs] = _dot(vn, dsp, NT)
                ddl = jnp.sum(jnp.sum(dsp * st, axis=1, keepdims=True), axis=0, keepdims=True)
                ddl_ref[c, h] = jnp.broadcast_to(ddl, (8, 128))
            for h in range(DN_HEADS):
                hs = slice(h * DN_HEAD_DIM, (h + 1) * DN_HEAD_DIM)
                do, st, dvn = dos_ref[rows, hs], sall_ref[c, h], du_ref[rows, hs]
                dw_ref[rows, hs] = -_dot(dvn, st, NT)
                ds_ref[h] = (ds_ref[h] * dl_ref[c, h][0:1, :] + _dot(qg_ref[rows, hs], do, TN)
                             - _dot(w_ref[rows, hs], dvn, TN))

    tok = pl.BlockSpec((PAIR, DN_W), lambda n: (rev(n), 0))
    hm = pl.BlockSpec((DN_HEADS, PAIR, PAIR), lambda n: (0, rev(n), 0))
    sc = pl.BlockSpec((2, DN_HEADS, 8, 128), lambda n: (rev(n), 0, 0, 0))
    return pl.pallas_call(
        body, name="dn_scan_bwd", grid=(npair,),
        in_specs=[tok, tok, tok, pl.BlockSpec((2, DN_HEADS, DN_HEAD_DIM, DN_HEAD_DIM), lambda n: (rev(n), 0, 0, 0)),
                  tok, tok, tok, hm, sc, pl.BlockSpec((PAIR, DN_W), lambda n: (rev(n), 3)), _full((1, 128))],
        out_specs=[tok] * 5 + [hm, sc, _full((8, 128))],
        out_shape=[jax.ShapeDtypeStruct((t, DN_W), F32)] * 5 + [jax.ShapeDtypeStruct((DN_HEADS, t, PAIR), F32),
                   jax.ShapeDtypeStruct((2 * npair, DN_HEADS, 8, 128), F32), jax.ShapeDtypeStruct((8, 128), F32)],
        scratch_shapes=[pltpu.VMEM((DN_HEADS, DN_HEAD_DIM, DN_HEAD_DIM), F32), pltpu.VMEM((PAIR, DN_W), F32)],
        compiler_params=_params("arbitrary"),
    )(dout, o, vnew, sall, w, qg, kd, a_qk, dlast, pd, dn_w)


def _dn_intra_bwd(qn, kn, vc, gc, gr, tinv, a_qk, du, dw, dqg, dkd, da_qk, ddlast, dlast, dep):
    t = qn.shape[0]
    npair = t // PAIR

    def body(q_ref, k_ref, v_ref, gc_ref, gr_ref, ti_ref, a_ref, du_ref, dw_ref, dqg_ref, dkd_ref, da_ref, ddl_ref, dl_ref,
             dep_ref, dq_ref, dk_ref, dv_ref, dg_ref):
        low, strict = _pair_masks()
        lane = lax.broadcasted_iota(jnp.int32, (PAIR, 128), 1)
        rowi = lax.broadcasted_iota(jnp.int32, (PAIR, 1), 0)
        rsum = lambda v: jnp.sum(v, axis=-1, keepdims=True)
        items = []
        for p in range(INTRA_PAIRS):
            rows = slice(p * PAIR, (p + 1) * PAIR)
            gc_v = gc_ref[rows, :]
            cum_c, cum_r = _pair_cumsums(gc_v, gr_ref[:, rows], low)
            for h in range(DN_HEADS):
                hs = slice(h * DN_HEAD_DIM, (h + 1) * DN_HEAD_DIM)
                items.append((p, h, rows, hs, _pair_gates(gc_v, cum_c, cum_r, low, h)))
        dtis, lmats, dvbs, dkbgs = [], [], [], []
        for p, h, rows, hs, (beta, gam, dm, gl) in items:
            k, tinv = k_ref[rows, hs], ti_ref[h, rows, :]
            kb = k * beta
            dtis.append(_dot(du_ref[rows, hs], v_ref[rows, hs] * beta, NT)
                        + _dot(dw_ref[rows, hs], kb * jnp.exp(gam), NT))
            lmats.append(jnp.where(strict, _dot(kb, k, NT) * dm, 0.0))
            dvbs.append(_dot(tinv, du_ref[rows, hs], TN))
            dkbgs.append(_dot(tinv, dw_ref[rows, hs], TN))
        xs = [_dot(ti_ref[h, rows, :], dti, TN) for (p, h, rows, hs, g), dti in zip(items, dtis)]
        dls = [jnp.where(strict, -_dot(x, ti_ref[h, rows, :], NT), 0.0) for (p, h, rows, hs, g), x in zip(items, xs)]
        dgam_all = [jnp.zeros((PAIR, 128), F32) for _ in range(INTRA_PAIRS)]
        dbeta_all = [jnp.zeros((PAIR, 128), F32) for _ in range(INTRA_PAIRS)]
        for (p, h, rows, hs, (beta, gam, dm, gl)), dl, lmat, dvb, dkbg in zip(items, dls, lmats, dvbs, dkbgs):
            q, k, v = q_ref[rows, hs], k_ref[rows, hs], v_ref[rows, hs]
            a = a_ref[h, rows, :]
            dqg, dkd = dqg_ref[rows, hs], dkd_ref[rows, hs]
            kb = k * beta
            eg = jnp.exp(gam)
            ekd = jnp.exp(gl - gam)
            dmm = dl * dm
            dam = jnp.where(low, da_ref[h, rows, :], 0.0)
            dn = dam * dm
            e = dl * lmat + dam * a
            dkb = _dot(dmm, k) + dkbg * eg
            dk_ref[rows, hs] = _dot(dmm, kb, TN) + _dot(dn, q, TN) + dkd * ekd + dkb * beta
            dq_ref[rows, hs] = _dot(dn, k) + dqg * eg
            dv_ref[rows, hs] = dvb * beta
            t_kd = rsum(dkd * (k * ekd))
            dgam = rsum(e) - rsum(e.T) + rsum(dqg * (q * eg)) + rsum(dkbg * (kb * eg)) - t_kd
            for c in range(2):
                crows = slice(c * DN_CHUNK, (c + 1) * DN_CHUNK)
                dgl = (jnp.sum(t_kd[crows, :], axis=0, keepdims=True)
                       + ddl_ref[2 * p + c, h][0:1, 0:1] * dl_ref[2 * p + c, h][0:1, 0:1])
                dgam = dgam + jnp.where(rowi == (c + 1) * DN_CHUNK - 1, dgl, 0.0)
            dgam_all[p] += jnp.where(lane == DN_HEADS + h, dgam, 0.0)
            dbeta_all[p] += jnp.where(lane == h, rsum(dkb * k) + rsum(dvb * v), 0.0)
        for p in range(INTRA_PAIRS):
            dg_ref[p * PAIR:(p + 1) * PAIR, :] = dbeta_all[p] + _dot(low.astype(F32), dgam_all[p], TN, HI)

    rows_step = INTRA_PAIRS * PAIR
    tok = lambda w: pl.BlockSpec((rows_step, w), lambda n: (n, 0))
    hm = pl.BlockSpec((DN_HEADS, rows_step, PAIR), lambda n: (0, n, 0))
    sc = pl.BlockSpec((2 * INTRA_PAIRS, DN_HEADS, 8, 128), lambda n: (n, 0, 0, 0))
    return pl.pallas_call(
        body, name="dn_intra_bwd", grid=(npair // INTRA_PAIRS,),
        in_specs=[tok(DN_W), tok(DN_W), tok(DN_W), tok(128), pl.BlockSpec((8, rows_step), lambda n: (0, n)), hm, hm,
                  tok(DN_W), tok(DN_W), tok(DN_W), tok(DN_W), hm, sc, sc, pl.BlockSpec(memory_space=pl.ANY)],
        out_specs=[tok(DN_W), tok(DN_W), tok(DN_W), tok(128)],
        out_shape=[jax.ShapeDtypeStruct((t, DN_W), F32)] * 3 + [jax.ShapeDtypeStruct((t, 128), F32)],
        compiler_params=_params("parallel"),
    )(qn, kn, vc, gc, gr, tinv, a_qk, du, dw, dqg, dkd, da_qk, ddlast, dlast, dep)


def _dn_prep_bwd(pd, conv_w, al_vec, dt_vec, dqn, dkn, dvc, dgc, dz, tm):
    t = pd.shape[0]
    nt = t // tm
    tile = lambda i: nt - 1 - i

    def body(cur_ref, prev_ref, ba_ref, cw_ref, al_ref, dt_ref, dq_ref, dk_ref, dv_ref, dg_ref, dz_ref,
             o_ref, accw_ref, accg_ref, xs_ref, dc_ref, ds_ref, carry_ref):
        @pl.when(pl.program_id(0) == 0)
        def _():
            accw_ref[...] = jnp.zeros_like(accw_ref)
            accg_ref[...] = jnp.zeros_like(accg_ref)
            carry_ref[...] = jnp.zeros_like(carry_ref)

        _conv_window(cur_ref, prev_ref, xs_ref, tm, tile(pl.program_id(0)) > 0)
        y = _conv_taps(xs_ref, cw_ref, tm)
        sg = _sigmoid(y)
        c = y * sg
        for h in range(DN_HEADS):
            qs = slice(h * DN_HEAD_DIM, (h + 1) * DN_HEAD_DIM)
            ksl = slice(DN_W + h * DN_HEAD_DIM, DN_W + (h + 1) * DN_HEAD_DIM)
            for src, sl, scale in ((dq_ref, qs, DN_SCALE), (dk_ref, ksl, 1.0)):
                xh = c[:, sl]
                r = lax.rsqrt(jnp.sum(xh * xh, axis=-1, keepdims=True) + EPS)
                unit = xh * r
                dn = src[:, qs] * scale
                dc_ref[:, sl] = r * (dn - unit * jnp.sum(dn * unit, axis=-1, keepdims=True))
        dc_ref[:, 2 * DN_W:3 * DN_W] = dv_ref[...]
        dy = dc_ref[...] * (sg + y * sg * (1.0 - sg))
        for j in range(DN_CONV):
            accw_ref[j:j + 1, :] += jnp.sum(dy * xs_ref[pl.ds(HALO - DN_CONV + 1 + j, tm), :], axis=0, keepdims=True)
        ds_ref[0:tm, :] = dy
        ds_ref[tm:tm + HALO, :] = carry_ref[...]
        carry_ref[...] = ds_ref[0:HALO, :]
        dx = cw_ref[0:1, :] * ds_ref[pl.ds(DN_CONV - 1, tm), :]
        for j in range(1, DN_CONV):
            dx += cw_ref[j:j + 1, :] * ds_ref[pl.ds(DN_CONV - 1 - j, tm), :]

        beta, g, pre = _gate_values(ba_ref[...], al_ref[...], dt_ref[...])
        dgb = dg_ref[...]
        lane = lax.broadcasted_iota(jnp.int32, dgb.shape, 1)
        is_b, is_a = lane < DN_HEADS, (lane >= DN_HEADS) & (lane < 2 * DN_HEADS)
        dpre = dgb * (-jnp.exp(al_ref[...])) * _sigmoid(pre)
        dba = jnp.where(is_b, dgb * beta * (1.0 - beta), jnp.where(is_a, dpre, 0.0))
        accg_ref[0:1, :] += jnp.sum(jnp.where(is_a, dgb * g, 0.0), axis=0, keepdims=True)
        accg_ref[1:2, :] += jnp.sum(jnp.where(is_a, dpre, 0.0), axis=0, keepdims=True)
        o_ref[:, 0:CONV_CH] = dx.astype(o_ref.dtype)
        o_ref[:, CONV_CH:CONV_CH + DN_W] = dz_ref[...].astype(o_ref.dtype)
        o_ref[:, CONV_CH + DN_W:DN_COLS] = dba.astype(o_ref.dtype)

    tok = lambda w: pl.BlockSpec((tm, w), lambda i: (tile(i), 0))
    return pl.pallas_call(
        body, name="dn_prep_bwd", grid=(nt,),
        in_specs=_dn_prep_specs(tm, tile) + [tok(DN_W), tok(DN_W), tok(DN_W), tok(128), tok(DN_W)],
        out_specs=[tok(DN_COLS), _full((8, CONV_CH)), _full((8, 128))],
        out_shape=[jax.ShapeDtypeStruct((t, DN_COLS), MXU_DTYPE),
                   jax.ShapeDtypeStruct((8, CONV_CH), F32), jax.ShapeDtypeStruct((8, 128), F32)],
        scratch_shapes=[pltpu.VMEM((HALO + tm, CONV_CH), F32), pltpu.VMEM((tm, CONV_CH), F32),
                        pltpu.VMEM((tm + HALO, CONV_CH), F32), pltpu.VMEM((HALO, CONV_CH), F32)],
        compiler_params=_params("arbitrary"),
    )(pd, pd, pd, conv_w, al_vec, dt_vec, dqn, dkn, dvc, dgc, dz)


def _pad_lanes(v, offset=0):
    return jnp.zeros((1, 128), F32).at[0, offset:offset + v.shape[0]].set(v.astype(F32))


class _LocalReducer:
    def start(self, grads):
        return jnp.zeros((8, 128), F32)

    def middle(self, after):
        return jnp.zeros((8, 128), F32)

    def finish(self, after):
        return None


def _local_step(x, p, tgt, sm, w, late, reducer):
    t = x.shape[0]
    tm = min(512, t // 2)
    tm_s = min(512, t // 2)
    tw = min(1024, t // 2)

    w_in = w["w_in"]
    wa = w_in[:, :ATTN_Q + 2 * ATTN_KV]
    wd = jnp.pad(w_in[:, ATTN_Q + 2 * ATTN_KV:], ((0, 0), (0, DN_COLS - (D_IN - ATTN_Q - 2 * ATTN_KV))))
    conv_w = w["conv_w"]
    al_vec, dt_vec = _pad_lanes(sm["a_log"], DN_HEADS), _pad_lanes(sm["dt_bias"], DN_HEADS)
    sinks_vec = _pad_lanes(sm["sinks"])
    dn_w = sm["dn_norm"].reshape(1, 128)
    row = lambda v: v.reshape(1, D_MODEL)
    cos, sin = _rope_tables(t)

    u, pa, pd = _inproj(x, row(sm["norm_mix"]), wa, wd, tm_s)
    ao = _attn_fwd(pa, cos, sin, sinks_vec)
    qn, kn, vc, gc, gr = _dn_prep(pd, conv_w, al_vec, dt_vec, tm_s)
    uu, ww, qg, kd, a_qk, tinv, dlast = _dn_intra(qn, kn, vc, gc, gr)
    dn_out, o, vnew, sall = _dn_scan_fwd(uu, ww, qg, kd, a_qk, dlast, pd, dn_w)
    w_o, late_rest = late(dn_out)
    wo_a, wo_d = w_o[:ATTN_Q], w_o[ATTN_Q:]
    h1 = _oproj(x, ao, dn_out, wo_a, wo_d, tm)
    w = dict(w, **late_rest(h1))
    w_proj = jnp.transpose(w["w_proj4"], (1, 0, 2)).reshape(PLE_DIM, D_MODEL)
    m, r, h2 = _mlp_fwd(h1, row(sm["norm_mlp"]), w["w_up4"], w["w_down"], tw)
    dh2, dh2b, dgp, dpp, n3, pb, acc_ple = _ple_loss(h2, p, tgt, row(sm["norm_ple"]), row(sm["norm_final"]),
                                                     w["w_gate"], w_proj, tm_s)
    g_w_gate = _wgrad(n3, dgp, "wgrad_gate", D_MODEL, D_MODEL, tw)
    g_w_proj = _wgrad(pb, dpp, "wgrad_proj", PLE_DIM, D_MODEL, tw)
    da, dh1, dh1b, acc_mlp = _mlp_bwd(dh2, dh2b, r, h1, row(sm["norm_mlp"]), w["w_up4"], w["w_down"], tm)
    g_w_up4 = _wgrad(m, da, "wgrad_up", D_MODEL, FF_BLOCK, tw, stacked=True)
    g_w_down = _wgrad(r, dh2b, "wgrad_down", FF_BLOCK, D_MODEL, tw,
                      prep=lambda rv: jnp.square(rv.astype(F32)).astype(MXU_DTYPE))
    g_w_o = _wgrad_cat([ao, dn_out], [dh1b], "wgrad_o", tw)
    early = dict(w_up4=g_w_up4, w_down=g_w_down, w_gate=g_w_gate, w_proj=g_w_proj, w_o=g_w_o)
    dep = reducer.start(early)
    dao, ddn = _oproj_bwd(dh1b, wo_a, wo_d, tm, dep)
    dz, du, dw, dqg, dkd, da_qk, ddlast, acc_dn = _dn_scan_bwd(ddn, o, vnew, sall, ww, qg, kd, a_qk, dlast, pd, dn_w)
    dep = reducer.middle(du)
    dqn, dkn, dvc, dgc = _dn_intra_bwd(qn, kn, vc, gc, gr, tinv, a_qk, du, dw, dqg, dkd, da_qk, ddlast, dlast, dep)
    d_dn, acc_conv, acc_gate = _dn_prep_bwd(pd, conv_w, al_vec, dt_vec, dqn, dkn, dvc, dgc, dz, tm_s)
    dq, dk, dv, acc_attn = _attn_bwd(pa, cos, sin, sinks_vec, dao)
    reducer.finish(dq)
    wq, wk, wv = wa[:, :ATTN_Q], wa[:, ATTN_Q:ATTN_Q + ATTN_KV], wa[:, ATTN_Q + ATTN_KV:]
    dx, acc_mix = _inproj_bwd(x, dh1, row(sm["norm_mix"]), [dq, dk, dv, d_dn], [wq, wk, wv, wd], tm_s)

    g_w_in_t = _wgrad_cat([dq, dk, dv, d_dn], [u], "wgrad_in", tw)[:D_IN]
    grads = dict(early, w_in_t=g_w_in_t)
    sums = dict(loss=acc_ple[2, 0], norm_final=acc_ple[0], norm_ple=acc_ple[1], norm_mlp=acc_mlp[0], norm_mix=acc_mix[0],
                dn_norm=acc_dn[0], sinks=acc_attn[0, :ATTN_HEADS], a_log=acc_gate[0, DN_HEADS:2 * DN_HEADS],
                dt_bias=acc_gate[1, DN_HEADS:2 * DN_HEADS], conv_w=acc_conv[:DN_CONV])
    return sums, dx, grads


MESH = pl.DeviceIdType.MESH
ANY = pl.BlockSpec(memory_space=pl.ANY)
N_CHIPS = 4
N_DEV = 8


def _place():
    x, y, c = lax.axis_index("x"), lax.axis_index("y"), lax.axis_index("c")
    chips = [(1 - x, y), (x, 1 - y), (1 - x, 1 - y)]
    return x, y, c, chips


def _gather_weights(shards, conv_s):
    n = len(shards)
    per = 7

    def body(*refs):
        in_refs, conv_ref = refs[:n], refs[n]
        out_refs, conv_out = refs[n + 1:2 * n + 1], refs[2 * n + 1]
        send_sems, recv_sems = refs[2 * n + 2:]
        x, y, c, chips = _place()
        sibling = (x, y, 1 - c)

        def blk(a, px, py, pc):
            hr = in_refs[a].shape[0] // 2
            return out_refs[a].at[2 * px + py, pl.ds(pc * hr, hr), :]

        def mine(a):
            hr = in_refs[a].shape[0] // 2
            return in_refs[a].at[pl.ds(c * hr, hr), :]

        def rcopy(a, k, block, to, src=None):
            return pltpu.make_async_remote_copy(
                src_ref=blk(a, *block) if src is None else src, dst_ref=blk(a, *block),
                send_sem=send_sems.at[per * a + k], recv_sem=recv_sems.at[per * a + k],
                device_id=to, device_id_type=MESH)

        def whole(a, to):
            return pltpu.make_async_remote_copy(
                src_ref=in_refs[a], dst_ref=out_refs[a].at[2 * x + y],
                send_sem=send_sems.at[per * a], recv_sem=recv_sems.at[per * a], device_id=to, device_id_type=MESH)

        def ccopy(j, to):
            return pltpu.make_async_remote_copy(
                src_ref=conv_ref, dst_ref=conv_out.at[2 * x + y],
                send_sem=send_sems.at[per * n + j], recv_sem=recv_sems.at[per * n + j],
                device_id=to, device_id_type=MESH)

        started = []
        for a in range(n):
            first = [whole(a, sibling)]
            first += [rcopy(a, 1 + j, (x, y, c), (*chip, c), src=mine(a)) for j, chip in enumerate(chips)]
            for cp in first:
                cp.start()
            started += first
        conv_sends = [ccopy(j, (*chip, c)) for j, chip in enumerate(chips)] + [ccopy(3, sibling)]
        for cp in conv_sends:
            cp.start()
        started += conv_sends
        for a in range(n):
            for j, chip in enumerate(chips):
                rcopy(a, 1 + j, (*chip, c), (x, y, c)).wait_recv()
                fwd = rcopy(a, 4 + j, (*chip, c), sibling)
                fwd.start()
                started.append(fwd)
        for a in range(n):
            whole(a, sibling).wait_recv()
            for j, chip in enumerate(chips):
                rcopy(a, 4 + j, (*chip, 1 - c), (x, y, c)).wait_recv()
        for j, chip in enumerate(chips + [(x, y)]):
            pltpu.make_async_remote_copy(
                src_ref=conv_ref, dst_ref=conv_out.at[2 * chip[0] + chip[1]],
                send_sem=send_sems.at[per * n + j], recv_sem=recv_sems.at[per * n + j],
                device_id=sibling, device_id_type=MESH).wait_recv()
        for cp in started:
            cp.wait_send()

    nsem = per * n + 4
    out_shape = [jax.ShapeDtypeStruct((N_CHIPS,) + s.shape, s.dtype) for s in shards]
    out_shape.append(jax.ShapeDtypeStruct((N_CHIPS,) + conv_s.shape, conv_s.dtype))
    return pl.pallas_call(
        body, name="gather_weights", in_specs=[ANY] * (n + 1), out_specs=[ANY] * (n + 1), out_shape=out_shape,
        scratch_shapes=[pltpu.SemaphoreType.DMA((nsem,)), pltpu.SemaphoreType.DMA((nsem,))],
    )(*shards, conv_s)


HBM = pl.BlockSpec(memory_space=pltpu.HBM)
SEM = pl.BlockSpec(memory_space=pltpu.SEMAPHORE)
EFFECT = pltpu.SideEffectType.DATAFLOW_SIDE_EFFECTING
LATE_COPIES = 7


def _late_copies(in_refs, land_refs, send_sems, recv_sems, only=None):
    x, y, c, chips = _place()
    sends, arrivals = [], []
    for a, (src, land) in enumerate(zip(in_refs, land_refs)):
        if only is not None and a not in only:
            continue
        hr = src.shape[0] // 2
        base = LATE_COPIES * a

        def cp(src_ref, dst_ref, s_idx, r_idx, to):
            return pltpu.make_async_remote_copy(src_ref=src_ref, dst_ref=dst_ref, send_sem=send_sems.at[base + s_idx],
                                                recv_sem=recv_sems.at[base + r_idx], device_id=to, device_id_type=MESH)

        sends.append(cp(src, land.at[2 * x + y], 0, 0, (x, y, 1 - c)))
        arrivals.append(cp(src, land.at[2 * x + y], 0, 0, (x, y, 1 - c)))
        for j, chip in enumerate(chips):
            for pc in range(2):
                half = src.at[pl.ds(c * hr, hr), :]
                sends.append(cp(half, land.at[2 * x + y, pl.ds(c * hr, hr), :], 1 + 2 * j + pc, 1 + 2 * j + c, (*chip, pc)))
                arrivals.append(cp(half, land.at[2 * chip[0] + chip[1], pl.ds(pc * hr, hr), :], 1 + 2 * j + pc,
                                   1 + 2 * j + pc, (*chip, pc)))
    return sends, arrivals


def _copies_start(name, build, nsem, srcs, land_shapes, after):
    n = len(srcs)

    def body(*refs):
        sends, _ = build(refs[:n], refs[n:2 * n], refs[2 * n + 1], refs[2 * n + 2])
        for cp in sends:
            cp.start()
        refs[-1][...] = jnp.zeros_like(refs[-1])

    lands = [pltpu.with_memory_space_constraint(lax.empty(s.shape, s.dtype), pltpu.HBM) for s in land_shapes]
    ins = [pltpu.with_memory_space_constraint(s, pltpu.HBM) for s in srcs]
    out = pl.pallas_call(
        body, name=name,
        out_shape=(pltpu.SemaphoreType.DMA((nsem,)), pltpu.SemaphoreType.DMA((nsem,)),
                   *[pltpu.HBM(s.shape, s.dtype) for s in srcs], *[pltpu.HBM(s.shape, s.dtype) for s in land_shapes],
                   jax.ShapeDtypeStruct((8, 128), F32)),
        in_specs=[HBM] * (2 * n) + [ANY],
        out_specs=(SEM, SEM, *[HBM] * (2 * n), pl.BlockSpec(memory_space=pltpu.VMEM)),
        input_output_aliases={i: 2 + i for i in range(2 * n)},
        compiler_params=pltpu.CompilerParams(has_side_effects=EFFECT),
    )(*ins, *lands, after)
    return out[0], out[1], out[2:2 + n], out[2 + n:2 + 2 * n], out[-1]


def _copies_wait(name, build, started, after):
    send_sems, recv_sems, srcs, lands, _ = started
    n = len(srcs)

    def body(*refs):
        sends, arrivals = build(refs[:n], refs[n:2 * n], refs[2 * n], refs[2 * n + 1])
        for cp in sends:
            cp.wait_send()
        for cp in arrivals:
            cp.wait_recv()

    out = pl.pallas_call(
        body, name=name,
        out_shape=(*[pltpu.HBM(s.shape, s.dtype) for s in srcs], *[pltpu.HBM(l.shape, l.dtype) for l in lands]),
        in_specs=[HBM] * (2 * n) + [SEM, SEM, ANY],
        out_specs=tuple([HBM] * (2 * n)),
        input_output_aliases={i: i for i in range(2 * n)},
        compiler_params=pltpu.CompilerParams(has_side_effects=EFFECT),
    )(*srcs, *lands, send_sems, recv_sems, after)
    return out[:n], out[n:]


def _exchange_copies(g_refs, got_refs, send_sems, recv_sems):
    x, y, c, _ = _place()
    sends, arrivals = [], []
    for a, (g, got) in enumerate(zip(g_refs, got_refs)):
        hr = g.shape[1] // 2
        cp = pltpu.make_async_remote_copy(
            src_ref=g.at[:, pl.ds((1 - c) * hr, hr), :], dst_ref=got, send_sem=send_sems.at[a],
            recv_sem=recv_sems.at[a], device_id=(x, y, 1 - c), device_id_type=MESH)
        sends.append(cp)
        arrivals.append(cp)
    return sends, arrivals


def _scatter_copies(s_refs, got_refs, send_sems, recv_sems):
    x, y, c, chips = _place()
    sends, arrivals = [], []
    for a, (s16, got) in enumerate(zip(s_refs, got_refs)):
        for j, chip in enumerate(chips):
            cp = pltpu.make_async_remote_copy(
                src_ref=s16.at[2 * chip[0] + chip[1]], dst_ref=got.at[j], send_sem=send_sems.at[3 * a + j],
                recv_sem=recv_sems.at[3 * a + j], device_id=(*chip, c), device_id_type=MESH)
            sends.append(cp)
            arrivals.append(cp)
    return sends, arrivals


def _share_halves(name, bufs, dep):
    n = len(bufs)

    def body(*refs):
        out_refs = refs[n + 1:2 * n + 1]
        send_sems, recv_sems = refs[2 * n + 1:]
        x, y, c, _ = _place()
        remote = [pltpu.make_async_remote_copy(
            src_ref=out_refs[a].at[c], dst_ref=out_refs[a].at[c], send_sem=send_sems.at[a], recv_sem=recv_sems.at[a],
            device_id=(x, y, 1 - c), device_id_type=MESH) for a in range(n)]
        for cp in remote:
            cp.start()
        for a in range(n):
            pltpu.make_async_remote_copy(
                src_ref=out_refs[a].at[c], dst_ref=out_refs[a].at[1 - c], send_sem=send_sems.at[a],
                recv_sem=recv_sems.at[a], device_id=(x, y, 1 - c), device_id_type=MESH).wait_recv()
        for cp in remote:
            cp.wait_send()

    return pl.pallas_call(
        body, name=name, in_specs=[ANY] * (n + 1), out_specs=[ANY] * n,
        out_shape=[jax.ShapeDtypeStruct(b.shape, b.dtype) for b in bufs],
        input_output_aliases={a: a for a in range(n)},
        scratch_shapes=[pltpu.SemaphoreType.DMA((n,)), pltpu.SemaphoreType.DMA((n,))],
    )(*bufs, dep)


SMALL_ROWS, SMALL_COLS = 16, CONV_CH


def _allreduce_small(block):
    m_per, ncol = block.shape

    def body(x_ref, sum_ref, all_ref, send_sems, recv_sems, local_sem):
        x, y, c, chips = _place()
        me, sibling = (x, y, c), (x, y, 1 - c)

        def rows(px, py, pc):
            return all_ref.at[pl.ds((4 * px + 2 * py + pc) * m_per, m_per), :]

        def copy(k, block_of, to, src=None):
            return pltpu.make_async_remote_copy(
                src_ref=rows(*block_of) if src is None else src, dst_ref=rows(*block_of),
                send_sem=send_sems.at[k], recv_sem=recv_sems.at[k], device_id=to, device_id_type=MESH)

        mine = pltpu.make_async_copy(x_ref, rows(*me), local_sem)
        mine.start()
        first = [copy(0, me, sibling, src=x_ref)]
        first += [copy(1 + j, me, (*chip, c), src=x_ref) for j, chip in enumerate(chips)]
        for cp in first:
            cp.start()
        passed = [copy(4 + j, (*chip, c), sibling) for j, chip in enumerate(chips)]
        for j, chip in enumerate(chips):
            copy(1 + j, (*chip, c), me).wait_recv()
            passed[j].start()
        copy(0, sibling, me).wait_recv()
        for j, chip in enumerate(chips):
            copy(4 + j, (*chip, 1 - c), me).wait_recv()
        for cp in first + passed:
            cp.wait_send()
        mine.wait()
        total = all_ref[0:m_per, :]
        for d in range(1, N_DEV):
            total = total + all_ref[d * m_per:(d + 1) * m_per, :]
        sum_ref[...] = total

    vm = pl.BlockSpec(memory_space=pltpu.VMEM)
    return pl.pallas_call(
        body, name="allreduce_small", in_specs=[vm], out_specs=vm,
        out_shape=jax.ShapeDtypeStruct((m_per, ncol), F32),
        scratch_shapes=[pltpu.VMEM((N_DEV * m_per, ncol), F32), pltpu.SemaphoreType.DMA((7,)),
                        pltpu.SemaphoreType.DMA((7,)), pltpu.SemaphoreType.DMA],
    )(block)


def _row_tile(rows, cols):
    tile = rows
    while tile * cols * 4 > (1 << 20) and tile % 16 == 0:
        tile //= 2
    return tile


def _elementwise(fn, name, ins, out_dtypes, dep):
    rows, cols = ins[0].shape
    tile = _row_tile(rows, cols)

    def body(*refs):
        outs = fn(*[r[...] for r in refs[:len(ins)]])
        for o_ref, o in zip(refs[len(ins) + 1:], outs):
            o_ref[...] = o.astype(o_ref.dtype)

    if tile * cols * 4 > (1 << 21) and cols % 512 == 0:
        spec = pl.BlockSpec((rows, 256), lambda i: (0, i))
        steps = cols // 256
    else:
        spec = pl.BlockSpec((tile, cols), lambda i: (i, 0))
        steps = rows // tile
    return pl.pallas_call(
        body, name=name, grid=(steps,), in_specs=[spec] * len(ins) + [pl.BlockSpec(memory_space=pl.ANY)],
        out_specs=[spec] * len(out_dtypes),
        out_shape=[jax.ShapeDtypeStruct((rows, cols), d) for d in out_dtypes],
        compiler_params=_params("parallel"),
    )(*ins, dep)


def _adamw_tile(w, g, m, v):
    m = ADAM_B1 * m + (1.0 - ADAM_B1) * g
    v = ADAM_B2 * v + (1.0 - ADAM_B2) * jnp.square(g)
    m_hat = m / (1.0 - ADAM_B1 ** ADAM_STEP)
    v_hat = v / (1.0 - ADAM_B2 ** ADAM_STEP)
    delta = -ADAM_LR * (m_hat / (jnp.sqrt(v_hat) + ADAM_EPS) + ADAM_WD * w)
    return delta, m, v


def _adamw(name, w, g, m, v, dep):
    return _elementwise(_adamw_tile, name, [w, g, m, v], [F32, F32, F32], dep)


def _chip_sum(name, g4, got, place):
    nchip, hr, cols = got.shape
    tile = _row_tile(hr, cols)
    nblk = hr // tile

    def body(pl_ref, g_ref, o_ref, s32_ref, s16_ref):
        s = g_ref[...] + o_ref[...]
        s32_ref[...] = s
        s16_ref[...] = s.astype(BF16)

    spec = pl.BlockSpec((None, tile, cols), lambda k, i, pr: (k, i, 0))
    return pl.pallas_call(
        body, name=name,
        grid_spec=pltpu.PrefetchScalarGridSpec(
            num_scalar_prefetch=1, grid=(nchip, nblk),
            in_specs=[pl.BlockSpec((None, tile, cols), lambda k, i, pr: (k, pr[1] * nblk + i, 0)), spec],
            out_specs=[spec, spec]),
        out_shape=[jax.ShapeDtypeStruct(got.shape, F32), jax.ShapeDtypeStruct(got.shape, BF16)],
        compiler_params=_params("parallel", "parallel"),
    )(place, g4, got)


def _mesh_sum(name, s32, got, place):
    _, hr, cols = s32.shape
    tile = _row_tile(hr, cols)

    def body(pl_ref, own_ref, g0_ref, g1_ref, g2_ref, o_ref):
        o_ref[...] = ((own_ref[...] + g0_ref[...].astype(F32)) + g1_ref[...].astype(F32)) + g2_ref[...].astype(F32)

    slab = lambda j: pl.BlockSpec((None, tile, cols), lambda i, pr: (j, i, 0))
    return pl.pallas_call(
        body, name=name,
        grid_spec=pltpu.PrefetchScalarGridSpec(
            num_scalar_prefetch=1, grid=(hr // tile,),
            in_specs=[pl.BlockSpec((None, tile, cols), lambda i, pr: (pr[0], i, 0)), slab(0), slab(1), slab(2)],
            out_specs=pl.BlockSpec((None, tile, cols), lambda i, pr: (pr[1], i, 0))),
        out_shape=jax.ShapeDtypeStruct((2, hr, cols), F32),
        compiler_params=_params("parallel"),
    )(place, s32, got, got, got)


def _place_operand():
    return jnp.stack([2 * lax.axis_index("x") + lax.axis_index("y"), lax.axis_index("c")]).astype(jnp.int32)


W_IN_ROWS = 720


def _per_chip(name, g):
    if name == "w_in_t":
        slabs = g.reshape(N_CHIPS, D_IN // N_CHIPS, D_MODEL)
        return jnp.pad(slabs, ((0, 0), (0, W_IN_ROWS - D_IN // N_CHIPS), (0, 0)))
    if name == "w_proj":
        return jnp.transpose(g.reshape(PLE_DIM, N_CHIPS, D_MODEL // N_CHIPS), (1, 0, 2))
    if name == "w_up4":
        return g
    return g.reshape(N_CHIPS, g.shape[0] // N_CHIPS, g.shape[1])


class _EarlyReducer:
    def __init__(self, tag):
        self.tag = tag

    def start(self, grads):
        self.names = list(grads)
        self.place = _place_operand()
        slabs = [_per_chip(k, grads[k]) for k in self.names]
        halves = [jax.ShapeDtypeStruct((s.shape[0], s.shape[1] // 2, s.shape[2]), F32) for s in slabs]
        self.a = _copies_start(self.tag + "exchange_start", _exchange_copies, len(slabs), slabs, halves,
                               slabs[0][0, :8, :128])
        return self.a[-1]

    def middle(self, after):
        slabs, got = _copies_wait(self.tag + "exchange_wait", _exchange_copies, self.a, after)
        self.sums = [_chip_sum(self.tag + "chip_sum_" + k, s, g, self.place) for k, s, g in zip(self.names, slabs, got)]
        s16 = [s[1] for s in self.sums]
        lands = [jax.ShapeDtypeStruct((3,) + s.shape[1:], BF16) for s in s16]
        self.b = _copies_start(self.tag + "scatter_start", _scatter_copies, 3 * len(s16), s16, lands,
                               self.sums[0][0][0, :8, :128])
        return self.b[-1]

    def finish(self, after):
        _, got = _copies_wait(self.tag + "scatter_wait", _scatter_copies, self.b, after)
        self.bufs = {k: _mesh_sum(self.tag + "mesh_sum_" + k, s[0], g, self.place)
                     for k, s, g in zip(self.names, self.sums, got)}


def kernel(x, p, norm_mix, w_in, conv_w, a_log, dt_bias, dn_norm, sinks, w_o, norm_mlp, w_up, w_down, norm_ple, w_ple_gate, w_ple_proj, norm_final, loss_target, m_norm_mix, m_w_in, m_conv_w, m_a_log, m_dt_bias, m_dn_norm, m_sinks, m_w_o, m_norm_mlp, m_w_up, m_w_down, m_norm_ple, m_w_ple_gate, m_w_ple_proj, m_norm_final, v_norm_mix, v_w_in, v_conv_w, v_a_log, v_dt_bias, v_dn_norm, v_sinks, v_w_o, v_norm_mlp, v_w_up, v_w_down, v_norm_ple, v_w_ple_gate, v_w_ple_proj, v_norm_final):
    chip = 2 * lax.axis_index("x") + lax.axis_index("y")
    big = dict(w_in=w_in[0], w_o=w_o[0], w_up=w_up[0], w_down=w_down[0], w_gate=w_ple_gate[0], w_proj=w_ple_proj[0])
    big_m = dict(w_in=m_w_in[0], w_o=m_w_o[0], w_up=m_w_up[0], w_down=m_w_down[0], w_gate=m_w_ple_gate[0], w_proj=m_w_ple_proj[0])
    big_v = dict(w_in=v_w_in[0], w_o=v_w_o[0], w_up=v_w_up[0], w_down=v_w_down[0], w_gate=v_w_ple_gate[0], w_proj=v_w_ple_proj[0])
    names = list(big)

    w_in_all, conv_all = _gather_weights([big["w_in"].astype(BF16)], conv_w[0])
    late_names = names[1:]
    late_shards = [big[k].astype(BF16) for k in late_names]
    gather = _copies_start("gather_start", _late_copies, LATE_COPIES * len(late_shards), late_shards,
                           [jax.ShapeDtypeStruct((N_CHIPS,) + s.shape, BF16) for s in late_shards], w_in_all)
    token = gather[-1]
    w = dict(w_in=jnp.transpose(w_in_all, (1, 0, 2)).reshape(D_MODEL, D_IN),
             conv_w=jnp.transpose(conv_all, (1, 0, 2)).reshape(DN_CONV, CONV_CH))
    sm = dict(norm_mix=norm_mix[0] + token[0, 0], a_log=a_log[0], dt_bias=dt_bias[0], dn_norm=dn_norm[0],
              sinks=sinks[0], norm_mlp=norm_mlp[0], norm_ple=norm_ple[0], norm_final=norm_final)

    def late(after):
        first = functools.partial(_late_copies, only=(0,))
        srcs, lands = _copies_wait("gather_wait_o", first, gather, after)

        def rest(after2):
            others = functools.partial(_late_copies, only=tuple(range(1, len(late_names))))
            gw = dict(zip(late_names, _copies_wait("gather_wait_rest", others, gather[:2] + (srcs, lands, None), after2)[1]))
            return dict(w_up4=gw["w_up"], w_down=gw["w_down"].reshape(D_FF, D_MODEL),
                        w_gate=gw["w_gate"].reshape(D_MODEL, D_MODEL), w_proj4=gw["w_proj"])

        return lands[0].reshape(D_MODEL, D_MODEL), rest

    reducer = _EarlyReducer("early_")
    sums, grad_x, g = _local_step(x[0], p[0, 0], loss_target[0], sm, w, late, reducer)

    last = _EarlyReducer("last_")
    dep_a = last.start({"w_in_t": g["w_in_t"]})

    row = lambda v: jnp.zeros((SMALL_COLS,), F32).at[:v.shape[0]].set(v)
    misc = jnp.zeros((SMALL_COLS,), F32).at[0:4].set(sums["a_log"]).at[4:8].set(sums["dt_bias"]) \
        .at[8:16].set(sums["sinks"]).at[128:256].set(sums["dn_norm"]).at[256].set(sums["loss"])
    small = jnp.concatenate([sums["conv_w"], jnp.stack([row(sums["norm_mix"]), row(sums["norm_mlp"]), row(sums["norm_ple"]),
                                                        row(sums["norm_final"]), misc]),
                             jnp.zeros((SMALL_ROWS - 9, SMALL_COLS), F32)], axis=0)
    tot = _allreduce_small(small + dep_a[0, 0])
    dep_b = last.middle(tot)
    grad_key = dict(w_o="w_o", w_up="w_up4", w_down="w_down", w_gate="w_gate", w_proj="w_proj")
    full = _share_halves("share_halves", [reducer.bufs[grad_key[k]] for k in late_names], dep_b)
    red = {k: f.reshape(-1, f.shape[-1]) for k, f in zip(late_names, full)}
    loss = tot[8, 256]
    ncw = CONV_CH // N_CHIPS

    def pack(cw, nmix, nmlp, nple, nfin, al, dtb, sk, dnn):
        misc_p = jnp.zeros((SMALL_COLS,), F32).at[0:4].set(al).at[4:8].set(dtb).at[8:16].set(sk).at[128:256].set(dnn)
        cw_p = jnp.zeros((DN_CONV, SMALL_COLS), F32).at[:, :ncw].set(cw)
        return jnp.concatenate([cw_p, jnp.stack([row(nmix), row(nmlp), row(nple), row(nfin), misc_p]),
                                jnp.zeros((SMALL_ROWS - 9, SMALL_COLS), F32)], axis=0)

    def unpack(buf):
        return dict(conv_w=buf[0:4, :ncw][None], norm_mix=buf[4, :D_MODEL][None], norm_mlp=buf[5, :D_MODEL][None],
                    norm_ple=buf[6, :D_MODEL][None], norm_final=buf[7, :D_MODEL], a_log=buf[8, 0:4][None],
                    dt_bias=buf[8, 4:8][None], sinks=buf[8, 8:16][None], dn_norm=buf[8, 128:256][None])

    g_conv_shard = lax.dynamic_slice(tot[0:4], (0, chip * ncw), (DN_CONV, ncw))
    g_small = pack(g_conv_shard, tot[4, :D_MODEL], tot[5, :D_MODEL], tot[6, :D_MODEL], tot[7, :D_MODEL],
                   tot[8, 0:4], tot[8, 4:8], tot[8, 8:16], tot[8, 128:256])
    w_small = pack(conv_w[0], norm_mix[0], norm_mlp[0], norm_ple[0], norm_final, a_log[0], dt_bias[0], sinks[0], dn_norm[0])
    m_small = pack(m_conv_w[0], m_norm_mix[0], m_norm_mlp[0], m_norm_ple[0], m_norm_final, m_a_log[0], m_dt_bias[0],
                   m_sinks[0], m_dn_norm[0])
    v_small = pack(v_conv_w[0], v_norm_mix[0], v_norm_mlp[0], v_norm_ple[0], v_norm_final, v_a_log[0], v_dt_bias[0],
                   v_sinks[0], v_dn_norm[0])

    ref_name = dict(w_in="w_in", w_o="w_o", w_up="w_up", w_down="w_down", w_gate="w_ple_gate", w_proj="w_ple_proj")
    out_g, out_d, out_m, out_v = {}, {}, {}, {}

    def update(k, dep):
        d_k, m_k, v_k = _adamw("adamw_" + k, big[k], red[k], big_m[k], big_v[k], dep)
        out_g[ref_name[k]], out_d[ref_name[k]] = red[k][None], d_k[None]
        out_m[ref_name[k]], out_v[ref_name[k]] = m_k[None], v_k[None]
        return d_k

    for k in late_names:
        done = update(k, dep_b)
    small_out = _adamw("adamw_small", w_small, g_small, m_small, v_small, dep_b)
    d_s, m_s, v_s = (unpack(b) for b in small_out)
    g_s = unpack(g_small)
    for src, dst in ((g_s, out_g), (d_s, out_d), (m_s, out_m), (v_s, out_v)):
        dst.update(src)
    last.finish(done + small_out[0][0:1, 0:1])
    (w_in_full,) = _share_halves("share_halves_w_in", [last.bufs["w_in_t"]], dep_b)
    g_t = w_in_full.reshape(W_IN_ROWS, D_MODEL)[:D_IN // N_CHIPS]
    d_t, m_t, v_t = _adamw("adamw_w_in", big["w_in"].T, g_t, big_m["w_in"].T, big_v["w_in"].T, dep_b)
    out_g["w_in"], out_d["w_in"], out_m["w_in"], out_v["w_in"] = g_t.T[None], d_t.T[None], m_t.T[None], v_t.T[None]
    order = ["norm_mix", "w_in", "conv_w", "a_log", "dt_bias", "dn_norm", "sinks", "w_o", "norm_mlp", "w_up", "w_down",
             "norm_ple", "w_ple_gate", "w_ple_proj", "norm_final"]
    return (loss, grad_x[None], *[out_g[k] for k in order], *[out_d[k] for k in order],
            *[out_m[k] for k in order], *[out_v[k] for k in order])
```

```python
import functools

import jax
import jax.numpy as jnp
from jax import lax
from jax.experimental import pallas as pl
from jax.experimental.pallas import tpu as pltpu

F32 = jnp.float32
BF16 = jnp.bfloat16
MXU_DTYPE = jnp.bfloat16
HI = lax.Precision.HIGHEST

D_MODEL = 1024
PLE_DIM = 256
ATTN_HEADS = 8
ATTN_KV_HEADS = 2
ATTN_GROUPS = ATTN_HEADS // ATTN_KV_HEADS
ATTN_HEAD_DIM = 64
ATTN_BLOCK = 128
ROPE_THETA = 10000.0
DN_HEADS = 4
DN_HEAD_DIM = 128
DN_CONV = 4
DN_CHUNK = 64
D_FF = 4 * D_MODEL
EPS = 1e-6
ATTN_Q = ATTN_HEADS * ATTN_HEAD_DIM
ATTN_KV = ATTN_KV_HEADS * ATTN_HEAD_DIM
DN_W = DN_HEADS * DN_HEAD_DIM
CONV_CH = 3 * DN_W
D_IN = ATTN_Q + 2 * ATTN_KV + 4 * DN_W + 2 * DN_HEADS
DN_COLS = 4 * DN_W + 128
DN_SCALE = DN_HEAD_DIM ** -0.5
ATTN_SCALE = ATTN_HEAD_DIM ** -0.5
FF_BLOCKS = 4
FF_BLOCK = D_FF // FF_BLOCKS

ADAM_LR = 0.001
ADAM_B1 = 0.9
ADAM_B2 = 0.999
ADAM_EPS = 1e-08
ADAM_WD = 0.01
ADAM_STEP = 10

V7X_VMEM_BYTES = 64 * 1024 * 1024
VMEM_LIMIT = 48 * 1024 * 1024

NN = ((1,), (0,))
NT = ((1,), (1,))
TN = ((0,), (0,))


def _dot(a, b, dims=NN, prec=None):
    if a.dtype != b.dtype:
        a, b = a.astype(MXU_DTYPE), b.astype(MXU_DTYPE)
    return lax.dot_general(a, b, (dims, ((), ())), precision=prec, preferred_element_type=F32)


def _sigmoid(x):
    return 1.0 / (1.0 + jnp.exp(-x))


def _softplus(x):
    return jnp.maximum(x, 0.0) + jnp.log(1.0 + jnp.exp(-jnp.abs(x)))


def _params(*sem):
    return pltpu.CompilerParams(dimension_semantics=sem, vmem_limit_bytes=VMEM_LIMIT)


def _rms_fwd(xv, g):
    r = lax.rsqrt(jnp.mean(xv * xv, axis=-1, keepdims=True) + EPS)
    return xv * r * g


def _rms_bwd(xv, g, dn):
    r = lax.rsqrt(jnp.mean(xv * xv, axis=-1, keepdims=True) + EPS)
    xh = xv * r
    dg = jnp.sum(dn * xh, axis=0, keepdims=True)
    dxh = dn * g
    dx = r * (dxh - xh * jnp.mean(dxh * xh, axis=-1, keepdims=True))
    return dx, dg


def _full(shape):
    return pl.BlockSpec(shape, lambda *_: (0,) * len(shape))


def _inproj(x, g_mix, wa, wd, tm):
    t = x.shape[0]

    def body(x_ref, g_ref, wa_ref, wd_ref, u_ref, pa_ref, pd_ref):
        u = _rms_fwd(x_ref[...], g_ref[...]).astype(MXU_DTYPE)
        u_ref[...] = u
        pa_ref[...] = _dot(u, wa_ref[...])
        pd_ref[...] = _dot(u, wd_ref[...])

    na, nd = wa.shape[1], wd.shape[1]
    return pl.pallas_call(
        body, name="inproj", grid=(t // tm,),
        in_specs=[pl.BlockSpec((tm, D_MODEL), lambda i: (i, 0)), _full((1, D_MODEL)),
                  _full((D_MODEL, na)), _full((D_MODEL, nd))],
        out_specs=[pl.BlockSpec((tm, D_MODEL), lambda i: (i, 0)), pl.BlockSpec((tm, na), lambda i: (i, 0)),
                   pl.BlockSpec((tm, nd), lambda i: (i, 0))],
        out_shape=[jax.ShapeDtypeStruct((t, D_MODEL), MXU_DTYPE), jax.ShapeDtypeStruct((t, na), F32),
                   jax.ShapeDtypeStruct((t, nd), F32)],
        compiler_params=_params("parallel"),
    )(x, g_mix, wa, wd)


def _oproj(x, ao, dn, wo_a, wo_d, tm):
    t = x.shape[0]

    def body(x_ref, ao_ref, dn_ref, wa_ref, wd_ref, h_ref):
        h_ref[...] = (x_ref[...] + _dot(ao_ref[...].astype(MXU_DTYPE), wa_ref[...])
                      + _dot(dn_ref[...].astype(MXU_DTYPE), wd_ref[...]))

    half = ao.shape[1]
    return pl.pallas_call(
        body, name="oproj", grid=(t // tm,),
        in_specs=[pl.BlockSpec((tm, D_MODEL), lambda i: (i, 0)), pl.BlockSpec((tm, half), lambda i: (i, 0)),
                  pl.BlockSpec((tm, half), lambda i: (i, 0)), _full((half, D_MODEL)), _full((half, D_MODEL))],
        out_specs=pl.BlockSpec((tm, D_MODEL), lambda i: (i, 0)),
        out_shape=jax.ShapeDtypeStruct((t, D_MODEL), F32),
        compiler_params=_params("parallel"),
    )(x, ao, dn, wo_a, wo_d)


def _mlp_fwd(h1, g_mlp, w_up4, w_down, tm):
    t = h1.shape[0]

    def body(h_ref, g_ref, wu_ref, wd_ref, m_ref, r_ref, h2_ref, acc_ref):
        k = pl.program_id(1)

        @pl.when(k == 0)
        def _():
            m_ref[...] = _rms_fwd(h_ref[...], g_ref[...]).astype(MXU_DTYPE)
            acc_ref[...] = jnp.zeros_like(acc_ref)

        r = jnp.maximum(_dot(m_ref[...], wu_ref[...]), 0.0)
        r_ref[...] = r.astype(MXU_DTYPE)
        s = jnp.square(r).astype(MXU_DTYPE)
        acc_ref[...] += _dot(s, wd_ref[...])

        @pl.when(k == FF_BLOCKS - 1)
        def _():
            h2_ref[...] = h_ref[...] + acc_ref[...]

    return pl.pallas_call(
        body, name="mlp_fwd", grid=(t // tm, FF_BLOCKS),
        in_specs=[pl.BlockSpec((tm, D_MODEL), lambda i, k: (i, 0)), _full((1, D_MODEL)),
                  pl.BlockSpec((None, D_MODEL, FF_BLOCK), lambda i, k: (k, 0, 0)),
                  pl.BlockSpec((FF_BLOCK, D_MODEL), lambda i, k: (k, 0))],
        out_specs=[pl.BlockSpec((tm, D_MODEL), lambda i, k: (i, 0)), pl.BlockSpec((tm, FF_BLOCK), lambda i, k: (i, k)),
                   pl.BlockSpec((tm, D_MODEL), lambda i, k: (i, 0))],
        out_shape=[jax.ShapeDtypeStruct((t, D_MODEL), MXU_DTYPE), jax.ShapeDtypeStruct((t, D_FF), MXU_DTYPE),
                   jax.ShapeDtypeStruct((t, D_MODEL), F32)],
        scratch_shapes=[pltpu.VMEM((tm, D_MODEL), F32)],
        compiler_params=_params("parallel", "arbitrary"),
    )(h1, g_mlp, w_up4, w_down)


def _ple_loss(h2, p, tgt, g_ple, g_fin, w_gate, w_proj, tm):
    t = h2.shape[0]

    def body(h_ref, p_ref, t_ref, gp_ref, gf_ref, wg_ref, wp_ref,
             dh_ref, dhb_ref, dgp_ref, dpp_ref, n3_ref, pb_ref, acc_ref):
        @pl.when(pl.program_id(0) == 0)
        def _():
            acc_ref[...] = jnp.zeros_like(acc_ref)

        h = h_ref[...]
        g_ple_v, g_fin_v = gp_ref[...], gf_ref[...]
        n3 = _rms_fwd(h, g_ple_v).astype(MXU_DTYPE)
        n3_ref[...] = n3
        gate = _sigmoid(_dot(n3, wg_ref[...]))
        pb = p_ref[...].astype(MXU_DTYPE)
        pb_ref[...] = pb
        pp = _dot(pb, wp_ref[...])
        h3 = h + gate * pp
        r4 = lax.rsqrt(jnp.mean(h3 * h3, axis=-1, keepdims=True) + EPS)
        xh4 = h3 * r4
        e = xh4 * g_fin_v - t_ref[...]
        loss = 0.5 * jnp.sum(jnp.mean(e * e, axis=-1, keepdims=True), axis=0, keepdims=True)
        dy = e * (1.0 / D_MODEL)
        dg_fin = jnp.sum(dy * xh4, axis=0, keepdims=True)
        dxh = dy * g_fin_v
        dh3 = r4 * (dxh - xh4 * jnp.mean(dxh * xh4, axis=-1, keepdims=True))
        dpp_ref[...] = (dh3 * gate).astype(MXU_DTYPE)
        dgp = (dh3 * pp * gate * (1.0 - gate)).astype(MXU_DTYPE)
        dgp_ref[...] = dgp
        dn3 = _dot(dgp, wg_ref[...], NT)
        dx, dg_ple = _rms_bwd(h, g_ple_v, dn3)
        dh2 = dh3 + dx
        dh_ref[...] = dh2
        dhb_ref[...] = dh2.astype(MXU_DTYPE)
        acc_ref[0:1, :] += dg_fin
        acc_ref[1:2, :] += dg_ple
        acc_ref[2:3, :] += jnp.broadcast_to(loss, (1, D_MODEL))

    row = lambda w: pl.BlockSpec((tm, w), lambda i: (i, 0))
    return pl.pallas_call(
        body, name="ple_loss", grid=(t // tm,),
        in_specs=[row(D_MODEL), row(PLE_DIM), row(D_MODEL), _full((1, D_MODEL)), _full((1, D_MODEL)),
                  _full((D_MODEL, D_MODEL)), _full((PLE_DIM, D_MODEL))],
        out_specs=[row(D_MODEL), row(D_MODEL), row(D_MODEL), row(D_MODEL), row(D_MODEL), row(PLE_DIM),
                   _full((8, D_MODEL))],
        out_shape=[jax.ShapeDtypeStruct((t, D_MODEL), F32), jax.ShapeDtypeStruct((t, D_MODEL), MXU_DTYPE),
                   jax.ShapeDtypeStruct((t, D_MODEL), MXU_DTYPE), jax.ShapeDtypeStruct((t, D_MODEL), MXU_DTYPE),
                   jax.ShapeDtypeStruct((t, D_MODEL), MXU_DTYPE), jax.ShapeDtypeStruct((t, PLE_DIM), MXU_DTYPE),
                   jax.ShapeDtypeStruct((8, D_MODEL), F32)],
        compiler_params=_params("arbitrary"),
    )(h2, p, tgt, g_ple, g_fin, w_gate, w_proj)


def _mlp_bwd(dh2, dh2b, r, h1, g_mlp, w_up4, w_down, tm):
    t = h1.shape[0]

    def body(dh_ref, dhb_ref, r_ref, h_ref, g_ref, wu_ref, wd_ref,
             da_ref, dh1_ref, dh1b_ref, acc_ref, dm_ref):
        i, k = pl.program_id(0), pl.program_id(1)

        @pl.when((i == 0) & (k == 0))
        def _():
            acc_ref[...] = jnp.zeros_like(acc_ref)

        @pl.when(k == 0)
        def _():
            dm_ref[...] = jnp.zeros_like(dm_ref)

        ds = _dot(dhb_ref[...], wd_ref[...], NT)
        da = (ds * (2.0 * r_ref[...].astype(F32))).astype(MXU_DTYPE)
        da_ref[...] = da
        dm_ref[...] += _dot(da, wu_ref[...], NT)

        @pl.when(k == FF_BLOCKS - 1)
        def _():
            dx, dg = _rms_bwd(h_ref[...], g_ref[...], dm_ref[...])
            dh1 = dh_ref[...] + dx
            dh1_ref[...] = dh1
            dh1b_ref[...] = dh1.astype(MXU_DTYPE)
            acc_ref[0:1, :] += dg

    tok = lambda w: pl.BlockSpec((tm, w), lambda i, k: (i, 0))
    return pl.pallas_call(
        body, name="mlp_bwd", grid=(t // tm, FF_BLOCKS),
        in_specs=[tok(D_MODEL), tok(D_MODEL), pl.BlockSpec((tm, FF_BLOCK), lambda i, k: (i, k)), tok(D_MODEL),
                  _full((1, D_MODEL)), pl.BlockSpec((None, D_MODEL, FF_BLOCK), lambda i, k: (k, 0, 0)),
                  pl.BlockSpec((FF_BLOCK, D_MODEL), lambda i, k: (k, 0))],
        out_specs=[pl.BlockSpec((tm, FF_BLOCK), lambda i, k: (i, k)),
                   tok(D_MODEL), tok(D_MODEL), pl.BlockSpec((8, D_MODEL), lambda i, k: (0, 0))],
        out_shape=[jax.ShapeDtypeStruct((t, D_FF), MXU_DTYPE),
                   jax.ShapeDtypeStruct((t, D_MODEL), F32), jax.ShapeDtypeStruct((t, D_MODEL), MXU_DTYPE),
                   jax.ShapeDtypeStruct((8, D_MODEL), F32)],
        scratch_shapes=[pltpu.VMEM((tm, D_MODEL), F32)],
        compiler_params=_params("arbitrary", "arbitrary"),
    )(dh2, dh2b, r, h1, g_mlp, w_up4, w_down)


def _oproj_bwd(dh1b, wo_a, wo_d, tm, dep):
    t = dh1b.shape[0]
    half = wo_a.shape[0]

    def body(d_ref, wa_ref, wd_ref, dep_ref, da_ref, dd_ref):
        d = d_ref[...]
        da_ref[...] = _dot(d, wa_ref[...], NT)
        dd_ref[...] = _dot(d, wd_ref[...], NT)

    return pl.pallas_call(
        body, name="oproj_bwd", grid=(t // tm,),
        in_specs=[pl.BlockSpec((tm, D_MODEL), lambda i: (i, 0)), _full((half, D_MODEL)), _full((half, D_MODEL)),
                  pl.BlockSpec(memory_space=pl.ANY)],
        out_specs=[pl.BlockSpec((tm, half), lambda i: (i, 0)), pl.BlockSpec((tm, half), lambda i: (i, 0))],
        out_shape=[jax.ShapeDtypeStruct((t, half), F32), jax.ShapeDtypeStruct((t, half), F32)],
        compiler_params=_params("parallel"),
    )(dh1b, wo_a, wo_d, dep)


def _inproj_bwd(x, dh1, g_mix, grads, weights, tm):
    t = x.shape[0]
    n = len(grads)

    def body(*refs):
        x_ref, dh_ref, g_ref = refs[:3]
        g_refs, w_refs = refs[3:3 + n], refs[3 + n:3 + 2 * n]
        dx_ref, acc_ref = refs[3 + 2 * n:]

        @pl.when(pl.program_id(0) == 0)
        def _():
            acc_ref[...] = jnp.zeros_like(acc_ref)

        du = _dot(g_refs[0][...], w_refs[0][...], NT)
        for j in range(1, n):
            du += _dot(g_refs[j][...], w_refs[j][...], NT)
        dx, dg = _rms_bwd(x_ref[...], g_ref[...], du)
        dx_ref[...] = dh_ref[...] + dx
        acc_ref[0:1, :] += dg

    tok = lambda w: pl.BlockSpec((tm, w), lambda i: (i, 0))
    return pl.pallas_call(
        body, name="inproj_bwd", grid=(t // tm,),
        in_specs=[tok(D_MODEL), tok(D_MODEL), _full((1, D_MODEL))] + [tok(g.shape[1]) for g in grads]
                 + [_full(w.shape) for w in weights],
        out_specs=[tok(D_MODEL), _full((8, D_MODEL))],
        out_shape=[jax.ShapeDtypeStruct((t, D_MODEL), F32), jax.ShapeDtypeStruct((8, D_MODEL), F32)],
        compiler_params=_params("arbitrary"),
    )(x, dh1, g_mix, *grads, *weights)


def _wgrad(a, b, name, tk, tn, tt, stacked=False, prep=None):
    t, kdim = a.shape
    ncols = b.shape[1]

    def body(a_ref, b_ref, o_ref):
        @pl.when(pl.program_id(2) == 0)
        def _():
            o_ref[...] = jnp.zeros_like(o_ref)

        av = a_ref[...] if prep is None else prep(a_ref[...])
        o_ref[...] += _dot(av, b_ref[...], TN)

    if stacked:
        out_spec = pl.BlockSpec((None, tk, tn), lambda i, j, s: (j, i, 0))
        out_shape = jax.ShapeDtypeStruct((ncols // tn, kdim, tn), F32)
    else:
        out_spec = pl.BlockSpec((tk, tn), lambda i, j, s: (i, j))
        out_shape = jax.ShapeDtypeStruct((kdim, ncols), F32)
    return pl.pallas_call(
        body, name=name, grid=(kdim // tk, ncols // tn, t // tt),
        in_specs=[pl.BlockSpec((tt, tk), lambda i, j, s: (s, i)), pl.BlockSpec((tt, tn), lambda i, j, s: (s, j))],
        out_specs=out_spec, out_shape=out_shape,
        compiler_params=_params("parallel", "parallel", "arbitrary"),
    )(a, b)


def _wgrad_cat(as_, bs, name, tt):
    t = as_[0].shape[0]
    heights = [a.shape[1] for a in as_]
    widths = [b.shape[1] for b in bs]

    def body(*refs):
        a_refs, b_refs, o_ref = refs[:len(as_)], refs[len(as_):-1], refs[-1]

        @pl.when(pl.program_id(0) == 0)
        def _():
            o_ref[...] = jnp.zeros_like(o_ref)

        row = 0
        for a_ref, k in zip(a_refs, heights):
            av = a_ref[...]
            col = 0
            for b_ref, n in zip(b_refs, widths):
                o_ref[row:row + k, col:col + n] += _dot(av, b_ref[...], TN)
                col += n
            row += k

    tok = lambda w: pl.BlockSpec((tt, w), lambda s: (s, 0))
    shape = (sum(heights), sum(widths))
    return pl.pallas_call(
        body, name=name, grid=(t // tt,),
        in_specs=[tok(k) for k in heights] + [tok(n) for n in widths],
        out_specs=_full(shape), out_shape=jax.ShapeDtypeStruct(shape, F32),
        compiler_params=_params("arbitrary"),
    )(*as_, *bs)


def _rope_tables(t):
    half = ATTN_HEAD_DIM // 2
    inv = 1.0 / (ROPE_THETA ** (jnp.arange(half, dtype=F32) * (2.0 / ATTN_HEAD_DIM)))
    ang = jnp.arange(t, dtype=F32)[:, None] * inv[None, :]
    cos, sin = jnp.cos(ang), jnp.sin(ang)
    cos2 = jnp.concatenate([cos, cos], axis=-1)
    sin2 = jnp.concatenate([-sin, sin], axis=-1)
    return jnp.tile(cos2, (1, 2)), jnp.tile(sin2, (1, 2))


def _swap_halves(tv):
    w = tv.shape[-1]
    lane = lax.broadcasted_iota(jnp.int32, tv.shape, tv.ndim - 1)
    first = (lane % ATTN_HEAD_DIM) < (ATTN_HEAD_DIM // 2)
    return jnp.where(first, pltpu.roll(tv, w - ATTN_HEAD_DIM // 2, tv.ndim - 1),
                     pltpu.roll(tv, ATTN_HEAD_DIM // 2, tv.ndim - 1))


def _rope(tv, cos, sin):
    return tv * cos + _swap_halves(tv) * sin


def _rope_bwd(dv, cos, sin):
    return dv * cos + _swap_halves(dv * sin)


def _attn_valid(first_block):
    c = lax.broadcasted_iota(jnp.int32, (2 * ATTN_BLOCK, ATTN_BLOCK), 0)
    r = lax.broadcasted_iota(jnp.int32, (2 * ATTN_BLOCK, ATTN_BLOCK), 1)
    return (c > r) & (c <= r + ATTN_BLOCK) & ((c >= ATTN_BLOCK) | jnp.logical_not(first_block))


def _attn_probs(st, sink, valid):
    s = jnp.where(valid, st * ATTN_SCALE, -jnp.inf)
    m = jnp.maximum(jnp.max(s, axis=0, keepdims=True), sink)
    e = jnp.where(valid, jnp.exp(s - m), 0.0)
    es = jnp.exp(sink - m)
    inv = 1.0 / (jnp.sum(e, axis=0, keepdims=True) + es)
    return e * inv, es * inv


def _lane_scalar(vec, idx):
    lane = lax.broadcasted_iota(jnp.int32, vec.shape, 1)
    return jnp.sum(jnp.where(lane == idx, vec, 0.0), axis=-1, keepdims=True)


def _attn_specs(nb):
    cur = lambda w, cb: pl.BlockSpec((ATTN_BLOCK, w), lambda i: (jnp.minimum(i, nb - 1), cb))
    prev = lambda w, cb: pl.BlockSpec((ATTN_BLOCK, w), lambda i: (jnp.maximum(jnp.minimum(i, nb - 1) - 1, 0), cb))
    kcol, vcol = ATTN_Q // ATTN_KV, ATTN_Q // ATTN_KV + 1
    return [cur(ATTN_Q, 0), cur(ATTN_KV, kcol), prev(ATTN_KV, kcol), cur(ATTN_KV, vcol), prev(ATTN_KV, vcol),
            cur(ATTN_KV, 0), cur(ATTN_KV, 0), prev(ATTN_KV, 0), prev(ATTN_KV, 0), _full((1, 128))]


def _attn_fwd(pa, cos, sin, sinks_vec):
    t = pa.shape[0]
    nb = t // ATTN_BLOCK

    def body(q_ref, kc_ref, kp_ref, vc_ref, vp_ref, cc_ref, sc_ref, cp_ref, sp_ref, sk_ref, o_ref):
        first = pl.program_id(0) == 0
        cc, sc = cc_ref[...], sc_ref[...]
        q = _rope(q_ref[...], jnp.tile(cc, (1, ATTN_Q // ATTN_KV)), jnp.tile(sc, (1, ATTN_Q // ATTN_KV)))
        kc = _rope(kc_ref[...], cc, sc)
        kp = _rope(kp_ref[...], cp_ref[...], sp_ref[...])
        vc, vp = vc_ref[...], vp_ref[...]
        sk = sk_ref[...]
        valid = _attn_valid(first)
        kv = lambda tp, tc, hk: jnp.concatenate([tp[:, hk * ATTN_HEAD_DIM:(hk + 1) * ATTN_HEAD_DIM],
                                                 tc[:, hk * ATTN_HEAD_DIM:(hk + 1) * ATTN_HEAD_DIM]], axis=0)
        kwins = [kv(kp, kc, hk) for hk in range(ATTN_KV_HEADS)]
        vwins_t = [kv(vp, vc, hk).T for hk in range(ATTN_KV_HEADS)]
        heads = [slice(h * ATTN_HEAD_DIM, (h + 1) * ATTN_HEAD_DIM) for h in range(ATTN_HEADS)]
        scores = [_dot(kwins[h // ATTN_GROUPS], q[:, hs], NT) for h, hs in enumerate(heads)]
        probs = [_attn_probs(st, _lane_scalar(sk, h), valid)[0] for h, st in enumerate(scores)]
        for h, (hs, pt) in enumerate(zip(heads, probs)):
            o_ref[:, hs] = _dot(vwins_t[h // ATTN_GROUPS], pt).T.astype(o_ref.dtype)

    return pl.pallas_call(
        body, name="attn_fwd", grid=(nb,),
        in_specs=_attn_specs(nb),
        out_specs=pl.BlockSpec((ATTN_BLOCK, ATTN_Q), lambda i: (i, 0)),
        out_shape=jax.ShapeDtypeStruct((t, ATTN_Q), MXU_DTYPE),
        compiler_params=_params("parallel"),
    )(pa, pa, pa, pa, pa, cos, sin, cos, sin, sinks_vec)


def _attn_bwd(pa, cos, sin, sinks_vec, dao):
    t = pa.shape[0]
    nb = t // ATTN_BLOCK

    def body(q_ref, kc_ref, kp_ref, vc_ref, vp_ref, cc_ref, sc_ref, cp_ref, sp_ref, sk_ref, do_ref,
             dq_ref, dk_ref, dv_ref, acc_ref, dqr_ref, dkw_ref, dvw_ref, ck_ref, cv_ref):
        i = pl.program_id(0)

        @pl.when(i == 0)
        def _():
            acc_ref[...] = jnp.zeros_like(acc_ref)
            ck_ref[...] = jnp.zeros_like(ck_ref)
            cv_ref[...] = jnp.zeros_like(cv_ref)

        @pl.when(i < nb)
        def _():
            first = i == 0
            cc, sc = cc_ref[...], sc_ref[...]
            cq, sq = jnp.tile(cc, (1, ATTN_Q // ATTN_KV)), jnp.tile(sc, (1, ATTN_Q // ATTN_KV))
            q = _rope(q_ref[...], cq, sq)
            kc = _rope(kc_ref[...], cc, sc)
            kp = _rope(kp_ref[...], cp_ref[...], sp_ref[...])
            vc, vp = vc_ref[...], vp_ref[...]
            sk = sk_ref[...]
            do = do_ref[...]
            lane = lax.broadcasted_iota(jnp.int32, (1, 128), 1)
            dsink = jnp.zeros((1, 128), F32)
            valid = _attn_valid(first)
            kv = lambda tp, tc, hk: jnp.concatenate([tp[:, hk * ATTN_HEAD_DIM:(hk + 1) * ATTN_HEAD_DIM],
                                                     tc[:, hk * ATTN_HEAD_DIM:(hk + 1) * ATTN_HEAD_DIM]], axis=0)
            kwins = [kv(kp, kc, hk) for hk in range(ATTN_KV_HEADS)]
            vwins = [kv(vp, vc, hk) for hk in range(ATTN_KV_HEADS)]
            kwins_t = [kw.T for kw in kwins]
            heads = [slice(h * ATTN_HEAD_DIM, (h + 1) * ATTN_HEAD_DIM) for h in range(ATTN_HEADS)]
            scores = [_dot(kwins[h // ATTN_GROUPS], q[:, hs], NT) for h, hs in enumerate(heads)]
            dps = [_dot(vwins[h // ATTN_GROUPS], do[:, hs], NT) for h, hs in enumerate(heads)]
            pts, dsts = [], []
            for h, (st, dp_t) in enumerate(zip(scores, dps)):
                probs_t, psink = _attn_probs(st, _lane_scalar(sk, h), valid)
                delta = jnp.sum(probs_t * dp_t, axis=0, keepdims=True)
                pts.append(probs_t)
                dsts.append(probs_t * (dp_t - delta) * ATTN_SCALE)
                dsink += jnp.where(lane == h, jnp.sum(-psink * delta, axis=1, keepdims=True), 0.0)
            for h, (hs, ds_t) in enumerate(zip(heads, dsts)):
                dqr_ref[:, hs] = _dot(kwins_t[h // ATTN_GROUPS], ds_t).T
            for hk in range(ATTN_KV_HEADS):
                ks = slice(hk * ATTN_HEAD_DIM, (hk + 1) * ATTN_HEAD_DIM)
                group = range(hk * ATTN_GROUPS, (hk + 1) * ATTN_GROUPS)
                ds_g = jnp.concatenate([dsts[h] for h in group], axis=1)
                p_g = jnp.concatenate([pts[h] for h in group], axis=1)
                q_g = jnp.concatenate([q[:, heads[h]] for h in group], axis=0)
                do_g = jnp.concatenate([do[:, heads[h]] for h in group], axis=0)
                dkw_ref[:, ks] = _dot(ds_g, q_g)
                dvw_ref[:, ks] = _dot(p_g, do_g)
            acc_ref[0:1, :] += dsink
            dq_ref[...] = _rope_bwd(dqr_ref[...], cq, sq).astype(dq_ref.dtype)
            dk_ref[...] = (ck_ref[...] + _rope_bwd(dkw_ref[0:ATTN_BLOCK, :], cp_ref[...], sp_ref[...])).astype(dk_ref.dtype)
            dv_ref[...] = (cv_ref[...] + dvw_ref[0:ATTN_BLOCK, :]).astype(dv_ref.dtype)
            ck_ref[...] = _rope_bwd(dkw_ref[ATTN_BLOCK:2 * ATTN_BLOCK, :], cc, sc)
            cv_ref[...] = dvw_ref[ATTN_BLOCK:2 * ATTN_BLOCK, :]

        @pl.when(i == nb)
        def _():
            dk_ref[...] = ck_ref[...].astype(dk_ref.dtype)
            dv_ref[...] = cv_ref[...].astype(dv_ref.dtype)

    prev_out = lambda w: pl.BlockSpec((ATTN_BLOCK, w), lambda i: (jnp.maximum(i - 1, 0), 0))
    return pl.pallas_call(
        body, name="attn_bwd", grid=(nb + 1,),
        in_specs=_attn_specs(nb) + [pl.BlockSpec((ATTN_BLOCK, ATTN_Q), lambda i: (jnp.minimum(i, nb - 1), 0))],
        out_specs=[pl.BlockSpec((ATTN_BLOCK, ATTN_Q), lambda i: (jnp.minimum(i, nb - 1), 0)), prev_out(ATTN_KV),
                   prev_out(ATTN_KV), _full((8, 128))],
        out_shape=[jax.ShapeDtypeStruct((t, ATTN_Q), MXU_DTYPE), jax.ShapeDtypeStruct((t, ATTN_KV), MXU_DTYPE),
                   jax.ShapeDtypeStruct((t, ATTN_KV), MXU_DTYPE), jax.ShapeDtypeStruct((8, 128), F32)],
        scratch_shapes=[pltpu.VMEM((ATTN_BLOCK, ATTN_Q), F32), pltpu.VMEM((2 * ATTN_BLOCK, ATTN_KV), F32),
                        pltpu.VMEM((2 * ATTN_BLOCK, ATTN_KV), F32), pltpu.VMEM((ATTN_BLOCK, ATTN_KV), F32),
                        pltpu.VMEM((ATTN_BLOCK, ATTN_KV), F32)],
        compiler_params=_params("arbitrary"),
    )(pa, pa, pa, pa, pa, cos, sin, cos, sin, sinks_vec, dao)


PAIR = 2 * DN_CHUNK
INTRA_PAIRS = 4
HALO = 8


def _conv_window(cur_ref, prev_ref, xs_ref, tm, has_prev):
    prev = jnp.where(has_prev, prev_ref[...], 0.0)
    xs_ref[0:HALO, :] = prev
    xs_ref[HALO:HALO + tm, :] = cur_ref[...]


def _conv_taps(xs_ref, cw_ref, tm):
    y = cw_ref[0:1, :] * xs_ref[pl.ds(HALO - DN_CONV + 1, tm), :]
    for j in range(1, DN_CONV):
        y += cw_ref[j:j + 1, :] * xs_ref[pl.ds(HALO - DN_CONV + 1 + j, tm), :]
    return y


def _gate_values(ba, al, dt):
    beta = _sigmoid(ba)
    pre = ba + dt
    g = -jnp.exp(al) * _softplus(pre)
    return beta, g, pre


def _dn_prep_specs(tm, tile):
    return [pl.BlockSpec((tm, CONV_CH), lambda i: (tile(i), 0)),
            pl.BlockSpec((HALO, CONV_CH), lambda i: (jnp.maximum(tile(i) * (tm // HALO) - 1, 0), 0)),
            pl.BlockSpec((tm, 128), lambda i: (tile(i), 4 * DN_W // 128)),
            _full((DN_CONV, CONV_CH)), _full((1, 128)), _full((1, 128))]


def _dn_prep(pd, conv_w, al_vec, dt_vec, tm):
    t = pd.shape[0]

    def body(cur_ref, prev_ref, ba_ref, cw_ref, al_ref, dt_ref, qn_ref, kn_ref, vc_ref, gc_ref, gr_ref, xs_ref):
        _conv_window(cur_ref, prev_ref, xs_ref, tm, pl.program_id(0) > 0)
        y = _conv_taps(xs_ref, cw_ref, tm)
        c = y * _sigmoid(y)
        for h in range(DN_HEADS):
            qs = slice(h * DN_HEAD_DIM, (h + 1) * DN_HEAD_DIM)
            ksl = slice(DN_W + h * DN_HEAD_DIM, DN_W + (h + 1) * DN_HEAD_DIM)
            qh, kh = c[:, qs], c[:, ksl]
            qn_ref[:, qs] = qh * lax.rsqrt(jnp.sum(qh * qh, axis=-1, keepdims=True) + EPS) * DN_SCALE
            kn_ref[:, qs] = kh * lax.rsqrt(jnp.sum(kh * kh, axis=-1, keepdims=True) + EPS)
        vc_ref[...] = c[:, 2 * DN_W:3 * DN_W]
        beta, g, _ = _gate_values(ba_ref[...], al_ref[...], dt_ref[...])
        lane = lax.broadcasted_iota(jnp.int32, beta.shape, 1)
        gb = jnp.where(lane < DN_HEADS, beta, jnp.where(lane < 2 * DN_HEADS, g, 0.0))
        gc_ref[...] = gb
        gr_ref[...] = gb.T[0:8, :]

    tok = lambda w: pl.BlockSpec((tm, w), lambda i: (i, 0))
    return pl.pallas_call(
        body, name="dn_prep", grid=(t // tm,),
        in_specs=_dn_prep_specs(tm, lambda i: i),
        out_specs=[tok(DN_W), tok(DN_W), tok(DN_W), tok(128), pl.BlockSpec((8, tm), lambda i: (0, i))],
        out_shape=[jax.ShapeDtypeStruct((t, DN_W), F32)] * 3 + [jax.ShapeDtypeStruct((t, 128), F32),
                                                                 jax.ShapeDtypeStruct((8, t), F32)],
        scratch_shapes=[pltpu.VMEM((HALO + tm, CONV_CH), F32)],
        compiler_params=_params("parallel"),
    )(pd, pd, pd, conv_w, al_vec, dt_vec)


def _pair_masks():
    r = lax.broadcasted_iota(jnp.int32, (PAIR, PAIR), 0)
    c = lax.broadcasted_iota(jnp.int32, (PAIR, PAIR), 1)
    same = (r < DN_CHUNK) == (c < DN_CHUNK)
    return same & (r >= c), same & (r > c)


def _lane_col(mat, idx):
    lane = lax.broadcasted_iota(jnp.int32, mat.shape, 1)
    return jnp.sum(jnp.where(lane == idx, mat, 0.0), axis=-1, keepdims=True)


def _pair_cumsums(gc, gr, low):
    lowf = low.astype(F32)
    return _dot(lowf, gc, NN, HI), _dot(gr, lowf, NT, HI)


def _pair_gates(gc, cum_c, cum_r, low, h):
    beta = _lane_col(gc, h)
    gam = _lane_col(cum_c, DN_HEADS + h)
    gam_row = cum_r[DN_HEADS + h:DN_HEADS + h + 1, :]
    dm = jnp.where(low, jnp.exp(jnp.where(low, gam - gam_row, 0.0)), 0.0)
    row = lax.broadcasted_iota(jnp.int32, gam.shape, 0)
    gl = jnp.where(row < DN_CHUNK, gam[DN_CHUNK - 1:DN_CHUNK, :], gam[PAIR - 1:PAIR, :])
    return beta, gam, dm, gl


def _split(a):
    hi = a.astype(BF16)
    return hi, (a - hi.astype(F32)).astype(BF16)


def _dot_split(a, b, dims=NN):
    (ah, al), (bh, bl) = a, b
    la, lb = (1, 1) if dims == TN else ((0, 1) if dims == NN else (0, 0))
    r = _dot(jnp.concatenate([ah, al], axis=la), jnp.concatenate([bh, bl], axis=lb), dims)
    m, n = r.shape[0] // 2, r.shape[1] // 2
    return (r[m:, n:] + (r[:m, n:] + r[m:, :n])) + r[:m, :n]


def _unit_lower_inverses(lmats):
    n = lmats[0].shape[0]
    r = lax.broadcasted_iota(jnp.int32, (n, n), 0)
    c = lax.broadcasted_iota(jnp.int32, (n, n), 1)
    same = lambda size: (r & ~(size - 1)) == (c & ~(size - 1))
    base = DN_CHUNK // 4
    diag = [jnp.where(same(base), l, 0.0) for l in lmats]
    accs = [(r == c).astype(F32) - d for d in diag]
    splits = [_split(d) for d in diag]
    step = 1
    while 2 * step < base:
        splits = [_split(_dot_split(s, s)) for s in splits]
        accs = [acc + _dot_split(_split(acc), s) for acc, s in zip(accs, splits)]
        step *= 2
    size = base
    while size < DN_CHUNK:
        below = same(2 * size) & jnp.logical_not(same(size))
        tb = [_dot(acc, jnp.where(below, l, 0.0)) for acc, l in zip(accs, lmats)]
        accs = [acc - _dot(t, acc) for acc, t in zip(accs, tb)]
        size *= 2
    return accs


def _dn_intra(qn, kn, vc, gc, gr):
    t = qn.shape[0]
    npair = t // PAIR
    rows_step = INTRA_PAIRS * PAIR

    def body(q_ref, k_ref, v_ref, gc_ref, gr_ref, u_ref, w_ref, qg_ref, kd_ref, a_ref, ti_ref, dl_ref):
        low, strict = _pair_masks()
        items = []
        for p in range(INTRA_PAIRS):
            rows = slice(p * PAIR, (p + 1) * PAIR)
            gc_v = gc_ref[rows, :]
            cum_c, cum_r = _pair_cumsums(gc_v, gr_ref[:, rows], low)
            for h in range(DN_HEADS):
                hs = slice(h * DN_HEAD_DIM, (h + 1) * DN_HEAD_DIM)
                items.append((p, h, rows, hs, _pair_gates(gc_v, cum_c, cum_r, low, h)))
        lmats = []
        for p, h, rows, hs, (beta, gam, dm, gl) in items:
            k = k_ref[rows, hs]
            lmats.append(jnp.where(strict, _dot(k * beta, k, NT) * dm, 0.0))
        tinvs = _unit_lower_inverses(lmats)
        for (p, h, rows, hs, (beta, gam, dm, gl)), tinv in zip(items, tinvs):
            q, k, v = q_ref[rows, hs], k_ref[rows, hs], v_ref[rows, hs]
            eg = jnp.exp(gam)
            u_ref[rows, hs] = _dot(tinv, v * beta)
            w_ref[rows, hs] = _dot(tinv, (k * beta) * eg).astype(w_ref.dtype)
            a_ref[h, rows, :] = _dot(q, k, NT) * dm
            ti_ref[h, rows, :] = tinv
            qg_ref[rows, hs] = (q * eg).astype(qg_ref.dtype)
            kd_ref[rows, hs] = (k * jnp.exp(gl - gam)).astype(kd_ref.dtype)
            for c in range(2):
                last = (c + 1) * DN_CHUNK - 1
                dl_ref[2 * p + c, h] = jnp.broadcast_to(jnp.exp(gam[last:last + 1, :]), (8, 128))

    tok = lambda w: pl.BlockSpec((rows_step, w), lambda n: (n, 0))
    hm = pl.BlockSpec((DN_HEADS, rows_step, PAIR), lambda n: (0, n, 0))
    return pl.pallas_call(
        body, name="dn_intra", grid=(npair // INTRA_PAIRS,),
        in_specs=[tok(DN_W), tok(DN_W), tok(DN_W), tok(128), pl.BlockSpec((8, rows_step), lambda n: (0, n))],
        out_specs=[tok(DN_W)] * 4 + [hm, hm, pl.BlockSpec((2 * INTRA_PAIRS, DN_HEADS, 8, 128), lambda n: (n, 0, 0, 0))],
        out_shape=[jax.ShapeDtypeStruct((t, DN_W), F32)] + [jax.ShapeDtypeStruct((t, DN_W), MXU_DTYPE)] * 3
                  + [jax.ShapeDtypeStruct((DN_HEADS, t, PAIR), F32)] * 2
                  + [jax.ShapeDtypeStruct((2 * npair, DN_HEADS, 8, 128), F32)],
        compiler_params=_params("parallel"),
    )(qn, kn, vc, gc, gr)


def _dn_scan_fwd(u, w, qg, kd, a_qk, dlast, pd, dn_w):
    t = u.shape[0]
    npair = t // PAIR

    def body(u_ref, w_ref, qg_ref, kd_ref, a_ref, dl_ref, z_ref, nw_ref, out_ref, o_ref, vn_ref, sall_ref, s_ref):
        @pl.when(pl.program_id(0) == 0)
        def _():
            s_ref[...] = jnp.zeros_like(s_ref)

        nw = nw_ref[...]
        for c in range(2):
            rows = slice(c * DN_CHUNK, (c + 1) * DN_CHUNK)
            for h in range(DN_HEADS):
                hs = slice(h * DN_HEAD_DIM, (h + 1) * DN_HEAD_DIM)
                st = s_ref[h]
                sall_ref[c, h] = st
                vn_ref[rows, hs] = (u_ref[rows, hs] - _dot(w_ref[rows, hs], st)).astype(vn_ref.dtype)
            for h in range(DN_HEADS):
                hs = slice(h * DN_HEAD_DIM, (h + 1) * DN_HEAD_DIM)
                st, vn = s_ref[h], vn_ref[rows, hs]
                o = _dot(qg_ref[rows, hs], st) + _dot(a_ref[h, rows, rows], vn)
                s_ref[h] = st * dl_ref[c, h][0:1, :] + _dot(kd_ref[rows, hs], vn, TN)
                o_ref[rows, hs] = o
                z = z_ref[rows, hs]
                on = o * lax.rsqrt(jnp.mean(o * o, axis=-1, keepdims=True) + EPS) * nw
                out_ref[rows, hs] = (on * (z * _sigmoid(z))).astype(out_ref.dtype)

    tok = pl.BlockSpec((PAIR, DN_W), lambda n: (n, 0))
    hm = pl.BlockSpec((DN_HEADS, PAIR, PAIR), lambda n: (0, n, 0))
    return pl.pallas_call(
        body, name="dn_scan_fwd", grid=(npair,),
        in_specs=[tok, tok, tok, tok, hm, pl.BlockSpec((2, DN_HEADS, 8, 128), lambda n: (n, 0, 0, 0)),
                  pl.BlockSpec((PAIR, DN_W), lambda n: (n, 3)), _full((1, 128))],
        out_specs=[tok, tok, tok, pl.BlockSpec((2, DN_HEADS, DN_HEAD_DIM, DN_HEAD_DIM), lambda n: (n, 0, 0, 0))],
        out_shape=[jax.ShapeDtypeStruct((t, DN_W), MXU_DTYPE), jax.ShapeDtypeStruct((t, DN_W), F32),
                   jax.ShapeDtypeStruct((t, DN_W), MXU_DTYPE),
                   jax.ShapeDtypeStruct((2 * npair, DN_HEADS, DN_HEAD_DIM, DN_HEAD_DIM), F32)],
        scratch_shapes=[pltpu.VMEM((DN_HEADS, DN_HEAD_DIM, DN_HEAD_DIM), F32)],
        compiler_params=_params("arbitrary"),
    )(u, w, qg, kd, a_qk, dlast, pd, dn_w)


def _dn_scan_bwd(dout, o, vnew, sall, w, qg, kd, a_qk, dlast, pd, dn_w):
    t = o.shape[0]
    npair = t // PAIR
    rev = lambda n: npair - 1 - n

    def body(do_ref, o_ref, vn_ref, sall_ref, w_ref, qg_ref, kd_ref, a_ref, dl_ref, z_ref, nw_ref,
             dz_ref, du_ref, dw_ref, dqg_ref, dkd_ref, da_ref, ddl_ref, acc_ref, ds_ref, dos_ref):
        @pl.when(pl.program_id(0) == 0)
        def _():
            ds_ref[...] = jnp.zeros_like(ds_ref)
            acc_ref[...] = jnp.zeros_like(acc_ref)

        nw = nw_ref[...]
        dnw = jnp.zeros((1, 128), F32)
        for h in range(DN_HEADS):
            hs = slice(h * DN_HEAD_DIM, (h + 1) * DN_HEAD_DIM)
            o, z, dout = o_ref[:, hs], z_ref[:, hs], do_ref[:, hs]
            r = lax.rsqrt(jnp.mean(o * o, axis=-1, keepdims=True) + EPS)
            oh = o * r
            sz = _sigmoid(z)
            dz_ref[:, hs] = dout * (oh * nw) * (sz + z * sz * (1.0 - sz))
            don = dout * (z * sz)
            dnw += jnp.sum(don * oh, axis=0, keepdims=True)
            doh = don * nw
            dos_ref[:, hs] = r * (doh - oh * jnp.mean(doh * oh, axis=-1, keepdims=True))
        acc_ref[0:1, :] += dnw
        for c in (1, 0):
            rows = slice(c * DN_CHUNK, (c + 1) * DN_CHUNK)
            other = slice((1 - c) * DN_CHUNK, (2 - c) * DN_CHUNK)
            for h in range(DN_HEADS):
                hs = slice(h * DN_HEAD_DIM, (h + 1) * DN_HEAD_DIM)
                do, st, dsp, vn = dos_ref[rows, hs], sall_ref[c, h], ds_ref[h], vn_ref[rows, hs]
                da_ref[h, rows, rows] = _dot(do, vn, NT)
                da_ref[h, rows, other] = jnp.zeros((DN_CHUNK, DN_CHUNK), F32)
                du_ref[rows, hs] = (_dot(a_ref[h, rows, rows], do, TN) + _dot(kd_ref[rows, hs], dsp)).astype(du_ref.dtype)
                dqg_ref[rows, hs] = _dot(do, st, NT)
                dkd_ref[rows, hs] = _dot(vn, dsp, NT)
                ddl = jnp.sum(jnp.sum(dsp * st, axis=1, keepdims=True), axis=0, keepdims=True)
                ddl_ref[c, h] = jnp.broadcast_to(ddl, (8, 128))
            for h in range(DN_HEADS):
                hs = slice(h * DN_HEAD_DIM, (h + 1) * DN_HEAD_DIM)
                do, st, dvn = dos_ref[rows, hs], sall_ref[c, h], du_ref[rows, hs]
                dw_ref[rows, hs] = (-_dot(dvn, st, NT)).astype(dw_ref.dtype)
                ds_ref[h] = (ds_ref[h] * dl_ref[c, h][0:1, :] + _dot(qg_ref[rows, hs], do, TN)
                             - _dot(w_ref[rows, hs], dvn, TN))

    tok = pl.BlockSpec((PAIR, DN_W), lambda n: (rev(n), 0))
    hm = pl.BlockSpec((DN_HEADS, PAIR, PAIR), lambda n: (0, rev(n), 0))
    sc = pl.BlockSpec((2, DN_HEADS, 8, 128), lambda n: (rev(n), 0, 0, 0))
    return pl.pallas_call(
        body, name="dn_scan_bwd", grid=(npair,),
        in_specs=[tok, tok, tok, pl.BlockSpec((2, DN_HEADS, DN_HEAD_DIM, DN_HEAD_DIM), lambda n: (rev(n), 0, 0, 0)),
                  tok, tok, tok, hm, sc, pl.BlockSpec((PAIR, DN_W), lambda n: (rev(n), 3)), _full((1, 128))],
        out_specs=[tok] * 5 + [hm, sc, _full((8, 128))],
        out_shape=[jax.ShapeDtypeStruct((t, DN_W), F32)] + [jax.ShapeDtypeStruct((t, DN_W), MXU_DTYPE)] * 2
                  + [jax.ShapeDtypeStruct((t, DN_W), F32)] * 2 + [jax.ShapeDtypeStruct((DN_HEADS, t, PAIR), F32),
                   jax.ShapeDtypeStruct((2 * npair, DN_HEADS, 8, 128), F32), jax.ShapeDtypeStruct((8, 128), F32)],
        scratch_shapes=[pltpu.VMEM((DN_HEADS, DN_HEAD_DIM, DN_HEAD_DIM), F32), pltpu.VMEM((PAIR, DN_W), F32)],
        compiler_params=_params("arbitrary"),
    )(dout, o, vnew, sall, w, qg, kd, a_qk, dlast, pd, dn_w)


def _dn_intra_bwd(qn, kn, vc, gc, gr, tinv, a_qk, du, dw, dqg, dkd, da_qk, ddlast, dlast, dep):
    t = qn.shape[0]
    npair = t // PAIR

    def body(q_ref, k_ref, v_ref, gc_ref, gr_ref, ti_ref, a_ref, du_ref, dw_ref, dqg_ref, dkd_ref, da_ref, ddl_ref, dl_ref,
             dep_ref, dq_ref, dk_ref, dv_ref, dg_ref):
        low, strict = _pair_masks()
        lane = lax.broadcasted_iota(jnp.int32, (PAIR, 128), 1)
        rowi = lax.broadcasted_iota(jnp.int32, (PAIR, 1), 0)
        rsum = lambda v: jnp.sum(v, axis=-1, keepdims=True)
        items = []
        for p in range(INTRA_PAIRS):
            rows = slice(p * PAIR, (p + 1) * PAIR)
            gc_v = gc_ref[rows, :]
            cum_c, cum_r = _pair_cumsums(gc_v, gr_ref[:, rows], low)
            for h in range(DN_HEADS):
                hs = slice(h * DN_HEAD_DIM, (h + 1) * DN_HEAD_DIM)
                items.append((p, h, rows, hs, _pair_gates(gc_v, cum_c, cum_r, low, h)))
        dtis, lmats, dvbs, dkbgs = [], [], [], []
        for p, h, rows, hs, (beta, gam, dm, gl) in items:
            k, tinv = k_ref[rows, hs], ti_ref[h, rows, :]
            kb = k * beta
            dtis.append(_dot(du_ref[rows, hs], v_ref[rows, hs] * beta, NT)
                        + _dot(dw_ref[rows, hs], kb * jnp.exp(gam), NT))
            lmats.append(jnp.where(strict, _dot(kb, k, NT) * dm, 0.0))
            dvbs.append(_dot(tinv, du_ref[rows, hs], TN))
            dkbgs.append(_dot(tinv, dw_ref[rows, hs], TN))
        xs = [_dot(ti_ref[h, rows, :], dti, TN) for (p, h, rows, hs, g), dti in zip(items, dtis)]
        dls = [jnp.where(strict, -_dot(x, ti_ref[h, rows, :], NT), 0.0) for (p, h, rows, hs, g), x in zip(items, xs)]
        dgam_all = [jnp.zeros((PAIR, 128), F32) for _ in range(INTRA_PAIRS)]
        dbeta_all = [jnp.zeros((PAIR, 128), F32) for _ in range(INTRA_PAIRS)]
        for (p, h, rows, hs, (beta, gam, dm, gl)), dl, lmat, dvb, dkbg in zip(items, dls, lmats, dvbs, dkbgs):
            q, k, v = q_ref[rows, hs], k_ref[rows, hs], v_ref[rows, hs]
            a = a_ref[h, rows, :]
            dqg, dkd = dqg_ref[rows, hs], dkd_ref[rows, hs]
            kb = k * beta
            eg = jnp.exp(gam)
            ekd = jnp.exp(gl - gam)
            dmm = dl * dm
            dam = jnp.where(low, da_ref[h, rows, :], 0.0)
            dn = dam * dm
            e = dl * lmat + dam * a
            dkb = _dot(dmm, k) + dkbg * eg
            dk_ref[rows, hs] = _dot(dmm, kb, TN) + _dot(dn, q, TN) + dkd * ekd + dkb * beta
            dq_ref[rows, hs] = _dot(dn, k) + dqg * eg
            dv_ref[rows, hs] = dvb * beta
            t_kd = rsum(dkd * (k * ekd))
            dgam = rsum(e) - rsum(e.T) + rsum(dqg * (q * eg)) + rsum(dkbg * (kb * eg)) - t_kd
            for c in range(2):
                crows = slice(c * DN_CHUNK, (c + 1) * DN_CHUNK)
                dgl = (jnp.sum(t_kd[crows, :], axis=0, keepdims=True)
                       + ddl_ref[2 * p + c, h][0:1, 0:1] * dl_ref[2 * p + c, h][0:1, 0:1])
                dgam = dgam + jnp.where(rowi == (c + 1) * DN_CHUNK - 1, dgl, 0.0)
            dgam_all[p] += jnp.where(lane == DN_HEADS + h, dgam, 0.0)
            dbeta_all[p] += jnp.where(lane == h, rsum(dkb * k) + rsum(dvb * v), 0.0)
        for p in range(INTRA_PAIRS):
            dg_ref[p * PAIR:(p + 1) * PAIR, :] = dbeta_all[p] + _dot(low.astype(F32), dgam_all[p], TN, HI)

    rows_step = INTRA_PAIRS * PAIR
    tok = lambda w: pl.BlockSpec((rows_step, w), lambda n: (n, 0))
    hm = pl.BlockSpec((DN_HEADS, rows_step, PAIR), lambda n: (0, n, 0))
    sc = pl.BlockSpec((2 * INTRA_PAIRS, DN_HEADS, 8, 128), lambda n: (n, 0, 0, 0))
    return pl.pallas_call(
        body, name="dn_intra_bwd", grid=(npair // INTRA_PAIRS,),
        in_specs=[tok(DN_W), tok(DN_W), tok(DN_W), tok(128), pl.BlockSpec((8, rows_step), lambda n: (0, n)), hm, hm,
                  tok(DN_W), tok(DN_W), tok(DN_W), tok(DN_W), hm, sc, sc, pl.BlockSpec(memory_space=pl.ANY)],
        out_specs=[tok(DN_W), tok(DN_W), tok(DN_W), tok(128)],
        out_shape=[jax.ShapeDtypeStruct((t, DN_W), F32)] * 3 + [jax.ShapeDtypeStruct((t, 128), F32)],
        compiler_params=_params("parallel"),
    )(qn, kn, vc, gc, gr, tinv, a_qk, du, dw, dqg, dkd, da_qk, ddlast, dlast, dep)


def _dn_prep_bwd(pd, conv_w, al_vec, dt_vec, dqn, dkn, dvc, dgc, dz, tm):
    t = pd.shape[0]
    nt = t // tm
    tile = lambda i: nt - 1 - i

    def body(cur_ref, prev_ref, ba_ref, cw_ref, al_ref, dt_ref, dq_ref, dk_ref, dv_ref, dg_ref, dz_ref,
             o_ref, accw_ref, accg_ref, xs_ref, dc_ref, ds_ref, carry_ref):
        @pl.when(pl.program_id(0) == 0)
        def _():
            accw_ref[...] = jnp.zeros_like(accw_ref)
            accg_ref[...] = jnp.zeros_like(accg_ref)
            carry_ref[...] = jnp.zeros_like(carry_ref)

        _conv_window(cur_ref, prev_ref, xs_ref, tm, tile(pl.program_id(0)) > 0)
        y = _conv_taps(xs_ref, cw_ref, tm)
        sg = _sigmoid(y)
        c = y * sg
        for h in range(DN_HEADS):
            qs = slice(h * DN_HEAD_DIM, (h + 1) * DN_HEAD_DIM)
            ksl = slice(DN_W + h * DN_HEAD_DIM, DN_W + (h + 1) * DN_HEAD_DIM)
            for src, sl, scale in ((dq_ref, qs, DN_SCALE), (dk_ref, ksl, 1.0)):
                xh = c[:, sl]
                r = lax.rsqrt(jnp.sum(xh * xh, axis=-1, keepdims=True) + EPS)
                unit = xh * r
                dn = src[:, qs] * scale
                dc_ref[:, sl] = r * (dn - unit * jnp.sum(dn * unit, axis=-1, keepdims=True))
        dc_ref[:, 2 * DN_W:3 * DN_W] = dv_ref[...]
        dy = dc_ref[...] * (sg + y * sg * (1.0 - sg))
        for j in range(DN_CONV):
            accw_ref[j:j + 1, :] += jnp.sum(dy * xs_ref[pl.ds(HALO - DN_CONV + 1 + j, tm), :], axis=0, keepdims=True)
        ds_ref[0:tm, :] = dy
        ds_ref[tm:tm + HALO, :] = carry_ref[...]
        carry_ref[...] = ds_ref[0:HALO, :]
        dx = cw_ref[0:1, :] * ds_ref[pl.ds(DN_CONV - 1, tm), :]
        for j in range(1, DN_CONV):
            dx += cw_ref[j:j + 1, :] * ds_ref[pl.ds(DN_CONV - 1 - j, tm), :]

        beta, g, pre = _gate_values(ba_ref[...], al_ref[...], dt_ref[...])
        dgb = dg_ref[...]
        lane = lax.broadcasted_iota(jnp.int32, dgb.shape, 1)
        is_b, is_a = lane < DN_HEADS, (lane >= DN_HEADS) & (lane < 2 * DN_HEADS)
        dpre = dgb * (-jnp.exp(al_ref[...])) * _sigmoid(pre)
        dba = jnp.where(is_b, dgb * beta * (1.0 - beta), jnp.where(is_a, dpre, 0.0))
        accg_ref[0:1, :] += jnp.sum(jnp.where(is_a, dgb * g, 0.0), axis=0, keepdims=True)
        accg_ref[1:2, :] += jnp.sum(jnp.where(is_a, dpre, 0.0), axis=0, keepdims=True)
        o_ref[:, 0:CONV_CH] = dx.astype(o_ref.dtype)
        o_ref[:, CONV_CH:CONV_CH + DN_W] = dz_ref[...].astype(o_ref.dtype)
        o_ref[:, CONV_CH + DN_W:DN_COLS] = dba.astype(o_ref.dtype)

    tok = lambda w: pl.BlockSpec((tm, w), lambda i: (tile(i), 0))
    return pl.pallas_call(
        body, name="dn_prep_bwd", grid=(nt,),
        in_specs=_dn_prep_specs(tm, tile) + [tok(DN_W), tok(DN_W), tok(DN_W), tok(128), tok(DN_W)],
        out_specs=[tok(DN_COLS), _full((8, CONV_CH)), _full((8, 128))],
        out_shape=[jax.ShapeDtypeStruct((t, DN_COLS), MXU_DTYPE),
                   jax.ShapeDtypeStruct((8, CONV_CH), F32), jax.ShapeDtypeStruct((8, 128), F32)],
        scratch_shapes=[pltpu.VMEM((HALO + tm, CONV_CH), F32), pltpu.VMEM((tm, CONV_CH), F32),
                        pltpu.VMEM((tm + HALO, CONV_CH), F32), pltpu.VMEM((HALO, CONV_CH), F32)],
        compiler_params=_params("arbitrary"),
    )(pd, pd, pd, conv_w, al_vec, dt_vec, dqn, dkn, dvc, dgc, dz)


def _pad_lanes(v, offset=0):
    return jnp.zeros((1, 128), F32).at[0, offset:offset + v.shape[0]].set(v.astype(F32))


class _LocalReducer:
    def start(self, grads):
        return jnp.zeros((8, 128), F32)

    def middle(self, after):
        return jnp.zeros((8, 128), F32)

    def finish(self, after):
        return None


def _local_step(x, p, tgt, sm, w, late, reducer):
    t = x.shape[0]
    tm = min(512, t // 2)
    tm_s = min(512, t // 2)
    tw = min(1024, t // 2)

    w_in = w["w_in"]
    wa = w_in[:, :ATTN_Q + 2 * ATTN_KV]
    wd = jnp.pad(w_in[:, ATTN_Q + 2 * ATTN_KV:], ((0, 0), (0, DN_COLS - (D_IN - ATTN_Q - 2 * ATTN_KV))))
    conv_w = w["conv_w"]
    al_vec, dt_vec = _pad_lanes(sm["a_log"], DN_HEADS), _pad_lanes(sm["dt_bias"], DN_HEADS)
    sinks_vec = _pad_lanes(sm["sinks"])
    dn_w = sm["dn_norm"].reshape(1, 128)
    row = lambda v: v.reshape(1, D_MODEL)
    cos, sin = _rope_tables(t)

    u, pa, pd = _inproj(x, row(sm["norm_mix"]), wa, wd, tm_s)
    ao = _attn_fwd(pa, cos, sin, sinks_vec)
    qn, kn, vc, gc, gr = _dn_prep(pd, conv_w, al_vec, dt_vec, tm_s)
    uu, ww, qg, kd, a_qk, tinv, dlast = _dn_intra(qn, kn, vc, gc, gr)
    dn_out, o, vnew, sall = _dn_scan_fwd(uu, ww, qg, kd, a_qk, dlast, pd, dn_w)
    w_o, late_rest = late(dn_out)
    wo_a, wo_d = w_o[:ATTN_Q], w_o[ATTN_Q:]
    h1 = _oproj(x, ao, dn_out, wo_a, wo_d, tm)
    w = dict(w, **late_rest(h1))
    w_proj = jnp.transpose(w["w_proj4"], (1, 0, 2)).reshape(PLE_DIM, D_MODEL)
    m, r, h2 = _mlp_fwd(h1, row(sm["norm_mlp"]), w["w_up4"], w["w_down"], tw)
    dh2, dh2b, dgp, dpp, n3, pb, acc_ple = _ple_loss(h2, p, tgt, row(sm["norm_ple"]), row(sm["norm_final"]),
                                                     w["w_gate"], w_proj, tm_s)
    g_w_gate = _wgrad(n3, dgp, "wgrad_gate", D_MODEL, D_MODEL, tw)
    g_w_proj = _wgrad(pb, dpp, "wgrad_proj", PLE_DIM, D_MODEL, tw)
    da, dh1, dh1b, acc_mlp = _mlp_bwd(dh2, dh2b, r, h1, row(sm["norm_mlp"]), w["w_up4"], w["w_down"], tm)
    g_w_up4 = _wgrad(m, da, "wgrad_up", D_MODEL, FF_BLOCK, tw, stacked=True)
    g_w_down = _wgrad(r, dh2b, "wgrad_down", FF_BLOCK, D_MODEL, tw,
                      prep=lambda rv: jnp.square(rv.astype(F32)).astype(MXU_DTYPE))
    g_w_o = _wgrad_cat([ao, dn_out], [dh1b], "wgrad_o", tw)
    early = dict(w_up4=g_w_up4, w_down=g_w_down, w_gate=g_w_gate, w_proj=g_w_proj, w_o=g_w_o)
    dep = reducer.start(early)
    dao, ddn = _oproj_bwd(dh1b, wo_a, wo_d, tm, dep)
    dz, du, dw, dqg, dkd, da_qk, ddlast, acc_dn = _dn_scan_bwd(ddn, o, vnew, sall, ww, qg, kd, a_qk, dlast, pd, dn_w)
    dep = reducer.middle(du)
    dqn, dkn, dvc, dgc = _dn_intra_bwd(qn, kn, vc, gc, gr, tinv, a_qk, du, dw, dqg, dkd, da_qk, ddlast, dlast, dep)
    d_dn, acc_conv, acc_gate = _dn_prep_bwd(pd, conv_w, al_vec, dt_vec, dqn, dkn, dvc, dgc, dz, tm_s)
    dq, dk, dv, acc_attn = _attn_bwd(pa, cos, sin, sinks_vec, dao)
    reducer.finish(dq)
    wq, wk, wv = wa[:, :ATTN_Q], wa[:, ATTN_Q:ATTN_Q + ATTN_KV], wa[:, ATTN_Q + ATTN_KV:]
    dx, acc_mix = _inproj_bwd(x, dh1, row(sm["norm_mix"]), [dq, dk, dv, d_dn], [wq, wk, wv, wd], tm_s)

    g_w_in_t = _wgrad_cat([dq, dk, dv, d_dn], [u], "wgrad_in", tw)[:D_IN]
    grads = dict(early, w_in_t=g_w_in_t)
    sums = dict(loss=acc_ple[2, 0], norm_final=acc_ple[0], norm_ple=acc_ple[1], norm_mlp=acc_mlp[0], norm_mix=acc_mix[0],
                dn_norm=acc_dn[0], sinks=acc_attn[0, :ATTN_HEADS], a_log=acc_gate[0, DN_HEADS:2 * DN_HEADS],
                dt_bias=acc_gate[1, DN_HEADS:2 * DN_HEADS], conv_w=acc_conv[:DN_CONV])
    return sums, dx, grads


MESH = pl.DeviceIdType.MESH
ANY = pl.BlockSpec(memory_space=pl.ANY)
N_CHIPS = 4
N_DEV = 8


def _place():
    x, y, c = lax.axis_index("x"), lax.axis_index("y"), lax.axis_index("c")
    chips = [(1 - x, y), (x, 1 - y), (1 - x, 1 - y)]
    return x, y, c, chips


def _gather_weights(shards, conv_s):
    n = len(shards)
    per = 7

    def body(*refs):
        in_refs, conv_ref = refs[:n], refs[n]
        out_refs, conv_out = refs[n + 1:2 * n + 1], refs[2 * n + 1]
        send_sems, recv_sems = refs[2 * n + 2:]
        x, y, c, chips = _place()
        sibling = (x, y, 1 - c)

        def blk(a, px, py, pc):
            hr = in_refs[a].shape[0] // 2
            return out_refs[a].at[2 * px + py, pl.ds(pc * hr, hr), :]

        def mine(a):
            hr = in_refs[a].shape[0] // 2
            return in_refs[a].at[pl.ds(c * hr, hr), :]

        def rcopy(a, k, block, to, src=None):
            return pltpu.make_async_remote_copy(
                src_ref=blk(a, *block) if src is None else src, dst_ref=blk(a, *block),
                send_sem=send_sems.at[per * a + k], recv_sem=recv_sems.at[per * a + k],
                device_id=to, device_id_type=MESH)

        def whole(a, to):
            return pltpu.make_async_remote_copy(
                src_ref=in_refs[a], dst_ref=out_refs[a].at[2 * x + y],
                send_sem=send_sems.at[per * a], recv_sem=recv_sems.at[per * a], device_id=to, device_id_type=MESH)

        def ccopy(j, to):
            return pltpu.make_async_remote_copy(
                src_ref=conv_ref, dst_ref=conv_out.at[2 * x + y],
                send_sem=send_sems.at[per * n + j], recv_sem=recv_sems.at[per * n + j],
                device_id=to, device_id_type=MESH)

        started = []
        for a in range(n):
            first = [whole(a, sibling)]
            first += [rcopy(a, 1 + j, (x, y, c), (*chip, c), src=mine(a)) for j, chip in enumerate(chips)]
            for cp in first:
                cp.start()
            started += first
        conv_sends = [ccopy(j, (*chip, c)) for j, chip in enumerate(chips)] + [ccopy(3, sibling)]
        for cp in conv_sends:
            cp.start()
        started += conv_sends
        for a in range(n):
            for j, chip in enumerate(chips):
                rcopy(a, 1 + j, (*chip, c), (x, y, c)).wait_recv()
                fwd = rcopy(a, 4 + j, (*chip, c), sibling)
                fwd.start()
                started.append(fwd)
        for a in range(n):
            whole(a, sibling).wait_recv()
            for j, chip in enumerate(chips):
                rcopy(a, 4 + j, (*chip, 1 - c), (x, y, c)).wait_recv()
        for j, chip in enumerate(chips + [(x, y)]):
            pltpu.make_async_remote_copy(
                src_ref=conv_ref, dst_ref=conv_out.at[2 * chip[0] + chip[1]],
                send_sem=send_sems.at[per * n + j], recv_sem=recv_sems.at[per * n + j],
                device_id=sibling, device_id_type=MESH).wait_recv()
        for cp in started:
            cp.wait_send()

    nsem = per * n + 4
    out_shape = [jax.ShapeDtypeStruct((N_CHIPS,) + s.shape, s.dtype) for s in shards]
    out_shape.append(jax.ShapeDtypeStruct((N_CHIPS,) + conv_s.shape, conv_s.dtype))
    return pl.pallas_call(
        body, name="gather_weights", in_specs=[ANY] * (n + 1), out_specs=[ANY] * (n + 1), out_shape=out_shape,
        scratch_shapes=[pltpu.SemaphoreType.DMA((nsem,)), pltpu.SemaphoreType.DMA((nsem,))],
    )(*shards, conv_s)


HBM = pl.BlockSpec(memory_space=pltpu.HBM)
SEM = pl.BlockSpec(memory_space=pltpu.SEMAPHORE)
EFFECT = pltpu.SideEffectType.DATAFLOW_SIDE_EFFECTING
LATE_COPIES = 7


def _late_copies(in_refs, land_refs, send_sems, recv_sems, only=None):
    x, y, c, chips = _place()
    sends, arrivals = [], []
    for a, (src, land) in enumerate(zip(in_refs, land_refs)):
        if only is not None and a not in only:
            continue
        hr = src.shape[0] // 2
        base = LATE_COPIES * a

        def cp(src_ref, dst_ref, s_idx, r_idx, to):
            return pltpu.make_async_remote_copy(src_ref=src_ref, dst_ref=dst_ref, send_sem=send_sems.at[base + s_idx],
                                                recv_sem=recv_sems.at[base + r_idx], device_id=to, device_id_type=MESH)

        sends.append(cp(src, land.at[2 * x + y], 0, 0, (x, y, 1 - c)))
        arrivals.append(cp(src, land.at[2 * x + y], 0, 0, (x, y, 1 - c)))
        for j, chip in enumerate(chips):
            for pc in range(2):
                half = src.at[pl.ds(c * hr, hr), :]
                sends.append(cp(half, land.at[2 * x + y, pl.ds(c * hr, hr), :], 1 + 2 * j + pc, 1 + 2 * j + c, (*chip, pc)))
                arrivals.append(cp(half, land.at[2 * chip[0] + chip[1], pl.ds(pc * hr, hr), :], 1 + 2 * j + pc,
                                   1 + 2 * j + pc, (*chip, pc)))
    return sends, arrivals


def _copies_start(name, build, nsem, srcs, land_shapes, after):
    n = len(srcs)

    def body(*refs):
        sends, _ = build(refs[:n], refs[n:2 * n], refs[2 * n + 1], refs[2 * n + 2])
        for cp in sends:
            cp.start()
        refs[-1][...] = jnp.zeros_like(refs[-1])

    lands = [pltpu.with_memory_space_constraint(lax.empty(s.shape, s.dtype), pltpu.HBM) for s in land_shapes]
    ins = [pltpu.with_memory_space_constraint(s, pltpu.HBM) for s in srcs]
    out = pl.pallas_call(
        body, name=name,
        out_shape=(pltpu.SemaphoreType.DMA((nsem,)), pltpu.SemaphoreType.DMA((nsem,)),
                   *[pltpu.HBM(s.shape, s.dtype) for s in srcs], *[pltpu.HBM(s.shape, s.dtype) for s in land_shapes],
                   jax.ShapeDtypeStruct((8, 128), F32)),
        in_specs=[HBM] * (2 * n) + [ANY],
        out_specs=(SEM, SEM, *[HBM] * (2 * n), pl.BlockSpec(memory_space=pltpu.VMEM)),
        input_output_aliases={i: 2 + i for i in range(2 * n)},
        compiler_params=pltpu.CompilerParams(has_side_effects=EFFECT),
    )(*ins, *lands, after)
    return out[0], out[1], out[2:2 + n], out[2 + n:2 + 2 * n], out[-1]


def _copies_wait(name, build, started, after):
    send_sems, recv_sems, srcs, lands, _ = started
    n = len(srcs)

    def body(*refs):
        sends, arrivals = build(refs[:n], refs[n:2 * n], refs[2 * n], refs[2 * n + 1])
        for cp in sends:
            cp.wait_send()
        for cp in arrivals:
            cp.wait_recv()

    out = pl.pallas_call(
        body, name=name,
        out_shape=(*[pltpu.HBM(s.shape, s.dtype) for s in srcs], *[pltpu.HBM(l.shape, l.dtype) for l in lands]),
        in_specs=[HBM] * (2 * n) + [SEM, SEM, ANY],
        out_specs=tuple([HBM] * (2 * n)),
        input_output_aliases={i: i for i in range(2 * n)},
        compiler_params=pltpu.CompilerParams(has_side_effects=EFFECT),
    )(*srcs, *lands, send_sems, recv_sems, after)
    return out[:n], out[n:]


def _exchange_copies(g_refs, got_refs, send_sems, recv_sems):
    x, y, c, _ = _place()
    sends, arrivals = [], []
    for a, (g, got) in enumerate(zip(g_refs, got_refs)):
        hr = g.shape[1] // 2
        cp = pltpu.make_async_remote_copy(
            src_ref=g.at[:, pl.ds((1 - c) * hr, hr), :], dst_ref=got, send_sem=send_sems.at[a],
            recv_sem=recv_sems.at[a], device_id=(x, y, 1 - c), device_id_type=MESH)
        sends.append(cp)
        arrivals.append(cp)
    return sends, arrivals


def _scatter_copies(s_refs, got_refs, send_sems, recv_sems):
    x, y, c, chips = _place()
    sends, arrivals = [], []
    for a, (s16, got) in enumerate(zip(s_refs, got_refs)):
        for j, chip in enumerate(chips):
            cp = pltpu.make_async_remote_copy(
                src_ref=s16.at[2 * chip[0] + chip[1]], dst_ref=got.at[j], send_sem=send_sems.at[3 * a + j],
                recv_sem=recv_sems.at[3 * a + j], device_id=(*chip, c), device_id_type=MESH)
            sends.append(cp)
            arrivals.append(cp)
    return sends, arrivals


def _share_halves(name, bufs, dep):
    n = len(bufs)

    def body(*refs):
        out_refs = refs[n + 1:2 * n + 1]
        send_sems, recv_sems = refs[2 * n + 1:]
        x, y, c, _ = _place()
        remote = [pltpu.make_async_remote_copy(
            src_ref=out_refs[a].at[c], dst_ref=out_refs[a].at[c], send_sem=send_sems.at[a], recv_sem=recv_sems.at[a],
            device_id=(x, y, 1 - c), device_id_type=MESH) for a in range(n)]
        for cp in remote:
            cp.start()
        for a in range(n):
            pltpu.make_async_remote_copy(
                src_ref=out_refs[a].at[c], dst_ref=out_refs[a].at[1 - c], send_sem=send_sems.at[a],
                recv_sem=recv_sems.at[a], device_id=(x, y, 1 - c), device_id_type=MESH).wait_recv()
        for cp in remote:
            cp.wait_send()

    return pl.pallas_call(
        body, name=name, in_specs=[ANY] * (n + 1), out_specs=[ANY] * n,
        out_shape=[jax.ShapeDtypeStruct(b.shape, b.dtype) for b in bufs],
        input_output_aliases={a: a for a in range(n)},
        scratch_shapes=[pltpu.SemaphoreType.DMA((n,)), pltpu.SemaphoreType.DMA((n,))],
    )(*bufs, dep)


SMALL_ROWS, SMALL_COLS = 16, CONV_CH


def _allreduce_small(block):
    m_per, ncol = block.shape

    def body(x_ref, sum_ref, all_ref, send_sems, recv_sems, local_sem):
        x, y, c, chips = _place()
        me, sibling = (x, y, c), (x, y, 1 - c)

        def rows(px, py, pc):
            return all_ref.at[pl.ds((4 * px + 2 * py + pc) * m_per, m_per), :]

        def copy(k, block_of, to, src=None):
            return pltpu.make_async_remote_copy(
                src_ref=rows(*block_of) if src is None else src, dst_ref=rows(*block_of),
                send_sem=send_sems.at[k], recv_sem=recv_sems.at[k], device_id=to, device_id_type=MESH)

        mine = pltpu.make_async_copy(x_ref, rows(*me), local_sem)
        mine.start()
        first = [copy(0, me, sibling, src=x_ref)]
        first += [copy(1 + j, me, (*chip, c), src=x_ref) for j, chip in enumerate(chips)]
        for cp in first:
            cp.start()
        passed = [copy(4 + j, (*chip, c), sibling) for j, chip in enumerate(chips)]
        for j, chip in enumerate(chips):
            copy(1 + j, (*chip, c), me).wait_recv()
            passed[j].start()
        copy(0, sibling, me).wait_recv()
        for j, chip in enumerate(chips):
            copy(4 + j, (*chip, 1 - c), me).wait_recv()
        for cp in first + passed:
            cp.wait_send()
        mine.wait()
        total = all_ref[0:m_per, :]
        for d in range(1, N_DEV):
            total = total + all_ref[d * m_per:(d + 1) * m_per, :]
        sum_ref[...] = total

    vm = pl.BlockSpec(memory_space=pltpu.VMEM)
    return pl.pallas_call(
        body, name="allreduce_small", in_specs=[vm], out_specs=vm,
        out_shape=jax.ShapeDtypeStruct((m_per, ncol), F32),
        scratch_shapes=[pltpu.VMEM((N_DEV * m_per, ncol), F32), pltpu.SemaphoreType.DMA((7,)),
                        pltpu.SemaphoreType.DMA((7,)), pltpu.SemaphoreType.DMA],
    )(block)


def _row_tile(rows, cols):
    tile = rows
    while tile * cols * 4 > (1 << 20) and tile % 16 == 0:
        tile //= 2
    return tile


def _elementwise(fn, name, ins, out_dtypes, dep):
    rows, cols = ins[0].shape
    tile = _row_tile(rows, cols)

    def body(*refs):
        outs = fn(*[r[...] for r in refs[:len(ins)]])
        for o_ref, o in zip(refs[len(ins) + 1:], outs):
            o_ref[...] = o.astype(o_ref.dtype)

    if tile * cols * 4 > (1 << 21) and cols % 512 == 0:
        spec = pl.BlockSpec((rows, 256), lambda i: (0, i))
        steps = cols // 256
    else:
        spec = pl.BlockSpec((tile, cols), lambda i: (i, 0))
        steps = rows // tile
    return pl.pallas_call(
        body, name=name, grid=(steps,), in_specs=[spec] * len(ins) + [pl.BlockSpec(memory_space=pl.ANY)],
        out_specs=[spec] * len(out_dtypes),
        out_shape=[jax.ShapeDtypeStruct((rows, cols), d) for d in out_dtypes],
        compiler_params=_params("parallel"),
    )(*ins, dep)


def _adamw_tile(w, g, m, v):
    m = ADAM_B1 * m + (1.0 - ADAM_B1) * g
    v = ADAM_B2 * v + (1.0 - ADAM_B2) * jnp.square(g)
    m_hat = m / (1.0 - ADAM_B1 ** ADAM_STEP)
    v_hat = v / (1.0 - ADAM_B2 ** ADAM_STEP)
    delta = -ADAM_LR * (m_hat / (jnp.sqrt(v_hat) + ADAM_EPS) + ADAM_WD * w)
    return delta, m, v


def _adamw(name, w, g, m, v, dep):
    return _elementwise(_adamw_tile, name, [w, g, m, v], [F32, F32, F32], dep)


def _chip_sum(name, g4, got, place):
    nchip, hr, cols = got.shape
    tile = _row_tile(hr, cols)
    nblk = hr // tile

    def body(pl_ref, g_ref, o_ref, s32_ref, s16_ref):
        s = g_ref[...] + o_ref[...]
        s32_ref[...] = s
        s16_ref[...] = s.astype(BF16)

    spec = pl.BlockSpec((None, tile, cols), lambda k, i, pr: (k, i, 0))
    return pl.pallas_call(
        body, name=name,
        grid_spec=pltpu.PrefetchScalarGridSpec(
            num_scalar_prefetch=1, grid=(nchip, nblk),
            in_specs=[pl.BlockSpec((None, tile, cols), lambda k, i, pr: (k, pr[1] * nblk + i, 0)), spec],
            out_specs=[spec, spec]),
        out_shape=[jax.ShapeDtypeStruct(got.shape, F32), jax.ShapeDtypeStruct(got.shape, BF16)],
        compiler_params=_params("parallel", "parallel"),
    )(place, g4, got)


def _mesh_sum(name, s32, got, place):
    _, hr, cols = s32.shape
    tile = _row_tile(hr, cols)

    def body(pl_ref, own_ref, g0_ref, g1_ref, g2_ref, o_ref):
        o_ref[...] = ((own_ref[...] + g0_ref[...].astype(F32)) + g1_ref[...].astype(F32)) + g2_ref[...].astype(F32)

    slab = lambda j: pl.BlockSpec((None, tile, cols), lambda i, pr: (j, i, 0))
    return pl.pallas_call(
        body, name=name,
        grid_spec=pltpu.PrefetchScalarGridSpec(
            num_scalar_prefetch=1, grid=(hr // tile,),
            in_specs=[pl.BlockSpec((None, tile, cols), lambda i, pr: (pr[0], i, 0)), slab(0), slab(1), slab(2)],
            out_specs=pl.BlockSpec((None, tile, cols), lambda i, pr: (pr[1], i, 0))),
        out_shape=jax.ShapeDtypeStruct((2, hr, cols), F32),
        compiler_params=_params("parallel"),
    )(place, s32, got, got, got)


def _place_operand():
    return jnp.stack([2 * lax.axis_index("x") + lax.axis_index("y"), lax.axis_index("c")]).astype(jnp.int32)


W_IN_ROWS = 720


def _per_chip(name, g):
    if name == "w_in_t":
        slabs = g.reshape(N_CHIPS, D_IN // N_CHIPS, D_MODEL)
        return jnp.pad(slabs, ((0, 0), (0, W_IN_ROWS - D_IN // N_CHIPS), (0, 0)))
    if name == "w_proj":
        return jnp.transpose(g.reshape(PLE_DIM, N_CHIPS, D_MODEL // N_CHIPS), (1, 0, 2))
    if name == "w_up4":
        return g
    return g.reshape(N_CHIPS, g.shape[0] // N_CHIPS, g.shape[1])


class _EarlyReducer:
    def __init__(self, tag):
        self.tag = tag

    def start(self, grads):
        self.names = list(grads)
        self.place = _place_operand()
        slabs = [_per_chip(k, grads[k]) for k in self.names]
        halves = [jax.ShapeDtypeStruct((s.shape[0], s.shape[1] // 2, s.shape[2]), F32) for s in slabs]
        self.a = _copies_start(self.tag + "exchange_start", _exchange_copies, len(slabs), slabs, halves,
                               slabs[0][0, :8, :128])
        return self.a[-1]

    def middle(self, after):
        slabs, got = _copies_wait(self.tag + "exchange_wait", _exchange_copies, self.a, after)
        self.sums = [_chip_sum(self.tag + "chip_sum_" + k, s, g, self.place) for k, s, g in zip(self.names, slabs, got)]
        s16 = [s[1] for s in self.sums]
        lands = [jax.ShapeDtypeStruct((3,) + s.shape[1:], BF16) for s in s16]
        self.b = _copies_start(self.tag + "scatter_start", _scatter_copies, 3 * len(s16), s16, lands,
                               self.sums[0][0][0, :8, :128])
        return self.b[-1]

    def finish(self, after):
        _, got = _copies_wait(self.tag + "scatter_wait", _scatter_copies, self.b, after)
        self.bufs = {k: _mesh_sum(self.tag + "mesh_sum_" + k, s[0], g, self.place)
                     for k, s, g in zip(self.names, self.sums, got)}


def kernel(x, p, norm_mix, w_in, conv_w, a_log, dt_bias, dn_norm, sinks, w_o, norm_mlp, w_up, w_down, norm_ple, w_ple_gate, w_ple_proj, norm_final, loss_target, m_norm_mix, m_w_in, m_conv_w, m_a_log, m_dt_bias, m_dn_norm, m_sinks, m_w_o, m_norm_mlp, m_w_up, m_w_down, m_norm_ple, m_w_ple_gate, m_w_ple_proj, m_norm_final, v_norm_mix, v_w_in, v_conv_w, v_a_log, v_dt_bias, v_dn_norm, v_sinks, v_w_o, v_norm_mlp, v_w_up, v_w_down, v_norm_ple, v_w_ple_gate, v_w_ple_proj, v_norm_final):
    chip = 2 * lax.axis_index("x") + lax.axis_index("y")
    big = dict(w_in=w_in[0], w_o=w_o[0], w_up=w_up[0], w_down=w_down[0], w_gate=w_ple_gate[0], w_proj=w_ple_proj[0])
    big_m = dict(w_in=m_w_in[0], w_o=m_w_o[0], w_up=m_w_up[0], w_down=m_w_down[0], w_gate=m_w_ple_gate[0], w_proj=m_w_ple_proj[0])
    big_v = dict(w_in=v_w_in[0], w_o=v_w_o[0], w_up=v_w_up[0], w_down=v_w_down[0], w_gate=v_w_ple_gate[0], w_proj=v_w_ple_proj[0])
    names = list(big)

    w_in_all, conv_all = _gather_weights([big["w_in"].astype(BF16)], conv_w[0])
    late_names = names[1:]
    late_shards = [big[k].astype(BF16) for k in late_names]
    gather = _copies_start("gather_start", _late_copies, LATE_COPIES * len(late_shards), late_shards,
                           [jax.ShapeDtypeStruct((N_CHIPS,) + s.shape, BF16) for s in late_shards], w_in_all)
    token = gather[-1]
    w = dict(w_in=jnp.transpose(w_in_all, (1, 0, 2)).reshape(D_MODEL, D_IN),
             conv_w=jnp.transpose(conv_all, (1, 0, 2)).reshape(DN_CONV, CONV_CH))
    sm = dict(norm_mix=norm_mix[0] + token[0, 0], a_log=a_log[0], dt_bias=dt_bias[0], dn_norm=dn_norm[0],
              sinks=sinks[0], norm_mlp=norm_mlp[0], norm_ple=norm_ple[0], norm_final=norm_final)

    def late(after):
        first = functools.partial(_late_copies, only=(0,))
        srcs, lands = _copies_wait("gather_wait_o", first, gather, after)

        def rest(after2):
            others = functools.partial(_late_copies, only=tuple(range(1, len(late_names))))
            gw = dict(zip(late_names, _copies_wait("gather_wait_rest", others, gather[:2] + (srcs, lands, None), after2)[1]))
            return dict(w_up4=gw["w_up"], w_down=gw["w_down"].reshape(D_FF, D_MODEL),
                        w_gate=gw["w_gate"].reshape(D_MODEL, D_MODEL), w_proj4=gw["w_proj"])

        return lands[0].reshape(D_MODEL, D_MODEL), rest

    reducer = _EarlyReducer("early_")
    sums, grad_x, g = _local_step(x[0], p[0, 0], loss_target[0], sm, w, late, reducer)

    last = _EarlyReducer("last_")
    dep_a = last.start({"w_in_t": g["w_in_t"]})

    row = lambda v: jnp.zeros((SMALL_COLS,), F32).at[:v.shape[0]].set(v)
    misc = jnp.zeros((SMALL_COLS,), F32).at[0:4].set(sums["a_log"]).at[4:8].set(sums["dt_bias"]) \
        .at[8:16].set(sums["sinks"]).at[128:256].set(sums["dn_norm"]).at[256].set(sums["loss"])
    small = jnp.concatenate([sums["conv_w"], jnp.stack([row(sums["norm_mix"]), row(sums["norm_mlp"]), row(sums["norm_ple"]),
                                                        row(sums["norm_final"]), misc]),
                             jnp.zeros((SMALL_ROWS - 9, SMALL_COLS), F32)], axis=0)
    tot = _allreduce_small(small + dep_a[0, 0])
    dep_b = last.middle(tot)
    grad_key = dict(w_o="w_o", w_up="w_up4", w_down="w_down", w_gate="w_gate", w_proj="w_proj")
    full = _share_halves("share_halves", [reducer.bufs[grad_key[k]] for k in late_names], dep_b)
    red = {k: f.reshape(-1, f.shape[-1]) for k, f in zip(late_names, full)}
    loss = tot[8, 256]
    ncw = CONV_CH // N_CHIPS

    def pack(cw, nmix, nmlp, nple, nfin, al, dtb, sk, dnn):
        misc_p = jnp.zeros((SMALL_COLS,), F32).at[0:4].set(al).at[4:8].set(dtb).at[8:16].set(sk).at[128:256].set(dnn)
        cw_p = jnp.zeros((DN_CONV, SMALL_COLS), F32).at[:, :ncw].set(cw)
        return jnp.concatenate([cw_p, jnp.stack([row(nmix), row(nmlp), row(nple), row(nfin), misc_p]),
                                jnp.zeros((SMALL_ROWS - 9, SMALL_COLS), F32)], axis=0)

    def unpack(buf):
        return dict(conv_w=buf[0:4, :ncw][None], norm_mix=buf[4, :D_MODEL][None], norm_mlp=buf[5, :D_MODEL][None],
                    norm_ple=buf[6, :D_MODEL][None], norm_final=buf[7, :D_MODEL], a_log=buf[8, 0:4][None],
                    dt_bias=buf[8, 4:8][None], sinks=buf[8, 8:16][None], dn_norm=buf[8, 128:256][None])

    g_conv_shard = lax.dynamic_slice(tot[0:4], (0, chip * ncw), (DN_CONV, ncw))
    g_small = pack(g_conv_shard, tot[4, :D_MODEL], tot[5, :D_MODEL], tot[6, :D_MODEL], tot[7, :D_MODEL],
                   tot[8, 0:4], tot[8, 4:8], tot[8, 8:16], tot[8, 128:256])
    w_small = pack(conv_w[0], norm_mix[0], norm_mlp[0], norm_ple[0], norm_final, a_log[0], dt_bias[0], sinks[0], dn_norm[0])
    m_small = pack(m_conv_w[0], m_norm_mix[0], m_norm_mlp[0], m_norm_ple[0], m_norm_final, m_a_log[0], m_dt_bias[0],
                   m_sinks[0], m_dn_norm[0])
    v_small = pack(v_conv_w[0], v_norm_mix[0], v_norm_mlp[0], v_norm_ple[0], v_norm_final, v_a_log[0], v_dt_bias[0],
                   v_sinks[0], v_dn_norm[0])

    ref_name = dict(w_in="w_in", w_o="w_o", w_up="w_up", w_down="w_down", w_gate="w_ple_gate", w_proj="w_ple_proj")
    out_g, out_d, out_m, out_v = {}, {}, {}, {}

    def update(k, dep):
        d_k, m_k, v_k = _adamw("adamw_" + k, big[k], red[k], big_m[k], big_v[k], dep)
        out_g[ref_name[k]], out_d[ref_name[k]] = red[k][None], d_k[None]
        out_m[ref_name[k]], out_v[ref_name[k]] = m_k[None], v_k[None]
        return d_k

    for k in late_names:
        done = update(k, dep_b)
    small_out = _adamw("adamw_small", w_small, g_small, m_small, v_small, dep_b)
    d_s, m_s, v_s = (unpack(b) for b in small_out)
    g_s = unpack(g_small)
    for src, dst in ((g_s, out_g), (d_s, out_d), (m_s, out_m), (v_s, out_v)):
        dst.update(src)
    last.finish(done + small_out[0][0:1, 0:1])
    (w_in_full,) = _share_halves("share_halves_w_in", [last.bufs["w_in_t"]], dep_b)
    g_t = w_in_full.reshape(W_IN_ROWS, D_MODEL)[:D_IN // N_CHIPS]
    d_t, m_t, v_t = _adamw("adamw_w_in", big["w_in"].T, g_t, big_m["w_in"].T, big_v["w_in"].T, dep_b)
    out_g["w_in"], out_d["w_in"], out_m["w_in"], out_v["w_in"] = g_t.T[None], d_t.T[None], m_t.T[None], v_t.T[None]
    order = ["norm_mix", "w_in", "conv_w", "a_log", "dt_bias", "dn_norm", "sinks", "w_o", "norm_mlp", "w_up", "w_down",
             "norm_ple", "w_ple_gate", "w_ple_proj", "norm_final"]
    return (loss, grad_x[None], *[out_g[k] for k in order], *[out_d[k] for k in order],
            *[out_m[k] for k in order], *[out_v[k] for k in order])
```

```python
import functools

import jax
import jax.numpy as jnp
from jax import lax
from jax.experimental import pallas as pl
from jax.experimental.pallas import tpu as pltpu

F32 = jnp.float32
BF16 = jnp.bfloat16
MXU_DTYPE = jnp.bfloat16
HI = lax.Precision.HIGHEST

D_MODEL = 1024
PLE_DIM = 256
ATTN_HEADS = 8
ATTN_KV_HEADS = 2
ATTN_GROUPS = ATTN_HEADS // ATTN_KV_HEADS
ATTN_HEAD_DIM = 64
ATTN_BLOCK = 128
ROPE_THETA = 10000.0
DN_HEADS = 4
DN_HEAD_DIM = 128
DN_CONV = 4
DN_CHUNK = 64
D_FF = 4 * D_MODEL
EPS = 1e-6
ATTN_Q = ATTN_HEADS * ATTN_HEAD_DIM
ATTN_KV = ATTN_KV_HEADS * ATTN_HEAD_DIM
DN_W = DN_HEADS * DN_HEAD_DIM
CONV_CH = 3 * DN_W
D_IN = ATTN_Q + 2 * ATTN_KV + 4 * DN_W + 2 * DN_HEADS
DN_COLS = 4 * DN_W + 128
DN_SCALE = DN_HEAD_DIM ** -0.5
ATTN_SCALE = ATTN_HEAD_DIM ** -0.5
FF_BLOCKS = 4
FF_BLOCK = D_FF // FF_BLOCKS

ADAM_LR = 0.001
ADAM_B1 = 0.9
ADAM_B2 = 0.999
ADAM_EPS = 1e-08
ADAM_WD = 0.01
ADAM_STEP = 10

V7X_VMEM_BYTES = 64 * 1024 * 1024
VMEM_LIMIT = 48 * 1024 * 1024

NN = ((1,), (0,))
NT = ((1,), (1,))
TN = ((0,), (0,))


def _dot(a, b, dims=NN, prec=None):
    if a.dtype != b.dtype:
        a, b = a.astype(MXU_DTYPE), b.astype(MXU_DTYPE)
    return lax.dot_general(a, b, (dims, ((), ())), precision=prec, preferred_element_type=F32)


def _sigmoid(x):
    return 1.0 / (1.0 + jnp.exp(-x))


def _softplus(x):
    return jnp.maximum(x, 0.0) + jnp.log(1.0 + jnp.exp(-jnp.abs(x)))


def _params(*sem):
    return pltpu.CompilerParams(dimension_semantics=sem, vmem_limit_bytes=VMEM_LIMIT)


def _rms_fwd(xv, g):
    r = lax.rsqrt(jnp.mean(xv * xv, axis=-1, keepdims=True) + EPS)
    return xv * r * g


def _rms_bwd(xv, g, dn):
    r = lax.rsqrt(jnp.mean(xv * xv, axis=-1, keepdims=True) + EPS)
    xh = xv * r
    dg = jnp.sum(dn * xh, axis=0, keepdims=True)
    dxh = dn * g
    dx = r * (dxh - xh * jnp.mean(dxh * xh, axis=-1, keepdims=True))
    return dx, dg


def _full(shape):
    return pl.BlockSpec(shape, lambda *_: (0,) * len(shape))


def _inproj(x, g_mix, wa, wd, tm):
    t = x.shape[0]

    def body(x_ref, g_ref, wa_ref, wd_ref, u_ref, pa_ref, pd_ref):
        u = _rms_fwd(x_ref[...], g_ref[...]).astype(MXU_DTYPE)
        u_ref[...] = u
        pa_ref[...] = _dot(u, wa_ref[...])
        pd_ref[...] = _dot(u, wd_ref[...])

    na, nd = wa.shape[1], wd.shape[1]
    return pl.pallas_call(
        body, name="inproj", grid=(t // tm,),
        in_specs=[pl.BlockSpec((tm, D_MODEL), lambda i: (i, 0)), _full((1, D_MODEL)),
                  _full((D_MODEL, na)), _full((D_MODEL, nd))],
        out_specs=[pl.BlockSpec((tm, D_MODEL), lambda i: (i, 0)), pl.BlockSpec((tm, na), lambda i: (i, 0)),
                   pl.BlockSpec((tm, nd), lambda i: (i, 0))],
        out_shape=[jax.ShapeDtypeStruct((t, D_MODEL), MXU_DTYPE), jax.ShapeDtypeStruct((t, na), F32),
                   jax.ShapeDtypeStruct((t, nd), F32)],
        compiler_params=_params("parallel"),
    )(x, g_mix, wa, wd)


def _oproj(x, ao, dn, wo_a, wo_d, tm):
    t = x.shape[0]

    def body(x_ref, ao_ref, dn_ref, wa_ref, wd_ref, h_ref):
        h_ref[...] = (x_ref[...] + _dot(ao_ref[...].astype(MXU_DTYPE), wa_ref[...])
                      + _dot(dn_ref[...].astype(MXU_DTYPE), wd_ref[...]))

    half = ao.shape[1]
    return pl.pallas_call(
        body, name="oproj", grid=(t // tm,),
        in_specs=[pl.BlockSpec((tm, D_MODEL), lambda i: (i, 0)), pl.BlockSpec((tm, half), lambda i: (i, 0)),
                  pl.BlockSpec((tm, half), lambda i: (i, 0)), _full((half, D_MODEL)), _full((half, D_MODEL))],
        out_specs=pl.BlockSpec((tm, D_MODEL), lambda i: (i, 0)),
        out_shape=jax.ShapeDtypeStruct((t, D_MODEL), F32),
        compiler_params=_params("parallel"),
    )(x, ao, dn, wo_a, wo_d)


def _mlp_fwd(h1, g_mlp, w_up4, w_down, tm):
    t = h1.shape[0]

    def body(h_ref, g_ref, wu_ref, wd_ref, m_ref, r_ref, h2_ref, acc_ref):
        k = pl.program_id(1)

        @pl.when(k == 0)
        def _():
            m_ref[...] = _rms_fwd(h_ref[...], g_ref[...]).astype(MXU_DTYPE)
            acc_ref[...] = jnp.zeros_like(acc_ref)

        r = jnp.maximum(_dot(m_ref[...], wu_ref[...]), 0.0)
        r_ref[...] = r.astype(MXU_DTYPE)
        s = jnp.square(r).astype(MXU_DTYPE)
        acc_ref[...] += _dot(s, wd_ref[...])

        @pl.when(k == FF_BLOCKS - 1)
        def _():
            h2_ref[...] = h_ref[...] + acc_ref[...]

    return pl.pallas_call(
        body, name="mlp_fwd", grid=(t // tm, FF_BLOCKS),
        in_specs=[pl.BlockSpec((tm, D_MODEL), lambda i, k: (i, 0)), _full((1, D_MODEL)),
                  pl.BlockSpec((None, D_MODEL, FF_BLOCK), lambda i, k: (k, 0, 0)),
                  pl.BlockSpec((FF_BLOCK, D_MODEL), lambda i, k: (k, 0))],
        out_specs=[pl.BlockSpec((tm, D_MODEL), lambda i, k: (i, 0)), pl.BlockSpec((tm, FF_BLOCK), lambda i, k: (i, k)),
                   pl.BlockSpec((tm, D_MODEL), lambda i, k: (i, 0))],
        out_shape=[jax.ShapeDtypeStruct((t, D_MODEL), MXU_DTYPE), jax.ShapeDtypeStruct((t, D_FF), MXU_DTYPE),
                   jax.ShapeDtypeStruct((t, D_MODEL), F32)],
        scratch_shapes=[pltpu.VMEM((tm, D_MODEL), F32)],
        compiler_params=_params("parallel", "arbitrary"),
    )(h1, g_mlp, w_up4, w_down)


def _ple_loss(h2, p, tgt, g_ple, g_fin, w_gate, w_proj, tm):
    t = h2.shape[0]

    def body(h_ref, p_ref, t_ref, gp_ref, gf_ref, wg_ref, wp_ref,
             dh_ref, dhb_ref, dgp_ref, dpp_ref, n3_ref, pb_ref, acc_ref):
        @pl.when(pl.program_id(0) == 0)
        def _():
            acc_ref[...] = jnp.zeros_like(acc_ref)

        h = h_ref[...]
        g_ple_v, g_fin_v = gp_ref[...], gf_ref[...]
        n3 = _rms_fwd(h, g_ple_v).astype(MXU_DTYPE)
        n3_ref[...] = n3
        gate = _sigmoid(_dot(n3, wg_ref[...]))
        pb = p_ref[...].astype(MXU_DTYPE)
        pb_ref[...] = pb
        pp = _dot(pb, wp_ref[...])
        h3 = h + gate * pp
        r4 = lax.rsqrt(jnp.mean(h3 * h3, axis=-1, keepdims=True) + EPS)
        xh4 = h3 * r4
        e = xh4 * g_fin_v - t_ref[...]
        loss = 0.5 * jnp.sum(jnp.mean(e * e, axis=-1, keepdims=True), axis=0, keepdims=True)
        dy = e * (1.0 / D_MODEL)
        dg_fin = jnp.sum(dy * xh4, axis=0, keepdims=True)
        dxh = dy * g_fin_v
        dh3 = r4 * (dxh - xh4 * jnp.mean(dxh * xh4, axis=-1, keepdims=True))
        dpp_ref[...] = (dh3 * gate).astype(MXU_DTYPE)
        dgp = (dh3 * pp * gate * (1.0 - gate)).astype(MXU_DTYPE)
        dgp_ref[...] = dgp
        dn3 = _dot(dgp, wg_ref[...], NT)
        dx, dg_ple = _rms_bwd(h, g_ple_v, dn3)
        dh2 = dh3 + dx
        dh_ref[...] = dh2
        dhb_ref[...] = dh2.astype(MXU_DTYPE)
        acc_ref[0:1, :] += dg_fin
        acc_ref[1:2, :] += dg_ple
        acc_ref[2:3, :] += jnp.broadcast_to(loss, (1, D_MODEL))

    row = lambda w: pl.BlockSpec((tm, w), lambda i: (i, 0))
    return pl.pallas_call(
        body, name="ple_loss", grid=(t // tm,),
        in_specs=[row(D_MODEL), row(PLE_DIM), row(D_MODEL), _full((1, D_MODEL)), _full((1, D_MODEL)),
                  _full((D_MODEL, D_MODEL)), _full((PLE_DIM, D_MODEL))],
        out_specs=[row(D_MODEL), row(D_MODEL), row(D_MODEL), row(D_MODEL), row(D_MODEL), row(PLE_DIM),
                   _full((8, D_MODEL))],
        out_shape=[jax.ShapeDtypeStruct((t, D_MODEL), F32), jax.ShapeDtypeStruct((t, D_MODEL), MXU_DTYPE),
                   jax.ShapeDtypeStruct((t, D_MODEL), MXU_DTYPE), jax.ShapeDtypeStruct((t, D_MODEL), MXU_DTYPE),
                   jax.ShapeDtypeStruct((t, D_MODEL), MXU_DTYPE), jax.ShapeDtypeStruct((t, PLE_DIM), MXU_DTYPE),
                   jax.ShapeDtypeStruct((8, D_MODEL), F32)],
        compiler_params=_params("arbitrary"),
    )(h2, p, tgt, g_ple, g_fin, w_gate, w_proj)


def _mlp_bwd(dh2, dh2b, r, h1, g_mlp, w_up4, w_down, tm):
    t = h1.shape[0]

    def body(dh_ref, dhb_ref, r_ref, h_ref, g_ref, wu_ref, wd_ref,
             da_ref, dh1_ref, dh1b_ref, acc_ref, dm_ref):
        i, k = pl.program_id(0), pl.program_id(1)

        @pl.when((i == 0) & (k == 0))
        def _():
            acc_ref[...] = jnp.zeros_like(acc_ref)

        @pl.when(k == 0)
        def _():
            dm_ref[...] = jnp.zeros_like(dm_ref)

        ds = _dot(dhb_ref[...], wd_ref[...], NT)
        da = (ds * (2.0 * r_ref[...].astype(F32))).astype(MXU_DTYPE)
        da_ref[...] = da
        dm_ref[...] += _dot(da, wu_ref[...], NT)

        @pl.when(k == FF_BLOCKS - 1)
        def _():
            dx, dg = _rms_bwd(h_ref[...], g_ref[...], dm_ref[...])
            dh1 = dh_ref[...] + dx
            dh1_ref[...] = dh1
            dh1b_ref[...] = dh1.astype(MXU_DTYPE)
            acc_ref[0:1, :] += dg

    tok = lambda w: pl.BlockSpec((tm, w), lambda i, k: (i, 0))
    return pl.pallas_call(
        body, name="mlp_bwd", grid=(t // tm, FF_BLOCKS),
        in_specs=[tok(D_MODEL), tok(D_MODEL), pl.BlockSpec((tm, FF_BLOCK), lambda i, k: (i, k)), tok(D_MODEL),
                  _full((1, D_MODEL)), pl.BlockSpec((None, D_MODEL, FF_BLOCK), lambda i, k: (k, 0, 0)),
                  pl.BlockSpec((FF_BLOCK, D_MODEL), lambda i, k: (k, 0))],
        out_specs=[pl.BlockSpec((tm, FF_BLOCK), lambda i, k: (i, k)),
                   tok(D_MODEL), tok(D_MODEL), pl.BlockSpec((8, D_MODEL), lambda i, k: (0, 0))],
        out_shape=[jax.ShapeDtypeStruct((t, D_FF), MXU_DTYPE),
                   jax.ShapeDtypeStruct((t, D_MODEL), F32), jax.ShapeDtypeStruct((t, D_MODEL), MXU_DTYPE),
                   jax.ShapeDtypeStruct((8, D_MODEL), F32)],
        scratch_shapes=[pltpu.VMEM((tm, D_MODEL), F32)],
        compiler_params=_params("arbitrary", "arbitrary"),
    )(dh2, dh2b, r, h1, g_mlp, w_up4, w_down)


def _oproj_bwd(dh1b, wo_a, wo_d, tm, dep):
    t = dh1b.shape[0]
    half = wo_a.shape[0]

    def body(d_ref, wa_ref, wd_ref, dep_ref, da_ref, dd_ref):
        d = d_ref[...]
        da_ref[...] = _dot(d, wa_ref[...], NT)
        dd_ref[...] = _dot(d, wd_ref[...], NT)

    return pl.pallas_call(
        body, name="oproj_bwd", grid=(t // tm,),
        in_specs=[pl.BlockSpec((tm, D_MODEL), lambda i: (i, 0)), _full((half, D_MODEL)), _full((half, D_MODEL)),
                  pl.BlockSpec(memory_space=pl.ANY)],
        out_specs=[pl.BlockSpec((tm, half), lambda i: (i, 0)), pl.BlockSpec((tm, half), lambda i: (i, 0))],
        out_shape=[jax.ShapeDtypeStruct((t, half), F32), jax.ShapeDtypeStruct((t, half), F32)],
        compiler_params=_params("parallel"),
    )(dh1b, wo_a, wo_d, dep)


def _inproj_bwd(x, dh1, g_mix, grads, weights, tm):
    t = x.shape[0]
    n = len(grads)

    def body(*refs):
        x_ref, dh_ref, g_ref = refs[:3]
        g_refs, w_refs = refs[3:3 + n], refs[3 + n:3 + 2 * n]
        dx_ref, acc_ref = refs[3 + 2 * n:]

        @pl.when(pl.program_id(0) == 0)
        def _():
            acc_ref[...] = jnp.zeros_like(acc_ref)

        du = _dot(g_refs[0][...], w_refs[0][...], NT)
        for j in range(1, n):
            du += _dot(g_refs[j][...], w_refs[j][...], NT)
        dx, dg = _rms_bwd(x_ref[...], g_ref[...], du)
        dx_ref[...] = dh_ref[...] + dx
        acc_ref[0:1, :] += dg

    tok = lambda w: pl.BlockSpec((tm, w), lambda i: (i, 0))
    return pl.pallas_call(
        body, name="inproj_bwd", grid=(t // tm,),
        in_specs=[tok(D_MODEL), tok(D_MODEL), _full((1, D_MODEL))] + [tok(g.shape[1]) for g in grads]
                 + [_full(w.shape) for w in weights],
        out_specs=[tok(D_MODEL), _full((8, D_MODEL))],
        out_shape=[jax.ShapeDtypeStruct((t, D_MODEL), F32), jax.ShapeDtypeStruct((8, D_MODEL), F32)],
        compiler_params=_params("arbitrary"),
    )(x, dh1, g_mix, *grads, *weights)


def _wgrad(a, b, name, tk, tn, tt, stacked=False, prep=None):
    t, kdim = a.shape
    ncols = b.shape[1]

    def body(a_ref, b_ref, o_ref):
        @pl.when(pl.program_id(2) == 0)
        def _():
            o_ref[...] = jnp.zeros_like(o_ref)

        av = a_ref[...] if prep is None else prep(a_ref[...])
        o_ref[...] += _dot(av, b_ref[...], TN)

    if stacked:
        out_spec = pl.BlockSpec((None, tk, tn), lambda i, j, s: (j, i, 0))
        out_shape = jax.ShapeDtypeStruct((ncols // tn, kdim, tn), F32)
    else:
        out_spec = pl.BlockSpec((tk, tn), lambda i, j, s: (i, j))
        out_shape = jax.ShapeDtypeStruct((kdim, ncols), F32)
    return pl.pallas_call(
        body, name=name, grid=(kdim // tk, ncols // tn, t // tt),
        in_specs=[pl.BlockSpec((tt, tk), lambda i, j, s: (s, i)), pl.BlockSpec((tt, tn), lambda i, j, s: (s, j))],
        out_specs=out_spec, out_shape=out_shape,
        compiler_params=_params("parallel", "parallel", "arbitrary"),
    )(a, b)


def _wgrad_cat(as_, bs, name, tt):
    t = as_[0].shape[0]
    heights = [a.shape[1] for a in as_]
    widths = [b.shape[1] for b in bs]

    def body(*refs):
        a_refs, b_refs, o_ref = refs[:len(as_)], refs[len(as_):-1], refs[-1]

        @pl.when(pl.program_id(0) == 0)
        def _():
            o_ref[...] = jnp.zeros_like(o_ref)

        row = 0
        for a_ref, k in zip(a_refs, heights):
            av = a_ref[...]
            col = 0
            for b_ref, n in zip(b_refs, widths):
                o_ref[row:row + k, col:col + n] += _dot(av, b_ref[...], TN)
                col += n
            row += k

    tok = lambda w: pl.BlockSpec((tt, w), lambda s: (s, 0))
    shape = (sum(heights), sum(widths))
    return pl.pallas_call(
        body, name=name, grid=(t // tt,),
        in_specs=[tok(k) for k in heights] + [tok(n) for n in widths],
        out_specs=_full(shape), out_shape=jax.ShapeDtypeStruct(shape, F32),
        compiler_params=_params("arbitrary"),
    )(*as_, *bs)


def _rope_tables(t):
    half = ATTN_HEAD_DIM // 2
    inv = 1.0 / (ROPE_THETA ** (jnp.arange(half, dtype=F32) * (2.0 / ATTN_HEAD_DIM)))
    ang = jnp.arange(t, dtype=F32)[:, None] * inv[None, :]
    cos, sin = jnp.cos(ang), jnp.sin(ang)
    cos2 = jnp.concatenate([cos, cos], axis=-1)
    sin2 = jnp.concatenate([-sin, sin], axis=-1)
    return jnp.tile(cos2, (1, 2)), jnp.tile(sin2, (1, 2))


def _swap_halves(tv):
    w = tv.shape[-1]
    lane = lax.broadcasted_iota(jnp.int32, tv.shape, tv.ndim - 1)
    first = (lane % ATTN_HEAD_DIM) < (ATTN_HEAD_DIM // 2)
    return jnp.where(first, pltpu.roll(tv, w - ATTN_HEAD_DIM // 2, tv.ndim - 1),
                     pltpu.roll(tv, ATTN_HEAD_DIM // 2, tv.ndim - 1))


def _rope(tv, cos, sin):
    return tv * cos + _swap_halves(tv) * sin


def _rope_bwd(dv, cos, sin):
    return dv * cos + _swap_halves(dv * sin)


def _attn_valid(first_block):
    c = lax.broadcasted_iota(jnp.int32, (2 * ATTN_BLOCK, ATTN_BLOCK), 0)
    r = lax.broadcasted_iota(jnp.int32, (2 * ATTN_BLOCK, ATTN_BLOCK), 1)
    return (c > r) & (c <= r + ATTN_BLOCK) & ((c >= ATTN_BLOCK) | jnp.logical_not(first_block))


def _attn_probs(st, sink, valid):
    s = jnp.where(valid, st * ATTN_SCALE, -jnp.inf)
    m = jnp.maximum(jnp.max(s, axis=0, keepdims=True), sink)
    e = jnp.where(valid, jnp.exp(s - m), 0.0)
    es = jnp.exp(sink - m)
    inv = 1.0 / (jnp.sum(e, axis=0, keepdims=True) + es)
    return e * inv, es * inv


def _lane_scalar(vec, idx):
    lane = lax.broadcasted_iota(jnp.int32, vec.shape, 1)
    return jnp.sum(jnp.where(lane == idx, vec, 0.0), axis=-1, keepdims=True)


def _attn_specs(nb):
    cur = lambda w, cb: pl.BlockSpec((ATTN_BLOCK, w), lambda i: (jnp.minimum(i, nb - 1), cb))
    prev = lambda w, cb: pl.BlockSpec((ATTN_BLOCK, w), lambda i: (jnp.maximum(jnp.minimum(i, nb - 1) - 1, 0), cb))
    kcol, vcol = ATTN_Q // ATTN_KV, ATTN_Q // ATTN_KV + 1
    return [cur(ATTN_Q, 0), cur(ATTN_KV, kcol), prev(ATTN_KV, kcol), cur(ATTN_KV, vcol), prev(ATTN_KV, vcol),
            cur(ATTN_KV, 0), cur(ATTN_KV, 0), prev(ATTN_KV, 0), prev(ATTN_KV, 0), _full((1, 128))]


def _attn_fwd(pa, cos, sin, sinks_vec):
    t = pa.shape[0]
    nb = t // ATTN_BLOCK

    def body(q_ref, kc_ref, kp_ref, vc_ref, vp_ref, cc_ref, sc_ref, cp_ref, sp_ref, sk_ref, o_ref):
        first = pl.program_id(0) == 0
        cc, sc = cc_ref[...], sc_ref[...]
        q = _rope(q_ref[...], jnp.tile(cc, (1, ATTN_Q // ATTN_KV)), jnp.tile(sc, (1, ATTN_Q // ATTN_KV)))
        kc = _rope(kc_ref[...], cc, sc)
        kp = _rope(kp_ref[...], cp_ref[...], sp_ref[...])
        vc, vp = vc_ref[...], vp_ref[...]
        sk = sk_ref[...]
        valid = _attn_valid(first)
        kv = lambda tp, tc, hk: jnp.concatenate([tp[:, hk * ATTN_HEAD_DIM:(hk + 1) * ATTN_HEAD_DIM],
                                                 tc[:, hk * ATTN_HEAD_DIM:(hk + 1) * ATTN_HEAD_DIM]], axis=0)
        kwins = [kv(kp, kc, hk) for hk in range(ATTN_KV_HEADS)]
        vwins_t = [kv(vp, vc, hk).T for hk in range(ATTN_KV_HEADS)]
        heads = [slice(h * ATTN_HEAD_DIM, (h + 1) * ATTN_HEAD_DIM) for h in range(ATTN_HEADS)]
        scores = [_dot(kwins[h // ATTN_GROUPS], q[:, hs], NT) for h, hs in enumerate(heads)]
        probs = [_attn_probs(st, _lane_scalar(sk, h), valid)[0] for h, st in enumerate(scores)]
        for h, (hs, pt) in enumerate(zip(heads, probs)):
            o_ref[:, hs] = _dot(vwins_t[h // ATTN_GROUPS], pt).T.astype(o_ref.dtype)

    return pl.pallas_call(
        body, name="attn_fwd", grid=(nb,),
        in_specs=_attn_specs(nb),
        out_specs=pl.BlockSpec((ATTN_BLOCK, ATTN_Q), lambda i: (i, 0)),
        out_shape=jax.ShapeDtypeStruct((t, ATTN_Q), MXU_DTYPE),
        compiler_params=_params("parallel"),
    )(pa, pa, pa, pa, pa, cos, sin, cos, sin, sinks_vec)


def _attn_bwd(pa, cos, sin, sinks_vec, dao):
    t = pa.shape[0]
    nb = t // ATTN_BLOCK

    def body(q_ref, kc_ref, kp_ref, vc_ref, vp_ref, cc_ref, sc_ref, cp_ref, sp_ref, sk_ref, do_ref,
             dq_ref, dk_ref, dv_ref, acc_ref, dqr_ref, dkw_ref, dvw_ref, ck_ref, cv_ref):
        i = pl.program_id(0)

        @pl.when(i == 0)
        def _():
            acc_ref[...] = jnp.zeros_like(acc_ref)
            ck_ref[...] = jnp.zeros_like(ck_ref)
            cv_ref[...] = jnp.zeros_like(cv_ref)

        @pl.when(i < nb)
        def _():
            first = i == 0
            cc, sc = cc_ref[...], sc_ref[...]
            cq, sq = jnp.tile(cc, (1, ATTN_Q // ATTN_KV)), jnp.tile(sc, (1, ATTN_Q // ATTN_KV))
            q = _rope(q_ref[...], cq, sq)
            kc = _rope(kc_ref[...], cc, sc)
            kp = _rope(kp_ref[...], cp_ref[...], sp_ref[...])
            vc, vp = vc_ref[...], vp_ref[...]
            sk = sk_ref[...]
            do = do_ref[...]
            lane = lax.broadcasted_iota(jnp.int32, (1, 128), 1)
            dsink = jnp.zeros((1, 128), F32)
            valid = _attn_valid(first)
            kv = lambda tp, tc, hk: jnp.concatenate([tp[:, hk * ATTN_HEAD_DIM:(hk + 1) * ATTN_HEAD_DIM],
                                                     tc[:, hk * ATTN_HEAD_DIM:(hk + 1) * ATTN_HEAD_DIM]], axis=0)
            kwins = [kv(kp, kc, hk) for hk in range(ATTN_KV_HEADS)]
            vwins = [kv(vp, vc, hk) for hk in range(ATTN_KV_HEADS)]
            kwins_t = [kw.T for kw in kwins]
            heads = [slice(h * ATTN_HEAD_DIM, (h + 1) * ATTN_HEAD_DIM) for h in range(ATTN_HEADS)]
            scores = [_dot(kwins[h // ATTN_GROUPS], q[:, hs], NT) for h, hs in enumerate(heads)]
            dps = [_dot(vwins[h // ATTN_GROUPS], do[:, hs], NT) for h, hs in enumerate(heads)]
            pts, dsts = [], []
            for h, (st, dp_t) in enumerate(zip(scores, dps)):
                probs_t, psink = _attn_probs(st, _lane_scalar(sk, h), valid)
                delta = jnp.sum(probs_t * dp_t, axis=0, keepdims=True)
                pts.append(probs_t)
                dsts.append(probs_t * (dp_t - delta) * ATTN_SCALE)
                dsink += jnp.where(lane == h, jnp.sum(-psink * delta, axis=1, keepdims=True), 0.0)
            for h, (hs, ds_t) in enumerate(zip(heads, dsts)):
                dqr_ref[:, hs] = _dot(kwins_t[h // ATTN_GROUPS], ds_t).T
            for hk in range(ATTN_KV_HEADS):
                ks = slice(hk * ATTN_HEAD_DIM, (hk + 1) * ATTN_HEAD_DIM)
                group = range(hk * ATTN_GROUPS, (hk + 1) * ATTN_GROUPS)
                ds_g = jnp.concatenate([dsts[h] for h in group], axis=1)
                p_g = jnp.concatenate([pts[h] for h in group], axis=1)
                q_g = jnp.concatenate([q[:, heads[h]] for h in group], axis=0)
                do_g = jnp.concatenate([do[:, heads[h]] for h in group], axis=0)
                dkw_ref[:, ks] = _dot(ds_g, q_g)
                dvw_ref[:, ks] = _dot(p_g, do_g)
            acc_ref[0:1, :] += dsink
            dq_ref[...] = _rope_bwd(dqr_ref[...], cq, sq).astype(dq_ref.dtype)
            dk_ref[...] = (ck_ref[...] + _rope_bwd(dkw_ref[0:ATTN_BLOCK, :], cp_ref[...], sp_ref[...])).astype(dk_ref.dtype)
            dv_ref[...] = (cv_ref[...] + dvw_ref[0:ATTN_BLOCK, :]).astype(dv_ref.dtype)
            ck_ref[...] = _rope_bwd(dkw_ref[ATTN_BLOCK:2 * ATTN_BLOCK, :], cc, sc)
            cv_ref[...] = dvw_ref[ATTN_BLOCK:2 * ATTN_BLOCK, :]

        @pl.when(i == nb)
        def _():
            dk_ref[...] = ck_ref[...].astype(dk_ref.dtype)
            dv_ref[...] = cv_ref[...].astype(dv_ref.dtype)

    prev_out = lambda w: pl.BlockSpec((ATTN_BLOCK, w), lambda i: (jnp.maximum(i - 1, 0), 0))
    return pl.pallas_call(
        body, name="attn_bwd", grid=(nb + 1,),
        in_specs=_attn_specs(nb) + [pl.BlockSpec((ATTN_BLOCK, ATTN_Q), lambda i: (jnp.minimum(i, nb - 1), 0))],
        out_specs=[pl.BlockSpec((ATTN_BLOCK, ATTN_Q), lambda i: (jnp.minimum(i, nb - 1), 0)), prev_out(ATTN_KV),
                   prev_out(ATTN_KV), _full((8, 128))],
        out_shape=[jax.ShapeDtypeStruct((t, ATTN_Q), MXU_DTYPE), jax.ShapeDtypeStruct((t, ATTN_KV), MXU_DTYPE),
                   jax.ShapeDtypeStruct((t, ATTN_KV), MXU_DTYPE), jax.ShapeDtypeStruct((8, 128), F32)],
        scratch_shapes=[pltpu.VMEM((ATTN_BLOCK, ATTN_Q), F32), pltpu.VMEM((2 * ATTN_BLOCK, ATTN_KV), F32),
                        pltpu.VMEM((2 * ATTN_BLOCK, ATTN_KV), F32), pltpu.VMEM((ATTN_BLOCK, ATTN_KV), F32),
                        pltpu.VMEM((ATTN_BLOCK, ATTN_KV), F32)],
        compiler_params=_params("arbitrary"),
    )(pa, pa, pa, pa, pa, cos, sin, cos, sin, sinks_vec, dao)


PAIR = 2 * DN_CHUNK
INTRA_PAIRS = 4
SCAN_PAIRS = 2
HALO = 8


def _conv_window(cur_ref, prev_ref, xs_ref, tm, has_prev):
    prev = jnp.where(has_prev, prev_ref[...], 0.0)
    xs_ref[0:HALO, :] = prev
    xs_ref[HALO:HALO + tm, :] = cur_ref[...]


def _conv_taps(xs_ref, cw_ref, tm):
    y = cw_ref[0:1, :] * xs_ref[pl.ds(HALO - DN_CONV + 1, tm), :]
    for j in range(1, DN_CONV):
        y += cw_ref[j:j + 1, :] * xs_ref[pl.ds(HALO - DN_CONV + 1 + j, tm), :]
    return y


def _gate_values(ba, al, dt):
    beta = _sigmoid(ba)
    pre = ba + dt
    g = -jnp.exp(al) * _softplus(pre)
    return beta, g, pre


def _dn_prep_specs(tm, tile):
    return [pl.BlockSpec((tm, CONV_CH), lambda i: (tile(i), 0)),
            pl.BlockSpec((HALO, CONV_CH), lambda i: (jnp.maximum(tile(i) * (tm // HALO) - 1, 0), 0)),
            pl.BlockSpec((tm, 128), lambda i: (tile(i), 4 * DN_W // 128)),
            _full((DN_CONV, CONV_CH)), _full((1, 128)), _full((1, 128))]


def _dn_prep(pd, conv_w, al_vec, dt_vec, tm):
    t = pd.shape[0]

    def body(cur_ref, prev_ref, ba_ref, cw_ref, al_ref, dt_ref, qn_ref, kn_ref, vc_ref, gc_ref, gr_ref, xs_ref):
        _conv_window(cur_ref, prev_ref, xs_ref, tm, pl.program_id(0) > 0)
        y = _conv_taps(xs_ref, cw_ref, tm)
        c = y * _sigmoid(y)
        for h in range(DN_HEADS):
            qs = slice(h * DN_HEAD_DIM, (h + 1) * DN_HEAD_DIM)
            ksl = slice(DN_W + h * DN_HEAD_DIM, DN_W + (h + 1) * DN_HEAD_DIM)
            qh, kh = c[:, qs], c[:, ksl]
            qn_ref[:, qs] = qh * lax.rsqrt(jnp.sum(qh * qh, axis=-1, keepdims=True) + EPS) * DN_SCALE
            kn_ref[:, qs] = kh * lax.rsqrt(jnp.sum(kh * kh, axis=-1, keepdims=True) + EPS)
        vc_ref[...] = c[:, 2 * DN_W:3 * DN_W]
        beta, g, _ = _gate_values(ba_ref[...], al_ref[...], dt_ref[...])
        lane = lax.broadcasted_iota(jnp.int32, beta.shape, 1)
        gb = jnp.where(lane < DN_HEADS, beta, jnp.where(lane < 2 * DN_HEADS, g, 0.0))
        gc_ref[...] = gb
        gr_ref[...] = gb.T[0:8, :]

    tok = lambda w: pl.BlockSpec((tm, w), lambda i: (i, 0))
    return pl.pallas_call(
        body, name="dn_prep", grid=(t // tm,),
        in_specs=_dn_prep_specs(tm, lambda i: i),
        out_specs=[tok(DN_W), tok(DN_W), tok(DN_W), tok(128), pl.BlockSpec((8, tm), lambda i: (0, i))],
        out_shape=[jax.ShapeDtypeStruct((t, DN_W), F32)] * 3 + [jax.ShapeDtypeStruct((t, 128), F32),
                                                                 jax.ShapeDtypeStruct((8, t), F32)],
        scratch_shapes=[pltpu.VMEM((HALO + tm, CONV_CH), F32)],
        compiler_params=_params("parallel"),
    )(pd, pd, pd, conv_w, al_vec, dt_vec)


def _pair_masks():
    r = lax.broadcasted_iota(jnp.int32, (PAIR, PAIR), 0)
    c = lax.broadcasted_iota(jnp.int32, (PAIR, PAIR), 1)
    same = (r < DN_CHUNK) == (c < DN_CHUNK)
    return same & (r >= c), same & (r > c)


def _lane_col(mat, idx):
    lane = lax.broadcasted_iota(jnp.int32, mat.shape, 1)
    return jnp.sum(jnp.where(lane == idx, mat, 0.0), axis=-1, keepdims=True)


def _pair_cumsums(gc, gr, low):
    lowf = low.astype(F32)
    return _dot(lowf, gc, NN, HI), _dot(gr, lowf, NT, HI)


def _pair_gates(gc, cum_c, cum_r, low, h):
    beta = _lane_col(gc, h)
    gam = _lane_col(cum_c, DN_HEADS + h)
    gam_row = cum_r[DN_HEADS + h:DN_HEADS + h + 1, :]
    dm = jnp.where(low, jnp.exp(jnp.where(low, gam - gam_row, 0.0)), 0.0)
    row = lax.broadcasted_iota(jnp.int32, gam.shape, 0)
    gl = jnp.where(row < DN_CHUNK, gam[DN_CHUNK - 1:DN_CHUNK, :], gam[PAIR - 1:PAIR, :])
    return beta, gam, dm, gl


def _split(a):
    hi = a.astype(BF16)
    return hi, (a - hi.astype(F32)).astype(BF16)


def _dot_split(a, b, dims=NN):
    (ah, al), (bh, bl) = a, b
    la, lb = (1, 1) if dims == TN else ((0, 1) if dims == NN else (0, 0))
    r = _dot(jnp.concatenate([ah, al], axis=la), jnp.concatenate([bh, bl], axis=lb), dims)
    m, n = r.shape[0] // 2, r.shape[1] // 2
    return (r[m:, n:] + (r[:m, n:] + r[m:, :n])) + r[:m, :n]


def _unit_lower_inverses(lmats):
    n = lmats[0].shape[0]
    r = lax.broadcasted_iota(jnp.int32, (n, n), 0)
    c = lax.broadcasted_iota(jnp.int32, (n, n), 1)
    same = lambda size: (r & ~(size - 1)) == (c & ~(size - 1))
    base = DN_CHUNK // 4
    diag = [jnp.where(same(base), l, 0.0) for l in lmats]
    accs = [(r == c).astype(F32) - d for d in diag]
    splits = [_split(d) for d in diag]
    step = 1
    while 2 * step < base:
        splits = [_split(_dot_split(s, s)) for s in splits]
        accs = [acc + _dot_split(_split(acc), s) for acc, s in zip(accs, splits)]
        step *= 2
    size = base
    while size < DN_CHUNK:
        below = same(2 * size) & jnp.logical_not(same(size))
        tb = [_dot(acc, jnp.where(below, l, 0.0)) for acc, l in zip(accs, lmats)]
        accs = [acc - _dot(t, acc) for acc, t in zip(accs, tb)]
        size *= 2
    return accs


def _dn_intra(qn, kn, vc, gc, gr):
    t = qn.shape[0]
    npair = t // PAIR
    rows_step = INTRA_PAIRS * PAIR

    def body(q_ref, k_ref, v_ref, gc_ref, gr_ref, u_ref, w_ref, qg_ref, kd_ref, a_ref, ti_ref, dl_ref):
        low, strict = _pair_masks()
        items = []
        for p in range(INTRA_PAIRS):
            rows = slice(p * PAIR, (p + 1) * PAIR)
            gc_v = gc_ref[rows, :]
            cum_c, cum_r = _pair_cumsums(gc_v, gr_ref[:, rows], low)
            for h in range(DN_HEADS):
                hs = slice(h * DN_HEAD_DIM, (h + 1) * DN_HEAD_DIM)
                items.append((p, h, rows, hs, _pair_gates(gc_v, cum_c, cum_r, low, h)))
        lmats = []
        for p, h, rows, hs, (beta, gam, dm, gl) in items:
            k = k_ref[rows, hs]
            lmats.append(jnp.where(strict, _dot(k * beta, k, NT) * dm, 0.0))
        tinvs = _unit_lower_inverses(lmats)
        for (p, h, rows, hs, (beta, gam, dm, gl)), tinv in zip(items, tinvs):
            q, k, v = q_ref[rows, hs], k_ref[rows, hs], v_ref[rows, hs]
            eg = jnp.exp(gam)
            u_ref[rows, hs] = _dot(tinv, v * beta)
            w_ref[rows, hs] = _dot(tinv, (k * beta) * eg).astype(w_ref.dtype)
            a_ref[h, rows, :] = _dot(q, k, NT) * dm
            ti_ref[h, rows, :] = tinv
            qg_ref[rows, hs] = (q * eg).astype(qg_ref.dtype)
            kd_ref[rows, hs] = (k * jnp.exp(gl - gam)).astype(kd_ref.dtype)
            for c in range(2):
                last = (c + 1) * DN_CHUNK - 1
                dl_ref[2 * p + c, h] = jnp.broadcast_to(jnp.exp(gam[last:last + 1, :]), (8, 128))

    tok = lambda w: pl.BlockSpec((rows_step, w), lambda n: (n, 0))
    hm = pl.BlockSpec((DN_HEADS, rows_step, PAIR), lambda n: (0, n, 0))
    return pl.pallas_call(
        body, name="dn_intra", grid=(npair // INTRA_PAIRS,),
        in_specs=[tok(DN_W), tok(DN_W), tok(DN_W), tok(128), pl.BlockSpec((8, rows_step), lambda n: (0, n))],
        out_specs=[tok(DN_W)] * 4 + [hm, hm, pl.BlockSpec((2 * INTRA_PAIRS, DN_HEADS, 8, 128), lambda n: (n, 0, 0, 0))],
        out_shape=[jax.ShapeDtypeStruct((t, DN_W), F32)] + [jax.ShapeDtypeStruct((t, DN_W), MXU_DTYPE)] * 3
                  + [jax.ShapeDtypeStruct((DN_HEADS, t, PAIR), F32)] * 2
                  + [jax.ShapeDtypeStruct((2 * npair, DN_HEADS, 8, 128), F32)],
        compiler_params=_params("parallel"),
    )(qn, kn, vc, gc, gr)


def _dn_scan_fwd(u, w, qg, kd, a_qk, dlast, pd, dn_w):
    t = u.shape[0]
    npair = t // PAIR

    def body(u_ref, w_ref, qg_ref, kd_ref, a_ref, dl_ref, z_ref, nw_ref, out_ref, o_ref, vn_ref, sall_ref, s_ref):
        @pl.when(pl.program_id(0) == 0)
        def _():
            s_ref[...] = jnp.zeros_like(s_ref)

        nw = nw_ref[...]
        for c in range(2 * SCAN_PAIRS):
            rows = slice(c * DN_CHUNK, (c + 1) * DN_CHUNK)
            diag = slice((c % 2) * DN_CHUNK, (c % 2 + 1) * DN_CHUNK)
            for h in range(DN_HEADS):
                hs = slice(h * DN_HEAD_DIM, (h + 1) * DN_HEAD_DIM)
                st = s_ref[h]
                sall_ref[c, h] = st
                vn_ref[rows, hs] = (u_ref[rows, hs] - _dot(w_ref[rows, hs], st)).astype(vn_ref.dtype)
            for h in range(DN_HEADS):
                hs = slice(h * DN_HEAD_DIM, (h + 1) * DN_HEAD_DIM)
                st, vn = s_ref[h], vn_ref[rows, hs]
                o = _dot(qg_ref[rows, hs], st) + _dot(a_ref[h, rows, diag], vn)
                s_ref[h] = st * dl_ref[c, h][0:1, :] + _dot(kd_ref[rows, hs], vn, TN)
                o_ref[rows, hs] = o
                z = z_ref[rows, hs]
                on = o * lax.rsqrt(jnp.mean(o * o, axis=-1, keepdims=True) + EPS) * nw
                out_ref[rows, hs] = (on * (z * _sigmoid(z))).astype(out_ref.dtype)

    rows_step = SCAN_PAIRS * PAIR
    tok = pl.BlockSpec((rows_step, DN_W), lambda n: (n, 0))
    hm = pl.BlockSpec((DN_HEADS, rows_step, PAIR), lambda n: (0, n, 0))
    return pl.pallas_call(
        body, name="dn_scan_fwd", grid=(npair // SCAN_PAIRS,),
        in_specs=[tok, tok, tok, tok, hm, pl.BlockSpec((2 * SCAN_PAIRS, DN_HEADS, 8, 128), lambda n: (n, 0, 0, 0)),
                  pl.BlockSpec((rows_step, DN_W), lambda n: (n, 3)), _full((1, 128))],
        out_specs=[tok, tok, tok,
                   pl.BlockSpec((2 * SCAN_PAIRS, DN_HEADS, DN_HEAD_DIM, DN_HEAD_DIM), lambda n: (n, 0, 0, 0))],
        out_shape=[jax.ShapeDtypeStruct((t, DN_W), MXU_DTYPE), jax.ShapeDtypeStruct((t, DN_W), F32),
                   jax.ShapeDtypeStruct((t, DN_W), MXU_DTYPE),
                   jax.ShapeDtypeStruct((2 * npair, DN_HEADS, DN_HEAD_DIM, DN_HEAD_DIM), F32)],
        scratch_shapes=[pltpu.VMEM((DN_HEADS, DN_HEAD_DIM, DN_HEAD_DIM), F32)],
        compiler_params=_params("arbitrary"),
    )(u, w, qg, kd, a_qk, dlast, pd, dn_w)


def _dn_scan_bwd(dout, o, vnew, sall, w, qg, kd, a_qk, dlast, pd, dn_w):
    t = o.shape[0]
    npair = t // PAIR
    nstep = npair // SCAN_PAIRS
    rev = lambda n: nstep - 1 - n

    def body(do_ref, o_ref, vn_ref, sall_ref, w_ref, qg_ref, kd_ref, a_ref, dl_ref, z_ref, nw_ref,
             dz_ref, du_ref, dw_ref, dqg_ref, dkd_ref, da_ref, ddl_ref, acc_ref, ds_ref, dos_ref):
        @pl.when(pl.program_id(0) == 0)
        def _():
            ds_ref[...] = jnp.zeros_like(ds_ref)
            acc_ref[...] = jnp.zeros_like(acc_ref)

        nw = nw_ref[...]
        dnw = jnp.zeros((1, 128), F32)
        for h in range(DN_HEADS):
            hs = slice(h * DN_HEAD_DIM, (h + 1) * DN_HEAD_DIM)
            o, z, dout = o_ref[:, hs], z_ref[:, hs], do_ref[:, hs]
            r = lax.rsqrt(jnp.mean(o * o, axis=-1, keepdims=True) + EPS)
            oh = o * r
            sz = _sigmoid(z)
            dz_ref[:, hs] = dout * (oh * nw) * (sz + z * sz * (1.0 - sz))
            don = dout * (z * sz)
            dnw += jnp.sum(don * oh, axis=0, keepdims=True)
            doh = don * nw
            dos_ref[:, hs] = r * (doh - oh * jnp.mean(doh * oh, axis=-1, keepdims=True))
        acc_ref[0:1, :] += dnw
        for c in reversed(range(2 * SCAN_PAIRS)):
            rows = slice(c * DN_CHUNK, (c + 1) * DN_CHUNK)
            diag = slice((c % 2) * DN_CHUNK, (c % 2 + 1) * DN_CHUNK)
            other = slice((1 - c % 2) * DN_CHUNK, (2 - c % 2) * DN_CHUNK)
            for h in range(DN_HEADS):
                hs = slice(h * DN_HEAD_DIM, (h + 1) * DN_HEAD_DIM)
                do, st, dsp, vn = dos_ref[rows, hs], sall_ref[c, h], ds_ref[h], vn_ref[rows, hs]
                da_ref[h, rows, diag] = _dot(do, vn, NT)
                da_ref[h, rows, other] = jnp.zeros((DN_CHUNK, DN_CHUNK), F32)
                du_ref[rows, hs] = (_dot(a_ref[h, rows, diag], do, TN) + _dot(kd_ref[rows, hs], dsp)).astype(du_ref.dtype)
                dqg_ref[rows, hs] = _dot(do, st, NT)
                dkd_ref[rows, hs] = _dot(vn, dsp, NT)
                ddl = jnp.sum(jnp.sum(dsp * st, axis=1, keepdims=True), axis=0, keepdims=True)
                ddl_ref[c, h] = jnp.broadcast_to(ddl, (8, 128))
            for h in range(DN_HEADS):
                hs = slice(h * DN_HEAD_DIM, (h + 1) * DN_HEAD_DIM)
                do, st, dvn = dos_ref[rows, hs], sall_ref[c, h], du_ref[rows, hs]
                dw_ref[rows, hs] = (-_dot(dvn, st, NT)).astype(dw_ref.dtype)
                ds_ref[h] = (ds_ref[h] * dl_ref[c, h][0:1, :] + _dot(qg_ref[rows, hs], do, TN)
                             - _dot(w_ref[rows, hs], dvn, TN))

    rows_step = SCAN_PAIRS * PAIR
    tok = pl.BlockSpec((rows_step, DN_W), lambda n: (rev(n), 0))
    hm = pl.BlockSpec((DN_HEADS, rows_step, PAIR), lambda n: (0, rev(n), 0))
    sc = pl.BlockSpec((2 * SCAN_PAIRS, DN_HEADS, 8, 128), lambda n: (rev(n), 0, 0, 0))
    return pl.pallas_call(
        body, name="dn_scan_bwd", grid=(nstep,),
        in_specs=[tok, tok, tok,
                  pl.BlockSpec((2 * SCAN_PAIRS, DN_HEADS, DN_HEAD_DIM, DN_HEAD_DIM), lambda n: (rev(n), 0, 0, 0)),
                  tok, tok, tok, hm, sc, pl.BlockSpec((rows_step, DN_W), lambda n: (rev(n), 3)), _full((1, 128))],
        out_specs=[tok] * 5 + [hm, sc, _full((8, 128))],
        out_shape=[jax.ShapeDtypeStruct((t, DN_W), F32)] + [jax.ShapeDtypeStruct((t, DN_W), MXU_DTYPE)] * 2
                  + [jax.ShapeDtypeStruct((t, DN_W), F32)] * 2 + [jax.ShapeDtypeStruct((DN_HEADS, t, PAIR), F32),
                   jax.ShapeDtypeStruct((2 * npair, DN_HEADS, 8, 128), F32), jax.ShapeDtypeStruct((8, 128), F32)],
        scratch_shapes=[pltpu.VMEM((DN_HEADS, DN_HEAD_DIM, DN_HEAD_DIM), F32), pltpu.VMEM((SCAN_PAIRS * PAIR, DN_W), F32)],
        compiler_params=_params("arbitrary"),
    )(dout, o, vnew, sall, w, qg, kd, a_qk, dlast, pd, dn_w)


def _dn_intra_bwd(qn, kn, vc, gc, gr, tinv, a_qk, du, dw, dqg, dkd, da_qk, ddlast, dlast, dep):
    t = qn.shape[0]
    npair = t // PAIR

    def body(q_ref, k_ref, v_ref, gc_ref, gr_ref, ti_ref, a_ref, du_ref, dw_ref, dqg_ref, dkd_ref, da_ref, ddl_ref, dl_ref,
             dep_ref, dq_ref, dk_ref, dv_ref, dg_ref):
        low, strict = _pair_masks()
        lane = lax.broadcasted_iota(jnp.int32, (PAIR, 128), 1)
        rowi = lax.broadcasted_iota(jnp.int32, (PAIR, 1), 0)
        rsum = lambda v: jnp.sum(v, axis=-1, keepdims=True)
        items = []
        for p in range(INTRA_PAIRS):
            rows = slice(p * PAIR, (p + 1) * PAIR)
            gc_v = gc_ref[rows, :]
            cum_c, cum_r = _pair_cumsums(gc_v, gr_ref[:, rows], low)
            for h in range(DN_HEADS):
                hs = slice(h * DN_HEAD_DIM, (h + 1) * DN_HEAD_DIM)
                items.append((p, h, rows, hs, _pair_gates(gc_v, cum_c, cum_r, low, h)))
        dtis, lmats, dvbs, dkbgs = [], [], [], []
        for p, h, rows, hs, (beta, gam, dm, gl) in items:
            k, tinv = k_ref[rows, hs], ti_ref[h, rows, :]
            kb = k * beta
            dtis.append(_dot(du_ref[rows, hs], v_ref[rows, hs] * beta, NT)
                        + _dot(dw_ref[rows, hs], kb * jnp.exp(gam), NT))
            lmats.append(jnp.where(strict, _dot(kb, k, NT) * dm, 0.0))
            dvbs.append(_dot(tinv, du_ref[rows, hs], TN))
            dkbgs.append(_dot(tinv, dw_ref[rows, hs], TN))
        xs = [_dot(ti_ref[h, rows, :], dti, TN) for (p, h, rows, hs, g), dti in zip(items, dtis)]
        dls = [jnp.where(strict, -_dot(x, ti_ref[h, rows, :], NT), 0.0) for (p, h, rows, hs, g), x in zip(items, xs)]
        dgam_all = [jnp.zeros((PAIR, 128), F32) for _ in range(INTRA_PAIRS)]
        dbeta_all = [jnp.zeros((PAIR, 128), F32) for _ in range(INTRA_PAIRS)]
        for (p, h, rows, hs, (beta, gam, dm, gl)), dl, lmat, dvb, dkbg in zip(items, dls, lmats, dvbs, dkbgs):
            q, k, v = q_ref[rows, hs], k_ref[rows, hs], v_ref[rows, hs]
            a = a_ref[h, rows, :]
            dqg, dkd = dqg_ref[rows, hs], dkd_ref[rows, hs]
            kb = k * beta
            eg = jnp.exp(gam)
            ekd = jnp.exp(gl - gam)
            dmm = dl * dm
            dam = jnp.where(low, da_ref[h, rows, :], 0.0)
            dn = dam * dm
            e = dl * lmat + dam * a
            dkb = _dot(dmm, k) + dkbg * eg
            dk_ref[rows, hs] = _dot(dmm, kb, TN) + _dot(dn, q, TN) + dkd * ekd + dkb * beta
            dq_ref[rows, hs] = _dot(dn, k) + dqg * eg
            dv_ref[rows, hs] = dvb * beta
            t_kd = rsum(dkd * (k * ekd))
            dgam = rsum(e) - rsum(e.T) + rsum(dqg * (q * eg)) + rsum(dkbg * (kb * eg)) - t_kd
            for c in range(2):
                crows = slice(c * DN_CHUNK, (c + 1) * DN_CHUNK)
                dgl = (jnp.sum(t_kd[crows, :], axis=0, keepdims=True)
                       + ddl_ref[2 * p + c, h][0:1, 0:1] * dl_ref[2 * p + c, h][0:1, 0:1])
                dgam = dgam + jnp.where(rowi == (c + 1) * DN_CHUNK - 1, dgl, 0.0)
            dgam_all[p] += jnp.where(lane == DN_HEADS + h, dgam, 0.0)
            dbeta_all[p] += jnp.where(lane == h, rsum(dkb * k) + rsum(dvb * v), 0.0)
        for p in range(INTRA_PAIRS):
            dg_ref[p * PAIR:(p + 1) * PAIR, :] = dbeta_all[p] + _dot(low.astype(F32), dgam_all[p], TN, HI)

    rows_step = INTRA_PAIRS * PAIR
    tok = lambda w: pl.BlockSpec((rows_step, w), lambda n: (n, 0))
    hm = pl.BlockSpec((DN_HEADS, rows_step, PAIR), lambda n: (0, n, 0))
    sc = pl.BlockSpec((2 * INTRA_PAIRS, DN_HEADS, 8, 128), lambda n: (n, 0, 0, 0))
    return pl.pallas_call(
        body, name="dn_intra_bwd", grid=(npair // INTRA_PAIRS,),
        in_specs=[tok(DN_W), tok(DN_W), tok(DN_W), tok(128), pl.BlockSpec((8, rows_step), lambda n: (0, n)), hm, hm,
                  tok(DN_W), tok(DN_W), tok(DN_W), tok(DN_W), hm, sc, sc, pl.BlockSpec(memory_space=pl.ANY)],
        out_specs=[tok(DN_W), tok(DN_W), tok(DN_W), tok(128)],
        out_shape=[jax.ShapeDtypeStruct((t, DN_W), F32)] * 3 + [jax.ShapeDtypeStruct((t, 128), F32)],
        compiler_params=_params("parallel"),
    )(qn, kn, vc, gc, gr, tinv, a_qk, du, dw, dqg, dkd, da_qk, ddlast, dlast, dep)


def _dn_prep_bwd(pd, conv_w, al_vec, dt_vec, dqn, dkn, dvc, dgc, dz, tm):
    t = pd.shape[0]
    nt = t // tm
    tile = lambda i: nt - 1 - i

    def body(cur_ref, prev_ref, ba_ref, cw_ref, al_ref, dt_ref, dq_ref, dk_ref, dv_ref, dg_ref, dz_ref,
             o_ref, accw_ref, accg_ref, xs_ref, dc_ref, ds_ref, carry_ref):
        @pl.when(pl.program_id(0) == 0)
        def _():
            accw_ref[...] = jnp.zeros_like(accw_ref)
            accg_ref[...] = jnp.zeros_like(accg_ref)
            carry_ref[...] = jnp.zeros_like(carry_ref)

        _conv_window(cur_ref, prev_ref, xs_ref, tm, tile(pl.program_id(0)) > 0)
        taps = [xs_ref[pl.ds(HALO - DN_CONV + 1 + j, tm), :] for j in range(DN_CONV)]
        y = cw_ref[0:1, :] * taps[0]
        for j in range(1, DN_CONV):
            y += cw_ref[j:j + 1, :] * taps[j]
        sg = _sigmoid(y)
        c = y * sg
        for h in range(DN_HEADS):
            qs = slice(h * DN_HEAD_DIM, (h + 1) * DN_HEAD_DIM)
            ksl = slice(DN_W + h * DN_HEAD_DIM, DN_W + (h + 1) * DN_HEAD_DIM)
            for src, sl, scale in ((dq_ref, qs, DN_SCALE), (dk_ref, ksl, 1.0)):
                xh = c[:, sl]
                r = lax.rsqrt(jnp.sum(xh * xh, axis=-1, keepdims=True) + EPS)
                unit = xh * r
                dn = src[:, qs] * scale
                dc_ref[:, sl] = r * (dn - unit * jnp.sum(dn * unit, axis=-1, keepdims=True))
        dc_ref[:, 2 * DN_W:3 * DN_W] = dv_ref[...]
        dy = dc_ref[...] * (sg + y * sg * (1.0 - sg))
        for j in range(DN_CONV):
            accw_ref[j:j + 1, :] += jnp.sum(dy * taps[j], axis=0, keepdims=True)
        ds_ref[0:tm, :] = dy
        ds_ref[tm:tm + HALO, :] = carry_ref[...]
        carry_ref[...] = ds_ref[0:HALO, :]
        dx = cw_ref[0:1, :] * ds_ref[pl.ds(DN_CONV - 1, tm), :]
        for j in range(1, DN_CONV):
            dx += cw_ref[j:j + 1, :] * ds_ref[pl.ds(DN_CONV - 1 - j, tm), :]

        beta, g, pre = _gate_values(ba_ref[...], al_ref[...], dt_ref[...])
        dgb = dg_ref[...]
        lane = lax.broadcasted_iota(jnp.int32, dgb.shape, 1)
        is_b, is_a = lane < DN_HEADS, (lane >= DN_HEADS) & (lane < 2 * DN_HEADS)
        dpre = dgb * (-jnp.exp(al_ref[...])) * _sigmoid(pre)
        dba = jnp.where(is_b, dgb * beta * (1.0 - beta), jnp.where(is_a, dpre, 0.0))
        accg_ref[0:1, :] += jnp.sum(jnp.where(is_a, dgb * g, 0.0), axis=0, keepdims=True)
        accg_ref[1:2, :] += jnp.sum(jnp.where(is_a, dpre, 0.0), axis=0, keepdims=True)
        o_ref[:, 0:CONV_CH] = dx.astype(o_ref.dtype)
        o_ref[:, CONV_CH:CONV_CH + DN_W] = dz_ref[...].astype(o_ref.dtype)
        o_ref[:, CONV_CH + DN_W:DN_COLS] = dba.astype(o_ref.dtype)

    tok = lambda w: pl.BlockSpec((tm, w), lambda i: (tile(i), 0))
    return pl.pallas_call(
        body, name="dn_prep_bwd", grid=(nt,),
        in_specs=_dn_prep_specs(tm, tile) + [tok(DN_W), tok(DN_W), tok(DN_W), tok(128), tok(DN_W)],
        out_specs=[tok(DN_COLS), _full((8, CONV_CH)), _full((8, 128))],
        out_shape=[jax.ShapeDtypeStruct((t, DN_COLS), MXU_DTYPE),
                   jax.ShapeDtypeStruct((8, CONV_CH), F32), jax.ShapeDtypeStruct((8, 128), F32)],
        scratch_shapes=[pltpu.VMEM((HALO + tm, CONV_CH), F32), pltpu.VMEM((tm, CONV_CH), F32),
                        pltpu.VMEM((tm + HALO, CONV_CH), F32), pltpu.VMEM((HALO, CONV_CH), F32)],
        compiler_params=_params("arbitrary"),
    )(pd, pd, pd, conv_w, al_vec, dt_vec, dqn, dkn, dvc, dgc, dz)


def _pad_lanes(v, offset=0):
    return jnp.zeros((1, 128), F32).at[0, offset:offset + v.shape[0]].set(v.astype(F32))


class _LocalReducer:
    def start(self, grads):
        return jnp.zeros((8, 128), F32)

    def middle(self, after):
        return jnp.zeros((8, 128), F32)

    def finish(self, after):
        return None


def _local_step(x, p, tgt, sm, w, late, reducer):
    t = x.shape[0]
    tm = min(512, t // 2)
    tm_s = min(512, t // 2)
    tw = min(1024, t // 2)

    w_in = w["w_in"]
    wa = w_in[:, :ATTN_Q + 2 * ATTN_KV]
    wd = jnp.pad(w_in[:, ATTN_Q + 2 * ATTN_KV:], ((0, 0), (0, DN_COLS - (D_IN - ATTN_Q - 2 * ATTN_KV))))
    conv_w = w["conv_w"]
    al_vec, dt_vec = _pad_lanes(sm["a_log"], DN_HEADS), _pad_lanes(sm["dt_bias"], DN_HEADS)
    sinks_vec = _pad_lanes(sm["sinks"])
    dn_w = sm["dn_norm"].reshape(1, 128)
    row = lambda v: v.reshape(1, D_MODEL)
    cos, sin = _rope_tables(t)

    u, pa, pd = _inproj(x, row(sm["norm_mix"]), wa, wd, tm_s)
    ao = _attn_fwd(pa, cos, sin, sinks_vec)
    qn, kn, vc, gc, gr = _dn_prep(pd, conv_w, al_vec, dt_vec, tm_s)
    uu, ww, qg, kd, a_qk, tinv, dlast = _dn_intra(qn, kn, vc, gc, gr)
    dn_out, o, vnew, sall = _dn_scan_fwd(uu, ww, qg, kd, a_qk, dlast, pd, dn_w)
    w_o, late_rest = late(dn_out)
    wo_a, wo_d = w_o[:ATTN_Q], w_o[ATTN_Q:]
    h1 = _oproj(x, ao, dn_out, wo_a, wo_d, tm)
    w = dict(w, **late_rest(h1))
    w_proj = jnp.transpose(w["w_proj4"], (1, 0, 2)).reshape(PLE_DIM, D_MODEL)
    m, r, h2 = _mlp_fwd(h1, row(sm["norm_mlp"]), w["w_up4"], w["w_down"], tw)
    dh2, dh2b, dgp, dpp, n3, pb, acc_ple = _ple_loss(h2, p, tgt, row(sm["norm_ple"]), row(sm["norm_final"]),
                                                     w["w_gate"], w_proj, tm_s)
    g_w_gate = _wgrad(n3, dgp, "wgrad_gate", D_MODEL, D_MODEL, tw)
    g_w_proj = _wgrad(pb, dpp, "wgrad_proj", PLE_DIM, D_MODEL, tw)
    da, dh1, dh1b, acc_mlp = _mlp_bwd(dh2, dh2b, r, h1, row(sm["norm_mlp"]), w["w_up4"], w["w_down"], tm)
    g_w_up4 = _wgrad(m, da, "wgrad_up", D_MODEL, FF_BLOCK, tw, stacked=True)
    g_w_down = _wgrad(r, dh2b, "wgrad_down", FF_BLOCK, D_MODEL, tw,
                      prep=lambda rv: jnp.square(rv.astype(F32)).astype(MXU_DTYPE))
    g_w_o = _wgrad_cat([ao, dn_out], [dh1b], "wgrad_o", tw)
    early = dict(w_up4=g_w_up4, w_down=g_w_down, w_gate=g_w_gate, w_proj=g_w_proj, w_o=g_w_o)
    dep = reducer.start(early)
    dao, ddn = _oproj_bwd(dh1b, wo_a, wo_d, tm, dep)
    dz, du, dw, dqg, dkd, da_qk, ddlast, acc_dn = _dn_scan_bwd(ddn, o, vnew, sall, ww, qg, kd, a_qk, dlast, pd, dn_w)
    dep = reducer.middle(du)
    dqn, dkn, dvc, dgc = _dn_intra_bwd(qn, kn, vc, gc, gr, tinv, a_qk, du, dw, dqg, dkd, da_qk, ddlast, dlast, dep)
    d_dn, acc_conv, acc_gate = _dn_prep_bwd(pd, conv_w, al_vec, dt_vec, dqn, dkn, dvc, dgc, dz, tm_s)
    dq, dk, dv, acc_attn = _attn_bwd(pa, cos, sin, sinks_vec, dao)
    reducer.finish(dq)
    wq, wk, wv = wa[:, :ATTN_Q], wa[:, ATTN_Q:ATTN_Q + ATTN_KV], wa[:, ATTN_Q + ATTN_KV:]
    dx, acc_mix = _inproj_bwd(x, dh1, row(sm["norm_mix"]), [dq, dk, dv, d_dn], [wq, wk, wv, wd], tm_s)

    g_w_in_t = _wgrad_cat([dq, dk, dv, d_dn], [u], "wgrad_in", tw)[:D_IN]
    grads = dict(early, w_in_t=g_w_in_t)
    sums = dict(loss=acc_ple[2, 0], norm_final=acc_ple[0], norm_ple=acc_ple[1], norm_mlp=acc_mlp[0], norm_mix=acc_mix[0],
                dn_norm=acc_dn[0], sinks=acc_attn[0, :ATTN_HEADS], a_log=acc_gate[0, DN_HEADS:2 * DN_HEADS],
                dt_bias=acc_gate[1, DN_HEADS:2 * DN_HEADS], conv_w=acc_conv[:DN_CONV])
    return sums, dx, grads


MESH = pl.DeviceIdType.MESH
ANY = pl.BlockSpec(memory_space=pl.ANY)
N_CHIPS = 4
N_DEV = 8


def _place():
    x, y, c = lax.axis_index("x"), lax.axis_index("y"), lax.axis_index("c")
    chips = [(1 - x, y), (x, 1 - y), (1 - x, 1 - y)]
    return x, y, c, chips


def _gather_weights(shards, conv_s):
    n = len(shards)
    per = 7

    def body(*refs):
        in_refs, conv_ref = refs[:n], refs[n]
        out_refs, conv_out = refs[n + 1:2 * n + 1], refs[2 * n + 1]
        send_sems, recv_sems = refs[2 * n + 2:]
        x, y, c, chips = _place()
        sibling = (x, y, 1 - c)

        def blk(a, px, py, pc):
            hr = in_refs[a].shape[0] // 2
            return out_refs[a].at[2 * px + py, pl.ds(pc * hr, hr), :]

        def mine(a):
            hr = in_refs[a].shape[0] // 2
            return in_refs[a].at[pl.ds(c * hr, hr), :]

        def rcopy(a, k, block, to, src=None):
            return pltpu.make_async_remote_copy(
                src_ref=blk(a, *block) if src is None else src, dst_ref=blk(a, *block),
                send_sem=send_sems.at[per * a + k], recv_sem=recv_sems.at[per * a + k],
                device_id=to, device_id_type=MESH)

        def whole(a, to):
            return pltpu.make_async_remote_copy(
                src_ref=in_refs[a], dst_ref=out_refs[a].at[2 * x + y],
                send_sem=send_sems.at[per * a], recv_sem=recv_sems.at[per * a], device_id=to, device_id_type=MESH)

        def ccopy(j, to):
            return pltpu.make_async_remote_copy(
                src_ref=conv_ref, dst_ref=conv_out.at[2 * x + y],
                send_sem=send_sems.at[per * n + j], recv_sem=recv_sems.at[per * n + j],
                device_id=to, device_id_type=MESH)

        started = []
        for a in range(n):
            first = [whole(a, sibling)]
            first += [rcopy(a, 1 + j, (x, y, c), (*chip, c), src=mine(a)) for j, chip in enumerate(chips)]
            for cp in first:
                cp.start()
            started += first
        conv_sends = [ccopy(j, (*chip, c)) for j, chip in enumerate(chips)] + [ccopy(3, sibling)]
        for cp in conv_sends:
            cp.start()
        started += conv_sends
        for a in range(n):
            for j, chip in enumerate(chips):
                rcopy(a, 1 + j, (*chip, c), (x, y, c)).wait_recv()
                fwd = rcopy(a, 4 + j, (*chip, c), sibling)
                fwd.start()
                started.append(fwd)
        for a in range(n):
            whole(a, sibling).wait_recv()
            for j, chip in enumerate(chips):
                rcopy(a, 4 + j, (*chip, 1 - c), (x, y, c)).wait_recv()
        for j, chip in enumerate(chips + [(x, y)]):
            pltpu.make_async_remote_copy(
                src_ref=conv_ref, dst_ref=conv_out.at[2 * chip[0] + chip[1]],
                send_sem=send_sems.at[per * n + j], recv_sem=recv_sems.at[per * n + j],
                device_id=sibling, device_id_type=MESH).wait_recv()
        for cp in started:
            cp.wait_send()

    nsem = per * n + 4
    out_shape = [jax.ShapeDtypeStruct((N_CHIPS,) + s.shape, s.dtype) for s in shards]
    out_shape.append(jax.ShapeDtypeStruct((N_CHIPS,) + conv_s.shape, conv_s.dtype))
    return pl.pallas_call(
        body, name="gather_weights", in_specs=[ANY] * (n + 1), out_specs=[ANY] * (n + 1), out_shape=out_shape,
        scratch_shapes=[pltpu.SemaphoreType.DMA((nsem,)), pltpu.SemaphoreType.DMA((nsem,))],
    )(*shards, conv_s)


HBM = pl.BlockSpec(memory_space=pltpu.HBM)
SEM = pl.BlockSpec(memory_space=pltpu.SEMAPHORE)
EFFECT = pltpu.SideEffectType.DATAFLOW_SIDE_EFFECTING
LATE_COPIES = 7


def _late_copies(in_refs, land_refs, send_sems, recv_sems, only=None):
    x, y, c, chips = _place()
    sends, arrivals = [], []
    for a, (src, land) in enumerate(zip(in_refs, land_refs)):
        if only is not None and a not in only:
            continue
        hr = src.shape[0] // 2
        base = LATE_COPIES * a

        def cp(src_ref, dst_ref, s_idx, r_idx, to):
            return pltpu.make_async_remote_copy(src_ref=src_ref, dst_ref=dst_ref, send_sem=send_sems.at[base + s_idx],
                                                recv_sem=recv_sems.at[base + r_idx], device_id=to, device_id_type=MESH)

        sends.append(cp(src, land.at[2 * x + y], 0, 0, (x, y, 1 - c)))
        arrivals.append(cp(src, land.at[2 * x + y], 0, 0, (x, y, 1 - c)))
        for j, chip in enumerate(chips):
            for pc in range(2):
                half = src.at[pl.ds(c * hr, hr), :]
                sends.append(cp(half, land.at[2 * x + y, pl.ds(c * hr, hr), :], 1 + 2 * j + pc, 1 + 2 * j + c, (*chip, pc)))
                arrivals.append(cp(half, land.at[2 * chip[0] + chip[1], pl.ds(pc * hr, hr), :], 1 + 2 * j + pc,
                                   1 + 2 * j + pc, (*chip, pc)))
    return sends, arrivals


def _copies_start(name, build, nsem, srcs, land_shapes, after):
    n = len(srcs)

    def body(*refs):
        sends, _ = build(refs[:n], refs[n:2 * n], refs[2 * n + 1], refs[2 * n + 2])
        for cp in sends:
            cp.start()
        refs[-1][...] = jnp.zeros_like(refs[-1])

    lands = [pltpu.with_memory_space_constraint(lax.empty(s.shape, s.dtype), pltpu.HBM) for s in land_shapes]
    ins = [pltpu.with_memory_space_constraint(s, pltpu.HBM) for s in srcs]
    out = pl.pallas_call(
        body, name=name,
        out_shape=(pltpu.SemaphoreType.DMA((nsem,)), pltpu.SemaphoreType.DMA((nsem,)),
                   *[pltpu.HBM(s.shape, s.dtype) for s in srcs], *[pltpu.HBM(s.shape, s.dtype) for s in land_shapes],
                   jax.ShapeDtypeStruct((8, 128), F32)),
        in_specs=[HBM] * (2 * n) + [ANY],
        out_specs=(SEM, SEM, *[HBM] * (2 * n), pl.BlockSpec(memory_space=pltpu.VMEM)),
        input_output_aliases={i: 2 + i for i in range(2 * n)},
        compiler_params=pltpu.CompilerParams(has_side_effects=EFFECT),
    )(*ins, *lands, after)
    return out[0], out[1], out[2:2 + n], out[2 + n:2 + 2 * n], out[-1]


def _copies_wait(name, build, started, after):
    send_sems, recv_sems, srcs, lands, _ = started
    n = len(srcs)

    def body(*refs):
        sends, arrivals = build(refs[:n], refs[n:2 * n], refs[2 * n], refs[2 * n + 1])
        for cp in sends:
            cp.wait_send()
        for cp in arrivals:
            cp.wait_recv()

    out = pl.pallas_call(
        body, name=name,
        out_shape=(*[pltpu.HBM(s.shape, s.dtype) for s in srcs], *[pltpu.HBM(l.shape, l.dtype) for l in lands]),
        in_specs=[HBM] * (2 * n) + [SEM, SEM, ANY],
        out_specs=tuple([HBM] * (2 * n)),
        input_output_aliases={i: i for i in range(2 * n)},
        compiler_params=pltpu.CompilerParams(has_side_effects=EFFECT),
    )(*srcs, *lands, send_sems, recv_sems, after)
    return out[:n], out[n:]


def _exchange_copies(g_refs, got_refs, send_sems, recv_sems):
    x, y, c, _ = _place()
    sends, arrivals = [], []
    for a, (g, got) in enumerate(zip(g_refs, got_refs)):
        hr = g.shape[1] // 2
        cp = pltpu.make_async_remote_copy(
            src_ref=g.at[:, pl.ds((1 - c) * hr, hr), :], dst_ref=got, send_sem=send_sems.at[a],
            recv_sem=recv_sems.at[a], device_id=(x, y, 1 - c), device_id_type=MESH)
        sends.append(cp)
        arrivals.append(cp)
    return sends, arrivals


def _scatter_copies(s_refs, got_refs, send_sems, recv_sems):
    x, y, c, chips = _place()
    sends, arrivals = [], []
    for a, (s16, got) in enumerate(zip(s_refs, got_refs)):
        for j, chip in enumerate(chips):
            cp = pltpu.make_async_remote_copy(
                src_ref=s16.at[2 * chip[0] + chip[1]], dst_ref=got.at[j], send_sem=send_sems.at[3 * a + j],
                recv_sem=recv_sems.at[3 * a + j], device_id=(*chip, c), device_id_type=MESH)
            sends.append(cp)
            arrivals.append(cp)
    return sends, arrivals


def _share_halves(name, bufs, dep):
    n = len(bufs)

    def body(*refs):
        out_refs = refs[n + 1:2 * n + 1]
        send_sems, recv_sems = refs[2 * n + 1:]
        x, y, c, _ = _place()
        remote = [pltpu.make_async_remote_copy(
            src_ref=out_refs[a].at[c], dst_ref=out_refs[a].at[c], send_sem=send_sems.at[a], recv_sem=recv_sems.at[a],
            device_id=(x, y, 1 - c), device_id_type=MESH) for a in range(n)]
        for cp in remote:
            cp.start()
        for a in range(n):
            pltpu.make_async_remote_copy(
                src_ref=out_refs[a].at[c], dst_ref=out_refs[a].at[1 - c], send_sem=send_sems.at[a],
                recv_sem=recv_sems.at[a], device_id=(x, y, 1 - c), device_id_type=MESH).wait_recv()
        for cp in remote:
            cp.wait_send()

    return pl.pallas_call(
        body, name=name, in_specs=[ANY] * (n + 1), out_specs=[ANY] * n,
        out_shape=[jax.ShapeDtypeStruct(b.shape, b.dtype) for b in bufs],
        input_output_aliases={a: a for a in range(n)},
        scratch_shapes=[pltpu.SemaphoreType.DMA((n,)), pltpu.SemaphoreType.DMA((n,))],
    )(*bufs, dep)


SMALL_ROWS, SMALL_COLS = 16, CONV_CH


def _allreduce_small(block):
    m_per, ncol = block.shape

    def body(x_ref, sum_ref, all_ref, send_sems, recv_sems, local_sem):
        x, y, c, chips = _place()
        me, sibling = (x, y, c), (x, y, 1 - c)

        def rows(px, py, pc):
            return all_ref.at[pl.ds((4 * px + 2 * py + pc) * m_per, m_per), :]

        def copy(k, block_of, to, src=None):
            return pltpu.make_async_remote_copy(
                src_ref=rows(*block_of) if src is None else src, dst_ref=rows(*block_of),
                send_sem=send_sems.at[k], recv_sem=recv_sems.at[k], device_id=to, device_id_type=MESH)

        mine = pltpu.make_async_copy(x_ref, rows(*me), local_sem)
        mine.start()
        first = [copy(0, me, sibling, src=x_ref)]
        first += [copy(1 + j, me, (*chip, c), src=x_ref) for j, chip in enumerate(chips)]
        for cp in first:
            cp.start()
        passed = [copy(4 + j, (*chip, c), sibling) for j, chip in enumerate(chips)]
        for j, chip in enumerate(chips):
            copy(1 + j, (*chip, c), me).wait_recv()
            passed[j].start()
        copy(0, sibling, me).wait_recv()
        for j, chip in enumerate(chips):
            copy(4 + j, (*chip, 1 - c), me).wait_recv()
        for cp in first + passed:
            cp.wait_send()
        mine.wait()
        total = all_ref[0:m_per, :]
        for d in range(1, N_DEV):
            total = total + all_ref[d * m_per:(d + 1) * m_per, :]
        sum_ref[...] = total

    vm = pl.BlockSpec(memory_space=pltpu.VMEM)
    return pl.pallas_call(
        body, name="allreduce_small", in_specs=[vm], out_specs=vm,
        out_shape=jax.ShapeDtypeStruct((m_per, ncol), F32),
        scratch_shapes=[pltpu.VMEM((N_DEV * m_per, ncol), F32), pltpu.SemaphoreType.DMA((7,)),
                        pltpu.SemaphoreType.DMA((7,)), pltpu.SemaphoreType.DMA],
    )(block)


def _row_tile(rows, cols):
    tile = rows
    while tile * cols * 4 > (1 << 20) and tile % 16 == 0:
        tile //= 2
    return tile


def _elementwise(fn, name, ins, out_dtypes, dep):
    rows, cols = ins[0].shape
    tile = _row_tile(rows, cols)

    def body(*refs):
        outs = fn(*[r[...] for r in refs[:len(ins)]])
        for o_ref, o in zip(refs[len(ins) + 1:], outs):
            o_ref[...] = o.astype(o_ref.dtype)

    if tile * cols * 4 > (1 << 21) and cols % 512 == 0:
        spec = pl.BlockSpec((rows, 256), lambda i: (0, i))
        steps = cols // 256
    else:
        spec = pl.BlockSpec((tile, cols), lambda i: (i, 0))
        steps = rows // tile
    return pl.pallas_call(
        body, name=name, grid=(steps,), in_specs=[spec] * len(ins) + [pl.BlockSpec(memory_space=pl.ANY)],
        out_specs=[spec] * len(out_dtypes),
        out_shape=[jax.ShapeDtypeStruct((rows, cols), d) for d in out_dtypes],
        compiler_params=_params("parallel"),
    )(*ins, dep)


def _adamw_tile(w, g, m, v):
    m = ADAM_B1 * m + (1.0 - ADAM_B1) * g
    v = ADAM_B2 * v + (1.0 - ADAM_B2) * jnp.square(g)
    m_hat = m / (1.0 - ADAM_B1 ** ADAM_STEP)
    v_hat = v / (1.0 - ADAM_B2 ** ADAM_STEP)
    delta = -ADAM_LR * (m_hat / (jnp.sqrt(v_hat) + ADAM_EPS) + ADAM_WD * w)
    return delta, m, v


def _adamw(name, w, g, m, v, dep):
    return _elementwise(_adamw_tile, name, [w, g, m, v], [F32, F32, F32], dep)


def _chip_sum(name, g4, got, place):
    nchip, hr, cols = got.shape
    tile = _row_tile(hr, cols)
    nblk = hr // tile

    def body(pl_ref, g_ref, o_ref, s32_ref, s16_ref):
        s = g_ref[...] + o_ref[...]
        s32_ref[...] = s
        s16_ref[...] = s.astype(BF16)

    spec = pl.BlockSpec((None, tile, cols), lambda k, i, pr: (k, i, 0))
    return pl.pallas_call(
        body, name=name,
        grid_spec=pltpu.PrefetchScalarGridSpec(
            num_scalar_prefetch=1, grid=(nchip, nblk),
            in_specs=[pl.BlockSpec((None, tile, cols), lambda k, i, pr: (k, pr[1] * nblk + i, 0)), spec],
            out_specs=[spec, spec]),
        out_shape=[jax.ShapeDtypeStruct(got.shape, F32), jax.ShapeDtypeStruct(got.shape, BF16)],
        compiler_params=_params("parallel", "parallel"),
    )(place, g4, got)


def _mesh_sum(name, s32, got, place):
    _, hr, cols = s32.shape
    tile = _row_tile(hr, cols)

    def body(pl_ref, own_ref, g0_ref, g1_ref, g2_ref, o_ref):
        o_ref[...] = ((own_ref[...] + g0_ref[...].astype(F32)) + g1_ref[...].astype(F32)) + g2_ref[...].astype(F32)

    slab = lambda j: pl.BlockSpec((None, tile, cols), lambda i, pr: (j, i, 0))
    return pl.pallas_call(
        body, name=name,
        grid_spec=pltpu.PrefetchScalarGridSpec(
            num_scalar_prefetch=1, grid=(hr // tile,),
            in_specs=[pl.BlockSpec((None, tile, cols), lambda i, pr: (pr[0], i, 0)), slab(0), slab(1), slab(2)],
            out_specs=pl.BlockSpec((None, tile, cols), lambda i, pr: (pr[1], i, 0))),
        out_shape=jax.ShapeDtypeStruct((2, hr, cols), F32),
        compiler_params=_params("parallel"),
    )(place, s32, got, got, got)


def _place_operand():
    return jnp.stack([2 * lax.axis_index("x") + lax.axis_index("y"), lax.axis_index("c")]).astype(jnp.int32)


W_IN_ROWS = 720


def _per_chip(name, g):
    if name == "w_in_t":
        slabs = g.reshape(N_CHIPS, D_IN // N_CHIPS, D_MODEL)
        return jnp.pad(slabs, ((0, 0), (0, W_IN_ROWS - D_IN // N_CHIPS), (0, 0)))
    if name == "w_proj":
        return jnp.transpose(g.reshape(PLE_DIM, N_CHIPS, D_MODEL // N_CHIPS), (1, 0, 2))
    if name == "w_up4":
        return g
    return g.reshape(N_CHIPS, g.shape[0] // N_CHIPS, g.shape[1])


class _EarlyReducer:
    def __init__(self, tag):
        self.tag = tag

    def start(self, grads):
        self.names = list(grads)
        self.place = _place_operand()
        slabs = [_per_chip(k, grads[k]) for k in self.names]
        halves = [jax.ShapeDtypeStruct((s.shape[0], s.shape[1] // 2, s.shape[2]), F32) for s in slabs]
        self.a = _copies_start(self.tag + "exchange_start", _exchange_copies, len(slabs), slabs, halves,
                               slabs[0][0, :8, :128])
        return self.a[-1]

    def middle(self, after):
        slabs, got = _copies_wait(self.tag + "exchange_wait", _exchange_copies, self.a, after)
        self.sums = [_chip_sum(self.tag + "chip_sum_" + k, s, g, self.place) for k, s, g in zip(self.names, slabs, got)]
        s16 = [s[1] for s in self.sums]
        lands = [jax.ShapeDtypeStruct((3,) + s.shape[1:], BF16) for s in s16]
        self.b = _copies_start(self.tag + "scatter_start", _scatter_copies, 3 * len(s16), s16, lands,
                               self.sums[0][0][0, :8, :128])
        return self.b[-1]

    def finish(self, after):
        _, got = _copies_wait(self.tag + "scatter_wait", _scatter_copies, self.b, after)
        self.bufs = {k: _mesh_sum(self.tag + "mesh_sum_" + k, s[0], g, self.place)
                     for k, s, g in zip(self.names, self.sums, got)}


def kernel(x, p, norm_mix, w_in, conv_w, a_log, dt_bias, dn_norm, sinks, w_o, norm_mlp, w_up, w_down, norm_ple, w_ple_gate, w_ple_proj, norm_final, loss_target, m_norm_mix, m_w_in, m_conv_w, m_a_log, m_dt_bias, m_dn_norm, m_sinks, m_w_o, m_norm_mlp, m_w_up, m_w_down, m_norm_ple, m_w_ple_gate, m_w_ple_proj, m_norm_final, v_norm_mix, v_w_in, v_conv_w, v_a_log, v_dt_bias, v_dn_norm, v_sinks, v_w_o, v_norm_mlp, v_w_up, v_w_down, v_norm_ple, v_w_ple_gate, v_w_ple_proj, v_norm_final):
    chip = 2 * lax.axis_index("x") + lax.axis_index("y")
    big = dict(w_in=w_in[0], w_o=w_o[0], w_up=w_up[0], w_down=w_down[0], w_gate=w_ple_gate[0], w_proj=w_ple_proj[0])
    big_m = dict(w_in=m_w_in[0], w_o=m_w_o[0], w_up=m_w_up[0], w_down=m_w_down[0], w_gate=m_w_ple_gate[0], w_proj=m_w_ple_proj[0])
    big_v = dict(w_in=v_w_in[0], w_o=v_w_o[0], w_up=v_w_up[0], w_down=v_w_down[0], w_gate=v_w_ple_gate[0], w_proj=v_w_ple_proj[0])
    names = list(big)

    w_in_all, conv_all = _gather_weights([big["w_in"].astype(BF16)], conv_w[0])
    late_names = names[1:]
    late_shards = [big[k].astype(BF16) for k in late_names]
    gather = _copies_start("gather_start", _late_copies, LATE_COPIES * len(late_shards), late_shards,
                           [jax.ShapeDtypeStruct((N_CHIPS,) + s.shape, BF16) for s in late_shards], w_in_all)
    token = gather[-1]
    w = dict(w_in=jnp.transpose(w_in_all, (1, 0, 2)).reshape(D_MODEL, D_IN),
             conv_w=jnp.transpose(conv_all, (1, 0, 2)).reshape(DN_CONV, CONV_CH))
    sm = dict(norm_mix=norm_mix[0] + token[0, 0], a_log=a_log[0], dt_bias=dt_bias[0], dn_norm=dn_norm[0],
              sinks=sinks[0], norm_mlp=norm_mlp[0], norm_ple=norm_ple[0], norm_final=norm_final)

    def late(after):
        first = functools.partial(_late_copies, only=(0,))
        srcs, lands = _copies_wait("gather_wait_o", first, gather, after)

        def rest(after2):
            others = functools.partial(_late_copies, only=tuple(range(1, len(late_names))))
            gw = dict(zip(late_names, _copies_wait("gather_wait_rest", others, gather[:2] + (srcs, lands, None), after2)[1]))
            return dict(w_up4=gw["w_up"], w_down=gw["w_down"].reshape(D_FF, D_MODEL),
                        w_gate=gw["w_gate"].reshape(D_MODEL, D_MODEL), w_proj4=gw["w_proj"])

        return lands[0].reshape(D_MODEL, D_MODEL), rest

    reducer = _EarlyReducer("early_")
    sums, grad_x, g = _local_step(x[0], p[0, 0], loss_target[0], sm, w, late, reducer)

    last = _EarlyReducer("last_")
    dep_a = last.start({"w_in_t": g["w_in_t"]})

    row = lambda v: jnp.zeros((SMALL_COLS,), F32).at[:v.shape[0]].set(v)
    misc = jnp.zeros((SMALL_COLS,), F32).at[0:4].set(sums["a_log"]).at[4:8].set(sums["dt_bias"]) \
        .at[8:16].set(sums["sinks"]).at[128:256].set(sums["dn_norm"]).at[256].set(sums["loss"])
    small = jnp.concatenate([sums["conv_w"], jnp.stack([row(sums["norm_mix"]), row(sums["norm_mlp"]), row(sums["norm_ple"]),
                                                        row(sums["norm_final"]), misc]),
                             jnp.zeros((SMALL_ROWS - 9, SMALL_COLS), F32)], axis=0)
    tot = _allreduce_small(small + dep_a[0, 0])
    dep_b = last.middle(tot)
    grad_key = dict(w_o="w_o", w_up="w_up4", w_down="w_down", w_gate="w_gate", w_proj="w_proj")
    full = _share_halves("share_halves", [reducer.bufs[grad_key[k]] for k in late_names], dep_b)
    red = {k: f.reshape(-1, f.shape[-1]) for k, f in zip(late_names, full)}
    loss = tot[8, 256]
    ncw = CONV_CH // N_CHIPS

    def pack(cw, nmix, nmlp, nple, nfin, al, dtb, sk, dnn):
        misc_p = jnp.zeros((SMALL_COLS,), F32).at[0:4].set(al).at[4:8].set(dtb).at[8:16].set(sk).at[128:256].set(dnn)
        cw_p = jnp.zeros((DN_CONV, SMALL_COLS), F32).at[:, :ncw].set(cw)
        return jnp.concatenate([cw_p, jnp.stack([row(nmix), row(nmlp), row(nple), row(nfin), misc_p]),
                                jnp.zeros((SMALL_ROWS - 9, SMALL_COLS), F32)], axis=0)

    def unpack(buf):
        return dict(conv_w=buf[0:4, :ncw][None], norm_mix=buf[4, :D_MODEL][None], norm_mlp=buf[5, :D_MODEL][None],
                    norm_ple=buf[6, :D_MODEL][None], norm_final=buf[7, :D_MODEL], a_log=buf[8, 0:4][None],
                    dt_bias=buf[8, 4:8][None], sinks=buf[8, 8:16][None], dn_norm=buf[8, 128:256][None])

    g_conv_shard = lax.dynamic_slice(tot[0:4], (0, chip * ncw), (DN_CONV, ncw))
    g_small = pack(g_conv_shard, tot[4, :D_MODEL], tot[5, :D_MODEL], tot[6, :D_MODEL], tot[7, :D_MODEL],
                   tot[8, 0:4], tot[8, 4:8], tot[8, 8:16], tot[8, 128:256])
    w_small = pack(conv_w[0], norm_mix[0], norm_mlp[0], norm_ple[0], norm_final, a_log[0], dt_bias[0], sinks[0], dn_norm[0])
    m_small = pack(m_conv_w[0], m_norm_mix[0], m_norm_mlp[0], m_norm_ple[0], m_norm_final, m_a_log[0], m_dt_bias[0],
                   m_sinks[0], m_dn_norm[0])
    v_small = pack(v_conv_w[0], v_norm_mix[0], v_norm_mlp[0], v_norm_ple[0], v_norm_final, v_a_log[0], v_dt_bias[0],
                   v_sinks[0], v_dn_norm[0])

    ref_name = dict(w_in="w_in", w_o="w_o", w_up="w_up", w_down="w_down", w_gate="w_ple_gate", w_proj="w_ple_proj")
    out_g, out_d, out_m, out_v = {}, {}, {}, {}

    def update(k, dep):
        d_k, m_k, v_k = _adamw("adamw_" + k, big[k], red[k], big_m[k], big_v[k], dep)
        out_g[ref_name[k]], out_d[ref_name[k]] = red[k][None], d_k[None]
        out_m[ref_name[k]], out_v[ref_name[k]] = m_k[None], v_k[None]
        return d_k

    for k in late_names:
        done = update(k, dep_b)
    small_out = _adamw("adamw_small", w_small, g_small, m_small, v_small, dep_b)
    d_s, m_s, v_s = (unpack(b) for b in small_out)
    g_s = unpack(g_small)
    for src, dst in ((g_s, out_g), (d_s, out_d), (m_s, out_m), (v_s, out_v)):
        dst.update(src)
    last.finish(done + small_out[0][0:1, 0:1])
    (w_in_full,) = _share_halves("share_halves_w_in", [last.bufs["w_in_t"]], dep_b)
    g_t = w_in_full.reshape(W_IN_ROWS, D_MODEL)[:D_IN // N_CHIPS]
    d_t, m_t, v_t = _adamw("adamw_w_in", big["w_in"].T, g_t, big_m["w_in"].T, big_v["w_in"].T, dep_b)
    out_g["w_in"], out_d["w_in"], out_m["w_in"], out_v["w_in"] = g_t.T[None], d_t.T[None], m_t.T[None], v_t.T[None]
    order = ["norm_mix", "w_in", "conv_w", "a_log", "dt_bias", "dn_norm", "sinks", "w_o", "norm_mlp", "w_up", "w_down",
             "norm_ple", "w_ple_gate", "w_ple_proj", "norm_final"]
    return (loss, grad_x[None], *[out_g[k] for k in order], *[out_d[k] for k in order],
            *[out_m[k] for k in order], *[out_v[k] for k in order])
```

```python
import functools

import jax
import jax.numpy as jnp
from jax import lax
from jax.experimental import pallas as pl
from jax.experimental.pallas import tpu as pltpu

F32 = jnp.float32
BF16 = jnp.bfloat16
MXU_DTYPE = jnp.bfloat16
HI = lax.Precision.HIGHEST

D_MODEL = 1024
PLE_DIM = 256
ATTN_HEADS = 8
ATTN_KV_HEADS = 2
ATTN_GROUPS = ATTN_HEADS // ATTN_KV_HEADS
ATTN_HEAD_DIM = 64
ATTN_BLOCK = 128
ROPE_THETA = 10000.0
DN_HEADS = 4
DN_HEAD_DIM = 128
DN_CONV = 4
DN_CHUNK = 64
D_FF = 4 * D_MODEL
EPS = 1e-6
ATTN_Q = ATTN_HEADS * ATTN_HEAD_DIM
ATTN_KV = ATTN_KV_HEADS * ATTN_HEAD_DIM
DN_W = DN_HEADS * DN_HEAD_DIM
CONV_CH = 3 * DN_W
D_IN = ATTN_Q + 2 * ATTN_KV + 4 * DN_W + 2 * DN_HEADS
DN_COLS = 4 * DN_W + 128
DN_SCALE = DN_HEAD_DIM ** -0.5
ATTN_SCALE = ATTN_HEAD_DIM ** -0.5
FF_BLOCKS = 4
FF_BLOCK = D_FF // FF_BLOCKS

ADAM_LR = 0.001
ADAM_B1 = 0.9
ADAM_B2 = 0.999
ADAM_EPS = 1e-08
ADAM_WD = 0.01
ADAM_STEP = 10

V7X_VMEM_BYTES = 64 * 1024 * 1024
VMEM_LIMIT = 48 * 1024 * 1024

NN = ((1,), (0,))
NT = ((1,), (1,))
TN = ((0,), (0,))


def _dot(a, b, dims=NN, prec=None):
    if a.dtype != b.dtype:
        a, b = a.astype(MXU_DTYPE), b.astype(MXU_DTYPE)
    return lax.dot_general(a, b, (dims, ((), ())), precision=prec, preferred_element_type=F32)


def _sigmoid(x):
    return 1.0 / (1.0 + jnp.exp(-x))


def _softplus(x):
    return jnp.maximum(x, 0.0) + jnp.log(1.0 + jnp.exp(-jnp.abs(x)))


def _params(*sem):
    return pltpu.CompilerParams(dimension_semantics=sem, vmem_limit_bytes=VMEM_LIMIT)


def _rms_fwd(xv, g):
    r = lax.rsqrt(jnp.mean(xv * xv, axis=-1, keepdims=True) + EPS)
    return xv * r * g


def _rms_bwd(xv, g, dn):
    r = lax.rsqrt(jnp.mean(xv * xv, axis=-1, keepdims=True) + EPS)
    xh = xv * r
    dg = jnp.sum(dn * xh, axis=0, keepdims=True)
    dxh = dn * g
    dx = r * (dxh - xh * jnp.mean(dxh * xh, axis=-1, keepdims=True))
    return dx, dg


def _full(shape):
    return pl.BlockSpec(shape, lambda *_: (0,) * len(shape))


def _inproj(x, g_mix, wa, wd, tm):
    t = x.shape[0]

    def body(x_ref, g_ref, wa_ref, wd_ref, u_ref, pa_ref, pd_ref):
        u = _rms_fwd(x_ref[...], g_ref[...]).astype(MXU_DTYPE)
        u_ref[...] = u
        pa_ref[...] = _dot(u, wa_ref[...])
        pd_ref[...] = _dot(u, wd_ref[...])

    na, nd = wa.shape[1], wd.shape[1]
    return pl.pallas_call(
        body, name="inproj", grid=(t // tm,),
        in_specs=[pl.BlockSpec((tm, D_MODEL), lambda i: (i, 0)), _full((1, D_MODEL)),
                  _full((D_MODEL, na)), _full((D_MODEL, nd))],
        out_specs=[pl.BlockSpec((tm, D_MODEL), lambda i: (i, 0)), pl.BlockSpec((tm, na), lambda i: (i, 0)),
                   pl.BlockSpec((tm, nd), lambda i: (i, 0))],
        out_shape=[jax.ShapeDtypeStruct((t, D_MODEL), MXU_DTYPE), jax.ShapeDtypeStruct((t, na), F32),
                   jax.ShapeDtypeStruct((t, nd), F32)],
        compiler_params=_params("parallel"),
    )(x, g_mix, wa, wd)


def _oproj(x, ao, dn, wo_a, wo_d, tm):
    t = x.shape[0]

    def body(x_ref, ao_ref, dn_ref, wa_ref, wd_ref, h_ref):
        h_ref[...] = (x_ref[...] + _dot(ao_ref[...].astype(MXU_DTYPE), wa_ref[...])
                      + _dot(dn_ref[...].astype(MXU_DTYPE), wd_ref[...]))

    half = ao.shape[1]
    return pl.pallas_call(
        body, name="oproj", grid=(t // tm,),
        in_specs=[pl.BlockSpec((tm, D_MODEL), lambda i: (i, 0)), pl.BlockSpec((tm, half), lambda i: (i, 0)),
                  pl.BlockSpec((tm, half), lambda i: (i, 0)), _full((half, D_MODEL)), _full((half, D_MODEL))],
        out_specs=pl.BlockSpec((tm, D_MODEL), lambda i: (i, 0)),
        out_shape=jax.ShapeDtypeStruct((t, D_MODEL), F32),
        compiler_params=_params("parallel"),
    )(x, ao, dn, wo_a, wo_d)


def _mlp_fwd(h1, g_mlp, w_up4, w_down, tm):
    t = h1.shape[0]

    def body(h_ref, g_ref, wu_ref, wd_ref, m_ref, r_ref, h2_ref, acc_ref):
        k = pl.program_id(1)

        @pl.when(k == 0)
        def _():
            m_ref[...] = _rms_fwd(h_ref[...], g_ref[...]).astype(MXU_DTYPE)
            acc_ref[...] = jnp.zeros_like(acc_ref)

        r = jnp.maximum(_dot(m_ref[...], wu_ref[...]), 0.0)
        r_ref[...] = r.astype(MXU_DTYPE)
        s = jnp.square(r).astype(MXU_DTYPE)
        acc_ref[...] += _dot(s, wd_ref[...])

        @pl.when(k == FF_BLOCKS - 1)
        def _():
            h2_ref[...] = h_ref[...] + acc_ref[...]

    return pl.pallas_call(
        body, name="mlp_fwd", grid=(t // tm, FF_BLOCKS),
        in_specs=[pl.BlockSpec((tm, D_MODEL), lambda i, k: (i, 0)), _full((1, D_MODEL)),
                  pl.BlockSpec((None, D_MODEL, FF_BLOCK), lambda i, k: (k, 0, 0)),
                  pl.BlockSpec((FF_BLOCK, D_MODEL), lambda i, k: (k, 0))],
        out_specs=[pl.BlockSpec((tm, D_MODEL), lambda i, k: (i, 0)), pl.BlockSpec((tm, FF_BLOCK), lambda i, k: (i, k)),
                   pl.BlockSpec((tm, D_MODEL), lambda i, k: (i, 0))],
        out_shape=[jax.ShapeDtypeStruct((t, D_MODEL), MXU_DTYPE), jax.ShapeDtypeStruct((t, D_FF), MXU_DTYPE),
                   jax.ShapeDtypeStruct((t, D_MODEL), F32)],
        scratch_shapes=[pltpu.VMEM((tm, D_MODEL), F32)],
        compiler_params=_params("parallel", "arbitrary"),
    )(h1, g_mlp, w_up4, w_down)


def _ple_loss(h2, p, tgt, g_ple, g_fin, w_gate, w_proj, tm):
    t = h2.shape[0]

    def body(h_ref, p_ref, t_ref, gp_ref, gf_ref, wg_ref, wp_ref,
             dh_ref, dhb_ref, dgp_ref, dpp_ref, n3_ref, pb_ref, acc_ref):
        @pl.when(pl.program_id(0) == 0)
        def _():
            acc_ref[...] = jnp.zeros_like(acc_ref)

        h = h_ref[...]
        g_ple_v, g_fin_v = gp_ref[...], gf_ref[...]
        n3 = _rms_fwd(h, g_ple_v).astype(MXU_DTYPE)
        n3_ref[...] = n3
        gate = _sigmoid(_dot(n3, wg_ref[...]))
        pb = p_ref[...].astype(MXU_DTYPE)
        pb_ref[...] = pb
        pp = _dot(pb, wp_ref[...])
        h3 = h + gate * pp
        r4 = lax.rsqrt(jnp.mean(h3 * h3, axis=-1, keepdims=True) + EPS)
        xh4 = h3 * r4
        e = xh4 * g_fin_v - t_ref[...]
        loss = 0.5 * jnp.sum(jnp.mean(e * e, axis=-1, keepdims=True), axis=0, keepdims=True)
        dy = e * (1.0 / D_MODEL)
        dg_fin = jnp.sum(dy * xh4, axis=0, keepdims=True)
        dxh = dy * g_fin_v
        dh3 = r4 * (dxh - xh4 * jnp.mean(dxh * xh4, axis=-1, keepdims=True))
        dpp_ref[...] = (dh3 * gate).astype(MXU_DTYPE)
        dgp = (dh3 * pp * gate * (1.0 - gate)).astype(MXU_DTYPE)
        dgp_ref[...] = dgp
        dn3 = _dot(dgp, wg_ref[...], NT)
        dx, dg_ple = _rms_bwd(h, g_ple_v, dn3)
        dh2 = dh3 + dx
        dh_ref[...] = dh2
        dhb_ref[...] = dh2.astype(MXU_DTYPE)
        acc_ref[0:1, :] += dg_fin
        acc_ref[1:2, :] += dg_ple
        acc_ref[2:3, :] += jnp.broadcast_to(loss, (1, D_MODEL))

    row = lambda w: pl.BlockSpec((tm, w), lambda i: (i, 0))
    return pl.pallas_call(
        body, name="ple_loss", grid=(t // tm,),
        in_specs=[row(D_MODEL), row(PLE_DIM), row(D_MODEL), _full((1, D_MODEL)), _full((1, D_MODEL)),
                  _full((D_MODEL, D_MODEL)), _full((PLE_DIM, D_MODEL))],
        out_specs=[row(D_MODEL), row(D_MODEL), row(D_MODEL), row(D_MODEL), row(D_MODEL), row(PLE_DIM),
                   _full((8, D_MODEL))],
        out_shape=[jax.ShapeDtypeStruct((t, D_MODEL), F32), jax.ShapeDtypeStruct((t, D_MODEL), MXU_DTYPE),
                   jax.ShapeDtypeStruct((t, D_MODEL), MXU_DTYPE), jax.ShapeDtypeStruct((t, D_MODEL), MXU_DTYPE),
                   jax.ShapeDtypeStruct((t, D_MODEL), MXU_DTYPE), jax.ShapeDtypeStruct((t, PLE_DIM), MXU_DTYPE),
                   jax.ShapeDtypeStruct((8, D_MODEL), F32)],
        compiler_params=_params("arbitrary"),
    )(h2, p, tgt, g_ple, g_fin, w_gate, w_proj)


def _mlp_bwd(dh2, dh2b, r, h1, g_mlp, w_up4, w_down, tm):
    t = h1.shape[0]

    def body(dh_ref, dhb_ref, r_ref, h_ref, g_ref, wu_ref, wd_ref,
             da_ref, dh1_ref, dh1b_ref, acc_ref, dm_ref):
        i, k = pl.program_id(0), pl.program_id(1)

        @pl.when((i == 0) & (k == 0))
        def _():
            acc_ref[...] = jnp.zeros_like(acc_ref)

        @pl.when(k == 0)
        def _():
            dm_ref[...] = jnp.zeros_like(dm_ref)

        ds = _dot(dhb_ref[...], wd_ref[...], NT)
        da = (ds * (2.0 * r_ref[...].astype(F32))).astype(MXU_DTYPE)
        da_ref[...] = da
        dm_ref[...] += _dot(da, wu_ref[...], NT)

        @pl.when(k == FF_BLOCKS - 1)
        def _():
            dx, dg = _rms_bwd(h_ref[...], g_ref[...], dm_ref[...])
            dh1 = dh_ref[...] + dx
            dh1_ref[...] = dh1
            dh1b_ref[...] = dh1.astype(MXU_DTYPE)
            acc_ref[0:1, :] += dg

    tok = lambda w: pl.BlockSpec((tm, w), lambda i, k: (i, 0))
    return pl.pallas_call(
        body, name="mlp_bwd", grid=(t // tm, FF_BLOCKS),
        in_specs=[tok(D_MODEL), tok(D_MODEL), pl.BlockSpec((tm, FF_BLOCK), lambda i, k: (i, k)), tok(D_MODEL),
                  _full((1, D_MODEL)), pl.BlockSpec((None, D_MODEL, FF_BLOCK), lambda i, k: (k, 0, 0)),
                  pl.BlockSpec((FF_BLOCK, D_MODEL), lambda i, k: (k, 0))],
        out_specs=[pl.BlockSpec((tm, FF_BLOCK), lambda i, k: (i, k)),
                   tok(D_MODEL), tok(D_MODEL), pl.BlockSpec((8, D_MODEL), lambda i, k: (0, 0))],
        out_shape=[jax.ShapeDtypeStruct((t, D_FF), MXU_DTYPE),
                   jax.ShapeDtypeStruct((t, D_MODEL), F32), jax.ShapeDtypeStruct((t, D_MODEL), MXU_DTYPE),
                   jax.ShapeDtypeStruct((8, D_MODEL), F32)],
        scratch_shapes=[pltpu.VMEM((tm, D_MODEL), F32)],
        compiler_params=_params("arbitrary", "arbitrary"),
    )(dh2, dh2b, r, h1, g_mlp, w_up4, w_down)


def _oproj_bwd(dh1b, wo_a, wo_d, tm, dep):
    t = dh1b.shape[0]
    half = wo_a.shape[0]

    def body(d_ref, wa_ref, wd_ref, dep_ref, da_ref, dd_ref):
        d = d_ref[...]
        da_ref[...] = _dot(d, wa_ref[...], NT)
        dd_ref[...] = _dot(d, wd_ref[...], NT)

    return pl.pallas_call(
        body, name="oproj_bwd", grid=(t // tm,),
        in_specs=[pl.BlockSpec((tm, D_MODEL), lambda i: (i, 0)), _full((half, D_MODEL)), _full((half, D_MODEL)),
                  pl.BlockSpec(memory_space=pl.ANY)],
        out_specs=[pl.BlockSpec((tm, half), lambda i: (i, 0)), pl.BlockSpec((tm, half), lambda i: (i, 0))],
        out_shape=[jax.ShapeDtypeStruct((t, half), F32), jax.ShapeDtypeStruct((t, half), F32)],
        compiler_params=_params("parallel"),
    )(dh1b, wo_a, wo_d, dep)


def _inproj_bwd(x, dh1, g_mix, grads, weights, tm):
    t = x.shape[0]
    n = len(grads)

    def body(*refs):
        x_ref, dh_ref, g_ref = refs[:3]
        g_refs, w_refs = refs[3:3 + n], refs[3 + n:3 + 2 * n]
        dx_ref, acc_ref = refs[3 + 2 * n:]

        @pl.when(pl.program_id(0) == 0)
        def _():
            acc_ref[...] = jnp.zeros_like(acc_ref)

        du = _dot(g_refs[0][...], w_refs[0][...], NT)
        for j in range(1, n):
            du += _dot(g_refs[j][...], w_refs[j][...], NT)
        dx, dg = _rms_bwd(x_ref[...], g_ref[...], du)
        dx_ref[...] = dh_ref[...] + dx
        acc_ref[0:1, :] += dg

    tok = lambda w: pl.BlockSpec((tm, w), lambda i: (i, 0))
    return pl.pallas_call(
        body, name="inproj_bwd", grid=(t // tm,),
        in_specs=[tok(D_MODEL), tok(D_MODEL), _full((1, D_MODEL))] + [tok(g.shape[1]) for g in grads]
                 + [_full(w.shape) for w in weights],
        out_specs=[tok(D_MODEL), _full((8, D_MODEL))],
        out_shape=[jax.ShapeDtypeStruct((t, D_MODEL), F32), jax.ShapeDtypeStruct((8, D_MODEL), F32)],
        compiler_params=_params("arbitrary"),
    )(x, dh1, g_mix, *grads, *weights)


def _wgrad(a, b, name, tk, tn, tt, stacked=False, prep=None):
    t, kdim = a.shape
    ncols = b.shape[1]

    def body(a_ref, b_ref, o_ref):
        @pl.when(pl.program_id(2) == 0)
        def _():
            o_ref[...] = jnp.zeros_like(o_ref)

        av = a_ref[...] if prep is None else prep(a_ref[...])
        o_ref[...] += _dot(av, b_ref[...], TN)

    if stacked:
        out_spec = pl.BlockSpec((None, tk, tn), lambda i, j, s: (j, i, 0))
        out_shape = jax.ShapeDtypeStruct((ncols // tn, kdim, tn), F32)
    else:
        out_spec = pl.BlockSpec((tk, tn), lambda i, j, s: (i, j))
        out_shape = jax.ShapeDtypeStruct((kdim, ncols), F32)
    return pl.pallas_call(
        body, name=name, grid=(kdim // tk, ncols // tn, t // tt),
        in_specs=[pl.BlockSpec((tt, tk), lambda i, j, s: (s, i)), pl.BlockSpec((tt, tn), lambda i, j, s: (s, j))],
        out_specs=out_spec, out_shape=out_shape,
        compiler_params=_params("parallel", "parallel", "arbitrary"),
    )(a, b)


def _wgrad_cat(as_, bs, name, tt):
    t = as_[0].shape[0]
    heights = [a.shape[1] for a in as_]
    widths = [b.shape[1] for b in bs]

    def body(*refs):
        a_refs, b_refs, o_ref = refs[:len(as_)], refs[len(as_):-1], refs[-1]

        @pl.when(pl.program_id(0) == 0)
        def _():
            o_ref[...] = jnp.zeros_like(o_ref)

        row = 0
        for a_ref, k in zip(a_refs, heights):
            av = a_ref[...]
            col = 0
            for b_ref, n in zip(b_refs, widths):
                o_ref[row:row + k, col:col + n] += _dot(av, b_ref[...], TN)
                col += n
            row += k

    tok = lambda w: pl.BlockSpec((tt, w), lambda s: (s, 0))
    shape = (sum(heights), sum(widths))
    return pl.pallas_call(
        body, name=name, grid=(t // tt,),
        in_specs=[tok(k) for k in heights] + [tok(n) for n in widths],
        out_specs=_full(shape), out_shape=jax.ShapeDtypeStruct(shape, F32),
        compiler_params=_params("arbitrary"),
    )(*as_, *bs)


def _rope_tables(t):
    half = ATTN_HEAD_DIM // 2
    inv = 1.0 / (ROPE_THETA ** (jnp.arange(half, dtype=F32) * (2.0 / ATTN_HEAD_DIM)))
    ang = jnp.arange(t, dtype=F32)[:, None] * inv[None, :]
    cos, sin = jnp.cos(ang), jnp.sin(ang)
    cos2 = jnp.concatenate([cos, cos], axis=-1)
    sin2 = jnp.concatenate([-sin, sin], axis=-1)
    return jnp.tile(cos2, (1, 2)), jnp.tile(sin2, (1, 2))


def _swap_halves(tv):
    w = tv.shape[-1]
    lane = lax.broadcasted_iota(jnp.int32, tv.shape, tv.ndim - 1)
    first = (lane % ATTN_HEAD_DIM) < (ATTN_HEAD_DIM // 2)
    return jnp.where(first, pltpu.roll(tv, w - ATTN_HEAD_DIM // 2, tv.ndim - 1),
                     pltpu.roll(tv, ATTN_HEAD_DIM // 2, tv.ndim - 1))


def _rope(tv, cos, sin):
    return tv * cos + _swap_halves(tv) * sin


def _rope_bwd(dv, cos, sin):
    return dv * cos + _swap_halves(dv * sin)


def _attn_valid(first_block):
    c = lax.broadcasted_iota(jnp.int32, (2 * ATTN_BLOCK, ATTN_BLOCK), 0)
    r = lax.broadcasted_iota(jnp.int32, (2 * ATTN_BLOCK, ATTN_BLOCK), 1)
    return (c > r) & (c <= r + ATTN_BLOCK) & ((c >= ATTN_BLOCK) | jnp.logical_not(first_block))


def _attn_probs(st, sink, valid):
    s = jnp.where(valid, st * ATTN_SCALE, -jnp.inf)
    m = jnp.maximum(jnp.max(s, axis=0, keepdims=True), sink)
    e = jnp.where(valid, jnp.exp(s - m), 0.0)
    es = jnp.exp(sink - m)
    inv = 1.0 / (jnp.sum(e, axis=0, keepdims=True) + es)
    return e * inv, es * inv


def _lane_scalar(vec, idx):
    lane = lax.broadcasted_iota(jnp.int32, vec.shape, 1)
    return jnp.sum(jnp.where(lane == idx, vec, 0.0), axis=-1, keepdims=True)


def _attn_specs(nb):
    cur = lambda w, cb: pl.BlockSpec((ATTN_BLOCK, w), lambda i: (jnp.minimum(i, nb - 1), cb))
    prev = lambda w, cb: pl.BlockSpec((ATTN_BLOCK, w), lambda i: (jnp.maximum(jnp.minimum(i, nb - 1) - 1, 0), cb))
    kcol, vcol = ATTN_Q // ATTN_KV, ATTN_Q // ATTN_KV + 1
    return [cur(ATTN_Q, 0), cur(ATTN_KV, kcol), prev(ATTN_KV, kcol), cur(ATTN_KV, vcol), prev(ATTN_KV, vcol),
            cur(ATTN_KV, 0), cur(ATTN_KV, 0), prev(ATTN_KV, 0), prev(ATTN_KV, 0), _full((1, 128))]


def _attn_fwd(pa, cos, sin, sinks_vec):
    t = pa.shape[0]
    nb = t // ATTN_BLOCK

    def body(q_ref, kc_ref, kp_ref, vc_ref, vp_ref, cc_ref, sc_ref, cp_ref, sp_ref, sk_ref, o_ref):
        first = pl.program_id(0) == 0
        cc, sc = cc_ref[...], sc_ref[...]
        q = _rope(q_ref[...], jnp.tile(cc, (1, ATTN_Q // ATTN_KV)), jnp.tile(sc, (1, ATTN_Q // ATTN_KV)))
        kc = _rope(kc_ref[...], cc, sc)
        kp = _rope(kp_ref[...], cp_ref[...], sp_ref[...])
        vc, vp = vc_ref[...], vp_ref[...]
        sk = sk_ref[...]
        valid = _attn_valid(first)
        kv = lambda tp, tc, hk: jnp.concatenate([tp[:, hk * ATTN_HEAD_DIM:(hk + 1) * ATTN_HEAD_DIM],
                                                 tc[:, hk * ATTN_HEAD_DIM:(hk + 1) * ATTN_HEAD_DIM]], axis=0)
        kwins = [kv(kp, kc, hk) for hk in range(ATTN_KV_HEADS)]
        vwins_t = [kv(vp, vc, hk).T for hk in range(ATTN_KV_HEADS)]
        heads = [slice(h * ATTN_HEAD_DIM, (h + 1) * ATTN_HEAD_DIM) for h in range(ATTN_HEADS)]
        scores = [_dot(kwins[h // ATTN_GROUPS], q[:, hs], NT) for h, hs in enumerate(heads)]
        probs = [_attn_probs(st, _lane_scalar(sk, h), valid)[0] for h, st in enumerate(scores)]
        for h, (hs, pt) in enumerate(zip(heads, probs)):
            o_ref[:, hs] = _dot(vwins_t[h // ATTN_GROUPS], pt).T.astype(o_ref.dtype)

    return pl.pallas_call(
        body, name="attn_fwd", grid=(nb,),
        in_specs=_attn_specs(nb),
        out_specs=pl.BlockSpec((ATTN_BLOCK, ATTN_Q), lambda i: (i, 0)),
        out_shape=jax.ShapeDtypeStruct((t, ATTN_Q), MXU_DTYPE),
        compiler_params=_params("parallel"),
    )(pa, pa, pa, pa, pa, cos, sin, cos, sin, sinks_vec)


def _attn_bwd(pa, cos, sin, sinks_vec, dao):
    t = pa.shape[0]
    nb = t // ATTN_BLOCK

    def body(q_ref, kc_ref, kp_ref, vc_ref, vp_ref, cc_ref, sc_ref, cp_ref, sp_ref, sk_ref, do_ref,
             dq_ref, dk_ref, dv_ref, acc_ref, dqr_ref, dkw_ref, dvw_ref, ck_ref, cv_ref):
        i = pl.program_id(0)

        @pl.when(i == 0)
        def _():
            acc_ref[...] = jnp.zeros_like(acc_ref)
            ck_ref[...] = jnp.zeros_like(ck_ref)
            cv_ref[...] = jnp.zeros_like(cv_ref)

        @pl.when(i < nb)
        def _():
            first = i == 0
            cc, sc = cc_ref[...], sc_ref[...]
            cq, sq = jnp.tile(cc, (1, ATTN_Q // ATTN_KV)), jnp.tile(sc, (1, ATTN_Q // ATTN_KV))
            q = _rope(q_ref[...], cq, sq)
            kc = _rope(kc_ref[...], cc, sc)
            kp = _rope(kp_ref[...], cp_ref[...], sp_ref[...])
            vc, vp = vc_ref[...], vp_ref[...]
            sk = sk_ref[...]
            do = do_ref[...]
            lane = lax.broadcasted_iota(jnp.int32, (1, 128), 1)
            dsink = jnp.zeros((1, 128), F32)
            valid = _attn_valid(first)
            kv = lambda tp, tc, hk: jnp.concatenate([tp[:, hk * ATTN_HEAD_DIM:(hk + 1) * ATTN_HEAD_DIM],
                                                     tc[:, hk * ATTN_HEAD_DIM:(hk + 1) * ATTN_HEAD_DIM]], axis=0)
            kwins = [kv(kp, kc, hk) for hk in range(ATTN_KV_HEADS)]
            vwins = [kv(vp, vc, hk) for hk in range(ATTN_KV_HEADS)]
            kwins_t = [kw.T for kw in kwins]
            heads = [slice(h * ATTN_HEAD_DIM, (h + 1) * ATTN_HEAD_DIM) for h in range(ATTN_HEADS)]
            scores = [_dot(kwins[h // ATTN_GROUPS], q[:, hs], NT) for h, hs in enumerate(heads)]
            dps = [_dot(vwins[h // ATTN_GROUPS], do[:, hs], NT) for h, hs in enumerate(heads)]
            pts, dsts = [], []
            for h, (st, dp_t) in enumerate(zip(scores, dps)):
                probs_t, psink = _attn_probs(st, _lane_scalar(sk, h), valid)
                delta = jnp.sum(probs_t * dp_t, axis=0, keepdims=True)
                pts.append(probs_t)
                dsts.append(probs_t * (dp_t - delta) * ATTN_SCALE)
                dsink += jnp.where(lane == h, jnp.sum(-psink * delta, axis=1, keepdims=True), 0.0)
            for h, (hs, ds_t) in enumerate(zip(heads, dsts)):
                dqr_ref[:, hs] = _dot(kwins_t[h // ATTN_GROUPS], ds_t).T
            for hk in range(ATTN_KV_HEADS):
                ks = slice(hk * ATTN_HEAD_DIM, (hk + 1) * ATTN_HEAD_DIM)
                group = range(hk * ATTN_GROUPS, (hk + 1) * ATTN_GROUPS)
                ds_g = jnp.concatenate([dsts[h] for h in group], axis=1)
                p_g = jnp.concatenate([pts[h] for h in group], axis=1)
                q_g = jnp.concatenate([q[:, heads[h]] for h in group], axis=0)
                do_g = jnp.concatenate([do[:, heads[h]] for h in group], axis=0)
                dkw_ref[:, ks] = _dot(ds_g, q_g)
                dvw_ref[:, ks] = _dot(p_g, do_g)
            acc_ref[0:1, :] += dsink
            dq_ref[...] = _rope_bwd(dqr_ref[...], cq, sq).astype(dq_ref.dtype)
            dk_ref[...] = (ck_ref[...] + _rope_bwd(dkw_ref[0:ATTN_BLOCK, :], cp_ref[...], sp_ref[...])).astype(dk_ref.dtype)
            dv_ref[...] = (cv_ref[...] + dvw_ref[0:ATTN_BLOCK, :]).astype(dv_ref.dtype)
            ck_ref[...] = _rope_bwd(dkw_ref[ATTN_BLOCK:2 * ATTN_BLOCK, :], cc, sc)
            cv_ref[...] = dvw_ref[ATTN_BLOCK:2 * ATTN_BLOCK, :]

        @pl.when(i == nb)
        def _():
            dk_ref[...] = ck_ref[...].astype(dk_ref.dtype)
            dv_ref[...] = cv_ref[...].astype(dv_ref.dtype)

    prev_out = lambda w: pl.BlockSpec((ATTN_BLOCK, w), lambda i: (jnp.maximum(i - 1, 0), 0))
    return pl.pallas_call(
        body, name="attn_bwd", grid=(nb + 1,),
        in_specs=_attn_specs(nb) + [pl.BlockSpec((ATTN_BLOCK, ATTN_Q), lambda i: (jnp.minimum(i, nb - 1), 0))],
        out_specs=[pl.BlockSpec((ATTN_BLOCK, ATTN_Q), lambda i: (jnp.minimum(i, nb - 1), 0)), prev_out(ATTN_KV),
                   prev_out(ATTN_KV), _full((8, 128))],
        out_shape=[jax.ShapeDtypeStruct((t, ATTN_Q), MXU_DTYPE), jax.ShapeDtypeStruct((t, ATTN_KV), MXU_DTYPE),
                   jax.ShapeDtypeStruct((t, ATTN_KV), MXU_DTYPE), jax.ShapeDtypeStruct((8, 128), F32)],
        scratch_shapes=[pltpu.VMEM((ATTN_BLOCK, ATTN_Q), F32), pltpu.VMEM((2 * ATTN_BLOCK, ATTN_KV), F32),
                        pltpu.VMEM((2 * ATTN_BLOCK, ATTN_KV), F32), pltpu.VMEM((ATTN_BLOCK, ATTN_KV), F32),
                        pltpu.VMEM((ATTN_BLOCK, ATTN_KV), F32)],
        compiler_params=_params("arbitrary"),
    )(pa, pa, pa, pa, pa, cos, sin, cos, sin, sinks_vec, dao)


PAIR = 2 * DN_CHUNK
INTRA_PAIRS = 4
SCAN_PAIRS = 4
HALO = 8


def _conv_window(cur_ref, prev_ref, xs_ref, tm, has_prev):
    prev = jnp.where(has_prev, prev_ref[...], 0.0)
    xs_ref[0:HALO, :] = prev
    xs_ref[HALO:HALO + tm, :] = cur_ref[...]


def _conv_taps(xs_ref, cw_ref, tm):
    y = cw_ref[0:1, :] * xs_ref[pl.ds(HALO - DN_CONV + 1, tm), :]
    for j in range(1, DN_CONV):
        y += cw_ref[j:j + 1, :] * xs_ref[pl.ds(HALO - DN_CONV + 1 + j, tm), :]
    return y


def _gate_values(ba, al, dt):
    beta = _sigmoid(ba)
    pre = ba + dt
    g = -jnp.exp(al) * _softplus(pre)
    return beta, g, pre


def _dn_prep_specs(tm, tile):
    return [pl.BlockSpec((tm, CONV_CH), lambda i: (tile(i), 0)),
            pl.BlockSpec((HALO, CONV_CH), lambda i: (jnp.maximum(tile(i) * (tm // HALO) - 1, 0), 0)),
            pl.BlockSpec((tm, 128), lambda i: (tile(i), 4 * DN_W // 128)),
            _full((DN_CONV, CONV_CH)), _full((1, 128)), _full((1, 128))]


def _dn_prep(pd, conv_w, al_vec, dt_vec, tm):
    t = pd.shape[0]

    def body(cur_ref, prev_ref, ba_ref, cw_ref, al_ref, dt_ref, qn_ref, kn_ref, vc_ref, gc_ref, gr_ref, xs_ref):
        _conv_window(cur_ref, prev_ref, xs_ref, tm, pl.program_id(0) > 0)
        y = _conv_taps(xs_ref, cw_ref, tm)
        c = y * _sigmoid(y)
        for h in range(DN_HEADS):
            qs = slice(h * DN_HEAD_DIM, (h + 1) * DN_HEAD_DIM)
            ksl = slice(DN_W + h * DN_HEAD_DIM, DN_W + (h + 1) * DN_HEAD_DIM)
            qh, kh = c[:, qs], c[:, ksl]
            qn_ref[:, qs] = qh * lax.rsqrt(jnp.sum(qh * qh, axis=-1, keepdims=True) + EPS) * DN_SCALE
            kn_ref[:, qs] = kh * lax.rsqrt(jnp.sum(kh * kh, axis=-1, keepdims=True) + EPS)
        vc_ref[...] = c[:, 2 * DN_W:3 * DN_W]
        beta, g, _ = _gate_values(ba_ref[...], al_ref[...], dt_ref[...])
        lane = lax.broadcasted_iota(jnp.int32, beta.shape, 1)
        gb = jnp.where(lane < DN_HEADS, beta, jnp.where(lane < 2 * DN_HEADS, g, 0.0))
        gc_ref[...] = gb
        gr_ref[...] = gb.T[0:8, :]

    tok = lambda w: pl.BlockSpec((tm, w), lambda i: (i, 0))
    return pl.pallas_call(
        body, name="dn_prep", grid=(t // tm,),
        in_specs=_dn_prep_specs(tm, lambda i: i),
        out_specs=[tok(DN_W), tok(DN_W), tok(DN_W), tok(128), pl.BlockSpec((8, tm), lambda i: (0, i))],
        out_shape=[jax.ShapeDtypeStruct((t, DN_W), F32)] * 3 + [jax.ShapeDtypeStruct((t, 128), F32),
                                                                 jax.ShapeDtypeStruct((8, t), F32)],
        scratch_shapes=[pltpu.VMEM((HALO + tm, CONV_CH), F32)],
        compiler_params=_params("parallel"),
    )(pd, pd, pd, conv_w, al_vec, dt_vec)


def _pair_masks():
    r = lax.broadcasted_iota(jnp.int32, (PAIR, PAIR), 0)
    c = lax.broadcasted_iota(jnp.int32, (PAIR, PAIR), 1)
    same = (r < DN_CHUNK) == (c < DN_CHUNK)
    return same & (r >= c), same & (r > c)


def _lane_col(mat, idx):
    lane = lax.broadcasted_iota(jnp.int32, mat.shape, 1)
    return jnp.sum(jnp.where(lane == idx, mat, 0.0), axis=-1, keepdims=True)


def _pair_cumsums(gc, gr, low):
    lowf = low.astype(F32)
    return _dot(lowf, gc, NN, HI), _dot(gr, lowf, NT, HI)


def _pair_gates(gc, cum_c, cum_r, low, h):
    beta = _lane_col(gc, h)
    gam = _lane_col(cum_c, DN_HEADS + h)
    gam_row = cum_r[DN_HEADS + h:DN_HEADS + h + 1, :]
    dm = jnp.where(low, jnp.exp(jnp.where(low, gam - gam_row, 0.0)), 0.0)
    row = lax.broadcasted_iota(jnp.int32, gam.shape, 0)
    gl = jnp.where(row < DN_CHUNK, gam[DN_CHUNK - 1:DN_CHUNK, :], gam[PAIR - 1:PAIR, :])
    return beta, gam, dm, gl


def _split(a):
    hi = a.astype(BF16)
    return hi, (a - hi.astype(F32)).astype(BF16)


def _dot_split(a, b, dims=NN):
    (ah, al), (bh, bl) = a, b
    la, lb = (1, 1) if dims == TN else ((0, 1) if dims == NN else (0, 0))
    r = _dot(jnp.concatenate([ah, al], axis=la), jnp.concatenate([bh, bl], axis=lb), dims)
    m, n = r.shape[0] // 2, r.shape[1] // 2
    return (r[m:, n:] + (r[:m, n:] + r[m:, :n])) + r[:m, :n]


def _unit_lower_inverses(lmats):
    n = lmats[0].shape[0]
    r = lax.broadcasted_iota(jnp.int32, (n, n), 0)
    c = lax.broadcasted_iota(jnp.int32, (n, n), 1)
    same = lambda size: (r & ~(size - 1)) == (c & ~(size - 1))
    base = DN_CHUNK // 4
    diag = [jnp.where(same(base), l, 0.0) for l in lmats]
    accs = [(r == c).astype(F32) - d for d in diag]
    splits = [_split(d) for d in diag]
    step = 1
    while 2 * step < base:
        splits = [_split(_dot_split(s, s)) for s in splits]
        accs = [acc + _dot_split(_split(acc), s) for acc, s in zip(accs, splits)]
        step *= 2
    size = base
    while size < DN_CHUNK:
        below = same(2 * size) & jnp.logical_not(same(size))
        tb = [_dot(acc, jnp.where(below, l, 0.0)) for acc, l in zip(accs, lmats)]
        accs = [acc - _dot(t, acc) for acc, t in zip(accs, tb)]
        size *= 2
    return accs


def _dn_intra(qn, kn, vc, gc, gr):
    t = qn.shape[0]
    npair = t // PAIR
    rows_step = INTRA_PAIRS * PAIR

    def body(q_ref, k_ref, v_ref, gc_ref, gr_ref, u_ref, w_ref, qg_ref, kd_ref, a_ref, ti_ref, dl_ref):
        low, strict = _pair_masks()
        items = []
        for p in range(INTRA_PAIRS):
            rows = slice(p * PAIR, (p + 1) * PAIR)
            gc_v = gc_ref[rows, :]
            cum_c, cum_r = _pair_cumsums(gc_v, gr_ref[:, rows], low)
            for h in range(DN_HEADS):
                hs = slice(h * DN_HEAD_DIM, (h + 1) * DN_HEAD_DIM)
                items.append((p, h, rows, hs, _pair_gates(gc_v, cum_c, cum_r, low, h)))
        lmats = []
        for p, h, rows, hs, (beta, gam, dm, gl) in items:
            k = k_ref[rows, hs]
            lmats.append(jnp.where(strict, _dot(k * beta, k, NT) * dm, 0.0))
        tinvs = _unit_lower_inverses(lmats)
        for (p, h, rows, hs, (beta, gam, dm, gl)), tinv in zip(items, tinvs):
            q, k, v = q_ref[rows, hs], k_ref[rows, hs], v_ref[rows, hs]
            eg = jnp.exp(gam)
            u_ref[rows, hs] = _dot(tinv, v * beta)
            w_ref[rows, hs] = _dot(tinv, (k * beta) * eg).astype(w_ref.dtype)
            a_ref[h, rows, :] = _dot(q, k, NT) * dm
            ti_ref[h, rows, :] = tinv
            qg_ref[rows, hs] = (q * eg).astype(qg_ref.dtype)
            kd_ref[rows, hs] = (k * jnp.exp(gl - gam)).astype(kd_ref.dtype)
            for c in range(2):
                last = (c + 1) * DN_CHUNK - 1
                dl_ref[2 * p + c, h] = jnp.broadcast_to(jnp.exp(gam[last:last + 1, :]), (8, 128))

    tok = lambda w: pl.BlockSpec((rows_step, w), lambda n: (n, 0))
    hm = pl.BlockSpec((DN_HEADS, rows_step, PAIR), lambda n: (0, n, 0))
    return pl.pallas_call(
        body, name="dn_intra", grid=(npair // INTRA_PAIRS,),
        in_specs=[tok(DN_W), tok(DN_W), tok(DN_W), tok(128), pl.BlockSpec((8, rows_step), lambda n: (0, n))],
        out_specs=[tok(DN_W)] * 4 + [hm, hm, pl.BlockSpec((2 * INTRA_PAIRS, DN_HEADS, 8, 128), lambda n: (n, 0, 0, 0))],
        out_shape=[jax.ShapeDtypeStruct((t, DN_W), F32)] + [jax.ShapeDtypeStruct((t, DN_W), MXU_DTYPE)] * 3
                  + [jax.ShapeDtypeStruct((DN_HEADS, t, PAIR), F32)] * 2
                  + [jax.ShapeDtypeStruct((2 * npair, DN_HEADS, 8, 128), F32)],
        compiler_params=_params("parallel"),
    )(qn, kn, vc, gc, gr)


def _dn_scan_fwd(u, w, qg, kd, a_qk, dlast, pd, dn_w):
    t = u.shape[0]
    npair = t // PAIR

    def body(u_ref, w_ref, qg_ref, kd_ref, a_ref, dl_ref, z_ref, nw_ref, out_ref, o_ref, vn_ref, sall_ref, s_ref):
        @pl.when(pl.program_id(0) == 0)
        def _():
            s_ref[...] = jnp.zeros_like(s_ref)

        nw = nw_ref[...]
        for c in range(2 * SCAN_PAIRS):
            rows = slice(c * DN_CHUNK, (c + 1) * DN_CHUNK)
            diag = slice((c % 2) * DN_CHUNK, (c % 2 + 1) * DN_CHUNK)
            for h in range(DN_HEADS):
                hs = slice(h * DN_HEAD_DIM, (h + 1) * DN_HEAD_DIM)
                st = s_ref[h]
                sall_ref[c, h] = st
                vn_ref[rows, hs] = (u_ref[rows, hs] - _dot(w_ref[rows, hs], st)).astype(vn_ref.dtype)
            for h in range(DN_HEADS):
                hs = slice(h * DN_HEAD_DIM, (h + 1) * DN_HEAD_DIM)
                st, vn = s_ref[h], vn_ref[rows, hs]
                o = _dot(qg_ref[rows, hs], st) + _dot(a_ref[h, rows, diag], vn)
                s_ref[h] = st * dl_ref[c, h][0:1, :] + _dot(kd_ref[rows, hs], vn, TN)
                o_ref[rows, hs] = o
                z = z_ref[rows, hs]
                on = o * lax.rsqrt(jnp.mean(o * o, axis=-1, keepdims=True) + EPS) * nw
                out_ref[rows, hs] = (on * (z * _sigmoid(z))).astype(out_ref.dtype)

    rows_step = SCAN_PAIRS * PAIR
    tok = pl.BlockSpec((rows_step, DN_W), lambda n: (n, 0))
    hm = pl.BlockSpec((DN_HEADS, rows_step, PAIR), lambda n: (0, n, 0))
    return pl.pallas_call(
        body, name="dn_scan_fwd", grid=(npair // SCAN_PAIRS,),
        in_specs=[tok, tok, tok, tok, hm, pl.BlockSpec((2 * SCAN_PAIRS, DN_HEADS, 8, 128), lambda n: (n, 0, 0, 0)),
                  pl.BlockSpec((rows_step, DN_W), lambda n: (n, 3)), _full((1, 128))],
        out_specs=[tok, tok, tok,
                   pl.BlockSpec((2 * SCAN_PAIRS, DN_HEADS, DN_HEAD_DIM, DN_HEAD_DIM), lambda n: (n, 0, 0, 0))],
        out_shape=[jax.ShapeDtypeStruct((t, DN_W), MXU_DTYPE), jax.ShapeDtypeStruct((t, DN_W), F32),
                   jax.ShapeDtypeStruct((t, DN_W), MXU_DTYPE),
                   jax.ShapeDtypeStruct((2 * npair, DN_HEADS, DN_HEAD_DIM, DN_HEAD_DIM), F32)],
        scratch_shapes=[pltpu.VMEM((DN_HEADS, DN_HEAD_DIM, DN_HEAD_DIM), F32)],
        compiler_params=_params("arbitrary"),
    )(u, w, qg, kd, a_qk, dlast, pd, dn_w)


def _dn_scan_bwd(dout, o, vnew, sall, w, qg, kd, a_qk, dlast, pd, dn_w):
    t = o.shape[0]
    npair = t // PAIR
    nstep = npair // SCAN_PAIRS
    rev = lambda n: nstep - 1 - n

    def body(do_ref, o_ref, vn_ref, sall_ref, w_ref, qg_ref, kd_ref, a_ref, dl_ref, z_ref, nw_ref,
             dz_ref, du_ref, dw_ref, dqg_ref, dkd_ref, da_ref, ddl_ref, acc_ref, ds_ref, dos_ref):
        @pl.when(pl.program_id(0) == 0)
        def _():
            ds_ref[...] = jnp.zeros_like(ds_ref)
            acc_ref[...] = jnp.zeros_like(acc_ref)

        nw = nw_ref[...]
        dnw = jnp.zeros((1, 128), F32)
        for h in range(DN_HEADS):
            hs = slice(h * DN_HEAD_DIM, (h + 1) * DN_HEAD_DIM)
            o, z, dout = o_ref[:, hs], z_ref[:, hs], do_ref[:, hs]
            r = lax.rsqrt(jnp.mean(o * o, axis=-1, keepdims=True) + EPS)
            oh = o * r
            sz = _sigmoid(z)
            dz_ref[:, hs] = dout * (oh * nw) * (sz + z * sz * (1.0 - sz))
            don = dout * (z * sz)
            dnw += jnp.sum(don * oh, axis=0, keepdims=True)
            doh = don * nw
            dos_ref[:, hs] = r * (doh - oh * jnp.mean(doh * oh, axis=-1, keepdims=True))
        acc_ref[0:1, :] += dnw
        for c in reversed(range(2 * SCAN_PAIRS)):
            rows = slice(c * DN_CHUNK, (c + 1) * DN_CHUNK)
            diag = slice((c % 2) * DN_CHUNK, (c % 2 + 1) * DN_CHUNK)
            other = slice((1 - c % 2) * DN_CHUNK, (2 - c % 2) * DN_CHUNK)
            for h in range(DN_HEADS):
                hs = slice(h * DN_HEAD_DIM, (h + 1) * DN_HEAD_DIM)
                do, st, dsp, vn = dos_ref[rows, hs], sall_ref[c, h], ds_ref[h], vn_ref[rows, hs]
                da_ref[h, rows, diag] = _dot(do, vn, NT)
                da_ref[h, rows, other] = jnp.zeros((DN_CHUNK, DN_CHUNK), F32)
                du_ref[rows, hs] = (_dot(a_ref[h, rows, diag], do, TN) + _dot(kd_ref[rows, hs], dsp)).astype(du_ref.dtype)
                dqg_ref[rows, hs] = _dot(do, st, NT)
                dkd_ref[rows, hs] = _dot(vn, dsp, NT)
                ddl = jnp.sum(jnp.sum(dsp * st, axis=1, keepdims=True), axis=0, keepdims=True)
                ddl_ref[c, h] = jnp.broadcast_to(ddl, (8, 128))
            for h in range(DN_HEADS):
                hs = slice(h * DN_HEAD_DIM, (h + 1) * DN_HEAD_DIM)
                do, st, dvn = dos_ref[rows, hs], sall_ref[c, h], du_ref[rows, hs]
                dw_ref[rows, hs] = (-_dot(dvn, st, NT)).astype(dw_ref.dtype)
                ds_ref[h] = (ds_ref[h] * dl_ref[c, h][0:1, :] + _dot(qg_ref[rows, hs], do, TN)
                             - _dot(w_ref[rows, hs], dvn, TN))

    rows_step = SCAN_PAIRS * PAIR
    tok = pl.BlockSpec((rows_step, DN_W), lambda n: (rev(n), 0))
    hm = pl.BlockSpec((DN_HEADS, rows_step, PAIR), lambda n: (0, rev(n), 0))
    sc = pl.BlockSpec((2 * SCAN_PAIRS, DN_HEADS, 8, 128), lambda n: (rev(n), 0, 0, 0))
    return pl.pallas_call(
        body, name="dn_scan_bwd", grid=(nstep,),
        in_specs=[tok, tok, tok,
                  pl.BlockSpec((2 * SCAN_PAIRS, DN_HEADS, DN_HEAD_DIM, DN_HEAD_DIM), lambda n: (rev(n), 0, 0, 0)),
                  tok, tok, tok, hm, sc, pl.BlockSpec((rows_step, DN_W), lambda n: (rev(n), 3)), _full((1, 128))],
        out_specs=[tok] * 5 + [hm, sc, _full((8, 128))],
        out_shape=[jax.ShapeDtypeStruct((t, DN_W), F32)] + [jax.ShapeDtypeStruct((t, DN_W), MXU_DTYPE)] * 2
                  + [jax.ShapeDtypeStruct((t, DN_W), F32)] * 2 + [jax.ShapeDtypeStruct((DN_HEADS, t, PAIR), F32),
                   jax.ShapeDtypeStruct((2 * npair, DN_HEADS, 8, 128), F32), jax.ShapeDtypeStruct((8, 128), F32)],
        scratch_shapes=[pltpu.VMEM((DN_HEADS, DN_HEAD_DIM, DN_HEAD_DIM), F32), pltpu.VMEM((SCAN_PAIRS * PAIR, DN_W), F32)],
        compiler_params=_params("arbitrary"),
    )(dout, o, vnew, sall, w, qg, kd, a_qk, dlast, pd, dn_w)


def _dn_intra_bwd(qn, kn, vc, gc, gr, tinv, a_qk, du, dw, dqg, dkd, da_qk, ddlast, dlast, dep):
    t = qn.shape[0]
    npair = t // PAIR

    def body(q_ref, k_ref, v_ref, gc_ref, gr_ref, ti_ref, a_ref, du_ref, dw_ref, dqg_ref, dkd_ref, da_ref, ddl_ref, dl_ref,
             dep_ref, dq_ref, dk_ref, dv_ref, dg_ref):
        low, strict = _pair_masks()
        lane = lax.broadcasted_iota(jnp.int32, (PAIR, 128), 1)
        rowi = lax.broadcasted_iota(jnp.int32, (PAIR, 1), 0)
        rsum = lambda v: jnp.sum(v, axis=-1, keepdims=True)
        items = []
        for p in range(INTRA_PAIRS):
            rows = slice(p * PAIR, (p + 1) * PAIR)
            gc_v = gc_ref[rows, :]
            cum_c, cum_r = _pair_cumsums(gc_v, gr_ref[:, rows], low)
            for h in range(DN_HEADS):
                hs = slice(h * DN_HEAD_DIM, (h + 1) * DN_HEAD_DIM)
                items.append((p, h, rows, hs, _pair_gates(gc_v, cum_c, cum_r, low, h)))
        dtis, lmats, dvbs, dkbgs = [], [], [], []
        for p, h, rows, hs, (beta, gam, dm, gl) in items:
            k, tinv = k_ref[rows, hs], ti_ref[h, rows, :]
            kb = k * beta
            dtis.append(_dot(du_ref[rows, hs], v_ref[rows, hs] * beta, NT)
                        + _dot(dw_ref[rows, hs], kb * jnp.exp(gam), NT))
            lmats.append(jnp.where(strict, _dot(kb, k, NT) * dm, 0.0))
            dvbs.append(_dot(tinv, du_ref[rows, hs], TN))
            dkbgs.append(_dot(tinv, dw_ref[rows, hs], TN))
        xs = [_dot(ti_ref[h, rows, :], dti, TN) for (p, h, rows, hs, g), dti in zip(items, dtis)]
        dls = [jnp.where(strict, -_dot(x, ti_ref[h, rows, :], NT), 0.0) for (p, h, rows, hs, g), x in zip(items, xs)]
        dgam_all = [jnp.zeros((PAIR, 128), F32) for _ in range(INTRA_PAIRS)]
        dbeta_all = [jnp.zeros((PAIR, 128), F32) for _ in range(INTRA_PAIRS)]
        for (p, h, rows, hs, (beta, gam, dm, gl)), dl, lmat, dvb, dkbg in zip(items, dls, lmats, dvbs, dkbgs):
            q, k, v = q_ref[rows, hs], k_ref[rows, hs], v_ref[rows, hs]
            a = a_ref[h, rows, :]
            dqg, dkd = dqg_ref[rows, hs], dkd_ref[rows, hs]
            kb = k * beta
            eg = jnp.exp(gam)
            ekd = jnp.exp(gl - gam)
            dmm = dl * dm
            dam = jnp.where(low, da_ref[h, rows, :], 0.0)
            dn = dam * dm
            e = dl * lmat + dam * a
            dkb = _dot(dmm, k) + dkbg * eg
            dk_ref[rows, hs] = _dot(dmm, kb, TN) + _dot(dn, q, TN) + dkd * ekd + dkb * beta
            dq_ref[rows, hs] = _dot(dn, k) + dqg * eg
            dv_ref[rows, hs] = dvb * beta
            t_kd = rsum(dkd * (k * ekd))
            dgam = rsum(e) - rsum(e.T) + rsum(dqg * (q * eg)) + rsum(dkbg * (kb * eg)) - t_kd
            for c in range(2):
                crows = slice(c * DN_CHUNK, (c + 1) * DN_CHUNK)
                dgl = (jnp.sum(t_kd[crows, :], axis=0, keepdims=True)
                       + ddl_ref[2 * p + c, h][0:1, 0:1] * dl_ref[2 * p + c, h][0:1, 0:1])
                dgam = dgam + jnp.where(rowi == (c + 1) * DN_CHUNK - 1, dgl, 0.0)
            dgam_all[p] += jnp.where(lane == DN_HEADS + h, dgam, 0.0)
            dbeta_all[p] += jnp.where(lane == h, rsum(dkb * k) + rsum(dvb * v), 0.0)
        for p in range(INTRA_PAIRS):
            dg_ref[p * PAIR:(p + 1) * PAIR, :] = dbeta_all[p] + _dot(low.astype(F32), dgam_all[p], TN, HI)

    rows_step = INTRA_PAIRS * PAIR
    tok = lambda w: pl.BlockSpec((rows_step, w), lambda n: (n, 0))
    hm = pl.BlockSpec((DN_HEADS, rows_step, PAIR), lambda n: (0, n, 0))
    sc = pl.BlockSpec((2 * INTRA_PAIRS, DN_HEADS, 8, 128), lambda n: (n, 0, 0, 0))
    return pl.pallas_call(
        body, name="dn_intra_bwd", grid=(npair // INTRA_PAIRS,),
        in_specs=[tok(DN_W), tok(DN_W), tok(DN_W), tok(128), pl.BlockSpec((8, rows_step), lambda n: (0, n)), hm, hm,
                  tok(DN_W), tok(DN_W), tok(DN_W), tok(DN_W), hm, sc, sc, pl.BlockSpec(memory_space=pl.ANY)],
        out_specs=[tok(DN_W), tok(DN_W), tok(DN_W), tok(128)],
        out_shape=[jax.ShapeDtypeStruct((t, DN_W), F32)] * 3 + [jax.ShapeDtypeStruct((t, 128), F32)],
        compiler_params=_params("parallel"),
    )(qn, kn, vc, gc, gr, tinv, a_qk, du, dw, dqg, dkd, da_qk, ddlast, dlast, dep)


def _dn_prep_bwd(pd, conv_w, al_vec, dt_vec, dqn, dkn, dvc, dgc, dz, tm):
    t = pd.shape[0]
    nt = t // tm
    tile = lambda i: nt - 1 - i

    def body(cur_ref, prev_ref, ba_ref, cw_ref, al_ref, dt_ref, dq_ref, dk_ref, dv_ref, dg_ref, dz_ref,
             o_ref, accw_ref, accg_ref, xs_ref, dc_ref, ds_ref, carry_ref):
        @pl.when(pl.program_id(0) == 0)
        def _():
            accw_ref[...] = jnp.zeros_like(accw_ref)
            accg_ref[...] = jnp.zeros_like(accg_ref)
            carry_ref[...] = jnp.zeros_like(carry_ref)

        _conv_window(cur_ref, prev_ref, xs_ref, tm, tile(pl.program_id(0)) > 0)
        taps = [xs_ref[pl.ds(HALO - DN_CONV + 1 + j, tm), :] for j in range(DN_CONV)]
        y = cw_ref[0:1, :] * taps[0]
        for j in range(1, DN_CONV):
            y += cw_ref[j:j + 1, :] * taps[j]
        sg = _sigmoid(y)
        c = y * sg
        for h in range(DN_HEADS):
            qs = slice(h * DN_HEAD_DIM, (h + 1) * DN_HEAD_DIM)
            ksl = slice(DN_W + h * DN_HEAD_DIM, DN_W + (h + 1) * DN_HEAD_DIM)
            for src, sl, scale in ((dq_ref, qs, DN_SCALE), (dk_ref, ksl, 1.0)):
                xh = c[:, sl]
                r = lax.rsqrt(jnp.sum(xh * xh, axis=-1, keepdims=True) + EPS)
                unit = xh * r
                dn = src[:, qs] * scale
                dc_ref[:, sl] = r * (dn - unit * jnp.sum(dn * unit, axis=-1, keepdims=True))
        dc_ref[:, 2 * DN_W:3 * DN_W] = dv_ref[...]
        dy = dc_ref[...] * (sg + y * sg * (1.0 - sg))
        for j in range(DN_CONV):
            accw_ref[j:j + 1, :] += jnp.sum(dy * taps[j], axis=0, keepdims=True)
        ds_ref[0:tm, :] = dy
        ds_ref[tm:tm + HALO, :] = carry_ref[...]
        carry_ref[...] = ds_ref[0:HALO, :]
        dx = cw_ref[0:1, :] * ds_ref[pl.ds(DN_CONV - 1, tm), :]
        for j in range(1, DN_CONV):
            dx += cw_ref[j:j + 1, :] * ds_ref[pl.ds(DN_CONV - 1 - j, tm), :]

        beta, g, pre = _gate_values(ba_ref[...], al_ref[...], dt_ref[...])
        dgb = dg_ref[...]
        lane = lax.broadcasted_iota(jnp.int32, dgb.shape, 1)
        is_b, is_a = lane < DN_HEADS, (lane >= DN_HEADS) & (lane < 2 * DN_HEADS)
        dpre = dgb * (-jnp.exp(al_ref[...])) * _sigmoid(pre)
        dba = jnp.where(is_b, dgb * beta * (1.0 - beta), jnp.where(is_a, dpre, 0.0))
        accg_ref[0:1, :] += jnp.sum(jnp.where(is_a, dgb * g, 0.0), axis=0, keepdims=True)
        accg_ref[1:2, :] += jnp.sum(jnp.where(is_a, dpre, 0.0), axis=0, keepdims=True)
        o_ref[:, 0:CONV_CH] = dx.astype(o_ref.dtype)
        o_ref[:, CONV_CH:CONV_CH + DN_W] = dz_ref[...].astype(o_ref.dtype)
        o_ref[:, CONV_CH + DN_W:DN_COLS] = dba.astype(o_ref.dtype)

    tok = lambda w: pl.BlockSpec((tm, w), lambda i: (tile(i), 0))
    return pl.pallas_call(
        body, name="dn_prep_bwd", grid=(nt,),
        in_specs=_dn_prep_specs(tm, tile) + [tok(DN_W), tok(DN_W), tok(DN_W), tok(128), tok(DN_W)],
        out_specs=[tok(DN_COLS), _full((8, CONV_CH)), _full((8, 128))],
        out_shape=[jax.ShapeDtypeStruct((t, DN_COLS), MXU_DTYPE),
                   jax.ShapeDtypeStruct((8, CONV_CH), F32), jax.ShapeDtypeStruct((8, 128), F32)],
        scratch_shapes=[pltpu.VMEM((HALO + tm, CONV_CH), F32), pltpu.VMEM((tm, CONV_CH), F32),
                        pltpu.VMEM((tm + HALO, CONV_CH), F32), pltpu.VMEM((HALO, CONV_CH), F32)],
        compiler_params=_params("arbitrary"),
    )(pd, pd, pd, conv_w, al_vec, dt_vec, dqn, dkn, dvc, dgc, dz)


def _pad_lanes(v, offset=0):
    return jnp.zeros((1, 128), F32).at[0, offset:offset + v.shape[0]].set(v.astype(F32))


class _LocalReducer:
    def start(self, grads):
        return jnp.zeros((8, 128), F32)

    def middle(self, after):
        return jnp.zeros((8, 128), F32)

    def finish(self, after):
        return None


def _local_step(x, p, tgt, sm, w, late, reducer):
    t = x.shape[0]
    tm = min(512, t // 2)
    tm_s = min(512, t // 2)
    tw = min(1024, t // 2)

    w_in = w["w_in"]
    wa = w_in[:, :ATTN_Q + 2 * ATTN_KV]
    wd = jnp.pad(w_in[:, ATTN_Q + 2 * ATTN_KV:], ((0, 0), (0, DN_COLS - (D_IN - ATTN_Q - 2 * ATTN_KV))))
    conv_w = w["conv_w"]
    al_vec, dt_vec = _pad_lanes(sm["a_log"], DN_HEADS), _pad_lanes(sm["dt_bias"], DN_HEADS)
    sinks_vec = _pad_lanes(sm["sinks"])
    dn_w = sm["dn_norm"].reshape(1, 128)
    row = lambda v: v.reshape(1, D_MODEL)
    cos, sin = _rope_tables(t)

    u, pa, pd = _inproj(x, row(sm["norm_mix"]), wa, wd, tm_s)
    ao = _attn_fwd(pa, cos, sin, sinks_vec)
    qn, kn, vc, gc, gr = _dn_prep(pd, conv_w, al_vec, dt_vec, tm_s)
    uu, ww, qg, kd, a_qk, tinv, dlast = _dn_intra(qn, kn, vc, gc, gr)
    dn_out, o, vnew, sall = _dn_scan_fwd(uu, ww, qg, kd, a_qk, dlast, pd, dn_w)
    w_o, late_rest = late(dn_out)
    wo_a, wo_d = w_o[:ATTN_Q], w_o[ATTN_Q:]
    h1 = _oproj(x, ao, dn_out, wo_a, wo_d, tm)
    w = dict(w, **late_rest(h1))
    w_proj = jnp.transpose(w["w_proj4"], (1, 0, 2)).reshape(PLE_DIM, D_MODEL)
    m, r, h2 = _mlp_fwd(h1, row(sm["norm_mlp"]), w["w_up4"], w["w_down"], tw)
    dh2, dh2b, dgp, dpp, n3, pb, acc_ple = _ple_loss(h2, p, tgt, row(sm["norm_ple"]), row(sm["norm_final"]),
                                                     w["w_gate"], w_proj, tm_s)
    g_w_gate = _wgrad(n3, dgp, "wgrad_gate", D_MODEL, D_MODEL, tw)
    g_w_proj = _wgrad(pb, dpp, "wgrad_proj", PLE_DIM, D_MODEL, tw)
    da, dh1, dh1b, acc_mlp = _mlp_bwd(dh2, dh2b, r, h1, row(sm["norm_mlp"]), w["w_up4"], w["w_down"], tm)
    g_w_up4 = _wgrad(m, da, "wgrad_up", D_MODEL, FF_BLOCK, tw, stacked=True)
    g_w_down = _wgrad(r, dh2b, "wgrad_down", FF_BLOCK, D_MODEL, tw,
                      prep=lambda rv: jnp.square(rv.astype(F32)).astype(MXU_DTYPE))
    g_w_o = _wgrad_cat([ao, dn_out], [dh1b], "wgrad_o", tw)
    early = dict(w_up4=g_w_up4, w_down=g_w_down, w_gate=g_w_gate, w_proj=g_w_proj, w_o=g_w_o)
    dep = reducer.start(early)
    dao, ddn = _oproj_bwd(dh1b, wo_a, wo_d, tm, dep)
    dz, du, dw, dqg, dkd, da_qk, ddlast, acc_dn = _dn_scan_bwd(ddn, o, vnew, sall, ww, qg, kd, a_qk, dlast, pd, dn_w)
    dep = reducer.middle(du)
    dqn, dkn, dvc, dgc = _dn_intra_bwd(qn, kn, vc, gc, gr, tinv, a_qk, du, dw, dqg, dkd, da_qk, ddlast, dlast, dep)
    d_dn, acc_conv, acc_gate = _dn_prep_bwd(pd, conv_w, al_vec, dt_vec, dqn, dkn, dvc, dgc, dz, tm_s)
    dq, dk, dv, acc_attn = _attn_bwd(pa, cos, sin, sinks_vec, dao)
    reducer.finish(dq)
    wq, wk, wv = wa[:, :ATTN_Q], wa[:, ATTN_Q:ATTN_Q + ATTN_KV], wa[:, ATTN_Q + ATTN_KV:]
    dx, acc_mix = _inproj_bwd(x, dh1, row(sm["norm_mix"]), [dq, dk, dv, d_dn], [wq, wk, wv, wd], tm_s)

    g_w_in_t = _wgrad_cat([dq, dk, dv, d_dn], [u], "wgrad_in", tw)[:D_IN]
    grads = dict(early, w_in_t=g_w_in_t)
    sums = dict(loss=acc_ple[2, 0], norm_final=acc_ple[0], norm_ple=acc_ple[1], norm_mlp=acc_mlp[0], norm_mix=acc_mix[0],
                dn_norm=acc_dn[0], sinks=acc_attn[0, :ATTN_HEADS], a_log=acc_gate[0, DN_HEADS:2 * DN_HEADS],
                dt_bias=acc_gate[1, DN_HEADS:2 * DN_HEADS], conv_w=acc_conv[:DN_CONV])
    return sums, dx, grads


MESH = pl.DeviceIdType.MESH
ANY = pl.BlockSpec(memory_space=pl.ANY)
N_CHIPS = 4
N_DEV = 8


def _place():
    x, y, c = lax.axis_index("x"), lax.axis_index("y"), lax.axis_index("c")
    chips = [(1 - x, y), (x, 1 - y), (1 - x, 1 - y)]
    return x, y, c, chips


def _gather_weights(shards, conv_s):
    n = len(shards)
    per = 7

    def body(*refs):
        in_refs, conv_ref = refs[:n], refs[n]
        out_refs, conv_out = refs[n + 1:2 * n + 1], refs[2 * n + 1]
        send_sems, recv_sems = refs[2 * n + 2:]
        x, y, c, chips = _place()
        sibling = (x, y, 1 - c)

        def blk(a, px, py, pc):
            hr = in_refs[a].shape[0] // 2
            return out_refs[a].at[2 * px + py, pl.ds(pc * hr, hr), :]

        def mine(a):
            hr = in_refs[a].shape[0] // 2
            return in_refs[a].at[pl.ds(c * hr, hr), :]

        def rcopy(a, k, block, to, src=None):
            return pltpu.make_async_remote_copy(
                src_ref=blk(a, *block) if src is None else src, dst_ref=blk(a, *block),
                send_sem=send_sems.at[per * a + k], recv_sem=recv_sems.at[per * a + k],
                device_id=to, device_id_type=MESH)

        def whole(a, to):
            return pltpu.make_async_remote_copy(
                src_ref=in_refs[a], dst_ref=out_refs[a].at[2 * x + y],
                send_sem=send_sems.at[per * a], recv_sem=recv_sems.at[per * a], device_id=to, device_id_type=MESH)

        def ccopy(j, to):
            return pltpu.make_async_remote_copy(
                src_ref=conv_ref, dst_ref=conv_out.at[2 * x + y],
                send_sem=send_sems.at[per * n + j], recv_sem=recv_sems.at[per * n + j],
                device_id=to, device_id_type=MESH)

        started = []
        for a in range(n):
            first = [whole(a, sibling)]
            first += [rcopy(a, 1 + j, (x, y, c), (*chip, c), src=mine(a)) for j, chip in enumerate(chips)]
            for cp in first:
                cp.start()
            started += first
        conv_sends = [ccopy(j, (*chip, c)) for j, chip in enumerate(chips)] + [ccopy(3, sibling)]
        for cp in conv_sends:
            cp.start()
        started += conv_sends
        for a in range(n):
            for j, chip in enumerate(chips):
                rcopy(a, 1 + j, (*chip, c), (x, y, c)).wait_recv()
                fwd = rcopy(a, 4 + j, (*chip, c), sibling)
                fwd.start()
                started.append(fwd)
        for a in range(n):
            whole(a, sibling).wait_recv()
            for j, chip in enumerate(chips):
                rcopy(a, 4 + j, (*chip, 1 - c), (x, y, c)).wait_recv()
        for j, chip in enumerate(chips + [(x, y)]):
            pltpu.make_async_remote_copy(
                src_ref=conv_ref, dst_ref=conv_out.at[2 * chip[0] + chip[1]],
                send_sem=send_sems.at[per * n + j], recv_sem=recv_sems.at[per * n + j],
                device_id=sibling, device_id_type=MESH).wait_recv()
        for cp in started:
            cp.wait_send()

    nsem = per * n + 4
    out_shape = [jax.ShapeDtypeStruct((N_CHIPS,) + s.shape, s.dtype) for s in shards]
    out_shape.append(jax.ShapeDtypeStruct((N_CHIPS,) + conv_s.shape, conv_s.dtype))
    return pl.pallas_call(
        body, name="gather_weights", in_specs=[ANY] * (n + 1), out_specs=[ANY] * (n + 1), out_shape=out_shape,
        scratch_shapes=[pltpu.SemaphoreType.DMA((nsem,)), pltpu.SemaphoreType.DMA((nsem,))],
    )(*shards, conv_s)


HBM = pl.BlockSpec(memory_space=pltpu.HBM)
SEM = pl.BlockSpec(memory_space=pltpu.SEMAPHORE)
EFFECT = pltpu.SideEffectType.DATAFLOW_SIDE_EFFECTING
LATE_COPIES = 7


def _late_copies(in_refs, land_refs, send_sems, recv_sems, only=None):
    x, y, c, chips = _place()
    sends, arrivals = [], []
    for a, (src, land) in enumerate(zip(in_refs, land_refs)):
        if only is not None and a not in only:
            continue
        hr = src.shape[0] // 2
        base = LATE_COPIES * a

        def cp(src_ref, dst_ref, s_idx, r_idx, to):
            return pltpu.make_async_remote_copy(src_ref=src_ref, dst_ref=dst_ref, send_sem=send_sems.at[base + s_idx],
                                                recv_sem=recv_sems.at[base + r_idx], device_id=to, device_id_type=MESH)

        sends.append(cp(src, land.at[2 * x + y], 0, 0, (x, y, 1 - c)))
        arrivals.append(cp(src, land.at[2 * x + y], 0, 0, (x, y, 1 - c)))
        for j, chip in enumerate(chips):
            for pc in range(2):
                half = src.at[pl.ds(c * hr, hr), :]
                sends.append(cp(half, land.at[2 * x + y, pl.ds(c * hr, hr), :], 1 + 2 * j + pc, 1 + 2 * j + c, (*chip, pc)))
                arrivals.append(cp(half, land.at[2 * chip[0] + chip[1], pl.ds(pc * hr, hr), :], 1 + 2 * j + pc,
                                   1 + 2 * j + pc, (*chip, pc)))
    return sends, arrivals


def _copies_start(name, build, nsem, srcs, land_shapes, after):
    n = len(srcs)

    def body(*refs):
        sends, _ = build(refs[:n], refs[n:2 * n], refs[2 * n + 1], refs[2 * n + 2])
        for cp in sends:
            cp.start()
        refs[-1][...] = jnp.zeros_like(refs[-1])

    lands = [pltpu.with_memory_space_constraint(lax.empty(s.shape, s.dtype), pltpu.HBM) for s in land_shapes]
    ins = [pltpu.with_memory_space_constraint(s, pltpu.HBM) for s in srcs]
    out = pl.pallas_call(
        body, name=name,
        out_shape=(pltpu.SemaphoreType.DMA((nsem,)), pltpu.SemaphoreType.DMA((nsem,)),
                   *[pltpu.HBM(s.shape, s.dtype) for s in srcs], *[pltpu.HBM(s.shape, s.dtype) for s in land_shapes],
                   jax.ShapeDtypeStruct((8, 128), F32)),
        in_specs=[HBM] * (2 * n) + [ANY],
        out_specs=(SEM, SEM, *[HBM] * (2 * n), pl.BlockSpec(memory_space=pltpu.VMEM)),
        input_output_aliases={i: 2 + i for i in range(2 * n)},
        compiler_params=pltpu.CompilerParams(has_side_effects=EFFECT),
    )(*ins, *lands, after)
    return out[0], out[1], out[2:2 + n], out[2 + n:2 + 2 * n], out[-1]


def _copies_wait(name, build, started, after):
    send_sems, recv_sems, srcs, lands, _ = started
    n = len(srcs)

    def body(*refs):
        sends, arrivals = build(refs[:n], refs[n:2 * n], refs[2 * n], refs[2 * n + 1])
        for cp in sends:
            cp.wait_send()
        for cp in arrivals:
            cp.wait_recv()

    out = pl.pallas_call(
        body, name=name,
        out_shape=(*[pltpu.HBM(s.shape, s.dtype) for s in srcs], *[pltpu.HBM(l.shape, l.dtype) for l in lands]),
        in_specs=[HBM] * (2 * n) + [SEM, SEM, ANY],
        out_specs=tuple([HBM] * (2 * n)),
        input_output_aliases={i: i for i in range(2 * n)},
        compiler_params=pltpu.CompilerParams(has_side_effects=EFFECT),
    )(*srcs, *lands, send_sems, recv_sems, after)
    return out[:n], out[n:]


def _exchange_copies(g_refs, got_refs, send_sems, recv_sems):
    x, y, c, _ = _place()
    sends, arrivals = [], []
    for a, (g, got) in enumerate(zip(g_refs, got_refs)):
        hr = g.shape[1] // 2
        cp = pltpu.make_async_remote_copy(
            src_ref=g.at[:, pl.ds((1 - c) * hr, hr), :], dst_ref=got, send_sem=send_sems.at[a],
            recv_sem=recv_sems.at[a], device_id=(x, y, 1 - c), device_id_type=MESH)
        sends.append(cp)
        arrivals.append(cp)
    return sends, arrivals


def _scatter_copies(s_refs, got_refs, send_sems, recv_sems):
    x, y, c, chips = _place()
    sends, arrivals = [], []
    for a, (s16, got) in enumerate(zip(s_refs, got_refs)):
        for j, chip in enumerate(chips):
            cp = pltpu.make_async_remote_copy(
                src_ref=s16.at[2 * chip[0] + chip[1]], dst_ref=got.at[j], send_sem=send_sems.at[3 * a + j],
                recv_sem=recv_sems.at[3 * a + j], device_id=(*chip, c), device_id_type=MESH)
            sends.append(cp)
            arrivals.append(cp)
    return sends, arrivals


def _share_halves(name, bufs, dep):
    n = len(bufs)

    def body(*refs):
        out_refs = refs[n + 1:2 * n + 1]
        send_sems, recv_sems = refs[2 * n + 1:]
        x, y, c, _ = _place()
        remote = [pltpu.make_async_remote_copy(
            src_ref=out_refs[a].at[c], dst_ref=out_refs[a].at[c], send_sem=send_sems.at[a], recv_sem=recv_sems.at[a],
            device_id=(x, y, 1 - c), device_id_type=MESH) for a in range(n)]
        for cp in remote:
            cp.start()
        for a in range(n):
            pltpu.make_async_remote_copy(
                src_ref=out_refs[a].at[c], dst_ref=out_refs[a].at[1 - c], send_sem=send_sems.at[a],
                recv_sem=recv_sems.at[a], device_id=(x, y, 1 - c), device_id_type=MESH).wait_recv()
        for cp in remote:
            cp.wait_send()

    return pl.pallas_call(
        body, name=name, in_specs=[ANY] * (n + 1), out_specs=[ANY] * n,
        out_shape=[jax.ShapeDtypeStruct(b.shape, b.dtype) for b in bufs],
        input_output_aliases={a: a for a in range(n)},
        scratch_shapes=[pltpu.SemaphoreType.DMA((n,)), pltpu.SemaphoreType.DMA((n,))],
    )(*bufs, dep)


SMALL_ROWS, SMALL_COLS = 16, CONV_CH


def _allreduce_small(block):
    m_per, ncol = block.shape

    def body(x_ref, sum_ref, all_ref, send_sems, recv_sems, local_sem):
        x, y, c, chips = _place()
        me, sibling = (x, y, c), (x, y, 1 - c)

        def rows(px, py, pc):
            return all_ref.at[pl.ds((4 * px + 2 * py + pc) * m_per, m_per), :]

        def copy(k, block_of, to, src=None):
            return pltpu.make_async_remote_copy(
                src_ref=rows(*block_of) if src is None else src, dst_ref=rows(*block_of),
                send_sem=send_sems.at[k], recv_sem=recv_sems.at[k], device_id=to, device_id_type=MESH)

        mine = pltpu.make_async_copy(x_ref, rows(*me), local_sem)
        mine.start()
        first = [copy(0, me, sibling, src=x_ref)]
        first += [copy(1 + j, me, (*chip, c), src=x_ref) for j, chip in enumerate(chips)]
        for cp in first:
            cp.start()
        passed = [copy(4 + j, (*chip, c), sibling) for j, chip in enumerate(chips)]
        for j, chip in enumerate(chips):
            copy(1 + j, (*chip, c), me).wait_recv()
            passed[j].start()
        copy(0, sibling, me).wait_recv()
        for j, chip in enumerate(chips):
            copy(4 + j, (*chip, 1 - c), me).wait_recv()
        for cp in first + passed:
            cp.wait_send()
        mine.wait()
        total = all_ref[0:m_per, :]
        for d in range(1, N_DEV):
            total = total + all_ref[d * m_per:(d + 1) * m_per, :]
        sum_ref[...] = total

    vm = pl.BlockSpec(memory_space=pltpu.VMEM)
    return pl.pallas_call(
        body, name="allreduce_small", in_specs=[vm], out_specs=vm,
        out_shape=jax.ShapeDtypeStruct((m_per, ncol), F32),
        scratch_shapes=[pltpu.VMEM((N_DEV * m_per, ncol), F32), pltpu.SemaphoreType.DMA((7,)),
                        pltpu.SemaphoreType.DMA((7,)), pltpu.SemaphoreType.DMA],
    )(block)


def _row_tile(rows, cols):
    tile = rows
    while tile * cols * 4 > (1 << 20) and tile % 16 == 0:
        tile //= 2
    return tile


def _elementwise(fn, name, ins, out_dtypes, dep):
    rows, cols = ins[0].shape
    tile = _row_tile(rows, cols)

    def body(*refs):
        outs = fn(*[r[...] for r in refs[:len(ins)]])
        for o_ref, o in zip(refs[len(ins) + 1:], outs):
            o_ref[...] = o.astype(o_ref.dtype)

    if tile * cols * 4 > (1 << 21) and cols % 512 == 0:
        spec = pl.BlockSpec((rows, 256), lambda i: (0, i))
        steps = cols // 256
    else:
        spec = pl.BlockSpec((tile, cols), lambda i: (i, 0))
        steps = rows // tile
    return pl.pallas_call(
        body, name=name, grid=(steps,), in_specs=[spec] * len(ins) + [pl.BlockSpec(memory_space=pl.ANY)],
        out_specs=[spec] * len(out_dtypes),
        out_shape=[jax.ShapeDtypeStruct((rows, cols), d) for d in out_dtypes],
        compiler_params=_params("parallel"),
    )(*ins, dep)


def _adamw_tile(w, g, m, v):
    m = ADAM_B1 * m + (1.0 - ADAM_B1) * g
    v = ADAM_B2 * v + (1.0 - ADAM_B2) * jnp.square(g)
    m_hat = m / (1.0 - ADAM_B1 ** ADAM_STEP)
    v_hat = v / (1.0 - ADAM_B2 ** ADAM_STEP)
    delta = -ADAM_LR * (m_hat / (jnp.sqrt(v_hat) + ADAM_EPS) + ADAM_WD * w)
    return delta, m, v


def _adamw(name, w, g, m, v, dep):
    return _elementwise(_adamw_tile, name, [w, g, m, v], [F32, F32, F32], dep)


def _chip_sum(name, g4, got, place):
    nchip, hr, cols = got.shape
    tile = _row_tile(hr, cols)
    nblk = hr // tile

    def body(pl_ref, g_ref, o_ref, s32_ref, s16_ref):
        s = g_ref[...] + o_ref[...]
        s32_ref[...] = s
        s16_ref[...] = s.astype(BF16)

    spec = pl.BlockSpec((None, tile, cols), lambda k, i, pr: (k, i, 0))
    return pl.pallas_call(
        body, name=name,
        grid_spec=pltpu.PrefetchScalarGridSpec(
            num_scalar_prefetch=1, grid=(nchip, nblk),
            in_specs=[pl.BlockSpec((None, tile, cols), lambda k, i, pr: (k, pr[1] * nblk + i, 0)), spec],
            out_specs=[spec, spec]),
        out_shape=[jax.ShapeDtypeStruct(got.shape, F32), jax.ShapeDtypeStruct(got.shape, BF16)],
        compiler_params=_params("parallel", "parallel"),
    )(place, g4, got)


def _mesh_sum(name, s32, got, place):
    _, hr, cols = s32.shape
    tile = _row_tile(hr, cols)

    def body(pl_ref, own_ref, g0_ref, g1_ref, g2_ref, o_ref):
        o_ref[...] = ((own_ref[...] + g0_ref[...].astype(F32)) + g1_ref[...].astype(F32)) + g2_ref[...].astype(F32)

    slab = lambda j: pl.BlockSpec((None, tile, cols), lambda i, pr: (j, i, 0))
    return pl.pallas_call(
        body, name=name,
        grid_spec=pltpu.PrefetchScalarGridSpec(
            num_scalar_prefetch=1, grid=(hr // tile,),
            in_specs=[pl.BlockSpec((None, tile, cols), lambda i, pr: (pr[0], i, 0)), slab(0), slab(1), slab(2)],
            out_specs=pl.BlockSpec((None, tile, cols), lambda i, pr: (pr[1], i, 0))),
        out_shape=jax.ShapeDtypeStruct((2, hr, cols), F32),
        compiler_params=_params("parallel"),
    )(place, s32, got, got, got)


def _place_operand():
    return jnp.stack([2 * lax.axis_index("x") + lax.axis_index("y"), lax.axis_index("c")]).astype(jnp.int32)


W_IN_ROWS = 720


def _per_chip(name, g):
    if name == "w_in_t":
        slabs = g.reshape(N_CHIPS, D_IN // N_CHIPS, D_MODEL)
        return jnp.pad(slabs, ((0, 0), (0, W_IN_ROWS - D_IN // N_CHIPS), (0, 0)))
    if name == "w_proj":
        return jnp.transpose(g.reshape(PLE_DIM, N_CHIPS, D_MODEL // N_CHIPS), (1, 0, 2))
    if name == "w_up4":
        return g
    return g.reshape(N_CHIPS, g.shape[0] // N_CHIPS, g.shape[1])


class _EarlyReducer:
    def __init__(self, tag):
        self.tag = tag

    def start(self, grads):
        self.names = list(grads)
        self.place = _place_operand()
        slabs = [_per_chip(k, grads[k]) for k in self.names]
        halves = [jax.ShapeDtypeStruct((s.shape[0], s.shape[1] // 2, s.shape[2]), F32) for s in slabs]
        self.a = _copies_start(self.tag + "exchange_start", _exchange_copies, len(slabs), slabs, halves,
                               slabs[0][0, :8, :128])
        return self.a[-1]

    def middle(self, after):
        slabs, got = _copies_wait(self.tag + "exchange_wait", _exchange_copies, self.a, after)
        self.sums = [_chip_sum(self.tag + "chip_sum_" + k, s, g, self.place) for k, s, g in zip(self.names, slabs, got)]
        s16 = [s[1] for s in self.sums]
        lands = [jax.ShapeDtypeStruct((3,) + s.shape[1:], BF16) for s in s16]
        self.b = _copies_start(self.tag + "scatter_start", _scatter_copies, 3 * len(s16), s16, lands,
                               self.sums[0][0][0, :8, :128])
        return self.b[-1]

    def finish(self, after):
        _, got = _copies_wait(self.tag + "scatter_wait", _scatter_copies, self.b, after)
        self.bufs = {k: _mesh_sum(self.tag + "mesh_sum_" + k, s[0], g, self.place)
                     for k, s, g in zip(self.names, self.sums, got)}


def kernel(x, p, norm_mix, w_in, conv_w, a_log, dt_bias, dn_norm, sinks, w_o, norm_mlp, w_up, w_down, norm_ple, w_ple_gate, w_ple_proj, norm_final, loss_target, m_norm_mix, m_w_in, m_conv_w, m_a_log, m_dt_bias, m_dn_norm, m_sinks, m_w_o, m_norm_mlp, m_w_up, m_w_down, m_norm_ple, m_w_ple_gate, m_w_ple_proj, m_norm_final, v_norm_mix, v_w_in, v_conv_w, v_a_log, v_dt_bias, v_dn_norm, v_sinks, v_w_o, v_norm_mlp, v_w_up, v_w_down, v_norm_ple, v_w_ple_gate, v_w_ple_proj, v_norm_final):
    chip = 2 * lax.axis_index("x") + lax.axis_index("y")
    big = dict(w_in=w_in[0], w_o=w_o[0], w_up=w_up[0], w_down=w_down[0], w_gate=w_ple_gate[0], w_proj=w_ple_proj[0])
    big_m = dict(w_in=m_w_in[0], w_o=m_w_o[0], w_up=m_w_up[0], w_down=m_w_down[0], w_gate=m_w_ple_gate[0], w_proj=m_w_ple_proj[0])
    big_v = dict(w_in=v_w_in[0], w_o=v_w_o[0], w_up=v_w_up[0], w_down=v_w_down[0], w_gate=v_w_ple_gate[0], w_proj=v_w_ple_proj[0])
    names = list(big)

    w_in_all, conv_all = _gather_weights([big["w_in"].astype(BF16)], conv_w[0])
    late_names = names[1:]
    late_shards = [big[k].astype(BF16) for k in late_names]
    gather = _copies_start("gather_start", _late_copies, LATE_COPIES * len(late_shards), late_shards,
                           [jax.ShapeDtypeStruct((N_CHIPS,) + s.shape, BF16) for s in late_shards], w_in_all)
    token = gather[-1]
    w = dict(w_in=jnp.transpose(w_in_all, (1, 0, 2)).reshape(D_MODEL, D_IN),
             conv_w=jnp.transpose(conv_all, (1, 0, 2)).reshape(DN_CONV, CONV_CH))
    sm = dict(norm_mix=norm_mix[0] + token[0, 0], a_log=a_log[0], dt_bias=dt_bias[0], dn_norm=dn_norm[0],
              sinks=sinks[0], norm_mlp=norm_mlp[0], norm_ple=norm_ple[0], norm_final=norm_final)

    def late(after):
        first = functools.partial(_late_copies, only=(0,))
        srcs, lands = _copies_wait("gather_wait_o", first, gather, after)

        def rest(after2):
            others = functools.partial(_late_copies, only=tuple(range(1, len(late_names))))
            gw = dict(zip(late_names, _copies_wait("gather_wait_rest", others, gather[:2] + (srcs, lands, None), after2)[1]))
            return dict(w_up4=gw["w_up"], w_down=gw["w_down"].reshape(D_FF, D_MODEL),
                        w_gate=gw["w_gate"].reshape(D_MODEL, D_MODEL), w_proj4=gw["w_proj"])

        return lands[0].reshape(D_MODEL, D_MODEL), rest

    reducer = _EarlyReducer("early_")
    sums, grad_x, g = _local_step(x[0], p[0, 0], loss_target[0], sm, w, late, reducer)

    last = _EarlyReducer("last_")
    dep_a = last.start({"w_in_t": g["w_in_t"]})

    row = lambda v: jnp.zeros((SMALL_COLS,), F32).at[:v.shape[0]].set(v)
    misc = jnp.zeros((SMALL_COLS,), F32).at[0:4].set(sums["a_log"]).at[4:8].set(sums["dt_bias"]) \
        .at[8:16].set(sums["sinks"]).at[128:256].set(sums["dn_norm"]).at[256].set(sums["loss"])
    small = jnp.concatenate([sums["conv_w"], jnp.stack([row(sums["norm_mix"]), row(sums["norm_mlp"]), row(sums["norm_ple"]),
                                                        row(sums["norm_final"]), misc]),
                             jnp.zeros((SMALL_ROWS - 9, SMALL_COLS), F32)], axis=0)
    tot = _allreduce_small(small + dep_a[0, 0])
    dep_b = last.middle(tot)
    grad_key = dict(w_o="w_o", w_up="w_up4", w_down="w_down", w_gate="w_gate", w_proj="w_proj")
    full = _share_halves("share_halves", [reducer.bufs[grad_key[k]] for k in late_names], dep_b)
    red = {k: f.reshape(-1, f.shape[-1]) for k, f in zip(late_names, full)}
    loss = tot[8, 256]
    ncw = CONV_CH // N_CHIPS

    def pack(cw, nmix, nmlp, nple, nfin, al, dtb, sk, dnn):
        misc_p = jnp.zeros((SMALL_COLS,), F32).at[0:4].set(al).at[4:8].set(dtb).at[8:16].set(sk).at[128:256].set(dnn)
        cw_p = jnp.zeros((DN_CONV, SMALL_COLS), F32).at[:, :ncw].set(cw)
        return jnp.concatenate([cw_p, jnp.stack([row(nmix), row(nmlp), row(nple), row(nfin), misc_p]),
                                jnp.zeros((SMALL_ROWS - 9, SMALL_COLS), F32)], axis=0)

    def unpack(buf):
        return dict(conv_w=buf[0:4, :ncw][None], norm_mix=buf[4, :D_MODEL][None], norm_mlp=buf[5, :D_MODEL][None],
                    norm_ple=buf[6, :D_MODEL][None], norm_final=buf[7, :D_MODEL], a_log=buf[8, 0:4][None],
                    dt_bias=buf[8, 4:8][None], sinks=buf[8, 8:16][None], dn_norm=buf[8, 128:256][None])

    g_conv_shard = lax.dynamic_slice(tot[0:4], (0, chip * ncw), (DN_CONV, ncw))
    g_small = pack(g_conv_shard, tot[4, :D_MODEL], tot[5, :D_MODEL], tot[6, :D_MODEL], tot[7, :D_MODEL],
                   tot[8, 0:4], tot[8, 4:8], tot[8, 8:16], tot[8, 128:256])
    w_small = pack(conv_w[0], norm_mix[0], norm_mlp[0], norm_ple[0], norm_final, a_log[0], dt_bias[0], sinks[0], dn_norm[0])
    m_small = pack(m_conv_w[0], m_norm_mix[0], m_norm_mlp[0], m_norm_ple[0], m_norm_final, m_a_log[0], m_dt_bias[0],
                   m_sinks[0], m_dn_norm[0])
    v_small = pack(v_conv_w[0], v_norm_mix[0], v_norm_mlp[0], v_norm_ple[0], v_norm_final, v_a_log[0], v_dt_bias[0],
                   v_sinks[0], v_dn_norm[0])

    ref_name = dict(w_in="w_in", w_o="w_o", w_up="w_up", w_down="w_down", w_gate="w_ple_gate", w_proj="w_ple_proj")
    out_g, out_d, out_m, out_v = {}, {}, {}, {}

    def update(k, dep):
        d_k, m_k, v_k = _adamw("adamw_" + k, big[k], red[k], big_m[k], big_v[k], dep)
        out_g[ref_name[k]], out_d[ref_name[k]] = red[k][None], d_k[None]
        out_m[ref_name[k]], out_v[ref_name[k]] = m_k[None], v_k[None]
        return d_k

    for k in late_names:
        done = update(k, dep_b)
    small_out = _adamw("adamw_small", w_small, g_small, m_small, v_small, dep_b)
    d_s, m_s, v_s = (unpack(b) for b in small_out)
    g_s = unpack(g_small)
    for src, dst in ((g_s, out_g), (d_s, out_d), (m_s, out_m), (v_s, out_v)):
        dst.update(src)
    last.finish(done + small_out[0][0:1, 0:1])
    (w_in_full,) = _share_halves("share_halves_w_in", [last.bufs["w_in_t"]], dep_b)
    g_t = w_in_full.reshape(W_IN_ROWS, D_MODEL)[:D_IN // N_CHIPS]
    d_t, m_t, v_t = _adamw("adamw_w_in", big["w_in"].T, g_t, big_m["w_in"].T, big_v["w_in"].T, dep_b)
    out_g["w_in"], out_d["w_in"], out_m["w_in"], out_v["w_in"] = g_t.T[None], d_t.T[None], m_t.T[None], v_t.T[None]
    order = ["norm_mix", "w_in", "conv_w", "a_log", "dt_bias", "dn_norm", "sinks", "w_o", "norm_mlp", "w_up", "w_down",
             "norm_ple", "w_ple_gate", "w_ple_proj", "norm_final"]
    return (loss, grad_x[None], *[out_g[k] for k in order], *[out_d[k] for k in order],
            *[out_m[k] for k in order], *[out_v[k] for k in order])
```

```python
import functools

import jax
import jax.numpy as jnp
from jax import lax
from jax.experimental import pallas as pl
from jax.experimental.pallas import tpu as pltpu

F32 = jnp.float32
BF16 = jnp.bfloat16
MXU_DTYPE = jnp.bfloat16
HI = lax.Precision.HIGHEST

D_MODEL = 1024
PLE_DIM = 256
ATTN_HEADS = 8
ATTN_KV_HEADS = 2
ATTN_GROUPS = ATTN_HEADS // ATTN_KV_HEADS
ATTN_HEAD_DIM = 64
ATTN_BLOCK = 128
ROPE_THETA = 10000.0
DN_HEADS = 4
DN_HEAD_DIM = 128
DN_CONV = 4
DN_CHUNK = 64
D_FF = 4 * D_MODEL
EPS = 1e-6
ATTN_Q = ATTN_HEADS * ATTN_HEAD_DIM
ATTN_KV = ATTN_KV_HEADS * ATTN_HEAD_DIM
DN_W = DN_HEADS * DN_HEAD_DIM
CONV_CH = 3 * DN_W
D_IN = ATTN_Q + 2 * ATTN_KV + 4 * DN_W + 2 * DN_HEADS
DN_COLS = 4 * DN_W + 128
DN_SCALE = DN_HEAD_DIM ** -0.5
ATTN_SCALE = ATTN_HEAD_DIM ** -0.5
FF_BLOCKS = 4
FF_BLOCK = D_FF // FF_BLOCKS

ADAM_LR = 0.001
ADAM_B1 = 0.9
ADAM_B2 = 0.999
ADAM_EPS = 1e-08
ADAM_WD = 0.01
ADAM_STEP = 10

V7X_VMEM_BYTES = 64 * 1024 * 1024
VMEM_LIMIT = 48 * 1024 * 1024

NN = ((1,), (0,))
NT = ((1,), (1,))
TN = ((0,), (0,))


def _dot(a, b, dims=NN, prec=None):
    if a.dtype != b.dtype:
        a, b = a.astype(MXU_DTYPE), b.astype(MXU_DTYPE)
    return lax.dot_general(a, b, (dims, ((), ())), precision=prec, preferred_element_type=F32)


def _sigmoid(x):
    return 1.0 / (1.0 + jnp.exp(-x))


def _softplus(x):
    return jnp.maximum(x, 0.0) + jnp.log(1.0 + jnp.exp(-jnp.abs(x)))


def _params(*sem):
    return pltpu.CompilerParams(dimension_semantics=sem, vmem_limit_bytes=VMEM_LIMIT)


def _rms_fwd(xv, g):
    r = lax.rsqrt(jnp.mean(xv * xv, axis=-1, keepdims=True) + EPS)
    return xv * r * g


def _rms_bwd(xv, g, dn):
    r = lax.rsqrt(jnp.mean(xv * xv, axis=-1, keepdims=True) + EPS)
    xh = xv * r
    dg = jnp.sum(dn * xh, axis=0, keepdims=True)
    dxh = dn * g
    dx = r * (dxh - xh * jnp.mean(dxh * xh, axis=-1, keepdims=True))
    return dx, dg


def _full(shape):
    return pl.BlockSpec(shape, lambda *_: (0,) * len(shape))


def _inproj(x, g_mix, wa_t, wd_t, tm):
    t = x.shape[0]

    def body(x_ref, g_ref, wa_ref, wd_ref, u_ref, pa_ref, pd_ref):
        u = _rms_fwd(x_ref[...], g_ref[...]).astype(MXU_DTYPE)
        u_ref[...] = u
        pa_ref[...] = _dot(u, wa_ref[...], NT)
        pd_ref[...] = _dot(u, wd_ref[...], NT)

    na, nd = wa_t.shape[0], wd_t.shape[0]
    return pl.pallas_call(
        body, name="inproj", grid=(t // tm,),
        in_specs=[pl.BlockSpec((tm, D_MODEL), lambda i: (i, 0)), _full((1, D_MODEL)),
                  _full((na, D_MODEL)), _full((nd, D_MODEL))],
        out_specs=[pl.BlockSpec((tm, D_MODEL), lambda i: (i, 0)), pl.BlockSpec((tm, na), lambda i: (i, 0)),
                   pl.BlockSpec((tm, nd), lambda i: (i, 0))],
        out_shape=[jax.ShapeDtypeStruct((t, D_MODEL), MXU_DTYPE), jax.ShapeDtypeStruct((t, na), F32),
                   jax.ShapeDtypeStruct((t, nd), F32)],
        compiler_params=_params("parallel"),
    )(x, g_mix, wa_t, wd_t)


def _oproj(x, ao, dn, wo_a, wo_d, tm):
    t = x.shape[0]

    def body(x_ref, ao_ref, dn_ref, wa_ref, wd_ref, h_ref):
        h_ref[...] = (x_ref[...] + _dot(ao_ref[...].astype(MXU_DTYPE), wa_ref[...])
                      + _dot(dn_ref[...].astype(MXU_DTYPE), wd_ref[...]))

    half = ao.shape[1]
    return pl.pallas_call(
        body, name="oproj", grid=(t // tm,),
        in_specs=[pl.BlockSpec((tm, D_MODEL), lambda i: (i, 0)), pl.BlockSpec((tm, half), lambda i: (i, 0)),
                  pl.BlockSpec((tm, half), lambda i: (i, 0)), _full((half, D_MODEL)), _full((half, D_MODEL))],
        out_specs=pl.BlockSpec((tm, D_MODEL), lambda i: (i, 0)),
        out_shape=jax.ShapeDtypeStruct((t, D_MODEL), F32),
        compiler_params=_params("parallel"),
    )(x, ao, dn, wo_a, wo_d)


def _mlp_fwd(h1, g_mlp, w_up4, w_down, tm):
    t = h1.shape[0]

    def body(h_ref, g_ref, wu_ref, wd_ref, m_ref, r_ref, h2_ref, acc_ref):
        k = pl.program_id(1)

        @pl.when(k == 0)
        def _():
            m_ref[...] = _rms_fwd(h_ref[...], g_ref[...]).astype(MXU_DTYPE)
            acc_ref[...] = jnp.zeros_like(acc_ref)

        r = jnp.maximum(_dot(m_ref[...], wu_ref[...]), 0.0)
        r_ref[...] = r.astype(MXU_DTYPE)
        s = jnp.square(r).astype(MXU_DTYPE)
        acc_ref[...] += _dot(s, wd_ref[...])

        @pl.when(k == FF_BLOCKS - 1)
        def _():
            h2_ref[...] = h_ref[...] + acc_ref[...]

    return pl.pallas_call(
        body, name="mlp_fwd", grid=(t // tm, FF_BLOCKS),
        in_specs=[pl.BlockSpec((tm, D_MODEL), lambda i, k: (i, 0)), _full((1, D_MODEL)),
                  pl.BlockSpec((None, D_MODEL, FF_BLOCK), lambda i, k: (k, 0, 0)),
                  pl.BlockSpec((FF_BLOCK, D_MODEL), lambda i, k: (k, 0))],
        out_specs=[pl.BlockSpec((tm, D_MODEL), lambda i, k: (i, 0)), pl.BlockSpec((tm, FF_BLOCK), lambda i, k: (i, k)),
                   pl.BlockSpec((tm, D_MODEL), lambda i, k: (i, 0))],
        out_shape=[jax.ShapeDtypeStruct((t, D_MODEL), MXU_DTYPE), jax.ShapeDtypeStruct((t, D_FF), MXU_DTYPE),
                   jax.ShapeDtypeStruct((t, D_MODEL), F32)],
        scratch_shapes=[pltpu.VMEM((tm, D_MODEL), F32)],
        compiler_params=_params("parallel", "arbitrary"),
    )(h1, g_mlp, w_up4, w_down)


def _ple_loss(h2, p, tgt, g_ple, g_fin, w_gate, w_proj, tm):
    t = h2.shape[0]

    def body(h_ref, p_ref, t_ref, gp_ref, gf_ref, wg_ref, wp_ref,
             dh_ref, dhb_ref, dgp_ref, dpp_ref, n3_ref, pb_ref, acc_ref):
        @pl.when(pl.program_id(0) == 0)
        def _():
            acc_ref[...] = jnp.zeros_like(acc_ref)

        h = h_ref[...]
        g_ple_v, g_fin_v = gp_ref[...], gf_ref[...]
        n3 = _rms_fwd(h, g_ple_v).astype(MXU_DTYPE)
        n3_ref[...] = n3
        gate = _sigmoid(_dot(n3, wg_ref[...]))
        pb = p_ref[...].astype(MXU_DTYPE)
        pb_ref[...] = pb
        pp = _dot(pb, wp_ref[...])
        h3 = h + gate * pp
        r4 = lax.rsqrt(jnp.mean(h3 * h3, axis=-1, keepdims=True) + EPS)
        xh4 = h3 * r4
        e = xh4 * g_fin_v - t_ref[...]
        loss = 0.5 * jnp.sum(jnp.mean(e * e, axis=-1, keepdims=True), axis=0, keepdims=True)
        dy = e * (1.0 / D_MODEL)
        dg_fin = jnp.sum(dy * xh4, axis=0, keepdims=True)
        dxh = dy * g_fin_v
        dh3 = r4 * (dxh - xh4 * jnp.mean(dxh * xh4, axis=-1, keepdims=True))
        dpp_ref[...] = (dh3 * gate).astype(MXU_DTYPE)
        dgp = (dh3 * pp * gate * (1.0 - gate)).astype(MXU_DTYPE)
        dgp_ref[...] = dgp
        dn3 = _dot(dgp, wg_ref[...], NT)
        dx, dg_ple = _rms_bwd(h, g_ple_v, dn3)
        dh2 = dh3 + dx
        dh_ref[...] = dh2
        dhb_ref[...] = dh2.astype(MXU_DTYPE)
        acc_ref[0:1, :] += dg_fin
        acc_ref[1:2, :] += dg_ple
        acc_ref[2:3, :] += jnp.broadcast_to(loss, (1, D_MODEL))

    row = lambda w: pl.BlockSpec((tm, w), lambda i: (i, 0))
    return pl.pallas_call(
        body, name="ple_loss", grid=(t // tm,),
        in_specs=[row(D_MODEL), row(PLE_DIM), row(D_MODEL), _full((1, D_MODEL)), _full((1, D_MODEL)),
                  _full((D_MODEL, D_MODEL)), _full((PLE_DIM, D_MODEL))],
        out_specs=[row(D_MODEL), row(D_MODEL), row(D_MODEL), row(D_MODEL), row(D_MODEL), row(PLE_DIM),
                   _full((8, D_MODEL))],
        out_shape=[jax.ShapeDtypeStruct((t, D_MODEL), F32), jax.ShapeDtypeStruct((t, D_MODEL), MXU_DTYPE),
                   jax.ShapeDtypeStruct((t, D_MODEL), MXU_DTYPE), jax.ShapeDtypeStruct((t, D_MODEL), MXU_DTYPE),
                   jax.ShapeDtypeStruct((t, D_MODEL), MXU_DTYPE), jax.ShapeDtypeStruct((t, PLE_DIM), MXU_DTYPE),
                   jax.ShapeDtypeStruct((8, D_MODEL), F32)],
        compiler_params=_params("arbitrary"),
    )(h2, p, tgt, g_ple, g_fin, w_gate, w_proj)


def _mlp_bwd(dh2, dh2b, r, h1, g_mlp, w_up4, w_down, tm):
    t = h1.shape[0]

    def body(dh_ref, dhb_ref, r_ref, h_ref, g_ref, wu_ref, wd_ref,
             da_ref, dh1_ref, dh1b_ref, acc_ref, dm_ref):
        i, k = pl.program_id(0), pl.program_id(1)

        @pl.when((i == 0) & (k == 0))
        def _():
            acc_ref[...] = jnp.zeros_like(acc_ref)

        @pl.when(k == 0)
        def _():
            dm_ref[...] = jnp.zeros_like(dm_ref)

        ds = _dot(dhb_ref[...], wd_ref[...], NT)
        da = (ds * (2.0 * r_ref[...].astype(F32))).astype(MXU_DTYPE)
        da_ref[...] = da
        dm_ref[...] += _dot(da, wu_ref[...], NT)

        @pl.when(k == FF_BLOCKS - 1)
        def _():
            dx, dg = _rms_bwd(h_ref[...], g_ref[...], dm_ref[...])
            dh1 = dh_ref[...] + dx
            dh1_ref[...] = dh1
            dh1b_ref[...] = dh1.astype(MXU_DTYPE)
            acc_ref[0:1, :] += dg

    tok = lambda w: pl.BlockSpec((tm, w), lambda i, k: (i, 0))
    return pl.pallas_call(
        body, name="mlp_bwd", grid=(t // tm, FF_BLOCKS),
        in_specs=[tok(D_MODEL), tok(D_MODEL), pl.BlockSpec((tm, FF_BLOCK), lambda i, k: (i, k)), tok(D_MODEL),
                  _full((1, D_MODEL)), pl.BlockSpec((None, D_MODEL, FF_BLOCK), lambda i, k: (k, 0, 0)),
                  pl.BlockSpec((FF_BLOCK, D_MODEL), lambda i, k: (k, 0))],
        out_specs=[pl.BlockSpec((tm, FF_BLOCK), lambda i, k: (i, k)),
                   tok(D_MODEL), tok(D_MODEL), pl.BlockSpec((8, D_MODEL), lambda i, k: (0, 0))],
        out_shape=[jax.ShapeDtypeStruct((t, D_FF), MXU_DTYPE),
                   jax.ShapeDtypeStruct((t, D_MODEL), F32), jax.ShapeDtypeStruct((t, D_MODEL), MXU_DTYPE),
                   jax.ShapeDtypeStruct((8, D_MODEL), F32)],
        scratch_shapes=[pltpu.VMEM((tm, D_MODEL), F32)],
        compiler_params=_params("arbitrary", "arbitrary"),
    )(dh2, dh2b, r, h1, g_mlp, w_up4, w_down)


def _oproj_bwd(dh1b, wo_a, wo_d, tm, dep):
    t = dh1b.shape[0]
    half = wo_a.shape[0]

    def body(d_ref, wa_ref, wd_ref, dep_ref, da_ref, dd_ref):
        d = d_ref[...]
        da_ref[...] = _dot(d, wa_ref[...], NT)
        dd_ref[...] = _dot(d, wd_ref[...], NT)

    return pl.pallas_call(
        body, name="oproj_bwd", grid=(t // tm,),
        in_specs=[pl.BlockSpec((tm, D_MODEL), lambda i: (i, 0)), _full((half, D_MODEL)), _full((half, D_MODEL)),
                  pl.BlockSpec(memory_space=pl.ANY)],
        out_specs=[pl.BlockSpec((tm, half), lambda i: (i, 0)), pl.BlockSpec((tm, half), lambda i: (i, 0))],
        out_shape=[jax.ShapeDtypeStruct((t, half), F32), jax.ShapeDtypeStruct((t, half), F32)],
        compiler_params=_params("parallel"),
    )(dh1b, wo_a, wo_d, dep)


def _inproj_bwd(x, dh1, g_mix, grads, weights, tm):
    t = x.shape[0]
    n = len(grads)

    def body(*refs):
        x_ref, dh_ref, g_ref = refs[:3]
        g_refs, w_refs = refs[3:3 + n], refs[3 + n:3 + 2 * n]
        dx_ref, acc_ref = refs[3 + 2 * n:]

        @pl.when(pl.program_id(0) == 0)
        def _():
            acc_ref[...] = jnp.zeros_like(acc_ref)

        du = _dot(g_refs[0][...], w_refs[0][...])
        for j in range(1, n):
            du += _dot(g_refs[j][...], w_refs[j][...])
        dx, dg = _rms_bwd(x_ref[...], g_ref[...], du)
        dx_ref[...] = dh_ref[...] + dx
        acc_ref[0:1, :] += dg

    tok = lambda w: pl.BlockSpec((tm, w), lambda i: (i, 0))
    return pl.pallas_call(
        body, name="inproj_bwd", grid=(t // tm,),
        in_specs=[tok(D_MODEL), tok(D_MODEL), _full((1, D_MODEL))] + [tok(g.shape[1]) for g in grads]
                 + [_full(w.shape) for w in weights],
        out_specs=[tok(D_MODEL), _full((8, D_MODEL))],
        out_shape=[jax.ShapeDtypeStruct((t, D_MODEL), F32), jax.ShapeDtypeStruct((8, D_MODEL), F32)],
        compiler_params=_params("arbitrary"),
    )(x, dh1, g_mix, *grads, *weights)


def _wgrad(a, b, name, tk, tn, tt, stacked=False, prep=None):
    t, kdim = a.shape
    ncols = b.shape[1]

    def body(a_ref, b_ref, o_ref):
        @pl.when(pl.program_id(2) == 0)
        def _():
            o_ref[...] = jnp.zeros_like(o_ref)

        av = a_ref[...] if prep is None else prep(a_ref[...])
        o_ref[...] += _dot(av, b_ref[...], TN)

    if stacked:
        out_spec = pl.BlockSpec((None, tk, tn), lambda i, j, s: (j, i, 0))
        out_shape = jax.ShapeDtypeStruct((ncols // tn, kdim, tn), F32)
    else:
        out_spec = pl.BlockSpec((tk, tn), lambda i, j, s: (i, j))
        out_shape = jax.ShapeDtypeStruct((kdim, ncols), F32)
    return pl.pallas_call(
        body, name=name, grid=(kdim // tk, ncols // tn, t // tt),
        in_specs=[pl.BlockSpec((tt, tk), lambda i, j, s: (s, i)), pl.BlockSpec((tt, tn), lambda i, j, s: (s, j))],
        out_specs=out_spec, out_shape=out_shape,
        compiler_params=_params("parallel", "parallel", "arbitrary"),
    )(a, b)


def _wgrad_cat(as_, bs, name, tt):
    t = as_[0].shape[0]
    heights = [a.shape[1] for a in as_]
    widths = [b.shape[1] for b in bs]

    def body(*refs):
        a_refs, b_refs, o_ref = refs[:len(as_)], refs[len(as_):-1], refs[-1]

        @pl.when(pl.program_id(0) == 0)
        def _():
            o_ref[...] = jnp.zeros_like(o_ref)

        row = 0
        for a_ref, k in zip(a_refs, heights):
            av = a_ref[...]
            col = 0
            for b_ref, n in zip(b_refs, widths):
                o_ref[row:row + k, col:col + n] += _dot(av, b_ref[...], TN)
                col += n
            row += k

    tok = lambda w: pl.BlockSpec((tt, w), lambda s: (s, 0))
    shape = (sum(heights), sum(widths))
    return pl.pallas_call(
        body, name=name, grid=(t // tt,),
        in_specs=[tok(k) for k in heights] + [tok(n) for n in widths],
        out_specs=_full(shape), out_shape=jax.ShapeDtypeStruct(shape, F32),
        compiler_params=_params("arbitrary"),
    )(*as_, *bs)


def _rope_tables(t):
    half = ATTN_HEAD_DIM // 2
    inv = 1.0 / (ROPE_THETA ** (jnp.arange(half, dtype=F32) * (2.0 / ATTN_HEAD_DIM)))
    ang = jnp.arange(t, dtype=F32)[:, None] * inv[None, :]
    cos, sin = jnp.cos(ang), jnp.sin(ang)
    cos2 = jnp.concatenate([cos, cos], axis=-1)
    sin2 = jnp.concatenate([-sin, sin], axis=-1)
    return jnp.tile(cos2, (1, 2)), jnp.tile(sin2, (1, 2))


def _swap_halves(tv):
    w = tv.shape[-1]
    lane = lax.broadcasted_iota(jnp.int32, tv.shape, tv.ndim - 1)
    first = (lane % ATTN_HEAD_DIM) < (ATTN_HEAD_DIM // 2)
    return jnp.where(first, pltpu.roll(tv, w - ATTN_HEAD_DIM // 2, tv.ndim - 1),
                     pltpu.roll(tv, ATTN_HEAD_DIM // 2, tv.ndim - 1))


def _rope(tv, cos, sin):
    return tv * cos + _swap_halves(tv) * sin


def _rope_bwd(dv, cos, sin):
    return dv * cos + _swap_halves(dv * sin)


def _attn_valid(first_block):
    c = lax.broadcasted_iota(jnp.int32, (2 * ATTN_BLOCK, ATTN_BLOCK), 0)
    r = lax.broadcasted_iota(jnp.int32, (2 * ATTN_BLOCK, ATTN_BLOCK), 1)
    return (c > r) & (c <= r + ATTN_BLOCK) & ((c >= ATTN_BLOCK) | jnp.logical_not(first_block))


def _attn_probs(st, sink, valid):
    s = jnp.where(valid, st * ATTN_SCALE, -jnp.inf)
    m = jnp.maximum(jnp.max(s, axis=0, keepdims=True), sink)
    e = jnp.where(valid, jnp.exp(s - m), 0.0)
    es = jnp.exp(sink - m)
    inv = 1.0 / (jnp.sum(e, axis=0, keepdims=True) + es)
    return e * inv, es * inv


def _lane_scalar(vec, idx):
    lane = lax.broadcasted_iota(jnp.int32, vec.shape, 1)
    return jnp.sum(jnp.where(lane == idx, vec, 0.0), axis=-1, keepdims=True)


def _attn_specs(nb):
    cur = lambda w, cb: pl.BlockSpec((ATTN_BLOCK, w), lambda i: (jnp.minimum(i, nb - 1), cb))
    prev = lambda w, cb: pl.BlockSpec((ATTN_BLOCK, w), lambda i: (jnp.maximum(jnp.minimum(i, nb - 1) - 1, 0), cb))
    kcol, vcol = ATTN_Q // ATTN_KV, ATTN_Q // ATTN_KV + 1
    return [cur(ATTN_Q, 0), cur(ATTN_KV, kcol), prev(ATTN_KV, kcol), cur(ATTN_KV, vcol), prev(ATTN_KV, vcol),
            cur(ATTN_KV, 0), cur(ATTN_KV, 0), prev(ATTN_KV, 0), prev(ATTN_KV, 0), _full((1, 128))]


def _attn_fwd(pa, cos, sin, sinks_vec):
    t = pa.shape[0]
    nb = t // ATTN_BLOCK

    def body(q_ref, kc_ref, kp_ref, vc_ref, vp_ref, cc_ref, sc_ref, cp_ref, sp_ref, sk_ref, o_ref):
        first = pl.program_id(0) == 0
        cc, sc = cc_ref[...], sc_ref[...]
        q = _rope(q_ref[...], jnp.tile(cc, (1, ATTN_Q // ATTN_KV)), jnp.tile(sc, (1, ATTN_Q // ATTN_KV)))
        kc = _rope(kc_ref[...], cc, sc)
        kp = _rope(kp_ref[...], cp_ref[...], sp_ref[...])
        vc, vp = vc_ref[...], vp_ref[...]
        sk = sk_ref[...]
        valid = _attn_valid(first)
        kv = lambda tp, tc, hk: jnp.concatenate([tp[:, hk * ATTN_HEAD_DIM:(hk + 1) * ATTN_HEAD_DIM],
                                                 tc[:, hk * ATTN_HEAD_DIM:(hk + 1) * ATTN_HEAD_DIM]], axis=0)
        kwins = [kv(kp, kc, hk) for hk in range(ATTN_KV_HEADS)]
        vwins_t = [kv(vp, vc, hk).T for hk in range(ATTN_KV_HEADS)]
        heads = [slice(h * ATTN_HEAD_DIM, (h + 1) * ATTN_HEAD_DIM) for h in range(ATTN_HEADS)]
        scores = [_dot(kwins[h // ATTN_GROUPS], q[:, hs], NT) for h, hs in enumerate(heads)]
        probs = [_attn_probs(st, _lane_scalar(sk, h), valid)[0] for h, st in enumerate(scores)]
        for h, (hs, pt) in enumerate(zip(heads, probs)):
            o_ref[:, hs] = _dot(vwins_t[h // ATTN_GROUPS], pt).T.astype(o_ref.dtype)

    return pl.pallas_call(
        body, name="attn_fwd", grid=(nb,),
        in_specs=_attn_specs(nb),
        out_specs=pl.BlockSpec((ATTN_BLOCK, ATTN_Q), lambda i: (i, 0)),
        out_shape=jax.ShapeDtypeStruct((t, ATTN_Q), MXU_DTYPE),
        compiler_params=_params("parallel"),
    )(pa, pa, pa, pa, pa, cos, sin, cos, sin, sinks_vec)


def _attn_bwd(pa, cos, sin, sinks_vec, dao):
    t = pa.shape[0]
    nb = t // ATTN_BLOCK

    def body(q_ref, kc_ref, kp_ref, vc_ref, vp_ref, cc_ref, sc_ref, cp_ref, sp_ref, sk_ref, do_ref,
             dq_ref, dk_ref, dv_ref, acc_ref, dqr_ref, dkw_ref, dvw_ref, ck_ref, cv_ref):
        i = pl.program_id(0)

        @pl.when(i == 0)
        def _():
            acc_ref[...] = jnp.zeros_like(acc_ref)
            ck_ref[...] = jnp.zeros_like(ck_ref)
            cv_ref[...] = jnp.zeros_like(cv_ref)

        @pl.when(i < nb)
        def _():
            first = i == 0
            cc, sc = cc_ref[...], sc_ref[...]
            cq, sq = jnp.tile(cc, (1, ATTN_Q // ATTN_KV)), jnp.tile(sc, (1, ATTN_Q // ATTN_KV))
            q = _rope(q_ref[...], cq, sq)
            kc = _rope(kc_ref[...], cc, sc)
            kp = _rope(kp_ref[...], cp_ref[...], sp_ref[...])
            vc, vp = vc_ref[...], vp_ref[...]
            sk = sk_ref[...]
            do = do_ref[...]
            lane = lax.broadcasted_iota(jnp.int32, (1, 128), 1)
            dsink = jnp.zeros((1, 128), F32)
            valid = _attn_valid(first)
            kv = lambda tp, tc, hk: jnp.concatenate([tp[:, hk * ATTN_HEAD_DIM:(hk + 1) * ATTN_HEAD_DIM],
                                                     tc[:, hk * ATTN_HEAD_DIM:(hk + 1) * ATTN_HEAD_DIM]], axis=0)
            kwins = [kv(kp, kc, hk) for hk in range(ATTN_KV_HEADS)]
            vwins = [kv(vp, vc, hk) for hk in range(ATTN_KV_HEADS)]
            kwins_t = [kw.T for kw in kwins]
            heads = [slice(h * ATTN_HEAD_DIM, (h + 1) * ATTN_HEAD_DIM) for h in range(ATTN_HEADS)]
            scores = [_dot(kwins[h // ATTN_GROUPS], q[:, hs], NT) for h, hs in enumerate(heads)]
            dps = [_dot(vwins[h // ATTN_GROUPS], do[:, hs], NT) for h, hs in enumerate(heads)]
            pts, dsts = [], []
            for h, (st, dp_t) in enumerate(zip(scores, dps)):
                probs_t, psink = _attn_probs(st, _lane_scalar(sk, h), valid)
                delta = jnp.sum(probs_t * dp_t, axis=0, keepdims=True)
                pts.append(probs_t)
                dsts.append(probs_t * (dp_t - delta) * ATTN_SCALE)
                dsink += jnp.where(lane == h, jnp.sum(-psink * delta, axis=1, keepdims=True), 0.0)
            for h, (hs, ds_t) in enumerate(zip(heads, dsts)):
                dqr_ref[:, hs] = _dot(kwins_t[h // ATTN_GROUPS], ds_t).T
            for hk in range(ATTN_KV_HEADS):
                ks = slice(hk * ATTN_HEAD_DIM, (hk + 1) * ATTN_HEAD_DIM)
                group = range(hk * ATTN_GROUPS, (hk + 1) * ATTN_GROUPS)
                ds_g = jnp.concatenate([dsts[h] for h in group], axis=1)
                p_g = jnp.concatenate([pts[h] for h in group], axis=1)
                q_g = jnp.concatenate([q[:, heads[h]] for h in group], axis=0)
                do_g = jnp.concatenate([do[:, heads[h]] for h in group], axis=0)
                dkw_ref[:, ks] = _dot(ds_g, q_g)
                dvw_ref[:, ks] = _dot(p_g, do_g)
            acc_ref[0:1, :] += dsink
            dq_ref[...] = _rope_bwd(dqr_ref[...], cq, sq).astype(dq_ref.dtype)
            dk_ref[...] = (ck_ref[...] + _rope_bwd(dkw_ref[0:ATTN_BLOCK, :], cp_ref[...], sp_ref[...])).astype(dk_ref.dtype)
            dv_ref[...] = (cv_ref[...] + dvw_ref[0:ATTN_BLOCK, :]).astype(dv_ref.dtype)
            ck_ref[...] = _rope_bwd(dkw_ref[ATTN_BLOCK:2 * ATTN_BLOCK, :], cc, sc)
            cv_ref[...] = dvw_ref[ATTN_BLOCK:2 * ATTN_BLOCK, :]

        @pl.when(i == nb)
        def _():
            dk_ref[...] = ck_ref[...].astype(dk_ref.dtype)
            dv_ref[...] = cv_ref[...].astype(dv_ref.dtype)

    prev_out = lambda w: pl.BlockSpec((ATTN_BLOCK, w), lambda i: (jnp.maximum(i - 1, 0), 0))
    return pl.pallas_call(
        body, name="attn_bwd", grid=(nb + 1,),
        in_specs=_attn_specs(nb) + [pl.BlockSpec((ATTN_BLOCK, ATTN_Q), lambda i: (jnp.minimum(i, nb - 1), 0))],
        out_specs=[pl.BlockSpec((ATTN_BLOCK, ATTN_Q), lambda i: (jnp.minimum(i, nb - 1), 0)), prev_out(ATTN_KV),
                   prev_out(ATTN_KV), _full((8, 128))],
        out_shape=[jax.ShapeDtypeStruct((t, ATTN_Q), MXU_DTYPE), jax.ShapeDtypeStruct((t, ATTN_KV), MXU_DTYPE),
                   jax.ShapeDtypeStruct((t, ATTN_KV), MXU_DTYPE), jax.ShapeDtypeStruct((8, 128), F32)],
        scratch_shapes=[pltpu.VMEM((ATTN_BLOCK, ATTN_Q), F32), pltpu.VMEM((2 * ATTN_BLOCK, ATTN_KV), F32),
                        pltpu.VMEM((2 * ATTN_BLOCK, ATTN_KV), F32), pltpu.VMEM((ATTN_BLOCK, ATTN_KV), F32),
                        pltpu.VMEM((ATTN_BLOCK, ATTN_KV), F32)],
        compiler_params=_params("arbitrary"),
    )(pa, pa, pa, pa, pa, cos, sin, cos, sin, sinks_vec, dao)


PAIR = 2 * DN_CHUNK
INTRA_PAIRS = 4
SCAN_PAIRS = 4
HALO = 8


def _conv_window(cur_ref, prev_ref, xs_ref, tm, has_prev):
    prev = jnp.where(has_prev, prev_ref[...], 0.0)
    xs_ref[0:HALO, :] = prev
    xs_ref[HALO:HALO + tm, :] = cur_ref[...]


def _conv_taps(xs_ref, cw_ref, tm):
    y = cw_ref[0:1, :] * xs_ref[pl.ds(HALO - DN_CONV + 1, tm), :]
    for j in range(1, DN_CONV):
        y += cw_ref[j:j + 1, :] * xs_ref[pl.ds(HALO - DN_CONV + 1 + j, tm), :]
    return y


def _gate_values(ba, al, dt):
    beta = _sigmoid(ba)
    pre = ba + dt
    g = -jnp.exp(al) * _softplus(pre)
    return beta, g, pre


def _dn_prep_specs(tm, tile):
    return [pl.BlockSpec((tm, CONV_CH), lambda i: (tile(i), 0)),
            pl.BlockSpec((HALO, CONV_CH), lambda i: (jnp.maximum(tile(i) * (tm // HALO) - 1, 0), 0)),
            pl.BlockSpec((tm, 128), lambda i: (tile(i), 4 * DN_W // 128)),
            _full((DN_CONV, CONV_CH)), _full((1, 128)), _full((1, 128))]


def _dn_prep(pd, conv_w, al_vec, dt_vec, tm):
    t = pd.shape[0]

    def body(cur_ref, prev_ref, ba_ref, cw_ref, al_ref, dt_ref, qn_ref, kn_ref, vc_ref, gc_ref, gr_ref, xs_ref):
        _conv_window(cur_ref, prev_ref, xs_ref, tm, pl.program_id(0) > 0)
        y = _conv_taps(xs_ref, cw_ref, tm)
        c = y * _sigmoid(y)
        for h in range(DN_HEADS):
            qs = slice(h * DN_HEAD_DIM, (h + 1) * DN_HEAD_DIM)
            ksl = slice(DN_W + h * DN_HEAD_DIM, DN_W + (h + 1) * DN_HEAD_DIM)
            qh, kh = c[:, qs], c[:, ksl]
            qn_ref[:, qs] = qh * lax.rsqrt(jnp.sum(qh * qh, axis=-1, keepdims=True) + EPS) * DN_SCALE
            kn_ref[:, qs] = kh * lax.rsqrt(jnp.sum(kh * kh, axis=-1, keepdims=True) + EPS)
        vc_ref[...] = c[:, 2 * DN_W:3 * DN_W]
        beta, g, _ = _gate_values(ba_ref[...], al_ref[...], dt_ref[...])
        lane = lax.broadcasted_iota(jnp.int32, beta.shape, 1)
        gb = jnp.where(lane < DN_HEADS, beta, jnp.where(lane < 2 * DN_HEADS, g, 0.0))
        gc_ref[...] = gb
        gr_ref[...] = gb.T[0:8, :]

    tok = lambda w: pl.BlockSpec((tm, w), lambda i: (i, 0))
    return pl.pallas_call(
        body, name="dn_prep", grid=(t // tm,),
        in_specs=_dn_prep_specs(tm, lambda i: i),
        out_specs=[tok(DN_W), tok(DN_W), tok(DN_W), tok(128), pl.BlockSpec((8, tm), lambda i: (0, i))],
        out_shape=[jax.ShapeDtypeStruct((t, DN_W), F32)] * 3 + [jax.ShapeDtypeStruct((t, 128), F32),
                                                                 jax.ShapeDtypeStruct((8, t), F32)],
        scratch_shapes=[pltpu.VMEM((HALO + tm, CONV_CH), F32)],
        compiler_params=_params("parallel"),
    )(pd, pd, pd, conv_w, al_vec, dt_vec)


def _pair_masks():
    r = lax.broadcasted_iota(jnp.int32, (PAIR, PAIR), 0)
    c = lax.broadcasted_iota(jnp.int32, (PAIR, PAIR), 1)
    same = (r < DN_CHUNK) == (c < DN_CHUNK)
    return same & (r >= c), same & (r > c)


def _lane_col(mat, idx):
    lane = lax.broadcasted_iota(jnp.int32, mat.shape, 1)
    return jnp.sum(jnp.where(lane == idx, mat, 0.0), axis=-1, keepdims=True)


def _pair_cumsums(gc, gr, low):
    lowf = low.astype(F32)
    return _dot(lowf, gc, NN, HI), _dot(gr, lowf, NT, HI)


def _pair_gates(gc, cum_c, cum_r, low, h):
    beta = _lane_col(gc, h)
    gam = _lane_col(cum_c, DN_HEADS + h)
    gam_row = cum_r[DN_HEADS + h:DN_HEADS + h + 1, :]
    dm = jnp.where(low, jnp.exp(jnp.where(low, gam - gam_row, 0.0)), 0.0)
    row = lax.broadcasted_iota(jnp.int32, gam.shape, 0)
    gl = jnp.where(row < DN_CHUNK, gam[DN_CHUNK - 1:DN_CHUNK, :], gam[PAIR - 1:PAIR, :])
    return beta, gam, dm, gl


def _split(a):
    hi = a.astype(BF16)
    return hi, (a - hi.astype(F32)).astype(BF16)


def _dot_split(a, b, dims=NN):
    (ah, al), (bh, bl) = a, b
    la, lb = (1, 1) if dims == TN else ((0, 1) if dims == NN else (0, 0))
    r = _dot(jnp.concatenate([ah, al], axis=la), jnp.concatenate([bh, bl], axis=lb), dims)
    m, n = r.shape[0] // 2, r.shape[1] // 2
    return (r[m:, n:] + (r[:m, n:] + r[m:, :n])) + r[:m, :n]


def _unit_lower_inverses(lmats):
    n = lmats[0].shape[0]
    r = lax.broadcasted_iota(jnp.int32, (n, n), 0)
    c = lax.broadcasted_iota(jnp.int32, (n, n), 1)
    same = lambda size: (r & ~(size - 1)) == (c & ~(size - 1))
    base = DN_CHUNK // 4
    diag = [jnp.where(same(base), l, 0.0) for l in lmats]
    accs = [(r == c).astype(F32) - d for d in diag]
    splits = [_split(d) for d in diag]
    step = 1
    while 2 * step < base:
        splits = [_split(_dot_split(s, s)) for s in splits]
        accs = [acc + _dot_split(_split(acc), s) for acc, s in zip(accs, splits)]
        step *= 2
    size = base
    while size < DN_CHUNK:
        below = same(2 * size) & jnp.logical_not(same(size))
        tb = [_dot(acc, jnp.where(below, l, 0.0)) for acc, l in zip(accs, lmats)]
        accs = [acc - _dot(t, acc) for acc, t in zip(accs, tb)]
        size *= 2
    return accs


def _dn_intra(qn, kn, vc, gc, gr):
    t = qn.shape[0]
    npair = t // PAIR
    rows_step = INTRA_PAIRS * PAIR

    def body(q_ref, k_ref, v_ref, gc_ref, gr_ref, u_ref, w_ref, qg_ref, kd_ref, a_ref, ti_ref, dl_ref):
        low, strict = _pair_masks()
        items = []
        for p in range(INTRA_PAIRS):
            rows = slice(p * PAIR, (p + 1) * PAIR)
            gc_v = gc_ref[rows, :]
            cum_c, cum_r = _pair_cumsums(gc_v, gr_ref[:, rows], low)
            for h in range(DN_HEADS):
                hs = slice(h * DN_HEAD_DIM, (h + 1) * DN_HEAD_DIM)
                items.append((p, h, rows, hs, _pair_gates(gc_v, cum_c, cum_r, low, h)))
        lmats = []
        for p, h, rows, hs, (beta, gam, dm, gl) in items:
            k = k_ref[rows, hs]
            lmats.append(jnp.where(strict, _dot(k * beta, k, NT) * dm, 0.0))
        tinvs = _unit_lower_inverses(lmats)
        for (p, h, rows, hs, (beta, gam, dm, gl)), tinv in zip(items, tinvs):
            q, k, v = q_ref[rows, hs], k_ref[rows, hs], v_ref[rows, hs]
            eg = jnp.exp(gam)
            u_ref[rows, hs] = _dot(tinv, v * beta)
            w_ref[rows, hs] = _dot(tinv, (k * beta) * eg).astype(w_ref.dtype)
            a_ref[h, rows, :] = _dot(q, k, NT) * dm
            ti_ref[h, rows, :] = tinv
            qg_ref[rows, hs] = (q * eg).astype(qg_ref.dtype)
            kd_ref[rows, hs] = (k * jnp.exp(gl - gam)).astype(kd_ref.dtype)
            for c in range(2):
                last = (c + 1) * DN_CHUNK - 1
                dl_ref[2 * p + c, h] = jnp.broadcast_to(jnp.exp(gam[last:last + 1, :]), (8, 128))

    tok = lambda w: pl.BlockSpec((rows_step, w), lambda n: (n, 0))
    hm = pl.BlockSpec((DN_HEADS, rows_step, PAIR), lambda n: (0, n, 0))
    return pl.pallas_call(
        body, name="dn_intra", grid=(npair // INTRA_PAIRS,),
        in_specs=[tok(DN_W), tok(DN_W), tok(DN_W), tok(128), pl.BlockSpec((8, rows_step), lambda n: (0, n))],
        out_specs=[tok(DN_W)] * 4 + [hm, hm, pl.BlockSpec((2 * INTRA_PAIRS, DN_HEADS, 8, 128), lambda n: (n, 0, 0, 0))],
        out_shape=[jax.ShapeDtypeStruct((t, DN_W), F32)] + [jax.ShapeDtypeStruct((t, DN_W), MXU_DTYPE)] * 3
                  + [jax.ShapeDtypeStruct((DN_HEADS, t, PAIR), F32)] * 2
                  + [jax.ShapeDtypeStruct((2 * npair, DN_HEADS, 8, 128), F32)],
        compiler_params=_params("parallel"),
    )(qn, kn, vc, gc, gr)


def _dn_scan_fwd(u, w, qg, kd, a_qk, dlast, pd, dn_w):
    t = u.shape[0]
    npair = t // PAIR

    def body(u_ref, w_ref, qg_ref, kd_ref, a_ref, dl_ref, z_ref, nw_ref, out_ref, o_ref, vn_ref, sall_ref, s_ref):
        @pl.when(pl.program_id(0) == 0)
        def _():
            s_ref[...] = jnp.zeros_like(s_ref)

        nw = nw_ref[...]
        for c in range(2 * SCAN_PAIRS):
            rows = slice(c * DN_CHUNK, (c + 1) * DN_CHUNK)
            diag = slice((c % 2) * DN_CHUNK, (c % 2 + 1) * DN_CHUNK)
            for h in range(DN_HEADS):
                hs = slice(h * DN_HEAD_DIM, (h + 1) * DN_HEAD_DIM)
                st = s_ref[h]
                sall_ref[c, h] = st
                vn_ref[rows, hs] = (u_ref[rows, hs] - _dot(w_ref[rows, hs], st)).astype(vn_ref.dtype)
            for h in range(DN_HEADS):
                hs = slice(h * DN_HEAD_DIM, (h + 1) * DN_HEAD_DIM)
                st, vn = s_ref[h], vn_ref[rows, hs]
                o = _dot(qg_ref[rows, hs], st) + _dot(a_ref[h, rows, diag], vn)
                s_ref[h] = st * dl_ref[c, h][0:1, :] + _dot(kd_ref[rows, hs], vn, TN)
                o_ref[rows, hs] = o
                z = z_ref[rows, hs]
                on = o * lax.rsqrt(jnp.mean(o * o, axis=-1, keepdims=True) + EPS) * nw
                out_ref[rows, hs] = (on * (z * _sigmoid(z))).astype(out_ref.dtype)

    rows_step = SCAN_PAIRS * PAIR
    tok = pl.BlockSpec((rows_step, DN_W), lambda n: (n, 0))
    hm = pl.BlockSpec((DN_HEADS, rows_step, PAIR), lambda n: (0, n, 0))
    return pl.pallas_call(
        body, name="dn_scan_fwd", grid=(npair // SCAN_PAIRS,),
        in_specs=[tok, tok, tok, tok, hm, pl.BlockSpec((2 * SCAN_PAIRS, DN_HEADS, 8, 128), lambda n: (n, 0, 0, 0)),
                  pl.BlockSpec((rows_step, DN_W), lambda n: (n, 3)), _full((1, 128))],
        out_specs=[tok, tok, tok,
                   pl.BlockSpec((2 * SCAN_PAIRS, DN_HEADS, DN_HEAD_DIM, DN_HEAD_DIM), lambda n: (n, 0, 0, 0))],
        out_shape=[jax.ShapeDtypeStruct((t, DN_W), MXU_DTYPE), jax.ShapeDtypeStruct((t, DN_W), F32),
                   jax.ShapeDtypeStruct((t, DN_W), MXU_DTYPE),
                   jax.ShapeDtypeStruct((2 * npair, DN_HEADS, DN_HEAD_DIM, DN_HEAD_DIM), F32)],
        scratch_shapes=[pltpu.VMEM((DN_HEADS, DN_HEAD_DIM, DN_HEAD_DIM), F32)],
        compiler_params=_params("arbitrary"),
    )(u, w, qg, kd, a_qk, dlast, pd, dn_w)


def _dn_scan_bwd(dout, o, vnew, sall, w, qg, kd, a_qk, dlast, pd, dn_w):
    t = o.shape[0]
    npair = t // PAIR
    nstep = npair // SCAN_PAIRS
    rev = lambda n: nstep - 1 - n

    def body(do_ref, o_ref, vn_ref, sall_ref, w_ref, qg_ref, kd_ref, a_ref, dl_ref, z_ref, nw_ref,
             dz_ref, du_ref, dw_ref, dqg_ref, dkd_ref, da_ref, ddl_ref, acc_ref, ds_ref, dos_ref):
        @pl.when(pl.program_id(0) == 0)
        def _():
            ds_ref[...] = jnp.zeros_like(ds_ref)
            acc_ref[...] = jnp.zeros_like(acc_ref)

        nw = nw_ref[...]
        dnw = jnp.zeros((1, 128), F32)
        for h in range(DN_HEADS):
            hs = slice(h * DN_HEAD_DIM, (h + 1) * DN_HEAD_DIM)
            o, z, dout = o_ref[:, hs], z_ref[:, hs], do_ref[:, hs]
            r = lax.rsqrt(jnp.mean(o * o, axis=-1, keepdims=True) + EPS)
            oh = o * r
            sz = _sigmoid(z)
            dz_ref[:, hs] = dout * (oh * nw) * (sz + z * sz * (1.0 - sz))
            don = dout * (z * sz)
            dnw += jnp.sum(don * oh, axis=0, keepdims=True)
            doh = don * nw
            dos_ref[:, hs] = r * (doh - oh * jnp.mean(doh * oh, axis=-1, keepdims=True))
        acc_ref[0:1, :] += dnw
        for c in reversed(range(2 * SCAN_PAIRS)):
            rows = slice(c * DN_CHUNK, (c + 1) * DN_CHUNK)
            diag = slice((c % 2) * DN_CHUNK, (c % 2 + 1) * DN_CHUNK)
            other = slice((1 - c % 2) * DN_CHUNK, (2 - c % 2) * DN_CHUNK)
            for h in range(DN_HEADS):
                hs = slice(h * DN_HEAD_DIM, (h + 1) * DN_HEAD_DIM)
                do, st, dsp, vn = dos_ref[rows, hs], sall_ref[c, h], ds_ref[h], vn_ref[rows, hs]
                da_ref[h, rows, diag] = _dot(do, vn, NT)
                da_ref[h, rows, other] = jnp.zeros((DN_CHUNK, DN_CHUNK), F32)
                du_ref[rows, hs] = (_dot(a_ref[h, rows, diag], do, TN) + _dot(kd_ref[rows, hs], dsp)).astype(du_ref.dtype)
                dqg_ref[rows, hs] = _dot(do, st, NT)
                dkd_ref[rows, hs] = _dot(vn, dsp, NT)
                ddl = jnp.sum(jnp.sum(dsp * st, axis=1, keepdims=True), axis=0, keepdims=True)
                ddl_ref[c, h] = jnp.broadcast_to(ddl, (8, 128))
            for h in range(DN_HEADS):
                hs = slice(h * DN_HEAD_DIM, (h + 1) * DN_HEAD_DIM)
                do, st, dvn = dos_ref[rows, hs], sall_ref[c, h], du_ref[rows, hs]
                dw_ref[rows, hs] = (-_dot(dvn, st, NT)).astype(dw_ref.dtype)
                ds_ref[h] = (ds_ref[h] * dl_ref[c, h][0:1, :] + _dot(qg_ref[rows, hs], do, TN)
                             - _dot(w_ref[rows, hs], dvn, TN))

    rows_step = SCAN_PAIRS * PAIR
    tok = pl.BlockSpec((rows_step, DN_W), lambda n: (rev(n), 0))
    hm = pl.BlockSpec((DN_HEADS, rows_step, PAIR), lambda n: (0, rev(n), 0))
    sc = pl.BlockSpec((2 * SCAN_PAIRS, DN_HEADS, 8, 128), lambda n: (rev(n), 0, 0, 0))
    return pl.pallas_call(
        body, name="dn_scan_bwd", grid=(nstep,),
        in_specs=[tok, tok, tok,
                  pl.BlockSpec((2 * SCAN_PAIRS, DN_HEADS, DN_HEAD_DIM, DN_HEAD_DIM), lambda n: (rev(n), 0, 0, 0)),
                  tok, tok, tok, hm, sc, pl.BlockSpec((rows_step, DN_W), lambda n: (rev(n), 3)), _full((1, 128))],
        out_specs=[tok] * 5 + [hm, sc, _full((8, 128))],
        out_shape=[jax.ShapeDtypeStruct((t, DN_W), F32)] + [jax.ShapeDtypeStruct((t, DN_W), MXU_DTYPE)] * 2
                  + [jax.ShapeDtypeStruct((t, DN_W), F32)] * 2 + [jax.ShapeDtypeStruct((DN_HEADS, t, PAIR), F32),
                   jax.ShapeDtypeStruct((2 * npair, DN_HEADS, 8, 128), F32), jax.ShapeDtypeStruct((8, 128), F32)],
        scratch_shapes=[pltpu.VMEM((DN_HEADS, DN_HEAD_DIM, DN_HEAD_DIM), F32), pltpu.VMEM((SCAN_PAIRS * PAIR, DN_W), F32)],
        compiler_params=_params("arbitrary"),
    )(dout, o, vnew, sall, w, qg, kd, a_qk, dlast, pd, dn_w)


def _dn_intra_bwd(qn, kn, vc, gc, gr, tinv, a_qk, du, dw, dqg, dkd, da_qk, ddlast, dlast, dep):
    t = qn.shape[0]
    npair = t // PAIR

    def body(q_ref, k_ref, v_ref, gc_ref, gr_ref, ti_ref, a_ref, du_ref, dw_ref, dqg_ref, dkd_ref, da_ref, ddl_ref, dl_ref,
             dep_ref, dq_ref, dk_ref, dv_ref, dg_ref):
        low, strict = _pair_masks()
        lane = lax.broadcasted_iota(jnp.int32, (PAIR, 128), 1)
        rowi = lax.broadcasted_iota(jnp.int32, (PAIR, 1), 0)
        rsum = lambda v: jnp.sum(v, axis=-1, keepdims=True)
        items = []
        for p in range(INTRA_PAIRS):
            rows = slice(p * PAIR, (p + 1) * PAIR)
            gc_v = gc_ref[rows, :]
            cum_c, cum_r = _pair_cumsums(gc_v, gr_ref[:, rows], low)
            for h in range(DN_HEADS):
                hs = slice(h * DN_HEAD_DIM, (h + 1) * DN_HEAD_DIM)
                items.append((p, h, rows, hs, _pair_gates(gc_v, cum_c, cum_r, low, h)))
        dtis, lmats, dvbs, dkbgs = [], [], [], []
        for p, h, rows, hs, (beta, gam, dm, gl) in items:
            k, tinv = k_ref[rows, hs], ti_ref[h, rows, :]
            kb = k * beta
            dtis.append(_dot(du_ref[rows, hs], v_ref[rows, hs] * beta, NT)
                        + _dot(dw_ref[rows, hs], kb * jnp.exp(gam), NT))
            lmats.append(jnp.where(strict, _dot(kb, k, NT) * dm, 0.0))
            dvbs.append(_dot(tinv, du_ref[rows, hs], TN))
            dkbgs.append(_dot(tinv, dw_ref[rows, hs], TN))
        xs = [_dot(ti_ref[h, rows, :], dti, TN) for (p, h, rows, hs, g), dti in zip(items, dtis)]
        dls = [jnp.where(strict, -_dot(x, ti_ref[h, rows, :], NT), 0.0) for (p, h, rows, hs, g), x in zip(items, xs)]
        dgam_all = [jnp.zeros((PAIR, 128), F32) for _ in range(INTRA_PAIRS)]
        dbeta_all = [jnp.zeros((PAIR, 128), F32) for _ in range(INTRA_PAIRS)]
        for (p, h, rows, hs, (beta, gam, dm, gl)), dl, lmat, dvb, dkbg in zip(items, dls, lmats, dvbs, dkbgs):
            q, k, v = q_ref[rows, hs], k_ref[rows, hs], v_ref[rows, hs]
            a = a_ref[h, rows, :]
            dqg, dkd = dqg_ref[rows, hs], dkd_ref[rows, hs]
            kb = k * beta
            eg = jnp.exp(gam)
            ekd = jnp.exp(gl - gam)
            dmm = dl * dm
            dam = jnp.where(low, da_ref[h, rows, :], 0.0)
            dn = dam * dm
            e = dl * lmat + dam * a
            dkb = _dot(dmm, k) + dkbg * eg
            dk_ref[rows, hs] = _dot(dmm, kb, TN) + _dot(dn, q, TN) + dkd * ekd + dkb * beta
            dq_ref[rows, hs] = _dot(dn, k) + dqg * eg
            dv_ref[rows, hs] = dvb * beta
            t_kd = rsum(dkd * (k * ekd))
            dgam = rsum(e) - rsum(e.T) + rsum(dqg * (q * eg)) + rsum(dkbg * (kb * eg)) - t_kd
            for c in range(2):
                crows = slice(c * DN_CHUNK, (c + 1) * DN_CHUNK)
                dgl = (jnp.sum(t_kd[crows, :], axis=0, keepdims=True)
                       + ddl_ref[2 * p + c, h][0:1, 0:1] * dl_ref[2 * p + c, h][0:1, 0:1])
                dgam = dgam + jnp.where(rowi == (c + 1) * DN_CHUNK - 1, dgl, 0.0)
            dgam_all[p] += jnp.where(lane == DN_HEADS + h, dgam, 0.0)
            dbeta_all[p] += jnp.where(lane == h, rsum(dkb * k) + rsum(dvb * v), 0.0)
        for p in range(INTRA_PAIRS):
            dg_ref[p * PAIR:(p + 1) * PAIR, :] = dbeta_all[p] + _dot(low.astype(F32), dgam_all[p], TN, HI)

    rows_step = INTRA_PAIRS * PAIR
    tok = lambda w: pl.BlockSpec((rows_step, w), lambda n: (n, 0))
    hm = pl.BlockSpec((DN_HEADS, rows_step, PAIR), lambda n: (0, n, 0))
    sc = pl.BlockSpec((2 * INTRA_PAIRS, DN_HEADS, 8, 128), lambda n: (n, 0, 0, 0))
    return pl.pallas_call(
        body, name="dn_intra_bwd", grid=(npair // INTRA_PAIRS,),
        in_specs=[tok(DN_W), tok(DN_W), tok(DN_W), tok(128), pl.BlockSpec((8, rows_step), lambda n: (0, n)), hm, hm,
                  tok(DN_W), tok(DN_W), tok(DN_W), tok(DN_W), hm, sc, sc, pl.BlockSpec(memory_space=pl.ANY)],
        out_specs=[tok(DN_W), tok(DN_W), tok(DN_W), tok(128)],
        out_shape=[jax.ShapeDtypeStruct((t, DN_W), F32)] * 3 + [jax.ShapeDtypeStruct((t, 128), F32)],
        compiler_params=_params("parallel"),
    )(qn, kn, vc, gc, gr, tinv, a_qk, du, dw, dqg, dkd, da_qk, ddlast, dlast, dep)


def _dn_prep_bwd(pd, conv_w, al_vec, dt_vec, dqn, dkn, dvc, dgc, dz, tm):
    t = pd.shape[0]
    nt = t // tm
    tile = lambda i: nt - 1 - i

    def body(cur_ref, prev_ref, ba_ref, cw_ref, al_ref, dt_ref, dq_ref, dk_ref, dv_ref, dg_ref, dz_ref,
             o_ref, accw_ref, accg_ref, xs_ref, dc_ref, ds_ref, carry_ref):
        @pl.when(pl.program_id(0) == 0)
        def _():
            accw_ref[...] = jnp.zeros_like(accw_ref)
            accg_ref[...] = jnp.zeros_like(accg_ref)
            carry_ref[...] = jnp.zeros_like(carry_ref)

        _conv_window(cur_ref, prev_ref, xs_ref, tm, tile(pl.program_id(0)) > 0)
        taps = [xs_ref[pl.ds(HALO - DN_CONV + 1 + j, tm), :] for j in range(DN_CONV)]
        y = cw_ref[0:1, :] * taps[0]
        for j in range(1, DN_CONV):
            y += cw_ref[j:j + 1, :] * taps[j]
        sg = _sigmoid(y)
        c = y * sg
        for h in range(DN_HEADS):
            qs = slice(h * DN_HEAD_DIM, (h + 1) * DN_HEAD_DIM)
            ksl = slice(DN_W + h * DN_HEAD_DIM, DN_W + (h + 1) * DN_HEAD_DIM)
            for src, sl, scale in ((dq_ref, qs, DN_SCALE), (dk_ref, ksl, 1.0)):
                xh = c[:, sl]
                r = lax.rsqrt(jnp.sum(xh * xh, axis=-1, keepdims=True) + EPS)
                unit = xh * r
                dn = src[:, qs] * scale
                dc_ref[:, sl] = r * (dn - unit * jnp.sum(dn * unit, axis=-1, keepdims=True))
        dc_ref[:, 2 * DN_W:3 * DN_W] = dv_ref[...]
        dy = dc_ref[...] * (sg + y * sg * (1.0 - sg))
        for j in range(DN_CONV):
            accw_ref[j:j + 1, :] += jnp.sum(dy * taps[j], axis=0, keepdims=True)
        ds_ref[0:tm, :] = dy
        ds_ref[tm:tm + HALO, :] = carry_ref[...]
        carry_ref[...] = ds_ref[0:HALO, :]
        dx = cw_ref[0:1, :] * ds_ref[pl.ds(DN_CONV - 1, tm), :]
        for j in range(1, DN_CONV):
            dx += cw_ref[j:j + 1, :] * ds_ref[pl.ds(DN_CONV - 1 - j, tm), :]

        beta, g, pre = _gate_values(ba_ref[...], al_ref[...], dt_ref[...])
        dgb = dg_ref[...]
        lane = lax.broadcasted_iota(jnp.int32, dgb.shape, 1)
        is_b, is_a = lane < DN_HEADS, (lane >= DN_HEADS) & (lane < 2 * DN_HEADS)
        dpre = dgb * (-jnp.exp(al_ref[...])) * _sigmoid(pre)
        dba = jnp.where(is_b, dgb * beta * (1.0 - beta), jnp.where(is_a, dpre, 0.0))
        accg_ref[0:1, :] += jnp.sum(jnp.where(is_a, dgb * g, 0.0), axis=0, keepdims=True)
        accg_ref[1:2, :] += jnp.sum(jnp.where(is_a, dpre, 0.0), axis=0, keepdims=True)
        o_ref[:, 0:CONV_CH] = dx.astype(o_ref.dtype)
        o_ref[:, CONV_CH:CONV_CH + DN_W] = dz_ref[...].astype(o_ref.dtype)
        o_ref[:, CONV_CH + DN_W:DN_COLS] = dba.astype(o_ref.dtype)

    tok = lambda w: pl.BlockSpec((tm, w), lambda i: (tile(i), 0))
    return pl.pallas_call(
        body, name="dn_prep_bwd", grid=(nt,),
        in_specs=_dn_prep_specs(tm, tile) + [tok(DN_W), tok(DN_W), tok(DN_W), tok(128), tok(DN_W)],
        out_specs=[tok(DN_COLS), _full((8, CONV_CH)), _full((8, 128))],
        out_shape=[jax.ShapeDtypeStruct((t, DN_COLS), MXU_DTYPE),
                   jax.ShapeDtypeStruct((8, CONV_CH), F32), jax.ShapeDtypeStruct((8, 128), F32)],
        scratch_shapes=[pltpu.VMEM((HALO + tm, CONV_CH), F32), pltpu.VMEM((tm, CONV_CH), F32),
                        pltpu.VMEM((tm + HALO, CONV_CH), F32), pltpu.VMEM((HALO, CONV_CH), F32)],
        compiler_params=_params("arbitrary"),
    )(pd, pd, pd, conv_w, al_vec, dt_vec, dqn, dkn, dvc, dgc, dz)


def _pad_lanes(v, offset=0):
    return jnp.zeros((1, 128), F32).at[0, offset:offset + v.shape[0]].set(v.astype(F32))


class _LocalReducer:
    def start(self, grads):
        return jnp.zeros((8, 128), F32)

    def middle(self, after):
        return jnp.zeros((8, 128), F32)

    def finish(self, after):
        return None


def _local_step(x, p, tgt, sm, w, late, reducer):
    t = x.shape[0]
    tm = min(512, t // 2)
    tm_s = min(512, t // 2)
    tw = min(1024, t // 2)

    w_in_t = w["w_in_t"]
    wa_t = w_in_t[:ATTN_Q + 2 * ATTN_KV]
    wd_t = jnp.pad(w_in_t[ATTN_Q + 2 * ATTN_KV:], ((0, DN_COLS - (D_IN - ATTN_Q - 2 * ATTN_KV)), (0, 0)))
    conv_w = w["conv_w"]
    al_vec, dt_vec = _pad_lanes(sm["a_log"], DN_HEADS), _pad_lanes(sm["dt_bias"], DN_HEADS)
    sinks_vec = _pad_lanes(sm["sinks"])
    dn_w = sm["dn_norm"].reshape(1, 128)
    row = lambda v: v.reshape(1, D_MODEL)
    cos, sin = _rope_tables(t)

    u, pa, pd = _inproj(x, row(sm["norm_mix"]), wa_t, wd_t, tm_s)
    ao = _attn_fwd(pa, cos, sin, sinks_vec)
    qn, kn, vc, gc, gr = _dn_prep(pd, conv_w, al_vec, dt_vec, tm_s)
    uu, ww, qg, kd, a_qk, tinv, dlast = _dn_intra(qn, kn, vc, gc, gr)
    dn_out, o, vnew, sall = _dn_scan_fwd(uu, ww, qg, kd, a_qk, dlast, pd, dn_w)
    w_o, late_rest = late(dn_out)
    wo_a, wo_d = w_o[:ATTN_Q], w_o[ATTN_Q:]
    h1 = _oproj(x, ao, dn_out, wo_a, wo_d, tm)
    w = dict(w, **late_rest(h1))
    w_proj = jnp.transpose(w["w_proj4"], (1, 0, 2)).reshape(PLE_DIM, D_MODEL)
    m, r, h2 = _mlp_fwd(h1, row(sm["norm_mlp"]), w["w_up4"], w["w_down"], tw)
    dh2, dh2b, dgp, dpp, n3, pb, acc_ple = _ple_loss(h2, p, tgt, row(sm["norm_ple"]), row(sm["norm_final"]),
                                                     w["w_gate"], w_proj, tm_s)
    g_w_gate = _wgrad(n3, dgp, "wgrad_gate", D_MODEL, D_MODEL, tw)
    g_w_proj = _wgrad(pb, dpp, "wgrad_proj", PLE_DIM, D_MODEL, tw)
    da, dh1, dh1b, acc_mlp = _mlp_bwd(dh2, dh2b, r, h1, row(sm["norm_mlp"]), w["w_up4"], w["w_down"], tm)
    g_w_up4 = _wgrad(m, da, "wgrad_up", D_MODEL, FF_BLOCK, tw, stacked=True)
    g_w_down = _wgrad(r, dh2b, "wgrad_down", FF_BLOCK, D_MODEL, tw,
                      prep=lambda rv: jnp.square(rv.astype(F32)).astype(MXU_DTYPE))
    g_w_o = _wgrad_cat([ao, dn_out], [dh1b], "wgrad_o", tw)
    early = dict(w_up4=g_w_up4, w_down=g_w_down, w_gate=g_w_gate, w_proj=g_w_proj, w_o=g_w_o)
    dep = reducer.start(early)
    dao, ddn = _oproj_bwd(dh1b, wo_a, wo_d, tm, dep)
    dz, du, dw, dqg, dkd, da_qk, ddlast, acc_dn = _dn_scan_bwd(ddn, o, vnew, sall, ww, qg, kd, a_qk, dlast, pd, dn_w)
    dep = reducer.middle(du)
    dqn, dkn, dvc, dgc = _dn_intra_bwd(qn, kn, vc, gc, gr, tinv, a_qk, du, dw, dqg, dkd, da_qk, ddlast, dlast, dep)
    d_dn, acc_conv, acc_gate = _dn_prep_bwd(pd, conv_w, al_vec, dt_vec, dqn, dkn, dvc, dgc, dz, tm_s)
    dq, dk, dv, acc_attn = _attn_bwd(pa, cos, sin, sinks_vec, dao)
    reducer.finish(dq)
    wq_t, wk_t, wv_t = wa_t[:ATTN_Q], wa_t[ATTN_Q:ATTN_Q + ATTN_KV], wa_t[ATTN_Q + ATTN_KV:]
    dx, acc_mix = _inproj_bwd(x, dh1, row(sm["norm_mix"]), [dq, dk, dv, d_dn], [wq_t, wk_t, wv_t, wd_t], tm_s)

    g_w_in_t = _wgrad_cat([dq, dk, dv, d_dn], [u], "wgrad_in", tw)
    grads = dict(early, w_in_t=g_w_in_t)
    sums = dict(loss=acc_ple[2, 0], norm_final=acc_ple[0], norm_ple=acc_ple[1], norm_mlp=acc_mlp[0], norm_mix=acc_mix[0],
                dn_norm=acc_dn[0], sinks=acc_attn[0, :ATTN_HEADS], a_log=acc_gate[0, DN_HEADS:2 * DN_HEADS],
                dt_bias=acc_gate[1, DN_HEADS:2 * DN_HEADS], conv_w=acc_conv[:DN_CONV])
    return sums, dx, grads


MESH = pl.DeviceIdType.MESH
ANY = pl.BlockSpec(memory_space=pl.ANY)
N_CHIPS = 4
N_DEV = 8


def _place():
    x, y, c = lax.axis_index("x"), lax.axis_index("y"), lax.axis_index("c")
    chips = [(1 - x, y), (x, 1 - y), (1 - x, 1 - y)]
    return x, y, c, chips


def _gather_weights(shards, conv_s):
    n = len(shards)
    per = 7

    def body(*refs):
        in_refs, conv_ref = refs[:n], refs[n]
        out_refs, conv_out = refs[n + 1:2 * n + 1], refs[2 * n + 1]
        send_sems, recv_sems = refs[2 * n + 2:]
        x, y, c, chips = _place()
        sibling = (x, y, 1 - c)

        def blk(a, px, py, pc):
            hr = in_refs[a].shape[0] // 2
            return out_refs[a].at[2 * px + py, pl.ds(pc * hr, hr), :]

        def mine(a):
            hr = in_refs[a].shape[0] // 2
            return in_refs[a].at[pl.ds(c * hr, hr), :]

        def rcopy(a, k, block, to, src=None):
            return pltpu.make_async_remote_copy(
                src_ref=blk(a, *block) if src is None else src, dst_ref=blk(a, *block),
                send_sem=send_sems.at[per * a + k], recv_sem=recv_sems.at[per * a + k],
                device_id=to, device_id_type=MESH)

        def whole(a, to):
            return pltpu.make_async_remote_copy(
                src_ref=in_refs[a], dst_ref=out_refs[a].at[2 * x + y],
                send_sem=send_sems.at[per * a], recv_sem=recv_sems.at[per * a], device_id=to, device_id_type=MESH)

        def ccopy(j, to):
            return pltpu.make_async_remote_copy(
                src_ref=conv_ref, dst_ref=conv_out.at[2 * x + y],
                send_sem=send_sems.at[per * n + j], recv_sem=recv_sems.at[per * n + j],
                device_id=to, device_id_type=MESH)

        started = []
        for a in range(n):
            first = [whole(a, sibling)]
            first += [rcopy(a, 1 + j, (x, y, c), (*chip, c), src=mine(a)) for j, chip in enumerate(chips)]
            for cp in first:
                cp.start()
            started += first
        conv_sends = [ccopy(j, (*chip, c)) for j, chip in enumerate(chips)] + [ccopy(3, sibling)]
        for cp in conv_sends:
            cp.start()
        started += conv_sends
        for a in range(n):
            for j, chip in enumerate(chips):
                rcopy(a, 1 + j, (*chip, c), (x, y, c)).wait_recv()
                fwd = rcopy(a, 4 + j, (*chip, c), sibling)
                fwd.start()
                started.append(fwd)
        for a in range(n):
            whole(a, sibling).wait_recv()
            for j, chip in enumerate(chips):
                rcopy(a, 4 + j, (*chip, 1 - c), (x, y, c)).wait_recv()
        for j, chip in enumerate(chips + [(x, y)]):
            pltpu.make_async_remote_copy(
                src_ref=conv_ref, dst_ref=conv_out.at[2 * chip[0] + chip[1]],
                send_sem=send_sems.at[per * n + j], recv_sem=recv_sems.at[per * n + j],
                device_id=sibling, device_id_type=MESH).wait_recv()
        for cp in started:
            cp.wait_send()

    nsem = per * n + 4
    out_shape = [jax.ShapeDtypeStruct((N_CHIPS,) + s.shape, s.dtype) for s in shards]
    out_shape.append(jax.ShapeDtypeStruct((N_CHIPS,) + conv_s.shape, conv_s.dtype))
    return pl.pallas_call(
        body, name="gather_weights", in_specs=[ANY] * (n + 1), out_specs=[ANY] * (n + 1), out_shape=out_shape,
        scratch_shapes=[pltpu.SemaphoreType.DMA((nsem,)), pltpu.SemaphoreType.DMA((nsem,))],
    )(*shards, conv_s)


HBM = pl.BlockSpec(memory_space=pltpu.HBM)
SEM = pl.BlockSpec(memory_space=pltpu.SEMAPHORE)
EFFECT = pltpu.SideEffectType.DATAFLOW_SIDE_EFFECTING
LATE_COPIES = 7


def _late_copies(in_refs, land_refs, send_sems, recv_sems, only=None):
    x, y, c, chips = _place()
    sends, arrivals = [], []
    for a, (src, land) in enumerate(zip(in_refs, land_refs)):
        if only is not None and a not in only:
            continue
        hr = src.shape[0] // 2
        base = LATE_COPIES * a

        def cp(src_ref, dst_ref, s_idx, r_idx, to):
            return pltpu.make_async_remote_copy(src_ref=src_ref, dst_ref=dst_ref, send_sem=send_sems.at[base + s_idx],
                                                recv_sem=recv_sems.at[base + r_idx], device_id=to, device_id_type=MESH)

        sends.append(cp(src, land.at[2 * x + y], 0, 0, (x, y, 1 - c)))
        arrivals.append(cp(src, land.at[2 * x + y], 0, 0, (x, y, 1 - c)))
        for j, chip in enumerate(chips):
            for pc in range(2):
                half = src.at[pl.ds(c * hr, hr), :]
                sends.append(cp(half, land.at[2 * x + y, pl.ds(c * hr, hr), :], 1 + 2 * j + pc, 1 + 2 * j + c, (*chip, pc)))
                arrivals.append(cp(half, land.at[2 * chip[0] + chip[1], pl.ds(pc * hr, hr), :], 1 + 2 * j + pc,
                                   1 + 2 * j + pc, (*chip, pc)))
    return sends, arrivals


def _copies_start(name, build, nsem, srcs, land_shapes, after):
    n = len(srcs)

    def body(*refs):
        sends, _ = build(refs[:n], refs[n:2 * n], refs[2 * n + 1], refs[2 * n + 2])
        for cp in sends:
            cp.start()
        refs[-1][...] = jnp.zeros_like(refs[-1])

    lands = [pltpu.with_memory_space_constraint(lax.empty(s.shape, s.dtype), pltpu.HBM) for s in land_shapes]
    ins = [pltpu.with_memory_space_constraint(s, pltpu.HBM) for s in srcs]
    out = pl.pallas_call(
        body, name=name,
        out_shape=(pltpu.SemaphoreType.DMA((nsem,)), pltpu.SemaphoreType.DMA((nsem,)),
                   *[pltpu.HBM(s.shape, s.dtype) for s in srcs], *[pltpu.HBM(s.shape, s.dtype) for s in land_shapes],
                   jax.ShapeDtypeStruct((8, 128), F32)),
        in_specs=[HBM] * (2 * n) + [ANY],
        out_specs=(SEM, SEM, *[HBM] * (2 * n), pl.BlockSpec(memory_space=pltpu.VMEM)),
        input_output_aliases={i: 2 + i for i in range(2 * n)},
        compiler_params=pltpu.CompilerParams(has_side_effects=EFFECT),
    )(*ins, *lands, after)
    return out[0], out[1], out[2:2 + n], out[2 + n:2 + 2 * n], out[-1]


def _copies_wait(name, build, started, after):
    send_sems, recv_sems, srcs, lands, _ = started
    n = len(srcs)

    def body(*refs):
        sends, arrivals = build(refs[:n], refs[n:2 * n], refs[2 * n], refs[2 * n + 1])
        for cp in sends:
            cp.wait_send()
        for cp in arrivals:
            cp.wait_recv()

    out = pl.pallas_call(
        body, name=name,
        out_shape=(*[pltpu.HBM(s.shape, s.dtype) for s in srcs], *[pltpu.HBM(l.shape, l.dtype) for l in lands]),
        in_specs=[HBM] * (2 * n) + [SEM, SEM, ANY],
        out_specs=tuple([HBM] * (2 * n)),
        input_output_aliases={i: i for i in range(2 * n)},
        compiler_params=pltpu.CompilerParams(has_side_effects=EFFECT),
    )(*srcs, *lands, send_sems, recv_sems, after)
    return out[:n], out[n:]


def _exchange_copies(g_refs, got_refs, send_sems, recv_sems):
    x, y, c, _ = _place()
    sends, arrivals = [], []
    for a, (g, got) in enumerate(zip(g_refs, got_refs)):
        hr = g.shape[1] // 2
        cp = pltpu.make_async_remote_copy(
            src_ref=g.at[:, pl.ds((1 - c) * hr, hr), :], dst_ref=got, send_sem=send_sems.at[a],
            recv_sem=recv_sems.at[a], device_id=(x, y, 1 - c), device_id_type=MESH)
        sends.append(cp)
        arrivals.append(cp)
    return sends, arrivals


def _scatter_copies(s_refs, got_refs, send_sems, recv_sems):
    x, y, c, chips = _place()
    sends, arrivals = [], []
    for a, (s16, got) in enumerate(zip(s_refs, got_refs)):
        for j, chip in enumerate(chips):
            cp = pltpu.make_async_remote_copy(
                src_ref=s16.at[2 * chip[0] + chip[1]], dst_ref=got.at[j], send_sem=send_sems.at[3 * a + j],
                recv_sem=recv_sems.at[3 * a + j], device_id=(*chip, c), device_id_type=MESH)
            sends.append(cp)
            arrivals.append(cp)
    return sends, arrivals


def _share_halves(name, bufs, dep):
    n = len(bufs)

    def body(*refs):
        out_refs = refs[n + 1:2 * n + 1]
        send_sems, recv_sems = refs[2 * n + 1:]
        x, y, c, _ = _place()
        remote = [pltpu.make_async_remote_copy(
            src_ref=out_refs[a].at[c], dst_ref=out_refs[a].at[c], send_sem=send_sems.at[a], recv_sem=recv_sems.at[a],
            device_id=(x, y, 1 - c), device_id_type=MESH) for a in range(n)]
        for cp in remote:
            cp.start()
        for a in range(n):
            pltpu.make_async_remote_copy(
                src_ref=out_refs[a].at[c], dst_ref=out_refs[a].at[1 - c], send_sem=send_sems.at[a],
                recv_sem=recv_sems.at[a], device_id=(x, y, 1 - c), device_id_type=MESH).wait_recv()
        for cp in remote:
            cp.wait_send()

    return pl.pallas_call(
        body, name=name, in_specs=[ANY] * (n + 1), out_specs=[ANY] * n,
        out_shape=[jax.ShapeDtypeStruct(b.shape, b.dtype) for b in bufs],
        input_output_aliases={a: a for a in range(n)},
        scratch_shapes=[pltpu.SemaphoreType.DMA((n,)), pltpu.SemaphoreType.DMA((n,))],
    )(*bufs, dep)


SMALL_ROWS, SMALL_COLS = 16, CONV_CH


def _allreduce_small(block):
    m_per, ncol = block.shape

    def body(x_ref, sum_ref, all_ref, send_sems, recv_sems, local_sem):
        x, y, c, chips = _place()
        me, sibling = (x, y, c), (x, y, 1 - c)

        def rows(px, py, pc):
            return all_ref.at[pl.ds((4 * px + 2 * py + pc) * m_per, m_per), :]

        def copy(k, block_of, to, src=None):
            return pltpu.make_async_remote_copy(
                src_ref=rows(*block_of) if src is None else src, dst_ref=rows(*block_of),
                send_sem=send_sems.at[k], recv_sem=recv_sems.at[k], device_id=to, device_id_type=MESH)

        mine = pltpu.make_async_copy(x_ref, rows(*me), local_sem)
        mine.start()
        first = [copy(0, me, sibling, src=x_ref)]
        first += [copy(1 + j, me, (*chip, c), src=x_ref) for j, chip in enumerate(chips)]
        for cp in first:
            cp.start()
        passed = [copy(4 + j, (*chip, c), sibling) for j, chip in enumerate(chips)]
        for j, chip in enumerate(chips):
            copy(1 + j, (*chip, c), me).wait_recv()
            passed[j].start()
        copy(0, sibling, me).wait_recv()
        for j, chip in enumerate(chips):
            copy(4 + j, (*chip, 1 - c), me).wait_recv()
        for cp in first + passed:
            cp.wait_send()
        mine.wait()
        total = all_ref[0:m_per, :]
        for d in range(1, N_DEV):
            total = total + all_ref[d * m_per:(d + 1) * m_per, :]
        sum_ref[...] = total

    vm = pl.BlockSpec(memory_space=pltpu.VMEM)
    return pl.pallas_call(
        body, name="allreduce_small", in_specs=[vm], out_specs=vm,
        out_shape=jax.ShapeDtypeStruct((m_per, ncol), F32),
        scratch_shapes=[pltpu.VMEM((N_DEV * m_per, ncol), F32), pltpu.SemaphoreType.DMA((7,)),
                        pltpu.SemaphoreType.DMA((7,)), pltpu.SemaphoreType.DMA],
    )(block)


def _row_tile(rows, cols):
    tile = rows
    while tile * cols * 4 > (1 << 20) and tile % 16 == 0:
        tile //= 2
    return tile


def _elementwise(fn, name, ins, out_dtypes, dep):
    rows, cols = ins[0].shape
    tile = _row_tile(rows, cols)

    def body(*refs):
        outs = fn(*[r[...] for r in refs[:len(ins)]])
        for o_ref, o in zip(refs[len(ins) + 1:], outs):
            o_ref[...] = o.astype(o_ref.dtype)

    if tile * cols * 4 > (1 << 21) and cols % 512 == 0:
        spec = pl.BlockSpec((rows, 256), lambda i: (0, i))
        steps = cols // 256
    else:
        spec = pl.BlockSpec((tile, cols), lambda i: (i, 0))
        steps = rows // tile
    return pl.pallas_call(
        body, name=name, grid=(steps,), in_specs=[spec] * len(ins) + [pl.BlockSpec(memory_space=pl.ANY)],
        out_specs=[spec] * len(out_dtypes),
        out_shape=[jax.ShapeDtypeStruct((rows, cols), d) for d in out_dtypes],
        compiler_params=_params("parallel"),
    )(*ins, dep)


def _adamw_tile(w, g, m, v):
    m = ADAM_B1 * m + (1.0 - ADAM_B1) * g
    v = ADAM_B2 * v + (1.0 - ADAM_B2) * jnp.square(g)
    m_hat = m / (1.0 - ADAM_B1 ** ADAM_STEP)
    v_hat = v / (1.0 - ADAM_B2 ** ADAM_STEP)
    delta = -ADAM_LR * (m_hat / (jnp.sqrt(v_hat) + ADAM_EPS) + ADAM_WD * w)
    return delta, m, v


def _adamw(name, w, g, m, v, dep):
    return _elementwise(_adamw_tile, name, [w, g, m, v], [F32, F32, F32], dep)


def _chip_sum(name, g4, got, place):
    nchip, hr, cols = got.shape
    tile = _row_tile(hr, cols)
    nblk = hr // tile

    def body(pl_ref, g_ref, o_ref, s32_ref, s16_ref):
        s = g_ref[...] + o_ref[...]
        s32_ref[...] = s
        s16_ref[...] = s.astype(BF16)

    spec = pl.BlockSpec((None, tile, cols), lambda k, i, pr: (k, i, 0))
    return pl.pallas_call(
        body, name=name,
        grid_spec=pltpu.PrefetchScalarGridSpec(
            num_scalar_prefetch=1, grid=(nchip, nblk),
            in_specs=[pl.BlockSpec((None, tile, cols), lambda k, i, pr: (k, pr[1] * nblk + i, 0)), spec],
            out_specs=[spec, spec]),
        out_shape=[jax.ShapeDtypeStruct(got.shape, F32), jax.ShapeDtypeStruct(got.shape, BF16)],
        compiler_params=_params("parallel", "parallel"),
    )(place, g4, got)


def _mesh_sum(name, s32, got, place):
    _, hr, cols = s32.shape
    tile = _row_tile(hr, cols)

    def body(pl_ref, own_ref, g0_ref, g1_ref, g2_ref, o_ref):
        o_ref[...] = ((own_ref[...] + g0_ref[...].astype(F32)) + g1_ref[...].astype(F32)) + g2_ref[...].astype(F32)

    slab = lambda j: pl.BlockSpec((None, tile, cols), lambda i, pr: (j, i, 0))
    return pl.pallas_call(
        body, name=name,
        grid_spec=pltpu.PrefetchScalarGridSpec(
            num_scalar_prefetch=1, grid=(hr // tile,),
            in_specs=[pl.BlockSpec((None, tile, cols), lambda i, pr: (pr[0], i, 0)), slab(0), slab(1), slab(2)],
            out_specs=pl.BlockSpec((None, tile, cols), lambda i, pr: (pr[1], i, 0))),
        out_shape=jax.ShapeDtypeStruct((2, hr, cols), F32),
        compiler_params=_params("parallel"),
    )(place, s32, got, got, got)


def _place_operand():
    return jnp.stack([2 * lax.axis_index("x") + lax.axis_index("y"), lax.axis_index("c")]).astype(jnp.int32)


W_IN_ROWS = 720
W_IN_GATHER_ROWS = 736


def _per_chip(name, g):
    if name == "w_in_t":
        rows = D_IN // N_CHIPS
        return jnp.stack([lax.slice_in_dim(g, rows * k, rows * k + W_IN_ROWS) for k in range(N_CHIPS)])
    if name == "w_proj":
        return jnp.transpose(g.reshape(PLE_DIM, N_CHIPS, D_MODEL // N_CHIPS), (1, 0, 2))
    if name == "w_up4":
        return g
    return g.reshape(N_CHIPS, g.shape[0] // N_CHIPS, g.shape[1])


class _EarlyReducer:
    def __init__(self, tag):
        self.tag = tag

    def start(self, grads):
        self.names = list(grads)
        self.place = _place_operand()
        slabs = [_per_chip(k, grads[k]) for k in self.names]
        halves = [jax.ShapeDtypeStruct((s.shape[0], s.shape[1] // 2, s.shape[2]), F32) for s in slabs]
        self.a = _copies_start(self.tag + "exchange_start", _exchange_copies, len(slabs), slabs, halves,
                               slabs[0][0, :8, :128])
        return self.a[-1]

    def middle(self, after):
        slabs, got = _copies_wait(self.tag + "exchange_wait", _exchange_copies, self.a, after)
        self.sums = [_chip_sum(self.tag + "chip_sum_" + k, s, g, self.place) for k, s, g in zip(self.names, slabs, got)]
        s16 = [s[1] for s in self.sums]
        lands = [jax.ShapeDtypeStruct((3,) + s.shape[1:], BF16) for s in s16]
        self.b = _copies_start(self.tag + "scatter_start", _scatter_copies, 3 * len(s16), s16, lands,
                               self.sums[0][0][0, :8, :128])
        return self.b[-1]

    def finish(self, after):
        _, got = _copies_wait(self.tag + "scatter_wait", _scatter_copies, self.b, after)
        self.bufs = {k: _mesh_sum(self.tag + "mesh_sum_" + k, s[0], g, self.place)
                     for k, s, g in zip(self.names, self.sums, got)}


def kernel(x, p, norm_mix, w_in, conv_w, a_log, dt_bias, dn_norm, sinks, w_o, norm_mlp, w_up, w_down, norm_ple, w_ple_gate, w_ple_proj, norm_final, loss_target, m_norm_mix, m_w_in, m_conv_w, m_a_log, m_dt_bias, m_dn_norm, m_sinks, m_w_o, m_norm_mlp, m_w_up, m_w_down, m_norm_ple, m_w_ple_gate, m_w_ple_proj, m_norm_final, v_norm_mix, v_w_in, v_conv_w, v_a_log, v_dt_bias, v_dn_norm, v_sinks, v_w_o, v_norm_mlp, v_w_up, v_w_down, v_norm_ple, v_w_ple_gate, v_w_ple_proj, v_norm_final):
    chip = 2 * lax.axis_index("x") + lax.axis_index("y")
    big = dict(w_in=w_in[0], w_o=w_o[0], w_up=w_up[0], w_down=w_down[0], w_gate=w_ple_gate[0], w_proj=w_ple_proj[0])
    big_m = dict(w_in=m_w_in[0], w_o=m_w_o[0], w_up=m_w_up[0], w_down=m_w_down[0], w_gate=m_w_ple_gate[0], w_proj=m_w_ple_proj[0])
    big_v = dict(w_in=v_w_in[0], w_o=v_w_o[0], w_up=v_w_up[0], w_down=v_w_down[0], w_gate=v_w_ple_gate[0], w_proj=v_w_ple_proj[0])
    names = list(big)

    rows_in = D_IN // N_CHIPS
    w_in_shard_t = jnp.pad(big["w_in"].T.astype(BF16), ((0, W_IN_GATHER_ROWS - rows_in), (0, 0)))
    w_in_all, conv_all = _gather_weights([w_in_shard_t], conv_w[0])
    late_names = names[1:]
    late_shards = [big[k].astype(BF16) for k in late_names]
    gather = _copies_start("gather_start", _late_copies, LATE_COPIES * len(late_shards), late_shards,
                           [jax.ShapeDtypeStruct((N_CHIPS,) + s.shape, BF16) for s in late_shards], w_in_all)
    token = gather[-1]
    w = dict(w_in_t=jnp.concatenate([w_in_all[k, :rows_in] for k in range(N_CHIPS)], axis=0),
             conv_w=jnp.transpose(conv_all, (1, 0, 2)).reshape(DN_CONV, CONV_CH))
    sm = dict(norm_mix=norm_mix[0] + token[0, 0], a_log=a_log[0], dt_bias=dt_bias[0], dn_norm=dn_norm[0],
              sinks=sinks[0], norm_mlp=norm_mlp[0], norm_ple=norm_ple[0], norm_final=norm_final)

    def late(after):
        first = functools.partial(_late_copies, only=(0,))
        srcs, lands = _copies_wait("gather_wait_o", first, gather, after)

        def rest(after2):
            others = functools.partial(_late_copies, only=tuple(range(1, len(late_names))))
            gw = dict(zip(late_names, _copies_wait("gather_wait_rest", others, gather[:2] + (srcs, lands, None), after2)[1]))
            return dict(w_up4=gw["w_up"], w_down=gw["w_down"].reshape(D_FF, D_MODEL),
                        w_gate=gw["w_gate"].reshape(D_MODEL, D_MODEL), w_proj4=gw["w_proj"])

        return lands[0].reshape(D_MODEL, D_MODEL), rest

    reducer = _EarlyReducer("early_")
    sums, grad_x, g = _local_step(x[0], p[0, 0], loss_target[0], sm, w, late, reducer)

    last = _EarlyReducer("last_")
    dep_a = last.start({"w_in_t": g["w_in_t"]})

    row = lambda v: jnp.zeros((SMALL_COLS,), F32).at[:v.shape[0]].set(v)
    misc = jnp.zeros((SMALL_COLS,), F32).at[0:4].set(sums["a_log"]).at[4:8].set(sums["dt_bias"]) \
        .at[8:16].set(sums["sinks"]).at[128:256].set(sums["dn_norm"]).at[256].set(sums["loss"])
    small = jnp.concatenate([sums["conv_w"], jnp.stack([row(sums["norm_mix"]), row(sums["norm_mlp"]), row(sums["norm_ple"]),
                                                        row(sums["norm_final"]), misc]),
                             jnp.zeros((SMALL_ROWS - 9, SMALL_COLS), F32)], axis=0)
    tot = _allreduce_small(small + dep_a[0, 0])
    dep_b = last.middle(tot)
    grad_key = dict(w_o="w_o", w_up="w_up4", w_down="w_down", w_gate="w_gate", w_proj="w_proj")
    full = _share_halves("share_halves", [reducer.bufs[grad_key[k]] for k in late_names], dep_b)
    red = {k: f.reshape(-1, f.shape[-1]) for k, f in zip(late_names, full)}
    loss = tot[8, 256]
    ncw = CONV_CH // N_CHIPS

    def pack(cw, nmix, nmlp, nple, nfin, al, dtb, sk, dnn):
        misc_p = jnp.zeros((SMALL_COLS,), F32).at[0:4].set(al).at[4:8].set(dtb).at[8:16].set(sk).at[128:256].set(dnn)
        cw_p = jnp.zeros((DN_CONV, SMALL_COLS), F32).at[:, :ncw].set(cw)
        return jnp.concatenate([cw_p, jnp.stack([row(nmix), row(nmlp), row(nple), row(nfin), misc_p]),
                                jnp.zeros((SMALL_ROWS - 9, SMALL_COLS), F32)], axis=0)

    def unpack(buf):
        return dict(conv_w=buf[0:4, :ncw][None], norm_mix=buf[4, :D_MODEL][None], norm_mlp=buf[5, :D_MODEL][None],
                    norm_ple=buf[6, :D_MODEL][None], norm_final=buf[7, :D_MODEL], a_log=buf[8, 0:4][None],
                    dt_bias=buf[8, 4:8][None], sinks=buf[8, 8:16][None], dn_norm=buf[8, 128:256][None])

    g_conv_shard = lax.dynamic_slice(tot[0:4], (0, chip * ncw), (DN_CONV, ncw))
    g_small = pack(g_conv_shard, tot[4, :D_MODEL], tot[5, :D_MODEL], tot[6, :D_MODEL], tot[7, :D_MODEL],
                   tot[8, 0:4], tot[8, 4:8], tot[8, 8:16], tot[8, 128:256])
    w_small = pack(conv_w[0], norm_mix[0], norm_mlp[0], norm_ple[0], norm_final, a_log[0], dt_bias[0], sinks[0], dn_norm[0])
    m_small = pack(m_conv_w[0], m_norm_mix[0], m_norm_mlp[0], m_norm_ple[0], m_norm_final, m_a_log[0], m_dt_bias[0],
                   m_sinks[0], m_dn_norm[0])
    v_small = pack(v_conv_w[0], v_norm_mix[0], v_norm_mlp[0], v_norm_ple[0], v_norm_final, v_a_log[0], v_dt_bias[0],
                   v_sinks[0], v_dn_norm[0])

    ref_name = dict(w_in="w_in", w_o="w_o", w_up="w_up", w_down="w_down", w_gate="w_ple_gate", w_proj="w_ple_proj")
    out_g, out_d, out_m, out_v = {}, {}, {}, {}

    def update(k, dep):
        d_k, m_k, v_k = _adamw("adamw_" + k, big[k], red[k], big_m[k], big_v[k], dep)
        out_g[ref_name[k]], out_d[ref_name[k]] = red[k][None], d_k[None]
        out_m[ref_name[k]], out_v[ref_name[k]] = m_k[None], v_k[None]
        return d_k

    for k in late_names:
        done = update(k, dep_b)
    small_out = _adamw("adamw_small", w_small, g_small, m_small, v_small, dep_b)
    d_s, m_s, v_s = (unpack(b) for b in small_out)
    g_s = unpack(g_small)
    for src, dst in ((g_s, out_g), (d_s, out_d), (m_s, out_m), (v_s, out_v)):
        dst.update(src)
    last.finish(done + small_out[0][0:1, 0:1])
    (w_in_full,) = _share_halves("share_halves_w_in", [last.bufs["w_in_t"]], dep_b)
    g_t = w_in_full.reshape(W_IN_ROWS, D_MODEL)[:D_IN // N_CHIPS]
    d_t, m_t, v_t = _adamw("adamw_w_in", big["w_in"].T, g_t, big_m["w_in"].T, big_v["w_in"].T, dep_b)
    out_g["w_in"], out_d["w_in"], out_m["w_in"], out_v["w_in"] = g_t.T[None], d_t.T[None], m_t.T[None], v_t.T[None]
    order = ["norm_mix", "w_in", "conv_w", "a_log", "dt_bias", "dn_norm", "sinks", "w_o", "norm_mlp", "w_up", "w_down",
             "norm_ple", "w_ple_gate", "w_ple_proj", "norm_final"]
    return (loss, grad_x[None], *[out_g[k] for k in order], *[out_d[k] for k in order],
            *[out_m[k] for k in order], *[out_v[k] for k in order])
```

```python
import functools

import jax
import jax.numpy as jnp
from jax import lax
from jax.experimental import pallas as pl
from jax.experimental.pallas import tpu as pltpu

F32 = jnp.float32
BF16 = jnp.bfloat16
MXU_DTYPE = jnp.bfloat16
HI = lax.Precision.HIGHEST

D_MODEL = 1024
PLE_DIM = 256
ATTN_HEADS = 8
ATTN_KV_HEADS = 2
ATTN_GROUPS = ATTN_HEADS // ATTN_KV_HEADS
ATTN_HEAD_DIM = 64
ATTN_BLOCK = 128
ROPE_THETA = 10000.0
DN_HEADS = 4
DN_HEAD_DIM = 128
DN_CONV = 4
DN_CHUNK = 64
D_FF = 4 * D_MODEL
EPS = 1e-6
ATTN_Q = ATTN_HEADS * ATTN_HEAD_DIM
ATTN_KV = ATTN_KV_HEADS * ATTN_HEAD_DIM
DN_W = DN_HEADS * DN_HEAD_DIM
CONV_CH = 3 * DN_W
D_IN = ATTN_Q + 2 * ATTN_KV + 4 * DN_W + 2 * DN_HEADS
DN_COLS = 4 * DN_W + 128
W_IN_PADDED = ATTN_Q + 2 * ATTN_KV + DN_COLS
DN_SCALE = DN_HEAD_DIM ** -0.5
ATTN_SCALE = ATTN_HEAD_DIM ** -0.5
FF_BLOCKS = 4
FF_BLOCK = D_FF // FF_BLOCKS

ADAM_LR = 0.001
ADAM_B1 = 0.9
ADAM_B2 = 0.999
ADAM_EPS = 1e-08
ADAM_WD = 0.01
ADAM_STEP = 10

V7X_VMEM_BYTES = 64 * 1024 * 1024
VMEM_LIMIT = 48 * 1024 * 1024

NN = ((1,), (0,))
NT = ((1,), (1,))
TN = ((0,), (0,))


def _dot(a, b, dims=NN, prec=None):
    if a.dtype != b.dtype:
        a, b = a.astype(MXU_DTYPE), b.astype(MXU_DTYPE)
    return lax.dot_general(a, b, (dims, ((), ())), precision=prec, preferred_element_type=F32)


def _sigmoid(x):
    return 1.0 / (1.0 + jnp.exp(-x))


def _softplus(x):
    return jnp.maximum(x, 0.0) + jnp.log(1.0 + jnp.exp(-jnp.abs(x)))


def _params(*sem):
    return pltpu.CompilerParams(dimension_semantics=sem, vmem_limit_bytes=VMEM_LIMIT)


def _rms_fwd(xv, g):
    r = lax.rsqrt(jnp.mean(xv * xv, axis=-1, keepdims=True) + EPS)
    return xv * r * g


def _rms_bwd(xv, g, dn):
    r = lax.rsqrt(jnp.mean(xv * xv, axis=-1, keepdims=True) + EPS)
    xh = xv * r
    dg = jnp.sum(dn * xh, axis=0, keepdims=True)
    dxh = dn * g
    dx = r * (dxh - xh * jnp.mean(dxh * xh, axis=-1, keepdims=True))
    return dx, dg


def _full(shape):
    return pl.BlockSpec(shape, lambda *_: (0,) * len(shape))


def _inproj(x, g_mix, w_t, na, tm):
    t = x.shape[0]
    nd = w_t.shape[0] - na

    def body(x_ref, g_ref, w_ref, u_ref, pa_ref, pd_ref):
        u = _rms_fwd(x_ref[...], g_ref[...]).astype(MXU_DTYPE)
        u_ref[...] = u
        pa_ref[...] = _dot(u, w_ref[0:na, :], NT)
        pd_ref[...] = _dot(u, w_ref[na:na + nd, :], NT)

    return pl.pallas_call(
        body, name="inproj", grid=(t // tm,),
        in_specs=[pl.BlockSpec((tm, D_MODEL), lambda i: (i, 0)), _full((1, D_MODEL)), _full(w_t.shape)],
        out_specs=[pl.BlockSpec((tm, D_MODEL), lambda i: (i, 0)), pl.BlockSpec((tm, na), lambda i: (i, 0)),
                   pl.BlockSpec((tm, nd), lambda i: (i, 0))],
        out_shape=[jax.ShapeDtypeStruct((t, D_MODEL), MXU_DTYPE), jax.ShapeDtypeStruct((t, na), F32),
                   jax.ShapeDtypeStruct((t, nd), F32)],
        compiler_params=_params("parallel"),
    )(x, g_mix, w_t)


def _oproj(x, ao, dn, wo_a, wo_d, tm):
    t = x.shape[0]

    def body(x_ref, ao_ref, dn_ref, wa_ref, wd_ref, h_ref):
        h_ref[...] = (x_ref[...] + _dot(ao_ref[...].astype(MXU_DTYPE), wa_ref[...])
                      + _dot(dn_ref[...].astype(MXU_DTYPE), wd_ref[...]))

    half = ao.shape[1]
    return pl.pallas_call(
        body, name="oproj", grid=(t // tm,),
        in_specs=[pl.BlockSpec((tm, D_MODEL), lambda i: (i, 0)), pl.BlockSpec((tm, half), lambda i: (i, 0)),
                  pl.BlockSpec((tm, half), lambda i: (i, 0)), _full((half, D_MODEL)), _full((half, D_MODEL))],
        out_specs=pl.BlockSpec((tm, D_MODEL), lambda i: (i, 0)),
        out_shape=jax.ShapeDtypeStruct((t, D_MODEL), F32),
        compiler_params=_params("parallel"),
    )(x, ao, dn, wo_a, wo_d)


def _mlp_fwd(h1, g_mlp, w_up4, w_down, tm):
    t = h1.shape[0]

    def body(h_ref, g_ref, wu_ref, wd_ref, m_ref, r_ref, h2_ref, acc_ref):
        k = pl.program_id(1)

        @pl.when(k == 0)
        def _():
            m_ref[...] = _rms_fwd(h_ref[...], g_ref[...]).astype(MXU_DTYPE)
            acc_ref[...] = jnp.zeros_like(acc_ref)

        r = jnp.maximum(_dot(m_ref[...], wu_ref[...]), 0.0)
        r_ref[...] = r.astype(MXU_DTYPE)
        s = jnp.square(r).astype(MXU_DTYPE)
        acc_ref[...] += _dot(s, wd_ref[...])

        @pl.when(k == FF_BLOCKS - 1)
        def _():
            h2_ref[...] = h_ref[...] + acc_ref[...]

    return pl.pallas_call(
        body, name="mlp_fwd", grid=(t // tm, FF_BLOCKS),
        in_specs=[pl.BlockSpec((tm, D_MODEL), lambda i, k: (i, 0)), _full((1, D_MODEL)),
                  pl.BlockSpec((None, D_MODEL, FF_BLOCK), lambda i, k: (k, 0, 0)),
                  pl.BlockSpec((FF_BLOCK, D_MODEL), lambda i, k: (k, 0))],
        out_specs=[pl.BlockSpec((tm, D_MODEL), lambda i, k: (i, 0)), pl.BlockSpec((tm, FF_BLOCK), lambda i, k: (i, k)),
                   pl.BlockSpec((tm, D_MODEL), lambda i, k: (i, 0))],
        out_shape=[jax.ShapeDtypeStruct((t, D_MODEL), MXU_DTYPE), jax.ShapeDtypeStruct((t, D_FF), MXU_DTYPE),
                   jax.ShapeDtypeStruct((t, D_MODEL), F32)],
        scratch_shapes=[pltpu.VMEM((tm, D_MODEL), F32)],
        compiler_params=_params("parallel", "arbitrary"),
    )(h1, g_mlp, w_up4, w_down)


def _ple_loss(h2, p, tgt, g_ple, g_fin, w_gate, w_proj, tm):
    t = h2.shape[0]

    def body(h_ref, p_ref, t_ref, gp_ref, gf_ref, wg_ref, wp_ref,
             dh_ref, dhb_ref, dgp_ref, dpp_ref, n3_ref, pb_ref, acc_ref):
        @pl.when(pl.program_id(0) == 0)
        def _():
            acc_ref[...] = jnp.zeros_like(acc_ref)

        h = h_ref[...]
        g_ple_v, g_fin_v = gp_ref[...], gf_ref[...]
        n3 = _rms_fwd(h, g_ple_v).astype(MXU_DTYPE)
        n3_ref[...] = n3
        gate = _sigmoid(_dot(n3, wg_ref[...]))
        pb = p_ref[...].astype(MXU_DTYPE)
        pb_ref[...] = pb
        pp = _dot(pb, wp_ref[...])
        h3 = h + gate * pp
        r4 = lax.rsqrt(jnp.mean(h3 * h3, axis=-1, keepdims=True) + EPS)
        xh4 = h3 * r4
        e = xh4 * g_fin_v - t_ref[...]
        loss = 0.5 * jnp.sum(jnp.mean(e * e, axis=-1, keepdims=True), axis=0, keepdims=True)
        dy = e * (1.0 / D_MODEL)
        dg_fin = jnp.sum(dy * xh4, axis=0, keepdims=True)
        dxh = dy * g_fin_v
        dh3 = r4 * (dxh - xh4 * jnp.mean(dxh * xh4, axis=-1, keepdims=True))
        dpp_ref[...] = (dh3 * gate).astype(MXU_DTYPE)
        dgp = (dh3 * pp * gate * (1.0 - gate)).astype(MXU_DTYPE)
        dgp_ref[...] = dgp
        dn3 = _dot(dgp, wg_ref[...], NT)
        dx, dg_ple = _rms_bwd(h, g_ple_v, dn3)
        dh2 = dh3 + dx
        dh_ref[...] = dh2
        dhb_ref[...] = dh2.astype(MXU_DTYPE)
        acc_ref[0:1, :] += dg_fin
        acc_ref[1:2, :] += dg_ple
        acc_ref[2:3, :] += jnp.broadcast_to(loss, (1, D_MODEL))

    row = lambda w: pl.BlockSpec((tm, w), lambda i: (i, 0))
    return pl.pallas_call(
        body, name="ple_loss", grid=(t // tm,),
        in_specs=[row(D_MODEL), row(PLE_DIM), row(D_MODEL), _full((1, D_MODEL)), _full((1, D_MODEL)),
                  _full((D_MODEL, D_MODEL)), _full((PLE_DIM, D_MODEL))],
        out_specs=[row(D_MODEL), row(D_MODEL), row(D_MODEL), row(D_MODEL), row(D_MODEL), row(PLE_DIM),
                   _full((8, D_MODEL))],
        out_shape=[jax.ShapeDtypeStruct((t, D_MODEL), F32), jax.ShapeDtypeStruct((t, D_MODEL), MXU_DTYPE),
                   jax.ShapeDtypeStruct((t, D_MODEL), MXU_DTYPE), jax.ShapeDtypeStruct((t, D_MODEL), MXU_DTYPE),
                   jax.ShapeDtypeStruct((t, D_MODEL), MXU_DTYPE), jax.ShapeDtypeStruct((t, PLE_DIM), MXU_DTYPE),
                   jax.ShapeDtypeStruct((8, D_MODEL), F32)],
        compiler_params=_params("arbitrary"),
    )(h2, p, tgt, g_ple, g_fin, w_gate, w_proj)


def _mlp_bwd(dh2, dh2b, r, h1, g_mlp, w_up4, w_down, tm):
    t = h1.shape[0]

    def body(dh_ref, dhb_ref, r_ref, h_ref, g_ref, wu_ref, wd_ref,
             da_ref, dh1_ref, dh1b_ref, acc_ref, dm_ref):
        i, k = pl.program_id(0), pl.program_id(1)

        @pl.when((i == 0) & (k == 0))
        def _():
            acc_ref[...] = jnp.zeros_like(acc_ref)

        @pl.when(k == 0)
        def _():
            dm_ref[...] = jnp.zeros_like(dm_ref)

        ds = _dot(dhb_ref[...], wd_ref[...], NT)
        da = (ds * (2.0 * r_ref[...].astype(F32))).astype(MXU_DTYPE)
        da_ref[...] = da
        dm_ref[...] += _dot(da, wu_ref[...], NT)

        @pl.when(k == FF_BLOCKS - 1)
        def _():
            dx, dg = _rms_bwd(h_ref[...], g_ref[...], dm_ref[...])
            dh1 = dh_ref[...] + dx
            dh1_ref[...] = dh1
            dh1b_ref[...] = dh1.astype(MXU_DTYPE)
            acc_ref[0:1, :] += dg

    tok = lambda w: pl.BlockSpec((tm, w), lambda i, k: (i, 0))
    return pl.pallas_call(
        body, name="mlp_bwd", grid=(t // tm, FF_BLOCKS),
        in_specs=[tok(D_MODEL), tok(D_MODEL), pl.BlockSpec((tm, FF_BLOCK), lambda i, k: (i, k)), tok(D_MODEL),
                  _full((1, D_MODEL)), pl.BlockSpec((None, D_MODEL, FF_BLOCK), lambda i, k: (k, 0, 0)),
                  pl.BlockSpec((FF_BLOCK, D_MODEL), lambda i, k: (k, 0))],
        out_specs=[pl.BlockSpec((tm, FF_BLOCK), lambda i, k: (i, k)),
                   tok(D_MODEL), tok(D_MODEL), pl.BlockSpec((8, D_MODEL), lambda i, k: (0, 0))],
        out_shape=[jax.ShapeDtypeStruct((t, D_FF), MXU_DTYPE),
                   jax.ShapeDtypeStruct((t, D_MODEL), F32), jax.ShapeDtypeStruct((t, D_MODEL), MXU_DTYPE),
                   jax.ShapeDtypeStruct((8, D_MODEL), F32)],
        scratch_shapes=[pltpu.VMEM((tm, D_MODEL), F32)],
        compiler_params=_params("arbitrary", "arbitrary"),
    )(dh2, dh2b, r, h1, g_mlp, w_up4, w_down)


def _oproj_bwd(dh1b, wo_a, wo_d, tm, dep):
    t = dh1b.shape[0]
    half = wo_a.shape[0]

    def body(d_ref, wa_ref, wd_ref, dep_ref, da_ref, dd_ref):
        d = d_ref[...]
        da_ref[...] = _dot(d, wa_ref[...], NT)
        dd_ref[...] = _dot(d, wd_ref[...], NT)

    return pl.pallas_call(
        body, name="oproj_bwd", grid=(t // tm,),
        in_specs=[pl.BlockSpec((tm, D_MODEL), lambda i: (i, 0)), _full((half, D_MODEL)), _full((half, D_MODEL)),
                  pl.BlockSpec(memory_space=pl.ANY)],
        out_specs=[pl.BlockSpec((tm, half), lambda i: (i, 0)), pl.BlockSpec((tm, half), lambda i: (i, 0))],
        out_shape=[jax.ShapeDtypeStruct((t, half), F32), jax.ShapeDtypeStruct((t, half), F32)],
        compiler_params=_params("parallel"),
    )(dh1b, wo_a, wo_d, dep)


def _inproj_bwd(x, dh1, g_mix, grads, w_t, tm):
    t = x.shape[0]
    n = len(grads)
    widths = [g.shape[1] for g in grads]

    def body(*refs):
        x_ref, dh_ref, g_ref = refs[:3]
        g_refs, w_ref = refs[3:3 + n], refs[3 + n]
        dx_ref, acc_ref = refs[4 + n:]

        @pl.when(pl.program_id(0) == 0)
        def _():
            acc_ref[...] = jnp.zeros_like(acc_ref)

        du, row = None, 0
        for g_r, width in zip(g_refs, widths):
            part = _dot(g_r[...], w_ref[row:row + width, :])
            du = part if du is None else du + part
            row += width
        dx, dg = _rms_bwd(x_ref[...], g_ref[...], du)
        dx_ref[...] = dh_ref[...] + dx
        acc_ref[0:1, :] += dg

    tok = lambda w: pl.BlockSpec((tm, w), lambda i: (i, 0))
    return pl.pallas_call(
        body, name="inproj_bwd", grid=(t // tm,),
        in_specs=[tok(D_MODEL), tok(D_MODEL), _full((1, D_MODEL))] + [tok(width) for width in widths]
                 + [_full(w_t.shape)],
        out_specs=[tok(D_MODEL), _full((8, D_MODEL))],
        out_shape=[jax.ShapeDtypeStruct((t, D_MODEL), F32), jax.ShapeDtypeStruct((8, D_MODEL), F32)],
        compiler_params=_params("arbitrary"),
    )(x, dh1, g_mix, *grads, w_t)


def _wgrad(a, b, name, tk, tn, tt, stacked=False, prep=None):
    t, kdim = a.shape
    ncols = b.shape[1]

    def body(a_ref, b_ref, o_ref):
        @pl.when(pl.program_id(2) == 0)
        def _():
            o_ref[...] = jnp.zeros_like(o_ref)

        av = a_ref[...] if prep is None else prep(a_ref[...])
        o_ref[...] += _dot(av, b_ref[...], TN)

    if stacked:
        out_spec = pl.BlockSpec((None, tk, tn), lambda i, j, s: (j, i, 0))
        out_shape = jax.ShapeDtypeStruct((ncols // tn, kdim, tn), F32)
    else:
        out_spec = pl.BlockSpec((tk, tn), lambda i, j, s: (i, j))
        out_shape = jax.ShapeDtypeStruct((kdim, ncols), F32)
    return pl.pallas_call(
        body, name=name, grid=(kdim // tk, ncols // tn, t // tt),
        in_specs=[pl.BlockSpec((tt, tk), lambda i, j, s: (s, i)), pl.BlockSpec((tt, tn), lambda i, j, s: (s, j))],
        out_specs=out_spec, out_shape=out_shape,
        compiler_params=_params("parallel", "parallel", "arbitrary"),
    )(a, b)


def _wgrad_cat(as_, bs, name, tt):
    t = as_[0].shape[0]
    heights = [a.shape[1] for a in as_]
    widths = [b.shape[1] for b in bs]

    def body(*refs):
        a_refs, b_refs, o_ref = refs[:len(as_)], refs[len(as_):-1], refs[-1]

        @pl.when(pl.program_id(0) == 0)
        def _():
            o_ref[...] = jnp.zeros_like(o_ref)

        row = 0
        for a_ref, k in zip(a_refs, heights):
            av = a_ref[...]
            col = 0
            for b_ref, n in zip(b_refs, widths):
                o_ref[row:row + k, col:col + n] += _dot(av, b_ref[...], TN)
                col += n
            row += k

    tok = lambda w: pl.BlockSpec((tt, w), lambda s: (s, 0))
    shape = (sum(heights), sum(widths))
    return pl.pallas_call(
        body, name=name, grid=(t // tt,),
        in_specs=[tok(k) for k in heights] + [tok(n) for n in widths],
        out_specs=_full(shape), out_shape=jax.ShapeDtypeStruct(shape, F32),
        compiler_params=_params("arbitrary"),
    )(*as_, *bs)


def _rope_tables(t):
    half = ATTN_HEAD_DIM // 2
    inv = 1.0 / (ROPE_THETA ** (jnp.arange(half, dtype=F32) * (2.0 / ATTN_HEAD_DIM)))
    ang = jnp.arange(t, dtype=F32)[:, None] * inv[None, :]
    cos, sin = jnp.cos(ang), jnp.sin(ang)
    cos2 = jnp.concatenate([cos, cos], axis=-1)
    sin2 = jnp.concatenate([-sin, sin], axis=-1)
    return jnp.tile(cos2, (1, 2)), jnp.tile(sin2, (1, 2))


def _swap_halves(tv):
    w = tv.shape[-1]
    lane = lax.broadcasted_iota(jnp.int32, tv.shape, tv.ndim - 1)
    first = (lane % ATTN_HEAD_DIM) < (ATTN_HEAD_DIM // 2)
    return jnp.where(first, pltpu.roll(tv, w - ATTN_HEAD_DIM // 2, tv.ndim - 1),
                     pltpu.roll(tv, ATTN_HEAD_DIM // 2, tv.ndim - 1))


def _rope(tv, cos, sin):
    return tv * cos + _swap_halves(tv) * sin


def _rope_bwd(dv, cos, sin):
    return dv * cos + _swap_halves(dv * sin)


def _attn_valid(first_block):
    c = lax.broadcasted_iota(jnp.int32, (2 * ATTN_BLOCK, ATTN_BLOCK), 0)
    r = lax.broadcasted_iota(jnp.int32, (2 * ATTN_BLOCK, ATTN_BLOCK), 1)
    return (c > r) & (c <= r + ATTN_BLOCK) & ((c >= ATTN_BLOCK) | jnp.logical_not(first_block))


def _attn_probs(st, sink, valid):
    s = jnp.where(valid, st * ATTN_SCALE, -jnp.inf)
    m = jnp.maximum(jnp.max(s, axis=0, keepdims=True), sink)
    e = jnp.where(valid, jnp.exp(s - m), 0.0)
    es = jnp.exp(sink - m)
    inv = 1.0 / (jnp.sum(e, axis=0, keepdims=True) + es)
    return e * inv, es * inv


def _lane_scalar(vec, idx):
    lane = lax.broadcasted_iota(jnp.int32, vec.shape, 1)
    return jnp.sum(jnp.where(lane == idx, vec, 0.0), axis=-1, keepdims=True)


def _attn_specs(nb):
    cur = lambda w, cb: pl.BlockSpec((ATTN_BLOCK, w), lambda i: (jnp.minimum(i, nb - 1), cb))
    prev = lambda w, cb: pl.BlockSpec((ATTN_BLOCK, w), lambda i: (jnp.maximum(jnp.minimum(i, nb - 1) - 1, 0), cb))
    kcol, vcol = ATTN_Q // ATTN_KV, ATTN_Q // ATTN_KV + 1
    return [cur(ATTN_Q, 0), cur(ATTN_KV, kcol), prev(ATTN_KV, kcol), cur(ATTN_KV, vcol), prev(ATTN_KV, vcol),
            cur(ATTN_KV, 0), cur(ATTN_KV, 0), prev(ATTN_KV, 0), prev(ATTN_KV, 0), _full((1, 128))]


def _attn_fwd(pa, cos, sin, sinks_vec):
    t = pa.shape[0]
    nb = t // ATTN_BLOCK

    def body(q_ref, kc_ref, kp_ref, vc_ref, vp_ref, cc_ref, sc_ref, cp_ref, sp_ref, sk_ref, o_ref):
        first = pl.program_id(0) == 0
        cc, sc = cc_ref[...], sc_ref[...]
        q = _rope(q_ref[...], jnp.tile(cc, (1, ATTN_Q // ATTN_KV)), jnp.tile(sc, (1, ATTN_Q // ATTN_KV)))
        kc = _rope(kc_ref[...], cc, sc)
        kp = _rope(kp_ref[...], cp_ref[...], sp_ref[...])
        vc, vp = vc_ref[...], vp_ref[...]
        sk = sk_ref[...]
        valid = _attn_valid(first)
        kv = lambda tp, tc, hk: jnp.concatenate([tp[:, hk * ATTN_HEAD_DIM:(hk + 1) * ATTN_HEAD_DIM],
                                                 tc[:, hk * ATTN_HEAD_DIM:(hk + 1) * ATTN_HEAD_DIM]], axis=0)
        kwins = [kv(kp, kc, hk) for hk in range(ATTN_KV_HEADS)]
        vwins_t = [kv(vp, vc, hk).T for hk in range(ATTN_KV_HEADS)]
        heads = [slice(h * ATTN_HEAD_DIM, (h + 1) * ATTN_HEAD_DIM) for h in range(ATTN_HEADS)]
        scores = [_dot(kwins[h // ATTN_GROUPS], q[:, hs], NT) for h, hs in enumerate(heads)]
        probs = [_attn_probs(st, _lane_scalar(sk, h), valid)[0] for h, st in enumerate(scores)]
        for h, (hs, pt) in enumerate(zip(heads, probs)):
            o_ref[:, hs] = _dot(vwins_t[h // ATTN_GROUPS], pt).T.astype(o_ref.dtype)

    return pl.pallas_call(
        body, name="attn_fwd", grid=(nb,),
        in_specs=_attn_specs(nb),
        out_specs=pl.BlockSpec((ATTN_BLOCK, ATTN_Q), lambda i: (i, 0)),
        out_shape=jax.ShapeDtypeStruct((t, ATTN_Q), MXU_DTYPE),
        compiler_params=_params("parallel"),
    )(pa, pa, pa, pa, pa, cos, sin, cos, sin, sinks_vec)


def _attn_bwd(pa, cos, sin, sinks_vec, dao):
    t = pa.shape[0]
    nb = t // ATTN_BLOCK

    def body(q_ref, kc_ref, kp_ref, vc_ref, vp_ref, cc_ref, sc_ref, cp_ref, sp_ref, sk_ref, do_ref,
             dq_ref, dk_ref, dv_ref, acc_ref, dqr_ref, dkw_ref, dvw_ref, ck_ref, cv_ref):
        i = pl.program_id(0)

        @pl.when(i == 0)
        def _():
            acc_ref[...] = jnp.zeros_like(acc_ref)
            ck_ref[...] = jnp.zeros_like(ck_ref)
            cv_ref[...] = jnp.zeros_like(cv_ref)

        @pl.when(i < nb)
        def _():
            first = i == 0
            cc, sc = cc_ref[...], sc_ref[...]
            cq, sq = jnp.tile(cc, (1, ATTN_Q // ATTN_KV)), jnp.tile(sc, (1, ATTN_Q // ATTN_KV))
            q = _rope(q_ref[...], cq, sq)
            kc = _rope(kc_ref[...], cc, sc)
            kp = _rope(kp_ref[...], cp_ref[...], sp_ref[...])
            vc, vp = vc_ref[...], vp_ref[...]
            sk = sk_ref[...]
            do = do_ref[...]
            lane = lax.broadcasted_iota(jnp.int32, (1, 128), 1)
            dsink = jnp.zeros((1, 128), F32)
            valid = _attn_valid(first)
            kv = lambda tp, tc, hk: jnp.concatenate([tp[:, hk * ATTN_HEAD_DIM:(hk + 1) * ATTN_HEAD_DIM],
                                                     tc[:, hk * ATTN_HEAD_DIM:(hk + 1) * ATTN_HEAD_DIM]], axis=0)
            kwins = [kv(kp, kc, hk) for hk in range(ATTN_KV_HEADS)]
            vwins = [kv(vp, vc, hk) for hk in range(ATTN_KV_HEADS)]
            kwins_t = [kw.T for kw in kwins]
            heads = [slice(h * ATTN_HEAD_DIM, (h + 1) * ATTN_HEAD_DIM) for h in range(ATTN_HEADS)]
            scores = [_dot(kwins[h // ATTN_GROUPS], q[:, hs], NT) for h, hs in enumerate(heads)]
            dps = [_dot(vwins[h // ATTN_GROUPS], do[:, hs], NT) for h, hs in enumerate(heads)]
            pts, dsts = [], []
            for h, (st, dp_t) in enumerate(zip(scores, dps)):
                probs_t, psink = _attn_probs(st, _lane_scalar(sk, h), valid)
                delta = jnp.sum(probs_t * dp_t, axis=0, keepdims=True)
                pts.append(probs_t)
                dsts.append(probs_t * (dp_t - delta) * ATTN_SCALE)
                dsink += jnp.where(lane == h, jnp.sum(-psink * delta, axis=1, keepdims=True), 0.0)
            for h, (hs, ds_t) in enumerate(zip(heads, dsts)):
                dqr_ref[:, hs] = _dot(kwins_t[h // ATTN_GROUPS], ds_t).T
            for hk in range(ATTN_KV_HEADS):
                ks = slice(hk * ATTN_HEAD_DIM, (hk + 1) * ATTN_HEAD_DIM)
                group = range(hk * ATTN_GROUPS, (hk + 1) * ATTN_GROUPS)
                ds_g = jnp.concatenate([dsts[h] for h in group], axis=1)
                p_g = jnp.concatenate([pts[h] for h in group], axis=1)
                q_g = jnp.concatenate([q[:, heads[h]] for h in group], axis=0)
                do_g = jnp.concatenate([do[:, heads[h]] for h in group], axis=0)
                dkw_ref[:, ks] = _dot(ds_g, q_g)
                dvw_ref[:, ks] = _dot(p_g, do_g)
            acc_ref[0:1, :] += dsink
            dq_ref[...] = _rope_bwd(dqr_ref[...], cq, sq).astype(dq_ref.dtype)
            dk_ref[...] = (ck_ref[...] + _rope_bwd(dkw_ref[0:ATTN_BLOCK, :], cp_ref[...], sp_ref[...])).astype(dk_ref.dtype)
            dv_ref[...] = (cv_ref[...] + dvw_ref[0:ATTN_BLOCK, :]).astype(dv_ref.dtype)
            ck_ref[...] = _rope_bwd(dkw_ref[ATTN_BLOCK:2 * ATTN_BLOCK, :], cc, sc)
            cv_ref[...] = dvw_ref[ATTN_BLOCK:2 * ATTN_BLOCK, :]

        @pl.when(i == nb)
        def _():
            dk_ref[...] = ck_ref[...].astype(dk_ref.dtype)
            dv_ref[...] = cv_ref[...].astype(dv_ref.dtype)

    prev_out = lambda w: pl.BlockSpec((ATTN_BLOCK, w), lambda i: (jnp.maximum(i - 1, 0), 0))
    return pl.pallas_call(
        body, name="attn_bwd", grid=(nb + 1,),
        in_specs=_attn_specs(nb) + [pl.BlockSpec((ATTN_BLOCK, ATTN_Q), lambda i: (jnp.minimum(i, nb - 1), 0))],
        out_specs=[pl.BlockSpec((ATTN_BLOCK, ATTN_Q), lambda i: (jnp.minimum(i, nb - 1), 0)), prev_out(ATTN_KV),
                   prev_out(ATTN_KV), _full((8, 128))],
        out_shape=[jax.ShapeDtypeStruct((t, ATTN_Q), MXU_DTYPE), jax.ShapeDtypeStruct((t, ATTN_KV), MXU_DTYPE),
                   jax.ShapeDtypeStruct((t, ATTN_KV), MXU_DTYPE), jax.ShapeDtypeStruct((8, 128), F32)],
        scratch_shapes=[pltpu.VMEM((ATTN_BLOCK, ATTN_Q), F32), pltpu.VMEM((2 * ATTN_BLOCK, ATTN_KV), F32),
                        pltpu.VMEM((2 * ATTN_BLOCK, ATTN_KV), F32), pltpu.VMEM((ATTN_BLOCK, ATTN_KV), F32),
                        pltpu.VMEM((ATTN_BLOCK, ATTN_KV), F32)],
        compiler_params=_params("arbitrary"),
    )(pa, pa, pa, pa, pa, cos, sin, cos, sin, sinks_vec, dao)


PAIR = 2 * DN_CHUNK
INTRA_PAIRS = 4
SCAN_PAIRS = 4
HALO = 8


def _conv_window(cur_ref, prev_ref, xs_ref, tm, has_prev):
    prev = jnp.where(has_prev, prev_ref[...], 0.0)
    xs_ref[0:HALO, :] = prev
    xs_ref[HALO:HALO + tm, :] = cur_ref[...]


def _conv_taps(xs_ref, cw_ref, tm):
    y = cw_ref[0:1, :] * xs_ref[pl.ds(HALO - DN_CONV + 1, tm), :]
    for j in range(1, DN_CONV):
        y += cw_ref[j:j + 1, :] * xs_ref[pl.ds(HALO - DN_CONV + 1 + j, tm), :]
    return y


def _gate_values(ba, al, dt):
    beta = _sigmoid(ba)
    pre = ba + dt
    g = -jnp.exp(al) * _softplus(pre)
    return beta, g, pre


def _dn_prep_specs(tm, tile):
    return [pl.BlockSpec((tm, CONV_CH), lambda i: (tile(i), 0)),
            pl.BlockSpec((HALO, CONV_CH), lambda i: (jnp.maximum(tile(i) * (tm // HALO) - 1, 0), 0)),
            pl.BlockSpec((tm, 128), lambda i: (tile(i), 4 * DN_W // 128)),
            _full((DN_CONV, CONV_CH)), _full((1, 128)), _full((1, 128))]


def _dn_prep(pd, conv_w, al_vec, dt_vec, tm):
    t = pd.shape[0]

    def body(cur_ref, prev_ref, ba_ref, cw_ref, al_ref, dt_ref, qn_ref, kn_ref, vc_ref, gc_ref, gr_ref, xs_ref):
        _conv_window(cur_ref, prev_ref, xs_ref, tm, pl.program_id(0) > 0)
        y = _conv_taps(xs_ref, cw_ref, tm)
        c = y * _sigmoid(y)
        for h in range(DN_HEADS):
            qs = slice(h * DN_HEAD_DIM, (h + 1) * DN_HEAD_DIM)
            ksl = slice(DN_W + h * DN_HEAD_DIM, DN_W + (h + 1) * DN_HEAD_DIM)
            qh, kh = c[:, qs], c[:, ksl]
            qn_ref[:, qs] = qh * lax.rsqrt(jnp.sum(qh * qh, axis=-1, keepdims=True) + EPS) * DN_SCALE
            kn_ref[:, qs] = kh * lax.rsqrt(jnp.sum(kh * kh, axis=-1, keepdims=True) + EPS)
        vc_ref[...] = c[:, 2 * DN_W:3 * DN_W]
        beta, g, _ = _gate_values(ba_ref[...], al_ref[...], dt_ref[...])
        lane = lax.broadcasted_iota(jnp.int32, beta.shape, 1)
        gb = jnp.where(lane < DN_HEADS, beta, jnp.where(lane < 2 * DN_HEADS, g, 0.0))
        gc_ref[...] = gb
        gr_ref[...] = gb.T[0:8, :]

    tok = lambda w: pl.BlockSpec((tm, w), lambda i: (i, 0))
    return pl.pallas_call(
        body, name="dn_prep", grid=(t // tm,),
        in_specs=_dn_prep_specs(tm, lambda i: i),
        out_specs=[tok(DN_W), tok(DN_W), tok(DN_W), tok(128), pl.BlockSpec((8, tm), lambda i: (0, i))],
        out_shape=[jax.ShapeDtypeStruct((t, DN_W), F32)] * 3 + [jax.ShapeDtypeStruct((t, 128), F32),
                                                                 jax.ShapeDtypeStruct((8, t), F32)],
        scratch_shapes=[pltpu.VMEM((HALO + tm, CONV_CH), F32)],
        compiler_params=_params("parallel"),
    )(pd, pd, pd, conv_w, al_vec, dt_vec)


def _pair_masks():
    r = lax.broadcasted_iota(jnp.int32, (PAIR, PAIR), 0)
    c = lax.broadcasted_iota(jnp.int32, (PAIR, PAIR), 1)
    same = (r < DN_CHUNK) == (c < DN_CHUNK)
    return same & (r >= c), same & (r > c)


def _lane_col(mat, idx):
    lane = lax.broadcasted_iota(jnp.int32, mat.shape, 1)
    return jnp.sum(jnp.where(lane == idx, mat, 0.0), axis=-1, keepdims=True)


def _pair_cumsums(gc, gr, low):
    lowf = low.astype(F32)
    return _dot(lowf, gc, NN, HI), _dot(gr, lowf, NT, HI)


def _pair_gates(gc, cum_c, cum_r, low, h):
    beta = _lane_col(gc, h)
    gam = _lane_col(cum_c, DN_HEADS + h)
    gam_row = cum_r[DN_HEADS + h:DN_HEADS + h + 1, :]
    dm = jnp.where(low, jnp.exp(jnp.where(low, gam - gam_row, 0.0)), 0.0)
    row = lax.broadcasted_iota(jnp.int32, gam.shape, 0)
    gl = jnp.where(row < DN_CHUNK, gam[DN_CHUNK - 1:DN_CHUNK, :], gam[PAIR - 1:PAIR, :])
    return beta, gam, dm, gl


def _split(a):
    hi = a.astype(BF16)
    return hi, (a - hi.astype(F32)).astype(BF16)


def _dot_split(a, b, dims=NN):
    (ah, al), (bh, bl) = a, b
    la, lb = (1, 1) if dims == TN else ((0, 1) if dims == NN else (0, 0))
    r = _dot(jnp.concatenate([ah, al], axis=la), jnp.concatenate([bh, bl], axis=lb), dims)
    m, n = r.shape[0] // 2, r.shape[1] // 2
    return (r[m:, n:] + (r[:m, n:] + r[m:, :n])) + r[:m, :n]


def _unit_lower_inverses(lmats):
    n = lmats[0].shape[0]
    r = lax.broadcasted_iota(jnp.int32, (n, n), 0)
    c = lax.broadcasted_iota(jnp.int32, (n, n), 1)
    same = lambda size: (r & ~(size - 1)) == (c & ~(size - 1))
    base = DN_CHUNK // 4
    diag = [jnp.where(same(base), l, 0.0) for l in lmats]
    accs = [(r == c).astype(F32) - d for d in diag]
    splits = [_split(d) for d in diag]
    step = 1
    while 2 * step < base:
        splits = [_split(_dot_split(s, s)) for s in splits]
        accs = [acc + _dot_split(_split(acc), s) for acc, s in zip(accs, splits)]
        step *= 2
    size = base
    while size < DN_CHUNK:
        below = same(2 * size) & jnp.logical_not(same(size))
        tb = [_dot(acc, jnp.where(below, l, 0.0)) for acc, l in zip(accs, lmats)]
        accs = [acc - _dot(t, acc) for acc, t in zip(accs, tb)]
        size *= 2
    return accs


def _dn_intra(qn, kn, vc, gc, gr):
    t = qn.shape[0]
    npair = t // PAIR
    rows_step = INTRA_PAIRS * PAIR

    def body(q_ref, k_ref, v_ref, gc_ref, gr_ref, u_ref, w_ref, qg_ref, kd_ref, a_ref, ti_ref, dl_ref):
        low, strict = _pair_masks()
        items = []
        for p in range(INTRA_PAIRS):
            rows = slice(p * PAIR, (p + 1) * PAIR)
            gc_v = gc_ref[rows, :]
            cum_c, cum_r = _pair_cumsums(gc_v, gr_ref[:, rows], low)
            for h in range(DN_HEADS):
                hs = slice(h * DN_HEAD_DIM, (h + 1) * DN_HEAD_DIM)
                items.append((p, h, rows, hs, _pair_gates(gc_v, cum_c, cum_r, low, h)))
        lmats = []
        for p, h, rows, hs, (beta, gam, dm, gl) in items:
            k = k_ref[rows, hs]
            lmats.append(jnp.where(strict, _dot(k * beta, k, NT) * dm, 0.0))
        tinvs = _unit_lower_inverses(lmats)
        for (p, h, rows, hs, (beta, gam, dm, gl)), tinv in zip(items, tinvs):
            q, k, v = q_ref[rows, hs], k_ref[rows, hs], v_ref[rows, hs]
            eg = jnp.exp(gam)
            u_ref[rows, hs] = _dot(tinv, v * beta)
            w_ref[rows, hs] = _dot(tinv, (k * beta) * eg).astype(w_ref.dtype)
            a_ref[h, rows, :] = _dot(q, k, NT) * dm
            ti_ref[h, rows, :] = tinv
            qg_ref[rows, hs] = (q * eg).astype(qg_ref.dtype)
            kd_ref[rows, hs] = (k * jnp.exp(gl - gam)).astype(kd_ref.dtype)
            for c in range(2):
                last = (c + 1) * DN_CHUNK - 1
                dl_ref[2 * p + c, h] = jnp.broadcast_to(jnp.exp(gam[last:last + 1, :]), (8, 128))

    tok = lambda w: pl.BlockSpec((rows_step, w), lambda n: (n, 0))
    hm = pl.BlockSpec((DN_HEADS, rows_step, PAIR), lambda n: (0, n, 0))
    return pl.pallas_call(
        body, name="dn_intra", grid=(npair // INTRA_PAIRS,),
        in_specs=[tok(DN_W), tok(DN_W), tok(DN_W), tok(128), pl.BlockSpec((8, rows_step), lambda n: (0, n))],
        out_specs=[tok(DN_W)] * 4 + [hm, hm, pl.BlockSpec((2 * INTRA_PAIRS, DN_HEADS, 8, 128), lambda n: (n, 0, 0, 0))],
        out_shape=[jax.ShapeDtypeStruct((t, DN_W), F32)] + [jax.ShapeDtypeStruct((t, DN_W), MXU_DTYPE)] * 3
                  + [jax.ShapeDtypeStruct((DN_HEADS, t, PAIR), F32)] * 2
                  + [jax.ShapeDtypeStruct((2 * npair, DN_HEADS, 8, 128), F32)],
        compiler_params=_params("parallel"),
    )(qn, kn, vc, gc, gr)


def _dn_scan_fwd(u, w, qg, kd, a_qk, dlast, pd, dn_w):
    t = u.shape[0]
    npair = t // PAIR

    def body(u_ref, w_ref, qg_ref, kd_ref, a_ref, dl_ref, z_ref, nw_ref, out_ref, o_ref, vn_ref, sall_ref, s_ref):
        @pl.when(pl.program_id(0) == 0)
        def _():
            s_ref[...] = jnp.zeros_like(s_ref)

        nw = nw_ref[...]
        for c in range(2 * SCAN_PAIRS):
            rows = slice(c * DN_CHUNK, (c + 1) * DN_CHUNK)
            diag = slice((c % 2) * DN_CHUNK, (c % 2 + 1) * DN_CHUNK)
            for h in range(DN_HEADS):
                hs = slice(h * DN_HEAD_DIM, (h + 1) * DN_HEAD_DIM)
                st = s_ref[h]
                sall_ref[c, h] = st
                vn_ref[rows, hs] = (u_ref[rows, hs] - _dot(w_ref[rows, hs], st)).astype(vn_ref.dtype)
            for h in range(DN_HEADS):
                hs = slice(h * DN_HEAD_DIM, (h + 1) * DN_HEAD_DIM)
                st, vn = s_ref[h], vn_ref[rows, hs]
                o = _dot(qg_ref[rows, hs], st) + _dot(a_ref[h, rows, diag], vn)
                s_ref[h] = st * dl_ref[c, h][0:1, :] + _dot(kd_ref[rows, hs], vn, TN)
                o_ref[rows, hs] = o
                z = z_ref[rows, hs]
                on = o * lax.rsqrt(jnp.mean(o * o, axis=-1, keepdims=True) + EPS) * nw
                out_ref[rows, hs] = (on * (z * _sigmoid(z))).astype(out_ref.dtype)

    rows_step = SCAN_PAIRS * PAIR
    tok = pl.BlockSpec((rows_step, DN_W), lambda n: (n, 0))
    hm = pl.BlockSpec((DN_HEADS, rows_step, PAIR), lambda n: (0, n, 0))
    return pl.pallas_call(
        body, name="dn_scan_fwd", grid=(npair // SCAN_PAIRS,),
        in_specs=[tok, tok, tok, tok, hm, pl.BlockSpec((2 * SCAN_PAIRS, DN_HEADS, 8, 128), lambda n: (n, 0, 0, 0)),
                  pl.BlockSpec((rows_step, DN_W), lambda n: (n, 3)), _full((1, 128))],
        out_specs=[tok, tok, tok,
                   pl.BlockSpec((2 * SCAN_PAIRS, DN_HEADS, DN_HEAD_DIM, DN_HEAD_DIM), lambda n: (n, 0, 0, 0))],
        out_shape=[jax.ShapeDtypeStruct((t, DN_W), MXU_DTYPE), jax.ShapeDtypeStruct((t, DN_W), F32),
                   jax.ShapeDtypeStruct((t, DN_W), MXU_DTYPE),
                   jax.ShapeDtypeStruct((2 * npair, DN_HEADS, DN_HEAD_DIM, DN_HEAD_DIM), F32)],
        scratch_shapes=[pltpu.VMEM((DN_HEADS, DN_HEAD_DIM, DN_HEAD_DIM), F32)],
        compiler_params=_params("arbitrary"),
    )(u, w, qg, kd, a_qk, dlast, pd, dn_w)


def _dn_scan_bwd(dout, o, vnew, sall, w, qg, kd, a_qk, dlast, pd, dn_w):
    t = o.shape[0]
    npair = t // PAIR
    nstep = npair // SCAN_PAIRS
    rev = lambda n: nstep - 1 - n

    def body(do_ref, o_ref, vn_ref, sall_ref, w_ref, qg_ref, kd_ref, a_ref, dl_ref, z_ref, nw_ref,
             dz_ref, du_ref, dw_ref, dqg_ref, dkd_ref, da_ref, ddl_ref, acc_ref, ds_ref, dos_ref):
        @pl.when(pl.program_id(0) == 0)
        def _():
            ds_ref[...] = jnp.zeros_like(ds_ref)
            acc_ref[...] = jnp.zeros_like(acc_ref)

        nw = nw_ref[...]
        dnw = jnp.zeros((1, 128), F32)
        for h in range(DN_HEADS):
            hs = slice(h * DN_HEAD_DIM, (h + 1) * DN_HEAD_DIM)
            o, z, dout = o_ref[:, hs], z_ref[:, hs], do_ref[:, hs]
            r = lax.rsqrt(jnp.mean(o * o, axis=-1, keepdims=True) + EPS)
            oh = o * r
            sz = _sigmoid(z)
            dz_ref[:, hs] = dout * (oh * nw) * (sz + z * sz * (1.0 - sz))
            don = dout * (z * sz)
            dnw += jnp.sum(don * oh, axis=0, keepdims=True)
            doh = don * nw
            dos_ref[:, hs] = r * (doh - oh * jnp.mean(doh * oh, axis=-1, keepdims=True))
        acc_ref[0:1, :] += dnw
        for c in reversed(range(2 * SCAN_PAIRS)):
            rows = slice(c * DN_CHUNK, (c + 1) * DN_CHUNK)
            diag = slice((c % 2) * DN_CHUNK, (c % 2 + 1) * DN_CHUNK)
            other = slice((1 - c % 2) * DN_CHUNK, (2 - c % 2) * DN_CHUNK)
            for h in range(DN_HEADS):
                hs = slice(h * DN_HEAD_DIM, (h + 1) * DN_HEAD_DIM)
                do, st, dsp, vn = dos_ref[rows, hs], sall_ref[c, h], ds_ref[h], vn_ref[rows, hs]
                da_ref[h, rows, diag] = _dot(do, vn, NT)
                da_ref[h, rows, other] = jnp.zeros((DN_CHUNK, DN_CHUNK), F32)
                du_ref[rows, hs] = (_dot(a_ref[h, rows, diag], do, TN) + _dot(kd_ref[rows, hs], dsp)).astype(du_ref.dtype)
                dqg_ref[rows, hs] = _dot(do, st, NT)
                dkd_ref[rows, hs] = _dot(vn, dsp, NT)
                ddl = jnp.sum(jnp.sum(dsp * st, axis=1, keepdims=True), axis=0, keepdims=True)
                ddl_ref[c, h] = jnp.broadcast_to(ddl, (8, 128))
            for h in range(DN_HEADS):
                hs = slice(h * DN_HEAD_DIM, (h + 1) * DN_HEAD_DIM)
                do, st, dvn = dos_ref[rows, hs], sall_ref[c, h], du_ref[rows, hs]
                dw_ref[rows, hs] = (-_dot(dvn, st, NT)).astype(dw_ref.dtype)
                ds_ref[h] = (ds_ref[h] * dl_ref[c, h][0:1, :] + _dot(qg_ref[rows, hs], do, TN)
                             - _dot(w_ref[rows, hs], dvn, TN))

    rows_step = SCAN_PAIRS * PAIR
    tok = pl.BlockSpec((rows_step, DN_W), lambda n: (rev(n), 0))
    hm = pl.BlockSpec((DN_HEADS, rows_step, PAIR), lambda n: (0, rev(n), 0))
    sc = pl.BlockSpec((2 * SCAN_PAIRS, DN_HEADS, 8, 128), lambda n: (rev(n), 0, 0, 0))
    return pl.pallas_call(
        body, name="dn_scan_bwd", grid=(nstep,),
        in_specs=[tok, tok, tok,
                  pl.BlockSpec((2 * SCAN_PAIRS, DN_HEADS, DN_HEAD_DIM, DN_HEAD_DIM), lambda n: (rev(n), 0, 0, 0)),
                  tok, tok, tok, hm, sc, pl.BlockSpec((rows_step, DN_W), lambda n: (rev(n), 3)), _full((1, 128))],
        out_specs=[tok] * 5 + [hm, sc, _full((8, 128))],
        out_shape=[jax.ShapeDtypeStruct((t, DN_W), F32)] + [jax.ShapeDtypeStruct((t, DN_W), MXU_DTYPE)] * 2
                  + [jax.ShapeDtypeStruct((t, DN_W), F32)] * 2 + [jax.ShapeDtypeStruct((DN_HEADS, t, PAIR), F32),
                   jax.ShapeDtypeStruct((2 * npair, DN_HEADS, 8, 128), F32), jax.ShapeDtypeStruct((8, 128), F32)],
        scratch_shapes=[pltpu.VMEM((DN_HEADS, DN_HEAD_DIM, DN_HEAD_DIM), F32), pltpu.VMEM((SCAN_PAIRS * PAIR, DN_W), F32)],
        compiler_params=_params("arbitrary"),
    )(dout, o, vnew, sall, w, qg, kd, a_qk, dlast, pd, dn_w)


def _dn_intra_bwd(qn, kn, vc, gc, gr, tinv, a_qk, du, dw, dqg, dkd, da_qk, ddlast, dlast, dep):
    t = qn.shape[0]
    npair = t // PAIR

    def body(q_ref, k_ref, v_ref, gc_ref, gr_ref, ti_ref, a_ref, du_ref, dw_ref, dqg_ref, dkd_ref, da_ref, ddl_ref, dl_ref,
             dep_ref, dq_ref, dk_ref, dv_ref, dg_ref):
        low, strict = _pair_masks()
        lane = lax.broadcasted_iota(jnp.int32, (PAIR, 128), 1)
        rowi = lax.broadcasted_iota(jnp.int32, (PAIR, 1), 0)
        rsum = lambda v: jnp.sum(v, axis=-1, keepdims=True)
        items = []
        for p in range(INTRA_PAIRS):
            rows = slice(p * PAIR, (p + 1) * PAIR)
            gc_v = gc_ref[rows, :]
            cum_c, cum_r = _pair_cumsums(gc_v, gr_ref[:, rows], low)
            for h in range(DN_HEADS):
                hs = slice(h * DN_HEAD_DIM, (h + 1) * DN_HEAD_DIM)
                items.append((p, h, rows, hs, _pair_gates(gc_v, cum_c, cum_r, low, h)))
        dtis, lmats, dvbs, dkbgs = [], [], [], []
        for p, h, rows, hs, (beta, gam, dm, gl) in items:
            k, tinv = k_ref[rows, hs], ti_ref[h, rows, :]
            kb = k * beta
            dtis.append(_dot(du_ref[rows, hs], v_ref[rows, hs] * beta, NT)
                        + _dot(dw_ref[rows, hs], kb * jnp.exp(gam), NT))
            lmats.append(jnp.where(strict, _dot(kb, k, NT) * dm, 0.0))
            dvbs.append(_dot(tinv, du_ref[rows, hs], TN))
            dkbgs.append(_dot(tinv, dw_ref[rows, hs], TN))
        xs = [_dot(ti_ref[h, rows, :], dti, TN) for (p, h, rows, hs, g), dti in zip(items, dtis)]
        dls = [jnp.where(strict, -_dot(x, ti_ref[h, rows, :], NT), 0.0) for (p, h, rows, hs, g), x in zip(items, xs)]
        dgam_all = [jnp.zeros((PAIR, 128), F32) for _ in range(INTRA_PAIRS)]
        dbeta_all = [jnp.zeros((PAIR, 128), F32) for _ in range(INTRA_PAIRS)]
        for (p, h, rows, hs, (beta, gam, dm, gl)), dl, lmat, dvb, dkbg in zip(items, dls, lmats, dvbs, dkbgs):
            q, k, v = q_ref[rows, hs], k_ref[rows, hs], v_ref[rows, hs]
            a = a_ref[h, rows, :]
            dqg, dkd = dqg_ref[rows, hs], dkd_ref[rows, hs]
            kb = k * beta
            eg = jnp.exp(gam)
            ekd = jnp.exp(gl - gam)
            dmm = dl * dm
            dam = jnp.where(low, da_ref[h, rows, :], 0.0)
            dn = dam * dm
            e = dl * lmat + dam * a
            dkb = _dot(dmm, k) + dkbg * eg
            dk_ref[rows, hs] = _dot(dmm, kb, TN) + _dot(dn, q, TN) + dkd * ekd + dkb * beta
            dq_ref[rows, hs] = _dot(dn, k) + dqg * eg
            dv_ref[rows, hs] = dvb * beta
            t_kd = rsum(dkd * (k * ekd))
            dgam = rsum(e) - rsum(e.T) + rsum(dqg * (q * eg)) + rsum(dkbg * (kb * eg)) - t_kd
            for c in range(2):
                crows = slice(c * DN_CHUNK, (c + 1) * DN_CHUNK)
                dgl = (jnp.sum(t_kd[crows, :], axis=0, keepdims=True)
                       + ddl_ref[2 * p + c, h][0:1, 0:1] * dl_ref[2 * p + c, h][0:1, 0:1])
                dgam = dgam + jnp.where(rowi == (c + 1) * DN_CHUNK - 1, dgl, 0.0)
            dgam_all[p] += jnp.where(lane == DN_HEADS + h, dgam, 0.0)
            dbeta_all[p] += jnp.where(lane == h, rsum(dkb * k) + rsum(dvb * v), 0.0)
        for p in range(INTRA_PAIRS):
            dg_ref[p * PAIR:(p + 1) * PAIR, :] = dbeta_all[p] + _dot(low.astype(F32), dgam_all[p], TN, HI)

    rows_step = INTRA_PAIRS * PAIR
    tok = lambda w: pl.BlockSpec((rows_step, w), lambda n: (n, 0))
    hm = pl.BlockSpec((DN_HEADS, rows_step, PAIR), lambda n: (0, n, 0))
    sc = pl.BlockSpec((2 * INTRA_PAIRS, DN_HEADS, 8, 128), lambda n: (n, 0, 0, 0))
    return pl.pallas_call(
        body, name="dn_intra_bwd", grid=(npair // INTRA_PAIRS,),
        in_specs=[tok(DN_W), tok(DN_W), tok(DN_W), tok(128), pl.BlockSpec((8, rows_step), lambda n: (0, n)), hm, hm,
                  tok(DN_W), tok(DN_W), tok(DN_W), tok(DN_W), hm, sc, sc, pl.BlockSpec(memory_space=pl.ANY)],
        out_specs=[tok(DN_W), tok(DN_W), tok(DN_W), tok(128)],
        out_shape=[jax.ShapeDtypeStruct((t, DN_W), F32)] * 3 + [jax.ShapeDtypeStruct((t, 128), F32)],
        compiler_params=_params("parallel"),
    )(qn, kn, vc, gc, gr, tinv, a_qk, du, dw, dqg, dkd, da_qk, ddlast, dlast, dep)


def _dn_prep_bwd(pd, conv_w, al_vec, dt_vec, dqn, dkn, dvc, dgc, dz, tm):
    t = pd.shape[0]
    nt = t // tm
    tile = lambda i: nt - 1 - i

    def body(cur_ref, prev_ref, ba_ref, cw_ref, al_ref, dt_ref, dq_ref, dk_ref, dv_ref, dg_ref, dz_ref,
             o_ref, accw_ref, accg_ref, xs_ref, dc_ref, ds_ref, carry_ref):
        @pl.when(pl.program_id(0) == 0)
        def _():
            accw_ref[...] = jnp.zeros_like(accw_ref)
            accg_ref[...] = jnp.zeros_like(accg_ref)
            carry_ref[...] = jnp.zeros_like(carry_ref)

        _conv_window(cur_ref, prev_ref, xs_ref, tm, tile(pl.program_id(0)) > 0)
        taps = [xs_ref[pl.ds(HALO - DN_CONV + 1 + j, tm), :] for j in range(DN_CONV)]
        y = cw_ref[0:1, :] * taps[0]
        for j in range(1, DN_CONV):
            y += cw_ref[j:j + 1, :] * taps[j]
        sg = _sigmoid(y)
        c = y * sg
        for h in range(DN_HEADS):
            qs = slice(h * DN_HEAD_DIM, (h + 1) * DN_HEAD_DIM)
            ksl = slice(DN_W + h * DN_HEAD_DIM, DN_W + (h + 1) * DN_HEAD_DIM)
            for src, sl, scale in ((dq_ref, qs, DN_SCALE), (dk_ref, ksl, 1.0)):
                xh = c[:, sl]
                r = lax.rsqrt(jnp.sum(xh * xh, axis=-1, keepdims=True) + EPS)
                unit = xh * r
                dn = src[:, qs] * scale
                dc_ref[:, sl] = r * (dn - unit * jnp.sum(dn * unit, axis=-1, keepdims=True))
        dc_ref[:, 2 * DN_W:3 * DN_W] = dv_ref[...]
        dy = dc_ref[...] * (sg + y * sg * (1.0 - sg))
        for j in range(DN_CONV):
            accw_ref[j:j + 1, :] += jnp.sum(dy * taps[j], axis=0, keepdims=True)
        ds_ref[0:tm, :] = dy
        ds_ref[tm:tm + HALO, :] = carry_ref[...]
        carry_ref[...] = ds_ref[0:HALO, :]
        dx = cw_ref[0:1, :] * ds_ref[pl.ds(DN_CONV - 1, tm), :]
        for j in range(1, DN_CONV):
            dx += cw_ref[j:j + 1, :] * ds_ref[pl.ds(DN_CONV - 1 - j, tm), :]

        beta, g, pre = _gate_values(ba_ref[...], al_ref[...], dt_ref[...])
        dgb = dg_ref[...]
        lane = lax.broadcasted_iota(jnp.int32, dgb.shape, 1)
        is_b, is_a = lane < DN_HEADS, (lane >= DN_HEADS) & (lane < 2 * DN_HEADS)
        dpre = dgb * (-jnp.exp(al_ref[...])) * _sigmoid(pre)
        dba = jnp.where(is_b, dgb * beta * (1.0 - beta), jnp.where(is_a, dpre, 0.0))
        accg_ref[0:1, :] += jnp.sum(jnp.where(is_a, dgb * g, 0.0), axis=0, keepdims=True)
        accg_ref[1:2, :] += jnp.sum(jnp.where(is_a, dpre, 0.0), axis=0, keepdims=True)
        o_ref[:, 0:CONV_CH] = dx.astype(o_ref.dtype)
        o_ref[:, CONV_CH:CONV_CH + DN_W] = dz_ref[...].astype(o_ref.dtype)
        o_ref[:, CONV_CH + DN_W:DN_COLS] = dba.astype(o_ref.dtype)

    tok = lambda w: pl.BlockSpec((tm, w), lambda i: (tile(i), 0))
    return pl.pallas_call(
        body, name="dn_prep_bwd", grid=(nt,),
        in_specs=_dn_prep_specs(tm, tile) + [tok(DN_W), tok(DN_W), tok(DN_W), tok(128), tok(DN_W)],
        out_specs=[tok(DN_COLS), _full((8, CONV_CH)), _full((8, 128))],
        out_shape=[jax.ShapeDtypeStruct((t, DN_COLS), MXU_DTYPE),
                   jax.ShapeDtypeStruct((8, CONV_CH), F32), jax.ShapeDtypeStruct((8, 128), F32)],
        scratch_shapes=[pltpu.VMEM((HALO + tm, CONV_CH), F32), pltpu.VMEM((tm, CONV_CH), F32),
                        pltpu.VMEM((tm + HALO, CONV_CH), F32), pltpu.VMEM((HALO, CONV_CH), F32)],
        compiler_params=_params("arbitrary"),
    )(pd, pd, pd, conv_w, al_vec, dt_vec, dqn, dkn, dvc, dgc, dz)


def _pad_lanes(v, offset=0):
    return jnp.zeros((1, 128), F32).at[0, offset:offset + v.shape[0]].set(v.astype(F32))


class _LocalReducer:
    def start(self, grads):
        return jnp.zeros((8, 128), F32)

    def middle(self, after):
        return jnp.zeros((8, 128), F32)

    def finish(self, after):
        return None


def _local_step(x, p, tgt, sm, w, late, reducer):
    t = x.shape[0]
    tm = min(512, t // 2)
    tm_s = min(512, t // 2)
    tw = min(1024, t // 2)

    w_in_t = w["w_in_t"]
    conv_w = w["conv_w"]
    al_vec, dt_vec = _pad_lanes(sm["a_log"], DN_HEADS), _pad_lanes(sm["dt_bias"], DN_HEADS)
    sinks_vec = _pad_lanes(sm["sinks"])
    dn_w = sm["dn_norm"].reshape(1, 128)
    row = lambda v: v.reshape(1, D_MODEL)
    cos, sin = _rope_tables(t)

    u, pa, pd = _inproj(x, row(sm["norm_mix"]), w_in_t, ATTN_Q + 2 * ATTN_KV, tm_s)
    ao = _attn_fwd(pa, cos, sin, sinks_vec)
    qn, kn, vc, gc, gr = _dn_prep(pd, conv_w, al_vec, dt_vec, tm_s)
    uu, ww, qg, kd, a_qk, tinv, dlast = _dn_intra(qn, kn, vc, gc, gr)
    dn_out, o, vnew, sall = _dn_scan_fwd(uu, ww, qg, kd, a_qk, dlast, pd, dn_w)
    w_o, late_rest = late(dn_out)
    wo_a, wo_d = w_o[:ATTN_Q], w_o[ATTN_Q:]
    h1 = _oproj(x, ao, dn_out, wo_a, wo_d, tm)
    w = dict(w, **late_rest(h1))
    w_proj = jnp.transpose(w["w_proj4"], (1, 0, 2)).reshape(PLE_DIM, D_MODEL)
    m, r, h2 = _mlp_fwd(h1, row(sm["norm_mlp"]), w["w_up4"], w["w_down"], tw)
    dh2, dh2b, dgp, dpp, n3, pb, acc_ple = _ple_loss(h2, p, tgt, row(sm["norm_ple"]), row(sm["norm_final"]),
                                                     w["w_gate"], w_proj, tm_s)
    g_w_gate = _wgrad(n3, dgp, "wgrad_gate", D_MODEL, D_MODEL, tw)
    g_w_proj = _wgrad(pb, dpp, "wgrad_proj", PLE_DIM, D_MODEL, tw)
    da, dh1, dh1b, acc_mlp = _mlp_bwd(dh2, dh2b, r, h1, row(sm["norm_mlp"]), w["w_up4"], w["w_down"], tm)
    g_w_up4 = _wgrad(m, da, "wgrad_up", D_MODEL, FF_BLOCK, tw, stacked=True)
    g_w_down = _wgrad(r, dh2b, "wgrad_down", FF_BLOCK, D_MODEL, tw,
                      prep=lambda rv: jnp.square(rv.astype(F32)).astype(MXU_DTYPE))
    g_w_o = _wgrad_cat([ao, dn_out], [dh1b], "wgrad_o", tw)
    early = dict(w_up4=g_w_up4, w_down=g_w_down, w_gate=g_w_gate, w_proj=g_w_proj, w_o=g_w_o)
    dep = reducer.start(early)
    dao, ddn = _oproj_bwd(dh1b, wo_a, wo_d, tm, dep)
    dz, du, dw, dqg, dkd, da_qk, ddlast, acc_dn = _dn_scan_bwd(ddn, o, vnew, sall, ww, qg, kd, a_qk, dlast, pd, dn_w)
    dep = reducer.middle(du)
    dqn, dkn, dvc, dgc = _dn_intra_bwd(qn, kn, vc, gc, gr, tinv, a_qk, du, dw, dqg, dkd, da_qk, ddlast, dlast, dep)
    d_dn, acc_conv, acc_gate = _dn_prep_bwd(pd, conv_w, al_vec, dt_vec, dqn, dkn, dvc, dgc, dz, tm_s)
    dq, dk, dv, acc_attn = _attn_bwd(pa, cos, sin, sinks_vec, dao)
    reducer.finish(dq)
    dx, acc_mix = _inproj_bwd(x, dh1, row(sm["norm_mix"]), [dq, dk, dv, d_dn], w_in_t, tm_s)

    g_w_in_t = _wgrad_cat([dq, dk, dv, d_dn], [u], "wgrad_in", tw)
    grads = dict(early, w_in_t=g_w_in_t)
    sums = dict(loss=acc_ple[2, 0], norm_final=acc_ple[0], norm_ple=acc_ple[1], norm_mlp=acc_mlp[0], norm_mix=acc_mix[0],
                dn_norm=acc_dn[0], sinks=acc_attn[0, :ATTN_HEADS], a_log=acc_gate[0, DN_HEADS:2 * DN_HEADS],
                dt_bias=acc_gate[1, DN_HEADS:2 * DN_HEADS], conv_w=acc_conv[:DN_CONV])
    return sums, dx, grads


MESH = pl.DeviceIdType.MESH
ANY = pl.BlockSpec(memory_space=pl.ANY)
N_CHIPS = 4
N_DEV = 8


def _place():
    x, y, c = lax.axis_index("x"), lax.axis_index("y"), lax.axis_index("c")
    chips = [(1 - x, y), (x, 1 - y), (1 - x, 1 - y)]
    return x, y, c, chips


def _gather_weights(shards, conv_s):
    n = len(shards)
    per = 7

    def body(*refs):
        in_refs, conv_ref = refs[:n], refs[n]
        out_refs, conv_out = refs[n + 1:2 * n + 1], refs[2 * n + 1]
        send_sems, recv_sems = refs[2 * n + 2:]
        x, y, c, chips = _place()
        sibling = (x, y, 1 - c)

        def blk(a, px, py, pc):
            hr = in_refs[a].shape[0] // 2
            return out_refs[a].at[2 * px + py, pl.ds(pc * hr, hr), :]

        def mine(a):
            hr = in_refs[a].shape[0] // 2
            return in_refs[a].at[pl.ds(c * hr, hr), :]

        def rcopy(a, k, block, to, src=None):
            return pltpu.make_async_remote_copy(
                src_ref=blk(a, *block) if src is None else src, dst_ref=blk(a, *block),
                send_sem=send_sems.at[per * a + k], recv_sem=recv_sems.at[per * a + k],
                device_id=to, device_id_type=MESH)

        def whole(a, to):
            return pltpu.make_async_remote_copy(
                src_ref=in_refs[a], dst_ref=out_refs[a].at[2 * x + y],
                send_sem=send_sems.at[per * a], recv_sem=recv_sems.at[per * a], device_id=to, device_id_type=MESH)

        def ccopy(j, to):
            return pltpu.make_async_remote_copy(
                src_ref=conv_ref, dst_ref=conv_out.at[2 * x + y],
                send_sem=send_sems.at[per * n + j], recv_sem=recv_sems.at[per * n + j],
                device_id=to, device_id_type=MESH)

        started = []
        for a in range(n):
            first = [whole(a, sibling)]
            first += [rcopy(a, 1 + j, (x, y, c), (*chip, c), src=mine(a)) for j, chip in enumerate(chips)]
            for cp in first:
                cp.start()
            started += first
        conv_sends = [ccopy(j, (*chip, c)) for j, chip in enumerate(chips)] + [ccopy(3, sibling)]
        for cp in conv_sends:
            cp.start()
        started += conv_sends
        for a in range(n):
            for j, chip in enumerate(chips):
                rcopy(a, 1 + j, (*chip, c), (x, y, c)).wait_recv()
                fwd = rcopy(a, 4 + j, (*chip, c), sibling)
                fwd.start()
                started.append(fwd)
        for a in range(n):
            whole(a, sibling).wait_recv()
            for j, chip in enumerate(chips):
                rcopy(a, 4 + j, (*chip, 1 - c), (x, y, c)).wait_recv()
        for j, chip in enumerate(chips + [(x, y)]):
            pltpu.make_async_remote_copy(
                src_ref=conv_ref, dst_ref=conv_out.at[2 * chip[0] + chip[1]],
                send_sem=send_sems.at[per * n + j], recv_sem=recv_sems.at[per * n + j],
                device_id=sibling, device_id_type=MESH).wait_recv()
        for cp in started:
            cp.wait_send()

    nsem = per * n + 4
    out_shape = [jax.ShapeDtypeStruct((N_CHIPS,) + s.shape, s.dtype) for s in shards]
    out_shape.append(jax.ShapeDtypeStruct((N_CHIPS,) + conv_s.shape, conv_s.dtype))
    return pl.pallas_call(
        body, name="gather_weights", in_specs=[ANY] * (n + 1), out_specs=[ANY] * (n + 1), out_shape=out_shape,
        scratch_shapes=[pltpu.SemaphoreType.DMA((nsem,)), pltpu.SemaphoreType.DMA((nsem,))],
    )(*shards, conv_s)


HBM = pl.BlockSpec(memory_space=pltpu.HBM)
SEM = pl.BlockSpec(memory_space=pltpu.SEMAPHORE)
EFFECT = pltpu.SideEffectType.DATAFLOW_SIDE_EFFECTING
LATE_COPIES = 7


def _late_copies(in_refs, land_refs, send_sems, recv_sems, only=None):
    x, y, c, chips = _place()
    sends, arrivals = [], []
    for a, (src, land) in enumerate(zip(in_refs, land_refs)):
        if only is not None and a not in only:
            continue
        hr = src.shape[0] // 2
        base = LATE_COPIES * a

        def cp(src_ref, dst_ref, s_idx, r_idx, to):
            return pltpu.make_async_remote_copy(src_ref=src_ref, dst_ref=dst_ref, send_sem=send_sems.at[base + s_idx],
                                                recv_sem=recv_sems.at[base + r_idx], device_id=to, device_id_type=MESH)

        sends.append(cp(src, land.at[2 * x + y], 0, 0, (x, y, 1 - c)))
        arrivals.append(cp(src, land.at[2 * x + y], 0, 0, (x, y, 1 - c)))
        for j, chip in enumerate(chips):
            for pc in range(2):
                half = src.at[pl.ds(c * hr, hr), :]
                sends.append(cp(half, land.at[2 * x + y, pl.ds(c * hr, hr), :], 1 + 2 * j + pc, 1 + 2 * j + c, (*chip, pc)))
                arrivals.append(cp(half, land.at[2 * chip[0] + chip[1], pl.ds(pc * hr, hr), :], 1 + 2 * j + pc,
                                   1 + 2 * j + pc, (*chip, pc)))
    return sends, arrivals


def _copies_start(name, build, nsem, srcs, land_shapes, after):
    n = len(srcs)

    def body(*refs):
        sends, _ = build(refs[:n], refs[n:2 * n], refs[2 * n + 1], refs[2 * n + 2])
        for cp in sends:
            cp.start()
        refs[-1][...] = jnp.zeros_like(refs[-1])

    lands = [pltpu.with_memory_space_constraint(lax.empty(s.shape, s.dtype), pltpu.HBM) for s in land_shapes]
    ins = [pltpu.with_memory_space_constraint(s, pltpu.HBM) for s in srcs]
    out = pl.pallas_call(
        body, name=name,
        out_shape=(pltpu.SemaphoreType.DMA((nsem,)), pltpu.SemaphoreType.DMA((nsem,)),
                   *[pltpu.HBM(s.shape, s.dtype) for s in srcs], *[pltpu.HBM(s.shape, s.dtype) for s in land_shapes],
                   jax.ShapeDtypeStruct((8, 128), F32)),
        in_specs=[HBM] * (2 * n) + [ANY],
        out_specs=(SEM, SEM, *[HBM] * (2 * n), pl.BlockSpec(memory_space=pltpu.VMEM)),
        input_output_aliases={i: 2 + i for i in range(2 * n)},
        compiler_params=pltpu.CompilerParams(has_side_effects=EFFECT),
    )(*ins, *lands, after)
    return out[0], out[1], out[2:2 + n], out[2 + n:2 + 2 * n], out[-1]


def _copies_wait(name, build, started, after):
    send_sems, recv_sems, srcs, lands, _ = started
    n = len(srcs)

    def body(*refs):
        sends, arrivals = build(refs[:n], refs[n:2 * n], refs[2 * n], refs[2 * n + 1])
        for cp in sends:
            cp.wait_send()
        for cp in arrivals:
            cp.wait_recv()

    out = pl.pallas_call(
        body, name=name,
        out_shape=(*[pltpu.HBM(s.shape, s.dtype) for s in srcs], *[pltpu.HBM(l.shape, l.dtype) for l in lands]),
        in_specs=[HBM] * (2 * n) + [SEM, SEM, ANY],
        out_specs=tuple([HBM] * (2 * n)),
        input_output_aliases={i: i for i in range(2 * n)},
        compiler_params=pltpu.CompilerParams(has_side_effects=EFFECT),
    )(*srcs, *lands, send_sems, recv_sems, after)
    return out[:n], out[n:]


def _exchange_copies(g_refs, got_refs, send_sems, recv_sems):
    x, y, c, _ = _place()
    sends, arrivals = [], []
    for a, (g, got) in enumerate(zip(g_refs, got_refs)):
        hr = g.shape[1] // 2
        cp = pltpu.make_async_remote_copy(
            src_ref=g.at[:, pl.ds((1 - c) * hr, hr), :], dst_ref=got, send_sem=send_sems.at[a],
            recv_sem=recv_sems.at[a], device_id=(x, y, 1 - c), device_id_type=MESH)
        sends.append(cp)
        arrivals.append(cp)
    return sends, arrivals


def _scatter_copies(s_refs, got_refs, send_sems, recv_sems):
    x, y, c, chips = _place()
    sends, arrivals = [], []
    for a, (s16, got) in enumerate(zip(s_refs, got_refs)):
        for j, chip in enumerate(chips):
            cp = pltpu.make_async_remote_copy(
                src_ref=s16.at[2 * chip[0] + chip[1]], dst_ref=got.at[j], send_sem=send_sems.at[3 * a + j],
                recv_sem=recv_sems.at[3 * a + j], device_id=(*chip, c), device_id_type=MESH)
            sends.append(cp)
            arrivals.append(cp)
    return sends, arrivals


def _share_halves(name, bufs, dep):
    n = len(bufs)

    def body(*refs):
        out_refs = refs[n + 1:2 * n + 1]
        send_sems, recv_sems = refs[2 * n + 1:]
        x, y, c, _ = _place()
        remote = [pltpu.make_async_remote_copy(
            src_ref=out_refs[a].at[c], dst_ref=out_refs[a].at[c], send_sem=send_sems.at[a], recv_sem=recv_sems.at[a],
            device_id=(x, y, 1 - c), device_id_type=MESH) for a in range(n)]
        for cp in remote:
            cp.start()
        for a in range(n):
            pltpu.make_async_remote_copy(
                src_ref=out_refs[a].at[c], dst_ref=out_refs[a].at[1 - c], send_sem=send_sems.at[a],
                recv_sem=recv_sems.at[a], device_id=(x, y, 1 - c), device_id_type=MESH).wait_recv()
        for cp in remote:
            cp.wait_send()

    return pl.pallas_call(
        body, name=name, in_specs=[ANY] * (n + 1), out_specs=[ANY] * n,
        out_shape=[jax.ShapeDtypeStruct(b.shape, b.dtype) for b in bufs],
        input_output_aliases={a: a for a in range(n)},
        scratch_shapes=[pltpu.SemaphoreType.DMA((n,)), pltpu.SemaphoreType.DMA((n,))],
    )(*bufs, dep)


SMALL_ROWS, SMALL_COLS = 16, CONV_CH


def _allreduce_small(block):
    m_per, ncol = block.shape

    def body(x_ref, sum_ref, all_ref, send_sems, recv_sems, local_sem):
        x, y, c, chips = _place()
        me, sibling = (x, y, c), (x, y, 1 - c)

        def rows(px, py, pc):
            return all_ref.at[pl.ds((4 * px + 2 * py + pc) * m_per, m_per), :]

        def copy(k, block_of, to, src=None):
            return pltpu.make_async_remote_copy(
                src_ref=rows(*block_of) if src is None else src, dst_ref=rows(*block_of),
                send_sem=send_sems.at[k], recv_sem=recv_sems.at[k], device_id=to, device_id_type=MESH)

        mine = pltpu.make_async_copy(x_ref, rows(*me), local_sem)
        mine.start()
        first = [copy(0, me, sibling, src=x_ref)]
        first += [copy(1 + j, me, (*chip, c), src=x_ref) for j, chip in enumerate(chips)]
        for cp in first:
            cp.start()
        passed = [copy(4 + j, (*chip, c), sibling) for j, chip in enumerate(chips)]
        for j, chip in enumerate(chips):
            copy(1 + j, (*chip, c), me).wait_recv()
            passed[j].start()
        copy(0, sibling, me).wait_recv()
        for j, chip in enumerate(chips):
            copy(4 + j, (*chip, 1 - c), me).wait_recv()
        for cp in first + passed:
            cp.wait_send()
        mine.wait()
        total = all_ref[0:m_per, :]
        for d in range(1, N_DEV):
            total = total + all_ref[d * m_per:(d + 1) * m_per, :]
        sum_ref[...] = total

    vm = pl.BlockSpec(memory_space=pltpu.VMEM)
    return pl.pallas_call(
        body, name="allreduce_small", in_specs=[vm], out_specs=vm,
        out_shape=jax.ShapeDtypeStruct((m_per, ncol), F32),
        scratch_shapes=[pltpu.VMEM((N_DEV * m_per, ncol), F32), pltpu.SemaphoreType.DMA((7,)),
                        pltpu.SemaphoreType.DMA((7,)), pltpu.SemaphoreType.DMA],
    )(block)


def _row_tile(rows, cols):
    tile = rows
    while tile * cols * 4 > (1 << 20) and tile % 16 == 0:
        tile //= 2
    return tile


def _elementwise(fn, name, ins, out_dtypes, dep):
    rows, cols = ins[0].shape
    tile = _row_tile(rows, cols)

    def body(*refs):
        outs = fn(*[r[...] for r in refs[:len(ins)]])
        for o_ref, o in zip(refs[len(ins) + 1:], outs):
            o_ref[...] = o.astype(o_ref.dtype)

    if tile * cols * 4 > (1 << 21) and cols % 512 == 0:
        spec = pl.BlockSpec((rows, 256), lambda i: (0, i))
        steps = cols // 256
    else:
        spec = pl.BlockSpec((tile, cols), lambda i: (i, 0))
        steps = rows // tile
    return pl.pallas_call(
        body, name=name, grid=(steps,), in_specs=[spec] * len(ins) + [pl.BlockSpec(memory_space=pl.ANY)],
        out_specs=[spec] * len(out_dtypes),
        out_shape=[jax.ShapeDtypeStruct((rows, cols), d) for d in out_dtypes],
        compiler_params=_params("parallel"),
    )(*ins, dep)


def _adamw_tile(w, g, m, v):
    m = ADAM_B1 * m + (1.0 - ADAM_B1) * g
    v = ADAM_B2 * v + (1.0 - ADAM_B2) * jnp.square(g)
    m_hat = m / (1.0 - ADAM_B1 ** ADAM_STEP)
    v_hat = v / (1.0 - ADAM_B2 ** ADAM_STEP)
    delta = -ADAM_LR * (m_hat / (jnp.sqrt(v_hat) + ADAM_EPS) + ADAM_WD * w)
    return delta, m, v


def _adamw(name, w, g, m, v, dep):
    return _elementwise(_adamw_tile, name, [w, g, m, v], [F32, F32, F32], dep)


def _chip_sum(name, g4, got, place):
    nchip, hr, cols = got.shape
    tile = _row_tile(hr, cols)
    nblk = hr // tile

    def body(pl_ref, g_ref, o_ref, s32_ref, s16_ref):
        s = g_ref[...] + o_ref[...]
        s32_ref[...] = s
        s16_ref[...] = s.astype(BF16)

    spec = pl.BlockSpec((None, tile, cols), lambda k, i, pr: (k, i, 0))
    return pl.pallas_call(
        body, name=name,
        grid_spec=pltpu.PrefetchScalarGridSpec(
            num_scalar_prefetch=1, grid=(nchip, nblk),
            in_specs=[pl.BlockSpec((None, tile, cols), lambda k, i, pr: (k, pr[1] * nblk + i, 0)), spec],
            out_specs=[spec, spec]),
        out_shape=[jax.ShapeDtypeStruct(got.shape, F32), jax.ShapeDtypeStruct(got.shape, BF16)],
        compiler_params=_params("parallel", "parallel"),
    )(place, g4, got)


def _mesh_sum(name, s32, got, place):
    _, hr, cols = s32.shape
    tile = _row_tile(hr, cols)

    def body(pl_ref, own_ref, g0_ref, g1_ref, g2_ref, o_ref):
        o_ref[...] = ((own_ref[...] + g0_ref[...].astype(F32)) + g1_ref[...].astype(F32)) + g2_ref[...].astype(F32)

    slab = lambda j: pl.BlockSpec((None, tile, cols), lambda i, pr: (j, i, 0))
    return pl.pallas_call(
        body, name=name,
        grid_spec=pltpu.PrefetchScalarGridSpec(
            num_scalar_prefetch=1, grid=(hr // tile,),
            in_specs=[pl.BlockSpec((None, tile, cols), lambda i, pr: (pr[0], i, 0)), slab(0), slab(1), slab(2)],
            out_specs=pl.BlockSpec((None, tile, cols), lambda i, pr: (pr[1], i, 0))),
        out_shape=jax.ShapeDtypeStruct((2, hr, cols), F32),
        compiler_params=_params("parallel"),
    )(place, s32, got, got, got)


def _place_operand():
    return jnp.stack([2 * lax.axis_index("x") + lax.axis_index("y"), lax.axis_index("c")]).astype(jnp.int32)


W_IN_ROWS = 720
W_IN_GATHER_ROWS = 736


def _per_chip(name, g):
    if name == "w_in_t":
        rows = D_IN // N_CHIPS
        return jnp.stack([lax.slice_in_dim(g, rows * k, rows * k + W_IN_ROWS) for k in range(N_CHIPS)])
    if name == "w_proj":
        return jnp.transpose(g.reshape(PLE_DIM, N_CHIPS, D_MODEL // N_CHIPS), (1, 0, 2))
    if name == "w_up4":
        return g
    return g.reshape(N_CHIPS, g.shape[0] // N_CHIPS, g.shape[1])


class _EarlyReducer:
    def __init__(self, tag):
        self.tag = tag

    def start(self, grads):
        self.names = list(grads)
        self.place = _place_operand()
        slabs = [_per_chip(k, grads[k]) for k in self.names]
        halves = [jax.ShapeDtypeStruct((s.shape[0], s.shape[1] // 2, s.shape[2]), F32) for s in slabs]
        self.a = _copies_start(self.tag + "exchange_start", _exchange_copies, len(slabs), slabs, halves,
                               slabs[0][0, :8, :128])
        return self.a[-1]

    def middle(self, after):
        slabs, got = _copies_wait(self.tag + "exchange_wait", _exchange_copies, self.a, after)
        self.sums = [_chip_sum(self.tag + "chip_sum_" + k, s, g, self.place) for k, s, g in zip(self.names, slabs, got)]
        s16 = [s[1] for s in self.sums]
        lands = [jax.ShapeDtypeStruct((3,) + s.shape[1:], BF16) for s in s16]
        self.b = _copies_start(self.tag + "scatter_start", _scatter_copies, 3 * len(s16), s16, lands,
                               self.sums[0][0][0, :8, :128])
        return self.b[-1]

    def finish(self, after):
        _, got = _copies_wait(self.tag + "scatter_wait", _scatter_copies, self.b, after)
        self.bufs = {k: _mesh_sum(self.tag + "mesh_sum_" + k, s[0], g, self.place)
                     for k, s, g in zip(self.names, self.sums, got)}


def kernel(x, p, norm_mix, w_in, conv_w, a_log, dt_bias, dn_norm, sinks, w_o, norm_mlp, w_up, w_down, norm_ple, w_ple_gate, w_ple_proj, norm_final, loss_target, m_norm_mix, m_w_in, m_conv_w, m_a_log, m_dt_bias, m_dn_norm, m_sinks, m_w_o, m_norm_mlp, m_w_up, m_w_down, m_norm_ple, m_w_ple_gate, m_w_ple_proj, m_norm_final, v_norm_mix, v_w_in, v_conv_w, v_a_log, v_dt_bias, v_dn_norm, v_sinks, v_w_o, v_norm_mlp, v_w_up, v_w_down, v_norm_ple, v_w_ple_gate, v_w_ple_proj, v_norm_final):
    chip = 2 * lax.axis_index("x") + lax.axis_index("y")
    big = dict(w_in=w_in[0], w_o=w_o[0], w_up=w_up[0], w_down=w_down[0], w_gate=w_ple_gate[0], w_proj=w_ple_proj[0])
    big_m = dict(w_in=m_w_in[0], w_o=m_w_o[0], w_up=m_w_up[0], w_down=m_w_down[0], w_gate=m_w_ple_gate[0], w_proj=m_w_ple_proj[0])
    big_v = dict(w_in=v_w_in[0], w_o=v_w_o[0], w_up=v_w_up[0], w_down=v_w_down[0], w_gate=v_w_ple_gate[0], w_proj=v_w_ple_proj[0])
    names = list(big)

    rows_in = D_IN // N_CHIPS
    w_in_shard_t = jnp.pad(big["w_in"].T.astype(BF16), ((0, W_IN_GATHER_ROWS - rows_in), (0, 0)))
    w_in_all, conv_all = _gather_weights([w_in_shard_t], conv_w[0])
    late_names = names[1:]
    late_shards = [big[k].astype(BF16) for k in late_names]
    gather = _copies_start("gather_start", _late_copies, LATE_COPIES * len(late_shards), late_shards,
                           [jax.ShapeDtypeStruct((N_CHIPS,) + s.shape, BF16) for s in late_shards], w_in_all)
    token = gather[-1]
    w_in_rows = [w_in_all[k, :rows_in] for k in range(N_CHIPS)] + [jnp.zeros((W_IN_PADDED - D_IN, D_MODEL), BF16)]
    w = dict(w_in_t=jnp.concatenate(w_in_rows, axis=0),
             conv_w=jnp.transpose(conv_all, (1, 0, 2)).reshape(DN_CONV, CONV_CH))
    sm = dict(norm_mix=norm_mix[0] + token[0, 0], a_log=a_log[0], dt_bias=dt_bias[0], dn_norm=dn_norm[0],
              sinks=sinks[0], norm_mlp=norm_mlp[0], norm_ple=norm_ple[0], norm_final=norm_final)

    def late(after):
        first = functools.partial(_late_copies, only=(0,))
        srcs, lands = _copies_wait("gather_wait_o", first, gather, after)

        def rest(after2):
            others = functools.partial(_late_copies, only=tuple(range(1, len(late_names))))
            gw = dict(zip(late_names, _copies_wait("gather_wait_rest", others, gather[:2] + (srcs, lands, None), after2)[1]))
            return dict(w_up4=gw["w_up"], w_down=gw["w_down"].reshape(D_FF, D_MODEL),
                        w_gate=gw["w_gate"].reshape(D_MODEL, D_MODEL), w_proj4=gw["w_proj"])

        return lands[0].reshape(D_MODEL, D_MODEL), rest

    reducer = _EarlyReducer("early_")
    sums, grad_x, g = _local_step(x[0], p[0, 0], loss_target[0], sm, w, late, reducer)

    last = _EarlyReducer("last_")
    dep_a = last.start({"w_in_t": g["w_in_t"]})

    row = lambda v: jnp.zeros((SMALL_COLS,), F32).at[:v.shape[0]].set(v)
    misc = jnp.zeros((SMALL_COLS,), F32).at[0:4].set(sums["a_log"]).at[4:8].set(sums["dt_bias"]) \
        .at[8:16].set(sums["sinks"]).at[128:256].set(sums["dn_norm"]).at[256].set(sums["loss"])
    small = jnp.concatenate([sums["conv_w"], jnp.stack([row(sums["norm_mix"]), row(sums["norm_mlp"]), row(sums["norm_ple"]),
                                                        row(sums["norm_final"]), misc]),
                             jnp.zeros((SMALL_ROWS - 9, SMALL_COLS), F32)], axis=0)
    tot = _allreduce_small(small + dep_a[0, 0])
    dep_b = last.middle(tot)
    grad_key = dict(w_o="w_o", w_up="w_up4", w_down="w_down", w_gate="w_gate", w_proj="w_proj")
    full = _share_halves("share_halves", [reducer.bufs[grad_key[k]] for k in late_names], dep_b)
    red = {k: f.reshape(-1, f.shape[-1]) for k, f in zip(late_names, full)}
    loss = tot[8, 256]
    ncw = CONV_CH // N_CHIPS

    def pack(cw, nmix, nmlp, nple, nfin, al, dtb, sk, dnn):
        misc_p = jnp.zeros((SMALL_COLS,), F32).at[0:4].set(al).at[4:8].set(dtb).at[8:16].set(sk).at[128:256].set(dnn)
        cw_p = jnp.zeros((DN_CONV, SMALL_COLS), F32).at[:, :ncw].set(cw)
        return jnp.concatenate([cw_p, jnp.stack([row(nmix), row(nmlp), row(nple), row(nfin), misc_p]),
                                jnp.zeros((SMALL_ROWS - 9, SMALL_COLS), F32)], axis=0)

    def unpack(buf):
        return dict(conv_w=buf[0:4, :ncw][None], norm_mix=buf[4, :D_MODEL][None], norm_mlp=buf[5, :D_MODEL][None],
                    norm_ple=buf[6, :D_MODEL][None], norm_final=buf[7, :D_MODEL], a_log=buf[8, 0:4][None],
                    dt_bias=buf[8, 4:8][None], sinks=buf[8, 8:16][None], dn_norm=buf[8, 128:256][None])

    g_conv_shard = lax.dynamic_slice(tot[0:4], (0, chip * ncw), (DN_CONV, ncw))
    g_small = pack(g_conv_shard, tot[4, :D_MODEL], tot[5, :D_MODEL], tot[6, :D_MODEL], tot[7, :D_MODEL],
                   tot[8, 0:4], tot[8, 4:8], tot[8, 8:16], tot[8, 128:256])
    w_small = pack(conv_w[0], norm_mix[0], norm_mlp[0], norm_ple[0], norm_final, a_log[0], dt_bias[0], sinks[0], dn_norm[0])
    m_small = pack(m_conv_w[0], m_norm_mix[0], m_norm_mlp[0], m_norm_ple[0], m_norm_final, m_a_log[0], m_dt_bias[0],
                   m_sinks[0], m_dn_norm[0])
    v_small = pack(v_conv_w[0], v_norm_mix[0], v_norm_mlp[0], v_norm_ple[0], v_norm_final, v_a_log[0], v_dt_bias[0],
                   v_sinks[0], v_dn_norm[0])

    ref_name = dict(w_in="w_in", w_o="w_o", w_up="w_up", w_down="w_down", w_gate="w_ple_gate", w_proj="w_ple_proj")
    out_g, out_d, out_m, out_v = {}, {}, {}, {}

    def update(k, dep):
        d_k, m_k, v_k = _adamw("adamw_" + k, big[k], red[k], big_m[k], big_v[k], dep)
        out_g[ref_name[k]], out_d[ref_name[k]] = red[k][None], d_k[None]
        out_m[ref_name[k]], out_v[ref_name[k]] = m_k[None], v_k[None]
        return d_k

    for k in late_names:
        done = update(k, dep_b)
    small_out = _adamw("adamw_small", w_small, g_small, m_small, v_small, dep_b)
    d_s, m_s, v_s = (unpack(b) for b in small_out)
    g_s = unpack(g_small)
    for src, dst in ((g_s, out_g), (d_s, out_d), (m_s, out_m), (v_s, out_v)):
        dst.update(src)
    last.finish(done + small_out[0][0:1, 0:1])
    (w_in_full,) = _share_halves("share_halves_w_in", [last.bufs["w_in_t"]], dep_b)
    g_t = w_in_full.reshape(W_IN_ROWS, D_MODEL)[:D_IN // N_CHIPS]
    d_t, m_t, v_t = _adamw("adamw_w_in", big["w_in"].T, g_t, big_m["w_in"].T, big_v["w_in"].T, dep_b)
    out_g["w_in"], out_d["w_in"], out_m["w_in"], out_v["w_in"] = g_t.T[None], d_t.T[None], m_t.T[None], v_t.T[None]
    order = ["norm_mix", "w_in", "conv_w", "a_log", "dt_bias", "dn_norm", "sinks", "w_o", "norm_mlp", "w_up", "w_down",
             "norm_ple", "w_ple_gate", "w_ple_proj", "norm_final"]
    return (loss, grad_x[None], *[out_g[k] for k in order], *[out_d[k] for k in order],
            *[out_m[k] for k in order], *[out_v[k] for k in order])
```

```python
import functools

import jax
import jax.numpy as jnp
from jax import lax
from jax.experimental import pallas as pl
from jax.experimental.pallas import tpu as pltpu

F32 = jnp.float32
BF16 = jnp.bfloat16
MXU_DTYPE = jnp.bfloat16
HI = lax.Precision.HIGHEST

D_MODEL = 1024
PLE_DIM = 256
ATTN_HEADS = 8
ATTN_KV_HEADS = 2
ATTN_GROUPS = ATTN_HEADS // ATTN_KV_HEADS
ATTN_HEAD_DIM = 64
ATTN_BLOCK = 128
ROPE_THETA = 10000.0
DN_HEADS = 4
DN_HEAD_DIM = 128
DN_CONV = 4
DN_CHUNK = 64
D_FF = 4 * D_MODEL
EPS = 1e-6
ATTN_Q = ATTN_HEADS * ATTN_HEAD_DIM
ATTN_KV = ATTN_KV_HEADS * ATTN_HEAD_DIM
DN_W = DN_HEADS * DN_HEAD_DIM
CONV_CH = 3 * DN_W
D_IN = ATTN_Q + 2 * ATTN_KV + 4 * DN_W + 2 * DN_HEADS
DN_COLS = 4 * DN_W + 128
DN_SCALE = DN_HEAD_DIM ** -0.5
ATTN_SCALE = ATTN_HEAD_DIM ** -0.5
FF_BLOCKS = 4
FF_BLOCK = D_FF // FF_BLOCKS

ADAM_LR = 0.001
ADAM_B1 = 0.9
ADAM_B2 = 0.999
ADAM_EPS = 1e-08
ADAM_WD = 0.01
ADAM_STEP = 10

V7X_VMEM_BYTES = 64 * 1024 * 1024
VMEM_LIMIT = 48 * 1024 * 1024

NN = ((1,), (0,))
NT = ((1,), (1,))
TN = ((0,), (0,))


def _dot(a, b, dims=NN, prec=None):
    if a.dtype != b.dtype:
        a, b = a.astype(MXU_DTYPE), b.astype(MXU_DTYPE)
    return lax.dot_general(a, b, (dims, ((), ())), precision=prec, preferred_element_type=F32)


def _sigmoid(x):
    return 1.0 / (1.0 + jnp.exp(-x))


def _softplus(x):
    return jnp.maximum(x, 0.0) + jnp.log(1.0 + jnp.exp(-jnp.abs(x)))


def _params(*sem):
    return pltpu.CompilerParams(dimension_semantics=sem, vmem_limit_bytes=VMEM_LIMIT)


def _rms_fwd(xv, g):
    r = lax.rsqrt(jnp.mean(xv * xv, axis=-1, keepdims=True) + EPS)
    return xv * r * g


def _rms_bwd(xv, g, dn):
    r = lax.rsqrt(jnp.mean(xv * xv, axis=-1, keepdims=True) + EPS)
    xh = xv * r
    dg = jnp.sum(dn * xh, axis=0, keepdims=True)
    dxh = dn * g
    dx = r * (dxh - xh * jnp.mean(dxh * xh, axis=-1, keepdims=True))
    return dx, dg


def _full(shape):
    return pl.BlockSpec(shape, lambda *_: (0,) * len(shape))


def _inproj(x, g_mix, wa_t, wd_t, tm):
    t = x.shape[0]

    def body(x_ref, g_ref, wa_ref, wd_ref, u_ref, pa_ref, pd_ref):
        u = _rms_fwd(x_ref[...], g_ref[...]).astype(MXU_DTYPE)
        u_ref[...] = u
        pa_ref[...] = _dot(u, wa_ref[...], NT)
        pd_ref[...] = _dot(u, wd_ref[...], NT)

    na, nd = wa_t.shape[0], wd_t.shape[0]
    return pl.pallas_call(
        body, name="inproj", grid=(t // tm,),
        in_specs=[pl.BlockSpec((tm, D_MODEL), lambda i: (i, 0)), _full((1, D_MODEL)),
                  _full((na, D_MODEL)), _full((nd, D_MODEL))],
        out_specs=[pl.BlockSpec((tm, D_MODEL), lambda i: (i, 0)), pl.BlockSpec((tm, na), lambda i: (i, 0)),
                   pl.BlockSpec((tm, nd), lambda i: (i, 0))],
        out_shape=[jax.ShapeDtypeStruct((t, D_MODEL), MXU_DTYPE), jax.ShapeDtypeStruct((t, na), F32),
                   jax.ShapeDtypeStruct((t, nd), F32)],
        compiler_params=_params("parallel"),
    )(x, g_mix, wa_t, wd_t)


def _oproj(x, ao, dn, wo_a, wo_d, tm):
    t = x.shape[0]

    def body(x_ref, ao_ref, dn_ref, wa_ref, wd_ref, h_ref):
        h_ref[...] = (x_ref[...] + _dot(ao_ref[...].astype(MXU_DTYPE), wa_ref[...])
                      + _dot(dn_ref[...].astype(MXU_DTYPE), wd_ref[...]))

    half = ao.shape[1]
    return pl.pallas_call(
        body, name="oproj", grid=(t // tm,),
        in_specs=[pl.BlockSpec((tm, D_MODEL), lambda i: (i, 0)), pl.BlockSpec((tm, half), lambda i: (i, 0)),
                  pl.BlockSpec((tm, half), lambda i: (i, 0)), _full((half, D_MODEL)), _full((half, D_MODEL))],
        out_specs=pl.BlockSpec((tm, D_MODEL), lambda i: (i, 0)),
        out_shape=jax.ShapeDtypeStruct((t, D_MODEL), F32),
        compiler_params=_params("parallel"),
    )(x, ao, dn, wo_a, wo_d)


def _mlp_fwd(h1, g_mlp, w_up4, w_down, tm):
    t = h1.shape[0]

    def body(h_ref, g_ref, wu_ref, wd_ref, m_ref, r_ref, h2_ref, acc_ref):
        k = pl.program_id(1)

        @pl.when(k == 0)
        def _():
            m_ref[...] = _rms_fwd(h_ref[...], g_ref[...]).astype(MXU_DTYPE)
            acc_ref[...] = jnp.zeros_like(acc_ref)

        r = jnp.maximum(_dot(m_ref[...], wu_ref[...]), 0.0)
        r_ref[...] = r.astype(MXU_DTYPE)
        s = jnp.square(r).astype(MXU_DTYPE)
        acc_ref[...] += _dot(s, wd_ref[...])

        @pl.when(k == FF_BLOCKS - 1)
        def _():
            h2_ref[...] = h_ref[...] + acc_ref[...]

    return pl.pallas_call(
        body, name="mlp_fwd", grid=(t // tm, FF_BLOCKS),
        in_specs=[pl.BlockSpec((tm, D_MODEL), lambda i, k: (i, 0)), _full((1, D_MODEL)),
                  pl.BlockSpec((None, D_MODEL, FF_BLOCK), lambda i, k: (k, 0, 0)),
                  pl.BlockSpec((FF_BLOCK, D_MODEL), lambda i, k: (k, 0))],
        out_specs=[pl.BlockSpec((tm, D_MODEL), lambda i, k: (i, 0)), pl.BlockSpec((tm, FF_BLOCK), lambda i, k: (i, k)),
                   pl.BlockSpec((tm, D_MODEL), lambda i, k: (i, 0))],
        out_shape=[jax.ShapeDtypeStruct((t, D_MODEL), MXU_DTYPE), jax.ShapeDtypeStruct((t, D_FF), MXU_DTYPE),
                   jax.ShapeDtypeStruct((t, D_MODEL), F32)],
        scratch_shapes=[pltpu.VMEM((tm, D_MODEL), F32)],
        compiler_params=_params("parallel", "arbitrary"),
    )(h1, g_mlp, w_up4, w_down)


def _ple_loss(h2, p, tgt, g_ple, g_fin, w_gate, w_proj, tm):
    t = h2.shape[0]

    def body(h_ref, p_ref, t_ref, gp_ref, gf_ref, wg_ref, wp_ref,
             dh_ref, dhb_ref, dgp_ref, dpp_ref, n3_ref, pb_ref, acc_ref):
        @pl.when(pl.program_id(0) == 0)
        def _():
            acc_ref[...] = jnp.zeros_like(acc_ref)

        h = h_ref[...]
        g_ple_v, g_fin_v = gp_ref[...], gf_ref[...]
        n3 = _rms_fwd(h, g_ple_v).astype(MXU_DTYPE)
        n3_ref[...] = n3
        gate = _sigmoid(_dot(n3, wg_ref[...]))
        pb = p_ref[...].astype(MXU_DTYPE)
        pb_ref[...] = pb
        pp = _dot(pb, wp_ref[...])
        h3 = h + gate * pp
        r4 = lax.rsqrt(jnp.mean(h3 * h3, axis=-1, keepdims=True) + EPS)
        xh4 = h3 * r4
        e = xh4 * g_fin_v - t_ref[...]
        loss = 0.5 * jnp.sum(jnp.mean(e * e, axis=-1, keepdims=True), axis=0, keepdims=True)
        dy = e * (1.0 / D_MODEL)
        dg_fin = jnp.sum(dy * xh4, axis=0, keepdims=True)
        dxh = dy * g_fin_v
        dh3 = r4 * (dxh - xh4 * jnp.mean(dxh * xh4, axis=-1, keepdims=True))
        dpp_ref[...] = (dh3 * gate).astype(MXU_DTYPE)
        dgp = (dh3 * pp * gate * (1.0 - gate)).astype(MXU_DTYPE)
        dgp_ref[...] = dgp
        dn3 = _dot(dgp, wg_ref[...], NT)
        dx, dg_ple = _rms_bwd(h, g_ple_v, dn3)
        dh2 = dh3 + dx
        dh_ref[...] = dh2
        dhb_ref[...] = dh2.astype(MXU_DTYPE)
        acc_ref[0:1, :] += dg_fin
        acc_ref[1:2, :] += dg_ple
        acc_ref[2:3, :] += jnp.broadcast_to(loss, (1, D_MODEL))

    row = lambda w: pl.BlockSpec((tm, w), lambda i: (i, 0))
    return pl.pallas_call(
        body, name="ple_loss", grid=(t // tm,),
        in_specs=[row(D_MODEL), row(PLE_DIM), row(D_MODEL), _full((1, D_MODEL)), _full((1, D_MODEL)),
                  _full((D_MODEL, D_MODEL)), _full((PLE_DIM, D_MODEL))],
        out_specs=[row(D_MODEL), row(D_MODEL), row(D_MODEL), row(D_MODEL), row(D_MODEL), row(PLE_DIM),
                   _full((8, D_MODEL))],
        out_shape=[jax.ShapeDtypeStruct((t, D_MODEL), F32), jax.ShapeDtypeStruct((t, D_MODEL), MXU_DTYPE),
                   jax.ShapeDtypeStruct((t, D_MODEL), MXU_DTYPE), jax.ShapeDtypeStruct((t, D_MODEL), MXU_DTYPE),
                   jax.ShapeDtypeStruct((t, D_MODEL), MXU_DTYPE), jax.ShapeDtypeStruct((t, PLE_DIM), MXU_DTYPE),
                   jax.ShapeDtypeStruct((8, D_MODEL), F32)],
        compiler_params=_params("arbitrary"),
    )(h2, p, tgt, g_ple, g_fin, w_gate, w_proj)


def _mlp_bwd(dh2, dh2b, r, h1, g_mlp, w_up4, w_down, wo_a, wo_d, tm):
    t = h1.shape[0]
    half = wo_a.shape[0]

    def body(dh_ref, dhb_ref, r_ref, h_ref, g_ref, wu_ref, wd_ref, woa_ref, wod_ref,
             da_ref, dh1_ref, dh1b_ref, dao_ref, ddn_ref, acc_ref, dm_ref):
        i, k = pl.program_id(0), pl.program_id(1)

        @pl.when((i == 0) & (k == 0))
        def _():
            acc_ref[...] = jnp.zeros_like(acc_ref)

        @pl.when(k == 0)
        def _():
            dm_ref[...] = jnp.zeros_like(dm_ref)

        ds = _dot(dhb_ref[...], wd_ref[...], NT)
        da = (ds * (2.0 * r_ref[...].astype(F32))).astype(MXU_DTYPE)
        da_ref[...] = da
        dm_ref[...] += _dot(da, wu_ref[...], NT)

        @pl.when(k == FF_BLOCKS - 1)
        def _():
            dx, dg = _rms_bwd(h_ref[...], g_ref[...], dm_ref[...])
            dh1 = dh_ref[...] + dx
            dh1_ref[...] = dh1
            dh1b = dh1.astype(MXU_DTYPE)
            dh1b_ref[...] = dh1b
            dao_ref[...] = _dot(dh1b, woa_ref[...], NT)
            ddn_ref[...] = _dot(dh1b, wod_ref[...], NT)
            acc_ref[0:1, :] += dg

    tok = lambda w: pl.BlockSpec((tm, w), lambda i, k: (i, 0))
    return pl.pallas_call(
        body, name="mlp_bwd", grid=(t // tm, FF_BLOCKS),
        in_specs=[tok(D_MODEL), tok(D_MODEL), pl.BlockSpec((tm, FF_BLOCK), lambda i, k: (i, k)), tok(D_MODEL),
                  _full((1, D_MODEL)), pl.BlockSpec((None, D_MODEL, FF_BLOCK), lambda i, k: (k, 0, 0)),
                  pl.BlockSpec((FF_BLOCK, D_MODEL), lambda i, k: (k, 0)),
                  pl.BlockSpec((half, D_MODEL), lambda i, k: (0, 0)), pl.BlockSpec((half, D_MODEL), lambda i, k: (0, 0))],
        out_specs=[pl.BlockSpec((tm, FF_BLOCK), lambda i, k: (i, k)),
                   tok(D_MODEL), tok(D_MODEL), tok(half), tok(half), pl.BlockSpec((8, D_MODEL), lambda i, k: (0, 0))],
        out_shape=[jax.ShapeDtypeStruct((t, D_FF), MXU_DTYPE),
                   jax.ShapeDtypeStruct((t, D_MODEL), F32), jax.ShapeDtypeStruct((t, D_MODEL), MXU_DTYPE),
                   jax.ShapeDtypeStruct((t, half), F32), jax.ShapeDtypeStruct((t, half), F32),
                   jax.ShapeDtypeStruct((8, D_MODEL), F32)],
        scratch_shapes=[pltpu.VMEM((tm, D_MODEL), F32)],
        compiler_params=_params("arbitrary", "arbitrary"),
    )(dh2, dh2b, r, h1, g_mlp, w_up4, w_down, wo_a, wo_d)


def _inproj_bwd(x, dh1, g_mix, grads, weights, tm):
    t = x.shape[0]
    n = len(grads)

    def body(*refs):
        x_ref, dh_ref, g_ref = refs[:3]
        g_refs, w_refs = refs[3:3 + n], refs[3 + n:3 + 2 * n]
        dx_ref, acc_ref = refs[3 + 2 * n:]

        @pl.when(pl.program_id(0) == 0)
        def _():
            acc_ref[...] = jnp.zeros_like(acc_ref)

        du = _dot(g_refs[0][...], w_refs[0][...])
        for j in range(1, n):
            du += _dot(g_refs[j][...], w_refs[j][...])
        dx, dg = _rms_bwd(x_ref[...], g_ref[...], du)
        dx_ref[...] = dh_ref[...] + dx
        acc_ref[0:1, :] += dg

    tok = lambda w: pl.BlockSpec((tm, w), lambda i: (i, 0))
    return pl.pallas_call(
        body, name="inproj_bwd", grid=(t // tm,),
        in_specs=[tok(D_MODEL), tok(D_MODEL), _full((1, D_MODEL))] + [tok(g.shape[1]) for g in grads]
                 + [_full(w.shape) for w in weights],
        out_specs=[tok(D_MODEL), _full((8, D_MODEL))],
        out_shape=[jax.ShapeDtypeStruct((t, D_MODEL), F32), jax.ShapeDtypeStruct((8, D_MODEL), F32)],
        compiler_params=_params("arbitrary"),
    )(x, dh1, g_mix, *grads, *weights)


def _wgrad(a, b, name, tk, tn, tt, stacked=False, prep=None):
    t, kdim = a.shape
    ncols = b.shape[1]

    def body(a_ref, b_ref, o_ref):
        @pl.when(pl.program_id(2) == 0)
        def _():
            o_ref[...] = jnp.zeros_like(o_ref)

        av = a_ref[...] if prep is None else prep(a_ref[...])
        o_ref[...] += _dot(av, b_ref[...], TN)

    if stacked:
        out_spec = pl.BlockSpec((None, tk, tn), lambda i, j, s: (j, i, 0))
        out_shape = jax.ShapeDtypeStruct((ncols // tn, kdim, tn), F32)
    else:
        out_spec = pl.BlockSpec((tk, tn), lambda i, j, s: (i, j))
        out_shape = jax.ShapeDtypeStruct((kdim, ncols), F32)
    return pl.pallas_call(
        body, name=name, grid=(kdim // tk, ncols // tn, t // tt),
        in_specs=[pl.BlockSpec((tt, tk), lambda i, j, s: (s, i)), pl.BlockSpec((tt, tn), lambda i, j, s: (s, j))],
        out_specs=out_spec, out_shape=out_shape,
        compiler_params=_params("parallel", "parallel", "arbitrary"),
    )(a, b)


def _wgrad_cat(as_, bs, name, tt):
    t = as_[0].shape[0]
    heights = [a.shape[1] for a in as_]
    widths = [b.shape[1] for b in bs]

    def body(*refs):
        a_refs, b_refs, o_ref = refs[:len(as_)], refs[len(as_):-1], refs[-1]

        @pl.when(pl.program_id(0) == 0)
        def _():
            o_ref[...] = jnp.zeros_like(o_ref)

        row = 0
        for a_ref, k in zip(a_refs, heights):
            av = a_ref[...]
            col = 0
            for b_ref, n in zip(b_refs, widths):
                o_ref[row:row + k, col:col + n] += _dot(av, b_ref[...], TN)
                col += n
            row += k

    tok = lambda w: pl.BlockSpec((tt, w), lambda s: (s, 0))
    shape = (sum(heights), sum(widths))
    return pl.pallas_call(
        body, name=name, grid=(t // tt,),
        in_specs=[tok(k) for k in heights] + [tok(n) for n in widths],
        out_specs=_full(shape), out_shape=jax.ShapeDtypeStruct(shape, F32),
        compiler_params=_params("arbitrary"),
    )(*as_, *bs)


def _rope_tables(t):
    half = ATTN_HEAD_DIM // 2
    inv = 1.0 / (ROPE_THETA ** (jnp.arange(half, dtype=F32) * (2.0 / ATTN_HEAD_DIM)))
    ang = jnp.arange(t, dtype=F32)[:, None] * inv[None, :]
    cos, sin = jnp.cos(ang), jnp.sin(ang)
    cos2 = jnp.concatenate([cos, cos], axis=-1)
    sin2 = jnp.concatenate([-sin, sin], axis=-1)
    return jnp.tile(cos2, (1, 2)), jnp.tile(sin2, (1, 2))


def _swap_halves(tv):
    w = tv.shape[-1]
    lane = lax.broadcasted_iota(jnp.int32, tv.shape, tv.ndim - 1)
    first = (lane % ATTN_HEAD_DIM) < (ATTN_HEAD_DIM // 2)
    return jnp.where(first, pltpu.roll(tv, w - ATTN_HEAD_DIM // 2, tv.ndim - 1),
                     pltpu.roll(tv, ATTN_HEAD_DIM // 2, tv.ndim - 1))


def _rope(tv, cos, sin):
    return tv * cos + _swap_halves(tv) * sin


def _rope_bwd(dv, cos, sin):
    return dv * cos + _swap_halves(dv * sin)


def _attn_valid(first_block):
    c = lax.broadcasted_iota(jnp.int32, (2 * ATTN_BLOCK, ATTN_BLOCK), 0)
    r = lax.broadcasted_iota(jnp.int32, (2 * ATTN_BLOCK, ATTN_BLOCK), 1)
    return (c > r) & (c <= r + ATTN_BLOCK) & ((c >= ATTN_BLOCK) | jnp.logical_not(first_block))


def _attn_probs(st, sink, valid):
    s = jnp.where(valid, st * ATTN_SCALE, -jnp.inf)
    m = jnp.maximum(jnp.max(s, axis=0, keepdims=True), sink)
    e = jnp.where(valid, jnp.exp(s - m), 0.0)
    es = jnp.exp(sink - m)
    inv = 1.0 / (jnp.sum(e, axis=0, keepdims=True) + es)
    return e * inv, es * inv


def _lane_scalar(vec, idx):
    lane = lax.broadcasted_iota(jnp.int32, vec.shape, 1)
    return jnp.sum(jnp.where(lane == idx, vec, 0.0), axis=-1, keepdims=True)


def _attn_specs(nb):
    cur = lambda w, cb: pl.BlockSpec((ATTN_BLOCK, w), lambda i: (jnp.minimum(i, nb - 1), cb))
    prev = lambda w, cb: pl.BlockSpec((ATTN_BLOCK, w), lambda i: (jnp.maximum(jnp.minimum(i, nb - 1) - 1, 0), cb))
    kcol, vcol = ATTN_Q // ATTN_KV, ATTN_Q // ATTN_KV + 1
    return [cur(ATTN_Q, 0), cur(ATTN_KV, kcol), prev(ATTN_KV, kcol), cur(ATTN_KV, vcol), prev(ATTN_KV, vcol),
            cur(ATTN_KV, 0), cur(ATTN_KV, 0), prev(ATTN_KV, 0), prev(ATTN_KV, 0), _full((1, 128))]


def _attn_fwd(pa, cos, sin, sinks_vec):
    t = pa.shape[0]
    nb = t // ATTN_BLOCK

    def body(q_ref, kc_ref, kp_ref, vc_ref, vp_ref, cc_ref, sc_ref, cp_ref, sp_ref, sk_ref, o_ref):
        first = pl.program_id(0) == 0
        cc, sc = cc_ref[...], sc_ref[...]
        q = _rope(q_ref[...], jnp.tile(cc, (1, ATTN_Q // ATTN_KV)), jnp.tile(sc, (1, ATTN_Q // ATTN_KV)))
        kc = _rope(kc_ref[...], cc, sc)
        kp = _rope(kp_ref[...], cp_ref[...], sp_ref[...])
        vc, vp = vc_ref[...], vp_ref[...]
        sk = sk_ref[...]
        valid = _attn_valid(first)
        kv = lambda tp, tc, hk: jnp.concatenate([tp[:, hk * ATTN_HEAD_DIM:(hk + 1) * ATTN_HEAD_DIM],
                                                 tc[:, hk * ATTN_HEAD_DIM:(hk + 1) * ATTN_HEAD_DIM]], axis=0)
        kwins = [kv(kp, kc, hk) for hk in range(ATTN_KV_HEADS)]
        vwins_t = [kv(vp, vc, hk).T for hk in range(ATTN_KV_HEADS)]
        heads = [slice(h * ATTN_HEAD_DIM, (h + 1) * ATTN_HEAD_DIM) for h in range(ATTN_HEADS)]
        scores = [_dot(kwins[h // ATTN_GROUPS], q[:, hs], NT) for h, hs in enumerate(heads)]
        probs = [_attn_probs(st, _lane_scalar(sk, h), valid)[0] for h, st in enumerate(scores)]
        for h, (hs, pt) in enumerate(zip(heads, probs)):
            o_ref[:, hs] = _dot(vwins_t[h // ATTN_GROUPS], pt).T.astype(o_ref.dtype)

    return pl.pallas_call(
        body, name="attn_fwd", grid=(nb,),
        in_specs=_attn_specs(nb),
        out_specs=pl.BlockSpec((ATTN_BLOCK, ATTN_Q), lambda i: (i, 0)),
        out_shape=jax.ShapeDtypeStruct((t, ATTN_Q), MXU_DTYPE),
        compiler_params=_params("parallel"),
    )(pa, pa, pa, pa, pa, cos, sin, cos, sin, sinks_vec)


def _attn_bwd(pa, cos, sin, sinks_vec, dao):
    t = pa.shape[0]
    nb = t // ATTN_BLOCK

    def body(q_ref, kc_ref, kp_ref, vc_ref, vp_ref, cc_ref, sc_ref, cp_ref, sp_ref, sk_ref, do_ref,
             dq_ref, dk_ref, dv_ref, acc_ref, dqr_ref, dkw_ref, dvw_ref, ck_ref, cv_ref):
        i = pl.program_id(0)

        @pl.when(i == 0)
        def _():
            acc_ref[...] = jnp.zeros_like(acc_ref)
            ck_ref[...] = jnp.zeros_like(ck_ref)
            cv_ref[...] = jnp.zeros_like(cv_ref)

        @pl.when(i < nb)
        def _():
            first = i == 0
            cc, sc = cc_ref[...], sc_ref[...]
            cq, sq = jnp.tile(cc, (1, ATTN_Q // ATTN_KV)), jnp.tile(sc, (1, ATTN_Q // ATTN_KV))
            q = _rope(q_ref[...], cq, sq)
            kc = _rope(kc_ref[...], cc, sc)
            kp = _rope(kp_ref[...], cp_ref[...], sp_ref[...])
            vc, vp = vc_ref[...], vp_ref[...]
            sk = sk_ref[...]
            do = do_ref[...]
            lane = lax.broadcasted_iota(jnp.int32, (1, 128), 1)
            dsink = jnp.zeros((1, 128), F32)
            valid = _attn_valid(first)
            kv = lambda tp, tc, hk: jnp.concatenate([tp[:, hk * ATTN_HEAD_DIM:(hk + 1) * ATTN_HEAD_DIM],
                                                     tc[:, hk * ATTN_HEAD_DIM:(hk + 1) * ATTN_HEAD_DIM]], axis=0)
            kwins = [kv(kp, kc, hk) for hk in range(ATTN_KV_HEADS)]
            vwins = [kv(vp, vc, hk) for hk in range(ATTN_KV_HEADS)]
            kwins_t = [kw.T for kw in kwins]
            heads = [slice(h * ATTN_HEAD_DIM, (h + 1) * ATTN_HEAD_DIM) for h in range(ATTN_HEADS)]
            scores = [_dot(kwins[h // ATTN_GROUPS], q[:, hs], NT) for h, hs in enumerate(heads)]
            dps = [_dot(vwins[h // ATTN_GROUPS], do[:, hs], NT) for h, hs in enumerate(heads)]
            pts, dsts = [], []
            for h, (st, dp_t) in enumerate(zip(scores, dps)):
                probs_t, psink = _attn_probs(st, _lane_scalar(sk, h), valid)
                delta = jnp.sum(probs_t * dp_t, axis=0, keepdims=True)
                pts.append(probs_t)
                dsts.append(probs_t * (dp_t - delta) * ATTN_SCALE)
                dsink += jnp.where(lane == h, jnp.sum(-psink * delta, axis=1, keepdims=True), 0.0)
            for h, (hs, ds_t) in enumerate(zip(heads, dsts)):
                dqr_ref[:, hs] = _dot(kwins_t[h // ATTN_GROUPS], ds_t).T
            for hk in range(ATTN_KV_HEADS):
                ks = slice(hk * ATTN_HEAD_DIM, (hk + 1) * ATTN_HEAD_DIM)
                group = range(hk * ATTN_GROUPS, (hk + 1) * ATTN_GROUPS)
                ds_g = jnp.concatenate([dsts[h] for h in group], axis=1)
                p_g = jnp.concatenate([pts[h] for h in group], axis=1)
                q_g = jnp.concatenate([q[:, heads[h]] for h in group], axis=0)
                do_g = jnp.concatenate([do[:, heads[h]] for h in group], axis=0)
                dkw_ref[:, ks] = _dot(ds_g, q_g)
                dvw_ref[:, ks] = _dot(p_g, do_g)
            acc_ref[0:1, :] += dsink
            dq_ref[...] = _rope_bwd(dqr_ref[...], cq, sq).astype(dq_ref.dtype)
            dk_ref[...] = (ck_ref[...] + _rope_bwd(dkw_ref[0:ATTN_BLOCK, :], cp_ref[...], sp_ref[...])).astype(dk_ref.dtype)
            dv_ref[...] = (cv_ref[...] + dvw_ref[0:ATTN_BLOCK, :]).astype(dv_ref.dtype)
            ck_ref[...] = _rope_bwd(dkw_ref[ATTN_BLOCK:2 * ATTN_BLOCK, :], cc, sc)
            cv_ref[...] = dvw_ref[ATTN_BLOCK:2 * ATTN_BLOCK, :]

        @pl.when(i == nb)
        def _():
            dk_ref[...] = ck_ref[...].astype(dk_ref.dtype)
            dv_ref[...] = cv_ref[...].astype(dv_ref.dtype)

    prev_out = lambda w: pl.BlockSpec((ATTN_BLOCK, w), lambda i: (jnp.maximum(i - 1, 0), 0))
    return pl.pallas_call(
        body, name="attn_bwd", grid=(nb + 1,),
        in_specs=_attn_specs(nb) + [pl.BlockSpec((ATTN_BLOCK, ATTN_Q), lambda i: (jnp.minimum(i, nb - 1), 0))],
        out_specs=[pl.BlockSpec((ATTN_BLOCK, ATTN_Q), lambda i: (jnp.minimum(i, nb - 1), 0)), prev_out(ATTN_KV),
                   prev_out(ATTN_KV), _full((8, 128))],
        out_shape=[jax.ShapeDtypeStruct((t, ATTN_Q), MXU_DTYPE), jax.ShapeDtypeStruct((t, ATTN_KV), MXU_DTYPE),
                   jax.ShapeDtypeStruct((t, ATTN_KV), MXU_DTYPE), jax.ShapeDtypeStruct((8, 128), F32)],
        scratch_shapes=[pltpu.VMEM((ATTN_BLOCK, ATTN_Q), F32), pltpu.VMEM((2 * ATTN_BLOCK, ATTN_KV), F32),
                        pltpu.VMEM((2 * ATTN_BLOCK, ATTN_KV), F32), pltpu.VMEM((ATTN_BLOCK, ATTN_KV), F32),
                        pltpu.VMEM((ATTN_BLOCK, ATTN_KV), F32)],
        compiler_params=_params("arbitrary"),
    )(pa, pa, pa, pa, pa, cos, sin, cos, sin, sinks_vec, dao)


PAIR = 2 * DN_CHUNK
INTRA_PAIRS = 4
SCAN_PAIRS = 4
HALO = 8


def _conv_window(cur_ref, prev_ref, xs_ref, tm, has_prev):
    prev = jnp.where(has_prev, prev_ref[...], 0.0)
    xs_ref[0:HALO, :] = prev
    xs_ref[HALO:HALO + tm, :] = cur_ref[...]


def _conv_taps(xs_ref, cw_ref, tm):
    y = cw_ref[0:1, :] * xs_ref[pl.ds(HALO - DN_CONV + 1, tm), :]
    for j in range(1, DN_CONV):
        y += cw_ref[j:j + 1, :] * xs_ref[pl.ds(HALO - DN_CONV + 1 + j, tm), :]
    return y


def _gate_values(ba, al, dt):
    beta = _sigmoid(ba)
    pre = ba + dt
    g = -jnp.exp(al) * _softplus(pre)
    return beta, g, pre


def _dn_prep_specs(tm, tile):
    return [pl.BlockSpec((tm, CONV_CH), lambda i: (tile(i), 0)),
            pl.BlockSpec((HALO, CONV_CH), lambda i: (jnp.maximum(tile(i) * (tm // HALO) - 1, 0), 0)),
            pl.BlockSpec((tm, 128), lambda i: (tile(i), 4 * DN_W // 128)),
            _full((DN_CONV, CONV_CH)), _full((1, 128)), _full((1, 128))]


def _dn_prep(pd, conv_w, al_vec, dt_vec, tm):
    t = pd.shape[0]

    def body(cur_ref, prev_ref, ba_ref, cw_ref, al_ref, dt_ref, qn_ref, kn_ref, vc_ref, gc_ref, gr_ref, xs_ref):
        _conv_window(cur_ref, prev_ref, xs_ref, tm, pl.program_id(0) > 0)
        y = _conv_taps(xs_ref, cw_ref, tm)
        c = y * _sigmoid(y)
        for h in range(DN_HEADS):
            qs = slice(h * DN_HEAD_DIM, (h + 1) * DN_HEAD_DIM)
            ksl = slice(DN_W + h * DN_HEAD_DIM, DN_W + (h + 1) * DN_HEAD_DIM)
            qh, kh = c[:, qs], c[:, ksl]
            qn_ref[:, qs] = qh * lax.rsqrt(jnp.sum(qh * qh, axis=-1, keepdims=True) + EPS) * DN_SCALE
            kn_ref[:, qs] = kh * lax.rsqrt(jnp.sum(kh * kh, axis=-1, keepdims=True) + EPS)
        vc_ref[...] = c[:, 2 * DN_W:3 * DN_W]
        beta, g, _ = _gate_values(ba_ref[...], al_ref[...], dt_ref[...])
        lane = lax.broadcasted_iota(jnp.int32, beta.shape, 1)
        gb = jnp.where(lane < DN_HEADS, beta, jnp.where(lane < 2 * DN_HEADS, g, 0.0))
        gc_ref[...] = gb
        gr_ref[...] = gb.T[0:8, :]

    tok = lambda w: pl.BlockSpec((tm, w), lambda i: (i, 0))
    return pl.pallas_call(
        body, name="dn_prep", grid=(t // tm,),
        in_specs=_dn_prep_specs(tm, lambda i: i),
        out_specs=[tok(DN_W), tok(DN_W), tok(DN_W), tok(128), pl.BlockSpec((8, tm), lambda i: (0, i))],
        out_shape=[jax.ShapeDtypeStruct((t, DN_W), F32)] * 3 + [jax.ShapeDtypeStruct((t, 128), F32),
                                                                 jax.ShapeDtypeStruct((8, t), F32)],
        scratch_shapes=[pltpu.VMEM((HALO + tm, CONV_CH), F32)],
        compiler_params=_params("parallel"),
    )(pd, pd, pd, conv_w, al_vec, dt_vec)


def _pair_masks():
    r = lax.broadcasted_iota(jnp.int32, (PAIR, PAIR), 0)
    c = lax.broadcasted_iota(jnp.int32, (PAIR, PAIR), 1)
    same = (r < DN_CHUNK) == (c < DN_CHUNK)
    return same & (r >= c), same & (r > c)


def _lane_col(mat, idx):
    lane = lax.broadcasted_iota(jnp.int32, mat.shape, 1)
    return jnp.sum(jnp.where(lane == idx, mat, 0.0), axis=-1, keepdims=True)


def _pair_cumsums(gc, gr, low):
    lowf = low.astype(F32)
    return _dot(lowf, gc, NN, HI), _dot(gr, lowf, NT, HI)


def _pair_gates(gc, cum_c, cum_r, low, h):
    beta = _lane_col(gc, h)
    gam = _lane_col(cum_c, DN_HEADS + h)
    gam_row = cum_r[DN_HEADS + h:DN_HEADS + h + 1, :]
    dm = jnp.where(low, jnp.exp(jnp.where(low, gam - gam_row, 0.0)), 0.0)
    row = lax.broadcasted_iota(jnp.int32, gam.shape, 0)
    gl = jnp.where(row < DN_CHUNK, gam[DN_CHUNK - 1:DN_CHUNK, :], gam[PAIR - 1:PAIR, :])
    return beta, gam, dm, gl


def _split(a):
    hi = a.astype(BF16)
    return hi, (a - hi.astype(F32)).astype(BF16)


def _dot_split(a, b, dims=NN):
    (ah, al), (bh, bl) = a, b
    la, lb = (1, 1) if dims == TN else ((0, 1) if dims == NN else (0, 0))
    r = _dot(jnp.concatenate([ah, al], axis=la), jnp.concatenate([bh, bl], axis=lb), dims)
    m, n = r.shape[0] // 2, r.shape[1] // 2
    return (r[m:, n:] + (r[:m, n:] + r[m:, :n])) + r[:m, :n]


def _unit_lower_inverses(lmats):
    n = lmats[0].shape[0]
    r = lax.broadcasted_iota(jnp.int32, (n, n), 0)
    c = lax.broadcasted_iota(jnp.int32, (n, n), 1)
    same = lambda size: (r & ~(size - 1)) == (c & ~(size - 1))
    base = DN_CHUNK // 4
    diag = [jnp.where(same(base), l, 0.0) for l in lmats]
    accs = [(r == c).astype(F32) - d for d in diag]
    splits = [_split(d) for d in diag]
    step = 1
    while 2 * step < base:
        splits = [_split(_dot_split(s, s)) for s in splits]
        accs = [acc + _dot_split(_split(acc), s) for acc, s in zip(accs, splits)]
        step *= 2
    size = base
    while size < DN_CHUNK:
        below = same(2 * size) & jnp.logical_not(same(size))
        tb = [_dot(acc, jnp.where(below, l, 0.0)) for acc, l in zip(accs, lmats)]
        accs = [acc - _dot(t, acc) for acc, t in zip(accs, tb)]
        size *= 2
    return accs


def _dn_intra(qn, kn, vc, gc, gr):
    t = qn.shape[0]
    npair = t // PAIR
    rows_step = INTRA_PAIRS * PAIR

    def body(q_ref, k_ref, v_ref, gc_ref, gr_ref, u_ref, w_ref, qg_ref, kd_ref, a_ref, ti_ref, dl_ref):
        low, strict = _pair_masks()
        items = []
        for p in range(INTRA_PAIRS):
            rows = slice(p * PAIR, (p + 1) * PAIR)
            gc_v = gc_ref[rows, :]
            cum_c, cum_r = _pair_cumsums(gc_v, gr_ref[:, rows], low)
            for h in range(DN_HEADS):
                hs = slice(h * DN_HEAD_DIM, (h + 1) * DN_HEAD_DIM)
                items.append((p, h, rows, hs, _pair_gates(gc_v, cum_c, cum_r, low, h)))
        lmats = []
        for p, h, rows, hs, (beta, gam, dm, gl) in items:
            k = k_ref[rows, hs]
            lmats.append(jnp.where(strict, _dot(k * beta, k, NT) * dm, 0.0))
        tinvs = _unit_lower_inverses(lmats)
        for (p, h, rows, hs, (beta, gam, dm, gl)), tinv in zip(items, tinvs):
            q, k, v = q_ref[rows, hs], k_ref[rows, hs], v_ref[rows, hs]
            eg = jnp.exp(gam)
            u_ref[rows, hs] = _dot(tinv, v * beta)
            w_ref[rows, hs] = _dot(tinv, (k * beta) * eg).astype(w_ref.dtype)
            a_ref[h, rows, :] = _dot(q, k, NT) * dm
            ti_ref[h, rows, :] = tinv
            qg_ref[rows, hs] = (q * eg).astype(qg_ref.dtype)
            kd_ref[rows, hs] = (k * jnp.exp(gl - gam)).astype(kd_ref.dtype)
            for c in range(2):
                last = (c + 1) * DN_CHUNK - 1
                dl_ref[2 * p + c, h] = jnp.broadcast_to(jnp.exp(gam[last:last + 1, :]), (8, 128))

    tok = lambda w: pl.BlockSpec((rows_step, w), lambda n: (n, 0))
    hm = pl.BlockSpec((DN_HEADS, rows_step, PAIR), lambda n: (0, n, 0))
    return pl.pallas_call(
        body, name="dn_intra", grid=(npair // INTRA_PAIRS,),
        in_specs=[tok(DN_W), tok(DN_W), tok(DN_W), tok(128), pl.BlockSpec((8, rows_step), lambda n: (0, n))],
        out_specs=[tok(DN_W)] * 4 + [hm, hm, pl.BlockSpec((2 * INTRA_PAIRS, DN_HEADS, 8, 128), lambda n: (n, 0, 0, 0))],
        out_shape=[jax.ShapeDtypeStruct((t, DN_W), F32)] + [jax.ShapeDtypeStruct((t, DN_W), MXU_DTYPE)] * 3
                  + [jax.ShapeDtypeStruct((DN_HEADS, t, PAIR), F32)] * 2
                  + [jax.ShapeDtypeStruct((2 * npair, DN_HEADS, 8, 128), F32)],
        compiler_params=_params("parallel"),
    )(qn, kn, vc, gc, gr)


def _dn_scan_fwd(u, w, qg, kd, a_qk, dlast, pd, dn_w):
    t = u.shape[0]
    npair = t // PAIR

    def body(u_ref, w_ref, qg_ref, kd_ref, a_ref, dl_ref, z_ref, nw_ref, out_ref, o_ref, vn_ref, sall_ref, s_ref):
        @pl.when(pl.program_id(0) == 0)
        def _():
            s_ref[...] = jnp.zeros_like(s_ref)

        nw = nw_ref[...]
        for c in range(2 * SCAN_PAIRS):
            rows = slice(c * DN_CHUNK, (c + 1) * DN_CHUNK)
            diag = slice((c % 2) * DN_CHUNK, (c % 2 + 1) * DN_CHUNK)
            for h in range(DN_HEADS):
                hs = slice(h * DN_HEAD_DIM, (h + 1) * DN_HEAD_DIM)
                st = s_ref[h]
                sall_ref[c, h] = st
                vn_ref[rows, hs] = (u_ref[rows, hs] - _dot(w_ref[rows, hs], st)).astype(vn_ref.dtype)
            for h in range(DN_HEADS):
                hs = slice(h * DN_HEAD_DIM, (h + 1) * DN_HEAD_DIM)
                st, vn = s_ref[h], vn_ref[rows, hs]
                o = _dot(qg_ref[rows, hs], st) + _dot(a_ref[h, rows, diag], vn)
                s_ref[h] = st * dl_ref[c, h][0:1, :] + _dot(kd_ref[rows, hs], vn, TN)
                o_ref[rows, hs] = o
                z = z_ref[rows, hs]
                on = o * lax.rsqrt(jnp.mean(o * o, axis=-1, keepdims=True) + EPS) * nw
                out_ref[rows, hs] = (on * (z * _sigmoid(z))).astype(out_ref.dtype)

    rows_step = SCAN_PAIRS * PAIR
    tok = pl.BlockSpec((rows_step, DN_W), lambda n: (n, 0))
    hm = pl.BlockSpec((DN_HEADS, rows_step, PAIR), lambda n: (0, n, 0))
    return pl.pallas_call(
        body, name="dn_scan_fwd", grid=(npair // SCAN_PAIRS,),
        in_specs=[tok, tok, tok, tok, hm, pl.BlockSpec((2 * SCAN_PAIRS, DN_HEADS, 8, 128), lambda n: (n, 0, 0, 0)),
                  pl.BlockSpec((rows_step, DN_W), lambda n: (n, 3)), _full((1, 128))],
        out_specs=[tok, tok, tok,
                   pl.BlockSpec((2 * SCAN_PAIRS, DN_HEADS, DN_HEAD_DIM, DN_HEAD_DIM), lambda n: (n, 0, 0, 0))],
        out_shape=[jax.ShapeDtypeStruct((t, DN_W), MXU_DTYPE), jax.ShapeDtypeStruct((t, DN_W), F32),
                   jax.ShapeDtypeStruct((t, DN_W), MXU_DTYPE),
                   jax.ShapeDtypeStruct((2 * npair, DN_HEADS, DN_HEAD_DIM, DN_HEAD_DIM), F32)],
        scratch_shapes=[pltpu.VMEM((DN_HEADS, DN_HEAD_DIM, DN_HEAD_DIM), F32)],
        compiler_params=_params("arbitrary"),
    )(u, w, qg, kd, a_qk, dlast, pd, dn_w)


def _dn_scan_bwd(dout, o, vnew, sall, w, qg, kd, a_qk, dlast, pd, dn_w, dep):
    t = o.shape[0]
    npair = t // PAIR
    nstep = npair // SCAN_PAIRS
    rev = lambda n: nstep - 1 - n

    def body(do_ref, o_ref, vn_ref, sall_ref, w_ref, qg_ref, kd_ref, a_ref, dl_ref, z_ref, nw_ref, dep_ref,
             dz_ref, du_ref, dw_ref, dqg_ref, dkd_ref, da_ref, ddl_ref, acc_ref, ds_ref, dos_ref):
        @pl.when(pl.program_id(0) == 0)
        def _():
            ds_ref[...] = jnp.zeros_like(ds_ref)
            acc_ref[...] = jnp.zeros_like(acc_ref)

        nw = nw_ref[...]
        dnw = jnp.zeros((1, 128), F32)
        for h in range(DN_HEADS):
            hs = slice(h * DN_HEAD_DIM, (h + 1) * DN_HEAD_DIM)
            o, z, dout = o_ref[:, hs], z_ref[:, hs], do_ref[:, hs]
            r = lax.rsqrt(jnp.mean(o * o, axis=-1, keepdims=True) + EPS)
            oh = o * r
            sz = _sigmoid(z)
            dz_ref[:, hs] = dout * (oh * nw) * (sz + z * sz * (1.0 - sz))
            don = dout * (z * sz)
            dnw += jnp.sum(don * oh, axis=0, keepdims=True)
            doh = don * nw
            dos_ref[:, hs] = r * (doh - oh * jnp.mean(doh * oh, axis=-1, keepdims=True))
        acc_ref[0:1, :] += dnw
        for c in reversed(range(2 * SCAN_PAIRS)):
            rows = slice(c * DN_CHUNK, (c + 1) * DN_CHUNK)
            diag = slice((c % 2) * DN_CHUNK, (c % 2 + 1) * DN_CHUNK)
            other = slice((1 - c % 2) * DN_CHUNK, (2 - c % 2) * DN_CHUNK)
            for h in range(DN_HEADS):
                hs = slice(h * DN_HEAD_DIM, (h + 1) * DN_HEAD_DIM)
                do, st, dsp, vn = dos_ref[rows, hs], sall_ref[c, h], ds_ref[h], vn_ref[rows, hs]
                da_ref[h, rows, diag] = _dot(do, vn, NT)
                da_ref[h, rows, other] = jnp.zeros((DN_CHUNK, DN_CHUNK), F32)
                du_ref[rows, hs] = (_dot(a_ref[h, rows, diag], do, TN) + _dot(kd_ref[rows, hs], dsp)).astype(du_ref.dtype)
                dqg_ref[rows, hs] = _dot(do, st, NT)
                dkd_ref[rows, hs] = _dot(vn, dsp, NT)
                ddl = jnp.sum(jnp.sum(dsp * st, axis=1, keepdims=True), axis=0, keepdims=True)
                ddl_ref[c, h] = jnp.broadcast_to(ddl, (8, 128))
            for h in range(DN_HEADS):
                hs = slice(h * DN_HEAD_DIM, (h + 1) * DN_HEAD_DIM)
                do, st, dvn = dos_ref[rows, hs], sall_ref[c, h], du_ref[rows, hs]
                dw_ref[rows, hs] = (-_dot(dvn, st, NT)).astype(dw_ref.dtype)
                ds_ref[h] = (ds_ref[h] * dl_ref[c, h][0:1, :] + _dot(qg_ref[rows, hs], do, TN)
                             - _dot(w_ref[rows, hs], dvn, TN))

    rows_step = SCAN_PAIRS * PAIR
    tok = pl.BlockSpec((rows_step, DN_W), lambda n: (rev(n), 0))
    hm = pl.BlockSpec((DN_HEADS, rows_step, PAIR), lambda n: (0, rev(n), 0))
    sc = pl.BlockSpec((2 * SCAN_PAIRS, DN_HEADS, 8, 128), lambda n: (rev(n), 0, 0, 0))
    return pl.pallas_call(
        body, name="dn_scan_bwd", grid=(nstep,),
        in_specs=[tok, tok, tok,
                  pl.BlockSpec((2 * SCAN_PAIRS, DN_HEADS, DN_HEAD_DIM, DN_HEAD_DIM), lambda n: (rev(n), 0, 0, 0)),
                  tok, tok, tok, hm, sc, pl.BlockSpec((rows_step, DN_W), lambda n: (rev(n), 3)), _full((1, 128)),
                  pl.BlockSpec(memory_space=pl.ANY)],
        out_specs=[tok] * 5 + [hm, sc, _full((8, 128))],
        out_shape=[jax.ShapeDtypeStruct((t, DN_W), F32)] + [jax.ShapeDtypeStruct((t, DN_W), MXU_DTYPE)] * 2
                  + [jax.ShapeDtypeStruct((t, DN_W), F32)] * 2 + [jax.ShapeDtypeStruct((DN_HEADS, t, PAIR), F32),
                   jax.ShapeDtypeStruct((2 * npair, DN_HEADS, 8, 128), F32), jax.ShapeDtypeStruct((8, 128), F32)],
        scratch_shapes=[pltpu.VMEM((DN_HEADS, DN_HEAD_DIM, DN_HEAD_DIM), F32), pltpu.VMEM((SCAN_PAIRS * PAIR, DN_W), F32)],
        compiler_params=_params("arbitrary"),
    )(dout, o, vnew, sall, w, qg, kd, a_qk, dlast, pd, dn_w, dep)


def _dn_intra_bwd(qn, kn, vc, gc, gr, tinv, a_qk, du, dw, dqg, dkd, da_qk, ddlast, dlast, dep):
    t = qn.shape[0]
    npair = t // PAIR

    def body(q_ref, k_ref, v_ref, gc_ref, gr_ref, ti_ref, a_ref, du_ref, dw_ref, dqg_ref, dkd_ref, da_ref, ddl_ref, dl_ref,
             dep_ref, dq_ref, dk_ref, dv_ref, dg_ref):
        low, strict = _pair_masks()
        lane = lax.broadcasted_iota(jnp.int32, (PAIR, 128), 1)
        rowi = lax.broadcasted_iota(jnp.int32, (PAIR, 1), 0)
        rsum = lambda v: jnp.sum(v, axis=-1, keepdims=True)
        items = []
        for p in range(INTRA_PAIRS):
            rows = slice(p * PAIR, (p + 1) * PAIR)
            gc_v = gc_ref[rows, :]
            cum_c, cum_r = _pair_cumsums(gc_v, gr_ref[:, rows], low)
            for h in range(DN_HEADS):
                hs = slice(h * DN_HEAD_DIM, (h + 1) * DN_HEAD_DIM)
                items.append((p, h, rows, hs, _pair_gates(gc_v, cum_c, cum_r, low, h)))
        dtis, lmats, dvbs, dkbgs = [], [], [], []
        for p, h, rows, hs, (beta, gam, dm, gl) in items:
            k, tinv = k_ref[rows, hs], ti_ref[h, rows, :]
            kb = k * beta
            dtis.append(_dot(du_ref[rows, hs], v_ref[rows, hs] * beta, NT)
                        + _dot(dw_ref[rows, hs], kb * jnp.exp(gam), NT))
            lmats.append(jnp.where(strict, _dot(kb, k, NT) * dm, 0.0))
            dvbs.append(_dot(tinv, du_ref[rows, hs], TN))
            dkbgs.append(_dot(tinv, dw_ref[rows, hs], TN))
        xs = [_dot(ti_ref[h, rows, :], dti, TN) for (p, h, rows, hs, g), dti in zip(items, dtis)]
        dls = [jnp.where(strict, -_dot(x, ti_ref[h, rows, :], NT), 0.0) for (p, h, rows, hs, g), x in zip(items, xs)]
        dgam_all = [jnp.zeros((PAIR, 128), F32) for _ in range(INTRA_PAIRS)]
        dbeta_all = [jnp.zeros((PAIR, 128), F32) for _ in range(INTRA_PAIRS)]
        for (p, h, rows, hs, (beta, gam, dm, gl)), dl, lmat, dvb, dkbg in zip(items, dls, lmats, dvbs, dkbgs):
            q, k, v = q_ref[rows, hs], k_ref[rows, hs], v_ref[rows, hs]
            a = a_ref[h, rows, :]
            dqg, dkd = dqg_ref[rows, hs], dkd_ref[rows, hs]
            kb = k * beta
            eg = jnp.exp(gam)
            ekd = jnp.exp(gl - gam)
            dmm = dl * dm
            dam = jnp.where(low, da_ref[h, rows, :], 0.0)
            dn = dam * dm
            e = dl * lmat + dam * a
            dkb = _dot(dmm, k) + dkbg * eg
            dk_ref[rows, hs] = _dot(dmm, kb, TN) + _dot(dn, q, TN) + dkd * ekd + dkb * beta
            dq_ref[rows, hs] = _dot(dn, k) + dqg * eg
            dv_ref[rows, hs] = dvb * beta
            t_kd = rsum(dkd * (k * ekd))
            dgam = rsum(e) - rsum(e.T) + rsum(dqg * (q * eg)) + rsum(dkbg * (kb * eg)) - t_kd
            for c in range(2):
                crows = slice(c * DN_CHUNK, (c + 1) * DN_CHUNK)
                dgl = (jnp.sum(t_kd[crows, :], axis=0, keepdims=True)
                       + ddl_ref[2 * p + c, h][0:1, 0:1] * dl_ref[2 * p + c, h][0:1, 0:1])
                dgam = dgam + jnp.where(rowi == (c + 1) * DN_CHUNK - 1, dgl, 0.0)
            dgam_all[p] += jnp.where(lane == DN_HEADS + h, dgam, 0.0)
            dbeta_all[p] += jnp.where(lane == h, rsum(dkb * k) + rsum(dvb * v), 0.0)
        for p in range(INTRA_PAIRS):
            dg_ref[p * PAIR:(p + 1) * PAIR, :] = dbeta_all[p] + _dot(low.astype(F32), dgam_all[p], TN, HI)

    rows_step = INTRA_PAIRS * PAIR
    tok = lambda w: pl.BlockSpec((rows_step, w), lambda n: (n, 0))
    hm = pl.BlockSpec((DN_HEADS, rows_step, PAIR), lambda n: (0, n, 0))
    sc = pl.BlockSpec((2 * INTRA_PAIRS, DN_HEADS, 8, 128), lambda n: (n, 0, 0, 0))
    return pl.pallas_call(
        body, name="dn_intra_bwd", grid=(npair // INTRA_PAIRS,),
        in_specs=[tok(DN_W), tok(DN_W), tok(DN_W), tok(128), pl.BlockSpec((8, rows_step), lambda n: (0, n)), hm, hm,
                  tok(DN_W), tok(DN_W), tok(DN_W), tok(DN_W), hm, sc, sc, pl.BlockSpec(memory_space=pl.ANY)],
        out_specs=[tok(DN_W), tok(DN_W), tok(DN_W), tok(128)],
        out_shape=[jax.ShapeDtypeStruct((t, DN_W), F32)] * 3 + [jax.ShapeDtypeStruct((t, 128), F32)],
        compiler_params=_params("parallel"),
    )(qn, kn, vc, gc, gr, tinv, a_qk, du, dw, dqg, dkd, da_qk, ddlast, dlast, dep)


def _dn_prep_bwd(pd, conv_w, al_vec, dt_vec, dqn, dkn, dvc, dgc, dz, tm):
    t = pd.shape[0]
    nt = t // tm
    tile = lambda i: nt - 1 - i

    def body(cur_ref, prev_ref, ba_ref, cw_ref, al_ref, dt_ref, dq_ref, dk_ref, dv_ref, dg_ref, dz_ref,
             o_ref, accw_ref, accg_ref, xs_ref, dc_ref, ds_ref, carry_ref):
        @pl.when(pl.program_id(0) == 0)
        def _():
            accw_ref[...] = jnp.zeros_like(accw_ref)
            accg_ref[...] = jnp.zeros_like(accg_ref)
            carry_ref[...] = jnp.zeros_like(carry_ref)

        _conv_window(cur_ref, prev_ref, xs_ref, tm, tile(pl.program_id(0)) > 0)
        taps = [xs_ref[pl.ds(HALO - DN_CONV + 1 + j, tm), :] for j in range(DN_CONV)]
        y = cw_ref[0:1, :] * taps[0]
        for j in range(1, DN_CONV):
            y += cw_ref[j:j + 1, :] * taps[j]
        sg = _sigmoid(y)
        c = y * sg
        for h in range(DN_HEADS):
            qs = slice(h * DN_HEAD_DIM, (h + 1) * DN_HEAD_DIM)
            ksl = slice(DN_W + h * DN_HEAD_DIM, DN_W + (h + 1) * DN_HEAD_DIM)
            for src, sl, scale in ((dq_ref, qs, DN_SCALE), (dk_ref, ksl, 1.0)):
                xh = c[:, sl]
                r = lax.rsqrt(jnp.sum(xh * xh, axis=-1, keepdims=True) + EPS)
                unit = xh * r
                dn = src[:, qs] * scale
                dc_ref[:, sl] = r * (dn - unit * jnp.sum(dn * unit, axis=-1, keepdims=True))
        dc_ref[:, 2 * DN_W:3 * DN_W] = dv_ref[...]
        dy = dc_ref[...] * (sg + y * sg * (1.0 - sg))
        for j in range(DN_CONV):
            accw_ref[j:j + 1, :] += jnp.sum(dy * taps[j], axis=0, keepdims=True)
        ds_ref[0:tm, :] = dy
        ds_ref[tm:tm + HALO, :] = carry_ref[...]
        carry_ref[...] = ds_ref[0:HALO, :]
        dx = cw_ref[0:1, :] * ds_ref[pl.ds(DN_CONV - 1, tm), :]
        for j in range(1, DN_CONV):
            dx += cw_ref[j:j + 1, :] * ds_ref[pl.ds(DN_CONV - 1 - j, tm), :]

        beta, g, pre = _gate_values(ba_ref[...], al_ref[...], dt_ref[...])
        dgb = dg_ref[...]
        lane = lax.broadcasted_iota(jnp.int32, dgb.shape, 1)
        is_b, is_a = lane < DN_HEADS, (lane >= DN_HEADS) & (lane < 2 * DN_HEADS)
        dpre = dgb * (-jnp.exp(al_ref[...])) * _sigmoid(pre)
        dba = jnp.where(is_b, dgb * beta * (1.0 - beta), jnp.where(is_a, dpre, 0.0))
        accg_ref[0:1, :] += jnp.sum(jnp.where(is_a, dgb * g, 0.0), axis=0, keepdims=True)
        accg_ref[1:2, :] += jnp.sum(jnp.where(is_a, dpre, 0.0), axis=0, keepdims=True)
        o_ref[:, 0:CONV_CH] = dx.astype(o_ref.dtype)
        o_ref[:, CONV_CH:CONV_CH + DN_W] = dz_ref[...].astype(o_ref.dtype)
        o_ref[:, CONV_CH + DN_W:DN_COLS] = dba.astype(o_ref.dtype)

    tok = lambda w: pl.BlockSpec((tm, w), lambda i: (tile(i), 0))
    return pl.pallas_call(
        body, name="dn_prep_bwd", grid=(nt,),
        in_specs=_dn_prep_specs(tm, tile) + [tok(DN_W), tok(DN_W), tok(DN_W), tok(128), tok(DN_W)],
        out_specs=[tok(DN_COLS), _full((8, CONV_CH)), _full((8, 128))],
        out_shape=[jax.ShapeDtypeStruct((t, DN_COLS), MXU_DTYPE),
                   jax.ShapeDtypeStruct((8, CONV_CH), F32), jax.ShapeDtypeStruct((8, 128), F32)],
        scratch_shapes=[pltpu.VMEM((HALO + tm, CONV_CH), F32), pltpu.VMEM((tm, CONV_CH), F32),
                        pltpu.VMEM((tm + HALO, CONV_CH), F32), pltpu.VMEM((HALO, CONV_CH), F32)],
        compiler_params=_params("arbitrary"),
    )(pd, pd, pd, conv_w, al_vec, dt_vec, dqn, dkn, dvc, dgc, dz)


def _pad_lanes(v, offset=0):
    return jnp.zeros((1, 128), F32).at[0, offset:offset + v.shape[0]].set(v.astype(F32))


class _LocalReducer:
    def start(self, grads):
        return jnp.zeros((8, 128), F32)

    def middle(self, after):
        return jnp.zeros((8, 128), F32)

    def finish(self, after):
        return None


def _local_step(x, p, tgt, sm, w, late, reducer):
    t = x.shape[0]
    tm = min(512, t // 2)
    tm_s = min(512, t // 2)
    tw = min(1024, t // 2)

    w_in_t = w["w_in_t"]
    wa_t = w_in_t[:ATTN_Q + 2 * ATTN_KV]
    wd_t = jnp.pad(w_in_t[ATTN_Q + 2 * ATTN_KV:], ((0, DN_COLS - (D_IN - ATTN_Q - 2 * ATTN_KV)), (0, 0)))
    conv_w = w["conv_w"]
    al_vec, dt_vec = _pad_lanes(sm["a_log"], DN_HEADS), _pad_lanes(sm["dt_bias"], DN_HEADS)
    sinks_vec = _pad_lanes(sm["sinks"])
    dn_w = sm["dn_norm"].reshape(1, 128)
    row = lambda v: v.reshape(1, D_MODEL)
    cos, sin = _rope_tables(t)

    u, pa, pd = _inproj(x, row(sm["norm_mix"]), wa_t, wd_t, tm_s)
    ao = _attn_fwd(pa, cos, sin, sinks_vec)
    qn, kn, vc, gc, gr = _dn_prep(pd, conv_w, al_vec, dt_vec, tm_s)
    uu, ww, qg, kd, a_qk, tinv, dlast = _dn_intra(qn, kn, vc, gc, gr)
    dn_out, o, vnew, sall = _dn_scan_fwd(uu, ww, qg, kd, a_qk, dlast, pd, dn_w)
    w_o, late_rest = late(dn_out)
    wo_a, wo_d = w_o[:ATTN_Q], w_o[ATTN_Q:]
    h1 = _oproj(x, ao, dn_out, wo_a, wo_d, tm)
    w = dict(w, **late_rest(h1))
    w_proj = jnp.transpose(w["w_proj4"], (1, 0, 2)).reshape(PLE_DIM, D_MODEL)
    m, r, h2 = _mlp_fwd(h1, row(sm["norm_mlp"]), w["w_up4"], w["w_down"], tw)
    dh2, dh2b, dgp, dpp, n3, pb, acc_ple = _ple_loss(h2, p, tgt, row(sm["norm_ple"]), row(sm["norm_final"]),
                                                     w["w_gate"], w_proj, tm_s)
    g_w_gate = _wgrad(n3, dgp, "wgrad_gate", D_MODEL, D_MODEL, tw)
    g_w_proj = _wgrad(pb, dpp, "wgrad_proj", PLE_DIM, D_MODEL, tw)
    da, dh1, dh1b, dao, ddn, acc_mlp = _mlp_bwd(dh2, dh2b, r, h1, row(sm["norm_mlp"]), w["w_up4"], w["w_down"],
                                                wo_a, wo_d, tm)
    g_w_up4 = _wgrad(m, da, "wgrad_up", D_MODEL, FF_BLOCK, tw, stacked=True)
    g_w_down = _wgrad(r, dh2b, "wgrad_down", FF_BLOCK, D_MODEL, tw,
                      prep=lambda rv: jnp.square(rv.astype(F32)).astype(MXU_DTYPE))
    g_w_o = _wgrad_cat([ao, dn_out], [dh1b], "wgrad_o", tw)
    early = dict(w_up4=g_w_up4, w_down=g_w_down, w_gate=g_w_gate, w_proj=g_w_proj, w_o=g_w_o)
    dep = reducer.start(early)
    dz, du, dw, dqg, dkd, da_qk, ddlast, acc_dn = _dn_scan_bwd(ddn, o, vnew, sall, ww, qg, kd, a_qk, dlast, pd, dn_w,
                                                               dep)
    dep = reducer.middle(du)
    dqn, dkn, dvc, dgc = _dn_intra_bwd(qn, kn, vc, gc, gr, tinv, a_qk, du, dw, dqg, dkd, da_qk, ddlast, dlast, dep)
    d_dn, acc_conv, acc_gate = _dn_prep_bwd(pd, conv_w, al_vec, dt_vec, dqn, dkn, dvc, dgc, dz, tm_s)
    dq, dk, dv, acc_attn = _attn_bwd(pa, cos, sin, sinks_vec, dao)
    reducer.finish(dq)
    wq_t, wk_t, wv_t = wa_t[:ATTN_Q], wa_t[ATTN_Q:ATTN_Q + ATTN_KV], wa_t[ATTN_Q + ATTN_KV:]
    dx, acc_mix = _inproj_bwd(x, dh1, row(sm["norm_mix"]), [dq, dk, dv, d_dn], [wq_t, wk_t, wv_t, wd_t], tm_s)

    g_w_in_t = _wgrad_cat([dq, dk, dv, d_dn], [u], "wgrad_in", tw)
    grads = dict(early, w_in_t=g_w_in_t)
    sums = dict(loss=acc_ple[2, 0], norm_final=acc_ple[0], norm_ple=acc_ple[1], norm_mlp=acc_mlp[0], norm_mix=acc_mix[0],
                dn_norm=acc_dn[0], sinks=acc_attn[0, :ATTN_HEADS], a_log=acc_gate[0, DN_HEADS:2 * DN_HEADS],
                dt_bias=acc_gate[1, DN_HEADS:2 * DN_HEADS], conv_w=acc_conv[:DN_CONV])
    return sums, dx, grads


MESH = pl.DeviceIdType.MESH
ANY = pl.BlockSpec(memory_space=pl.ANY)
N_CHIPS = 4
N_DEV = 8


def _place():
    x, y, c = lax.axis_index("x"), lax.axis_index("y"), lax.axis_index("c")
    chips = [(1 - x, y), (x, 1 - y), (1 - x, 1 - y)]
    return x, y, c, chips


def _gather_weights(shards, conv_s):
    n = len(shards)
    per = 7

    def body(*refs):
        in_refs, conv_ref = refs[:n], refs[n]
        out_refs, conv_out = refs[n + 1:2 * n + 1], refs[2 * n + 1]
        send_sems, recv_sems = refs[2 * n + 2:]
        x, y, c, chips = _place()
        sibling = (x, y, 1 - c)

        def blk(a, px, py, pc):
            hr = in_refs[a].shape[0] // 2
            return out_refs[a].at[2 * px + py, pl.ds(pc * hr, hr), :]

        def mine(a):
            hr = in_refs[a].shape[0] // 2
            return in_refs[a].at[pl.ds(c * hr, hr), :]

        def rcopy(a, k, block, to, src=None):
            return pltpu.make_async_remote_copy(
                src_ref=blk(a, *block) if src is None else src, dst_ref=blk(a, *block),
                send_sem=send_sems.at[per * a + k], recv_sem=recv_sems.at[per * a + k],
                device_id=to, device_id_type=MESH)

        def whole(a, to):
            return pltpu.make_async_remote_copy(
                src_ref=in_refs[a], dst_ref=out_refs[a].at[2 * x + y],
                send_sem=send_sems.at[per * a], recv_sem=recv_sems.at[per * a], device_id=to, device_id_type=MESH)

        def ccopy(j, to):
            return pltpu.make_async_remote_copy(
                src_ref=conv_ref, dst_ref=conv_out.at[2 * x + y],
                send_sem=send_sems.at[per * n + j], recv_sem=recv_sems.at[per * n + j],
                device_id=to, device_id_type=MESH)

        started = []
        for a in range(n):
            first = [whole(a, sibling)]
            first += [rcopy(a, 1 + j, (x, y, c), (*chip, c), src=mine(a)) for j, chip in enumerate(chips)]
            for cp in first:
                cp.start()
            started += first
        conv_sends = [ccopy(j, (*chip, c)) for j, chip in enumerate(chips)] + [ccopy(3, sibling)]
        for cp in conv_sends:
            cp.start()
        started += conv_sends
        for a in range(n):
            for j, chip in enumerate(chips):
                rcopy(a, 1 + j, (*chip, c), (x, y, c)).wait_recv()
                fwd = rcopy(a, 4 + j, (*chip, c), sibling)
                fwd.start()
                started.append(fwd)
        for a in range(n):
            whole(a, sibling).wait_recv()
            for j, chip in enumerate(chips):
                rcopy(a, 4 + j, (*chip, 1 - c), (x, y, c)).wait_recv()
        for j, chip in enumerate(chips + [(x, y)]):
            pltpu.make_async_remote_copy(
                src_ref=conv_ref, dst_ref=conv_out.at[2 * chip[0] + chip[1]],
                send_sem=send_sems.at[per * n + j], recv_sem=recv_sems.at[per * n + j],
                device_id=sibling, device_id_type=MESH).wait_recv()
        for cp in started:
            cp.wait_send()

    nsem = per * n + 4
    out_shape = [jax.ShapeDtypeStruct((N_CHIPS,) + s.shape, s.dtype) for s in shards]
    out_shape.append(jax.ShapeDtypeStruct((N_CHIPS,) + conv_s.shape, conv_s.dtype))
    return pl.pallas_call(
        body, name="gather_weights", in_specs=[ANY] * (n + 1), out_specs=[ANY] * (n + 1), out_shape=out_shape,
        scratch_shapes=[pltpu.SemaphoreType.DMA((nsem,)), pltpu.SemaphoreType.DMA((nsem,))],
    )(*shards, conv_s)


HBM = pl.BlockSpec(memory_space=pltpu.HBM)
SEM = pl.BlockSpec(memory_space=pltpu.SEMAPHORE)
EFFECT = pltpu.SideEffectType.DATAFLOW_SIDE_EFFECTING
LATE_COPIES = 7


def _late_copies(in_refs, land_refs, send_sems, recv_sems, only=None):
    x, y, c, chips = _place()
    sends, arrivals = [], []
    for a, (src, land) in enumerate(zip(in_refs, land_refs)):
        if only is not None and a not in only:
            continue
        hr = src.shape[0] // 2
        base = LATE_COPIES * a

        def cp(src_ref, dst_ref, s_idx, r_idx, to):
            return pltpu.make_async_remote_copy(src_ref=src_ref, dst_ref=dst_ref, send_sem=send_sems.at[base + s_idx],
                                                recv_sem=recv_sems.at[base + r_idx], device_id=to, device_id_type=MESH)

        sends.append(cp(src, land.at[2 * x + y], 0, 0, (x, y, 1 - c)))
        arrivals.append(cp(src, land.at[2 * x + y], 0, 0, (x, y, 1 - c)))
        for j, chip in enumerate(chips):
            for pc in range(2):
                half = src.at[pl.ds(c * hr, hr), :]
                sends.append(cp(half, land.at[2 * x + y, pl.ds(c * hr, hr), :], 1 + 2 * j + pc, 1 + 2 * j + c, (*chip, pc)))
                arrivals.append(cp(half, land.at[2 * chip[0] + chip[1], pl.ds(pc * hr, hr), :], 1 + 2 * j + pc,
                                   1 + 2 * j + pc, (*chip, pc)))
    return sends, arrivals


def _copies_start(name, build, nsem, srcs, land_shapes, after):
    n = len(srcs)

    def body(*refs):
        sends, _ = build(refs[:n], refs[n:2 * n], refs[2 * n + 1], refs[2 * n + 2])
        for cp in sends:
            cp.start()
        refs[-1][...] = jnp.zeros_like(refs[-1])

    lands = [pltpu.with_memory_space_constraint(lax.empty(s.shape, s.dtype), pltpu.HBM) for s in land_shapes]
    ins = [pltpu.with_memory_space_constraint(s, pltpu.HBM) for s in srcs]
    out = pl.pallas_call(
        body, name=name,
        out_shape=(pltpu.SemaphoreType.DMA((nsem,)), pltpu.SemaphoreType.DMA((nsem,)),
                   *[pltpu.HBM(s.shape, s.dtype) for s in srcs], *[pltpu.HBM(s.shape, s.dtype) for s in land_shapes],
                   jax.ShapeDtypeStruct((8, 128), F32)),
        in_specs=[HBM] * (2 * n) + [ANY],
        out_specs=(SEM, SEM, *[HBM] * (2 * n), pl.BlockSpec(memory_space=pltpu.VMEM)),
        input_output_aliases={i: 2 + i for i in range(2 * n)},
        compiler_params=pltpu.CompilerParams(has_side_effects=EFFECT),
    )(*ins, *lands, after)
    return out[0], out[1], out[2:2 + n], out[2 + n:2 + 2 * n], out[-1]


def _copies_wait(name, build, started, after):
    send_sems, recv_sems, srcs, lands, _ = started
    n = len(srcs)

    def body(*refs):
        sends, arrivals = build(refs[:n], refs[n:2 * n], refs[2 * n], refs[2 * n + 1])
        for cp in sends:
            cp.wait_send()
        for cp in arrivals:
            cp.wait_recv()

    out = pl.pallas_call(
        body, name=name,
        out_shape=(*[pltpu.HBM(s.shape, s.dtype) for s in srcs], *[pltpu.HBM(l.shape, l.dtype) for l in lands]),
        in_specs=[HBM] * (2 * n) + [SEM, SEM, ANY],
        out_specs=tuple([HBM] * (2 * n)),
        input_output_aliases={i: i for i in range(2 * n)},
        compiler_params=pltpu.CompilerParams(has_side_effects=EFFECT),
    )(*srcs, *lands, send_sems, recv_sems, after)
    return out[:n], out[n:]


def _exchange_copies(g_refs, got_refs, send_sems, recv_sems):
    x, y, c, _ = _place()
    sends, arrivals = [], []
    for a, (g, got) in enumerate(zip(g_refs, got_refs)):
        hr = g.shape[1] // 2
        cp = pltpu.make_async_remote_copy(
            src_ref=g.at[:, pl.ds((1 - c) * hr, hr), :], dst_ref=got, send_sem=send_sems.at[a],
            recv_sem=recv_sems.at[a], device_id=(x, y, 1 - c), device_id_type=MESH)
        sends.append(cp)
        arrivals.append(cp)
    return sends, arrivals


def _scatter_copies(s_refs, got_refs, send_sems, recv_sems):
    x, y, c, chips = _place()
    sends, arrivals = [], []
    for a, (s16, got) in enumerate(zip(s_refs, got_refs)):
        for j, chip in enumerate(chips):
            cp = pltpu.make_async_remote_copy(
                src_ref=s16.at[2 * chip[0] + chip[1]], dst_ref=got.at[j], send_sem=send_sems.at[3 * a + j],
                recv_sem=recv_sems.at[3 * a + j], device_id=(*chip, c), device_id_type=MESH)
            sends.append(cp)
            arrivals.append(cp)
    return sends, arrivals


def _share_halves(name, bufs, dep):
    n = len(bufs)

    def body(*refs):
        out_refs = refs[n + 1:2 * n + 1]
        send_sems, recv_sems = refs[2 * n + 1:]
        x, y, c, _ = _place()
        remote = [pltpu.make_async_remote_copy(
            src_ref=out_refs[a].at[c], dst_ref=out_refs[a].at[c], send_sem=send_sems.at[a], recv_sem=recv_sems.at[a],
            device_id=(x, y, 1 - c), device_id_type=MESH) for a in range(n)]
        for cp in remote:
            cp.start()
        for a in range(n):
            pltpu.make_async_remote_copy(
                src_ref=out_refs[a].at[c], dst_ref=out_refs[a].at[1 - c], send_sem=send_sems.at[a],
                recv_sem=recv_sems.at[a], device_id=(x, y, 1 - c), device_id_type=MESH).wait_recv()
        for cp in remote:
            cp.wait_send()

    return pl.pallas_call(
        body, name=name, in_specs=[ANY] * (n + 1), out_specs=[ANY] * n,
        out_shape=[jax.ShapeDtypeStruct(b.shape, b.dtype) for b in bufs],
        input_output_aliases={a: a for a in range(n)},
        scratch_shapes=[pltpu.SemaphoreType.DMA((n,)), pltpu.SemaphoreType.DMA((n,))],
    )(*bufs, dep)


SMALL_ROWS, SMALL_COLS = 16, CONV_CH


def _allreduce_small(block):
    m_per, ncol = block.shape

    def body(x_ref, sum_ref, all_ref, send_sems, recv_sems, local_sem):
        x, y, c, chips = _place()
        me, sibling = (x, y, c), (x, y, 1 - c)

        def rows(px, py, pc):
            return all_ref.at[pl.ds((4 * px + 2 * py + pc) * m_per, m_per), :]

        def copy(k, block_of, to, src=None):
            return pltpu.make_async_remote_copy(
                src_ref=rows(*block_of) if src is None else src, dst_ref=rows(*block_of),
                send_sem=send_sems.at[k], recv_sem=recv_sems.at[k], device_id=to, device_id_type=MESH)

        mine = pltpu.make_async_copy(x_ref, rows(*me), local_sem)
        mine.start()
        first = [copy(0, me, sibling, src=x_ref)]
        first += [copy(1 + j, me, (*chip, c), src=x_ref) for j, chip in enumerate(chips)]
        for cp in first:
            cp.start()
        passed = [copy(4 + j, (*chip, c), sibling) for j, chip in enumerate(chips)]
        for j, chip in enumerate(chips):
            copy(1 + j, (*chip, c), me).wait_recv()
            passed[j].start()
        copy(0, sibling, me).wait_recv()
        for j, chip in enumerate(chips):
            copy(4 + j, (*chip, 1 - c), me).wait_recv()
        for cp in first + passed:
            cp.wait_send()
        mine.wait()
        total = all_ref[0:m_per, :]
        for d in range(1, N_DEV):
            total = total + all_ref[d * m_per:(d + 1) * m_per, :]
        sum_ref[...] = total

    vm = pl.BlockSpec(memory_space=pltpu.VMEM)
    return pl.pallas_call(
        body, name="allreduce_small", in_specs=[vm], out_specs=vm,
        out_shape=jax.ShapeDtypeStruct((m_per, ncol), F32),
        scratch_shapes=[pltpu.VMEM((N_DEV * m_per, ncol), F32), pltpu.SemaphoreType.DMA((7,)),
                        pltpu.SemaphoreType.DMA((7,)), pltpu.SemaphoreType.DMA],
    )(block)


def _row_tile(rows, cols):
    tile = rows
    while tile * cols * 4 > (1 << 20) and tile % 16 == 0:
        tile //= 2
    return tile


def _elementwise(fn, name, ins, out_dtypes, dep):
    rows, cols = ins[0].shape
    tile = _row_tile(rows, cols)

    def body(*refs):
        outs = fn(*[r[...] for r in refs[:len(ins)]])
        for o_ref, o in zip(refs[len(ins) + 1:], outs):
            o_ref[...] = o.astype(o_ref.dtype)

    if tile * cols * 4 > (1 << 21) and cols % 512 == 0:
        spec = pl.BlockSpec((rows, 256), lambda i: (0, i))
        steps = cols // 256
    else:
        spec = pl.BlockSpec((tile, cols), lambda i: (i, 0))
        steps = rows // tile
    return pl.pallas_call(
        body, name=name, grid=(steps,), in_specs=[spec] * len(ins) + [pl.BlockSpec(memory_space=pl.ANY)],
        out_specs=[spec] * len(out_dtypes),
        out_shape=[jax.ShapeDtypeStruct((rows, cols), d) for d in out_dtypes],
        compiler_params=_params("parallel"),
    )(*ins, dep)


def _adamw_tile(w, g, m, v):
    m = ADAM_B1 * m + (1.0 - ADAM_B1) * g
    v = ADAM_B2 * v + (1.0 - ADAM_B2) * jnp.square(g)
    m_hat = m / (1.0 - ADAM_B1 ** ADAM_STEP)
    v_hat = v / (1.0 - ADAM_B2 ** ADAM_STEP)
    delta = -ADAM_LR * (m_hat / (jnp.sqrt(v_hat) + ADAM_EPS) + ADAM_WD * w)
    return delta, m, v


def _adamw(name, w, g, m, v, dep):
    return _elementwise(_adamw_tile, name, [w, g, m, v], [F32, F32, F32], dep)


def _chip_sum(name, g4, got, place):
    nchip, hr, cols = got.shape
    tile = _row_tile(hr, cols)
    nblk = hr // tile

    def body(pl_ref, g_ref, o_ref, s32_ref, s16_ref):
        s = g_ref[...] + o_ref[...]
        s32_ref[...] = s
        s16_ref[...] = s.astype(BF16)

    spec = pl.BlockSpec((None, tile, cols), lambda k, i, pr: (k, i, 0))
    return pl.pallas_call(
        body, name=name,
        grid_spec=pltpu.PrefetchScalarGridSpec(
            num_scalar_prefetch=1, grid=(nchip, nblk),
            in_specs=[pl.BlockSpec((None, tile, cols), lambda k, i, pr: (k, pr[1] * nblk + i, 0)), spec],
            out_specs=[spec, spec]),
        out_shape=[jax.ShapeDtypeStruct(got.shape, F32), jax.ShapeDtypeStruct(got.shape, BF16)],
        compiler_params=_params("parallel", "parallel"),
    )(place, g4, got)


def _mesh_sum(name, s32, got, place):
    _, hr, cols = s32.shape
    tile = _row_tile(hr, cols)

    def body(pl_ref, own_ref, g0_ref, g1_ref, g2_ref, o_ref):
        o_ref[...] = ((own_ref[...] + g0_ref[...].astype(F32)) + g1_ref[...].astype(F32)) + g2_ref[...].astype(F32)

    slab = lambda j: pl.BlockSpec((None, tile, cols), lambda i, pr: (j, i, 0))
    return pl.pallas_call(
        body, name=name,
        grid_spec=pltpu.PrefetchScalarGridSpec(
            num_scalar_prefetch=1, grid=(hr // tile,),
            in_specs=[pl.BlockSpec((None, tile, cols), lambda i, pr: (pr[0], i, 0)), slab(0), slab(1), slab(2)],
            out_specs=pl.BlockSpec((None, tile, cols), lambda i, pr: (pr[1], i, 0))),
        out_shape=jax.ShapeDtypeStruct((2, hr, cols), F32),
        compiler_params=_params("parallel"),
    )(place, s32, got, got, got)


def _place_operand():
    return jnp.stack([2 * lax.axis_index("x") + lax.axis_index("y"), lax.axis_index("c")]).astype(jnp.int32)


W_IN_ROWS = 720
W_IN_GATHER_ROWS = 736


def _per_chip(name, g):
    if name == "w_in_t":
        rows = D_IN // N_CHIPS
        return jnp.stack([lax.slice_in_dim(g, rows * k, rows * k + W_IN_ROWS) for k in range(N_CHIPS)])
    if name == "w_proj":
        return jnp.transpose(g.reshape(PLE_DIM, N_CHIPS, D_MODEL // N_CHIPS), (1, 0, 2))
    if name == "w_up4":
        return g
    return g.reshape(N_CHIPS, g.shape[0] // N_CHIPS, g.shape[1])


class _EarlyReducer:
    def __init__(self, tag):
        self.tag = tag

    def start(self, grads):
        self.names = list(grads)
        self.place = _place_operand()
        slabs = [_per_chip(k, grads[k]) for k in self.names]
        halves = [jax.ShapeDtypeStruct((s.shape[0], s.shape[1] // 2, s.shape[2]), F32) for s in slabs]
        self.a = _copies_start(self.tag + "exchange_start", _exchange_copies, len(slabs), slabs, halves,
                               slabs[0][0, :8, :128])
        return self.a[-1]

    def middle(self, after):
        slabs, got = _copies_wait(self.tag + "exchange_wait", _exchange_copies, self.a, after)
        self.sums = [_chip_sum(self.tag + "chip_sum_" + k, s, g, self.place) for k, s, g in zip(self.names, slabs, got)]
        s16 = [s[1] for s in self.sums]
        lands = [jax.ShapeDtypeStruct((3,) + s.shape[1:], BF16) for s in s16]
        self.b = _copies_start(self.tag + "scatter_start", _scatter_copies, 3 * len(s16), s16, lands,
                               self.sums[0][0][0, :8, :128])
        return self.b[-1]

    def finish(self, after):
        _, got = _copies_wait(self.tag + "scatter_wait", _scatter_copies, self.b, after)
        self.bufs = {k: _mesh_sum(self.tag + "mesh_sum_" + k, s[0], g, self.place)
                     for k, s, g in zip(self.names, self.sums, got)}


def kernel(x, p, norm_mix, w_in, conv_w, a_log, dt_bias, dn_norm, sinks, w_o, norm_mlp, w_up, w_down, norm_ple, w_ple_gate, w_ple_proj, norm_final, loss_target, m_norm_mix, m_w_in, m_conv_w, m_a_log, m_dt_bias, m_dn_norm, m_sinks, m_w_o, m_norm_mlp, m_w_up, m_w_down, m_norm_ple, m_w_ple_gate, m_w_ple_proj, m_norm_final, v_norm_mix, v_w_in, v_conv_w, v_a_log, v_dt_bias, v_dn_norm, v_sinks, v_w_o, v_norm_mlp, v_w_up, v_w_down, v_norm_ple, v_w_ple_gate, v_w_ple_proj, v_norm_final):
    chip = 2 * lax.axis_index("x") + lax.axis_index("y")
    big = dict(w_in=w_in[0], w_o=w_o[0], w_up=w_up[0], w_down=w_down[0], w_gate=w_ple_gate[0], w_proj=w_ple_proj[0])
    big_m = dict(w_in=m_w_in[0], w_o=m_w_o[0], w_up=m_w_up[0], w_down=m_w_down[0], w_gate=m_w_ple_gate[0], w_proj=m_w_ple_proj[0])
    big_v = dict(w_in=v_w_in[0], w_o=v_w_o[0], w_up=v_w_up[0], w_down=v_w_down[0], w_gate=v_w_ple_gate[0], w_proj=v_w_ple_proj[0])
    names = list(big)

    rows_in = D_IN // N_CHIPS
    w_in_shard_t = jnp.pad(big["w_in"].T.astype(BF16), ((0, W_IN_GATHER_ROWS - rows_in), (0, 0)))
    w_in_all, conv_all = _gather_weights([w_in_shard_t], conv_w[0])
    late_names = names[1:]
    late_shards = [big[k].astype(BF16) for k in late_names]
    gather = _copies_start("gather_start", _late_copies, LATE_COPIES * len(late_shards), late_shards,
                           [jax.ShapeDtypeStruct((N_CHIPS,) + s.shape, BF16) for s in late_shards], w_in_all)
    token = gather[-1]
    w = dict(w_in_t=jnp.concatenate([w_in_all[k, :rows_in] for k in range(N_CHIPS)], axis=0),
             conv_w=jnp.transpose(conv_all, (1, 0, 2)).reshape(DN_CONV, CONV_CH))
    sm = dict(norm_mix=norm_mix[0] + token[0, 0], a_log=a_log[0], dt_bias=dt_bias[0], dn_norm=dn_norm[0],
              sinks=sinks[0], norm_mlp=norm_mlp[0], norm_ple=norm_ple[0], norm_final=norm_final)

    def late(after):
        first = functools.partial(_late_copies, only=(0,))
        srcs, lands = _copies_wait("gather_wait_o", first, gather, after)

        def rest(after2):
            others = functools.partial(_late_copies, only=tuple(range(1, len(late_names))))
            gw = dict(zip(late_names, _copies_wait("gather_wait_rest", others, gather[:2] + (srcs, lands, None), after2)[1]))
            return dict(w_up4=gw["w_up"], w_down=gw["w_down"].reshape(D_FF, D_MODEL),
                        w_gate=gw["w_gate"].reshape(D_MODEL, D_MODEL), w_proj4=gw["w_proj"])

        return lands[0].reshape(D_MODEL, D_MODEL), rest

    reducer = _EarlyReducer("early_")
    sums, grad_x, g = _local_step(x[0], p[0, 0], loss_target[0], sm, w, late, reducer)

    last = _EarlyReducer("last_")
    dep_a = last.start({"w_in_t": g["w_in_t"]})

    row = lambda v: jnp.zeros((SMALL_COLS,), F32).at[:v.shape[0]].set(v)
    misc = jnp.zeros((SMALL_COLS,), F32).at[0:4].set(sums["a_log"]).at[4:8].set(sums["dt_bias"]) \
        .at[8:16].set(sums["sinks"]).at[128:256].set(sums["dn_norm"]).at[256].set(sums["loss"])
    small = jnp.concatenate([sums["conv_w"], jnp.stack([row(sums["norm_mix"]), row(sums["norm_mlp"]), row(sums["norm_ple"]),
                                                        row(sums["norm_final"]), misc]),
                             jnp.zeros((SMALL_ROWS - 9, SMALL_COLS), F32)], axis=0)
    tot = _allreduce_small(small + dep_a[0, 0])
    dep_b = last.middle(tot)
    grad_key = dict(w_o="w_o", w_up="w_up4", w_down="w_down", w_gate="w_gate", w_proj="w_proj")
    full = _share_halves("share_halves", [reducer.bufs[grad_key[k]] for k in late_names], dep_b)
    red = {k: f.reshape(-1, f.shape[-1]) for k, f in zip(late_names, full)}
    loss = tot[8, 256]
    ncw = CONV_CH // N_CHIPS

    def pack(cw, nmix, nmlp, nple, nfin, al, dtb, sk, dnn):
        misc_p = jnp.zeros((SMALL_COLS,), F32).at[0:4].set(al).at[4:8].set(dtb).at[8:16].set(sk).at[128:256].set(dnn)
        cw_p = jnp.zeros((DN_CONV, SMALL_COLS), F32).at[:, :ncw].set(cw)
        return jnp.concatenate([cw_p, jnp.stack([row(nmix), row(nmlp), row(nple), row(nfin), misc_p]),
                                jnp.zeros((SMALL_ROWS - 9, SMALL_COLS), F32)], axis=0)

    def unpack(buf):
        return dict(conv_w=buf[0:4, :ncw][None], norm_mix=buf[4, :D_MODEL][None], norm_mlp=buf[5, :D_MODEL][None],
                    norm_ple=buf[6, :D_MODEL][None], norm_final=buf[7, :D_MODEL], a_log=buf[8, 0:4][None],
                    dt_bias=buf[8, 4:8][None], sinks=buf[8, 8:16][None], dn_norm=buf[8, 128:256][None])

    g_conv_shard = lax.dynamic_slice(tot[0:4], (0, chip * ncw), (DN_CONV, ncw))
    g_small = pack(g_conv_shard, tot[4, :D_MODEL], tot[5, :D_MODEL], tot[6, :D_MODEL], tot[7, :D_MODEL],
                   tot[8, 0:4], tot[8, 4:8], tot[8, 8:16], tot[8, 128:256])
    w_small = pack(conv_w[0], norm_mix[0], norm_mlp[0], norm_ple[0], norm_final, a_log[0], dt_bias[0], sinks[0], dn_norm[0])
    m_small = pack(m_conv_w[0], m_norm_mix[0], m_norm_mlp[0], m_norm_ple[0], m_norm_final, m_a_log[0], m_dt_bias[0],
                   m_sinks[0], m_dn_norm[0])
    v_small = pack(v_conv_w[0], v_norm_mix[0], v_norm_mlp[0], v_norm_ple[0], v_norm_final, v_a_log[0], v_dt_bias[0],
                   v_sinks[0], v_dn_norm[0])

    ref_name = dict(w_in="w_in", w_o="w_o", w_up="w_up", w_down="w_down", w_gate="w_ple_gate", w_proj="w_ple_proj")
    out_g, out_d, out_m, out_v = {}, {}, {}, {}

    def update(k, dep):
        d_k, m_k, v_k = _adamw("adamw_" + k, big[k], red[k], big_m[k], big_v[k], dep)
        out_g[ref_name[k]], out_d[ref_name[k]] = red[k][None], d_k[None]
        out_m[ref_name[k]], out_v[ref_name[k]] = m_k[None], v_k[None]
        return d_k

    for k in late_names:
        done = update(k, dep_b)
    small_out = _adamw("adamw_small", w_small, g_small, m_small, v_small, dep_b)
    d_s, m_s, v_s = (unpack(b) for b in small_out)
    g_s = unpack(g_small)
    for src, dst in ((g_s, out_g), (d_s, out_d), (m_s, out_m), (v_s, out_v)):
        dst.update(src)
    last.finish(done + small_out[0][0:1, 0:1])
    (w_in_full,) = _share_halves("share_halves_w_in", [last.bufs["w_in_t"]], dep_b)
    g_t = w_in_full.reshape(W_IN_ROWS, D_MODEL)[:D_IN // N_CHIPS]
    d_t, m_t, v_t = _adamw("adamw_w_in", big["w_in"].T, g_t, big_m["w_in"].T, big_v["w_in"].T, dep_b)
    out_g["w_in"], out_d["w_in"], out_m["w_in"], out_v["w_in"] = g_t.T[None], d_t.T[None], m_t.T[None], v_t.T[None]
    order = ["norm_mix", "w_in", "conv_w", "a_log", "dt_bias", "dn_norm", "sinks", "w_o", "norm_mlp", "w_up", "w_down",
             "norm_ple", "w_ple_gate", "w_ple_proj", "norm_final"]
    return (loss, grad_x[None], *[out_g[k] for k in order], *[out_d[k] for k in order],
            *[out_m[k] for k in order], *[out_v[k] for k in order])
```

```python
import functools

import jax
import jax.numpy as jnp
from jax import lax
from jax.experimental import pallas as pl
from jax.experimental.pallas import tpu as pltpu

F32 = jnp.float32
BF16 = jnp.bfloat16
MXU_DTYPE = jnp.bfloat16
HI = lax.Precision.HIGHEST

D_MODEL = 1024
PLE_DIM = 256
ATTN_HEADS = 8
ATTN_KV_HEADS = 2
ATTN_GROUPS = ATTN_HEADS // ATTN_KV_HEADS
ATTN_HEAD_DIM = 64
ATTN_BLOCK = 128
ROPE_THETA = 10000.0
DN_HEADS = 4
DN_HEAD_DIM = 128
DN_CONV = 4
DN_CHUNK = 64
D_FF = 4 * D_MODEL
EPS = 1e-6
ATTN_Q = ATTN_HEADS * ATTN_HEAD_DIM
ATTN_KV = ATTN_KV_HEADS * ATTN_HEAD_DIM
DN_W = DN_HEADS * DN_HEAD_DIM
CONV_CH = 3 * DN_W
D_IN = ATTN_Q + 2 * ATTN_KV + 4 * DN_W + 2 * DN_HEADS
DN_COLS = 4 * DN_W + 128
DN_SCALE = DN_HEAD_DIM ** -0.5
ATTN_SCALE = ATTN_HEAD_DIM ** -0.5
FF_BLOCKS = 4
FF_BLOCK = D_FF // FF_BLOCKS

ADAM_LR = 0.001
ADAM_B1 = 0.9
ADAM_B2 = 0.999
ADAM_EPS = 1e-08
ADAM_WD = 0.01
ADAM_STEP = 10

V7X_VMEM_BYTES = 64 * 1024 * 1024
VMEM_LIMIT = 48 * 1024 * 1024

NN = ((1,), (0,))
NT = ((1,), (1,))
TN = ((0,), (0,))


def _dot(a, b, dims=NN, prec=None):
    if a.dtype != b.dtype:
        a, b = a.astype(MXU_DTYPE), b.astype(MXU_DTYPE)
    return lax.dot_general(a, b, (dims, ((), ())), precision=prec, preferred_element_type=F32)


def _sigmoid(x):
    return 1.0 / (1.0 + jnp.exp(-x))


def _softplus(x):
    return jnp.maximum(x, 0.0) + jnp.log(1.0 + jnp.exp(-jnp.abs(x)))


def _params(*sem):
    return pltpu.CompilerParams(dimension_semantics=sem, vmem_limit_bytes=VMEM_LIMIT)


def _rms_fwd(xv, g):
    r = lax.rsqrt(jnp.mean(xv * xv, axis=-1, keepdims=True) + EPS)
    return xv * r * g


def _rms_bwd(xv, g, dn):
    r = lax.rsqrt(jnp.mean(xv * xv, axis=-1, keepdims=True) + EPS)
    xh = xv * r
    dg = jnp.sum(dn * xh, axis=0, keepdims=True)
    dxh = dn * g
    dx = r * (dxh - xh * jnp.mean(dxh * xh, axis=-1, keepdims=True))
    return dx, dg


def _full(shape):
    return pl.BlockSpec(shape, lambda *_: (0,) * len(shape))


def _inproj(x, g_mix, wa_t, wd_t, tm):
    t = x.shape[0]

    def body(x_ref, g_ref, wa_ref, wd_ref, u_ref, pa_ref, pd_ref):
        u = _rms_fwd(x_ref[...], g_ref[...]).astype(MXU_DTYPE)
        u_ref[...] = u
        pa_ref[...] = _dot(u, wa_ref[...], NT)
        pd_ref[...] = _dot(u, wd_ref[...], NT)

    na, nd = wa_t.shape[0], wd_t.shape[0]
    return pl.pallas_call(
        body, name="inproj", grid=(t // tm,),
        in_specs=[pl.BlockSpec((tm, D_MODEL), lambda i: (i, 0)), _full((1, D_MODEL)),
                  _full((na, D_MODEL)), _full((nd, D_MODEL))],
        out_specs=[pl.BlockSpec((tm, D_MODEL), lambda i: (i, 0)), pl.BlockSpec((tm, na), lambda i: (i, 0)),
                   pl.BlockSpec((tm, nd), lambda i: (i, 0))],
        out_shape=[jax.ShapeDtypeStruct((t, D_MODEL), MXU_DTYPE), jax.ShapeDtypeStruct((t, na), F32),
                   jax.ShapeDtypeStruct((t, nd), F32)],
        compiler_params=_params("parallel"),
    )(x, g_mix, wa_t, wd_t)


def _oproj(x, ao, dn, wo_a, wo_d, tm):
    t = x.shape[0]

    def body(x_ref, ao_ref, dn_ref, wa_ref, wd_ref, h_ref):
        h_ref[...] = (x_ref[...] + _dot(ao_ref[...].astype(MXU_DTYPE), wa_ref[...])
                      + _dot(dn_ref[...].astype(MXU_DTYPE), wd_ref[...]))

    half = ao.shape[1]
    return pl.pallas_call(
        body, name="oproj", grid=(t // tm,),
        in_specs=[pl.BlockSpec((tm, D_MODEL), lambda i: (i, 0)), pl.BlockSpec((tm, half), lambda i: (i, 0)),
                  pl.BlockSpec((tm, half), lambda i: (i, 0)), _full((half, D_MODEL)), _full((half, D_MODEL))],
        out_specs=pl.BlockSpec((tm, D_MODEL), lambda i: (i, 0)),
        out_shape=jax.ShapeDtypeStruct((t, D_MODEL), F32),
        compiler_params=_params("parallel"),
    )(x, ao, dn, wo_a, wo_d)


def _mlp_fwd(h1, g_mlp, w_up4, w_down, tm):
    t = h1.shape[0]

    def body(h_ref, g_ref, wu_ref, wd_ref, m_ref, r_ref, h2_ref, acc_ref):
        k = pl.program_id(1)

        @pl.when(k == 0)
        def _():
            m_ref[...] = _rms_fwd(h_ref[...], g_ref[...]).astype(MXU_DTYPE)
            acc_ref[...] = jnp.zeros_like(acc_ref)

        r = jnp.maximum(_dot(m_ref[...], wu_ref[...]), 0.0)
        r_ref[...] = r.astype(MXU_DTYPE)
        s = jnp.square(r).astype(MXU_DTYPE)
        acc_ref[...] += _dot(s, wd_ref[...])

        @pl.when(k == FF_BLOCKS - 1)
        def _():
            h2_ref[...] = h_ref[...] + acc_ref[...]

    return pl.pallas_call(
        body, name="mlp_fwd", grid=(t // tm, FF_BLOCKS),
        in_specs=[pl.BlockSpec((tm, D_MODEL), lambda i, k: (i, 0)), _full((1, D_MODEL)),
                  pl.BlockSpec((None, D_MODEL, FF_BLOCK), lambda i, k: (k, 0, 0)),
                  pl.BlockSpec((FF_BLOCK, D_MODEL), lambda i, k: (k, 0))],
        out_specs=[pl.BlockSpec((tm, D_MODEL), lambda i, k: (i, 0)), pl.BlockSpec((tm, FF_BLOCK), lambda i, k: (i, k)),
                   pl.BlockSpec((tm, D_MODEL), lambda i, k: (i, 0))],
        out_shape=[jax.ShapeDtypeStruct((t, D_MODEL), MXU_DTYPE), jax.ShapeDtypeStruct((t, D_FF), MXU_DTYPE),
                   jax.ShapeDtypeStruct((t, D_MODEL), F32)],
        scratch_shapes=[pltpu.VMEM((tm, D_MODEL), F32)],
        compiler_params=_params("parallel", "arbitrary"),
    )(h1, g_mlp, w_up4, w_down)


def _ple_loss(h2, p, tgt, g_ple, g_fin, w_gate, w_proj, tm):
    t = h2.shape[0]

    def body(h_ref, p_ref, t_ref, gp_ref, gf_ref, wg_ref, wp_ref,
             dh_ref, dhb_ref, dgp_ref, dpp_ref, n3_ref, pb_ref, acc_ref):
        @pl.when(pl.program_id(0) == 0)
        def _():
            acc_ref[...] = jnp.zeros_like(acc_ref)

        h = h_ref[...]
        g_ple_v, g_fin_v = gp_ref[...], gf_ref[...]
        n3 = _rms_fwd(h, g_ple_v).astype(MXU_DTYPE)
        n3_ref[...] = n3
        gate = _sigmoid(_dot(n3, wg_ref[...]))
        pb = p_ref[...].astype(MXU_DTYPE)
        pb_ref[...] = pb
        pp = _dot(pb, wp_ref[...])
        h3 = h + gate * pp
        r4 = lax.rsqrt(jnp.mean(h3 * h3, axis=-1, keepdims=True) + EPS)
        xh4 = h3 * r4
        e = xh4 * g_fin_v - t_ref[...]
        loss = 0.5 * jnp.sum(jnp.mean(e * e, axis=-1, keepdims=True), axis=0, keepdims=True)
        dy = e * (1.0 / D_MODEL)
        dg_fin = jnp.sum(dy * xh4, axis=0, keepdims=True)
        dxh = dy * g_fin_v
        dh3 = r4 * (dxh - xh4 * jnp.mean(dxh * xh4, axis=-1, keepdims=True))
        dpp_ref[...] = (dh3 * gate).astype(MXU_DTYPE)
        dgp = (dh3 * pp * gate * (1.0 - gate)).astype(MXU_DTYPE)
        dgp_ref[...] = dgp
        dn3 = _dot(dgp, wg_ref[...], NT)
        dx, dg_ple = _rms_bwd(h, g_ple_v, dn3)
        dh2 = dh3 + dx
        dh_ref[...] = dh2
        dhb_ref[...] = dh2.astype(MXU_DTYPE)
        acc_ref[0:1, :] += dg_fin
        acc_ref[1:2, :] += dg_ple
        acc_ref[2:3, :] += jnp.broadcast_to(loss, (1, D_MODEL))

    row = lambda w: pl.BlockSpec((tm, w), lambda i: (i, 0))
    return pl.pallas_call(
        body, name="ple_loss", grid=(t // tm,),
        in_specs=[row(D_MODEL), row(PLE_DIM), row(D_MODEL), _full((1, D_MODEL)), _full((1, D_MODEL)),
                  _full((D_MODEL, D_MODEL)), _full((PLE_DIM, D_MODEL))],
        out_specs=[row(D_MODEL), row(D_MODEL), row(D_MODEL), row(D_MODEL), row(D_MODEL), row(PLE_DIM),
                   _full((8, D_MODEL))],
        out_shape=[jax.ShapeDtypeStruct((t, D_MODEL), F32), jax.ShapeDtypeStruct((t, D_MODEL), MXU_DTYPE),
                   jax.ShapeDtypeStruct((t, D_MODEL), MXU_DTYPE), jax.ShapeDtypeStruct((t, D_MODEL), MXU_DTYPE),
                   jax.ShapeDtypeStruct((t, D_MODEL), MXU_DTYPE), jax.ShapeDtypeStruct((t, PLE_DIM), MXU_DTYPE),
                   jax.ShapeDtypeStruct((8, D_MODEL), F32)],
        compiler_params=_params("arbitrary"),
    )(h2, p, tgt, g_ple, g_fin, w_gate, w_proj)


def _mlp_bwd(dh2, dh2b, r, h1, g_mlp, w_up4, w_down, wo_a, wo_d, tm):
    t = h1.shape[0]
    half = wo_a.shape[0]

    def body(dh_ref, dhb_ref, r_ref, h_ref, g_ref, wu_ref, wd_ref, woa_ref, wod_ref,
             da_ref, dh1_ref, dh1b_ref, dao_ref, ddn_ref, acc_ref, dm_ref):
        i, k = pl.program_id(0), pl.program_id(1)

        @pl.when((i == 0) & (k == 0))
        def _():
            acc_ref[...] = jnp.zeros_like(acc_ref)

        @pl.when(k == 0)
        def _():
            dm_ref[...] = jnp.zeros_like(dm_ref)

        ds = _dot(dhb_ref[...], wd_ref[...], NT)
        da = (ds * (2.0 * r_ref[...].astype(F32))).astype(MXU_DTYPE)
        da_ref[...] = da
        dm_ref[...] += _dot(da, wu_ref[...], NT)

        @pl.when(k == FF_BLOCKS - 1)
        def _():
            dx, dg = _rms_bwd(h_ref[...], g_ref[...], dm_ref[...])
            dh1 = dh_ref[...] + dx
            dh1_ref[...] = dh1
            dh1b = dh1.astype(MXU_DTYPE)
            dh1b_ref[...] = dh1b
            dao_ref[...] = _dot(dh1b, woa_ref[...], NT)
            ddn_ref[...] = _dot(dh1b, wod_ref[...], NT)
            acc_ref[0:1, :] += dg

    tok = lambda w: pl.BlockSpec((tm, w), lambda i, k: (i, 0))
    return pl.pallas_call(
        body, name="mlp_bwd", grid=(t // tm, FF_BLOCKS),
        in_specs=[tok(D_MODEL), tok(D_MODEL), pl.BlockSpec((tm, FF_BLOCK), lambda i, k: (i, k)), tok(D_MODEL),
                  _full((1, D_MODEL)), pl.BlockSpec((None, D_MODEL, FF_BLOCK), lambda i, k: (k, 0, 0)),
                  pl.BlockSpec((FF_BLOCK, D_MODEL), lambda i, k: (k, 0)),
                  pl.BlockSpec((half, D_MODEL), lambda i, k: (0, 0)), pl.BlockSpec((half, D_MODEL), lambda i, k: (0, 0))],
        out_specs=[pl.BlockSpec((tm, FF_BLOCK), lambda i, k: (i, k)),
                   tok(D_MODEL), tok(D_MODEL), tok(half), tok(half), pl.BlockSpec((8, D_MODEL), lambda i, k: (0, 0))],
        out_shape=[jax.ShapeDtypeStruct((t, D_FF), MXU_DTYPE),
                   jax.ShapeDtypeStruct((t, D_MODEL), F32), jax.ShapeDtypeStruct((t, D_MODEL), MXU_DTYPE),
                   jax.ShapeDtypeStruct((t, half), F32), jax.ShapeDtypeStruct((t, half), F32),
                   jax.ShapeDtypeStruct((8, D_MODEL), F32)],
        scratch_shapes=[pltpu.VMEM((tm, D_MODEL), F32)],
        compiler_params=_params("arbitrary", "arbitrary"),
    )(dh2, dh2b, r, h1, g_mlp, w_up4, w_down, wo_a, wo_d)


def _inproj_bwd(x, dh1, g_mix, grads, weights, tm):
    t = x.shape[0]
    n = len(grads)

    def body(*refs):
        x_ref, dh_ref, g_ref = refs[:3]
        g_refs, w_refs = refs[3:3 + n], refs[3 + n:3 + 2 * n]
        dx_ref, acc_ref = refs[3 + 2 * n:]

        @pl.when(pl.program_id(0) == 0)
        def _():
            acc_ref[...] = jnp.zeros_like(acc_ref)

        du = _dot(g_refs[0][...], w_refs[0][...])
        for j in range(1, n):
            du += _dot(g_refs[j][...], w_refs[j][...])
        dx, dg = _rms_bwd(x_ref[...], g_ref[...], du)
        dx_ref[...] = dh_ref[...] + dx
        acc_ref[0:1, :] += dg

    tok = lambda w: pl.BlockSpec((tm, w), lambda i: (i, 0))
    return pl.pallas_call(
        body, name="inproj_bwd", grid=(t // tm,),
        in_specs=[tok(D_MODEL), tok(D_MODEL), _full((1, D_MODEL))] + [tok(g.shape[1]) for g in grads]
                 + [_full(w.shape) for w in weights],
        out_specs=[tok(D_MODEL), _full((8, D_MODEL))],
        out_shape=[jax.ShapeDtypeStruct((t, D_MODEL), F32), jax.ShapeDtypeStruct((8, D_MODEL), F32)],
        compiler_params=_params("arbitrary"),
    )(x, dh1, g_mix, *grads, *weights)


def _wgrad(a, b, name, tk, tn, tt, stacked=False, prep=None):
    t, kdim = a.shape
    ncols = b.shape[1]

    def body(a_ref, b_ref, o_ref):
        @pl.when(pl.program_id(2) == 0)
        def _():
            o_ref[...] = jnp.zeros_like(o_ref)

        av = a_ref[...] if prep is None else prep(a_ref[...])
        o_ref[...] += _dot(av, b_ref[...], TN)

    if stacked:
        out_spec = pl.BlockSpec((None, tk, tn), lambda i, j, s: (j, i, 0))
        out_shape = jax.ShapeDtypeStruct((ncols // tn, kdim, tn), F32)
    else:
        out_spec = pl.BlockSpec((tk, tn), lambda i, j, s: (i, j))
        out_shape = jax.ShapeDtypeStruct((kdim, ncols), F32)
    return pl.pallas_call(
        body, name=name, grid=(kdim // tk, ncols // tn, t // tt),
        in_specs=[pl.BlockSpec((tt, tk), lambda i, j, s: (s, i)), pl.BlockSpec((tt, tn), lambda i, j, s: (s, j))],
        out_specs=out_spec, out_shape=out_shape,
        compiler_params=_params("parallel", "parallel", "arbitrary"),
    )(a, b)


def _wgrad_cat(as_, bs, name, tt):
    t = as_[0].shape[0]
    heights = [a.shape[1] for a in as_]
    widths = [b.shape[1] for b in bs]

    def body(*refs):
        a_refs, b_refs, o_ref = refs[:len(as_)], refs[len(as_):-1], refs[-1]

        @pl.when(pl.program_id(0) == 0)
        def _():
            o_ref[...] = jnp.zeros_like(o_ref)

        row = 0
        for a_ref, k in zip(a_refs, heights):
            av = a_ref[...]
            col = 0
            for b_ref, n in zip(b_refs, widths):
                o_ref[row:row + k, col:col + n] += _dot(av, b_ref[...], TN)
                col += n
            row += k

    tok = lambda w: pl.BlockSpec((tt, w), lambda s: (s, 0))
    shape = (sum(heights), sum(widths))
    return pl.pallas_call(
        body, name=name, grid=(t // tt,),
        in_specs=[tok(k) for k in heights] + [tok(n) for n in widths],
        out_specs=_full(shape), out_shape=jax.ShapeDtypeStruct(shape, F32),
        compiler_params=_params("arbitrary"),
    )(*as_, *bs)


def _rope_tables(t):
    half = ATTN_HEAD_DIM // 2
    inv = 1.0 / (ROPE_THETA ** (jnp.arange(half, dtype=F32) * (2.0 / ATTN_HEAD_DIM)))
    ang = jnp.arange(t, dtype=F32)[:, None] * inv[None, :]
    cos, sin = jnp.cos(ang), jnp.sin(ang)
    cos2 = jnp.concatenate([cos, cos], axis=-1)
    sin2 = jnp.concatenate([-sin, sin], axis=-1)
    return jnp.tile(cos2, (1, 2)), jnp.tile(sin2, (1, 2))


def _swap_halves(tv):
    w = tv.shape[-1]
    lane = lax.broadcasted_iota(jnp.int32, tv.shape, tv.ndim - 1)
    first = (lane % ATTN_HEAD_DIM) < (ATTN_HEAD_DIM // 2)
    return jnp.where(first, pltpu.roll(tv, w - ATTN_HEAD_DIM // 2, tv.ndim - 1),
                     pltpu.roll(tv, ATTN_HEAD_DIM // 2, tv.ndim - 1))


def _rope(tv, cos, sin):
    return tv * cos + _swap_halves(tv) * sin


def _rope_bwd(dv, cos, sin):
    return dv * cos + _swap_halves(dv * sin)


def _attn_valid(first_block):
    c = lax.broadcasted_iota(jnp.int32, (2 * ATTN_BLOCK, ATTN_BLOCK), 0)
    r = lax.broadcasted_iota(jnp.int32, (2 * ATTN_BLOCK, ATTN_BLOCK), 1)
    return (c > r) & (c <= r + ATTN_BLOCK) & ((c >= ATTN_BLOCK) | jnp.logical_not(first_block))


def _attn_probs(st, sink, valid):
    s = jnp.where(valid, st * ATTN_SCALE, -jnp.inf)
    m = jnp.maximum(jnp.max(s, axis=0, keepdims=True), sink)
    e = jnp.where(valid, jnp.exp(s - m), 0.0)
    es = jnp.exp(sink - m)
    inv = 1.0 / (jnp.sum(e, axis=0, keepdims=True) + es)
    return e * inv, es * inv


def _lane_scalar(vec, idx):
    lane = lax.broadcasted_iota(jnp.int32, vec.shape, 1)
    return jnp.sum(jnp.where(lane == idx, vec, 0.0), axis=-1, keepdims=True)


def _attn_specs(nb):
    cur = lambda w, cb: pl.BlockSpec((ATTN_BLOCK, w), lambda i: (jnp.minimum(i, nb - 1), cb))
    prev = lambda w, cb: pl.BlockSpec((ATTN_BLOCK, w), lambda i: (jnp.maximum(jnp.minimum(i, nb - 1) - 1, 0), cb))
    kcol, vcol = ATTN_Q // ATTN_KV, ATTN_Q // ATTN_KV + 1
    return [cur(ATTN_Q, 0), cur(ATTN_KV, kcol), prev(ATTN_KV, kcol), cur(ATTN_KV, vcol), prev(ATTN_KV, vcol),
            cur(ATTN_KV, 0), cur(ATTN_KV, 0), prev(ATTN_KV, 0), prev(ATTN_KV, 0), _full((1, 128))]


def _attn_fwd(pa, cos, sin, sinks_vec):
    t = pa.shape[0]
    nb = t // ATTN_BLOCK

    def body(q_ref, kc_ref, kp_ref, vc_ref, vp_ref, cc_ref, sc_ref, cp_ref, sp_ref, sk_ref, o_ref):
        first = pl.program_id(0) == 0
        cc, sc = cc_ref[...], sc_ref[...]
        q = _rope(q_ref[...], jnp.tile(cc, (1, ATTN_Q // ATTN_KV)), jnp.tile(sc, (1, ATTN_Q // ATTN_KV)))
        kc = _rope(kc_ref[...], cc, sc)
        kp = _rope(kp_ref[...], cp_ref[...], sp_ref[...])
        vc, vp = vc_ref[...], vp_ref[...]
        sk = sk_ref[...]
        valid = _attn_valid(first)
        kv = lambda tp, tc, hk: jnp.concatenate([tp[:, hk * ATTN_HEAD_DIM:(hk + 1) * ATTN_HEAD_DIM],
                                                 tc[:, hk * ATTN_HEAD_DIM:(hk + 1) * ATTN_HEAD_DIM]], axis=0)
        kwins = [kv(kp, kc, hk) for hk in range(ATTN_KV_HEADS)]
        vwins_t = [kv(vp, vc, hk).T for hk in range(ATTN_KV_HEADS)]
        heads = [slice(h * ATTN_HEAD_DIM, (h + 1) * ATTN_HEAD_DIM) for h in range(ATTN_HEADS)]
        scores = [_dot(kwins[h // ATTN_GROUPS], q[:, hs], NT) for h, hs in enumerate(heads)]
        probs = [_attn_probs(st, _lane_scalar(sk, h), valid)[0] for h, st in enumerate(scores)]
        for h, (hs, pt) in enumerate(zip(heads, probs)):
            o_ref[:, hs] = _dot(vwins_t[h // ATTN_GROUPS], pt).T.astype(o_ref.dtype)

    return pl.pallas_call(
        body, name="attn_fwd", grid=(nb,),
        in_specs=_attn_specs(nb),
        out_specs=pl.BlockSpec((ATTN_BLOCK, ATTN_Q), lambda i: (i, 0)),
        out_shape=jax.ShapeDtypeStruct((t, ATTN_Q), MXU_DTYPE),
        compiler_params=_params("parallel"),
    )(pa, pa, pa, pa, pa, cos, sin, cos, sin, sinks_vec)


def _attn_bwd(pa, cos, sin, sinks_vec, dao):
    t = pa.shape[0]
    nb = t // ATTN_BLOCK

    def body(q_ref, kc_ref, kp_ref, vc_ref, vp_ref, cc_ref, sc_ref, cp_ref, sp_ref, sk_ref, do_ref,
             dq_ref, dk_ref, dv_ref, acc_ref, dqr_ref, dkw_ref, dvw_ref, ck_ref, cv_ref):
        i = pl.program_id(0)

        @pl.when(i == 0)
        def _():
            acc_ref[...] = jnp.zeros_like(acc_ref)
            ck_ref[...] = jnp.zeros_like(ck_ref)
            cv_ref[...] = jnp.zeros_like(cv_ref)

        @pl.when(i < nb)
        def _():
            first = i == 0
            cc, sc = cc_ref[...], sc_ref[...]
            cq, sq = jnp.tile(cc, (1, ATTN_Q // ATTN_KV)), jnp.tile(sc, (1, ATTN_Q // ATTN_KV))
            q = _rope(q_ref[...], cq, sq)
            kc = _rope(kc_ref[...], cc, sc)
            kp = _rope(kp_ref[...], cp_ref[...], sp_ref[...])
            vc, vp = vc_ref[...], vp_ref[...]
            sk = sk_ref[...]
            do = do_ref[...]
            lane = lax.broadcasted_iota(jnp.int32, (1, 128), 1)
            dsink = jnp.zeros((1, 128), F32)
            valid = _attn_valid(first)
            kv = lambda tp, tc, hk: jnp.concatenate([tp[:, hk * ATTN_HEAD_DIM:(hk + 1) * ATTN_HEAD_DIM],
                                                     tc[:, hk * ATTN_HEAD_DIM:(hk + 1) * ATTN_HEAD_DIM]], axis=0)
            kwins = [kv(kp, kc, hk) for hk in range(ATTN_KV_HEADS)]
            vwins = [kv(vp, vc, hk) for hk in range(ATTN_KV_HEADS)]
            kwins_t = [kw.T for kw in kwins]
            heads = [slice(h * ATTN_HEAD_DIM, (h + 1) * ATTN_HEAD_DIM) for h in range(ATTN_HEADS)]
            scores = [_dot(kwins[h // ATTN_GROUPS], q[:, hs], NT) for h, hs in enumerate(heads)]
            dps = [_dot(vwins[h // ATTN_GROUPS], do[:, hs], NT) for h, hs in enumerate(heads)]
            pts, dsts = [], []
            for h, (st, dp_t) in enumerate(zip(scores, dps)):
                probs_t, psink = _attn_probs(st, _lane_scalar(sk, h), valid)
                delta = jnp.sum(probs_t * dp_t, axis=0, keepdims=True)
                pts.append(probs_t)
                dsts.append(probs_t * (dp_t - delta) * ATTN_SCALE)
                dsink += jnp.where(lane == h, jnp.sum(-psink * delta, axis=1, keepdims=True), 0.0)
            for h, (hs, ds_t) in enumerate(zip(heads, dsts)):
                dqr_ref[:, hs] = _dot(kwins_t[h // ATTN_GROUPS], ds_t).T
            for hk in range(ATTN_KV_HEADS):
                ks = slice(hk * ATTN_HEAD_DIM, (hk + 1) * ATTN_HEAD_DIM)
                group = range(hk * ATTN_GROUPS, (hk + 1) * ATTN_GROUPS)
                ds_g = jnp.concatenate([dsts[h] for h in group], axis=1)
                p_g = jnp.concatenate([pts[h] for h in group], axis=1)
                q_g = jnp.concatenate([q[:, heads[h]] for h in group], axis=0)
                do_g = jnp.concatenate([do[:, heads[h]] for h in group], axis=0)
                dkw_ref[:, ks] = _dot(ds_g, q_g)
                dvw_ref[:, ks] = _dot(p_g, do_g)
            acc_ref[0:1, :] += dsink
            dq_ref[...] = _rope_bwd(dqr_ref[...], cq, sq).astype(dq_ref.dtype)
            dk_ref[...] = (ck_ref[...] + _rope_bwd(dkw_ref[0:ATTN_BLOCK, :], cp_ref[...], sp_ref[...])).astype(dk_ref.dtype)
            dv_ref[...] = (cv_ref[...] + dvw_ref[0:ATTN_BLOCK, :]).astype(dv_ref.dtype)
            ck_ref[...] = _rope_bwd(dkw_ref[ATTN_BLOCK:2 * ATTN_BLOCK, :], cc, sc)
            cv_ref[...] = dvw_ref[ATTN_BLOCK:2 * ATTN_BLOCK, :]

        @pl.when(i == nb)
        def _():
            dk_ref[...] = ck_ref[...].astype(dk_ref.dtype)
            dv_ref[...] = cv_ref[...].astype(dv_ref.dtype)

    prev_out = lambda w: pl.BlockSpec((ATTN_BLOCK, w), lambda i: (jnp.maximum(i - 1, 0), 0))
    return pl.pallas_call(
        body, name="attn_bwd", grid=(nb + 1,),
        in_specs=_attn_specs(nb) + [pl.BlockSpec((ATTN_BLOCK, ATTN_Q), lambda i: (jnp.minimum(i, nb - 1), 0))],
        out_specs=[pl.BlockSpec((ATTN_BLOCK, ATTN_Q), lambda i: (jnp.minimum(i, nb - 1), 0)), prev_out(ATTN_KV),
                   prev_out(ATTN_KV), _full((8, 128))],
        out_shape=[jax.ShapeDtypeStruct((t, ATTN_Q), MXU_DTYPE), jax.ShapeDtypeStruct((t, ATTN_KV), MXU_DTYPE),
                   jax.ShapeDtypeStruct((t, ATTN_KV), MXU_DTYPE), jax.ShapeDtypeStruct((8, 128), F32)],
        scratch_shapes=[pltpu.VMEM((ATTN_BLOCK, ATTN_Q), F32), pltpu.VMEM((2 * ATTN_BLOCK, ATTN_KV), F32),
                        pltpu.VMEM((2 * ATTN_BLOCK, ATTN_KV), F32), pltpu.VMEM((ATTN_BLOCK, ATTN_KV), F32),
                        pltpu.VMEM((ATTN_BLOCK, ATTN_KV), F32)],
        compiler_params=_params("arbitrary"),
    )(pa, pa, pa, pa, pa, cos, sin, cos, sin, sinks_vec, dao)


PAIR = 2 * DN_CHUNK
INTRA_PAIRS = 4
SCAN_PAIRS = 4
HALO = 8


def _conv_window(cur_ref, prev_ref, xs_ref, tm, has_prev):
    prev = jnp.where(has_prev, prev_ref[...], 0.0)
    xs_ref[0:HALO, :] = prev
    xs_ref[HALO:HALO + tm, :] = cur_ref[...]


def _conv_taps(xs_ref, cw_ref, tm):
    y = cw_ref[0:1, :] * xs_ref[pl.ds(HALO - DN_CONV + 1, tm), :]
    for j in range(1, DN_CONV):
        y += cw_ref[j:j + 1, :] * xs_ref[pl.ds(HALO - DN_CONV + 1 + j, tm), :]
    return y


def _gate_values(ba, al, dt):
    beta = _sigmoid(ba)
    pre = ba + dt
    g = -jnp.exp(al) * _softplus(pre)
    return beta, g, pre


def _dn_prep_specs(tm, tile):
    return [pl.BlockSpec((tm, CONV_CH), lambda i: (tile(i), 0)),
            pl.BlockSpec((HALO, CONV_CH), lambda i: (jnp.maximum(tile(i) * (tm // HALO) - 1, 0), 0)),
            pl.BlockSpec((tm, 128), lambda i: (tile(i), 4 * DN_W // 128)),
            _full((DN_CONV, CONV_CH)), _full((1, 128)), _full((1, 128))]


def _dn_prep(pd, conv_w, al_vec, dt_vec, tm):
    t = pd.shape[0]

    def body(cur_ref, prev_ref, ba_ref, cw_ref, al_ref, dt_ref, qn_ref, kn_ref, vc_ref, gc_ref, gr_ref, xs_ref):
        _conv_window(cur_ref, prev_ref, xs_ref, tm, pl.program_id(0) > 0)
        y = _conv_taps(xs_ref, cw_ref, tm)
        c = y * _sigmoid(y)
        for h in range(DN_HEADS):
            qs = slice(h * DN_HEAD_DIM, (h + 1) * DN_HEAD_DIM)
            ksl = slice(DN_W + h * DN_HEAD_DIM, DN_W + (h + 1) * DN_HEAD_DIM)
            qh, kh = c[:, qs], c[:, ksl]
            qn_ref[:, qs] = qh * lax.rsqrt(jnp.sum(qh * qh, axis=-1, keepdims=True) + EPS) * DN_SCALE
            kn_ref[:, qs] = kh * lax.rsqrt(jnp.sum(kh * kh, axis=-1, keepdims=True) + EPS)
        vc_ref[...] = c[:, 2 * DN_W:3 * DN_W]
        beta, g, _ = _gate_values(ba_ref[...], al_ref[...], dt_ref[...])
        lane = lax.broadcasted_iota(jnp.int32, beta.shape, 1)
        gb = jnp.where(lane < DN_HEADS, beta, jnp.where(lane < 2 * DN_HEADS, g, 0.0))
        gc_ref[...] = gb
        gr_ref[...] = gb.T[0:8, :]

    tok = lambda w: pl.BlockSpec((tm, w), lambda i: (i, 0))
    return pl.pallas_call(
        body, name="dn_prep", grid=(t // tm,),
        in_specs=_dn_prep_specs(tm, lambda i: i),
        out_specs=[tok(DN_W), tok(DN_W), tok(DN_W), tok(128), pl.BlockSpec((8, tm), lambda i: (0, i))],
        out_shape=[jax.ShapeDtypeStruct((t, DN_W), F32)] * 3 + [jax.ShapeDtypeStruct((t, 128), F32),
                                                                 jax.ShapeDtypeStruct((8, t), F32)],
        scratch_shapes=[pltpu.VMEM((HALO + tm, CONV_CH), F32)],
        compiler_params=_params("parallel"),
    )(pd, pd, pd, conv_w, al_vec, dt_vec)


def _pair_masks():
    r = lax.broadcasted_iota(jnp.int32, (PAIR, PAIR), 0)
    c = lax.broadcasted_iota(jnp.int32, (PAIR, PAIR), 1)
    same = (r < DN_CHUNK) == (c < DN_CHUNK)
    return same & (r >= c), same & (r > c)


def _lane_col(mat, idx):
    lane = lax.broadcasted_iota(jnp.int32, mat.shape, 1)
    return jnp.sum(jnp.where(lane == idx, mat, 0.0), axis=-1, keepdims=True)


def _pair_cumsums(gc, gr, low):
    lowf = low.astype(F32)
    return _dot(lowf, gc, NN, HI), _dot(gr, lowf, NT, HI)


def _pair_gates(gc, cum_c, cum_r, low, h):
    beta = _lane_col(gc, h)
    gam = _lane_col(cum_c, DN_HEADS + h)
    gam_row = cum_r[DN_HEADS + h:DN_HEADS + h + 1, :]
    dm = jnp.where(low, jnp.exp(jnp.where(low, gam - gam_row, 0.0)), 0.0)
    row = lax.broadcasted_iota(jnp.int32, gam.shape, 0)
    gl = jnp.where(row < DN_CHUNK, gam[DN_CHUNK - 1:DN_CHUNK, :], gam[PAIR - 1:PAIR, :])
    return beta, gam, dm, gl


def _split(a):
    hi = a.astype(BF16)
    return hi, (a - hi.astype(F32)).astype(BF16)


def _dot_split(a, b, dims=NN):
    (ah, al), (bh, bl) = a, b
    la, lb = (1, 1) if dims == TN else ((0, 1) if dims == NN else (0, 0))
    r = _dot(jnp.concatenate([ah, al], axis=la), jnp.concatenate([bh, bl], axis=lb), dims)
    m, n = r.shape[0] // 2, r.shape[1] // 2
    return (r[m:, n:] + (r[:m, n:] + r[m:, :n])) + r[:m, :n]


def _unit_lower_inverses(lmats):
    n = lmats[0].shape[0]
    r = lax.broadcasted_iota(jnp.int32, (n, n), 0)
    c = lax.broadcasted_iota(jnp.int32, (n, n), 1)
    same = lambda size: (r & ~(size - 1)) == (c & ~(size - 1))
    base = DN_CHUNK // 4
    diag = [jnp.where(same(base), l, 0.0) for l in lmats]
    accs = [(r == c).astype(F32) - d for d in diag]
    splits = [_split(d) for d in diag]
    step = 1
    while 2 * step < base:
        splits = [_split(_dot_split(s, s)) for s in splits]
        accs = [acc + _dot_split(_split(acc), s) for acc, s in zip(accs, splits)]
        step *= 2
    size = base
    while size < DN_CHUNK:
        below = same(2 * size) & jnp.logical_not(same(size))
        tb = [_dot(acc, jnp.where(below, l, 0.0)) for acc, l in zip(accs, lmats)]
        accs = [acc - _dot(t, acc) for acc, t in zip(accs, tb)]
        size *= 2
    return accs


def _dn_intra(qn, kn, vc, gc, gr):
    t = qn.shape[0]
    npair = t // PAIR
    rows_step = INTRA_PAIRS * PAIR

    def body(q_ref, k_ref, v_ref, gc_ref, gr_ref, u_ref, w_ref, qg_ref, kd_ref, a_ref, ti_ref, dl_ref):
        low, strict = _pair_masks()
        items = []
        for p in range(INTRA_PAIRS):
            rows = slice(p * PAIR, (p + 1) * PAIR)
            gc_v = gc_ref[rows, :]
            cum_c, cum_r = _pair_cumsums(gc_v, gr_ref[:, rows], low)
            for h in range(DN_HEADS):
                hs = slice(h * DN_HEAD_DIM, (h + 1) * DN_HEAD_DIM)
                items.append((p, h, rows, hs, _pair_gates(gc_v, cum_c, cum_r, low, h)))
        lmats = []
        for p, h, rows, hs, (beta, gam, dm, gl) in items:
            k = k_ref[rows, hs]
            lmats.append(jnp.where(strict, _dot(k * beta, k, NT) * dm, 0.0))
        tinvs = _unit_lower_inverses(lmats)
        for (p, h, rows, hs, (beta, gam, dm, gl)), tinv in zip(items, tinvs):
            q, k, v = q_ref[rows, hs], k_ref[rows, hs], v_ref[rows, hs]
            eg = jnp.exp(gam)
            u_ref[rows, hs] = _dot(tinv, v * beta)
            w_ref[rows, hs] = _dot(tinv, (k * beta) * eg).astype(w_ref.dtype)
            a_ref[h, rows, :] = _dot(q, k, NT) * dm
            ti_ref[h, rows, :] = tinv
            qg_ref[rows, hs] = (q * eg).astype(qg_ref.dtype)
            kd_ref[rows, hs] = (k * jnp.exp(gl - gam)).astype(kd_ref.dtype)
            for c in range(2):
                last = (c + 1) * DN_CHUNK - 1
                dl_ref[2 * p + c, h] = jnp.broadcast_to(jnp.exp(gam[last:last + 1, :]), (8, 128))

    tok = lambda w: pl.BlockSpec((rows_step, w), lambda n: (n, 0))
    hm = pl.BlockSpec((DN_HEADS, rows_step, PAIR), lambda n: (0, n, 0))
    return pl.pallas_call(
        body, name="dn_intra", grid=(npair // INTRA_PAIRS,),
        in_specs=[tok(DN_W), tok(DN_W), tok(DN_W), tok(128), pl.BlockSpec((8, rows_step), lambda n: (0, n))],
        out_specs=[tok(DN_W)] * 4 + [hm, hm, pl.BlockSpec((2 * INTRA_PAIRS, DN_HEADS, 8, 128), lambda n: (n, 0, 0, 0))],
        out_shape=[jax.ShapeDtypeStruct((t, DN_W), F32)] + [jax.ShapeDtypeStruct((t, DN_W), MXU_DTYPE)] * 3
                  + [jax.ShapeDtypeStruct((DN_HEADS, t, PAIR), F32)] * 2
                  + [jax.ShapeDtypeStruct((2 * npair, DN_HEADS, 8, 128), F32)],
        compiler_params=_params("parallel"),
    )(qn, kn, vc, gc, gr)


def _dn_scan_fwd(u, w, qg, kd, a_qk, dlast, pd, dn_w):
    t = u.shape[0]
    npair = t // PAIR

    def body(u_ref, w_ref, qg_ref, kd_ref, a_ref, dl_ref, z_ref, nw_ref, out_ref, o_ref, vn_ref, sall_ref, s_ref):
        @pl.when(pl.program_id(0) == 0)
        def _():
            s_ref[...] = jnp.zeros_like(s_ref)

        nw = nw_ref[...]
        for c in range(2 * SCAN_PAIRS):
            rows = slice(c * DN_CHUNK, (c + 1) * DN_CHUNK)
            diag = slice((c % 2) * DN_CHUNK, (c % 2 + 1) * DN_CHUNK)
            for h in range(DN_HEADS):
                hs = slice(h * DN_HEAD_DIM, (h + 1) * DN_HEAD_DIM)
                st = s_ref[h]
                sall_ref[c, h] = st
                vn_ref[rows, hs] = (u_ref[rows, hs] - _dot(w_ref[rows, hs], st)).astype(vn_ref.dtype)
            for h in range(DN_HEADS):
                hs = slice(h * DN_HEAD_DIM, (h + 1) * DN_HEAD_DIM)
                st, vn = s_ref[h], vn_ref[rows, hs]
                o = _dot(qg_ref[rows, hs], st) + _dot(a_ref[h, rows, diag], vn)
                s_ref[h] = st * dl_ref[c, h][0:1, :] + _dot(kd_ref[rows, hs], vn, TN)
                o_ref[rows, hs] = o
                z = z_ref[rows, hs]
                on = o * lax.rsqrt(jnp.mean(o * o, axis=-1, keepdims=True) + EPS) * nw
                out_ref[rows, hs] = (on * (z * _sigmoid(z))).astype(out_ref.dtype)

    rows_step = SCAN_PAIRS * PAIR
    tok = pl.BlockSpec((rows_step, DN_W), lambda n: (n, 0))
    hm = pl.BlockSpec((DN_HEADS, rows_step, PAIR), lambda n: (0, n, 0))
    return pl.pallas_call(
        body, name="dn_scan_fwd", grid=(npair // SCAN_PAIRS,),
        in_specs=[tok, tok, tok, tok, hm, pl.BlockSpec((2 * SCAN_PAIRS, DN_HEADS, 8, 128), lambda n: (n, 0, 0, 0)),
                  pl.BlockSpec((rows_step, DN_W), lambda n: (n, 3)), _full((1, 128))],
        out_specs=[tok, tok, tok,
                   pl.BlockSpec((2 * SCAN_PAIRS, DN_HEADS, DN_HEAD_DIM, DN_HEAD_DIM), lambda n: (n, 0, 0, 0))],
        out_shape=[jax.ShapeDtypeStruct((t, DN_W), MXU_DTYPE), jax.ShapeDtypeStruct((t, DN_W), F32),
                   jax.ShapeDtypeStruct((t, DN_W), MXU_DTYPE),
                   jax.ShapeDtypeStruct((2 * npair, DN_HEADS, DN_HEAD_DIM, DN_HEAD_DIM), F32)],
        scratch_shapes=[pltpu.VMEM((DN_HEADS, DN_HEAD_DIM, DN_HEAD_DIM), F32)],
        compiler_params=_params("arbitrary"),
    )(u, w, qg, kd, a_qk, dlast, pd, dn_w)


def _dn_scan_bwd(dout, o, vnew, sall, w, qg, kd, a_qk, dlast, pd, dn_w, dep):
    t = o.shape[0]
    npair = t // PAIR
    nstep = npair // SCAN_PAIRS
    rev = lambda n: nstep - 1 - n

    def body(do_ref, o_ref, vn_ref, sall_ref, w_ref, qg_ref, kd_ref, a_ref, dl_ref, z_ref, nw_ref, dep_ref,
             dz_ref, du_ref, dw_ref, dqg_ref, dkd_ref, da_ref, ddl_ref, acc_ref, ds_ref, dos_ref):
        @pl.when(pl.program_id(0) == 0)
        def _():
            ds_ref[...] = jnp.zeros_like(ds_ref)
            acc_ref[...] = jnp.zeros_like(acc_ref)

        nw = nw_ref[...]
        dnw = jnp.zeros((1, 128), F32)
        for h in range(DN_HEADS):
            hs = slice(h * DN_HEAD_DIM, (h + 1) * DN_HEAD_DIM)
            o, z, dout = o_ref[:, hs], z_ref[:, hs], do_ref[:, hs]
            r = lax.rsqrt(jnp.mean(o * o, axis=-1, keepdims=True) + EPS)
            oh = o * r
            sz = _sigmoid(z)
            dz_ref[:, hs] = dout * (oh * nw) * (sz + z * sz * (1.0 - sz))
            don = dout * (z * sz)
            dnw += jnp.sum(don * oh, axis=0, keepdims=True)
            doh = don * nw
            dos_ref[:, hs] = r * (doh - oh * jnp.mean(doh * oh, axis=-1, keepdims=True))
        acc_ref[0:1, :] += dnw
        for c in reversed(range(2 * SCAN_PAIRS)):
            rows = slice(c * DN_CHUNK, (c + 1) * DN_CHUNK)
            diag = slice((c % 2) * DN_CHUNK, (c % 2 + 1) * DN_CHUNK)
            other = slice((1 - c % 2) * DN_CHUNK, (2 - c % 2) * DN_CHUNK)
            for h in range(DN_HEADS):
                hs = slice(h * DN_HEAD_DIM, (h + 1) * DN_HEAD_DIM)
                do, st, dsp, vn = dos_ref[rows, hs], sall_ref[c, h], ds_ref[h], vn_ref[rows, hs]
                da_ref[h, rows, diag] = _dot(do, vn, NT)
                da_ref[h, rows, other] = jnp.zeros((DN_CHUNK, DN_CHUNK), F32)
                du_ref[rows, hs] = (_dot(a_ref[h, rows, diag], do, TN) + _dot(kd_ref[rows, hs], dsp)).astype(du_ref.dtype)
                dqg_ref[rows, hs] = _dot(do, st, NT)
                dkd_ref[rows, hs] = _dot(vn, dsp, NT)
                ddl = jnp.sum(jnp.sum(dsp * st, axis=1, keepdims=True), axis=0, keepdims=True)
                ddl_ref[c, h] = jnp.broadcast_to(ddl, (8, 128))
            for h in range(DN_HEADS):
                hs = slice(h * DN_HEAD_DIM, (h + 1) * DN_HEAD_DIM)
                do, st, dvn = dos_ref[rows, hs], sall_ref[c, h], du_ref[rows, hs]
                dw_ref[rows, hs] = (-_dot(dvn, st, NT)).astype(dw_ref.dtype)
                ds_ref[h] = (ds_ref[h] * dl_ref[c, h][0:1, :] + _dot(qg_ref[rows, hs], do, TN)
                             - _dot(w_ref[rows, hs], dvn, TN))

    rows_step = SCAN_PAIRS * PAIR
    tok = pl.BlockSpec((rows_step, DN_W), lambda n: (rev(n), 0))
    hm = pl.BlockSpec((DN_HEADS, rows_step, PAIR), lambda n: (0, rev(n), 0))
    sc = pl.BlockSpec((2 * SCAN_PAIRS, DN_HEADS, 8, 128), lambda n: (rev(n), 0, 0, 0))
    return pl.pallas_call(
        body, name="dn_scan_bwd", grid=(nstep,),
        in_specs=[tok, tok, tok,
                  pl.BlockSpec((2 * SCAN_PAIRS, DN_HEADS, DN_HEAD_DIM, DN_HEAD_DIM), lambda n: (rev(n), 0, 0, 0)),
                  tok, tok, tok, hm, sc, pl.BlockSpec((rows_step, DN_W), lambda n: (rev(n), 3)), _full((1, 128)),
                  pl.BlockSpec(memory_space=pl.ANY)],
        out_specs=[tok] * 5 + [hm, sc, _full((8, 128))],
        out_shape=[jax.ShapeDtypeStruct((t, DN_W), F32)] + [jax.ShapeDtypeStruct((t, DN_W), MXU_DTYPE)] * 2
                  + [jax.ShapeDtypeStruct((t, DN_W), F32)] * 2 + [jax.ShapeDtypeStruct((DN_HEADS, t, PAIR), F32),
                   jax.ShapeDtypeStruct((2 * npair, DN_HEADS, 8, 128), F32), jax.ShapeDtypeStruct((8, 128), F32)],
        scratch_shapes=[pltpu.VMEM((DN_HEADS, DN_HEAD_DIM, DN_HEAD_DIM), F32), pltpu.VMEM((SCAN_PAIRS * PAIR, DN_W), F32)],
        compiler_params=_params("arbitrary"),
    )(dout, o, vnew, sall, w, qg, kd, a_qk, dlast, pd, dn_w, dep)


def _dn_intra_bwd(qn, kn, vc, gc, gr, tinv, a_qk, du, dw, dqg, dkd, da_qk, ddlast, dlast, dep):
    t = qn.shape[0]
    npair = t // PAIR

    def body(q_ref, k_ref, v_ref, gc_ref, gr_ref, ti_ref, a_ref, du_ref, dw_ref, dqg_ref, dkd_ref, da_ref, ddl_ref, dl_ref,
             dep_ref, dq_ref, dk_ref, dv_ref, dg_ref):
        low, strict = _pair_masks()
        lane = lax.broadcasted_iota(jnp.int32, (PAIR, 128), 1)
        rowi = lax.broadcasted_iota(jnp.int32, (PAIR, 1), 0)
        rsum = lambda v: jnp.sum(v, axis=-1, keepdims=True)
        items = []
        for p in range(INTRA_PAIRS):
            rows = slice(p * PAIR, (p + 1) * PAIR)
            gc_v = gc_ref[rows, :]
            cum_c, cum_r = _pair_cumsums(gc_v, gr_ref[:, rows], low)
            for h in range(DN_HEADS):
                hs = slice(h * DN_HEAD_DIM, (h + 1) * DN_HEAD_DIM)
                items.append((p, h, rows, hs, _pair_gates(gc_v, cum_c, cum_r, low, h)))
        dtis, lmats, dvbs, dkbgs = [], [], [], []
        for p, h, rows, hs, (beta, gam, dm, gl) in items:
            k, tinv = k_ref[rows, hs], ti_ref[h, rows, :]
            kb = k * beta
            dtis.append(_dot(du_ref[rows, hs], v_ref[rows, hs] * beta, NT)
                        + _dot(dw_ref[rows, hs], kb * jnp.exp(gam), NT))
            lmats.append(jnp.where(strict, _dot(kb, k, NT) * dm, 0.0))
            dvbs.append(_dot(tinv, du_ref[rows, hs], TN))
            dkbgs.append(_dot(tinv, dw_ref[rows, hs], TN))
        xs = [_dot(ti_ref[h, rows, :], dti, TN) for (p, h, rows, hs, g), dti in zip(items, dtis)]
        dls = [jnp.where(strict, -_dot(x, ti_ref[h, rows, :], NT), 0.0) for (p, h, rows, hs, g), x in zip(items, xs)]
        dgam_all = [jnp.zeros((PAIR, 128), F32) for _ in range(INTRA_PAIRS)]
        dbeta_all = [jnp.zeros((PAIR, 128), F32) for _ in range(INTRA_PAIRS)]
        for (p, h, rows, hs, (beta, gam, dm, gl)), dl, lmat, dvb, dkbg in zip(items, dls, lmats, dvbs, dkbgs):
            q, k, v = q_ref[rows, hs], k_ref[rows, hs], v_ref[rows, hs]
            a = a_ref[h, rows, :]
            dqg, dkd = dqg_ref[rows, hs], dkd_ref[rows, hs]
            kb = k * beta
            eg = jnp.exp(gam)
            ekd = jnp.exp(gl - gam)
            dmm = dl * dm
            dam = jnp.where(low, da_ref[h, rows, :], 0.0)
            dn = dam * dm
            e = dl * lmat + dam * a
            dkb = _dot(dmm, k) + dkbg * eg
            dk_ref[rows, hs] = _dot(dmm, kb, TN) + _dot(dn, q, TN) + dkd * ekd + dkb * beta
            dq_ref[rows, hs] = _dot(dn, k) + dqg * eg
            dv_ref[rows, hs] = dvb * beta
            t_kd = rsum(dkd * (k * ekd))
            dgam = rsum(e) - rsum(e.T) + rsum(dqg * (q * eg)) + rsum(dkbg * (kb * eg)) - t_kd
            for c in range(2):
                crows = slice(c * DN_CHUNK, (c + 1) * DN_CHUNK)
                dgl = (jnp.sum(t_kd[crows, :], axis=0, keepdims=True)
                       + ddl_ref[2 * p + c, h][0:1, 0:1] * dl_ref[2 * p + c, h][0:1, 0:1])
                dgam = dgam + jnp.where(rowi == (c + 1) * DN_CHUNK - 1, dgl, 0.0)
            dgam_all[p] += jnp.where(lane == DN_HEADS + h, dgam, 0.0)
            dbeta_all[p] += jnp.where(lane == h, rsum(dkb * k) + rsum(dvb * v), 0.0)
        for p in range(INTRA_PAIRS):
            dg_ref[p * PAIR:(p + 1) * PAIR, :] = dbeta_all[p] + _dot(low.astype(F32), dgam_all[p], TN, HI)

    rows_step = INTRA_PAIRS * PAIR
    tok = lambda w: pl.BlockSpec((rows_step, w), lambda n: (n, 0))
    hm = pl.BlockSpec((DN_HEADS, rows_step, PAIR), lambda n: (0, n, 0))
    sc = pl.BlockSpec((2 * INTRA_PAIRS, DN_HEADS, 8, 128), lambda n: (n, 0, 0, 0))
    return pl.pallas_call(
        body, name="dn_intra_bwd", grid=(npair // INTRA_PAIRS,),
        in_specs=[tok(DN_W), tok(DN_W), tok(DN_W), tok(128), pl.BlockSpec((8, rows_step), lambda n: (0, n)), hm, hm,
                  tok(DN_W), tok(DN_W), tok(DN_W), tok(DN_W), hm, sc, sc, pl.BlockSpec(memory_space=pl.ANY)],
        out_specs=[tok(DN_W), tok(DN_W), tok(DN_W), tok(128)],
        out_shape=[jax.ShapeDtypeStruct((t, DN_W), F32)] * 3 + [jax.ShapeDtypeStruct((t, 128), F32)],
        compiler_params=_params("parallel"),
    )(qn, kn, vc, gc, gr, tinv, a_qk, du, dw, dqg, dkd, da_qk, ddlast, dlast, dep)


def _dn_prep_bwd(pd, conv_w, al_vec, dt_vec, dqn, dkn, dvc, dgc, dz, tm):
    t = pd.shape[0]
    nt = t // tm
    tile = lambda i: nt - 1 - i

    def body(cur_ref, prev_ref, ba_ref, cw_ref, al_ref, dt_ref, dq_ref, dk_ref, dv_ref, dg_ref, dz_ref,
             o_ref, accw_ref, accg_ref, xs_ref, dc_ref, ds_ref, carry_ref):
        @pl.when(pl.program_id(0) == 0)
        def _():
            accw_ref[...] = jnp.zeros_like(accw_ref)
            accg_ref[...] = jnp.zeros_like(accg_ref)
            carry_ref[...] = jnp.zeros_like(carry_ref)

        _conv_window(cur_ref, prev_ref, xs_ref, tm, tile(pl.program_id(0)) > 0)
        taps = [xs_ref[pl.ds(HALO - DN_CONV + 1 + j, tm), :] for j in range(DN_CONV)]
        y = cw_ref[0:1, :] * taps[0]
        for j in range(1, DN_CONV):
            y += cw_ref[j:j + 1, :] * taps[j]
        sg = _sigmoid(y)
        c = y * sg
        for h in range(DN_HEADS):
            qs = slice(h * DN_HEAD_DIM, (h + 1) * DN_HEAD_DIM)
            ksl = slice(DN_W + h * DN_HEAD_DIM, DN_W + (h + 1) * DN_HEAD_DIM)
            for src, sl, scale in ((dq_ref, qs, DN_SCALE), (dk_ref, ksl, 1.0)):
                xh = c[:, sl]
                r = lax.rsqrt(jnp.sum(xh * xh, axis=-1, keepdims=True) + EPS)
                unit = xh * r
                dn = src[:, qs] * scale
                dc_ref[:, sl] = r * (dn - unit * jnp.sum(dn * unit, axis=-1, keepdims=True))
        dc_ref[:, 2 * DN_W:3 * DN_W] = dv_ref[...]
        dy = dc_ref[...] * (sg + y * sg * (1.0 - sg))
        for j in range(DN_CONV):
            accw_ref[j:j + 1, :] += jnp.sum(dy * taps[j], axis=0, keepdims=True)
        ds_ref[0:tm, :] = dy
        ds_ref[tm:tm + HALO, :] = carry_ref[...]
        carry_ref[...] = ds_ref[0:HALO, :]
        dx = cw_ref[0:1, :] * ds_ref[pl.ds(DN_CONV - 1, tm), :]
        for j in range(1, DN_CONV):
            dx += cw_ref[j:j + 1, :] * ds_ref[pl.ds(DN_CONV - 1 - j, tm), :]

        beta, g, pre = _gate_values(ba_ref[...], al_ref[...], dt_ref[...])
        dgb = dg_ref[...]
        lane = lax.broadcasted_iota(jnp.int32, dgb.shape, 1)
        is_b, is_a = lane < DN_HEADS, (lane >= DN_HEADS) & (lane < 2 * DN_HEADS)
        dpre = dgb * (-jnp.exp(al_ref[...])) * _sigmoid(pre)
        dba = jnp.where(is_b, dgb * beta * (1.0 - beta), jnp.where(is_a, dpre, 0.0))
        accg_ref[0:1, :] += jnp.sum(jnp.where(is_a, dgb * g, 0.0), axis=0, keepdims=True)
        accg_ref[1:2, :] += jnp.sum(jnp.where(is_a, dpre, 0.0), axis=0, keepdims=True)
        o_ref[:, 0:CONV_CH] = dx.astype(o_ref.dtype)
        o_ref[:, CONV_CH:CONV_CH + DN_W] = dz_ref[...].astype(o_ref.dtype)
        o_ref[:, CONV_CH + DN_W:DN_COLS] = dba.astype(o_ref.dtype)

    tok = lambda w: pl.BlockSpec((tm, w), lambda i: (tile(i), 0))
    return pl.pallas_call(
        body, name="dn_prep_bwd", grid=(nt,),
        in_specs=_dn_prep_specs(tm, tile) + [tok(DN_W), tok(DN_W), tok(DN_W), tok(128), tok(DN_W)],
        out_specs=[tok(DN_COLS), _full((8, CONV_CH)), _full((8, 128))],
        out_shape=[jax.ShapeDtypeStruct((t, DN_COLS), MXU_DTYPE),
                   jax.ShapeDtypeStruct((8, CONV_CH), F32), jax.ShapeDtypeStruct((8, 128), F32)],
        scratch_shapes=[pltpu.VMEM((HALO + tm, CONV_CH), F32), pltpu.VMEM((tm, CONV_CH), F32),
                        pltpu.VMEM((tm + HALO, CONV_CH), F32), pltpu.VMEM((HALO, CONV_CH), F32)],
        compiler_params=_params("arbitrary"),
    )(pd, pd, pd, conv_w, al_vec, dt_vec, dqn, dkn, dvc, dgc, dz)


def _pad_lanes(v, offset=0):
    return jnp.zeros((1, 128), F32).at[0, offset:offset + v.shape[0]].set(v.astype(F32))


class _LocalReducer:
    def start(self, grads):
        return jnp.zeros((8, 128), F32)

    def middle(self, after):
        return jnp.zeros((8, 128), F32)

    def finish(self, after):
        return None


def _local_step(x, p, tgt, sm, w, late, reducer):
    t = x.shape[0]
    tm = min(512, t // 2)
    tm_s = min(512, t // 2)
    tw = min(1024, t // 2)
    tw_ff = min(2048, t // 2)

    w_in_t = w["w_in_t"]
    wa_t = w_in_t[:ATTN_Q + 2 * ATTN_KV]
    wd_t = jnp.pad(w_in_t[ATTN_Q + 2 * ATTN_KV:], ((0, DN_COLS - (D_IN - ATTN_Q - 2 * ATTN_KV)), (0, 0)))
    conv_w = w["conv_w"]
    al_vec, dt_vec = _pad_lanes(sm["a_log"], DN_HEADS), _pad_lanes(sm["dt_bias"], DN_HEADS)
    sinks_vec = _pad_lanes(sm["sinks"])
    dn_w = sm["dn_norm"].reshape(1, 128)
    row = lambda v: v.reshape(1, D_MODEL)
    cos, sin = _rope_tables(t)

    u, pa, pd = _inproj(x, row(sm["norm_mix"]), wa_t, wd_t, tm_s)
    ao = _attn_fwd(pa, cos, sin, sinks_vec)
    qn, kn, vc, gc, gr = _dn_prep(pd, conv_w, al_vec, dt_vec, tm_s)
    uu, ww, qg, kd, a_qk, tinv, dlast = _dn_intra(qn, kn, vc, gc, gr)
    dn_out, o, vnew, sall = _dn_scan_fwd(uu, ww, qg, kd, a_qk, dlast, pd, dn_w)
    w_o, late_rest = late(dn_out)
    wo_a, wo_d = w_o[:ATTN_Q], w_o[ATTN_Q:]
    h1 = _oproj(x, ao, dn_out, wo_a, wo_d, tm)
    w = dict(w, **late_rest(h1))
    w_proj = jnp.transpose(w["w_proj4"], (1, 0, 2)).reshape(PLE_DIM, D_MODEL)
    m, r, h2 = _mlp_fwd(h1, row(sm["norm_mlp"]), w["w_up4"], w["w_down"], tw)
    dh2, dh2b, dgp, dpp, n3, pb, acc_ple = _ple_loss(h2, p, tgt, row(sm["norm_ple"]), row(sm["norm_final"]),
                                                     w["w_gate"], w_proj, tm_s)
    g_w_gate = _wgrad(n3, dgp, "wgrad_gate", D_MODEL, D_MODEL, tw)
    g_w_proj = _wgrad(pb, dpp, "wgrad_proj", PLE_DIM, D_MODEL, tw)
    da, dh1, dh1b, dao, ddn, acc_mlp = _mlp_bwd(dh2, dh2b, r, h1, row(sm["norm_mlp"]), w["w_up4"], w["w_down"],
                                                wo_a, wo_d, tm)
    g_w_up4 = _wgrad(m, da, "wgrad_up", D_MODEL, FF_BLOCK, tw_ff, stacked=True)
    g_w_down = _wgrad(r, dh2b, "wgrad_down", FF_BLOCK, D_MODEL, tw_ff,
                      prep=lambda rv: jnp.square(rv.astype(F32)).astype(MXU_DTYPE))
    g_w_o = _wgrad_cat([ao, dn_out], [dh1b], "wgrad_o", tw)
    early = dict(w_up4=g_w_up4, w_down=g_w_down, w_gate=g_w_gate, w_proj=g_w_proj, w_o=g_w_o)
    dep = reducer.start(early)
    dz, du, dw, dqg, dkd, da_qk, ddlast, acc_dn = _dn_scan_bwd(ddn, o, vnew, sall, ww, qg, kd, a_qk, dlast, pd, dn_w,
                                                               dep)
    dep = reducer.middle(du)
    dqn, dkn, dvc, dgc = _dn_intra_bwd(qn, kn, vc, gc, gr, tinv, a_qk, du, dw, dqg, dkd, da_qk, ddlast, dlast, dep)
    d_dn, acc_conv, acc_gate = _dn_prep_bwd(pd, conv_w, al_vec, dt_vec, dqn, dkn, dvc, dgc, dz, tm_s)
    dq, dk, dv, acc_attn = _attn_bwd(pa, cos, sin, sinks_vec, dao)
    reducer.finish(dq)
    wq_t, wk_t, wv_t = wa_t[:ATTN_Q], wa_t[ATTN_Q:ATTN_Q + ATTN_KV], wa_t[ATTN_Q + ATTN_KV:]
    dx, acc_mix = _inproj_bwd(x, dh1, row(sm["norm_mix"]), [dq, dk, dv, d_dn], [wq_t, wk_t, wv_t, wd_t], tm_s)

    g_w_in_t = _wgrad_cat([dq, dk, dv, d_dn], [u], "wgrad_in", tw)
    grads = dict(early, w_in_t=g_w_in_t)
    sums = dict(loss=acc_ple[2, 0], norm_final=acc_ple[0], norm_ple=acc_ple[1], norm_mlp=acc_mlp[0], norm_mix=acc_mix[0],
                dn_norm=acc_dn[0], sinks=acc_attn[0, :ATTN_HEADS], a_log=acc_gate[0, DN_HEADS:2 * DN_HEADS],
                dt_bias=acc_gate[1, DN_HEADS:2 * DN_HEADS], conv_w=acc_conv[:DN_CONV])
    return sums, dx, grads


MESH = pl.DeviceIdType.MESH
ANY = pl.BlockSpec(memory_space=pl.ANY)
N_CHIPS = 4
N_DEV = 8


def _place():
    x, y, c = lax.axis_index("x"), lax.axis_index("y"), lax.axis_index("c")
    chips = [(1 - x, y), (x, 1 - y), (1 - x, 1 - y)]
    return x, y, c, chips


def _gather_weights(shards, conv_s):
    n = len(shards)
    per = 7

    def body(*refs):
        in_refs, conv_ref = refs[:n], refs[n]
        out_refs, conv_out = refs[n + 1:2 * n + 1], refs[2 * n + 1]
        send_sems, recv_sems = refs[2 * n + 2:]
        x, y, c, chips = _place()
        sibling = (x, y, 1 - c)

        def blk(a, px, py, pc):
            hr = in_refs[a].shape[0] // 2
            return out_refs[a].at[2 * px + py, pl.ds(pc * hr, hr), :]

        def mine(a):
            hr = in_refs[a].shape[0] // 2
            return in_refs[a].at[pl.ds(c * hr, hr), :]

        def rcopy(a, k, block, to, src=None):
            return pltpu.make_async_remote_copy(
                src_ref=blk(a, *block) if src is None else src, dst_ref=blk(a, *block),
                send_sem=send_sems.at[per * a + k], recv_sem=recv_sems.at[per * a + k],
                device_id=to, device_id_type=MESH)

        def whole(a, to):
            return pltpu.make_async_remote_copy(
                src_ref=in_refs[a], dst_ref=out_refs[a].at[2 * x + y],
                send_sem=send_sems.at[per * a], recv_sem=recv_sems.at[per * a], device_id=to, device_id_type=MESH)

        def ccopy(j, to):
            return pltpu.make_async_remote_copy(
                src_ref=conv_ref, dst_ref=conv_out.at[2 * x + y],
                send_sem=send_sems.at[per * n + j], recv_sem=recv_sems.at[per * n + j],
                device_id=to, device_id_type=MESH)

        started = []
        for a in range(n):
            first = [whole(a, sibling)]
            first += [rcopy(a, 1 + j, (x, y, c), (*chip, c), src=mine(a)) for j, chip in enumerate(chips)]
            for cp in first:
                cp.start()
            started += first
        conv_sends = [ccopy(j, (*chip, c)) for j, chip in enumerate(chips)] + [ccopy(3, sibling)]
        for cp in conv_sends:
            cp.start()
        started += conv_sends
        for a in range(n):
            for j, chip in enumerate(chips):
                rcopy(a, 1 + j, (*chip, c), (x, y, c)).wait_recv()
                fwd = rcopy(a, 4 + j, (*chip, c), sibling)
                fwd.start()
                started.append(fwd)
        for a in range(n):
            whole(a, sibling).wait_recv()
            for j, chip in enumerate(chips):
                rcopy(a, 4 + j, (*chip, 1 - c), (x, y, c)).wait_recv()
        for j, chip in enumerate(chips + [(x, y)]):
            pltpu.make_async_remote_copy(
                src_ref=conv_ref, dst_ref=conv_out.at[2 * chip[0] + chip[1]],
                send_sem=send_sems.at[per * n + j], recv_sem=recv_sems.at[per * n + j],
                device_id=sibling, device_id_type=MESH).wait_recv()
        for cp in started:
            cp.wait_send()

    nsem = per * n + 4
    out_shape = [jax.ShapeDtypeStruct((N_CHIPS,) + s.shape, s.dtype) for s in shards]
    out_shape.append(jax.ShapeDtypeStruct((N_CHIPS,) + conv_s.shape, conv_s.dtype))
    return pl.pallas_call(
        body, name="gather_weights", in_specs=[ANY] * (n + 1), out_specs=[ANY] * (n + 1), out_shape=out_shape,
        scratch_shapes=[pltpu.SemaphoreType.DMA((nsem,)), pltpu.SemaphoreType.DMA((nsem,))],
    )(*shards, conv_s)


HBM = pl.BlockSpec(memory_space=pltpu.HBM)
SEM = pl.BlockSpec(memory_space=pltpu.SEMAPHORE)
EFFECT = pltpu.SideEffectType.DATAFLOW_SIDE_EFFECTING
LATE_COPIES = 7


def _late_copies(in_refs, land_refs, send_sems, recv_sems, only=None):
    x, y, c, chips = _place()
    sends, arrivals = [], []
    for a, (src, land) in enumerate(zip(in_refs, land_refs)):
        if only is not None and a not in only:
            continue
        hr = src.shape[0] // 2
        base = LATE_COPIES * a

        def cp(src_ref, dst_ref, s_idx, r_idx, to):
            return pltpu.make_async_remote_copy(src_ref=src_ref, dst_ref=dst_ref, send_sem=send_sems.at[base + s_idx],
                                                recv_sem=recv_sems.at[base + r_idx], device_id=to, device_id_type=MESH)

        sends.append(cp(src, land.at[2 * x + y], 0, 0, (x, y, 1 - c)))
        arrivals.append(cp(src, land.at[2 * x + y], 0, 0, (x, y, 1 - c)))
        for j, chip in enumerate(chips):
            for pc in range(2):
                half = src.at[pl.ds(c * hr, hr), :]
                sends.append(cp(half, land.at[2 * x + y, pl.ds(c * hr, hr), :], 1 + 2 * j + pc, 1 + 2 * j + c, (*chip, pc)))
                arrivals.append(cp(half, land.at[2 * chip[0] + chip[1], pl.ds(pc * hr, hr), :], 1 + 2 * j + pc,
                                   1 + 2 * j + pc, (*chip, pc)))
    return sends, arrivals


def _copies_start(name, build, nsem, srcs, land_shapes, after):
    n = len(srcs)

    def body(*refs):
        sends, _ = build(refs[:n], refs[n:2 * n], refs[2 * n + 1], refs[2 * n + 2])
        for cp in sends:
            cp.start()
        refs[-1][...] = jnp.zeros_like(refs[-1])

    lands = [pltpu.with_memory_space_constraint(lax.empty(s.shape, s.dtype), pltpu.HBM) for s in land_shapes]
    ins = [pltpu.with_memory_space_constraint(s, pltpu.HBM) for s in srcs]
    out = pl.pallas_call(
        body, name=name,
        out_shape=(pltpu.SemaphoreType.DMA((nsem,)), pltpu.SemaphoreType.DMA((nsem,)),
                   *[pltpu.HBM(s.shape, s.dtype) for s in srcs], *[pltpu.HBM(s.shape, s.dtype) for s in land_shapes],
                   jax.ShapeDtypeStruct((8, 128), F32)),
        in_specs=[HBM] * (2 * n) + [ANY],
        out_specs=(SEM, SEM, *[HBM] * (2 * n), pl.BlockSpec(memory_space=pltpu.VMEM)),
        input_output_aliases={i: 2 + i for i in range(2 * n)},
        compiler_params=pltpu.CompilerParams(has_side_effects=EFFECT),
    )(*ins, *lands, after)
    return out[0], out[1], out[2:2 + n], out[2 + n:2 + 2 * n], out[-1]


def _copies_wait(name, build, started, after):
    send_sems, recv_sems, srcs, lands, _ = started
    n = len(srcs)

    def body(*refs):
        sends, arrivals = build(refs[:n], refs[n:2 * n], refs[2 * n], refs[2 * n + 1])
        for cp in sends:
            cp.wait_send()
        for cp in arrivals:
            cp.wait_recv()

    out = pl.pallas_call(
        body, name=name,
        out_shape=(*[pltpu.HBM(s.shape, s.dtype) for s in srcs], *[pltpu.HBM(l.shape, l.dtype) for l in lands]),
        in_specs=[HBM] * (2 * n) + [SEM, SEM, ANY],
        out_specs=tuple([HBM] * (2 * n)),
        input_output_aliases={i: i for i in range(2 * n)},
        compiler_params=pltpu.CompilerParams(has_side_effects=EFFECT),
    )(*srcs, *lands, send_sems, recv_sems, after)
    return out[:n], out[n:]


def _exchange_copies(g_refs, got_refs, send_sems, recv_sems):
    x, y, c, _ = _place()
    sends, arrivals = [], []
    for a, (g, got) in enumerate(zip(g_refs, got_refs)):
        hr = g.shape[1] // 2
        cp = pltpu.make_async_remote_copy(
            src_ref=g.at[:, pl.ds((1 - c) * hr, hr), :], dst_ref=got, send_sem=send_sems.at[a],
            recv_sem=recv_sems.at[a], device_id=(x, y, 1 - c), device_id_type=MESH)
        sends.append(cp)
        arrivals.append(cp)
    return sends, arrivals


def _scatter_copies(s_refs, got_refs, send_sems, recv_sems):
    x, y, c, chips = _place()
    sends, arrivals = [], []
    for a, (s16, got) in enumerate(zip(s_refs, got_refs)):
        for j, chip in enumerate(chips):
            cp = pltpu.make_async_remote_copy(
                src_ref=s16.at[2 * chip[0] + chip[1]], dst_ref=got.at[j], send_sem=send_sems.at[3 * a + j],
                recv_sem=recv_sems.at[3 * a + j], device_id=(*chip, c), device_id_type=MESH)
            sends.append(cp)
            arrivals.append(cp)
    return sends, arrivals


def _share_halves(name, bufs, dep):
    n = len(bufs)

    def body(*refs):
        out_refs = refs[n + 1:2 * n + 1]
        send_sems, recv_sems = refs[2 * n + 1:]
        x, y, c, _ = _place()
        remote = [pltpu.make_async_remote_copy(
            src_ref=out_refs[a].at[c], dst_ref=out_refs[a].at[c], send_sem=send_sems.at[a], recv_sem=recv_sems.at[a],
            device_id=(x, y, 1 - c), device_id_type=MESH) for a in range(n)]
        for cp in remote:
            cp.start()
        for a in range(n):
            pltpu.make_async_remote_copy(
                src_ref=out_refs[a].at[c], dst_ref=out_refs[a].at[1 - c], send_sem=send_sems.at[a],
                recv_sem=recv_sems.at[a], device_id=(x, y, 1 - c), device_id_type=MESH).wait_recv()
        for cp in remote:
            cp.wait_send()

    return pl.pallas_call(
        body, name=name, in_specs=[ANY] * (n + 1), out_specs=[ANY] * n,
        out_shape=[jax.ShapeDtypeStruct(b.shape, b.dtype) for b in bufs],
        input_output_aliases={a: a for a in range(n)},
        scratch_shapes=[pltpu.SemaphoreType.DMA((n,)), pltpu.SemaphoreType.DMA((n,))],
    )(*bufs, dep)


SMALL_ROWS, SMALL_COLS = 16, CONV_CH


def _allreduce_small(block):
    m_per, ncol = block.shape

    def body(x_ref, sum_ref, all_ref, send_sems, recv_sems, local_sem):
        x, y, c, chips = _place()
        me, sibling = (x, y, c), (x, y, 1 - c)

        def rows(px, py, pc):
            return all_ref.at[pl.ds((4 * px + 2 * py + pc) * m_per, m_per), :]

        def copy(k, block_of, to, src=None):
            return pltpu.make_async_remote_copy(
                src_ref=rows(*block_of) if src is None else src, dst_ref=rows(*block_of),
                send_sem=send_sems.at[k], recv_sem=recv_sems.at[k], device_id=to, device_id_type=MESH)

        mine = pltpu.make_async_copy(x_ref, rows(*me), local_sem)
        mine.start()
        first = [copy(0, me, sibling, src=x_ref)]
        first += [copy(1 + j, me, (*chip, c), src=x_ref) for j, chip in enumerate(chips)]
        for cp in first:
            cp.start()
        passed = [copy(4 + j, (*chip, c), sibling) for j, chip in enumerate(chips)]
        for j, chip in enumerate(chips):
            copy(1 + j, (*chip, c), me).wait_recv()
            passed[j].start()
        copy(0, sibling, me).wait_recv()
        for j, chip in enumerate(chips):
            copy(4 + j, (*chip, 1 - c), me).wait_recv()
        for cp in first + passed:
            cp.wait_send()
        mine.wait()
        total = all_ref[0:m_per, :]
        for d in range(1, N_DEV):
            total = total + all_ref[d * m_per:(d + 1) * m_per, :]
        sum_ref[...] = total

    vm = pl.BlockSpec(memory_space=pltpu.VMEM)
    return pl.pallas_call(
        body, name="allreduce_small", in_specs=[vm], out_specs=vm,
        out_shape=jax.ShapeDtypeStruct((m_per, ncol), F32),
        scratch_shapes=[pltpu.VMEM((N_DEV * m_per, ncol), F32), pltpu.SemaphoreType.DMA((7,)),
                        pltpu.SemaphoreType.DMA((7,)), pltpu.SemaphoreType.DMA],
    )(block)


def _row_tile(rows, cols):
    tile = rows
    while tile * cols * 4 > (1 << 20) and tile % 16 == 0:
        tile //= 2
    return tile


def _elementwise(fn, name, ins, out_dtypes, dep):
    rows, cols = ins[0].shape
    tile = _row_tile(rows, cols)

    def body(*refs):
        outs = fn(*[r[...] for r in refs[:len(ins)]])
        for o_ref, o in zip(refs[len(ins) + 1:], outs):
            o_ref[...] = o.astype(o_ref.dtype)

    if tile * cols * 4 > (1 << 21) and cols % 512 == 0:
        spec = pl.BlockSpec((rows, 256), lambda i: (0, i))
        steps = cols // 256
    else:
        spec = pl.BlockSpec((tile, cols), lambda i: (i, 0))
        steps = rows // tile
    return pl.pallas_call(
        body, name=name, grid=(steps,), in_specs=[spec] * len(ins) + [pl.BlockSpec(memory_space=pl.ANY)],
        out_specs=[spec] * len(out_dtypes),
        out_shape=[jax.ShapeDtypeStruct((rows, cols), d) for d in out_dtypes],
        compiler_params=_params("parallel"),
    )(*ins, dep)


def _adamw_tile(w, g, m, v):
    m = ADAM_B1 * m + (1.0 - ADAM_B1) * g
    v = ADAM_B2 * v + (1.0 - ADAM_B2) * jnp.square(g)
    m_hat = m / (1.0 - ADAM_B1 ** ADAM_STEP)
    v_hat = v / (1.0 - ADAM_B2 ** ADAM_STEP)
    delta = -ADAM_LR * (m_hat / (jnp.sqrt(v_hat) + ADAM_EPS) + ADAM_WD * w)
    return delta, m, v


def _adamw(name, w, g, m, v, dep):
    return _elementwise(_adamw_tile, name, [w, g, m, v], [F32, F32, F32], dep)


def _chip_sum(name, g4, got, place):
    nchip, hr, cols = got.shape
    tile = _row_tile(hr, cols)
    nblk = hr // tile

    def body(pl_ref, g_ref, o_ref, s32_ref, s16_ref):
        s = g_ref[...] + o_ref[...]
        s32_ref[...] = s
        s16_ref[...] = s.astype(BF16)

    spec = pl.BlockSpec((None, tile, cols), lambda k, i, pr: (k, i, 0))
    return pl.pallas_call(
        body, name=name,
        grid_spec=pltpu.PrefetchScalarGridSpec(
            num_scalar_prefetch=1, grid=(nchip, nblk),
            in_specs=[pl.BlockSpec((None, tile, cols), lambda k, i, pr: (k, pr[1] * nblk + i, 0)), spec],
            out_specs=[spec, spec]),
        out_shape=[jax.ShapeDtypeStruct(got.shape, F32), jax.ShapeDtypeStruct(got.shape, BF16)],
        compiler_params=_params("parallel", "parallel"),
    )(place, g4, got)


def _mesh_sum(name, s32, got, place):
    _, hr, cols = s32.shape
    tile = _row_tile(hr, cols)

    def body(pl_ref, own_ref, g0_ref, g1_ref, g2_ref, o_ref):
        o_ref[...] = ((own_ref[...] + g0_ref[...].astype(F32)) + g1_ref[...].astype(F32)) + g2_ref[...].astype(F32)

    slab = lambda j: pl.BlockSpec((None, tile, cols), lambda i, pr: (j, i, 0))
    return pl.pallas_call(
        body, name=name,
        grid_spec=pltpu.PrefetchScalarGridSpec(
            num_scalar_prefetch=1, grid=(hr // tile,),
            in_specs=[pl.BlockSpec((None, tile, cols), lambda i, pr: (pr[0], i, 0)), slab(0), slab(1), slab(2)],
            out_specs=pl.BlockSpec((None, tile, cols), lambda i, pr: (pr[1], i, 0))),
        out_shape=jax.ShapeDtypeStruct((2, hr, cols), F32),
        compiler_params=_params("parallel"),
    )(place, s32, got, got, got)


def _place_operand():
    return jnp.stack([2 * lax.axis_index("x") + lax.axis_index("y"), lax.axis_index("c")]).astype(jnp.int32)


W_IN_ROWS = 720
W_IN_GATHER_ROWS = 736


def _per_chip(name, g):
    if name == "w_in_t":
        rows = D_IN // N_CHIPS
        return jnp.stack([lax.slice_in_dim(g, rows * k, rows * k + W_IN_ROWS) for k in range(N_CHIPS)])
    if name == "w_proj":
        return jnp.transpose(g.reshape(PLE_DIM, N_CHIPS, D_MODEL // N_CHIPS), (1, 0, 2))
    if name == "w_up4":
        return g
    return g.reshape(N_CHIPS, g.shape[0] // N_CHIPS, g.shape[1])


class _EarlyReducer:
    def __init__(self, tag):
        self.tag = tag

    def start(self, grads):
        self.names = list(grads)
        self.place = _place_operand()
        slabs = [_per_chip(k, grads[k]) for k in self.names]
        halves = [jax.ShapeDtypeStruct((s.shape[0], s.shape[1] // 2, s.shape[2]), F32) for s in slabs]
        self.a = _copies_start(self.tag + "exchange_start", _exchange_copies, len(slabs), slabs, halves,
                               slabs[0][0, :8, :128])
        return self.a[-1]

    def middle(self, after):
        slabs, got = _copies_wait(self.tag + "exchange_wait", _exchange_copies, self.a, after)
        self.sums = [_chip_sum(self.tag + "chip_sum_" + k, s, g, self.place) for k, s, g in zip(self.names, slabs, got)]
        s16 = [s[1] for s in self.sums]
        lands = [jax.ShapeDtypeStruct((3,) + s.shape[1:], BF16) for s in s16]
        self.b = _copies_start(self.tag + "scatter_start", _scatter_copies, 3 * len(s16), s16, lands,
                               self.sums[0][0][0, :8, :128])
        return self.b[-1]

    def finish(self, after):
        _, got = _copies_wait(self.tag + "scatter_wait", _scatter_copies, self.b, after)
        self.bufs = {k: _mesh_sum(self.tag + "mesh_sum_" + k, s[0], g, self.place)
                     for k, s, g in zip(self.names, self.sums, got)}


def kernel(x, p, norm_mix, w_in, conv_w, a_log, dt_bias, dn_norm, sinks, w_o, norm_mlp, w_up, w_down, norm_ple, w_ple_gate, w_ple_proj, norm_final, loss_target, m_norm_mix, m_w_in, m_conv_w, m_a_log, m_dt_bias, m_dn_norm, m_sinks, m_w_o, m_norm_mlp, m_w_up, m_w_down, m_norm_ple, m_w_ple_gate, m_w_ple_proj, m_norm_final, v_norm_mix, v_w_in, v_conv_w, v_a_log, v_dt_bias, v_dn_norm, v_sinks, v_w_o, v_norm_mlp, v_w_up, v_w_down, v_norm_ple, v_w_ple_gate, v_w_ple_proj, v_norm_final):
    chip = 2 * lax.axis_index("x") + lax.axis_index("y")
    big = dict(w_in=w_in[0], w_o=w_o[0], w_up=w_up[0], w_down=w_down[0], w_gate=w_ple_gate[0], w_proj=w_ple_proj[0])
    big_m = dict(w_in=m_w_in[0], w_o=m_w_o[0], w_up=m_w_up[0], w_down=m_w_down[0], w_gate=m_w_ple_gate[0], w_proj=m_w_ple_proj[0])
    big_v = dict(w_in=v_w_in[0], w_o=v_w_o[0], w_up=v_w_up[0], w_down=v_w_down[0], w_gate=v_w_ple_gate[0], w_proj=v_w_ple_proj[0])
    names = list(big)

    rows_in = D_IN // N_CHIPS
    w_in_shard_t = jnp.pad(big["w_in"].T.astype(BF16), ((0, W_IN_GATHER_ROWS - rows_in), (0, 0)))
    w_in_all, conv_all = _gather_weights([w_in_shard_t], conv_w[0])
    late_names = names[1:]
    late_shards = [big[k].astype(BF16) for k in late_names]
    gather = _copies_start("gather_start", _late_copies, LATE_COPIES * len(late_shards), late_shards,
                           [jax.ShapeDtypeStruct((N_CHIPS,) + s.shape, BF16) for s in late_shards], w_in_all)
    token = gather[-1]
    w = dict(w_in_t=jnp.concatenate([w_in_all[k, :rows_in] for k in range(N_CHIPS)], axis=0),
             conv_w=jnp.transpose(conv_all, (1, 0, 2)).reshape(DN_CONV, CONV_CH))
    sm = dict(norm_mix=norm_mix[0] + token[0, 0], a_log=a_log[0], dt_bias=dt_bias[0], dn_norm=dn_norm[0],
              sinks=sinks[0], norm_mlp=norm_mlp[0], norm_ple=norm_ple[0], norm_final=norm_final)

    def late(after):
        first = functools.partial(_late_copies, only=(0,))
        srcs, lands = _copies_wait("gather_wait_o", first, gather, after)

        def rest(after2):
            others = functools.partial(_late_copies, only=tuple(range(1, len(late_names))))
            gw = dict(zip(late_names, _copies_wait("gather_wait_rest", others, gather[:2] + (srcs, lands, None), after2)[1]))
            return dict(w_up4=gw["w_up"], w_down=gw["w_down"].reshape(D_FF, D_MODEL),
                        w_gate=gw["w_gate"].reshape(D_MODEL, D_MODEL), w_proj4=gw["w_proj"])

        return lands[0].reshape(D_MODEL, D_MODEL), rest

    reducer = _EarlyReducer("early_")
    sums, grad_x, g = _local_step(x[0], p[0, 0], loss_target[0], sm, w, late, reducer)

    last = _EarlyReducer("last_")
    dep_a = last.start({"w_in_t": g["w_in_t"]})

    row = lambda v: jnp.zeros((SMALL_COLS,), F32).at[:v.shape[0]].set(v)
    misc = jnp.zeros((SMALL_COLS,), F32).at[0:4].set(sums["a_log"]).at[4:8].set(sums["dt_bias"]) \
        .at[8:16].set(sums["sinks"]).at[128:256].set(sums["dn_norm"]).at[256].set(sums["loss"])
    small = jnp.concatenate([sums["conv_w"], jnp.stack([row(sums["norm_mix"]), row(sums["norm_mlp"]), row(sums["norm_ple"]),
                                                        row(sums["norm_final"]), misc]),
                             jnp.zeros((SMALL_ROWS - 9, SMALL_COLS), F32)], axis=0)
    tot = _allreduce_small(small + dep_a[0, 0])
    dep_b = last.middle(tot)
    grad_key = dict(w_o="w_o", w_up="w_up4", w_down="w_down", w_gate="w_gate", w_proj="w_proj")
    full = _share_halves("share_halves", [reducer.bufs[grad_key[k]] for k in late_names], dep_b)
    red = {k: f.reshape(-1, f.shape[-1]) for k, f in zip(late_names, full)}
    loss = tot[8, 256]
    ncw = CONV_CH // N_CHIPS

    def pack(cw, nmix, nmlp, nple, nfin, al, dtb, sk, dnn):
        misc_p = jnp.zeros((SMALL_COLS,), F32).at[0:4].set(al).at[4:8].set(dtb).at[8:16].set(sk).at[128:256].set(dnn)
        cw_p = jnp.zeros((DN_CONV, SMALL_COLS), F32).at[:, :ncw].set(cw)
        return jnp.concatenate([cw_p, jnp.stack([row(nmix), row(nmlp), row(nple), row(nfin), misc_p]),
                                jnp.zeros((SMALL_ROWS - 9, SMALL_COLS), F32)], axis=0)

    def unpack(buf):
        return dict(conv_w=buf[0:4, :ncw][None], norm_mix=buf[4, :D_MODEL][None], norm_mlp=buf[5, :D_MODEL][None],
                    norm_ple=buf[6, :D_MODEL][None], norm_final=buf[7, :D_MODEL], a_log=buf[8, 0:4][None],
                    dt_bias=buf[8, 4:8][None], sinks=buf[8, 8:16][None], dn_norm=buf[8, 128:256][None])

    g_conv_shard = lax.dynamic_slice(tot[0:4], (0, chip * ncw), (DN_CONV, ncw))
    g_small = pack(g_conv_shard, tot[4, :D_MODEL], tot[5, :D_MODEL], tot[6, :D_MODEL], tot[7, :D_MODEL],
                   tot[8, 0:4], tot[8, 4:8], tot[8, 8:16], tot[8, 128:256])
    w_small = pack(conv_w[0], norm_mix[0], norm_mlp[0], norm_ple[0], norm_final, a_log[0], dt_bias[0], sinks[0], dn_norm[0])
    m_small = pack(m_conv_w[0], m_norm_mix[0], m_norm_mlp[0], m_norm_ple[0], m_norm_final, m_a_log[0], m_dt_bias[0],
                   m_sinks[0], m_dn_norm[0])
    v_small = pack(v_conv_w[0], v_norm_mix[0], v_norm_mlp[0], v_norm_ple[0], v_norm_final, v_a_log[0], v_dt_bias[0],
                   v_sinks[0], v_dn_norm[0])

    ref_name = dict(w_in="w_in", w_o="w_o", w_up="w_up", w_down="w_down", w_gate="w_ple_gate", w_proj="w_ple_proj")
    out_g, out_d, out_m, out_v = {}, {}, {}, {}

    def update(k, dep):
        d_k, m_k, v_k = _adamw("adamw_" + k, big[k], red[k], big_m[k], big_v[k], dep)
        out_g[ref_name[k]], out_d[ref_name[k]] = red[k][None], d_k[None]
        out_m[ref_name[k]], out_v[ref_name[k]] = m_k[None], v_k[None]
        return d_k

    for k in late_names:
        done = update(k, dep_b)
    small_out = _adamw("adamw_small", w_small, g_small, m_small, v_small, dep_b)
    d_s, m_s, v_s = (unpack(b) for b in small_out)
    g_s = unpack(g_small)
    for src, dst in ((g_s, out_g), (d_s, out_d), (m_s, out_m), (v_s, out_v)):
        dst.update(src)
    last.finish(done + small_out[0][0:1, 0:1])
    (w_in_full,) = _share_halves("share_halves_w_in", [last.bufs["w_in_t"]], dep_b)
    g_t = w_in_full.reshape(W_IN_ROWS, D_MODEL)[:D_IN // N_CHIPS]
    d_t, m_t, v_t = _adamw("adamw_w_in", big["w_in"].T, g_t, big_m["w_in"].T, big_v["w_in"].T, dep_b)
    out_g["w_in"], out_d["w_in"], out_m["w_in"], out_v["w_in"] = g_t.T[None], d_t.T[None], m_t.T[None], v_t.T[None]
    order = ["norm_mix", "w_in", "conv_w", "a_log", "dt_bias", "dn_norm", "sinks", "w_o", "norm_mlp", "w_up", "w_down",
             "norm_ple", "w_ple_gate", "w_ple_proj", "norm_final"]
    return (loss, grad_x[None], *[out_g[k] for k in order], *[out_d[k] for k in order],
            *[out_m[k] for k in order], *[out_v[k] for k in order])
```

```python
import functools

import jax
import jax.numpy as jnp
from jax import lax
from jax.experimental import pallas as pl
from jax.experimental.pallas import tpu as pltpu

F32 = jnp.float32
BF16 = jnp.bfloat16
MXU_DTYPE = jnp.bfloat16
HI = lax.Precision.HIGHEST

D_MODEL = 1024
PLE_DIM = 256
ATTN_HEADS = 8
ATTN_KV_HEADS = 2
ATTN_GROUPS = ATTN_HEADS // ATTN_KV_HEADS
ATTN_HEAD_DIM = 64
ATTN_BLOCK = 128
ROPE_THETA = 10000.0
DN_HEADS = 4
DN_HEAD_DIM = 128
DN_CONV = 4
DN_CHUNK = 64
D_FF = 4 * D_MODEL
EPS = 1e-6
ATTN_Q = ATTN_HEADS * ATTN_HEAD_DIM
ATTN_KV = ATTN_KV_HEADS * ATTN_HEAD_DIM
DN_W = DN_HEADS * DN_HEAD_DIM
CONV_CH = 3 * DN_W
D_IN = ATTN_Q + 2 * ATTN_KV + 4 * DN_W + 2 * DN_HEADS
DN_COLS = 4 * DN_W + 128
DN_SCALE = DN_HEAD_DIM ** -0.5
ATTN_SCALE = ATTN_HEAD_DIM ** -0.5
FF_BLOCKS = 4
FF_BLOCK = D_FF // FF_BLOCKS

ADAM_LR = 0.001
ADAM_B1 = 0.9
ADAM_B2 = 0.999
ADAM_EPS = 1e-08
ADAM_WD = 0.01
ADAM_STEP = 10

V7X_VMEM_BYTES = 64 * 1024 * 1024
VMEM_LIMIT = 48 * 1024 * 1024

NN = ((1,), (0,))
NT = ((1,), (1,))
TN = ((0,), (0,))


def _dot(a, b, dims=NN, prec=None):
    if a.dtype != b.dtype:
        a, b = a.astype(MXU_DTYPE), b.astype(MXU_DTYPE)
    return lax.dot_general(a, b, (dims, ((), ())), precision=prec, preferred_element_type=F32)


def _sigmoid(x):
    return 1.0 / (1.0 + jnp.exp(-x))


def _softplus(x):
    return jnp.maximum(x, 0.0) + jnp.log(1.0 + jnp.exp(-jnp.abs(x)))


def _params(*sem):
    return pltpu.CompilerParams(dimension_semantics=sem, vmem_limit_bytes=VMEM_LIMIT)


def _rms_fwd(xv, g):
    r = lax.rsqrt(jnp.mean(xv * xv, axis=-1, keepdims=True) + EPS)
    return xv * r * g


def _rms_bwd(xv, g, dn):
    r = lax.rsqrt(jnp.mean(xv * xv, axis=-1, keepdims=True) + EPS)
    xh = xv * r
    dg = jnp.sum(dn * xh, axis=0, keepdims=True)
    dxh = dn * g
    dx = r * (dxh - xh * jnp.mean(dxh * xh, axis=-1, keepdims=True))
    return dx, dg


def _full(shape):
    return pl.BlockSpec(shape, lambda *_: (0,) * len(shape))


def _inproj(x, g_mix, wa_t, wd_t, tm):
    t = x.shape[0]

    def body(x_ref, g_ref, wa_ref, wd_ref, u_ref, pa_ref, pd_ref):
        u = _rms_fwd(x_ref[...], g_ref[...]).astype(MXU_DTYPE)
        u_ref[...] = u
        pa_ref[...] = _dot(u, wa_ref[...], NT)
        pd_ref[...] = _dot(u, wd_ref[...], NT)

    na, nd = wa_t.shape[0], wd_t.shape[0]
    return pl.pallas_call(
        body, name="inproj", grid=(t // tm,),
        in_specs=[pl.BlockSpec((tm, D_MODEL), lambda i: (i, 0)), _full((1, D_MODEL)),
                  _full((na, D_MODEL)), _full((nd, D_MODEL))],
        out_specs=[pl.BlockSpec((tm, D_MODEL), lambda i: (i, 0)), pl.BlockSpec((tm, na), lambda i: (i, 0)),
                   pl.BlockSpec((tm, nd), lambda i: (i, 0))],
        out_shape=[jax.ShapeDtypeStruct((t, D_MODEL), MXU_DTYPE), jax.ShapeDtypeStruct((t, na), F32),
                   jax.ShapeDtypeStruct((t, nd), F32)],
        compiler_params=_params("parallel"),
    )(x, g_mix, wa_t, wd_t)


def _oproj(x, ao, dn, wo_a, wo_d, tm):
    t = x.shape[0]

    def body(x_ref, ao_ref, dn_ref, wa_ref, wd_ref, h_ref):
        h_ref[...] = (x_ref[...] + _dot(ao_ref[...].astype(MXU_DTYPE), wa_ref[...])
                      + _dot(dn_ref[...].astype(MXU_DTYPE), wd_ref[...]))

    half = ao.shape[1]
    return pl.pallas_call(
        body, name="oproj", grid=(t // tm,),
        in_specs=[pl.BlockSpec((tm, D_MODEL), lambda i: (i, 0)), pl.BlockSpec((tm, half), lambda i: (i, 0)),
                  pl.BlockSpec((tm, half), lambda i: (i, 0)), _full((half, D_MODEL)), _full((half, D_MODEL))],
        out_specs=pl.BlockSpec((tm, D_MODEL), lambda i: (i, 0)),
        out_shape=jax.ShapeDtypeStruct((t, D_MODEL), F32),
        compiler_params=_params("parallel"),
    )(x, ao, dn, wo_a, wo_d)


def _mlp_fwd(h1, g_mlp, w_up4, w_down, tm):
    t = h1.shape[0]

    def body(h_ref, g_ref, wu_ref, wd_ref, m_ref, r_ref, h2_ref, acc_ref):
        k = pl.program_id(1)

        @pl.when(k == 0)
        def _():
            m_ref[...] = _rms_fwd(h_ref[...], g_ref[...]).astype(MXU_DTYPE)
            acc_ref[...] = jnp.zeros_like(acc_ref)

        r = jnp.maximum(_dot(m_ref[...], wu_ref[...]), 0.0)
        r_ref[...] = r.astype(MXU_DTYPE)
        s = jnp.square(r).astype(MXU_DTYPE)
        acc_ref[...] += _dot(s, wd_ref[...])

        @pl.when(k == FF_BLOCKS - 1)
        def _():
            h2_ref[...] = h_ref[...] + acc_ref[...]

    return pl.pallas_call(
        body, name="mlp_fwd", grid=(t // tm, FF_BLOCKS),
        in_specs=[pl.BlockSpec((tm, D_MODEL), lambda i, k: (i, 0)), _full((1, D_MODEL)),
                  pl.BlockSpec((None, D_MODEL, FF_BLOCK), lambda i, k: (k, 0, 0)),
                  pl.BlockSpec((FF_BLOCK, D_MODEL), lambda i, k: (k, 0))],
        out_specs=[pl.BlockSpec((tm, D_MODEL), lambda i, k: (i, 0)), pl.BlockSpec((tm, FF_BLOCK), lambda i, k: (i, k)),
                   pl.BlockSpec((tm, D_MODEL), lambda i, k: (i, 0))],
        out_shape=[jax.ShapeDtypeStruct((t, D_MODEL), MXU_DTYPE), jax.ShapeDtypeStruct((t, D_FF), MXU_DTYPE),
                   jax.ShapeDtypeStruct((t, D_MODEL), F32)],
        scratch_shapes=[pltpu.VMEM((tm, D_MODEL), F32)],
        compiler_params=_params("parallel", "arbitrary"),
    )(h1, g_mlp, w_up4, w_down)


def _ple_loss(h2, p, tgt, g_ple, g_fin, w_gate, w_proj, tm):
    t = h2.shape[0]

    def body(h_ref, p_ref, t_ref, gp_ref, gf_ref, wg_ref, wp_ref,
             dh_ref, dhb_ref, dgp_ref, dpp_ref, n3_ref, pb_ref, acc_ref):
        @pl.when(pl.program_id(0) == 0)
        def _():
            acc_ref[...] = jnp.zeros_like(acc_ref)

        h = h_ref[...]
        g_ple_v, g_fin_v = gp_ref[...], gf_ref[...]
        n3 = _rms_fwd(h, g_ple_v).astype(MXU_DTYPE)
        n3_ref[...] = n3
        gate = _sigmoid(_dot(n3, wg_ref[...]))
        pb = p_ref[...].astype(MXU_DTYPE)
        pb_ref[...] = pb
        pp = _dot(pb, wp_ref[...])
        h3 = h + gate * pp
        r4 = lax.rsqrt(jnp.mean(h3 * h3, axis=-1, keepdims=True) + EPS)
        xh4 = h3 * r4
        e = xh4 * g_fin_v - t_ref[...]
        loss = 0.5 * jnp.sum(jnp.mean(e * e, axis=-1, keepdims=True), axis=0, keepdims=True)
        dy = e * (1.0 / D_MODEL)
        dg_fin = jnp.sum(dy * xh4, axis=0, keepdims=True)
        dxh = dy * g_fin_v
        dh3 = r4 * (dxh - xh4 * jnp.mean(dxh * xh4, axis=-1, keepdims=True))
        dpp_ref[...] = (dh3 * gate).astype(MXU_DTYPE)
        dgp = (dh3 * pp * gate * (1.0 - gate)).astype(MXU_DTYPE)
        dgp_ref[...] = dgp
        dn3 = _dot(dgp, wg_ref[...], NT)
        dx, dg_ple = _rms_bwd(h, g_ple_v, dn3)
        dh2 = dh3 + dx
        dh_ref[...] = dh2
        dhb_ref[...] = dh2.astype(MXU_DTYPE)
        acc_ref[0:1, :] += dg_fin
        acc_ref[1:2, :] += dg_ple
        acc_ref[2:3, :] += jnp.broadcast_to(loss, (1, D_MODEL))

    row = lambda w: pl.BlockSpec((tm, w), lambda i: (i, 0))
    return pl.pallas_call(
        body, name="ple_loss", grid=(t // tm,),
        in_specs=[row(D_MODEL), row(PLE_DIM), row(D_MODEL), _full((1, D_MODEL)), _full((1, D_MODEL)),
                  _full((D_MODEL, D_MODEL)), _full((PLE_DIM, D_MODEL))],
        out_specs=[row(D_MODEL), row(D_MODEL), row(D_MODEL), row(D_MODEL), row(D_MODEL), row(PLE_DIM),
                   _full((8, D_MODEL))],
        out_shape=[jax.ShapeDtypeStruct((t, D_MODEL), F32), jax.ShapeDtypeStruct((t, D_MODEL), MXU_DTYPE),
                   jax.ShapeDtypeStruct((t, D_MODEL), MXU_DTYPE), jax.ShapeDtypeStruct((t, D_MODEL), MXU_DTYPE),
                   jax.ShapeDtypeStruct((t, D_MODEL), MXU_DTYPE), jax.ShapeDtypeStruct((t, PLE_DIM), MXU_DTYPE),
                   jax.ShapeDtypeStruct((8, D_MODEL), F32)],
        compiler_params=_params("arbitrary"),
    )(h2, p, tgt, g_ple, g_fin, w_gate, w_proj)


def _mlp_bwd(dh2, dh2b, r, h1, g_mlp, w_up4, w_down, wo_a, wo_d, tm):
    t = h1.shape[0]
    half = wo_a.shape[0]

    def body(dh_ref, dhb_ref, r_ref, h_ref, g_ref, wu_ref, wd_ref, woa_ref, wod_ref,
             da_ref, dh1_ref, dh1b_ref, dao_ref, ddn_ref, acc_ref, dm_ref):
        i, k = pl.program_id(0), pl.program_id(1)

        @pl.when((i == 0) & (k == 0))
        def _():
            acc_ref[...] = jnp.zeros_like(acc_ref)

        @pl.when(k == 0)
        def _():
            dm_ref[...] = jnp.zeros_like(dm_ref)

        ds = _dot(dhb_ref[...], wd_ref[...], NT)
        da = (ds * (2.0 * r_ref[...].astype(F32))).astype(MXU_DTYPE)
        da_ref[...] = da
        dm_ref[...] += _dot(da, wu_ref[...], NT)

        @pl.when(k == FF_BLOCKS - 1)
        def _():
            dx, dg = _rms_bwd(h_ref[...], g_ref[...], dm_ref[...])
            dh1 = dh_ref[...] + dx
            dh1_ref[...] = dh1
            dh1b = dh1.astype(MXU_DTYPE)
            dh1b_ref[...] = dh1b
            dao_ref[...] = _dot(dh1b, woa_ref[...], NT)
            ddn_ref[...] = _dot(dh1b, wod_ref[...], NT)
            acc_ref[0:1, :] += dg

    tok = lambda w: pl.BlockSpec((tm, w), lambda i, k: (i, 0))
    return pl.pallas_call(
        body, name="mlp_bwd", grid=(t // tm, FF_BLOCKS),
        in_specs=[tok(D_MODEL), tok(D_MODEL), pl.BlockSpec((tm, FF_BLOCK), lambda i, k: (i, k)), tok(D_MODEL),
                  _full((1, D_MODEL)), pl.BlockSpec((None, D_MODEL, FF_BLOCK), lambda i, k: (k, 0, 0)),
                  pl.BlockSpec((FF_BLOCK, D_MODEL), lambda i, k: (k, 0)),
                  pl.BlockSpec((half, D_MODEL), lambda i, k: (0, 0)), pl.BlockSpec((half, D_MODEL), lambda i, k: (0, 0))],
        out_specs=[pl.BlockSpec((tm, FF_BLOCK), lambda i, k: (i, k)),
                   tok(D_MODEL), tok(D_MODEL), tok(half), tok(half), pl.BlockSpec((8, D_MODEL), lambda i, k: (0, 0))],
        out_shape=[jax.ShapeDtypeStruct((t, D_FF), MXU_DTYPE),
                   jax.ShapeDtypeStruct((t, D_MODEL), F32), jax.ShapeDtypeStruct((t, D_MODEL), MXU_DTYPE),
                   jax.ShapeDtypeStruct((t, half), F32), jax.ShapeDtypeStruct((t, half), F32),
                   jax.ShapeDtypeStruct((8, D_MODEL), F32)],
        scratch_shapes=[pltpu.VMEM((tm, D_MODEL), F32)],
        compiler_params=_params("arbitrary", "arbitrary"),
    )(dh2, dh2b, r, h1, g_mlp, w_up4, w_down, wo_a, wo_d)


def _inproj_bwd(x, dh1, g_mix, grads, weights, tm):
    t = x.shape[0]
    n = len(grads)

    def body(*refs):
        x_ref, dh_ref, g_ref = refs[:3]
        g_refs, w_refs = refs[3:3 + n], refs[3 + n:3 + 2 * n]
        dx_ref, acc_ref = refs[3 + 2 * n:]

        @pl.when(pl.program_id(0) == 0)
        def _():
            acc_ref[...] = jnp.zeros_like(acc_ref)

        du = _dot(g_refs[0][...], w_refs[0][...])
        for j in range(1, n):
            du += _dot(g_refs[j][...], w_refs[j][...])
        dx, dg = _rms_bwd(x_ref[...], g_ref[...], du)
        dx_ref[...] = dh_ref[...] + dx
        acc_ref[0:1, :] += dg

    tok = lambda w: pl.BlockSpec((tm, w), lambda i: (i, 0))
    return pl.pallas_call(
        body, name="inproj_bwd", grid=(t // tm,),
        in_specs=[tok(D_MODEL), tok(D_MODEL), _full((1, D_MODEL))] + [tok(g.shape[1]) for g in grads]
                 + [_full(w.shape) for w in weights],
        out_specs=[tok(D_MODEL), _full((8, D_MODEL))],
        out_shape=[jax.ShapeDtypeStruct((t, D_MODEL), F32), jax.ShapeDtypeStruct((8, D_MODEL), F32)],
        compiler_params=_params("arbitrary"),
    )(x, dh1, g_mix, *grads, *weights)


def _wgrad(a, b, name, tk, tn, tt, stacked=False, prep=None):
    t, kdim = a.shape
    ncols = b.shape[1]

    def body(a_ref, b_ref, o_ref):
        @pl.when(pl.program_id(2) == 0)
        def _():
            o_ref[...] = jnp.zeros_like(o_ref)

        av = a_ref[...] if prep is None else prep(a_ref[...])
        o_ref[...] += _dot(av, b_ref[...], TN)

    if stacked:
        out_spec = pl.BlockSpec((None, tk, tn), lambda i, j, s: (j, i, 0))
        out_shape = jax.ShapeDtypeStruct((ncols // tn, kdim, tn), F32)
    else:
        out_spec = pl.BlockSpec((tk, tn), lambda i, j, s: (i, j))
        out_shape = jax.ShapeDtypeStruct((kdim, ncols), F32)
    return pl.pallas_call(
        body, name=name, grid=(kdim // tk, ncols // tn, t // tt),
        in_specs=[pl.BlockSpec((tt, tk), lambda i, j, s: (s, i)), pl.BlockSpec((tt, tn), lambda i, j, s: (s, j))],
        out_specs=out_spec, out_shape=out_shape,
        compiler_params=_params("parallel", "parallel", "arbitrary"),
    )(a, b)


def _wgrad_cat(as_, bs, name, tt):
    t = as_[0].shape[0]
    heights = [a.shape[1] for a in as_]
    widths = [b.shape[1] for b in bs]

    def body(*refs):
        a_refs, b_refs, o_ref = refs[:len(as_)], refs[len(as_):-1], refs[-1]

        @pl.when(pl.program_id(0) == 0)
        def _():
            o_ref[...] = jnp.zeros_like(o_ref)

        row = 0
        for a_ref, k in zip(a_refs, heights):
            av = a_ref[...]
            col = 0
            for b_ref, n in zip(b_refs, widths):
                o_ref[row:row + k, col:col + n] += _dot(av, b_ref[...], TN)
                col += n
            row += k

    tok = lambda w: pl.BlockSpec((tt, w), lambda s: (s, 0))
    shape = (sum(heights), sum(widths))
    return pl.pallas_call(
        body, name=name, grid=(t // tt,),
        in_specs=[tok(k) for k in heights] + [tok(n) for n in widths],
        out_specs=_full(shape), out_shape=jax.ShapeDtypeStruct(shape, F32),
        compiler_params=_params("arbitrary"),
    )(*as_, *bs)


def _rope_tables(t):
    half = ATTN_HEAD_DIM // 2
    inv = 1.0 / (ROPE_THETA ** (jnp.arange(half, dtype=F32) * (2.0 / ATTN_HEAD_DIM)))
    ang = jnp.arange(t, dtype=F32)[:, None] * inv[None, :]
    cos, sin = jnp.cos(ang), jnp.sin(ang)
    cos2 = jnp.concatenate([cos, cos], axis=-1)
    sin2 = jnp.concatenate([-sin, sin], axis=-1)
    return jnp.tile(cos2, (1, 2)), jnp.tile(sin2, (1, 2))


def _swap_halves(tv):
    w = tv.shape[-1]
    lane = lax.broadcasted_iota(jnp.int32, tv.shape, tv.ndim - 1)
    first = (lane % ATTN_HEAD_DIM) < (ATTN_HEAD_DIM // 2)
    return jnp.where(first, pltpu.roll(tv, w - ATTN_HEAD_DIM // 2, tv.ndim - 1),
                     pltpu.roll(tv, ATTN_HEAD_DIM // 2, tv.ndim - 1))


def _rope(tv, cos, sin):
    return tv * cos + _swap_halves(tv) * sin


def _rope_bwd(dv, cos, sin):
    return dv * cos + _swap_halves(dv * sin)


def _attn_valid(first_block):
    c = lax.broadcasted_iota(jnp.int32, (2 * ATTN_BLOCK, ATTN_BLOCK), 0)
    r = lax.broadcasted_iota(jnp.int32, (2 * ATTN_BLOCK, ATTN_BLOCK), 1)
    return (c > r) & (c <= r + ATTN_BLOCK) & ((c >= ATTN_BLOCK) | jnp.logical_not(first_block))


def _attn_probs(st, sink, valid):
    s = jnp.where(valid, st * ATTN_SCALE, -jnp.inf)
    m = jnp.maximum(jnp.max(s, axis=0, keepdims=True), sink)
    e = jnp.where(valid, jnp.exp(s - m), 0.0)
    es = jnp.exp(sink - m)
    inv = 1.0 / (jnp.sum(e, axis=0, keepdims=True) + es)
    return e * inv, es * inv


def _lane_scalar(vec, idx):
    lane = lax.broadcasted_iota(jnp.int32, vec.shape, 1)
    return jnp.sum(jnp.where(lane == idx, vec, 0.0), axis=-1, keepdims=True)


def _attn_specs(nb):
    cur = lambda w, cb: pl.BlockSpec((ATTN_BLOCK, w), lambda i: (jnp.minimum(i, nb - 1), cb))
    prev = lambda w, cb: pl.BlockSpec((ATTN_BLOCK, w), lambda i: (jnp.maximum(jnp.minimum(i, nb - 1) - 1, 0), cb))
    kcol, vcol = ATTN_Q // ATTN_KV, ATTN_Q // ATTN_KV + 1
    return [cur(ATTN_Q, 0), cur(ATTN_KV, kcol), prev(ATTN_KV, kcol), cur(ATTN_KV, vcol), prev(ATTN_KV, vcol),
            cur(ATTN_KV, 0), cur(ATTN_KV, 0), prev(ATTN_KV, 0), prev(ATTN_KV, 0), _full((1, 128))]


def _attn_fwd(pa, cos, sin, sinks_vec):
    t = pa.shape[0]
    nb = t // ATTN_BLOCK

    def body(q_ref, kc_ref, kp_ref, vc_ref, vp_ref, cc_ref, sc_ref, cp_ref, sp_ref, sk_ref, o_ref):
        first = pl.program_id(0) == 0
        cc, sc = cc_ref[...], sc_ref[...]
        q = _rope(q_ref[...], jnp.tile(cc, (1, ATTN_Q // ATTN_KV)), jnp.tile(sc, (1, ATTN_Q // ATTN_KV)))
        kc = _rope(kc_ref[...], cc, sc)
        kp = _rope(kp_ref[...], cp_ref[...], sp_ref[...])
        vc, vp = vc_ref[...], vp_ref[...]
        sk = sk_ref[...]
        valid = _attn_valid(first)
        kv = lambda tp, tc, hk: jnp.concatenate([tp[:, hk * ATTN_HEAD_DIM:(hk + 1) * ATTN_HEAD_DIM],
                                                 tc[:, hk * ATTN_HEAD_DIM:(hk + 1) * ATTN_HEAD_DIM]], axis=0)
        kwins = [kv(kp, kc, hk) for hk in range(ATTN_KV_HEADS)]
        vwins_t = [kv(vp, vc, hk).T for hk in range(ATTN_KV_HEADS)]
        heads = [slice(h * ATTN_HEAD_DIM, (h + 1) * ATTN_HEAD_DIM) for h in range(ATTN_HEADS)]
        scores = [_dot(kwins[h // ATTN_GROUPS], q[:, hs], NT) for h, hs in enumerate(heads)]
        probs = [_attn_probs(st, _lane_scalar(sk, h), valid)[0] for h, st in enumerate(scores)]
        for h, (hs, pt) in enumerate(zip(heads, probs)):
            o_ref[:, hs] = _dot(vwins_t[h // ATTN_GROUPS], pt).T.astype(o_ref.dtype)

    return pl.pallas_call(
        body, name="attn_fwd", grid=(nb,),
        in_specs=_attn_specs(nb),
        out_specs=pl.BlockSpec((ATTN_BLOCK, ATTN_Q), lambda i: (i, 0)),
        out_shape=jax.ShapeDtypeStruct((t, ATTN_Q), MXU_DTYPE),
        compiler_params=_params("parallel"),
    )(pa, pa, pa, pa, pa, cos, sin, cos, sin, sinks_vec)


def _attn_bwd(pa, cos, sin, sinks_vec, dao):
    t = pa.shape[0]
    nb = t // ATTN_BLOCK

    def body(q_ref, kc_ref, kp_ref, vc_ref, vp_ref, cc_ref, sc_ref, cp_ref, sp_ref, sk_ref, do_ref,
             dq_ref, dk_ref, dv_ref, acc_ref, dqr_ref, dkw_ref, dvw_ref, ck_ref, cv_ref):
        i = pl.program_id(0)

        @pl.when(i == 0)
        def _():
            acc_ref[...] = jnp.zeros_like(acc_ref)
            ck_ref[...] = jnp.zeros_like(ck_ref)
            cv_ref[...] = jnp.zeros_like(cv_ref)

        @pl.when(i < nb)
        def _():
            first = i == 0
            cc, sc = cc_ref[...], sc_ref[...]
            cq, sq = jnp.tile(cc, (1, ATTN_Q // ATTN_KV)), jnp.tile(sc, (1, ATTN_Q // ATTN_KV))
            q = _rope(q_ref[...], cq, sq)
            kc = _rope(kc_ref[...], cc, sc)
            kp = _rope(kp_ref[...], cp_ref[...], sp_ref[...])
            vc, vp = vc_ref[...], vp_ref[...]
            sk = sk_ref[...]
            do = do_ref[...]
            lane = lax.broadcasted_iota(jnp.int32, (1, 128), 1)
            dsink = jnp.zeros((1, 128), F32)
            valid = _attn_valid(first)
            kv = lambda tp, tc, hk: jnp.concatenate([tp[:, hk * ATTN_HEAD_DIM:(hk + 1) * ATTN_HEAD_DIM],
                                                     tc[:, hk * ATTN_HEAD_DIM:(hk + 1) * ATTN_HEAD_DIM]], axis=0)
            kwins = [kv(kp, kc, hk) for hk in range(ATTN_KV_HEADS)]
            vwins = [kv(vp, vc, hk) for hk in range(ATTN_KV_HEADS)]
            kwins_t = [kw.T for kw in kwins]
            heads = [slice(h * ATTN_HEAD_DIM, (h + 1) * ATTN_HEAD_DIM) for h in range(ATTN_HEADS)]
            scores = [_dot(kwins[h // ATTN_GROUPS], q[:, hs], NT) for h, hs in enumerate(heads)]
            dps = [_dot(vwins[h // ATTN_GROUPS], do[:, hs], NT) for h, hs in enumerate(heads)]
            pts, dsts = [], []
            for h, (st, dp_t) in enumerate(zip(scores, dps)):
                probs_t, psink = _attn_probs(st, _lane_scalar(sk, h), valid)
                delta = jnp.sum(probs_t * dp_t, axis=0, keepdims=True)
                pts.append(probs_t)
                dsts.append(probs_t * (dp_t - delta) * ATTN_SCALE)
                dsink += jnp.where(lane == h, jnp.sum(-psink * delta, axis=1, keepdims=True), 0.0)
            for h, (hs, ds_t) in enumerate(zip(heads, dsts)):
                dqr_ref[:, hs] = _dot(kwins_t[h // ATTN_GROUPS], ds_t).T
            for hk in range(ATTN_KV_HEADS):
                ks = slice(hk * ATTN_HEAD_DIM, (hk + 1) * ATTN_HEAD_DIM)
                group = range(hk * ATTN_GROUPS, (hk + 1) * ATTN_GROUPS)
                ds_g = jnp.concatenate([dsts[h] for h in group], axis=1)
                p_g = jnp.concatenate([pts[h] for h in group], axis=1)
                q_g = jnp.concatenate([q[:, heads[h]] for h in group], axis=0)
                do_g = jnp.concatenate([do[:, heads[h]] for h in group], axis=0)
                dkw_ref[:, ks] = _dot(ds_g, q_g)
                dvw_ref[:, ks] = _dot(p_g, do_g)
            acc_ref[0:1, :] += dsink
            dq_ref[...] = _rope_bwd(dqr_ref[...], cq, sq).astype(dq_ref.dtype)
            dk_ref[...] = (ck_ref[...] + _rope_bwd(dkw_ref[0:ATTN_BLOCK, :], cp_ref[...], sp_ref[...])).astype(dk_ref.dtype)
            dv_ref[...] = (cv_ref[...] + dvw_ref[0:ATTN_BLOCK, :]).astype(dv_ref.dtype)
            ck_ref[...] = _rope_bwd(dkw_ref[ATTN_BLOCK:2 * ATTN_BLOCK, :], cc, sc)
            cv_ref[...] = dvw_ref[ATTN_BLOCK:2 * ATTN_BLOCK, :]

        @pl.when(i == nb)
        def _():
            dk_ref[...] = ck_ref[...].astype(dk_ref.dtype)
            dv_ref[...] = cv_ref[...].astype(dv_ref.dtype)

    prev_out = lambda w: pl.BlockSpec((ATTN_BLOCK, w), lambda i: (jnp.maximum(i - 1, 0), 0))
    return pl.pallas_call(
        body, name="attn_bwd", grid=(nb + 1,),
        in_specs=_attn_specs(nb) + [pl.BlockSpec((ATTN_BLOCK, ATTN_Q), lambda i: (jnp.minimum(i, nb - 1), 0))],
        out_specs=[pl.BlockSpec((ATTN_BLOCK, ATTN_Q), lambda i: (jnp.minimum(i, nb - 1), 0)), prev_out(ATTN_KV),
                   prev_out(ATTN_KV), _full((8, 128))],
        out_shape=[jax.ShapeDtypeStruct((t, ATTN_Q), MXU_DTYPE), jax.ShapeDtypeStruct((t, ATTN_KV), MXU_DTYPE),
                   jax.ShapeDtypeStruct((t, ATTN_KV), MXU_DTYPE), jax.ShapeDtypeStruct((8, 128), F32)],
        scratch_shapes=[pltpu.VMEM((ATTN_BLOCK, ATTN_Q), F32), pltpu.VMEM((2 * ATTN_BLOCK, ATTN_KV), F32),
                        pltpu.VMEM((2 * ATTN_BLOCK, ATTN_KV), F32), pltpu.VMEM((ATTN_BLOCK, ATTN_KV), F32),
                        pltpu.VMEM((ATTN_BLOCK, ATTN_KV), F32)],
        compiler_params=_params("arbitrary"),
    )(pa, pa, pa, pa, pa, cos, sin, cos, sin, sinks_vec, dao)


PAIR = 2 * DN_CHUNK
INTRA_PAIRS = 4
SCAN_PAIRS = 4
HALO = 8


def _conv_window(cur_ref, prev_ref, xs_ref, tm, has_prev):
    prev = jnp.where(has_prev, prev_ref[...], 0.0)
    xs_ref[0:HALO, :] = prev
    xs_ref[HALO:HALO + tm, :] = cur_ref[...]


def _conv_taps(xs_ref, cw_ref, tm):
    y = cw_ref[0:1, :] * xs_ref[pl.ds(HALO - DN_CONV + 1, tm), :]
    for j in range(1, DN_CONV):
        y += cw_ref[j:j + 1, :] * xs_ref[pl.ds(HALO - DN_CONV + 1 + j, tm), :]
    return y


def _gate_values(ba, al, dt):
    beta = _sigmoid(ba)
    pre = ba + dt
    g = -jnp.exp(al) * _softplus(pre)
    return beta, g, pre


def _dn_prep_specs(tm, tile):
    return [pl.BlockSpec((tm, CONV_CH), lambda i: (tile(i), 0)),
            pl.BlockSpec((HALO, CONV_CH), lambda i: (jnp.maximum(tile(i) * (tm // HALO) - 1, 0), 0)),
            pl.BlockSpec((tm, 128), lambda i: (tile(i), 4 * DN_W // 128)),
            _full((DN_CONV, CONV_CH)), _full((1, 128)), _full((1, 128))]


def _dn_prep(pd, conv_w, al_vec, dt_vec, tm):
    t = pd.shape[0]

    def body(cur_ref, prev_ref, ba_ref, cw_ref, al_ref, dt_ref, qn_ref, kn_ref, vc_ref, gc_ref, gr_ref, xs_ref):
        _conv_window(cur_ref, prev_ref, xs_ref, tm, pl.program_id(0) > 0)
        y = _conv_taps(xs_ref, cw_ref, tm)
        c = y * _sigmoid(y)
        for h in range(DN_HEADS):
            qs = slice(h * DN_HEAD_DIM, (h + 1) * DN_HEAD_DIM)
            ksl = slice(DN_W + h * DN_HEAD_DIM, DN_W + (h + 1) * DN_HEAD_DIM)
            qh, kh = c[:, qs], c[:, ksl]
            qn_ref[:, qs] = qh * lax.rsqrt(jnp.sum(qh * qh, axis=-1, keepdims=True) + EPS) * DN_SCALE
            kn_ref[:, qs] = kh * lax.rsqrt(jnp.sum(kh * kh, axis=-1, keepdims=True) + EPS)
        vc_ref[...] = c[:, 2 * DN_W:3 * DN_W]
        beta, g, _ = _gate_values(ba_ref[...], al_ref[...], dt_ref[...])
        lane = lax.broadcasted_iota(jnp.int32, beta.shape, 1)
        gb = jnp.where(lane < DN_HEADS, beta, jnp.where(lane < 2 * DN_HEADS, g, 0.0))
        gc_ref[...] = gb
        gr_ref[...] = gb.T[0:8, :]

    tok = lambda w: pl.BlockSpec((tm, w), lambda i: (i, 0))
    return pl.pallas_call(
        body, name="dn_prep", grid=(t // tm,),
        in_specs=_dn_prep_specs(tm, lambda i: i),
        out_specs=[tok(DN_W), tok(DN_W), tok(DN_W), tok(128), pl.BlockSpec((8, tm), lambda i: (0, i))],
        out_shape=[jax.ShapeDtypeStruct((t, DN_W), F32)] * 3 + [jax.ShapeDtypeStruct((t, 128), F32),
                                                                 jax.ShapeDtypeStruct((8, t), F32)],
        scratch_shapes=[pltpu.VMEM((HALO + tm, CONV_CH), F32)],
        compiler_params=_params("parallel"),
    )(pd, pd, pd, conv_w, al_vec, dt_vec)


def _pair_masks():
    r = lax.broadcasted_iota(jnp.int32, (PAIR, PAIR), 0)
    c = lax.broadcasted_iota(jnp.int32, (PAIR, PAIR), 1)
    same = (r < DN_CHUNK) == (c < DN_CHUNK)
    return same & (r >= c), same & (r > c)


def _lane_col(mat, idx):
    lane = lax.broadcasted_iota(jnp.int32, mat.shape, 1)
    return jnp.sum(jnp.where(lane == idx, mat, 0.0), axis=-1, keepdims=True)


def _pair_cumsums(gc, gr, low):
    lowf = low.astype(F32)
    return _dot(lowf, gc, NN, HI), _dot(gr, lowf, NT, HI)


def _pair_gates(gc, cum_c, cum_r, low, h):
    beta = _lane_col(gc, h)
    gam = _lane_col(cum_c, DN_HEADS + h)
    gam_row = cum_r[DN_HEADS + h:DN_HEADS + h + 1, :]
    dm = jnp.where(low, jnp.exp(jnp.where(low, gam - gam_row, 0.0)), 0.0)
    row = lax.broadcasted_iota(jnp.int32, gam.shape, 0)
    gl = jnp.where(row < DN_CHUNK, gam[DN_CHUNK - 1:DN_CHUNK, :], gam[PAIR - 1:PAIR, :])
    return beta, gam, dm, gl


def _split(a):
    hi = a.astype(BF16)
    return hi, (a - hi.astype(F32)).astype(BF16)


def _dot_split(a, b, dims=NN):
    (ah, al), (bh, bl) = a, b
    la, lb = (1, 1) if dims == TN else ((0, 1) if dims == NN else (0, 0))
    r = _dot(jnp.concatenate([ah, al], axis=la), jnp.concatenate([bh, bl], axis=lb), dims)
    m, n = r.shape[0] // 2, r.shape[1] // 2
    return (r[m:, n:] + (r[:m, n:] + r[m:, :n])) + r[:m, :n]


def _unit_lower_inverses(lmats):
    n = lmats[0].shape[0]
    r = lax.broadcasted_iota(jnp.int32, (n, n), 0)
    c = lax.broadcasted_iota(jnp.int32, (n, n), 1)
    same = lambda size: (r & ~(size - 1)) == (c & ~(size - 1))
    base = DN_CHUNK // 4
    diag = [jnp.where(same(base), l, 0.0) for l in lmats]
    accs = [(r == c).astype(F32) - d for d in diag]
    splits = [_split(d) for d in diag]
    step = 1
    while 2 * step < base:
        splits = [_split(_dot_split(s, s)) for s in splits]
        accs = [acc + _dot_split(_split(acc), s) for acc, s in zip(accs, splits)]
        step *= 2
    size = base
    while size < DN_CHUNK:
        below = same(2 * size) & jnp.logical_not(same(size))
        tb = [_dot(acc, jnp.where(below, l, 0.0)) for acc, l in zip(accs, lmats)]
        accs = [acc - _dot(t, acc) for acc, t in zip(accs, tb)]
        size *= 2
    return accs


def _dn_intra(qn, kn, vc, gc, gr):
    t = qn.shape[0]
    npair = t // PAIR
    rows_step = INTRA_PAIRS * PAIR

    def body(q_ref, k_ref, v_ref, gc_ref, gr_ref, u_ref, w_ref, qg_ref, kd_ref, a_ref, ti_ref, dl_ref):
        low, strict = _pair_masks()
        items = []
        for p in range(INTRA_PAIRS):
            rows = slice(p * PAIR, (p + 1) * PAIR)
            gc_v = gc_ref[rows, :]
            cum_c, cum_r = _pair_cumsums(gc_v, gr_ref[:, rows], low)
            for h in range(DN_HEADS):
                hs = slice(h * DN_HEAD_DIM, (h + 1) * DN_HEAD_DIM)
                items.append((p, h, rows, hs, _pair_gates(gc_v, cum_c, cum_r, low, h)))
        lmats = []
        for p, h, rows, hs, (beta, gam, dm, gl) in items:
            k = k_ref[rows, hs]
            lmats.append(jnp.where(strict, _dot(k * beta, k, NT) * dm, 0.0))
        tinvs = _unit_lower_inverses(lmats)
        for (p, h, rows, hs, (beta, gam, dm, gl)), tinv in zip(items, tinvs):
            q, k, v = q_ref[rows, hs], k_ref[rows, hs], v_ref[rows, hs]
            eg = jnp.exp(gam)
            u_ref[rows, hs] = _dot(tinv, v * beta)
            w_ref[rows, hs] = _dot(tinv, (k * beta) * eg).astype(w_ref.dtype)
            a_ref[h, rows, :] = _dot(q, k, NT) * dm
            ti_ref[h, rows, :] = tinv
            qg_ref[rows, hs] = (q * eg).astype(qg_ref.dtype)
            kd_ref[rows, hs] = (k * jnp.exp(gl - gam)).astype(kd_ref.dtype)
            for c in range(2):
                last = (c + 1) * DN_CHUNK - 1
                dl_ref[2 * p + c, h] = jnp.broadcast_to(jnp.exp(gam[last:last + 1, :]), (8, 128))

    tok = lambda w: pl.BlockSpec((rows_step, w), lambda n: (n, 0))
    hm = pl.BlockSpec((DN_HEADS, rows_step, PAIR), lambda n: (0, n, 0))
    return pl.pallas_call(
        body, name="dn_intra", grid=(npair // INTRA_PAIRS,),
        in_specs=[tok(DN_W), tok(DN_W), tok(DN_W), tok(128), pl.BlockSpec((8, rows_step), lambda n: (0, n))],
        out_specs=[tok(DN_W)] * 4 + [hm, hm, pl.BlockSpec((2 * INTRA_PAIRS, DN_HEADS, 8, 128), lambda n: (n, 0, 0, 0))],
        out_shape=[jax.ShapeDtypeStruct((t, DN_W), F32)] + [jax.ShapeDtypeStruct((t, DN_W), MXU_DTYPE)] * 3
                  + [jax.ShapeDtypeStruct((DN_HEADS, t, PAIR), F32)] * 2
                  + [jax.ShapeDtypeStruct((2 * npair, DN_HEADS, 8, 128), F32)],
        compiler_params=_params("parallel"),
    )(qn, kn, vc, gc, gr)


def _dn_scan_fwd(u, w, qg, kd, a_qk, dlast, pd, dn_w):
    t = u.shape[0]
    npair = t // PAIR

    def body(u_ref, w_ref, qg_ref, kd_ref, a_ref, dl_ref, z_ref, nw_ref, out_ref, o_ref, vn_ref, sall_ref, s_ref):
        @pl.when(pl.program_id(0) == 0)
        def _():
            s_ref[...] = jnp.zeros_like(s_ref)

        nw = nw_ref[...]
        for c in range(2 * SCAN_PAIRS):
            rows = slice(c * DN_CHUNK, (c + 1) * DN_CHUNK)
            diag = slice((c % 2) * DN_CHUNK, (c % 2 + 1) * DN_CHUNK)
            for h in range(DN_HEADS):
                hs = slice(h * DN_HEAD_DIM, (h + 1) * DN_HEAD_DIM)
                st = s_ref[h]
                sall_ref[c, h] = st
                vn_ref[rows, hs] = (u_ref[rows, hs] - _dot(w_ref[rows, hs], st)).astype(vn_ref.dtype)
            for h in range(DN_HEADS):
                hs = slice(h * DN_HEAD_DIM, (h + 1) * DN_HEAD_DIM)
                st, vn = s_ref[h], vn_ref[rows, hs]
                o = _dot(qg_ref[rows, hs], st) + _dot(a_ref[h, rows, diag], vn)
                s_ref[h] = st * dl_ref[c, h][0:1, :] + _dot(kd_ref[rows, hs], vn, TN)
                o_ref[rows, hs] = o
                z = z_ref[rows, hs]
                on = o * lax.rsqrt(jnp.mean(o * o, axis=-1, keepdims=True) + EPS) * nw
                out_ref[rows, hs] = (on * (z * _sigmoid(z))).astype(out_ref.dtype)

    rows_step = SCAN_PAIRS * PAIR
    tok = pl.BlockSpec((rows_step, DN_W), lambda n: (n, 0))
    hm = pl.BlockSpec((DN_HEADS, rows_step, PAIR), lambda n: (0, n, 0))
    return pl.pallas_call(
        body, name="dn_scan_fwd", grid=(npair // SCAN_PAIRS,),
        in_specs=[tok, tok, tok, tok, hm, pl.BlockSpec((2 * SCAN_PAIRS, DN_HEADS, 8, 128), lambda n: (n, 0, 0, 0)),
                  pl.BlockSpec((rows_step, DN_W), lambda n: (n, 3)), _full((1, 128))],
        out_specs=[tok, tok, tok,
                   pl.BlockSpec((2 * SCAN_PAIRS, DN_HEADS, DN_HEAD_DIM, DN_HEAD_DIM), lambda n: (n, 0, 0, 0))],
        out_shape=[jax.ShapeDtypeStruct((t, DN_W), MXU_DTYPE), jax.ShapeDtypeStruct((t, DN_W), F32),
                   jax.ShapeDtypeStruct((t, DN_W), MXU_DTYPE),
                   jax.ShapeDtypeStruct((2 * npair, DN_HEADS, DN_HEAD_DIM, DN_HEAD_DIM), F32)],
        scratch_shapes=[pltpu.VMEM((DN_HEADS, DN_HEAD_DIM, DN_HEAD_DIM), F32)],
        compiler_params=_params("arbitrary"),
    )(u, w, qg, kd, a_qk, dlast, pd, dn_w)


def _dn_scan_bwd(dout, o, vnew, sall, w, qg, kd, a_qk, dlast, pd, dn_w, dep):
    t = o.shape[0]
    npair = t // PAIR
    nstep = npair // SCAN_PAIRS
    rev = lambda n: nstep - 1 - n

    def body(do_ref, o_ref, vn_ref, sall_ref, w_ref, qg_ref, kd_ref, a_ref, dl_ref, z_ref, nw_ref, dep_ref,
             dz_ref, du_ref, dw_ref, dqg_ref, dkd_ref, da_ref, ddl_ref, acc_ref, ds_ref, dos_ref):
        @pl.when(pl.program_id(0) == 0)
        def _():
            ds_ref[...] = jnp.zeros_like(ds_ref)
            acc_ref[...] = jnp.zeros_like(acc_ref)

        nw = nw_ref[...]
        dnw = jnp.zeros((1, 128), F32)
        for h in range(DN_HEADS):
            hs = slice(h * DN_HEAD_DIM, (h + 1) * DN_HEAD_DIM)
            o, z, dout = o_ref[:, hs], z_ref[:, hs], do_ref[:, hs]
            r = lax.rsqrt(jnp.mean(o * o, axis=-1, keepdims=True) + EPS)
            oh = o * r
            sz = _sigmoid(z)
            dz_ref[:, hs] = dout * (oh * nw) * (sz + z * sz * (1.0 - sz))
            don = dout * (z * sz)
            dnw += jnp.sum(don * oh, axis=0, keepdims=True)
            doh = don * nw
            dos_ref[:, hs] = r * (doh - oh * jnp.mean(doh * oh, axis=-1, keepdims=True))
        acc_ref[0:1, :] += dnw
        for c in reversed(range(2 * SCAN_PAIRS)):
            rows = slice(c * DN_CHUNK, (c + 1) * DN_CHUNK)
            diag = slice((c % 2) * DN_CHUNK, (c % 2 + 1) * DN_CHUNK)
            other = slice((1 - c % 2) * DN_CHUNK, (2 - c % 2) * DN_CHUNK)
            for h in range(DN_HEADS):
                hs = slice(h * DN_HEAD_DIM, (h + 1) * DN_HEAD_DIM)
                do, st, dsp, vn = dos_ref[rows, hs], sall_ref[c, h], ds_ref[h], vn_ref[rows, hs]
                da_ref[h, rows, diag] = _dot(do, vn, NT)
                da_ref[h, rows, other] = jnp.zeros((DN_CHUNK, DN_CHUNK), F32)
                du_ref[rows, hs] = (_dot(a_ref[h, rows, diag], do, TN) + _dot(kd_ref[rows, hs], dsp)).astype(du_ref.dtype)
                dqg_ref[rows, hs] = _dot(do, st, NT)
                dkd_ref[rows, hs] = _dot(vn, dsp, NT)
                ddl = jnp.sum(jnp.sum(dsp * st, axis=1, keepdims=True), axis=0, keepdims=True)
                ddl_ref[c, h] = jnp.broadcast_to(ddl, (8, 128))
            for h in range(DN_HEADS):
                hs = slice(h * DN_HEAD_DIM, (h + 1) * DN_HEAD_DIM)
                do, st, dvn = dos_ref[rows, hs], sall_ref[c, h], du_ref[rows, hs]
                dw_ref[rows, hs] = (-_dot(dvn, st, NT)).astype(dw_ref.dtype)
                ds_ref[h] = (ds_ref[h] * dl_ref[c, h][0:1, :] + _dot(qg_ref[rows, hs], do, TN)
                             - _dot(w_ref[rows, hs], dvn, TN))

    rows_step = SCAN_PAIRS * PAIR
    tok = pl.BlockSpec((rows_step, DN_W), lambda n: (rev(n), 0))
    hm = pl.BlockSpec((DN_HEADS, rows_step, PAIR), lambda n: (0, rev(n), 0))
    sc = pl.BlockSpec((2 * SCAN_PAIRS, DN_HEADS, 8, 128), lambda n: (rev(n), 0, 0, 0))
    return pl.pallas_call(
        body, name="dn_scan_bwd", grid=(nstep,),
        in_specs=[tok, tok, tok,
                  pl.BlockSpec((2 * SCAN_PAIRS, DN_HEADS, DN_HEAD_DIM, DN_HEAD_DIM), lambda n: (rev(n), 0, 0, 0)),
                  tok, tok, tok, hm, sc, pl.BlockSpec((rows_step, DN_W), lambda n: (rev(n), 3)), _full((1, 128)),
                  pl.BlockSpec(memory_space=pl.ANY)],
        out_specs=[tok] * 5 + [hm, sc, _full((8, 128))],
        out_shape=[jax.ShapeDtypeStruct((t, DN_W), F32)] + [jax.ShapeDtypeStruct((t, DN_W), MXU_DTYPE)] * 2
                  + [jax.ShapeDtypeStruct((t, DN_W), F32)] * 2 + [jax.ShapeDtypeStruct((DN_HEADS, t, PAIR), F32),
                   jax.ShapeDtypeStruct((2 * npair, DN_HEADS, 8, 128), F32), jax.ShapeDtypeStruct((8, 128), F32)],
        scratch_shapes=[pltpu.VMEM((DN_HEADS, DN_HEAD_DIM, DN_HEAD_DIM), F32), pltpu.VMEM((SCAN_PAIRS * PAIR, DN_W), F32)],
        compiler_params=_params("arbitrary"),
    )(dout, o, vnew, sall, w, qg, kd, a_qk, dlast, pd, dn_w, dep)


def _dn_intra_bwd(qn, kn, vc, gc, gr, tinv, a_qk, du, dw, dqg, dkd, da_qk, ddlast, dlast, dep):
    t = qn.shape[0]
    npair = t // PAIR

    def body(q_ref, k_ref, v_ref, gc_ref, gr_ref, ti_ref, a_ref, du_ref, dw_ref, dqg_ref, dkd_ref, da_ref, ddl_ref, dl_ref,
             dep_ref, dq_ref, dk_ref, dv_ref, dg_ref):
        low, strict = _pair_masks()
        lane = lax.broadcasted_iota(jnp.int32, (PAIR, 128), 1)
        rowi = lax.broadcasted_iota(jnp.int32, (PAIR, 1), 0)
        rsum = lambda v: jnp.sum(v, axis=-1, keepdims=True)
        items = []
        for p in range(INTRA_PAIRS):
            rows = slice(p * PAIR, (p + 1) * PAIR)
            gc_v = gc_ref[rows, :]
            cum_c, cum_r = _pair_cumsums(gc_v, gr_ref[:, rows], low)
            for h in range(DN_HEADS):
                hs = slice(h * DN_HEAD_DIM, (h + 1) * DN_HEAD_DIM)
                items.append((p, h, rows, hs, _pair_gates(gc_v, cum_c, cum_r, low, h)))
        dtis, lmats, dvbs, dkbgs = [], [], [], []
        for p, h, rows, hs, (beta, gam, dm, gl) in items:
            k, tinv = k_ref[rows, hs], ti_ref[h, rows, :]
            kb = k * beta
            dtis.append(_dot(du_ref[rows, hs], v_ref[rows, hs] * beta, NT)
                        + _dot(dw_ref[rows, hs], kb * jnp.exp(gam), NT))
            lmats.append(jnp.where(strict, _dot(kb, k, NT) * dm, 0.0))
            dvbs.append(_dot(tinv, du_ref[rows, hs], TN))
            dkbgs.append(_dot(tinv, dw_ref[rows, hs], TN))
        xs = [_dot(ti_ref[h, rows, :], dti, TN) for (p, h, rows, hs, g), dti in zip(items, dtis)]
        dls = [jnp.where(strict, -_dot(x, ti_ref[h, rows, :], NT), 0.0) for (p, h, rows, hs, g), x in zip(items, xs)]
        dgam_all = [jnp.zeros((PAIR, 128), F32) for _ in range(INTRA_PAIRS)]
        dbeta_all = [jnp.zeros((PAIR, 128), F32) for _ in range(INTRA_PAIRS)]
        for (p, h, rows, hs, (beta, gam, dm, gl)), dl, lmat, dvb, dkbg in zip(items, dls, lmats, dvbs, dkbgs):
            q, k, v = q_ref[rows, hs], k_ref[rows, hs], v_ref[rows, hs]
            a = a_ref[h, rows, :]
            dqg, dkd = dqg_ref[rows, hs], dkd_ref[rows, hs]
            kb = k * beta
            eg = jnp.exp(gam)
            ekd = jnp.exp(gl - gam)
            dmm = dl * dm
            dam = jnp.where(low, da_ref[h, rows, :], 0.0)
            dn = dam * dm
            e = dl * lmat + dam * a
            dkb = _dot(dmm, k) + dkbg * eg
            dk_ref[rows, hs] = _dot(dmm, kb, TN) + _dot(dn, q, TN) + dkd * ekd + dkb * beta
            dq_ref[rows, hs] = _dot(dn, k) + dqg * eg
            dv_ref[rows, hs] = dvb * beta
            t_kd = rsum(dkd * (k * ekd))
            dgam = rsum(e) - rsum(e.T) + rsum(dqg * (q * eg)) + rsum(dkbg * (kb * eg)) - t_kd
            for c in range(2):
                crows = slice(c * DN_CHUNK, (c + 1) * DN_CHUNK)
                dgl = (jnp.sum(t_kd[crows, :], axis=0, keepdims=True)
                       + ddl_ref[2 * p + c, h][0:1, 0:1] * dl_ref[2 * p + c, h][0:1, 0:1])
                dgam = dgam + jnp.where(rowi == (c + 1) * DN_CHUNK - 1, dgl, 0.0)
            dgam_all[p] += jnp.where(lane == DN_HEADS + h, dgam, 0.0)
            dbeta_all[p] += jnp.where(lane == h, rsum(dkb * k) + rsum(dvb * v), 0.0)
        for p in range(INTRA_PAIRS):
            dg_ref[p * PAIR:(p + 1) * PAIR, :] = dbeta_all[p] + _dot(low.astype(F32), dgam_all[p], TN, HI)

    rows_step = INTRA_PAIRS * PAIR
    tok = lambda w: pl.BlockSpec((rows_step, w), lambda n: (n, 0))
    hm = pl.BlockSpec((DN_HEADS, rows_step, PAIR), lambda n: (0, n, 0))
    sc = pl.BlockSpec((2 * INTRA_PAIRS, DN_HEADS, 8, 128), lambda n: (n, 0, 0, 0))
    return pl.pallas_call(
        body, name="dn_intra_bwd", grid=(npair // INTRA_PAIRS,),
        in_specs=[tok(DN_W), tok(DN_W), tok(DN_W), tok(128), pl.BlockSpec((8, rows_step), lambda n: (0, n)), hm, hm,
                  tok(DN_W), tok(DN_W), tok(DN_W), tok(DN_W), hm, sc, sc, pl.BlockSpec(memory_space=pl.ANY)],
        out_specs=[tok(DN_W), tok(DN_W), tok(DN_W), tok(128)],
        out_shape=[jax.ShapeDtypeStruct((t, DN_W), F32)] * 3 + [jax.ShapeDtypeStruct((t, 128), F32)],
        compiler_params=_params("parallel"),
    )(qn, kn, vc, gc, gr, tinv, a_qk, du, dw, dqg, dkd, da_qk, ddlast, dlast, dep)


def _dn_prep_bwd(pd, conv_w, al_vec, dt_vec, dqn, dkn, dvc, dgc, dz, tm):
    t = pd.shape[0]
    nt = t // tm
    tile = lambda i: nt - 1 - i

    def body(cur_ref, prev_ref, ba_ref, cw_ref, al_ref, dt_ref, dq_ref, dk_ref, dv_ref, dg_ref, dz_ref,
             o_ref, accw_ref, accg_ref, xs_ref, dc_ref, ds_ref, carry_ref):
        @pl.when(pl.program_id(0) == 0)
        def _():
            accw_ref[...] = jnp.zeros_like(accw_ref)
            accg_ref[...] = jnp.zeros_like(accg_ref)
            carry_ref[...] = jnp.zeros_like(carry_ref)

        _conv_window(cur_ref, prev_ref, xs_ref, tm, tile(pl.program_id(0)) > 0)
        taps = [xs_ref[pl.ds(HALO - DN_CONV + 1 + j, tm), :] for j in range(DN_CONV)]
        y = cw_ref[0:1, :] * taps[0]
        for j in range(1, DN_CONV):
            y += cw_ref[j:j + 1, :] * taps[j]
        sg = _sigmoid(y)
        c = y * sg
        for h in range(DN_HEADS):
            qs = slice(h * DN_HEAD_DIM, (h + 1) * DN_HEAD_DIM)
            ksl = slice(DN_W + h * DN_HEAD_DIM, DN_W + (h + 1) * DN_HEAD_DIM)
            for src, sl, scale in ((dq_ref, qs, DN_SCALE), (dk_ref, ksl, 1.0)):
                xh = c[:, sl]
                r = lax.rsqrt(jnp.sum(xh * xh, axis=-1, keepdims=True) + EPS)
                unit = xh * r
                dn = src[:, qs] * scale
                dc_ref[:, sl] = r * (dn - unit * jnp.sum(dn * unit, axis=-1, keepdims=True))
        dc_ref[:, 2 * DN_W:3 * DN_W] = dv_ref[...]
        dy = dc_ref[...] * (sg + y * sg * (1.0 - sg))
        for j in range(DN_CONV):
            accw_ref[j:j + 1, :] += jnp.sum(dy * taps[j], axis=0, keepdims=True)
        ds_ref[0:tm, :] = dy
        ds_ref[tm:tm + HALO, :] = carry_ref[...]
        carry_ref[...] = ds_ref[0:HALO, :]
        dx = cw_ref[0:1, :] * ds_ref[pl.ds(DN_CONV - 1, tm), :]
        for j in range(1, DN_CONV):
            dx += cw_ref[j:j + 1, :] * ds_ref[pl.ds(DN_CONV - 1 - j, tm), :]

        beta, g, pre = _gate_values(ba_ref[...], al_ref[...], dt_ref[...])
        dgb = dg_ref[...]
        lane = lax.broadcasted_iota(jnp.int32, dgb.shape, 1)
        is_b, is_a = lane < DN_HEADS, (lane >= DN_HEADS) & (lane < 2 * DN_HEADS)
        dpre = dgb * (-jnp.exp(al_ref[...])) * _sigmoid(pre)
        dba = jnp.where(is_b, dgb * beta * (1.0 - beta), jnp.where(is_a, dpre, 0.0))
        accg_ref[0:1, :] += jnp.sum(jnp.where(is_a, dgb * g, 0.0), axis=0, keepdims=True)
        accg_ref[1:2, :] += jnp.sum(jnp.where(is_a, dpre, 0.0), axis=0, keepdims=True)
        o_ref[:, 0:CONV_CH] = dx.astype(o_ref.dtype)
        o_ref[:, CONV_CH:CONV_CH + DN_W] = dz_ref[...].astype(o_ref.dtype)
        o_ref[:, CONV_CH + DN_W:DN_COLS] = dba.astype(o_ref.dtype)

    tok = lambda w: pl.BlockSpec((tm, w), lambda i: (tile(i), 0))
    return pl.pallas_call(
        body, name="dn_prep_bwd", grid=(nt,),
        in_specs=_dn_prep_specs(tm, tile) + [tok(DN_W), tok(DN_W), tok(DN_W), tok(128), tok(DN_W)],
        out_specs=[tok(DN_COLS), _full((8, CONV_CH)), _full((8, 128))],
        out_shape=[jax.ShapeDtypeStruct((t, DN_COLS), MXU_DTYPE),
                   jax.ShapeDtypeStruct((8, CONV_CH), F32), jax.ShapeDtypeStruct((8, 128), F32)],
        scratch_shapes=[pltpu.VMEM((HALO + tm, CONV_CH), F32), pltpu.VMEM((tm, CONV_CH), F32),
                        pltpu.VMEM((tm + HALO, CONV_CH), F32), pltpu.VMEM((HALO, CONV_CH), F32)],
        compiler_params=_params("arbitrary"),
    )(pd, pd, pd, conv_w, al_vec, dt_vec, dqn, dkn, dvc, dgc, dz)


def _pad_lanes(v, offset=0):
    return jnp.zeros((1, 128), F32).at[0, offset:offset + v.shape[0]].set(v.astype(F32))


class _LocalReducer:
    def start(self, grads):
        return jnp.zeros((8, 128), F32)

    def middle(self, after):
        return jnp.zeros((8, 128), F32)

    def finish(self, after):
        return None


def _local_step(x, p, tgt, sm, w, late, reducer):
    t = x.shape[0]
    tm = min(512, t // 2)
    tm_s = min(512, t // 2)
    tw = min(1024, t // 2)
    tw_ff = min(2048, t // 2)

    w_in_t = w["w_in_t"]
    wa_t = w_in_t[:ATTN_Q + 2 * ATTN_KV]
    wd_t = jnp.pad(w_in_t[ATTN_Q + 2 * ATTN_KV:], ((0, DN_COLS - (D_IN - ATTN_Q - 2 * ATTN_KV)), (0, 0)))
    conv_w = w["conv_w"]
    al_vec, dt_vec = _pad_lanes(sm["a_log"], DN_HEADS), _pad_lanes(sm["dt_bias"], DN_HEADS)
    sinks_vec = _pad_lanes(sm["sinks"])
    dn_w = sm["dn_norm"].reshape(1, 128)
    row = lambda v: v.reshape(1, D_MODEL)
    cos, sin = _rope_tables(t)

    u, pa, pd = _inproj(x, row(sm["norm_mix"]), wa_t, wd_t, tm_s)
    ao = _attn_fwd(pa, cos, sin, sinks_vec)
    qn, kn, vc, gc, gr = _dn_prep(pd, conv_w, al_vec, dt_vec, tm_s)
    uu, ww, qg, kd, a_qk, tinv, dlast = _dn_intra(qn, kn, vc, gc, gr)
    dn_out, o, vnew, sall = _dn_scan_fwd(uu, ww, qg, kd, a_qk, dlast, pd, dn_w)
    w_o, late_rest = late(dn_out)
    wo_a, wo_d = w_o[:ATTN_Q], w_o[ATTN_Q:]
    h1 = _oproj(x, ao, dn_out, wo_a, wo_d, tm)
    w = dict(w, **late_rest(h1))
    w_proj = jnp.transpose(w["w_proj4"], (1, 0, 2)).reshape(PLE_DIM, D_MODEL)
    m, r, h2 = _mlp_fwd(h1, row(sm["norm_mlp"]), w["w_up4"], w["w_down"], tw)
    dh2, dh2b, dgp, dpp, n3, pb, acc_ple = _ple_loss(h2, p, tgt, row(sm["norm_ple"]), row(sm["norm_final"]),
                                                     w["w_gate"], w_proj, tm_s)
    g_w_gate = _wgrad(n3, dgp, "wgrad_gate", D_MODEL, D_MODEL, tw)
    g_w_proj = _wgrad(pb, dpp, "wgrad_proj", PLE_DIM, D_MODEL, tw)
    da, dh1, dh1b, dao, ddn, acc_mlp = _mlp_bwd(dh2, dh2b, r, h1, row(sm["norm_mlp"]), w["w_up4"], w["w_down"],
                                                wo_a, wo_d, tm)
    g_w_up4 = _wgrad(m, da, "wgrad_up", D_MODEL, FF_BLOCK, tw_ff, stacked=True)
    g_w_down = _wgrad(r, dh2b, "wgrad_down", FF_BLOCK, D_MODEL, tw_ff,
                      prep=lambda rv: jnp.square(rv.astype(F32)).astype(MXU_DTYPE))
    g_w_o = _wgrad_cat([ao, dn_out], [dh1b], "wgrad_o", tw)
    early = dict(w_up4=g_w_up4, w_down=g_w_down, w_gate=g_w_gate, w_proj=g_w_proj, w_o=g_w_o)
    dep = reducer.start(early)
    dz, du, dw, dqg, dkd, da_qk, ddlast, acc_dn = _dn_scan_bwd(ddn, o, vnew, sall, ww, qg, kd, a_qk, dlast, pd, dn_w,
                                                               dep)
    dep = reducer.middle(du)
    dqn, dkn, dvc, dgc = _dn_intra_bwd(qn, kn, vc, gc, gr, tinv, a_qk, du, dw, dqg, dkd, da_qk, ddlast, dlast, dep)
    d_dn, acc_conv, acc_gate = _dn_prep_bwd(pd, conv_w, al_vec, dt_vec, dqn, dkn, dvc, dgc, dz, tm_s)
    dq, dk, dv, acc_attn = _attn_bwd(pa, cos, sin, sinks_vec, dao)
    reducer.finish(dq)
    wq_t, wk_t, wv_t = wa_t[:ATTN_Q], wa_t[ATTN_Q:ATTN_Q + ATTN_KV], wa_t[ATTN_Q + ATTN_KV:]
    dx, acc_mix = _inproj_bwd(x, dh1, row(sm["norm_mix"]), [dq, dk, dv, d_dn], [wq_t, wk_t, wv_t, wd_t], tm_s)

    g_w_in_t = _wgrad_cat([dq, dk, dv, d_dn], [u], "wgrad_in", tw)
    grads = dict(early, w_in_t=g_w_in_t)
    sums = dict(loss=acc_ple[2, 0], norm_final=acc_ple[0], norm_ple=acc_ple[1], norm_mlp=acc_mlp[0], norm_mix=acc_mix[0],
                dn_norm=acc_dn[0], sinks=acc_attn[0, :ATTN_HEADS], a_log=acc_gate[0, DN_HEADS:2 * DN_HEADS],
                dt_bias=acc_gate[1, DN_HEADS:2 * DN_HEADS], conv_w=acc_conv[:DN_CONV])
    return sums, dx, grads


MESH = pl.DeviceIdType.MESH
ANY = pl.BlockSpec(memory_space=pl.ANY)
N_CHIPS = 4
N_DEV = 8


def _place():
    x, y, c = lax.axis_index("x"), lax.axis_index("y"), lax.axis_index("c")
    chips = [(1 - x, y), (x, 1 - y), (1 - x, 1 - y)]
    return x, y, c, chips


def _gather_weights(shards, conv_s):
    n = len(shards)
    per = 7

    def body(*refs):
        in_refs, conv_ref = refs[:n], refs[n]
        out_refs, conv_out = refs[n + 1:2 * n + 1], refs[2 * n + 1]
        send_sems, recv_sems = refs[2 * n + 2:]
        x, y, c, chips = _place()
        sibling = (x, y, 1 - c)

        def blk(a, px, py, pc):
            hr = in_refs[a].shape[0] // 2
            return out_refs[a].at[2 * px + py, pl.ds(pc * hr, hr), :]

        def mine(a):
            hr = in_refs[a].shape[0] // 2
            return in_refs[a].at[pl.ds(c * hr, hr), :]

        def rcopy(a, k, block, to, src=None):
            return pltpu.make_async_remote_copy(
                src_ref=blk(a, *block) if src is None else src, dst_ref=blk(a, *block),
                send_sem=send_sems.at[per * a + k], recv_sem=recv_sems.at[per * a + k],
                device_id=to, device_id_type=MESH)

        def whole(a, to):
            return pltpu.make_async_remote_copy(
                src_ref=in_refs[a], dst_ref=out_refs[a].at[2 * x + y],
                send_sem=send_sems.at[per * a], recv_sem=recv_sems.at[per * a], device_id=to, device_id_type=MESH)

        def ccopy(j, to):
            return pltpu.make_async_remote_copy(
                src_ref=conv_ref, dst_ref=conv_out.at[2 * x + y],
                send_sem=send_sems.at[per * n + j], recv_sem=recv_sems.at[per * n + j],
                device_id=to, device_id_type=MESH)

        started = []
        for a in range(n):
            first = [whole(a, sibling)]
            first += [rcopy(a, 1 + j, (x, y, c), (*chip, c), src=mine(a)) for j, chip in enumerate(chips)]
            for cp in first:
                cp.start()
            started += first
        conv_sends = [ccopy(j, (*chip, c)) for j, chip in enumerate(chips)] + [ccopy(3, sibling)]
        for cp in conv_sends:
            cp.start()
        started += conv_sends
        for a in range(n):
            for j, chip in enumerate(chips):
                rcopy(a, 1 + j, (*chip, c), (x, y, c)).wait_recv()
                fwd = rcopy(a, 4 + j, (*chip, c), sibling)
                fwd.start()
                started.append(fwd)
        for a in range(n):
            whole(a, sibling).wait_recv()
            for j, chip in enumerate(chips):
                rcopy(a, 4 + j, (*chip, 1 - c), (x, y, c)).wait_recv()
        for j, chip in enumerate(chips + [(x, y)]):
            pltpu.make_async_remote_copy(
                src_ref=conv_ref, dst_ref=conv_out.at[2 * chip[0] + chip[1]],
                send_sem=send_sems.at[per * n + j], recv_sem=recv_sems.at[per * n + j],
                device_id=sibling, device_id_type=MESH).wait_recv()
        for cp in started:
            cp.wait_send()

    nsem = per * n + 4
    out_shape = [jax.ShapeDtypeStruct((N_CHIPS,) + s.shape, s.dtype) for s in shards]
    out_shape.append(jax.ShapeDtypeStruct((N_CHIPS,) + conv_s.shape, conv_s.dtype))
    return pl.pallas_call(
        body, name="gather_weights", in_specs=[ANY] * (n + 1), out_specs=[ANY] * (n + 1), out_shape=out_shape,
        scratch_shapes=[pltpu.SemaphoreType.DMA((nsem,)), pltpu.SemaphoreType.DMA((nsem,))],
    )(*shards, conv_s)


HBM = pl.BlockSpec(memory_space=pltpu.HBM)
SEM = pl.BlockSpec(memory_space=pltpu.SEMAPHORE)
EFFECT = pltpu.SideEffectType.DATAFLOW_SIDE_EFFECTING
LATE_COPIES = 7


def _late_copies(in_refs, land_refs, send_sems, recv_sems, only=None):
    x, y, c, chips = _place()
    sends, arrivals = [], []
    for a, (src, land) in enumerate(zip(in_refs, land_refs)):
        if only is not None and a not in only:
            continue
        hr = src.shape[0] // 2
        base = LATE_COPIES * a

        def cp(src_ref, dst_ref, s_idx, r_idx, to):
            return pltpu.make_async_remote_copy(src_ref=src_ref, dst_ref=dst_ref, send_sem=send_sems.at[base + s_idx],
                                                recv_sem=recv_sems.at[base + r_idx], device_id=to, device_id_type=MESH)

        sends.append(cp(src, land.at[2 * x + y], 0, 0, (x, y, 1 - c)))
        arrivals.append(cp(src, land.at[2 * x + y], 0, 0, (x, y, 1 - c)))
        for j, chip in enumerate(chips):
            for pc in range(2):
                half = src.at[pl.ds(c * hr, hr), :]
                sends.append(cp(half, land.at[2 * x + y, pl.ds(c * hr, hr), :], 1 + 2 * j + pc, 1 + 2 * j + c, (*chip, pc)))
                arrivals.append(cp(half, land.at[2 * chip[0] + chip[1], pl.ds(pc * hr, hr), :], 1 + 2 * j + pc,
                                   1 + 2 * j + pc, (*chip, pc)))
    return sends, arrivals


def _copies_start(name, build, nsem, srcs, land_shapes, after):
    n = len(srcs)

    def body(*refs):
        sends, _ = build(refs[:n], refs[n:2 * n], refs[2 * n + 1], refs[2 * n + 2])
        for cp in sends:
            cp.start()
        refs[-1][...] = jnp.zeros_like(refs[-1])

    lands = [pltpu.with_memory_space_constraint(lax.empty(s.shape, s.dtype), pltpu.HBM) for s in land_shapes]
    ins = [pltpu.with_memory_space_constraint(s, pltpu.HBM) for s in srcs]
    out = pl.pallas_call(
        body, name=name,
        out_shape=(pltpu.SemaphoreType.DMA((nsem,)), pltpu.SemaphoreType.DMA((nsem,)),
                   *[pltpu.HBM(s.shape, s.dtype) for s in srcs], *[pltpu.HBM(s.shape, s.dtype) for s in land_shapes],
                   jax.ShapeDtypeStruct((8, 128), F32)),
        in_specs=[HBM] * (2 * n) + [ANY],
        out_specs=(SEM, SEM, *[HBM] * (2 * n), pl.BlockSpec(memory_space=pltpu.VMEM)),
        input_output_aliases={i: 2 + i for i in range(2 * n)},
        compiler_params=pltpu.CompilerParams(has_side_effects=EFFECT),
    )(*ins, *lands, after)
    return out[0], out[1], out[2:2 + n], out[2 + n:2 + 2 * n], out[-1]


def _copies_wait(name, build, started, after):
    send_sems, recv_sems, srcs, lands, _ = started
    n = len(srcs)

    def body(*refs):
        sends, arrivals = build(refs[:n], refs[n:2 * n], refs[2 * n], refs[2 * n + 1])
        for cp in sends:
            cp.wait_send()
        for cp in arrivals:
            cp.wait_recv()

    out = pl.pallas_call(
        body, name=name,
        out_shape=(*[pltpu.HBM(s.shape, s.dtype) for s in srcs], *[pltpu.HBM(l.shape, l.dtype) for l in lands]),
        in_specs=[HBM] * (2 * n) + [SEM, SEM, ANY],
        out_specs=tuple([HBM] * (2 * n)),
        input_output_aliases={i: i for i in range(2 * n)},
        compiler_params=pltpu.CompilerParams(has_side_effects=EFFECT),
    )(*srcs, *lands, send_sems, recv_sems, after)
    return out[:n], out[n:]


def _exchange_copies(g_refs, got_refs, send_sems, recv_sems):
    x, y, c, _ = _place()
    sends, arrivals = [], []
    for a, (g, got) in enumerate(zip(g_refs, got_refs)):
        hr = g.shape[1] // 2
        cp = pltpu.make_async_remote_copy(
            src_ref=g.at[:, pl.ds((1 - c) * hr, hr), :], dst_ref=got, send_sem=send_sems.at[a],
            recv_sem=recv_sems.at[a], device_id=(x, y, 1 - c), device_id_type=MESH)
        sends.append(cp)
        arrivals.append(cp)
    return sends, arrivals


def _scatter_copies(s_refs, got_refs, send_sems, recv_sems):
    x, y, c, chips = _place()
    sends, arrivals = [], []
    for a, (s16, got) in enumerate(zip(s_refs, got_refs)):
        for j, chip in enumerate(chips):
            cp = pltpu.make_async_remote_copy(
                src_ref=s16.at[2 * chip[0] + chip[1]], dst_ref=got.at[j], send_sem=send_sems.at[3 * a + j],
                recv_sem=recv_sems.at[3 * a + j], device_id=(*chip, c), device_id_type=MESH)
            sends.append(cp)
            arrivals.append(cp)
    return sends, arrivals


def _share_halves(name, bufs, dep):
    n = len(bufs)

    def body(*refs):
        out_refs = refs[n + 1:2 * n + 1]
        send_sems, recv_sems = refs[2 * n + 1:]
        x, y, c, _ = _place()
        remote = [pltpu.make_async_remote_copy(
            src_ref=out_refs[a].at[c], dst_ref=out_refs[a].at[c], send_sem=send_sems.at[a], recv_sem=recv_sems.at[a],
            device_id=(x, y, 1 - c), device_id_type=MESH) for a in range(n)]
        for cp in remote:
            cp.start()
        for a in range(n):
            pltpu.make_async_remote_copy(
                src_ref=out_refs[a].at[c], dst_ref=out_refs[a].at[1 - c], send_sem=send_sems.at[a],
                recv_sem=recv_sems.at[a], device_id=(x, y, 1 - c), device_id_type=MESH).wait_recv()
        for cp in remote:
            cp.wait_send()

    return pl.pallas_call(
        body, name=name, in_specs=[ANY] * (n + 1), out_specs=[ANY] * n,
        out_shape=[jax.ShapeDtypeStruct(b.shape, b.dtype) for b in bufs],
        input_output_aliases={a: a for a in range(n)},
        scratch_shapes=[pltpu.SemaphoreType.DMA((n,)), pltpu.SemaphoreType.DMA((n,))],
    )(*bufs, dep)


SMALL_ROWS, SMALL_COLS = 16, CONV_CH


def _allreduce_small(block):
    m_per, ncol = block.shape

    def body(x_ref, sum_ref, all_ref, send_sems, recv_sems, local_sem):
        x, y, c, chips = _place()
        me, sibling = (x, y, c), (x, y, 1 - c)

        def rows(px, py, pc):
            return all_ref.at[pl.ds((4 * px + 2 * py + pc) * m_per, m_per), :]

        def copy(k, block_of, to, src=None):
            return pltpu.make_async_remote_copy(
                src_ref=rows(*block_of) if src is None else src, dst_ref=rows(*block_of),
                send_sem=send_sems.at[k], recv_sem=recv_sems.at[k], device_id=to, device_id_type=MESH)

        mine = pltpu.make_async_copy(x_ref, rows(*me), local_sem)
        mine.start()
        first = [copy(0, me, sibling, src=x_ref)]
        first += [copy(1 + j, me, (*chip, c), src=x_ref) for j, chip in enumerate(chips)]
        for cp in first:
            cp.start()
        passed = [copy(4 + j, (*chip, c), sibling) for j, chip in enumerate(chips)]
        for j, chip in enumerate(chips):
            copy(1 + j, (*chip, c), me).wait_recv()
            passed[j].start()
        copy(0, sibling, me).wait_recv()
        for j, chip in enumerate(chips):
            copy(4 + j, (*chip, 1 - c), me).wait_recv()
        for cp in first + passed:
            cp.wait_send()
        mine.wait()
        total = all_ref[0:m_per, :]
        for d in range(1, N_DEV):
            total = total + all_ref[d * m_per:(d + 1) * m_per, :]
        sum_ref[...] = total

    vm = pl.BlockSpec(memory_space=pltpu.VMEM)
    return pl.pallas_call(
        body, name="allreduce_small", in_specs=[vm], out_specs=vm,
        out_shape=jax.ShapeDtypeStruct((m_per, ncol), F32),
        scratch_shapes=[pltpu.VMEM((N_DEV * m_per, ncol), F32), pltpu.SemaphoreType.DMA((7,)),
                        pltpu.SemaphoreType.DMA((7,)), pltpu.SemaphoreType.DMA],
    )(block)


def _row_tile(rows, cols):
    tile = rows
    while tile * cols * 4 > (1 << 20) and tile % 16 == 0:
        tile //= 2
    return tile


def _elementwise(fn, name, ins, out_dtypes, dep):
    rows, cols = ins[0].shape
    tile = _row_tile(rows, cols)

    def body(*refs):
        outs = fn(*[r[...] for r in refs[:len(ins)]])
        for o_ref, o in zip(refs[len(ins) + 1:], outs):
            o_ref[...] = o.astype(o_ref.dtype)

    if tile * cols * 4 > (1 << 21) and cols % 512 == 0:
        spec = pl.BlockSpec((rows, 256), lambda i: (0, i))
        steps = cols // 256
    else:
        spec = pl.BlockSpec((tile, cols), lambda i: (i, 0))
        steps = rows // tile
    return pl.pallas_call(
        body, name=name, grid=(steps,), in_specs=[spec] * len(ins) + [pl.BlockSpec(memory_space=pl.ANY)],
        out_specs=[spec] * len(out_dtypes),
        out_shape=[jax.ShapeDtypeStruct((rows, cols), d) for d in out_dtypes],
        compiler_params=_params("parallel"),
    )(*ins, dep)


def _adamw_tile(w, g, m, v):
    m = ADAM_B1 * m + (1.0 - ADAM_B1) * g
    v = ADAM_B2 * v + (1.0 - ADAM_B2) * jnp.square(g)
    m_hat = m / (1.0 - ADAM_B1 ** ADAM_STEP)
    v_hat = v / (1.0 - ADAM_B2 ** ADAM_STEP)
    delta = -ADAM_LR * (m_hat / (jnp.sqrt(v_hat) + ADAM_EPS) + ADAM_WD * w)
    return delta, m, v


def _adamw(name, w, g, m, v, dep):
    return _elementwise(_adamw_tile, name, [w, g, m, v], [F32, F32, F32], dep)


def _chip_sum(name, g4, got, place):
    nchip, hr, cols = got.shape
    tile = _row_tile(hr, cols)
    nblk = hr // tile

    def body(pl_ref, g_ref, o_ref, s32_ref, s16_ref):
        s = g_ref[...] + o_ref[...]
        s16_ref[...] = s.astype(BF16)

        @pl.when(pl.program_id(1) == pl_ref[0])
        def _():
            s32_ref[...] = s

    spec = pl.BlockSpec((None, tile, cols), lambda i, k, pr: (k, i, 0))
    return pl.pallas_call(
        body, name=name,
        grid_spec=pltpu.PrefetchScalarGridSpec(
            num_scalar_prefetch=1, grid=(nblk, nchip),
            in_specs=[pl.BlockSpec((None, tile, cols), lambda i, k, pr: (k, pr[1] * nblk + i, 0)), spec],
            out_specs=[pl.BlockSpec((tile, cols), lambda i, k, pr: (i, 0)), spec]),
        out_shape=[jax.ShapeDtypeStruct((hr, cols), F32), jax.ShapeDtypeStruct(got.shape, BF16)],
        compiler_params=_params("parallel", "arbitrary"),
    )(place, g4, got)


def _mesh_sum(name, s32, got, place):
    hr, cols = s32.shape
    tile = _row_tile(hr, cols)

    def body(pl_ref, own_ref, g0_ref, g1_ref, g2_ref, o_ref):
        o_ref[...] = ((own_ref[...] + g0_ref[...].astype(F32)) + g1_ref[...].astype(F32)) + g2_ref[...].astype(F32)

    slab = lambda j: pl.BlockSpec((None, tile, cols), lambda i, pr: (j, i, 0))
    return pl.pallas_call(
        body, name=name,
        grid_spec=pltpu.PrefetchScalarGridSpec(
            num_scalar_prefetch=1, grid=(hr // tile,),
            in_specs=[pl.BlockSpec((tile, cols), lambda i, pr: (i, 0)), slab(0), slab(1), slab(2)],
            out_specs=pl.BlockSpec((None, tile, cols), lambda i, pr: (pr[1], i, 0))),
        out_shape=jax.ShapeDtypeStruct((2, hr, cols), F32),
        compiler_params=_params("parallel"),
    )(place, s32, got, got, got)


def _place_operand():
    return jnp.stack([2 * lax.axis_index("x") + lax.axis_index("y"), lax.axis_index("c")]).astype(jnp.int32)


W_IN_ROWS = 720
W_IN_GATHER_ROWS = 736


def _per_chip(name, g):
    if name == "w_in_t":
        rows = D_IN // N_CHIPS
        return jnp.stack([lax.slice_in_dim(g, rows * k, rows * k + W_IN_ROWS) for k in range(N_CHIPS)])
    if name == "w_proj":
        return jnp.transpose(g.reshape(PLE_DIM, N_CHIPS, D_MODEL // N_CHIPS), (1, 0, 2))
    if name == "w_up4":
        return g
    return g.reshape(N_CHIPS, g.shape[0] // N_CHIPS, g.shape[1])


class _EarlyReducer:
    def __init__(self, tag):
        self.tag = tag

    def start(self, grads):
        self.names = list(grads)
        self.place = _place_operand()
        slabs = [_per_chip(k, grads[k]) for k in self.names]
        halves = [jax.ShapeDtypeStruct((s.shape[0], s.shape[1] // 2, s.shape[2]), F32) for s in slabs]
        self.a = _copies_start(self.tag + "exchange_start", _exchange_copies, len(slabs), slabs, halves,
                               slabs[0][0, :8, :128])
        return self.a[-1]

    def middle(self, after):
        slabs, got = _copies_wait(self.tag + "exchange_wait", _exchange_copies, self.a, after)
        self.sums = [_chip_sum(self.tag + "chip_sum_" + k, s, g, self.place) for k, s, g in zip(self.names, slabs, got)]
        s16 = [s[1] for s in self.sums]
        lands = [jax.ShapeDtypeStruct((3,) + s.shape[1:], BF16) for s in s16]
        self.b = _copies_start(self.tag + "scatter_start", _scatter_copies, 3 * len(s16), s16, lands,
                               self.sums[0][0][:8, :128])
        return self.b[-1]

    def finish(self, after):
        _, got = _copies_wait(self.tag + "scatter_wait", _scatter_copies, self.b, after)
        self.bufs = {k: _mesh_sum(self.tag + "mesh_sum_" + k, s[0], g, self.place)
                     for k, s, g in zip(self.names, self.sums, got)}


def kernel(x, p, norm_mix, w_in, conv_w, a_log, dt_bias, dn_norm, sinks, w_o, norm_mlp, w_up, w_down, norm_ple, w_ple_gate, w_ple_proj, norm_final, loss_target, m_norm_mix, m_w_in, m_conv_w, m_a_log, m_dt_bias, m_dn_norm, m_sinks, m_w_o, m_norm_mlp, m_w_up, m_w_down, m_norm_ple, m_w_ple_gate, m_w_ple_proj, m_norm_final, v_norm_mix, v_w_in, v_conv_w, v_a_log, v_dt_bias, v_dn_norm, v_sinks, v_w_o, v_norm_mlp, v_w_up, v_w_down, v_norm_ple, v_w_ple_gate, v_w_ple_proj, v_norm_final):
    chip = 2 * lax.axis_index("x") + lax.axis_index("y")
    big = dict(w_in=w_in[0], w_o=w_o[0], w_up=w_up[0], w_down=w_down[0], w_gate=w_ple_gate[0], w_proj=w_ple_proj[0])
    big_m = dict(w_in=m_w_in[0], w_o=m_w_o[0], w_up=m_w_up[0], w_down=m_w_down[0], w_gate=m_w_ple_gate[0], w_proj=m_w_ple_proj[0])
    big_v = dict(w_in=v_w_in[0], w_o=v_w_o[0], w_up=v_w_up[0], w_down=v_w_down[0], w_gate=v_w_ple_gate[0], w_proj=v_w_ple_proj[0])
    names = list(big)

    rows_in = D_IN // N_CHIPS
    w_in_shard_t = jnp.pad(big["w_in"].T.astype(BF16), ((0, W_IN_GATHER_ROWS - rows_in), (0, 0)))
    w_in_all, conv_all = _gather_weights([w_in_shard_t], conv_w[0])
    late_names = names[1:]
    late_shards = [big[k].astype(BF16) for k in late_names]
    gather = _copies_start("gather_start", _late_copies, LATE_COPIES * len(late_shards), late_shards,
                           [jax.ShapeDtypeStruct((N_CHIPS,) + s.shape, BF16) for s in late_shards], w_in_all)
    token = gather[-1]
    w = dict(w_in_t=jnp.concatenate([w_in_all[k, :rows_in] for k in range(N_CHIPS)], axis=0),
             conv_w=jnp.transpose(conv_all, (1, 0, 2)).reshape(DN_CONV, CONV_CH))
    sm = dict(norm_mix=norm_mix[0] + token[0, 0], a_log=a_log[0], dt_bias=dt_bias[0], dn_norm=dn_norm[0],
              sinks=sinks[0], norm_mlp=norm_mlp[0], norm_ple=norm_ple[0], norm_final=norm_final)

    def late(after):
        first = functools.partial(_late_copies, only=(0,))
        srcs, lands = _copies_wait("gather_wait_o", first, gather, after)

        def rest(after2):
            others = functools.partial(_late_copies, only=tuple(range(1, len(late_names))))
            gw = dict(zip(late_names, _copies_wait("gather_wait_rest", others, gather[:2] + (srcs, lands, None), after2)[1]))
            return dict(w_up4=gw["w_up"], w_down=gw["w_down"].reshape(D_FF, D_MODEL),
                        w_gate=gw["w_gate"].reshape(D_MODEL, D_MODEL), w_proj4=gw["w_proj"])

        return lands[0].reshape(D_MODEL, D_MODEL), rest

    reducer = _EarlyReducer("early_")
    sums, grad_x, g = _local_step(x[0], p[0, 0], loss_target[0], sm, w, late, reducer)

    last = _EarlyReducer("last_")
    dep_a = last.start({"w_in_t": g["w_in_t"]})

    row = lambda v: jnp.zeros((SMALL_COLS,), F32).at[:v.shape[0]].set(v)
    misc = jnp.zeros((SMALL_COLS,), F32).at[0:4].set(sums["a_log"]).at[4:8].set(sums["dt_bias"]) \
        .at[8:16].set(sums["sinks"]).at[128:256].set(sums["dn_norm"]).at[256].set(sums["loss"])
    small = jnp.concatenate([sums["conv_w"], jnp.stack([row(sums["norm_mix"]), row(sums["norm_mlp"]), row(sums["norm_ple"]),
                                                        row(sums["norm_final"]), misc]),
                             jnp.zeros((SMALL_ROWS - 9, SMALL_COLS), F32)], axis=0)
    tot = _allreduce_small(small + dep_a[0, 0])
    dep_b = last.middle(tot)
    grad_key = dict(w_o="w_o", w_up="w_up4", w_down="w_down", w_gate="w_gate", w_proj="w_proj")
    full = _share_halves("share_halves", [reducer.bufs[grad_key[k]] for k in late_names], dep_b)
    red = {k: f.reshape(-1, f.shape[-1]) for k, f in zip(late_names, full)}
    loss = tot[8, 256]
    ncw = CONV_CH // N_CHIPS

    def pack(cw, nmix, nmlp, nple, nfin, al, dtb, sk, dnn):
        misc_p = jnp.zeros((SMALL_COLS,), F32).at[0:4].set(al).at[4:8].set(dtb).at[8:16].set(sk).at[128:256].set(dnn)
        cw_p = jnp.zeros((DN_CONV, SMALL_COLS), F32).at[:, :ncw].set(cw)
        return jnp.concatenate([cw_p, jnp.stack([row(nmix), row(nmlp), row(nple), row(nfin), misc_p]),
                                jnp.zeros((SMALL_ROWS - 9, SMALL_COLS), F32)], axis=0)

    def unpack(buf):
        return dict(conv_w=buf[0:4, :ncw][None], norm_mix=buf[4, :D_MODEL][None], norm_mlp=buf[5, :D_MODEL][None],
                    norm_ple=buf[6, :D_MODEL][None], norm_final=buf[7, :D_MODEL], a_log=buf[8, 0:4][None],
                    dt_bias=buf[8, 4:8][None], sinks=buf[8, 8:16][None], dn_norm=buf[8, 128:256][None])

    g_conv_shard = lax.dynamic_slice(tot[0:4], (0, chip * ncw), (DN_CONV, ncw))
    g_small = pack(g_conv_shard, tot[4, :D_MODEL], tot[5, :D_MODEL], tot[6, :D_MODEL], tot[7, :D_MODEL],
                   tot[8, 0:4], tot[8, 4:8], tot[8, 8:16], tot[8, 128:256])
    w_small = pack(conv_w[0], norm_mix[0], norm_mlp[0], norm_ple[0], norm_final, a_log[0], dt_bias[0], sinks[0], dn_norm[0])
    m_small = pack(m_conv_w[0], m_norm_mix[0], m_norm_mlp[0], m_norm_ple[0], m_norm_final, m_a_log[0], m_dt_bias[0],
                   m_sinks[0], m_dn_norm[0])
    v_small = pack(v_conv_w[0], v_norm_mix[0], v_norm_mlp[0], v_norm_ple[0], v_norm_final, v_a_log[0], v_dt_bias[0],
                   v_sinks[0], v_dn_norm[0])

    ref_name = dict(w_in="w_in", w_o="w_o", w_up="w_up", w_down="w_down", w_gate="w_ple_gate", w_proj="w_ple_proj")
    out_g, out_d, out_m, out_v = {}, {}, {}, {}

    def update(k, dep):
        d_k, m_k, v_k = _adamw("adamw_" + k, big[k], red[k], big_m[k], big_v[k], dep)
        out_g[ref_name[k]], out_d[ref_name[k]] = red[k][None], d_k[None]
        out_m[ref_name[k]], out_v[ref_name[k]] = m_k[None], v_k[None]
        return d_k

    for k in late_names:
        done = update(k, dep_b)
    small_out = _adamw("adamw_small", w_small, g_small, m_small, v_small, dep_b)
    d_s, m_s, v_s = (unpack(b) for b in small_out)
    g_s = unpack(g_small)
    for src, dst in ((g_s, out_g), (d_s, out_d), (m_s, out_m), (v_s, out_v)):
        dst.update(src)
    last.finish(done + small_out[0][0:1, 0:1])
    (w_in_full,) = _share_halves("share_halves_w_in", [last.bufs["w_in_t"]], dep_b)
    g_t = w_in_full.reshape(W_IN_ROWS, D_MODEL)[:D_IN // N_CHIPS]
    d_t, m_t, v_t = _adamw("adamw_w_in", big["w_in"].T, g_t, big_m["w_in"].T, big_v["w_in"].T, dep_b)
    out_g["w_in"], out_d["w_in"], out_m["w_in"], out_v["w_in"] = g_t.T[None], d_t.T[None], m_t.T[None], v_t.T[None]
    order = ["norm_mix", "w_in", "conv_w", "a_log", "dt_bias", "dn_norm", "sinks", "w_o", "norm_mlp", "w_up", "w_down",
             "norm_ple", "w_ple_gate", "w_ple_proj", "norm_final"]
    return (loss, grad_x[None], *[out_g[k] for k in order], *[out_d[k] for k in order],
            *[out_m[k] for k in order], *[out_v[k] for k in order])
```

```python
import functools

import jax
import jax.numpy as jnp
from jax import lax
from jax.experimental import pallas as pl
from jax.experimental.pallas import tpu as pltpu

F32 = jnp.float32
BF16 = jnp.bfloat16
MXU_DTYPE = jnp.bfloat16
HI = lax.Precision.HIGHEST

D_MODEL = 1024
PLE_DIM = 256
ATTN_HEADS = 8
ATTN_KV_HEADS = 2
ATTN_GROUPS = ATTN_HEADS // ATTN_KV_HEADS
ATTN_HEAD_DIM = 64
ATTN_BLOCK = 128
ROPE_THETA = 10000.0
DN_HEADS = 4
DN_HEAD_DIM = 128
DN_CONV = 4
DN_CHUNK = 64
D_FF = 4 * D_MODEL
EPS = 1e-6
ATTN_Q = ATTN_HEADS * ATTN_HEAD_DIM
ATTN_KV = ATTN_KV_HEADS * ATTN_HEAD_DIM
DN_W = DN_HEADS * DN_HEAD_DIM
CONV_CH = 3 * DN_W
D_IN = ATTN_Q + 2 * ATTN_KV + 4 * DN_W + 2 * DN_HEADS
DN_COLS = 4 * DN_W + 128
DN_SCALE = DN_HEAD_DIM ** -0.5
ATTN_SCALE = ATTN_HEAD_DIM ** -0.5
FF_BLOCKS = 4
FF_BLOCK = D_FF // FF_BLOCKS

ADAM_LR = 0.001
ADAM_B1 = 0.9
ADAM_B2 = 0.999
ADAM_EPS = 1e-08
ADAM_WD = 0.01
ADAM_STEP = 10

V7X_VMEM_BYTES = 64 * 1024 * 1024
VMEM_LIMIT = 48 * 1024 * 1024

NN = ((1,), (0,))
NT = ((1,), (1,))
TN = ((0,), (0,))


def _dot(a, b, dims=NN, prec=None):
    if a.dtype != b.dtype:
        a, b = a.astype(MXU_DTYPE), b.astype(MXU_DTYPE)
    return lax.dot_general(a, b, (dims, ((), ())), precision=prec, preferred_element_type=F32)


def _sigmoid(x):
    return 1.0 / (1.0 + jnp.exp(-x))


def _softplus(x):
    return jnp.maximum(x, 0.0) + jnp.log(1.0 + jnp.exp(-jnp.abs(x)))


def _params(*sem):
    return pltpu.CompilerParams(dimension_semantics=sem, vmem_limit_bytes=VMEM_LIMIT)


def _rms_fwd(xv, g):
    r = lax.rsqrt(jnp.mean(xv * xv, axis=-1, keepdims=True) + EPS)
    return xv * r * g


def _rms_bwd(xv, g, dn):
    r = lax.rsqrt(jnp.mean(xv * xv, axis=-1, keepdims=True) + EPS)
    xh = xv * r
    dg = jnp.sum(dn * xh, axis=0, keepdims=True)
    dxh = dn * g
    dx = r * (dxh - xh * jnp.mean(dxh * xh, axis=-1, keepdims=True))
    return dx, dg


def _full(shape):
    return pl.BlockSpec(shape, lambda *_: (0,) * len(shape))


def _inproj(x, g_mix, wa_t, wd_t, tm):
    t = x.shape[0]

    def body(x_ref, g_ref, wa_ref, wd_ref, u_ref, pa_ref, pd_ref):
        u = _rms_fwd(x_ref[...], g_ref[...]).astype(MXU_DTYPE)
        u_ref[...] = u
        pa_ref[...] = _dot(u, wa_ref[...], NT)
        pd_ref[...] = _dot(u, wd_ref[...], NT)

    na, nd = wa_t.shape[0], wd_t.shape[0]
    return pl.pallas_call(
        body, name="inproj", grid=(t // tm,),
        in_specs=[pl.BlockSpec((tm, D_MODEL), lambda i: (i, 0)), _full((1, D_MODEL)),
                  _full((na, D_MODEL)), _full((nd, D_MODEL))],
        out_specs=[pl.BlockSpec((tm, D_MODEL), lambda i: (i, 0)), pl.BlockSpec((tm, na), lambda i: (i, 0)),
                   pl.BlockSpec((tm, nd), lambda i: (i, 0))],
        out_shape=[jax.ShapeDtypeStruct((t, D_MODEL), MXU_DTYPE), jax.ShapeDtypeStruct((t, na), F32),
                   jax.ShapeDtypeStruct((t, nd), F32)],
        compiler_params=_params("parallel"),
    )(x, g_mix, wa_t, wd_t)


def _oproj(x, ao, dn, wo_a, wo_d, tm):
    t = x.shape[0]

    def body(x_ref, ao_ref, dn_ref, wa_ref, wd_ref, h_ref):
        h_ref[...] = (x_ref[...] + _dot(ao_ref[...].astype(MXU_DTYPE), wa_ref[...])
                      + _dot(dn_ref[...].astype(MXU_DTYPE), wd_ref[...]))

    half = ao.shape[1]
    return pl.pallas_call(
        body, name="oproj", grid=(t // tm,),
        in_specs=[pl.BlockSpec((tm, D_MODEL), lambda i: (i, 0)), pl.BlockSpec((tm, half), lambda i: (i, 0)),
                  pl.BlockSpec((tm, half), lambda i: (i, 0)), _full((half, D_MODEL)), _full((half, D_MODEL))],
        out_specs=pl.BlockSpec((tm, D_MODEL), lambda i: (i, 0)),
        out_shape=jax.ShapeDtypeStruct((t, D_MODEL), F32),
        compiler_params=_params("parallel"),
    )(x, ao, dn, wo_a, wo_d)


def _mlp_fwd(h1, g_mlp, w_up4, w_down, tm):
    t = h1.shape[0]

    def body(h_ref, g_ref, wu_ref, wd_ref, m_ref, r_ref, h2_ref, acc_ref):
        k = pl.program_id(1)

        @pl.when(k == 0)
        def _():
            m_ref[...] = _rms_fwd(h_ref[...], g_ref[...]).astype(MXU_DTYPE)
            acc_ref[...] = jnp.zeros_like(acc_ref)

        r = jnp.maximum(_dot(m_ref[...], wu_ref[...]), 0.0)
        r_ref[...] = r.astype(MXU_DTYPE)
        s = jnp.square(r).astype(MXU_DTYPE)
        acc_ref[...] += _dot(s, wd_ref[...])

        @pl.when(k == FF_BLOCKS - 1)
        def _():
            h2_ref[...] = h_ref[...] + acc_ref[...]

    return pl.pallas_call(
        body, name="mlp_fwd", grid=(t // tm, FF_BLOCKS),
        in_specs=[pl.BlockSpec((tm, D_MODEL), lambda i, k: (i, 0)), _full((1, D_MODEL)),
                  pl.BlockSpec((None, D_MODEL, FF_BLOCK), lambda i, k: (k, 0, 0)),
                  pl.BlockSpec((FF_BLOCK, D_MODEL), lambda i, k: (k, 0))],
        out_specs=[pl.BlockSpec((tm, D_MODEL), lambda i, k: (i, 0)), pl.BlockSpec((tm, FF_BLOCK), lambda i, k: (i, k)),
                   pl.BlockSpec((tm, D_MODEL), lambda i, k: (i, 0))],
        out_shape=[jax.ShapeDtypeStruct((t, D_MODEL), MXU_DTYPE), jax.ShapeDtypeStruct((t, D_FF), MXU_DTYPE),
                   jax.ShapeDtypeStruct((t, D_MODEL), F32)],
        scratch_shapes=[pltpu.VMEM((tm, D_MODEL), F32)],
        compiler_params=_params("parallel", "arbitrary"),
    )(h1, g_mlp, w_up4, w_down)


def _ple_loss(h2, p, tgt, g_ple, g_fin, w_gate, w_proj, tm):
    t = h2.shape[0]

    def body(h_ref, p_ref, t_ref, gp_ref, gf_ref, wg_ref, wp_ref,
             dh_ref, dhb_ref, dgp_ref, dpp_ref, n3_ref, pb_ref, acc_ref):
        @pl.when(pl.program_id(0) == 0)
        def _():
            acc_ref[...] = jnp.zeros_like(acc_ref)

        h = h_ref[...]
        g_ple_v, g_fin_v = gp_ref[...], gf_ref[...]
        n3 = _rms_fwd(h, g_ple_v).astype(MXU_DTYPE)
        n3_ref[...] = n3
        gate = _sigmoid(_dot(n3, wg_ref[...]))
        pb = p_ref[...].astype(MXU_DTYPE)
        pb_ref[...] = pb
        pp = _dot(pb, wp_ref[...])
        h3 = h + gate * pp
        r4 = lax.rsqrt(jnp.mean(h3 * h3, axis=-1, keepdims=True) + EPS)
        xh4 = h3 * r4
        e = xh4 * g_fin_v - t_ref[...]
        loss = 0.5 * jnp.sum(jnp.mean(e * e, axis=-1, keepdims=True), axis=0, keepdims=True)
        dy = e * (1.0 / D_MODEL)
        dg_fin = jnp.sum(dy * xh4, axis=0, keepdims=True)
        dxh = dy * g_fin_v
        dh3 = r4 * (dxh - xh4 * jnp.mean(dxh * xh4, axis=-1, keepdims=True))
        dpp_ref[...] = (dh3 * gate).astype(MXU_DTYPE)
        dgp = (dh3 * pp * gate * (1.0 - gate)).astype(MXU_DTYPE)
        dgp_ref[...] = dgp
        dn3 = _dot(dgp, wg_ref[...], NT)
        dx, dg_ple = _rms_bwd(h, g_ple_v, dn3)
        dh2 = dh3 + dx
        dh_ref[...] = dh2
        dhb_ref[...] = dh2.astype(MXU_DTYPE)
        acc_ref[0:1, :] += dg_fin
        acc_ref[1:2, :] += dg_ple
        acc_ref[2:3, :] += jnp.broadcast_to(loss, (1, D_MODEL))

    row = lambda w: pl.BlockSpec((tm, w), lambda i: (i, 0))
    return pl.pallas_call(
        body, name="ple_loss", grid=(t // tm,),
        in_specs=[row(D_MODEL), row(PLE_DIM), row(D_MODEL), _full((1, D_MODEL)), _full((1, D_MODEL)),
                  _full((D_MODEL, D_MODEL)), _full((PLE_DIM, D_MODEL))],
        out_specs=[row(D_MODEL), row(D_MODEL), row(D_MODEL), row(D_MODEL), row(D_MODEL), row(PLE_DIM),
                   _full((8, D_MODEL))],
        out_shape=[jax.ShapeDtypeStruct((t, D_MODEL), F32), jax.ShapeDtypeStruct((t, D_MODEL), MXU_DTYPE),
                   jax.ShapeDtypeStruct((t, D_MODEL), MXU_DTYPE), jax.ShapeDtypeStruct((t, D_MODEL), MXU_DTYPE),
                   jax.ShapeDtypeStruct((t, D_MODEL), MXU_DTYPE), jax.ShapeDtypeStruct((t, PLE_DIM), MXU_DTYPE),
                   jax.ShapeDtypeStruct((8, D_MODEL), F32)],
        compiler_params=_params("arbitrary"),
    )(h2, p, tgt, g_ple, g_fin, w_gate, w_proj)


def _mlp_bwd(dh2, dh2b, r, h1, g_mlp, w_up4, w_down, wo_a, wo_d, tm):
    t = h1.shape[0]
    half = wo_a.shape[0]

    def body(dh_ref, dhb_ref, r_ref, h_ref, g_ref, wu_ref, wd_ref, woa_ref, wod_ref,
             da_ref, dh1_ref, dh1b_ref, dao_ref, ddn_ref, acc_ref, dm_ref):
        i, k = pl.program_id(0), pl.program_id(1)

        @pl.when((i == 0) & (k == 0))
        def _():
            acc_ref[...] = jnp.zeros_like(acc_ref)

        @pl.when(k == 0)
        def _():
            dm_ref[...] = jnp.zeros_like(dm_ref)

        ds = _dot(dhb_ref[...], wd_ref[...], NT)
        da = (ds * (2.0 * r_ref[...].astype(F32))).astype(MXU_DTYPE)
        da_ref[...] = da
        dm_ref[...] += _dot(da, wu_ref[...], NT)

        @pl.when(k == FF_BLOCKS - 1)
        def _():
            dx, dg = _rms_bwd(h_ref[...], g_ref[...], dm_ref[...])
            dh1 = dh_ref[...] + dx
            dh1_ref[...] = dh1
            dh1b = dh1.astype(MXU_DTYPE)
            dh1b_ref[...] = dh1b
            dao_ref[...] = _dot(dh1b, woa_ref[...], NT)
            ddn_ref[...] = _dot(dh1b, wod_ref[...], NT)
            acc_ref[0:1, :] += dg

    tok = lambda w: pl.BlockSpec((tm, w), lambda i, k: (i, 0))
    return pl.pallas_call(
        body, name="mlp_bwd", grid=(t // tm, FF_BLOCKS),
        in_specs=[tok(D_MODEL), tok(D_MODEL), pl.BlockSpec((tm, FF_BLOCK), lambda i, k: (i, k)), tok(D_MODEL),
                  _full((1, D_MODEL)), pl.BlockSpec((None, D_MODEL, FF_BLOCK), lambda i, k: (k, 0, 0)),
                  pl.BlockSpec((FF_BLOCK, D_MODEL), lambda i, k: (k, 0)),
                  pl.BlockSpec((half, D_MODEL), lambda i, k: (0, 0)), pl.BlockSpec((half, D_MODEL), lambda i, k: (0, 0))],
        out_specs=[pl.BlockSpec((tm, FF_BLOCK), lambda i, k: (i, k)),
                   tok(D_MODEL), tok(D_MODEL), tok(half), tok(half), pl.BlockSpec((8, D_MODEL), lambda i, k: (0, 0))],
        out_shape=[jax.ShapeDtypeStruct((t, D_FF), MXU_DTYPE),
                   jax.ShapeDtypeStruct((t, D_MODEL), F32), jax.ShapeDtypeStruct((t, D_MODEL), MXU_DTYPE),
                   jax.ShapeDtypeStruct((t, half), F32), jax.ShapeDtypeStruct((t, half), F32),
                   jax.ShapeDtypeStruct((8, D_MODEL), F32)],
        scratch_shapes=[pltpu.VMEM((tm, D_MODEL), F32)],
        compiler_params=_params("arbitrary", "arbitrary"),
    )(dh2, dh2b, r, h1, g_mlp, w_up4, w_down, wo_a, wo_d)


def _inproj_bwd(x, dh1, g_mix, grads, weights, tm):
    t = x.shape[0]
    n = len(grads)

    def body(*refs):
        x_ref, dh_ref, g_ref = refs[:3]
        g_refs, w_refs = refs[3:3 + n], refs[3 + n:3 + 2 * n]
        dx_ref, acc_ref = refs[3 + 2 * n:]

        @pl.when(pl.program_id(0) == 0)
        def _():
            acc_ref[...] = jnp.zeros_like(acc_ref)

        du = _dot(g_refs[0][...], w_refs[0][...])
        for j in range(1, n):
            du += _dot(g_refs[j][...], w_refs[j][...])
        dx, dg = _rms_bwd(x_ref[...], g_ref[...], du)
        dx_ref[...] = dh_ref[...] + dx
        acc_ref[0:1, :] += dg

    tok = lambda w: pl.BlockSpec((tm, w), lambda i: (i, 0))
    return pl.pallas_call(
        body, name="inproj_bwd", grid=(t // tm,),
        in_specs=[tok(D_MODEL), tok(D_MODEL), _full((1, D_MODEL))] + [tok(g.shape[1]) for g in grads]
                 + [_full(w.shape) for w in weights],
        out_specs=[tok(D_MODEL), _full((8, D_MODEL))],
        out_shape=[jax.ShapeDtypeStruct((t, D_MODEL), F32), jax.ShapeDtypeStruct((8, D_MODEL), F32)],
        compiler_params=_params("arbitrary"),
    )(x, dh1, g_mix, *grads, *weights)


def _wgrad(a, b, name, tk, tn, tt, stacked=False, prep=None):
    t, kdim = a.shape
    ncols = b.shape[1]

    def body(a_ref, b_ref, o_ref):
        @pl.when(pl.program_id(2) == 0)
        def _():
            o_ref[...] = jnp.zeros_like(o_ref)

        av = a_ref[...] if prep is None else prep(a_ref[...])
        o_ref[...] += _dot(av, b_ref[...], TN)

    if stacked:
        out_spec = pl.BlockSpec((None, tk, tn), lambda i, j, s: (j, i, 0))
        out_shape = jax.ShapeDtypeStruct((ncols // tn, kdim, tn), F32)
    else:
        out_spec = pl.BlockSpec((tk, tn), lambda i, j, s: (i, j))
        out_shape = jax.ShapeDtypeStruct((kdim, ncols), F32)
    return pl.pallas_call(
        body, name=name, grid=(kdim // tk, ncols // tn, t // tt),
        in_specs=[pl.BlockSpec((tt, tk), lambda i, j, s: (s, i)), pl.BlockSpec((tt, tn), lambda i, j, s: (s, j))],
        out_specs=out_spec, out_shape=out_shape,
        compiler_params=_params("parallel", "parallel", "arbitrary"),
    )(a, b)


def _wgrad_cat(as_, bs, name, tt):
    t = as_[0].shape[0]
    heights = [a.shape[1] for a in as_]
    widths = [b.shape[1] for b in bs]

    def body(*refs):
        a_refs, b_refs, o_ref = refs[:len(as_)], refs[len(as_):-1], refs[-1]

        @pl.when(pl.program_id(0) == 0)
        def _():
            o_ref[...] = jnp.zeros_like(o_ref)

        row = 0
        for a_ref, k in zip(a_refs, heights):
            av = a_ref[...]
            col = 0
            for b_ref, n in zip(b_refs, widths):
                o_ref[row:row + k, col:col + n] += _dot(av, b_ref[...], TN)
                col += n
            row += k

    tok = lambda w: pl.BlockSpec((tt, w), lambda s: (s, 0))
    shape = (sum(heights), sum(widths))
    return pl.pallas_call(
        body, name=name, grid=(t // tt,),
        in_specs=[tok(k) for k in heights] + [tok(n) for n in widths],
        out_specs=_full(shape), out_shape=jax.ShapeDtypeStruct(shape, F32),
        compiler_params=_params("arbitrary"),
    )(*as_, *bs)


def _rope_tables(t):
    half = ATTN_HEAD_DIM // 2
    inv = 1.0 / (ROPE_THETA ** (jnp.arange(half, dtype=F32) * (2.0 / ATTN_HEAD_DIM)))
    ang = jnp.arange(t, dtype=F32)[:, None] * inv[None, :]
    cos, sin = jnp.cos(ang), jnp.sin(ang)
    cos2 = jnp.concatenate([cos, cos], axis=-1)
    sin2 = jnp.concatenate([-sin, sin], axis=-1)
    return jnp.tile(cos2, (1, 2)), jnp.tile(sin2, (1, 2))


def _swap_halves(tv):
    w = tv.shape[-1]
    lane = lax.broadcasted_iota(jnp.int32, tv.shape, tv.ndim - 1)
    first = (lane % ATTN_HEAD_DIM) < (ATTN_HEAD_DIM // 2)
    return jnp.where(first, pltpu.roll(tv, w - ATTN_HEAD_DIM // 2, tv.ndim - 1),
                     pltpu.roll(tv, ATTN_HEAD_DIM // 2, tv.ndim - 1))


def _rope(tv, cos, sin):
    return tv * cos + _swap_halves(tv) * sin


def _rope_bwd(dv, cos, sin):
    return dv * cos + _swap_halves(dv * sin)


def _attn_valid(first_block):
    c = lax.broadcasted_iota(jnp.int32, (2 * ATTN_BLOCK, ATTN_BLOCK), 0)
    r = lax.broadcasted_iota(jnp.int32, (2 * ATTN_BLOCK, ATTN_BLOCK), 1)
    return (c > r) & (c <= r + ATTN_BLOCK) & ((c >= ATTN_BLOCK) | jnp.logical_not(first_block))


def _attn_probs(st, sink, valid):
    s = jnp.where(valid, st * ATTN_SCALE, -jnp.inf)
    m = jnp.maximum(jnp.max(s, axis=0, keepdims=True), sink)
    e = jnp.where(valid, jnp.exp(s - m), 0.0)
    es = jnp.exp(sink - m)
    inv = 1.0 / (jnp.sum(e, axis=0, keepdims=True) + es)
    return e * inv, es * inv


def _lane_scalar(vec, idx):
    lane = lax.broadcasted_iota(jnp.int32, vec.shape, 1)
    return jnp.sum(jnp.where(lane == idx, vec, 0.0), axis=-1, keepdims=True)


ATTN_STEP = 2 * ATTN_BLOCK


def _attn_specs(ns):
    cur = lambda w, cb: pl.BlockSpec((ATTN_STEP, w), lambda i: (jnp.minimum(i, ns - 1), cb))
    prev = lambda w, cb: pl.BlockSpec((ATTN_BLOCK, w), lambda i: (jnp.maximum(2 * jnp.minimum(i, ns - 1) - 1, 0), cb))
    kcol, vcol = ATTN_Q // ATTN_KV, ATTN_Q // ATTN_KV + 1
    return [cur(ATTN_Q, 0), cur(ATTN_KV, kcol), prev(ATTN_KV, kcol), cur(ATTN_KV, vcol), prev(ATTN_KV, vcol),
            cur(ATTN_KV, 0), cur(ATTN_KV, 0), prev(ATTN_KV, 0), prev(ATTN_KV, 0), _full((1, 128))]


def _attn_windows(tp, tc):
    hsl = lambda hk: slice(hk * ATTN_HEAD_DIM, (hk + 1) * ATTN_HEAD_DIM)
    return [[jnp.concatenate([tp[:, hsl(hk)], tc[0:ATTN_BLOCK, hsl(hk)]], axis=0) for hk in range(ATTN_KV_HEADS)],
            [tc[:, hsl(hk)] for hk in range(ATTN_KV_HEADS)]]


def _attn_items():
    return [(s, h, slice(s * ATTN_BLOCK, (s + 1) * ATTN_BLOCK), slice(h * ATTN_HEAD_DIM, (h + 1) * ATTN_HEAD_DIM))
            for s in range(2) for h in range(ATTN_HEADS)]


def _attn_fwd(pa, cos, sin, sinks_vec):
    t = pa.shape[0]
    ns = t // ATTN_STEP

    def body(q_ref, kc_ref, kp_ref, vc_ref, vp_ref, cc_ref, sc_ref, cp_ref, sp_ref, sk_ref, o_ref):
        cc, sc = cc_ref[...], sc_ref[...]
        q = _rope(q_ref[...], jnp.tile(cc, (1, ATTN_Q // ATTN_KV)), jnp.tile(sc, (1, ATTN_Q // ATTN_KV)))
        kc = _rope(kc_ref[...], cc, sc)
        kp = _rope(kp_ref[...], cp_ref[...], sp_ref[...])
        sk = sk_ref[...]
        valids = [_attn_valid(pl.program_id(0) == 0), _attn_valid(False)]
        kwins = _attn_windows(kp, kc)
        vwins_t = [[v.T for v in vs] for vs in _attn_windows(vp_ref[...], vc_ref[...])]
        items = _attn_items()
        scores = [_dot(kwins[s][h // ATTN_GROUPS], q[rows, hs], NT) for s, h, rows, hs in items]
        probs = [_attn_probs(st, _lane_scalar(sk, h), valids[s])[0] for (s, h, rows, hs), st in zip(items, scores)]
        for (s, h, rows, hs), pt in zip(items, probs):
            o_ref[rows, hs] = _dot(vwins_t[s][h // ATTN_GROUPS], pt).T.astype(o_ref.dtype)

    return pl.pallas_call(
        body, name="attn_fwd", grid=(ns,),
        in_specs=_attn_specs(ns),
        out_specs=pl.BlockSpec((ATTN_STEP, ATTN_Q), lambda i: (i, 0)),
        out_shape=jax.ShapeDtypeStruct((t, ATTN_Q), MXU_DTYPE),
        compiler_params=_params("parallel"),
    )(pa, pa, pa, pa, pa, cos, sin, cos, sin, sinks_vec)


def _attn_bwd(pa, cos, sin, sinks_vec, dao):
    t = pa.shape[0]
    ns = t // ATTN_STEP
    lo, hi = slice(0, ATTN_BLOCK), slice(ATTN_BLOCK, ATTN_STEP)

    def body(q_ref, kc_ref, kp_ref, vc_ref, vp_ref, cc_ref, sc_ref, cp_ref, sp_ref, sk_ref, do_ref,
             dq_ref, dk_ref, dv_ref, acc_ref, dqr_ref, dkw_ref, dvw_ref, ck_ref, cv_ref):
        i = pl.program_id(0)

        @pl.when(i == 0)
        def _():
            acc_ref[...] = jnp.zeros_like(acc_ref)
            ck_ref[...] = jnp.zeros_like(ck_ref)
            cv_ref[...] = jnp.zeros_like(cv_ref)

        @pl.when(i < ns)
        def _():
            cc, sc = cc_ref[...], sc_ref[...]
            cq, sq = jnp.tile(cc, (1, ATTN_Q // ATTN_KV)), jnp.tile(sc, (1, ATTN_Q // ATTN_KV))
            q = _rope(q_ref[...], cq, sq)
            kc = _rope(kc_ref[...], cc, sc)
            kp = _rope(kp_ref[...], cp_ref[...], sp_ref[...])
            sk = sk_ref[...]
            do = do_ref[...]
            lane = lax.broadcasted_iota(jnp.int32, (1, 128), 1)
            dsink = jnp.zeros((1, 128), F32)
            valids = [_attn_valid(i == 0), _attn_valid(False)]
            kwins = _attn_windows(kp, kc)
            vwins = _attn_windows(vp_ref[...], vc_ref[...])
            kwins_t = [[kw.T for kw in kws] for kws in kwins]
            items = _attn_items()
            scores = [_dot(kwins[s][h // ATTN_GROUPS], q[rows, hs], NT) for s, h, rows, hs in items]
            dps = [_dot(vwins[s][h // ATTN_GROUPS], do[rows, hs], NT) for s, h, rows, hs in items]
            pts, dsts = {}, {}
            for (s, h, rows, hs), st, dp_t in zip(items, scores, dps):
                probs_t, psink = _attn_probs(st, _lane_scalar(sk, h), valids[s])
                delta = jnp.sum(probs_t * dp_t, axis=0, keepdims=True)
                pts[s, h] = probs_t
                dsts[s, h] = probs_t * (dp_t - delta) * ATTN_SCALE
                dsink += jnp.where(lane == h, jnp.sum(-psink * delta, axis=1, keepdims=True), 0.0)
            for s, h, rows, hs in items:
                dqr_ref[rows, hs] = _dot(kwins_t[s][h // ATTN_GROUPS], dsts[s, h]).T
            for s in range(2):
                rows = slice(s * ATTN_BLOCK, (s + 1) * ATTN_BLOCK)
                for hk in range(ATTN_KV_HEADS):
                    ks = slice(hk * ATTN_HEAD_DIM, (hk + 1) * ATTN_HEAD_DIM)
                    group = range(hk * ATTN_GROUPS, (hk + 1) * ATTN_GROUPS)
                    heads = [slice(h * ATTN_HEAD_DIM, (h + 1) * ATTN_HEAD_DIM) for h in group]
                    ds_g = jnp.concatenate([dsts[s, h] for h in group], axis=1)
                    p_g = jnp.concatenate([pts[s, h] for h in group], axis=1)
                    q_g = jnp.concatenate([q[rows, hs] for hs in heads], axis=0)
                    do_g = jnp.concatenate([do[rows, hs] for hs in heads], axis=0)
                    dkw_ref[s, :, ks] = _dot(ds_g, q_g)
                    dvw_ref[s, :, ks] = _dot(p_g, do_g)
            acc_ref[0:1, :] += dsink
            dq_ref[...] = _rope_bwd(dqr_ref[...], cq, sq).astype(dq_ref.dtype)
            dk_ref[lo, :] = ck_ref[lo, :].astype(dk_ref.dtype)
            dk_ref[hi, :] = (ck_ref[hi, :] + _rope_bwd(dkw_ref[0, lo, :], cp_ref[...], sp_ref[...])).astype(dk_ref.dtype)
            dv_ref[lo, :] = cv_ref[lo, :].astype(dv_ref.dtype)
            dv_ref[hi, :] = (cv_ref[hi, :] + dvw_ref[0, lo, :]).astype(dv_ref.dtype)
            ck_ref[lo, :] = _rope_bwd(dkw_ref[0, hi, :] + dkw_ref[1, lo, :], cc[lo, :], sc[lo, :])
            ck_ref[hi, :] = _rope_bwd(dkw_ref[1, hi, :], cc[hi, :], sc[hi, :])
            cv_ref[lo, :] = dvw_ref[0, hi, :] + dvw_ref[1, lo, :]
            cv_ref[hi, :] = dvw_ref[1, hi, :]

        @pl.when(i == ns)
        def _():
            dk_ref[...] = ck_ref[...].astype(dk_ref.dtype)
            dv_ref[...] = cv_ref[...].astype(dv_ref.dtype)

    prev_out = lambda w: pl.BlockSpec((ATTN_STEP, w), lambda i: (jnp.maximum(i - 1, 0), 0))
    return pl.pallas_call(
        body, name="attn_bwd", grid=(ns + 1,),
        in_specs=_attn_specs(ns) + [pl.BlockSpec((ATTN_STEP, ATTN_Q), lambda i: (jnp.minimum(i, ns - 1), 0))],
        out_specs=[pl.BlockSpec((ATTN_STEP, ATTN_Q), lambda i: (jnp.minimum(i, ns - 1), 0)), prev_out(ATTN_KV),
                   prev_out(ATTN_KV), _full((8, 128))],
        out_shape=[jax.ShapeDtypeStruct((t, ATTN_Q), MXU_DTYPE), jax.ShapeDtypeStruct((t, ATTN_KV), MXU_DTYPE),
                   jax.ShapeDtypeStruct((t, ATTN_KV), MXU_DTYPE), jax.ShapeDtypeStruct((8, 128), F32)],
        scratch_shapes=[pltpu.VMEM((ATTN_STEP, ATTN_Q), F32), pltpu.VMEM((2, ATTN_STEP, ATTN_KV), F32),
                        pltpu.VMEM((2, ATTN_STEP, ATTN_KV), F32), pltpu.VMEM((ATTN_STEP, ATTN_KV), F32),
                        pltpu.VMEM((ATTN_STEP, ATTN_KV), F32)],
        compiler_params=_params("arbitrary"),
    )(pa, pa, pa, pa, pa, cos, sin, cos, sin, sinks_vec, dao)


PAIR = 2 * DN_CHUNK
INTRA_PAIRS = 4
SCAN_PAIRS = 4
HALO = 8


def _conv_window(cur_ref, prev_ref, xs_ref, tm, has_prev):
    prev = jnp.where(has_prev, prev_ref[...], 0.0)
    xs_ref[0:HALO, :] = prev
    xs_ref[HALO:HALO + tm, :] = cur_ref[...]


def _conv_taps(xs_ref, cw_ref, tm):
    y = cw_ref[0:1, :] * xs_ref[pl.ds(HALO - DN_CONV + 1, tm), :]
    for j in range(1, DN_CONV):
        y += cw_ref[j:j + 1, :] * xs_ref[pl.ds(HALO - DN_CONV + 1 + j, tm), :]
    return y


def _gate_values(ba, al, dt):
    beta = _sigmoid(ba)
    pre = ba + dt
    g = -jnp.exp(al) * _softplus(pre)
    return beta, g, pre


def _dn_prep_specs(tm, tile):
    return [pl.BlockSpec((tm, CONV_CH), lambda i: (tile(i), 0)),
            pl.BlockSpec((HALO, CONV_CH), lambda i: (jnp.maximum(tile(i) * (tm // HALO) - 1, 0), 0)),
            pl.BlockSpec((tm, 128), lambda i: (tile(i), 4 * DN_W // 128)),
            _full((DN_CONV, CONV_CH)), _full((1, 128)), _full((1, 128))]


def _dn_prep(pd, conv_w, al_vec, dt_vec, tm):
    t = pd.shape[0]

    def body(cur_ref, prev_ref, ba_ref, cw_ref, al_ref, dt_ref, qn_ref, kn_ref, vc_ref, gc_ref, gr_ref, xs_ref):
        _conv_window(cur_ref, prev_ref, xs_ref, tm, pl.program_id(0) > 0)
        y = _conv_taps(xs_ref, cw_ref, tm)
        c = y * _sigmoid(y)
        for h in range(DN_HEADS):
            qs = slice(h * DN_HEAD_DIM, (h + 1) * DN_HEAD_DIM)
            ksl = slice(DN_W + h * DN_HEAD_DIM, DN_W + (h + 1) * DN_HEAD_DIM)
            qh, kh = c[:, qs], c[:, ksl]
            qn_ref[:, qs] = qh * lax.rsqrt(jnp.sum(qh * qh, axis=-1, keepdims=True) + EPS) * DN_SCALE
            kn_ref[:, qs] = kh * lax.rsqrt(jnp.sum(kh * kh, axis=-1, keepdims=True) + EPS)
        vc_ref[...] = c[:, 2 * DN_W:3 * DN_W]
        beta, g, _ = _gate_values(ba_ref[...], al_ref[...], dt_ref[...])
        lane = lax.broadcasted_iota(jnp.int32, beta.shape, 1)
        gb = jnp.where(lane < DN_HEADS, beta, jnp.where(lane < 2 * DN_HEADS, g, 0.0))
        gc_ref[...] = gb
        gr_ref[...] = gb.T[0:8, :]

    tok = lambda w: pl.BlockSpec((tm, w), lambda i: (i, 0))
    return pl.pallas_call(
        body, name="dn_prep", grid=(t // tm,),
        in_specs=_dn_prep_specs(tm, lambda i: i),
        out_specs=[tok(DN_W), tok(DN_W), tok(DN_W), tok(128), pl.BlockSpec((8, tm), lambda i: (0, i))],
        out_shape=[jax.ShapeDtypeStruct((t, DN_W), F32)] * 3 + [jax.ShapeDtypeStruct((t, 128), F32),
                                                                 jax.ShapeDtypeStruct((8, t), F32)],
        scratch_shapes=[pltpu.VMEM((HALO + tm, CONV_CH), F32)],
        compiler_params=_params("parallel"),
    )(pd, pd, pd, conv_w, al_vec, dt_vec)


def _pair_masks():
    r = lax.broadcasted_iota(jnp.int32, (PAIR, PAIR), 0)
    c = lax.broadcasted_iota(jnp.int32, (PAIR, PAIR), 1)
    same = (r < DN_CHUNK) == (c < DN_CHUNK)
    return same & (r >= c), same & (r > c)


def _lane_col(mat, idx):
    lane = lax.broadcasted_iota(jnp.int32, mat.shape, 1)
    return jnp.sum(jnp.where(lane == idx, mat, 0.0), axis=-1, keepdims=True)


def _pair_cumsums(gc, gr, low):
    lowf = low.astype(F32)
    return _dot(lowf, gc, NN, HI), _dot(gr, lowf, NT, HI)


def _pair_gates(gc, cum_c, cum_r, low, h):
    beta = _lane_col(gc, h)
    gam = _lane_col(cum_c, DN_HEADS + h)
    gam_row = cum_r[DN_HEADS + h:DN_HEADS + h + 1, :]
    dm = jnp.where(low, jnp.exp(jnp.where(low, gam - gam_row, 0.0)), 0.0)
    row = lax.broadcasted_iota(jnp.int32, gam.shape, 0)
    gl = jnp.where(row < DN_CHUNK, gam[DN_CHUNK - 1:DN_CHUNK, :], gam[PAIR - 1:PAIR, :])
    return beta, gam, dm, gl


def _split(a):
    hi = a.astype(BF16)
    return hi, (a - hi.astype(F32)).astype(BF16)


def _dot_split(a, b, dims=NN):
    (ah, al), (bh, bl) = a, b
    la, lb = (1, 1) if dims == TN else ((0, 1) if dims == NN else (0, 0))
    r = _dot(jnp.concatenate([ah, al], axis=la), jnp.concatenate([bh, bl], axis=lb), dims)
    m, n = r.shape[0] // 2, r.shape[1] // 2
    return (r[m:, n:] + (r[:m, n:] + r[m:, :n])) + r[:m, :n]


def _unit_lower_inverses(lmats):
    n = lmats[0].shape[0]
    r = lax.broadcasted_iota(jnp.int32, (n, n), 0)
    c = lax.broadcasted_iota(jnp.int32, (n, n), 1)
    same = lambda size: (r & ~(size - 1)) == (c & ~(size - 1))
    base = DN_CHUNK // 4
    diag = [jnp.where(same(base), l, 0.0) for l in lmats]
    accs = [(r == c).astype(F32) - d for d in diag]
    splits = [_split(d) for d in diag]
    step = 1
    while 2 * step < base:
        splits = [_split(_dot_split(s, s)) for s in splits]
        accs = [acc + _dot_split(_split(acc), s) for acc, s in zip(accs, splits)]
        step *= 2
    size = base
    while size < DN_CHUNK:
        below = same(2 * size) & jnp.logical_not(same(size))
        tb = [_dot(acc, jnp.where(below, l, 0.0)) for acc, l in zip(accs, lmats)]
        accs = [acc - _dot(t, acc) for acc, t in zip(accs, tb)]
        size *= 2
    return accs


def _dn_intra(qn, kn, vc, gc, gr):
    t = qn.shape[0]
    npair = t // PAIR
    rows_step = INTRA_PAIRS * PAIR

    def body(q_ref, k_ref, v_ref, gc_ref, gr_ref, u_ref, w_ref, qg_ref, kd_ref, a_ref, ti_ref, dl_ref):
        low, strict = _pair_masks()
        items = []
        for p in range(INTRA_PAIRS):
            rows = slice(p * PAIR, (p + 1) * PAIR)
            gc_v = gc_ref[rows, :]
            cum_c, cum_r = _pair_cumsums(gc_v, gr_ref[:, rows], low)
            for h in range(DN_HEADS):
                hs = slice(h * DN_HEAD_DIM, (h + 1) * DN_HEAD_DIM)
                items.append((p, h, rows, hs, _pair_gates(gc_v, cum_c, cum_r, low, h)))
        lmats = []
        for p, h, rows, hs, (beta, gam, dm, gl) in items:
            k = k_ref[rows, hs]
            lmats.append(jnp.where(strict, _dot(k * beta, k, NT) * dm, 0.0))
        tinvs = _unit_lower_inverses(lmats)
        for (p, h, rows, hs, (beta, gam, dm, gl)), tinv in zip(items, tinvs):
            q, k, v = q_ref[rows, hs], k_ref[rows, hs], v_ref[rows, hs]
            eg = jnp.exp(gam)
            u_ref[rows, hs] = _dot(tinv, v * beta)
            w_ref[rows, hs] = _dot(tinv, (k * beta) * eg).astype(w_ref.dtype)
            a_ref[h, rows, :] = _dot(q, k, NT) * dm
            ti_ref[h, rows, :] = tinv
            qg_ref[rows, hs] = (q * eg).astype(qg_ref.dtype)
            kd_ref[rows, hs] = (k * jnp.exp(gl - gam)).astype(kd_ref.dtype)
            for c in range(2):
                last = (c + 1) * DN_CHUNK - 1
                dl_ref[2 * p + c, h] = jnp.broadcast_to(jnp.exp(gam[last:last + 1, :]), (8, 128))

    tok = lambda w: pl.BlockSpec((rows_step, w), lambda n: (n, 0))
    hm = pl.BlockSpec((DN_HEADS, rows_step, PAIR), lambda n: (0, n, 0))
    return pl.pallas_call(
        body, name="dn_intra", grid=(npair // INTRA_PAIRS,),
        in_specs=[tok(DN_W), tok(DN_W), tok(DN_W), tok(128), pl.BlockSpec((8, rows_step), lambda n: (0, n))],
        out_specs=[tok(DN_W)] * 4 + [hm, hm, pl.BlockSpec((2 * INTRA_PAIRS, DN_HEADS, 8, 128), lambda n: (n, 0, 0, 0))],
        out_shape=[jax.ShapeDtypeStruct((t, DN_W), F32)] + [jax.ShapeDtypeStruct((t, DN_W), MXU_DTYPE)] * 3
                  + [jax.ShapeDtypeStruct((DN_HEADS, t, PAIR), F32)] * 2
                  + [jax.ShapeDtypeStruct((2 * npair, DN_HEADS, 8, 128), F32)],
        compiler_params=_params("parallel"),
    )(qn, kn, vc, gc, gr)


def _dn_scan_fwd(u, w, qg, kd, a_qk, dlast, pd, dn_w):
    t = u.shape[0]
    npair = t // PAIR

    def body(u_ref, w_ref, qg_ref, kd_ref, a_ref, dl_ref, z_ref, nw_ref, out_ref, o_ref, vn_ref, sall_ref, s_ref):
        @pl.when(pl.program_id(0) == 0)
        def _():
            s_ref[...] = jnp.zeros_like(s_ref)

        nw = nw_ref[...]
        for c in range(2 * SCAN_PAIRS):
            rows = slice(c * DN_CHUNK, (c + 1) * DN_CHUNK)
            diag = slice((c % 2) * DN_CHUNK, (c % 2 + 1) * DN_CHUNK)
            for h in range(DN_HEADS):
                hs = slice(h * DN_HEAD_DIM, (h + 1) * DN_HEAD_DIM)
                st = s_ref[h]
                sall_ref[c, h] = st
                vn_ref[rows, hs] = (u_ref[rows, hs] - _dot(w_ref[rows, hs], st)).astype(vn_ref.dtype)
            for h in range(DN_HEADS):
                hs = slice(h * DN_HEAD_DIM, (h + 1) * DN_HEAD_DIM)
                st, vn = s_ref[h], vn_ref[rows, hs]
                o = _dot(qg_ref[rows, hs], st) + _dot(a_ref[h, rows, diag], vn)
                s_ref[h] = st * dl_ref[c, h][0:1, :] + _dot(kd_ref[rows, hs], vn, TN)
                o_ref[rows, hs] = o
                z = z_ref[rows, hs]
                on = o * lax.rsqrt(jnp.mean(o * o, axis=-1, keepdims=True) + EPS) * nw
                out_ref[rows, hs] = (on * (z * _sigmoid(z))).astype(out_ref.dtype)

    rows_step = SCAN_PAIRS * PAIR
    tok = pl.BlockSpec((rows_step, DN_W), lambda n: (n, 0))
    hm = pl.BlockSpec((DN_HEADS, rows_step, PAIR), lambda n: (0, n, 0))
    return pl.pallas_call(
        body, name="dn_scan_fwd", grid=(npair // SCAN_PAIRS,),
        in_specs=[tok, tok, tok, tok, hm, pl.BlockSpec((2 * SCAN_PAIRS, DN_HEADS, 8, 128), lambda n: (n, 0, 0, 0)),
                  pl.BlockSpec((rows_step, DN_W), lambda n: (n, 3)), _full((1, 128))],
        out_specs=[tok, tok, tok,
                   pl.BlockSpec((2 * SCAN_PAIRS, DN_HEADS, DN_HEAD_DIM, DN_HEAD_DIM), lambda n: (n, 0, 0, 0))],
        out_shape=[jax.ShapeDtypeStruct((t, DN_W), MXU_DTYPE), jax.ShapeDtypeStruct((t, DN_W), F32),
                   jax.ShapeDtypeStruct((t, DN_W), MXU_DTYPE),
                   jax.ShapeDtypeStruct((2 * npair, DN_HEADS, DN_HEAD_DIM, DN_HEAD_DIM), F32)],
        scratch_shapes=[pltpu.VMEM((DN_HEADS, DN_HEAD_DIM, DN_HEAD_DIM), F32)],
        compiler_params=_params("arbitrary"),
    )(u, w, qg, kd, a_qk, dlast, pd, dn_w)


def _dn_scan_bwd(dout, o, vnew, sall, w, qg, kd, a_qk, dlast, pd, dn_w, dep):
    t = o.shape[0]
    npair = t // PAIR
    nstep = npair // SCAN_PAIRS
    rev = lambda n: nstep - 1 - n

    def body(do_ref, o_ref, vn_ref, sall_ref, w_ref, qg_ref, kd_ref, a_ref, dl_ref, z_ref, nw_ref, dep_ref,
             dz_ref, du_ref, dw_ref, dqg_ref, dkd_ref, da_ref, ddl_ref, acc_ref, ds_ref, dos_ref):
        @pl.when(pl.program_id(0) == 0)
        def _():
            ds_ref[...] = jnp.zeros_like(ds_ref)
            acc_ref[...] = jnp.zeros_like(acc_ref)

        nw = nw_ref[...]
        dnw = jnp.zeros((1, 128), F32)
        for h in range(DN_HEADS):
            hs = slice(h * DN_HEAD_DIM, (h + 1) * DN_HEAD_DIM)
            o, z, dout = o_ref[:, hs], z_ref[:, hs], do_ref[:, hs]
            r = lax.rsqrt(jnp.mean(o * o, axis=-1, keepdims=True) + EPS)
            oh = o * r
            sz = _sigmoid(z)
            dz_ref[:, hs] = dout * (oh * nw) * (sz + z * sz * (1.0 - sz))
            don = dout * (z * sz)
            dnw += jnp.sum(don * oh, axis=0, keepdims=True)
            doh = don * nw
            dos_ref[:, hs] = r * (doh - oh * jnp.mean(doh * oh, axis=-1, keepdims=True))
        acc_ref[0:1, :] += dnw
        for c in reversed(range(2 * SCAN_PAIRS)):
            rows = slice(c * DN_CHUNK, (c + 1) * DN_CHUNK)
            diag = slice((c % 2) * DN_CHUNK, (c % 2 + 1) * DN_CHUNK)
            other = slice((1 - c % 2) * DN_CHUNK, (2 - c % 2) * DN_CHUNK)
            for h in range(DN_HEADS):
                hs = slice(h * DN_HEAD_DIM, (h + 1) * DN_HEAD_DIM)
                do, st, dsp, vn = dos_ref[rows, hs], sall_ref[c, h], ds_ref[h], vn_ref[rows, hs]
                da_ref[h, rows, diag] = _dot(do, vn, NT)
                da_ref[h, rows, other] = jnp.zeros((DN_CHUNK, DN_CHUNK), F32)
                du_ref[rows, hs] = (_dot(a_ref[h, rows, diag], do, TN) + _dot(kd_ref[rows, hs], dsp)).astype(du_ref.dtype)
                dqg_ref[rows, hs] = _dot(do, st, NT)
                dkd_ref[rows, hs] = _dot(vn, dsp, NT)
                ddl = jnp.sum(jnp.sum(dsp * st, axis=1, keepdims=True), axis=0, keepdims=True)
                ddl_ref[c, h] = jnp.broadcast_to(ddl, (8, 128))
            for h in range(DN_HEADS):
                hs = slice(h * DN_HEAD_DIM, (h + 1) * DN_HEAD_DIM)
                do, st, dvn = dos_ref[rows, hs], sall_ref[c, h], du_ref[rows, hs]
                dw_ref[rows, hs] = (-_dot(dvn, st, NT)).astype(dw_ref.dtype)
                ds_ref[h] = (ds_ref[h] * dl_ref[c, h][0:1, :] + _dot(qg_ref[rows, hs], do, TN)
                             - _dot(w_ref[rows, hs], dvn, TN))

    rows_step = SCAN_PAIRS * PAIR
    tok = pl.BlockSpec((rows_step, DN_W), lambda n: (rev(n), 0))
    hm = pl.BlockSpec((DN_HEADS, rows_step, PAIR), lambda n: (0, rev(n), 0))
    sc = pl.BlockSpec((2 * SCAN_PAIRS, DN_HEADS, 8, 128), lambda n: (rev(n), 0, 0, 0))
    return pl.pallas_call(
        body, name="dn_scan_bwd", grid=(nstep,),
        in_specs=[tok, tok, tok,
                  pl.BlockSpec((2 * SCAN_PAIRS, DN_HEADS, DN_HEAD_DIM, DN_HEAD_DIM), lambda n: (rev(n), 0, 0, 0)),
                  tok, tok, tok, hm, sc, pl.BlockSpec((rows_step, DN_W), lambda n: (rev(n), 3)), _full((1, 128)),
                  pl.BlockSpec(memory_space=pl.ANY)],
        out_specs=[tok] * 5 + [hm, sc, _full((8, 128))],
        out_shape=[jax.ShapeDtypeStruct((t, DN_W), F32)] + [jax.ShapeDtypeStruct((t, DN_W), MXU_DTYPE)] * 2
                  + [jax.ShapeDtypeStruct((t, DN_W), F32)] * 2 + [jax.ShapeDtypeStruct((DN_HEADS, t, PAIR), F32),
                   jax.ShapeDtypeStruct((2 * npair, DN_HEADS, 8, 128), F32), jax.ShapeDtypeStruct((8, 128), F32)],
        scratch_shapes=[pltpu.VMEM((DN_HEADS, DN_HEAD_DIM, DN_HEAD_DIM), F32), pltpu.VMEM((SCAN_PAIRS * PAIR, DN_W), F32)],
        compiler_params=_params("arbitrary"),
    )(dout, o, vnew, sall, w, qg, kd, a_qk, dlast, pd, dn_w, dep)


def _dn_intra_bwd(qn, kn, vc, gc, gr, tinv, a_qk, du, dw, dqg, dkd, da_qk, ddlast, dlast, dep):
    t = qn.shape[0]
    npair = t // PAIR

    def body(q_ref, k_ref, v_ref, gc_ref, gr_ref, ti_ref, a_ref, du_ref, dw_ref, dqg_ref, dkd_ref, da_ref, ddl_ref, dl_ref,
             dep_ref, dq_ref, dk_ref, dv_ref, dg_ref):
        low, strict = _pair_masks()
        lane = lax.broadcasted_iota(jnp.int32, (PAIR, 128), 1)
        rowi = lax.broadcasted_iota(jnp.int32, (PAIR, 1), 0)
        rsum = lambda v: jnp.sum(v, axis=-1, keepdims=True)
        items = []
        for p in range(INTRA_PAIRS):
            rows = slice(p * PAIR, (p + 1) * PAIR)
            gc_v = gc_ref[rows, :]
            cum_c, cum_r = _pair_cumsums(gc_v, gr_ref[:, rows], low)
            for h in range(DN_HEADS):
                hs = slice(h * DN_HEAD_DIM, (h + 1) * DN_HEAD_DIM)
                items.append((p, h, rows, hs, _pair_gates(gc_v, cum_c, cum_r, low, h)))
        dtis, lmats, dvbs, dkbgs = [], [], [], []
        for p, h, rows, hs, (beta, gam, dm, gl) in items:
            k, tinv = k_ref[rows, hs], ti_ref[h, rows, :]
            kb = k * beta
            dtis.append(_dot(du_ref[rows, hs], v_ref[rows, hs] * beta, NT)
                        + _dot(dw_ref[rows, hs], kb * jnp.exp(gam), NT))
            lmats.append(jnp.where(strict, _dot(kb, k, NT) * dm, 0.0))
            dvbs.append(_dot(tinv, du_ref[rows, hs], TN))
            dkbgs.append(_dot(tinv, dw_ref[rows, hs], TN))
        xs = [_dot(ti_ref[h, rows, :], dti, TN) for (p, h, rows, hs, g), dti in zip(items, dtis)]
        dls = [jnp.where(strict, -_dot(x, ti_ref[h, rows, :], NT), 0.0) for (p, h, rows, hs, g), x in zip(items, xs)]
        dgam_all = [jnp.zeros((PAIR, 128), F32) for _ in range(INTRA_PAIRS)]
        dbeta_all = [jnp.zeros((PAIR, 128), F32) for _ in range(INTRA_PAIRS)]
        for (p, h, rows, hs, (beta, gam, dm, gl)), dl, lmat, dvb, dkbg in zip(items, dls, lmats, dvbs, dkbgs):
            q, k, v = q_ref[rows, hs], k_ref[rows, hs], v_ref[rows, hs]
            a = a_ref[h, rows, :]
            dqg, dkd = dqg_ref[rows, hs], dkd_ref[rows, hs]
            kb = k * beta
            eg = jnp.exp(gam)
            ekd = jnp.exp(gl - gam)
            dmm = dl * dm
            dam = jnp.where(low, da_ref[h, rows, :], 0.0)
            dn = dam * dm
            e = dl * lmat + dam * a
            dkb = _dot(dmm, k) + dkbg * eg
            dk_ref[rows, hs] = _dot(dmm, kb, TN) + _dot(dn, q, TN) + dkd * ekd + dkb * beta
            dq_ref[rows, hs] = _dot(dn, k) + dqg * eg
            dv_ref[rows, hs] = dvb * beta
            t_kd = rsum(dkd * (k * ekd))
            dgam = rsum(e) - rsum(e.T) + rsum(dqg * (q * eg)) + rsum(dkbg * (kb * eg)) - t_kd
            for c in range(2):
                crows = slice(c * DN_CHUNK, (c + 1) * DN_CHUNK)
                dgl = (jnp.sum(t_kd[crows, :], axis=0, keepdims=True)
                       + ddl_ref[2 * p + c, h][0:1, 0:1] * dl_ref[2 * p + c, h][0:1, 0:1])
                dgam = dgam + jnp.where(rowi == (c + 1) * DN_CHUNK - 1, dgl, 0.0)
            dgam_all[p] += jnp.where(lane == DN_HEADS + h, dgam, 0.0)
            dbeta_all[p] += jnp.where(lane == h, rsum(dkb * k) + rsum(dvb * v), 0.0)
        for p in range(INTRA_PAIRS):
            dg_ref[p * PAIR:(p + 1) * PAIR, :] = dbeta_all[p] + _dot(low.astype(F32), dgam_all[p], TN, HI)

    rows_step = INTRA_PAIRS * PAIR
    tok = lambda w: pl.BlockSpec((rows_step, w), lambda n: (n, 0))
    hm = pl.BlockSpec((DN_HEADS, rows_step, PAIR), lambda n: (0, n, 0))
    sc = pl.BlockSpec((2 * INTRA_PAIRS, DN_HEADS, 8, 128), lambda n: (n, 0, 0, 0))
    return pl.pallas_call(
        body, name="dn_intra_bwd", grid=(npair // INTRA_PAIRS,),
        in_specs=[tok(DN_W), tok(DN_W), tok(DN_W), tok(128), pl.BlockSpec((8, rows_step), lambda n: (0, n)), hm, hm,
                  tok(DN_W), tok(DN_W), tok(DN_W), tok(DN_W), hm, sc, sc, pl.BlockSpec(memory_space=pl.ANY)],
        out_specs=[tok(DN_W), tok(DN_W), tok(DN_W), tok(128)],
        out_shape=[jax.ShapeDtypeStruct((t, DN_W), F32)] * 3 + [jax.ShapeDtypeStruct((t, 128), F32)],
        compiler_params=_params("parallel"),
    )(qn, kn, vc, gc, gr, tinv, a_qk, du, dw, dqg, dkd, da_qk, ddlast, dlast, dep)


def _dn_prep_bwd(pd, conv_w, al_vec, dt_vec, dqn, dkn, dvc, dgc, dz, tm):
    t = pd.shape[0]
    nt = t // tm
    tile = lambda i: nt - 1 - i

    def body(cur_ref, prev_ref, ba_ref, cw_ref, al_ref, dt_ref, dq_ref, dk_ref, dv_ref, dg_ref, dz_ref,
             o_ref, accw_ref, accg_ref, xs_ref, dc_ref, ds_ref, carry_ref):
        @pl.when(pl.program_id(0) == 0)
        def _():
            accw_ref[...] = jnp.zeros_like(accw_ref)
            accg_ref[...] = jnp.zeros_like(accg_ref)
            carry_ref[...] = jnp.zeros_like(carry_ref)

        _conv_window(cur_ref, prev_ref, xs_ref, tm, tile(pl.program_id(0)) > 0)
        taps = [xs_ref[pl.ds(HALO - DN_CONV + 1 + j, tm), :] for j in range(DN_CONV)]
        y = cw_ref[0:1, :] * taps[0]
        for j in range(1, DN_CONV):
            y += cw_ref[j:j + 1, :] * taps[j]
        sg = _sigmoid(y)
        c = y * sg
        for h in range(DN_HEADS):
            qs = slice(h * DN_HEAD_DIM, (h + 1) * DN_HEAD_DIM)
            ksl = slice(DN_W + h * DN_HEAD_DIM, DN_W + (h + 1) * DN_HEAD_DIM)
            for src, sl, scale in ((dq_ref, qs, DN_SCALE), (dk_ref, ksl, 1.0)):
                xh = c[:, sl]
                r = lax.rsqrt(jnp.sum(xh * xh, axis=-1, keepdims=True) + EPS)
                unit = xh * r
                dn = src[:, qs] * scale
                dc_ref[:, sl] = r * (dn - unit * jnp.sum(dn * unit, axis=-1, keepdims=True))
        dc_ref[:, 2 * DN_W:3 * DN_W] = dv_ref[...]
        dy = dc_ref[...] * (sg + y * sg * (1.0 - sg))
        for j in range(DN_CONV):
            accw_ref[j:j + 1, :] += jnp.sum(dy * taps[j], axis=0, keepdims=True)
        ds_ref[0:tm, :] = dy
        ds_ref[tm:tm + HALO, :] = carry_ref[...]
        carry_ref[...] = ds_ref[0:HALO, :]
        dx = cw_ref[0:1, :] * ds_ref[pl.ds(DN_CONV - 1, tm), :]
        for j in range(1, DN_CONV):
            dx += cw_ref[j:j + 1, :] * ds_ref[pl.ds(DN_CONV - 1 - j, tm), :]

        beta, g, pre = _gate_values(ba_ref[...], al_ref[...], dt_ref[...])
        dgb = dg_ref[...]
        lane = lax.broadcasted_iota(jnp.int32, dgb.shape, 1)
        is_b, is_a = lane < DN_HEADS, (lane >= DN_HEADS) & (lane < 2 * DN_HEADS)
        dpre = dgb * (-jnp.exp(al_ref[...])) * _sigmoid(pre)
        dba = jnp.where(is_b, dgb * beta * (1.0 - beta), jnp.where(is_a, dpre, 0.0))
        accg_ref[0:1, :] += jnp.sum(jnp.where(is_a, dgb * g, 0.0), axis=0, keepdims=True)
        accg_ref[1:2, :] += jnp.sum(jnp.where(is_a, dpre, 0.0), axis=0, keepdims=True)
        o_ref[:, 0:CONV_CH] = dx.astype(o_ref.dtype)
        o_ref[:, CONV_CH:CONV_CH + DN_W] = dz_ref[...].astype(o_ref.dtype)
        o_ref[:, CONV_CH + DN_W:DN_COLS] = dba.astype(o_ref.dtype)

    tok = lambda w: pl.BlockSpec((tm, w), lambda i: (tile(i), 0))
    return pl.pallas_call(
        body, name="dn_prep_bwd", grid=(nt,),
        in_specs=_dn_prep_specs(tm, tile) + [tok(DN_W), tok(DN_W), tok(DN_W), tok(128), tok(DN_W)],
        out_specs=[tok(DN_COLS), _full((8, CONV_CH)), _full((8, 128))],
        out_shape=[jax.ShapeDtypeStruct((t, DN_COLS), MXU_DTYPE),
                   jax.ShapeDtypeStruct((8, CONV_CH), F32), jax.ShapeDtypeStruct((8, 128), F32)],
        scratch_shapes=[pltpu.VMEM((HALO + tm, CONV_CH), F32), pltpu.VMEM((tm, CONV_CH), F32),
                        pltpu.VMEM((tm + HALO, CONV_CH), F32), pltpu.VMEM((HALO, CONV_CH), F32)],
        compiler_params=_params("arbitrary"),
    )(pd, pd, pd, conv_w, al_vec, dt_vec, dqn, dkn, dvc, dgc, dz)


def _pad_lanes(v, offset=0):
    return jnp.zeros((1, 128), F32).at[0, offset:offset + v.shape[0]].set(v.astype(F32))


class _LocalReducer:
    def start(self, grads):
        return jnp.zeros((8, 128), F32)

    def middle(self, after):
        return jnp.zeros((8, 128), F32)

    def finish(self, after):
        return None


def _local_step(x, p, tgt, sm, w, late, reducer):
    t = x.shape[0]
    tm = min(512, t // 2)
    tm_s = min(512, t // 2)
    tw = min(1024, t // 2)
    tw_ff = min(2048, t // 2)

    w_in_t = w["w_in_t"]
    wa_t = w_in_t[:ATTN_Q + 2 * ATTN_KV]
    wd_t = jnp.pad(w_in_t[ATTN_Q + 2 * ATTN_KV:], ((0, DN_COLS - (D_IN - ATTN_Q - 2 * ATTN_KV)), (0, 0)))
    conv_w = w["conv_w"]
    al_vec, dt_vec = _pad_lanes(sm["a_log"], DN_HEADS), _pad_lanes(sm["dt_bias"], DN_HEADS)
    sinks_vec = _pad_lanes(sm["sinks"])
    dn_w = sm["dn_norm"].reshape(1, 128)
    row = lambda v: v.reshape(1, D_MODEL)
    cos, sin = _rope_tables(t)

    u, pa, pd = _inproj(x, row(sm["norm_mix"]), wa_t, wd_t, tm_s)
    ao = _attn_fwd(pa, cos, sin, sinks_vec)
    qn, kn, vc, gc, gr = _dn_prep(pd, conv_w, al_vec, dt_vec, tm_s)
    uu, ww, qg, kd, a_qk, tinv, dlast = _dn_intra(qn, kn, vc, gc, gr)
    dn_out, o, vnew, sall = _dn_scan_fwd(uu, ww, qg, kd, a_qk, dlast, pd, dn_w)
    w_o, late_rest = late(dn_out)
    wo_a, wo_d = w_o[:ATTN_Q], w_o[ATTN_Q:]
    h1 = _oproj(x, ao, dn_out, wo_a, wo_d, tm)
    w = dict(w, **late_rest(h1))
    w_proj = jnp.transpose(w["w_proj4"], (1, 0, 2)).reshape(PLE_DIM, D_MODEL)
    m, r, h2 = _mlp_fwd(h1, row(sm["norm_mlp"]), w["w_up4"], w["w_down"], tw)
    dh2, dh2b, dgp, dpp, n3, pb, acc_ple = _ple_loss(h2, p, tgt, row(sm["norm_ple"]), row(sm["norm_final"]),
                                                     w["w_gate"], w_proj, tm_s)
    g_w_gate = _wgrad(n3, dgp, "wgrad_gate", D_MODEL, D_MODEL, tw)
    g_w_proj = _wgrad(pb, dpp, "wgrad_proj", PLE_DIM, D_MODEL, tw)
    da, dh1, dh1b, dao, ddn, acc_mlp = _mlp_bwd(dh2, dh2b, r, h1, row(sm["norm_mlp"]), w["w_up4"], w["w_down"],
                                                wo_a, wo_d, tm)
    g_w_up4 = _wgrad(m, da, "wgrad_up", D_MODEL, FF_BLOCK, tw_ff, stacked=True)
    g_w_down = _wgrad(r, dh2b, "wgrad_down", FF_BLOCK, D_MODEL, tw_ff,
                      prep=lambda rv: jnp.square(rv.astype(F32)).astype(MXU_DTYPE))
    g_w_o = _wgrad_cat([ao, dn_out], [dh1b], "wgrad_o", tw)
    early = dict(w_up4=g_w_up4, w_down=g_w_down, w_gate=g_w_gate, w_proj=g_w_proj, w_o=g_w_o)
    dep = reducer.start(early)
    dz, du, dw, dqg, dkd, da_qk, ddlast, acc_dn = _dn_scan_bwd(ddn, o, vnew, sall, ww, qg, kd, a_qk, dlast, pd, dn_w,
                                                               dep)
    dep = reducer.middle(du)
    dqn, dkn, dvc, dgc = _dn_intra_bwd(qn, kn, vc, gc, gr, tinv, a_qk, du, dw, dqg, dkd, da_qk, ddlast, dlast, dep)
    d_dn, acc_conv, acc_gate = _dn_prep_bwd(pd, conv_w, al_vec, dt_vec, dqn, dkn, dvc, dgc, dz, tm_s)
    dq, dk, dv, acc_attn = _attn_bwd(pa, cos, sin, sinks_vec, dao)
    reducer.finish(dq)
    wq_t, wk_t, wv_t = wa_t[:ATTN_Q], wa_t[ATTN_Q:ATTN_Q + ATTN_KV], wa_t[ATTN_Q + ATTN_KV:]
    dx, acc_mix = _inproj_bwd(x, dh1, row(sm["norm_mix"]), [dq, dk, dv, d_dn], [wq_t, wk_t, wv_t, wd_t], tm_s)

    g_w_in_t = _wgrad_cat([dq, dk, dv, d_dn], [u], "wgrad_in", tw)
    grads = dict(early, w_in_t=g_w_in_t)
    sums = dict(loss=acc_ple[2, 0], norm_final=acc_ple[0], norm_ple=acc_ple[1], norm_mlp=acc_mlp[0], norm_mix=acc_mix[0],
                dn_norm=acc_dn[0], sinks=acc_attn[0, :ATTN_HEADS], a_log=acc_gate[0, DN_HEADS:2 * DN_HEADS],
                dt_bias=acc_gate[1, DN_HEADS:2 * DN_HEADS], conv_w=acc_conv[:DN_CONV])
    return sums, dx, grads


MESH = pl.DeviceIdType.MESH
ANY = pl.BlockSpec(memory_space=pl.ANY)
N_CHIPS = 4
N_DEV = 8


def _place():
    x, y, c = lax.axis_index("x"), lax.axis_index("y"), lax.axis_index("c")
    chips = [(1 - x, y), (x, 1 - y), (1 - x, 1 - y)]
    return x, y, c, chips


def _gather_weights(shards, conv_s):
    n = len(shards)
    per = 7

    def body(*refs):
        in_refs, conv_ref = refs[:n], refs[n]
        out_refs, conv_out = refs[n + 1:2 * n + 1], refs[2 * n + 1]
        send_sems, recv_sems = refs[2 * n + 2:]
        x, y, c, chips = _place()
        sibling = (x, y, 1 - c)

        def blk(a, px, py, pc):
            hr = in_refs[a].shape[0] // 2
            return out_refs[a].at[2 * px + py, pl.ds(pc * hr, hr), :]

        def mine(a):
            hr = in_refs[a].shape[0] // 2
            return in_refs[a].at[pl.ds(c * hr, hr), :]

        def rcopy(a, k, block, to, src=None):
            return pltpu.make_async_remote_copy(
                src_ref=blk(a, *block) if src is None else src, dst_ref=blk(a, *block),
                send_sem=send_sems.at[per * a + k], recv_sem=recv_sems.at[per * a + k],
                device_id=to, device_id_type=MESH)

        def whole(a, to):
            return pltpu.make_async_remote_copy(
                src_ref=in_refs[a], dst_ref=out_refs[a].at[2 * x + y],
                send_sem=send_sems.at[per * a], recv_sem=recv_sems.at[per * a], device_id=to, device_id_type=MESH)

        def ccopy(j, to):
            return pltpu.make_async_remote_copy(
                src_ref=conv_ref, dst_ref=conv_out.at[2 * x + y],
                send_sem=send_sems.at[per * n + j], recv_sem=recv_sems.at[per * n + j],
                device_id=to, device_id_type=MESH)

        started = []
        for a in range(n):
            first = [whole(a, sibling)]
            first += [rcopy(a, 1 + j, (x, y, c), (*chip, c), src=mine(a)) for j, chip in enumerate(chips)]
            for cp in first:
                cp.start()
            started += first
        conv_sends = [ccopy(j, (*chip, c)) for j, chip in enumerate(chips)] + [ccopy(3, sibling)]
        for cp in conv_sends:
            cp.start()
        started += conv_sends
        for a in range(n):
            for j, chip in enumerate(chips):
                rcopy(a, 1 + j, (*chip, c), (x, y, c)).wait_recv()
                fwd = rcopy(a, 4 + j, (*chip, c), sibling)
                fwd.start()
                started.append(fwd)
        for a in range(n):
            whole(a, sibling).wait_recv()
            for j, chip in enumerate(chips):
                rcopy(a, 4 + j, (*chip, 1 - c), (x, y, c)).wait_recv()
        for j, chip in enumerate(chips + [(x, y)]):
            pltpu.make_async_remote_copy(
                src_ref=conv_ref, dst_ref=conv_out.at[2 * chip[0] + chip[1]],
                send_sem=send_sems.at[per * n + j], recv_sem=recv_sems.at[per * n + j],
                device_id=sibling, device_id_type=MESH).wait_recv()
        for cp in started:
            cp.wait_send()

    nsem = per * n + 4
    out_shape = [jax.ShapeDtypeStruct((N_CHIPS,) + s.shape, s.dtype) for s in shards]
    out_shape.append(jax.ShapeDtypeStruct((N_CHIPS,) + conv_s.shape, conv_s.dtype))
    return pl.pallas_call(
        body, name="gather_weights", in_specs=[ANY] * (n + 1), out_specs=[ANY] * (n + 1), out_shape=out_shape,
        scratch_shapes=[pltpu.SemaphoreType.DMA((nsem,)), pltpu.SemaphoreType.DMA((nsem,))],
    )(*shards, conv_s)


HBM = pl.BlockSpec(memory_space=pltpu.HBM)
SEM = pl.BlockSpec(memory_space=pltpu.SEMAPHORE)
EFFECT = pltpu.SideEffectType.DATAFLOW_SIDE_EFFECTING
LATE_COPIES = 7


def _late_copies(in_refs, land_refs, send_sems, recv_sems, only=None):
    x, y, c, chips = _place()
    sends, arrivals = [], []
    for a, (src, land) in enumerate(zip(in_refs, land_refs)):
        if only is not None and a not in only:
            continue
        hr = src.shape[0] // 2
        base = LATE_COPIES * a

        def cp(src_ref, dst_ref, s_idx, r_idx, to):
            return pltpu.make_async_remote_copy(src_ref=src_ref, dst_ref=dst_ref, send_sem=send_sems.at[base + s_idx],
                                                recv_sem=recv_sems.at[base + r_idx], device_id=to, device_id_type=MESH)

        sends.append(cp(src, land.at[2 * x + y], 0, 0, (x, y, 1 - c)))
        arrivals.append(cp(src, land.at[2 * x + y], 0, 0, (x, y, 1 - c)))
        for j, chip in enumerate(chips):
            for pc in range(2):
                half = src.at[pl.ds(c * hr, hr), :]
                sends.append(cp(half, land.at[2 * x + y, pl.ds(c * hr, hr), :], 1 + 2 * j + pc, 1 + 2 * j + c, (*chip, pc)))
                arrivals.append(cp(half, land.at[2 * chip[0] + chip[1], pl.ds(pc * hr, hr), :], 1 + 2 * j + pc,
                                   1 + 2 * j + pc, (*chip, pc)))
    return sends, arrivals


def _copies_start(name, build, nsem, srcs, land_shapes, after):
    n = len(srcs)

    def body(*refs):
        sends, _ = build(refs[:n], refs[n:2 * n], refs[2 * n + 1], refs[2 * n + 2])
        for cp in sends:
            cp.start()
        refs[-1][...] = jnp.zeros_like(refs[-1])

    lands = [pltpu.with_memory_space_constraint(lax.empty(s.shape, s.dtype), pltpu.HBM) for s in land_shapes]
    ins = [pltpu.with_memory_space_constraint(s, pltpu.HBM) for s in srcs]
    out = pl.pallas_call(
        body, name=name,
        out_shape=(pltpu.SemaphoreType.DMA((nsem,)), pltpu.SemaphoreType.DMA((nsem,)),
                   *[pltpu.HBM(s.shape, s.dtype) for s in srcs], *[pltpu.HBM(s.shape, s.dtype) for s in land_shapes],
                   jax.ShapeDtypeStruct((8, 128), F32)),
        in_specs=[HBM] * (2 * n) + [ANY],
        out_specs=(SEM, SEM, *[HBM] * (2 * n), pl.BlockSpec(memory_space=pltpu.VMEM)),
        input_output_aliases={i: 2 + i for i in range(2 * n)},
        compiler_params=pltpu.CompilerParams(has_side_effects=EFFECT),
    )(*ins, *lands, after)
    return out[0], out[1], out[2:2 + n], out[2 + n:2 + 2 * n], out[-1]


def _copies_wait(name, build, started, after):
    send_sems, recv_sems, srcs, lands, _ = started
    n = len(srcs)

    def body(*refs):
        sends, arrivals = build(refs[:n], refs[n:2 * n], refs[2 * n], refs[2 * n + 1])
        for cp in sends:
            cp.wait_send()
        for cp in arrivals:
            cp.wait_recv()

    out = pl.pallas_call(
        body, name=name,
        out_shape=(*[pltpu.HBM(s.shape, s.dtype) for s in srcs], *[pltpu.HBM(l.shape, l.dtype) for l in lands]),
        in_specs=[HBM] * (2 * n) + [SEM, SEM, ANY],
        out_specs=tuple([HBM] * (2 * n)),
        input_output_aliases={i: i for i in range(2 * n)},
        compiler_params=pltpu.CompilerParams(has_side_effects=EFFECT),
    )(*srcs, *lands, send_sems, recv_sems, after)
    return out[:n], out[n:]


def _exchange_copies(g_refs, got_refs, send_sems, recv_sems):
    x, y, c, _ = _place()
    sends, arrivals = [], []
    for a, (g, got) in enumerate(zip(g_refs, got_refs)):
        hr = g.shape[1] // 2
        cp = pltpu.make_async_remote_copy(
            src_ref=g.at[:, pl.ds((1 - c) * hr, hr), :], dst_ref=got, send_sem=send_sems.at[a],
            recv_sem=recv_sems.at[a], device_id=(x, y, 1 - c), device_id_type=MESH)
        sends.append(cp)
        arrivals.append(cp)
    return sends, arrivals


def _scatter_copies(s_refs, got_refs, send_sems, recv_sems):
    x, y, c, chips = _place()
    sends, arrivals = [], []
    for a, (s16, got) in enumerate(zip(s_refs, got_refs)):
        for j, chip in enumerate(chips):
            cp = pltpu.make_async_remote_copy(
                src_ref=s16.at[2 * chip[0] + chip[1]], dst_ref=got.at[j], send_sem=send_sems.at[3 * a + j],
                recv_sem=recv_sems.at[3 * a + j], device_id=(*chip, c), device_id_type=MESH)
            sends.append(cp)
            arrivals.append(cp)
    return sends, arrivals


def _share_halves(name, bufs, dep):
    n = len(bufs)

    def body(*refs):
        out_refs = refs[n + 1:2 * n + 1]
        send_sems, recv_sems = refs[2 * n + 1:]
        x, y, c, _ = _place()
        remote = [pltpu.make_async_remote_copy(
            src_ref=out_refs[a].at[c], dst_ref=out_refs[a].at[c], send_sem=send_sems.at[a], recv_sem=recv_sems.at[a],
            device_id=(x, y, 1 - c), device_id_type=MESH) for a in range(n)]
        for cp in remote:
            cp.start()
        for a in range(n):
            pltpu.make_async_remote_copy(
                src_ref=out_refs[a].at[c], dst_ref=out_refs[a].at[1 - c], send_sem=send_sems.at[a],
                recv_sem=recv_sems.at[a], device_id=(x, y, 1 - c), device_id_type=MESH).wait_recv()
        for cp in remote:
            cp.wait_send()

    return pl.pallas_call(
        body, name=name, in_specs=[ANY] * (n + 1), out_specs=[ANY] * n,
        out_shape=[jax.ShapeDtypeStruct(b.shape, b.dtype) for b in bufs],
        input_output_aliases={a: a for a in range(n)},
        scratch_shapes=[pltpu.SemaphoreType.DMA((n,)), pltpu.SemaphoreType.DMA((n,))],
    )(*bufs, dep)


SMALL_ROWS, SMALL_COLS = 16, CONV_CH


def _allreduce_small(block):
    m_per, ncol = block.shape

    def body(x_ref, sum_ref, all_ref, send_sems, recv_sems, local_sem):
        x, y, c, chips = _place()
        me, sibling = (x, y, c), (x, y, 1 - c)

        def rows(px, py, pc):
            return all_ref.at[pl.ds((4 * px + 2 * py + pc) * m_per, m_per), :]

        def copy(k, block_of, to, src=None):
            return pltpu.make_async_remote_copy(
                src_ref=rows(*block_of) if src is None else src, dst_ref=rows(*block_of),
                send_sem=send_sems.at[k], recv_sem=recv_sems.at[k], device_id=to, device_id_type=MESH)

        mine = pltpu.make_async_copy(x_ref, rows(*me), local_sem)
        mine.start()
        first = [copy(0, me, sibling, src=x_ref)]
        first += [copy(1 + j, me, (*chip, c), src=x_ref) for j, chip in enumerate(chips)]
        for cp in first:
            cp.start()
        passed = [copy(4 + j, (*chip, c), sibling) for j, chip in enumerate(chips)]
        for j, chip in enumerate(chips):
            copy(1 + j, (*chip, c), me).wait_recv()
            passed[j].start()
        copy(0, sibling, me).wait_recv()
        for j, chip in enumerate(chips):
            copy(4 + j, (*chip, 1 - c), me).wait_recv()
        for cp in first + passed:
            cp.wait_send()
        mine.wait()
        total = all_ref[0:m_per, :]
        for d in range(1, N_DEV):
            total = total + all_ref[d * m_per:(d + 1) * m_per, :]
        sum_ref[...] = total

    vm = pl.BlockSpec(memory_space=pltpu.VMEM)
    return pl.pallas_call(
        body, name="allreduce_small", in_specs=[vm], out_specs=vm,
        out_shape=jax.ShapeDtypeStruct((m_per, ncol), F32),
        scratch_shapes=[pltpu.VMEM((N_DEV * m_per, ncol), F32), pltpu.SemaphoreType.DMA((7,)),
                        pltpu.SemaphoreType.DMA((7,)), pltpu.SemaphoreType.DMA],
    )(block)


def _row_tile(rows, cols):
    tile = rows
    while tile * cols * 4 > (1 << 20) and tile % 16 == 0:
        tile //= 2
    return tile


def _elementwise(fn, name, ins, out_dtypes, dep):
    rows, cols = ins[0].shape
    tile = _row_tile(rows, cols)

    def body(*refs):
        outs = fn(*[r[...] for r in refs[:len(ins)]])
        for o_ref, o in zip(refs[len(ins) + 1:], outs):
            o_ref[...] = o.astype(o_ref.dtype)

    if tile * cols * 4 > (1 << 21) and cols % 512 == 0:
        spec = pl.BlockSpec((rows, 256), lambda i: (0, i))
        steps = cols // 256
    else:
        spec = pl.BlockSpec((tile, cols), lambda i: (i, 0))
        steps = rows // tile
    return pl.pallas_call(
        body, name=name, grid=(steps,), in_specs=[spec] * len(ins) + [pl.BlockSpec(memory_space=pl.ANY)],
        out_specs=[spec] * len(out_dtypes),
        out_shape=[jax.ShapeDtypeStruct((rows, cols), d) for d in out_dtypes],
        compiler_params=_params("parallel"),
    )(*ins, dep)


def _adamw_tile(w, g, m, v):
    m = ADAM_B1 * m + (1.0 - ADAM_B1) * g
    v = ADAM_B2 * v + (1.0 - ADAM_B2) * jnp.square(g)
    m_hat = m / (1.0 - ADAM_B1 ** ADAM_STEP)
    v_hat = v / (1.0 - ADAM_B2 ** ADAM_STEP)
    delta = -ADAM_LR * (m_hat / (jnp.sqrt(v_hat) + ADAM_EPS) + ADAM_WD * w)
    return delta, m, v


def _adamw(name, w, g, m, v, dep):
    return _elementwise(_adamw_tile, name, [w, g, m, v], [F32, F32, F32], dep)


def _chip_sum(name, g4, got, place):
    nchip, hr, cols = got.shape
    tile = _row_tile(hr, cols)
    nblk = hr // tile

    def body(pl_ref, g_ref, o_ref, s32_ref, s16_ref):
        s = g_ref[...] + o_ref[...]
        s16_ref[...] = s.astype(BF16)

        @pl.when(pl.program_id(1) == pl_ref[0])
        def _():
            s32_ref[...] = s

    spec = pl.BlockSpec((None, tile, cols), lambda i, k, pr: (k, i, 0))
    return pl.pallas_call(
        body, name=name,
        grid_spec=pltpu.PrefetchScalarGridSpec(
            num_scalar_prefetch=1, grid=(nblk, nchip),
            in_specs=[pl.BlockSpec((None, tile, cols), lambda i, k, pr: (k, pr[1] * nblk + i, 0)), spec],
            out_specs=[pl.BlockSpec((tile, cols), lambda i, k, pr: (i, 0)), spec]),
        out_shape=[jax.ShapeDtypeStruct((hr, cols), F32), jax.ShapeDtypeStruct(got.shape, BF16)],
        compiler_params=_params("parallel", "arbitrary"),
    )(place, g4, got)


def _mesh_sum(name, s32, got, place):
    hr, cols = s32.shape
    tile = _row_tile(hr, cols)

    def body(pl_ref, own_ref, g0_ref, g1_ref, g2_ref, o_ref):
        o_ref[...] = ((own_ref[...] + g0_ref[...].astype(F32)) + g1_ref[...].astype(F32)) + g2_ref[...].astype(F32)

    slab = lambda j: pl.BlockSpec((None, tile, cols), lambda i, pr: (j, i, 0))
    return pl.pallas_call(
        body, name=name,
        grid_spec=pltpu.PrefetchScalarGridSpec(
            num_scalar_prefetch=1, grid=(hr // tile,),
            in_specs=[pl.BlockSpec((tile, cols), lambda i, pr: (i, 0)), slab(0), slab(1), slab(2)],
            out_specs=pl.BlockSpec((None, tile, cols), lambda i, pr: (pr[1], i, 0))),
        out_shape=jax.ShapeDtypeStruct((2, hr, cols), F32),
        compiler_params=_params("parallel"),
    )(place, s32, got, got, got)


def _place_operand():
    return jnp.stack([2 * lax.axis_index("x") + lax.axis_index("y"), lax.axis_index("c")]).astype(jnp.int32)


W_IN_ROWS = 720
W_IN_GATHER_ROWS = 736


def _per_chip(name, g):
    if name == "w_in_t":
        rows = D_IN // N_CHIPS
        return jnp.stack([lax.slice_in_dim(g, rows * k, rows * k + W_IN_ROWS) for k in range(N_CHIPS)])
    if name == "w_proj":
        return jnp.transpose(g.reshape(PLE_DIM, N_CHIPS, D_MODEL // N_CHIPS), (1, 0, 2))
    if name == "w_up4":
        return g
    return g.reshape(N_CHIPS, g.shape[0] // N_CHIPS, g.shape[1])


class _EarlyReducer:
    def __init__(self, tag):
        self.tag = tag

    def start(self, grads):
        self.names = list(grads)
        self.place = _place_operand()
        slabs = [_per_chip(k, grads[k]) for k in self.names]
        halves = [jax.ShapeDtypeStruct((s.shape[0], s.shape[1] // 2, s.shape[2]), F32) for s in slabs]
        self.a = _copies_start(self.tag + "exchange_start", _exchange_copies, len(slabs), slabs, halves,
                               slabs[0][0, :8, :128])
        return self.a[-1]

    def middle(self, after):
        slabs, got = _copies_wait(self.tag + "exchange_wait", _exchange_copies, self.a, after)
        self.sums = [_chip_sum(self.tag + "chip_sum_" + k, s, g, self.place) for k, s, g in zip(self.names, slabs, got)]
        s16 = [s[1] for s in self.sums]
        lands = [jax.ShapeDtypeStruct((3,) + s.shape[1:], BF16) for s in s16]
        self.b = _copies_start(self.tag + "scatter_start", _scatter_copies, 3 * len(s16), s16, lands,
                               self.sums[0][0][:8, :128])
        return self.b[-1]

    def finish(self, after):
        _, got = _copies_wait(self.tag + "scatter_wait", _scatter_copies, self.b, after)
        self.bufs = {k: _mesh_sum(self.tag + "mesh_sum_" + k, s[0], g, self.place)
                     for k, s, g in zip(self.names, self.sums, got)}


def kernel(x, p, norm_mix, w_in, conv_w, a_log, dt_bias, dn_norm, sinks, w_o, norm_mlp, w_up, w_down, norm_ple, w_ple_gate, w_ple_proj, norm_final, loss_target, m_norm_mix, m_w_in, m_conv_w, m_a_log, m_dt_bias, m_dn_norm, m_sinks, m_w_o, m_norm_mlp, m_w_up, m_w_down, m_norm_ple, m_w_ple_gate, m_w_ple_proj, m_norm_final, v_norm_mix, v_w_in, v_conv_w, v_a_log, v_dt_bias, v_dn_norm, v_sinks, v_w_o, v_norm_mlp, v_w_up, v_w_down, v_norm_ple, v_w_ple_gate, v_w_ple_proj, v_norm_final):
    chip = 2 * lax.axis_index("x") + lax.axis_index("y")
    big = dict(w_in=w_in[0], w_o=w_o[0], w_up=w_up[0], w_down=w_down[0], w_gate=w_ple_gate[0], w_proj=w_ple_proj[0])
    big_m = dict(w_in=m_w_in[0], w_o=m_w_o[0], w_up=m_w_up[0], w_down=m_w_down[0], w_gate=m_w_ple_gate[0], w_proj=m_w_ple_proj[0])
    big_v = dict(w_in=v_w_in[0], w_o=v_w_o[0], w_up=v_w_up[0], w_down=v_w_down[0], w_gate=v_w_ple_gate[0], w_proj=v_w_ple_proj[0])
    names = list(big)

    rows_in = D_IN // N_CHIPS
    w_in_shard_t = jnp.pad(big["w_in"].T.astype(BF16), ((0, W_IN_GATHER_ROWS - rows_in), (0, 0)))
    w_in_all, conv_all = _gather_weights([w_in_shard_t], conv_w[0])
    late_names = names[1:]
    late_shards = [big[k].astype(BF16) for k in late_names]
    gather = _copies_start("gather_start", _late_copies, LATE_COPIES * len(late_shards), late_shards,
                           [jax.ShapeDtypeStruct((N_CHIPS,) + s.shape, BF16) for s in late_shards], w_in_all)
    token = gather[-1]
    w = dict(w_in_t=jnp.concatenate([w_in_all[k, :rows_in] for k in range(N_CHIPS)], axis=0),
             conv_w=jnp.transpose(conv_all, (1, 0, 2)).reshape(DN_CONV, CONV_CH))
    sm = dict(norm_mix=norm_mix[0] + token[0, 0], a_log=a_log[0], dt_bias=dt_bias[0], dn_norm=dn_norm[0],
              sinks=sinks[0], norm_mlp=norm_mlp[0], norm_ple=norm_ple[0], norm_final=norm_final)

    def late(after):
        first = functools.partial(_late_copies, only=(0,))
        srcs, lands = _copies_wait("gather_wait_o", first, gather, after)

        def rest(after2):
            others = functools.partial(_late_copies, only=tuple(range(1, len(late_names))))
            gw = dict(zip(late_names, _copies_wait("gather_wait_rest", others, gather[:2] + (srcs, lands, None), after2)[1]))
            return dict(w_up4=gw["w_up"], w_down=gw["w_down"].reshape(D_FF, D_MODEL),
                        w_gate=gw["w_gate"].reshape(D_MODEL, D_MODEL), w_proj4=gw["w_proj"])

        return lands[0].reshape(D_MODEL, D_MODEL), rest

    reducer = _EarlyReducer("early_")
    sums, grad_x, g = _local_step(x[0], p[0, 0], loss_target[0], sm, w, late, reducer)

    last = _EarlyReducer("last_")
    dep_a = last.start({"w_in_t": g["w_in_t"]})

    row = lambda v: jnp.zeros((SMALL_COLS,), F32).at[:v.shape[0]].set(v)
    misc = jnp.zeros((SMALL_COLS,), F32).at[0:4].set(sums["a_log"]).at[4:8].set(sums["dt_bias"]) \
        .at[8:16].set(sums["sinks"]).at[128:256].set(sums["dn_norm"]).at[256].set(sums["loss"])
    small = jnp.concatenate([sums["conv_w"], jnp.stack([row(sums["norm_mix"]), row(sums["norm_mlp"]), row(sums["norm_ple"]),
                                                        row(sums["norm_final"]), misc]),
                             jnp.zeros((SMALL_ROWS - 9, SMALL_COLS), F32)], axis=0)
    tot = _allreduce_small(small + dep_a[0, 0])
    dep_b = last.middle(tot)
    grad_key = dict(w_o="w_o", w_up="w_up4", w_down="w_down", w_gate="w_gate", w_proj="w_proj")
    full = _share_halves("share_halves", [reducer.bufs[grad_key[k]] for k in late_names], dep_b)
    red = {k: f.reshape(-1, f.shape[-1]) for k, f in zip(late_names, full)}
    loss = tot[8, 256]
    ncw = CONV_CH // N_CHIPS

    def pack(cw, nmix, nmlp, nple, nfin, al, dtb, sk, dnn):
        misc_p = jnp.zeros((SMALL_COLS,), F32).at[0:4].set(al).at[4:8].set(dtb).at[8:16].set(sk).at[128:256].set(dnn)
        cw_p = jnp.zeros((DN_CONV, SMALL_COLS), F32).at[:, :ncw].set(cw)
        return jnp.concatenate([cw_p, jnp.stack([row(nmix), row(nmlp), row(nple), row(nfin), misc_p]),
                                jnp.zeros((SMALL_ROWS - 9, SMALL_COLS), F32)], axis=0)

    def unpack(buf):
        return dict(conv_w=buf[0:4, :ncw][None], norm_mix=buf[4, :D_MODEL][None], norm_mlp=buf[5, :D_MODEL][None],
                    norm_ple=buf[6, :D_MODEL][None], norm_final=buf[7, :D_MODEL], a_log=buf[8, 0:4][None],
                    dt_bias=buf[8, 4:8][None], sinks=buf[8, 8:16][None], dn_norm=buf[8, 128:256][None])

    g_conv_shard = lax.dynamic_slice(tot[0:4], (0, chip * ncw), (DN_CONV, ncw))
    g_small = pack(g_conv_shard, tot[4, :D_MODEL], tot[5, :D_MODEL], tot[6, :D_MODEL], tot[7, :D_MODEL],
                   tot[8, 0:4], tot[8, 4:8], tot[8, 8:16], tot[8, 128:256])
    w_small = pack(conv_w[0], norm_mix[0], norm_mlp[0], norm_ple[0], norm_final, a_log[0], dt_bias[0], sinks[0], dn_norm[0])
    m_small = pack(m_conv_w[0], m_norm_mix[0], m_norm_mlp[0], m_norm_ple[0], m_norm_final, m_a_log[0], m_dt_bias[0],
                   m_sinks[0], m_dn_norm[0])
    v_small = pack(v_conv_w[0], v_norm_mix[0], v_norm_mlp[0], v_norm_ple[0], v_norm_final, v_a_log[0], v_dt_bias[0],
                   v_sinks[0], v_dn_norm[0])

    ref_name = dict(w_in="w_in", w_o="w_o", w_up="w_up", w_down="w_down", w_gate="w_ple_gate", w_proj="w_ple_proj")
    out_g, out_d, out_m, out_v = {}, {}, {}, {}

    def update(k, dep):
        d_k, m_k, v_k = _adamw("adamw_" + k, big[k], red[k], big_m[k], big_v[k], dep)
        out_g[ref_name[k]], out_d[ref_name[k]] = red[k][None], d_k[None]
        out_m[ref_name[k]], out_v[ref_name[k]] = m_k[None], v_k[None]
        return d_k

    for k in late_names:
        done = update(k, dep_b)
    small_out = _adamw("adamw_small", w_small, g_small, m_small, v_small, dep_b)
    d_s, m_s, v_s = (unpack(b) for b in small_out)
    g_s = unpack(g_small)
    for src, dst in ((g_s, out_g), (d_s, out_d), (m_s, out_m), (v_s, out_v)):
        dst.update(src)
    last.finish(done + small_out[0][0:1, 0:1])
    (w_in_full,) = _share_halves("share_halves_w_in", [last.bufs["w_in_t"]], dep_b)
    g_t = w_in_full.reshape(W_IN_ROWS, D_MODEL)[:D_IN // N_CHIPS]
    d_t, m_t, v_t = _adamw("adamw_w_in", big["w_in"].T, g_t, big_m["w_in"].T, big_v["w_in"].T, dep_b)
    out_g["w_in"], out_d["w_in"], out_m["w_in"], out_v["w_in"] = g_t.T[None], d_t.T[None], m_t.T[None], v_t.T[None]
    order = ["norm_mix", "w_in", "conv_w", "a_log", "dt_bias", "dn_norm", "sinks", "w_o", "norm_mlp", "w_up", "w_down",
             "norm_ple", "w_ple_gate", "w_ple_proj", "norm_final"]
    return (loss, grad_x[None], *[out_g[k] for k in order], *[out_d[k] for k in order],
            *[out_m[k] for k in order], *[out_v[k] for k in order])
```

```python
import functools

import jax
import jax.numpy as jnp
from jax import lax
from jax.experimental import pallas as pl
from jax.experimental.pallas import tpu as pltpu

F32 = jnp.float32
BF16 = jnp.bfloat16
MXU_DTYPE = jnp.bfloat16
HI = lax.Precision.HIGHEST

D_MODEL = 1024
PLE_DIM = 256
ATTN_HEADS = 8
ATTN_KV_HEADS = 2
ATTN_GROUPS = ATTN_HEADS // ATTN_KV_HEADS
ATTN_HEAD_DIM = 64
ATTN_BLOCK = 128
ROPE_THETA = 10000.0
DN_HEADS = 4
DN_HEAD_DIM = 128
DN_CONV = 4
DN_CHUNK = 64
D_FF = 4 * D_MODEL
EPS = 1e-6
ATTN_Q = ATTN_HEADS * ATTN_HEAD_DIM
ATTN_KV = ATTN_KV_HEADS * ATTN_HEAD_DIM
DN_W = DN_HEADS * DN_HEAD_DIM
CONV_CH = 3 * DN_W
D_IN = ATTN_Q + 2 * ATTN_KV + 4 * DN_W + 2 * DN_HEADS
DN_COLS = 4 * DN_W + 128
DN_SCALE = DN_HEAD_DIM ** -0.5
ATTN_SCALE = ATTN_HEAD_DIM ** -0.5
FF_BLOCKS = 4
FF_BLOCK = D_FF // FF_BLOCKS

ADAM_LR = 0.001
ADAM_B1 = 0.9
ADAM_B2 = 0.999
ADAM_EPS = 1e-08
ADAM_WD = 0.01
ADAM_STEP = 10

V7X_VMEM_BYTES = 64 * 1024 * 1024
VMEM_LIMIT = 48 * 1024 * 1024

NN = ((1,), (0,))
NT = ((1,), (1,))
TN = ((0,), (0,))


def _dot(a, b, dims=NN, prec=None):
    if a.dtype != b.dtype:
        a, b = a.astype(MXU_DTYPE), b.astype(MXU_DTYPE)
    return lax.dot_general(a, b, (dims, ((), ())), precision=prec, preferred_element_type=F32)


def _sigmoid(x):
    return 1.0 / (1.0 + jnp.exp(-x))


def _softplus(x):
    return jnp.maximum(x, 0.0) + jnp.log(1.0 + jnp.exp(-jnp.abs(x)))


def _params(*sem):
    return pltpu.CompilerParams(dimension_semantics=sem, vmem_limit_bytes=VMEM_LIMIT)


def _rms_fwd(xv, g):
    r = lax.rsqrt(jnp.mean(xv * xv, axis=-1, keepdims=True) + EPS)
    return xv * r * g


def _rms_bwd(xv, g, dn):
    r = lax.rsqrt(jnp.mean(xv * xv, axis=-1, keepdims=True) + EPS)
    xh = xv * r
    dg = jnp.sum(dn * xh, axis=0, keepdims=True)
    dxh = dn * g
    dx = r * (dxh - xh * jnp.mean(dxh * xh, axis=-1, keepdims=True))
    return dx, dg


def _full(shape):
    return pl.BlockSpec(shape, lambda *_: (0,) * len(shape))


def _inproj(x, g_mix, wa_t, wd_t, tm):
    t = x.shape[0]

    def body(x_ref, g_ref, wa_ref, wd_ref, u_ref, pa_ref, pd_ref):
        u = _rms_fwd(x_ref[...], g_ref[...]).astype(MXU_DTYPE)
        u_ref[...] = u
        pa_ref[...] = _dot(u, wa_ref[...], NT)
        pd_ref[...] = _dot(u, wd_ref[...], NT)

    na, nd = wa_t.shape[0], wd_t.shape[0]
    return pl.pallas_call(
        body, name="inproj", grid=(t // tm,),
        in_specs=[pl.BlockSpec((tm, D_MODEL), lambda i: (i, 0)), _full((1, D_MODEL)),
                  _full((na, D_MODEL)), _full((nd, D_MODEL))],
        out_specs=[pl.BlockSpec((tm, D_MODEL), lambda i: (i, 0)), pl.BlockSpec((tm, na), lambda i: (i, 0)),
                   pl.BlockSpec((tm, nd), lambda i: (i, 0))],
        out_shape=[jax.ShapeDtypeStruct((t, D_MODEL), MXU_DTYPE), jax.ShapeDtypeStruct((t, na), F32),
                   jax.ShapeDtypeStruct((t, nd), F32)],
        compiler_params=_params("parallel"),
    )(x, g_mix, wa_t, wd_t)


def _oproj(x, ao, dn, wo_a, wo_d, tm):
    t = x.shape[0]

    def body(x_ref, ao_ref, dn_ref, wa_ref, wd_ref, h_ref):
        h_ref[...] = (x_ref[...] + _dot(ao_ref[...].astype(MXU_DTYPE), wa_ref[...])
                      + _dot(dn_ref[...].astype(MXU_DTYPE), wd_ref[...]))

    half = ao.shape[1]
    return pl.pallas_call(
        body, name="oproj", grid=(t // tm,),
        in_specs=[pl.BlockSpec((tm, D_MODEL), lambda i: (i, 0)), pl.BlockSpec((tm, half), lambda i: (i, 0)),
                  pl.BlockSpec((tm, half), lambda i: (i, 0)), _full((half, D_MODEL)), _full((half, D_MODEL))],
        out_specs=pl.BlockSpec((tm, D_MODEL), lambda i: (i, 0)),
        out_shape=jax.ShapeDtypeStruct((t, D_MODEL), F32),
        compiler_params=_params("parallel"),
    )(x, ao, dn, wo_a, wo_d)


def _mlp_fwd(h1, g_mlp, w_up4, w_down, tm):
    t = h1.shape[0]

    def body(h_ref, g_ref, wu_ref, wd_ref, m_ref, r_ref, h2_ref, acc_ref):
        k = pl.program_id(1)

        @pl.when(k == 0)
        def _():
            m_ref[...] = _rms_fwd(h_ref[...], g_ref[...]).astype(MXU_DTYPE)
            acc_ref[...] = jnp.zeros_like(acc_ref)

        r = jnp.maximum(_dot(m_ref[...], wu_ref[...]), 0.0)
        r_ref[...] = r.astype(MXU_DTYPE)
        s = jnp.square(r).astype(MXU_DTYPE)
        acc_ref[...] += _dot(s, wd_ref[...])

        @pl.when(k == FF_BLOCKS - 1)
        def _():
            h2_ref[...] = h_ref[...] + acc_ref[...]

    return pl.pallas_call(
        body, name="mlp_fwd", grid=(t // tm, FF_BLOCKS),
        in_specs=[pl.BlockSpec((tm, D_MODEL), lambda i, k: (i, 0)), _full((1, D_MODEL)),
                  pl.BlockSpec((None, D_MODEL, FF_BLOCK), lambda i, k: (k, 0, 0)),
                  pl.BlockSpec((FF_BLOCK, D_MODEL), lambda i, k: (k, 0))],
        out_specs=[pl.BlockSpec((tm, D_MODEL), lambda i, k: (i, 0)), pl.BlockSpec((tm, FF_BLOCK), lambda i, k: (i, k)),
                   pl.BlockSpec((tm, D_MODEL), lambda i, k: (i, 0))],
        out_shape=[jax.ShapeDtypeStruct((t, D_MODEL), MXU_DTYPE), jax.ShapeDtypeStruct((t, D_FF), MXU_DTYPE),
                   jax.ShapeDtypeStruct((t, D_MODEL), F32)],
        scratch_shapes=[pltpu.VMEM((tm, D_MODEL), F32)],
        compiler_params=_params("parallel", "arbitrary"),
    )(h1, g_mlp, w_up4, w_down)


def _ple_loss(h2, p, tgt, g_ple, g_fin, w_gate, w_proj, tm):
    t = h2.shape[0]

    def body(h_ref, p_ref, t_ref, gp_ref, gf_ref, wg_ref, wp_ref,
             dh_ref, dhb_ref, dgp_ref, dpp_ref, n3_ref, pb_ref, acc_ref):
        @pl.when(pl.program_id(0) == 0)
        def _():
            acc_ref[...] = jnp.zeros_like(acc_ref)

        h = h_ref[...]
        g_ple_v, g_fin_v = gp_ref[...], gf_ref[...]
        n3 = _rms_fwd(h, g_ple_v).astype(MXU_DTYPE)
        n3_ref[...] = n3
        gate = _sigmoid(_dot(n3, wg_ref[...]))
        pb = p_ref[...].astype(MXU_DTYPE)
        pb_ref[...] = pb
        pp = _dot(pb, wp_ref[...])
        h3 = h + gate * pp
        r4 = lax.rsqrt(jnp.mean(h3 * h3, axis=-1, keepdims=True) + EPS)
        xh4 = h3 * r4
        e = xh4 * g_fin_v - t_ref[...]
        loss = 0.5 * jnp.sum(jnp.mean(e * e, axis=-1, keepdims=True), axis=0, keepdims=True)
        dy = e * (1.0 / D_MODEL)
        dg_fin = jnp.sum(dy * xh4, axis=0, keepdims=True)
        dxh = dy * g_fin_v
        dh3 = r4 * (dxh - xh4 * jnp.mean(dxh * xh4, axis=-1, keepdims=True))
        dpp_ref[...] = (dh3 * gate).astype(MXU_DTYPE)
        dgp = (dh3 * pp * gate * (1.0 - gate)).astype(MXU_DTYPE)
        dgp_ref[...] = dgp
        dn3 = _dot(dgp, wg_ref[...], NT)
        dx, dg_ple = _rms_bwd(h, g_ple_v, dn3)
        dh2 = dh3 + dx
        dh_ref[...] = dh2
        dhb_ref[...] = dh2.astype(MXU_DTYPE)
        acc_ref[0:1, :] += dg_fin
        acc_ref[1:2, :] += dg_ple
        acc_ref[2:3, :] += jnp.broadcast_to(loss, (1, D_MODEL))

    row = lambda w: pl.BlockSpec((tm, w), lambda i: (i, 0))
    return pl.pallas_call(
        body, name="ple_loss", grid=(t // tm,),
        in_specs=[row(D_MODEL), row(PLE_DIM), row(D_MODEL), _full((1, D_MODEL)), _full((1, D_MODEL)),
                  _full((D_MODEL, D_MODEL)), _full((PLE_DIM, D_MODEL))],
        out_specs=[row(D_MODEL), row(D_MODEL), row(D_MODEL), row(D_MODEL), row(D_MODEL), row(PLE_DIM),
                   _full((8, D_MODEL))],
        out_shape=[jax.ShapeDtypeStruct((t, D_MODEL), F32), jax.ShapeDtypeStruct((t, D_MODEL), MXU_DTYPE),
                   jax.ShapeDtypeStruct((t, D_MODEL), MXU_DTYPE), jax.ShapeDtypeStruct((t, D_MODEL), MXU_DTYPE),
                   jax.ShapeDtypeStruct((t, D_MODEL), MXU_DTYPE), jax.ShapeDtypeStruct((t, PLE_DIM), MXU_DTYPE),
                   jax.ShapeDtypeStruct((8, D_MODEL), F32)],
        compiler_params=_params("arbitrary"),
    )(h2, p, tgt, g_ple, g_fin, w_gate, w_proj)


def _mlp_bwd(dh2, dh2b, r, h1, g_mlp, w_up4, w_down, wo_a, wo_d, tm):
    t = h1.shape[0]
    half = wo_a.shape[0]

    def body(dh_ref, dhb_ref, r_ref, h_ref, g_ref, wu_ref, wd_ref, woa_ref, wod_ref,
             da_ref, dh1_ref, dh1b_ref, dao_ref, ddn_ref, acc_ref, dm_ref):
        i, k = pl.program_id(0), pl.program_id(1)

        @pl.when((i == 0) & (k == 0))
        def _():
            acc_ref[...] = jnp.zeros_like(acc_ref)

        @pl.when(k == 0)
        def _():
            dm_ref[...] = jnp.zeros_like(dm_ref)

        ds = _dot(dhb_ref[...], wd_ref[...], NT)
        da = (ds * (2.0 * r_ref[...].astype(F32))).astype(MXU_DTYPE)
        da_ref[...] = da
        dm_ref[...] += _dot(da, wu_ref[...], NT)

        @pl.when(k == FF_BLOCKS - 1)
        def _():
            dx, dg = _rms_bwd(h_ref[...], g_ref[...], dm_ref[...])
            dh1 = dh_ref[...] + dx
            dh1_ref[...] = dh1
            dh1b = dh1.astype(MXU_DTYPE)
            dh1b_ref[...] = dh1b
            dao_ref[...] = _dot(dh1b, woa_ref[...], NT)
            ddn_ref[...] = _dot(dh1b, wod_ref[...], NT)
            acc_ref[0:1, :] += dg

    tok = lambda w: pl.BlockSpec((tm, w), lambda i, k: (i, 0))
    return pl.pallas_call(
        body, name="mlp_bwd", grid=(t // tm, FF_BLOCKS),
        in_specs=[tok(D_MODEL), tok(D_MODEL), pl.BlockSpec((tm, FF_BLOCK), lambda i, k: (i, k)), tok(D_MODEL),
                  _full((1, D_MODEL)), pl.BlockSpec((None, D_MODEL, FF_BLOCK), lambda i, k: (k, 0, 0)),
                  pl.BlockSpec((FF_BLOCK, D_MODEL), lambda i, k: (k, 0)),
                  pl.BlockSpec((half, D_MODEL), lambda i, k: (0, 0)), pl.BlockSpec((half, D_MODEL), lambda i, k: (0, 0))],
        out_specs=[pl.BlockSpec((tm, FF_BLOCK), lambda i, k: (i, k)),
                   tok(D_MODEL), tok(D_MODEL), tok(half), tok(half), pl.BlockSpec((8, D_MODEL), lambda i, k: (0, 0))],
        out_shape=[jax.ShapeDtypeStruct((t, D_FF), MXU_DTYPE),
                   jax.ShapeDtypeStruct((t, D_MODEL), F32), jax.ShapeDtypeStruct((t, D_MODEL), MXU_DTYPE),
                   jax.ShapeDtypeStruct((t, half), F32), jax.ShapeDtypeStruct((t, half), F32),
                   jax.ShapeDtypeStruct((8, D_MODEL), F32)],
        scratch_shapes=[pltpu.VMEM((tm, D_MODEL), F32)],
        compiler_params=_params("arbitrary", "arbitrary"),
    )(dh2, dh2b, r, h1, g_mlp, w_up4, w_down, wo_a, wo_d)


def _inproj_bwd(x, dh1, g_mix, grads, weights, tm):
    t = x.shape[0]
    n = len(grads)

    def body(*refs):
        x_ref, dh_ref, g_ref = refs[:3]
        g_refs, w_refs = refs[3:3 + n], refs[3 + n:3 + 2 * n]
        dx_ref, acc_ref = refs[3 + 2 * n:]

        @pl.when(pl.program_id(0) == 0)
        def _():
            acc_ref[...] = jnp.zeros_like(acc_ref)

        du = _dot(g_refs[0][...], w_refs[0][...])
        for j in range(1, n):
            du += _dot(g_refs[j][...], w_refs[j][...])
        dx, dg = _rms_bwd(x_ref[...], g_ref[...], du)
        dx_ref[...] = dh_ref[...] + dx
        acc_ref[0:1, :] += dg

    tok = lambda w: pl.BlockSpec((tm, w), lambda i: (i, 0))
    return pl.pallas_call(
        body, name="inproj_bwd", grid=(t // tm,),
        in_specs=[tok(D_MODEL), tok(D_MODEL), _full((1, D_MODEL))] + [tok(g.shape[1]) for g in grads]
                 + [_full(w.shape) for w in weights],
        out_specs=[tok(D_MODEL), _full((8, D_MODEL))],
        out_shape=[jax.ShapeDtypeStruct((t, D_MODEL), F32), jax.ShapeDtypeStruct((8, D_MODEL), F32)],
        compiler_params=_params("arbitrary"),
    )(x, dh1, g_mix, *grads, *weights)


def _wgrad(a, b, name, tk, tn, tt, stacked=False, prep=None):
    t, kdim = a.shape
    ncols = b.shape[1]

    def body(a_ref, b_ref, o_ref):
        @pl.when(pl.program_id(2) == 0)
        def _():
            o_ref[...] = jnp.zeros_like(o_ref)

        av = a_ref[...] if prep is None else prep(a_ref[...])
        o_ref[...] += _dot(av, b_ref[...], TN)

    if stacked:
        out_spec = pl.BlockSpec((None, tk, tn), lambda i, j, s: (j, i, 0))
        out_shape = jax.ShapeDtypeStruct((ncols // tn, kdim, tn), F32)
    else:
        out_spec = pl.BlockSpec((tk, tn), lambda i, j, s: (i, j))
        out_shape = jax.ShapeDtypeStruct((kdim, ncols), F32)
    return pl.pallas_call(
        body, name=name, grid=(kdim // tk, ncols // tn, t // tt),
        in_specs=[pl.BlockSpec((tt, tk), lambda i, j, s: (s, i)), pl.BlockSpec((tt, tn), lambda i, j, s: (s, j))],
        out_specs=out_spec, out_shape=out_shape,
        compiler_params=_params("parallel", "parallel", "arbitrary"),
    )(a, b)


def _wgrad_cat(as_, bs, name, tt):
    t = as_[0].shape[0]
    heights = [a.shape[1] for a in as_]
    widths = [b.shape[1] for b in bs]

    def body(*refs):
        a_refs, b_refs, o_ref = refs[:len(as_)], refs[len(as_):-1], refs[-1]

        @pl.when(pl.program_id(0) == 0)
        def _():
            o_ref[...] = jnp.zeros_like(o_ref)

        row = 0
        for a_ref, k in zip(a_refs, heights):
            av = a_ref[...]
            col = 0
            for b_ref, n in zip(b_refs, widths):
                o_ref[row:row + k, col:col + n] += _dot(av, b_ref[...], TN)
                col += n
            row += k

    tok = lambda w: pl.BlockSpec((tt, w), lambda s: (s, 0))
    shape = (sum(heights), sum(widths))
    return pl.pallas_call(
        body, name=name, grid=(t // tt,),
        in_specs=[tok(k) for k in heights] + [tok(n) for n in widths],
        out_specs=_full(shape), out_shape=jax.ShapeDtypeStruct(shape, F32),
        compiler_params=_params("arbitrary"),
    )(*as_, *bs)


def _rope_tables(t):
    half = ATTN_HEAD_DIM // 2
    inv = 1.0 / (ROPE_THETA ** (jnp.arange(half, dtype=F32) * (2.0 / ATTN_HEAD_DIM)))
    ang = jnp.arange(t, dtype=F32)[:, None] * inv[None, :]
    cos, sin = jnp.cos(ang), jnp.sin(ang)
    cos2 = jnp.concatenate([cos, cos], axis=-1)
    sin2 = jnp.concatenate([-sin, sin], axis=-1)
    return jnp.tile(cos2, (1, 2)), jnp.tile(sin2, (1, 2))


def _swap_halves(tv):
    w = tv.shape[-1]
    lane = lax.broadcasted_iota(jnp.int32, tv.shape, tv.ndim - 1)
    first = (lane % ATTN_HEAD_DIM) < (ATTN_HEAD_DIM // 2)
    return jnp.where(first, pltpu.roll(tv, w - ATTN_HEAD_DIM // 2, tv.ndim - 1),
                     pltpu.roll(tv, ATTN_HEAD_DIM // 2, tv.ndim - 1))


def _rope(tv, cos, sin):
    return tv * cos + _swap_halves(tv) * sin


def _rope_bwd(dv, cos, sin):
    return dv * cos + _swap_halves(dv * sin)


def _attn_valid(first_block):
    c = lax.broadcasted_iota(jnp.int32, (2 * ATTN_BLOCK, ATTN_BLOCK), 0)
    r = lax.broadcasted_iota(jnp.int32, (2 * ATTN_BLOCK, ATTN_BLOCK), 1)
    return (c > r) & (c <= r + ATTN_BLOCK) & ((c >= ATTN_BLOCK) | jnp.logical_not(first_block))


def _attn_probs(st, sink, valid):
    s = jnp.where(valid, st * ATTN_SCALE, -jnp.inf)
    m = jnp.maximum(jnp.max(s, axis=0, keepdims=True), sink)
    e = jnp.where(valid, jnp.exp(s - m), 0.0)
    es = jnp.exp(sink - m)
    inv = 1.0 / (jnp.sum(e, axis=0, keepdims=True) + es)
    return e * inv, es * inv


def _lane_scalar(vec, idx):
    lane = lax.broadcasted_iota(jnp.int32, vec.shape, 1)
    return jnp.sum(jnp.where(lane == idx, vec, 0.0), axis=-1, keepdims=True)


ATTN_STEP = 2 * ATTN_BLOCK


def _attn_specs(ns):
    cur = lambda w, cb: pl.BlockSpec((ATTN_STEP, w), lambda i: (jnp.minimum(i, ns - 1), cb))
    prev = lambda w, cb: pl.BlockSpec((ATTN_BLOCK, w), lambda i: (jnp.maximum(2 * jnp.minimum(i, ns - 1) - 1, 0), cb))
    kcol, vcol = ATTN_Q // ATTN_KV, ATTN_Q // ATTN_KV + 1
    return [cur(ATTN_Q, 0), cur(ATTN_KV, kcol), prev(ATTN_KV, kcol), cur(ATTN_KV, vcol), prev(ATTN_KV, vcol),
            cur(ATTN_KV, 0), cur(ATTN_KV, 0), prev(ATTN_KV, 0), prev(ATTN_KV, 0), _full((1, 128))]


def _attn_windows(tp, tc):
    hsl = lambda hk: slice(hk * ATTN_HEAD_DIM, (hk + 1) * ATTN_HEAD_DIM)
    return [[jnp.concatenate([tp[:, hsl(hk)], tc[0:ATTN_BLOCK, hsl(hk)]], axis=0) for hk in range(ATTN_KV_HEADS)],
            [tc[:, hsl(hk)] for hk in range(ATTN_KV_HEADS)]]


def _attn_items():
    return [(s, h, slice(s * ATTN_BLOCK, (s + 1) * ATTN_BLOCK), slice(h * ATTN_HEAD_DIM, (h + 1) * ATTN_HEAD_DIM))
            for s in range(2) for h in range(ATTN_HEADS)]


def _attn_fwd(pa, cos, sin, sinks_vec):
    t = pa.shape[0]
    ns = t // ATTN_STEP

    def body(q_ref, kc_ref, kp_ref, vc_ref, vp_ref, cc_ref, sc_ref, cp_ref, sp_ref, sk_ref, o_ref):
        cc, sc = cc_ref[...], sc_ref[...]
        q = _rope(q_ref[...], jnp.tile(cc, (1, ATTN_Q // ATTN_KV)), jnp.tile(sc, (1, ATTN_Q // ATTN_KV)))
        kc = _rope(kc_ref[...], cc, sc)
        kp = _rope(kp_ref[...], cp_ref[...], sp_ref[...])
        sk = sk_ref[...]
        valids = [_attn_valid(pl.program_id(0) == 0), _attn_valid(False)]
        kwins = _attn_windows(kp, kc)
        vwins_t = [[v.T for v in vs] for vs in _attn_windows(vp_ref[...], vc_ref[...])]
        items = _attn_items()
        scores = [_dot(kwins[s][h // ATTN_GROUPS], q[rows, hs], NT) for s, h, rows, hs in items]
        probs = [_attn_probs(st, _lane_scalar(sk, h), valids[s])[0] for (s, h, rows, hs), st in zip(items, scores)]
        for (s, h, rows, hs), pt in zip(items, probs):
            o_ref[rows, hs] = _dot(vwins_t[s][h // ATTN_GROUPS], pt).T.astype(o_ref.dtype)

    return pl.pallas_call(
        body, name="attn_fwd", grid=(ns,),
        in_specs=_attn_specs(ns),
        out_specs=pl.BlockSpec((ATTN_STEP, ATTN_Q), lambda i: (i, 0)),
        out_shape=jax.ShapeDtypeStruct((t, ATTN_Q), MXU_DTYPE),
        compiler_params=_params("parallel"),
    )(pa, pa, pa, pa, pa, cos, sin, cos, sin, sinks_vec)


def _attn_bwd(pa, cos, sin, sinks_vec, dao):
    t = pa.shape[0]
    ns = t // ATTN_STEP
    lo, hi = slice(0, ATTN_BLOCK), slice(ATTN_BLOCK, ATTN_STEP)

    def body(q_ref, kc_ref, kp_ref, vc_ref, vp_ref, cc_ref, sc_ref, cp_ref, sp_ref, sk_ref, do_ref,
             dq_ref, dk_ref, dv_ref, acc_ref, dqr_ref, dkw_ref, dvw_ref, ck_ref, cv_ref):
        i = pl.program_id(0)

        @pl.when(i == 0)
        def _():
            acc_ref[...] = jnp.zeros_like(acc_ref)
            ck_ref[...] = jnp.zeros_like(ck_ref)
            cv_ref[...] = jnp.zeros_like(cv_ref)

        @pl.when(i < ns)
        def _():
            cc, sc = cc_ref[...], sc_ref[...]
            cq, sq = jnp.tile(cc, (1, ATTN_Q // ATTN_KV)), jnp.tile(sc, (1, ATTN_Q // ATTN_KV))
            q = _rope(q_ref[...], cq, sq)
            kc = _rope(kc_ref[...], cc, sc)
            kp = _rope(kp_ref[...], cp_ref[...], sp_ref[...])
            sk = sk_ref[...]
            do = do_ref[...]
            lane = lax.broadcasted_iota(jnp.int32, (1, 128), 1)
            dsink = jnp.zeros((1, 128), F32)
            valids = [_attn_valid(i == 0), _attn_valid(False)]
            kwins = _attn_windows(kp, kc)
            vwins = _attn_windows(vp_ref[...], vc_ref[...])
            kwins_t = [[kw.T for kw in kws] for kws in kwins]
            items = _attn_items()
            scores = [_dot(kwins[s][h // ATTN_GROUPS], q[rows, hs], NT) for s, h, rows, hs in items]
            dps = [_dot(vwins[s][h // ATTN_GROUPS], do[rows, hs], NT) for s, h, rows, hs in items]
            pts, dsts = {}, {}
            for (s, h, rows, hs), st, dp_t in zip(items, scores, dps):
                probs_t, psink = _attn_probs(st, _lane_scalar(sk, h), valids[s])
                delta = jnp.sum(probs_t * dp_t, axis=0, keepdims=True)
                pts[s, h] = probs_t
                dsts[s, h] = probs_t * (dp_t - delta) * ATTN_SCALE
                dsink += jnp.where(lane == h, jnp.sum(-psink * delta, axis=1, keepdims=True), 0.0)
            for s, h, rows, hs in items:
                dqr_ref[rows, hs] = _dot(kwins_t[s][h // ATTN_GROUPS], dsts[s, h]).T
            for s in range(2):
                rows = slice(s * ATTN_BLOCK, (s + 1) * ATTN_BLOCK)
                for hk in range(ATTN_KV_HEADS):
                    ks = slice(hk * ATTN_HEAD_DIM, (hk + 1) * ATTN_HEAD_DIM)
                    group = range(hk * ATTN_GROUPS, (hk + 1) * ATTN_GROUPS)
                    heads = [slice(h * ATTN_HEAD_DIM, (h + 1) * ATTN_HEAD_DIM) for h in group]
                    ds_g = jnp.concatenate([dsts[s, h] for h in group], axis=1)
                    p_g = jnp.concatenate([pts[s, h] for h in group], axis=1)
                    q_g = jnp.concatenate([q[rows, hs] for hs in heads], axis=0)
                    do_g = jnp.concatenate([do[rows, hs] for hs in heads], axis=0)
                    dkw_ref[s, :, ks] = _dot(ds_g, q_g)
                    dvw_ref[s, :, ks] = _dot(p_g, do_g)
            acc_ref[0:1, :] += dsink
            dq_ref[...] = _rope_bwd(dqr_ref[...], cq, sq).astype(dq_ref.dtype)
            dk_ref[lo, :] = ck_ref[lo, :].astype(dk_ref.dtype)
            dk_ref[hi, :] = (ck_ref[hi, :] + _rope_bwd(dkw_ref[0, lo, :], cp_ref[...], sp_ref[...])).astype(dk_ref.dtype)
            dv_ref[lo, :] = cv_ref[lo, :].astype(dv_ref.dtype)
            dv_ref[hi, :] = (cv_ref[hi, :] + dvw_ref[0, lo, :]).astype(dv_ref.dtype)
            ck_ref[lo, :] = _rope_bwd(dkw_ref[0, hi, :] + dkw_ref[1, lo, :], cc[lo, :], sc[lo, :])
            ck_ref[hi, :] = _rope_bwd(dkw_ref[1, hi, :], cc[hi, :], sc[hi, :])
            cv_ref[lo, :] = dvw_ref[0, hi, :] + dvw_ref[1, lo, :]
            cv_ref[hi, :] = dvw_ref[1, hi, :]

        @pl.when(i == ns)
        def _():
            dk_ref[...] = ck_ref[...].astype(dk_ref.dtype)
            dv_ref[...] = cv_ref[...].astype(dv_ref.dtype)

    prev_out = lambda w: pl.BlockSpec((ATTN_STEP, w), lambda i: (jnp.maximum(i - 1, 0), 0))
    return pl.pallas_call(
        body, name="attn_bwd", grid=(ns + 1,),
        in_specs=_attn_specs(ns) + [pl.BlockSpec((ATTN_STEP, ATTN_Q), lambda i: (jnp.minimum(i, ns - 1), 0))],
        out_specs=[pl.BlockSpec((ATTN_STEP, ATTN_Q), lambda i: (jnp.minimum(i, ns - 1), 0)), prev_out(ATTN_KV),
                   prev_out(ATTN_KV), _full((8, 128))],
        out_shape=[jax.ShapeDtypeStruct((t, ATTN_Q), MXU_DTYPE), jax.ShapeDtypeStruct((t, ATTN_KV), MXU_DTYPE),
                   jax.ShapeDtypeStruct((t, ATTN_KV), MXU_DTYPE), jax.ShapeDtypeStruct((8, 128), F32)],
        scratch_shapes=[pltpu.VMEM((ATTN_STEP, ATTN_Q), F32), pltpu.VMEM((2, ATTN_STEP, ATTN_KV), F32),
                        pltpu.VMEM((2, ATTN_STEP, ATTN_KV), F32), pltpu.VMEM((ATTN_STEP, ATTN_KV), F32),
                        pltpu.VMEM((ATTN_STEP, ATTN_KV), F32)],
        compiler_params=_params("arbitrary"),
    )(pa, pa, pa, pa, pa, cos, sin, cos, sin, sinks_vec, dao)


PAIR = 2 * DN_CHUNK
INTRA_PAIRS = 4
SCAN_PAIRS = 4
HALO = 8


def _conv_window(cur_ref, prev_ref, xs_ref, tm, has_prev):
    prev = jnp.where(has_prev, prev_ref[...], 0.0)
    xs_ref[0:HALO, :] = prev
    xs_ref[HALO:HALO + tm, :] = cur_ref[...]


def _conv_taps(xs_ref, cw_ref, tm):
    y = cw_ref[0:1, :] * xs_ref[pl.ds(HALO - DN_CONV + 1, tm), :]
    for j in range(1, DN_CONV):
        y += cw_ref[j:j + 1, :] * xs_ref[pl.ds(HALO - DN_CONV + 1 + j, tm), :]
    return y


def _gate_values(ba, al, dt):
    beta = _sigmoid(ba)
    pre = ba + dt
    g = -jnp.exp(al) * _softplus(pre)
    return beta, g, pre


def _dn_prep_specs(tm, tile):
    return [pl.BlockSpec((tm, CONV_CH), lambda i: (tile(i), 0)),
            pl.BlockSpec((HALO, CONV_CH), lambda i: (jnp.maximum(tile(i) * (tm // HALO) - 1, 0), 0)),
            pl.BlockSpec((tm, 128), lambda i: (tile(i), 4 * DN_W // 128)),
            _full((DN_CONV, CONV_CH)), _full((1, 128)), _full((1, 128))]


def _dn_prep(pd, conv_w, al_vec, dt_vec, tm):
    t = pd.shape[0]

    def body(cur_ref, prev_ref, ba_ref, cw_ref, al_ref, dt_ref, qn_ref, kn_ref, vc_ref, gc_ref, gr_ref, xs_ref):
        _conv_window(cur_ref, prev_ref, xs_ref, tm, pl.program_id(0) > 0)
        y = _conv_taps(xs_ref, cw_ref, tm)
        c = y * _sigmoid(y)
        for h in range(DN_HEADS):
            qs = slice(h * DN_HEAD_DIM, (h + 1) * DN_HEAD_DIM)
            ksl = slice(DN_W + h * DN_HEAD_DIM, DN_W + (h + 1) * DN_HEAD_DIM)
            qh, kh = c[:, qs], c[:, ksl]
            qn_ref[:, qs] = qh * lax.rsqrt(jnp.sum(qh * qh, axis=-1, keepdims=True) + EPS) * DN_SCALE
            kn_ref[:, qs] = kh * lax.rsqrt(jnp.sum(kh * kh, axis=-1, keepdims=True) + EPS)
        vc_ref[...] = c[:, 2 * DN_W:3 * DN_W]
        beta, g, _ = _gate_values(ba_ref[...], al_ref[...], dt_ref[...])
        lane = lax.broadcasted_iota(jnp.int32, beta.shape, 1)
        gb = jnp.where(lane < DN_HEADS, beta, jnp.where(lane < 2 * DN_HEADS, g, 0.0))
        gc_ref[...] = gb
        gr_ref[...] = gb.T[0:8, :]

    tok = lambda w: pl.BlockSpec((tm, w), lambda i: (i, 0))
    return pl.pallas_call(
        body, name="dn_prep", grid=(t // tm,),
        in_specs=_dn_prep_specs(tm, lambda i: i),
        out_specs=[tok(DN_W), tok(DN_W), tok(DN_W), tok(128), pl.BlockSpec((8, tm), lambda i: (0, i))],
        out_shape=[jax.ShapeDtypeStruct((t, DN_W), F32)] * 3 + [jax.ShapeDtypeStruct((t, 128), F32),
                                                                 jax.ShapeDtypeStruct((8, t), F32)],
        scratch_shapes=[pltpu.VMEM((HALO + tm, CONV_CH), F32)],
        compiler_params=_params("parallel"),
    )(pd, pd, pd, conv_w, al_vec, dt_vec)


def _pair_masks():
    r = lax.broadcasted_iota(jnp.int32, (PAIR, PAIR), 0)
    c = lax.broadcasted_iota(jnp.int32, (PAIR, PAIR), 1)
    same = (r < DN_CHUNK) == (c < DN_CHUNK)
    return same & (r >= c), same & (r > c)


def _lane_col(mat, idx):
    lane = lax.broadcasted_iota(jnp.int32, mat.shape, 1)
    return jnp.sum(jnp.where(lane == idx, mat, 0.0), axis=-1, keepdims=True)


def _pair_cumsums(gc, gr, low):
    lowf = low.astype(F32)
    return _dot(lowf, gc, NN, HI), _dot(gr, lowf, NT, HI)


def _pair_gates(gc, cum_c, cum_r, low, h):
    beta = _lane_col(gc, h)
    gam = _lane_col(cum_c, DN_HEADS + h)
    gam_row = cum_r[DN_HEADS + h:DN_HEADS + h + 1, :]
    dm = jnp.where(low, jnp.exp(jnp.where(low, gam - gam_row, 0.0)), 0.0)
    row = lax.broadcasted_iota(jnp.int32, gam.shape, 0)
    gl = jnp.where(row < DN_CHUNK, gam[DN_CHUNK - 1:DN_CHUNK, :], gam[PAIR - 1:PAIR, :])
    return beta, gam, dm, gl


def _split(a):
    hi = a.astype(BF16)
    return hi, (a - hi.astype(F32)).astype(BF16)


def _dot_split(a, b, dims=NN):
    (ah, al), (bh, bl) = a, b
    la, lb = (1, 1) if dims == TN else ((0, 1) if dims == NN else (0, 0))
    r = _dot(jnp.concatenate([ah, al], axis=la), jnp.concatenate([bh, bl], axis=lb), dims)
    m, n = r.shape[0] // 2, r.shape[1] // 2
    return (r[m:, n:] + (r[:m, n:] + r[m:, :n])) + r[:m, :n]


def _unit_lower_inverses(lmats):
    n = lmats[0].shape[0]
    r = lax.broadcasted_iota(jnp.int32, (n, n), 0)
    c = lax.broadcasted_iota(jnp.int32, (n, n), 1)
    same = lambda size: (r & ~(size - 1)) == (c & ~(size - 1))
    base = DN_CHUNK // 4
    diag = [jnp.where(same(base), l, 0.0) for l in lmats]
    accs = [(r == c).astype(F32) - d for d in diag]
    splits = [_split(d) for d in diag]
    step = 1
    while 2 * step < base:
        splits = [_split(_dot_split(s, s)) for s in splits]
        accs = [acc + _dot_split(_split(acc), s) for acc, s in zip(accs, splits)]
        step *= 2
    size = base
    while size < DN_CHUNK:
        below = same(2 * size) & jnp.logical_not(same(size))
        tb = [_dot(acc, jnp.where(below, l, 0.0)) for acc, l in zip(accs, lmats)]
        accs = [acc - _dot(t, acc) for acc, t in zip(accs, tb)]
        size *= 2
    return accs


def _dn_intra(qn, kn, vc, gc, gr):
    t = qn.shape[0]
    npair = t // PAIR
    rows_step = INTRA_PAIRS * PAIR

    def body(q_ref, k_ref, v_ref, gc_ref, gr_ref, u_ref, w_ref, qg_ref, kd_ref, a_ref, ti_ref, dl_ref):
        low, strict = _pair_masks()
        items = []
        for p in range(INTRA_PAIRS):
            rows = slice(p * PAIR, (p + 1) * PAIR)
            gc_v = gc_ref[rows, :]
            cum_c, cum_r = _pair_cumsums(gc_v, gr_ref[:, rows], low)
            for h in range(DN_HEADS):
                hs = slice(h * DN_HEAD_DIM, (h + 1) * DN_HEAD_DIM)
                items.append((p, h, rows, hs, _pair_gates(gc_v, cum_c, cum_r, low, h)))
        lmats = []
        for p, h, rows, hs, (beta, gam, dm, gl) in items:
            k = k_ref[rows, hs]
            lmats.append(jnp.where(strict, _dot(k * beta, k, NT) * dm, 0.0))
        tinvs = _unit_lower_inverses(lmats)
        for (p, h, rows, hs, (beta, gam, dm, gl)), tinv in zip(items, tinvs):
            q, k, v = q_ref[rows, hs], k_ref[rows, hs], v_ref[rows, hs]
            eg = jnp.exp(gam)
            u_ref[rows, hs] = _dot(tinv, v * beta)
            w_ref[rows, hs] = _dot(tinv, (k * beta) * eg).astype(w_ref.dtype)
            a_ref[h, rows, :] = _dot(q, k, NT) * dm
            ti_ref[h, rows, :] = tinv
            qg_ref[rows, hs] = (q * eg).astype(qg_ref.dtype)
            kd_ref[rows, hs] = (k * jnp.exp(gl - gam)).astype(kd_ref.dtype)
            for c in range(2):
                last = (c + 1) * DN_CHUNK - 1
                dl_ref[2 * p + c, h] = jnp.broadcast_to(jnp.exp(gam[last:last + 1, :]), (8, 128))

    tok = lambda w: pl.BlockSpec((rows_step, w), lambda n: (n, 0))
    hm = pl.BlockSpec((DN_HEADS, rows_step, PAIR), lambda n: (0, n, 0))
    return pl.pallas_call(
        body, name="dn_intra", grid=(npair // INTRA_PAIRS,),
        in_specs=[tok(DN_W), tok(DN_W), tok(DN_W), tok(128), pl.BlockSpec((8, rows_step), lambda n: (0, n))],
        out_specs=[tok(DN_W)] * 4 + [hm, hm, pl.BlockSpec((2 * INTRA_PAIRS, DN_HEADS, 8, 128), lambda n: (n, 0, 0, 0))],
        out_shape=[jax.ShapeDtypeStruct((t, DN_W), F32)] + [jax.ShapeDtypeStruct((t, DN_W), MXU_DTYPE)] * 3
                  + [jax.ShapeDtypeStruct((DN_HEADS, t, PAIR), F32)] * 2
                  + [jax.ShapeDtypeStruct((2 * npair, DN_HEADS, 8, 128), F32)],
        compiler_params=_params("parallel"),
    )(qn, kn, vc, gc, gr)


def _dn_scan_fwd(u, w, qg, kd, a_qk, dlast, pd, dn_w):
    t = u.shape[0]
    npair = t // PAIR

    def body(u_ref, w_ref, qg_ref, kd_ref, a_ref, dl_ref, z_ref, nw_ref, out_ref, o_ref, vn_ref, sall_ref, s_ref):
        @pl.when(pl.program_id(0) == 0)
        def _():
            s_ref[...] = jnp.zeros_like(s_ref)

        nw = nw_ref[...]
        for c in range(2 * SCAN_PAIRS):
            rows = slice(c * DN_CHUNK, (c + 1) * DN_CHUNK)
            diag = slice((c % 2) * DN_CHUNK, (c % 2 + 1) * DN_CHUNK)
            for h in range(DN_HEADS):
                hs = slice(h * DN_HEAD_DIM, (h + 1) * DN_HEAD_DIM)
                st = s_ref[h]
                sall_ref[c, h] = st
                vn_ref[rows, hs] = (u_ref[rows, hs] - _dot(w_ref[rows, hs], st)).astype(vn_ref.dtype)
            for h in range(DN_HEADS):
                hs = slice(h * DN_HEAD_DIM, (h + 1) * DN_HEAD_DIM)
                st, vn = s_ref[h], vn_ref[rows, hs]
                o = _dot(qg_ref[rows, hs], st) + _dot(a_ref[h, rows, diag], vn)
                s_ref[h] = st * dl_ref[c, h][0:1, :] + _dot(kd_ref[rows, hs], vn, TN)
                o_ref[rows, hs] = o
                z = z_ref[rows, hs]
                on = o * lax.rsqrt(jnp.mean(o * o, axis=-1, keepdims=True) + EPS) * nw
                out_ref[rows, hs] = (on * (z * _sigmoid(z))).astype(out_ref.dtype)

    rows_step = SCAN_PAIRS * PAIR
    tok = pl.BlockSpec((rows_step, DN_W), lambda n: (n, 0))
    hm = pl.BlockSpec((DN_HEADS, rows_step, PAIR), lambda n: (0, n, 0))
    return pl.pallas_call(
        body, name="dn_scan_fwd", grid=(npair // SCAN_PAIRS,),
        in_specs=[tok, tok, tok, tok, hm, pl.BlockSpec((2 * SCAN_PAIRS, DN_HEADS, 8, 128), lambda n: (n, 0, 0, 0)),
                  pl.BlockSpec((rows_step, DN_W), lambda n: (n, 3)), _full((1, 128))],
        out_specs=[tok, tok, tok,
                   pl.BlockSpec((2 * SCAN_PAIRS, DN_HEADS, DN_HEAD_DIM, DN_HEAD_DIM), lambda n: (n, 0, 0, 0))],
        out_shape=[jax.ShapeDtypeStruct((t, DN_W), MXU_DTYPE), jax.ShapeDtypeStruct((t, DN_W), F32),
                   jax.ShapeDtypeStruct((t, DN_W), MXU_DTYPE),
                   jax.ShapeDtypeStruct((2 * npair, DN_HEADS, DN_HEAD_DIM, DN_HEAD_DIM), F32)],
        scratch_shapes=[pltpu.VMEM((DN_HEADS, DN_HEAD_DIM, DN_HEAD_DIM), F32)],
        compiler_params=_params("arbitrary"),
    )(u, w, qg, kd, a_qk, dlast, pd, dn_w)


def _dn_scan_bwd(dout, o, vnew, sall, w, qg, kd, a_qk, dlast, pd, dn_w, dep):
    t = o.shape[0]
    npair = t // PAIR
    nstep = npair // SCAN_PAIRS
    rev = lambda n: nstep - 1 - n

    def body(do_ref, o_ref, vn_ref, sall_ref, w_ref, qg_ref, kd_ref, a_ref, dl_ref, z_ref, nw_ref, dep_ref,
             dz_ref, du_ref, dw_ref, dqg_ref, dkd_ref, da_ref, ddl_ref, acc_ref, ds_ref, dos_ref):
        @pl.when(pl.program_id(0) == 0)
        def _():
            ds_ref[...] = jnp.zeros_like(ds_ref)
            acc_ref[...] = jnp.zeros_like(acc_ref)

        nw = nw_ref[...]
        dnw = jnp.zeros((1, 128), F32)
        for h in range(DN_HEADS):
            hs = slice(h * DN_HEAD_DIM, (h + 1) * DN_HEAD_DIM)
            o, z, dout = o_ref[:, hs], z_ref[:, hs], do_ref[:, hs]
            r = lax.rsqrt(jnp.mean(o * o, axis=-1, keepdims=True) + EPS)
            oh = o * r
            sz = _sigmoid(z)
            dz_ref[:, hs] = dout * (oh * nw) * (sz + z * sz * (1.0 - sz))
            don = dout * (z * sz)
            dnw += jnp.sum(don * oh, axis=0, keepdims=True)
            doh = don * nw
            dos_ref[:, hs] = r * (doh - oh * jnp.mean(doh * oh, axis=-1, keepdims=True))
        acc_ref[0:1, :] += dnw
        for c in reversed(range(2 * SCAN_PAIRS)):
            rows = slice(c * DN_CHUNK, (c + 1) * DN_CHUNK)
            diag = slice((c % 2) * DN_CHUNK, (c % 2 + 1) * DN_CHUNK)
            other = slice((1 - c % 2) * DN_CHUNK, (2 - c % 2) * DN_CHUNK)
            for h in range(DN_HEADS):
                hs = slice(h * DN_HEAD_DIM, (h + 1) * DN_HEAD_DIM)
                do, st, dsp, vn = dos_ref[rows, hs], sall_ref[c, h], ds_ref[h], vn_ref[rows, hs]
                da_ref[h, rows, diag] = _dot(do, vn, NT)
                da_ref[h, rows, other] = jnp.zeros((DN_CHUNK, DN_CHUNK), F32)
                du_ref[rows, hs] = (_dot(a_ref[h, rows, diag], do, TN) + _dot(kd_ref[rows, hs], dsp)).astype(du_ref.dtype)
                dqg_ref[rows, hs] = _dot(do, st, NT)
                dkd_ref[rows, hs] = _dot(vn, dsp, NT)
                ddl = jnp.sum(jnp.sum(dsp * st, axis=1, keepdims=True), axis=0, keepdims=True)
                ddl_ref[c, h] = jnp.broadcast_to(ddl, (8, 128))
            for h in range(DN_HEADS):
                hs = slice(h * DN_HEAD_DIM, (h + 1) * DN_HEAD_DIM)
                do, st, dvn = dos_ref[rows, hs], sall_ref[c, h], du_ref[rows, hs]
                dw_ref[rows, hs] = (-_dot(dvn, st, NT)).astype(dw_ref.dtype)
                ds_ref[h] = (ds_ref[h] * dl_ref[c, h][0:1, :] + _dot(qg_ref[rows, hs], do, TN)
                             - _dot(w_ref[rows, hs], dvn, TN))

    rows_step = SCAN_PAIRS * PAIR
    tok = pl.BlockSpec((rows_step, DN_W), lambda n: (rev(n), 0))
    hm = pl.BlockSpec((DN_HEADS, rows_step, PAIR), lambda n: (0, rev(n), 0))
    sc = pl.BlockSpec((2 * SCAN_PAIRS, DN_HEADS, 8, 128), lambda n: (rev(n), 0, 0, 0))
    return pl.pallas_call(
        body, name="dn_scan_bwd", grid=(nstep,),
        in_specs=[tok, tok, tok,
                  pl.BlockSpec((2 * SCAN_PAIRS, DN_HEADS, DN_HEAD_DIM, DN_HEAD_DIM), lambda n: (rev(n), 0, 0, 0)),
                  tok, tok, tok, hm, sc, pl.BlockSpec((rows_step, DN_W), lambda n: (rev(n), 3)), _full((1, 128)),
                  pl.BlockSpec(memory_space=pl.ANY)],
        out_specs=[tok] * 5 + [hm, sc, _full((8, 128))],
        out_shape=[jax.ShapeDtypeStruct((t, DN_W), F32)] + [jax.ShapeDtypeStruct((t, DN_W), MXU_DTYPE)] * 2
                  + [jax.ShapeDtypeStruct((t, DN_W), F32)] * 2 + [jax.ShapeDtypeStruct((DN_HEADS, t, PAIR), F32),
                   jax.ShapeDtypeStruct((2 * npair, DN_HEADS, 8, 128), F32), jax.ShapeDtypeStruct((8, 128), F32)],
        scratch_shapes=[pltpu.VMEM((DN_HEADS, DN_HEAD_DIM, DN_HEAD_DIM), F32), pltpu.VMEM((SCAN_PAIRS * PAIR, DN_W), F32)],
        compiler_params=_params("arbitrary"),
    )(dout, o, vnew, sall, w, qg, kd, a_qk, dlast, pd, dn_w, dep)


def _dn_intra_bwd(qn, kn, vc, gc, gr, tinv, a_qk, du, dw, dqg, dkd, da_qk, ddlast, dlast, dep):
    t = qn.shape[0]
    npair = t // PAIR

    def body(q_ref, k_ref, v_ref, gc_ref, gr_ref, ti_ref, a_ref, du_ref, dw_ref, dqg_ref, dkd_ref, da_ref, ddl_ref, dl_ref,
             dep_ref, dq_ref, dk_ref, dv_ref, dg_ref):
        low, strict = _pair_masks()
        lane = lax.broadcasted_iota(jnp.int32, (PAIR, 128), 1)
        rowi = lax.broadcasted_iota(jnp.int32, (PAIR, 1), 0)
        rsum = lambda v: jnp.sum(v, axis=-1, keepdims=True)
        items = []
        for p in range(INTRA_PAIRS):
            rows = slice(p * PAIR, (p + 1) * PAIR)
            gc_v = gc_ref[rows, :]
            cum_c, cum_r = _pair_cumsums(gc_v, gr_ref[:, rows], low)
            for h in range(DN_HEADS):
                hs = slice(h * DN_HEAD_DIM, (h + 1) * DN_HEAD_DIM)
                items.append((p, h, rows, hs, _pair_gates(gc_v, cum_c, cum_r, low, h)))
        dtis, lmats, dvbs, dkbgs = [], [], [], []
        for p, h, rows, hs, (beta, gam, dm, gl) in items:
            k, tinv = k_ref[rows, hs], ti_ref[h, rows, :]
            kb = k * beta
            dtis.append(_dot(du_ref[rows, hs], v_ref[rows, hs] * beta, NT)
                        + _dot(dw_ref[rows, hs], kb * jnp.exp(gam), NT))
            lmats.append(jnp.where(strict, _dot(kb, k, NT) * dm, 0.0))
            dvbs.append(_dot(tinv, du_ref[rows, hs], TN))
            dkbgs.append(_dot(tinv, dw_ref[rows, hs], TN))
        xs = [_dot(ti_ref[h, rows, :], dti, TN) for (p, h, rows, hs, g), dti in zip(items, dtis)]
        dls = [jnp.where(strict, -_dot(x, ti_ref[h, rows, :], NT), 0.0) for (p, h, rows, hs, g), x in zip(items, xs)]
        dgam_all = [jnp.zeros((PAIR, 128), F32) for _ in range(INTRA_PAIRS)]
        dbeta_all = [jnp.zeros((PAIR, 128), F32) for _ in range(INTRA_PAIRS)]
        for (p, h, rows, hs, (beta, gam, dm, gl)), dl, lmat, dvb, dkbg in zip(items, dls, lmats, dvbs, dkbgs):
            q, k, v = q_ref[rows, hs], k_ref[rows, hs], v_ref[rows, hs]
            a = a_ref[h, rows, :]
            dqg, dkd = dqg_ref[rows, hs], dkd_ref[rows, hs]
            kb = k * beta
            eg = jnp.exp(gam)
            ekd = jnp.exp(gl - gam)
            dmm = dl * dm
            dam = jnp.where(low, da_ref[h, rows, :], 0.0)
            dn = dam * dm
            e = dl * lmat + dam * a
            dkb = _dot(dmm, k) + dkbg * eg
            dk_ref[rows, hs] = _dot(dmm, kb, TN) + _dot(dn, q, TN) + dkd * ekd + dkb * beta
            dq_ref[rows, hs] = _dot(dn, k) + dqg * eg
            dv_ref[rows, hs] = dvb * beta
            t_kd = rsum(dkd * (k * ekd))
            dgam = rsum(e) - rsum(e.T) + rsum(dqg * (q * eg)) + rsum(dkbg * (kb * eg)) - t_kd
            for c in range(2):
                crows = slice(c * DN_CHUNK, (c + 1) * DN_CHUNK)
                dgl = (jnp.sum(t_kd[crows, :], axis=0, keepdims=True)
                       + ddl_ref[2 * p + c, h][0:1, 0:1] * dl_ref[2 * p + c, h][0:1, 0:1])
                dgam = dgam + jnp.where(rowi == (c + 1) * DN_CHUNK - 1, dgl, 0.0)
            dgam_all[p] += jnp.where(lane == DN_HEADS + h, dgam, 0.0)
            dbeta_all[p] += jnp.where(lane == h, rsum(dkb * k) + rsum(dvb * v), 0.0)
        for p in range(INTRA_PAIRS):
            dg_ref[p * PAIR:(p + 1) * PAIR, :] = dbeta_all[p] + _dot(low.astype(F32), dgam_all[p], TN, HI)

    rows_step = INTRA_PAIRS * PAIR
    tok = lambda w: pl.BlockSpec((rows_step, w), lambda n: (n, 0))
    hm = pl.BlockSpec((DN_HEADS, rows_step, PAIR), lambda n: (0, n, 0))
    sc = pl.BlockSpec((2 * INTRA_PAIRS, DN_HEADS, 8, 128), lambda n: (n, 0, 0, 0))
    return pl.pallas_call(
        body, name="dn_intra_bwd", grid=(npair // INTRA_PAIRS,),
        in_specs=[tok(DN_W), tok(DN_W), tok(DN_W), tok(128), pl.BlockSpec((8, rows_step), lambda n: (0, n)), hm, hm,
                  tok(DN_W), tok(DN_W), tok(DN_W), tok(DN_W), hm, sc, sc, pl.BlockSpec(memory_space=pl.ANY)],
        out_specs=[tok(DN_W), tok(DN_W), tok(DN_W), tok(128)],
        out_shape=[jax.ShapeDtypeStruct((t, DN_W), F32)] * 3 + [jax.ShapeDtypeStruct((t, 128), F32)],
        compiler_params=_params("parallel"),
    )(qn, kn, vc, gc, gr, tinv, a_qk, du, dw, dqg, dkd, da_qk, ddlast, dlast, dep)


def _dn_prep_bwd(pd, conv_w, al_vec, dt_vec, dqn, dkn, dvc, dgc, dz, tm):
    t = pd.shape[0]
    nt = t // tm
    tile = lambda i: nt - 1 - i

    def body(cur_ref, prev_ref, ba_ref, cw_ref, al_ref, dt_ref, dq_ref, dk_ref, dv_ref, dg_ref, dz_ref,
             o_ref, accw_ref, accg_ref, xs_ref, dc_ref, ds_ref, carry_ref):
        @pl.when(pl.program_id(0) == 0)
        def _():
            accw_ref[...] = jnp.zeros_like(accw_ref)
            accg_ref[...] = jnp.zeros_like(accg_ref)
            carry_ref[...] = jnp.zeros_like(carry_ref)

        _conv_window(cur_ref, prev_ref, xs_ref, tm, tile(pl.program_id(0)) > 0)
        taps = [xs_ref[pl.ds(HALO - DN_CONV + 1 + j, tm), :] for j in range(DN_CONV)]
        y = cw_ref[0:1, :] * taps[0]
        for j in range(1, DN_CONV):
            y += cw_ref[j:j + 1, :] * taps[j]
        sg = _sigmoid(y)
        c = y * sg
        for h in range(DN_HEADS):
            qs = slice(h * DN_HEAD_DIM, (h + 1) * DN_HEAD_DIM)
            ksl = slice(DN_W + h * DN_HEAD_DIM, DN_W + (h + 1) * DN_HEAD_DIM)
            for src, sl, scale in ((dq_ref, qs, DN_SCALE), (dk_ref, ksl, 1.0)):
                xh = c[:, sl]
                r = lax.rsqrt(jnp.sum(xh * xh, axis=-1, keepdims=True) + EPS)
                unit = xh * r
                dn = src[:, qs] * scale
                dc_ref[:, sl] = r * (dn - unit * jnp.sum(dn * unit, axis=-1, keepdims=True))
        dc_ref[:, 2 * DN_W:3 * DN_W] = dv_ref[...]
        dy = dc_ref[...] * (sg + y * sg * (1.0 - sg))
        for j in range(DN_CONV):
            accw_ref[j:j + 1, :] += jnp.sum(dy * taps[j], axis=0, keepdims=True)
        ds_ref[0:tm, :] = dy
        ds_ref[tm:tm + HALO, :] = carry_ref[...]
        carry_ref[...] = ds_ref[0:HALO, :]
        dx = cw_ref[0:1, :] * ds_ref[pl.ds(DN_CONV - 1, tm), :]
        for j in range(1, DN_CONV):
            dx += cw_ref[j:j + 1, :] * ds_ref[pl.ds(DN_CONV - 1 - j, tm), :]

        beta, g, pre = _gate_values(ba_ref[...], al_ref[...], dt_ref[...])
        dgb = dg_ref[...]
        lane = lax.broadcasted_iota(jnp.int32, dgb.shape, 1)
        is_b, is_a = lane < DN_HEADS, (lane >= DN_HEADS) & (lane < 2 * DN_HEADS)
        dpre = dgb * (-jnp.exp(al_ref[...])) * _sigmoid(pre)
        dba = jnp.where(is_b, dgb * beta * (1.0 - beta), jnp.where(is_a, dpre, 0.0))
        accg_ref[0:1, :] += jnp.sum(jnp.where(is_a, dgb * g, 0.0), axis=0, keepdims=True)
        accg_ref[1:2, :] += jnp.sum(jnp.where(is_a, dpre, 0.0), axis=0, keepdims=True)
        o_ref[:, 0:CONV_CH] = dx.astype(o_ref.dtype)
        o_ref[:, CONV_CH:CONV_CH + DN_W] = dz_ref[...].astype(o_ref.dtype)
        o_ref[:, CONV_CH + DN_W:DN_COLS] = dba.astype(o_ref.dtype)

    tok = lambda w: pl.BlockSpec((tm, w), lambda i: (tile(i), 0))
    return pl.pallas_call(
        body, name="dn_prep_bwd", grid=(nt,),
        in_specs=_dn_prep_specs(tm, tile) + [tok(DN_W), tok(DN_W), tok(DN_W), tok(128), tok(DN_W)],
        out_specs=[tok(DN_COLS), _full((8, CONV_CH)), _full((8, 128))],
        out_shape=[jax.ShapeDtypeStruct((t, DN_COLS), MXU_DTYPE),
                   jax.ShapeDtypeStruct((8, CONV_CH), F32), jax.ShapeDtypeStruct((8, 128), F32)],
        scratch_shapes=[pltpu.VMEM((HALO + tm, CONV_CH), F32), pltpu.VMEM((tm, CONV_CH), F32),
                        pltpu.VMEM((tm + HALO, CONV_CH), F32), pltpu.VMEM((HALO, CONV_CH), F32)],
        compiler_params=_params("arbitrary"),
    )(pd, pd, pd, conv_w, al_vec, dt_vec, dqn, dkn, dvc, dgc, dz)


def _pad_lanes(v, offset=0):
    return jnp.zeros((1, 128), F32).at[0, offset:offset + v.shape[0]].set(v.astype(F32))


class _LocalReducer:
    def start(self, grads):
        return jnp.zeros((8, 128), F32)

    def middle(self, after):
        return jnp.zeros((8, 128), F32)

    def finish(self, after):
        return None


def _local_step(x, p, tgt, sm, w, late, reducer):
    t = x.shape[0]
    tm = min(512, t // 2)
    tm_s = min(512, t // 2)
    tw = min(1024, t // 2)
    tw_ff = min(2048, t // 2)

    attn_cols = ATTN_Q + 2 * ATTN_KV
    w_in_t = w["w_in_t"]
    w_in_t = jnp.pad(w_in_t, ((0, max(0, attn_cols + DN_COLS - w_in_t.shape[0])), (0, 0)))
    wa_t = w_in_t[:attn_cols]
    wd_t = w_in_t[attn_cols:attn_cols + DN_COLS]
    conv_w = w["conv_w"]
    al_vec, dt_vec = _pad_lanes(sm["a_log"], DN_HEADS), _pad_lanes(sm["dt_bias"], DN_HEADS)
    sinks_vec = _pad_lanes(sm["sinks"])
    dn_w = sm["dn_norm"].reshape(1, 128)
    row = lambda v: v.reshape(1, D_MODEL)
    cos, sin = _rope_tables(t)

    u, pa, pd = _inproj(x, row(sm["norm_mix"]), wa_t, wd_t, tm_s)
    ao = _attn_fwd(pa, cos, sin, sinks_vec)
    qn, kn, vc, gc, gr = _dn_prep(pd, conv_w, al_vec, dt_vec, tm_s)
    uu, ww, qg, kd, a_qk, tinv, dlast = _dn_intra(qn, kn, vc, gc, gr)
    dn_out, o, vnew, sall = _dn_scan_fwd(uu, ww, qg, kd, a_qk, dlast, pd, dn_w)
    w_o, late_rest = late(dn_out)
    wo_a, wo_d = w_o[:ATTN_Q], w_o[ATTN_Q:]
    h1 = _oproj(x, ao, dn_out, wo_a, wo_d, tm)
    w = dict(w, **late_rest(h1))
    w_proj = jnp.transpose(w["w_proj4"], (1, 0, 2)).reshape(PLE_DIM, D_MODEL)
    m, r, h2 = _mlp_fwd(h1, row(sm["norm_mlp"]), w["w_up4"], w["w_down"], tw)
    dh2, dh2b, dgp, dpp, n3, pb, acc_ple = _ple_loss(h2, p, tgt, row(sm["norm_ple"]), row(sm["norm_final"]),
                                                     w["w_gate"], w_proj, tm_s)
    g_w_gate = _wgrad(n3, dgp, "wgrad_gate", D_MODEL, D_MODEL, tw)
    g_w_proj = _wgrad(pb, dpp, "wgrad_proj", PLE_DIM, D_MODEL, tw)
    da, dh1, dh1b, dao, ddn, acc_mlp = _mlp_bwd(dh2, dh2b, r, h1, row(sm["norm_mlp"]), w["w_up4"], w["w_down"],
                                                wo_a, wo_d, tm)
    g_w_up4 = _wgrad(m, da, "wgrad_up", D_MODEL, FF_BLOCK, tw_ff, stacked=True)
    g_w_down = _wgrad(r, dh2b, "wgrad_down", FF_BLOCK, D_MODEL, tw_ff,
                      prep=lambda rv: jnp.square(rv.astype(F32)).astype(MXU_DTYPE))
    g_w_o = _wgrad_cat([ao, dn_out], [dh1b], "wgrad_o", tw)
    early = dict(w_up4=g_w_up4, w_down=g_w_down, w_gate=g_w_gate, w_proj=g_w_proj, w_o=g_w_o)
    dep = reducer.start(early)
    dz, du, dw, dqg, dkd, da_qk, ddlast, acc_dn = _dn_scan_bwd(ddn, o, vnew, sall, ww, qg, kd, a_qk, dlast, pd, dn_w,
                                                               dep)
    dep = reducer.middle(du)
    dqn, dkn, dvc, dgc = _dn_intra_bwd(qn, kn, vc, gc, gr, tinv, a_qk, du, dw, dqg, dkd, da_qk, ddlast, dlast, dep)
    d_dn, acc_conv, acc_gate = _dn_prep_bwd(pd, conv_w, al_vec, dt_vec, dqn, dkn, dvc, dgc, dz, tm_s)
    dq, dk, dv, acc_attn = _attn_bwd(pa, cos, sin, sinks_vec, dao)
    reducer.finish(dq)
    wq_t, wk_t, wv_t = wa_t[:ATTN_Q], wa_t[ATTN_Q:ATTN_Q + ATTN_KV], wa_t[ATTN_Q + ATTN_KV:]
    dx, acc_mix = _inproj_bwd(x, dh1, row(sm["norm_mix"]), [dq, dk, dv, d_dn], [wq_t, wk_t, wv_t, wd_t], tm_s)

    g_w_in_t = _wgrad_cat([dq, dk, dv, d_dn], [u], "wgrad_in", tw)
    grads = dict(early, w_in_t=g_w_in_t)
    sums = dict(loss=acc_ple[2, 0], norm_final=acc_ple[0], norm_ple=acc_ple[1], norm_mlp=acc_mlp[0], norm_mix=acc_mix[0],
                dn_norm=acc_dn[0], sinks=acc_attn[0, :ATTN_HEADS], a_log=acc_gate[0, DN_HEADS:2 * DN_HEADS],
                dt_bias=acc_gate[1, DN_HEADS:2 * DN_HEADS], conv_w=acc_conv[:DN_CONV])
    return sums, dx, grads


MESH = pl.DeviceIdType.MESH
ANY = pl.BlockSpec(memory_space=pl.ANY)
N_CHIPS = 4
N_DEV = 8


def _place():
    x, y, c = lax.axis_index("x"), lax.axis_index("y"), lax.axis_index("c")
    chips = [(1 - x, y), (x, 1 - y), (1 - x, 1 - y)]
    return x, y, c, chips


def _gather_weights(shards, conv_s):
    n = len(shards)
    per = 7

    def body(*refs):
        in_refs, conv_ref = refs[:n], refs[n]
        out_refs, conv_out = refs[n + 1:2 * n + 1], refs[2 * n + 1]
        send_sems, recv_sems = refs[2 * n + 2:]
        x, y, c, chips = _place()
        sibling = (x, y, 1 - c)

        def blk(a, px, py, pc):
            hr = in_refs[a].shape[0] // 2
            return out_refs[a].at[2 * px + py, pl.ds(pc * hr, hr), :]

        def mine(a):
            hr = in_refs[a].shape[0] // 2
            return in_refs[a].at[pl.ds(c * hr, hr), :]

        def rcopy(a, k, block, to, src=None):
            return pltpu.make_async_remote_copy(
                src_ref=blk(a, *block) if src is None else src, dst_ref=blk(a, *block),
                send_sem=send_sems.at[per * a + k], recv_sem=recv_sems.at[per * a + k],
                device_id=to, device_id_type=MESH)

        def whole(a, to):
            return pltpu.make_async_remote_copy(
                src_ref=in_refs[a], dst_ref=out_refs[a].at[2 * x + y],
                send_sem=send_sems.at[per * a], recv_sem=recv_sems.at[per * a], device_id=to, device_id_type=MESH)

        def ccopy(j, to):
            return pltpu.make_async_remote_copy(
                src_ref=conv_ref, dst_ref=conv_out.at[2 * x + y],
                send_sem=send_sems.at[per * n + j], recv_sem=recv_sems.at[per * n + j],
                device_id=to, device_id_type=MESH)

        started = []
        for a in range(n):
            first = [whole(a, sibling)]
            first += [rcopy(a, 1 + j, (x, y, c), (*chip, c), src=mine(a)) for j, chip in enumerate(chips)]
            for cp in first:
                cp.start()
            started += first
        conv_sends = [ccopy(j, (*chip, c)) for j, chip in enumerate(chips)] + [ccopy(3, sibling)]
        for cp in conv_sends:
            cp.start()
        started += conv_sends
        for a in range(n):
            for j, chip in enumerate(chips):
                rcopy(a, 1 + j, (*chip, c), (x, y, c)).wait_recv()
                fwd = rcopy(a, 4 + j, (*chip, c), sibling)
                fwd.start()
                started.append(fwd)
        for a in range(n):
            whole(a, sibling).wait_recv()
            for j, chip in enumerate(chips):
                rcopy(a, 4 + j, (*chip, 1 - c), (x, y, c)).wait_recv()
        for j, chip in enumerate(chips + [(x, y)]):
            pltpu.make_async_remote_copy(
                src_ref=conv_ref, dst_ref=conv_out.at[2 * chip[0] + chip[1]],
                send_sem=send_sems.at[per * n + j], recv_sem=recv_sems.at[per * n + j],
                device_id=sibling, device_id_type=MESH).wait_recv()
        for cp in started:
            cp.wait_send()

    nsem = per * n + 4
    out_shape = [jax.ShapeDtypeStruct((N_CHIPS,) + s.shape, s.dtype) for s in shards]
    out_shape.append(jax.ShapeDtypeStruct((N_CHIPS,) + conv_s.shape, conv_s.dtype))
    return pl.pallas_call(
        body, name="gather_weights", in_specs=[ANY] * (n + 1), out_specs=[ANY] * (n + 1), out_shape=out_shape,
        scratch_shapes=[pltpu.SemaphoreType.DMA((nsem,)), pltpu.SemaphoreType.DMA((nsem,))],
    )(*shards, conv_s)


HBM = pl.BlockSpec(memory_space=pltpu.HBM)
SEM = pl.BlockSpec(memory_space=pltpu.SEMAPHORE)
EFFECT = pltpu.SideEffectType.DATAFLOW_SIDE_EFFECTING
LATE_COPIES = 7


def _late_copies(in_refs, land_refs, send_sems, recv_sems, only=None):
    x, y, c, chips = _place()
    sends, arrivals = [], []
    for a, (src, land) in enumerate(zip(in_refs, land_refs)):
        if only is not None and a not in only:
            continue
        hr = src.shape[0] // 2
        base = LATE_COPIES * a

        def cp(src_ref, dst_ref, s_idx, r_idx, to):
            return pltpu.make_async_remote_copy(src_ref=src_ref, dst_ref=dst_ref, send_sem=send_sems.at[base + s_idx],
                                                recv_sem=recv_sems.at[base + r_idx], device_id=to, device_id_type=MESH)

        sends.append(cp(src, land.at[2 * x + y], 0, 0, (x, y, 1 - c)))
        arrivals.append(cp(src, land.at[2 * x + y], 0, 0, (x, y, 1 - c)))
        for j, chip in enumerate(chips):
            for pc in range(2):
                half = src.at[pl.ds(c * hr, hr), :]
                sends.append(cp(half, land.at[2 * x + y, pl.ds(c * hr, hr), :], 1 + 2 * j + pc, 1 + 2 * j + c, (*chip, pc)))
                arrivals.append(cp(half, land.at[2 * chip[0] + chip[1], pl.ds(pc * hr, hr), :], 1 + 2 * j + pc,
                                   1 + 2 * j + pc, (*chip, pc)))
    return sends, arrivals


def _copies_start(name, build, nsem, srcs, land_shapes, after):
    n = len(srcs)

    def body(*refs):
        sends, _ = build(refs[:n], refs[n:2 * n], refs[2 * n + 1], refs[2 * n + 2])
        for cp in sends:
            cp.start()
        refs[-1][...] = jnp.zeros_like(refs[-1])

    lands = [pltpu.with_memory_space_constraint(lax.empty(s.shape, s.dtype), pltpu.HBM) for s in land_shapes]
    ins = [pltpu.with_memory_space_constraint(s, pltpu.HBM) for s in srcs]
    out = pl.pallas_call(
        body, name=name,
        out_shape=(pltpu.SemaphoreType.DMA((nsem,)), pltpu.SemaphoreType.DMA((nsem,)),
                   *[pltpu.HBM(s.shape, s.dtype) for s in srcs], *[pltpu.HBM(s.shape, s.dtype) for s in land_shapes],
                   jax.ShapeDtypeStruct((8, 128), F32)),
        in_specs=[HBM] * (2 * n) + [ANY],
        out_specs=(SEM, SEM, *[HBM] * (2 * n), pl.BlockSpec(memory_space=pltpu.VMEM)),
        input_output_aliases={i: 2 + i for i in range(2 * n)},
        compiler_params=pltpu.CompilerParams(has_side_effects=EFFECT),
    )(*ins, *lands, after)
    return out[0], out[1], out[2:2 + n], out[2 + n:2 + 2 * n], out[-1]


def _copies_wait(name, build, started, after):
    send_sems, recv_sems, srcs, lands, _ = started
    n = len(srcs)

    def body(*refs):
        sends, arrivals = build(refs[:n], refs[n:2 * n], refs[2 * n], refs[2 * n + 1])
        for cp in sends:
            cp.wait_send()
        for cp in arrivals:
            cp.wait_recv()

    out = pl.pallas_call(
        body, name=name,
        out_shape=(*[pltpu.HBM(s.shape, s.dtype) for s in srcs], *[pltpu.HBM(l.shape, l.dtype) for l in lands]),
        in_specs=[HBM] * (2 * n) + [SEM, SEM, ANY],
        out_specs=tuple([HBM] * (2 * n)),
        input_output_aliases={i: i for i in range(2 * n)},
        compiler_params=pltpu.CompilerParams(has_side_effects=EFFECT),
    )(*srcs, *lands, send_sems, recv_sems, after)
    return out[:n], out[n:]


def _exchange_copies(g_refs, got_refs, send_sems, recv_sems):
    x, y, c, _ = _place()
    sends, arrivals = [], []
    for a, (g, got) in enumerate(zip(g_refs, got_refs)):
        hr = g.shape[1] // 2
        cp = pltpu.make_async_remote_copy(
            src_ref=g.at[:, pl.ds((1 - c) * hr, hr), :], dst_ref=got, send_sem=send_sems.at[a],
            recv_sem=recv_sems.at[a], device_id=(x, y, 1 - c), device_id_type=MESH)
        sends.append(cp)
        arrivals.append(cp)
    return sends, arrivals


def _scatter_copies(s_refs, got_refs, send_sems, recv_sems):
    x, y, c, chips = _place()
    sends, arrivals = [], []
    for a, (s16, got) in enumerate(zip(s_refs, got_refs)):
        for j, chip in enumerate(chips):
            cp = pltpu.make_async_remote_copy(
                src_ref=s16.at[2 * chip[0] + chip[1]], dst_ref=got.at[j], send_sem=send_sems.at[3 * a + j],
                recv_sem=recv_sems.at[3 * a + j], device_id=(*chip, c), device_id_type=MESH)
            sends.append(cp)
            arrivals.append(cp)
    return sends, arrivals


def _share_halves(name, bufs, dep):
    n = len(bufs)

    def body(*refs):
        out_refs = refs[n + 1:2 * n + 1]
        send_sems, recv_sems = refs[2 * n + 1:]
        x, y, c, _ = _place()
        remote = [pltpu.make_async_remote_copy(
            src_ref=out_refs[a].at[c], dst_ref=out_refs[a].at[c], send_sem=send_sems.at[a], recv_sem=recv_sems.at[a],
            device_id=(x, y, 1 - c), device_id_type=MESH) for a in range(n)]
        for cp in remote:
            cp.start()
        for a in range(n):
            pltpu.make_async_remote_copy(
                src_ref=out_refs[a].at[c], dst_ref=out_refs[a].at[1 - c], send_sem=send_sems.at[a],
                recv_sem=recv_sems.at[a], device_id=(x, y, 1 - c), device_id_type=MESH).wait_recv()
        for cp in remote:
            cp.wait_send()

    return pl.pallas_call(
        body, name=name, in_specs=[ANY] * (n + 1), out_specs=[ANY] * n,
        out_shape=[jax.ShapeDtypeStruct(b.shape, b.dtype) for b in bufs],
        input_output_aliases={a: a for a in range(n)},
        scratch_shapes=[pltpu.SemaphoreType.DMA((n,)), pltpu.SemaphoreType.DMA((n,))],
    )(*bufs, dep)


SMALL_ROWS, SMALL_COLS = 16, CONV_CH


def _allreduce_small(block):
    m_per, ncol = block.shape

    def body(x_ref, sum_ref, all_ref, send_sems, recv_sems, local_sem):
        x, y, c, chips = _place()
        me, sibling = (x, y, c), (x, y, 1 - c)

        def rows(px, py, pc):
            return all_ref.at[pl.ds((4 * px + 2 * py + pc) * m_per, m_per), :]

        def copy(k, block_of, to, src=None):
            return pltpu.make_async_remote_copy(
                src_ref=rows(*block_of) if src is None else src, dst_ref=rows(*block_of),
                send_sem=send_sems.at[k], recv_sem=recv_sems.at[k], device_id=to, device_id_type=MESH)

        mine = pltpu.make_async_copy(x_ref, rows(*me), local_sem)
        mine.start()
        first = [copy(0, me, sibling, src=x_ref)]
        first += [copy(1 + j, me, (*chip, c), src=x_ref) for j, chip in enumerate(chips)]
        for cp in first:
            cp.start()
        passed = [copy(4 + j, (*chip, c), sibling) for j, chip in enumerate(chips)]
        for j, chip in enumerate(chips):
            copy(1 + j, (*chip, c), me).wait_recv()
            passed[j].start()
        copy(0, sibling, me).wait_recv()
        for j, chip in enumerate(chips):
            copy(4 + j, (*chip, 1 - c), me).wait_recv()
        for cp in first + passed:
            cp.wait_send()
        mine.wait()
        total = all_ref[0:m_per, :]
        for d in range(1, N_DEV):
            total = total + all_ref[d * m_per:(d + 1) * m_per, :]
        sum_ref[...] = total

    vm = pl.BlockSpec(memory_space=pltpu.VMEM)
    return pl.pallas_call(
        body, name="allreduce_small", in_specs=[vm], out_specs=vm,
        out_shape=jax.ShapeDtypeStruct((m_per, ncol), F32),
        scratch_shapes=[pltpu.VMEM((N_DEV * m_per, ncol), F32), pltpu.SemaphoreType.DMA((7,)),
                        pltpu.SemaphoreType.DMA((7,)), pltpu.SemaphoreType.DMA],
    )(block)


def _row_tile(rows, cols):
    tile = rows
    while tile * cols * 4 > (1 << 20) and tile % 16 == 0:
        tile //= 2
    return tile


def _elementwise(fn, name, ins, out_dtypes, dep):
    rows, cols = ins[0].shape
    tile = _row_tile(rows, cols)

    def body(*refs):
        outs = fn(*[r[...] for r in refs[:len(ins)]])
        for o_ref, o in zip(refs[len(ins) + 1:], outs):
            o_ref[...] = o.astype(o_ref.dtype)

    if tile * cols * 4 > (1 << 21) and cols % 512 == 0:
        spec = pl.BlockSpec((rows, 256), lambda i: (0, i))
        steps = cols // 256
    else:
        spec = pl.BlockSpec((tile, cols), lambda i: (i, 0))
        steps = rows // tile
    return pl.pallas_call(
        body, name=name, grid=(steps,), in_specs=[spec] * len(ins) + [pl.BlockSpec(memory_space=pl.ANY)],
        out_specs=[spec] * len(out_dtypes),
        out_shape=[jax.ShapeDtypeStruct((rows, cols), d) for d in out_dtypes],
        compiler_params=_params("parallel"),
    )(*ins, dep)


def _adamw_tile(w, g, m, v):
    m = ADAM_B1 * m + (1.0 - ADAM_B1) * g
    v = ADAM_B2 * v + (1.0 - ADAM_B2) * jnp.square(g)
    m_hat = m / (1.0 - ADAM_B1 ** ADAM_STEP)
    v_hat = v / (1.0 - ADAM_B2 ** ADAM_STEP)
    delta = -ADAM_LR * (m_hat / (jnp.sqrt(v_hat) + ADAM_EPS) + ADAM_WD * w)
    return delta, m, v


def _adamw(name, w, g, m, v, dep):
    return _elementwise(_adamw_tile, name, [w, g, m, v], [F32, F32, F32], dep)


def _chip_sum(name, g4, got, place):
    nchip, hr, cols = got.shape
    tile = _row_tile(hr, cols)
    nblk = hr // tile

    def body(pl_ref, g_ref, o_ref, s32_ref, s16_ref):
        s = g_ref[...] + o_ref[...]
        s16_ref[...] = s.astype(BF16)

        @pl.when(pl.program_id(1) == pl_ref[0])
        def _():
            s32_ref[...] = s

    spec = pl.BlockSpec((None, tile, cols), lambda i, k, pr: (k, i, 0))
    return pl.pallas_call(
        body, name=name,
        grid_spec=pltpu.PrefetchScalarGridSpec(
            num_scalar_prefetch=1, grid=(nblk, nchip),
            in_specs=[pl.BlockSpec((None, tile, cols), lambda i, k, pr: (k, pr[1] * nblk + i, 0)), spec],
            out_specs=[pl.BlockSpec((tile, cols), lambda i, k, pr: (i, 0)), spec]),
        out_shape=[jax.ShapeDtypeStruct((hr, cols), F32), jax.ShapeDtypeStruct(got.shape, BF16)],
        compiler_params=_params("parallel", "arbitrary"),
    )(place, g4, got)


def _mesh_sum(name, s32, got, place):
    hr, cols = s32.shape
    tile = _row_tile(hr, cols)

    def body(pl_ref, own_ref, g0_ref, g1_ref, g2_ref, o_ref):
        o_ref[...] = ((own_ref[...] + g0_ref[...].astype(F32)) + g1_ref[...].astype(F32)) + g2_ref[...].astype(F32)

    slab = lambda j: pl.BlockSpec((None, tile, cols), lambda i, pr: (j, i, 0))
    return pl.pallas_call(
        body, name=name,
        grid_spec=pltpu.PrefetchScalarGridSpec(
            num_scalar_prefetch=1, grid=(hr // tile,),
            in_specs=[pl.BlockSpec((tile, cols), lambda i, pr: (i, 0)), slab(0), slab(1), slab(2)],
            out_specs=pl.BlockSpec((None, tile, cols), lambda i, pr: (pr[1], i, 0))),
        out_shape=jax.ShapeDtypeStruct((2, hr, cols), F32),
        compiler_params=_params("parallel"),
    )(place, s32, got, got, got)


def _place_operand():
    return jnp.stack([2 * lax.axis_index("x") + lax.axis_index("y"), lax.axis_index("c")]).astype(jnp.int32)


W_IN_ROWS = 720
W_IN_GATHER_ROWS = 736
BF16_TILE_ROWS = 16


def _join_w_in(blocks):
    rows, t = D_IN // N_CHIPS, BF16_TILE_ROWS
    first = [rows * k // t * t for k in range(N_CHIPS)]
    parts = []
    for k in range(N_CHIPS):
        lo = t if k else 0
        if k + 1 < N_CHIPS:
            hi = first[k + 1] - first[k]
            assert rows * (k + 1) <= first[k + 1] + t and hi + t <= W_IN_GATHER_ROWS
            parts += [blocks[k, lo:hi], blocks[k, hi:hi + t] + blocks[k + 1, :t]]
        else:
            parts.append(blocks[k, lo:])
    return jnp.concatenate(parts, axis=0)


def _per_chip(name, g):
    if name == "w_in_t":
        rows = D_IN // N_CHIPS
        return jnp.stack([lax.slice_in_dim(g, rows * k, rows * k + W_IN_ROWS) for k in range(N_CHIPS)])
    if name == "w_proj":
        return jnp.transpose(g.reshape(PLE_DIM, N_CHIPS, D_MODEL // N_CHIPS), (1, 0, 2))
    if name == "w_up4":
        return g
    return g.reshape(N_CHIPS, g.shape[0] // N_CHIPS, g.shape[1])


class _EarlyReducer:
    def __init__(self, tag):
        self.tag = tag

    def start(self, grads):
        self.names = list(grads)
        self.place = _place_operand()
        slabs = [_per_chip(k, grads[k]) for k in self.names]
        halves = [jax.ShapeDtypeStruct((s.shape[0], s.shape[1] // 2, s.shape[2]), F32) for s in slabs]
        self.a = _copies_start(self.tag + "exchange_start", _exchange_copies, len(slabs), slabs, halves,
                               slabs[0][0, :8, :128])
        return self.a[-1]

    def middle(self, after):
        slabs, got = _copies_wait(self.tag + "exchange_wait", _exchange_copies, self.a, after)
        self.sums = [_chip_sum(self.tag + "chip_sum_" + k, s, g, self.place) for k, s, g in zip(self.names, slabs, got)]
        s16 = [s[1] for s in self.sums]
        lands = [jax.ShapeDtypeStruct((3,) + s.shape[1:], BF16) for s in s16]
        self.b = _copies_start(self.tag + "scatter_start", _scatter_copies, 3 * len(s16), s16, lands,
                               self.sums[0][0][:8, :128])
        return self.b[-1]

    def finish(self, after):
        _, got = _copies_wait(self.tag + "scatter_wait", _scatter_copies, self.b, after)
        self.bufs = {k: _mesh_sum(self.tag + "mesh_sum_" + k, s[0], g, self.place)
                     for k, s, g in zip(self.names, self.sums, got)}


def kernel(x, p, norm_mix, w_in, conv_w, a_log, dt_bias, dn_norm, sinks, w_o, norm_mlp, w_up, w_down, norm_ple, w_ple_gate, w_ple_proj, norm_final, loss_target, m_norm_mix, m_w_in, m_conv_w, m_a_log, m_dt_bias, m_dn_norm, m_sinks, m_w_o, m_norm_mlp, m_w_up, m_w_down, m_norm_ple, m_w_ple_gate, m_w_ple_proj, m_norm_final, v_norm_mix, v_w_in, v_conv_w, v_a_log, v_dt_bias, v_dn_norm, v_sinks, v_w_o, v_norm_mlp, v_w_up, v_w_down, v_norm_ple, v_w_ple_gate, v_w_ple_proj, v_norm_final):
    chip = 2 * lax.axis_index("x") + lax.axis_index("y")
    big = dict(w_in=w_in[0], w_o=w_o[0], w_up=w_up[0], w_down=w_down[0], w_gate=w_ple_gate[0], w_proj=w_ple_proj[0])
    big_m = dict(w_in=m_w_in[0], w_o=m_w_o[0], w_up=m_w_up[0], w_down=m_w_down[0], w_gate=m_w_ple_gate[0], w_proj=m_w_ple_proj[0])
    big_v = dict(w_in=v_w_in[0], w_o=v_w_o[0], w_up=v_w_up[0], w_down=v_w_down[0], w_gate=v_w_ple_gate[0], w_proj=v_w_ple_proj[0])
    names = list(big)

    rows_in = D_IN // N_CHIPS
    chip_index = 2 * lax.axis_index("x") + lax.axis_index("y")
    w_in_shard_t = lax.dynamic_update_slice(jnp.zeros((W_IN_GATHER_ROWS, D_MODEL), BF16), big["w_in"].T.astype(BF16),
                                            ((rows_in * chip_index) % BF16_TILE_ROWS, 0))
    w_in_all, conv_all = _gather_weights([w_in_shard_t], conv_w[0])
    late_names = names[1:]
    late_shards = [big[k].astype(BF16) for k in late_names]
    gather = _copies_start("gather_start", _late_copies, LATE_COPIES * len(late_shards), late_shards,
                           [jax.ShapeDtypeStruct((N_CHIPS,) + s.shape, BF16) for s in late_shards], w_in_all)
    token = gather[-1]
    w = dict(w_in_t=_join_w_in(w_in_all),
             conv_w=jnp.transpose(conv_all, (1, 0, 2)).reshape(DN_CONV, CONV_CH))
    sm = dict(norm_mix=norm_mix[0] + token[0, 0], a_log=a_log[0], dt_bias=dt_bias[0], dn_norm=dn_norm[0],
              sinks=sinks[0], norm_mlp=norm_mlp[0], norm_ple=norm_ple[0], norm_final=norm_final)

    def late(after):
        first = functools.partial(_late_copies, only=(0,))
        srcs, lands = _copies_wait("gather_wait_o", first, gather, after)

        def rest(after2):
            others = functools.partial(_late_copies, only=tuple(range(1, len(late_names))))
            gw = dict(zip(late_names, _copies_wait("gather_wait_rest", others, gather[:2] + (srcs, lands, None), after2)[1]))
            return dict(w_up4=gw["w_up"], w_down=gw["w_down"].reshape(D_FF, D_MODEL),
                        w_gate=gw["w_gate"].reshape(D_MODEL, D_MODEL), w_proj4=gw["w_proj"])

        return lands[0].reshape(D_MODEL, D_MODEL), rest

    reducer = _EarlyReducer("early_")
    sums, grad_x, g = _local_step(x[0], p[0, 0], loss_target[0], sm, w, late, reducer)

    last = _EarlyReducer("last_")
    dep_a = last.start({"w_in_t": g["w_in_t"]})

    row = lambda v: jnp.zeros((SMALL_COLS,), F32).at[:v.shape[0]].set(v)
    misc = jnp.zeros((SMALL_COLS,), F32).at[0:4].set(sums["a_log"]).at[4:8].set(sums["dt_bias"]) \
        .at[8:16].set(sums["sinks"]).at[128:256].set(sums["dn_norm"]).at[256].set(sums["loss"])
    small = jnp.concatenate([sums["conv_w"], jnp.stack([row(sums["norm_mix"]), row(sums["norm_mlp"]), row(sums["norm_ple"]),
                                                        row(sums["norm_final"]), misc]),
                             jnp.zeros((SMALL_ROWS - 9, SMALL_COLS), F32)], axis=0)
    tot = _allreduce_small(small + dep_a[0, 0])
    dep_b = last.middle(tot)
    grad_key = dict(w_o="w_o", w_up="w_up4", w_down="w_down", w_gate="w_gate", w_proj="w_proj")
    full = _share_halves("share_halves", [reducer.bufs[grad_key[k]] for k in late_names], dep_b)
    red = {k: f.reshape(-1, f.shape[-1]) for k, f in zip(late_names, full)}
    loss = tot[8, 256]
    ncw = CONV_CH // N_CHIPS

    def pack(cw, nmix, nmlp, nple, nfin, al, dtb, sk, dnn):
        misc_p = jnp.zeros((SMALL_COLS,), F32).at[0:4].set(al).at[4:8].set(dtb).at[8:16].set(sk).at[128:256].set(dnn)
        cw_p = jnp.zeros((DN_CONV, SMALL_COLS), F32).at[:, :ncw].set(cw)
        return jnp.concatenate([cw_p, jnp.stack([row(nmix), row(nmlp), row(nple), row(nfin), misc_p]),
                                jnp.zeros((SMALL_ROWS - 9, SMALL_COLS), F32)], axis=0)

    def unpack(buf):
        return dict(conv_w=buf[0:4, :ncw][None], norm_mix=buf[4, :D_MODEL][None], norm_mlp=buf[5, :D_MODEL][None],
                    norm_ple=buf[6, :D_MODEL][None], norm_final=buf[7, :D_MODEL], a_log=buf[8, 0:4][None],
                    dt_bias=buf[8, 4:8][None], sinks=buf[8, 8:16][None], dn_norm=buf[8, 128:256][None])

    g_conv_shard = lax.dynamic_slice(tot[0:4], (0, chip * ncw), (DN_CONV, ncw))
    g_small = pack(g_conv_shard, tot[4, :D_MODEL], tot[5, :D_MODEL], tot[6, :D_MODEL], tot[7, :D_MODEL],
                   tot[8, 0:4], tot[8, 4:8], tot[8, 8:16], tot[8, 128:256])
    w_small = pack(conv_w[0], norm_mix[0], norm_mlp[0], norm_ple[0], norm_final, a_log[0], dt_bias[0], sinks[0], dn_norm[0])
    m_small = pack(m_conv_w[0], m_norm_mix[0], m_norm_mlp[0], m_norm_ple[0], m_norm_final, m_a_log[0], m_dt_bias[0],
                   m_sinks[0], m_dn_norm[0])
    v_small = pack(v_conv_w[0], v_norm_mix[0], v_norm_mlp[0], v_norm_ple[0], v_norm_final, v_a_log[0], v_dt_bias[0],
                   v_sinks[0], v_dn_norm[0])

    ref_name = dict(w_in="w_in", w_o="w_o", w_up="w_up", w_down="w_down", w_gate="w_ple_gate", w_proj="w_ple_proj")
    out_g, out_d, out_m, out_v = {}, {}, {}, {}

    def update(k, dep):
        d_k, m_k, v_k = _adamw("adamw_" + k, big[k], red[k], big_m[k], big_v[k], dep)
        out_g[ref_name[k]], out_d[ref_name[k]] = red[k][None], d_k[None]
        out_m[ref_name[k]], out_v[ref_name[k]] = m_k[None], v_k[None]
        return d_k

    for k in late_names:
        done = update(k, dep_b)
    small_out = _adamw("adamw_small", w_small, g_small, m_small, v_small, dep_b)
    d_s, m_s, v_s = (unpack(b) for b in small_out)
    g_s = unpack(g_small)
    for src, dst in ((g_s, out_g), (d_s, out_d), (m_s, out_m), (v_s, out_v)):
        dst.update(src)
    last.finish(done + small_out[0][0:1, 0:1])
    (w_in_full,) = _share_halves("share_halves_w_in", [last.bufs["w_in_t"]], dep_b)
    g_t = w_in_full.reshape(W_IN_ROWS, D_MODEL)[:D_IN // N_CHIPS]
    d_t, m_t, v_t = _adamw("adamw_w_in", big["w_in"].T, g_t, big_m["w_in"].T, big_v["w_in"].T, dep_b)
    out_g["w_in"], out_d["w_in"], out_m["w_in"], out_v["w_in"] = g_t.T[None], d_t.T[None], m_t.T[None], v_t.T[None]
    order = ["norm_mix", "w_in", "conv_w", "a_log", "dt_bias", "dn_norm", "sinks", "w_o", "norm_mlp", "w_up", "w_down",
             "norm_ple", "w_ple_gate", "w_ple_proj", "norm_final"]
    return (loss, grad_x[None], *[out_g[k] for k in order], *[out_d[k] for k in order],
            *[out_m[k] for k in order], *[out_v[k] for k in order])
```

```python
import functools

import jax
import jax.numpy as jnp
from jax import lax
from jax.experimental import pallas as pl
from jax.experimental.pallas import tpu as pltpu

F32 = jnp.float32
BF16 = jnp.bfloat16
MXU_DTYPE = jnp.bfloat16
HI = lax.Precision.HIGHEST

D_MODEL = 1024
PLE_DIM = 256
ATTN_HEADS = 8
ATTN_KV_HEADS = 2
ATTN_GROUPS = ATTN_HEADS // ATTN_KV_HEADS
ATTN_HEAD_DIM = 64
ATTN_BLOCK = 128
ROPE_THETA = 10000.0
DN_HEADS = 4
DN_HEAD_DIM = 128
DN_CONV = 4
DN_CHUNK = 64
D_FF = 4 * D_MODEL
EPS = 1e-6
ATTN_Q = ATTN_HEADS * ATTN_HEAD_DIM
ATTN_KV = ATTN_KV_HEADS * ATTN_HEAD_DIM
DN_W = DN_HEADS * DN_HEAD_DIM
CONV_CH = 3 * DN_W
D_IN = ATTN_Q + 2 * ATTN_KV + 4 * DN_W + 2 * DN_HEADS
DN_COLS = 4 * DN_W + 128
DN_SCALE = DN_HEAD_DIM ** -0.5
ATTN_SCALE = ATTN_HEAD_DIM ** -0.5
FF_BLOCKS = 4
FF_BLOCK = D_FF // FF_BLOCKS

ADAM_LR = 0.001
ADAM_B1 = 0.9
ADAM_B2 = 0.999
ADAM_EPS = 1e-08
ADAM_WD = 0.01
ADAM_STEP = 10

V7X_VMEM_BYTES = 64 * 1024 * 1024
VMEM_LIMIT = 48 * 1024 * 1024

NN = ((1,), (0,))
NT = ((1,), (1,))
TN = ((0,), (0,))


def _dot(a, b, dims=NN, prec=None):
    if a.dtype != b.dtype:
        a, b = a.astype(MXU_DTYPE), b.astype(MXU_DTYPE)
    return lax.dot_general(a, b, (dims, ((), ())), precision=prec, preferred_element_type=F32)


def _sigmoid(x):
    return 1.0 / (1.0 + jnp.exp(-x))


def _softplus(x):
    return jnp.maximum(x, 0.0) + jnp.log(1.0 + jnp.exp(-jnp.abs(x)))


def _params(*sem):
    return pltpu.CompilerParams(dimension_semantics=sem, vmem_limit_bytes=VMEM_LIMIT)


def _rms_fwd(xv, g):
    r = lax.rsqrt(jnp.mean(xv * xv, axis=-1, keepdims=True) + EPS)
    return xv * r * g


def _rms_bwd(xv, g, dn):
    r = lax.rsqrt(jnp.mean(xv * xv, axis=-1, keepdims=True) + EPS)
    xh = xv * r
    dg = jnp.sum(dn * xh, axis=0, keepdims=True)
    dxh = dn * g
    dx = r * (dxh - xh * jnp.mean(dxh * xh, axis=-1, keepdims=True))
    return dx, dg


def _full(shape):
    return pl.BlockSpec(shape, lambda *_: (0,) * len(shape))


def _inproj(x, g_mix, wa_t, wd_t, tm):
    t = x.shape[0]

    def body(x_ref, g_ref, wa_ref, wd_ref, u_ref, pa_ref, pd_ref):
        u = _rms_fwd(x_ref[...], g_ref[...]).astype(MXU_DTYPE)
        u_ref[...] = u
        pa_ref[...] = _dot(u, wa_ref[...], NT)
        pd_ref[...] = _dot(u, wd_ref[...], NT)

    na, nd = wa_t.shape[0], wd_t.shape[0]
    return pl.pallas_call(
        body, name="inproj", grid=(t // tm,),
        in_specs=[pl.BlockSpec((tm, D_MODEL), lambda i: (i, 0)), _full((1, D_MODEL)),
                  _full((na, D_MODEL)), _full((nd, D_MODEL))],
        out_specs=[pl.BlockSpec((tm, D_MODEL), lambda i: (i, 0)), pl.BlockSpec((tm, na), lambda i: (i, 0)),
                   pl.BlockSpec((tm, nd), lambda i: (i, 0))],
        out_shape=[jax.ShapeDtypeStruct((t, D_MODEL), MXU_DTYPE), jax.ShapeDtypeStruct((t, na), F32),
                   jax.ShapeDtypeStruct((t, nd), F32)],
        compiler_params=_params("parallel"),
    )(x, g_mix, wa_t, wd_t)


def _oproj(x, ao, dn, wo_a, wo_d, tm):
    t = x.shape[0]

    def body(x_ref, ao_ref, dn_ref, wa_ref, wd_ref, h_ref):
        h_ref[...] = (x_ref[...] + _dot(ao_ref[...].astype(MXU_DTYPE), wa_ref[...])
                      + _dot(dn_ref[...].astype(MXU_DTYPE), wd_ref[...]))

    half = ao.shape[1]
    return pl.pallas_call(
        body, name="oproj", grid=(t // tm,),
        in_specs=[pl.BlockSpec((tm, D_MODEL), lambda i: (i, 0)), pl.BlockSpec((tm, half), lambda i: (i, 0)),
                  pl.BlockSpec((tm, half), lambda i: (i, 0)), _full((half, D_MODEL)), _full((half, D_MODEL))],
        out_specs=pl.BlockSpec((tm, D_MODEL), lambda i: (i, 0)),
        out_shape=jax.ShapeDtypeStruct((t, D_MODEL), F32),
        compiler_params=_params("parallel"),
    )(x, ao, dn, wo_a, wo_d)


def _mlp_fwd(h1, g_mlp, w_up4, w_down, tm):
    t = h1.shape[0]

    def body(h_ref, g_ref, wu_ref, wd_ref, m_ref, r_ref, h2_ref, acc_ref):
        k = pl.program_id(1)

        @pl.when(k == 0)
        def _():
            m_ref[...] = _rms_fwd(h_ref[...], g_ref[...]).astype(MXU_DTYPE)
            acc_ref[...] = jnp.zeros_like(acc_ref)

        r = jnp.maximum(_dot(m_ref[...], wu_ref[...]), 0.0)
        r_ref[...] = r.astype(MXU_DTYPE)
        s = jnp.square(r).astype(MXU_DTYPE)
        acc_ref[...] += _dot(s, wd_ref[...])

        @pl.when(k == FF_BLOCKS - 1)
        def _():
            h2_ref[...] = h_ref[...] + acc_ref[...]

    return pl.pallas_call(
        body, name="mlp_fwd", grid=(t // tm, FF_BLOCKS),
        in_specs=[pl.BlockSpec((tm, D_MODEL), lambda i, k: (i, 0)), _full((1, D_MODEL)),
                  pl.BlockSpec((None, D_MODEL, FF_BLOCK), lambda i, k: (k, 0, 0)),
                  pl.BlockSpec((FF_BLOCK, D_MODEL), lambda i, k: (k, 0))],
        out_specs=[pl.BlockSpec((tm, D_MODEL), lambda i, k: (i, 0)), pl.BlockSpec((tm, FF_BLOCK), lambda i, k: (i, k)),
                   pl.BlockSpec((tm, D_MODEL), lambda i, k: (i, 0))],
        out_shape=[jax.ShapeDtypeStruct((t, D_MODEL), MXU_DTYPE), jax.ShapeDtypeStruct((t, D_FF), MXU_DTYPE),
                   jax.ShapeDtypeStruct((t, D_MODEL), F32)],
        scratch_shapes=[pltpu.VMEM((tm, D_MODEL), F32)],
        compiler_params=_params("parallel", "arbitrary"),
    )(h1, g_mlp, w_up4, w_down)


def _ple_loss(h2, p, tgt, g_ple, g_fin, w_gate, w_proj, tm):
    t = h2.shape[0]

    def body(h_ref, p_ref, t_ref, gp_ref, gf_ref, wg_ref, wp_ref,
             dh_ref, dhb_ref, dgp_ref, dpp_ref, n3_ref, pb_ref, acc_ref):
        @pl.when(pl.program_id(0) == 0)
        def _():
            acc_ref[...] = jnp.zeros_like(acc_ref)

        h = h_ref[...]
        g_ple_v, g_fin_v = gp_ref[...], gf_ref[...]
        n3 = _rms_fwd(h, g_ple_v).astype(MXU_DTYPE)
        n3_ref[...] = n3
        gate = _sigmoid(_dot(n3, wg_ref[...]))
        pb = p_ref[...].astype(MXU_DTYPE)
        pb_ref[...] = pb
        pp = _dot(pb, wp_ref[...])
        h3 = h + gate * pp
        r4 = lax.rsqrt(jnp.mean(h3 * h3, axis=-1, keepdims=True) + EPS)
        xh4 = h3 * r4
        e = xh4 * g_fin_v - t_ref[...]
        loss = 0.5 * jnp.sum(jnp.mean(e * e, axis=-1, keepdims=True), axis=0, keepdims=True)
        dy = e * (1.0 / D_MODEL)
        dg_fin = jnp.sum(dy * xh4, axis=0, keepdims=True)
        dxh = dy * g_fin_v
        dh3 = r4 * (dxh - xh4 * jnp.mean(dxh * xh4, axis=-1, keepdims=True))
        dpp_ref[...] = (dh3 * gate).astype(MXU_DTYPE)
        dgp = (dh3 * pp * gate * (1.0 - gate)).astype(MXU_DTYPE)
        dgp_ref[...] = dgp
        dn3 = _dot(dgp, wg_ref[...], NT)
        dx, dg_ple = _rms_bwd(h, g_ple_v, dn3)
        dh2 = dh3 + dx
        dh_ref[...] = dh2
        dhb_ref[...] = dh2.astype(MXU_DTYPE)
        acc_ref[0:1, :] += dg_fin
        acc_ref[1:2, :] += dg_ple
        acc_ref[2:3, :] += jnp.broadcast_to(loss, (1, D_MODEL))

    row = lambda w: pl.BlockSpec((tm, w), lambda i: (i, 0))
    return pl.pallas_call(
        body, name="ple_loss", grid=(t // tm,),
        in_specs=[row(D_MODEL), row(PLE_DIM), row(D_MODEL), _full((1, D_MODEL)), _full((1, D_MODEL)),
                  _full((D_MODEL, D_MODEL)), _full((PLE_DIM, D_MODEL))],
        out_specs=[row(D_MODEL), row(D_MODEL), row(D_MODEL), row(D_MODEL), row(D_MODEL), row(PLE_DIM),
                   _full((8, D_MODEL))],
        out_shape=[jax.ShapeDtypeStruct((t, D_MODEL), F32), jax.ShapeDtypeStruct((t, D_MODEL), MXU_DTYPE),
                   jax.ShapeDtypeStruct((t, D_MODEL), MXU_DTYPE), jax.ShapeDtypeStruct((t, D_MODEL), MXU_DTYPE),
                   jax.ShapeDtypeStruct((t, D_MODEL), MXU_DTYPE), jax.ShapeDtypeStruct((t, PLE_DIM), MXU_DTYPE),
                   jax.ShapeDtypeStruct((8, D_MODEL), F32)],
        compiler_params=_params("arbitrary"),
    )(h2, p, tgt, g_ple, g_fin, w_gate, w_proj)


def _mlp_bwd(dh2, dh2b, r, h1, g_mlp, w_up4, w_down, wo_a, wo_d, tm):
    t = h1.shape[0]
    half = wo_a.shape[0]

    def body(dh_ref, dhb_ref, r_ref, h_ref, g_ref, wu_ref, wd_ref, woa_ref, wod_ref,
             da_ref, dh1_ref, dh1b_ref, dao_ref, ddn_ref, acc_ref, dm_ref):
        i, k = pl.program_id(0), pl.program_id(1)

        @pl.when((i == 0) & (k == 0))
        def _():
            acc_ref[...] = jnp.zeros_like(acc_ref)

        @pl.when(k == 0)
        def _():
            dm_ref[...] = jnp.zeros_like(dm_ref)

        ds = _dot(dhb_ref[...], wd_ref[...], NT)
        da = (ds * (2.0 * r_ref[...].astype(F32))).astype(MXU_DTYPE)
        da_ref[...] = da
        dm_ref[...] += _dot(da, wu_ref[...], NT)

        @pl.when(k == FF_BLOCKS - 1)
        def _():
            dx, dg = _rms_bwd(h_ref[...], g_ref[...], dm_ref[...])
            dh1 = dh_ref[...] + dx
            dh1_ref[...] = dh1
            dh1b = dh1.astype(MXU_DTYPE)
            dh1b_ref[...] = dh1b
            dao_ref[...] = _dot(dh1b, woa_ref[...], NT)
            ddn_ref[...] = _dot(dh1b, wod_ref[...], NT)
            acc_ref[0:1, :] += dg

    tok = lambda w: pl.BlockSpec((tm, w), lambda i, k: (i, 0))
    return pl.pallas_call(
        body, name="mlp_bwd", grid=(t // tm, FF_BLOCKS),
        in_specs=[tok(D_MODEL), tok(D_MODEL), pl.BlockSpec((tm, FF_BLOCK), lambda i, k: (i, k)), tok(D_MODEL),
                  _full((1, D_MODEL)), pl.BlockSpec((None, D_MODEL, FF_BLOCK), lambda i, k: (k, 0, 0)),
                  pl.BlockSpec((FF_BLOCK, D_MODEL), lambda i, k: (k, 0)),
                  pl.BlockSpec((half, D_MODEL), lambda i, k: (0, 0)), pl.BlockSpec((half, D_MODEL), lambda i, k: (0, 0))],
        out_specs=[pl.BlockSpec((tm, FF_BLOCK), lambda i, k: (i, k)),
                   tok(D_MODEL), tok(D_MODEL), tok(half), tok(half), pl.BlockSpec((8, D_MODEL), lambda i, k: (0, 0))],
        out_shape=[jax.ShapeDtypeStruct((t, D_FF), MXU_DTYPE),
                   jax.ShapeDtypeStruct((t, D_MODEL), F32), jax.ShapeDtypeStruct((t, D_MODEL), MXU_DTYPE),
                   jax.ShapeDtypeStruct((t, half), F32), jax.ShapeDtypeStruct((t, half), F32),
                   jax.ShapeDtypeStruct((8, D_MODEL), F32)],
        scratch_shapes=[pltpu.VMEM((tm, D_MODEL), F32)],
        compiler_params=_params("arbitrary", "arbitrary"),
    )(dh2, dh2b, r, h1, g_mlp, w_up4, w_down, wo_a, wo_d)


def _inproj_bwd(x, dh1, g_mix, grads, weights, tm):
    t = x.shape[0]
    n = len(grads)

    def body(*refs):
        x_ref, dh_ref, g_ref = refs[:3]
        g_refs, w_refs = refs[3:3 + n], refs[3 + n:3 + 2 * n]
        dx_ref, acc_ref = refs[3 + 2 * n:]

        @pl.when(pl.program_id(0) == 0)
        def _():
            acc_ref[...] = jnp.zeros_like(acc_ref)

        du = _dot(g_refs[0][...], w_refs[0][...])
        for j in range(1, n):
            du += _dot(g_refs[j][...], w_refs[j][...])
        dx, dg = _rms_bwd(x_ref[...], g_ref[...], du)
        dx_ref[...] = dh_ref[...] + dx
        acc_ref[0:1, :] += dg

    tok = lambda w: pl.BlockSpec((tm, w), lambda i: (i, 0))
    return pl.pallas_call(
        body, name="inproj_bwd", grid=(t // tm,),
        in_specs=[tok(D_MODEL), tok(D_MODEL), _full((1, D_MODEL))] + [tok(g.shape[1]) for g in grads]
                 + [_full(w.shape) for w in weights],
        out_specs=[tok(D_MODEL), _full((8, D_MODEL))],
        out_shape=[jax.ShapeDtypeStruct((t, D_MODEL), F32), jax.ShapeDtypeStruct((8, D_MODEL), F32)],
        compiler_params=_params("arbitrary"),
    )(x, dh1, g_mix, *grads, *weights)


def _wgrad(a, b, name, tk, tn, tt, stacked=False, prep=None):
    t, kdim = a.shape
    ncols = b.shape[1]

    def body(a_ref, b_ref, o_ref):
        @pl.when(pl.program_id(2) == 0)
        def _():
            o_ref[...] = jnp.zeros_like(o_ref)

        av = a_ref[...] if prep is None else prep(a_ref[...])
        o_ref[...] += _dot(av, b_ref[...], TN)

    if stacked:
        out_spec = pl.BlockSpec((None, tk, tn), lambda i, j, s: (j, i, 0))
        out_shape = jax.ShapeDtypeStruct((ncols // tn, kdim, tn), F32)
    else:
        out_spec = pl.BlockSpec((tk, tn), lambda i, j, s: (i, j))
        out_shape = jax.ShapeDtypeStruct((kdim, ncols), F32)
    return pl.pallas_call(
        body, name=name, grid=(kdim // tk, ncols // tn, t // tt),
        in_specs=[pl.BlockSpec((tt, tk), lambda i, j, s: (s, i)), pl.BlockSpec((tt, tn), lambda i, j, s: (s, j))],
        out_specs=out_spec, out_shape=out_shape,
        compiler_params=_params("parallel", "parallel", "arbitrary"),
    )(a, b)


def _wgrad_cat(as_, bs, name, tt):
    t = as_[0].shape[0]
    heights = [a.shape[1] for a in as_]
    widths = [b.shape[1] for b in bs]

    def body(*refs):
        a_refs, b_refs, o_ref = refs[:len(as_)], refs[len(as_):-1], refs[-1]

        @pl.when(pl.program_id(0) == 0)
        def _():
            o_ref[...] = jnp.zeros_like(o_ref)

        row = 0
        for a_ref, k in zip(a_refs, heights):
            av = a_ref[...]
            col = 0
            for b_ref, n in zip(b_refs, widths):
                o_ref[row:row + k, col:col + n] += _dot(av, b_ref[...], TN)
                col += n
            row += k

    tok = lambda w: pl.BlockSpec((tt, w), lambda s: (s, 0))
    shape = (sum(heights), sum(widths))
    return pl.pallas_call(
        body, name=name, grid=(t // tt,),
        in_specs=[tok(k) for k in heights] + [tok(n) for n in widths],
        out_specs=_full(shape), out_shape=jax.ShapeDtypeStruct(shape, F32),
        compiler_params=_params("arbitrary"),
    )(*as_, *bs)


def _rope_tables(t):
    half = ATTN_HEAD_DIM // 2
    inv = 1.0 / (ROPE_THETA ** (jnp.arange(half, dtype=F32) * (2.0 / ATTN_HEAD_DIM)))
    ang = jnp.arange(t, dtype=F32)[:, None] * inv[None, :]
    cos, sin = jnp.cos(ang), jnp.sin(ang)
    cos2 = jnp.concatenate([cos, cos], axis=-1)
    sin2 = jnp.concatenate([-sin, sin], axis=-1)
    return jnp.tile(cos2, (1, 2)), jnp.tile(sin2, (1, 2))


def _swap_halves(tv):
    w = tv.shape[-1]
    lane = lax.broadcasted_iota(jnp.int32, tv.shape, tv.ndim - 1)
    first = (lane % ATTN_HEAD_DIM) < (ATTN_HEAD_DIM // 2)
    return jnp.where(first, pltpu.roll(tv, w - ATTN_HEAD_DIM // 2, tv.ndim - 1),
                     pltpu.roll(tv, ATTN_HEAD_DIM // 2, tv.ndim - 1))


def _rope(tv, cos, sin):
    return tv * cos + _swap_halves(tv) * sin


def _rope_bwd(dv, cos, sin):
    return dv * cos + _swap_halves(dv * sin)


def _attn_valid(first_block):
    c = lax.broadcasted_iota(jnp.int32, (2 * ATTN_BLOCK, ATTN_BLOCK), 0)
    r = lax.broadcasted_iota(jnp.int32, (2 * ATTN_BLOCK, ATTN_BLOCK), 1)
    return (c > r) & (c <= r + ATTN_BLOCK) & ((c >= ATTN_BLOCK) | jnp.logical_not(first_block))


def _attn_probs(st, sink, valid):
    s = jnp.where(valid, st * ATTN_SCALE, -jnp.inf)
    m = jnp.maximum(jnp.max(s, axis=0, keepdims=True), sink)
    e = jnp.where(valid, jnp.exp(s - m), 0.0)
    es = jnp.exp(sink - m)
    inv = 1.0 / (jnp.sum(e, axis=0, keepdims=True) + es)
    return e * inv, es * inv


def _lane_scalar(vec, idx):
    lane = lax.broadcasted_iota(jnp.int32, vec.shape, 1)
    return jnp.sum(jnp.where(lane == idx, vec, 0.0), axis=-1, keepdims=True)


ATTN_STEP = 2 * ATTN_BLOCK


def _attn_specs(ns):
    cur = lambda w, cb: pl.BlockSpec((ATTN_STEP, w), lambda i: (jnp.minimum(i, ns - 1), cb))
    prev = lambda w, cb: pl.BlockSpec((ATTN_BLOCK, w), lambda i: (jnp.maximum(2 * jnp.minimum(i, ns - 1) - 1, 0), cb))
    kcol, vcol = ATTN_Q // ATTN_KV, ATTN_Q // ATTN_KV + 1
    return [cur(ATTN_Q, 0), cur(ATTN_KV, kcol), prev(ATTN_KV, kcol), cur(ATTN_KV, vcol), prev(ATTN_KV, vcol),
            cur(ATTN_KV, 0), cur(ATTN_KV, 0), prev(ATTN_KV, 0), prev(ATTN_KV, 0), _full((1, 128))]


def _attn_windows(tp, tc):
    hsl = lambda hk: slice(hk * ATTN_HEAD_DIM, (hk + 1) * ATTN_HEAD_DIM)
    return [[jnp.concatenate([tp[:, hsl(hk)], tc[0:ATTN_BLOCK, hsl(hk)]], axis=0) for hk in range(ATTN_KV_HEADS)],
            [tc[:, hsl(hk)] for hk in range(ATTN_KV_HEADS)]]


def _attn_items():
    return [(s, h, slice(s * ATTN_BLOCK, (s + 1) * ATTN_BLOCK), slice(h * ATTN_HEAD_DIM, (h + 1) * ATTN_HEAD_DIM))
            for s in range(2) for h in range(ATTN_HEADS)]


def _attn_fwd(pa, cos, sin, sinks_vec):
    t = pa.shape[0]
    ns = t // ATTN_STEP

    def body(q_ref, kc_ref, kp_ref, vc_ref, vp_ref, cc_ref, sc_ref, cp_ref, sp_ref, sk_ref, o_ref):
        cc, sc = cc_ref[...], sc_ref[...]
        q = _rope(q_ref[...], jnp.tile(cc, (1, ATTN_Q // ATTN_KV)), jnp.tile(sc, (1, ATTN_Q // ATTN_KV)))
        kc = _rope(kc_ref[...], cc, sc)
        kp = _rope(kp_ref[...], cp_ref[...], sp_ref[...])
        sk = sk_ref[...]
        valids = [_attn_valid(pl.program_id(0) == 0), _attn_valid(False)]
        kwins = _attn_windows(kp, kc)
        vwins_t = [[v.T for v in vs] for vs in _attn_windows(vp_ref[...], vc_ref[...])]
        items = _attn_items()
        scores = [_dot(kwins[s][h // ATTN_GROUPS], q[rows, hs], NT) for s, h, rows, hs in items]
        probs = [_attn_probs(st, _lane_scalar(sk, h), valids[s])[0] for (s, h, rows, hs), st in zip(items, scores)]
        for (s, h, rows, hs), pt in zip(items, probs):
            o_ref[rows, hs] = _dot(vwins_t[s][h // ATTN_GROUPS], pt).T.astype(o_ref.dtype)

    return pl.pallas_call(
        body, name="attn_fwd", grid=(ns,),
        in_specs=_attn_specs(ns),
        out_specs=pl.BlockSpec((ATTN_STEP, ATTN_Q), lambda i: (i, 0)),
        out_shape=jax.ShapeDtypeStruct((t, ATTN_Q), MXU_DTYPE),
        compiler_params=_params("parallel"),
    )(pa, pa, pa, pa, pa, cos, sin, cos, sin, sinks_vec)


def _attn_bwd(pa, cos, sin, sinks_vec, dao):
    t = pa.shape[0]
    ns = t // ATTN_STEP
    lo, hi = slice(0, ATTN_BLOCK), slice(ATTN_BLOCK, ATTN_STEP)

    def body(q_ref, kc_ref, kp_ref, vc_ref, vp_ref, cc_ref, sc_ref, cp_ref, sp_ref, sk_ref, do_ref,
             dq_ref, dk_ref, dv_ref, acc_ref, dqr_ref, dkw_ref, dvw_ref, ck_ref, cv_ref):
        i = pl.program_id(0)

        @pl.when(i == 0)
        def _():
            acc_ref[...] = jnp.zeros_like(acc_ref)
            ck_ref[...] = jnp.zeros_like(ck_ref)
            cv_ref[...] = jnp.zeros_like(cv_ref)

        @pl.when(i < ns)
        def _():
            cc, sc = cc_ref[...], sc_ref[...]
            cq, sq = jnp.tile(cc, (1, ATTN_Q // ATTN_KV)), jnp.tile(sc, (1, ATTN_Q // ATTN_KV))
            q = _rope(q_ref[...], cq, sq)
            kc = _rope(kc_ref[...], cc, sc)
            kp = _rope(kp_ref[...], cp_ref[...], sp_ref[...])
            sk = sk_ref[...]
            do = do_ref[...]
            lane = lax.broadcasted_iota(jnp.int32, (1, 128), 1)
            dsink = jnp.zeros((1, 128), F32)
            valids = [_attn_valid(i == 0), _attn_valid(False)]
            kwins = _attn_windows(kp, kc)
            vwins = _attn_windows(vp_ref[...], vc_ref[...])
            kwins_t = [[kw.T for kw in kws] for kws in kwins]
            items = _attn_items()
            scores = [_dot(kwins[s][h // ATTN_GROUPS], q[rows, hs], NT) for s, h, rows, hs in items]
            dps = [_dot(vwins[s][h // ATTN_GROUPS], do[rows, hs], NT) for s, h, rows, hs in items]
            pts, dsts = {}, {}
            for (s, h, rows, hs), st, dp_t in zip(items, scores, dps):
                probs_t, psink = _attn_probs(st, _lane_scalar(sk, h), valids[s])
                delta = jnp.sum(probs_t * dp_t, axis=0, keepdims=True)
                pts[s, h] = probs_t
                dsts[s, h] = probs_t * (dp_t - delta) * ATTN_SCALE
                dsink += jnp.where(lane == h, jnp.sum(-psink * delta, axis=1, keepdims=True), 0.0)
            for s, h, rows, hs in items:
                dqr_ref[rows, hs] = _dot(kwins_t[s][h // ATTN_GROUPS], dsts[s, h]).T
            for s in range(2):
                rows = slice(s * ATTN_BLOCK, (s + 1) * ATTN_BLOCK)
                for hk in range(ATTN_KV_HEADS):
                    ks = slice(hk * ATTN_HEAD_DIM, (hk + 1) * ATTN_HEAD_DIM)
                    group = range(hk * ATTN_GROUPS, (hk + 1) * ATTN_GROUPS)
                    heads = [slice(h * ATTN_HEAD_DIM, (h + 1) * ATTN_HEAD_DIM) for h in group]
                    ds_g = jnp.concatenate([dsts[s, h] for h in group], axis=1)
                    p_g = jnp.concatenate([pts[s, h] for h in group], axis=1)
                    q_g = jnp.concatenate([q[rows, hs] for hs in heads], axis=0)
                    do_g = jnp.concatenate([do[rows, hs] for hs in heads], axis=0)
                    dkw_ref[s, :, ks] = _dot(ds_g, q_g)
                    dvw_ref[s, :, ks] = _dot(p_g, do_g)
            acc_ref[0:1, :] += dsink
            dq_ref[...] = _rope_bwd(dqr_ref[...], cq, sq).astype(dq_ref.dtype)
            dk_ref[lo, :] = ck_ref[lo, :].astype(dk_ref.dtype)
            dk_ref[hi, :] = (ck_ref[hi, :] + _rope_bwd(dkw_ref[0, lo, :], cp_ref[...], sp_ref[...])).astype(dk_ref.dtype)
            dv_ref[lo, :] = cv_ref[lo, :].astype(dv_ref.dtype)
            dv_ref[hi, :] = (cv_ref[hi, :] + dvw_ref[0, lo, :]).astype(dv_ref.dtype)
            ck_ref[lo, :] = _rope_bwd(dkw_ref[0, hi, :] + dkw_ref[1, lo, :], cc[lo, :], sc[lo, :])
            ck_ref[hi, :] = _rope_bwd(dkw_ref[1, hi, :], cc[hi, :], sc[hi, :])
            cv_ref[lo, :] = dvw_ref[0, hi, :] + dvw_ref[1, lo, :]
            cv_ref[hi, :] = dvw_ref[1, hi, :]

        @pl.when(i == ns)
        def _():
            dk_ref[...] = ck_ref[...].astype(dk_ref.dtype)
            dv_ref[...] = cv_ref[...].astype(dv_ref.dtype)

    prev_out = lambda w: pl.BlockSpec((ATTN_STEP, w), lambda i: (jnp.maximum(i - 1, 0), 0))
    return pl.pallas_call(
        body, name="attn_bwd", grid=(ns + 1,),
        in_specs=_attn_specs(ns) + [pl.BlockSpec((ATTN_STEP, ATTN_Q), lambda i: (jnp.minimum(i, ns - 1), 0))],
        out_specs=[pl.BlockSpec((ATTN_STEP, ATTN_Q), lambda i: (jnp.minimum(i, ns - 1), 0)), prev_out(ATTN_KV),
                   prev_out(ATTN_KV), _full((8, 128))],
        out_shape=[jax.ShapeDtypeStruct((t, ATTN_Q), MXU_DTYPE), jax.ShapeDtypeStruct((t, ATTN_KV), MXU_DTYPE),
                   jax.ShapeDtypeStruct((t, ATTN_KV), MXU_DTYPE), jax.ShapeDtypeStruct((8, 128), F32)],
        scratch_shapes=[pltpu.VMEM((ATTN_STEP, ATTN_Q), F32), pltpu.VMEM((2, ATTN_STEP, ATTN_KV), F32),
                        pltpu.VMEM((2, ATTN_STEP, ATTN_KV), F32), pltpu.VMEM((ATTN_STEP, ATTN_KV), F32),
                        pltpu.VMEM((ATTN_STEP, ATTN_KV), F32)],
        compiler_params=_params("arbitrary"),
    )(pa, pa, pa, pa, pa, cos, sin, cos, sin, sinks_vec, dao)


PAIR = 2 * DN_CHUNK
INTRA_PAIRS = 4
SCAN_PAIRS = 4
HALO = 8


def _conv_window(cur_ref, prev_ref, xs_ref, tm, has_prev):
    prev = jnp.where(has_prev, prev_ref[...], 0.0)
    xs_ref[0:HALO, :] = prev
    xs_ref[HALO:HALO + tm, :] = cur_ref[...]


def _conv_taps(xs_ref, cw_ref, tm):
    y = cw_ref[0:1, :] * xs_ref[pl.ds(HALO - DN_CONV + 1, tm), :]
    for j in range(1, DN_CONV):
        y += cw_ref[j:j + 1, :] * xs_ref[pl.ds(HALO - DN_CONV + 1 + j, tm), :]
    return y


def _gate_values(ba, al, dt):
    beta = _sigmoid(ba)
    pre = ba + dt
    g = -jnp.exp(al) * _softplus(pre)
    return beta, g, pre


def _dn_prep_specs(tm, tile):
    return [pl.BlockSpec((tm, CONV_CH), lambda i: (tile(i), 0)),
            pl.BlockSpec((HALO, CONV_CH), lambda i: (jnp.maximum(tile(i) * (tm // HALO) - 1, 0), 0)),
            pl.BlockSpec((tm, 128), lambda i: (tile(i), 4 * DN_W // 128)),
            _full((DN_CONV, CONV_CH)), _full((1, 128)), _full((1, 128))]


def _dn_prep(pd, conv_w, al_vec, dt_vec, tm):
    t = pd.shape[0]

    def body(cur_ref, prev_ref, ba_ref, cw_ref, al_ref, dt_ref, qn_ref, kn_ref, vc_ref, gc_ref, gr_ref, xs_ref):
        _conv_window(cur_ref, prev_ref, xs_ref, tm, pl.program_id(0) > 0)
        y = _conv_taps(xs_ref, cw_ref, tm)
        c = y * _sigmoid(y)
        for h in range(DN_HEADS):
            qs = slice(h * DN_HEAD_DIM, (h + 1) * DN_HEAD_DIM)
            ksl = slice(DN_W + h * DN_HEAD_DIM, DN_W + (h + 1) * DN_HEAD_DIM)
            qh, kh = c[:, qs], c[:, ksl]
            qn_ref[:, qs] = qh * lax.rsqrt(jnp.sum(qh * qh, axis=-1, keepdims=True) + EPS) * DN_SCALE
            kn_ref[:, qs] = kh * lax.rsqrt(jnp.sum(kh * kh, axis=-1, keepdims=True) + EPS)
        vc_ref[...] = c[:, 2 * DN_W:3 * DN_W]
        beta, g, _ = _gate_values(ba_ref[...], al_ref[...], dt_ref[...])
        lane = lax.broadcasted_iota(jnp.int32, beta.shape, 1)
        gb = jnp.where(lane < DN_HEADS, beta, jnp.where(lane < 2 * DN_HEADS, g, 0.0))
        gc_ref[...] = gb
        gr_ref[...] = gb.T[0:8, :]

    tok = lambda w: pl.BlockSpec((tm, w), lambda i: (i, 0))
    return pl.pallas_call(
        body, name="dn_prep", grid=(t // tm,),
        in_specs=_dn_prep_specs(tm, lambda i: i),
        out_specs=[tok(DN_W), tok(DN_W), tok(DN_W), tok(128), pl.BlockSpec((8, tm), lambda i: (0, i))],
        out_shape=[jax.ShapeDtypeStruct((t, DN_W), F32)] * 3 + [jax.ShapeDtypeStruct((t, 128), F32),
                                                                 jax.ShapeDtypeStruct((8, t), F32)],
        scratch_shapes=[pltpu.VMEM((HALO + tm, CONV_CH), F32)],
        compiler_params=_params("parallel"),
    )(pd, pd, pd, conv_w, al_vec, dt_vec)


def _pair_masks():
    r = lax.broadcasted_iota(jnp.int32, (PAIR, PAIR), 0)
    c = lax.broadcasted_iota(jnp.int32, (PAIR, PAIR), 1)
    same = (r < DN_CHUNK) == (c < DN_CHUNK)
    return same & (r >= c), same & (r > c)


def _lane_col(mat, idx):
    lane = lax.broadcasted_iota(jnp.int32, mat.shape, 1)
    return jnp.sum(jnp.where(lane == idx, mat, 0.0), axis=-1, keepdims=True)


def _pair_cumsums(gc, gr, low):
    lowf = low.astype(F32)
    return _dot(lowf, gc, NN, HI), _dot(gr, lowf, NT, HI)


def _pair_gates(gc, cum_c, cum_r, low, h):
    beta = _lane_col(gc, h)
    gam = _lane_col(cum_c, DN_HEADS + h)
    gam_row = cum_r[DN_HEADS + h:DN_HEADS + h + 1, :]
    dm = jnp.where(low, jnp.exp(jnp.where(low, gam - gam_row, 0.0)), 0.0)
    row = lax.broadcasted_iota(jnp.int32, gam.shape, 0)
    gl = jnp.where(row < DN_CHUNK, gam[DN_CHUNK - 1:DN_CHUNK, :], gam[PAIR - 1:PAIR, :])
    return beta, gam, dm, gl


def _split(a):
    hi = a.astype(BF16)
    return hi, (a - hi.astype(F32)).astype(BF16)


def _dot_split(a, b, dims=NN):
    (ah, al), (bh, bl) = a, b
    la, lb = (1, 1) if dims == TN else ((0, 1) if dims == NN else (0, 0))
    r = _dot(jnp.concatenate([ah, al], axis=la), jnp.concatenate([bh, bl], axis=lb), dims)
    m, n = r.shape[0] // 2, r.shape[1] // 2
    return (r[m:, n:] + (r[:m, n:] + r[m:, :n])) + r[:m, :n]


def _unit_lower_inverses(lmats):
    n = lmats[0].shape[0]
    r = lax.broadcasted_iota(jnp.int32, (n, n), 0)
    c = lax.broadcasted_iota(jnp.int32, (n, n), 1)
    same = lambda size: (r & ~(size - 1)) == (c & ~(size - 1))
    base = DN_CHUNK // 4
    diag = [jnp.where(same(base), l, 0.0) for l in lmats]
    accs = [(r == c).astype(F32) - d for d in diag]
    splits = [_split(d) for d in diag]
    step = 1
    while 2 * step < base:
        splits = [_split(_dot_split(s, s)) for s in splits]
        accs = [acc + _dot_split(_split(acc), s) for acc, s in zip(accs, splits)]
        step *= 2
    size = base
    while size < DN_CHUNK:
        below = same(2 * size) & jnp.logical_not(same(size))
        tb = [_dot(acc, jnp.where(below, l, 0.0)) for acc, l in zip(accs, lmats)]
        accs = [acc - _dot(t, acc) for acc, t in zip(accs, tb)]
        size *= 2
    return accs


def _dn_intra(qn, kn, vc, gc, gr):
    t = qn.shape[0]
    npair = t // PAIR
    rows_step = INTRA_PAIRS * PAIR

    def body(q_ref, k_ref, v_ref, gc_ref, gr_ref, u_ref, w_ref, qg_ref, kd_ref, a_ref, ti_ref, dl_ref):
        low, strict = _pair_masks()
        items = []
        for p in range(INTRA_PAIRS):
            rows = slice(p * PAIR, (p + 1) * PAIR)
            gc_v = gc_ref[rows, :]
            cum_c, cum_r = _pair_cumsums(gc_v, gr_ref[:, rows], low)
            for h in range(DN_HEADS):
                hs = slice(h * DN_HEAD_DIM, (h + 1) * DN_HEAD_DIM)
                items.append((p, h, rows, hs, _pair_gates(gc_v, cum_c, cum_r, low, h)))
        lmats = []
        for p, h, rows, hs, (beta, gam, dm, gl) in items:
            k = k_ref[rows, hs]
            lmats.append(jnp.where(strict, _dot(k * beta, k, NT) * dm, 0.0))
        tinvs = _unit_lower_inverses(lmats)
        for (p, h, rows, hs, (beta, gam, dm, gl)), tinv in zip(items, tinvs):
            q, k, v = q_ref[rows, hs], k_ref[rows, hs], v_ref[rows, hs]
            eg = jnp.exp(gam)
            u_ref[rows, hs] = _dot(tinv, v * beta)
            w_ref[rows, hs] = _dot(tinv, (k * beta) * eg).astype(w_ref.dtype)
            a_ref[h, rows, :] = _dot(q, k, NT) * dm
            ti_ref[h, rows, :] = tinv
            qg_ref[rows, hs] = (q * eg).astype(qg_ref.dtype)
            kd_ref[rows, hs] = (k * jnp.exp(gl - gam)).astype(kd_ref.dtype)
            for c in range(2):
                last = (c + 1) * DN_CHUNK - 1
                dl_ref[2 * p + c, h] = jnp.broadcast_to(jnp.exp(gam[last:last + 1, :]), (8, 128))

    tok = lambda w: pl.BlockSpec((rows_step, w), lambda n: (n, 0))
    hm = pl.BlockSpec((DN_HEADS, rows_step, PAIR), lambda n: (0, n, 0))
    return pl.pallas_call(
        body, name="dn_intra", grid=(npair // INTRA_PAIRS,),
        in_specs=[tok(DN_W), tok(DN_W), tok(DN_W), tok(128), pl.BlockSpec((8, rows_step), lambda n: (0, n))],
        out_specs=[tok(DN_W)] * 4 + [hm, hm, pl.BlockSpec((2 * INTRA_PAIRS, DN_HEADS, 8, 128), lambda n: (n, 0, 0, 0))],
        out_shape=[jax.ShapeDtypeStruct((t, DN_W), F32)] + [jax.ShapeDtypeStruct((t, DN_W), MXU_DTYPE)] * 3
                  + [jax.ShapeDtypeStruct((DN_HEADS, t, PAIR), F32)] * 2
                  + [jax.ShapeDtypeStruct((2 * npair, DN_HEADS, 8, 128), F32)],
        compiler_params=_params("parallel"),
    )(qn, kn, vc, gc, gr)


def _dn_scan_fwd(u, w, qg, kd, a_qk, dlast, pd, dn_w):
    t = u.shape[0]
    npair = t // PAIR

    def body(u_ref, w_ref, qg_ref, kd_ref, a_ref, dl_ref, z_ref, nw_ref, out_ref, o_ref, vn_ref, sall_ref, s_ref):
        @pl.when(pl.program_id(0) == 0)
        def _():
            s_ref[...] = jnp.zeros_like(s_ref)

        nw = nw_ref[...]
        for c in range(2 * SCAN_PAIRS):
            rows = slice(c * DN_CHUNK, (c + 1) * DN_CHUNK)
            diag = slice((c % 2) * DN_CHUNK, (c % 2 + 1) * DN_CHUNK)
            for h in range(DN_HEADS):
                hs = slice(h * DN_HEAD_DIM, (h + 1) * DN_HEAD_DIM)
                st = s_ref[h]
                sall_ref[c, h] = st
                vn_ref[rows, hs] = (u_ref[rows, hs] - _dot(w_ref[rows, hs], st)).astype(vn_ref.dtype)
            for h in range(DN_HEADS):
                hs = slice(h * DN_HEAD_DIM, (h + 1) * DN_HEAD_DIM)
                st, vn = s_ref[h], vn_ref[rows, hs]
                o = _dot(qg_ref[rows, hs], st) + _dot(a_ref[h, rows, diag], vn)
                s_ref[h] = st * dl_ref[c, h][0:1, :] + _dot(kd_ref[rows, hs], vn, TN)
                o_ref[rows, hs] = o
                z = z_ref[rows, hs]
                on = o * lax.rsqrt(jnp.mean(o * o, axis=-1, keepdims=True) + EPS) * nw
                out_ref[rows, hs] = (on * (z * _sigmoid(z))).astype(out_ref.dtype)

    rows_step = SCAN_PAIRS * PAIR
    tok = pl.BlockSpec((rows_step, DN_W), lambda n: (n, 0))
    hm = pl.BlockSpec((DN_HEADS, rows_step, PAIR), lambda n: (0, n, 0))
    return pl.pallas_call(
        body, name="dn_scan_fwd", grid=(npair // SCAN_PAIRS,),
        in_specs=[tok, tok, tok, tok, hm, pl.BlockSpec((2 * SCAN_PAIRS, DN_HEADS, 8, 128), lambda n: (n, 0, 0, 0)),
                  pl.BlockSpec((rows_step, DN_W), lambda n: (n, 3)), _full((1, 128))],
        out_specs=[tok, tok, tok,
                   pl.BlockSpec((2 * SCAN_PAIRS, DN_HEADS, DN_HEAD_DIM, DN_HEAD_DIM), lambda n: (n, 0, 0, 0))],
        out_shape=[jax.ShapeDtypeStruct((t, DN_W), MXU_DTYPE), jax.ShapeDtypeStruct((t, DN_W), F32),
                   jax.ShapeDtypeStruct((t, DN_W), MXU_DTYPE),
                   jax.ShapeDtypeStruct((2 * npair, DN_HEADS, DN_HEAD_DIM, DN_HEAD_DIM), F32)],
        scratch_shapes=[pltpu.VMEM((DN_HEADS, DN_HEAD_DIM, DN_HEAD_DIM), F32)],
        compiler_params=_params("arbitrary"),
    )(u, w, qg, kd, a_qk, dlast, pd, dn_w)


def _dn_scan_bwd(dout, o, vnew, sall, w, qg, kd, a_qk, dlast, pd, dn_w, dep):
    t = o.shape[0]
    npair = t // PAIR
    nstep = npair // SCAN_PAIRS
    rev = lambda n: nstep - 1 - n

    def body(do_ref, o_ref, vn_ref, sall_ref, w_ref, qg_ref, kd_ref, a_ref, dl_ref, z_ref, nw_ref, dep_ref,
             dz_ref, du_ref, dw_ref, dqg_ref, dkd_ref, da_ref, ddl_ref, acc_ref, ds_ref, dos_ref):
        @pl.when(pl.program_id(0) == 0)
        def _():
            ds_ref[...] = jnp.zeros_like(ds_ref)
            acc_ref[...] = jnp.zeros_like(acc_ref)

        nw = nw_ref[...]
        dnw = jnp.zeros((1, 128), F32)
        for h in range(DN_HEADS):
            hs = slice(h * DN_HEAD_DIM, (h + 1) * DN_HEAD_DIM)
            o, z, dout = o_ref[:, hs], z_ref[:, hs], do_ref[:, hs]
            r = lax.rsqrt(jnp.mean(o * o, axis=-1, keepdims=True) + EPS)
            oh = o * r
            sz = _sigmoid(z)
            dz_ref[:, hs] = dout * (oh * nw) * (sz + z * sz * (1.0 - sz))
            don = dout * (z * sz)
            dnw += jnp.sum(don * oh, axis=0, keepdims=True)
            doh = don * nw
            dos_ref[:, hs] = r * (doh - oh * jnp.mean(doh * oh, axis=-1, keepdims=True))
        acc_ref[0:1, :] += dnw
        for c in reversed(range(2 * SCAN_PAIRS)):
            rows = slice(c * DN_CHUNK, (c + 1) * DN_CHUNK)
            diag = slice((c % 2) * DN_CHUNK, (c % 2 + 1) * DN_CHUNK)
            other = slice((1 - c % 2) * DN_CHUNK, (2 - c % 2) * DN_CHUNK)
            for h in range(DN_HEADS):
                hs = slice(h * DN_HEAD_DIM, (h + 1) * DN_HEAD_DIM)
                do, st, dsp, vn = dos_ref[rows, hs], sall_ref[c, h], ds_ref[h], vn_ref[rows, hs]
                da_ref[h, rows, diag] = _dot(do, vn, NT)
                da_ref[h, rows, other] = jnp.zeros((DN_CHUNK, DN_CHUNK), F32)
                du_ref[rows, hs] = (_dot(a_ref[h, rows, diag], do, TN) + _dot(kd_ref[rows, hs], dsp)).astype(du_ref.dtype)
                dqg_ref[rows, hs] = _dot(do, st, NT)
                dkd_ref[rows, hs] = _dot(vn, dsp, NT)
                ddl = jnp.sum(jnp.sum(dsp * st, axis=1, keepdims=True), axis=0, keepdims=True)
                ddl_ref[c, h] = jnp.broadcast_to(ddl, (8, 128))
            for h in range(DN_HEADS):
                hs = slice(h * DN_HEAD_DIM, (h + 1) * DN_HEAD_DIM)
                do, st, dvn = dos_ref[rows, hs], sall_ref[c, h], du_ref[rows, hs]
                dw_ref[rows, hs] = (-_dot(dvn, st, NT)).astype(dw_ref.dtype)
                ds_ref[h] = (ds_ref[h] * dl_ref[c, h][0:1, :] + _dot(qg_ref[rows, hs], do, TN)
                             - _dot(w_ref[rows, hs], dvn, TN))

    rows_step = SCAN_PAIRS * PAIR
    tok = pl.BlockSpec((rows_step, DN_W), lambda n: (rev(n), 0))
    hm = pl.BlockSpec((DN_HEADS, rows_step, PAIR), lambda n: (0, rev(n), 0))
    sc = pl.BlockSpec((2 * SCAN_PAIRS, DN_HEADS, 8, 128), lambda n: (rev(n), 0, 0, 0))
    return pl.pallas_call(
        body, name="dn_scan_bwd", grid=(nstep,),
        in_specs=[tok, tok, tok,
                  pl.BlockSpec((2 * SCAN_PAIRS, DN_HEADS, DN_HEAD_DIM, DN_HEAD_DIM), lambda n: (rev(n), 0, 0, 0)),
                  tok, tok, tok, hm, sc, pl.BlockSpec((rows_step, DN_W), lambda n: (rev(n), 3)), _full((1, 128)),
                  pl.BlockSpec(memory_space=pl.ANY)],
        out_specs=[tok] * 5 + [hm, sc, _full((8, 128))],
        out_shape=[jax.ShapeDtypeStruct((t, DN_W), F32)] + [jax.ShapeDtypeStruct((t, DN_W), MXU_DTYPE)] * 2
                  + [jax.ShapeDtypeStruct((t, DN_W), F32)] * 2 + [jax.ShapeDtypeStruct((DN_HEADS, t, PAIR), F32),
                   jax.ShapeDtypeStruct((2 * npair, DN_HEADS, 8, 128), F32), jax.ShapeDtypeStruct((8, 128), F32)],
        scratch_shapes=[pltpu.VMEM((DN_HEADS, DN_HEAD_DIM, DN_HEAD_DIM), F32), pltpu.VMEM((SCAN_PAIRS * PAIR, DN_W), F32)],
        compiler_params=_params("arbitrary"),
    )(dout, o, vnew, sall, w, qg, kd, a_qk, dlast, pd, dn_w, dep)


def _dn_intra_bwd(qn, kn, vc, gc, gr, tinv, a_qk, du, dw, dqg, dkd, da_qk, ddlast, dlast, dep):
    t = qn.shape[0]
    npair = t // PAIR

    def body(q_ref, k_ref, v_ref, gc_ref, gr_ref, ti_ref, a_ref, du_ref, dw_ref, dqg_ref, dkd_ref, da_ref, ddl_ref, dl_ref,
             dep_ref, dq_ref, dk_ref, dv_ref, dg_ref):
        low, strict = _pair_masks()
        lane = lax.broadcasted_iota(jnp.int32, (PAIR, 128), 1)
        rowi = lax.broadcasted_iota(jnp.int32, (PAIR, 1), 0)
        rsum = lambda v: jnp.sum(v, axis=-1, keepdims=True)
        items = []
        for p in range(INTRA_PAIRS):
            rows = slice(p * PAIR, (p + 1) * PAIR)
            gc_v = gc_ref[rows, :]
            cum_c, cum_r = _pair_cumsums(gc_v, gr_ref[:, rows], low)
            for h in range(DN_HEADS):
                hs = slice(h * DN_HEAD_DIM, (h + 1) * DN_HEAD_DIM)
                items.append((p, h, rows, hs, _pair_gates(gc_v, cum_c, cum_r, low, h)))
        dtis, lmats, dvbs, dkbgs = [], [], [], []
        for p, h, rows, hs, (beta, gam, dm, gl) in items:
            k, tinv = k_ref[rows, hs], ti_ref[h, rows, :]
            kb = k * beta
            dtis.append(_dot(du_ref[rows, hs], v_ref[rows, hs] * beta, NT)
                        + _dot(dw_ref[rows, hs], kb * jnp.exp(gam), NT))
            lmats.append(jnp.where(strict, _dot(kb, k, NT) * dm, 0.0))
            dvbs.append(_dot(tinv, du_ref[rows, hs], TN))
            dkbgs.append(_dot(tinv, dw_ref[rows, hs], TN))
        xs = [_dot(ti_ref[h, rows, :], dti, TN) for (p, h, rows, hs, g), dti in zip(items, dtis)]
        dls = [jnp.where(strict, -_dot(x, ti_ref[h, rows, :], NT), 0.0) for (p, h, rows, hs, g), x in zip(items, xs)]
        dgam_all = [jnp.zeros((PAIR, 128), F32) for _ in range(INTRA_PAIRS)]
        dbeta_all = [jnp.zeros((PAIR, 128), F32) for _ in range(INTRA_PAIRS)]
        for (p, h, rows, hs, (beta, gam, dm, gl)), dl, lmat, dvb, dkbg in zip(items, dls, lmats, dvbs, dkbgs):
            q, k, v = q_ref[rows, hs], k_ref[rows, hs], v_ref[rows, hs]
            a = a_ref[h, rows, :]
            dqg, dkd = dqg_ref[rows, hs], dkd_ref[rows, hs]
            kb = k * beta
            eg = jnp.exp(gam)
            ekd = jnp.exp(gl - gam)
            dmm = dl * dm
            dam = jnp.where(low, da_ref[h, rows, :], 0.0)
            dn = dam * dm
            e = dl * lmat + dam * a
            dkb = _dot(dmm, k) + dkbg * eg
            dk_ref[rows, hs] = _dot(dmm, kb, TN) + _dot(dn, q, TN) + dkd * ekd + dkb * beta
            dq_ref[rows, hs] = _dot(dn, k) + dqg * eg
            dv_ref[rows, hs] = dvb * beta
            t_kd = rsum(dkd * (k * ekd))
            dgam = rsum(e) - rsum(e.T) + rsum(dqg * (q * eg)) + rsum(dkbg * (kb * eg)) - t_kd
            for c in range(2):
                crows = slice(c * DN_CHUNK, (c + 1) * DN_CHUNK)
                dgl = (jnp.sum(t_kd[crows, :], axis=0, keepdims=True)
                       + ddl_ref[2 * p + c, h][0:1, 0:1] * dl_ref[2 * p + c, h][0:1, 0:1])
                dgam = dgam + jnp.where(rowi == (c + 1) * DN_CHUNK - 1, dgl, 0.0)
            dgam_all[p] += jnp.where(lane == DN_HEADS + h, dgam, 0.0)
            dbeta_all[p] += jnp.where(lane == h, rsum(dkb * k) + rsum(dvb * v), 0.0)
        for p in range(INTRA_PAIRS):
            dg_ref[p * PAIR:(p + 1) * PAIR, :] = dbeta_all[p] + _dot(low.astype(F32), dgam_all[p], TN, HI)

    rows_step = INTRA_PAIRS * PAIR
    tok = lambda w: pl.BlockSpec((rows_step, w), lambda n: (n, 0))
    hm = pl.BlockSpec((DN_HEADS, rows_step, PAIR), lambda n: (0, n, 0))
    sc = pl.BlockSpec((2 * INTRA_PAIRS, DN_HEADS, 8, 128), lambda n: (n, 0, 0, 0))
    return pl.pallas_call(
        body, name="dn_intra_bwd", grid=(npair // INTRA_PAIRS,),
        in_specs=[tok(DN_W), tok(DN_W), tok(DN_W), tok(128), pl.BlockSpec((8, rows_step), lambda n: (0, n)), hm, hm,
                  tok(DN_W), tok(DN_W), tok(DN_W), tok(DN_W), hm, sc, sc, pl.BlockSpec(memory_space=pl.ANY)],
        out_specs=[tok(DN_W), tok(DN_W), tok(DN_W), tok(128)],
        out_shape=[jax.ShapeDtypeStruct((t, DN_W), F32)] * 3 + [jax.ShapeDtypeStruct((t, 128), F32)],
        compiler_params=_params("parallel"),
    )(qn, kn, vc, gc, gr, tinv, a_qk, du, dw, dqg, dkd, da_qk, ddlast, dlast, dep)


def _dn_prep_bwd(pd, conv_w, al_vec, dt_vec, dqn, dkn, dvc, dgc, dz, tm):
    t = pd.shape[0]
    nt = t // tm
    tile = lambda i: nt - 1 - i

    def body(cur_ref, prev_ref, ba_ref, cw_ref, al_ref, dt_ref, dq_ref, dk_ref, dv_ref, dg_ref, dz_ref,
             o_ref, accw_ref, accg_ref, xs_ref, dc_ref, ds_ref, carry_ref):
        @pl.when(pl.program_id(0) == 0)
        def _():
            accw_ref[...] = jnp.zeros_like(accw_ref)
            accg_ref[...] = jnp.zeros_like(accg_ref)
            carry_ref[...] = jnp.zeros_like(carry_ref)

        _conv_window(cur_ref, prev_ref, xs_ref, tm, tile(pl.program_id(0)) > 0)
        taps = [xs_ref[pl.ds(HALO - DN_CONV + 1 + j, tm), :] for j in range(DN_CONV)]
        y = cw_ref[0:1, :] * taps[0]
        for j in range(1, DN_CONV):
            y += cw_ref[j:j + 1, :] * taps[j]
        sg = _sigmoid(y)
        c = y * sg
        for h in range(DN_HEADS):
            qs = slice(h * DN_HEAD_DIM, (h + 1) * DN_HEAD_DIM)
            ksl = slice(DN_W + h * DN_HEAD_DIM, DN_W + (h + 1) * DN_HEAD_DIM)
            for src, sl, scale in ((dq_ref, qs, DN_SCALE), (dk_ref, ksl, 1.0)):
                xh = c[:, sl]
                r = lax.rsqrt(jnp.sum(xh * xh, axis=-1, keepdims=True) + EPS)
                unit = xh * r
                dn = src[:, qs] * scale
                dc_ref[:, sl] = r * (dn - unit * jnp.sum(dn * unit, axis=-1, keepdims=True))
        dc_ref[:, 2 * DN_W:3 * DN_W] = dv_ref[...]
        dy = dc_ref[...] * (sg + y * sg * (1.0 - sg))
        for j in range(DN_CONV):
            accw_ref[j:j + 1, :] += jnp.sum(dy * taps[j], axis=0, keepdims=True)
        ds_ref[0:tm, :] = dy
        ds_ref[tm:tm + HALO, :] = carry_ref[...]
        carry_ref[...] = ds_ref[0:HALO, :]
        dx = cw_ref[0:1, :] * ds_ref[pl.ds(DN_CONV - 1, tm), :]
        for j in range(1, DN_CONV):
            dx += cw_ref[j:j + 1, :] * ds_ref[pl.ds(DN_CONV - 1 - j, tm), :]

        beta, g, pre = _gate_values(ba_ref[...], al_ref[...], dt_ref[...])
        dgb = dg_ref[...]
        lane = lax.broadcasted_iota(jnp.int32, dgb.shape, 1)
        is_b, is_a = lane < DN_HEADS, (lane >= DN_HEADS) & (lane < 2 * DN_HEADS)
        dpre = dgb * (-jnp.exp(al_ref[...])) * _sigmoid(pre)
        dba = jnp.where(is_b, dgb * beta * (1.0 - beta), jnp.where(is_a, dpre, 0.0))
        accg_ref[0:1, :] += jnp.sum(jnp.where(is_a, dgb * g, 0.0), axis=0, keepdims=True)
        accg_ref[1:2, :] += jnp.sum(jnp.where(is_a, dpre, 0.0), axis=0, keepdims=True)
        o_ref[:, 0:CONV_CH] = dx.astype(o_ref.dtype)
        o_ref[:, CONV_CH:CONV_CH + DN_W] = dz_ref[...].astype(o_ref.dtype)
        o_ref[:, CONV_CH + DN_W:DN_COLS] = dba.astype(o_ref.dtype)

    tok = lambda w: pl.BlockSpec((tm, w), lambda i: (tile(i), 0))
    return pl.pallas_call(
        body, name="dn_prep_bwd", grid=(nt,),
        in_specs=_dn_prep_specs(tm, tile) + [tok(DN_W), tok(DN_W), tok(DN_W), tok(128), tok(DN_W)],
        out_specs=[tok(DN_COLS), _full((8, CONV_CH)), _full((8, 128))],
        out_shape=[jax.ShapeDtypeStruct((t, DN_COLS), MXU_DTYPE),
                   jax.ShapeDtypeStruct((8, CONV_CH), F32), jax.ShapeDtypeStruct((8, 128), F32)],
        scratch_shapes=[pltpu.VMEM((HALO + tm, CONV_CH), F32), pltpu.VMEM((tm, CONV_CH), F32),
                        pltpu.VMEM((tm + HALO, CONV_CH), F32), pltpu.VMEM((HALO, CONV_CH), F32)],
        compiler_params=_params("arbitrary"),
    )(pd, pd, pd, conv_w, al_vec, dt_vec, dqn, dkn, dvc, dgc, dz)


def _pad_lanes(v, offset=0):
    return jnp.pad(v.astype(F32), (offset, 128 - offset - v.shape[0]))[None]


class _LocalReducer:
    def start(self, grads):
        return jnp.zeros((8, 128), F32)

    def middle(self, after):
        return jnp.zeros((8, 128), F32)

    def finish(self, after):
        return None


def _local_step(x, p, tgt, sm, w, late, reducer):
    t = x.shape[0]
    tm = min(512, t // 2)
    tm_s = min(512, t // 2)
    tw = min(1024, t // 2)
    tw_ff = min(2048, t // 2)

    attn_cols = ATTN_Q + 2 * ATTN_KV
    w_in_t = w["w_in_t"]
    w_in_t = jnp.pad(w_in_t, ((0, max(0, attn_cols + DN_COLS - w_in_t.shape[0])), (0, 0)))
    wa_t = w_in_t[:attn_cols]
    wd_t = w_in_t[attn_cols:attn_cols + DN_COLS]
    conv_w = w["conv_w"]
    al_vec, dt_vec = _pad_lanes(sm["a_log"], DN_HEADS), _pad_lanes(sm["dt_bias"], DN_HEADS)
    sinks_vec = _pad_lanes(sm["sinks"])
    dn_w = sm["dn_norm"].reshape(1, 128)
    row = lambda v: v.reshape(1, D_MODEL)
    cos, sin = _rope_tables(t)

    u, pa, pd = _inproj(x, row(sm["norm_mix"]), wa_t, wd_t, tm_s)
    ao = _attn_fwd(pa, cos, sin, sinks_vec)
    qn, kn, vc, gc, gr = _dn_prep(pd, conv_w, al_vec, dt_vec, tm_s)
    uu, ww, qg, kd, a_qk, tinv, dlast = _dn_intra(qn, kn, vc, gc, gr)
    dn_out, o, vnew, sall = _dn_scan_fwd(uu, ww, qg, kd, a_qk, dlast, pd, dn_w)
    w_o, late_rest = late(dn_out)
    wo_a, wo_d = w_o[:ATTN_Q], w_o[ATTN_Q:]
    h1 = _oproj(x, ao, dn_out, wo_a, wo_d, tm)
    w = dict(w, **late_rest(h1))
    w_proj = jnp.transpose(w["w_proj4"], (1, 0, 2)).reshape(PLE_DIM, D_MODEL)
    m, r, h2 = _mlp_fwd(h1, row(sm["norm_mlp"]), w["w_up4"], w["w_down"], tw)
    dh2, dh2b, dgp, dpp, n3, pb, acc_ple = _ple_loss(h2, p, tgt, row(sm["norm_ple"]), row(sm["norm_final"]),
                                                     w["w_gate"], w_proj, tm_s)
    g_w_gate = _wgrad(n3, dgp, "wgrad_gate", D_MODEL, D_MODEL, tw)
    g_w_proj = _wgrad(pb, dpp, "wgrad_proj", PLE_DIM, D_MODEL, tw)
    da, dh1, dh1b, dao, ddn, acc_mlp = _mlp_bwd(dh2, dh2b, r, h1, row(sm["norm_mlp"]), w["w_up4"], w["w_down"],
                                                wo_a, wo_d, tm)
    g_w_up4 = _wgrad(m, da, "wgrad_up", D_MODEL, FF_BLOCK, tw_ff, stacked=True)
    g_w_down = _wgrad(r, dh2b, "wgrad_down", FF_BLOCK, D_MODEL, tw_ff,
                      prep=lambda rv: jnp.square(rv.astype(F32)).astype(MXU_DTYPE))
    g_w_o = _wgrad_cat([ao, dn_out], [dh1b], "wgrad_o", tw)
    early = dict(w_up4=g_w_up4, w_down=g_w_down, w_gate=g_w_gate, w_proj=g_w_proj, w_o=g_w_o)
    dep = reducer.start(early)
    dz, du, dw, dqg, dkd, da_qk, ddlast, acc_dn = _dn_scan_bwd(ddn, o, vnew, sall, ww, qg, kd, a_qk, dlast, pd, dn_w,
                                                               dep)
    dep = reducer.middle(du)
    dqn, dkn, dvc, dgc = _dn_intra_bwd(qn, kn, vc, gc, gr, tinv, a_qk, du, dw, dqg, dkd, da_qk, ddlast, dlast, dep)
    d_dn, acc_conv, acc_gate = _dn_prep_bwd(pd, conv_w, al_vec, dt_vec, dqn, dkn, dvc, dgc, dz, tm_s)
    dq, dk, dv, acc_attn = _attn_bwd(pa, cos, sin, sinks_vec, dao)
    reducer.finish(dq)
    wq_t, wk_t, wv_t = wa_t[:ATTN_Q], wa_t[ATTN_Q:ATTN_Q + ATTN_KV], wa_t[ATTN_Q + ATTN_KV:]
    dx, acc_mix = _inproj_bwd(x, dh1, row(sm["norm_mix"]), [dq, dk, dv, d_dn], [wq_t, wk_t, wv_t, wd_t], tm_s)

    g_w_in_t = _wgrad_cat([dq, dk, dv, d_dn], [u], "wgrad_in", tw)
    grads = dict(early, w_in_t=g_w_in_t)
    sums = dict(loss=acc_ple[2, 0], norm_final=acc_ple[0], norm_ple=acc_ple[1], norm_mlp=acc_mlp[0], norm_mix=acc_mix[0],
                dn_norm=acc_dn[0], sinks=acc_attn[0, :ATTN_HEADS], a_log=acc_gate[0, DN_HEADS:2 * DN_HEADS],
                dt_bias=acc_gate[1, DN_HEADS:2 * DN_HEADS], conv_w=acc_conv[:DN_CONV])
    return sums, dx, grads


MESH = pl.DeviceIdType.MESH
ANY = pl.BlockSpec(memory_space=pl.ANY)
N_CHIPS = 4
N_DEV = 8


def _place():
    x, y, c = lax.axis_index("x"), lax.axis_index("y"), lax.axis_index("c")
    chips = [(1 - x, y), (x, 1 - y), (1 - x, 1 - y)]
    return x, y, c, chips


def _gather_weights(shards, conv_s):
    n = len(shards)
    per = 7

    def body(*refs):
        in_refs, conv_ref = refs[:n], refs[n]
        out_refs, conv_out = refs[n + 1:2 * n + 1], refs[2 * n + 1]
        send_sems, recv_sems = refs[2 * n + 2:]
        x, y, c, chips = _place()
        sibling = (x, y, 1 - c)

        def blk(a, px, py, pc):
            hr = in_refs[a].shape[0] // 2
            return out_refs[a].at[2 * px + py, pl.ds(pc * hr, hr), :]

        def mine(a):
            hr = in_refs[a].shape[0] // 2
            return in_refs[a].at[pl.ds(c * hr, hr), :]

        def rcopy(a, k, block, to, src=None):
            return pltpu.make_async_remote_copy(
                src_ref=blk(a, *block) if src is None else src, dst_ref=blk(a, *block),
                send_sem=send_sems.at[per * a + k], recv_sem=recv_sems.at[per * a + k],
                device_id=to, device_id_type=MESH)

        def whole(a, to):
            return pltpu.make_async_remote_copy(
                src_ref=in_refs[a], dst_ref=out_refs[a].at[2 * x + y],
                send_sem=send_sems.at[per * a], recv_sem=recv_sems.at[per * a], device_id=to, device_id_type=MESH)

        def ccopy(j, to):
            return pltpu.make_async_remote_copy(
                src_ref=conv_ref, dst_ref=conv_out.at[2 * x + y],
                send_sem=send_sems.at[per * n + j], recv_sem=recv_sems.at[per * n + j],
                device_id=to, device_id_type=MESH)

        started = []
        for a in range(n):
            first = [whole(a, sibling)]
            first += [rcopy(a, 1 + j, (x, y, c), (*chip, c), src=mine(a)) for j, chip in enumerate(chips)]
            for cp in first:
                cp.start()
            started += first
        conv_sends = [ccopy(j, (*chip, c)) for j, chip in enumerate(chips)] + [ccopy(3, sibling)]
        for cp in conv_sends:
            cp.start()
        started += conv_sends
        for a in range(n):
            for j, chip in enumerate(chips):
                rcopy(a, 1 + j, (*chip, c), (x, y, c)).wait_recv()
                fwd = rcopy(a, 4 + j, (*chip, c), sibling)
                fwd.start()
                started.append(fwd)
        for a in range(n):
            whole(a, sibling).wait_recv()
            for j, chip in enumerate(chips):
                rcopy(a, 4 + j, (*chip, 1 - c), (x, y, c)).wait_recv()
        for j, chip in enumerate(chips + [(x, y)]):
            pltpu.make_async_remote_copy(
                src_ref=conv_ref, dst_ref=conv_out.at[2 * chip[0] + chip[1]],
                send_sem=send_sems.at[per * n + j], recv_sem=recv_sems.at[per * n + j],
                device_id=sibling, device_id_type=MESH).wait_recv()
        for cp in started:
            cp.wait_send()

    nsem = per * n + 4
    out_shape = [jax.ShapeDtypeStruct((N_CHIPS,) + s.shape, s.dtype) for s in shards]
    out_shape.append(jax.ShapeDtypeStruct((N_CHIPS,) + conv_s.shape, conv_s.dtype))
    return pl.pallas_call(
        body, name="gather_weights", in_specs=[ANY] * (n + 1), out_specs=[ANY] * (n + 1), out_shape=out_shape,
        scratch_shapes=[pltpu.SemaphoreType.DMA((nsem,)), pltpu.SemaphoreType.DMA((nsem,))],
    )(*shards, conv_s)


HBM = pl.BlockSpec(memory_space=pltpu.HBM)
SEM = pl.BlockSpec(memory_space=pltpu.SEMAPHORE)
EFFECT = pltpu.SideEffectType.DATAFLOW_SIDE_EFFECTING
LATE_COPIES = 7


def _late_copies(in_refs, land_refs, send_sems, recv_sems, only=None):
    x, y, c, chips = _place()
    sends, arrivals = [], []
    for a, (src, land) in enumerate(zip(in_refs, land_refs)):
        if only is not None and a not in only:
            continue
        hr = src.shape[0] // 2
        base = LATE_COPIES * a

        def cp(src_ref, dst_ref, s_idx, r_idx, to):
            return pltpu.make_async_remote_copy(src_ref=src_ref, dst_ref=dst_ref, send_sem=send_sems.at[base + s_idx],
                                                recv_sem=recv_sems.at[base + r_idx], device_id=to, device_id_type=MESH)

        sends.append(cp(src, land.at[2 * x + y], 0, 0, (x, y, 1 - c)))
        arrivals.append(cp(src, land.at[2 * x + y], 0, 0, (x, y, 1 - c)))
        for j, chip in enumerate(chips):
            for pc in range(2):
                half = src.at[pl.ds(c * hr, hr), :]
                sends.append(cp(half, land.at[2 * x + y, pl.ds(c * hr, hr), :], 1 + 2 * j + pc, 1 + 2 * j + c, (*chip, pc)))
                arrivals.append(cp(half, land.at[2 * chip[0] + chip[1], pl.ds(pc * hr, hr), :], 1 + 2 * j + pc,
                                   1 + 2 * j + pc, (*chip, pc)))
    return sends, arrivals


def _copies_start(name, build, nsem, srcs, land_shapes, after):
    n = len(srcs)

    def body(*refs):
        sends, _ = build(refs[:n], refs[n:2 * n], refs[2 * n + 1], refs[2 * n + 2])
        for cp in sends:
            cp.start()
        refs[-1][...] = jnp.zeros_like(refs[-1])

    lands = [pltpu.with_memory_space_constraint(lax.empty(s.shape, s.dtype), pltpu.HBM) for s in land_shapes]
    ins = [pltpu.with_memory_space_constraint(s, pltpu.HBM) for s in srcs]
    out = pl.pallas_call(
        body, name=name,
        out_shape=(pltpu.SemaphoreType.DMA((nsem,)), pltpu.SemaphoreType.DMA((nsem,)),
                   *[pltpu.HBM(s.shape, s.dtype) for s in srcs], *[pltpu.HBM(s.shape, s.dtype) for s in land_shapes],
                   jax.ShapeDtypeStruct((8, 128), F32)),
        in_specs=[HBM] * (2 * n) + [ANY],
        out_specs=(SEM, SEM, *[HBM] * (2 * n), pl.BlockSpec(memory_space=pltpu.VMEM)),
        input_output_aliases={i: 2 + i for i in range(2 * n)},
        compiler_params=pltpu.CompilerParams(has_side_effects=EFFECT),
    )(*ins, *lands, after)
    return out[0], out[1], out[2:2 + n], out[2 + n:2 + 2 * n], out[-1]


def _copies_wait(name, build, started, after):
    send_sems, recv_sems, srcs, lands, _ = started
    n = len(srcs)

    def body(*refs):
        sends, arrivals = build(refs[:n], refs[n:2 * n], refs[2 * n], refs[2 * n + 1])
        for cp in sends:
            cp.wait_send()
        for cp in arrivals:
            cp.wait_recv()

    out = pl.pallas_call(
        body, name=name,
        out_shape=(*[pltpu.HBM(s.shape, s.dtype) for s in srcs], *[pltpu.HBM(l.shape, l.dtype) for l in lands]),
        in_specs=[HBM] * (2 * n) + [SEM, SEM, ANY],
        out_specs=tuple([HBM] * (2 * n)),
        input_output_aliases={i: i for i in range(2 * n)},
        compiler_params=pltpu.CompilerParams(has_side_effects=EFFECT),
    )(*srcs, *lands, send_sems, recv_sems, after)
    return out[:n], out[n:]


def _exchange_copies(g_refs, got_refs, send_sems, recv_sems):
    x, y, c, _ = _place()
    sends, arrivals = [], []
    for a, (g, got) in enumerate(zip(g_refs, got_refs)):
        hr = g.shape[1] // 2
        cp = pltpu.make_async_remote_copy(
            src_ref=g.at[:, pl.ds((1 - c) * hr, hr), :], dst_ref=got, send_sem=send_sems.at[a],
            recv_sem=recv_sems.at[a], device_id=(x, y, 1 - c), device_id_type=MESH)
        sends.append(cp)
        arrivals.append(cp)
    return sends, arrivals


def _scatter_copies(s_refs, got_refs, send_sems, recv_sems):
    x, y, c, chips = _place()
    sends, arrivals = [], []
    for a, (s16, got) in enumerate(zip(s_refs, got_refs)):
        for j, chip in enumerate(chips):
            cp = pltpu.make_async_remote_copy(
                src_ref=s16.at[2 * chip[0] + chip[1]], dst_ref=got.at[j], send_sem=send_sems.at[3 * a + j],
                recv_sem=recv_sems.at[3 * a + j], device_id=(*chip, c), device_id_type=MESH)
            sends.append(cp)
            arrivals.append(cp)
    return sends, arrivals


def _share_halves(name, bufs, dep):
    n = len(bufs)

    def body(*refs):
        out_refs = refs[n + 1:2 * n + 1]
        send_sems, recv_sems = refs[2 * n + 1:]
        x, y, c, _ = _place()
        remote = [pltpu.make_async_remote_copy(
            src_ref=out_refs[a].at[c], dst_ref=out_refs[a].at[c], send_sem=send_sems.at[a], recv_sem=recv_sems.at[a],
            device_id=(x, y, 1 - c), device_id_type=MESH) for a in range(n)]
        for cp in remote:
            cp.start()
        for a in range(n):
            pltpu.make_async_remote_copy(
                src_ref=out_refs[a].at[c], dst_ref=out_refs[a].at[1 - c], send_sem=send_sems.at[a],
                recv_sem=recv_sems.at[a], device_id=(x, y, 1 - c), device_id_type=MESH).wait_recv()
        for cp in remote:
            cp.wait_send()

    return pl.pallas_call(
        body, name=name, in_specs=[ANY] * (n + 1), out_specs=[ANY] * n,
        out_shape=[jax.ShapeDtypeStruct(b.shape, b.dtype) for b in bufs],
        input_output_aliases={a: a for a in range(n)},
        scratch_shapes=[pltpu.SemaphoreType.DMA((n,)), pltpu.SemaphoreType.DMA((n,))],
    )(*bufs, dep)


SMALL_ROWS, SMALL_COLS = 16, CONV_CH
DN_NORM_LANE = 128


def _allreduce_small(block):
    m_per, ncol = block.shape

    def body(x_ref, sum_ref, all_ref, send_sems, recv_sems, local_sem):
        x, y, c, chips = _place()
        me, sibling = (x, y, c), (x, y, 1 - c)

        def rows(px, py, pc):
            return all_ref.at[pl.ds((4 * px + 2 * py + pc) * m_per, m_per), :]

        def copy(k, block_of, to, src=None):
            return pltpu.make_async_remote_copy(
                src_ref=rows(*block_of) if src is None else src, dst_ref=rows(*block_of),
                send_sem=send_sems.at[k], recv_sem=recv_sems.at[k], device_id=to, device_id_type=MESH)

        mine = pltpu.make_async_copy(x_ref, rows(*me), local_sem)
        mine.start()
        first = [copy(0, me, sibling, src=x_ref)]
        first += [copy(1 + j, me, (*chip, c), src=x_ref) for j, chip in enumerate(chips)]
        for cp in first:
            cp.start()
        passed = [copy(4 + j, (*chip, c), sibling) for j, chip in enumerate(chips)]
        for j, chip in enumerate(chips):
            copy(1 + j, (*chip, c), me).wait_recv()
            passed[j].start()
        copy(0, sibling, me).wait_recv()
        for j, chip in enumerate(chips):
            copy(4 + j, (*chip, 1 - c), me).wait_recv()
        for cp in first + passed:
            cp.wait_send()
        mine.wait()
        total = all_ref[0:m_per, :]
        for d in range(1, N_DEV):
            total = total + all_ref[d * m_per:(d + 1) * m_per, :]
        sum_ref[...] = total

    vm = pl.BlockSpec(memory_space=pltpu.VMEM)
    return pl.pallas_call(
        body, name="allreduce_small", in_specs=[vm], out_specs=vm,
        out_shape=jax.ShapeDtypeStruct((m_per, ncol), F32),
        scratch_shapes=[pltpu.VMEM((N_DEV * m_per, ncol), F32), pltpu.SemaphoreType.DMA((7,)),
                        pltpu.SemaphoreType.DMA((7,)), pltpu.SemaphoreType.DMA],
    )(block)


def _row_tile(rows, cols):
    tile = rows
    while tile * cols * 4 > (1 << 20) and tile % 16 == 0:
        tile //= 2
    return tile


def _elementwise(fn, name, ins, out_dtypes, dep):
    rows, cols = ins[0].shape
    tile = _row_tile(rows, cols)

    def body(*refs):
        outs = fn(*[r[...] for r in refs[:len(ins)]])
        for o_ref, o in zip(refs[len(ins) + 1:], outs):
            o_ref[...] = o.astype(o_ref.dtype)

    if tile * cols * 4 > (1 << 21) and cols % 512 == 0:
        spec = pl.BlockSpec((rows, 256), lambda i: (0, i))
        steps = cols // 256
    else:
        spec = pl.BlockSpec((tile, cols), lambda i: (i, 0))
        steps = rows // tile
    return pl.pallas_call(
        body, name=name, grid=(steps,), in_specs=[spec] * len(ins) + [pl.BlockSpec(memory_space=pl.ANY)],
        out_specs=[spec] * len(out_dtypes),
        out_shape=[jax.ShapeDtypeStruct((rows, cols), d) for d in out_dtypes],
        compiler_params=_params("parallel"),
    )(*ins, dep)


def _adamw_tile(w, g, m, v):
    m = ADAM_B1 * m + (1.0 - ADAM_B1) * g
    v = ADAM_B2 * v + (1.0 - ADAM_B2) * jnp.square(g)
    m_hat = m / (1.0 - ADAM_B1 ** ADAM_STEP)
    v_hat = v / (1.0 - ADAM_B2 ** ADAM_STEP)
    delta = -ADAM_LR * (m_hat / (jnp.sqrt(v_hat) + ADAM_EPS) + ADAM_WD * w)
    return delta, m, v


def _adamw(name, w, g, m, v, dep):
    return _elementwise(_adamw_tile, name, [w, g, m, v], [F32, F32, F32], dep)


def _chip_sum(name, g4, got, place):
    nchip, hr, cols = got.shape
    tile = _row_tile(hr, cols)
    nblk = hr // tile

    def body(pl_ref, g_ref, o_ref, s32_ref, s16_ref):
        s = g_ref[...] + o_ref[...]
        s16_ref[...] = s.astype(BF16)

        @pl.when(pl.program_id(1) == pl_ref[0])
        def _():
            s32_ref[...] = s

    spec = pl.BlockSpec((None, tile, cols), lambda i, k, pr: (k, i, 0))
    return pl.pallas_call(
        body, name=name,
        grid_spec=pltpu.PrefetchScalarGridSpec(
            num_scalar_prefetch=1, grid=(nblk, nchip),
            in_specs=[pl.BlockSpec((None, tile, cols), lambda i, k, pr: (k, pr[1] * nblk + i, 0)), spec],
            out_specs=[pl.BlockSpec((tile, cols), lambda i, k, pr: (i, 0)), spec]),
        out_shape=[jax.ShapeDtypeStruct((hr, cols), F32), jax.ShapeDtypeStruct(got.shape, BF16)],
        compiler_params=_params("parallel", "arbitrary"),
    )(place, g4, got)


def _mesh_sum(name, s32, got, place):
    hr, cols = s32.shape
    tile = _row_tile(hr, cols)

    def body(pl_ref, own_ref, g0_ref, g1_ref, g2_ref, o_ref):
        o_ref[...] = ((own_ref[...] + g0_ref[...].astype(F32)) + g1_ref[...].astype(F32)) + g2_ref[...].astype(F32)

    slab = lambda j: pl.BlockSpec((None, tile, cols), lambda i, pr: (j, i, 0))
    return pl.pallas_call(
        body, name=name,
        grid_spec=pltpu.PrefetchScalarGridSpec(
            num_scalar_prefetch=1, grid=(hr // tile,),
            in_specs=[pl.BlockSpec((tile, cols), lambda i, pr: (i, 0)), slab(0), slab(1), slab(2)],
            out_specs=pl.BlockSpec((None, tile, cols), lambda i, pr: (pr[1], i, 0))),
        out_shape=jax.ShapeDtypeStruct((2, hr, cols), F32),
        compiler_params=_params("parallel"),
    )(place, s32, got, got, got)


def _place_operand():
    return jnp.stack([2 * lax.axis_index("x") + lax.axis_index("y"), lax.axis_index("c")]).astype(jnp.int32)


W_IN_ROWS = 720
W_IN_GATHER_ROWS = 736
BF16_TILE_ROWS = 16


def _join_w_in(blocks):
    rows, t = D_IN // N_CHIPS, BF16_TILE_ROWS
    first = [rows * k // t * t for k in range(N_CHIPS)]
    parts = []
    for k in range(N_CHIPS):
        lo = t if k else 0
        if k + 1 < N_CHIPS:
            hi = first[k + 1] - first[k]
            assert rows * (k + 1) <= first[k + 1] + t and hi + t <= W_IN_GATHER_ROWS
            parts += [blocks[k, lo:hi], blocks[k, hi:hi + t] + blocks[k + 1, :t]]
        else:
            parts.append(blocks[k, lo:])
    return jnp.concatenate(parts, axis=0)


def _per_chip(name, g):
    if name == "w_in_t":
        rows = D_IN // N_CHIPS
        return jnp.stack([lax.slice_in_dim(g, rows * k, rows * k + W_IN_ROWS) for k in range(N_CHIPS)])
    if name == "w_proj":
        return jnp.transpose(g.reshape(PLE_DIM, N_CHIPS, D_MODEL // N_CHIPS), (1, 0, 2))
    if name == "w_up4":
        return g
    return g.reshape(N_CHIPS, g.shape[0] // N_CHIPS, g.shape[1])


class _EarlyReducer:
    def __init__(self, tag):
        self.tag = tag

    def start(self, grads):
        self.names = list(grads)
        self.place = _place_operand()
        slabs = [_per_chip(k, grads[k]) for k in self.names]
        halves = [jax.ShapeDtypeStruct((s.shape[0], s.shape[1] // 2, s.shape[2]), F32) for s in slabs]
        self.a = _copies_start(self.tag + "exchange_start", _exchange_copies, len(slabs), slabs, halves,
                               slabs[0][0, :8, :128])
        return self.a[-1]

    def middle(self, after):
        slabs, got = _copies_wait(self.tag + "exchange_wait", _exchange_copies, self.a, after)
        self.sums = [_chip_sum(self.tag + "chip_sum_" + k, s, g, self.place) for k, s, g in zip(self.names, slabs, got)]
        s16 = [s[1] for s in self.sums]
        lands = [jax.ShapeDtypeStruct((3,) + s.shape[1:], BF16) for s in s16]
        self.b = _copies_start(self.tag + "scatter_start", _scatter_copies, 3 * len(s16), s16, lands,
                               self.sums[0][0][:8, :128])
        return self.b[-1]

    def finish(self, after):
        _, got = _copies_wait(self.tag + "scatter_wait", _scatter_copies, self.b, after)
        self.bufs = {k: _mesh_sum(self.tag + "mesh_sum_" + k, s[0], g, self.place)
                     for k, s, g in zip(self.names, self.sums, got)}


def kernel(x, p, norm_mix, w_in, conv_w, a_log, dt_bias, dn_norm, sinks, w_o, norm_mlp, w_up, w_down, norm_ple, w_ple_gate, w_ple_proj, norm_final, loss_target, m_norm_mix, m_w_in, m_conv_w, m_a_log, m_dt_bias, m_dn_norm, m_sinks, m_w_o, m_norm_mlp, m_w_up, m_w_down, m_norm_ple, m_w_ple_gate, m_w_ple_proj, m_norm_final, v_norm_mix, v_w_in, v_conv_w, v_a_log, v_dt_bias, v_dn_norm, v_sinks, v_w_o, v_norm_mlp, v_w_up, v_w_down, v_norm_ple, v_w_ple_gate, v_w_ple_proj, v_norm_final):
    chip = 2 * lax.axis_index("x") + lax.axis_index("y")
    big = dict(w_in=w_in[0], w_o=w_o[0], w_up=w_up[0], w_down=w_down[0], w_gate=w_ple_gate[0], w_proj=w_ple_proj[0])
    big_m = dict(w_in=m_w_in[0], w_o=m_w_o[0], w_up=m_w_up[0], w_down=m_w_down[0], w_gate=m_w_ple_gate[0], w_proj=m_w_ple_proj[0])
    big_v = dict(w_in=v_w_in[0], w_o=v_w_o[0], w_up=v_w_up[0], w_down=v_w_down[0], w_gate=v_w_ple_gate[0], w_proj=v_w_ple_proj[0])
    names = list(big)

    rows_in = D_IN // N_CHIPS
    chip_index = 2 * lax.axis_index("x") + lax.axis_index("y")
    w_in_shard_t = lax.dynamic_update_slice(jnp.zeros((W_IN_GATHER_ROWS, D_MODEL), BF16), big["w_in"].T.astype(BF16),
                                            ((rows_in * chip_index) % BF16_TILE_ROWS, 0))
    w_in_all, conv_all = _gather_weights([w_in_shard_t], conv_w[0])
    late_names = names[1:]
    late_shards = [big[k].astype(BF16) for k in late_names]
    gather = _copies_start("gather_start", _late_copies, LATE_COPIES * len(late_shards), late_shards,
                           [jax.ShapeDtypeStruct((N_CHIPS,) + s.shape, BF16) for s in late_shards], w_in_all)
    token = gather[-1]
    w = dict(w_in_t=_join_w_in(w_in_all),
             conv_w=jnp.transpose(conv_all, (1, 0, 2)).reshape(DN_CONV, CONV_CH))
    sm = dict(norm_mix=norm_mix[0] + token[0, 0], a_log=a_log[0], dt_bias=dt_bias[0], dn_norm=dn_norm[0],
              sinks=sinks[0], norm_mlp=norm_mlp[0], norm_ple=norm_ple[0], norm_final=norm_final)

    def late(after):
        first = functools.partial(_late_copies, only=(0,))
        srcs, lands = _copies_wait("gather_wait_o", first, gather, after)

        def rest(after2):
            others = functools.partial(_late_copies, only=tuple(range(1, len(late_names))))
            gw = dict(zip(late_names, _copies_wait("gather_wait_rest", others, gather[:2] + (srcs, lands, None), after2)[1]))
            return dict(w_up4=gw["w_up"], w_down=gw["w_down"].reshape(D_FF, D_MODEL),
                        w_gate=gw["w_gate"].reshape(D_MODEL, D_MODEL), w_proj4=gw["w_proj"])

        return lands[0].reshape(D_MODEL, D_MODEL), rest

    reducer = _EarlyReducer("early_")
    sums, grad_x, g = _local_step(x[0], p[0, 0], loss_target[0], sm, w, late, reducer)

    last = _EarlyReducer("last_")
    dep_a = last.start({"w_in_t": g["w_in_t"]})

    row = lambda v: jnp.pad(v, (0, SMALL_COLS - v.shape[0]))

    def misc_row(al, dtb, sk, dnn, rest):
        head = jnp.concatenate([al, dtb, sk])
        return row(jnp.concatenate([head, jnp.zeros((DN_NORM_LANE - head.shape[0],), F32), dnn, rest]))

    misc = misc_row(sums["a_log"], sums["dt_bias"], sums["sinks"], sums["dn_norm"], sums["loss"].reshape(1))
    small = jnp.concatenate([sums["conv_w"], jnp.stack([row(sums["norm_mix"]), row(sums["norm_mlp"]), row(sums["norm_ple"]),
                                                        row(sums["norm_final"]), misc]),
                             jnp.zeros((SMALL_ROWS - 9, SMALL_COLS), F32)], axis=0)
    tot = _allreduce_small(small + dep_a[0, 0])
    dep_b = last.middle(tot)
    grad_key = dict(w_o="w_o", w_up="w_up4", w_down="w_down", w_gate="w_gate", w_proj="w_proj")
    full = _share_halves("share_halves", [reducer.bufs[grad_key[k]] for k in late_names], dep_b)
    red = {k: f.reshape(-1, f.shape[-1]) for k, f in zip(late_names, full)}
    loss = tot[8, 256]
    ncw = CONV_CH // N_CHIPS

    def pack(cw, nmix, nmlp, nple, nfin, al, dtb, sk, dnn):
        misc_p = misc_row(al, dtb, sk, dnn, jnp.zeros((0,), F32))
        cw_p = jnp.pad(cw, ((0, 0), (0, SMALL_COLS - ncw)))
        return jnp.concatenate([cw_p, jnp.stack([row(nmix), row(nmlp), row(nple), row(nfin), misc_p]),
                                jnp.zeros((SMALL_ROWS - 9, SMALL_COLS), F32)], axis=0)

    def unpack(buf):
        return dict(conv_w=buf[0:4, :ncw][None], norm_mix=buf[4, :D_MODEL][None], norm_mlp=buf[5, :D_MODEL][None],
                    norm_ple=buf[6, :D_MODEL][None], norm_final=buf[7, :D_MODEL], a_log=buf[8, 0:4][None],
                    dt_bias=buf[8, 4:8][None], sinks=buf[8, 8:16][None], dn_norm=buf[8, 128:256][None])

    g_conv_shard = lax.dynamic_slice(tot[0:4], (0, chip * ncw), (DN_CONV, ncw))
    g_small = pack(g_conv_shard, tot[4, :D_MODEL], tot[5, :D_MODEL], tot[6, :D_MODEL], tot[7, :D_MODEL],
                   tot[8, 0:4], tot[8, 4:8], tot[8, 8:16], tot[8, 128:256])
    w_small = pack(conv_w[0], norm_mix[0], norm_mlp[0], norm_ple[0], norm_final, a_log[0], dt_bias[0], sinks[0], dn_norm[0])
    m_small = pack(m_conv_w[0], m_norm_mix[0], m_norm_mlp[0], m_norm_ple[0], m_norm_final, m_a_log[0], m_dt_bias[0],
                   m_sinks[0], m_dn_norm[0])
    v_small = pack(v_conv_w[0], v_norm_mix[0], v_norm_mlp[0], v_norm_ple[0], v_norm_final, v_a_log[0], v_dt_bias[0],
                   v_sinks[0], v_dn_norm[0])

    ref_name = dict(w_in="w_in", w_o="w_o", w_up="w_up", w_down="w_down", w_gate="w_ple_gate", w_proj="w_ple_proj")
    out_g, out_d, out_m, out_v = {}, {}, {}, {}

    def update(k, dep):
        d_k, m_k, v_k = _adamw("adamw_" + k, big[k], red[k], big_m[k], big_v[k], dep)
        out_g[ref_name[k]], out_d[ref_name[k]] = red[k][None], d_k[None]
        out_m[ref_name[k]], out_v[ref_name[k]] = m_k[None], v_k[None]
        return d_k

    for k in late_names:
        done = update(k, dep_b)
    small_out = _adamw("adamw_small", w_small, g_small, m_small, v_small, dep_b)
    d_s, m_s, v_s = (unpack(b) for b in small_out)
    g_s = unpack(g_small)
    for src, dst in ((g_s, out_g), (d_s, out_d), (m_s, out_m), (v_s, out_v)):
        dst.update(src)
    last.finish(done + small_out[0][0:1, 0:1])
    (w_in_full,) = _share_halves("share_halves_w_in", [last.bufs["w_in_t"]], dep_b)
    g_t = w_in_full.reshape(W_IN_ROWS, D_MODEL)[:D_IN // N_CHIPS]
    d_t, m_t, v_t = _adamw("adamw_w_in", big["w_in"].T, g_t, big_m["w_in"].T, big_v["w_in"].T, dep_b)
    out_g["w_in"], out_d["w_in"], out_m["w_in"], out_v["w_in"] = g_t.T[None], d_t.T[None], m_t.T[None], v_t.T[None]
    order = ["norm_mix", "w_in", "conv_w", "a_log", "dt_bias", "dn_norm", "sinks", "w_o", "norm_mlp", "w_up", "w_down",
             "norm_ple", "w_ple_gate", "w_ple_proj", "norm_final"]
    return (loss, grad_x[None], *[out_g[k] for k in order], *[out_d[k] for k in order],
            *[out_m[k] for k in order], *[out_v[k] for k in order])
```

```python
import functools

import jax
import jax.numpy as jnp
from jax import lax
from jax.experimental import pallas as pl
from jax.experimental.pallas import tpu as pltpu

F32 = jnp.float32
BF16 = jnp.bfloat16
MXU_DTYPE = jnp.bfloat16
HI = lax.Precision.HIGHEST

D_MODEL = 1024
PLE_DIM = 256
ATTN_HEADS = 8
ATTN_KV_HEADS = 2
ATTN_GROUPS = ATTN_HEADS // ATTN_KV_HEADS
ATTN_HEAD_DIM = 64
ATTN_BLOCK = 128
ROPE_THETA = 10000.0
DN_HEADS = 4
DN_HEAD_DIM = 128
DN_CONV = 4
DN_CHUNK = 64
D_FF = 4 * D_MODEL
EPS = 1e-6
ATTN_Q = ATTN_HEADS * ATTN_HEAD_DIM
ATTN_KV = ATTN_KV_HEADS * ATTN_HEAD_DIM
DN_W = DN_HEADS * DN_HEAD_DIM
CONV_CH = 3 * DN_W
D_IN = ATTN_Q + 2 * ATTN_KV + 4 * DN_W + 2 * DN_HEADS
DN_COLS = 4 * DN_W + 128
DN_SCALE = DN_HEAD_DIM ** -0.5
ATTN_SCALE = ATTN_HEAD_DIM ** -0.5
FF_BLOCKS = 4
FF_BLOCK = D_FF // FF_BLOCKS

ADAM_LR = 0.001
ADAM_B1 = 0.9
ADAM_B2 = 0.999
ADAM_EPS = 1e-08
ADAM_WD = 0.01
ADAM_STEP = 10

V7X_VMEM_BYTES = 64 * 1024 * 1024
VMEM_LIMIT = 48 * 1024 * 1024

NN = ((1,), (0,))
NT = ((1,), (1,))
TN = ((0,), (0,))


def _dot(a, b, dims=NN, prec=None):
    if a.dtype != b.dtype:
        a, b = a.astype(MXU_DTYPE), b.astype(MXU_DTYPE)
    return lax.dot_general(a, b, (dims, ((), ())), precision=prec, preferred_element_type=F32)


def _sigmoid(x):
    return 1.0 / (1.0 + jnp.exp(-x))


def _softplus(x):
    return jnp.maximum(x, 0.0) + jnp.log(1.0 + jnp.exp(-jnp.abs(x)))


def _params(*sem):
    return pltpu.CompilerParams(dimension_semantics=sem, vmem_limit_bytes=VMEM_LIMIT)


def _rms_fwd(xv, g):
    r = lax.rsqrt(jnp.mean(xv * xv, axis=-1, keepdims=True) + EPS)
    return xv * r * g


def _rms_bwd(xv, g, dn):
    r = lax.rsqrt(jnp.mean(xv * xv, axis=-1, keepdims=True) + EPS)
    xh = xv * r
    dg = jnp.sum(dn * xh, axis=0, keepdims=True)
    dxh = dn * g
    dx = r * (dxh - xh * jnp.mean(dxh * xh, axis=-1, keepdims=True))
    return dx, dg


def _full(shape):
    return pl.BlockSpec(shape, lambda *_: (0,) * len(shape))


def _inproj(x, g_mix, wa_t, wd_t, tm):
    t = x.shape[0]

    def body(x_ref, g_ref, wa_ref, wd_ref, u_ref, pa_ref, pd_ref):
        u = _rms_fwd(x_ref[...], g_ref[...]).astype(MXU_DTYPE)
        u_ref[...] = u
        pa_ref[...] = _dot(u, wa_ref[...], NT)
        pd_ref[...] = _dot(u, wd_ref[...], NT)

    na, nd = wa_t.shape[0], wd_t.shape[0]
    return pl.pallas_call(
        body, name="inproj", grid=(t // tm,),
        in_specs=[pl.BlockSpec((tm, D_MODEL), lambda i: (i, 0)), _full((1, D_MODEL)),
                  _full((na, D_MODEL)), _full((nd, D_MODEL))],
        out_specs=[pl.BlockSpec((tm, D_MODEL), lambda i: (i, 0)), pl.BlockSpec((tm, na), lambda i: (i, 0)),
                   pl.BlockSpec((tm, nd), lambda i: (i, 0))],
        out_shape=[jax.ShapeDtypeStruct((t, D_MODEL), MXU_DTYPE), jax.ShapeDtypeStruct((t, na), F32),
                   jax.ShapeDtypeStruct((t, nd), F32)],
        compiler_params=_params("parallel"),
    )(x, g_mix, wa_t, wd_t)


def _oproj(x, ao, dn, wo_a, wo_d, tm):
    t = x.shape[0]

    def body(x_ref, ao_ref, dn_ref, wa_ref, wd_ref, h_ref):
        h_ref[...] = (x_ref[...] + _dot(ao_ref[...].astype(MXU_DTYPE), wa_ref[...])
                      + _dot(dn_ref[...].astype(MXU_DTYPE), wd_ref[...]))

    half = ao.shape[1]
    return pl.pallas_call(
        body, name="oproj", grid=(t // tm,),
        in_specs=[pl.BlockSpec((tm, D_MODEL), lambda i: (i, 0)), pl.BlockSpec((tm, half), lambda i: (i, 0)),
                  pl.BlockSpec((tm, half), lambda i: (i, 0)), _full((half, D_MODEL)), _full((half, D_MODEL))],
        out_specs=pl.BlockSpec((tm, D_MODEL), lambda i: (i, 0)),
        out_shape=jax.ShapeDtypeStruct((t, D_MODEL), F32),
        compiler_params=_params("parallel"),
    )(x, ao, dn, wo_a, wo_d)


def _mlp_fwd(h1, g_mlp, w_up4, w_down, tm):
    t = h1.shape[0]

    def body(h_ref, g_ref, wu_ref, wd_ref, m_ref, r_ref, h2_ref, acc_ref):
        k = pl.program_id(1)

        @pl.when(k == 0)
        def _():
            m_ref[...] = _rms_fwd(h_ref[...], g_ref[...]).astype(MXU_DTYPE)
            acc_ref[...] = jnp.zeros_like(acc_ref)

        r = jnp.maximum(_dot(m_ref[...], wu_ref[...]), 0.0)
        r_ref[...] = r.astype(MXU_DTYPE)
        s = jnp.square(r).astype(MXU_DTYPE)
        acc_ref[...] += _dot(s, wd_ref[...])

        @pl.when(k == FF_BLOCKS - 1)
        def _():
            h2_ref[...] = h_ref[...] + acc_ref[...]

    return pl.pallas_call(
        body, name="mlp_fwd", grid=(t // tm, FF_BLOCKS),
        in_specs=[pl.BlockSpec((tm, D_MODEL), lambda i, k: (i, 0)), _full((1, D_MODEL)),
                  pl.BlockSpec((None, D_MODEL, FF_BLOCK), lambda i, k: (k, 0, 0)),
                  pl.BlockSpec((FF_BLOCK, D_MODEL), lambda i, k: (k, 0))],
        out_specs=[pl.BlockSpec((tm, D_MODEL), lambda i, k: (i, 0)), pl.BlockSpec((tm, FF_BLOCK), lambda i, k: (i, k)),
                   pl.BlockSpec((tm, D_MODEL), lambda i, k: (i, 0))],
        out_shape=[jax.ShapeDtypeStruct((t, D_MODEL), MXU_DTYPE), jax.ShapeDtypeStruct((t, D_FF), MXU_DTYPE),
                   jax.ShapeDtypeStruct((t, D_MODEL), F32)],
        scratch_shapes=[pltpu.VMEM((tm, D_MODEL), F32)],
        compiler_params=_params("parallel", "arbitrary"),
    )(h1, g_mlp, w_up4, w_down)


def _ple_loss(h2, p, tgt, g_ple, g_fin, w_gate, w_proj, tm):
    t = h2.shape[0]

    def body(h_ref, p_ref, t_ref, gp_ref, gf_ref, wg_ref, wp_ref,
             dh_ref, dhb_ref, dgp_ref, dpp_ref, n3_ref, pb_ref, acc_ref):
        @pl.when(pl.program_id(0) == 0)
        def _():
            acc_ref[...] = jnp.zeros_like(acc_ref)

        h = h_ref[...]
        g_ple_v, g_fin_v = gp_ref[...], gf_ref[...]
        n3 = _rms_fwd(h, g_ple_v).astype(MXU_DTYPE)
        n3_ref[...] = n3
        gate = _sigmoid(_dot(n3, wg_ref[...]))
        pb = p_ref[...].astype(MXU_DTYPE)
        pb_ref[...] = pb
        pp = _dot(pb, wp_ref[...])
        h3 = h + gate * pp
        r4 = lax.rsqrt(jnp.mean(h3 * h3, axis=-1, keepdims=True) + EPS)
        xh4 = h3 * r4
        e = xh4 * g_fin_v - t_ref[...]
        loss = 0.5 * jnp.sum(jnp.mean(e * e, axis=-1, keepdims=True), axis=0, keepdims=True)
        dy = e * (1.0 / D_MODEL)
        dg_fin = jnp.sum(dy * xh4, axis=0, keepdims=True)
        dxh = dy * g_fin_v
        dh3 = r4 * (dxh - xh4 * jnp.mean(dxh * xh4, axis=-1, keepdims=True))
        dpp_ref[...] = (dh3 * gate).astype(MXU_DTYPE)
        dgp = (dh3 * pp * gate * (1.0 - gate)).astype(MXU_DTYPE)
        dgp_ref[...] = dgp
        dn3 = _dot(dgp, wg_ref[...], NT)
        dx, dg_ple = _rms_bwd(h, g_ple_v, dn3)
        dh2 = dh3 + dx
        dh_ref[...] = dh2
        dhb_ref[...] = dh2.astype(MXU_DTYPE)
        acc_ref[0:1, :] += dg_fin
        acc_ref[1:2, :] += dg_ple
        acc_ref[2:3, :] += jnp.broadcast_to(loss, (1, D_MODEL))

    row = lambda w: pl.BlockSpec((tm, w), lambda i: (i, 0))
    return pl.pallas_call(
        body, name="ple_loss", grid=(t // tm,),
        in_specs=[row(D_MODEL), row(PLE_DIM), row(D_MODEL), _full((1, D_MODEL)), _full((1, D_MODEL)),
                  _full((D_MODEL, D_MODEL)), _full((PLE_DIM, D_MODEL))],
        out_specs=[row(D_MODEL), row(D_MODEL), row(D_MODEL), row(D_MODEL), row(D_MODEL), row(PLE_DIM),
                   _full((8, D_MODEL))],
        out_shape=[jax.ShapeDtypeStruct((t, D_MODEL), F32), jax.ShapeDtypeStruct((t, D_MODEL), MXU_DTYPE),
                   jax.ShapeDtypeStruct((t, D_MODEL), MXU_DTYPE), jax.ShapeDtypeStruct((t, D_MODEL), MXU_DTYPE),
                   jax.ShapeDtypeStruct((t, D_MODEL), MXU_DTYPE), jax.ShapeDtypeStruct((t, PLE_DIM), MXU_DTYPE),
                   jax.ShapeDtypeStruct((8, D_MODEL), F32)],
        compiler_params=_params("arbitrary"),
    )(h2, p, tgt, g_ple, g_fin, w_gate, w_proj)


def _mlp_bwd(dh2, dh2b, r, h1, g_mlp, w_up4, w_down, wo_a, wo_d, tm):
    t = h1.shape[0]
    half = wo_a.shape[0]

    def body(dh_ref, dhb_ref, r_ref, h_ref, g_ref, wu_ref, wd_ref, woa_ref, wod_ref,
             da_ref, dh1_ref, dh1b_ref, dao_ref, ddn_ref, acc_ref, dm_ref):
        i, k = pl.program_id(0), pl.program_id(1)

        @pl.when((i == 0) & (k == 0))
        def _():
            acc_ref[...] = jnp.zeros_like(acc_ref)

        @pl.when(k == 0)
        def _():
            dm_ref[...] = jnp.zeros_like(dm_ref)

        ds = _dot(dhb_ref[...], wd_ref[...], NT)
        da = (ds * (2.0 * r_ref[...].astype(F32))).astype(MXU_DTYPE)
        da_ref[...] = da
        dm_ref[...] += _dot(da, wu_ref[...], NT)

        @pl.when(k == FF_BLOCKS - 1)
        def _():
            dx, dg = _rms_bwd(h_ref[...], g_ref[...], dm_ref[...])
            dh1 = dh_ref[...] + dx
            dh1_ref[...] = dh1
            dh1b = dh1.astype(MXU_DTYPE)
            dh1b_ref[...] = dh1b
            dao_ref[...] = _dot(dh1b, woa_ref[...], NT)
            ddn_ref[...] = _dot(dh1b, wod_ref[...], NT)
            acc_ref[0:1, :] += dg

    tok = lambda w: pl.BlockSpec((tm, w), lambda i, k: (i, 0))
    return pl.pallas_call(
        body, name="mlp_bwd", grid=(t // tm, FF_BLOCKS),
        in_specs=[tok(D_MODEL), tok(D_MODEL), pl.BlockSpec((tm, FF_BLOCK), lambda i, k: (i, k)), tok(D_MODEL),
                  _full((1, D_MODEL)), pl.BlockSpec((None, D_MODEL, FF_BLOCK), lambda i, k: (k, 0, 0)),
                  pl.BlockSpec((FF_BLOCK, D_MODEL), lambda i, k: (k, 0)),
                  pl.BlockSpec((half, D_MODEL), lambda i, k: (0, 0)), pl.BlockSpec((half, D_MODEL), lambda i, k: (0, 0))],
        out_specs=[pl.BlockSpec((tm, FF_BLOCK), lambda i, k: (i, k)),
                   tok(D_MODEL), tok(D_MODEL), tok(half), tok(half), pl.BlockSpec((8, D_MODEL), lambda i, k: (0, 0))],
        out_shape=[jax.ShapeDtypeStruct((t, D_FF), MXU_DTYPE),
                   jax.ShapeDtypeStruct((t, D_MODEL), F32), jax.ShapeDtypeStruct((t, D_MODEL), MXU_DTYPE),
                   jax.ShapeDtypeStruct((t, half), F32), jax.ShapeDtypeStruct((t, half), F32),
                   jax.ShapeDtypeStruct((8, D_MODEL), F32)],
        scratch_shapes=[pltpu.VMEM((tm, D_MODEL), F32)],
        compiler_params=_params("arbitrary", "arbitrary"),
    )(dh2, dh2b, r, h1, g_mlp, w_up4, w_down, wo_a, wo_d)


def _inproj_bwd(x, dh1, g_mix, grads, weights, tm):
    t = x.shape[0]
    n = len(grads)

    def body(*refs):
        x_ref, dh_ref, g_ref = refs[:3]
        g_refs, w_refs = refs[3:3 + n], refs[3 + n:3 + 2 * n]
        dx_ref, acc_ref = refs[3 + 2 * n:]

        @pl.when(pl.program_id(0) == 0)
        def _():
            acc_ref[...] = jnp.zeros_like(acc_ref)

        du = _dot(g_refs[0][...], w_refs[0][...])
        for j in range(1, n):
            du += _dot(g_refs[j][...], w_refs[j][...])
        dx, dg = _rms_bwd(x_ref[...], g_ref[...], du)
        dx_ref[...] = dh_ref[...] + dx
        acc_ref[0:1, :] += dg

    tok = lambda w: pl.BlockSpec((tm, w), lambda i: (i, 0))
    return pl.pallas_call(
        body, name="inproj_bwd", grid=(t // tm,),
        in_specs=[tok(D_MODEL), tok(D_MODEL), _full((1, D_MODEL))] + [tok(g.shape[1]) for g in grads]
                 + [_full(w.shape) for w in weights],
        out_specs=[tok(D_MODEL), _full((8, D_MODEL))],
        out_shape=[jax.ShapeDtypeStruct((t, D_MODEL), F32), jax.ShapeDtypeStruct((8, D_MODEL), F32)],
        compiler_params=_params("arbitrary"),
    )(x, dh1, g_mix, *grads, *weights)


def _wgrad(a, b, name, tk, tn, tt, stacked=False, prep=None):
    t, kdim = a.shape
    ncols = b.shape[1]

    def body(a_ref, b_ref, o_ref):
        @pl.when(pl.program_id(2) == 0)
        def _():
            o_ref[...] = jnp.zeros_like(o_ref)

        av = a_ref[...] if prep is None else prep(a_ref[...])
        o_ref[...] += _dot(av, b_ref[...], TN)

    if stacked:
        out_spec = pl.BlockSpec((None, tk, tn), lambda i, j, s: (j, i, 0))
        out_shape = jax.ShapeDtypeStruct((ncols // tn, kdim, tn), F32)
    else:
        out_spec = pl.BlockSpec((tk, tn), lambda i, j, s: (i, j))
        out_shape = jax.ShapeDtypeStruct((kdim, ncols), F32)
    return pl.pallas_call(
        body, name=name, grid=(kdim // tk, ncols // tn, t // tt),
        in_specs=[pl.BlockSpec((tt, tk), lambda i, j, s: (s, i)), pl.BlockSpec((tt, tn), lambda i, j, s: (s, j))],
        out_specs=out_spec, out_shape=out_shape,
        compiler_params=_params("parallel", "parallel", "arbitrary"),
    )(a, b)


def _wgrad_cat(as_, bs, name, tt):
    t = as_[0].shape[0]
    heights = [a.shape[1] for a in as_]
    widths = [b.shape[1] for b in bs]

    def body(*refs):
        a_refs, b_refs, o_ref = refs[:len(as_)], refs[len(as_):-1], refs[-1]

        @pl.when(pl.program_id(0) == 0)
        def _():
            o_ref[...] = jnp.zeros_like(o_ref)

        row = 0
        for a_ref, k in zip(a_refs, heights):
            av = a_ref[...]
            col = 0
            for b_ref, n in zip(b_refs, widths):
                o_ref[row:row + k, col:col + n] += _dot(av, b_ref[...], TN)
                col += n
            row += k

    tok = lambda w: pl.BlockSpec((tt, w), lambda s: (s, 0))
    shape = (sum(heights), sum(widths))
    return pl.pallas_call(
        body, name=name, grid=(t // tt,),
        in_specs=[tok(k) for k in heights] + [tok(n) for n in widths],
        out_specs=_full(shape), out_shape=jax.ShapeDtypeStruct(shape, F32),
        compiler_params=_params("arbitrary"),
    )(*as_, *bs)


def _rope_tables(t):
    half = ATTN_HEAD_DIM // 2
    inv = 1.0 / (ROPE_THETA ** (jnp.arange(half, dtype=F32) * (2.0 / ATTN_HEAD_DIM)))
    ang = jnp.arange(t, dtype=F32)[:, None] * inv[None, :]
    cos, sin = jnp.cos(ang), jnp.sin(ang)
    cos2 = jnp.concatenate([cos, cos], axis=-1)
    sin2 = jnp.concatenate([-sin, sin], axis=-1)
    return jnp.tile(cos2, (1, 2)), jnp.tile(sin2, (1, 2))


def _swap_halves(tv):
    w = tv.shape[-1]
    lane = lax.broadcasted_iota(jnp.int32, tv.shape, tv.ndim - 1)
    first = (lane % ATTN_HEAD_DIM) < (ATTN_HEAD_DIM // 2)
    return jnp.where(first, pltpu.roll(tv, w - ATTN_HEAD_DIM // 2, tv.ndim - 1),
                     pltpu.roll(tv, ATTN_HEAD_DIM // 2, tv.ndim - 1))


def _rope(tv, cos, sin):
    return tv * cos + _swap_halves(tv) * sin


def _rope_bwd(dv, cos, sin):
    return dv * cos + _swap_halves(dv * sin)


def _attn_valid(first_block):
    c = lax.broadcasted_iota(jnp.int32, (2 * ATTN_BLOCK, ATTN_BLOCK), 0)
    r = lax.broadcasted_iota(jnp.int32, (2 * ATTN_BLOCK, ATTN_BLOCK), 1)
    return (c > r) & (c <= r + ATTN_BLOCK) & ((c >= ATTN_BLOCK) | jnp.logical_not(first_block))


def _attn_probs(st, sink, valid):
    s = jnp.where(valid, st * ATTN_SCALE, -jnp.inf)
    m = jnp.maximum(jnp.max(s, axis=0, keepdims=True), sink)
    e = jnp.where(valid, jnp.exp(s - m), 0.0)
    es = jnp.exp(sink - m)
    inv = 1.0 / (jnp.sum(e, axis=0, keepdims=True) + es)
    return e * inv, es * inv


def _lane_scalar(vec, idx):
    lane = lax.broadcasted_iota(jnp.int32, vec.shape, 1)
    return jnp.sum(jnp.where(lane == idx, vec, 0.0), axis=-1, keepdims=True)


ATTN_STEP = 2 * ATTN_BLOCK


def _attn_specs(ns):
    cur = lambda w, cb: pl.BlockSpec((ATTN_STEP, w), lambda i: (jnp.minimum(i, ns - 1), cb))
    prev = lambda w, cb: pl.BlockSpec((ATTN_BLOCK, w), lambda i: (jnp.maximum(2 * jnp.minimum(i, ns - 1) - 1, 0), cb))
    kcol, vcol = ATTN_Q // ATTN_KV, ATTN_Q // ATTN_KV + 1
    return [cur(ATTN_Q, 0), cur(ATTN_KV, kcol), prev(ATTN_KV, kcol), cur(ATTN_KV, vcol), prev(ATTN_KV, vcol),
            cur(ATTN_KV, 0), cur(ATTN_KV, 0), prev(ATTN_KV, 0), prev(ATTN_KV, 0), _full((1, 128))]


def _attn_windows(tp, tc):
    hsl = lambda hk: slice(hk * ATTN_HEAD_DIM, (hk + 1) * ATTN_HEAD_DIM)
    return [[jnp.concatenate([tp[:, hsl(hk)], tc[0:ATTN_BLOCK, hsl(hk)]], axis=0) for hk in range(ATTN_KV_HEADS)],
            [tc[:, hsl(hk)] for hk in range(ATTN_KV_HEADS)]]


def _attn_items():
    return [(s, h, slice(s * ATTN_BLOCK, (s + 1) * ATTN_BLOCK), slice(h * ATTN_HEAD_DIM, (h + 1) * ATTN_HEAD_DIM))
            for s in range(2) for h in range(ATTN_HEADS)]


def _attn_fwd(pa, cos, sin, sinks_vec):
    t = pa.shape[0]
    ns = t // ATTN_STEP

    def body(q_ref, kc_ref, kp_ref, vc_ref, vp_ref, cc_ref, sc_ref, cp_ref, sp_ref, sk_ref, o_ref):
        cc, sc = cc_ref[...], sc_ref[...]
        q = _rope(q_ref[...], jnp.tile(cc, (1, ATTN_Q // ATTN_KV)), jnp.tile(sc, (1, ATTN_Q // ATTN_KV)))
        kc = _rope(kc_ref[...], cc, sc)
        kp = _rope(kp_ref[...], cp_ref[...], sp_ref[...])
        sk = sk_ref[...]
        valids = [_attn_valid(pl.program_id(0) == 0), _attn_valid(False)]
        kwins = _attn_windows(kp, kc)
        vwins_t = [[v.T for v in vs] for vs in _attn_windows(vp_ref[...], vc_ref[...])]
        items = _attn_items()
        scores = [_dot(kwins[s][h // ATTN_GROUPS], q[rows, hs], NT) for s, h, rows, hs in items]
        probs = [_attn_probs(st, _lane_scalar(sk, h), valids[s])[0] for (s, h, rows, hs), st in zip(items, scores)]
        for (s, h, rows, hs), pt in zip(items, probs):
            o_ref[rows, hs] = _dot(vwins_t[s][h // ATTN_GROUPS], pt).T.astype(o_ref.dtype)

    return pl.pallas_call(
        body, name="attn_fwd", grid=(ns,),
        in_specs=_attn_specs(ns),
        out_specs=pl.BlockSpec((ATTN_STEP, ATTN_Q), lambda i: (i, 0)),
        out_shape=jax.ShapeDtypeStruct((t, ATTN_Q), MXU_DTYPE),
        compiler_params=_params("parallel"),
    )(pa, pa, pa, pa, pa, cos, sin, cos, sin, sinks_vec)


def _attn_bwd(pa, cos, sin, sinks_vec, dao):
    t = pa.shape[0]
    ns = t // ATTN_STEP
    lo, hi = slice(0, ATTN_BLOCK), slice(ATTN_BLOCK, ATTN_STEP)

    def body(q_ref, kc_ref, kp_ref, vc_ref, vp_ref, cc_ref, sc_ref, cp_ref, sp_ref, sk_ref, do_ref,
             dq_ref, dk_ref, dv_ref, acc_ref, dqr_ref, dkw_ref, dvw_ref, ck_ref, cv_ref):
        i = pl.program_id(0)

        @pl.when(i == 0)
        def _():
            acc_ref[...] = jnp.zeros_like(acc_ref)
            ck_ref[...] = jnp.zeros_like(ck_ref)
            cv_ref[...] = jnp.zeros_like(cv_ref)

        @pl.when(i < ns)
        def _():
            cc, sc = cc_ref[...], sc_ref[...]
            cq, sq = jnp.tile(cc, (1, ATTN_Q // ATTN_KV)), jnp.tile(sc, (1, ATTN_Q // ATTN_KV))
            q = _rope(q_ref[...], cq, sq)
            kc = _rope(kc_ref[...], cc, sc)
            kp = _rope(kp_ref[...], cp_ref[...], sp_ref[...])
            sk = sk_ref[...]
            do = do_ref[...]
            lane = lax.broadcasted_iota(jnp.int32, (1, 128), 1)
            dsink = jnp.zeros((1, 128), F32)
            valids = [_attn_valid(i == 0), _attn_valid(False)]
            kwins = _attn_windows(kp, kc)
            vwins = _attn_windows(vp_ref[...], vc_ref[...])
            kwins_t = [[kw.T for kw in kws] for kws in kwins]
            items = _attn_items()
            scores = [_dot(kwins[s][h // ATTN_GROUPS], q[rows, hs], NT) for s, h, rows, hs in items]
            dps = [_dot(vwins[s][h // ATTN_GROUPS], do[rows, hs], NT) for s, h, rows, hs in items]
            pts, dsts = {}, {}
            for (s, h, rows, hs), st, dp_t in zip(items, scores, dps):
                probs_t, psink = _attn_probs(st, _lane_scalar(sk, h), valids[s])
                delta = jnp.sum(probs_t * dp_t, axis=0, keepdims=True)
                pts[s, h] = probs_t
                dsts[s, h] = probs_t * (dp_t - delta) * ATTN_SCALE
                dsink += jnp.where(lane == h, jnp.sum(-psink * delta, axis=1, keepdims=True), 0.0)
            for s, h, rows, hs in items:
                dqr_ref[rows, hs] = _dot(kwins_t[s][h // ATTN_GROUPS], dsts[s, h]).T
            for s in range(2):
                rows = slice(s * ATTN_BLOCK, (s + 1) * ATTN_BLOCK)
                for hk in range(ATTN_KV_HEADS):
                    ks = slice(hk * ATTN_HEAD_DIM, (hk + 1) * ATTN_HEAD_DIM)
                    group = range(hk * ATTN_GROUPS, (hk + 1) * ATTN_GROUPS)
                    heads = [slice(h * ATTN_HEAD_DIM, (h + 1) * ATTN_HEAD_DIM) for h in group]
                    ds_g = jnp.concatenate([dsts[s, h] for h in group], axis=1)
                    p_g = jnp.concatenate([pts[s, h] for h in group], axis=1)
                    q_g = jnp.concatenate([q[rows, hs] for hs in heads], axis=0)
                    do_g = jnp.concatenate([do[rows, hs] for hs in heads], axis=0)
                    dkw_ref[s, :, ks] = _dot(ds_g, q_g)
                    dvw_ref[s, :, ks] = _dot(p_g, do_g)
            acc_ref[0:1, :] += dsink
            dq_ref[...] = _rope_bwd(dqr_ref[...], cq, sq).astype(dq_ref.dtype)
            dk_ref[lo, :] = ck_ref[lo, :].astype(dk_ref.dtype)
            dk_ref[hi, :] = (ck_ref[hi, :] + _rope_bwd(dkw_ref[0, lo, :], cp_ref[...], sp_ref[...])).astype(dk_ref.dtype)
            dv_ref[lo, :] = cv_ref[lo, :].astype(dv_ref.dtype)
            dv_ref[hi, :] = (cv_ref[hi, :] + dvw_ref[0, lo, :]).astype(dv_ref.dtype)
            ck_ref[lo, :] = _rope_bwd(dkw_ref[0, hi, :] + dkw_ref[1, lo, :], cc[lo, :], sc[lo, :])
            ck_ref[hi, :] = _rope_bwd(dkw_ref[1, hi, :], cc[hi, :], sc[hi, :])
            cv_ref[lo, :] = dvw_ref[0, hi, :] + dvw_ref[1, lo, :]
            cv_ref[hi, :] = dvw_ref[1, hi, :]

        @pl.when(i == ns)
        def _():
            dk_ref[...] = ck_ref[...].astype(dk_ref.dtype)
            dv_ref[...] = cv_ref[...].astype(dv_ref.dtype)

    prev_out = lambda w: pl.BlockSpec((ATTN_STEP, w), lambda i: (jnp.maximum(i - 1, 0), 0))
    return pl.pallas_call(
        body, name="attn_bwd", grid=(ns + 1,),
        in_specs=_attn_specs(ns) + [pl.BlockSpec((ATTN_STEP, ATTN_Q), lambda i: (jnp.minimum(i, ns - 1), 0))],
        out_specs=[pl.BlockSpec((ATTN_STEP, ATTN_Q), lambda i: (jnp.minimum(i, ns - 1), 0)), prev_out(ATTN_KV),
                   prev_out(ATTN_KV), _full((8, 128))],
        out_shape=[jax.ShapeDtypeStruct((t, ATTN_Q), MXU_DTYPE), jax.ShapeDtypeStruct((t, ATTN_KV), MXU_DTYPE),
                   jax.ShapeDtypeStruct((t, ATTN_KV), MXU_DTYPE), jax.ShapeDtypeStruct((8, 128), F32)],
        scratch_shapes=[pltpu.VMEM((ATTN_STEP, ATTN_Q), F32), pltpu.VMEM((2, ATTN_STEP, ATTN_KV), F32),
                        pltpu.VMEM((2, ATTN_STEP, ATTN_KV), F32), pltpu.VMEM((ATTN_STEP, ATTN_KV), F32),
                        pltpu.VMEM((ATTN_STEP, ATTN_KV), F32)],
        compiler_params=_params("arbitrary"),
    )(pa, pa, pa, pa, pa, cos, sin, cos, sin, sinks_vec, dao)


PAIR = 2 * DN_CHUNK
INTRA_PAIRS = 4
SCAN_PAIRS = 4
HALO = 8


def _conv_window(cur_ref, prev_ref, xs_ref, tm, has_prev):
    prev = jnp.where(has_prev, prev_ref[...], 0.0)
    xs_ref[0:HALO, :] = prev
    xs_ref[HALO:HALO + tm, :] = cur_ref[...]


def _conv_taps(xs_ref, cw_ref, tm):
    y = cw_ref[0:1, :] * xs_ref[pl.ds(HALO - DN_CONV + 1, tm), :]
    for j in range(1, DN_CONV):
        y += cw_ref[j:j + 1, :] * xs_ref[pl.ds(HALO - DN_CONV + 1 + j, tm), :]
    return y


def _gate_values(ba, al, dt):
    beta = _sigmoid(ba)
    pre = ba + dt
    g = -jnp.exp(al) * _softplus(pre)
    return beta, g, pre


def _dn_prep_specs(tm, tile):
    return [pl.BlockSpec((tm, CONV_CH), lambda i: (tile(i), 0)),
            pl.BlockSpec((HALO, CONV_CH), lambda i: (jnp.maximum(tile(i) * (tm // HALO) - 1, 0), 0)),
            pl.BlockSpec((tm, 128), lambda i: (tile(i), 4 * DN_W // 128)),
            _full((DN_CONV, CONV_CH)), _full((1, 128)), _full((1, 128))]


def _dn_prep(pd, conv_w, al_vec, dt_vec, tm):
    t = pd.shape[0]

    def body(cur_ref, prev_ref, ba_ref, cw_ref, al_ref, dt_ref, qn_ref, kn_ref, vc_ref, gc_ref, gr_ref, xs_ref):
        _conv_window(cur_ref, prev_ref, xs_ref, tm, pl.program_id(0) > 0)
        y = _conv_taps(xs_ref, cw_ref, tm)
        c = y * _sigmoid(y)
        for h in range(DN_HEADS):
            qs = slice(h * DN_HEAD_DIM, (h + 1) * DN_HEAD_DIM)
            ksl = slice(DN_W + h * DN_HEAD_DIM, DN_W + (h + 1) * DN_HEAD_DIM)
            qh, kh = c[:, qs], c[:, ksl]
            qn_ref[:, qs] = qh * lax.rsqrt(jnp.sum(qh * qh, axis=-1, keepdims=True) + EPS) * DN_SCALE
            kn_ref[:, qs] = kh * lax.rsqrt(jnp.sum(kh * kh, axis=-1, keepdims=True) + EPS)
        vc_ref[...] = c[:, 2 * DN_W:3 * DN_W]
        beta, g, _ = _gate_values(ba_ref[...], al_ref[...], dt_ref[...])
        lane = lax.broadcasted_iota(jnp.int32, beta.shape, 1)
        gb = jnp.where(lane < DN_HEADS, beta, jnp.where(lane < 2 * DN_HEADS, g, 0.0))
        gc_ref[...] = gb
        gr_ref[...] = gb.T[0:8, :]

    tok = lambda w: pl.BlockSpec((tm, w), lambda i: (i, 0))
    return pl.pallas_call(
        body, name="dn_prep", grid=(t // tm,),
        in_specs=_dn_prep_specs(tm, lambda i: i),
        out_specs=[tok(DN_W), tok(DN_W), tok(DN_W), tok(128), pl.BlockSpec((8, tm), lambda i: (0, i))],
        out_shape=[jax.ShapeDtypeStruct((t, DN_W), F32)] * 3 + [jax.ShapeDtypeStruct((t, 128), F32),
                                                                 jax.ShapeDtypeStruct((8, t), F32)],
        scratch_shapes=[pltpu.VMEM((HALO + tm, CONV_CH), F32)],
        compiler_params=_params("parallel"),
    )(pd, pd, pd, conv_w, al_vec, dt_vec)


def _pair_masks():
    r = lax.broadcasted_iota(jnp.int32, (PAIR, PAIR), 0)
    c = lax.broadcasted_iota(jnp.int32, (PAIR, PAIR), 1)
    same = (r < DN_CHUNK) == (c < DN_CHUNK)
    return same & (r >= c), same & (r > c)


def _lane_col(mat, idx):
    lane = lax.broadcasted_iota(jnp.int32, mat.shape, 1)
    return jnp.sum(jnp.where(lane == idx, mat, 0.0), axis=-1, keepdims=True)


def _pair_cumsums(gc, gr, low):
    lowf = low.astype(F32)
    return _dot(lowf, gc, NN, HI), _dot(gr, lowf, NT, HI)


def _pair_gates(gc, cum_c, cum_r, low, h):
    beta = _lane_col(gc, h)
    gam = _lane_col(cum_c, DN_HEADS + h)
    gam_row = cum_r[DN_HEADS + h:DN_HEADS + h + 1, :]
    dm = jnp.where(low, jnp.exp(jnp.where(low, gam - gam_row, 0.0)), 0.0)
    row = lax.broadcasted_iota(jnp.int32, gam.shape, 0)
    gl = jnp.where(row < DN_CHUNK, gam[DN_CHUNK - 1:DN_CHUNK, :], gam[PAIR - 1:PAIR, :])
    return beta, gam, dm, gl


def _split(a):
    hi = a.astype(BF16)
    return hi, (a - hi.astype(F32)).astype(BF16)


def _dot_split(a, b, dims=NN):
    (ah, al), (bh, bl) = a, b
    la, lb = (1, 1) if dims == TN else ((0, 1) if dims == NN else (0, 0))
    r = _dot(jnp.concatenate([ah, al], axis=la), jnp.concatenate([bh, bl], axis=lb), dims)
    m, n = r.shape[0] // 2, r.shape[1] // 2
    return (r[m:, n:] + (r[:m, n:] + r[m:, :n])) + r[:m, :n]


def _unit_lower_inverses(lmats):
    n = lmats[0].shape[0]
    r = lax.broadcasted_iota(jnp.int32, (n, n), 0)
    c = lax.broadcasted_iota(jnp.int32, (n, n), 1)
    same = lambda size: (r & ~(size - 1)) == (c & ~(size - 1))
    base = DN_CHUNK // 4
    diag = [jnp.where(same(base), l, 0.0) for l in lmats]
    accs = [(r == c).astype(F32) - d for d in diag]
    splits = [_split(d) for d in diag]
    step = 1
    while 2 * step < base:
        splits = [_split(_dot_split(s, s)) for s in splits]
        accs = [acc + _dot_split(_split(acc), s) for acc, s in zip(accs, splits)]
        step *= 2
    size = base
    while size < DN_CHUNK:
        below = same(2 * size) & jnp.logical_not(same(size))
        tb = [_dot(acc, jnp.where(below, l, 0.0)) for acc, l in zip(accs, lmats)]
        accs = [acc - _dot(t, acc) for acc, t in zip(accs, tb)]
        size *= 2
    return accs


def _dn_intra(qn, kn, vc, gc, gr):
    t = qn.shape[0]
    npair = t // PAIR
    rows_step = INTRA_PAIRS * PAIR

    def body(q_ref, k_ref, v_ref, gc_ref, gr_ref, u_ref, w_ref, qg_ref, kd_ref, a_ref, ti_ref, dl_ref):
        low, strict = _pair_masks()
        items = []
        for p in range(INTRA_PAIRS):
            rows = slice(p * PAIR, (p + 1) * PAIR)
            gc_v = gc_ref[rows, :]
            cum_c, cum_r = _pair_cumsums(gc_v, gr_ref[:, rows], low)
            for h in range(DN_HEADS):
                hs = slice(h * DN_HEAD_DIM, (h + 1) * DN_HEAD_DIM)
                items.append((p, h, rows, hs, _pair_gates(gc_v, cum_c, cum_r, low, h)))
        lmats = []
        for p, h, rows, hs, (beta, gam, dm, gl) in items:
            k = k_ref[rows, hs]
            lmats.append(jnp.where(strict, _dot(k * beta, k, NT) * dm, 0.0))
        tinvs = _unit_lower_inverses(lmats)
        for (p, h, rows, hs, (beta, gam, dm, gl)), tinv in zip(items, tinvs):
            q, k, v = q_ref[rows, hs], k_ref[rows, hs], v_ref[rows, hs]
            eg = jnp.exp(gam)
            u_ref[rows, hs] = _dot(tinv, v * beta)
            w_ref[rows, hs] = _dot(tinv, (k * beta) * eg).astype(w_ref.dtype)
            a_ref[h, rows, :] = _dot(q, k, NT) * dm
            ti_ref[h, rows, :] = tinv
            qg_ref[rows, hs] = (q * eg).astype(qg_ref.dtype)
            kd_ref[rows, hs] = (k * jnp.exp(gl - gam)).astype(kd_ref.dtype)
            for c in range(2):
                last = (c + 1) * DN_CHUNK - 1
                dl_ref[2 * p + c, h] = jnp.broadcast_to(jnp.exp(gam[last:last + 1, :]), (8, 128))

    tok = lambda w: pl.BlockSpec((rows_step, w), lambda n: (n, 0))
    hm = pl.BlockSpec((DN_HEADS, rows_step, PAIR), lambda n: (0, n, 0))
    return pl.pallas_call(
        body, name="dn_intra", grid=(npair // INTRA_PAIRS,),
        in_specs=[tok(DN_W), tok(DN_W), tok(DN_W), tok(128), pl.BlockSpec((8, rows_step), lambda n: (0, n))],
        out_specs=[tok(DN_W)] * 4 + [hm, hm, pl.BlockSpec((2 * INTRA_PAIRS, DN_HEADS, 8, 128), lambda n: (n, 0, 0, 0))],
        out_shape=[jax.ShapeDtypeStruct((t, DN_W), F32)] + [jax.ShapeDtypeStruct((t, DN_W), MXU_DTYPE)] * 3
                  + [jax.ShapeDtypeStruct((DN_HEADS, t, PAIR), F32)] * 2
                  + [jax.ShapeDtypeStruct((2 * npair, DN_HEADS, 8, 128), F32)],
        compiler_params=_params("parallel"),
    )(qn, kn, vc, gc, gr)


def _dn_scan_fwd(u, w, qg, kd, a_qk, dlast, pd, dn_w):
    t = u.shape[0]
    npair = t // PAIR

    def body(u_ref, w_ref, qg_ref, kd_ref, a_ref, dl_ref, z_ref, nw_ref, out_ref, o_ref, vn_ref, sall_ref, s_ref):
        @pl.when(pl.program_id(0) == 0)
        def _():
            s_ref[...] = jnp.zeros_like(s_ref)

        nw = nw_ref[...]
        for c in range(2 * SCAN_PAIRS):
            rows = slice(c * DN_CHUNK, (c + 1) * DN_CHUNK)
            diag = slice((c % 2) * DN_CHUNK, (c % 2 + 1) * DN_CHUNK)
            for h in range(DN_HEADS):
                hs = slice(h * DN_HEAD_DIM, (h + 1) * DN_HEAD_DIM)
                st = s_ref[h]
                sall_ref[c, h] = st
                vn_ref[rows, hs] = (u_ref[rows, hs] - _dot(w_ref[rows, hs], st)).astype(vn_ref.dtype)
            for h in range(DN_HEADS):
                hs = slice(h * DN_HEAD_DIM, (h + 1) * DN_HEAD_DIM)
                st, vn = s_ref[h], vn_ref[rows, hs]
                o = _dot(qg_ref[rows, hs], st) + _dot(a_ref[h, rows, diag], vn)
                s_ref[h] = st * dl_ref[c, h][0:1, :] + _dot(kd_ref[rows, hs], vn, TN)
                o_ref[rows, hs] = o
                z = z_ref[rows, hs]
                on = o * lax.rsqrt(jnp.mean(o * o, axis=-1, keepdims=True) + EPS) * nw
                out_ref[rows, hs] = (on * (z * _sigmoid(z))).astype(out_ref.dtype)

    rows_step = SCAN_PAIRS * PAIR
    tok = pl.BlockSpec((rows_step, DN_W), lambda n: (n, 0))
    hm = pl.BlockSpec((DN_HEADS, rows_step, PAIR), lambda n: (0, n, 0))
    return pl.pallas_call(
        body, name="dn_scan_fwd", grid=(npair // SCAN_PAIRS,),
        in_specs=[tok, tok, tok, tok, hm, pl.BlockSpec((2 * SCAN_PAIRS, DN_HEADS, 8, 128), lambda n: (n, 0, 0, 0)),
                  pl.BlockSpec((rows_step, DN_W), lambda n: (n, 3)), _full((1, 128))],
        out_specs=[tok, tok, tok,
                   pl.BlockSpec((2 * SCAN_PAIRS, DN_HEADS, DN_HEAD_DIM, DN_HEAD_DIM), lambda n: (n, 0, 0, 0))],
        out_shape=[jax.ShapeDtypeStruct((t, DN_W), MXU_DTYPE), jax.ShapeDtypeStruct((t, DN_W), F32),
                   jax.ShapeDtypeStruct((t, DN_W), MXU_DTYPE),
                   jax.ShapeDtypeStruct((2 * npair, DN_HEADS, DN_HEAD_DIM, DN_HEAD_DIM), F32)],
        scratch_shapes=[pltpu.VMEM((DN_HEADS, DN_HEAD_DIM, DN_HEAD_DIM), F32)],
        compiler_params=_params("arbitrary"),
    )(u, w, qg, kd, a_qk, dlast, pd, dn_w)


def _dn_scan_bwd(dout, o, vnew, sall, w, qg, kd, a_qk, dlast, pd, dn_w, dep):
    t = o.shape[0]
    npair = t // PAIR
    nstep = npair // SCAN_PAIRS
    rev = lambda n: nstep - 1 - n

    def body(do_ref, o_ref, vn_ref, sall_ref, w_ref, qg_ref, kd_ref, a_ref, dl_ref, z_ref, nw_ref, dep_ref,
             dz_ref, du_ref, dw_ref, dqg_ref, dkd_ref, da_ref, ddl_ref, acc_ref, ds_ref, dos_ref):
        @pl.when(pl.program_id(0) == 0)
        def _():
            ds_ref[...] = jnp.zeros_like(ds_ref)
            acc_ref[...] = jnp.zeros_like(acc_ref)

        nw = nw_ref[...]
        dnw = jnp.zeros((1, 128), F32)
        for h in range(DN_HEADS):
            hs = slice(h * DN_HEAD_DIM, (h + 1) * DN_HEAD_DIM)
            o, z, dout = o_ref[:, hs], z_ref[:, hs], do_ref[:, hs]
            r = lax.rsqrt(jnp.mean(o * o, axis=-1, keepdims=True) + EPS)
            oh = o * r
            sz = _sigmoid(z)
            dz_ref[:, hs] = dout * (oh * nw) * (sz + z * sz * (1.0 - sz))
            don = dout * (z * sz)
            dnw += jnp.sum(don * oh, axis=0, keepdims=True)
            doh = don * nw
            dos_ref[:, hs] = r * (doh - oh * jnp.mean(doh * oh, axis=-1, keepdims=True))
        acc_ref[0:1, :] += dnw
        for c in reversed(range(2 * SCAN_PAIRS)):
            rows = slice(c * DN_CHUNK, (c + 1) * DN_CHUNK)
            diag = slice((c % 2) * DN_CHUNK, (c % 2 + 1) * DN_CHUNK)
            other = slice((1 - c % 2) * DN_CHUNK, (2 - c % 2) * DN_CHUNK)
            for h in range(DN_HEADS):
                hs = slice(h * DN_HEAD_DIM, (h + 1) * DN_HEAD_DIM)
                do, st, dsp, vn = dos_ref[rows, hs], sall_ref[c, h], ds_ref[h], vn_ref[rows, hs]
                da_ref[h, rows, diag] = _dot(do, vn, NT)
                da_ref[h, rows, other] = jnp.zeros((DN_CHUNK, DN_CHUNK), F32)
                du_ref[rows, hs] = (_dot(a_ref[h, rows, diag], do, TN) + _dot(kd_ref[rows, hs], dsp)).astype(du_ref.dtype)
                dqg_ref[rows, hs] = _dot(do, st, NT)
                dkd_ref[rows, hs] = _dot(vn, dsp, NT)
                ddl = jnp.sum(jnp.sum(dsp * st, axis=1, keepdims=True), axis=0, keepdims=True)
                ddl_ref[c, h] = jnp.broadcast_to(ddl, (8, 128))
            for h in range(DN_HEADS):
                hs = slice(h * DN_HEAD_DIM, (h + 1) * DN_HEAD_DIM)
                do, st, dvn = dos_ref[rows, hs], sall_ref[c, h], du_ref[rows, hs]
                dw_ref[rows, hs] = (-_dot(dvn, st, NT)).astype(dw_ref.dtype)
                ds_ref[h] = (ds_ref[h] * dl_ref[c, h][0:1, :] + _dot(qg_ref[rows, hs], do, TN)
                             - _dot(w_ref[rows, hs], dvn, TN))

    rows_step = SCAN_PAIRS * PAIR
    tok = pl.BlockSpec((rows_step, DN_W), lambda n: (rev(n), 0))
    hm = pl.BlockSpec((DN_HEADS, rows_step, PAIR), lambda n: (0, rev(n), 0))
    sc = pl.BlockSpec((2 * SCAN_PAIRS, DN_HEADS, 8, 128), lambda n: (rev(n), 0, 0, 0))
    return pl.pallas_call(
        body, name="dn_scan_bwd", grid=(nstep,),
        in_specs=[tok, tok, tok,
                  pl.BlockSpec((2 * SCAN_PAIRS, DN_HEADS, DN_HEAD_DIM, DN_HEAD_DIM), lambda n: (rev(n), 0, 0, 0)),
                  tok, tok, tok, hm, sc, pl.BlockSpec((rows_step, DN_W), lambda n: (rev(n), 3)), _full((1, 128)),
                  pl.BlockSpec(memory_space=pl.ANY)],
        out_specs=[tok] * 5 + [hm, sc, _full((8, 128))],
        out_shape=[jax.ShapeDtypeStruct((t, DN_W), F32)] + [jax.ShapeDtypeStruct((t, DN_W), MXU_DTYPE)] * 2
                  + [jax.ShapeDtypeStruct((t, DN_W), F32)] * 2 + [jax.ShapeDtypeStruct((DN_HEADS, t, PAIR), F32),
                   jax.ShapeDtypeStruct((2 * npair, DN_HEADS, 8, 128), F32), jax.ShapeDtypeStruct((8, 128), F32)],
        scratch_shapes=[pltpu.VMEM((DN_HEADS, DN_HEAD_DIM, DN_HEAD_DIM), F32), pltpu.VMEM((SCAN_PAIRS * PAIR, DN_W), F32)],
        compiler_params=_params("arbitrary"),
    )(dout, o, vnew, sall, w, qg, kd, a_qk, dlast, pd, dn_w, dep)


def _dn_intra_bwd(qn, kn, vc, gc, gr, tinv, a_qk, du, dw, dqg, dkd, da_qk, ddlast, dlast, dep):
    t = qn.shape[0]
    npair = t // PAIR

    def body(q_ref, k_ref, v_ref, gc_ref, gr_ref, ti_ref, a_ref, du_ref, dw_ref, dqg_ref, dkd_ref, da_ref, ddl_ref, dl_ref,
             dep_ref, dq_ref, dk_ref, dv_ref, dg_ref):
        low, strict = _pair_masks()
        lane = lax.broadcasted_iota(jnp.int32, (PAIR, 128), 1)
        rowi = lax.broadcasted_iota(jnp.int32, (PAIR, 1), 0)
        rsum = lambda v: jnp.sum(v, axis=-1, keepdims=True)
        items = []
        for p in range(INTRA_PAIRS):
            rows = slice(p * PAIR, (p + 1) * PAIR)
            gc_v = gc_ref[rows, :]
            cum_c, cum_r = _pair_cumsums(gc_v, gr_ref[:, rows], low)
            for h in range(DN_HEADS):
                hs = slice(h * DN_HEAD_DIM, (h + 1) * DN_HEAD_DIM)
                items.append((p, h, rows, hs, _pair_gates(gc_v, cum_c, cum_r, low, h)))
        dtis, lmats, dvbs, dkbgs = [], [], [], []
        for p, h, rows, hs, (beta, gam, dm, gl) in items:
            k, tinv = k_ref[rows, hs], ti_ref[h, rows, :]
            kb = k * beta
            dtis.append(_dot(du_ref[rows, hs], v_ref[rows, hs] * beta, NT)
                        + _dot(dw_ref[rows, hs], kb * jnp.exp(gam), NT))
            lmats.append(jnp.where(strict, _dot(kb, k, NT) * dm, 0.0))
            dvbs.append(_dot(tinv, du_ref[rows, hs], TN))
            dkbgs.append(_dot(tinv, dw_ref[rows, hs], TN))
        xs = [_dot(ti_ref[h, rows, :], dti, TN) for (p, h, rows, hs, g), dti in zip(items, dtis)]
        dls = [jnp.where(strict, -_dot(x, ti_ref[h, rows, :], NT), 0.0) for (p, h, rows, hs, g), x in zip(items, xs)]
        dgam_all = [jnp.zeros((PAIR, 128), F32) for _ in range(INTRA_PAIRS)]
        dbeta_all = [jnp.zeros((PAIR, 128), F32) for _ in range(INTRA_PAIRS)]
        for (p, h, rows, hs, (beta, gam, dm, gl)), dl, lmat, dvb, dkbg in zip(items, dls, lmats, dvbs, dkbgs):
            q, k, v = q_ref[rows, hs], k_ref[rows, hs], v_ref[rows, hs]
            a = a_ref[h, rows, :]
            dqg, dkd = dqg_ref[rows, hs], dkd_ref[rows, hs]
            kb = k * beta
            eg = jnp.exp(gam)
            ekd = jnp.exp(gl - gam)
            dmm = dl * dm
            dam = jnp.where(low, da_ref[h, rows, :], 0.0)
            dn = dam * dm
            e = dl * lmat + dam * a
            dkb = _dot(dmm, k) + dkbg * eg
            dk_ref[rows, hs] = _dot(dmm, kb, TN) + _dot(dn, q, TN) + dkd * ekd + dkb * beta
            dq_ref[rows, hs] = _dot(dn, k) + dqg * eg
            dv_ref[rows, hs] = dvb * beta
            t_kd = rsum(dkd * (k * ekd))
            dgam = rsum(e) - rsum(e.T) + rsum(dqg * (q * eg)) + rsum(dkbg * (kb * eg)) - t_kd
            for c in range(2):
                crows = slice(c * DN_CHUNK, (c + 1) * DN_CHUNK)
                dgl = (jnp.sum(t_kd[crows, :], axis=0, keepdims=True)
                       + ddl_ref[2 * p + c, h][0:1, 0:1] * dl_ref[2 * p + c, h][0:1, 0:1])
                dgam = dgam + jnp.where(rowi == (c + 1) * DN_CHUNK - 1, dgl, 0.0)
            dgam_all[p] += jnp.where(lane == DN_HEADS + h, dgam, 0.0)
            dbeta_all[p] += jnp.where(lane == h, rsum(dkb * k) + rsum(dvb * v), 0.0)
        for p in range(INTRA_PAIRS):
            dg_ref[p * PAIR:(p + 1) * PAIR, :] = dbeta_all[p] + _dot(low.astype(F32), dgam_all[p], TN, HI)

    rows_step = INTRA_PAIRS * PAIR
    tok = lambda w: pl.BlockSpec((rows_step, w), lambda n: (n, 0))
    hm = pl.BlockSpec((DN_HEADS, rows_step, PAIR), lambda n: (0, n, 0))
    sc = pl.BlockSpec((2 * INTRA_PAIRS, DN_HEADS, 8, 128), lambda n: (n, 0, 0, 0))
    return pl.pallas_call(
        body, name="dn_intra_bwd", grid=(npair // INTRA_PAIRS,),
        in_specs=[tok(DN_W), tok(DN_W), tok(DN_W), tok(128), pl.BlockSpec((8, rows_step), lambda n: (0, n)), hm, hm,
                  tok(DN_W), tok(DN_W), tok(DN_W), tok(DN_W), hm, sc, sc, pl.BlockSpec(memory_space=pl.ANY)],
        out_specs=[tok(DN_W), tok(DN_W), tok(DN_W), tok(128)],
        out_shape=[jax.ShapeDtypeStruct((t, DN_W), F32)] * 3 + [jax.ShapeDtypeStruct((t, 128), F32)],
        compiler_params=_params("parallel"),
    )(qn, kn, vc, gc, gr, tinv, a_qk, du, dw, dqg, dkd, da_qk, ddlast, dlast, dep)


def _dn_prep_bwd(pd, conv_w, al_vec, dt_vec, dqn, dkn, dvc, dgc, dz, tm):
    t = pd.shape[0]
    nt = t // tm
    tile = lambda i: nt - 1 - i

    def body(cur_ref, prev_ref, ba_ref, cw_ref, al_ref, dt_ref, dq_ref, dk_ref, dv_ref, dg_ref, dz_ref,
             o_ref, accw_ref, accg_ref, xs_ref, dc_ref, ds_ref, carry_ref):
        @pl.when(pl.program_id(0) == 0)
        def _():
            accw_ref[...] = jnp.zeros_like(accw_ref)
            accg_ref[...] = jnp.zeros_like(accg_ref)
            carry_ref[...] = jnp.zeros_like(carry_ref)

        _conv_window(cur_ref, prev_ref, xs_ref, tm, tile(pl.program_id(0)) > 0)
        taps = [xs_ref[pl.ds(HALO - DN_CONV + 1 + j, tm), :] for j in range(DN_CONV)]
        y = cw_ref[0:1, :] * taps[0]
        for j in range(1, DN_CONV):
            y += cw_ref[j:j + 1, :] * taps[j]
        sg = _sigmoid(y)
        c = y * sg
        for h in range(DN_HEADS):
            qs = slice(h * DN_HEAD_DIM, (h + 1) * DN_HEAD_DIM)
            ksl = slice(DN_W + h * DN_HEAD_DIM, DN_W + (h + 1) * DN_HEAD_DIM)
            for src, sl, scale in ((dq_ref, qs, DN_SCALE), (dk_ref, ksl, 1.0)):
                xh = c[:, sl]
                r = lax.rsqrt(jnp.sum(xh * xh, axis=-1, keepdims=True) + EPS)
                unit = xh * r
                dn = src[:, qs] * scale
                dc_ref[:, sl] = r * (dn - unit * jnp.sum(dn * unit, axis=-1, keepdims=True))
        dc_ref[:, 2 * DN_W:3 * DN_W] = dv_ref[...]
        dy = dc_ref[...] * (sg + y * sg * (1.0 - sg))
        for j in range(DN_CONV):
            accw_ref[j:j + 1, :] += jnp.sum(dy * taps[j], axis=0, keepdims=True)
        ds_ref[0:tm, :] = dy
        ds_ref[tm:tm + HALO, :] = carry_ref[...]
        carry_ref[...] = ds_ref[0:HALO, :]
        dx = cw_ref[0:1, :] * ds_ref[pl.ds(DN_CONV - 1, tm), :]
        for j in range(1, DN_CONV):
            dx += cw_ref[j:j + 1, :] * ds_ref[pl.ds(DN_CONV - 1 - j, tm), :]

        beta, g, pre = _gate_values(ba_ref[...], al_ref[...], dt_ref[...])
        dgb = dg_ref[...]
        lane = lax.broadcasted_iota(jnp.int32, dgb.shape, 1)
        is_b, is_a = lane < DN_HEADS, (lane >= DN_HEADS) & (lane < 2 * DN_HEADS)
        dpre = dgb * (-jnp.exp(al_ref[...])) * _sigmoid(pre)
        dba = jnp.where(is_b, dgb * beta * (1.0 - beta), jnp.where(is_a, dpre, 0.0))
        accg_ref[0:1, :] += jnp.sum(jnp.where(is_a, dgb * g, 0.0), axis=0, keepdims=True)
        accg_ref[1:2, :] += jnp.sum(jnp.where(is_a, dpre, 0.0), axis=0, keepdims=True)
        o_ref[:, 0:CONV_CH] = dx.astype(o_ref.dtype)
        o_ref[:, CONV_CH:CONV_CH + DN_W] = dz_ref[...].astype(o_ref.dtype)
        o_ref[:, CONV_CH + DN_W:DN_COLS] = dba.astype(o_ref.dtype)

    tok = lambda w: pl.BlockSpec((tm, w), lambda i: (tile(i), 0))
    return pl.pallas_call(
        body, name="dn_prep_bwd", grid=(nt,),
        in_specs=_dn_prep_specs(tm, tile) + [tok(DN_W), tok(DN_W), tok(DN_W), tok(128), tok(DN_W)],
        out_specs=[tok(DN_COLS), _full((8, CONV_CH)), _full((8, 128))],
        out_shape=[jax.ShapeDtypeStruct((t, DN_COLS), MXU_DTYPE),
                   jax.ShapeDtypeStruct((8, CONV_CH), F32), jax.ShapeDtypeStruct((8, 128), F32)],
        scratch_shapes=[pltpu.VMEM((HALO + tm, CONV_CH), F32), pltpu.VMEM((tm, CONV_CH), F32),
                        pltpu.VMEM((tm + HALO, CONV_CH), F32), pltpu.VMEM((HALO, CONV_CH), F32)],
        compiler_params=_params("arbitrary"),
    )(pd, pd, pd, conv_w, al_vec, dt_vec, dqn, dkn, dvc, dgc, dz)


def _pad_lanes(v, offset=0):
    return jnp.pad(v.astype(F32), (offset, 128 - offset - v.shape[0]))[None]


class _LocalReducer:
    def start(self, grads):
        return jnp.zeros((8, 128), F32)

    def middle(self, after):
        return jnp.zeros((8, 128), F32)

    def finish(self, after):
        return None


def _local_step(x, p, tgt, sm, w, late, reducer):
    t = x.shape[0]
    tm = min(512, t // 2)
    tm_s = min(512, t // 2)
    tw = min(1024, t // 2)
    tw_ff = min(2048, t // 2)

    attn_cols = ATTN_Q + 2 * ATTN_KV
    w_in_t = w["w_in_t"]
    w_in_t = jnp.pad(w_in_t, ((0, max(0, attn_cols + DN_COLS - w_in_t.shape[0])), (0, 0)))
    wa_t = w_in_t[:attn_cols]
    wd_t = w_in_t[attn_cols:attn_cols + DN_COLS]
    conv_w = w["conv_w"]
    al_vec, dt_vec = _pad_lanes(sm["a_log"], DN_HEADS), _pad_lanes(sm["dt_bias"], DN_HEADS)
    sinks_vec = _pad_lanes(sm["sinks"])
    dn_w = sm["dn_norm"].reshape(1, 128)
    row = lambda v: v.reshape(1, D_MODEL)
    cos, sin = _rope_tables(t)

    u, pa, pd = _inproj(x, row(sm["norm_mix"]), wa_t, wd_t, tm_s)
    ao = _attn_fwd(pa, cos, sin, sinks_vec)
    qn, kn, vc, gc, gr = _dn_prep(pd, conv_w, al_vec, dt_vec, tm_s)
    uu, ww, qg, kd, a_qk, tinv, dlast = _dn_intra(qn, kn, vc, gc, gr)
    dn_out, o, vnew, sall = _dn_scan_fwd(uu, ww, qg, kd, a_qk, dlast, pd, dn_w)
    w_o, late_rest = late(dn_out)
    wo_a, wo_d = w_o[:ATTN_Q], w_o[ATTN_Q:]
    h1 = _oproj(x, ao, dn_out, wo_a, wo_d, tm)
    w = dict(w, **late_rest(h1))
    w_proj = jnp.transpose(w["w_proj4"], (1, 0, 2)).reshape(PLE_DIM, D_MODEL)
    m, r, h2 = _mlp_fwd(h1, row(sm["norm_mlp"]), w["w_up4"], w["w_down"], tw)
    dh2, dh2b, dgp, dpp, n3, pb, acc_ple = _ple_loss(h2, p, tgt, row(sm["norm_ple"]), row(sm["norm_final"]),
                                                     w["w_gate"], w_proj, tm_s)
    g_w_gate = _wgrad(n3, dgp, "wgrad_gate", D_MODEL, D_MODEL, tw)
    g_w_proj = _wgrad(pb, dpp, "wgrad_proj", PLE_DIM, D_MODEL, tw)
    da, dh1, dh1b, dao, ddn, acc_mlp = _mlp_bwd(dh2, dh2b, r, h1, row(sm["norm_mlp"]), w["w_up4"], w["w_down"],
                                                wo_a, wo_d, tm)
    g_w_up4 = _wgrad(m, da, "wgrad_up", D_MODEL, FF_BLOCK, tw_ff, stacked=True)
    g_w_down = _wgrad(r, dh2b, "wgrad_down", FF_BLOCK, D_MODEL, tw_ff,
                      prep=lambda rv: jnp.square(rv.astype(F32)).astype(MXU_DTYPE))
    g_w_o = _wgrad_cat([ao, dn_out], [dh1b], "wgrad_o", tw)
    early = dict(w_up4=g_w_up4, w_down=g_w_down, w_gate=g_w_gate, w_proj=g_w_proj, w_o=g_w_o)
    dep = reducer.start(early)
    dz, du, dw, dqg, dkd, da_qk, ddlast, acc_dn = _dn_scan_bwd(ddn, o, vnew, sall, ww, qg, kd, a_qk, dlast, pd, dn_w,
                                                               dep)
    dep = reducer.middle(du)
    dqn, dkn, dvc, dgc = _dn_intra_bwd(qn, kn, vc, gc, gr, tinv, a_qk, du, dw, dqg, dkd, da_qk, ddlast, dlast, dep)
    d_dn, acc_conv, acc_gate = _dn_prep_bwd(pd, conv_w, al_vec, dt_vec, dqn, dkn, dvc, dgc, dz, tm_s)
    dq, dk, dv, acc_attn = _attn_bwd(pa, cos, sin, sinks_vec, dao)
    reducer.finish(dq)
    wq_t, wk_t, wv_t = wa_t[:ATTN_Q], wa_t[ATTN_Q:ATTN_Q + ATTN_KV], wa_t[ATTN_Q + ATTN_KV:]
    dx, acc_mix = _inproj_bwd(x, dh1, row(sm["norm_mix"]), [dq, dk, dv, d_dn], [wq_t, wk_t, wv_t, wd_t], tm_s)

    g_w_in_t = _wgrad_cat([dq, dk, dv, d_dn], [u], "wgrad_in", tw)
    grads = dict(early, w_in_t=g_w_in_t)
    sums = dict(loss=acc_ple[2, 0], norm_final=acc_ple[0], norm_ple=acc_ple[1], norm_mlp=acc_mlp[0], norm_mix=acc_mix[0],
                dn_norm=acc_dn[0], sinks=acc_attn[0, :ATTN_HEADS], a_log=acc_gate[0, DN_HEADS:2 * DN_HEADS],
                dt_bias=acc_gate[1, DN_HEADS:2 * DN_HEADS], conv_w=acc_conv[:DN_CONV])
    return sums, dx, grads


MESH = pl.DeviceIdType.MESH
ANY = pl.BlockSpec(memory_space=pl.ANY)
N_CHIPS = 4
N_DEV = 8


def _place():
    x, y, c = lax.axis_index("x"), lax.axis_index("y"), lax.axis_index("c")
    chips = [(1 - x, y), (x, 1 - y), (1 - x, 1 - y)]
    return x, y, c, chips


CAST_ROWS = 256


def _gather_weights(shards, conv_s, casts):
    n, m = len(shards), len(casts)
    per = 7
    cast_cols = max(a.shape[1] for a in casts)
    pieces = [(a, r0) for a, arr in enumerate(casts) for r0 in range(0, arr.shape[0], CAST_ROWS)]
    assert all(arr.shape[0] % CAST_ROWS == 0 and arr.shape[1] % 128 == 0 for arr in casts)

    def body(*refs):
        in_refs, conv_ref, cast_in = refs[:n], refs[n], refs[n + 1:n + 1 + m]
        refs = refs[n + 1 + m:]
        out_refs, conv_out, cast_out = refs[:n], refs[n], refs[n + 1:n + 1 + m]
        send_sems, recv_sems, f32_buf, bf16_buf, cast_sems = refs[n + 1 + m:]
        x, y, c, chips = _place()

        def piece(i, store):
            a, r0 = pieces[i]
            cols, slot = cast_in[a].shape[1], i % 2
            if store:
                return pltpu.make_async_copy(bf16_buf.at[slot, :, pl.ds(0, cols)],
                                             cast_out[a].at[pl.ds(r0, CAST_ROWS), :], cast_sems.at[2 + slot])
            return pltpu.make_async_copy(cast_in[a].at[pl.ds(r0, CAST_ROWS), :],
                                         f32_buf.at[slot, :, pl.ds(0, cols)], cast_sems.at[slot])

        def cast_all():
            piece(0, False).start()
            for i in range(len(pieces)):
                if i + 1 < len(pieces):
                    piece(i + 1, False).start()
                piece(i, False).wait()
                if i >= 2:
                    piece(i - 2, True).wait()
                cols = cast_in[pieces[i][0]].shape[1]
                bf16_buf[i % 2, :, pl.ds(0, cols)] = f32_buf[i % 2, :, pl.ds(0, cols)].astype(BF16)
                piece(i, True).start()
            for i in range(max(0, len(pieces) - 2), len(pieces)):
                piece(i, True).wait()

        sibling = (x, y, 1 - c)

        def blk(a, px, py, pc):
            hr = in_refs[a].shape[0] // 2
            return out_refs[a].at[2 * px + py, pl.ds(pc * hr, hr), :]

        def mine(a):
            hr = in_refs[a].shape[0] // 2
            return in_refs[a].at[pl.ds(c * hr, hr), :]

        def rcopy(a, k, block, to, src=None):
            return pltpu.make_async_remote_copy(
                src_ref=blk(a, *block) if src is None else src, dst_ref=blk(a, *block),
                send_sem=send_sems.at[per * a + k], recv_sem=recv_sems.at[per * a + k],
                device_id=to, device_id_type=MESH)

        def whole(a, to):
            return pltpu.make_async_remote_copy(
                src_ref=in_refs[a], dst_ref=out_refs[a].at[2 * x + y],
                send_sem=send_sems.at[per * a], recv_sem=recv_sems.at[per * a], device_id=to, device_id_type=MESH)

        def ccopy(j, to):
            return pltpu.make_async_remote_copy(
                src_ref=conv_ref, dst_ref=conv_out.at[2 * x + y],
                send_sem=send_sems.at[per * n + j], recv_sem=recv_sems.at[per * n + j],
                device_id=to, device_id_type=MESH)

        started = []
        for a in range(n):
            first = [whole(a, sibling)]
            first += [rcopy(a, 1 + j, (x, y, c), (*chip, c), src=mine(a)) for j, chip in enumerate(chips)]
            for cp in first:
                cp.start()
            started += first
        conv_sends = [ccopy(j, (*chip, c)) for j, chip in enumerate(chips)] + [ccopy(3, sibling)]
        for cp in conv_sends:
            cp.start()
        started += conv_sends
        cast_all()
        for a in range(n):
            for j, chip in enumerate(chips):
                rcopy(a, 1 + j, (*chip, c), (x, y, c)).wait_recv()
                fwd = rcopy(a, 4 + j, (*chip, c), sibling)
                fwd.start()
                started.append(fwd)
        for a in range(n):
            whole(a, sibling).wait_recv()
            for j, chip in enumerate(chips):
                rcopy(a, 4 + j, (*chip, 1 - c), (x, y, c)).wait_recv()
        for j, chip in enumerate(chips + [(x, y)]):
            pltpu.make_async_remote_copy(
                src_ref=conv_ref, dst_ref=conv_out.at[2 * chip[0] + chip[1]],
                send_sem=send_sems.at[per * n + j], recv_sem=recv_sems.at[per * n + j],
                device_id=sibling, device_id_type=MESH).wait_recv()
        for cp in started:
            cp.wait_send()

    nsem = per * n + 4
    out_shape = [jax.ShapeDtypeStruct((N_CHIPS,) + s.shape, s.dtype) for s in shards]
    out_shape.append(jax.ShapeDtypeStruct((N_CHIPS,) + conv_s.shape, conv_s.dtype))
    out_shape += [jax.ShapeDtypeStruct(a.shape, BF16) for a in casts]
    res = pl.pallas_call(
        body, name="gather_weights", in_specs=[ANY] * (n + 1 + m), out_specs=[ANY] * (n + 1 + m), out_shape=out_shape,
        scratch_shapes=[pltpu.SemaphoreType.DMA((nsem,)), pltpu.SemaphoreType.DMA((nsem,)),
                        pltpu.VMEM((2, CAST_ROWS, cast_cols), F32), pltpu.VMEM((2, CAST_ROWS, cast_cols), BF16),
                        pltpu.SemaphoreType.DMA((4,))],
    )(*shards, conv_s, *casts)
    return res[:n], res[n], res[n + 1:]


HBM = pl.BlockSpec(memory_space=pltpu.HBM)
SEM = pl.BlockSpec(memory_space=pltpu.SEMAPHORE)
EFFECT = pltpu.SideEffectType.DATAFLOW_SIDE_EFFECTING
LATE_COPIES = 7


def _late_copies(in_refs, land_refs, send_sems, recv_sems, only=None):
    x, y, c, chips = _place()
    sends, arrivals = [], []
    for a, (src, land) in enumerate(zip(in_refs, land_refs)):
        if only is not None and a not in only:
            continue
        hr = src.shape[0] // 2
        base = LATE_COPIES * a

        def cp(src_ref, dst_ref, s_idx, r_idx, to):
            return pltpu.make_async_remote_copy(src_ref=src_ref, dst_ref=dst_ref, send_sem=send_sems.at[base + s_idx],
                                                recv_sem=recv_sems.at[base + r_idx], device_id=to, device_id_type=MESH)

        sends.append(cp(src, land.at[2 * x + y], 0, 0, (x, y, 1 - c)))
        arrivals.append(cp(src, land.at[2 * x + y], 0, 0, (x, y, 1 - c)))
        for j, chip in enumerate(chips):
            for pc in range(2):
                half = src.at[pl.ds(c * hr, hr), :]
                sends.append(cp(half, land.at[2 * x + y, pl.ds(c * hr, hr), :], 1 + 2 * j + pc, 1 + 2 * j + c, (*chip, pc)))
                arrivals.append(cp(half, land.at[2 * chip[0] + chip[1], pl.ds(pc * hr, hr), :], 1 + 2 * j + pc,
                                   1 + 2 * j + pc, (*chip, pc)))
    return sends, arrivals


def _copies_start(name, build, nsem, srcs, land_shapes, after):
    n = len(srcs)

    def body(*refs):
        sends, _ = build(refs[:n], refs[n:2 * n], refs[2 * n + 1], refs[2 * n + 2])
        for cp in sends:
            cp.start()
        refs[-1][...] = jnp.zeros_like(refs[-1])

    lands = [pltpu.with_memory_space_constraint(lax.empty(s.shape, s.dtype), pltpu.HBM) for s in land_shapes]
    ins = [pltpu.with_memory_space_constraint(s, pltpu.HBM) for s in srcs]
    out = pl.pallas_call(
        body, name=name,
        out_shape=(pltpu.SemaphoreType.DMA((nsem,)), pltpu.SemaphoreType.DMA((nsem,)),
                   *[pltpu.HBM(s.shape, s.dtype) for s in srcs], *[pltpu.HBM(s.shape, s.dtype) for s in land_shapes],
                   jax.ShapeDtypeStruct((8, 128), F32)),
        in_specs=[HBM] * (2 * n) + [ANY],
        out_specs=(SEM, SEM, *[HBM] * (2 * n), pl.BlockSpec(memory_space=pltpu.VMEM)),
        input_output_aliases={i: 2 + i for i in range(2 * n)},
        compiler_params=pltpu.CompilerParams(has_side_effects=EFFECT),
    )(*ins, *lands, after)
    return out[0], out[1], out[2:2 + n], out[2 + n:2 + 2 * n], out[-1]


def _copies_wait(name, build, started, after):
    send_sems, recv_sems, srcs, lands, _ = started
    n = len(srcs)

    def body(*refs):
        sends, arrivals = build(refs[:n], refs[n:2 * n], refs[2 * n], refs[2 * n + 1])
        for cp in sends:
            cp.wait_send()
        for cp in arrivals:
            cp.wait_recv()

    out = pl.pallas_call(
        body, name=name,
        out_shape=(*[pltpu.HBM(s.shape, s.dtype) for s in srcs], *[pltpu.HBM(l.shape, l.dtype) for l in lands]),
        in_specs=[HBM] * (2 * n) + [SEM, SEM, ANY],
        out_specs=tuple([HBM] * (2 * n)),
        input_output_aliases={i: i for i in range(2 * n)},
        compiler_params=pltpu.CompilerParams(has_side_effects=EFFECT),
    )(*srcs, *lands, send_sems, recv_sems, after)
    return out[:n], out[n:]


def _exchange_copies(g_refs, got_refs, send_sems, recv_sems):
    x, y, c, _ = _place()
    sends, arrivals = [], []
    for a, (g, got) in enumerate(zip(g_refs, got_refs)):
        hr = g.shape[1] // 2
        cp = pltpu.make_async_remote_copy(
            src_ref=g.at[:, pl.ds((1 - c) * hr, hr), :], dst_ref=got, send_sem=send_sems.at[a],
            recv_sem=recv_sems.at[a], device_id=(x, y, 1 - c), device_id_type=MESH)
        sends.append(cp)
        arrivals.append(cp)
    return sends, arrivals


def _scatter_copies(s_refs, got_refs, send_sems, recv_sems):
    x, y, c, chips = _place()
    sends, arrivals = [], []
    for a, (s16, got) in enumerate(zip(s_refs, got_refs)):
        for j, chip in enumerate(chips):
            cp = pltpu.make_async_remote_copy(
                src_ref=s16.at[2 * chip[0] + chip[1]], dst_ref=got.at[j], send_sem=send_sems.at[3 * a + j],
                recv_sem=recv_sems.at[3 * a + j], device_id=(*chip, c), device_id_type=MESH)
            sends.append(cp)
            arrivals.append(cp)
    return sends, arrivals


def _share_halves(name, bufs, dep):
    n = len(bufs)

    def body(*refs):
        out_refs = refs[n + 1:2 * n + 1]
        send_sems, recv_sems = refs[2 * n + 1:]
        x, y, c, _ = _place()
        remote = [pltpu.make_async_remote_copy(
            src_ref=out_refs[a].at[c], dst_ref=out_refs[a].at[c], send_sem=send_sems.at[a], recv_sem=recv_sems.at[a],
            device_id=(x, y, 1 - c), device_id_type=MESH) for a in range(n)]
        for cp in remote:
            cp.start()
        for a in range(n):
            pltpu.make_async_remote_copy(
                src_ref=out_refs[a].at[c], dst_ref=out_refs[a].at[1 - c], send_sem=send_sems.at[a],
                recv_sem=recv_sems.at[a], device_id=(x, y, 1 - c), device_id_type=MESH).wait_recv()
        for cp in remote:
            cp.wait_send()

    return pl.pallas_call(
        body, name=name, in_specs=[ANY] * (n + 1), out_specs=[ANY] * n,
        out_shape=[jax.ShapeDtypeStruct(b.shape, b.dtype) for b in bufs],
        input_output_aliases={a: a for a in range(n)},
        scratch_shapes=[pltpu.SemaphoreType.DMA((n,)), pltpu.SemaphoreType.DMA((n,))],
    )(*bufs, dep)


SMALL_ROWS, SMALL_COLS = 16, CONV_CH
DN_NORM_LANE = 128


def _allreduce_small(block):
    m_per, ncol = block.shape

    def body(x_ref, sum_ref, all_ref, send_sems, recv_sems, local_sem):
        x, y, c, chips = _place()
        me, sibling = (x, y, c), (x, y, 1 - c)

        def rows(px, py, pc):
            return all_ref.at[pl.ds((4 * px + 2 * py + pc) * m_per, m_per), :]

        def copy(k, block_of, to, src=None):
            return pltpu.make_async_remote_copy(
                src_ref=rows(*block_of) if src is None else src, dst_ref=rows(*block_of),
                send_sem=send_sems.at[k], recv_sem=recv_sems.at[k], device_id=to, device_id_type=MESH)

        mine = pltpu.make_async_copy(x_ref, rows(*me), local_sem)
        mine.start()
        first = [copy(0, me, sibling, src=x_ref)]
        first += [copy(1 + j, me, (*chip, c), src=x_ref) for j, chip in enumerate(chips)]
        for cp in first:
            cp.start()
        passed = [copy(4 + j, (*chip, c), sibling) for j, chip in enumerate(chips)]
        for j, chip in enumerate(chips):
            copy(1 + j, (*chip, c), me).wait_recv()
            passed[j].start()
        copy(0, sibling, me).wait_recv()
        for j, chip in enumerate(chips):
            copy(4 + j, (*chip, 1 - c), me).wait_recv()
        for cp in first + passed:
            cp.wait_send()
        mine.wait()
        total = all_ref[0:m_per, :]
        for d in range(1, N_DEV):
            total = total + all_ref[d * m_per:(d + 1) * m_per, :]
        sum_ref[...] = total

    vm = pl.BlockSpec(memory_space=pltpu.VMEM)
    return pl.pallas_call(
        body, name="allreduce_small", in_specs=[vm], out_specs=vm,
        out_shape=jax.ShapeDtypeStruct((m_per, ncol), F32),
        scratch_shapes=[pltpu.VMEM((N_DEV * m_per, ncol), F32), pltpu.SemaphoreType.DMA((7,)),
                        pltpu.SemaphoreType.DMA((7,)), pltpu.SemaphoreType.DMA],
    )(block)


def _row_tile(rows, cols):
    tile = rows
    while tile * cols * 4 > (1 << 20) and tile % 16 == 0:
        tile //= 2
    return tile


def _elementwise(fn, name, ins, out_dtypes, dep):
    rows, cols = ins[0].shape
    tile = _row_tile(rows, cols)

    def body(*refs):
        outs = fn(*[r[...] for r in refs[:len(ins)]])
        for o_ref, o in zip(refs[len(ins) + 1:], outs):
            o_ref[...] = o.astype(o_ref.dtype)

    if tile * cols * 4 > (1 << 21) and cols % 512 == 0:
        spec = pl.BlockSpec((rows, 256), lambda i: (0, i))
        steps = cols // 256
    else:
        spec = pl.BlockSpec((tile, cols), lambda i: (i, 0))
        steps = rows // tile
    return pl.pallas_call(
        body, name=name, grid=(steps,), in_specs=[spec] * len(ins) + [pl.BlockSpec(memory_space=pl.ANY)],
        out_specs=[spec] * len(out_dtypes),
        out_shape=[jax.ShapeDtypeStruct((rows, cols), d) for d in out_dtypes],
        compiler_params=_params("parallel"),
    )(*ins, dep)


def _adamw_tile(w, g, m, v):
    m = ADAM_B1 * m + (1.0 - ADAM_B1) * g
    v = ADAM_B2 * v + (1.0 - ADAM_B2) * jnp.square(g)
    m_hat = m / (1.0 - ADAM_B1 ** ADAM_STEP)
    v_hat = v / (1.0 - ADAM_B2 ** ADAM_STEP)
    delta = -ADAM_LR * (m_hat / (jnp.sqrt(v_hat) + ADAM_EPS) + ADAM_WD * w)
    return delta, m, v


def _adamw(name, w, g, m, v, dep):
    return _elementwise(_adamw_tile, name, [w, g, m, v], [F32, F32, F32], dep)


def _chip_sum(name, g4, got, place):
    nchip, hr, cols = got.shape
    tile = _row_tile(hr, cols)
    nblk = hr // tile

    def body(pl_ref, g_ref, o_ref, s32_ref, s16_ref):
        s = g_ref[...] + o_ref[...]
        s16_ref[...] = s.astype(BF16)

        @pl.when(pl.program_id(1) == pl_ref[0])
        def _():
            s32_ref[...] = s

    spec = pl.BlockSpec((None, tile, cols), lambda i, k, pr: (k, i, 0))
    return pl.pallas_call(
        body, name=name,
        grid_spec=pltpu.PrefetchScalarGridSpec(
            num_scalar_prefetch=1, grid=(nblk, nchip),
            in_specs=[pl.BlockSpec((None, tile, cols), lambda i, k, pr: (k, pr[1] * nblk + i, 0)), spec],
            out_specs=[pl.BlockSpec((tile, cols), lambda i, k, pr: (i, 0)), spec]),
        out_shape=[jax.ShapeDtypeStruct((hr, cols), F32), jax.ShapeDtypeStruct(got.shape, BF16)],
        compiler_params=_params("parallel", "arbitrary"),
    )(place, g4, got)


def _mesh_sum(name, s32, got, place):
    hr, cols = s32.shape
    tile = _row_tile(hr, cols)

    def body(pl_ref, own_ref, g0_ref, g1_ref, g2_ref, o_ref):
        o_ref[...] = ((own_ref[...] + g0_ref[...].astype(F32)) + g1_ref[...].astype(F32)) + g2_ref[...].astype(F32)

    slab = lambda j: pl.BlockSpec((None, tile, cols), lambda i, pr: (j, i, 0))
    return pl.pallas_call(
        body, name=name,
        grid_spec=pltpu.PrefetchScalarGridSpec(
            num_scalar_prefetch=1, grid=(hr // tile,),
            in_specs=[pl.BlockSpec((tile, cols), lambda i, pr: (i, 0)), slab(0), slab(1), slab(2)],
            out_specs=pl.BlockSpec((None, tile, cols), lambda i, pr: (pr[1], i, 0))),
        out_shape=jax.ShapeDtypeStruct((2, hr, cols), F32),
        compiler_params=_params("parallel"),
    )(place, s32, got, got, got)


def _place_operand():
    return jnp.stack([2 * lax.axis_index("x") + lax.axis_index("y"), lax.axis_index("c")]).astype(jnp.int32)


W_IN_ROWS = 720
W_IN_GATHER_ROWS = 736
BF16_TILE_ROWS = 16


def _join_w_in(blocks):
    rows, t = D_IN // N_CHIPS, BF16_TILE_ROWS
    first = [rows * k // t * t for k in range(N_CHIPS)]
    parts = []
    for k in range(N_CHIPS):
        lo = t if k else 0
        if k + 1 < N_CHIPS:
            hi = first[k + 1] - first[k]
            assert rows * (k + 1) <= first[k + 1] + t and hi + t <= W_IN_GATHER_ROWS
            parts += [blocks[k, lo:hi], blocks[k, hi:hi + t] + blocks[k + 1, :t]]
        else:
            parts.append(blocks[k, lo:])
    return jnp.concatenate(parts, axis=0)


def _per_chip(name, g):
    if name == "w_in_t":
        rows = D_IN // N_CHIPS
        return jnp.stack([lax.slice_in_dim(g, rows * k, rows * k + W_IN_ROWS) for k in range(N_CHIPS)])
    if name == "w_proj":
        return jnp.transpose(g.reshape(PLE_DIM, N_CHIPS, D_MODEL // N_CHIPS), (1, 0, 2))
    if name == "w_up4":
        return g
    return g.reshape(N_CHIPS, g.shape[0] // N_CHIPS, g.shape[1])


class _EarlyReducer:
    def __init__(self, tag):
        self.tag = tag

    def start(self, grads):
        self.names = list(grads)
        self.place = _place_operand()
        slabs = [_per_chip(k, grads[k]) for k in self.names]
        halves = [jax.ShapeDtypeStruct((s.shape[0], s.shape[1] // 2, s.shape[2]), F32) for s in slabs]
        self.a = _copies_start(self.tag + "exchange_start", _exchange_copies, len(slabs), slabs, halves,
                               slabs[0][0, :8, :128])
        return self.a[-1]

    def middle(self, after):
        slabs, got = _copies_wait(self.tag + "exchange_wait", _exchange_copies, self.a, after)
        self.sums = [_chip_sum(self.tag + "chip_sum_" + k, s, g, self.place) for k, s, g in zip(self.names, slabs, got)]
        s16 = [s[1] for s in self.sums]
        lands = [jax.ShapeDtypeStruct((3,) + s.shape[1:], BF16) for s in s16]
        self.b = _copies_start(self.tag + "scatter_start", _scatter_copies, 3 * len(s16), s16, lands,
                               self.sums[0][0][:8, :128])
        return self.b[-1]

    def finish(self, after):
        _, got = _copies_wait(self.tag + "scatter_wait", _scatter_copies, self.b, after)
        self.bufs = {k: _mesh_sum(self.tag + "mesh_sum_" + k, s[0], g, self.place)
                     for k, s, g in zip(self.names, self.sums, got)}


def kernel(x, p, norm_mix, w_in, conv_w, a_log, dt_bias, dn_norm, sinks, w_o, norm_mlp, w_up, w_down, norm_ple, w_ple_gate, w_ple_proj, norm_final, loss_target, m_norm_mix, m_w_in, m_conv_w, m_a_log, m_dt_bias, m_dn_norm, m_sinks, m_w_o, m_norm_mlp, m_w_up, m_w_down, m_norm_ple, m_w_ple_gate, m_w_ple_proj, m_norm_final, v_norm_mix, v_w_in, v_conv_w, v_a_log, v_dt_bias, v_dn_norm, v_sinks, v_w_o, v_norm_mlp, v_w_up, v_w_down, v_norm_ple, v_w_ple_gate, v_w_ple_proj, v_norm_final):
    chip = 2 * lax.axis_index("x") + lax.axis_index("y")
    big = dict(w_in=w_in[0], w_o=w_o[0], w_up=w_up[0], w_down=w_down[0], w_gate=w_ple_gate[0], w_proj=w_ple_proj[0])
    big_m = dict(w_in=m_w_in[0], w_o=m_w_o[0], w_up=m_w_up[0], w_down=m_w_down[0], w_gate=m_w_ple_gate[0], w_proj=m_w_ple_proj[0])
    big_v = dict(w_in=v_w_in[0], w_o=v_w_o[0], w_up=v_w_up[0], w_down=v_w_down[0], w_gate=v_w_ple_gate[0], w_proj=v_w_ple_proj[0])
    names = list(big)

    rows_in = D_IN // N_CHIPS
    chip_index = 2 * lax.axis_index("x") + lax.axis_index("y")
    w_in_shard_t = lax.dynamic_update_slice(jnp.zeros((W_IN_GATHER_ROWS, D_MODEL), BF16), big["w_in"].T.astype(BF16),
                                            ((rows_in * chip_index) % BF16_TILE_ROWS, 0))
    late_names = names[1:]
    (w_in_all,), conv_all, late_shards = _gather_weights([w_in_shard_t], conv_w[0], [big[k] for k in late_names])
    gather = _copies_start("gather_start", _late_copies, LATE_COPIES * len(late_shards), late_shards,
                           [jax.ShapeDtypeStruct((N_CHIPS,) + s.shape, BF16) for s in late_shards], w_in_all)
    token = gather[-1]
    w = dict(w_in_t=_join_w_in(w_in_all),
             conv_w=jnp.transpose(conv_all, (1, 0, 2)).reshape(DN_CONV, CONV_CH))
    sm = dict(norm_mix=norm_mix[0] + token[0, 0], a_log=a_log[0], dt_bias=dt_bias[0], dn_norm=dn_norm[0],
              sinks=sinks[0], norm_mlp=norm_mlp[0], norm_ple=norm_ple[0], norm_final=norm_final)

    def late(after):
        first = functools.partial(_late_copies, only=(0,))
        srcs, lands = _copies_wait("gather_wait_o", first, gather, after)

        def rest(after2):
            others = functools.partial(_late_copies, only=tuple(range(1, len(late_names))))
            gw = dict(zip(late_names, _copies_wait("gather_wait_rest", others, gather[:2] + (srcs, lands, None), after2)[1]))
            return dict(w_up4=gw["w_up"], w_down=gw["w_down"].reshape(D_FF, D_MODEL),
                        w_gate=gw["w_gate"].reshape(D_MODEL, D_MODEL), w_proj4=gw["w_proj"])

        return lands[0].reshape(D_MODEL, D_MODEL), rest

    reducer = _EarlyReducer("early_")
    sums, grad_x, g = _local_step(x[0], p[0, 0], loss_target[0], sm, w, late, reducer)

    last = _EarlyReducer("last_")
    dep_a = last.start({"w_in_t": g["w_in_t"]})

    row = lambda v: jnp.pad(v, (0, SMALL_COLS - v.shape[0]))

    def misc_row(al, dtb, sk, dnn, rest):
        head = jnp.concatenate([al, dtb, sk])
        return row(jnp.concatenate([head, jnp.zeros((DN_NORM_LANE - head.shape[0],), F32), dnn, rest]))

    misc = misc_row(sums["a_log"], sums["dt_bias"], sums["sinks"], sums["dn_norm"], sums["loss"].reshape(1))
    small = jnp.concatenate([sums["conv_w"], jnp.stack([row(sums["norm_mix"]), row(sums["norm_mlp"]), row(sums["norm_ple"]),
                                                        row(sums["norm_final"]), misc]),
                             jnp.zeros((SMALL_ROWS - 9, SMALL_COLS), F32)], axis=0)
    tot = _allreduce_small(small + dep_a[0, 0])
    dep_b = last.middle(tot)
    grad_key = dict(w_o="w_o", w_up="w_up4", w_down="w_down", w_gate="w_gate", w_proj="w_proj")
    full = _share_halves("share_halves", [reducer.bufs[grad_key[k]] for k in late_names], dep_b)
    red = {k: f.reshape(-1, f.shape[-1]) for k, f in zip(late_names, full)}
    loss = tot[8, 256]
    ncw = CONV_CH // N_CHIPS

    def pack(cw, nmix, nmlp, nple, nfin, al, dtb, sk, dnn):
        misc_p = misc_row(al, dtb, sk, dnn, jnp.zeros((0,), F32))
        cw_p = jnp.pad(cw, ((0, 0), (0, SMALL_COLS - ncw)))
        return jnp.concatenate([cw_p, jnp.stack([row(nmix), row(nmlp), row(nple), row(nfin), misc_p]),
                                jnp.zeros((SMALL_ROWS - 9, SMALL_COLS), F32)], axis=0)

    def unpack(buf):
        return dict(conv_w=buf[0:4, :ncw][None], norm_mix=buf[4, :D_MODEL][None], norm_mlp=buf[5, :D_MODEL][None],
                    norm_ple=buf[6, :D_MODEL][None], norm_final=buf[7, :D_MODEL], a_log=buf[8, 0:4][None],
                    dt_bias=buf[8, 4:8][None], sinks=buf[8, 8:16][None], dn_norm=buf[8, 128:256][None])

    g_conv_shard = lax.dynamic_slice(tot[0:4], (0, chip * ncw), (DN_CONV, ncw))
    g_small = pack(g_conv_shard, tot[4, :D_MODEL], tot[5, :D_MODEL], tot[6, :D_MODEL], tot[7, :D_MODEL],
                   tot[8, 0:4], tot[8, 4:8], tot[8, 8:16], tot[8, 128:256])
    w_small = pack(conv_w[0], norm_mix[0], norm_mlp[0], norm_ple[0], norm_final, a_log[0], dt_bias[0], sinks[0], dn_norm[0])
    m_small = pack(m_conv_w[0], m_norm_mix[0], m_norm_mlp[0], m_norm_ple[0], m_norm_final, m_a_log[0], m_dt_bias[0],
                   m_sinks[0], m_dn_norm[0])
    v_small = pack(v_conv_w[0], v_norm_mix[0], v_norm_mlp[0], v_norm_ple[0], v_norm_final, v_a_log[0], v_dt_bias[0],
                   v_sinks[0], v_dn_norm[0])

    ref_name = dict(w_in="w_in", w_o="w_o", w_up="w_up", w_down="w_down", w_gate="w_ple_gate", w_proj="w_ple_proj")
    out_g, out_d, out_m, out_v = {}, {}, {}, {}

    def update(k, dep):
        d_k, m_k, v_k = _adamw("adamw_" + k, big[k], red[k], big_m[k], big_v[k], dep)
        out_g[ref_name[k]], out_d[ref_name[k]] = red[k][None], d_k[None]
        out_m[ref_name[k]], out_v[ref_name[k]] = m_k[None], v_k[None]
        return d_k

    for k in late_names:
        done = update(k, dep_b)
    small_out = _adamw("adamw_small", w_small, g_small, m_small, v_small, dep_b)
    d_s, m_s, v_s = (unpack(b) for b in small_out)
    g_s = unpack(g_small)
    for src, dst in ((g_s, out_g), (d_s, out_d), (m_s, out_m), (v_s, out_v)):
        dst.update(src)
    last.finish(done + small_out[0][0:1, 0:1])
    (w_in_full,) = _share_halves("share_halves_w_in", [last.bufs["w_in_t"]], dep_b)
    g_t = w_in_full.reshape(W_IN_ROWS, D_MODEL)[:D_IN // N_CHIPS]
    d_t, m_t, v_t = _adamw("adamw_w_in", big["w_in"].T, g_t, big_m["w_in"].T, big_v["w_in"].T, dep_b)
    out_g["w_in"], out_d["w_in"], out_m["w_in"], out_v["w_in"] = g_t.T[None], d_t.T[None], m_t.T[None], v_t.T[None]
    order = ["norm_mix", "w_in", "conv_w", "a_log", "dt_bias", "dn_norm", "sinks", "w_o", "norm_mlp", "w_up", "w_down",
             "norm_ple", "w_ple_gate", "w_ple_proj", "norm_final"]
    return (loss, grad_x[None], *[out_g[k] for k in order], *[out_d[k] for k in order],
            *[out_m[k] for k in order], *[out_v[k] for k in order])
```

```python
import functools

import jax
import jax.numpy as jnp
from jax import lax
from jax.experimental import pallas as pl
from jax.experimental.pallas import tpu as pltpu

F32 = jnp.float32
BF16 = jnp.bfloat16
MXU_DTYPE = jnp.bfloat16
HI = lax.Precision.HIGHEST

D_MODEL = 1024
PLE_DIM = 256
ATTN_HEADS = 8
ATTN_KV_HEADS = 2
ATTN_GROUPS = ATTN_HEADS // ATTN_KV_HEADS
ATTN_HEAD_DIM = 64
ATTN_BLOCK = 128
ROPE_THETA = 10000.0
DN_HEADS = 4
DN_HEAD_DIM = 128
DN_CONV = 4
DN_CHUNK = 64
D_FF = 4 * D_MODEL
EPS = 1e-6
ATTN_Q = ATTN_HEADS * ATTN_HEAD_DIM
ATTN_KV = ATTN_KV_HEADS * ATTN_HEAD_DIM
DN_W = DN_HEADS * DN_HEAD_DIM
CONV_CH = 3 * DN_W
D_IN = ATTN_Q + 2 * ATTN_KV + 4 * DN_W + 2 * DN_HEADS
DN_COLS = 4 * DN_W + 128
DN_SCALE = DN_HEAD_DIM ** -0.5
ATTN_SCALE = ATTN_HEAD_DIM ** -0.5
FF_BLOCKS = 4
FF_BLOCK = D_FF // FF_BLOCKS

ADAM_LR = 0.001
ADAM_B1 = 0.9
ADAM_B2 = 0.999
ADAM_EPS = 1e-08
ADAM_WD = 0.01
ADAM_STEP = 10

V7X_VMEM_BYTES = 64 * 1024 * 1024
VMEM_LIMIT = 48 * 1024 * 1024

NN = ((1,), (0,))
NT = ((1,), (1,))
TN = ((0,), (0,))


def _dot(a, b, dims=NN, prec=None):
    if a.dtype != b.dtype:
        a, b = a.astype(MXU_DTYPE), b.astype(MXU_DTYPE)
    return lax.dot_general(a, b, (dims, ((), ())), precision=prec, preferred_element_type=F32)


def _sigmoid(x):
    return 1.0 / (1.0 + jnp.exp(-x))


def _softplus(x):
    return jnp.maximum(x, 0.0) + jnp.log(1.0 + jnp.exp(-jnp.abs(x)))


def _params(*sem):
    return pltpu.CompilerParams(dimension_semantics=sem, vmem_limit_bytes=VMEM_LIMIT)


def _rms_fwd(xv, g):
    r = lax.rsqrt(jnp.mean(xv * xv, axis=-1, keepdims=True) + EPS)
    return xv * r * g


def _rms_bwd(xv, g, dn):
    r = lax.rsqrt(jnp.mean(xv * xv, axis=-1, keepdims=True) + EPS)
    xh = xv * r
    dg = jnp.sum(dn * xh, axis=0, keepdims=True)
    dxh = dn * g
    dx = r * (dxh - xh * jnp.mean(dxh * xh, axis=-1, keepdims=True))
    return dx, dg


def _full(shape):
    return pl.BlockSpec(shape, lambda *_: (0,) * len(shape))


def _inproj(x, g_mix, wa_t, wd_t, tm):
    t = x.shape[0]

    def body(x_ref, g_ref, wa_ref, wd_ref, u_ref, pa_ref, pd_ref):
        u = _rms_fwd(x_ref[...], g_ref[...]).astype(MXU_DTYPE)
        u_ref[...] = u
        pa_ref[...] = _dot(u, wa_ref[...], NT)
        pd_ref[...] = _dot(u, wd_ref[...], NT)

    na, nd = wa_t.shape[0], wd_t.shape[0]
    return pl.pallas_call(
        body, name="inproj", grid=(t // tm,),
        in_specs=[pl.BlockSpec((tm, D_MODEL), lambda i: (i, 0)), _full((1, D_MODEL)),
                  _full((na, D_MODEL)), _full((nd, D_MODEL))],
        out_specs=[pl.BlockSpec((tm, D_MODEL), lambda i: (i, 0)), pl.BlockSpec((tm, na), lambda i: (i, 0)),
                   pl.BlockSpec((tm, nd), lambda i: (i, 0))],
        out_shape=[jax.ShapeDtypeStruct((t, D_MODEL), MXU_DTYPE), jax.ShapeDtypeStruct((t, na), F32),
                   jax.ShapeDtypeStruct((t, nd), F32)],
        compiler_params=_params("parallel"),
    )(x, g_mix, wa_t, wd_t)


def _oproj(x, ao, dn, wo_a, wo_d, tm):
    t = x.shape[0]

    def body(x_ref, ao_ref, dn_ref, wa_ref, wd_ref, h_ref):
        h_ref[...] = (x_ref[...] + _dot(ao_ref[...].astype(MXU_DTYPE), wa_ref[...])
                      + _dot(dn_ref[...].astype(MXU_DTYPE), wd_ref[...]))

    half = ao.shape[1]
    return pl.pallas_call(
        body, name="oproj", grid=(t // tm,),
        in_specs=[pl.BlockSpec((tm, D_MODEL), lambda i: (i, 0)), pl.BlockSpec((tm, half), lambda i: (i, 0)),
                  pl.BlockSpec((tm, half), lambda i: (i, 0)), _full((half, D_MODEL)), _full((half, D_MODEL))],
        out_specs=pl.BlockSpec((tm, D_MODEL), lambda i: (i, 0)),
        out_shape=jax.ShapeDtypeStruct((t, D_MODEL), F32),
        compiler_params=_params("parallel"),
    )(x, ao, dn, wo_a, wo_d)


def _mlp_fwd(h1, g_mlp, w_up4, w_down, tm):
    t = h1.shape[0]

    def body(h_ref, g_ref, wu_ref, wd_ref, m_ref, r_ref, h2_ref, acc_ref):
        k = pl.program_id(1)

        @pl.when(k == 0)
        def _():
            m_ref[...] = _rms_fwd(h_ref[...], g_ref[...]).astype(MXU_DTYPE)
            acc_ref[...] = jnp.zeros_like(acc_ref)

        r = jnp.maximum(_dot(m_ref[...], wu_ref[...]), 0.0)
        r_ref[...] = r.astype(MXU_DTYPE)
        s = jnp.square(r).astype(MXU_DTYPE)
        acc_ref[...] += _dot(s, wd_ref[...])

        @pl.when(k == FF_BLOCKS - 1)
        def _():
            h2_ref[...] = h_ref[...] + acc_ref[...]

    return pl.pallas_call(
        body, name="mlp_fwd", grid=(t // tm, FF_BLOCKS),
        in_specs=[pl.BlockSpec((tm, D_MODEL), lambda i, k: (i, 0)), _full((1, D_MODEL)),
                  pl.BlockSpec((None, D_MODEL, FF_BLOCK), lambda i, k: (k, 0, 0)),
                  pl.BlockSpec((FF_BLOCK, D_MODEL), lambda i, k: (k, 0))],
        out_specs=[pl.BlockSpec((tm, D_MODEL), lambda i, k: (i, 0)), pl.BlockSpec((tm, FF_BLOCK), lambda i, k: (i, k)),
                   pl.BlockSpec((tm, D_MODEL), lambda i, k: (i, 0))],
        out_shape=[jax.ShapeDtypeStruct((t, D_MODEL), MXU_DTYPE), jax.ShapeDtypeStruct((t, D_FF), MXU_DTYPE),
                   jax.ShapeDtypeStruct((t, D_MODEL), F32)],
        scratch_shapes=[pltpu.VMEM((tm, D_MODEL), F32)],
        compiler_params=_params("parallel", "arbitrary"),
    )(h1, g_mlp, w_up4, w_down)


def _ple_loss(h2, p, tgt, g_ple, g_fin, w_gate, w_proj, tm):
    t = h2.shape[0]

    def body(h_ref, p_ref, t_ref, gp_ref, gf_ref, wg_ref, wp_ref,
             dh_ref, dhb_ref, dgp_ref, dpp_ref, n3_ref, pb_ref, acc_ref):
        @pl.when(pl.program_id(0) == 0)
        def _():
            acc_ref[...] = jnp.zeros_like(acc_ref)

        h = h_ref[...]
        g_ple_v, g_fin_v = gp_ref[...], gf_ref[...]
        n3 = _rms_fwd(h, g_ple_v).astype(MXU_DTYPE)
        n3_ref[...] = n3
        gate = _sigmoid(_dot(n3, wg_ref[...]))
        pb = p_ref[...].astype(MXU_DTYPE)
        pb_ref[...] = pb
        pp = _dot(pb, wp_ref[...])
        h3 = h + gate * pp
        r4 = lax.rsqrt(jnp.mean(h3 * h3, axis=-1, keepdims=True) + EPS)
        xh4 = h3 * r4
        e = xh4 * g_fin_v - t_ref[...]
        loss = 0.5 * jnp.sum(jnp.mean(e * e, axis=-1, keepdims=True), axis=0, keepdims=True)
        dy = e * (1.0 / D_MODEL)
        dg_fin = jnp.sum(dy * xh4, axis=0, keepdims=True)
        dxh = dy * g_fin_v
        dh3 = r4 * (dxh - xh4 * jnp.mean(dxh * xh4, axis=-1, keepdims=True))
        dpp_ref[...] = (dh3 * gate).astype(MXU_DTYPE)
        dgp = (dh3 * pp * gate * (1.0 - gate)).astype(MXU_DTYPE)
        dgp_ref[...] = dgp
        dn3 = _dot(dgp, wg_ref[...], NT)
        dx, dg_ple = _rms_bwd(h, g_ple_v, dn3)
        dh2 = dh3 + dx
        dh_ref[...] = dh2
        dhb_ref[...] = dh2.astype(MXU_DTYPE)
        acc_ref[0:1, :] += dg_fin
        acc_ref[1:2, :] += dg_ple
        acc_ref[2:3, :] += jnp.broadcast_to(loss, (1, D_MODEL))

    row = lambda w: pl.BlockSpec((tm, w), lambda i: (i, 0))
    return pl.pallas_call(
        body, name="ple_loss", grid=(t // tm,),
        in_specs=[row(D_MODEL), row(PLE_DIM), row(D_MODEL), _full((1, D_MODEL)), _full((1, D_MODEL)),
                  _full((D_MODEL, D_MODEL)), _full((PLE_DIM, D_MODEL))],
        out_specs=[row(D_MODEL), row(D_MODEL), row(D_MODEL), row(D_MODEL), row(D_MODEL), row(PLE_DIM),
                   _full((8, D_MODEL))],
        out_shape=[jax.ShapeDtypeStruct((t, D_MODEL), F32), jax.ShapeDtypeStruct((t, D_MODEL), MXU_DTYPE),
                   jax.ShapeDtypeStruct((t, D_MODEL), MXU_DTYPE), jax.ShapeDtypeStruct((t, D_MODEL), MXU_DTYPE),
                   jax.ShapeDtypeStruct((t, D_MODEL), MXU_DTYPE), jax.ShapeDtypeStruct((t, PLE_DIM), MXU_DTYPE),
                   jax.ShapeDtypeStruct((8, D_MODEL), F32)],
        compiler_params=_params("arbitrary"),
    )(h2, p, tgt, g_ple, g_fin, w_gate, w_proj)


def _mlp_bwd(dh2, dh2b, r, h1, g_mlp, w_up4, w_down, wo_a, wo_d, tm):
    t = h1.shape[0]
    half = wo_a.shape[0]

    def body(dh_ref, dhb_ref, r_ref, h_ref, g_ref, wu_ref, wd_ref, woa_ref, wod_ref,
             da_ref, dh1_ref, dh1b_ref, dao_ref, ddn_ref, acc_ref, dm_ref):
        i, k = pl.program_id(0), pl.program_id(1)

        @pl.when((i == 0) & (k == 0))
        def _():
            acc_ref[...] = jnp.zeros_like(acc_ref)

        @pl.when(k == 0)
        def _():
            dm_ref[...] = jnp.zeros_like(dm_ref)

        ds = _dot(dhb_ref[...], wd_ref[...], NT)
        da = (ds * (2.0 * r_ref[...].astype(F32))).astype(MXU_DTYPE)
        da_ref[...] = da
        dm_ref[...] += _dot(da, wu_ref[...], NT)

        @pl.when(k == FF_BLOCKS - 1)
        def _():
            dx, dg = _rms_bwd(h_ref[...], g_ref[...], dm_ref[...])
            dh1 = dh_ref[...] + dx
            dh1_ref[...] = dh1
            dh1b = dh1.astype(MXU_DTYPE)
            dh1b_ref[...] = dh1b
            dao_ref[...] = _dot(dh1b, woa_ref[...], NT)
            ddn_ref[...] = _dot(dh1b, wod_ref[...], NT)
            acc_ref[0:1, :] += dg

    tok = lambda w: pl.BlockSpec((tm, w), lambda i, k: (i, 0))
    return pl.pallas_call(
        body, name="mlp_bwd", grid=(t // tm, FF_BLOCKS),
        in_specs=[tok(D_MODEL), tok(D_MODEL), pl.BlockSpec((tm, FF_BLOCK), lambda i, k: (i, k)), tok(D_MODEL),
                  _full((1, D_MODEL)), pl.BlockSpec((None, D_MODEL, FF_BLOCK), lambda i, k: (k, 0, 0)),
                  pl.BlockSpec((FF_BLOCK, D_MODEL), lambda i, k: (k, 0)),
                  pl.BlockSpec((half, D_MODEL), lambda i, k: (0, 0)), pl.BlockSpec((half, D_MODEL), lambda i, k: (0, 0))],
        out_specs=[pl.BlockSpec((tm, FF_BLOCK), lambda i, k: (i, k)),
                   tok(D_MODEL), tok(D_MODEL), tok(half), tok(half), pl.BlockSpec((8, D_MODEL), lambda i, k: (0, 0))],
        out_shape=[jax.ShapeDtypeStruct((t, D_FF), MXU_DTYPE),
                   jax.ShapeDtypeStruct((t, D_MODEL), F32), jax.ShapeDtypeStruct((t, D_MODEL), MXU_DTYPE),
                   jax.ShapeDtypeStruct((t, half), F32), jax.ShapeDtypeStruct((t, half), F32),
                   jax.ShapeDtypeStruct((8, D_MODEL), F32)],
        scratch_shapes=[pltpu.VMEM((tm, D_MODEL), F32)],
        compiler_params=_params("arbitrary", "arbitrary"),
    )(dh2, dh2b, r, h1, g_mlp, w_up4, w_down, wo_a, wo_d)


def _inproj_bwd(x, dh1, g_mix, grads, weights, tm):
    t = x.shape[0]
    n = len(grads)

    def body(*refs):
        x_ref, dh_ref, g_ref = refs[:3]
        g_refs, w_refs = refs[3:3 + n], refs[3 + n:3 + 2 * n]
        dx_ref, acc_ref = refs[3 + 2 * n:]

        @pl.when(pl.program_id(0) == 0)
        def _():
            acc_ref[...] = jnp.zeros_like(acc_ref)

        du = _dot(g_refs[0][...], w_refs[0][...])
        for j in range(1, n):
            du += _dot(g_refs[j][...], w_refs[j][...])
        dx, dg = _rms_bwd(x_ref[...], g_ref[...], du)
        dx_ref[...] = dh_ref[...] + dx
        acc_ref[0:1, :] += dg

    tok = lambda w: pl.BlockSpec((tm, w), lambda i: (i, 0))
    return pl.pallas_call(
        body, name="inproj_bwd", grid=(t // tm,),
        in_specs=[tok(D_MODEL), tok(D_MODEL), _full((1, D_MODEL))] + [tok(g.shape[1]) for g in grads]
                 + [_full(w.shape) for w in weights],
        out_specs=[tok(D_MODEL), _full((8, D_MODEL))],
        out_shape=[jax.ShapeDtypeStruct((t, D_MODEL), F32), jax.ShapeDtypeStruct((8, D_MODEL), F32)],
        compiler_params=_params("arbitrary"),
    )(x, dh1, g_mix, *grads, *weights)


def _wgrad(a, b, name, tk, tn, tt, stacked=False, prep=None):
    t, kdim = a.shape
    ncols = b.shape[1]

    def body(a_ref, b_ref, o_ref):
        @pl.when(pl.program_id(2) == 0)
        def _():
            o_ref[...] = jnp.zeros_like(o_ref)

        av = a_ref[...] if prep is None else prep(a_ref[...])
        o_ref[...] += _dot(av, b_ref[...], TN)

    if stacked:
        out_spec = pl.BlockSpec((None, tk, tn), lambda i, j, s: (j, i, 0))
        out_shape = jax.ShapeDtypeStruct((ncols // tn, kdim, tn), F32)
    else:
        out_spec = pl.BlockSpec((tk, tn), lambda i, j, s: (i, j))
        out_shape = jax.ShapeDtypeStruct((kdim, ncols), F32)
    return pl.pallas_call(
        body, name=name, grid=(kdim // tk, ncols // tn, t // tt),
        in_specs=[pl.BlockSpec((tt, tk), lambda i, j, s: (s, i)), pl.BlockSpec((tt, tn), lambda i, j, s: (s, j))],
        out_specs=out_spec, out_shape=out_shape,
        compiler_params=_params("parallel", "parallel", "arbitrary"),
    )(a, b)


def _wgrad_cat(as_, bs, name, tt):
    t = as_[0].shape[0]
    heights = [a.shape[1] for a in as_]
    widths = [b.shape[1] for b in bs]

    def body(*refs):
        a_refs, b_refs, o_ref = refs[:len(as_)], refs[len(as_):-1], refs[-1]

        @pl.when(pl.program_id(0) == 0)
        def _():
            o_ref[...] = jnp.zeros_like(o_ref)

        row = 0
        for a_ref, k in zip(a_refs, heights):
            av = a_ref[...]
            col = 0
            for b_ref, n in zip(b_refs, widths):
                o_ref[row:row + k, col:col + n] += _dot(av, b_ref[...], TN)
                col += n
            row += k

    tok = lambda w: pl.BlockSpec((tt, w), lambda s: (s, 0))
    shape = (sum(heights), sum(widths))
    return pl.pallas_call(
        body, name=name, grid=(t // tt,),
        in_specs=[tok(k) for k in heights] + [tok(n) for n in widths],
        out_specs=_full(shape), out_shape=jax.ShapeDtypeStruct(shape, F32),
        compiler_params=_params("arbitrary"),
    )(*as_, *bs)


def _rope_tables(t):
    half = ATTN_HEAD_DIM // 2
    lane = jnp.arange(2 * ATTN_HEAD_DIM)
    inv = 1.0 / (ROPE_THETA ** ((lane % half).astype(F32) * (2.0 / ATTN_HEAD_DIM)))
    ang = jnp.arange(t, dtype=F32)[:, None] * inv[None, :]
    sign = jnp.where(lane % ATTN_HEAD_DIM < half, -1.0, 1.0).astype(F32)
    return jnp.cos(ang), jnp.sin(ang) * sign[None, :]


def _swap_halves(tv):
    w = tv.shape[-1]
    lane = lax.broadcasted_iota(jnp.int32, tv.shape, tv.ndim - 1)
    first = (lane % ATTN_HEAD_DIM) < (ATTN_HEAD_DIM // 2)
    return jnp.where(first, pltpu.roll(tv, w - ATTN_HEAD_DIM // 2, tv.ndim - 1),
                     pltpu.roll(tv, ATTN_HEAD_DIM // 2, tv.ndim - 1))


def _rope(tv, cos, sin):
    return tv * cos + _swap_halves(tv) * sin


def _rope_bwd(dv, cos, sin):
    return dv * cos + _swap_halves(dv * sin)


def _attn_valid(first_block):
    c = lax.broadcasted_iota(jnp.int32, (2 * ATTN_BLOCK, ATTN_BLOCK), 0)
    r = lax.broadcasted_iota(jnp.int32, (2 * ATTN_BLOCK, ATTN_BLOCK), 1)
    return (c > r) & (c <= r + ATTN_BLOCK) & ((c >= ATTN_BLOCK) | jnp.logical_not(first_block))


def _attn_probs(st, sink, valid):
    s = jnp.where(valid, st * ATTN_SCALE, -jnp.inf)
    m = jnp.maximum(jnp.max(s, axis=0, keepdims=True), sink)
    e = jnp.where(valid, jnp.exp(s - m), 0.0)
    es = jnp.exp(sink - m)
    inv = 1.0 / (jnp.sum(e, axis=0, keepdims=True) + es)
    return e * inv, es * inv


def _lane_scalar(vec, idx):
    lane = lax.broadcasted_iota(jnp.int32, vec.shape, 1)
    return jnp.sum(jnp.where(lane == idx, vec, 0.0), axis=-1, keepdims=True)


ATTN_STEP = 2 * ATTN_BLOCK


def _attn_specs(ns):
    cur = lambda w, cb: pl.BlockSpec((ATTN_STEP, w), lambda i: (jnp.minimum(i, ns - 1), cb))
    prev = lambda w, cb: pl.BlockSpec((ATTN_BLOCK, w), lambda i: (jnp.maximum(2 * jnp.minimum(i, ns - 1) - 1, 0), cb))
    kcol, vcol = ATTN_Q // ATTN_KV, ATTN_Q // ATTN_KV + 1
    return [cur(ATTN_Q, 0), cur(ATTN_KV, kcol), prev(ATTN_KV, kcol), cur(ATTN_KV, vcol), prev(ATTN_KV, vcol),
            cur(ATTN_KV, 0), cur(ATTN_KV, 0), prev(ATTN_KV, 0), prev(ATTN_KV, 0), _full((1, 128))]


def _attn_windows(tp, tc):
    hsl = lambda hk: slice(hk * ATTN_HEAD_DIM, (hk + 1) * ATTN_HEAD_DIM)
    return [[jnp.concatenate([tp[:, hsl(hk)], tc[0:ATTN_BLOCK, hsl(hk)]], axis=0) for hk in range(ATTN_KV_HEADS)],
            [tc[:, hsl(hk)] for hk in range(ATTN_KV_HEADS)]]


def _attn_items():
    return [(s, h, slice(s * ATTN_BLOCK, (s + 1) * ATTN_BLOCK), slice(h * ATTN_HEAD_DIM, (h + 1) * ATTN_HEAD_DIM))
            for s in range(2) for h in range(ATTN_HEADS)]


def _attn_fwd(pa, cos, sin, sinks_vec):
    t = pa.shape[0]
    ns = t // ATTN_STEP

    def body(q_ref, kc_ref, kp_ref, vc_ref, vp_ref, cc_ref, sc_ref, cp_ref, sp_ref, sk_ref, o_ref):
        cc, sc = cc_ref[...], sc_ref[...]
        q = _rope(q_ref[...], jnp.tile(cc, (1, ATTN_Q // ATTN_KV)), jnp.tile(sc, (1, ATTN_Q // ATTN_KV)))
        kc = _rope(kc_ref[...], cc, sc)
        kp = _rope(kp_ref[...], cp_ref[...], sp_ref[...])
        sk = sk_ref[...]
        valids = [_attn_valid(pl.program_id(0) == 0), _attn_valid(False)]
        kwins = _attn_windows(kp, kc)
        vwins_t = [[v.T for v in vs] for vs in _attn_windows(vp_ref[...], vc_ref[...])]
        items = _attn_items()
        scores = [_dot(kwins[s][h // ATTN_GROUPS], q[rows, hs], NT) for s, h, rows, hs in items]
        probs = [_attn_probs(st, _lane_scalar(sk, h), valids[s])[0] for (s, h, rows, hs), st in zip(items, scores)]
        for (s, h, rows, hs), pt in zip(items, probs):
            o_ref[rows, hs] = _dot(vwins_t[s][h // ATTN_GROUPS], pt).T.astype(o_ref.dtype)

    return pl.pallas_call(
        body, name="attn_fwd", grid=(ns,),
        in_specs=_attn_specs(ns),
        out_specs=pl.BlockSpec((ATTN_STEP, ATTN_Q), lambda i: (i, 0)),
        out_shape=jax.ShapeDtypeStruct((t, ATTN_Q), MXU_DTYPE),
        compiler_params=_params("parallel"),
    )(pa, pa, pa, pa, pa, cos, sin, cos, sin, sinks_vec)


def _attn_bwd(pa, cos, sin, sinks_vec, dao):
    t = pa.shape[0]
    ns = t // ATTN_STEP
    lo, hi = slice(0, ATTN_BLOCK), slice(ATTN_BLOCK, ATTN_STEP)

    def body(q_ref, kc_ref, kp_ref, vc_ref, vp_ref, cc_ref, sc_ref, cp_ref, sp_ref, sk_ref, do_ref,
             dq_ref, dk_ref, dv_ref, acc_ref, dqr_ref, dkw_ref, dvw_ref, ck_ref, cv_ref):
        i = pl.program_id(0)

        @pl.when(i == 0)
        def _():
            acc_ref[...] = jnp.zeros_like(acc_ref)
            ck_ref[...] = jnp.zeros_like(ck_ref)
            cv_ref[...] = jnp.zeros_like(cv_ref)

        @pl.when(i < ns)
        def _():
            cc, sc = cc_ref[...], sc_ref[...]
            cq, sq = jnp.tile(cc, (1, ATTN_Q // ATTN_KV)), jnp.tile(sc, (1, ATTN_Q // ATTN_KV))
            q = _rope(q_ref[...], cq, sq)
            kc = _rope(kc_ref[...], cc, sc)
            kp = _rope(kp_ref[...], cp_ref[...], sp_ref[...])
            sk = sk_ref[...]
            do = do_ref[...]
            lane = lax.broadcasted_iota(jnp.int32, (1, 128), 1)
            dsink = jnp.zeros((1, 128), F32)
            valids = [_attn_valid(i == 0), _attn_valid(False)]
            kwins = _attn_windows(kp, kc)
            vwins = _attn_windows(vp_ref[...], vc_ref[...])
            kwins_t = [[kw.T for kw in kws] for kws in kwins]
            items = _attn_items()
            scores = [_dot(kwins[s][h // ATTN_GROUPS], q[rows, hs], NT) for s, h, rows, hs in items]
            dps = [_dot(vwins[s][h // ATTN_GROUPS], do[rows, hs], NT) for s, h, rows, hs in items]
            pts, dsts = {}, {}
            for (s, h, rows, hs), st, dp_t in zip(items, scores, dps):
                probs_t, psink = _attn_probs(st, _lane_scalar(sk, h), valids[s])
                delta = jnp.sum(probs_t * dp_t, axis=0, keepdims=True)
                pts[s, h] = probs_t
                dsts[s, h] = probs_t * (dp_t - delta) * ATTN_SCALE
                dsink += jnp.where(lane == h, jnp.sum(-psink * delta, axis=1, keepdims=True), 0.0)
            for s, h, rows, hs in items:
                dqr_ref[rows, hs] = _dot(kwins_t[s][h // ATTN_GROUPS], dsts[s, h]).T
            for s in range(2):
                rows = slice(s * ATTN_BLOCK, (s + 1) * ATTN_BLOCK)
                for hk in range(ATTN_KV_HEADS):
                    ks = slice(hk * ATTN_HEAD_DIM, (hk + 1) * ATTN_HEAD_DIM)
                    group = range(hk * ATTN_GROUPS, (hk + 1) * ATTN_GROUPS)
                    heads = [slice(h * ATTN_HEAD_DIM, (h + 1) * ATTN_HEAD_DIM) for h in group]
                    ds_g = jnp.concatenate([dsts[s, h] for h in group], axis=1)
                    p_g = jnp.concatenate([pts[s, h] for h in group], axis=1)
                    q_g = jnp.concatenate([q[rows, hs] for hs in heads], axis=0)
                    do_g = jnp.concatenate([do[rows, hs] for hs in heads], axis=0)
                    dkw_ref[s, :, ks] = _dot(ds_g, q_g)
                    dvw_ref[s, :, ks] = _dot(p_g, do_g)
            acc_ref[0:1, :] += dsink
            dq_ref[...] = _rope_bwd(dqr_ref[...], cq, sq).astype(dq_ref.dtype)
            dk_ref[lo, :] = ck_ref[lo, :].astype(dk_ref.dtype)
            dk_ref[hi, :] = (ck_ref[hi, :] + _rope_bwd(dkw_ref[0, lo, :], cp_ref[...], sp_ref[...])).astype(dk_ref.dtype)
            dv_ref[lo, :] = cv_ref[lo, :].astype(dv_ref.dtype)
            dv_ref[hi, :] = (cv_ref[hi, :] + dvw_ref[0, lo, :]).astype(dv_ref.dtype)
            ck_ref[lo, :] = _rope_bwd(dkw_ref[0, hi, :] + dkw_ref[1, lo, :], cc[lo, :], sc[lo, :])
            ck_ref[hi, :] = _rope_bwd(dkw_ref[1, hi, :], cc[hi, :], sc[hi, :])
            cv_ref[lo, :] = dvw_ref[0, hi, :] + dvw_ref[1, lo, :]
            cv_ref[hi, :] = dvw_ref[1, hi, :]

        @pl.when(i == ns)
        def _():
            dk_ref[...] = ck_ref[...].astype(dk_ref.dtype)
            dv_ref[...] = cv_ref[...].astype(dv_ref.dtype)

    prev_out = lambda w: pl.BlockSpec((ATTN_STEP, w), lambda i: (jnp.maximum(i - 1, 0), 0))
    return pl.pallas_call(
        body, name="attn_bwd", grid=(ns + 1,),
        in_specs=_attn_specs(ns) + [pl.BlockSpec((ATTN_STEP, ATTN_Q), lambda i: (jnp.minimum(i, ns - 1), 0))],
        out_specs=[pl.BlockSpec((ATTN_STEP, ATTN_Q), lambda i: (jnp.minimum(i, ns - 1), 0)), prev_out(ATTN_KV),
                   prev_out(ATTN_KV), _full((8, 128))],
        out_shape=[jax.ShapeDtypeStruct((t, ATTN_Q), MXU_DTYPE), jax.ShapeDtypeStruct((t, ATTN_KV), MXU_DTYPE),
                   jax.ShapeDtypeStruct((t, ATTN_KV), MXU_DTYPE), jax.ShapeDtypeStruct((8, 128), F32)],
        scratch_shapes=[pltpu.VMEM((ATTN_STEP, ATTN_Q), F32), pltpu.VMEM((2, ATTN_STEP, ATTN_KV), F32),
                        pltpu.VMEM((2, ATTN_STEP, ATTN_KV), F32), pltpu.VMEM((ATTN_STEP, ATTN_KV), F32),
                        pltpu.VMEM((ATTN_STEP, ATTN_KV), F32)],
        compiler_params=_params("arbitrary"),
    )(pa, pa, pa, pa, pa, cos, sin, cos, sin, sinks_vec, dao)


PAIR = 2 * DN_CHUNK
INTRA_PAIRS = 4
SCAN_PAIRS = 4
HALO = 8


def _conv_window(cur_ref, prev_ref, xs_ref, tm, has_prev):
    prev = jnp.where(has_prev, prev_ref[...], 0.0)
    xs_ref[0:HALO, :] = prev
    xs_ref[HALO:HALO + tm, :] = cur_ref[...]


def _conv_taps(xs_ref, cw_ref, tm):
    y = cw_ref[0:1, :] * xs_ref[pl.ds(HALO - DN_CONV + 1, tm), :]
    for j in range(1, DN_CONV):
        y += cw_ref[j:j + 1, :] * xs_ref[pl.ds(HALO - DN_CONV + 1 + j, tm), :]
    return y


def _gate_values(ba, al, dt):
    beta = _sigmoid(ba)
    pre = ba + dt
    g = -jnp.exp(al) * _softplus(pre)
    return beta, g, pre


def _dn_prep_specs(tm, tile):
    return [pl.BlockSpec((tm, CONV_CH), lambda i: (tile(i), 0)),
            pl.BlockSpec((HALO, CONV_CH), lambda i: (jnp.maximum(tile(i) * (tm // HALO) - 1, 0), 0)),
            pl.BlockSpec((tm, 128), lambda i: (tile(i), 4 * DN_W // 128)),
            _full((DN_CONV, CONV_CH)), _full((1, 128)), _full((1, 128))]


def _dn_prep(pd, conv_w, al_vec, dt_vec, tm):
    t = pd.shape[0]

    def body(cur_ref, prev_ref, ba_ref, cw_ref, al_ref, dt_ref, qn_ref, kn_ref, vc_ref, gc_ref, gr_ref, xs_ref):
        _conv_window(cur_ref, prev_ref, xs_ref, tm, pl.program_id(0) > 0)
        y = _conv_taps(xs_ref, cw_ref, tm)
        c = y * _sigmoid(y)
        for h in range(DN_HEADS):
            qs = slice(h * DN_HEAD_DIM, (h + 1) * DN_HEAD_DIM)
            ksl = slice(DN_W + h * DN_HEAD_DIM, DN_W + (h + 1) * DN_HEAD_DIM)
            qh, kh = c[:, qs], c[:, ksl]
            qn_ref[:, qs] = qh * lax.rsqrt(jnp.sum(qh * qh, axis=-1, keepdims=True) + EPS) * DN_SCALE
            kn_ref[:, qs] = kh * lax.rsqrt(jnp.sum(kh * kh, axis=-1, keepdims=True) + EPS)
        vc_ref[...] = c[:, 2 * DN_W:3 * DN_W]
        beta, g, _ = _gate_values(ba_ref[...], al_ref[...], dt_ref[...])
        lane = lax.broadcasted_iota(jnp.int32, beta.shape, 1)
        gb = jnp.where(lane < DN_HEADS, beta, jnp.where(lane < 2 * DN_HEADS, g, 0.0))
        gc_ref[...] = gb
        gr_ref[...] = gb.T[0:8, :]

    tok = lambda w: pl.BlockSpec((tm, w), lambda i: (i, 0))
    return pl.pallas_call(
        body, name="dn_prep", grid=(t // tm,),
        in_specs=_dn_prep_specs(tm, lambda i: i),
        out_specs=[tok(DN_W), tok(DN_W), tok(DN_W), tok(128), pl.BlockSpec((8, tm), lambda i: (0, i))],
        out_shape=[jax.ShapeDtypeStruct((t, DN_W), F32)] * 3 + [jax.ShapeDtypeStruct((t, 128), F32),
                                                                 jax.ShapeDtypeStruct((8, t), F32)],
        scratch_shapes=[pltpu.VMEM((HALO + tm, CONV_CH), F32)],
        compiler_params=_params("parallel"),
    )(pd, pd, pd, conv_w, al_vec, dt_vec)


def _pair_masks():
    r = lax.broadcasted_iota(jnp.int32, (PAIR, PAIR), 0)
    c = lax.broadcasted_iota(jnp.int32, (PAIR, PAIR), 1)
    same = (r < DN_CHUNK) == (c < DN_CHUNK)
    return same & (r >= c), same & (r > c)


def _lane_col(mat, idx):
    lane = lax.broadcasted_iota(jnp.int32, mat.shape, 1)
    return jnp.sum(jnp.where(lane == idx, mat, 0.0), axis=-1, keepdims=True)


def _pair_cumsums(gc, gr, low):
    lowf = low.astype(F32)
    return _dot(lowf, gc, NN, HI), _dot(gr, lowf, NT, HI)


def _pair_gates(gc, cum_c, cum_r, low, h):
    beta = _lane_col(gc, h)
    gam = _lane_col(cum_c, DN_HEADS + h)
    gam_row = cum_r[DN_HEADS + h:DN_HEADS + h + 1, :]
    dm = jnp.where(low, jnp.exp(jnp.where(low, gam - gam_row, 0.0)), 0.0)
    row = lax.broadcasted_iota(jnp.int32, gam.shape, 0)
    gl = jnp.where(row < DN_CHUNK, gam[DN_CHUNK - 1:DN_CHUNK, :], gam[PAIR - 1:PAIR, :])
    return beta, gam, dm, gl


def _split(a):
    hi = a.astype(BF16)
    return hi, (a - hi.astype(F32)).astype(BF16)


def _dot_split(a, b, dims=NN):
    (ah, al), (bh, bl) = a, b
    la, lb = (1, 1) if dims == TN else ((0, 1) if dims == NN else (0, 0))
    r = _dot(jnp.concatenate([ah, al], axis=la), jnp.concatenate([bh, bl], axis=lb), dims)
    m, n = r.shape[0] // 2, r.shape[1] // 2
    return (r[m:, n:] + (r[:m, n:] + r[m:, :n])) + r[:m, :n]


def _unit_lower_inverses(lmats):
    n = lmats[0].shape[0]
    r = lax.broadcasted_iota(jnp.int32, (n, n), 0)
    c = lax.broadcasted_iota(jnp.int32, (n, n), 1)
    same = lambda size: (r & ~(size - 1)) == (c & ~(size - 1))
    base = DN_CHUNK // 4
    diag = [jnp.where(same(base), l, 0.0) for l in lmats]
    accs = [(r == c).astype(F32) - d for d in diag]
    splits = [_split(d) for d in diag]
    step = 1
    while 2 * step < base:
        splits = [_split(_dot_split(s, s)) for s in splits]
        accs = [acc + _dot_split(_split(acc), s) for acc, s in zip(accs, splits)]
        step *= 2
    size = base
    while size < DN_CHUNK:
        below = same(2 * size) & jnp.logical_not(same(size))
        tb = [_dot(acc, jnp.where(below, l, 0.0)) for acc, l in zip(accs, lmats)]
        accs = [acc - _dot(t, acc) for acc, t in zip(accs, tb)]
        size *= 2
    return accs


def _dn_intra(qn, kn, vc, gc, gr):
    t = qn.shape[0]
    npair = t // PAIR
    rows_step = INTRA_PAIRS * PAIR

    def body(q_ref, k_ref, v_ref, gc_ref, gr_ref, u_ref, w_ref, qg_ref, kd_ref, a_ref, ti_ref, dl_ref):
        low, strict = _pair_masks()
        items = []
        for p in range(INTRA_PAIRS):
            rows = slice(p * PAIR, (p + 1) * PAIR)
            gc_v = gc_ref[rows, :]
            cum_c, cum_r = _pair_cumsums(gc_v, gr_ref[:, rows], low)
            for h in range(DN_HEADS):
                hs = slice(h * DN_HEAD_DIM, (h + 1) * DN_HEAD_DIM)
                items.append((p, h, rows, hs, _pair_gates(gc_v, cum_c, cum_r, low, h)))
        lmats = []
        for p, h, rows, hs, (beta, gam, dm, gl) in items:
            k = k_ref[rows, hs]
            lmats.append(jnp.where(strict, _dot(k * beta, k, NT) * dm, 0.0))
        tinvs = _unit_lower_inverses(lmats)
        for (p, h, rows, hs, (beta, gam, dm, gl)), tinv in zip(items, tinvs):
            q, k, v = q_ref[rows, hs], k_ref[rows, hs], v_ref[rows, hs]
            eg = jnp.exp(gam)
            u_ref[rows, hs] = _dot(tinv, v * beta)
            w_ref[rows, hs] = _dot(tinv, (k * beta) * eg).astype(w_ref.dtype)
            a_ref[h, rows, :] = _dot(q, k, NT) * dm
            ti_ref[h, rows, :] = tinv
            qg_ref[rows, hs] = (q * eg).astype(qg_ref.dtype)
            kd_ref[rows, hs] = (k * jnp.exp(gl - gam)).astype(kd_ref.dtype)
            for c in range(2):
                last = (c + 1) * DN_CHUNK - 1
                dl_ref[2 * p + c, h] = jnp.broadcast_to(jnp.exp(gam[last:last + 1, :]), (8, 128))

    tok = lambda w: pl.BlockSpec((rows_step, w), lambda n: (n, 0))
    hm = pl.BlockSpec((DN_HEADS, rows_step, PAIR), lambda n: (0, n, 0))
    return pl.pallas_call(
        body, name="dn_intra", grid=(npair // INTRA_PAIRS,),
        in_specs=[tok(DN_W), tok(DN_W), tok(DN_W), tok(128), pl.BlockSpec((8, rows_step), lambda n: (0, n))],
        out_specs=[tok(DN_W)] * 4 + [hm, hm, pl.BlockSpec((2 * INTRA_PAIRS, DN_HEADS, 8, 128), lambda n: (n, 0, 0, 0))],
        out_shape=[jax.ShapeDtypeStruct((t, DN_W), F32)] + [jax.ShapeDtypeStruct((t, DN_W), MXU_DTYPE)] * 3
                  + [jax.ShapeDtypeStruct((DN_HEADS, t, PAIR), F32)] * 2
                  + [jax.ShapeDtypeStruct((2 * npair, DN_HEADS, 8, 128), F32)],
        compiler_params=_params("parallel"),
    )(qn, kn, vc, gc, gr)


def _dn_scan_fwd(u, w, qg, kd, a_qk, dlast, pd, dn_w):
    t = u.shape[0]
    npair = t // PAIR

    def body(u_ref, w_ref, qg_ref, kd_ref, a_ref, dl_ref, z_ref, nw_ref, out_ref, o_ref, vn_ref, sall_ref, s_ref):
        @pl.when(pl.program_id(0) == 0)
        def _():
            s_ref[...] = jnp.zeros_like(s_ref)

        nw = nw_ref[...]
        for c in range(2 * SCAN_PAIRS):
            rows = slice(c * DN_CHUNK, (c + 1) * DN_CHUNK)
            diag = slice((c % 2) * DN_CHUNK, (c % 2 + 1) * DN_CHUNK)
            for h in range(DN_HEADS):
                hs = slice(h * DN_HEAD_DIM, (h + 1) * DN_HEAD_DIM)
                st = s_ref[h]
                sall_ref[c, h] = st
                vn_ref[rows, hs] = (u_ref[rows, hs] - _dot(w_ref[rows, hs], st)).astype(vn_ref.dtype)
            for h in range(DN_HEADS):
                hs = slice(h * DN_HEAD_DIM, (h + 1) * DN_HEAD_DIM)
                st, vn = s_ref[h], vn_ref[rows, hs]
                o = _dot(qg_ref[rows, hs], st) + _dot(a_ref[h, rows, diag], vn)
                s_ref[h] = st * dl_ref[c, h][0:1, :] + _dot(kd_ref[rows, hs], vn, TN)
                o_ref[rows, hs] = o
                z = z_ref[rows, hs]
                on = o * lax.rsqrt(jnp.mean(o * o, axis=-1, keepdims=True) + EPS) * nw
                out_ref[rows, hs] = (on * (z * _sigmoid(z))).astype(out_ref.dtype)

    rows_step = SCAN_PAIRS * PAIR
    tok = pl.BlockSpec((rows_step, DN_W), lambda n: (n, 0))
    hm = pl.BlockSpec((DN_HEADS, rows_step, PAIR), lambda n: (0, n, 0))
    return pl.pallas_call(
        body, name="dn_scan_fwd", grid=(npair // SCAN_PAIRS,),
        in_specs=[tok, tok, tok, tok, hm, pl.BlockSpec((2 * SCAN_PAIRS, DN_HEADS, 8, 128), lambda n: (n, 0, 0, 0)),
                  pl.BlockSpec((rows_step, DN_W), lambda n: (n, 3)), _full((1, 128))],
        out_specs=[tok, tok, tok,
                   pl.BlockSpec((2 * SCAN_PAIRS, DN_HEADS, DN_HEAD_DIM, DN_HEAD_DIM), lambda n: (n, 0, 0, 0))],
        out_shape=[jax.ShapeDtypeStruct((t, DN_W), MXU_DTYPE), jax.ShapeDtypeStruct((t, DN_W), F32),
                   jax.ShapeDtypeStruct((t, DN_W), MXU_DTYPE),
                   jax.ShapeDtypeStruct((2 * npair, DN_HEADS, DN_HEAD_DIM, DN_HEAD_DIM), F32)],
        scratch_shapes=[pltpu.VMEM((DN_HEADS, DN_HEAD_DIM, DN_HEAD_DIM), F32)],
        compiler_params=_params("arbitrary"),
    )(u, w, qg, kd, a_qk, dlast, pd, dn_w)


def _dn_scan_bwd(dout, o, vnew, sall, w, qg, kd, a_qk, dlast, pd, dn_w, dep):
    t = o.shape[0]
    npair = t // PAIR
    nstep = npair // SCAN_PAIRS
    rev = lambda n: nstep - 1 - n

    def body(do_ref, o_ref, vn_ref, sall_ref, w_ref, qg_ref, kd_ref, a_ref, dl_ref, z_ref, nw_ref, dep_ref,
             dz_ref, du_ref, dw_ref, dqg_ref, dkd_ref, da_ref, ddl_ref, acc_ref, ds_ref, dos_ref):
        @pl.when(pl.program_id(0) == 0)
        def _():
            ds_ref[...] = jnp.zeros_like(ds_ref)
            acc_ref[...] = jnp.zeros_like(acc_ref)

        nw = nw_ref[...]
        dnw = jnp.zeros((1, 128), F32)
        for h in range(DN_HEADS):
            hs = slice(h * DN_HEAD_DIM, (h + 1) * DN_HEAD_DIM)
            o, z, dout = o_ref[:, hs], z_ref[:, hs], do_ref[:, hs]
            r = lax.rsqrt(jnp.mean(o * o, axis=-1, keepdims=True) + EPS)
            oh = o * r
            sz = _sigmoid(z)
            dz_ref[:, hs] = dout * (oh * nw) * (sz + z * sz * (1.0 - sz))
            don = dout * (z * sz)
            dnw += jnp.sum(don * oh, axis=0, keepdims=True)
            doh = don * nw
            dos_ref[:, hs] = r * (doh - oh * jnp.mean(doh * oh, axis=-1, keepdims=True))
        acc_ref[0:1, :] += dnw
        for c in reversed(range(2 * SCAN_PAIRS)):
            rows = slice(c * DN_CHUNK, (c + 1) * DN_CHUNK)
            diag = slice((c % 2) * DN_CHUNK, (c % 2 + 1) * DN_CHUNK)
            other = slice((1 - c % 2) * DN_CHUNK, (2 - c % 2) * DN_CHUNK)
            for h in range(DN_HEADS):
                hs = slice(h * DN_HEAD_DIM, (h + 1) * DN_HEAD_DIM)
                do, st, dsp, vn = dos_ref[rows, hs], sall_ref[c, h], ds_ref[h], vn_ref[rows, hs]
                da_ref[h, rows, diag] = _dot(do, vn, NT)
                da_ref[h, rows, other] = jnp.zeros((DN_CHUNK, DN_CHUNK), F32)
                du_ref[rows, hs] = (_dot(a_ref[h, rows, diag], do, TN) + _dot(kd_ref[rows, hs], dsp)).astype(du_ref.dtype)
                dqg_ref[rows, hs] = _dot(do, st, NT)
                dkd_ref[rows, hs] = _dot(vn, dsp, NT)
                ddl = jnp.sum(jnp.sum(dsp * st, axis=1, keepdims=True), axis=0, keepdims=True)
                ddl_ref[c, h] = jnp.broadcast_to(ddl, (8, 128))
            for h in range(DN_HEADS):
                hs = slice(h * DN_HEAD_DIM, (h + 1) * DN_HEAD_DIM)
                do, st, dvn = dos_ref[rows, hs], sall_ref[c, h], du_ref[rows, hs]
                dw_ref[rows, hs] = (-_dot(dvn, st, NT)).astype(dw_ref.dtype)
                ds_ref[h] = (ds_ref[h] * dl_ref[c, h][0:1, :] + _dot(qg_ref[rows, hs], do, TN)
                             - _dot(w_ref[rows, hs], dvn, TN))

    rows_step = SCAN_PAIRS * PAIR
    tok = pl.BlockSpec((rows_step, DN_W), lambda n: (rev(n), 0))
    hm = pl.BlockSpec((DN_HEADS, rows_step, PAIR), lambda n: (0, rev(n), 0))
    sc = pl.BlockSpec((2 * SCAN_PAIRS, DN_HEADS, 8, 128), lambda n: (rev(n), 0, 0, 0))
    return pl.pallas_call(
        body, name="dn_scan_bwd", grid=(nstep,),
        in_specs=[tok, tok, tok,
                  pl.BlockSpec((2 * SCAN_PAIRS, DN_HEADS, DN_HEAD_DIM, DN_HEAD_DIM), lambda n: (rev(n), 0, 0, 0)),
                  tok, tok, tok, hm, sc, pl.BlockSpec((rows_step, DN_W), lambda n: (rev(n), 3)), _full((1, 128)),
                  pl.BlockSpec(memory_space=pl.ANY)],
        out_specs=[tok] * 5 + [hm, sc, _full((8, 128))],
        out_shape=[jax.ShapeDtypeStruct((t, DN_W), F32)] + [jax.ShapeDtypeStruct((t, DN_W), MXU_DTYPE)] * 2
                  + [jax.ShapeDtypeStruct((t, DN_W), F32)] * 2 + [jax.ShapeDtypeStruct((DN_HEADS, t, PAIR), F32),
                   jax.ShapeDtypeStruct((2 * npair, DN_HEADS, 8, 128), F32), jax.ShapeDtypeStruct((8, 128), F32)],
        scratch_shapes=[pltpu.VMEM((DN_HEADS, DN_HEAD_DIM, DN_HEAD_DIM), F32), pltpu.VMEM((SCAN_PAIRS * PAIR, DN_W), F32)],
        compiler_params=_params("arbitrary"),
    )(dout, o, vnew, sall, w, qg, kd, a_qk, dlast, pd, dn_w, dep)


def _dn_intra_bwd(qn, kn, vc, gc, gr, tinv, a_qk, du, dw, dqg, dkd, da_qk, ddlast, dlast, dep):
    t = qn.shape[0]
    npair = t // PAIR

    def body(q_ref, k_ref, v_ref, gc_ref, gr_ref, ti_ref, a_ref, du_ref, dw_ref, dqg_ref, dkd_ref, da_ref, ddl_ref, dl_ref,
             dep_ref, dq_ref, dk_ref, dv_ref, dg_ref):
        low, strict = _pair_masks()
        lane = lax.broadcasted_iota(jnp.int32, (PAIR, 128), 1)
        rowi = lax.broadcasted_iota(jnp.int32, (PAIR, 1), 0)
        rsum = lambda v: jnp.sum(v, axis=-1, keepdims=True)
        items = []
        for p in range(INTRA_PAIRS):
            rows = slice(p * PAIR, (p + 1) * PAIR)
            gc_v = gc_ref[rows, :]
            cum_c, cum_r = _pair_cumsums(gc_v, gr_ref[:, rows], low)
            for h in range(DN_HEADS):
                hs = slice(h * DN_HEAD_DIM, (h + 1) * DN_HEAD_DIM)
                items.append((p, h, rows, hs, _pair_gates(gc_v, cum_c, cum_r, low, h)))
        dtis, lmats, dvbs, dkbgs = [], [], [], []
        for p, h, rows, hs, (beta, gam, dm, gl) in items:
            k, tinv = k_ref[rows, hs], ti_ref[h, rows, :]
            kb = k * beta
            dtis.append(_dot(du_ref[rows, hs], v_ref[rows, hs] * beta, NT)
                        + _dot(dw_ref[rows, hs], kb * jnp.exp(gam), NT))
            lmats.append(jnp.where(strict, _dot(kb, k, NT) * dm, 0.0))
            dvbs.append(_dot(tinv, du_ref[rows, hs], TN))
            dkbgs.append(_dot(tinv, dw_ref[rows, hs], TN))
        xs = [_dot(ti_ref[h, rows, :], dti, TN) for (p, h, rows, hs, g), dti in zip(items, dtis)]
        dls = [jnp.where(strict, -_dot(x, ti_ref[h, rows, :], NT), 0.0) for (p, h, rows, hs, g), x in zip(items, xs)]
        dgam_all = [jnp.zeros((PAIR, 128), F32) for _ in range(INTRA_PAIRS)]
        dbeta_all = [jnp.zeros((PAIR, 128), F32) for _ in range(INTRA_PAIRS)]
        for (p, h, rows, hs, (beta, gam, dm, gl)), dl, lmat, dvb, dkbg in zip(items, dls, lmats, dvbs, dkbgs):
            q, k, v = q_ref[rows, hs], k_ref[rows, hs], v_ref[rows, hs]
            a = a_ref[h, rows, :]
            dqg, dkd = dqg_ref[rows, hs], dkd_ref[rows, hs]
            kb = k * beta
            eg = jnp.exp(gam)
            ekd = jnp.exp(gl - gam)
            dmm = dl * dm
            dam = jnp.where(low, da_ref[h, rows, :], 0.0)
            dn = dam * dm
            e = dl * lmat + dam * a
            dkb = _dot(dmm, k) + dkbg * eg
            dk_ref[rows, hs] = _dot(dmm, kb, TN) + _dot(dn, q, TN) + dkd * ekd + dkb * beta
            dq_ref[rows, hs] = _dot(dn, k) + dqg * eg
            dv_ref[rows, hs] = dvb * beta
            t_kd = rsum(dkd * (k * ekd))
            dgam = rsum(e) - rsum(e.T) + rsum(dqg * (q * eg)) + rsum(dkbg * (kb * eg)) - t_kd
            for c in range(2):
                crows = slice(c * DN_CHUNK, (c + 1) * DN_CHUNK)
                dgl = (jnp.sum(t_kd[crows, :], axis=0, keepdims=True)
                       + ddl_ref[2 * p + c, h][0:1, 0:1] * dl_ref[2 * p + c, h][0:1, 0:1])
                dgam = dgam + jnp.where(rowi == (c + 1) * DN_CHUNK - 1, dgl, 0.0)
            dgam_all[p] += jnp.where(lane == DN_HEADS + h, dgam, 0.0)
            dbeta_all[p] += jnp.where(lane == h, rsum(dkb * k) + rsum(dvb * v), 0.0)
        for p in range(INTRA_PAIRS):
            dg_ref[p * PAIR:(p + 1) * PAIR, :] = dbeta_all[p] + _dot(low.astype(F32), dgam_all[p], TN, HI)

    rows_step = INTRA_PAIRS * PAIR
    tok = lambda w: pl.BlockSpec((rows_step, w), lambda n: (n, 0))
    hm = pl.BlockSpec((DN_HEADS, rows_step, PAIR), lambda n: (0, n, 0))
    sc = pl.BlockSpec((2 * INTRA_PAIRS, DN_HEADS, 8, 128), lambda n: (n, 0, 0, 0))
    return pl.pallas_call(
        body, name="dn_intra_bwd", grid=(npair // INTRA_PAIRS,),
        in_specs=[tok(DN_W), tok(DN_W), tok(DN_W), tok(128), pl.BlockSpec((8, rows_step), lambda n: (0, n)), hm, hm,
                  tok(DN_W), tok(DN_W), tok(DN_W), tok(DN_W), hm, sc, sc, pl.BlockSpec(memory_space=pl.ANY)],
        out_specs=[tok(DN_W), tok(DN_W), tok(DN_W), tok(128)],
        out_shape=[jax.ShapeDtypeStruct((t, DN_W), F32)] * 3 + [jax.ShapeDtypeStruct((t, 128), F32)],
        compiler_params=_params("parallel"),
    )(qn, kn, vc, gc, gr, tinv, a_qk, du, dw, dqg, dkd, da_qk, ddlast, dlast, dep)


def _dn_prep_bwd(pd, conv_w, al_vec, dt_vec, dqn, dkn, dvc, dgc, dz, tm):
    t = pd.shape[0]
    nt = t // tm
    tile = lambda i: nt - 1 - i

    def body(cur_ref, prev_ref, ba_ref, cw_ref, al_ref, dt_ref, dq_ref, dk_ref, dv_ref, dg_ref, dz_ref,
             o_ref, accw_ref, accg_ref, xs_ref, dc_ref, ds_ref, carry_ref):
        @pl.when(pl.program_id(0) == 0)
        def _():
            accw_ref[...] = jnp.zeros_like(accw_ref)
            accg_ref[...] = jnp.zeros_like(accg_ref)
            carry_ref[...] = jnp.zeros_like(carry_ref)

        _conv_window(cur_ref, prev_ref, xs_ref, tm, tile(pl.program_id(0)) > 0)
        taps = [xs_ref[pl.ds(HALO - DN_CONV + 1 + j, tm), :] for j in range(DN_CONV)]
        y = cw_ref[0:1, :] * taps[0]
        for j in range(1, DN_CONV):
            y += cw_ref[j:j + 1, :] * taps[j]
        sg = _sigmoid(y)
        c = y * sg
        for h in range(DN_HEADS):
            qs = slice(h * DN_HEAD_DIM, (h + 1) * DN_HEAD_DIM)
            ksl = slice(DN_W + h * DN_HEAD_DIM, DN_W + (h + 1) * DN_HEAD_DIM)
            for src, sl, scale in ((dq_ref, qs, DN_SCALE), (dk_ref, ksl, 1.0)):
                xh = c[:, sl]
                r = lax.rsqrt(jnp.sum(xh * xh, axis=-1, keepdims=True) + EPS)
                unit = xh * r
                dn = src[:, qs] * scale
                dc_ref[:, sl] = r * (dn - unit * jnp.sum(dn * unit, axis=-1, keepdims=True))
        dc_ref[:, 2 * DN_W:3 * DN_W] = dv_ref[...]
        dy = dc_ref[...] * (sg + y * sg * (1.0 - sg))
        for j in range(DN_CONV):
            accw_ref[j:j + 1, :] += jnp.sum(dy * taps[j], axis=0, keepdims=True)
        ds_ref[0:tm, :] = dy
        ds_ref[tm:tm + HALO, :] = carry_ref[...]
        carry_ref[...] = ds_ref[0:HALO, :]
        dx = cw_ref[0:1, :] * ds_ref[pl.ds(DN_CONV - 1, tm), :]
        for j in range(1, DN_CONV):
            dx += cw_ref[j:j + 1, :] * ds_ref[pl.ds(DN_CONV - 1 - j, tm), :]

        beta, g, pre = _gate_values(ba_ref[...], al_ref[...], dt_ref[...])
        dgb = dg_ref[...]
        lane = lax.broadcasted_iota(jnp.int32, dgb.shape, 1)
        is_b, is_a = lane < DN_HEADS, (lane >= DN_HEADS) & (lane < 2 * DN_HEADS)
        dpre = dgb * (-jnp.exp(al_ref[...])) * _sigmoid(pre)
        dba = jnp.where(is_b, dgb * beta * (1.0 - beta), jnp.where(is_a, dpre, 0.0))
        accg_ref[0:1, :] += jnp.sum(jnp.where(is_a, dgb * g, 0.0), axis=0, keepdims=True)
        accg_ref[1:2, :] += jnp.sum(jnp.where(is_a, dpre, 0.0), axis=0, keepdims=True)
        o_ref[:, 0:CONV_CH] = dx.astype(o_ref.dtype)
        o_ref[:, CONV_CH:CONV_CH + DN_W] = dz_ref[...].astype(o_ref.dtype)
        o_ref[:, CONV_CH + DN_W:DN_COLS] = dba.astype(o_ref.dtype)

    tok = lambda w: pl.BlockSpec((tm, w), lambda i: (tile(i), 0))
    return pl.pallas_call(
        body, name="dn_prep_bwd", grid=(nt,),
        in_specs=_dn_prep_specs(tm, tile) + [tok(DN_W), tok(DN_W), tok(DN_W), tok(128), tok(DN_W)],
        out_specs=[tok(DN_COLS), _full((8, CONV_CH)), _full((8, 128))],
        out_shape=[jax.ShapeDtypeStruct((t, DN_COLS), MXU_DTYPE),
                   jax.ShapeDtypeStruct((8, CONV_CH), F32), jax.ShapeDtypeStruct((8, 128), F32)],
        scratch_shapes=[pltpu.VMEM((HALO + tm, CONV_CH), F32), pltpu.VMEM((tm, CONV_CH), F32),
                        pltpu.VMEM((tm + HALO, CONV_CH), F32), pltpu.VMEM((HALO, CONV_CH), F32)],
        compiler_params=_params("arbitrary"),
    )(pd, pd, pd, conv_w, al_vec, dt_vec, dqn, dkn, dvc, dgc, dz)


def _pad_lanes(v, offset=0):
    return jnp.pad(v.astype(F32), (offset, 128 - offset - v.shape[0]))[None]


class _LocalReducer:
    def start(self, grads):
        return jnp.zeros((8, 128), F32)

    def middle(self, after):
        return jnp.zeros((8, 128), F32)

    def finish(self, after):
        return None


def _local_step(x, p, tgt, sm, w, late, reducer):
    t = x.shape[0]
    tm = min(512, t // 2)
    tm_s = min(512, t // 2)
    tw = min(1024, t // 2)
    tw_ff = min(2048, t // 2)

    attn_cols = ATTN_Q + 2 * ATTN_KV
    w_in_t = w["w_in_t"]
    w_in_t = jnp.pad(w_in_t, ((0, max(0, attn_cols + DN_COLS - w_in_t.shape[0])), (0, 0)))
    wa_t = w_in_t[:attn_cols]
    wd_t = w_in_t[attn_cols:attn_cols + DN_COLS]
    conv_w = w["conv_w"]
    al_vec, dt_vec = _pad_lanes(sm["a_log"], DN_HEADS), _pad_lanes(sm["dt_bias"], DN_HEADS)
    sinks_vec = _pad_lanes(sm["sinks"])
    dn_w = sm["dn_norm"].reshape(1, 128)
    row = lambda v: v.reshape(1, D_MODEL)
    cos, sin = _rope_tables(t)

    u, pa, pd = _inproj(x, row(sm["norm_mix"]), wa_t, wd_t, tm_s)
    ao = _attn_fwd(pa, cos, sin, sinks_vec)
    qn, kn, vc, gc, gr = _dn_prep(pd, conv_w, al_vec, dt_vec, tm_s)
    uu, ww, qg, kd, a_qk, tinv, dlast = _dn_intra(qn, kn, vc, gc, gr)
    dn_out, o, vnew, sall = _dn_scan_fwd(uu, ww, qg, kd, a_qk, dlast, pd, dn_w)
    w_o, late_rest = late(dn_out)
    wo_a, wo_d = w_o[:ATTN_Q], w_o[ATTN_Q:]
    h1 = _oproj(x, ao, dn_out, wo_a, wo_d, tm)
    w = dict(w, **late_rest(h1))
    w_proj = jnp.transpose(w["w_proj4"], (1, 0, 2)).reshape(PLE_DIM, D_MODEL)
    m, r, h2 = _mlp_fwd(h1, row(sm["norm_mlp"]), w["w_up4"], w["w_down"], tw)
    dh2, dh2b, dgp, dpp, n3, pb, acc_ple = _ple_loss(h2, p, tgt, row(sm["norm_ple"]), row(sm["norm_final"]),
                                                     w["w_gate"], w_proj, tm_s)
    g_w_gate = _wgrad(n3, dgp, "wgrad_gate", D_MODEL, D_MODEL, tw)
    g_w_proj = _wgrad(pb, dpp, "wgrad_proj", PLE_DIM, D_MODEL, tw)
    da, dh1, dh1b, dao, ddn, acc_mlp = _mlp_bwd(dh2, dh2b, r, h1, row(sm["norm_mlp"]), w["w_up4"], w["w_down"],
                                                wo_a, wo_d, tm)
    g_w_up4 = _wgrad(m, da, "wgrad_up", D_MODEL, FF_BLOCK, tw_ff, stacked=True)
    g_w_down = _wgrad(r, dh2b, "wgrad_down", FF_BLOCK, D_MODEL, tw_ff,
                      prep=lambda rv: jnp.square(rv.astype(F32)).astype(MXU_DTYPE))
    g_w_o = _wgrad_cat([ao, dn_out], [dh1b], "wgrad_o", tw)
    early = dict(w_up4=g_w_up4, w_down=g_w_down, w_gate=g_w_gate, w_proj=g_w_proj, w_o=g_w_o)
    dep = reducer.start(early)
    dz, du, dw, dqg, dkd, da_qk, ddlast, acc_dn = _dn_scan_bwd(ddn, o, vnew, sall, ww, qg, kd, a_qk, dlast, pd, dn_w,
                                                               dep)
    dep = reducer.middle(du)
    dqn, dkn, dvc, dgc = _dn_intra_bwd(qn, kn, vc, gc, gr, tinv, a_qk, du, dw, dqg, dkd, da_qk, ddlast, dlast, dep)
    d_dn, acc_conv, acc_gate = _dn_prep_bwd(pd, conv_w, al_vec, dt_vec, dqn, dkn, dvc, dgc, dz, tm_s)
    dq, dk, dv, acc_attn = _attn_bwd(pa, cos, sin, sinks_vec, dao)
    reducer.finish(dq)
    wq_t, wk_t, wv_t = wa_t[:ATTN_Q], wa_t[ATTN_Q:ATTN_Q + ATTN_KV], wa_t[ATTN_Q + ATTN_KV:]
    dx, acc_mix = _inproj_bwd(x, dh1, row(sm["norm_mix"]), [dq, dk, dv, d_dn], [wq_t, wk_t, wv_t, wd_t], tm_s)

    g_w_in_t = _wgrad_cat([dq, dk, dv, d_dn], [u], "wgrad_in", tw)
    grads = dict(early, w_in_t=g_w_in_t)
    sums = dict(loss=acc_ple[2, 0], norm_final=acc_ple[0], norm_ple=acc_ple[1], norm_mlp=acc_mlp[0], norm_mix=acc_mix[0],
                dn_norm=acc_dn[0], sinks=acc_attn[0, :ATTN_HEADS], a_log=acc_gate[0, DN_HEADS:2 * DN_HEADS],
                dt_bias=acc_gate[1, DN_HEADS:2 * DN_HEADS], conv_w=acc_conv[:DN_CONV])
    return sums, dx, grads


MESH = pl.DeviceIdType.MESH
ANY = pl.BlockSpec(memory_space=pl.ANY)
N_CHIPS = 4
N_DEV = 8


def _place():
    x, y, c = lax.axis_index("x"), lax.axis_index("y"), lax.axis_index("c")
    chips = [(1 - x, y), (x, 1 - y), (1 - x, 1 - y)]
    return x, y, c, chips


CAST_ROWS = 256


def _gather_weights(shards, conv_s, casts):
    n, m = len(shards), len(casts)
    per = 7
    cast_cols = max(a.shape[1] for a in casts)
    pieces = [(a, r0) for a, arr in enumerate(casts) for r0 in range(0, arr.shape[0], CAST_ROWS)]
    assert all(arr.shape[0] % CAST_ROWS == 0 and arr.shape[1] % 128 == 0 for arr in casts)

    def body(*refs):
        in_refs, conv_ref, cast_in = refs[:n], refs[n], refs[n + 1:n + 1 + m]
        refs = refs[n + 1 + m:]
        out_refs, conv_out, cast_out = refs[:n], refs[n], refs[n + 1:n + 1 + m]
        send_sems, recv_sems, f32_buf, bf16_buf, cast_sems = refs[n + 1 + m:]
        x, y, c, chips = _place()

        def piece(i, store):
            a, r0 = pieces[i]
            cols, slot = cast_in[a].shape[1], i % 2
            if store:
                return pltpu.make_async_copy(bf16_buf.at[slot, :, pl.ds(0, cols)],
                                             cast_out[a].at[pl.ds(r0, CAST_ROWS), :], cast_sems.at[2 + slot])
            return pltpu.make_async_copy(cast_in[a].at[pl.ds(r0, CAST_ROWS), :],
                                         f32_buf.at[slot, :, pl.ds(0, cols)], cast_sems.at[slot])

        def cast_all():
            piece(0, False).start()
            for i in range(len(pieces)):
                if i + 1 < len(pieces):
                    piece(i + 1, False).start()
                piece(i, False).wait()
                if i >= 2:
                    piece(i - 2, True).wait()
                cols = cast_in[pieces[i][0]].shape[1]
                bf16_buf[i % 2, :, pl.ds(0, cols)] = f32_buf[i % 2, :, pl.ds(0, cols)].astype(BF16)
                piece(i, True).start()
            for i in range(max(0, len(pieces) - 2), len(pieces)):
                piece(i, True).wait()

        sibling = (x, y, 1 - c)

        def blk(a, px, py, pc):
            hr = in_refs[a].shape[0] // 2
            return out_refs[a].at[2 * px + py, pl.ds(pc * hr, hr), :]

        def mine(a):
            hr = in_refs[a].shape[0] // 2
            return in_refs[a].at[pl.ds(c * hr, hr), :]

        def rcopy(a, k, block, to, src=None):
            return pltpu.make_async_remote_copy(
                src_ref=blk(a, *block) if src is None else src, dst_ref=blk(a, *block),
                send_sem=send_sems.at[per * a + k], recv_sem=recv_sems.at[per * a + k],
                device_id=to, device_id_type=MESH)

        def whole(a, to):
            return pltpu.make_async_remote_copy(
                src_ref=in_refs[a], dst_ref=out_refs[a].at[2 * x + y],
                send_sem=send_sems.at[per * a], recv_sem=recv_sems.at[per * a], device_id=to, device_id_type=MESH)

        def ccopy(j, to):
            return pltpu.make_async_remote_copy(
                src_ref=conv_ref, dst_ref=conv_out.at[2 * x + y],
                send_sem=send_sems.at[per * n + j], recv_sem=recv_sems.at[per * n + j],
                device_id=to, device_id_type=MESH)

        started = []
        for a in range(n):
            first = [whole(a, sibling)]
            first += [rcopy(a, 1 + j, (x, y, c), (*chip, c), src=mine(a)) for j, chip in enumerate(chips)]
            for cp in first:
                cp.start()
            started += first
        conv_sends = [ccopy(j, (*chip, c)) for j, chip in enumerate(chips)] + [ccopy(3, sibling)]
        for cp in conv_sends:
            cp.start()
        started += conv_sends
        cast_all()
        for a in range(n):
            for j, chip in enumerate(chips):
                rcopy(a, 1 + j, (*chip, c), (x, y, c)).wait_recv()
                fwd = rcopy(a, 4 + j, (*chip, c), sibling)
                fwd.start()
                started.append(fwd)
        for a in range(n):
            whole(a, sibling).wait_recv()
            for j, chip in enumerate(chips):
                rcopy(a, 4 + j, (*chip, 1 - c), (x, y, c)).wait_recv()
        for j, chip in enumerate(chips + [(x, y)]):
            pltpu.make_async_remote_copy(
                src_ref=conv_ref, dst_ref=conv_out.at[2 * chip[0] + chip[1]],
                send_sem=send_sems.at[per * n + j], recv_sem=recv_sems.at[per * n + j],
                device_id=sibling, device_id_type=MESH).wait_recv()
        for cp in started:
            cp.wait_send()

    nsem = per * n + 4
    out_shape = [jax.ShapeDtypeStruct((N_CHIPS,) + s.shape, s.dtype) for s in shards]
    out_shape.append(jax.ShapeDtypeStruct((N_CHIPS,) + conv_s.shape, conv_s.dtype))
    out_shape += [jax.ShapeDtypeStruct(a.shape, BF16) for a in casts]
    res = pl.pallas_call(
        body, name="gather_weights", in_specs=[ANY] * (n + 1 + m), out_specs=[ANY] * (n + 1 + m), out_shape=out_shape,
        scratch_shapes=[pltpu.SemaphoreType.DMA((nsem,)), pltpu.SemaphoreType.DMA((nsem,)),
                        pltpu.VMEM((2, CAST_ROWS, cast_cols), F32), pltpu.VMEM((2, CAST_ROWS, cast_cols), BF16),
                        pltpu.SemaphoreType.DMA((4,))],
    )(*shards, conv_s, *casts)
    return res[:n], res[n], res[n + 1:]


HBM = pl.BlockSpec(memory_space=pltpu.HBM)
SEM = pl.BlockSpec(memory_space=pltpu.SEMAPHORE)
EFFECT = pltpu.SideEffectType.DATAFLOW_SIDE_EFFECTING
LATE_COPIES = 7


def _late_copies(in_refs, land_refs, send_sems, recv_sems, only=None):
    x, y, c, chips = _place()
    sends, arrivals = [], []
    for a, (src, land) in enumerate(zip(in_refs, land_refs)):
        if only is not None and a not in only:
            continue
        hr = src.shape[0] // 2
        base = LATE_COPIES * a

        def cp(src_ref, dst_ref, s_idx, r_idx, to):
            return pltpu.make_async_remote_copy(src_ref=src_ref, dst_ref=dst_ref, send_sem=send_sems.at[base + s_idx],
                                                recv_sem=recv_sems.at[base + r_idx], device_id=to, device_id_type=MESH)

        sends.append(cp(src, land.at[2 * x + y], 0, 0, (x, y, 1 - c)))
        arrivals.append(cp(src, land.at[2 * x + y], 0, 0, (x, y, 1 - c)))
        for j, chip in enumerate(chips):
            for pc in range(2):
                half = src.at[pl.ds(c * hr, hr), :]
                sends.append(cp(half, land.at[2 * x + y, pl.ds(c * hr, hr), :], 1 + 2 * j + pc, 1 + 2 * j + c, (*chip, pc)))
                arrivals.append(cp(half, land.at[2 * chip[0] + chip[1], pl.ds(pc * hr, hr), :], 1 + 2 * j + pc,
                                   1 + 2 * j + pc, (*chip, pc)))
    return sends, arrivals


def _copies_start(name, build, nsem, srcs, land_shapes, after):
    n = len(srcs)

    def body(*refs):
        sends, _ = build(refs[:n], refs[n:2 * n], refs[2 * n + 1], refs[2 * n + 2])
        for cp in sends:
            cp.start()
        refs[-1][...] = jnp.zeros_like(refs[-1])

    lands = [pltpu.with_memory_space_constraint(lax.empty(s.shape, s.dtype), pltpu.HBM) for s in land_shapes]
    ins = [pltpu.with_memory_space_constraint(s, pltpu.HBM) for s in srcs]
    out = pl.pallas_call(
        body, name=name,
        out_shape=(pltpu.SemaphoreType.DMA((nsem,)), pltpu.SemaphoreType.DMA((nsem,)),
                   *[pltpu.HBM(s.shape, s.dtype) for s in srcs], *[pltpu.HBM(s.shape, s.dtype) for s in land_shapes],
                   jax.ShapeDtypeStruct((8, 128), F32)),
        in_specs=[HBM] * (2 * n) + [ANY],
        out_specs=(SEM, SEM, *[HBM] * (2 * n), pl.BlockSpec(memory_space=pltpu.VMEM)),
        input_output_aliases={i: 2 + i for i in range(2 * n)},
        compiler_params=pltpu.CompilerParams(has_side_effects=EFFECT),
    )(*ins, *lands, after)
    return out[0], out[1], out[2:2 + n], out[2 + n:2 + 2 * n], out[-1]


def _copies_wait(name, build, started, after):
    send_sems, recv_sems, srcs, lands, _ = started
    n = len(srcs)

    def body(*refs):
        sends, arrivals = build(refs[:n], refs[n:2 * n], refs[2 * n], refs[2 * n + 1])
        for cp in sends:
            cp.wait_send()
        for cp in arrivals:
            cp.wait_recv()

    out = pl.pallas_call(
        body, name=name,
        out_shape=(*[pltpu.HBM(s.shape, s.dtype) for s in srcs], *[pltpu.HBM(l.shape, l.dtype) for l in lands]),
        in_specs=[HBM] * (2 * n) + [SEM, SEM, ANY],
        out_specs=tuple([HBM] * (2 * n)),
        input_output_aliases={i: i for i in range(2 * n)},
        compiler_params=pltpu.CompilerParams(has_side_effects=EFFECT),
    )(*srcs, *lands, send_sems, recv_sems, after)
    return out[:n], out[n:]


def _exchange_copies(g_refs, got_refs, send_sems, recv_sems):
    x, y, c, _ = _place()
    sends, arrivals = [], []
    for a, (g, got) in enumerate(zip(g_refs, got_refs)):
        hr = g.shape[1] // 2
        cp = pltpu.make_async_remote_copy(
            src_ref=g.at[:, pl.ds((1 - c) * hr, hr), :], dst_ref=got, send_sem=send_sems.at[a],
            recv_sem=recv_sems.at[a], device_id=(x, y, 1 - c), device_id_type=MESH)
        sends.append(cp)
        arrivals.append(cp)
    return sends, arrivals


def _scatter_copies(s_refs, got_refs, send_sems, recv_sems):
    x, y, c, chips = _place()
    sends, arrivals = [], []
    for a, (s16, got) in enumerate(zip(s_refs, got_refs)):
        for j, chip in enumerate(chips):
            cp = pltpu.make_async_remote_copy(
                src_ref=s16.at[2 * chip[0] + chip[1]], dst_ref=got.at[j], send_sem=send_sems.at[3 * a + j],
                recv_sem=recv_sems.at[3 * a + j], device_id=(*chip, c), device_id_type=MESH)
            sends.append(cp)
            arrivals.append(cp)
    return sends, arrivals


def _share_halves(name, bufs, dep):
    n = len(bufs)

    def body(*refs):
        out_refs = refs[n + 1:2 * n + 1]
        send_sems, recv_sems = refs[2 * n + 1:]
        x, y, c, _ = _place()
        remote = [pltpu.make_async_remote_copy(
            src_ref=out_refs[a].at[c], dst_ref=out_refs[a].at[c], send_sem=send_sems.at[a], recv_sem=recv_sems.at[a],
            device_id=(x, y, 1 - c), device_id_type=MESH) for a in range(n)]
        for cp in remote:
            cp.start()
        for a in range(n):
            pltpu.make_async_remote_copy(
                src_ref=out_refs[a].at[c], dst_ref=out_refs[a].at[1 - c], send_sem=send_sems.at[a],
                recv_sem=recv_sems.at[a], device_id=(x, y, 1 - c), device_id_type=MESH).wait_recv()
        for cp in remote:
            cp.wait_send()

    return pl.pallas_call(
        body, name=name, in_specs=[ANY] * (n + 1), out_specs=[ANY] * n,
        out_shape=[jax.ShapeDtypeStruct(b.shape, b.dtype) for b in bufs],
        input_output_aliases={a: a for a in range(n)},
        scratch_shapes=[pltpu.SemaphoreType.DMA((n,)), pltpu.SemaphoreType.DMA((n,))],
    )(*bufs, dep)


SMALL_ROWS, SMALL_COLS = 16, CONV_CH
DN_NORM_LANE = 128


def _allreduce_small(block):
    m_per, ncol = block.shape

    def body(x_ref, sum_ref, all_ref, send_sems, recv_sems, local_sem):
        x, y, c, chips = _place()
        me, sibling = (x, y, c), (x, y, 1 - c)

        def rows(px, py, pc):
            return all_ref.at[pl.ds((4 * px + 2 * py + pc) * m_per, m_per), :]

        def copy(k, block_of, to, src=None):
            return pltpu.make_async_remote_copy(
                src_ref=rows(*block_of) if src is None else src, dst_ref=rows(*block_of),
                send_sem=send_sems.at[k], recv_sem=recv_sems.at[k], device_id=to, device_id_type=MESH)

        mine = pltpu.make_async_copy(x_ref, rows(*me), local_sem)
        mine.start()
        first = [copy(0, me, sibling, src=x_ref)]
        first += [copy(1 + j, me, (*chip, c), src=x_ref) for j, chip in enumerate(chips)]
        for cp in first:
            cp.start()
        passed = [copy(4 + j, (*chip, c), sibling) for j, chip in enumerate(chips)]
        for j, chip in enumerate(chips):
            copy(1 + j, (*chip, c), me).wait_recv()
            passed[j].start()
        copy(0, sibling, me).wait_recv()
        for j, chip in enumerate(chips):
            copy(4 + j, (*chip, 1 - c), me).wait_recv()
        for cp in first + passed:
            cp.wait_send()
        mine.wait()
        total = all_ref[0:m_per, :]
        for d in range(1, N_DEV):
            total = total + all_ref[d * m_per:(d + 1) * m_per, :]
        sum_ref[...] = total

    vm = pl.BlockSpec(memory_space=pltpu.VMEM)
    return pl.pallas_call(
        body, name="allreduce_small", in_specs=[vm], out_specs=vm,
        out_shape=jax.ShapeDtypeStruct((m_per, ncol), F32),
        scratch_shapes=[pltpu.VMEM((N_DEV * m_per, ncol), F32), pltpu.SemaphoreType.DMA((7,)),
                        pltpu.SemaphoreType.DMA((7,)), pltpu.SemaphoreType.DMA],
    )(block)


def _row_tile(rows, cols):
    tile = rows
    while tile * cols * 4 > (1 << 20) and tile % 16 == 0:
        tile //= 2
    return tile


def _elementwise(fn, name, ins, out_dtypes, dep):
    rows, cols = ins[0].shape
    tile = _row_tile(rows, cols)

    def body(*refs):
        outs = fn(*[r[...] for r in refs[:len(ins)]])
        for o_ref, o in zip(refs[len(ins) + 1:], outs):
            o_ref[...] = o.astype(o_ref.dtype)

    if tile * cols * 4 > (1 << 21) and cols % 512 == 0:
        spec = pl.BlockSpec((rows, 256), lambda i: (0, i))
        steps = cols // 256
    else:
        spec = pl.BlockSpec((tile, cols), lambda i: (i, 0))
        steps = rows // tile
    return pl.pallas_call(
        body, name=name, grid=(steps,), in_specs=[spec] * len(ins) + [pl.BlockSpec(memory_space=pl.ANY)],
        out_specs=[spec] * len(out_dtypes),
        out_shape=[jax.ShapeDtypeStruct((rows, cols), d) for d in out_dtypes],
        compiler_params=_params("parallel"),
    )(*ins, dep)


def _adamw_tile(w, g, m, v):
    m = ADAM_B1 * m + (1.0 - ADAM_B1) * g
    v = ADAM_B2 * v + (1.0 - ADAM_B2) * jnp.square(g)
    m_hat = m / (1.0 - ADAM_B1 ** ADAM_STEP)
    v_hat = v / (1.0 - ADAM_B2 ** ADAM_STEP)
    delta = -ADAM_LR * (m_hat / (jnp.sqrt(v_hat) + ADAM_EPS) + ADAM_WD * w)
    return delta, m, v


def _adamw(name, w, g, m, v, dep):
    return _elementwise(_adamw_tile, name, [w, g, m, v], [F32, F32, F32], dep)


def _chip_sum(name, g4, got, place):
    nchip, hr, cols = got.shape
    tile = _row_tile(hr, cols)
    nblk = hr // tile

    def body(pl_ref, g_ref, o_ref, s32_ref, s16_ref):
        s = g_ref[...] + o_ref[...]
        s16_ref[...] = s.astype(BF16)

        @pl.when(pl.program_id(1) == pl_ref[0])
        def _():
            s32_ref[...] = s

    spec = pl.BlockSpec((None, tile, cols), lambda i, k, pr: (k, i, 0))
    return pl.pallas_call(
        body, name=name,
        grid_spec=pltpu.PrefetchScalarGridSpec(
            num_scalar_prefetch=1, grid=(nblk, nchip),
            in_specs=[pl.BlockSpec((None, tile, cols), lambda i, k, pr: (k, pr[1] * nblk + i, 0)), spec],
            out_specs=[pl.BlockSpec((tile, cols), lambda i, k, pr: (i, 0)), spec]),
        out_shape=[jax.ShapeDtypeStruct((hr, cols), F32), jax.ShapeDtypeStruct(got.shape, BF16)],
        compiler_params=_params("parallel", "arbitrary"),
    )(place, g4, got)


def _mesh_sum(name, s32, got, place):
    hr, cols = s32.shape
    tile = _row_tile(hr, cols)

    def body(pl_ref, own_ref, g0_ref, g1_ref, g2_ref, o_ref):
        o_ref[...] = ((own_ref[...] + g0_ref[...].astype(F32)) + g1_ref[...].astype(F32)) + g2_ref[...].astype(F32)

    slab = lambda j: pl.BlockSpec((None, tile, cols), lambda i, pr: (j, i, 0))
    return pl.pallas_call(
        body, name=name,
        grid_spec=pltpu.PrefetchScalarGridSpec(
            num_scalar_prefetch=1, grid=(hr // tile,),
            in_specs=[pl.BlockSpec((tile, cols), lambda i, pr: (i, 0)), slab(0), slab(1), slab(2)],
            out_specs=pl.BlockSpec((None, tile, cols), lambda i, pr: (pr[1], i, 0))),
        out_shape=jax.ShapeDtypeStruct((2, hr, cols), F32),
        compiler_params=_params("parallel"),
    )(place, s32, got, got, got)


def _place_operand():
    return jnp.stack([2 * lax.axis_index("x") + lax.axis_index("y"), lax.axis_index("c")]).astype(jnp.int32)


W_IN_ROWS = 720
W_IN_GATHER_ROWS = 736
BF16_TILE_ROWS = 16


def _join_w_in(blocks):
    rows, t = D_IN // N_CHIPS, BF16_TILE_ROWS
    first = [rows * k // t * t for k in range(N_CHIPS)]
    parts = []
    for k in range(N_CHIPS):
        lo = t if k else 0
        if k + 1 < N_CHIPS:
            hi = first[k + 1] - first[k]
            assert rows * (k + 1) <= first[k + 1] + t and hi + t <= W_IN_GATHER_ROWS
            parts += [blocks[k, lo:hi], blocks[k, hi:hi + t] + blocks[k + 1, :t]]
        else:
            parts.append(blocks[k, lo:])
    return jnp.concatenate(parts, axis=0)


def _per_chip(name, g):
    if name == "w_in_t":
        rows = D_IN // N_CHIPS
        return jnp.stack([lax.slice_in_dim(g, rows * k, rows * k + W_IN_ROWS) for k in range(N_CHIPS)])
    if name == "w_proj":
        return jnp.transpose(g.reshape(PLE_DIM, N_CHIPS, D_MODEL // N_CHIPS), (1, 0, 2))
    if name == "w_up4":
        return g
    return g.reshape(N_CHIPS, g.shape[0] // N_CHIPS, g.shape[1])


class _EarlyReducer:
    def __init__(self, tag):
        self.tag = tag

    def start(self, grads):
        self.names = list(grads)
        self.place = _place_operand()
        slabs = [_per_chip(k, grads[k]) for k in self.names]
        halves = [jax.ShapeDtypeStruct((s.shape[0], s.shape[1] // 2, s.shape[2]), F32) for s in slabs]
        self.a = _copies_start(self.tag + "exchange_start", _exchange_copies, len(slabs), slabs, halves,
                               slabs[0][0, :8, :128])
        return self.a[-1]

    def middle(self, after):
        slabs, got = _copies_wait(self.tag + "exchange_wait", _exchange_copies, self.a, after)
        self.sums = [_chip_sum(self.tag + "chip_sum_" + k, s, g, self.place) for k, s, g in zip(self.names, slabs, got)]
        s16 = [s[1] for s in self.sums]
        lands = [jax.ShapeDtypeStruct((3,) + s.shape[1:], BF16) for s in s16]
        self.b = _copies_start(self.tag + "scatter_start", _scatter_copies, 3 * len(s16), s16, lands,
                               self.sums[0][0][:8, :128])
        return self.b[-1]

    def finish(self, after):
        _, got = _copies_wait(self.tag + "scatter_wait", _scatter_copies, self.b, after)
        self.bufs = {k: _mesh_sum(self.tag + "mesh_sum_" + k, s[0], g, self.place)
                     for k, s, g in zip(self.names, self.sums, got)}


def kernel(x, p, norm_mix, w_in, conv_w, a_log, dt_bias, dn_norm, sinks, w_o, norm_mlp, w_up, w_down, norm_ple, w_ple_gate, w_ple_proj, norm_final, loss_target, m_norm_mix, m_w_in, m_conv_w, m_a_log, m_dt_bias, m_dn_norm, m_sinks, m_w_o, m_norm_mlp, m_w_up, m_w_down, m_norm_ple, m_w_ple_gate, m_w_ple_proj, m_norm_final, v_norm_mix, v_w_in, v_conv_w, v_a_log, v_dt_bias, v_dn_norm, v_sinks, v_w_o, v_norm_mlp, v_w_up, v_w_down, v_norm_ple, v_w_ple_gate, v_w_ple_proj, v_norm_final):
    chip = 2 * lax.axis_index("x") + lax.axis_index("y")
    big = dict(w_in=w_in[0], w_o=w_o[0], w_up=w_up[0], w_down=w_down[0], w_gate=w_ple_gate[0], w_proj=w_ple_proj[0])
    big_m = dict(w_in=m_w_in[0], w_o=m_w_o[0], w_up=m_w_up[0], w_down=m_w_down[0], w_gate=m_w_ple_gate[0], w_proj=m_w_ple_proj[0])
    big_v = dict(w_in=v_w_in[0], w_o=v_w_o[0], w_up=v_w_up[0], w_down=v_w_down[0], w_gate=v_w_ple_gate[0], w_proj=v_w_ple_proj[0])
    names = list(big)

    rows_in = D_IN // N_CHIPS
    chip_index = 2 * lax.axis_index("x") + lax.axis_index("y")
    w_in_shard_t = lax.dynamic_update_slice(jnp.zeros((W_IN_GATHER_ROWS, D_MODEL), BF16), big["w_in"].T.astype(BF16),
                                            ((rows_in * chip_index) % BF16_TILE_ROWS, 0))
    late_names = names[1:]
    (w_in_all,), conv_all, late_shards = _gather_weights([w_in_shard_t], conv_w[0], [big[k] for k in late_names])
    gather = _copies_start("gather_start", _late_copies, LATE_COPIES * len(late_shards), late_shards,
                           [jax.ShapeDtypeStruct((N_CHIPS,) + s.shape, BF16) for s in late_shards], w_in_all)
    token = gather[-1]
    w = dict(w_in_t=_join_w_in(w_in_all),
             conv_w=jnp.transpose(conv_all, (1, 0, 2)).reshape(DN_CONV, CONV_CH))
    sm = dict(norm_mix=norm_mix[0] + token[0, 0], a_log=a_log[0], dt_bias=dt_bias[0], dn_norm=dn_norm[0],
              sinks=sinks[0], norm_mlp=norm_mlp[0], norm_ple=norm_ple[0], norm_final=norm_final)

    def late(after):
        first = functools.partial(_late_copies, only=(0,))
        srcs, lands = _copies_wait("gather_wait_o", first, gather, after)

        def rest(after2):
            others = functools.partial(_late_copies, only=tuple(range(1, len(late_names))))
            gw = dict(zip(late_names, _copies_wait("gather_wait_rest", others, gather[:2] + (srcs, lands, None), after2)[1]))
            return dict(w_up4=gw["w_up"], w_down=gw["w_down"].reshape(D_FF, D_MODEL),
                        w_gate=gw["w_gate"].reshape(D_MODEL, D_MODEL), w_proj4=gw["w_proj"])

        return lands[0].reshape(D_MODEL, D_MODEL), rest

    reducer = _EarlyReducer("early_")
    sums, grad_x, g = _local_step(x[0], p[0, 0], loss_target[0], sm, w, late, reducer)

    last = _EarlyReducer("last_")
    dep_a = last.start({"w_in_t": g["w_in_t"]})

    row = lambda v: jnp.pad(v, (0, SMALL_COLS - v.shape[0]))

    def misc_row(al, dtb, sk, dnn, rest):
        head = jnp.concatenate([al, dtb, sk])
        return row(jnp.concatenate([head, jnp.zeros((DN_NORM_LANE - head.shape[0],), F32), dnn, rest]))

    misc = misc_row(sums["a_log"], sums["dt_bias"], sums["sinks"], sums["dn_norm"], sums["loss"].reshape(1))
    small = jnp.concatenate([sums["conv_w"], jnp.stack([row(sums["norm_mix"]), row(sums["norm_mlp"]), row(sums["norm_ple"]),
                                                        row(sums["norm_final"]), misc]),
                             jnp.zeros((SMALL_ROWS - 9, SMALL_COLS), F32)], axis=0)
    tot = _allreduce_small(small + dep_a[0, 0])
    dep_b = last.middle(tot)
    grad_key = dict(w_o="w_o", w_up="w_up4", w_down="w_down", w_gate="w_gate", w_proj="w_proj")
    full = _share_halves("share_halves", [reducer.bufs[grad_key[k]] for k in late_names], dep_b)
    red = {k: f.reshape(-1, f.shape[-1]) for k, f in zip(late_names, full)}
    loss = tot[8, 256]
    ncw = CONV_CH // N_CHIPS

    def pack(cw, nmix, nmlp, nple, nfin, al, dtb, sk, dnn):
        misc_p = misc_row(al, dtb, sk, dnn, jnp.zeros((0,), F32))
        cw_p = jnp.pad(cw, ((0, 0), (0, SMALL_COLS - ncw)))
        return jnp.concatenate([cw_p, jnp.stack([row(nmix), row(nmlp), row(nple), row(nfin), misc_p]),
                                jnp.zeros((SMALL_ROWS - 9, SMALL_COLS), F32)], axis=0)

    def unpack(buf):
        return dict(conv_w=buf[0:4, :ncw][None], norm_mix=buf[4, :D_MODEL][None], norm_mlp=buf[5, :D_MODEL][None],
                    norm_ple=buf[6, :D_MODEL][None], norm_final=buf[7, :D_MODEL], a_log=buf[8, 0:4][None],
                    dt_bias=buf[8, 4:8][None], sinks=buf[8, 8:16][None], dn_norm=buf[8, 128:256][None])

    g_conv_shard = lax.dynamic_slice(tot[0:4], (0, chip * ncw), (DN_CONV, ncw))
    g_small = pack(g_conv_shard, tot[4, :D_MODEL], tot[5, :D_MODEL], tot[6, :D_MODEL], tot[7, :D_MODEL],
                   tot[8, 0:4], tot[8, 4:8], tot[8, 8:16], tot[8, 128:256])
    w_small = pack(conv_w[0], norm_mix[0], norm_mlp[0], norm_ple[0], norm_final, a_log[0], dt_bias[0], sinks[0], dn_norm[0])
    m_small = pack(m_conv_w[0], m_norm_mix[0], m_norm_mlp[0], m_norm_ple[0], m_norm_final, m_a_log[0], m_dt_bias[0],
                   m_sinks[0], m_dn_norm[0])
    v_small = pack(v_conv_w[0], v_norm_mix[0], v_norm_mlp[0], v_norm_ple[0], v_norm_final, v_a_log[0], v_dt_bias[0],
                   v_sinks[0], v_dn_norm[0])

    ref_name = dict(w_in="w_in", w_o="w_o", w_up="w_up", w_down="w_down", w_gate="w_ple_gate", w_proj="w_ple_proj")
    out_g, out_d, out_m, out_v = {}, {}, {}, {}

    def update(k, dep):
        d_k, m_k, v_k = _adamw("adamw_" + k, big[k], red[k], big_m[k], big_v[k], dep)
        out_g[ref_name[k]], out_d[ref_name[k]] = red[k][None], d_k[None]
        out_m[ref_name[k]], out_v[ref_name[k]] = m_k[None], v_k[None]
        return d_k

    for k in late_names:
        done = update(k, dep_b)
    small_out = _adamw("adamw_small", w_small, g_small, m_small, v_small, dep_b)
    d_s, m_s, v_s = (unpack(b) for b in small_out)
    g_s = unpack(g_small)
    for src, dst in ((g_s, out_g), (d_s, out_d), (m_s, out_m), (v_s, out_v)):
        dst.update(src)
    last.finish(done + small_out[0][0:1, 0:1])
    (w_in_full,) = _share_halves("share_halves_w_in", [last.bufs["w_in_t"]], dep_b)
    g_t = w_in_full.reshape(W_IN_ROWS, D_MODEL)[:D_IN // N_CHIPS]
    d_t, m_t, v_t = _adamw("adamw_w_in", big["w_in"].T, g_t, big_m["w_in"].T, big_v["w_in"].T, dep_b)
    out_g["w_in"], out_d["w_in"], out_m["w_in"], out_v["w_in"] = g_t.T[None], d_t.T[None], m_t.T[None], v_t.T[None]
    order = ["norm_mix", "w_in", "conv_w", "a_log", "dt_bias", "dn_norm", "sinks", "w_o", "norm_mlp", "w_up", "w_down",
             "norm_ple", "w_ple_gate", "w_ple_proj", "norm_final"]
    return (loss, grad_x[None], *[out_g[k] for k in order], *[out_d[k] for k in order],
            *[out_m[k] for k in order], *[out_v[k] for k in order])
```

```python
import functools

import jax
import jax.numpy as jnp
from jax import lax
from jax.experimental import pallas as pl
from jax.experimental.pallas import tpu as pltpu

F32 = jnp.float32
BF16 = jnp.bfloat16
MXU_DTYPE = jnp.bfloat16
HI = lax.Precision.HIGHEST

D_MODEL = 1024
PLE_DIM = 256
ATTN_HEADS = 8
ATTN_KV_HEADS = 2
ATTN_GROUPS = ATTN_HEADS // ATTN_KV_HEADS
ATTN_HEAD_DIM = 64
ATTN_BLOCK = 128
ROPE_THETA = 10000.0
DN_HEADS = 4
DN_HEAD_DIM = 128
DN_CONV = 4
DN_CHUNK = 64
D_FF = 4 * D_MODEL
EPS = 1e-6
ATTN_Q = ATTN_HEADS * ATTN_HEAD_DIM
ATTN_KV = ATTN_KV_HEADS * ATTN_HEAD_DIM
DN_W = DN_HEADS * DN_HEAD_DIM
CONV_CH = 3 * DN_W
D_IN = ATTN_Q + 2 * ATTN_KV + 4 * DN_W + 2 * DN_HEADS
DN_COLS = 4 * DN_W + 128
DN_SCALE = DN_HEAD_DIM ** -0.5
ATTN_SCALE = ATTN_HEAD_DIM ** -0.5
FF_BLOCKS = 4
FF_BLOCK = D_FF // FF_BLOCKS

ADAM_LR = 0.001
ADAM_B1 = 0.9
ADAM_B2 = 0.999
ADAM_EPS = 1e-08
ADAM_WD = 0.01
ADAM_STEP = 10

V7X_VMEM_BYTES = 64 * 1024 * 1024
VMEM_LIMIT = 48 * 1024 * 1024

NN = ((1,), (0,))
NT = ((1,), (1,))
TN = ((0,), (0,))


def _dot(a, b, dims=NN, prec=None):
    if a.dtype != b.dtype:
        a, b = a.astype(MXU_DTYPE), b.astype(MXU_DTYPE)
    return lax.dot_general(a, b, (dims, ((), ())), precision=prec, preferred_element_type=F32)


def _sigmoid(x):
    return 1.0 / (1.0 + jnp.exp(-x))


def _softplus(x):
    return jnp.maximum(x, 0.0) + jnp.log(1.0 + jnp.exp(-jnp.abs(x)))


def _params(*sem):
    return pltpu.CompilerParams(dimension_semantics=sem, vmem_limit_bytes=VMEM_LIMIT)


def _rms_fwd(xv, g):
    r = lax.rsqrt(jnp.mean(xv * xv, axis=-1, keepdims=True) + EPS)
    return xv * r * g


def _rms_bwd(xv, g, dn):
    r = lax.rsqrt(jnp.mean(xv * xv, axis=-1, keepdims=True) + EPS)
    xh = xv * r
    dg = jnp.sum(dn * xh, axis=0, keepdims=True)
    dxh = dn * g
    dx = r * (dxh - xh * jnp.mean(dxh * xh, axis=-1, keepdims=True))
    return dx, dg


def _full(shape):
    return pl.BlockSpec(shape, lambda *_: (0,) * len(shape))


def _inproj(x, g_mix, wa_t, wd_t, tm):
    t = x.shape[0]

    def body(x_ref, g_ref, wa_ref, wd_ref, u_ref, pa_ref, pd_ref):
        u = _rms_fwd(x_ref[...], g_ref[...]).astype(MXU_DTYPE)
        u_ref[...] = u
        pa_ref[...] = _dot(u, wa_ref[...], NT)
        pd_ref[...] = _dot(u, wd_ref[...], NT)

    na, nd = wa_t.shape[0], wd_t.shape[0]
    return pl.pallas_call(
        body, name="inproj", grid=(t // tm,),
        in_specs=[pl.BlockSpec((tm, D_MODEL), lambda i: (i, 0)), _full((1, D_MODEL)),
                  _full((na, D_MODEL)), _full((nd, D_MODEL))],
        out_specs=[pl.BlockSpec((tm, D_MODEL), lambda i: (i, 0)), pl.BlockSpec((tm, na), lambda i: (i, 0)),
                   pl.BlockSpec((tm, nd), lambda i: (i, 0))],
        out_shape=[jax.ShapeDtypeStruct((t, D_MODEL), MXU_DTYPE), jax.ShapeDtypeStruct((t, na), F32),
                   jax.ShapeDtypeStruct((t, nd), F32)],
        compiler_params=_params("parallel"),
    )(x, g_mix, wa_t, wd_t)


def _oproj(x, ao, dn, wo_a, wo_d, tm):
    t = x.shape[0]

    def body(x_ref, ao_ref, dn_ref, wa_ref, wd_ref, h_ref):
        h_ref[...] = (x_ref[...] + _dot(ao_ref[...].astype(MXU_DTYPE), wa_ref[...])
                      + _dot(dn_ref[...].astype(MXU_DTYPE), wd_ref[...]))

    half = ao.shape[1]
    return pl.pallas_call(
        body, name="oproj", grid=(t // tm,),
        in_specs=[pl.BlockSpec((tm, D_MODEL), lambda i: (i, 0)), pl.BlockSpec((tm, half), lambda i: (i, 0)),
                  pl.BlockSpec((tm, half), lambda i: (i, 0)), _full((half, D_MODEL)), _full((half, D_MODEL))],
        out_specs=pl.BlockSpec((tm, D_MODEL), lambda i: (i, 0)),
        out_shape=jax.ShapeDtypeStruct((t, D_MODEL), F32),
        compiler_params=_params("parallel"),
    )(x, ao, dn, wo_a, wo_d)


def _mlp_fwd(h1, g_mlp, w_up4, w_down, tm):
    t = h1.shape[0]

    def body(h_ref, g_ref, wu_ref, wd_ref, m_ref, r_ref, h2_ref, acc_ref):
        k = pl.program_id(1)

        @pl.when(k == 0)
        def _():
            m_ref[...] = _rms_fwd(h_ref[...], g_ref[...]).astype(MXU_DTYPE)
            acc_ref[...] = jnp.zeros_like(acc_ref)

        r = jnp.maximum(_dot(m_ref[...], wu_ref[...]), 0.0)
        r_ref[...] = r.astype(MXU_DTYPE)
        s = jnp.square(r).astype(MXU_DTYPE)
        acc_ref[...] += _dot(s, wd_ref[...])

        @pl.when(k == FF_BLOCKS - 1)
        def _():
            h2_ref[...] = h_ref[...] + acc_ref[...]

    return pl.pallas_call(
        body, name="mlp_fwd", grid=(t // tm, FF_BLOCKS),
        in_specs=[pl.BlockSpec((tm, D_MODEL), lambda i, k: (i, 0)), _full((1, D_MODEL)),
                  pl.BlockSpec((None, D_MODEL, FF_BLOCK), lambda i, k: (k, 0, 0)),
                  pl.BlockSpec((FF_BLOCK, D_MODEL), lambda i, k: (k, 0))],
        out_specs=[pl.BlockSpec((tm, D_MODEL), lambda i, k: (i, 0)), pl.BlockSpec((tm, FF_BLOCK), lambda i, k: (i, k)),
                   pl.BlockSpec((tm, D_MODEL), lambda i, k: (i, 0))],
        out_shape=[jax.ShapeDtypeStruct((t, D_MODEL), MXU_DTYPE), jax.ShapeDtypeStruct((t, D_FF), MXU_DTYPE),
                   jax.ShapeDtypeStruct((t, D_MODEL), F32)],
        scratch_shapes=[pltpu.VMEM((tm, D_MODEL), F32)],
        compiler_params=_params("parallel", "arbitrary"),
    )(h1, g_mlp, w_up4, w_down)


def _ple_loss(h2, p, tgt, g_ple, g_fin, w_gate, w_proj, tm):
    t = h2.shape[0]

    def body(h_ref, p_ref, t_ref, gp_ref, gf_ref, wg_ref, wp_ref,
             dh_ref, dhb_ref, dgp_ref, dpp_ref, n3_ref, pb_ref, acc_ref):
        @pl.when(pl.program_id(0) == 0)
        def _():
            acc_ref[...] = jnp.zeros_like(acc_ref)

        h = h_ref[...]
        g_ple_v, g_fin_v = gp_ref[...], gf_ref[...]
        n3 = _rms_fwd(h, g_ple_v).astype(MXU_DTYPE)
        n3_ref[...] = n3
        gate = _sigmoid(_dot(n3, wg_ref[...]))
        pb = p_ref[...].astype(MXU_DTYPE)
        pb_ref[...] = pb
        pp = _dot(pb, wp_ref[...])
        h3 = h + gate * pp
        r4 = lax.rsqrt(jnp.mean(h3 * h3, axis=-1, keepdims=True) + EPS)
        xh4 = h3 * r4
        e = xh4 * g_fin_v - t_ref[...]
        loss = 0.5 * jnp.sum(jnp.mean(e * e, axis=-1, keepdims=True), axis=0, keepdims=True)
        dy = e * (1.0 / D_MODEL)
        dg_fin = jnp.sum(dy * xh4, axis=0, keepdims=True)
        dxh = dy * g_fin_v
        dh3 = r4 * (dxh - xh4 * jnp.mean(dxh * xh4, axis=-1, keepdims=True))
        dpp_ref[...] = (dh3 * gate).astype(MXU_DTYPE)
        dgp = (dh3 * pp * gate * (1.0 - gate)).astype(MXU_DTYPE)
        dgp_ref[...] = dgp
        dn3 = _dot(dgp, wg_ref[...], NT)
        dx, dg_ple = _rms_bwd(h, g_ple_v, dn3)
        dh2 = dh3 + dx
        dh_ref[...] = dh2
        dhb_ref[...] = dh2.astype(MXU_DTYPE)
        acc_ref[0:1, :] += dg_fin
        acc_ref[1:2, :] += dg_ple
        acc_ref[2:3, :] += jnp.broadcast_to(loss, (1, D_MODEL))

    row = lambda w: pl.BlockSpec((tm, w), lambda i: (i, 0))
    return pl.pallas_call(
        body, name="ple_loss", grid=(t // tm,),
        in_specs=[row(D_MODEL), row(PLE_DIM), row(D_MODEL), _full((1, D_MODEL)), _full((1, D_MODEL)),
                  _full((D_MODEL, D_MODEL)), _full((PLE_DIM, D_MODEL))],
        out_specs=[row(D_MODEL), row(D_MODEL), row(D_MODEL), row(D_MODEL), row(D_MODEL), row(PLE_DIM),
                   _full((8, D_MODEL))],
        out_shape=[jax.ShapeDtypeStruct((t, D_MODEL), F32), jax.ShapeDtypeStruct((t, D_MODEL), MXU_DTYPE),
                   jax.ShapeDtypeStruct((t, D_MODEL), MXU_DTYPE), jax.ShapeDtypeStruct((t, D_MODEL), MXU_DTYPE),
                   jax.ShapeDtypeStruct((t, D_MODEL), MXU_DTYPE), jax.ShapeDtypeStruct((t, PLE_DIM), MXU_DTYPE),
                   jax.ShapeDtypeStruct((8, D_MODEL), F32)],
        compiler_params=_params("arbitrary"),
    )(h2, p, tgt, g_ple, g_fin, w_gate, w_proj)


def _mlp_bwd(dh2, dh2b, r, h1, g_mlp, w_up4, w_down, wo_a, wo_d, tm):
    t = h1.shape[0]
    half = wo_a.shape[0]

    def body(dh_ref, dhb_ref, r_ref, h_ref, g_ref, wu_ref, wd_ref, woa_ref, wod_ref,
             da_ref, dh1_ref, dh1b_ref, dao_ref, ddn_ref, acc_ref, dm_ref):
        i, k = pl.program_id(0), pl.program_id(1)

        @pl.when((i == 0) & (k == 0))
        def _():
            acc_ref[...] = jnp.zeros_like(acc_ref)

        @pl.when(k == 0)
        def _():
            dm_ref[...] = jnp.zeros_like(dm_ref)

        ds = _dot(dhb_ref[...], wd_ref[...], NT)
        da = (ds * (2.0 * r_ref[...].astype(F32))).astype(MXU_DTYPE)
        da_ref[...] = da
        dm_ref[...] += _dot(da, wu_ref[...], NT)

        @pl.when(k == FF_BLOCKS - 1)
        def _():
            dx, dg = _rms_bwd(h_ref[...], g_ref[...], dm_ref[...])
            dh1 = dh_ref[...] + dx
            dh1_ref[...] = dh1
            dh1b = dh1.astype(MXU_DTYPE)
            dh1b_ref[...] = dh1b
            dao_ref[...] = _dot(dh1b, woa_ref[...], NT)
            ddn_ref[...] = _dot(dh1b, wod_ref[...], NT)
            acc_ref[0:1, :] += dg

    tok = lambda w: pl.BlockSpec((tm, w), lambda i, k: (i, 0))
    return pl.pallas_call(
        body, name="mlp_bwd", grid=(t // tm, FF_BLOCKS),
        in_specs=[tok(D_MODEL), tok(D_MODEL), pl.BlockSpec((tm, FF_BLOCK), lambda i, k: (i, k)), tok(D_MODEL),
                  _full((1, D_MODEL)), pl.BlockSpec((None, D_MODEL, FF_BLOCK), lambda i, k: (k, 0, 0)),
                  pl.BlockSpec((FF_BLOCK, D_MODEL), lambda i, k: (k, 0)),
                  pl.BlockSpec((half, D_MODEL), lambda i, k: (0, 0)), pl.BlockSpec((half, D_MODEL), lambda i, k: (0, 0))],
        out_specs=[pl.BlockSpec((tm, FF_BLOCK), lambda i, k: (i, k)),
                   tok(D_MODEL), tok(D_MODEL), tok(half), tok(half), pl.BlockSpec((8, D_MODEL), lambda i, k: (0, 0))],
        out_shape=[jax.ShapeDtypeStruct((t, D_FF), MXU_DTYPE),
                   jax.ShapeDtypeStruct((t, D_MODEL), F32), jax.ShapeDtypeStruct((t, D_MODEL), MXU_DTYPE),
                   jax.ShapeDtypeStruct((t, half), F32), jax.ShapeDtypeStruct((t, half), F32),
                   jax.ShapeDtypeStruct((8, D_MODEL), F32)],
        scratch_shapes=[pltpu.VMEM((tm, D_MODEL), F32)],
        compiler_params=_params("arbitrary", "arbitrary"),
    )(dh2, dh2b, r, h1, g_mlp, w_up4, w_down, wo_a, wo_d)


def _inproj_bwd(x, dh1, g_mix, grads, weights, tm):
    t = x.shape[0]
    n = len(grads)

    def body(*refs):
        x_ref, dh_ref, g_ref = refs[:3]
        g_refs, w_refs = refs[3:3 + n], refs[3 + n:3 + 2 * n]
        dx_ref, acc_ref = refs[3 + 2 * n:]

        @pl.when(pl.program_id(0) == 0)
        def _():
            acc_ref[...] = jnp.zeros_like(acc_ref)

        du = _dot(g_refs[0][...], w_refs[0][...])
        for j in range(1, n):
            du += _dot(g_refs[j][...], w_refs[j][...])
        dx, dg = _rms_bwd(x_ref[...], g_ref[...], du)
        dx_ref[...] = dh_ref[...] + dx
        acc_ref[0:1, :] += dg

    tok = lambda w: pl.BlockSpec((tm, w), lambda i: (i, 0))
    return pl.pallas_call(
        body, name="inproj_bwd", grid=(t // tm,),
        in_specs=[tok(D_MODEL), tok(D_MODEL), _full((1, D_MODEL))] + [tok(g.shape[1]) for g in grads]
                 + [_full(w.shape) for w in weights],
        out_specs=[tok(D_MODEL), _full((8, D_MODEL))],
        out_shape=[jax.ShapeDtypeStruct((t, D_MODEL), F32), jax.ShapeDtypeStruct((8, D_MODEL), F32)],
        compiler_params=_params("arbitrary"),
    )(x, dh1, g_mix, *grads, *weights)


def _wgrad(a, b, name, tk, tn, tt, stacked=False, prep=None):
    t, kdim = a.shape
    ncols = b.shape[1]

    def body(a_ref, b_ref, o_ref):
        @pl.when(pl.program_id(2) == 0)
        def _():
            o_ref[...] = jnp.zeros_like(o_ref)

        av = a_ref[...] if prep is None else prep(a_ref[...])
        o_ref[...] += _dot(av, b_ref[...], TN)

    if stacked:
        out_spec = pl.BlockSpec((None, tk, tn), lambda i, j, s: (j, i, 0))
        out_shape = jax.ShapeDtypeStruct((ncols // tn, kdim, tn), F32)
    else:
        out_spec = pl.BlockSpec((tk, tn), lambda i, j, s: (i, j))
        out_shape = jax.ShapeDtypeStruct((kdim, ncols), F32)
    return pl.pallas_call(
        body, name=name, grid=(kdim // tk, ncols // tn, t // tt),
        in_specs=[pl.BlockSpec((tt, tk), lambda i, j, s: (s, i)), pl.BlockSpec((tt, tn), lambda i, j, s: (s, j))],
        out_specs=out_spec, out_shape=out_shape,
        compiler_params=_params("parallel", "parallel", "arbitrary"),
    )(a, b)


def _wgrad_cat(as_, bs, name, tt):
    t = as_[0].shape[0]
    heights = [a.shape[1] for a in as_]
    widths = [b.shape[1] for b in bs]

    def body(*refs):
        a_refs, b_refs, o_ref = refs[:len(as_)], refs[len(as_):-1], refs[-1]

        @pl.when(pl.program_id(0) == 0)
        def _():
            o_ref[...] = jnp.zeros_like(o_ref)

        row = 0
        for a_ref, k in zip(a_refs, heights):
            av = a_ref[...]
            col = 0
            for b_ref, n in zip(b_refs, widths):
                o_ref[row:row + k, col:col + n] += _dot(av, b_ref[...], TN)
                col += n
            row += k

    tok = lambda w: pl.BlockSpec((tt, w), lambda s: (s, 0))
    shape = (sum(heights), sum(widths))
    return pl.pallas_call(
        body, name=name, grid=(t // tt,),
        in_specs=[tok(k) for k in heights] + [tok(n) for n in widths],
        out_specs=_full(shape), out_shape=jax.ShapeDtypeStruct(shape, F32),
        compiler_params=_params("arbitrary"),
    )(*as_, *bs)


def _rope_tables(t):
    half = ATTN_HEAD_DIM // 2
    lane = jnp.arange(2 * ATTN_HEAD_DIM)
    inv = 1.0 / (ROPE_THETA ** ((lane % half).astype(F32) * (2.0 / ATTN_HEAD_DIM)))
    ang = jnp.arange(t, dtype=F32)[:, None] * inv[None, :]
    sign = jnp.where(lane % ATTN_HEAD_DIM < half, -1.0, 1.0).astype(F32)
    return jnp.cos(ang), jnp.sin(ang) * sign[None, :]


def _swap_halves(tv):
    w = tv.shape[-1]
    lane = lax.broadcasted_iota(jnp.int32, tv.shape, tv.ndim - 1)
    first = (lane % ATTN_HEAD_DIM) < (ATTN_HEAD_DIM // 2)
    return jnp.where(first, pltpu.roll(tv, w - ATTN_HEAD_DIM // 2, tv.ndim - 1),
                     pltpu.roll(tv, ATTN_HEAD_DIM // 2, tv.ndim - 1))


def _rope(tv, cos, sin):
    return tv * cos + _swap_halves(tv) * sin


def _rope_bwd(dv, cos, sin):
    return dv * cos + _swap_halves(dv * sin)


def _attn_valid(first_block):
    c = lax.broadcasted_iota(jnp.int32, (2 * ATTN_BLOCK, ATTN_BLOCK), 0)
    r = lax.broadcasted_iota(jnp.int32, (2 * ATTN_BLOCK, ATTN_BLOCK), 1)
    return (c > r) & (c <= r + ATTN_BLOCK) & ((c >= ATTN_BLOCK) | jnp.logical_not(first_block))


def _attn_probs(st, sink, valid):
    s = jnp.where(valid, st * ATTN_SCALE, -jnp.inf)
    m = jnp.maximum(jnp.max(s, axis=0, keepdims=True), sink)
    e = jnp.where(valid, jnp.exp(s - m), 0.0)
    es = jnp.exp(sink - m)
    inv = 1.0 / (jnp.sum(e, axis=0, keepdims=True) + es)
    return e * inv, es * inv


def _lane_scalar(vec, idx):
    lane = lax.broadcasted_iota(jnp.int32, vec.shape, 1)
    return jnp.sum(jnp.where(lane == idx, vec, 0.0), axis=-1, keepdims=True)


ATTN_STEP = 2 * ATTN_BLOCK


def _attn_specs(ns):
    cur = lambda w, cb: pl.BlockSpec((ATTN_STEP, w), lambda i: (jnp.minimum(i, ns - 1), cb))
    prev = lambda w, cb: pl.BlockSpec((ATTN_BLOCK, w), lambda i: (jnp.maximum(2 * jnp.minimum(i, ns - 1) - 1, 0), cb))
    kcol, vcol = ATTN_Q // ATTN_KV, ATTN_Q // ATTN_KV + 1
    return [cur(ATTN_Q, 0), cur(ATTN_KV, kcol), prev(ATTN_KV, kcol), cur(ATTN_KV, vcol), prev(ATTN_KV, vcol),
            cur(ATTN_KV, 0), cur(ATTN_KV, 0), prev(ATTN_KV, 0), prev(ATTN_KV, 0), _full((1, 128))]


def _attn_windows(tp, tc):
    hsl = lambda hk: slice(hk * ATTN_HEAD_DIM, (hk + 1) * ATTN_HEAD_DIM)
    return [[jnp.concatenate([tp[:, hsl(hk)], tc[0:ATTN_BLOCK, hsl(hk)]], axis=0) for hk in range(ATTN_KV_HEADS)],
            [tc[:, hsl(hk)] for hk in range(ATTN_KV_HEADS)]]


def _attn_items():
    return [(s, h, slice(s * ATTN_BLOCK, (s + 1) * ATTN_BLOCK), slice(h * ATTN_HEAD_DIM, (h + 1) * ATTN_HEAD_DIM))
            for s in range(2) for h in range(ATTN_HEADS)]


def _attn_fwd(pa, cos, sin, sinks_vec):
    t = pa.shape[0]
    ns = t // ATTN_STEP

    def body(q_ref, kc_ref, kp_ref, vc_ref, vp_ref, cc_ref, sc_ref, cp_ref, sp_ref, sk_ref, o_ref):
        cc, sc = cc_ref[...], sc_ref[...]
        q = _rope(q_ref[...], jnp.tile(cc, (1, ATTN_Q // ATTN_KV)), jnp.tile(sc, (1, ATTN_Q // ATTN_KV)))
        kc = _rope(kc_ref[...], cc, sc)
        kp = _rope(kp_ref[...], cp_ref[...], sp_ref[...])
        sk = sk_ref[...]
        valids = [_attn_valid(pl.program_id(0) == 0), _attn_valid(False)]
        kwins = _attn_windows(kp, kc)
        vwins_t = [[v.T for v in vs] for vs in _attn_windows(vp_ref[...], vc_ref[...])]
        items = _attn_items()
        scores = [_dot(kwins[s][h // ATTN_GROUPS], q[rows, hs], NT) for s, h, rows, hs in items]
        probs = [_attn_probs(st, _lane_scalar(sk, h), valids[s])[0] for (s, h, rows, hs), st in zip(items, scores)]
        for (s, h, rows, hs), pt in zip(items, probs):
            o_ref[rows, hs] = _dot(vwins_t[s][h // ATTN_GROUPS], pt).T.astype(o_ref.dtype)

    return pl.pallas_call(
        body, name="attn_fwd", grid=(ns,),
        in_specs=_attn_specs(ns),
        out_specs=pl.BlockSpec((ATTN_STEP, ATTN_Q), lambda i: (i, 0)),
        out_shape=jax.ShapeDtypeStruct((t, ATTN_Q), MXU_DTYPE),
        compiler_params=_params("parallel"),
    )(pa, pa, pa, pa, pa, cos, sin, cos, sin, sinks_vec)


def _attn_bwd(pa, cos, sin, sinks_vec, dao):
    t = pa.shape[0]
    ns = t // ATTN_STEP
    lo, hi = slice(0, ATTN_BLOCK), slice(ATTN_BLOCK, ATTN_STEP)

    def body(q_ref, kc_ref, kp_ref, vc_ref, vp_ref, cc_ref, sc_ref, cp_ref, sp_ref, sk_ref, do_ref,
             dq_ref, dk_ref, dv_ref, acc_ref, dqr_ref, dkw_ref, dvw_ref, ck_ref, cv_ref):
        i = pl.program_id(0)

        @pl.when(i == 0)
        def _():
            acc_ref[...] = jnp.zeros_like(acc_ref)
            ck_ref[...] = jnp.zeros_like(ck_ref)
            cv_ref[...] = jnp.zeros_like(cv_ref)

        @pl.when(i < ns)
        def _():
            cc, sc = cc_ref[...], sc_ref[...]
            cq, sq = jnp.tile(cc, (1, ATTN_Q // ATTN_KV)), jnp.tile(sc, (1, ATTN_Q // ATTN_KV))
            q = _rope(q_ref[...], cq, sq)
            kc = _rope(kc_ref[...], cc, sc)
            kp = _rope(kp_ref[...], cp_ref[...], sp_ref[...])
            sk = sk_ref[...]
            do = do_ref[...]
            lane = lax.broadcasted_iota(jnp.int32, (1, 128), 1)
            dsink = jnp.zeros((1, 128), F32)
            valids = [_attn_valid(i == 0), _attn_valid(False)]
            kwins = _attn_windows(kp, kc)
            vwins = _attn_windows(vp_ref[...], vc_ref[...])
            kwins_t = [[kw.T for kw in kws] for kws in kwins]
            items = _attn_items()
            scores = [_dot(kwins[s][h // ATTN_GROUPS], q[rows, hs], NT) for s, h, rows, hs in items]
            dps = [_dot(vwins[s][h // ATTN_GROUPS], do[rows, hs], NT) for s, h, rows, hs in items]
            pts, dsts = {}, {}
            for (s, h, rows, hs), st, dp_t in zip(items, scores, dps):
                probs_t, psink = _attn_probs(st, _lane_scalar(sk, h), valids[s])
                delta = jnp.sum(probs_t * dp_t, axis=0, keepdims=True)
                pts[s, h] = probs_t
                dsts[s, h] = probs_t * (dp_t - delta) * ATTN_SCALE
                dsink += jnp.where(lane == h, jnp.sum(-psink * delta, axis=1, keepdims=True), 0.0)
            for s, h, rows, hs in items:
                dqr_ref[rows, hs] = _dot(kwins_t[s][h // ATTN_GROUPS], dsts[s, h]).T
            for s in range(2):
                rows = slice(s * ATTN_BLOCK, (s + 1) * ATTN_BLOCK)
                for hk in range(ATTN_KV_HEADS):
                    ks = slice(hk * ATTN_HEAD_DIM, (hk + 1) * ATTN_HEAD_DIM)
                    group = range(hk * ATTN_GROUPS, (hk + 1) * ATTN_GROUPS)
                    heads = [slice(h * ATTN_HEAD_DIM, (h + 1) * ATTN_HEAD_DIM) for h in group]
                    ds_g = jnp.concatenate([dsts[s, h] for h in group], axis=1)
                    p_g = jnp.concatenate([pts[s, h] for h in group], axis=1)
                    q_g = jnp.concatenate([q[rows, hs] for hs in heads], axis=0)
                    do_g = jnp.concatenate([do[rows, hs] for hs in heads], axis=0)
                    dkw_ref[s, :, ks] = _dot(ds_g, q_g)
                    dvw_ref[s, :, ks] = _dot(p_g, do_g)
            acc_ref[0:1, :] += dsink
            dq_ref[...] = _rope_bwd(dqr_ref[...], cq, sq).astype(dq_ref.dtype)
            dk_ref[lo, :] = ck_ref[lo, :].astype(dk_ref.dtype)
            dk_ref[hi, :] = (ck_ref[hi, :] + _rope_bwd(dkw_ref[0, lo, :], cp_ref[...], sp_ref[...])).astype(dk_ref.dtype)
            dv_ref[lo, :] = cv_ref[lo, :].astype(dv_ref.dtype)
            dv_ref[hi, :] = (cv_ref[hi, :] + dvw_ref[0, lo, :]).astype(dv_ref.dtype)
            ck_ref[lo, :] = _rope_bwd(dkw_ref[0, hi, :] + dkw_ref[1, lo, :], cc[lo, :], sc[lo, :])
            ck_ref[hi, :] = _rope_bwd(dkw_ref[1, hi, :], cc[hi, :], sc[hi, :])
            cv_ref[lo, :] = dvw_ref[0, hi, :] + dvw_ref[1, lo, :]
            cv_ref[hi, :] = dvw_ref[1, hi, :]

        @pl.when(i == ns)
        def _():
            dk_ref[...] = ck_ref[...].astype(dk_ref.dtype)
            dv_ref[...] = cv_ref[...].astype(dv_ref.dtype)

    prev_out = lambda w: pl.BlockSpec((ATTN_STEP, w), lambda i: (jnp.maximum(i - 1, 0), 0))
    return pl.pallas_call(
        body, name="attn_bwd", grid=(ns + 1,),
        in_specs=_attn_specs(ns) + [pl.BlockSpec((ATTN_STEP, ATTN_Q), lambda i: (jnp.minimum(i, ns - 1), 0))],
        out_specs=[pl.BlockSpec((ATTN_STEP, ATTN_Q), lambda i: (jnp.minimum(i, ns - 1), 0)), prev_out(ATTN_KV),
                   prev_out(ATTN_KV), _full((8, 128))],
        out_shape=[jax.ShapeDtypeStruct((t, ATTN_Q), MXU_DTYPE), jax.ShapeDtypeStruct((t, ATTN_KV), MXU_DTYPE),
                   jax.ShapeDtypeStruct((t, ATTN_KV), MXU_DTYPE), jax.ShapeDtypeStruct((8, 128), F32)],
        scratch_shapes=[pltpu.VMEM((ATTN_STEP, ATTN_Q), F32), pltpu.VMEM((2, ATTN_STEP, ATTN_KV), F32),
                        pltpu.VMEM((2, ATTN_STEP, ATTN_KV), F32), pltpu.VMEM((ATTN_STEP, ATTN_KV), F32),
                        pltpu.VMEM((ATTN_STEP, ATTN_KV), F32)],
        compiler_params=_params("arbitrary"),
    )(pa, pa, pa, pa, pa, cos, sin, cos, sin, sinks_vec, dao)


PAIR = 2 * DN_CHUNK
INTRA_PAIRS = 4
SCAN_PAIRS = 4
HALO = 8


def _conv_window(cur_ref, prev_ref, xs_ref, tm, has_prev):
    prev = jnp.where(has_prev, prev_ref[...], 0.0)
    xs_ref[0:HALO, :] = prev
    xs_ref[HALO:HALO + tm, :] = cur_ref[...]


def _conv_taps(xs_ref, cw_ref, tm):
    y = cw_ref[0:1, :] * xs_ref[pl.ds(HALO - DN_CONV + 1, tm), :]
    for j in range(1, DN_CONV):
        y += cw_ref[j:j + 1, :] * xs_ref[pl.ds(HALO - DN_CONV + 1 + j, tm), :]
    return y


def _gate_values(ba, al, dt):
    beta = _sigmoid(ba)
    pre = ba + dt
    g = -jnp.exp(al) * _softplus(pre)
    return beta, g, pre


def _dn_prep_specs(tm, tile):
    return [pl.BlockSpec((tm, CONV_CH), lambda i: (tile(i), 0)),
            pl.BlockSpec((HALO, CONV_CH), lambda i: (jnp.maximum(tile(i) * (tm // HALO) - 1, 0), 0)),
            pl.BlockSpec((tm, 128), lambda i: (tile(i), 4 * DN_W // 128)),
            _full((DN_CONV, CONV_CH)), _full((1, 128)), _full((1, 128))]


def _dn_prep(pd, conv_w, al_vec, dt_vec, tm):
    t = pd.shape[0]

    def body(cur_ref, prev_ref, ba_ref, cw_ref, al_ref, dt_ref, qn_ref, kn_ref, vc_ref, gc_ref, gr_ref, xs_ref):
        _conv_window(cur_ref, prev_ref, xs_ref, tm, pl.program_id(0) > 0)
        y = _conv_taps(xs_ref, cw_ref, tm)
        c = y * _sigmoid(y)
        for h in range(DN_HEADS):
            qs = slice(h * DN_HEAD_DIM, (h + 1) * DN_HEAD_DIM)
            ksl = slice(DN_W + h * DN_HEAD_DIM, DN_W + (h + 1) * DN_HEAD_DIM)
            qh, kh = c[:, qs], c[:, ksl]
            qn_ref[:, qs] = qh * lax.rsqrt(jnp.sum(qh * qh, axis=-1, keepdims=True) + EPS) * DN_SCALE
            kn_ref[:, qs] = kh * lax.rsqrt(jnp.sum(kh * kh, axis=-1, keepdims=True) + EPS)
        vc_ref[...] = c[:, 2 * DN_W:3 * DN_W]
        beta, g, _ = _gate_values(ba_ref[...], al_ref[...], dt_ref[...])
        lane = lax.broadcasted_iota(jnp.int32, beta.shape, 1)
        gb = jnp.where(lane < DN_HEADS, beta, jnp.where(lane < 2 * DN_HEADS, g, 0.0))
        gc_ref[...] = gb
        gr_ref[...] = gb.T[0:8, :]

    tok = lambda w: pl.BlockSpec((tm, w), lambda i: (i, 0))
    return pl.pallas_call(
        body, name="dn_prep", grid=(t // tm,),
        in_specs=_dn_prep_specs(tm, lambda i: i),
        out_specs=[tok(DN_W), tok(DN_W), tok(DN_W), tok(128), pl.BlockSpec((8, tm), lambda i: (0, i))],
        out_shape=[jax.ShapeDtypeStruct((t, DN_W), F32)] * 3 + [jax.ShapeDtypeStruct((t, 128), F32),
                                                                 jax.ShapeDtypeStruct((8, t), F32)],
        scratch_shapes=[pltpu.VMEM((HALO + tm, CONV_CH), F32)],
        compiler_params=_params("parallel"),
    )(pd, pd, pd, conv_w, al_vec, dt_vec)


def _pair_masks():
    r = lax.broadcasted_iota(jnp.int32, (PAIR, PAIR), 0)
    c = lax.broadcasted_iota(jnp.int32, (PAIR, PAIR), 1)
    same = (r < DN_CHUNK) == (c < DN_CHUNK)
    return same & (r >= c), same & (r > c)


def _lane_col(mat, idx):
    lane = lax.broadcasted_iota(jnp.int32, mat.shape, 1)
    return jnp.sum(jnp.where(lane == idx, mat, 0.0), axis=-1, keepdims=True)


def _pair_cumsums(gc, gr, low):
    lowf = low.astype(F32)
    return _dot(lowf, gc, NN, HI), _dot(gr, lowf, NT, HI)


def _pair_gates(gc, cum_c, cum_r, low, h):
    beta = _lane_col(gc, h)
    gam = _lane_col(cum_c, DN_HEADS + h)
    gam_row = cum_r[DN_HEADS + h:DN_HEADS + h + 1, :]
    dm = jnp.where(low, jnp.exp(jnp.where(low, gam - gam_row, 0.0)), 0.0)
    row = lax.broadcasted_iota(jnp.int32, gam.shape, 0)
    gl = jnp.where(row < DN_CHUNK, gam[DN_CHUNK - 1:DN_CHUNK, :], gam[PAIR - 1:PAIR, :])
    return beta, gam, dm, gl


def _split(a):
    hi = a.astype(BF16)
    return hi, (a - hi.astype(F32)).astype(BF16)


def _dot_split(a, b, dims=NN):
    (ah, al), (bh, bl) = a, b
    la, lb = (1, 1) if dims == TN else ((0, 1) if dims == NN else (0, 0))
    r = _dot(jnp.concatenate([ah, al], axis=la), jnp.concatenate([bh, bl], axis=lb), dims)
    m, n = r.shape[0] // 2, r.shape[1] // 2
    return (r[m:, n:] + (r[:m, n:] + r[m:, :n])) + r[:m, :n]


def _unit_lower_inverses(lmats):
    n = lmats[0].shape[0]
    r = lax.broadcasted_iota(jnp.int32, (n, n), 0)
    c = lax.broadcasted_iota(jnp.int32, (n, n), 1)
    same = lambda size: (r & ~(size - 1)) == (c & ~(size - 1))
    base = DN_CHUNK // 4
    diag = [jnp.where(same(base), l, 0.0) for l in lmats]
    accs = [(r == c).astype(F32) - d for d in diag]
    splits = [_split(d) for d in diag]
    step = 1
    while 2 * step < base:
        splits = [_split(_dot_split(s, s)) for s in splits]
        accs = [acc + _dot_split(_split(acc), s) for acc, s in zip(accs, splits)]
        step *= 2
    size = base
    while size < DN_CHUNK:
        below = same(2 * size) & jnp.logical_not(same(size))
        tb = [_dot(acc, jnp.where(below, l, 0.0)) for acc, l in zip(accs, lmats)]
        accs = [acc - _dot(t, acc) for acc, t in zip(accs, tb)]
        size *= 2
    return accs


def _dn_intra(qn, kn, vc, gc, gr):
    t = qn.shape[0]
    npair = t // PAIR
    rows_step = INTRA_PAIRS * PAIR

    def body(q_ref, k_ref, v_ref, gc_ref, gr_ref, u_ref, w_ref, qg_ref, kd_ref, a_ref, ti_ref, dl_ref):
        low, strict = _pair_masks()
        items = []
        for p in range(INTRA_PAIRS):
            rows = slice(p * PAIR, (p + 1) * PAIR)
            gc_v = gc_ref[rows, :]
            cum_c, cum_r = _pair_cumsums(gc_v, gr_ref[:, rows], low)
            for h in range(DN_HEADS):
                hs = slice(h * DN_HEAD_DIM, (h + 1) * DN_HEAD_DIM)
                items.append((p, h, rows, hs, _pair_gates(gc_v, cum_c, cum_r, low, h)))
        lmats = []
        for p, h, rows, hs, (beta, gam, dm, gl) in items:
            k = k_ref[rows, hs]
            lmats.append(jnp.where(strict, _dot(k * beta, k, NT) * dm, 0.0))
        tinvs = _unit_lower_inverses(lmats)
        for (p, h, rows, hs, (beta, gam, dm, gl)), tinv in zip(items, tinvs):
            q, k, v = q_ref[rows, hs], k_ref[rows, hs], v_ref[rows, hs]
            eg = jnp.exp(gam)
            u_ref[rows, hs] = _dot(tinv, v * beta)
            w_ref[rows, hs] = _dot(tinv, (k * beta) * eg).astype(w_ref.dtype)
            a_ref[h, rows, :] = _dot(q, k, NT) * dm
            ti_ref[h, rows, :] = tinv
            qg_ref[rows, hs] = (q * eg).astype(qg_ref.dtype)
            kd_ref[rows, hs] = (k * jnp.exp(gl - gam)).astype(kd_ref.dtype)
            for c in range(2):
                last = (c + 1) * DN_CHUNK - 1
                dl_ref[2 * p + c, h] = jnp.broadcast_to(jnp.exp(gam[last:last + 1, :]), (8, 128))

    tok = lambda w: pl.BlockSpec((rows_step, w), lambda n: (n, 0))
    hm = pl.BlockSpec((DN_HEADS, rows_step, PAIR), lambda n: (0, n, 0))
    return pl.pallas_call(
        body, name="dn_intra", grid=(npair // INTRA_PAIRS,),
        in_specs=[tok(DN_W), tok(DN_W), tok(DN_W), tok(128), pl.BlockSpec((8, rows_step), lambda n: (0, n))],
        out_specs=[tok(DN_W)] * 4 + [hm, hm, pl.BlockSpec((2 * INTRA_PAIRS, DN_HEADS, 8, 128), lambda n: (n, 0, 0, 0))],
        out_shape=[jax.ShapeDtypeStruct((t, DN_W), F32)] + [jax.ShapeDtypeStruct((t, DN_W), MXU_DTYPE)] * 3
                  + [jax.ShapeDtypeStruct((DN_HEADS, t, PAIR), F32)] * 2
                  + [jax.ShapeDtypeStruct((2 * npair, DN_HEADS, 8, 128), F32)],
        compiler_params=_params("parallel"),
    )(qn, kn, vc, gc, gr)


def _dn_scan_fwd(u, w, qg, kd, a_qk, dlast, pd, dn_w):
    t = u.shape[0]
    npair = t // PAIR

    def body(u_ref, w_ref, qg_ref, kd_ref, a_ref, dl_ref, z_ref, nw_ref, out_ref, o_ref, vn_ref, sall_ref, s_ref):
        @pl.when(pl.program_id(0) == 0)
        def _():
            s_ref[...] = jnp.zeros_like(s_ref)

        nw = nw_ref[...]
        for c in range(2 * SCAN_PAIRS):
            rows = slice(c * DN_CHUNK, (c + 1) * DN_CHUNK)
            diag = slice((c % 2) * DN_CHUNK, (c % 2 + 1) * DN_CHUNK)
            for h in range(DN_HEADS):
                hs = slice(h * DN_HEAD_DIM, (h + 1) * DN_HEAD_DIM)
                st = s_ref[h]
                sall_ref[c, h] = st
                vn_ref[rows, hs] = (u_ref[rows, hs] - _dot(w_ref[rows, hs], st)).astype(vn_ref.dtype)
            for h in range(DN_HEADS):
                hs = slice(h * DN_HEAD_DIM, (h + 1) * DN_HEAD_DIM)
                st, vn = s_ref[h], vn_ref[rows, hs]
                o = _dot(qg_ref[rows, hs], st) + _dot(a_ref[h, rows, diag], vn)
                s_ref[h] = st * dl_ref[c, h][0:1, :] + _dot(kd_ref[rows, hs], vn, TN)
                o_ref[rows, hs] = o
                z = z_ref[rows, hs]
                on = o * lax.rsqrt(jnp.mean(o * o, axis=-1, keepdims=True) + EPS) * nw
                out_ref[rows, hs] = (on * (z * _sigmoid(z))).astype(out_ref.dtype)

    rows_step = SCAN_PAIRS * PAIR
    tok = pl.BlockSpec((rows_step, DN_W), lambda n: (n, 0))
    hm = pl.BlockSpec((DN_HEADS, rows_step, PAIR), lambda n: (0, n, 0))
    return pl.pallas_call(
        body, name="dn_scan_fwd", grid=(npair // SCAN_PAIRS,),
        in_specs=[tok, tok, tok, tok, hm, pl.BlockSpec((2 * SCAN_PAIRS, DN_HEADS, 8, 128), lambda n: (n, 0, 0, 0)),
                  pl.BlockSpec((rows_step, DN_W), lambda n: (n, 3)), _full((1, 128))],
        out_specs=[tok, tok, tok,
                   pl.BlockSpec((2 * SCAN_PAIRS, DN_HEADS, DN_HEAD_DIM, DN_HEAD_DIM), lambda n: (n, 0, 0, 0))],
        out_shape=[jax.ShapeDtypeStruct((t, DN_W), MXU_DTYPE), jax.ShapeDtypeStruct((t, DN_W), F32),
                   jax.ShapeDtypeStruct((t, DN_W), MXU_DTYPE),
                   jax.ShapeDtypeStruct((2 * npair, DN_HEADS, DN_HEAD_DIM, DN_HEAD_DIM), F32)],
        scratch_shapes=[pltpu.VMEM((DN_HEADS, DN_HEAD_DIM, DN_HEAD_DIM), F32)],
        compiler_params=_params("arbitrary"),
    )(u, w, qg, kd, a_qk, dlast, pd, dn_w)


def _dn_scan_bwd(dout, o, vnew, sall, w, qg, kd, a_qk, dlast, pd, dn_w, dep):
    t = o.shape[0]
    npair = t // PAIR
    nstep = npair // SCAN_PAIRS
    rev = lambda n: nstep - 1 - n

    def body(do_ref, o_ref, vn_ref, sall_ref, w_ref, qg_ref, kd_ref, a_ref, dl_ref, z_ref, nw_ref, dep_ref,
             dz_ref, du_ref, dw_ref, dqg_ref, dkd_ref, da_ref, ddl_ref, acc_ref, ds_ref, dos_ref):
        @pl.when(pl.program_id(0) == 0)
        def _():
            ds_ref[...] = jnp.zeros_like(ds_ref)
            acc_ref[...] = jnp.zeros_like(acc_ref)

        nw = nw_ref[...]
        dnw = jnp.zeros((1, 128), F32)
        for h in range(DN_HEADS):
            hs = slice(h * DN_HEAD_DIM, (h + 1) * DN_HEAD_DIM)
            o, z, dout = o_ref[:, hs], z_ref[:, hs], do_ref[:, hs]
            r = lax.rsqrt(jnp.mean(o * o, axis=-1, keepdims=True) + EPS)
            oh = o * r
            sz = _sigmoid(z)
            dz_ref[:, hs] = dout * (oh * nw) * (sz + z * sz * (1.0 - sz))
            don = dout * (z * sz)
            dnw += jnp.sum(don * oh, axis=0, keepdims=True)
            doh = don * nw
            dos_ref[:, hs] = r * (doh - oh * jnp.mean(doh * oh, axis=-1, keepdims=True))
        acc_ref[0:1, :] += dnw
        for c in reversed(range(2 * SCAN_PAIRS)):
            rows = slice(c * DN_CHUNK, (c + 1) * DN_CHUNK)
            diag = slice((c % 2) * DN_CHUNK, (c % 2 + 1) * DN_CHUNK)
            other = slice((1 - c % 2) * DN_CHUNK, (2 - c % 2) * DN_CHUNK)
            for h in range(DN_HEADS):
                hs = slice(h * DN_HEAD_DIM, (h + 1) * DN_HEAD_DIM)
                do, st, dsp, vn = dos_ref[rows, hs], sall_ref[c, h], ds_ref[h], vn_ref[rows, hs]
                da_ref[h, rows, diag] = _dot(do, vn, NT)
                da_ref[h, rows, other] = jnp.zeros((DN_CHUNK, DN_CHUNK), F32)
                du_ref[rows, hs] = (_dot(a_ref[h, rows, diag], do, TN) + _dot(kd_ref[rows, hs], dsp)).astype(du_ref.dtype)
                dqg_ref[rows, hs] = _dot(do, st, NT)
                dkd_ref[rows, hs] = _dot(vn, dsp, NT)
                ddl = jnp.sum(jnp.sum(dsp * st, axis=1, keepdims=True), axis=0, keepdims=True)
                ddl_ref[c, h] = jnp.broadcast_to(ddl, (8, 128))
            for h in range(DN_HEADS):
                hs = slice(h * DN_HEAD_DIM, (h + 1) * DN_HEAD_DIM)
                do, st, dvn = dos_ref[rows, hs], sall_ref[c, h], du_ref[rows, hs]
                dw_ref[rows, hs] = (-_dot(dvn, st, NT)).astype(dw_ref.dtype)
                ds_ref[h] = (ds_ref[h] * dl_ref[c, h][0:1, :] + _dot(qg_ref[rows, hs], do, TN)
                             - _dot(w_ref[rows, hs], dvn, TN))

    rows_step = SCAN_PAIRS * PAIR
    tok = pl.BlockSpec((rows_step, DN_W), lambda n: (rev(n), 0))
    hm = pl.BlockSpec((DN_HEADS, rows_step, PAIR), lambda n: (0, rev(n), 0))
    sc = pl.BlockSpec((2 * SCAN_PAIRS, DN_HEADS, 8, 128), lambda n: (rev(n), 0, 0, 0))
    return pl.pallas_call(
        body, name="dn_scan_bwd", grid=(nstep,),
        in_specs=[tok, tok, tok,
                  pl.BlockSpec((2 * SCAN_PAIRS, DN_HEADS, DN_HEAD_DIM, DN_HEAD_DIM), lambda n: (rev(n), 0, 0, 0)),
                  tok, tok, tok, hm, sc, pl.BlockSpec((rows_step, DN_W), lambda n: (rev(n), 3)), _full((1, 128)),
                  pl.BlockSpec(memory_space=pl.ANY)],
        out_specs=[tok] * 5 + [hm, sc, _full((8, 128))],
        out_shape=[jax.ShapeDtypeStruct((t, DN_W), F32)] + [jax.ShapeDtypeStruct((t, DN_W), MXU_DTYPE)] * 2
                  + [jax.ShapeDtypeStruct((t, DN_W), F32)] * 2 + [jax.ShapeDtypeStruct((DN_HEADS, t, PAIR), F32),
                   jax.ShapeDtypeStruct((2 * npair, DN_HEADS, 8, 128), F32), jax.ShapeDtypeStruct((8, 128), F32)],
        scratch_shapes=[pltpu.VMEM((DN_HEADS, DN_HEAD_DIM, DN_HEAD_DIM), F32), pltpu.VMEM((SCAN_PAIRS * PAIR, DN_W), F32)],
        compiler_params=_params("arbitrary"),
    )(dout, o, vnew, sall, w, qg, kd, a_qk, dlast, pd, dn_w, dep)


def _dn_intra_bwd(qn, kn, vc, gc, gr, tinv, a_qk, du, dw, dqg, dkd, da_qk, ddlast, dlast, dep):
    t = qn.shape[0]
    npair = t // PAIR

    def body(q_ref, k_ref, v_ref, gc_ref, gr_ref, ti_ref, a_ref, du_ref, dw_ref, dqg_ref, dkd_ref, da_ref, ddl_ref, dl_ref,
             dep_ref, dq_ref, dk_ref, dv_ref, dg_ref):
        low, strict = _pair_masks()
        lane = lax.broadcasted_iota(jnp.int32, (PAIR, 128), 1)
        rowi = lax.broadcasted_iota(jnp.int32, (PAIR, 1), 0)
        rsum = lambda v: jnp.sum(v, axis=-1, keepdims=True)
        items = []
        for p in range(INTRA_PAIRS):
            rows = slice(p * PAIR, (p + 1) * PAIR)
            gc_v = gc_ref[rows, :]
            cum_c, cum_r = _pair_cumsums(gc_v, gr_ref[:, rows], low)
            for h in range(DN_HEADS):
                hs = slice(h * DN_HEAD_DIM, (h + 1) * DN_HEAD_DIM)
                items.append((p, h, rows, hs, _pair_gates(gc_v, cum_c, cum_r, low, h)))
        dtis, lmats, dvbs, dkbgs = [], [], [], []
        for p, h, rows, hs, (beta, gam, dm, gl) in items:
            k, tinv = k_ref[rows, hs], ti_ref[h, rows, :]
            kb = k * beta
            dtis.append(_dot(du_ref[rows, hs], v_ref[rows, hs] * beta, NT)
                        + _dot(dw_ref[rows, hs], kb * jnp.exp(gam), NT))
            lmats.append(jnp.where(strict, _dot(kb, k, NT) * dm, 0.0))
            dvbs.append(_dot(tinv, du_ref[rows, hs], TN))
            dkbgs.append(_dot(tinv, dw_ref[rows, hs], TN))
        xs = [_dot(ti_ref[h, rows, :], dti, TN) for (p, h, rows, hs, g), dti in zip(items, dtis)]
        dls = [jnp.where(strict, -_dot(x, ti_ref[h, rows, :], NT), 0.0) for (p, h, rows, hs, g), x in zip(items, xs)]
        dgam_all = [jnp.zeros((PAIR, 128), F32) for _ in range(INTRA_PAIRS)]
        dbeta_all = [jnp.zeros((PAIR, 128), F32) for _ in range(INTRA_PAIRS)]
        for (p, h, rows, hs, (beta, gam, dm, gl)), dl, lmat, dvb, dkbg in zip(items, dls, lmats, dvbs, dkbgs):
            q, k, v = q_ref[rows, hs], k_ref[rows, hs], v_ref[rows, hs]
            a = a_ref[h, rows, :]
            dqg, dkd = dqg_ref[rows, hs], dkd_ref[rows, hs]
            kb = k * beta
            eg = jnp.exp(gam)
            ekd = jnp.exp(gl - gam)
            dmm = dl * dm
            dam = jnp.where(low, da_ref[h, rows, :], 0.0)
            dn = dam * dm
            e = dl * lmat + dam * a
            dkb = _dot(dmm, k) + dkbg * eg
            dk_ref[rows, hs] = _dot(dmm, kb, TN) + _dot(dn, q, TN) + dkd * ekd + dkb * beta
            dq_ref[rows, hs] = _dot(dn, k) + dqg * eg
            dv_ref[rows, hs] = dvb * beta
            t_kd = rsum(dkd * (k * ekd))
            dgam = rsum(e) - rsum(e.T) + rsum(dqg * (q * eg)) + rsum(dkbg * (kb * eg)) - t_kd
            for c in range(2):
                crows = slice(c * DN_CHUNK, (c + 1) * DN_CHUNK)
                dgl = (jnp.sum(t_kd[crows, :], axis=0, keepdims=True)
                       + ddl_ref[2 * p + c, h][0:1, 0:1] * dl_ref[2 * p + c, h][0:1, 0:1])
                dgam = dgam + jnp.where(rowi == (c + 1) * DN_CHUNK - 1, dgl, 0.0)
            dgam_all[p] += jnp.where(lane == DN_HEADS + h, dgam, 0.0)
            dbeta_all[p] += jnp.where(lane == h, rsum(dkb * k) + rsum(dvb * v), 0.0)
        for p in range(INTRA_PAIRS):
            dg_ref[p * PAIR:(p + 1) * PAIR, :] = dbeta_all[p] + _dot(low.astype(F32), dgam_all[p], TN, HI)

    rows_step = INTRA_PAIRS * PAIR
    tok = lambda w: pl.BlockSpec((rows_step, w), lambda n: (n, 0))
    hm = pl.BlockSpec((DN_HEADS, rows_step, PAIR), lambda n: (0, n, 0))
    sc = pl.BlockSpec((2 * INTRA_PAIRS, DN_HEADS, 8, 128), lambda n: (n, 0, 0, 0))
    return pl.pallas_call(
        body, name="dn_intra_bwd", grid=(npair // INTRA_PAIRS,),
        in_specs=[tok(DN_W), tok(DN_W), tok(DN_W), tok(128), pl.BlockSpec((8, rows_step), lambda n: (0, n)), hm, hm,
                  tok(DN_W), tok(DN_W), tok(DN_W), tok(DN_W), hm, sc, sc, pl.BlockSpec(memory_space=pl.ANY)],
        out_specs=[tok(DN_W), tok(DN_W), tok(DN_W), tok(128)],
        out_shape=[jax.ShapeDtypeStruct((t, DN_W), F32)] * 3 + [jax.ShapeDtypeStruct((t, 128), F32)],
        compiler_params=_params("parallel"),
    )(qn, kn, vc, gc, gr, tinv, a_qk, du, dw, dqg, dkd, da_qk, ddlast, dlast, dep)


def _dn_prep_bwd(pd, conv_w, al_vec, dt_vec, dqn, dkn, dvc, dgc, dz, tm):
    t = pd.shape[0]
    nt = t // tm
    tile = lambda i: nt - 1 - i

    def body(cur_ref, prev_ref, ba_ref, cw_ref, al_ref, dt_ref, dq_ref, dk_ref, dv_ref, dg_ref, dz_ref,
             o_ref, accw_ref, accg_ref, xs_ref, dc_ref, ds_ref, carry_ref):
        @pl.when(pl.program_id(0) == 0)
        def _():
            accw_ref[...] = jnp.zeros_like(accw_ref)
            accg_ref[...] = jnp.zeros_like(accg_ref)
            carry_ref[...] = jnp.zeros_like(carry_ref)

        _conv_window(cur_ref, prev_ref, xs_ref, tm, tile(pl.program_id(0)) > 0)
        taps = [xs_ref[pl.ds(HALO - DN_CONV + 1 + j, tm), :] for j in range(DN_CONV)]
        y = cw_ref[0:1, :] * taps[0]
        for j in range(1, DN_CONV):
            y += cw_ref[j:j + 1, :] * taps[j]
        sg = _sigmoid(y)
        c = y * sg
        for h in range(DN_HEADS):
            qs = slice(h * DN_HEAD_DIM, (h + 1) * DN_HEAD_DIM)
            ksl = slice(DN_W + h * DN_HEAD_DIM, DN_W + (h + 1) * DN_HEAD_DIM)
            for src, sl, scale in ((dq_ref, qs, DN_SCALE), (dk_ref, ksl, 1.0)):
                xh = c[:, sl]
                r = lax.rsqrt(jnp.sum(xh * xh, axis=-1, keepdims=True) + EPS)
                unit = xh * r
                dn = src[:, qs] * scale
                dc_ref[:, sl] = r * (dn - unit * jnp.sum(dn * unit, axis=-1, keepdims=True))
        dc_ref[:, 2 * DN_W:3 * DN_W] = dv_ref[...]
        dy = dc_ref[...] * (sg + y * sg * (1.0 - sg))
        for j in range(DN_CONV):
            accw_ref[j:j + 1, :] += jnp.sum(dy * taps[j], axis=0, keepdims=True)
        ds_ref[0:tm, :] = dy
        ds_ref[tm:tm + HALO, :] = carry_ref[...]
        carry_ref[...] = ds_ref[0:HALO, :]
        dx = cw_ref[0:1, :] * ds_ref[pl.ds(DN_CONV - 1, tm), :]
        for j in range(1, DN_CONV):
            dx += cw_ref[j:j + 1, :] * ds_ref[pl.ds(DN_CONV - 1 - j, tm), :]

        beta, g, pre = _gate_values(ba_ref[...], al_ref[...], dt_ref[...])
        dgb = dg_ref[...]
        lane = lax.broadcasted_iota(jnp.int32, dgb.shape, 1)
        is_b, is_a = lane < DN_HEADS, (lane >= DN_HEADS) & (lane < 2 * DN_HEADS)
        dpre = dgb * (-jnp.exp(al_ref[...])) * _sigmoid(pre)
        dba = jnp.where(is_b, dgb * beta * (1.0 - beta), jnp.where(is_a, dpre, 0.0))
        accg_ref[0:1, :] += jnp.sum(jnp.where(is_a, dgb * g, 0.0), axis=0, keepdims=True)
        accg_ref[1:2, :] += jnp.sum(jnp.where(is_a, dpre, 0.0), axis=0, keepdims=True)
        o_ref[:, 0:CONV_CH] = dx.astype(o_ref.dtype)
        o_ref[:, CONV_CH:CONV_CH + DN_W] = dz_ref[...].astype(o_ref.dtype)
        o_ref[:, CONV_CH + DN_W:DN_COLS] = dba.astype(o_ref.dtype)

    tok = lambda w: pl.BlockSpec((tm, w), lambda i: (tile(i), 0))
    return pl.pallas_call(
        body, name="dn_prep_bwd", grid=(nt,),
        in_specs=_dn_prep_specs(tm, tile) + [tok(DN_W), tok(DN_W), tok(DN_W), tok(128), tok(DN_W)],
        out_specs=[tok(DN_COLS), _full((8, CONV_CH)), _full((8, 128))],
        out_shape=[jax.ShapeDtypeStruct((t, DN_COLS), MXU_DTYPE),
                   jax.ShapeDtypeStruct((8, CONV_CH), F32), jax.ShapeDtypeStruct((8, 128), F32)],
        scratch_shapes=[pltpu.VMEM((HALO + tm, CONV_CH), F32), pltpu.VMEM((tm, CONV_CH), F32),
                        pltpu.VMEM((tm + HALO, CONV_CH), F32), pltpu.VMEM((HALO, CONV_CH), F32)],
        compiler_params=_params("arbitrary"),
    )(pd, pd, pd, conv_w, al_vec, dt_vec, dqn, dkn, dvc, dgc, dz)


def _pad_lanes(v, offset=0):
    return jnp.pad(v.astype(F32), (offset, 128 - offset - v.shape[0]))[None]


class _LocalReducer:
    def start(self, grads):
        return jnp.zeros((8, 128), F32)

    def middle(self, after):
        return jnp.zeros((8, 128), F32)

    def finish(self, after):
        return None


def _local_step(x, p, tgt, sm, w, late, reducer):
    t = x.shape[0]
    tm = min(512, t // 2)
    tm_s = min(512, t // 2)
    tw = min(1024, t // 2)
    tw_ff = min(2048, t // 2)

    attn_cols = ATTN_Q + 2 * ATTN_KV
    w_in_t = w["w_in_t"]
    w_in_t = jnp.pad(w_in_t, ((0, max(0, attn_cols + DN_COLS - w_in_t.shape[0])), (0, 0)))
    wa_t = w_in_t[:attn_cols]
    wd_t = w_in_t[attn_cols:attn_cols + DN_COLS]
    conv_w = w["conv_w"]
    al_vec, dt_vec = _pad_lanes(sm["a_log"], DN_HEADS), _pad_lanes(sm["dt_bias"], DN_HEADS)
    sinks_vec = _pad_lanes(sm["sinks"])
    dn_w = sm["dn_norm"].reshape(1, 128)
    row = lambda v: v.reshape(1, D_MODEL)
    cos, sin = _rope_tables(t)

    u, pa, pd = _inproj(x, row(sm["norm_mix"]), wa_t, wd_t, tm_s)
    ao = _attn_fwd(pa, cos, sin, sinks_vec)
    qn, kn, vc, gc, gr = _dn_prep(pd, conv_w, al_vec, dt_vec, tm_s)
    uu, ww, qg, kd, a_qk, tinv, dlast = _dn_intra(qn, kn, vc, gc, gr)
    dn_out, o, vnew, sall = _dn_scan_fwd(uu, ww, qg, kd, a_qk, dlast, pd, dn_w)
    w_o, late_rest = late(dn_out)
    wo_a, wo_d = w_o[:ATTN_Q], w_o[ATTN_Q:]
    h1 = _oproj(x, ao, dn_out, wo_a, wo_d, tm)
    w = dict(w, **late_rest(h1))
    w_proj = jnp.transpose(w["w_proj4"], (1, 0, 2)).reshape(PLE_DIM, D_MODEL)
    m, r, h2 = _mlp_fwd(h1, row(sm["norm_mlp"]), w["w_up4"], w["w_down"], tw)
    dh2, dh2b, dgp, dpp, n3, pb, acc_ple = _ple_loss(h2, p, tgt, row(sm["norm_ple"]), row(sm["norm_final"]),
                                                     w["w_gate"], w_proj, tm_s)
    g_w_gate = _wgrad(n3, dgp, "wgrad_gate", D_MODEL, D_MODEL, tw)
    g_w_proj = _wgrad(pb, dpp, "wgrad_proj", PLE_DIM, D_MODEL, tw)
    da, dh1, dh1b, dao, ddn, acc_mlp = _mlp_bwd(dh2, dh2b, r, h1, row(sm["norm_mlp"]), w["w_up4"], w["w_down"],
                                                wo_a, wo_d, tm)
    g_w_up4 = _wgrad(m, da, "wgrad_up", D_MODEL, FF_BLOCK, tw_ff, stacked=True)
    g_w_down = _wgrad(r, dh2b, "wgrad_down", FF_BLOCK, D_MODEL, tw_ff,
                      prep=lambda rv: jnp.square(rv.astype(F32)).astype(MXU_DTYPE))
    g_w_o = _wgrad_cat([ao, dn_out], [dh1b], "wgrad_o", tw)
    early = dict(w_up4=g_w_up4, w_down=g_w_down, w_gate=g_w_gate, w_proj=g_w_proj, w_o=g_w_o)
    dep = reducer.start(early)
    dz, du, dw, dqg, dkd, da_qk, ddlast, acc_dn = _dn_scan_bwd(ddn, o, vnew, sall, ww, qg, kd, a_qk, dlast, pd, dn_w,
                                                               dep)
    dep = reducer.middle(du)
    dqn, dkn, dvc, dgc = _dn_intra_bwd(qn, kn, vc, gc, gr, tinv, a_qk, du, dw, dqg, dkd, da_qk, ddlast, dlast, dep)
    d_dn, acc_conv, acc_gate = _dn_prep_bwd(pd, conv_w, al_vec, dt_vec, dqn, dkn, dvc, dgc, dz, tm_s)
    dq, dk, dv, acc_attn = _attn_bwd(pa, cos, sin, sinks_vec, dao)
    reducer.finish(dq)
    wq_t, wk_t, wv_t = wa_t[:ATTN_Q], wa_t[ATTN_Q:ATTN_Q + ATTN_KV], wa_t[ATTN_Q + ATTN_KV:]
    dx, acc_mix = _inproj_bwd(x, dh1, row(sm["norm_mix"]), [dq, dk, dv, d_dn], [wq_t, wk_t, wv_t, wd_t], tm_s)

    g_w_in_t = _wgrad_cat([dq, dk, dv, d_dn], [u], "wgrad_in", tw)
    grads = dict(early, w_in_t=g_w_in_t)
    sums = dict(loss=acc_ple[2, 0], norm_final=acc_ple[0], norm_ple=acc_ple[1], norm_mlp=acc_mlp[0], norm_mix=acc_mix[0],
                dn_norm=acc_dn[0], sinks=acc_attn[0, :ATTN_HEADS], a_log=acc_gate[0, DN_HEADS:2 * DN_HEADS],
                dt_bias=acc_gate[1, DN_HEADS:2 * DN_HEADS], conv_w=acc_conv[:DN_CONV])
    return sums, dx, grads


MESH = pl.DeviceIdType.MESH
ANY = pl.BlockSpec(memory_space=pl.ANY)
N_CHIPS = 4
N_DEV = 8


def _place():
    x, y, c = lax.axis_index("x"), lax.axis_index("y"), lax.axis_index("c")
    chips = [(1 - x, y), (x, 1 - y), (1 - x, 1 - y)]
    return x, y, c, chips


CAST_ROWS = 256


def _gather_weights(shards, conv_s, casts):
    n, m = len(shards), len(casts)
    per = 7
    cast_cols = max(a.shape[1] for a in casts)
    pieces = [(a, r0) for a, arr in enumerate(casts) for r0 in range(0, arr.shape[0], CAST_ROWS)]
    assert all(arr.shape[0] % CAST_ROWS == 0 and arr.shape[1] % 128 == 0 for arr in casts)

    def body(*refs):
        in_refs, conv_ref, cast_in = refs[:n], refs[n], refs[n + 1:n + 1 + m]
        refs = refs[n + 1 + m:]
        out_refs, conv_out, cast_out = refs[:n], refs[n], refs[n + 1:n + 1 + m]
        send_sems, recv_sems, f32_buf, bf16_buf, cast_sems = refs[n + 1 + m:]
        x, y, c, chips = _place()

        def piece(i, store):
            a, r0 = pieces[i]
            cols, slot = cast_in[a].shape[1], i % 2
            if store:
                return pltpu.make_async_copy(bf16_buf.at[slot, :, pl.ds(0, cols)],
                                             cast_out[a].at[pl.ds(r0, CAST_ROWS), :], cast_sems.at[2 + slot])
            return pltpu.make_async_copy(cast_in[a].at[pl.ds(r0, CAST_ROWS), :],
                                         f32_buf.at[slot, :, pl.ds(0, cols)], cast_sems.at[slot])

        def cast_all():
            piece(0, False).start()
            for i in range(len(pieces)):
                if i + 1 < len(pieces):
                    piece(i + 1, False).start()
                piece(i, False).wait()
                if i >= 2:
                    piece(i - 2, True).wait()
                cols = cast_in[pieces[i][0]].shape[1]
                bf16_buf[i % 2, :, pl.ds(0, cols)] = f32_buf[i % 2, :, pl.ds(0, cols)].astype(BF16)
                piece(i, True).start()
            for i in range(max(0, len(pieces) - 2), len(pieces)):
                piece(i, True).wait()

        sibling = (x, y, 1 - c)

        def blk(a, px, py, pc):
            hr = in_refs[a].shape[0] // 2
            return out_refs[a].at[2 * px + py, pl.ds(pc * hr, hr), :]

        def mine(a):
            hr = in_refs[a].shape[0] // 2
            return in_refs[a].at[pl.ds(c * hr, hr), :]

        def rcopy(a, k, block, to, src=None):
            return pltpu.make_async_remote_copy(
                src_ref=blk(a, *block) if src is None else src, dst_ref=blk(a, *block),
                send_sem=send_sems.at[per * a + k], recv_sem=recv_sems.at[per * a + k],
                device_id=to, device_id_type=MESH)

        def whole(a, to):
            return pltpu.make_async_remote_copy(
                src_ref=in_refs[a], dst_ref=out_refs[a].at[2 * x + y],
                send_sem=send_sems.at[per * a], recv_sem=recv_sems.at[per * a], device_id=to, device_id_type=MESH)

        def ccopy(j, to):
            return pltpu.make_async_remote_copy(
                src_ref=conv_ref, dst_ref=conv_out.at[2 * x + y],
                send_sem=send_sems.at[per * n + j], recv_sem=recv_sems.at[per * n + j],
                device_id=to, device_id_type=MESH)

        started = []
        for a in range(n):
            first = [whole(a, sibling)]
            first += [rcopy(a, 1 + j, (x, y, c), (*chip, c), src=mine(a)) for j, chip in enumerate(chips)]
            for cp in first:
                cp.start()
            started += first
        conv_sends = [ccopy(j, (*chip, c)) for j, chip in enumerate(chips)] + [ccopy(3, sibling)]
        for cp in conv_sends:
            cp.start()
        started += conv_sends
        cast_all()
        for a in range(n):
            for j, chip in enumerate(chips):
                rcopy(a, 1 + j, (*chip, c), (x, y, c)).wait_recv()
                fwd = rcopy(a, 4 + j, (*chip, c), sibling)
                fwd.start()
                started.append(fwd)
        for a in range(n):
            whole(a, sibling).wait_recv()
            for j, chip in enumerate(chips):
                rcopy(a, 4 + j, (*chip, 1 - c), (x, y, c)).wait_recv()
        for j, chip in enumerate(chips + [(x, y)]):
            pltpu.make_async_remote_copy(
                src_ref=conv_ref, dst_ref=conv_out.at[2 * chip[0] + chip[1]],
                send_sem=send_sems.at[per * n + j], recv_sem=recv_sems.at[per * n + j],
                device_id=sibling, device_id_type=MESH).wait_recv()
        for cp in started:
            cp.wait_send()

    nsem = per * n + 4
    out_shape = [jax.ShapeDtypeStruct((N_CHIPS,) + s.shape, s.dtype) for s in shards]
    out_shape.append(jax.ShapeDtypeStruct((N_CHIPS,) + conv_s.shape, conv_s.dtype))
    out_shape += [jax.ShapeDtypeStruct(a.shape, BF16) for a in casts]
    res = pl.pallas_call(
        body, name="gather_weights", in_specs=[ANY] * (n + 1 + m), out_specs=[ANY] * (n + 1 + m), out_shape=out_shape,
        scratch_shapes=[pltpu.SemaphoreType.DMA((nsem,)), pltpu.SemaphoreType.DMA((nsem,)),
                        pltpu.VMEM((2, CAST_ROWS, cast_cols), F32), pltpu.VMEM((2, CAST_ROWS, cast_cols), BF16),
                        pltpu.SemaphoreType.DMA((4,))],
    )(*shards, conv_s, *casts)
    return res[:n], res[n], res[n + 1:]


HBM = pl.BlockSpec(memory_space=pltpu.HBM)
SEM = pl.BlockSpec(memory_space=pltpu.SEMAPHORE)
EFFECT = pltpu.SideEffectType.DATAFLOW_SIDE_EFFECTING
LATE_COPIES = 7


def _late_copies(in_refs, land_refs, send_sems, recv_sems, only=None):
    x, y, c, chips = _place()
    sends, arrivals = [], []
    for a, (src, land) in enumerate(zip(in_refs, land_refs)):
        if only is not None and a not in only:
            continue
        hr = src.shape[0] // 2
        base = LATE_COPIES * a

        def cp(src_ref, dst_ref, s_idx, r_idx, to):
            return pltpu.make_async_remote_copy(src_ref=src_ref, dst_ref=dst_ref, send_sem=send_sems.at[base + s_idx],
                                                recv_sem=recv_sems.at[base + r_idx], device_id=to, device_id_type=MESH)

        sends.append(cp(src, land.at[2 * x + y], 0, 0, (x, y, 1 - c)))
        arrivals.append(cp(src, land.at[2 * x + y], 0, 0, (x, y, 1 - c)))
        for j, chip in enumerate(chips):
            for pc in range(2):
                half = src.at[pl.ds(c * hr, hr), :]
                sends.append(cp(half, land.at[2 * x + y, pl.ds(c * hr, hr), :], 1 + 2 * j + pc, 1 + 2 * j + c, (*chip, pc)))
                arrivals.append(cp(half, land.at[2 * chip[0] + chip[1], pl.ds(pc * hr, hr), :], 1 + 2 * j + pc,
                                   1 + 2 * j + pc, (*chip, pc)))
    return sends, arrivals


def _copies_start(name, build, nsem, srcs, land_shapes, after):
    n = len(srcs)

    def body(*refs):
        sends, _ = build(refs[:n], refs[n:2 * n], refs[2 * n + 1], refs[2 * n + 2])
        for cp in sends:
            cp.start()
        refs[-1][...] = jnp.zeros_like(refs[-1])

    lands = [pltpu.with_memory_space_constraint(lax.empty(s.shape, s.dtype), pltpu.HBM) for s in land_shapes]
    ins = [pltpu.with_memory_space_constraint(s, pltpu.HBM) for s in srcs]
    out = pl.pallas_call(
        body, name=name,
        out_shape=(pltpu.SemaphoreType.DMA((nsem,)), pltpu.SemaphoreType.DMA((nsem,)),
                   *[pltpu.HBM(s.shape, s.dtype) for s in srcs], *[pltpu.HBM(s.shape, s.dtype) for s in land_shapes],
                   jax.ShapeDtypeStruct((8, 128), F32)),
        in_specs=[HBM] * (2 * n) + [ANY],
        out_specs=(SEM, SEM, *[HBM] * (2 * n), pl.BlockSpec(memory_space=pltpu.VMEM)),
        input_output_aliases={i: 2 + i for i in range(2 * n)},
        compiler_params=pltpu.CompilerParams(has_side_effects=EFFECT),
    )(*ins, *lands, after)
    return out[0], out[1], out[2:2 + n], out[2 + n:2 + 2 * n], out[-1]


def _copies_wait(name, build, started, after):
    send_sems, recv_sems, srcs, lands, _ = started
    n = len(srcs)

    def body(*refs):
        sends, arrivals = build(refs[:n], refs[n:2 * n], refs[2 * n], refs[2 * n + 1])
        for cp in sends:
            cp.wait_send()
        for cp in arrivals:
            cp.wait_recv()

    out = pl.pallas_call(
        body, name=name,
        out_shape=(*[pltpu.HBM(s.shape, s.dtype) for s in srcs], *[pltpu.HBM(l.shape, l.dtype) for l in lands]),
        in_specs=[HBM] * (2 * n) + [SEM, SEM, ANY],
        out_specs=tuple([HBM] * (2 * n)),
        input_output_aliases={i: i for i in range(2 * n)},
        compiler_params=pltpu.CompilerParams(has_side_effects=EFFECT),
    )(*srcs, *lands, send_sems, recv_sems, after)
    return out[:n], out[n:]


def _exchange_copies(g_refs, got_refs, send_sems, recv_sems):
    x, y, c, _ = _place()
    sends, arrivals = [], []
    for a, (g, got) in enumerate(zip(g_refs, got_refs)):
        hr = g.shape[1] // 2
        cp = pltpu.make_async_remote_copy(
            src_ref=g.at[:, pl.ds((1 - c) * hr, hr), :], dst_ref=got, send_sem=send_sems.at[a],
            recv_sem=recv_sems.at[a], device_id=(x, y, 1 - c), device_id_type=MESH)
        sends.append(cp)
        arrivals.append(cp)
    return sends, arrivals


def _scatter_copies(s_refs, got_refs, send_sems, recv_sems):
    x, y, c, chips = _place()
    sends, arrivals = [], []
    for a, (s16, got) in enumerate(zip(s_refs, got_refs)):
        for j, chip in enumerate(chips):
            cp = pltpu.make_async_remote_copy(
                src_ref=s16.at[2 * chip[0] + chip[1]], dst_ref=got.at[j], send_sem=send_sems.at[3 * a + j],
                recv_sem=recv_sems.at[3 * a + j], device_id=(*chip, c), device_id_type=MESH)
            sends.append(cp)
            arrivals.append(cp)
    return sends, arrivals


def _share_halves(name, bufs, dep):
    n = len(bufs)

    def body(*refs):
        out_refs = refs[n + 1:2 * n + 1]
        send_sems, recv_sems = refs[2 * n + 1:]
        x, y, c, _ = _place()
        remote = [pltpu.make_async_remote_copy(
            src_ref=out_refs[a].at[c], dst_ref=out_refs[a].at[c], send_sem=send_sems.at[a], recv_sem=recv_sems.at[a],
            device_id=(x, y, 1 - c), device_id_type=MESH) for a in range(n)]
        for cp in remote:
            cp.start()
        for a in range(n):
            pltpu.make_async_remote_copy(
                src_ref=out_refs[a].at[c], dst_ref=out_refs[a].at[1 - c], send_sem=send_sems.at[a],
                recv_sem=recv_sems.at[a], device_id=(x, y, 1 - c), device_id_type=MESH).wait_recv()
        for cp in remote:
            cp.wait_send()

    return pl.pallas_call(
        body, name=name, in_specs=[ANY] * (n + 1), out_specs=[ANY] * n,
        out_shape=[jax.ShapeDtypeStruct(b.shape, b.dtype) for b in bufs],
        input_output_aliases={a: a for a in range(n)},
        scratch_shapes=[pltpu.SemaphoreType.DMA((n,)), pltpu.SemaphoreType.DMA((n,))],
    )(*bufs, dep)


SMALL_ROWS, SMALL_COLS = 16, CONV_CH
DN_NORM_LANE = 128


def _allreduce_small(block):
    m_per, ncol = block.shape

    def body(x_ref, sum_ref, all_ref, send_sems, recv_sems, local_sem):
        x, y, c, chips = _place()
        me, sibling = (x, y, c), (x, y, 1 - c)

        def rows(px, py, pc):
            return all_ref.at[pl.ds((4 * px + 2 * py + pc) * m_per, m_per), :]

        def copy(k, block_of, to, src=None):
            return pltpu.make_async_remote_copy(
                src_ref=rows(*block_of) if src is None else src, dst_ref=rows(*block_of),
                send_sem=send_sems.at[k], recv_sem=recv_sems.at[k], device_id=to, device_id_type=MESH)

        mine = pltpu.make_async_copy(x_ref, rows(*me), local_sem)
        mine.start()
        first = [copy(0, me, sibling, src=x_ref)]
        first += [copy(1 + j, me, (*chip, c), src=x_ref) for j, chip in enumerate(chips)]
        for cp in first:
            cp.start()
        passed = [copy(4 + j, (*chip, c), sibling) for j, chip in enumerate(chips)]
        for j, chip in enumerate(chips):
            copy(1 + j, (*chip, c), me).wait_recv()
            passed[j].start()
        copy(0, sibling, me).wait_recv()
        for j, chip in enumerate(chips):
            copy(4 + j, (*chip, 1 - c), me).wait_recv()
        for cp in first + passed:
            cp.wait_send()
        mine.wait()
        total = all_ref[0:m_per, :]
        for d in range(1, N_DEV):
            total = total + all_ref[d * m_per:(d + 1) * m_per, :]
        sum_ref[...] = total

    vm = pl.BlockSpec(memory_space=pltpu.VMEM)
    return pl.pallas_call(
        body, name="allreduce_small", in_specs=[vm], out_specs=vm,
        out_shape=jax.ShapeDtypeStruct((m_per, ncol), F32),
        scratch_shapes=[pltpu.VMEM((N_DEV * m_per, ncol), F32), pltpu.SemaphoreType.DMA((7,)),
                        pltpu.SemaphoreType.DMA((7,)), pltpu.SemaphoreType.DMA],
    )(block)


def _row_tile(rows, cols):
    tile = rows
    while tile * cols * 4 > (1 << 20) and tile % 16 == 0:
        tile //= 2
    return tile


def _elementwise(fn, name, ins, out_dtypes, dep):
    rows, cols = ins[0].shape
    tile = _row_tile(rows, cols)

    def body(*refs):
        outs = fn(*[r[...] for r in refs[:len(ins)]])
        for o_ref, o in zip(refs[len(ins) + 1:], outs):
            o_ref[...] = o.astype(o_ref.dtype)

    if tile * cols * 4 > (1 << 21) and cols % 512 == 0:
        spec = pl.BlockSpec((rows, 256), lambda i: (0, i))
        steps = cols // 256
    else:
        spec = pl.BlockSpec((tile, cols), lambda i: (i, 0))
        steps = rows // tile
    return pl.pallas_call(
        body, name=name, grid=(steps,), in_specs=[spec] * len(ins) + [pl.BlockSpec(memory_space=pl.ANY)],
        out_specs=[spec] * len(out_dtypes),
        out_shape=[jax.ShapeDtypeStruct((rows, cols), d) for d in out_dtypes],
        compiler_params=_params("parallel"),
    )(*ins, dep)


def _adamw_tile(w, g, m, v):
    m = ADAM_B1 * m + (1.0 - ADAM_B1) * g
    v = ADAM_B2 * v + (1.0 - ADAM_B2) * jnp.square(g)
    m_hat = m / (1.0 - ADAM_B1 ** ADAM_STEP)
    v_hat = v / (1.0 - ADAM_B2 ** ADAM_STEP)
    delta = -ADAM_LR * (m_hat / (jnp.sqrt(v_hat) + ADAM_EPS) + ADAM_WD * w)
    return delta, m, v


def _adamw(name, w, g, m, v, dep):
    return _elementwise(_adamw_tile, name, [w, g, m, v], [F32, F32, F32], dep)


def _chip_sum(name, g4, got, place):
    nchip, hr, cols = got.shape
    tile = _row_tile(hr, cols)
    nblk = hr // tile

    def body(pl_ref, g_ref, o_ref, s32_ref, s16_ref):
        s = g_ref[...] + o_ref[...]
        s16_ref[...] = s.astype(BF16)

        @pl.when(pl.program_id(1) == pl_ref[0])
        def _():
            s32_ref[...] = s

    spec = pl.BlockSpec((None, tile, cols), lambda i, k, pr: (k, i, 0))
    return pl.pallas_call(
        body, name=name,
        grid_spec=pltpu.PrefetchScalarGridSpec(
            num_scalar_prefetch=1, grid=(nblk, nchip),
            in_specs=[pl.BlockSpec((None, tile, cols), lambda i, k, pr: (k, pr[1] * nblk + i, 0)), spec],
            out_specs=[pl.BlockSpec((tile, cols), lambda i, k, pr: (i, 0)), spec]),
        out_shape=[jax.ShapeDtypeStruct((hr, cols), F32), jax.ShapeDtypeStruct(got.shape, BF16)],
        compiler_params=_params("parallel", "arbitrary"),
    )(place, g4, got)


def _mesh_sum(name, s32, got, place):
    hr, cols = s32.shape
    tile = _row_tile(hr, cols)

    def body(pl_ref, own_ref, g0_ref, g1_ref, g2_ref, o_ref):
        o_ref[...] = ((own_ref[...] + g0_ref[...].astype(F32)) + g1_ref[...].astype(F32)) + g2_ref[...].astype(F32)

    slab = lambda j: pl.BlockSpec((None, tile, cols), lambda i, pr: (j, i, 0))
    return pl.pallas_call(
        body, name=name,
        grid_spec=pltpu.PrefetchScalarGridSpec(
            num_scalar_prefetch=1, grid=(hr // tile,),
            in_specs=[pl.BlockSpec((tile, cols), lambda i, pr: (i, 0)), slab(0), slab(1), slab(2)],
            out_specs=pl.BlockSpec((None, tile, cols), lambda i, pr: (pr[1], i, 0))),
        out_shape=jax.ShapeDtypeStruct((2, hr, cols), F32),
        compiler_params=_params("parallel"),
    )(place, s32, got, got, got)


def _place_operand():
    return jnp.stack([2 * lax.axis_index("x") + lax.axis_index("y"), lax.axis_index("c")]).astype(jnp.int32)


W_IN_ROWS = 720
W_IN_GATHER_ROWS = 736
BF16_TILE_ROWS = 16
F32_TILE_ROWS = 8


def _join_w_in(blocks):
    rows, t = D_IN // N_CHIPS, BF16_TILE_ROWS
    first = [rows * k // t * t for k in range(N_CHIPS)]
    parts = []
    for k in range(N_CHIPS):
        lo = t if k else 0
        if k + 1 < N_CHIPS:
            hi = first[k + 1] - first[k]
            assert rows * (k + 1) <= first[k + 1] + t and hi + t <= W_IN_GATHER_ROWS
            parts += [blocks[k, lo:hi], blocks[k, hi:hi + t] + blocks[k + 1, :t]]
        else:
            parts.append(blocks[k, lo:])
    return jnp.concatenate(parts, axis=0)


def _per_chip(name, g):
    if name == "w_in_t":
        rows = D_IN // N_CHIPS
        first = [rows * k // F32_TILE_ROWS * F32_TILE_ROWS for k in range(N_CHIPS)]
        assert all(rows * (k + 1) <= first[k] + W_IN_ROWS <= g.shape[0] for k in range(N_CHIPS))
        return jnp.stack([lax.slice_in_dim(g, f, f + W_IN_ROWS) for f in first])
    if name == "w_proj":
        return jnp.transpose(g.reshape(PLE_DIM, N_CHIPS, D_MODEL // N_CHIPS), (1, 0, 2))
    if name == "w_up4":
        return g
    return g.reshape(N_CHIPS, g.shape[0] // N_CHIPS, g.shape[1])


class _EarlyReducer:
    def __init__(self, tag):
        self.tag = tag

    def start(self, grads):
        self.names = list(grads)
        self.place = _place_operand()
        slabs = [_per_chip(k, grads[k]) for k in self.names]
        halves = [jax.ShapeDtypeStruct((s.shape[0], s.shape[1] // 2, s.shape[2]), F32) for s in slabs]
        self.a = _copies_start(self.tag + "exchange_start", _exchange_copies, len(slabs), slabs, halves,
                               slabs[0][0, :8, :128])
        return self.a[-1]

    def middle(self, after):
        slabs, got = _copies_wait(self.tag + "exchange_wait", _exchange_copies, self.a, after)
        self.sums = [_chip_sum(self.tag + "chip_sum_" + k, s, g, self.place) for k, s, g in zip(self.names, slabs, got)]
        s16 = [s[1] for s in self.sums]
        lands = [jax.ShapeDtypeStruct((3,) + s.shape[1:], BF16) for s in s16]
        self.b = _copies_start(self.tag + "scatter_start", _scatter_copies, 3 * len(s16), s16, lands,
                               self.sums[0][0][:8, :128])
        return self.b[-1]

    def finish(self, after):
        _, got = _copies_wait(self.tag + "scatter_wait", _scatter_copies, self.b, after)
        self.bufs = {k: _mesh_sum(self.tag + "mesh_sum_" + k, s[0], g, self.place)
                     for k, s, g in zip(self.names, self.sums, got)}


def kernel(x, p, norm_mix, w_in, conv_w, a_log, dt_bias, dn_norm, sinks, w_o, norm_mlp, w_up, w_down, norm_ple, w_ple_gate, w_ple_proj, norm_final, loss_target, m_norm_mix, m_w_in, m_conv_w, m_a_log, m_dt_bias, m_dn_norm, m_sinks, m_w_o, m_norm_mlp, m_w_up, m_w_down, m_norm_ple, m_w_ple_gate, m_w_ple_proj, m_norm_final, v_norm_mix, v_w_in, v_conv_w, v_a_log, v_dt_bias, v_dn_norm, v_sinks, v_w_o, v_norm_mlp, v_w_up, v_w_down, v_norm_ple, v_w_ple_gate, v_w_ple_proj, v_norm_final):
    chip = 2 * lax.axis_index("x") + lax.axis_index("y")
    big = dict(w_in=w_in[0], w_o=w_o[0], w_up=w_up[0], w_down=w_down[0], w_gate=w_ple_gate[0], w_proj=w_ple_proj[0])
    big_m = dict(w_in=m_w_in[0], w_o=m_w_o[0], w_up=m_w_up[0], w_down=m_w_down[0], w_gate=m_w_ple_gate[0], w_proj=m_w_ple_proj[0])
    big_v = dict(w_in=v_w_in[0], w_o=v_w_o[0], w_up=v_w_up[0], w_down=v_w_down[0], w_gate=v_w_ple_gate[0], w_proj=v_w_ple_proj[0])
    names = list(big)

    rows_in = D_IN // N_CHIPS
    w_in_shard_t = lax.dynamic_update_slice(jnp.zeros((W_IN_GATHER_ROWS, D_MODEL), BF16), big["w_in"].T.astype(BF16),
                                            ((rows_in * chip) % BF16_TILE_ROWS, 0))
    late_names = names[1:]
    (w_in_all,), conv_all, late_shards = _gather_weights([w_in_shard_t], conv_w[0], [big[k] for k in late_names])
    gather = _copies_start("gather_start", _late_copies, LATE_COPIES * len(late_shards), late_shards,
                           [jax.ShapeDtypeStruct((N_CHIPS,) + s.shape, BF16) for s in late_shards], w_in_all)
    token = gather[-1]
    w = dict(w_in_t=_join_w_in(w_in_all),
             conv_w=jnp.transpose(conv_all, (1, 0, 2)).reshape(DN_CONV, CONV_CH))
    sm = dict(norm_mix=norm_mix[0] + token[0, 0], a_log=a_log[0], dt_bias=dt_bias[0], dn_norm=dn_norm[0],
              sinks=sinks[0], norm_mlp=norm_mlp[0], norm_ple=norm_ple[0], norm_final=norm_final)

    def late(after):
        first = functools.partial(_late_copies, only=(0,))
        srcs, lands = _copies_wait("gather_wait_o", first, gather, after)

        def rest(after2):
            others = functools.partial(_late_copies, only=tuple(range(1, len(late_names))))
            gw = dict(zip(late_names, _copies_wait("gather_wait_rest", others, gather[:2] + (srcs, lands, None), after2)[1]))
            return dict(w_up4=gw["w_up"], w_down=gw["w_down"].reshape(D_FF, D_MODEL),
                        w_gate=gw["w_gate"].reshape(D_MODEL, D_MODEL), w_proj4=gw["w_proj"])

        return lands[0].reshape(D_MODEL, D_MODEL), rest

    reducer = _EarlyReducer("early_")
    sums, grad_x, g = _local_step(x[0], p[0, 0], loss_target[0], sm, w, late, reducer)

    last = _EarlyReducer("last_")
    dep_a = last.start({"w_in_t": g["w_in_t"]})

    row = lambda v: jnp.pad(v, (0, SMALL_COLS - v.shape[0]))

    def misc_row(al, dtb, sk, dnn, rest):
        head = jnp.concatenate([al, dtb, sk])
        return row(jnp.concatenate([head, jnp.zeros((DN_NORM_LANE - head.shape[0],), F32), dnn, rest]))

    misc = misc_row(sums["a_log"], sums["dt_bias"], sums["sinks"], sums["dn_norm"], sums["loss"].reshape(1))
    small = jnp.concatenate([sums["conv_w"], jnp.stack([row(sums["norm_mix"]), row(sums["norm_mlp"]), row(sums["norm_ple"]),
                                                        row(sums["norm_final"]), misc]),
                             jnp.zeros((SMALL_ROWS - 9, SMALL_COLS), F32)], axis=0)
    tot = _allreduce_small(small + dep_a[0, 0])
    dep_b = last.middle(tot)
    grad_key = dict(w_o="w_o", w_up="w_up4", w_down="w_down", w_gate="w_gate", w_proj="w_proj")
    full = _share_halves("share_halves", [reducer.bufs[grad_key[k]] for k in late_names], dep_b)
    red = {k: f.reshape(-1, f.shape[-1]) for k, f in zip(late_names, full)}
    loss = tot[8, 256]
    ncw = CONV_CH // N_CHIPS

    def pack(cw, nmix, nmlp, nple, nfin, al, dtb, sk, dnn):
        misc_p = misc_row(al, dtb, sk, dnn, jnp.zeros((0,), F32))
        cw_p = jnp.pad(cw, ((0, 0), (0, SMALL_COLS - ncw)))
        return jnp.concatenate([cw_p, jnp.stack([row(nmix), row(nmlp), row(nple), row(nfin), misc_p]),
                                jnp.zeros((SMALL_ROWS - 9, SMALL_COLS), F32)], axis=0)

    def unpack(buf):
        return dict(conv_w=buf[0:4, :ncw][None], norm_mix=buf[4, :D_MODEL][None], norm_mlp=buf[5, :D_MODEL][None],
                    norm_ple=buf[6, :D_MODEL][None], norm_final=buf[7, :D_MODEL], a_log=buf[8, 0:4][None],
                    dt_bias=buf[8, 4:8][None], sinks=buf[8, 8:16][None], dn_norm=buf[8, 128:256][None])

    g_conv_shard = lax.dynamic_slice(tot[0:4], (0, chip * ncw), (DN_CONV, ncw))
    g_small = pack(g_conv_shard, tot[4, :D_MODEL], tot[5, :D_MODEL], tot[6, :D_MODEL], tot[7, :D_MODEL],
                   tot[8, 0:4], tot[8, 4:8], tot[8, 8:16], tot[8, 128:256])
    w_small = pack(conv_w[0], norm_mix[0], norm_mlp[0], norm_ple[0], norm_final, a_log[0], dt_bias[0], sinks[0], dn_norm[0])
    m_small = pack(m_conv_w[0], m_norm_mix[0], m_norm_mlp[0], m_norm_ple[0], m_norm_final, m_a_log[0], m_dt_bias[0],
                   m_sinks[0], m_dn_norm[0])
    v_small = pack(v_conv_w[0], v_norm_mix[0], v_norm_mlp[0], v_norm_ple[0], v_norm_final, v_a_log[0], v_dt_bias[0],
                   v_sinks[0], v_dn_norm[0])

    ref_name = dict(w_in="w_in", w_o="w_o", w_up="w_up", w_down="w_down", w_gate="w_ple_gate", w_proj="w_ple_proj")
    out_g, out_d, out_m, out_v = {}, {}, {}, {}

    def update(k, dep):
        d_k, m_k, v_k = _adamw("adamw_" + k, big[k], red[k], big_m[k], big_v[k], dep)
        out_g[ref_name[k]], out_d[ref_name[k]] = red[k][None], d_k[None]
        out_m[ref_name[k]], out_v[ref_name[k]] = m_k[None], v_k[None]
        return d_k

    for k in late_names:
        done = update(k, dep_b)
    small_out = _adamw("adamw_small", w_small, g_small, m_small, v_small, dep_b)
    d_s, m_s, v_s = (unpack(b) for b in small_out)
    g_s = unpack(g_small)
    for src, dst in ((g_s, out_g), (d_s, out_d), (m_s, out_m), (v_s, out_v)):
        dst.update(src)
    last.finish(done + small_out[0][0:1, 0:1])
    (w_in_full,) = _share_halves("share_halves_w_in", [last.bufs["w_in_t"]], dep_b)
    g_t = lax.dynamic_slice(w_in_full.reshape(W_IN_ROWS, D_MODEL), ((rows_in * chip) % F32_TILE_ROWS, 0),
                            (rows_in, D_MODEL))
    d_t, m_t, v_t = _adamw("adamw_w_in", big["w_in"].T, g_t, big_m["w_in"].T, big_v["w_in"].T, dep_b)
    out_g["w_in"], out_d["w_in"], out_m["w_in"], out_v["w_in"] = g_t.T[None], d_t.T[None], m_t.T[None], v_t.T[None]
    order = ["norm_mix", "w_in", "conv_w", "a_log", "dt_bias", "dn_norm", "sinks", "w_o", "norm_mlp", "w_up", "w_down",
             "norm_ple", "w_ple_gate", "w_ple_proj", "norm_final"]
    return (loss, grad_x[None], *[out_g[k] for k in order], *[out_d[k] for k in order],
            *[out_m[k] for k in order], *[out_v[k] for k in order])
```

```python
import functools

import jax
import jax.numpy as jnp
from jax import lax
from jax.experimental import pallas as pl
from jax.experimental.pallas import tpu as pltpu

F32 = jnp.float32
BF16 = jnp.bfloat16
MXU_DTYPE = jnp.bfloat16
HI = lax.Precision.HIGHEST

D_MODEL = 1024
PLE_DIM = 256
ATTN_HEADS = 8
ATTN_KV_HEADS = 2
ATTN_GROUPS = ATTN_HEADS // ATTN_KV_HEADS
ATTN_HEAD_DIM = 64
ATTN_BLOCK = 128
ROPE_THETA = 10000.0
DN_HEADS = 4
DN_HEAD_DIM = 128
DN_CONV = 4
DN_CHUNK = 64
D_FF = 4 * D_MODEL
EPS = 1e-6
ATTN_Q = ATTN_HEADS * ATTN_HEAD_DIM
ATTN_KV = ATTN_KV_HEADS * ATTN_HEAD_DIM
DN_W = DN_HEADS * DN_HEAD_DIM
CONV_CH = 3 * DN_W
D_IN = ATTN_Q + 2 * ATTN_KV + 4 * DN_W + 2 * DN_HEADS
DN_COLS = 4 * DN_W + 128
DN_SCALE = DN_HEAD_DIM ** -0.5
ATTN_SCALE = ATTN_HEAD_DIM ** -0.5
FF_BLOCKS = 4
FF_BLOCK = D_FF // FF_BLOCKS

ADAM_LR = 0.001
ADAM_B1 = 0.9
ADAM_B2 = 0.999
ADAM_EPS = 1e-08
ADAM_WD = 0.01
ADAM_STEP = 10

V7X_VMEM_BYTES = 64 * 1024 * 1024
VMEM_LIMIT = 48 * 1024 * 1024

NN = ((1,), (0,))
NT = ((1,), (1,))
TN = ((0,), (0,))


def _dot(a, b, dims=NN, prec=None):
    if a.dtype != b.dtype:
        a, b = a.astype(MXU_DTYPE), b.astype(MXU_DTYPE)
    return lax.dot_general(a, b, (dims, ((), ())), precision=prec, preferred_element_type=F32)


def _sigmoid(x):
    return 1.0 / (1.0 + jnp.exp(-x))


def _softplus(x):
    return jnp.maximum(x, 0.0) + jnp.log(1.0 + jnp.exp(-jnp.abs(x)))


def _params(*sem):
    return pltpu.CompilerParams(dimension_semantics=sem, vmem_limit_bytes=VMEM_LIMIT)


def _rms_fwd(xv, g):
    r = lax.rsqrt(jnp.mean(xv * xv, axis=-1, keepdims=True) + EPS)
    return xv * r * g


def _rms_bwd(xv, g, dn):
    r = lax.rsqrt(jnp.mean(xv * xv, axis=-1, keepdims=True) + EPS)
    xh = xv * r
    dg = jnp.sum(dn * xh, axis=0, keepdims=True)
    dxh = dn * g
    dx = r * (dxh - xh * jnp.mean(dxh * xh, axis=-1, keepdims=True))
    return dx, dg


def _full(shape):
    return pl.BlockSpec(shape, lambda *_: (0,) * len(shape))


def _inproj(x, g_mix, wa_t, wd_t, tm):
    t = x.shape[0]

    def body(x_ref, g_ref, wa_ref, wd_ref, u_ref, pa_ref, pd_ref):
        u = _rms_fwd(x_ref[...], g_ref[...]).astype(MXU_DTYPE)
        u_ref[...] = u
        pa_ref[...] = _dot(u, wa_ref[...], NT)
        pd_ref[...] = _dot(u, wd_ref[...], NT)

    na, nd = wa_t.shape[0], wd_t.shape[0]
    return pl.pallas_call(
        body, name="inproj", grid=(t // tm,),
        in_specs=[pl.BlockSpec((tm, D_MODEL), lambda i: (i, 0)), _full((1, D_MODEL)),
                  _full((na, D_MODEL)), _full((nd, D_MODEL))],
        out_specs=[pl.BlockSpec((tm, D_MODEL), lambda i: (i, 0)), pl.BlockSpec((tm, na), lambda i: (i, 0)),
                   pl.BlockSpec((tm, nd), lambda i: (i, 0))],
        out_shape=[jax.ShapeDtypeStruct((t, D_MODEL), MXU_DTYPE), jax.ShapeDtypeStruct((t, na), F32),
                   jax.ShapeDtypeStruct((t, nd), F32)],
        compiler_params=_params("parallel"),
    )(x, g_mix, wa_t, wd_t)


def _oproj(x, ao, dn, wo_a, wo_d, tm):
    t = x.shape[0]

    def body(x_ref, ao_ref, dn_ref, wa_ref, wd_ref, h_ref):
        h_ref[...] = (x_ref[...] + _dot(ao_ref[...].astype(MXU_DTYPE), wa_ref[...])
                      + _dot(dn_ref[...].astype(MXU_DTYPE), wd_ref[...]))

    half = ao.shape[1]
    return pl.pallas_call(
        body, name="oproj", grid=(t // tm,),
        in_specs=[pl.BlockSpec((tm, D_MODEL), lambda i: (i, 0)), pl.BlockSpec((tm, half), lambda i: (i, 0)),
                  pl.BlockSpec((tm, half), lambda i: (i, 0)), _full((half, D_MODEL)), _full((half, D_MODEL))],
        out_specs=pl.BlockSpec((tm, D_MODEL), lambda i: (i, 0)),
        out_shape=jax.ShapeDtypeStruct((t, D_MODEL), F32),
        compiler_params=_params("parallel"),
    )(x, ao, dn, wo_a, wo_d)


def _mlp_fwd(h1, g_mlp, w_up4, w_down, tm):
    t = h1.shape[0]

    def body(h_ref, g_ref, wu_ref, wd_ref, m_ref, r_ref, h2_ref, acc_ref):
        k = pl.program_id(1)

        @pl.when(k == 0)
        def _():
            m_ref[...] = _rms_fwd(h_ref[...], g_ref[...]).astype(MXU_DTYPE)
            acc_ref[...] = jnp.zeros_like(acc_ref)

        r = jnp.maximum(_dot(m_ref[...], wu_ref[...]), 0.0)
        r_ref[...] = r.astype(MXU_DTYPE)
        s = jnp.square(r).astype(MXU_DTYPE)
        acc_ref[...] += _dot(s, wd_ref[...])

        @pl.when(k == FF_BLOCKS - 1)
        def _():
            h2_ref[...] = h_ref[...] + acc_ref[...]

    return pl.pallas_call(
        body, name="mlp_fwd", grid=(t // tm, FF_BLOCKS),
        in_specs=[pl.BlockSpec((tm, D_MODEL), lambda i, k: (i, 0)), _full((1, D_MODEL)),
                  pl.BlockSpec((None, D_MODEL, FF_BLOCK), lambda i, k: (k, 0, 0)),
                  pl.BlockSpec((FF_BLOCK, D_MODEL), lambda i, k: (k, 0))],
        out_specs=[pl.BlockSpec((tm, D_MODEL), lambda i, k: (i, 0)), pl.BlockSpec((tm, FF_BLOCK), lambda i, k: (i, k)),
                   pl.BlockSpec((tm, D_MODEL), lambda i, k: (i, 0))],
        out_shape=[jax.ShapeDtypeStruct((t, D_MODEL), MXU_DTYPE), jax.ShapeDtypeStruct((t, D_FF), MXU_DTYPE),
                   jax.ShapeDtypeStruct((t, D_MODEL), F32)],
        scratch_shapes=[pltpu.VMEM((tm, D_MODEL), F32)],
        compiler_params=_params("parallel", "arbitrary"),
    )(h1, g_mlp, w_up4, w_down)


def _ple_loss(h2, p, tgt, g_ple, g_fin, w_gate, w_proj, tm):
    t = h2.shape[0]

    def body(h_ref, p_ref, t_ref, gp_ref, gf_ref, wg_ref, wp_ref,
             dh_ref, dhb_ref, dgp_ref, dpp_ref, n3_ref, pb_ref, acc_ref):
        @pl.when(pl.program_id(0) == 0)
        def _():
            acc_ref[...] = jnp.zeros_like(acc_ref)

        h = h_ref[...]
        g_ple_v, g_fin_v = gp_ref[...], gf_ref[...]
        n3 = _rms_fwd(h, g_ple_v).astype(MXU_DTYPE)
        n3_ref[...] = n3
        gate = _sigmoid(_dot(n3, wg_ref[...]))
        pb = p_ref[...].astype(MXU_DTYPE)
        pb_ref[...] = pb
        pp = _dot(pb, wp_ref[...])
        h3 = h + gate * pp
        r4 = lax.rsqrt(jnp.mean(h3 * h3, axis=-1, keepdims=True) + EPS)
        xh4 = h3 * r4
        e = xh4 * g_fin_v - t_ref[...]
        loss = 0.5 * jnp.sum(jnp.mean(e * e, axis=-1, keepdims=True), axis=0, keepdims=True)
        dy = e * (1.0 / D_MODEL)
        dg_fin = jnp.sum(dy * xh4, axis=0, keepdims=True)
        dxh = dy * g_fin_v
        dh3 = r4 * (dxh - xh4 * jnp.mean(dxh * xh4, axis=-1, keepdims=True))
        dpp_ref[...] = (dh3 * gate).astype(MXU_DTYPE)
        dgp = (dh3 * pp * gate * (1.0 - gate)).astype(MXU_DTYPE)
        dgp_ref[...] = dgp
        dn3 = _dot(dgp, wg_ref[...], NT)
        dx, dg_ple = _rms_bwd(h, g_ple_v, dn3)
        dh2 = dh3 + dx
        dh_ref[...] = dh2
        dhb_ref[...] = dh2.astype(MXU_DTYPE)
        acc_ref[0:1, :] += dg_fin
        acc_ref[1:2, :] += dg_ple
        acc_ref[2:3, :] += jnp.broadcast_to(loss, (1, D_MODEL))

    row = lambda w: pl.BlockSpec((tm, w), lambda i: (i, 0))
    return pl.pallas_call(
        body, name="ple_loss", grid=(t // tm,),
        in_specs=[row(D_MODEL), row(PLE_DIM), row(D_MODEL), _full((1, D_MODEL)), _full((1, D_MODEL)),
                  _full((D_MODEL, D_MODEL)), _full((PLE_DIM, D_MODEL))],
        out_specs=[row(D_MODEL), row(D_MODEL), row(D_MODEL), row(D_MODEL), row(D_MODEL), row(PLE_DIM),
                   _full((8, D_MODEL))],
        out_shape=[jax.ShapeDtypeStruct((t, D_MODEL), F32), jax.ShapeDtypeStruct((t, D_MODEL), MXU_DTYPE),
                   jax.ShapeDtypeStruct((t, D_MODEL), MXU_DTYPE), jax.ShapeDtypeStruct((t, D_MODEL), MXU_DTYPE),
                   jax.ShapeDtypeStruct((t, D_MODEL), MXU_DTYPE), jax.ShapeDtypeStruct((t, PLE_DIM), MXU_DTYPE),
                   jax.ShapeDtypeStruct((8, D_MODEL), F32)],
        compiler_params=_params("arbitrary"),
    )(h2, p, tgt, g_ple, g_fin, w_gate, w_proj)


def _mlp_bwd(dh2, dh2b, r, h1, g_mlp, w_up4, w_down, wo_a, wo_d, tm):
    t = h1.shape[0]
    half = wo_a.shape[0]

    def body(dh_ref, dhb_ref, r_ref, h_ref, g_ref, wu_ref, wd_ref, woa_ref, wod_ref,
             da_ref, dh1_ref, dh1b_ref, dao_ref, ddn_ref, acc_ref, dm_ref):
        i, k = pl.program_id(0), pl.program_id(1)

        @pl.when((i == 0) & (k == 0))
        def _():
            acc_ref[...] = jnp.zeros_like(acc_ref)

        @pl.when(k == 0)
        def _():
            dm_ref[...] = jnp.zeros_like(dm_ref)

        ds = _dot(dhb_ref[...], wd_ref[...], NT)
        da = (ds * (2.0 * r_ref[...].astype(F32))).astype(MXU_DTYPE)
        da_ref[...] = da
        dm_ref[...] += _dot(da, wu_ref[...], NT)

        @pl.when(k == FF_BLOCKS - 1)
        def _():
            dx, dg = _rms_bwd(h_ref[...], g_ref[...], dm_ref[...])
            dh1 = dh_ref[...] + dx
            dh1_ref[...] = dh1
            dh1b = dh1.astype(MXU_DTYPE)
            dh1b_ref[...] = dh1b
            dao_ref[...] = _dot(dh1b, woa_ref[...], NT)
            ddn_ref[...] = _dot(dh1b, wod_ref[...], NT)
            acc_ref[0:1, :] += dg

    tok = lambda w: pl.BlockSpec((tm, w), lambda i, k: (i, 0))
    return pl.pallas_call(
        body, name="mlp_bwd", grid=(t // tm, FF_BLOCKS),
        in_specs=[tok(D_MODEL), tok(D_MODEL), pl.BlockSpec((tm, FF_BLOCK), lambda i, k: (i, k)), tok(D_MODEL),
                  _full((1, D_MODEL)), pl.BlockSpec((None, D_MODEL, FF_BLOCK), lambda i, k: (k, 0, 0)),
                  pl.BlockSpec((FF_BLOCK, D_MODEL), lambda i, k: (k, 0)),
                  pl.BlockSpec((half, D_MODEL), lambda i, k: (0, 0)), pl.BlockSpec((half, D_MODEL), lambda i, k: (0, 0))],
        out_specs=[pl.BlockSpec((tm, FF_BLOCK), lambda i, k: (i, k)),
                   tok(D_MODEL), tok(D_MODEL), tok(half), tok(half), pl.BlockSpec((8, D_MODEL), lambda i, k: (0, 0))],
        out_shape=[jax.ShapeDtypeStruct((t, D_FF), MXU_DTYPE),
                   jax.ShapeDtypeStruct((t, D_MODEL), F32), jax.ShapeDtypeStruct((t, D_MODEL), MXU_DTYPE),
                   jax.ShapeDtypeStruct((t, half), F32), jax.ShapeDtypeStruct((t, half), F32),
                   jax.ShapeDtypeStruct((8, D_MODEL), F32)],
        scratch_shapes=[pltpu.VMEM((tm, D_MODEL), F32)],
        compiler_params=_params("arbitrary", "arbitrary"),
    )(dh2, dh2b, r, h1, g_mlp, w_up4, w_down, wo_a, wo_d)


def _inproj_bwd(x, dh1, g_mix, grads, weights, tm):
    t = x.shape[0]
    n = len(grads)

    def body(*refs):
        x_ref, dh_ref, g_ref = refs[:3]
        g_refs, w_refs = refs[3:3 + n], refs[3 + n:3 + 2 * n]
        dx_ref, acc_ref = refs[3 + 2 * n:]

        @pl.when(pl.program_id(0) == 0)
        def _():
            acc_ref[...] = jnp.zeros_like(acc_ref)

        du = _dot(g_refs[0][...], w_refs[0][...])
        for j in range(1, n):
            du += _dot(g_refs[j][...], w_refs[j][...])
        dx, dg = _rms_bwd(x_ref[...], g_ref[...], du)
        dx_ref[...] = dh_ref[...] + dx
        acc_ref[0:1, :] += dg

    tok = lambda w: pl.BlockSpec((tm, w), lambda i: (i, 0))
    return pl.pallas_call(
        body, name="inproj_bwd", grid=(t // tm,),
        in_specs=[tok(D_MODEL), tok(D_MODEL), _full((1, D_MODEL))] + [tok(g.shape[1]) for g in grads]
                 + [_full(w.shape) for w in weights],
        out_specs=[tok(D_MODEL), _full((8, D_MODEL))],
        out_shape=[jax.ShapeDtypeStruct((t, D_MODEL), F32), jax.ShapeDtypeStruct((8, D_MODEL), F32)],
        compiler_params=_params("arbitrary"),
    )(x, dh1, g_mix, *grads, *weights)


def _wgrad(a, b, name, tk, tn, tt, stacked=False, prep=None):
    t, kdim = a.shape
    ncols = b.shape[1]

    def body(a_ref, b_ref, o_ref):
        @pl.when(pl.program_id(2) == 0)
        def _():
            o_ref[...] = jnp.zeros_like(o_ref)

        av = a_ref[...] if prep is None else prep(a_ref[...])
        o_ref[...] += _dot(av, b_ref[...], TN)

    if stacked:
        out_spec = pl.BlockSpec((None, tk, tn), lambda i, j, s: (j, i, 0))
        out_shape = jax.ShapeDtypeStruct((ncols // tn, kdim, tn), F32)
    else:
        out_spec = pl.BlockSpec((tk, tn), lambda i, j, s: (i, j))
        out_shape = jax.ShapeDtypeStruct((kdim, ncols), F32)
    return pl.pallas_call(
        body, name=name, grid=(kdim // tk, ncols // tn, t // tt),
        in_specs=[pl.BlockSpec((tt, tk), lambda i, j, s: (s, i)), pl.BlockSpec((tt, tn), lambda i, j, s: (s, j))],
        out_specs=out_spec, out_shape=out_shape,
        compiler_params=_params("parallel", "parallel", "arbitrary"),
    )(a, b)


def _wgrad_cat(as_, bs, name, tt):
    t = as_[0].shape[0]
    heights = [a.shape[1] for a in as_]
    widths = [b.shape[1] for b in bs]

    def body(*refs):
        a_refs, b_refs, o_ref = refs[:len(as_)], refs[len(as_):-1], refs[-1]

        @pl.when(pl.program_id(0) == 0)
        def _():
            o_ref[...] = jnp.zeros_like(o_ref)

        row = 0
        for a_ref, k in zip(a_refs, heights):
            av = a_ref[...]
            col = 0
            for b_ref, n in zip(b_refs, widths):
                o_ref[row:row + k, col:col + n] += _dot(av, b_ref[...], TN)
                col += n
            row += k

    tok = lambda w: pl.BlockSpec((tt, w), lambda s: (s, 0))
    shape = (sum(heights), sum(widths))
    return pl.pallas_call(
        body, name=name, grid=(t // tt,),
        in_specs=[tok(k) for k in heights] + [tok(n) for n in widths],
        out_specs=_full(shape), out_shape=jax.ShapeDtypeStruct(shape, F32),
        compiler_params=_params("arbitrary"),
    )(*as_, *bs)


def _rope_lanes():
    half = ATTN_HEAD_DIM // 2
    lane = jnp.arange(2 * ATTN_HEAD_DIM)
    inv = 1.0 / (ROPE_THETA ** ((lane % half).astype(F32) * (2.0 / ATTN_HEAD_DIM)))
    sign = jnp.where(lane % ATTN_HEAD_DIM < half, -1.0, 1.0).astype(F32)
    return jnp.concatenate([inv[None], sign[None], jnp.zeros((6, 2 * ATTN_HEAD_DIM), F32)], axis=0)


def _rope_rows(lanes, first, rows):
    pos = (lax.broadcasted_iota(jnp.int32, (rows, 2 * ATTN_HEAD_DIM), 0) + first).astype(F32)
    ang = pos * lanes[0:1, :]
    return jnp.cos(ang), jnp.sin(ang) * lanes[1:2, :]


def _swap_halves(tv):
    w = tv.shape[-1]
    lane = lax.broadcasted_iota(jnp.int32, tv.shape, tv.ndim - 1)
    first = (lane % ATTN_HEAD_DIM) < (ATTN_HEAD_DIM // 2)
    return jnp.where(first, pltpu.roll(tv, w - ATTN_HEAD_DIM // 2, tv.ndim - 1),
                     pltpu.roll(tv, ATTN_HEAD_DIM // 2, tv.ndim - 1))


def _rope(tv, cos, sin):
    return tv * cos + _swap_halves(tv) * sin


def _rope_bwd(dv, cos, sin):
    return dv * cos + _swap_halves(dv * sin)


def _attn_valid(first_block):
    c = lax.broadcasted_iota(jnp.int32, (2 * ATTN_BLOCK, ATTN_BLOCK), 0)
    r = lax.broadcasted_iota(jnp.int32, (2 * ATTN_BLOCK, ATTN_BLOCK), 1)
    return (c > r) & (c <= r + ATTN_BLOCK) & ((c >= ATTN_BLOCK) | jnp.logical_not(first_block))


def _attn_probs(st, sink, valid):
    s = jnp.where(valid, st * ATTN_SCALE, -jnp.inf)
    m = jnp.maximum(jnp.max(s, axis=0, keepdims=True), sink)
    e = jnp.where(valid, jnp.exp(s - m), 0.0)
    es = jnp.exp(sink - m)
    inv = 1.0 / (jnp.sum(e, axis=0, keepdims=True) + es)
    return e * inv, es * inv


def _lane_scalar(vec, idx):
    lane = lax.broadcasted_iota(jnp.int32, vec.shape, 1)
    return jnp.sum(jnp.where(lane == idx, vec, 0.0), axis=-1, keepdims=True)


ATTN_STEP = 2 * ATTN_BLOCK


def _attn_specs(ns):
    cur = lambda w, cb: pl.BlockSpec((ATTN_STEP, w), lambda i: (jnp.minimum(i, ns - 1), cb))
    prev = lambda w, cb: pl.BlockSpec((ATTN_BLOCK, w), lambda i: (jnp.maximum(2 * jnp.minimum(i, ns - 1) - 1, 0), cb))
    kcol, vcol = ATTN_Q // ATTN_KV, ATTN_Q // ATTN_KV + 1
    return [cur(ATTN_Q, 0), cur(ATTN_KV, kcol), prev(ATTN_KV, kcol), cur(ATTN_KV, vcol), prev(ATTN_KV, vcol),
            cur(ATTN_KV, 0), cur(ATTN_KV, 0), prev(ATTN_KV, 0), prev(ATTN_KV, 0), _full((1, 128))]


def _attn_windows(tp, tc):
    hsl = lambda hk: slice(hk * ATTN_HEAD_DIM, (hk + 1) * ATTN_HEAD_DIM)
    return [[jnp.concatenate([tp[:, hsl(hk)], tc[0:ATTN_BLOCK, hsl(hk)]], axis=0) for hk in range(ATTN_KV_HEADS)],
            [tc[:, hsl(hk)] for hk in range(ATTN_KV_HEADS)]]


def _attn_items():
    return [(s, h, slice(s * ATTN_BLOCK, (s + 1) * ATTN_BLOCK), slice(h * ATTN_HEAD_DIM, (h + 1) * ATTN_HEAD_DIM))
            for s in range(2) for h in range(ATTN_HEADS)]


def _attn_fwd(pa, cos, sin, sinks_vec):
    t = pa.shape[0]
    ns = t // ATTN_STEP

    def body(q_ref, kc_ref, kp_ref, vc_ref, vp_ref, cc_ref, sc_ref, cp_ref, sp_ref, sk_ref, o_ref):
        cc, sc = cc_ref[...], sc_ref[...]
        q = _rope(q_ref[...], jnp.tile(cc, (1, ATTN_Q // ATTN_KV)), jnp.tile(sc, (1, ATTN_Q // ATTN_KV)))
        kc = _rope(kc_ref[...], cc, sc)
        kp = _rope(kp_ref[...], cp_ref[...], sp_ref[...])
        sk = sk_ref[...]
        valids = [_attn_valid(pl.program_id(0) == 0), _attn_valid(False)]
        kwins = _attn_windows(kp, kc)
        vwins_t = [[v.T for v in vs] for vs in _attn_windows(vp_ref[...], vc_ref[...])]
        items = _attn_items()
        scores = [_dot(kwins[s][h // ATTN_GROUPS], q[rows, hs], NT) for s, h, rows, hs in items]
        probs = [_attn_probs(st, _lane_scalar(sk, h), valids[s])[0] for (s, h, rows, hs), st in zip(items, scores)]
        for (s, h, rows, hs), pt in zip(items, probs):
            o_ref[rows, hs] = _dot(vwins_t[s][h // ATTN_GROUPS], pt).T.astype(o_ref.dtype)

    return pl.pallas_call(
        body, name="attn_fwd", grid=(ns,),
        in_specs=_attn_specs(ns),
        out_specs=pl.BlockSpec((ATTN_STEP, ATTN_Q), lambda i: (i, 0)),
        out_shape=jax.ShapeDtypeStruct((t, ATTN_Q), MXU_DTYPE),
        compiler_params=_params("parallel"),
    )(pa, pa, pa, pa, pa, cos, sin, cos, sin, sinks_vec)


def _attn_bwd(pa, cos, sin, sinks_vec, dao):
    t = pa.shape[0]
    ns = t // ATTN_STEP
    lo, hi = slice(0, ATTN_BLOCK), slice(ATTN_BLOCK, ATTN_STEP)

    def body(q_ref, kc_ref, kp_ref, vc_ref, vp_ref, cc_ref, sc_ref, cp_ref, sp_ref, sk_ref, do_ref,
             dq_ref, dk_ref, dv_ref, acc_ref, dqr_ref, dkw_ref, dvw_ref, ck_ref, cv_ref):
        i = pl.program_id(0)

        @pl.when(i == 0)
        def _():
            acc_ref[...] = jnp.zeros_like(acc_ref)
            ck_ref[...] = jnp.zeros_like(ck_ref)
            cv_ref[...] = jnp.zeros_like(cv_ref)

        @pl.when(i < ns)
        def _():
            cc, sc = cc_ref[...], sc_ref[...]
            cq, sq = jnp.tile(cc, (1, ATTN_Q // ATTN_KV)), jnp.tile(sc, (1, ATTN_Q // ATTN_KV))
            q = _rope(q_ref[...], cq, sq)
            kc = _rope(kc_ref[...], cc, sc)
            kp = _rope(kp_ref[...], cp_ref[...], sp_ref[...])
            sk = sk_ref[...]
            do = do_ref[...]
            lane = lax.broadcasted_iota(jnp.int32, (1, 128), 1)
            dsink = jnp.zeros((1, 128), F32)
            valids = [_attn_valid(i == 0), _attn_valid(False)]
            kwins = _attn_windows(kp, kc)
            vwins = _attn_windows(vp_ref[...], vc_ref[...])
            kwins_t = [[kw.T for kw in kws] for kws in kwins]
            items = _attn_items()
            scores = [_dot(kwins[s][h // ATTN_GROUPS], q[rows, hs], NT) for s, h, rows, hs in items]
            dps = [_dot(vwins[s][h // ATTN_GROUPS], do[rows, hs], NT) for s, h, rows, hs in items]
            pts, dsts = {}, {}
            for (s, h, rows, hs), st, dp_t in zip(items, scores, dps):
                probs_t, psink = _attn_probs(st, _lane_scalar(sk, h), valids[s])
                delta = jnp.sum(probs_t * dp_t, axis=0, keepdims=True)
                pts[s, h] = probs_t
                dsts[s, h] = probs_t * (dp_t - delta) * ATTN_SCALE
                dsink += jnp.where(lane == h, jnp.sum(-psink * delta, axis=1, keepdims=True), 0.0)
            for s, h, rows, hs in items:
                dqr_ref[rows, hs] = _dot(kwins_t[s][h // ATTN_GROUPS], dsts[s, h]).T
            for s in range(2):
                rows = slice(s * ATTN_BLOCK, (s + 1) * ATTN_BLOCK)
                for hk in range(ATTN_KV_HEADS):
                    ks = slice(hk * ATTN_HEAD_DIM, (hk + 1) * ATTN_HEAD_DIM)
                    group = range(hk * ATTN_GROUPS, (hk + 1) * ATTN_GROUPS)
                    heads = [slice(h * ATTN_HEAD_DIM, (h + 1) * ATTN_HEAD_DIM) for h in group]
                    ds_g = jnp.concatenate([dsts[s, h] for h in group], axis=1)
                    p_g = jnp.concatenate([pts[s, h] for h in group], axis=1)
                    q_g = jnp.concatenate([q[rows, hs] for hs in heads], axis=0)
                    do_g = jnp.concatenate([do[rows, hs] for hs in heads], axis=0)
                    dkw_ref[s, :, ks] = _dot(ds_g, q_g)
                    dvw_ref[s, :, ks] = _dot(p_g, do_g)
            acc_ref[0:1, :] += dsink
            dq_ref[...] = _rope_bwd(dqr_ref[...], cq, sq).astype(dq_ref.dtype)
            dk_ref[lo, :] = ck_ref[lo, :].astype(dk_ref.dtype)
            dk_ref[hi, :] = (ck_ref[hi, :] + _rope_bwd(dkw_ref[0, lo, :], cp_ref[...], sp_ref[...])).astype(dk_ref.dtype)
            dv_ref[lo, :] = cv_ref[lo, :].astype(dv_ref.dtype)
            dv_ref[hi, :] = (cv_ref[hi, :] + dvw_ref[0, lo, :]).astype(dv_ref.dtype)
            ck_ref[lo, :] = _rope_bwd(dkw_ref[0, hi, :] + dkw_ref[1, lo, :], cc[lo, :], sc[lo, :])
            ck_ref[hi, :] = _rope_bwd(dkw_ref[1, hi, :], cc[hi, :], sc[hi, :])
            cv_ref[lo, :] = dvw_ref[0, hi, :] + dvw_ref[1, lo, :]
            cv_ref[hi, :] = dvw_ref[1, hi, :]

        @pl.when(i == ns)
        def _():
            dk_ref[...] = ck_ref[...].astype(dk_ref.dtype)
            dv_ref[...] = cv_ref[...].astype(dv_ref.dtype)

    prev_out = lambda w: pl.BlockSpec((ATTN_STEP, w), lambda i: (jnp.maximum(i - 1, 0), 0))
    return pl.pallas_call(
        body, name="attn_bwd", grid=(ns + 1,),
        in_specs=_attn_specs(ns) + [pl.BlockSpec((ATTN_STEP, ATTN_Q), lambda i: (jnp.minimum(i, ns - 1), 0))],
        out_specs=[pl.BlockSpec((ATTN_STEP, ATTN_Q), lambda i: (jnp.minimum(i, ns - 1), 0)), prev_out(ATTN_KV),
                   prev_out(ATTN_KV), _full((8, 128))],
        out_shape=[jax.ShapeDtypeStruct((t, ATTN_Q), MXU_DTYPE), jax.ShapeDtypeStruct((t, ATTN_KV), MXU_DTYPE),
                   jax.ShapeDtypeStruct((t, ATTN_KV), MXU_DTYPE), jax.ShapeDtypeStruct((8, 128), F32)],
        scratch_shapes=[pltpu.VMEM((ATTN_STEP, ATTN_Q), F32), pltpu.VMEM((2, ATTN_STEP, ATTN_KV), F32),
                        pltpu.VMEM((2, ATTN_STEP, ATTN_KV), F32), pltpu.VMEM((ATTN_STEP, ATTN_KV), F32),
                        pltpu.VMEM((ATTN_STEP, ATTN_KV), F32)],
        compiler_params=_params("arbitrary"),
    )(pa, pa, pa, pa, pa, cos, sin, cos, sin, sinks_vec, dao)


PAIR = 2 * DN_CHUNK
INTRA_PAIRS = 4
SCAN_PAIRS = 4
HALO = 8


def _conv_window(cur_ref, prev_ref, xs_ref, tm, has_prev):
    prev = jnp.where(has_prev, prev_ref[...], 0.0)
    xs_ref[0:HALO, :] = prev
    xs_ref[HALO:HALO + tm, :] = cur_ref[...]


def _conv_taps(xs_ref, cw_ref, tm):
    y = cw_ref[0:1, :] * xs_ref[pl.ds(HALO - DN_CONV + 1, tm), :]
    for j in range(1, DN_CONV):
        y += cw_ref[j:j + 1, :] * xs_ref[pl.ds(HALO - DN_CONV + 1 + j, tm), :]
    return y


def _gate_values(ba, al, dt):
    beta = _sigmoid(ba)
    pre = ba + dt
    g = -jnp.exp(al) * _softplus(pre)
    return beta, g, pre


def _dn_prep_specs(tm, tile):
    return [pl.BlockSpec((tm, CONV_CH), lambda i: (tile(i), 0)),
            pl.BlockSpec((HALO, CONV_CH), lambda i: (jnp.maximum(tile(i) * (tm // HALO) - 1, 0), 0)),
            pl.BlockSpec((tm, 128), lambda i: (tile(i), 4 * DN_W // 128)),
            _full((DN_CONV, CONV_CH)), _full((1, 128)), _full((1, 128))]


def _dn_prep(pd, conv_w, al_vec, dt_vec, tm):
    t = pd.shape[0]

    def body(cur_ref, prev_ref, ba_ref, cw_ref, al_ref, dt_ref, qn_ref, kn_ref, vc_ref, gc_ref, gr_ref, xs_ref):
        _conv_window(cur_ref, prev_ref, xs_ref, tm, pl.program_id(0) > 0)
        y = _conv_taps(xs_ref, cw_ref, tm)
        c = y * _sigmoid(y)
        for h in range(DN_HEADS):
            qs = slice(h * DN_HEAD_DIM, (h + 1) * DN_HEAD_DIM)
            ksl = slice(DN_W + h * DN_HEAD_DIM, DN_W + (h + 1) * DN_HEAD_DIM)
            qh, kh = c[:, qs], c[:, ksl]
            qn_ref[:, qs] = qh * lax.rsqrt(jnp.sum(qh * qh, axis=-1, keepdims=True) + EPS) * DN_SCALE
            kn_ref[:, qs] = kh * lax.rsqrt(jnp.sum(kh * kh, axis=-1, keepdims=True) + EPS)
        vc_ref[...] = c[:, 2 * DN_W:3 * DN_W]
        beta, g, _ = _gate_values(ba_ref[...], al_ref[...], dt_ref[...])
        lane = lax.broadcasted_iota(jnp.int32, beta.shape, 1)
        gb = jnp.where(lane < DN_HEADS, beta, jnp.where(lane < 2 * DN_HEADS, g, 0.0))
        gc_ref[...] = gb
        gr_ref[...] = gb.T[0:8, :]

    tok = lambda w: pl.BlockSpec((tm, w), lambda i: (i, 0))
    return pl.pallas_call(
        body, name="dn_prep", grid=(t // tm,),
        in_specs=_dn_prep_specs(tm, lambda i: i),
        out_specs=[tok(DN_W), tok(DN_W), tok(DN_W), tok(128), pl.BlockSpec((8, tm), lambda i: (0, i))],
        out_shape=[jax.ShapeDtypeStruct((t, DN_W), F32)] * 3 + [jax.ShapeDtypeStruct((t, 128), F32),
                                                                 jax.ShapeDtypeStruct((8, t), F32)],
        scratch_shapes=[pltpu.VMEM((HALO + tm, CONV_CH), F32)],
        compiler_params=_params("parallel"),
    )(pd, pd, pd, conv_w, al_vec, dt_vec)


def _pair_masks():
    r = lax.broadcasted_iota(jnp.int32, (PAIR, PAIR), 0)
    c = lax.broadcasted_iota(jnp.int32, (PAIR, PAIR), 1)
    same = (r < DN_CHUNK) == (c < DN_CHUNK)
    return same & (r >= c), same & (r > c)


def _lane_col(mat, idx):
    lane = lax.broadcasted_iota(jnp.int32, mat.shape, 1)
    return jnp.sum(jnp.where(lane == idx, mat, 0.0), axis=-1, keepdims=True)


def _pair_cumsums(gc, gr, low):
    lowf = low.astype(F32)
    return _dot(lowf, gc, NN, HI), _dot(gr, lowf, NT, HI)


def _pair_gates(gc, cum_c, cum_r, low, h):
    beta = _lane_col(gc, h)
    gam = _lane_col(cum_c, DN_HEADS + h)
    gam_row = cum_r[DN_HEADS + h:DN_HEADS + h + 1, :]
    dm = jnp.where(low, jnp.exp(jnp.where(low, gam - gam_row, 0.0)), 0.0)
    row = lax.broadcasted_iota(jnp.int32, gam.shape, 0)
    gl = jnp.where(row < DN_CHUNK, gam[DN_CHUNK - 1:DN_CHUNK, :], gam[PAIR - 1:PAIR, :])
    return beta, gam, dm, gl


def _split(a):
    hi = a.astype(BF16)
    return hi, (a - hi.astype(F32)).astype(BF16)


def _dot_split(a, b, dims=NN):
    (ah, al), (bh, bl) = a, b
    la, lb = (1, 1) if dims == TN else ((0, 1) if dims == NN else (0, 0))
    r = _dot(jnp.concatenate([ah, al], axis=la), jnp.concatenate([bh, bl], axis=lb), dims)
    m, n = r.shape[0] // 2, r.shape[1] // 2
    return (r[m:, n:] + (r[:m, n:] + r[m:, :n])) + r[:m, :n]


def _unit_lower_inverses(lmats):
    n = lmats[0].shape[0]
    r = lax.broadcasted_iota(jnp.int32, (n, n), 0)
    c = lax.broadcasted_iota(jnp.int32, (n, n), 1)
    same = lambda size: (r & ~(size - 1)) == (c & ~(size - 1))
    base = DN_CHUNK // 4
    diag = [jnp.where(same(base), l, 0.0) for l in lmats]
    accs = [(r == c).astype(F32) - d for d in diag]
    splits = [_split(d) for d in diag]
    step = 1
    while 2 * step < base:
        splits = [_split(_dot_split(s, s)) for s in splits]
        accs = [acc + _dot_split(_split(acc), s) for acc, s in zip(accs, splits)]
        step *= 2
    size = base
    while size < DN_CHUNK:
        below = same(2 * size) & jnp.logical_not(same(size))
        tb = [_dot(acc, jnp.where(below, l, 0.0)) for acc, l in zip(accs, lmats)]
        accs = [acc - _dot(t, acc) for acc, t in zip(accs, tb)]
        size *= 2
    return accs


def _dn_intra(qn, kn, vc, gc, gr):
    t = qn.shape[0]
    npair = t // PAIR
    rows_step = INTRA_PAIRS * PAIR

    def body(q_ref, k_ref, v_ref, gc_ref, gr_ref, u_ref, w_ref, qg_ref, kd_ref, a_ref, ti_ref, dl_ref):
        low, strict = _pair_masks()
        items = []
        for p in range(INTRA_PAIRS):
            rows = slice(p * PAIR, (p + 1) * PAIR)
            gc_v = gc_ref[rows, :]
            cum_c, cum_r = _pair_cumsums(gc_v, gr_ref[:, rows], low)
            for h in range(DN_HEADS):
                hs = slice(h * DN_HEAD_DIM, (h + 1) * DN_HEAD_DIM)
                items.append((p, h, rows, hs, _pair_gates(gc_v, cum_c, cum_r, low, h)))
        lmats = []
        for p, h, rows, hs, (beta, gam, dm, gl) in items:
            k = k_ref[rows, hs]
            lmats.append(jnp.where(strict, _dot(k * beta, k, NT) * dm, 0.0))
        tinvs = _unit_lower_inverses(lmats)
        for (p, h, rows, hs, (beta, gam, dm, gl)), tinv in zip(items, tinvs):
            q, k, v = q_ref[rows, hs], k_ref[rows, hs], v_ref[rows, hs]
            eg = jnp.exp(gam)
            u_ref[rows, hs] = _dot(tinv, v * beta)
            w_ref[rows, hs] = _dot(tinv, (k * beta) * eg).astype(w_ref.dtype)
            a_ref[h, rows, :] = _dot(q, k, NT) * dm
            ti_ref[h, rows, :] = tinv
            qg_ref[rows, hs] = (q * eg).astype(qg_ref.dtype)
            kd_ref[rows, hs] = (k * jnp.exp(gl - gam)).astype(kd_ref.dtype)
            for c in range(2):
                last = (c + 1) * DN_CHUNK - 1
                dl_ref[2 * p + c, h] = jnp.broadcast_to(jnp.exp(gam[last:last + 1, :]), (8, 128))

    tok = lambda w: pl.BlockSpec((rows_step, w), lambda n: (n, 0))
    hm = pl.BlockSpec((DN_HEADS, rows_step, PAIR), lambda n: (0, n, 0))
    return pl.pallas_call(
        body, name="dn_intra", grid=(npair // INTRA_PAIRS,),
        in_specs=[tok(DN_W), tok(DN_W), tok(DN_W), tok(128), pl.BlockSpec((8, rows_step), lambda n: (0, n))],
        out_specs=[tok(DN_W)] * 4 + [hm, hm, pl.BlockSpec((2 * INTRA_PAIRS, DN_HEADS, 8, 128), lambda n: (n, 0, 0, 0))],
        out_shape=[jax.ShapeDtypeStruct((t, DN_W), F32)] + [jax.ShapeDtypeStruct((t, DN_W), MXU_DTYPE)] * 3
                  + [jax.ShapeDtypeStruct((DN_HEADS, t, PAIR), F32)] * 2
                  + [jax.ShapeDtypeStruct((2 * npair, DN_HEADS, 8, 128), F32)],
        compiler_params=_params("parallel"),
    )(qn, kn, vc, gc, gr)


def _dn_scan_fwd(u, w, qg, kd, a_qk, dlast, pd, dn_w):
    t = u.shape[0]
    npair = t // PAIR

    def body(u_ref, w_ref, qg_ref, kd_ref, a_ref, dl_ref, z_ref, nw_ref, out_ref, o_ref, vn_ref, sall_ref, s_ref):
        @pl.when(pl.program_id(0) == 0)
        def _():
            s_ref[...] = jnp.zeros_like(s_ref)

        nw = nw_ref[...]
        for c in range(2 * SCAN_PAIRS):
            rows = slice(c * DN_CHUNK, (c + 1) * DN_CHUNK)
            diag = slice((c % 2) * DN_CHUNK, (c % 2 + 1) * DN_CHUNK)
            for h in range(DN_HEADS):
                hs = slice(h * DN_HEAD_DIM, (h + 1) * DN_HEAD_DIM)
                st = s_ref[h]
                sall_ref[c, h] = st
                vn_ref[rows, hs] = (u_ref[rows, hs] - _dot(w_ref[rows, hs], st)).astype(vn_ref.dtype)
            for h in range(DN_HEADS):
                hs = slice(h * DN_HEAD_DIM, (h + 1) * DN_HEAD_DIM)
                st, vn = s_ref[h], vn_ref[rows, hs]
                o = _dot(qg_ref[rows, hs], st) + _dot(a_ref[h, rows, diag], vn)
                s_ref[h] = st * dl_ref[c, h][0:1, :] + _dot(kd_ref[rows, hs], vn, TN)
                o_ref[rows, hs] = o
                z = z_ref[rows, hs]
                on = o * lax.rsqrt(jnp.mean(o * o, axis=-1, keepdims=True) + EPS) * nw
                out_ref[rows, hs] = (on * (z * _sigmoid(z))).astype(out_ref.dtype)

    rows_step = SCAN_PAIRS * PAIR
    tok = pl.BlockSpec((rows_step, DN_W), lambda n: (n, 0))
    hm = pl.BlockSpec((DN_HEADS, rows_step, PAIR), lambda n: (0, n, 0))
    return pl.pallas_call(
        body, name="dn_scan_fwd", grid=(npair // SCAN_PAIRS,),
        in_specs=[tok, tok, tok, tok, hm, pl.BlockSpec((2 * SCAN_PAIRS, DN_HEADS, 8, 128), lambda n: (n, 0, 0, 0)),
                  pl.BlockSpec((rows_step, DN_W), lambda n: (n, 3)), _full((1, 128))],
        out_specs=[tok, tok, tok,
                   pl.BlockSpec((2 * SCAN_PAIRS, DN_HEADS, DN_HEAD_DIM, DN_HEAD_DIM), lambda n: (n, 0, 0, 0))],
        out_shape=[jax.ShapeDtypeStruct((t, DN_W), MXU_DTYPE), jax.ShapeDtypeStruct((t, DN_W), F32),
                   jax.ShapeDtypeStruct((t, DN_W), MXU_DTYPE),
                   jax.ShapeDtypeStruct((2 * npair, DN_HEADS, DN_HEAD_DIM, DN_HEAD_DIM), F32)],
        scratch_shapes=[pltpu.VMEM((DN_HEADS, DN_HEAD_DIM, DN_HEAD_DIM), F32)],
        compiler_params=_params("arbitrary"),
    )(u, w, qg, kd, a_qk, dlast, pd, dn_w)


def _dn_scan_bwd(dout, o, vnew, sall, w, qg, kd, a_qk, dlast, pd, dn_w, dep):
    t = o.shape[0]
    npair = t // PAIR
    nstep = npair // SCAN_PAIRS
    rev = lambda n: nstep - 1 - n

    def body(do_ref, o_ref, vn_ref, sall_ref, w_ref, qg_ref, kd_ref, a_ref, dl_ref, z_ref, nw_ref, dep_ref,
             dz_ref, du_ref, dw_ref, dqg_ref, dkd_ref, da_ref, ddl_ref, acc_ref, ds_ref, dos_ref):
        @pl.when(pl.program_id(0) == 0)
        def _():
            ds_ref[...] = jnp.zeros_like(ds_ref)
            acc_ref[...] = jnp.zeros_like(acc_ref)

        nw = nw_ref[...]
        dnw = jnp.zeros((1, 128), F32)
        for h in range(DN_HEADS):
            hs = slice(h * DN_HEAD_DIM, (h + 1) * DN_HEAD_DIM)
            o, z, dout = o_ref[:, hs], z_ref[:, hs], do_ref[:, hs]
            r = lax.rsqrt(jnp.mean(o * o, axis=-1, keepdims=True) + EPS)
            oh = o * r
            sz = _sigmoid(z)
            dz_ref[:, hs] = dout * (oh * nw) * (sz + z * sz * (1.0 - sz))
            don = dout * (z * sz)
            dnw += jnp.sum(don * oh, axis=0, keepdims=True)
            doh = don * nw
            dos_ref[:, hs] = r * (doh - oh * jnp.mean(doh * oh, axis=-1, keepdims=True))
        acc_ref[0:1, :] += dnw
        for c in reversed(range(2 * SCAN_PAIRS)):
            rows = slice(c * DN_CHUNK, (c + 1) * DN_CHUNK)
            diag = slice((c % 2) * DN_CHUNK, (c % 2 + 1) * DN_CHUNK)
            other = slice((1 - c % 2) * DN_CHUNK, (2 - c % 2) * DN_CHUNK)
            for h in range(DN_HEADS):
                hs = slice(h * DN_HEAD_DIM, (h + 1) * DN_HEAD_DIM)
                do, st, dsp, vn = dos_ref[rows, hs], sall_ref[c, h], ds_ref[h], vn_ref[rows, hs]
                da_ref[h, rows, diag] = _dot(do, vn, NT)
                da_ref[h, rows, other] = jnp.zeros((DN_CHUNK, DN_CHUNK), F32)
                du_ref[rows, hs] = (_dot(a_ref[h, rows, diag], do, TN) + _dot(kd_ref[rows, hs], dsp)).astype(du_ref.dtype)
                dqg_ref[rows, hs] = _dot(do, st, NT)
                dkd_ref[rows, hs] = _dot(vn, dsp, NT)
                ddl = jnp.sum(jnp.sum(dsp * st, axis=1, keepdims=True), axis=0, keepdims=True)
                ddl_ref[c, h] = jnp.broadcast_to(ddl, (8, 128))
            for h in range(DN_HEADS):
                hs = slice(h * DN_HEAD_DIM, (h + 1) * DN_HEAD_DIM)
                do, st, dvn = dos_ref[rows, hs], sall_ref[c, h], du_ref[rows, hs]
                dw_ref[rows, hs] = (-_dot(dvn, st, NT)).astype(dw_ref.dtype)
                ds_ref[h] = (ds_ref[h] * dl_ref[c, h][0:1, :] + _dot(qg_ref[rows, hs], do, TN)
                             - _dot(w_ref[rows, hs], dvn, TN))

    rows_step = SCAN_PAIRS * PAIR
    tok = pl.BlockSpec((rows_step, DN_W), lambda n: (rev(n), 0))
    hm = pl.BlockSpec((DN_HEADS, rows_step, PAIR), lambda n: (0, rev(n), 0))
    sc = pl.BlockSpec((2 * SCAN_PAIRS, DN_HEADS, 8, 128), lambda n: (rev(n), 0, 0, 0))
    return pl.pallas_call(
        body, name="dn_scan_bwd", grid=(nstep,),
        in_specs=[tok, tok, tok,
                  pl.BlockSpec((2 * SCAN_PAIRS, DN_HEADS, DN_HEAD_DIM, DN_HEAD_DIM), lambda n: (rev(n), 0, 0, 0)),
                  tok, tok, tok, hm, sc, pl.BlockSpec((rows_step, DN_W), lambda n: (rev(n), 3)), _full((1, 128)),
                  pl.BlockSpec(memory_space=pl.ANY)],
        out_specs=[tok] * 5 + [hm, sc, _full((8, 128))],
        out_shape=[jax.ShapeDtypeStruct((t, DN_W), F32)] + [jax.ShapeDtypeStruct((t, DN_W), MXU_DTYPE)] * 2
                  + [jax.ShapeDtypeStruct((t, DN_W), F32)] * 2 + [jax.ShapeDtypeStruct((DN_HEADS, t, PAIR), F32),
                   jax.ShapeDtypeStruct((2 * npair, DN_HEADS, 8, 128), F32), jax.ShapeDtypeStruct((8, 128), F32)],
        scratch_shapes=[pltpu.VMEM((DN_HEADS, DN_HEAD_DIM, DN_HEAD_DIM), F32), pltpu.VMEM((SCAN_PAIRS * PAIR, DN_W), F32)],
        compiler_params=_params("arbitrary"),
    )(dout, o, vnew, sall, w, qg, kd, a_qk, dlast, pd, dn_w, dep)


def _dn_intra_bwd(qn, kn, vc, gc, gr, tinv, a_qk, du, dw, dqg, dkd, da_qk, ddlast, dlast, dep):
    t = qn.shape[0]
    npair = t // PAIR

    def body(q_ref, k_ref, v_ref, gc_ref, gr_ref, ti_ref, a_ref, du_ref, dw_ref, dqg_ref, dkd_ref, da_ref, ddl_ref, dl_ref,
             dep_ref, dq_ref, dk_ref, dv_ref, dg_ref):
        low, strict = _pair_masks()
        lane = lax.broadcasted_iota(jnp.int32, (PAIR, 128), 1)
        rowi = lax.broadcasted_iota(jnp.int32, (PAIR, 1), 0)
        rsum = lambda v: jnp.sum(v, axis=-1, keepdims=True)
        items = []
        for p in range(INTRA_PAIRS):
            rows = slice(p * PAIR, (p + 1) * PAIR)
            gc_v = gc_ref[rows, :]
            cum_c, cum_r = _pair_cumsums(gc_v, gr_ref[:, rows], low)
            for h in range(DN_HEADS):
                hs = slice(h * DN_HEAD_DIM, (h + 1) * DN_HEAD_DIM)
                items.append((p, h, rows, hs, _pair_gates(gc_v, cum_c, cum_r, low, h)))
        dtis, lmats, dvbs, dkbgs = [], [], [], []
        for p, h, rows, hs, (beta, gam, dm, gl) in items:
            k, tinv = k_ref[rows, hs], ti_ref[h, rows, :]
            kb = k * beta
            dtis.append(_dot(du_ref[rows, hs], v_ref[rows, hs] * beta, NT)
                        + _dot(dw_ref[rows, hs], kb * jnp.exp(gam), NT))
            lmats.append(jnp.where(strict, _dot(kb, k, NT) * dm, 0.0))
            dvbs.append(_dot(tinv, du_ref[rows, hs], TN))
            dkbgs.append(_dot(tinv, dw_ref[rows, hs], TN))
        xs = [_dot(ti_ref[h, rows, :], dti, TN) for (p, h, rows, hs, g), dti in zip(items, dtis)]
        dls = [jnp.where(strict, -_dot(x, ti_ref[h, rows, :], NT), 0.0) for (p, h, rows, hs, g), x in zip(items, xs)]
        dgam_all = [jnp.zeros((PAIR, 128), F32) for _ in range(INTRA_PAIRS)]
        dbeta_all = [jnp.zeros((PAIR, 128), F32) for _ in range(INTRA_PAIRS)]
        for (p, h, rows, hs, (beta, gam, dm, gl)), dl, lmat, dvb, dkbg in zip(items, dls, lmats, dvbs, dkbgs):
            q, k, v = q_ref[rows, hs], k_ref[rows, hs], v_ref[rows, hs]
            a = a_ref[h, rows, :]
            dqg, dkd = dqg_ref[rows, hs], dkd_ref[rows, hs]
            kb = k * beta
            eg = jnp.exp(gam)
            ekd = jnp.exp(gl - gam)
            dmm = dl * dm
            dam = jnp.where(low, da_ref[h, rows, :], 0.0)
            dn = dam * dm
            e = dl * lmat + dam * a
            dkb = _dot(dmm, k) + dkbg * eg
            dk_ref[rows, hs] = _dot(dmm, kb, TN) + _dot(dn, q, TN) + dkd * ekd + dkb * beta
            dq_ref[rows, hs] = _dot(dn, k) + dqg * eg
            dv_ref[rows, hs] = dvb * beta
            t_kd = rsum(dkd * (k * ekd))
            dgam = rsum(e) - rsum(e.T) + rsum(dqg * (q * eg)) + rsum(dkbg * (kb * eg)) - t_kd
            for c in range(2):
                crows = slice(c * DN_CHUNK, (c + 1) * DN_CHUNK)
                dgl = (jnp.sum(t_kd[crows, :], axis=0, keepdims=True)
                       + ddl_ref[2 * p + c, h][0:1, 0:1] * dl_ref[2 * p + c, h][0:1, 0:1])
                dgam = dgam + jnp.where(rowi == (c + 1) * DN_CHUNK - 1, dgl, 0.0)
            dgam_all[p] += jnp.where(lane == DN_HEADS + h, dgam, 0.0)
            dbeta_all[p] += jnp.where(lane == h, rsum(dkb * k) + rsum(dvb * v), 0.0)
        for p in range(INTRA_PAIRS):
            dg_ref[p * PAIR:(p + 1) * PAIR, :] = dbeta_all[p] + _dot(low.astype(F32), dgam_all[p], TN, HI)

    rows_step = INTRA_PAIRS * PAIR
    tok = lambda w: pl.BlockSpec((rows_step, w), lambda n: (n, 0))
    hm = pl.BlockSpec((DN_HEADS, rows_step, PAIR), lambda n: (0, n, 0))
    sc = pl.BlockSpec((2 * INTRA_PAIRS, DN_HEADS, 8, 128), lambda n: (n, 0, 0, 0))
    return pl.pallas_call(
        body, name="dn_intra_bwd", grid=(npair // INTRA_PAIRS,),
        in_specs=[tok(DN_W), tok(DN_W), tok(DN_W), tok(128), pl.BlockSpec((8, rows_step), lambda n: (0, n)), hm, hm,
                  tok(DN_W), tok(DN_W), tok(DN_W), tok(DN_W), hm, sc, sc, pl.BlockSpec(memory_space=pl.ANY)],
        out_specs=[tok(DN_W), tok(DN_W), tok(DN_W), tok(128)],
        out_shape=[jax.ShapeDtypeStruct((t, DN_W), F32)] * 3 + [jax.ShapeDtypeStruct((t, 128), F32)],
        compiler_params=_params("parallel"),
    )(qn, kn, vc, gc, gr, tinv, a_qk, du, dw, dqg, dkd, da_qk, ddlast, dlast, dep)


def _dn_prep_bwd(pd, conv_w, al_vec, dt_vec, dqn, dkn, dvc, dgc, dz, tm):
    t = pd.shape[0]
    nt = t // tm
    tile = lambda i: nt - 1 - i

    def body(cur_ref, prev_ref, ba_ref, cw_ref, al_ref, dt_ref, dq_ref, dk_ref, dv_ref, dg_ref, dz_ref,
             o_ref, accw_ref, accg_ref, xs_ref, dc_ref, ds_ref, carry_ref):
        @pl.when(pl.program_id(0) == 0)
        def _():
            accw_ref[...] = jnp.zeros_like(accw_ref)
            accg_ref[...] = jnp.zeros_like(accg_ref)
            carry_ref[...] = jnp.zeros_like(carry_ref)

        _conv_window(cur_ref, prev_ref, xs_ref, tm, tile(pl.program_id(0)) > 0)
        taps = [xs_ref[pl.ds(HALO - DN_CONV + 1 + j, tm), :] for j in range(DN_CONV)]
        y = cw_ref[0:1, :] * taps[0]
        for j in range(1, DN_CONV):
            y += cw_ref[j:j + 1, :] * taps[j]
        sg = _sigmoid(y)
        c = y * sg
        for h in range(DN_HEADS):
            qs = slice(h * DN_HEAD_DIM, (h + 1) * DN_HEAD_DIM)
            ksl = slice(DN_W + h * DN_HEAD_DIM, DN_W + (h + 1) * DN_HEAD_DIM)
            for src, sl, scale in ((dq_ref, qs, DN_SCALE), (dk_ref, ksl, 1.0)):
                xh = c[:, sl]
                r = lax.rsqrt(jnp.sum(xh * xh, axis=-1, keepdims=True) + EPS)
                unit = xh * r
                dn = src[:, qs] * scale
                dc_ref[:, sl] = r * (dn - unit * jnp.sum(dn * unit, axis=-1, keepdims=True))
        dc_ref[:, 2 * DN_W:3 * DN_W] = dv_ref[...]
        dy = dc_ref[...] * (sg + y * sg * (1.0 - sg))
        for j in range(DN_CONV):
            accw_ref[j:j + 1, :] += jnp.sum(dy * taps[j], axis=0, keepdims=True)
        ds_ref[0:tm, :] = dy
        ds_ref[tm:tm + HALO, :] = carry_ref[...]
        carry_ref[...] = ds_ref[0:HALO, :]
        dx = cw_ref[0:1, :] * ds_ref[pl.ds(DN_CONV - 1, tm), :]
        for j in range(1, DN_CONV):
            dx += cw_ref[j:j + 1, :] * ds_ref[pl.ds(DN_CONV - 1 - j, tm), :]

        beta, g, pre = _gate_values(ba_ref[...], al_ref[...], dt_ref[...])
        dgb = dg_ref[...]
        lane = lax.broadcasted_iota(jnp.int32, dgb.shape, 1)
        is_b, is_a = lane < DN_HEADS, (lane >= DN_HEADS) & (lane < 2 * DN_HEADS)
        dpre = dgb * (-jnp.exp(al_ref[...])) * _sigmoid(pre)
        dba = jnp.where(is_b, dgb * beta * (1.0 - beta), jnp.where(is_a, dpre, 0.0))
        accg_ref[0:1, :] += jnp.sum(jnp.where(is_a, dgb * g, 0.0), axis=0, keepdims=True)
        accg_ref[1:2, :] += jnp.sum(jnp.where(is_a, dpre, 0.0), axis=0, keepdims=True)
        o_ref[:, 0:CONV_CH] = dx.astype(o_ref.dtype)
        o_ref[:, CONV_CH:CONV_CH + DN_W] = dz_ref[...].astype(o_ref.dtype)
        o_ref[:, CONV_CH + DN_W:DN_COLS] = dba.astype(o_ref.dtype)

    tok = lambda w: pl.BlockSpec((tm, w), lambda i: (tile(i), 0))
    return pl.pallas_call(
        body, name="dn_prep_bwd", grid=(nt,),
        in_specs=_dn_prep_specs(tm, tile) + [tok(DN_W), tok(DN_W), tok(DN_W), tok(128), tok(DN_W)],
        out_specs=[tok(DN_COLS), _full((8, CONV_CH)), _full((8, 128))],
        out_shape=[jax.ShapeDtypeStruct((t, DN_COLS), MXU_DTYPE),
                   jax.ShapeDtypeStruct((8, CONV_CH), F32), jax.ShapeDtypeStruct((8, 128), F32)],
        scratch_shapes=[pltpu.VMEM((HALO + tm, CONV_CH), F32), pltpu.VMEM((tm, CONV_CH), F32),
                        pltpu.VMEM((tm + HALO, CONV_CH), F32), pltpu.VMEM((HALO, CONV_CH), F32)],
        compiler_params=_params("arbitrary"),
    )(pd, pd, pd, conv_w, al_vec, dt_vec, dqn, dkn, dvc, dgc, dz)


def _pad_lanes(v, offset=0):
    return jnp.pad(v.astype(F32), (offset, 128 - offset - v.shape[0]))[None]


class _LocalReducer:
    def start(self, grads):
        return jnp.zeros((8, 128), F32)

    def middle(self, after):
        return jnp.zeros((8, 128), F32)

    def finish(self, after):
        return None


def _local_step(x, p, tgt, sm, w, late, reducer):
    t = x.shape[0]
    tm = min(512, t // 2)
    tm_s = min(512, t // 2)
    tw = min(1024, t // 2)
    tw_ff = min(2048, t // 2)

    attn_cols = ATTN_Q + 2 * ATTN_KV
    w_in_t = w["w_in_t"]
    w_in_t = jnp.pad(w_in_t, ((0, max(0, attn_cols + DN_COLS - w_in_t.shape[0])), (0, 0)))
    wa_t = w_in_t[:attn_cols]
    wd_t = w_in_t[attn_cols:attn_cols + DN_COLS]
    conv_w = w["conv_w"]
    al_vec, dt_vec = _pad_lanes(sm["a_log"], DN_HEADS), _pad_lanes(sm["dt_bias"], DN_HEADS)
    sinks_vec = _pad_lanes(sm["sinks"])
    dn_w = sm["dn_norm"].reshape(1, 128)
    row = lambda v: v.reshape(1, D_MODEL)
    cos, sin = w["rope"] if "rope" in w else _rope_rows(_rope_lanes(), 0, t)

    u, pa, pd = _inproj(x, row(sm["norm_mix"]), wa_t, wd_t, tm_s)
    ao = _attn_fwd(pa, cos, sin, sinks_vec)
    qn, kn, vc, gc, gr = _dn_prep(pd, conv_w, al_vec, dt_vec, tm_s)
    uu, ww, qg, kd, a_qk, tinv, dlast = _dn_intra(qn, kn, vc, gc, gr)
    dn_out, o, vnew, sall = _dn_scan_fwd(uu, ww, qg, kd, a_qk, dlast, pd, dn_w)
    w_o, late_rest = late(dn_out)
    wo_a, wo_d = w_o[:ATTN_Q], w_o[ATTN_Q:]
    h1 = _oproj(x, ao, dn_out, wo_a, wo_d, tm)
    w = dict(w, **late_rest(h1))
    w_proj = jnp.transpose(w["w_proj4"], (1, 0, 2)).reshape(PLE_DIM, D_MODEL)
    m, r, h2 = _mlp_fwd(h1, row(sm["norm_mlp"]), w["w_up4"], w["w_down"], tw)
    dh2, dh2b, dgp, dpp, n3, pb, acc_ple = _ple_loss(h2, p, tgt, row(sm["norm_ple"]), row(sm["norm_final"]),
                                                     w["w_gate"], w_proj, tm_s)
    g_w_gate = _wgrad(n3, dgp, "wgrad_gate", D_MODEL, D_MODEL, tw)
    g_w_proj = _wgrad(pb, dpp, "wgrad_proj", PLE_DIM, D_MODEL, tw)
    da, dh1, dh1b, dao, ddn, acc_mlp = _mlp_bwd(dh2, dh2b, r, h1, row(sm["norm_mlp"]), w["w_up4"], w["w_down"],
                                                wo_a, wo_d, tm)
    g_w_up4 = _wgrad(m, da, "wgrad_up", D_MODEL, FF_BLOCK, tw_ff, stacked=True)
    g_w_down = _wgrad(r, dh2b, "wgrad_down", FF_BLOCK, D_MODEL, tw_ff,
                      prep=lambda rv: jnp.square(rv.astype(F32)).astype(MXU_DTYPE))
    g_w_o = _wgrad_cat([ao, dn_out], [dh1b], "wgrad_o", tw)
    early = dict(w_up4=g_w_up4, w_down=g_w_down, w_gate=g_w_gate, w_proj=g_w_proj, w_o=g_w_o)
    dep = reducer.start(early)
    dz, du, dw, dqg, dkd, da_qk, ddlast, acc_dn = _dn_scan_bwd(ddn, o, vnew, sall, ww, qg, kd, a_qk, dlast, pd, dn_w,
                                                               dep)
    dep = reducer.middle(du)
    dqn, dkn, dvc, dgc = _dn_intra_bwd(qn, kn, vc, gc, gr, tinv, a_qk, du, dw, dqg, dkd, da_qk, ddlast, dlast, dep)
    d_dn, acc_conv, acc_gate = _dn_prep_bwd(pd, conv_w, al_vec, dt_vec, dqn, dkn, dvc, dgc, dz, tm_s)
    dq, dk, dv, acc_attn = _attn_bwd(pa, cos, sin, sinks_vec, dao)
    reducer.finish(dq)
    wq_t, wk_t, wv_t = wa_t[:ATTN_Q], wa_t[ATTN_Q:ATTN_Q + ATTN_KV], wa_t[ATTN_Q + ATTN_KV:]
    dx, acc_mix = _inproj_bwd(x, dh1, row(sm["norm_mix"]), [dq, dk, dv, d_dn], [wq_t, wk_t, wv_t, wd_t], tm_s)

    g_w_in_t = _wgrad_cat([dq, dk, dv, d_dn], [u], "wgrad_in", tw)
    grads = dict(early, w_in_t=g_w_in_t)
    sums = dict(loss=acc_ple[2, 0], norm_final=acc_ple[0], norm_ple=acc_ple[1], norm_mlp=acc_mlp[0], norm_mix=acc_mix[0],
                dn_norm=acc_dn[0], sinks=acc_attn[0, :ATTN_HEADS], a_log=acc_gate[0, DN_HEADS:2 * DN_HEADS],
                dt_bias=acc_gate[1, DN_HEADS:2 * DN_HEADS], conv_w=acc_conv[:DN_CONV])
    return sums, dx, grads


MESH = pl.DeviceIdType.MESH
ANY = pl.BlockSpec(memory_space=pl.ANY)
N_CHIPS = 4
N_DEV = 8


def _place():
    x, y, c = lax.axis_index("x"), lax.axis_index("y"), lax.axis_index("c")
    chips = [(1 - x, y), (x, 1 - y), (1 - x, 1 - y)]
    return x, y, c, chips


CAST_ROWS = 256
ROPE_ROWS = 512


def _gather_weights(shards, conv_s, casts, t):
    n, m = len(shards), len(casts)
    per = 7
    assert t % (2 * ROPE_ROWS) == 0
    cast_cols = max(a.shape[1] for a in casts)
    pieces = [(a, r0) for a, arr in enumerate(casts) for r0 in range(0, arr.shape[0], CAST_ROWS)]
    assert all(arr.shape[0] % CAST_ROWS == 0 and arr.shape[1] % 128 == 0 for arr in casts)

    def body(*refs):
        in_refs, conv_ref, cast_in, lanes_ref = refs[:n], refs[n], refs[n + 1:n + 1 + m], refs[n + 1 + m]
        refs = refs[n + 2 + m:]
        out_refs, conv_out, cast_out = refs[:n], refs[n], refs[n + 1:n + 1 + m]
        table_out = refs[n + 1 + m:n + 3 + m]
        send_sems, recv_sems, f32_buf, bf16_buf, cast_sems, table_buf, table_sems = refs[n + 3 + m:]
        x, y, c, chips = _place()

        def table_copy(j, which):
            return pltpu.make_async_copy(table_buf.at[j % 2, which], table_out[which].at[pl.ds(j * ROPE_ROWS, ROPE_ROWS), :],
                                         table_sems.at[j % 2, which])

        def rope_all():
            for j in range(t // ROPE_ROWS):
                if j >= 2:
                    table_copy(j - 2, 0).wait()
                    table_copy(j - 2, 1).wait()
                table_buf[j % 2, 0], table_buf[j % 2, 1] = _rope_rows(lanes_ref[...], j * ROPE_ROWS, ROPE_ROWS)
                table_copy(j, 0).start()
                table_copy(j, 1).start()
            for j in range(t // ROPE_ROWS - 2, t // ROPE_ROWS):
                table_copy(j, 0).wait()
                table_copy(j, 1).wait()

        def piece(i, store):
            a, r0 = pieces[i]
            cols, slot = cast_in[a].shape[1], i % 2
            if store:
                return pltpu.make_async_copy(bf16_buf.at[slot, :, pl.ds(0, cols)],
                                             cast_out[a].at[pl.ds(r0, CAST_ROWS), :], cast_sems.at[2 + slot])
            return pltpu.make_async_copy(cast_in[a].at[pl.ds(r0, CAST_ROWS), :],
                                         f32_buf.at[slot, :, pl.ds(0, cols)], cast_sems.at[slot])

        def cast_all():
            piece(0, False).start()
            for i in range(len(pieces)):
                if i + 1 < len(pieces):
                    piece(i + 1, False).start()
                piece(i, False).wait()
                if i >= 2:
                    piece(i - 2, True).wait()
                cols = cast_in[pieces[i][0]].shape[1]
                bf16_buf[i % 2, :, pl.ds(0, cols)] = f32_buf[i % 2, :, pl.ds(0, cols)].astype(BF16)
                piece(i, True).start()
            for i in range(max(0, len(pieces) - 2), len(pieces)):
                piece(i, True).wait()

        sibling = (x, y, 1 - c)

        def blk(a, px, py, pc):
            hr = in_refs[a].shape[0] // 2
            return out_refs[a].at[2 * px + py, pl.ds(pc * hr, hr), :]

        def mine(a):
            hr = in_refs[a].shape[0] // 2
            return in_refs[a].at[pl.ds(c * hr, hr), :]

        def rcopy(a, k, block, to, src=None):
            return pltpu.make_async_remote_copy(
                src_ref=blk(a, *block) if src is None else src, dst_ref=blk(a, *block),
                send_sem=send_sems.at[per * a + k], recv_sem=recv_sems.at[per * a + k],
                device_id=to, device_id_type=MESH)

        def whole(a, to):
            return pltpu.make_async_remote_copy(
                src_ref=in_refs[a], dst_ref=out_refs[a].at[2 * x + y],
                send_sem=send_sems.at[per * a], recv_sem=recv_sems.at[per * a], device_id=to, device_id_type=MESH)

        def ccopy(j, to):
            return pltpu.make_async_remote_copy(
                src_ref=conv_ref, dst_ref=conv_out.at[2 * x + y],
                send_sem=send_sems.at[per * n + j], recv_sem=recv_sems.at[per * n + j],
                device_id=to, device_id_type=MESH)

        started = []
        for a in range(n):
            first = [whole(a, sibling)]
            first += [rcopy(a, 1 + j, (x, y, c), (*chip, c), src=mine(a)) for j, chip in enumerate(chips)]
            for cp in first:
                cp.start()
            started += first
        conv_sends = [ccopy(j, (*chip, c)) for j, chip in enumerate(chips)] + [ccopy(3, sibling)]
        for cp in conv_sends:
            cp.start()
        started += conv_sends
        cast_all()
        rope_all()
        for a in range(n):
            for j, chip in enumerate(chips):
                rcopy(a, 1 + j, (*chip, c), (x, y, c)).wait_recv()
                fwd = rcopy(a, 4 + j, (*chip, c), sibling)
                fwd.start()
                started.append(fwd)
        for a in range(n):
            whole(a, sibling).wait_recv()
            for j, chip in enumerate(chips):
                rcopy(a, 4 + j, (*chip, 1 - c), (x, y, c)).wait_recv()
        for j, chip in enumerate(chips + [(x, y)]):
            pltpu.make_async_remote_copy(
                src_ref=conv_ref, dst_ref=conv_out.at[2 * chip[0] + chip[1]],
                send_sem=send_sems.at[per * n + j], recv_sem=recv_sems.at[per * n + j],
                device_id=sibling, device_id_type=MESH).wait_recv()
        for cp in started:
            cp.wait_send()

    nsem = per * n + 4
    out_shape = [jax.ShapeDtypeStruct((N_CHIPS,) + s.shape, s.dtype) for s in shards]
    out_shape.append(jax.ShapeDtypeStruct((N_CHIPS,) + conv_s.shape, conv_s.dtype))
    out_shape += [jax.ShapeDtypeStruct(a.shape, BF16) for a in casts]
    out_shape += [jax.ShapeDtypeStruct((t, 2 * ATTN_HEAD_DIM), F32)] * 2
    res = pl.pallas_call(
        body, name="gather_weights", in_specs=[ANY] * (n + 1 + m) + [pl.BlockSpec(memory_space=pltpu.VMEM)],
        out_specs=[ANY] * (n + 3 + m), out_shape=out_shape,
        scratch_shapes=[pltpu.SemaphoreType.DMA((nsem,)), pltpu.SemaphoreType.DMA((nsem,)),
                        pltpu.VMEM((2, CAST_ROWS, cast_cols), F32), pltpu.VMEM((2, CAST_ROWS, cast_cols), BF16),
                        pltpu.SemaphoreType.DMA((4,)),
                        pltpu.VMEM((2, 2, ROPE_ROWS, 2 * ATTN_HEAD_DIM), F32), pltpu.SemaphoreType.DMA((2, 2))],
    )(*shards, conv_s, *casts, _rope_lanes())
    return res[:n], res[n], res[n + 1:n + 1 + m], tuple(res[n + 1 + m:])


HBM = pl.BlockSpec(memory_space=pltpu.HBM)
SEM = pl.BlockSpec(memory_space=pltpu.SEMAPHORE)
EFFECT = pltpu.SideEffectType.DATAFLOW_SIDE_EFFECTING
LATE_COPIES = 7


def _late_copies(in_refs, land_refs, send_sems, recv_sems, only=None):
    x, y, c, chips = _place()
    sends, arrivals = [], []
    for a, (src, land) in enumerate(zip(in_refs, land_refs)):
        if only is not None and a not in only:
            continue
        hr = src.shape[0] // 2
        base = LATE_COPIES * a

        def cp(src_ref, dst_ref, s_idx, r_idx, to):
            return pltpu.make_async_remote_copy(src_ref=src_ref, dst_ref=dst_ref, send_sem=send_sems.at[base + s_idx],
                                                recv_sem=recv_sems.at[base + r_idx], device_id=to, device_id_type=MESH)

        sends.append(cp(src, land.at[2 * x + y], 0, 0, (x, y, 1 - c)))
        arrivals.append(cp(src, land.at[2 * x + y], 0, 0, (x, y, 1 - c)))
        for j, chip in enumerate(chips):
            for pc in range(2):
                half = src.at[pl.ds(c * hr, hr), :]
                sends.append(cp(half, land.at[2 * x + y, pl.ds(c * hr, hr), :], 1 + 2 * j + pc, 1 + 2 * j + c, (*chip, pc)))
                arrivals.append(cp(half, land.at[2 * chip[0] + chip[1], pl.ds(pc * hr, hr), :], 1 + 2 * j + pc,
                                   1 + 2 * j + pc, (*chip, pc)))
    return sends, arrivals


def _copies_start(name, build, nsem, srcs, land_shapes, after):
    n = len(srcs)

    def body(*refs):
        sends, _ = build(refs[:n], refs[n:2 * n], refs[2 * n + 1], refs[2 * n + 2])
        for cp in sends:
            cp.start()
        refs[-1][...] = jnp.zeros_like(refs[-1])

    lands = [pltpu.with_memory_space_constraint(lax.empty(s.shape, s.dtype), pltpu.HBM) for s in land_shapes]
    ins = [pltpu.with_memory_space_constraint(s, pltpu.HBM) for s in srcs]
    out = pl.pallas_call(
        body, name=name,
        out_shape=(pltpu.SemaphoreType.DMA((nsem,)), pltpu.SemaphoreType.DMA((nsem,)),
                   *[pltpu.HBM(s.shape, s.dtype) for s in srcs], *[pltpu.HBM(s.shape, s.dtype) for s in land_shapes],
                   jax.ShapeDtypeStruct((8, 128), F32)),
        in_specs=[HBM] * (2 * n) + [ANY],
        out_specs=(SEM, SEM, *[HBM] * (2 * n), pl.BlockSpec(memory_space=pltpu.VMEM)),
        input_output_aliases={i: 2 + i for i in range(2 * n)},
        compiler_params=pltpu.CompilerParams(has_side_effects=EFFECT),
    )(*ins, *lands, after)
    return out[0], out[1], out[2:2 + n], out[2 + n:2 + 2 * n], out[-1]


def _copies_wait(name, build, started, after):
    send_sems, recv_sems, srcs, lands, _ = started
    n = len(srcs)

    def body(*refs):
        sends, arrivals = build(refs[:n], refs[n:2 * n], refs[2 * n], refs[2 * n + 1])
        for cp in sends:
            cp.wait_send()
        for cp in arrivals:
            cp.wait_recv()

    out = pl.pallas_call(
        body, name=name,
        out_shape=(*[pltpu.HBM(s.shape, s.dtype) for s in srcs], *[pltpu.HBM(l.shape, l.dtype) for l in lands]),
        in_specs=[HBM] * (2 * n) + [SEM, SEM, ANY],
        out_specs=tuple([HBM] * (2 * n)),
        input_output_aliases={i: i for i in range(2 * n)},
        compiler_params=pltpu.CompilerParams(has_side_effects=EFFECT),
    )(*srcs, *lands, send_sems, recv_sems, after)
    return out[:n], out[n:]


def _exchange_copies(g_refs, got_refs, send_sems, recv_sems):
    x, y, c, _ = _place()
    sends, arrivals = [], []
    for a, (g, got) in enumerate(zip(g_refs, got_refs)):
        hr = g.shape[1] // 2
        cp = pltpu.make_async_remote_copy(
            src_ref=g.at[:, pl.ds((1 - c) * hr, hr), :], dst_ref=got, send_sem=send_sems.at[a],
            recv_sem=recv_sems.at[a], device_id=(x, y, 1 - c), device_id_type=MESH)
        sends.append(cp)
        arrivals.append(cp)
    return sends, arrivals


def _scatter_copies(s_refs, got_refs, send_sems, recv_sems):
    x, y, c, chips = _place()
    sends, arrivals = [], []
    for a, (s16, got) in enumerate(zip(s_refs, got_refs)):
        for j, chip in enumerate(chips):
            cp = pltpu.make_async_remote_copy(
                src_ref=s16.at[2 * chip[0] + chip[1]], dst_ref=got.at[j], send_sem=send_sems.at[3 * a + j],
                recv_sem=recv_sems.at[3 * a + j], device_id=(*chip, c), device_id_type=MESH)
            sends.append(cp)
            arrivals.append(cp)
    return sends, arrivals


def _share_halves(name, bufs, dep):
    n = len(bufs)

    def body(*refs):
        out_refs = refs[n + 1:2 * n + 1]
        send_sems, recv_sems = refs[2 * n + 1:]
        x, y, c, _ = _place()
        remote = [pltpu.make_async_remote_copy(
            src_ref=out_refs[a].at[c], dst_ref=out_refs[a].at[c], send_sem=send_sems.at[a], recv_sem=recv_sems.at[a],
            device_id=(x, y, 1 - c), device_id_type=MESH) for a in range(n)]
        for cp in remote:
            cp.start()
        for a in range(n):
            pltpu.make_async_remote_copy(
                src_ref=out_refs[a].at[c], dst_ref=out_refs[a].at[1 - c], send_sem=send_sems.at[a],
                recv_sem=recv_sems.at[a], device_id=(x, y, 1 - c), device_id_type=MESH).wait_recv()
        for cp in remote:
            cp.wait_send()

    return pl.pallas_call(
        body, name=name, in_specs=[ANY] * (n + 1), out_specs=[ANY] * n,
        out_shape=[jax.ShapeDtypeStruct(b.shape, b.dtype) for b in bufs],
        input_output_aliases={a: a for a in range(n)},
        scratch_shapes=[pltpu.SemaphoreType.DMA((n,)), pltpu.SemaphoreType.DMA((n,))],
    )(*bufs, dep)


SMALL_ROWS, SMALL_COLS = 16, CONV_CH
DN_NORM_LANE = 128


def _allreduce_small(block):
    m_per, ncol = block.shape

    def body(x_ref, sum_ref, all_ref, send_sems, recv_sems, local_sem):
        x, y, c, chips = _place()
        me, sibling = (x, y, c), (x, y, 1 - c)

        def rows(px, py, pc):
            return all_ref.at[pl.ds((4 * px + 2 * py + pc) * m_per, m_per), :]

        def copy(k, block_of, to, src=None):
            return pltpu.make_async_remote_copy(
                src_ref=rows(*block_of) if src is None else src, dst_ref=rows(*block_of),
                send_sem=send_sems.at[k], recv_sem=recv_sems.at[k], device_id=to, device_id_type=MESH)

        mine = pltpu.make_async_copy(x_ref, rows(*me), local_sem)
        mine.start()
        first = [copy(0, me, sibling, src=x_ref)]
        first += [copy(1 + j, me, (*chip, c), src=x_ref) for j, chip in enumerate(chips)]
        for cp in first:
            cp.start()
        passed = [copy(4 + j, (*chip, c), sibling) for j, chip in enumerate(chips)]
        for j, chip in enumerate(chips):
            copy(1 + j, (*chip, c), me).wait_recv()
            passed[j].start()
        copy(0, sibling, me).wait_recv()
        for j, chip in enumerate(chips):
            copy(4 + j, (*chip, 1 - c), me).wait_recv()
        for cp in first + passed:
            cp.wait_send()
        mine.wait()
        total = all_ref[0:m_per, :]
        for d in range(1, N_DEV):
            total = total + all_ref[d * m_per:(d + 1) * m_per, :]
        sum_ref[...] = total

    vm = pl.BlockSpec(memory_space=pltpu.VMEM)
    return pl.pallas_call(
        body, name="allreduce_small", in_specs=[vm], out_specs=vm,
        out_shape=jax.ShapeDtypeStruct((m_per, ncol), F32),
        scratch_shapes=[pltpu.VMEM((N_DEV * m_per, ncol), F32), pltpu.SemaphoreType.DMA((7,)),
                        pltpu.SemaphoreType.DMA((7,)), pltpu.SemaphoreType.DMA],
    )(block)


def _row_tile(rows, cols):
    tile = rows
    while tile * cols * 4 > (1 << 20) and tile % 16 == 0:
        tile //= 2
    return tile


def _elementwise(fn, name, ins, out_dtypes, dep):
    rows, cols = ins[0].shape
    tile = _row_tile(rows, cols)

    def body(*refs):
        outs = fn(*[r[...] for r in refs[:len(ins)]])
        for o_ref, o in zip(refs[len(ins) + 1:], outs):
            o_ref[...] = o.astype(o_ref.dtype)

    if tile * cols * 4 > (1 << 21) and cols % 512 == 0:
        spec = pl.BlockSpec((rows, 256), lambda i: (0, i))
        steps = cols // 256
    else:
        spec = pl.BlockSpec((tile, cols), lambda i: (i, 0))
        steps = rows // tile
    return pl.pallas_call(
        body, name=name, grid=(steps,), in_specs=[spec] * len(ins) + [pl.BlockSpec(memory_space=pl.ANY)],
        out_specs=[spec] * len(out_dtypes),
        out_shape=[jax.ShapeDtypeStruct((rows, cols), d) for d in out_dtypes],
        compiler_params=_params("parallel"),
    )(*ins, dep)


def _adamw_tile(w, g, m, v):
    m = ADAM_B1 * m + (1.0 - ADAM_B1) * g
    v = ADAM_B2 * v + (1.0 - ADAM_B2) * jnp.square(g)
    m_hat = m / (1.0 - ADAM_B1 ** ADAM_STEP)
    v_hat = v / (1.0 - ADAM_B2 ** ADAM_STEP)
    delta = -ADAM_LR * (m_hat / (jnp.sqrt(v_hat) + ADAM_EPS) + ADAM_WD * w)
    return delta, m, v


def _adamw(name, w, g, m, v, dep):
    return _elementwise(_adamw_tile, name, [w, g, m, v], [F32, F32, F32], dep)


def _chip_sum(name, g4, got, place):
    nchip, hr, cols = got.shape
    tile = _row_tile(hr, cols)
    nblk = hr // tile

    def body(pl_ref, g_ref, o_ref, s32_ref, s16_ref):
        s = g_ref[...] + o_ref[...]
        s16_ref[...] = s.astype(BF16)

        @pl.when(pl.program_id(1) == pl_ref[0])
        def _():
            s32_ref[...] = s

    spec = pl.BlockSpec((None, tile, cols), lambda i, k, pr: (k, i, 0))
    return pl.pallas_call(
        body, name=name,
        grid_spec=pltpu.PrefetchScalarGridSpec(
            num_scalar_prefetch=1, grid=(nblk, nchip),
            in_specs=[pl.BlockSpec((None, tile, cols), lambda i, k, pr: (k, pr[1] * nblk + i, 0)), spec],
            out_specs=[pl.BlockSpec((tile, cols), lambda i, k, pr: (i, 0)), spec]),
        out_shape=[jax.ShapeDtypeStruct((hr, cols), F32), jax.ShapeDtypeStruct(got.shape, BF16)],
        compiler_params=_params("parallel", "arbitrary"),
    )(place, g4, got)


def _mesh_sum(name, s32, got, place):
    hr, cols = s32.shape
    tile = _row_tile(hr, cols)

    def body(pl_ref, own_ref, g0_ref, g1_ref, g2_ref, o_ref):
        o_ref[...] = ((own_ref[...] + g0_ref[...].astype(F32)) + g1_ref[...].astype(F32)) + g2_ref[...].astype(F32)

    slab = lambda j: pl.BlockSpec((None, tile, cols), lambda i, pr: (j, i, 0))
    return pl.pallas_call(
        body, name=name,
        grid_spec=pltpu.PrefetchScalarGridSpec(
            num_scalar_prefetch=1, grid=(hr // tile,),
            in_specs=[pl.BlockSpec((tile, cols), lambda i, pr: (i, 0)), slab(0), slab(1), slab(2)],
            out_specs=pl.BlockSpec((None, tile, cols), lambda i, pr: (pr[1], i, 0))),
        out_shape=jax.ShapeDtypeStruct((2, hr, cols), F32),
        compiler_params=_params("parallel"),
    )(place, s32, got, got, got)


def _place_operand():
    return jnp.stack([2 * lax.axis_index("x") + lax.axis_index("y"), lax.axis_index("c")]).astype(jnp.int32)


W_IN_ROWS = 720
W_IN_GATHER_ROWS = 736
BF16_TILE_ROWS = 16


def _join_w_in(blocks):
    rows, t = D_IN // N_CHIPS, BF16_TILE_ROWS
    first = [rows * k // t * t for k in range(N_CHIPS)]
    parts = []
    for k in range(N_CHIPS):
        lo = t if k else 0
        if k + 1 < N_CHIPS:
            hi = first[k + 1] - first[k]
            assert rows * (k + 1) <= first[k + 1] + t and hi + t <= W_IN_GATHER_ROWS
            parts += [blocks[k, lo:hi], blocks[k, hi:hi + t] + blocks[k + 1, :t]]
        else:
            parts.append(blocks[k, lo:])
    return jnp.concatenate(parts, axis=0)


def _per_chip(name, g):
    if name == "w_in_t":
        rows = D_IN // N_CHIPS
        return jnp.stack([lax.slice_in_dim(g, rows * k, rows * k + W_IN_ROWS) for k in range(N_CHIPS)])
    if name == "w_proj":
        return jnp.transpose(g.reshape(PLE_DIM, N_CHIPS, D_MODEL // N_CHIPS), (1, 0, 2))
    if name == "w_up4":
        return g
    return g.reshape(N_CHIPS, g.shape[0] // N_CHIPS, g.shape[1])


class _EarlyReducer:
    def __init__(self, tag):
        self.tag = tag

    def start(self, grads):
        self.names = list(grads)
        self.place = _place_operand()
        slabs = [_per_chip(k, grads[k]) for k in self.names]
        halves = [jax.ShapeDtypeStruct((s.shape[0], s.shape[1] // 2, s.shape[2]), F32) for s in slabs]
        self.a = _copies_start(self.tag + "exchange_start", _exchange_copies, len(slabs), slabs, halves,
                               slabs[0][0, :8, :128])
        return self.a[-1]

    def middle(self, after):
        slabs, got = _copies_wait(self.tag + "exchange_wait", _exchange_copies, self.a, after)
        self.sums = [_chip_sum(self.tag + "chip_sum_" + k, s, g, self.place) for k, s, g in zip(self.names, slabs, got)]
        s16 = [s[1] for s in self.sums]
        lands = [jax.ShapeDtypeStruct((3,) + s.shape[1:], BF16) for s in s16]
        self.b = _copies_start(self.tag + "scatter_start", _scatter_copies, 3 * len(s16), s16, lands,
                               self.sums[0][0][:8, :128])
        return self.b[-1]

    def finish(self, after):
        _, got = _copies_wait(self.tag + "scatter_wait", _scatter_copies, self.b, after)
        self.bufs = {k: _mesh_sum(self.tag + "mesh_sum_" + k, s[0], g, self.place)
                     for k, s, g in zip(self.names, self.sums, got)}


def kernel(x, p, norm_mix, w_in, conv_w, a_log, dt_bias, dn_norm, sinks, w_o, norm_mlp, w_up, w_down, norm_ple, w_ple_gate, w_ple_proj, norm_final, loss_target, m_norm_mix, m_w_in, m_conv_w, m_a_log, m_dt_bias, m_dn_norm, m_sinks, m_w_o, m_norm_mlp, m_w_up, m_w_down, m_norm_ple, m_w_ple_gate, m_w_ple_proj, m_norm_final, v_norm_mix, v_w_in, v_conv_w, v_a_log, v_dt_bias, v_dn_norm, v_sinks, v_w_o, v_norm_mlp, v_w_up, v_w_down, v_norm_ple, v_w_ple_gate, v_w_ple_proj, v_norm_final):
    chip = 2 * lax.axis_index("x") + lax.axis_index("y")
    big = dict(w_in=w_in[0], w_o=w_o[0], w_up=w_up[0], w_down=w_down[0], w_gate=w_ple_gate[0], w_proj=w_ple_proj[0])
    big_m = dict(w_in=m_w_in[0], w_o=m_w_o[0], w_up=m_w_up[0], w_down=m_w_down[0], w_gate=m_w_ple_gate[0], w_proj=m_w_ple_proj[0])
    big_v = dict(w_in=v_w_in[0], w_o=v_w_o[0], w_up=v_w_up[0], w_down=v_w_down[0], w_gate=v_w_ple_gate[0], w_proj=v_w_ple_proj[0])
    names = list(big)

    rows_in = D_IN // N_CHIPS
    chip_index = 2 * lax.axis_index("x") + lax.axis_index("y")
    w_in_shard_t = lax.dynamic_update_slice(jnp.zeros((W_IN_GATHER_ROWS, D_MODEL), BF16), big["w_in"].T.astype(BF16),
                                            ((rows_in * chip_index) % BF16_TILE_ROWS, 0))
    late_names = names[1:]
    (w_in_all,), conv_all, late_shards, rope = _gather_weights([w_in_shard_t], conv_w[0], [big[k] for k in late_names],
                                                               x.shape[1])
    gather = _copies_start("gather_start", _late_copies, LATE_COPIES * len(late_shards), late_shards,
                           [jax.ShapeDtypeStruct((N_CHIPS,) + s.shape, BF16) for s in late_shards], w_in_all)
    token = gather[-1]
    w = dict(w_in_t=_join_w_in(w_in_all), rope=rope,
             conv_w=jnp.transpose(conv_all, (1, 0, 2)).reshape(DN_CONV, CONV_CH))
    sm = dict(norm_mix=norm_mix[0] + token[0, 0], a_log=a_log[0], dt_bias=dt_bias[0], dn_norm=dn_norm[0],
              sinks=sinks[0], norm_mlp=norm_mlp[0], norm_ple=norm_ple[0], norm_final=norm_final)

    def late(after):
        first = functools.partial(_late_copies, only=(0,))
        srcs, lands = _copies_wait("gather_wait_o", first, gather, after)

        def rest(after2):
            others = functools.partial(_late_copies, only=tuple(range(1, len(late_names))))
            gw = dict(zip(late_names, _copies_wait("gather_wait_rest", others, gather[:2] + (srcs, lands, None), after2)[1]))
            return dict(w_up4=gw["w_up"], w_down=gw["w_down"].reshape(D_FF, D_MODEL),
                        w_gate=gw["w_gate"].reshape(D_MODEL, D_MODEL), w_proj4=gw["w_proj"])

        return lands[0].reshape(D_MODEL, D_MODEL), rest

    reducer = _EarlyReducer("early_")
    sums, grad_x, g = _local_step(x[0], p[0, 0], loss_target[0], sm, w, late, reducer)

    last = _EarlyReducer("last_")
    dep_a = last.start({"w_in_t": g["w_in_t"]})

    row = lambda v: jnp.pad(v, (0, SMALL_COLS - v.shape[0]))

    def misc_row(al, dtb, sk, dnn, rest):
        head = jnp.concatenate([al, dtb, sk])
        return row(jnp.concatenate([head, jnp.zeros((DN_NORM_LANE - head.shape[0],), F32), dnn, rest]))

    misc = misc_row(sums["a_log"], sums["dt_bias"], sums["sinks"], sums["dn_norm"], sums["loss"].reshape(1))
    small = jnp.concatenate([sums["conv_w"], jnp.stack([row(sums["norm_mix"]), row(sums["norm_mlp"]), row(sums["norm_ple"]),
                                                        row(sums["norm_final"]), misc]),
                             jnp.zeros((SMALL_ROWS - 9, SMALL_COLS), F32)], axis=0)
    tot = _allreduce_small(small + dep_a[0, 0])
    dep_b = last.middle(tot)
    grad_key = dict(w_o="w_o", w_up="w_up4", w_down="w_down", w_gate="w_gate", w_proj="w_proj")
    full = _share_halves("share_halves", [reducer.bufs[grad_key[k]] for k in late_names], dep_b)
    red = {k: f.reshape(-1, f.shape[-1]) for k, f in zip(late_names, full)}
    loss = tot[8, 256]
    ncw = CONV_CH // N_CHIPS

    def pack(cw, nmix, nmlp, nple, nfin, al, dtb, sk, dnn):
        misc_p = misc_row(al, dtb, sk, dnn, jnp.zeros((0,), F32))
        cw_p = jnp.pad(cw, ((0, 0), (0, SMALL_COLS - ncw)))
        return jnp.concatenate([cw_p, jnp.stack([row(nmix), row(nmlp), row(nple), row(nfin), misc_p]),
                                jnp.zeros((SMALL_ROWS - 9, SMALL_COLS), F32)], axis=0)

    def unpack(buf):
        return dict(conv_w=buf[0:4, :ncw][None], norm_mix=buf[4, :D_MODEL][None], norm_mlp=buf[5, :D_MODEL][None],
                    norm_ple=buf[6, :D_MODEL][None], norm_final=buf[7, :D_MODEL], a_log=buf[8, 0:4][None],
                    dt_bias=buf[8, 4:8][None], sinks=buf[8, 8:16][None], dn_norm=buf[8, 128:256][None])

    g_conv_shard = lax.dynamic_slice(tot[0:4], (0, chip * ncw), (DN_CONV, ncw))
    g_small = pack(g_conv_shard, tot[4, :D_MODEL], tot[5, :D_MODEL], tot[6, :D_MODEL], tot[7, :D_MODEL],
                   tot[8, 0:4], tot[8, 4:8], tot[8, 8:16], tot[8, 128:256])
    w_small = pack(conv_w[0], norm_mix[0], norm_mlp[0], norm_ple[0], norm_final, a_log[0], dt_bias[0], sinks[0], dn_norm[0])
    m_small = pack(m_conv_w[0], m_norm_mix[0], m_norm_mlp[0], m_norm_ple[0], m_norm_final, m_a_log[0], m_dt_bias[0],
                   m_sinks[0], m_dn_norm[0])
    v_small = pack(v_conv_w[0], v_norm_mix[0], v_norm_mlp[0], v_norm_ple[0], v_norm_final, v_a_log[0], v_dt_bias[0],
                   v_sinks[0], v_dn_norm[0])

    ref_name = dict(w_in="w_in", w_o="w_o", w_up="w_up", w_down="w_down", w_gate="w_ple_gate", w_proj="w_ple_proj")
    out_g, out_d, out_m, out_v = {}, {}, {}, {}

    def update(k, dep):
        d_k, m_k, v_k = _adamw("adamw_" + k, big[k], red[k], big_m[k], big_v[k], dep)
        out_g[ref_name[k]], out_d[ref_name[k]] = red[k][None], d_k[None]
        out_m[ref_name[k]], out_v[ref_name[k]] = m_k[None], v_k[None]
        return d_k

    for k in late_names:
        done = update(k, dep_b)
    small_out = _adamw("adamw_small", w_small, g_small, m_small, v_small, dep_b)
    d_s, m_s, v_s = (unpack(b) for b in small_out)
    g_s = unpack(g_small)
    for src, dst in ((g_s, out_g), (d_s, out_d), (m_s, out_m), (v_s, out_v)):
        dst.update(src)
    last.finish(done + small_out[0][0:1, 0:1])
    (w_in_full,) = _share_halves("share_halves_w_in", [last.bufs["w_in_t"]], dep_b)
    g_t = w_in_full.reshape(W_IN_ROWS, D_MODEL)[:D_IN // N_CHIPS]
    d_t, m_t, v_t = _adamw("adamw_w_in", big["w_in"].T, g_t, big_m["w_in"].T, big_v["w_in"].T, dep_b)
    out_g["w_in"], out_d["w_in"], out_m["w_in"], out_v["w_in"] = g_t.T[None], d_t.T[None], m_t.T[None], v_t.T[None]
    order = ["norm_mix", "w_in", "conv_w", "a_log", "dt_bias", "dn_norm", "sinks", "w_o", "norm_mlp", "w_up", "w_down",
             "norm_ple", "w_ple_gate", "w_ple_proj", "norm_final"]
    return (loss, grad_x[None], *[out_g[k] for k in order], *[out_d[k] for k in order],
            *[out_m[k] for k in order], *[out_v[k] for k in order])
```
